```python
import jax, jax.numpy as jnp
from jax import lax
import numpy as np

D_MODEL = 1024
BATCH = 8
SEQ = 8192
DEPTH = 2

N_A_LAYERS = DEPTH // 2
N_B_LAYERS = DEPTH - N_A_LAYERS
HEAD_DIM = 64
SB_HEADS = D_MODEL // HEAD_DIM
SWA_Q_HEADS = D_MODEL // HEAD_DIM
SWA_KV_HEADS = 4
SWA_GROUP = SWA_Q_HEADS // SWA_KV_HEADS
WINDOW = 128
Q_BLOCK = 128
D_FF = 2816
ROPE_THETA = 10000.0
RMS_EPS = 1e-6
FFN_RES_SCALE = 0.5

kernel_name = "yoco_stickbreak_swa_sink_macaron"


def rms_norm(x, g):
    xf = x.astype(jnp.float32)
    y = xf * lax.rsqrt(jnp.mean(xf * xf, axis=-1, keepdims=True) + RMS_EPS)
    return (y * g.astype(jnp.float32)).astype(x.dtype)


def swiglu(x, w_in, w_out):
    gate, up = jnp.split(x @ w_in, 2, axis=-1)
    return (jax.nn.silu(gate) * up) @ w_out


def rotary(x, pos):
    half = HEAD_DIM // 2
    inv_freq = ROPE_THETA ** (-jnp.arange(half, dtype=jnp.float32) / half)
    ang = pos.astype(jnp.float32)[:, None] * inv_freq[None, :]
    cos = jnp.cos(ang)[None, :, None, :]
    sin = jnp.sin(ang)[None, :, None, :]
    xf = x.astype(jnp.float32)
    x1, x2 = xf[..., :half], xf[..., half:]
    return jnp.concatenate([x1 * cos - x2 * sin, x2 * cos + x1 * sin], axis=-1).astype(x.dtype)


def stick_breaking_attention(q, k, v):
    B, S, H, Dh = q.shape
    nb = S // Q_BLOCK
    scale = Dh ** -0.5
    key_pos = jnp.arange(S)
    qb = q.reshape(B, nb, Q_BLOCK, H, Dh).transpose(1, 0, 2, 3, 4)

    def block(args):
        qi, i = args
        z = jnp.einsum('bqhd,bshd->bhqs', qi, k).astype(jnp.float32) * scale
        q_pos = i * Q_BLOCK + jnp.arange(Q_BLOCK)
        strict = key_pos[None, :] < q_pos[:, None]
        log_beta = jax.nn.log_sigmoid(z)
        log_1m_beta = jnp.where(strict, log_beta - z, 0.0)
        suffix = lax.cumsum(log_1m_beta, axis=3, reverse=True) - log_1m_beta
        w = jnp.where(strict, jnp.exp(log_beta + suffix), 0.0)
        return jnp.einsum('bhqs,bshd->bqhd', w.astype(v.dtype), v)

    out = lax.map(block, (qb, jnp.arange(nb)))
    return out.transpose(1, 0, 2, 3, 4).reshape(B, S, H, Dh)


def sliding_window_sink_attention(q, k, v, sinks):
    B, S, Hq, Dh = q.shape
    nb = S // WINDOW
    qb = q.reshape(B, nb, WINDOW, SWA_KV_HEADS, SWA_GROUP, Dh)

    def band(t):
        tb = t.reshape(B, nb, WINDOW, SWA_KV_HEADS, Dh)
        prev = jnp.pad(tb[:, :-1], ((0, 0), (1, 0), (0, 0), (0, 0), (0, 0)))
        return jnp.concatenate([prev, tb], axis=2)

    kb, vb = band(k), band(v)
    s = jnp.einsum('bnqhgd,bnkhd->bnhgqk', qb, kb).astype(jnp.float32) * (Dh ** -0.5)
    qi = jnp.arange(WINDOW)[:, None]
    ki = jnp.arange(2 * WINDOW)[None, :]
    diff = qi + WINDOW - ki
    in_window = (diff >= 0) & (diff < WINDOW)
    blk = jnp.arange(nb)[:, None, None]
    valid = in_window[None] & ((blk > 0) | (ki[None] >= WINDOW))
    s = jnp.where(valid[None, :, None, None], s, -jnp.inf)
    sink = jnp.broadcast_to(
        sinks.astype(jnp.float32).reshape(SWA_KV_HEADS, SWA_GROUP)[None, None, :, :, None, None],
        s.shape[:-1] + (1,))
    p = jax.nn.softmax(jnp.concatenate([s, sink], axis=-1), axis=-1)[..., :-1]
    out = jnp.einsum('bnhgqk,bnkhd->bnqhgd', p.astype(v.dtype), vb)
    return out.reshape(B, S, Hq, Dh)


def _fwd_setup_inputs(seed: int = 0) -> dict:
    key = jax.random.key(seed)
    ks = jax.random.split(key, 20)
    f32 = jnp.float32

    def w(k, shape, fan_in):
        return jax.random.normal(k, shape, f32) * (fan_in ** -0.5)

    def gain(k, shape):
        return 1.0 + 0.02 * jax.random.normal(k, shape, f32)

    return {
        "x": jax.random.normal(ks[0], (BATCH, SEQ, D_MODEL), f32),
        "ffn1_norm": gain(ks[1], (DEPTH, D_MODEL)),
        "ffn1_w_in": w(ks[2], (DEPTH, D_MODEL, 2 * D_FF), D_MODEL),
        "ffn1_w_out": w(ks[3], (DEPTH, D_FF, D_MODEL), D_FF),
        "mix_norm": gain(ks[4], (DEPTH, D_MODEL)),
        "ffn2_norm": gain(ks[5], (DEPTH, D_MODEL)),
        "ffn2_w_in": w(ks[6], (DEPTH, D_MODEL, 2 * D_FF), D_MODEL),
        "ffn2_w_out": w(ks[7], (DEPTH, D_FF, D_MODEL), D_FF),
        "sb_w_qkv": w(ks[8], (N_A_LAYERS, D_MODEL, 3 * SB_HEADS * HEAD_DIM), D_MODEL),
        "sb_w_o": w(ks[9], (N_A_LAYERS, SB_HEADS * HEAD_DIM, D_MODEL), SB_HEADS * HEAD_DIM),
        "kv_norm": gain(ks[10], (D_MODEL,)),
        "kv_w": w(ks[11], (D_MODEL, 2 * SWA_KV_HEADS * HEAD_DIM), D_MODEL),
        "swa_w_q": w(ks[12], (N_B_LAYERS, D_MODEL, SWA_Q_HEADS * HEAD_DIM), D_MODEL),
        "swa_sinks": 0.5 * jax.random.normal(ks[13], (N_B_LAYERS, SWA_Q_HEADS), f32),
        "swa_w_o": w(ks[14], (N_B_LAYERS, SWA_Q_HEADS * HEAD_DIM, D_MODEL), SWA_Q_HEADS * HEAD_DIM),
        "final_norm": gain(ks[15], (D_MODEL,)),
    }


def _fwd_reference(x, ffn1_norm, ffn1_w_in, ffn1_w_out, mix_norm, ffn2_norm, ffn2_w_in, ffn2_w_out,
              sb_w_qkv, sb_w_o, kv_norm, kv_w, swa_w_q, swa_sinks, swa_w_o, final_norm):
    B, S, D = x.shape
    pos = jnp.arange(S)
    h = x
    k_shared = None
    v_shared = None
    for layer in range(DEPTH):
        h = h + FFN_RES_SCALE * swiglu(rms_norm(h, ffn1_norm[layer]), ffn1_w_in[layer], ffn1_w_out[layer])
        hn = rms_norm(h, mix_norm[layer])
        if layer < N_A_LAYERS:
            qkv = (hn @ sb_w_qkv[layer]).reshape(B, S, 3, SB_HEADS, HEAD_DIM)
            o = stick_breaking_attention(qkv[:, :, 0], qkv[:, :, 1], qkv[:, :, 2])
            h = h + o.reshape(B, S, SB_HEADS * HEAD_DIM) @ sb_w_o[layer]
        else:
            j = layer - N_A_LAYERS
            q = rotary((hn @ swa_w_q[j]).reshape(B, S, SWA_Q_HEADS, HEAD_DIM), pos)
            o = sliding_window_sink_attention(q, k_shared, v_shared, swa_sinks[j])
            h = h + o.reshape(B, S, SWA_Q_HEADS * HEAD_DIM) @ swa_w_o[j]
        h = h + FFN_RES_SCALE * swiglu(rms_norm(h, ffn2_norm[layer]), ffn2_w_in[layer], ffn2_w_out[layer])
        if layer == N_A_LAYERS - 1:
            kv = (rms_norm(h, kv_norm) @ kv_w).reshape(B, S, 2, SWA_KV_HEADS, HEAD_DIM)
            k_shared = rotary(kv[:, :, 0], pos)
            v_shared = kv[:, :, 1]
    return rms_norm(h, final_norm)


import jax as _jax
import jax.numpy as _jnp

TWIN_FORMAT = 'train_step'
FWD_PARAMS = ['x', 'ffn1_norm', 'ffn1_w_in', 'ffn1_w_out', 'mix_norm', 'ffn2_norm', 'ffn2_w_in', 'ffn2_w_out', 'sb_w_qkv', 'sb_w_o', 'kv_norm', 'kv_w', 'swa_w_q', 'swa_sinks', 'swa_w_o', 'final_norm']
TWIN_WEIGHTS = ['ffn1_norm', 'ffn1_w_in', 'ffn1_w_out', 'mix_norm', 'ffn2_norm', 'ffn2_w_in', 'ffn2_w_out', 'sb_w_qkv', 'sb_w_o', 'kv_norm', 'kv_w', 'swa_w_q', 'swa_sinks', 'swa_w_o', 'final_norm']
TWIN_DIFF_INPUT = 'x'
TWIN_INPUTS = ['x', 'ffn1_norm', 'ffn1_w_in', 'ffn1_w_out', 'mix_norm', 'ffn2_norm', 'ffn2_w_in', 'ffn2_w_out', 'sb_w_qkv', 'sb_w_o', 'kv_norm', 'kv_w', 'swa_w_q', 'swa_sinks', 'swa_w_o', 'final_norm', 'loss_target', 'm_ffn1_norm', 'm_ffn1_w_in', 'm_ffn1_w_out', 'm_mix_norm', 'm_ffn2_norm', 'm_ffn2_w_in', 'm_ffn2_w_out', 'm_sb_w_qkv', 'm_sb_w_o', 'm_kv_norm', 'm_kv_w', 'm_swa_w_q', 'm_swa_sinks', 'm_swa_w_o', 'm_final_norm', 'v_ffn1_norm', 'v_ffn1_w_in', 'v_ffn1_w_out', 'v_mix_norm', 'v_ffn2_norm', 'v_ffn2_w_in', 'v_ffn2_w_out', 'v_sb_w_qkv', 'v_sb_w_o', 'v_kv_norm', 'v_kv_w', 'v_swa_w_q', 'v_swa_sinks', 'v_swa_w_o', 'v_final_norm']
TWIN_OUTPUTS = ['loss', 'grad_x', 'grad_ffn1_norm', 'grad_ffn1_w_in', 'grad_ffn1_w_out', 'grad_mix_norm', 'grad_ffn2_norm', 'grad_ffn2_w_in', 'grad_ffn2_w_out', 'grad_sb_w_qkv', 'grad_sb_w_o', 'grad_kv_norm', 'grad_kv_w', 'grad_swa_w_q', 'grad_swa_sinks', 'grad_swa_w_o', 'grad_final_norm', 'delta_ffn1_norm', 'delta_ffn1_w_in', 'delta_ffn1_w_out', 'delta_mix_norm', 'delta_ffn2_norm', 'delta_ffn2_w_in', 'delta_ffn2_w_out', 'delta_sb_w_qkv', 'delta_sb_w_o', 'delta_kv_norm', 'delta_kv_w', 'delta_swa_w_q', 'delta_swa_sinks', 'delta_swa_w_o', 'delta_final_norm', 'new_m_ffn1_norm', 'new_m_ffn1_w_in', 'new_m_ffn1_w_out', 'new_m_mix_norm', 'new_m_ffn2_norm', 'new_m_ffn2_w_in', 'new_m_ffn2_w_out', 'new_m_sb_w_qkv', 'new_m_sb_w_o', 'new_m_kv_norm', 'new_m_kv_w', 'new_m_swa_w_q', 'new_m_swa_sinks', 'new_m_swa_w_o', 'new_m_final_norm', 'new_v_ffn1_norm', 'new_v_ffn1_w_in', 'new_v_ffn1_w_out', 'new_v_mix_norm', 'new_v_ffn2_norm', 'new_v_ffn2_w_in', 'new_v_ffn2_w_out', 'new_v_sb_w_qkv', 'new_v_sb_w_o', 'new_v_kv_norm', 'new_v_kv_w', 'new_v_swa_w_q', 'new_v_swa_sinks', 'new_v_swa_w_o', 'new_v_final_norm']
TWIN_LEAF_KINDS = {'loss': 'loss', 'grad_x': 'grad_x', 'grad_ffn1_norm': 'grad_w', 'grad_ffn1_w_in': 'grad_w', 'grad_ffn1_w_out': 'grad_w', 'grad_mix_norm': 'grad_w', 'grad_ffn2_norm': 'grad_w', 'grad_ffn2_w_in': 'grad_w', 'grad_ffn2_w_out': 'grad_w', 'grad_sb_w_qkv': 'grad_w', 'grad_sb_w_o': 'grad_w', 'grad_kv_norm': 'grad_w', 'grad_kv_w': 'grad_w', 'grad_swa_w_q': 'grad_w', 'grad_swa_sinks': 'grad_w', 'grad_swa_w_o': 'grad_w', 'grad_final_norm': 'grad_w', 'delta_ffn1_norm': 'delta_w', 'delta_ffn1_w_in': 'delta_w', 'delta_ffn1_w_out': 'delta_w', 'delta_mix_norm': 'delta_w', 'delta_ffn2_norm': 'delta_w', 'delta_ffn2_w_in': 'delta_w', 'delta_ffn2_w_out': 'delta_w', 'delta_sb_w_qkv': 'delta_w', 'delta_sb_w_o': 'delta_w', 'delta_kv_norm': 'delta_w', 'delta_kv_w': 'delta_w', 'delta_swa_w_q': 'delta_w', 'delta_swa_sinks': 'delta_w', 'delta_swa_w_o': 'delta_w', 'delta_final_norm': 'delta_w', 'new_m_ffn1_norm': 'new_m', 'new_m_ffn1_w_in': 'new_m', 'new_m_ffn1_w_out': 'new_m', 'new_m_mix_norm': 'new_m', 'new_m_ffn2_norm': 'new_m', 'new_m_ffn2_w_in': 'new_m', 'new_m_ffn2_w_out': 'new_m', 'new_m_sb_w_qkv': 'new_m', 'new_m_sb_w_o': 'new_m', 'new_m_kv_norm': 'new_m', 'new_m_kv_w': 'new_m', 'new_m_swa_w_q': 'new_m', 'new_m_swa_sinks': 'new_m', 'new_m_swa_w_o': 'new_m', 'new_m_final_norm': 'new_m', 'new_v_ffn1_norm': 'new_v', 'new_v_ffn1_w_in': 'new_v', 'new_v_ffn1_w_out': 'new_v', 'new_v_mix_norm': 'new_v', 'new_v_ffn2_norm': 'new_v', 'new_v_ffn2_w_in': 'new_v', 'new_v_ffn2_w_out': 'new_v', 'new_v_sb_w_qkv': 'new_v', 'new_v_sb_w_o': 'new_v', 'new_v_kv_norm': 'new_v', 'new_v_kv_w': 'new_v', 'new_v_swa_w_q': 'new_v', 'new_v_swa_sinks': 'new_v', 'new_v_swa_w_o': 'new_v', 'new_v_final_norm': 'new_v'}


def _forward(args):
    return _fwd_reference(*[args[k] for k in FWD_PARAMS])


def _output_shape():
    def fwd():
        inp = _fwd_setup_inputs(0)
        return _fwd_reference(*[inp[k] for k in FWD_PARAMS])
    out = _jax.eval_shape(fwd)
    return out.shape, out.dtype

N_MICROBATCH = 1
ADAM_LR = 0.001
ADAM_B1 = 0.9
ADAM_B2 = 0.999
ADAM_EPS = 1e-08
ADAM_WD = 0.01
ADAM_STEP = 10
PER_EXAMPLE_BATCH_AXIS = {'x': 0, 'loss_target': 0}
SHARED_INPUTS = []
_WEIGHT_DTYPES = {'ffn1_norm': _jnp.float32, 'ffn1_w_in': _jnp.float32, 'ffn1_w_out': _jnp.float32, 'mix_norm': _jnp.float32, 'ffn2_norm': _jnp.float32, 'ffn2_w_in': _jnp.float32, 'ffn2_w_out': _jnp.float32, 'sb_w_qkv': _jnp.float32, 'sb_w_o': _jnp.float32, 'kv_norm': _jnp.float32, 'kv_w': _jnp.float32, 'swa_w_q': _jnp.float32, 'swa_sinks': _jnp.float32, 'swa_w_o': _jnp.float32, 'final_norm': _jnp.float32}
MOMENT_SCALE = {'ffn1_norm': 1.093490e-01, 'ffn1_w_in': 4.431969e-02, 'ffn1_w_out': 7.212903e-02, 'mix_norm': 1.282965e-01, 'ffn2_norm': 8.686289e-02, 'ffn2_w_in': 3.651607e-02, 'ffn2_w_out': 5.959709e-02, 'sb_w_qkv': 9.993846e-02, 'sb_w_o': 1.445205e-01, 'kv_norm': 5.359725e-02, 'kv_w': 7.723492e-02, 'swa_w_q': 3.459358e-02, 'swa_sinks': 3.425557e-02, 'swa_w_o': 4.224446e-02, 'final_norm': 6.396693e+01}


def _to_microbatches(a, axis):
    t = _jnp.moveaxis(a, axis, 0)
    t = t.reshape((N_MICROBATCH, t.shape[0] // N_MICROBATCH) + t.shape[1:])
    return _jnp.moveaxis(t, 1, axis + 1)


def setup_inputs(seed: int = 0) -> dict:
    inp = _fwd_setup_inputs(seed)
    key = _jax.random.fold_in(_jax.random.key(seed), 7919)
    shape, _ = _output_shape()
    out = dict(inp)
    out["loss_target"] = _jax.random.normal(_jax.random.fold_in(key, 0), shape, _jnp.float32)
    for i, name in enumerate(TWIN_WEIGHTS):
        w = inp[name].astype(_jnp.float32)
        if MOMENT_SCALE is None:
            s = _jnp.sqrt(_jnp.mean(_jnp.square(w)) + 1e-30)
        else:
            s = MOMENT_SCALE[name]
        km, kv = _jax.random.split(_jax.random.fold_in(key, i + 1))
        out[name] = w
        out["m_" + name] = s * _jax.random.normal(km, w.shape, _jnp.float32)
        out["v_" + name] = (s * s) * _jax.random.uniform(kv, w.shape, _jnp.float32, 0.5, 1.5)
    if N_MICROBATCH > 1:
        for name, axis in PER_EXAMPLE_BATCH_AXIS.items():
            out[name] = _to_microbatches(out[name], axis)
    return {'x': out['x'], 'ffn1_norm': out['ffn1_norm'], 'ffn1_w_in': out['ffn1_w_in'], 'ffn1_w_out': out['ffn1_w_out'], 'mix_norm': out['mix_norm'], 'ffn2_norm': out['ffn2_norm'], 'ffn2_w_in': out['ffn2_w_in'], 'ffn2_w_out': out['ffn2_w_out'], 'sb_w_qkv': out['sb_w_qkv'], 'sb_w_o': out['sb_w_o'], 'kv_norm': out['kv_norm'], 'kv_w': out['kv_w'], 'swa_w_q': out['swa_w_q'], 'swa_sinks': out['swa_sinks'], 'swa_w_o': out['swa_w_o'], 'final_norm': out['final_norm'], 'loss_target': out['loss_target'], 'm_ffn1_norm': out['m_ffn1_norm'], 'm_ffn1_w_in': out['m_ffn1_w_in'], 'm_ffn1_w_out': out['m_ffn1_w_out'], 'm_mix_norm': out['m_mix_norm'], 'm_ffn2_norm': out['m_ffn2_norm'], 'm_ffn2_w_in': out['m_ffn2_w_in'], 'm_ffn2_w_out': out['m_ffn2_w_out'], 'm_sb_w_qkv': out['m_sb_w_qkv'], 'm_sb_w_o': out['m_sb_w_o'], 'm_kv_norm': out['m_kv_norm'], 'm_kv_w': out['m_kv_w'], 'm_swa_w_q': out['m_swa_w_q'], 'm_swa_sinks': out['m_swa_sinks'], 'm_swa_w_o': out['m_swa_w_o'], 'm_final_norm': out['m_final_norm'], 'v_ffn1_norm': out['v_ffn1_norm'], 'v_ffn1_w_in': out['v_ffn1_w_in'], 'v_ffn1_w_out': out['v_ffn1_w_out'], 'v_mix_norm': out['v_mix_norm'], 'v_ffn2_norm': out['v_ffn2_norm'], 'v_ffn2_w_in': out['v_ffn2_w_in'], 'v_ffn2_w_out': out['v_ffn2_w_out'], 'v_sb_w_qkv': out['v_sb_w_qkv'], 'v_sb_w_o': out['v_sb_w_o'], 'v_kv_norm': out['v_kv_norm'], 'v_kv_w': out['v_kv_w'], 'v_swa_w_q': out['v_swa_w_q'], 'v_swa_sinks': out['v_swa_sinks'], 'v_swa_w_o': out['v_swa_w_o'], 'v_final_norm': out['v_final_norm']}


def _loss(weights, diff, rest, loss_target):
    with _jax.named_scope("forward"):
        args = {**rest, TWIN_DIFF_INPUT: diff, **{k: w.astype(_WEIGHT_DTYPES[k]) for k, w in weights.items()}}
        y = _forward(args)
    with _jax.named_scope("loss_head"):
        err = _jnp.square(y.astype(_jnp.float32) - loss_target)
        return 0.5 * _jnp.sum(_jnp.mean(err, axis=-1)) if err.ndim else 0.5 * err


def _adamw(w, g, m, v):
    m = ADAM_B1 * m + (1.0 - ADAM_B1) * g
    v = ADAM_B2 * v + (1.0 - ADAM_B2) * _jnp.square(g)
    m_hat = m / (1.0 - ADAM_B1 ** ADAM_STEP)
    v_hat = v / (1.0 - ADAM_B2 ** ADAM_STEP)
    delta = -ADAM_LR * (m_hat / (_jnp.sqrt(v_hat) + ADAM_EPS) + ADAM_WD * w)
    return delta, m, v


def reference(x, ffn1_norm, ffn1_w_in, ffn1_w_out, mix_norm, ffn2_norm, ffn2_w_in, ffn2_w_out, sb_w_qkv, sb_w_o, kv_norm, kv_w, swa_w_q, swa_sinks, swa_w_o, final_norm, loss_target, m_ffn1_norm, m_ffn1_w_in, m_ffn1_w_out, m_mix_norm, m_ffn2_norm, m_ffn2_w_in, m_ffn2_w_out, m_sb_w_qkv, m_sb_w_o, m_kv_norm, m_kv_w, m_swa_w_q, m_swa_sinks, m_swa_w_o, m_final_norm, v_ffn1_norm, v_ffn1_w_in, v_ffn1_w_out, v_mix_norm, v_ffn2_norm, v_ffn2_w_in, v_ffn2_w_out, v_sb_w_qkv, v_sb_w_o, v_kv_norm, v_kv_w, v_swa_w_q, v_swa_sinks, v_swa_w_o, v_final_norm):
    given = dict(x=x, ffn1_norm=ffn1_norm, ffn1_w_in=ffn1_w_in, ffn1_w_out=ffn1_w_out, mix_norm=mix_norm, ffn2_norm=ffn2_norm, ffn2_w_in=ffn2_w_in, ffn2_w_out=ffn2_w_out, sb_w_qkv=sb_w_qkv, sb_w_o=sb_w_o, kv_norm=kv_norm, kv_w=kv_w, swa_w_q=swa_w_q, swa_sinks=swa_sinks, swa_w_o=swa_w_o, final_norm=final_norm, loss_target=loss_target, m_ffn1_norm=m_ffn1_norm, m_ffn1_w_in=m_ffn1_w_in, m_ffn1_w_out=m_ffn1_w_out, m_mix_norm=m_mix_norm, m_ffn2_norm=m_ffn2_norm, m_ffn2_w_in=m_ffn2_w_in, m_ffn2_w_out=m_ffn2_w_out, m_sb_w_qkv=m_sb_w_qkv, m_sb_w_o=m_sb_w_o, m_kv_norm=m_kv_norm, m_kv_w=m_kv_w, m_swa_w_q=m_swa_w_q, m_swa_sinks=m_swa_sinks, m_swa_w_o=m_swa_w_o, m_final_norm=m_final_norm, v_ffn1_norm=v_ffn1_norm, v_ffn1_w_in=v_ffn1_w_in, v_ffn1_w_out=v_ffn1_w_out, v_mix_norm=v_mix_norm, v_ffn2_norm=v_ffn2_norm, v_ffn2_w_in=v_ffn2_w_in, v_ffn2_w_out=v_ffn2_w_out, v_sb_w_qkv=v_sb_w_qkv, v_sb_w_o=v_sb_w_o, v_kv_norm=v_kv_norm, v_kv_w=v_kv_w, v_swa_w_q=v_swa_w_q, v_swa_sinks=v_swa_sinks, v_swa_w_o=v_swa_w_o, v_final_norm=v_final_norm)
    weights = {n: given[n] for n in TWIN_WEIGHTS}
    shared = {n: given[n] for n in SHARED_INPUTS}
    per_example = {n: given[n] for n in ['x']}
    grad_fn = _jax.value_and_grad(_loss, argnums=(0, 1))

    def one_microbatch(ex, loss_target):
        ex = dict(ex)
        diff = ex.pop(TWIN_DIFF_INPUT)
        return grad_fn(weights, diff, {**shared, **ex}, loss_target)

    if N_MICROBATCH == 1:
        loss, (grad_w, grad_x) = one_microbatch(per_example, given["loss_target"])
    else:
        def body(carry, xs):
            loss_sum, grad_sum = carry
            l_k, (gw_k, gx_k) = one_microbatch(xs[0], xs[1])
            with _jax.named_scope("update"):
                return (loss_sum + l_k, _jax.tree.map(_jnp.add, grad_sum, gw_k)), gx_k

        init = (_jnp.zeros((), _jnp.float32), _jax.tree.map(_jnp.zeros_like, weights))
        (loss, grad_w), grad_x = _jax.lax.scan(body, init, (per_example, given["loss_target"]))
    with _jax.named_scope("update"):
        delta_w, new_m, new_v = {}, {}, {}
        for n in TWIN_WEIGHTS:
            delta_w[n], new_m[n], new_v[n] = _adamw(weights[n], grad_w[n], given["m_" + n], given["v_" + n])
    return (loss, grad_x, *[grad_w[n] for n in TWIN_WEIGHTS], *[delta_w[n] for n in TWIN_WEIGHTS],
            *[new_m[n] for n in TWIN_WEIGHTS], *[new_v[n] for n in TWIN_WEIGHTS])
```

```python
import functools

import jax
import jax.numpy as jnp
from jax import lax
from jax.experimental import pallas as pl
from jax.experimental.pallas import tpu as pltpu

F32 = jnp.float32
BF16 = jnp.bfloat16
MESH = pl.DeviceIdType.MESH

D_MODEL = 1024
D_FF = 2816
HEAD_DIM = 64
SB_HEADS = 16
SWA_Q_HEADS = 16
SWA_KV_HEADS = 4
WINDOW = 128
ROPE_THETA = 10000.0
RMS_EPS = 1e-6
FFN_RES_SCALE = 0.5
ATTN_SCALE = HEAD_DIM ** -0.5

ADAM_LR = 0.001
ADAM_B1 = 0.9
ADAM_B2 = 0.999
ADAM_EPS = 1e-08
ADAM_WD = 0.01
ADAM_STEP = 10

N_CHIPS = 4
N_DEV = 8
LANES = 128
FF_CHUNK = D_FF // 2
FF_ROWS = D_FF // N_CHIPS
SQ_ROWS = D_MODEL // N_CHIPS
QKV_COLS = 3 * D_MODEL // N_CHIPS
VMEM_LIMIT = 56 * 1024 * 1024
NEG_BIG = -1e30

SLOT_FFN1 = (0, 1)
SLOT_FFN2 = (2, 3)
SQ_SB_O, SQ_SWA_Q, SQ_SWA_O = 0, 1, 2


def _cparams():
    return pltpu.CompilerParams(vmem_limit_bytes=VMEM_LIMIT)


def _dot(a, b):
    return jnp.dot(a, b, preferred_element_type=F32)


def _dot_nt(a, b):
    return lax.dot_general(a, b, (((1,), (1,)), ((), ())), preferred_element_type=F32)


def _dot_tn(a, b):
    return lax.dot_general(a, b, (((0,), (0,)), ((), ())), preferred_element_type=F32)


def _rstd(h):
    return lax.rsqrt(jnp.mean(h * h, axis=-1, keepdims=True) + RMS_EPS)


def _swap32(x):
    n = x.shape[-1]
    lane = lax.broadcasted_iota(jnp.int32, x.shape, x.ndim - 1)
    first = (lane % HEAD_DIM) < (HEAD_DIM // 2)
    return jnp.where(first, pltpu.roll(x, n - HEAD_DIM // 2, x.ndim - 1), pltpu.roll(x, HEAD_DIM // 2, x.ndim - 1))


def _tile_lanes(t, n):
    return t if n == LANES else jnp.tile(t, (1, n // LANES))


def ffn_fwd(h, g, w_in, w_out, slot):
    T = h.shape[0]
    tm = 256
    nch = D_FF // FF_CHUNK

    def body(h_ref, g_ref, wg_ref, wu_ref, wo_ref, out_ref, gate_ref, up_ref, xn_s, acc_s):
        j = pl.program_id(1)

        @pl.when(j == 0)
        def _():
            hh = h_ref[...]
            xn_s[...] = (hh * _rstd(hh) * g_ref[...]).astype(BF16)
            acc_s[...] = jnp.zeros_like(acc_s)

        xn = xn_s[...]
        gate = _dot(xn, wg_ref[...])
        up = _dot(xn, wu_ref[...])
        gate_ref[...] = gate
        up_ref[...] = up
        a = gate * jax.nn.sigmoid(gate) * up
        acc_s[...] += _dot(a.astype(BF16), wo_ref[...].reshape(FF_CHUNK, D_MODEL))

        @pl.when(j == nch - 1)
        def _():
            out_ref[...] = h_ref[...] + FFN_RES_SCALE * acc_s[...]

    return pl.pallas_call(
        body,
        name=f"ffn_fwd_{slot}",
        grid=(T // tm, nch),
        in_specs=[
            pl.BlockSpec((tm, D_MODEL), lambda i, j: (i, 0)),
            pl.BlockSpec((1, D_MODEL), lambda i, j: (0, 0)),
            pl.BlockSpec((None, None, D_MODEL, FF_CHUNK), lambda i, j: (j, slot, 0, 0)),
            pl.BlockSpec((None, None, D_MODEL, FF_CHUNK), lambda i, j: (2 + j, slot, 0, 0)),
            pl.BlockSpec((2, None, FF_ROWS, D_MODEL), lambda i, j: (j, slot, 0, 0)),
        ],
        out_specs=[
            pl.BlockSpec((tm, D_MODEL), lambda i, j: (i, 0)),
            pl.BlockSpec((tm, FF_CHUNK), lambda i, j: (i, j)),
            pl.BlockSpec((tm, FF_CHUNK), lambda i, j: (i, j)),
        ],
        out_shape=[
            jax.ShapeDtypeStruct((T, D_MODEL), F32),
            jax.ShapeDtypeStruct((T, D_FF), F32),
            jax.ShapeDtypeStruct((T, D_FF), F32),
        ],
        scratch_shapes=[pltpu.VMEM((tm, D_MODEL), BF16), pltpu.VMEM((tm, D_MODEL), F32)],
        compiler_params=_cparams(),
    )(h, g, w_in, w_in, w_out)


def ffn_bwd_act(dh, gate, up, w_out, slot):
    T = dh.shape[0]
    tm = 256
    nch = D_FF // FF_CHUNK

    def body(dh_ref, gate_ref, up_ref, wo_ref, dg_ref, du_ref, a_ref, dhb_ref):
        dhb = (FFN_RES_SCALE * dh_ref[...]).astype(BF16)
        dhb_ref[...] = dhb
        da = _dot_nt(dhb, wo_ref[...].reshape(FF_CHUNK, D_MODEL))
        gt = gate_ref[...]
        u = up_ref[...]
        s = jax.nn.sigmoid(gt)
        silu = gt * s
        a_ref[...] = (silu * u).astype(BF16)
        dg_ref[...] = (da * u * (s * (1.0 + gt * (1.0 - s)))).astype(BF16)
        du_ref[...] = (da * silu).astype(BF16)

    ff = pl.BlockSpec((tm, FF_CHUNK), lambda i, j: (i, j))
    return pl.pallas_call(
        body,
        name=f"ffn_bwd_act_{slot}",
        grid=(T // tm, nch),
        in_specs=[
            pl.BlockSpec((tm, D_MODEL), lambda i, j: (i, 0)),
            ff, ff,
            pl.BlockSpec((2, None, FF_ROWS, D_MODEL), lambda i, j: (j, slot, 0, 0)),
        ],
        out_specs=[ff, ff, ff, pl.BlockSpec((tm, D_MODEL), lambda i, j: (i, 0))],
        out_shape=[
            jax.ShapeDtypeStruct((T, D_FF), BF16),
            jax.ShapeDtypeStruct((T, D_FF), BF16),
            jax.ShapeDtypeStruct((T, D_FF), BF16),
            jax.ShapeDtypeStruct((T, D_MODEL), BF16),
        ],
        compiler_params=_cparams(),
    )(dh, gate, up, w_out)


def rms_linear(name, h, g, w, w_spec, w_prep, n_out, tn, *, rope=None, scale=None):
    T = h.shape[0]
    tm = 512
    extra, extra_specs = [], []
    if rope is not None:
        extra += list(rope)
        extra_specs += [pl.BlockSpec((tm, LANES), lambda i, j: (i, 0))] * 2
    if scale is not None:
        extra.append(scale)
        extra_specs.append(pl.BlockSpec((1, tn), lambda i, j: (0, j)))

    def body(h_ref, g_ref, w_ref, *rest):
        rest = list(rest)
        cos_ref = sin_ref = sc_ref = None
        if rope is not None:
            cos_ref, sin_ref = rest[0], rest[1]
            rest = rest[2:]
        if scale is not None:
            sc_ref = rest[0]
            rest = rest[1:]
        out_ref, xn_s = rest

        @pl.when(pl.program_id(1) == 0)
        def _():
            hh = h_ref[...]
            xn_s[...] = (hh * _rstd(hh) * g_ref[...]).astype(BF16)

        y = _dot(xn_s[...], w_prep(w_ref[...]))
        if rope is not None:
            y = y * _tile_lanes(cos_ref[...], tn) + _swap32(y) * _tile_lanes(sin_ref[...], tn)
        if scale is not None:
            y = y * sc_ref[...]
        out_ref[...] = y.astype(BF16)

    return pl.pallas_call(
        body,
        name=name,
        grid=(T // tm, n_out // tn),
        in_specs=[
            pl.BlockSpec((tm, D_MODEL), lambda i, j: (i, 0)),
            pl.BlockSpec((1, D_MODEL), lambda i, j: (0, 0)),
            w_spec,
        ] + extra_specs,
        out_specs=pl.BlockSpec((tm, tn), lambda i, j: (i, j)),
        out_shape=jax.ShapeDtypeStruct((T, n_out), BF16),
        scratch_shapes=[pltpu.VMEM((tm, D_MODEL), BF16)],
        compiler_params=_cparams(),
    )(h, g, w, *extra)


def linear_res(name, a, w_sq, t, res):
    T = a.shape[0]
    tm = 512

    def body(a_ref, w_ref, res_ref, out_ref):
        out_ref[...] = res_ref[...] + _dot(a_ref[...], w_ref[...].reshape(D_MODEL, D_MODEL))

    row = pl.BlockSpec((tm, D_MODEL), lambda i: (i, 0))
    return pl.pallas_call(
        body,
        name=name,
        grid=(T // tm,),
        in_specs=[row, pl.BlockSpec((N_CHIPS, None, SQ_ROWS, D_MODEL), lambda i: (0, t, 0, 0)), row],
        out_specs=row,
        out_shape=jax.ShapeDtypeStruct((T, D_MODEL), F32),
        compiler_params=_cparams(),
    )(a, w_sq, res)


def linear_bwd_plain(name, dy, w_sq, t):
    T = dy.shape[0]
    tm = 512

    def body(dy_ref, w_ref, da_ref, dyb_ref):
        dyb = dy_ref[...].astype(BF16)
        dyb_ref[...] = dyb
        da_ref[...] = _dot_nt(dyb, w_ref[...].reshape(D_MODEL, D_MODEL)).astype(BF16)

    row = pl.BlockSpec((tm, D_MODEL), lambda i: (i, 0))
    return pl.pallas_call(
        body,
        name=name,
        grid=(T // tm,),
        in_specs=[row, pl.BlockSpec((N_CHIPS, None, SQ_ROWS, D_MODEL), lambda i: (0, t, 0, 0))],
        out_specs=[row, row],
        out_shape=[jax.ShapeDtypeStruct((T, D_MODEL), BF16), jax.ShapeDtypeStruct((T, D_MODEL), BF16)],
        compiler_params=_cparams(),
    )(dy, w_sq)


def linear_bwd_rms(name, pairs, h, g, dres, nch, tn, tm=256):
    T = h.shape[0]
    npair = len(pairs)

    def body(*refs):
        dy_refs = refs[:npair]
        w_refs = refs[npair:2 * npair]
        h_ref, g_ref, dres_ref, dh_ref, xn_ref, dg_ref, acc_s = refs[2 * npair:]
        i = pl.program_id(0)
        j = pl.program_id(1)

        @pl.when(j == 0)
        def _():
            acc_s[...] = jnp.zeros_like(acc_s)

        @pl.when((i == 0) & (j == 0))
        def _():
            dg_ref[...] = jnp.zeros_like(dg_ref)

        part = None
        for p in range(npair):
            d = _dot_nt(dy_refs[p][...], pairs[p][3](w_refs[p][...]))
            part = d if part is None else part + d
        acc_s[...] += part

        @pl.when(j == nch - 1)
        def _():
            dxn = acc_s[...]
            hh = h_ref[...]
            gg = g_ref[...]
            r = _rstd(hh)
            hr = hh * r
            xn_ref[...] = (hr * gg).astype(BF16)
            dg_ref[...] += jnp.sum(dxn * hr, axis=0, keepdims=True)
            gd = gg * dxn
            dh_ref[...] = dres_ref[...] + r * (gd - hr * jnp.mean(gd * hr, axis=-1, keepdims=True))

    row = pl.BlockSpec((tm, D_MODEL), lambda i, j: (i, 0))
    vec = pl.BlockSpec((1, D_MODEL), lambda i, j: (0, 0))
    return pl.pallas_call(
        body,
        name=name,
        grid=(T // tm, nch),
        in_specs=[pl.BlockSpec((tm, tn), lambda i, j: (i, j))] * npair + [p[2] for p in pairs] + [row, vec, row],
        out_specs=[row, row, vec],
        out_shape=[
            jax.ShapeDtypeStruct((T, D_MODEL), F32),
            jax.ShapeDtypeStruct((T, D_MODEL), BF16),
            jax.ShapeDtypeStruct((1, D_MODEL), F32),
        ],
        scratch_shapes=[pltpu.VMEM((tm, D_MODEL), F32)],
        compiler_params=_cparams(),
    )(*[p[0] for p in pairs], *[p[1] for p in pairs], h, g, dres)


def loss_bwd(h, g, tgt):
    T = h.shape[0]
    tm = 512

    def body(h_ref, g_ref, t_ref, dh_ref, loss_ref, dg_ref):
        @pl.when(pl.program_id(0) == 0)
        def _():
            loss_ref[...] = jnp.zeros_like(loss_ref)
            dg_ref[...] = jnp.zeros_like(dg_ref)

        hh = h_ref[...]
        gg = g_ref[...]
        r = _rstd(hh)
        hr = hh * r
        err = hr * gg - t_ref[...]
        loss_ref[...] += 0.5 * jnp.sum(jnp.mean(err * err, axis=-1, keepdims=True), axis=0, keepdims=True)
        dy = err * (1.0 / D_MODEL)
        dg_ref[...] += jnp.sum(dy * hr, axis=0, keepdims=True)
        gd = gg * dy
        dh_ref[...] = r * (gd - hr * jnp.mean(gd * hr, axis=-1, keepdims=True))

    row = pl.BlockSpec((tm, D_MODEL), lambda i: (i, 0))
    vec = pl.BlockSpec((1, D_MODEL), lambda i: (0, 0))
    return pl.pallas_call(
        body,
        name="loss_bwd",
        grid=(T // tm,),
        in_specs=[row, vec, row],
        out_specs=[row, pl.BlockSpec((1, LANES), lambda i: (0, 0)), vec],
        out_shape=[
            jax.ShapeDtypeStruct((T, D_MODEL), F32),
            jax.ShapeDtypeStruct((1, LANES), F32),
            jax.ShapeDtypeStruct((1, D_MODEL), F32),
        ],
        compiler_params=_cparams(),
    )(h, g, tgt)


def mm_tn(name, a, b, tk, tn, out_block, out_index, out_shape, prev=None, tt=1024):
    T = a.shape[0]
    ns, r = out_block[1], out_block[3]
    tt = min(tt, T)
    nt = T // tt

    def body(*refs):
        if prev is None:
            a_ref, b_ref, out_ref = refs
        else:
            a_ref, b_ref, _, out_ref = refs
        t = pl.program_id(2)
        res = _dot_tn(a_ref[...], b_ref[...])

        @pl.when(t == 0)
        def _():
            for u in range(ns):
                out_ref[u] = res[u * r:(u + 1) * r]

        @pl.when(t > 0)
        def _():
            for u in range(ns):
                out_ref[u] += res[u * r:(u + 1) * r]

    in_specs = [
        pl.BlockSpec((tt, tk), lambda k, n, t: (t, k)),
        pl.BlockSpec((tt, tn), lambda k, n, t: (t, n)),
    ]
    args = [a, b]
    aliases = {}
    if prev is not None:
        in_specs.append(pl.BlockSpec(memory_space=pl.ANY))
        args.append(prev)
        aliases = {2: 0}
    return pl.pallas_call(
        body,
        name=name,
        grid=(a.shape[1] // tk, b.shape[1] // tn, nt),
        in_specs=in_specs,
        out_specs=pl.BlockSpec(out_block, lambda k, n, t: out_index(k, n)),
        out_shape=jax.ShapeDtypeStruct(out_shape, F32),
        input_output_aliases=aliases,
        compiler_params=_cparams(),
    )(*args)


SB_BLOCK = 256


def _log_sigmoid(z):
    return jnp.minimum(z, 0.0) - jnp.log1p(jnp.exp(-jnp.abs(z)))


def _split_bf16(x):
    hi = x.astype(BF16)
    lo = (x - hi.astype(F32)).astype(BF16)
    return hi, lo


def sb_fwd(qkv):
    T = qkv.shape[0]
    tq = SB_BLOCK
    npair = SB_HEADS // 2

    def body(q_ref, k_ref, v_ref, o_ref, tot_ref):
        i = pl.program_id(1)
        q = q_ref[...]
        lane = lax.broadcasted_iota(jnp.int32, (tq, LANES), 1)
        first = lane < HEAD_DIM
        zero = jnp.zeros_like(q)
        q_heads = (jnp.where(first, q, zero), jnp.where(first, zero, q))
        row = lax.broadcasted_iota(jnp.int32, (tq, tq), 0)
        col = lax.broadcasted_iota(jnp.int32, (tq, tq), 1)
        strict = col < row
        later = (row > col).astype(BF16)

        def block(j, carry, diag):
            start = pl.multiple_of(j * tq, tq)
            kb = k_ref[pl.ds(start, tq), :]
            vb = v_ref[pl.ds(start, tq), :]
            new = []
            for hd in range(2):
                c, acc = carry[2 * hd], carry[2 * hd + 1]
                z = _dot_nt(q_heads[hd], kb)
                lb = _log_sigmoid(z)
                lm = lb - z
                if diag:
                    lm = jnp.where(strict, lm, 0.0)
                hi, lo = _split_bf16(lm)
                suffix = _dot(hi, later) + _dot(lo, later)
                w = jnp.exp(lb + suffix + c)
                if diag:
                    w = jnp.where(strict, w, 0.0)
                acc = acc + _dot(w.astype(BF16), vb)
                c = c + jnp.sum(lm, axis=1, keepdims=True)
                new += [c, acc]
            return tuple(new)

        c0 = jnp.zeros((tq, 1), F32)
        a0 = jnp.zeros((tq, LANES), F32)
        carry = block(i, (c0, a0, c0, a0), True)
        carry = lax.fori_loop(0, i, lambda jj, cr: block(i - 1 - jj, cr, False), carry)
        o_ref[...] = jnp.where(first, carry[1], carry[3]).astype(BF16)
        tot_ref[...] = jnp.where(first, carry[0], carry[2])

    return pl.pallas_call(
        body,
        name="sb_fwd",
        grid=(npair, T // tq),
        in_specs=[
            pl.BlockSpec((tq, LANES), lambda p, i: (i, p)),
            pl.BlockSpec((T, LANES), lambda p, i: (0, npair + p)),
            pl.BlockSpec((T, LANES), lambda p, i: (0, 2 * npair + p)),
        ],
        out_specs=[pl.BlockSpec((tq, LANES), lambda p, i: (i, p))] * 2,
        out_shape=[jax.ShapeDtypeStruct((T, D_MODEL), BF16), jax.ShapeDtypeStruct((T, D_MODEL), F32)],
        compiler_params=_cparams(),
    )(qkv, qkv, qkv)


def sb_bwd(qkv, do, tot):
    T = qkv.shape[0]
    tq = SB_BLOCK
    npair = SB_HEADS // 2
    nq = T // tq

    def body(q_ref, k_ref, v_ref, do_ref, tot_ref, dq_ref, dk_ref, dv_ref, dk_s, dv_s):
        i = pl.program_id(1)

        @pl.when(i == 0)
        def _():
            dk_s[...] = jnp.zeros_like(dk_s)
            dv_s[...] = jnp.zeros_like(dv_s)

        q = q_ref[...]
        do_ = do_ref[...]
        tot_ = tot_ref[...]
        lane = lax.broadcasted_iota(jnp.int32, (tq, LANES), 1)
        first = lane < HEAD_DIM
        zero = jnp.zeros_like(q)
        q_heads = (jnp.where(first, q, zero), jnp.where(first, zero, q))
        do_heads = (jnp.where(first, do_, zero), jnp.where(first, zero, do_))
        tots = (tot_[:, 0:1], tot_[:, HEAD_DIM:HEAD_DIM + 1])
        row = lax.broadcasted_iota(jnp.int32, (tq, tq), 0)
        col = lax.broadcasted_iota(jnp.int32, (tq, tq), 1)
        strict = col < row
        upto = (row <= col).astype(BF16)
        before = (row < col).astype(BF16)

        def block(j, carry, diag):
            start = pl.multiple_of(j * tq, tq)
            kb = k_ref[pl.ds(start, tq), :]
            vb = v_ref[pl.ds(start, tq), :]
            new = []
            dk_add = None
            dv_add = None
            for hd in range(2):
                clm, cg, dq_acc = carry[3 * hd], carry[3 * hd + 1], carry[3 * hd + 2]
                z = _dot_nt(q_heads[hd], kb)
                lb = _log_sigmoid(z)
                lm = lb - z
                if diag:
                    lm = jnp.where(strict, lm, 0.0)
                hi, lo = _split_bf16(lm)
                prefix = _dot(hi, upto) + _dot(lo, upto)
                a = jnp.exp(lb + (tots[hd] - (clm + prefix)))
                if diag:
                    a = jnp.where(strict, a, 0.0)
                da = _dot_nt(do_heads[hd], vb)
                gmat = a * da
                pex = cg + _dot(gmat.astype(BF16), before)
                dz = gmat - jnp.exp(lb) * (gmat + pex)
                if diag:
                    dz = jnp.where(strict, dz, 0.0)
                dzb = dz.astype(BF16)
                dq_acc = dq_acc + _dot(dzb, kb)
                dk_h = _dot_tn(dzb, q_heads[hd])
                dv_h = _dot_tn(a.astype(BF16), do_heads[hd])
                dk_add = dk_h if dk_add is None else dk_add + dk_h
                dv_add = dv_h if dv_add is None else dv_add + dv_h
                clm = clm + jnp.sum(lm, axis=1, keepdims=True)
                cg = cg + jnp.sum(gmat, axis=1, keepdims=True)
                new += [clm, cg, dq_acc]
            dk_s[pl.ds(start, tq), :] += dk_add
            dv_s[pl.ds(start, tq), :] += dv_add
            return tuple(new)

        c0 = jnp.zeros((tq, 1), F32)
        a0 = jnp.zeros((tq, LANES), F32)
        carry = lax.fori_loop(0, i, lambda j, cr: block(j, cr, False), (c0, c0, a0, c0, c0, a0))
        carry = block(i, carry, True)
        dq_ref[...] = (jnp.where(first, carry[2], carry[5]) * ATTN_SCALE).astype(BF16)

        @pl.when(i == nq - 1)
        def _():
            dk_ref[...] = dk_s[...].astype(BF16)
            dv_ref[...] = dv_s[...].astype(BF16)

    qblk = pl.BlockSpec((tq, LANES), lambda p, i: (i, p))
    full = pl.BlockSpec((T, LANES), lambda p, i: (0, p))
    return pl.pallas_call(
        body,
        name="sb_bwd",
        grid=(npair, nq),
        in_specs=[
            qblk,
            pl.BlockSpec((T, LANES), lambda p, i: (0, npair + p)),
            pl.BlockSpec((T, LANES), lambda p, i: (0, 2 * npair + p)),
            qblk, qblk,
        ],
        out_specs=[qblk, full, full],
        out_shape=[jax.ShapeDtypeStruct((T, D_MODEL), BF16)] * 3,
        scratch_shapes=[pltpu.VMEM((T, LANES), F32), pltpu.VMEM((T, LANES), F32)],
        compiler_params=_cparams(),
    )(qkv, qkv, qkv, do, tot)


def _swa_valid(n):
    qi = lax.broadcasted_iota(jnp.int32, (WINDOW, 2 * WINDOW), 0)
    ki = lax.broadcasted_iota(jnp.int32, (WINDOW, 2 * WINDOW), 1)
    diff = qi + WINDOW - ki
    return (diff >= 0) & (diff < WINDOW) & ((n > 0) | (ki >= WINDOW))


def _to_half(x, first, src, dst):
    keep = first if src == 0 else jnp.logical_not(first)
    x = jnp.where(keep, x, jnp.zeros_like(x))
    if src != dst:
        x = pltpu.roll(x.astype(F32), HEAD_DIM, 1).astype(BF16)
    return x


def _kv_band(prev_ref, cur_ref, pb):
    cols = slice(pb * LANES, (pb + 1) * LANES)
    return jnp.concatenate([prev_ref[:, cols], cur_ref[:, cols]], axis=0)


def _swa_specs(T):
    nb = T // WINDOW
    kv_w = SWA_KV_HEADS * HEAD_DIM
    qrow = pl.BlockSpec((WINDOW, D_MODEL), lambda n: (n, 0))
    cur = pl.BlockSpec((WINDOW, kv_w), lambda n: (n, 0))
    prev = pl.BlockSpec((WINDOW, kv_w), lambda n: (jnp.maximum(n - 1, 0), 0))
    smem = pl.BlockSpec(memory_space=pltpu.SMEM)
    return nb, qrow, cur, prev, smem


def swa_fwd(q, k, v, sinks):
    T = q.shape[0]
    nb, qrow, cur, prev, smem = _swa_specs(T)

    def body(sink_ref, q_ref, kc_ref, kp_ref, vc_ref, vp_ref, o_ref, lse_ref):
        n = pl.program_id(0)
        lane = lax.broadcasted_iota(jnp.int32, (WINDOW, LANES), 1)
        first = lane < HEAD_DIM
        valid = _swa_valid(n)
        lse_acc = jnp.zeros((WINDOW, LANES), F32)
        for pb in range(SWA_KV_HEADS // 2):
            k2 = _kv_band(kp_ref, kc_ref, pb)
            v2 = _kv_band(vp_ref, vc_ref, pb)
            for b in range(2):
                kvh = 2 * pb + b
                for qq in range(2):
                    cols = slice((2 * kvh + qq) * LANES, (2 * kvh + qq + 1) * LANES)
                    qp = q_ref[:, cols]
                    outs = []
                    for a in range(2):
                        head = 4 * kvh + 2 * qq + a
                        qh = _to_half(qp, first, a, b)
                        s = jnp.where(valid, _dot_nt(qh, k2), NEG_BIG)
                        sink = sink_ref[head]
                        m = jnp.maximum(jnp.max(s, axis=1, keepdims=True), sink)
                        p = jnp.exp(s - m)
                        den = jnp.sum(p, axis=1, keepdims=True) + jnp.exp(sink - m)
                        o = _dot((p / den).astype(BF16), v2)
                        if a != b:
                            o = pltpu.roll(o, HEAD_DIM, 1)
                        outs.append(o)
                        lse_acc = jnp.where(lane == head, m + jnp.log(den), lse_acc)
                    o_ref[:, cols] = jnp.where(first, outs[0], outs[1]).astype(BF16)
        lse_ref[...] = lse_acc

    return pl.pallas_call(
        body,
        name="swa_fwd",
        grid=(nb,),
        in_specs=[smem, qrow, cur, prev, cur, prev],
        out_specs=[qrow, pl.BlockSpec((WINDOW, LANES), lambda n: (n, 0))],
        out_shape=[jax.ShapeDtypeStruct((T, D_MODEL), BF16), jax.ShapeDtypeStruct((T, LANES), F32)],
        compiler_params=_cparams(),
    )(sinks, q, k, k, v, v)


def swa_bwd(q, k, v, sinks, do, lse, cos, sin):
    T = q.shape[0]
    nb, qrow, cur, prev, smem = _swa_specs(T)
    kv_w = SWA_KV_HEADS * HEAD_DIM

    def body(sink_ref, q_ref, kc_ref, kp_ref, vc_ref, vp_ref, do_ref, lse_ref, cos_ref, sin_ref,
             dq_ref, own_ref, prv_ref, dsink_ref):
        n = pl.program_id(0)

        @pl.when(n == 0)
        def _():
            dsink_ref[...] = jnp.zeros_like(dsink_ref)

        lane = lax.broadcasted_iota(jnp.int32, (WINDOW, LANES), 1)
        lane1 = lax.broadcasted_iota(jnp.int32, (1, LANES), 1)
        first = lane < HEAD_DIM
        valid = _swa_valid(n)
        cos_ = cos_ref[...]
        sin_ = sin_ref[...]
        dsink = jnp.zeros((1, LANES), F32)
        for pb in range(SWA_KV_HEADS // 2):
            k2 = _kv_band(kp_ref, kc_ref, pb)
            v2 = _kv_band(vp_ref, vc_ref, pb)
            dk2 = jnp.zeros((2 * WINDOW, LANES), F32)
            dv2 = jnp.zeros((2 * WINDOW, LANES), F32)
            for b in range(2):
                kvh = 2 * pb + b
                for qq in range(2):
                    cols = slice((2 * kvh + qq) * LANES, (2 * kvh + qq + 1) * LANES)
                    qp = q_ref[:, cols]
                    dop = do_ref[:, cols]
                    dqs = []
                    for a in range(2):
                        head = 4 * kvh + 2 * qq + a
                        qh = _to_half(qp, first, a, b)
                        doh = _to_half(dop, first, a, b)
                        s = jnp.where(valid, _dot_nt(qh, k2), NEG_BIG)
                        lse_h = lse_ref[:, head:head + 1]
                        p = jnp.exp(s - lse_h)
                        dp = _dot_nt(doh, v2)
                        delta = jnp.sum(p * dp, axis=1, keepdims=True)
                        ds = (p * (dp - delta)).astype(BF16)
                        p_sink = jnp.exp(sink_ref[head] - lse_h)
                        dsink = dsink + jnp.where(lane1 == head, -jnp.sum(p_sink * delta, axis=0, keepdims=True), 0.0)
                        dq = _dot(ds, k2)
                        if a != b:
                            dq = pltpu.roll(dq, HEAD_DIM, 1)
                        dqs.append(dq)
                        dk2 = dk2 + _dot_tn(ds, qh)
                        dv2 = dv2 + _dot_tn(p.astype(BF16), doh)
                    dqp = jnp.where(first, dqs[0], dqs[1])
                    dq_ref[:, cols] = ((dqp * cos_ + _swap32(dqp * sin_)) * ATTN_SCALE).astype(BF16)
            kcols = slice(pb * LANES, (pb + 1) * LANES)
            vcols = slice(kv_w + pb * LANES, kv_w + (pb + 1) * LANES)
            prv_ref[:, kcols] = dk2[:WINDOW]
            own_ref[:, kcols] = dk2[WINDOW:]
            prv_ref[:, vcols] = dv2[:WINDOW]
            own_ref[:, vcols] = dv2[WINDOW:]
        dsink_ref[...] += dsink

    tab = pl.BlockSpec((WINDOW, LANES), lambda n: (n, 0))
    kvrow = pl.BlockSpec((WINDOW, 2 * kv_w), lambda n: (n, 0))
    return pl.pallas_call(
        body,
        name="swa_bwd",
        grid=(nb,),
        in_specs=[smem, qrow, cur, prev, cur, prev, qrow, tab, tab, tab],
        out_specs=[qrow, kvrow, kvrow, pl.BlockSpec((1, LANES), lambda n: (0, 0))],
        out_shape=[
            jax.ShapeDtypeStruct((T, D_MODEL), BF16),
            jax.ShapeDtypeStruct((T, 2 * kv_w), F32),
            jax.ShapeDtypeStruct((T, 2 * kv_w), F32),
            jax.ShapeDtypeStruct((1, LANES), F32),
        ],
        compiler_params=_cparams(),
    )(sinks, q, k, k, v, v, do, lse, cos, sin)


def kv_grad_combine(own, prv, cos, sin):
    T = own.shape[0]
    nb = T // WINDOW
    kv_w = SWA_KV_HEADS * HEAD_DIM

    def body(own_ref, nxt_ref, cos_ref, sin_ref, out_ref):
        n = pl.program_id(0)
        nxt = jnp.where(n + 1 < nb, nxt_ref[...], 0.0)
        tot = own_ref[...] + nxt
        dk = tot[:, :kv_w]
        c = _tile_lanes(cos_ref[...], kv_w)
        s = _tile_lanes(sin_ref[...], kv_w)
        out_ref[:, :kv_w] = (dk * c + _swap32(dk * s)).astype(BF16)
        out_ref[:, kv_w:] = tot[:, kv_w:].astype(BF16)

    tab = pl.BlockSpec((WINDOW, LANES), lambda n: (n, 0))
    kvrow = pl.BlockSpec((WINDOW, 2 * kv_w), lambda n: (n, 0))
    return pl.pallas_call(
        body,
        name="kv_grad_combine",
        grid=(nb,),
        in_specs=[kvrow, pl.BlockSpec((WINDOW, 2 * kv_w), lambda n: (jnp.minimum(n + 1, nb - 1), 0)), tab, tab],
        out_specs=kvrow,
        out_shape=jax.ShapeDtypeStruct((T, 2 * kv_w), BF16),
        compiler_params=_cparams(),
    )(own, prv, cos, sin)


ANY = pl.BlockSpec(memory_space=pl.ANY)


def _place():
    x, y, c = lax.axis_index("x"), lax.axis_index("y"), lax.axis_index("c")
    other_chips = [(1 - x, y), (x, 1 - y), (1 - x, 1 - y)]
    return x, y, c, 2 * x + y, other_chips


def all_gather_weights(shards):
    n = len(shards)

    def body(*refs):
        ins, outs = refs[:n], refs[n:2 * n]
        send_sems, recv_sems, local_sems = refs[2 * n:]
        _, _, c, me, chips = _place()
        copies = []
        for t in range(n):
            cp = pltpu.make_async_copy(ins[t], outs[t].at[me], local_sems.at[t])
            cp.start()
            copies.append(cp)
            for jdx, (px, py) in enumerate(chips):
                cp = pltpu.make_async_remote_copy(
                    src_ref=ins[t], dst_ref=outs[t].at[me], send_sem=send_sems.at[t, jdx],
                    recv_sem=recv_sems.at[t, jdx], device_id=(px, py, c), device_id_type=MESH)
                cp.start()
                copies.append(cp)
        for cp in copies:
            cp.wait()

    return pl.pallas_call(
        body,
        name="all_gather_weights",
        in_specs=[ANY] * n,
        out_specs=[ANY] * n,
        out_shape=[jax.ShapeDtypeStruct((N_CHIPS,) + s.shape, s.dtype) for s in shards],
        scratch_shapes=[
            pltpu.SemaphoreType.DMA((n, 3)), pltpu.SemaphoreType.DMA((n, 3)), pltpu.SemaphoreType.DMA((n,)),
        ],
    )(*shards)


def exchange_halves(slabs):
    n = len(slabs)

    def body(*refs):
        ins, mine, theirs = refs[:n], refs[n:2 * n], refs[2 * n:3 * n]
        send_sems, recv_sems, local_sems = refs[3 * n:]
        x, y, c, _, _ = _place()
        copies = []
        for t in range(n):
            cp = pltpu.make_async_copy(ins[t].at[c], mine[t], local_sems.at[t])
            cp.start()
            copies.append(cp)
            cp = pltpu.make_async_remote_copy(
                src_ref=ins[t].at[1 - c], dst_ref=theirs[t], send_sem=send_sems.at[t],
                recv_sem=recv_sems.at[t], device_id=(x, y, 1 - c), device_id_type=MESH)
            cp.start()
            copies.append(cp)
        for cp in copies:
            cp.wait()

    half = [jax.ShapeDtypeStruct(s.shape[1:], s.dtype) for s in slabs]
    return pl.pallas_call(
        body,
        name="exchange_halves",
        in_specs=[ANY] * n,
        out_specs=[ANY] * (2 * n),
        out_shape=half + half,
        scratch_shapes=[pltpu.SemaphoreType.DMA((n,)), pltpu.SemaphoreType.DMA((n,)), pltpu.SemaphoreType.DMA((n,))],
    )(*slabs)


def exchange_chip_partials(parts):
    n = len(parts)

    def body(*refs):
        ins, outs = refs[:n], refs[n:2 * n]
        send_sems, recv_sems, local_sems = refs[2 * n:]
        _, _, c, me, chips = _place()
        copies = []
        for t in range(n):
            cp = pltpu.make_async_copy(ins[t].at[me], outs[t].at[me], local_sems.at[t])
            cp.start()
            copies.append(cp)
            for jdx, (px, py) in enumerate(chips):
                cp = pltpu.make_async_remote_copy(
                    src_ref=ins[t].at[2 * px + py], dst_ref=outs[t].at[me], send_sem=send_sems.at[t, jdx],
                    recv_sem=recv_sems.at[t, jdx], device_id=(px, py, c), device_id_type=MESH)
                cp.start()
                copies.append(cp)
        for cp in copies:
            cp.wait()

    return pl.pallas_call(
        body,
        name="exchange_chip_partials",
        in_specs=[ANY] * n,
        out_specs=[ANY] * n,
        out_shape=[jax.ShapeDtypeStruct(p.shape, p.dtype) for p in parts],
        scratch_shapes=[
            pltpu.SemaphoreType.DMA((n, 3)), pltpu.SemaphoreType.DMA((n, 3)), pltpu.SemaphoreType.DMA((n,)),
        ],
    )(*parts)


def share_reduced_halves(halves):
    n = len(halves)

    def body(*refs):
        ins, outs = refs[:n], refs[n:2 * n]
        send_sems, recv_sems, local_sems = refs[2 * n:]
        x, y, c, _, _ = _place()
        copies = []
        for t in range(n):
            cp = pltpu.make_async_copy(ins[t], outs[t].at[c], local_sems.at[t])
            cp.start()
            copies.append(cp)
            cp = pltpu.make_async_remote_copy(
                src_ref=ins[t], dst_ref=outs[t].at[c], send_sem=send_sems.at[t],
                recv_sem=recv_sems.at[t], device_id=(x, y, 1 - c), device_id_type=MESH)
            cp.start()
            copies.append(cp)
        for cp in copies:
            cp.wait()

    return pl.pallas_call(
        body,
        name="share_reduced_halves",
        in_specs=[ANY] * n,
        out_specs=[ANY] * n,
        out_shape=[jax.ShapeDtypeStruct((2,) + h.shape, h.dtype) for h in halves],
        scratch_shapes=[pltpu.SemaphoreType.DMA((n,)), pltpu.SemaphoreType.DMA((n,)), pltpu.SemaphoreType.DMA((n,))],
    )(*halves)


def _row_tile(r, c):
    tr = r
    while tr * c * 4 > (3 << 19) and tr % 16 == 0:
        tr //= 2
    return tr


def add_slabs(name, a, b):
    n, r, c = a.shape
    tr = _row_tile(r, c)

    def body(a_ref, b_ref, o_ref):
        o_ref[...] = a_ref[...] + b_ref[...]

    blk = pl.BlockSpec((None, tr, c), lambda i, j: (i, j, 0))
    return pl.pallas_call(
        body, name=name, grid=(n, r // tr), in_specs=[blk, blk], out_specs=blk,
        out_shape=jax.ShapeDtypeStruct(a.shape, F32), compiler_params=_cparams(),
    )(a, b)


def sum_chips(name, x):
    _, slots, r, c = x.shape
    tr = _row_tile(r, c)

    def body(a_ref, b_ref, c_ref, d_ref, o_ref):
        o_ref[...] = ((a_ref[...] + b_ref[...]) + c_ref[...]) + d_ref[...]

    def src(s):
        return pl.BlockSpec((None, None, tr, c), lambda i, j: (s, i, j, 0))

    return pl.pallas_call(
        body, name=name, grid=(slots, r // tr),
        in_specs=[src(0), src(1), src(2), src(3)],
        out_specs=pl.BlockSpec((None, tr, c), lambda i, j: (i, j, 0)),
        out_shape=jax.ShapeDtypeStruct((slots, r, c), F32), compiler_params=_cparams(),
    )(x, x, x, x)


def _adamw_math(w, g, m, v):
    m = ADAM_B1 * m + (1.0 - ADAM_B1) * g
    v = ADAM_B2 * v + (1.0 - ADAM_B2) * (g * g)
    m_hat = m / (1.0 - ADAM_B1 ** ADAM_STEP)
    v_hat = v / (1.0 - ADAM_B2 ** ADAM_STEP)
    delta = -ADAM_LR * (m_hat / (jnp.sqrt(v_hat) + ADAM_EPS) + ADAM_WD * w)
    return delta, m, v


def adamw_shard(name, w, m, v, gfull, slot0, row_halves):
    n = w.shape[0]
    _, _, r, c = gfull.shape
    tr = _row_tile(r, c)
    nr = r // tr

    def body(w_ref, m_ref, v_ref, g_ref, go_ref, d_ref, mo_ref, vo_ref):
        g = g_ref[...]
        delta, mm, vv = _adamw_math(w_ref[...], g, m_ref[...], v_ref[...])
        go_ref[...] = g
        d_ref[...] = delta
        mo_ref[...] = mm
        vo_ref[...] = vv

    if row_halves:
        wspec = pl.BlockSpec((None, tr, c), lambda l, h, i: (l, h * nr + i, 0))
    else:
        wspec = pl.BlockSpec((None, tr, c), lambda l, h, i: (l, i, h))
    gspec = pl.BlockSpec((None, None, tr, c), lambda l, h, i: (h, slot0 + l, i, 0))
    shp = jax.ShapeDtypeStruct(w.shape, F32)
    return pl.pallas_call(
        body, name=name, grid=(n, 2, nr),
        in_specs=[wspec, wspec, wspec, gspec],
        out_specs=[wspec] * 4, out_shape=[shp] * 4, compiler_params=_cparams(),
    )(w, m, v, gfull)


SMALL_ROWS = 16


def small_allreduce_adamw(part, w, m, v):
    def body(p_ref, w_ref, m_ref, v_ref, g_ref, d_ref, mo_ref, vo_ref, buf, send_sems, recv_sems):
        x, y, c, _, _ = _place()
        me = 4 * x + 2 * y + c
        buf[me] = p_ref[...]
        copies = []
        for k in range(1, N_DEV):
            kx, ky, kc = (k >> 2) & 1, (k >> 1) & 1, k & 1
            peer = (x ^ kx, y ^ ky, c ^ kc)
            cp = pltpu.make_async_remote_copy(
                src_ref=p_ref, dst_ref=buf.at[me], send_sem=send_sems.at[k - 1],
                recv_sem=recv_sems.at[k - 1], device_id=peer, device_id_type=MESH)
            cp.start()
            copies.append(cp)
        for cp in copies:
            cp.wait()
        g = buf[0]
        for dev in range(1, N_DEV):
            g = g + buf[dev]
        delta, mm, vv = _adamw_math(w_ref[...], g, m_ref[...], v_ref[...])
        g_ref[...] = g
        d_ref[...] = delta
        mo_ref[...] = mm
        vo_ref[...] = vv

    vm = pl.BlockSpec(memory_space=pltpu.VMEM)
    shp = jax.ShapeDtypeStruct(part.shape, F32)
    return pl.pallas_call(
        body, name="small_allreduce_adamw",
        in_specs=[vm] * 4, out_specs=[vm] * 4, out_shape=[shp] * 4,
        scratch_shapes=[
            pltpu.VMEM((N_DEV,) + part.shape, F32),
            pltpu.SemaphoreType.DMA((N_DEV - 1,)), pltpu.SemaphoreType.DMA((N_DEV - 1,)),
        ],
    )(part, w, m, v)


def _rope_tables(T):
    half = HEAD_DIM // 2
    inv_freq = ROPE_THETA ** (-jnp.arange(half, dtype=F32) / half)
    ang = jnp.arange(T).astype(F32)[:, None] * inv_freq[None, :]
    cos = jnp.tile(jnp.cos(ang), (1, LANES // half))
    sin = jnp.tile(jnp.sin(ang), (1, LANES // half))
    lane = jnp.arange(LANES)
    sign = jnp.where((lane % HEAD_DIM) < half, -1.0, 1.0).astype(F32)
    return cos, sin * sign[None, :]


def _pack_small(ffn1, mix, ffn2, kvn, fin, sinks, loss_row):
    sink_row = jnp.zeros((1, D_MODEL), F32).at[0, :SWA_Q_HEADS].set(sinks.reshape(-1))
    rows = jnp.concatenate([ffn1, mix, ffn2, kvn.reshape(1, -1), fin.reshape(1, -1), sink_row, loss_row], axis=0)
    return jnp.concatenate([rows, jnp.zeros((SMALL_ROWS - rows.shape[0], D_MODEL), F32)], axis=0)


def kernel(x, ffn1_norm, ffn1_w_in, ffn1_w_out, mix_norm, ffn2_norm, ffn2_w_in, ffn2_w_out, sb_w_qkv, sb_w_o, kv_norm, kv_w, swa_w_q, swa_sinks, swa_w_o, final_norm, loss_target, m_ffn1_norm, m_ffn1_w_in, m_ffn1_w_out, m_mix_norm, m_ffn2_norm, m_ffn2_w_in, m_ffn2_w_out, m_sb_w_qkv, m_sb_w_o, m_kv_norm, m_kv_w, m_swa_w_q, m_swa_sinks, m_swa_w_o, m_final_norm, v_ffn1_norm, v_ffn1_w_in, v_ffn1_w_out, v_mix_norm, v_ffn2_norm, v_ffn2_w_in, v_ffn2_w_out, v_sb_w_qkv, v_sb_w_o, v_kv_norm, v_kv_w, v_swa_w_q, v_swa_sinks, v_swa_w_o, v_final_norm):
    T = x.shape[1]
    kv_cols = SWA_KV_HEADS * HEAD_DIM
    x2 = x.reshape(T, D_MODEL)
    tgt = loss_target.reshape(T, D_MODEL)
    cos, sin = _rope_tables(T)

    w_in_l = jnp.concatenate([ffn1_w_in, ffn2_w_in], axis=0).astype(BF16)
    w_out_l = jnp.concatenate([ffn1_w_out, ffn2_w_out], axis=0).astype(BF16)
    sq_l = jnp.concatenate([sb_w_o, swa_w_q, swa_w_o], axis=0).astype(BF16)
    qkv_l = sb_w_qkv[0].astype(BF16)
    kvw_l = kv_w.astype(BF16)
    w_in, w_out, w_sq, w_qkv, w_kv = all_gather_weights([w_in_l, w_out_l, sq_l, qkv_l, kvw_l])
    w_kv = w_kv.reshape(D_MODEL, 2 * kv_cols)

    def vec(a, i):
        return a[i].reshape(1, D_MODEL)

    ident = lambda w: w
    sq_prep = lambda w: w.reshape(D_MODEL, w.shape[-1])
    qscale = jnp.concatenate([jnp.full((1, D_MODEL), ATTN_SCALE, F32), jnp.ones((1, 2 * D_MODEL), F32)], axis=1)
    swa_scale = jnp.full((1, D_MODEL), ATTN_SCALE, F32)
    sinks = swa_sinks.reshape(SWA_Q_HEADS)

    h1, gate1, up1 = ffn_fwd(x2, vec(ffn1_norm, 0), w_in, w_out, SLOT_FFN1[0])
    qkv = rms_linear("sb_qkv", h1, vec(mix_norm, 0), w_qkv,
                     pl.BlockSpec((None, D_MODEL, QKV_COLS), lambda i, j: (j, 0, 0)), ident,
                     3 * D_MODEL, QKV_COLS, scale=qscale)
    o_sb, tot = sb_fwd(qkv)
    h2 = linear_res("sb_out", o_sb, w_sq, SQ_SB_O, h1)
    h3, gate2, up2 = ffn_fwd(h2, vec(ffn2_norm, 0), w_in, w_out, SLOT_FFN2[0])
    kvn = kv_norm.reshape(1, D_MODEL)
    k_sw = rms_linear("kv_k", h3, kvn, w_kv, pl.BlockSpec((D_MODEL, kv_cols), lambda i, j: (0, 0)), ident,
                      kv_cols, kv_cols, rope=(cos, sin))
    v_sw = rms_linear("kv_v", h3, kvn, w_kv, pl.BlockSpec((D_MODEL, kv_cols), lambda i, j: (0, 1)), ident,
                      kv_cols, kv_cols)
    h4, gate3, up3 = ffn_fwd(h3, vec(ffn1_norm, 1), w_in, w_out, SLOT_FFN1[1])
    q_sw = rms_linear("swa_q", h4, vec(mix_norm, 1), w_sq,
                      pl.BlockSpec((N_CHIPS, None, SQ_ROWS, 512), lambda i, j: (0, SQ_SWA_Q, 0, j)), sq_prep,
                      D_MODEL, 512, rope=(cos, sin), scale=swa_scale)
    o_sw, lse = swa_fwd(q_sw, k_sw, v_sw, sinks)
    h5 = linear_res("swa_out", o_sw, w_sq, SQ_SWA_O, h4)
    h6, gate4, up4 = ffn_fwd(h5, vec(ffn2_norm, 1), w_in, w_out, SLOT_FFN2[1])
    dh6, loss_p, d_final = loss_bwd(h6, final_norm.reshape(1, D_MODEL), tgt)

    slab = {"in": None, "out": None, "sq": None}
    in_shape = (2, N_CHIPS, 4, D_MODEL // 2, FF_CHUNK)
    out_shape = (2, N_CHIPS, 4, FF_ROWS, D_MODEL // 2)
    sq_shape = (2, N_CHIPS, 3, SQ_ROWS, D_MODEL // 2)

    def ffn_bwd(tag, dh, h_in, g, gate, up, slot):
        dg_, du_, act, dhb = ffn_bwd_act(dh, gate, up, w_out, slot)
        pairs = [
            (dg_, w_in, pl.BlockSpec((None, None, D_MODEL, FF_CHUNK), lambda i, j: (j, slot, 0, 0)), ident),
            (du_, w_in, pl.BlockSpec((None, None, D_MODEL, FF_CHUNK), lambda i, j: (2 + j, slot, 0, 0)), ident),
        ]
        dh_in, xn, dnorm = linear_bwd_rms(f"ffn_bwd_in_{tag}", pairs, h_in, g, dh, D_FF // FF_CHUNK, FF_CHUNK)
        blk = (None, 1, None, D_MODEL // 2, FF_CHUNK)
        slab["in"] = mm_tn(f"dw_gate_{tag}", xn, dg_, D_MODEL // 2, FF_CHUNK, blk,
                           lambda k, n: (k, n, slot, 0, 0), in_shape, prev=slab["in"])
        slab["in"] = mm_tn(f"dw_up_{tag}", xn, du_, D_MODEL // 2, FF_CHUNK, blk,
                           lambda k, n: (k, 2 + n, slot, 0, 0), in_shape, prev=slab["in"])
        slab["out"] = mm_tn(f"dw_out_{tag}", act, dhb, FF_CHUNK, D_MODEL // 2,
                            (None, 2, None, FF_ROWS, D_MODEL // 2),
                            lambda k, n: (n, k, slot, 0, 0), out_shape, prev=slab["out"])
        return dh_in, dnorm

    def sq_grad(tag, a, dyb, t):
        slab["sq"] = mm_tn(f"dw_sq_{tag}", a, dyb, D_MODEL, D_MODEL // 2,
                           (None, N_CHIPS, None, SQ_ROWS, D_MODEL // 2),
                           lambda k, n: (n, 0, t, 0, 0), sq_shape, prev=slab["sq"])

    dh5, d_ffn2_1 = ffn_bwd("l1b", dh6, h5, vec(ffn2_norm, 1), gate4, up4, SLOT_FFN2[1])
    do_sw, dh5b = linear_bwd_plain("swa_out_bwd", dh5, w_sq, SQ_SWA_O)
    sq_grad("swa_o", o_sw, dh5b, SQ_SWA_O)
    dq_sw, kv_own, kv_prev, d_sinks = swa_bwd(q_sw, k_sw, v_sw, sinks, do_sw, lse, cos, sin)
    sq_w_spec = pl.BlockSpec((N_CHIPS, None, SQ_ROWS, D_MODEL), lambda i, j: (0, SQ_SWA_Q, 0, 0))
    dh4, hn4, d_mix_1 = linear_bwd_rms("swa_q_bwd", [(dq_sw, w_sq, sq_w_spec, sq_prep)], h4, vec(mix_norm, 1), dh5,
                                       1, D_MODEL)
    sq_grad("swa_q", hn4, dq_sw, SQ_SWA_Q)
    dh3a, d_ffn1_1 = ffn_bwd("l1a", dh4, h3, vec(ffn1_norm, 1), gate3, up3, SLOT_FFN1[1])
    dkv = kv_grad_combine(kv_own, kv_prev, cos, sin)
    kv_w_spec = pl.BlockSpec((D_MODEL, 2 * kv_cols), lambda i, j: (0, 0))
    dh3, xn3, d_kvn = linear_bwd_rms("kv_bwd", [(dkv, w_kv, kv_w_spec, ident)], h3, kvn, dh3a, 1, 2 * kv_cols)
    slab_kv = mm_tn("dw_kv", xn3, dkv, D_MODEL, kv_cols, (None, N_CHIPS, None, SQ_ROWS, kv_cols),
                    lambda k, n: (n, 0, 0, 0, 0), (2, N_CHIPS, 1, SQ_ROWS, kv_cols))
    dh2, d_ffn2_0 = ffn_bwd("l0b", dh3, h2, vec(ffn2_norm, 0), gate2, up2, SLOT_FFN2[0])
    do_sb, dh2b = linear_bwd_plain("sb_out_bwd", dh2, w_sq, SQ_SB_O)
    sq_grad("sb_o", o_sb, dh2b, SQ_SB_O)
    dq_sb, dk_sb, dv_sb = sb_bwd(qkv, do_sb, tot)
    dqkv = jnp.concatenate([dq_sb, dk_sb, dv_sb], axis=1)
    qkv_w_spec = pl.BlockSpec((None, D_MODEL, QKV_COLS), lambda i, j: (j, 0, 0))
    dh1, hn1, d_mix_0 = linear_bwd_rms("sb_qkv_bwd", [(dqkv, w_qkv, qkv_w_spec, ident)], h1, vec(mix_norm, 0), dh2,
                                       N_CHIPS, QKV_COLS)
    slab_qkv = mm_tn("dw_qkv", hn1, dqkv, D_MODEL // 2, QKV_COLS, (None, 1, None, D_MODEL // 2, QKV_COLS),
                     lambda k, n: (k, n, 0, 0, 0), (2, N_CHIPS, 1, D_MODEL // 2, QKV_COLS))
    dx, d_ffn1_0 = ffn_bwd("l0a", dh1, x2, vec(ffn1_norm, 0), gate1, up1, SLOT_FFN1[0])

    slabs = [slab["in"], slab["out"], slab["sq"], slab_qkv, slab_kv]
    names = ["in", "out", "sq", "qkv", "kv"]
    res = exchange_halves(slabs)
    mine, theirs = res[:len(slabs)], res[len(slabs):]
    parts = []
    for nm, a, b in zip(names, mine, theirs):
        shp = a.shape
        flat = (shp[0] * shp[1],) + shp[2:]
        parts.append(add_slabs(f"add_sibling_{nm}", a.reshape(flat), b.reshape(flat)).reshape(shp))
    gathered = exchange_chip_partials(parts)
    halves = [sum_chips(f"sum_chips_{nm}", g) for nm, g in zip(names, gathered)]
    g_in, g_out, g_sq, g_qkv, g_kv = share_reduced_halves(halves)

    def upd(name, w, m, v, gfull, slot0, row_halves):
        shp = w.shape
        w3 = w.reshape((-1,) + shp[-2:])
        outs = adamw_shard(name, w3, m.reshape(w3.shape), v.reshape(w3.shape), gfull, slot0, row_halves)
        return [o.reshape(shp) for o in outs]

    r_ffn1_in = upd("adamw_ffn1_in", ffn1_w_in, m_ffn1_w_in, v_ffn1_w_in, g_in, 0, True)
    r_ffn2_in = upd("adamw_ffn2_in", ffn2_w_in, m_ffn2_w_in, v_ffn2_w_in, g_in, 2, True)
    r_ffn1_out = upd("adamw_ffn1_out", ffn1_w_out, m_ffn1_w_out, v_ffn1_w_out, g_out, 0, False)
    r_ffn2_out = upd("adamw_ffn2_out", ffn2_w_out, m_ffn2_w_out, v_ffn2_w_out, g_out, 2, False)
    r_qkv = upd("adamw_qkv", sb_w_qkv, m_sb_w_qkv, v_sb_w_qkv, g_qkv, 0, True)
    r_sb_o = upd("adamw_sb_o", sb_w_o, m_sb_w_o, v_sb_w_o, g_sq, SQ_SB_O, False)
    r_swa_q = upd("adamw_swa_q", swa_w_q, m_swa_w_q, v_swa_w_q, g_sq, SQ_SWA_Q, False)
    r_swa_o = upd("adamw_swa_o", swa_w_o, m_swa_w_o, v_swa_w_o, g_sq, SQ_SWA_O, False)
    r_kv = upd("adamw_kv", kv_w, m_kv_w, v_kv_w, g_kv, 0, False)

    loss_row = jnp.zeros((1, D_MODEL), F32).at[0, :LANES].set(loss_p[0])
    d_sink_row = d_sinks[0, :SWA_Q_HEADS]
    part = _pack_small(jnp.concatenate([d_ffn1_0, d_ffn1_1], axis=0), jnp.concatenate([d_mix_0, d_mix_1], axis=0),
                       jnp.concatenate([d_ffn2_0, d_ffn2_1], axis=0), d_kvn, d_final, d_sink_row, loss_row)
    zrow = jnp.zeros((1, D_MODEL), F32)
    small = small_allreduce_adamw(
        part,
        _pack_small(ffn1_norm, mix_norm, ffn2_norm, kv_norm, final_norm, swa_sinks, zrow),
        _pack_small(m_ffn1_norm, m_mix_norm, m_ffn2_norm, m_kv_norm, m_final_norm, m_swa_sinks, zrow),
        _pack_small(v_ffn1_norm, v_mix_norm, v_ffn2_norm, v_kv_norm, v_final_norm, v_swa_sinks, zrow))

    def unpack(p):
        return dict(ffn1_norm=p[0:2], mix_norm=p[2:4], ffn2_norm=p[4:6], kv_norm=p[6], final_norm=p[7],
                    swa_sinks=p[8:9, :SWA_Q_HEADS])

    big = dict(ffn1_w_in=r_ffn1_in, ffn1_w_out=r_ffn1_out, ffn2_w_in=r_ffn2_in, ffn2_w_out=r_ffn2_out,
               sb_w_qkv=r_qkv, sb_w_o=r_sb_o, kv_w=r_kv, swa_w_q=r_swa_q, swa_w_o=r_swa_o)
    order = ["ffn1_norm", "ffn1_w_in", "ffn1_w_out", "mix_norm", "ffn2_norm", "ffn2_w_in", "ffn2_w_out",
             "sb_w_qkv", "sb_w_o", "kv_norm", "kv_w", "swa_w_q", "swa_sinks", "swa_w_o", "final_norm"]
    outs = []
    for kind in range(4):
        sm = unpack(small[kind])
        for nm in order:
            outs.append(big[nm][kind] if nm in big else sm[nm])
    loss = small[0][9, 0]
    return (loss, dx.reshape(x.shape), *outs)
```

```python
import functools

import jax
import jax.numpy as jnp
from jax import lax
from jax.experimental import pallas as pl
from jax.experimental.pallas import tpu as pltpu

F32 = jnp.float32
BF16 = jnp.bfloat16
MESH = pl.DeviceIdType.MESH

D_MODEL = 1024
D_FF = 2816
HEAD_DIM = 64
SB_HEADS = 16
SWA_Q_HEADS = 16
SWA_KV_HEADS = 4
WINDOW = 128
ROPE_THETA = 10000.0
RMS_EPS = 1e-6
FFN_RES_SCALE = 0.5
ATTN_SCALE = HEAD_DIM ** -0.5

ADAM_LR = 0.001
ADAM_B1 = 0.9
ADAM_B2 = 0.999
ADAM_EPS = 1e-08
ADAM_WD = 0.01
ADAM_STEP = 10

N_CHIPS = 4
N_DEV = 8
LANES = 128
FF_CHUNK = D_FF // 2
FF_ROWS = D_FF // N_CHIPS
SQ_ROWS = D_MODEL // N_CHIPS
QKV_COLS = 3 * D_MODEL // N_CHIPS
VMEM_LIMIT = 56 * 1024 * 1024
NEG_BIG = -1e30

SLOT_FFN1 = (0, 1)
SLOT_FFN2 = (2, 3)
SQ_SB_O, SQ_SWA_Q, SQ_SWA_O = 0, 1, 2


def _cparams():
    return pltpu.CompilerParams(vmem_limit_bytes=VMEM_LIMIT)


def _dot(a, b):
    return jnp.dot(a, b, preferred_element_type=F32)


def _dot_nt(a, b):
    return lax.dot_general(a, b, (((1,), (1,)), ((), ())), preferred_element_type=F32)


def _dot_tn(a, b):
    return lax.dot_general(a, b, (((0,), (0,)), ((), ())), preferred_element_type=F32)


def _rstd(h):
    return lax.rsqrt(jnp.mean(h * h, axis=-1, keepdims=True) + RMS_EPS)


def _swap32(x):
    n = x.shape[-1]
    lane = lax.broadcasted_iota(jnp.int32, x.shape, x.ndim - 1)
    first = (lane % HEAD_DIM) < (HEAD_DIM // 2)
    return jnp.where(first, pltpu.roll(x, n - HEAD_DIM // 2, x.ndim - 1), pltpu.roll(x, HEAD_DIM // 2, x.ndim - 1))


def _tile_lanes(t, n):
    return t if n == LANES else jnp.tile(t, (1, n // LANES))


def ffn_fwd(h, g, w_in, w_out, slot):
    T = h.shape[0]
    tm = 256
    nch = D_FF // FF_CHUNK

    def body(h_ref, g_ref, wg_ref, wu_ref, wo_ref, out_ref, gate_ref, up_ref, xn_s, acc_s):
        j = pl.program_id(1)

        @pl.when(j == 0)
        def _():
            hh = h_ref[...]
            xn_s[...] = (hh * _rstd(hh) * g_ref[...]).astype(BF16)
            acc_s[...] = jnp.zeros_like(acc_s)

        xn = xn_s[...]
        gate = _dot(xn, wg_ref[...])
        up = _dot(xn, wu_ref[...])
        gate_ref[...] = gate
        up_ref[...] = up
        a = gate * jax.nn.sigmoid(gate) * up
        acc_s[...] += _dot(a.astype(BF16), wo_ref[...].reshape(FF_CHUNK, D_MODEL))

        @pl.when(j == nch - 1)
        def _():
            out_ref[...] = h_ref[...] + FFN_RES_SCALE * acc_s[...]

    return pl.pallas_call(
        body,
        name=f"ffn_fwd_{slot}",
        grid=(T // tm, nch),
        in_specs=[
            pl.BlockSpec((tm, D_MODEL), lambda i, j: (i, 0)),
            pl.BlockSpec((1, D_MODEL), lambda i, j: (0, 0)),
            pl.BlockSpec((None, None, D_MODEL, FF_CHUNK), lambda i, j: (j, slot, 0, 0)),
            pl.BlockSpec((None, None, D_MODEL, FF_CHUNK), lambda i, j: (2 + j, slot, 0, 0)),
            pl.BlockSpec((2, None, FF_ROWS, D_MODEL), lambda i, j: (j, slot, 0, 0)),
        ],
        out_specs=[
            pl.BlockSpec((tm, D_MODEL), lambda i, j: (i, 0)),
            pl.BlockSpec((tm, FF_CHUNK), lambda i, j: (i, j)),
            pl.BlockSpec((tm, FF_CHUNK), lambda i, j: (i, j)),
        ],
        out_shape=[
            jax.ShapeDtypeStruct((T, D_MODEL), F32),
            jax.ShapeDtypeStruct((T, D_FF), F32),
            jax.ShapeDtypeStruct((T, D_FF), F32),
        ],
        scratch_shapes=[pltpu.VMEM((tm, D_MODEL), BF16), pltpu.VMEM((tm, D_MODEL), F32)],
        compiler_params=_cparams(),
    )(h, g, w_in, w_in, w_out)


def ffn_bwd_act(dh, gate, up, w_out, slot):
    T = dh.shape[0]
    tm = 256
    nch = D_FF // FF_CHUNK

    def body(dh_ref, gate_ref, up_ref, wo_ref, dg_ref, du_ref, a_ref, dhb_ref):
        dhb = (FFN_RES_SCALE * dh_ref[...]).astype(BF16)
        dhb_ref[...] = dhb
        da = _dot_nt(dhb, wo_ref[...].reshape(FF_CHUNK, D_MODEL))
        gt = gate_ref[...]
        u = up_ref[...]
        s = jax.nn.sigmoid(gt)
        silu = gt * s
        a_ref[...] = (silu * u).astype(BF16)
        dg_ref[...] = (da * u * (s * (1.0 + gt * (1.0 - s)))).astype(BF16)
        du_ref[...] = (da * silu).astype(BF16)

    ff = pl.BlockSpec((tm, FF_CHUNK), lambda i, j: (i, j))
    return pl.pallas_call(
        body,
        name=f"ffn_bwd_act_{slot}",
        grid=(T // tm, nch),
        in_specs=[
            pl.BlockSpec((tm, D_MODEL), lambda i, j: (i, 0)),
            ff, ff,
            pl.BlockSpec((2, None, FF_ROWS, D_MODEL), lambda i, j: (j, slot, 0, 0)),
        ],
        out_specs=[ff, ff, ff, pl.BlockSpec((tm, D_MODEL), lambda i, j: (i, 0))],
        out_shape=[
            jax.ShapeDtypeStruct((T, D_FF), BF16),
            jax.ShapeDtypeStruct((T, D_FF), BF16),
            jax.ShapeDtypeStruct((T, D_FF), BF16),
            jax.ShapeDtypeStruct((T, D_MODEL), BF16),
        ],
        compiler_params=_cparams(),
    )(dh, gate, up, w_out)


def rms_linear(name, h, g, w, w_spec, w_prep, n_out, tn, *, rope=None, scale=None):
    T = h.shape[0]
    tm = 512
    extra, extra_specs = [], []
    if rope is not None:
        extra += list(rope)
        extra_specs += [pl.BlockSpec((tm, LANES), lambda i, j: (i, 0))] * 2
    if scale is not None:
        extra.append(scale)
        extra_specs.append(pl.BlockSpec((1, tn), lambda i, j: (0, j)))

    def body(h_ref, g_ref, w_ref, *rest):
        rest = list(rest)
        cos_ref = sin_ref = sc_ref = None
        if rope is not None:
            cos_ref, sin_ref = rest[0], rest[1]
            rest = rest[2:]
        if scale is not None:
            sc_ref = rest[0]
            rest = rest[1:]
        out_ref, xn_s = rest

        @pl.when(pl.program_id(1) == 0)
        def _():
            hh = h_ref[...]
            xn_s[...] = (hh * _rstd(hh) * g_ref[...]).astype(BF16)

        y = _dot(xn_s[...], w_prep(w_ref[...]))
        if rope is not None:
            y = y * _tile_lanes(cos_ref[...], tn) + _swap32(y) * _tile_lanes(sin_ref[...], tn)
        if scale is not None:
            y = y * sc_ref[...]
        out_ref[...] = y.astype(BF16)

    return pl.pallas_call(
        body,
        name=name,
        grid=(T // tm, n_out // tn),
        in_specs=[
            pl.BlockSpec((tm, D_MODEL), lambda i, j: (i, 0)),
            pl.BlockSpec((1, D_MODEL), lambda i, j: (0, 0)),
            w_spec,
        ] + extra_specs,
        out_specs=pl.BlockSpec((tm, tn), lambda i, j: (i, j)),
        out_shape=jax.ShapeDtypeStruct((T, n_out), BF16),
        scratch_shapes=[pltpu.VMEM((tm, D_MODEL), BF16)],
        compiler_params=_cparams(),
    )(h, g, w, *extra)


def linear_res(name, a, w_sq, t, res):
    T = a.shape[0]
    tm = 512

    def body(a_ref, w_ref, res_ref, out_ref):
        out_ref[...] = res_ref[...] + _dot(a_ref[...], w_ref[...].reshape(D_MODEL, D_MODEL))

    row = pl.BlockSpec((tm, D_MODEL), lambda i: (i, 0))
    return pl.pallas_call(
        body,
        name=name,
        grid=(T // tm,),
        in_specs=[row, pl.BlockSpec((N_CHIPS, None, SQ_ROWS, D_MODEL), lambda i: (0, t, 0, 0)), row],
        out_specs=row,
        out_shape=jax.ShapeDtypeStruct((T, D_MODEL), F32),
        compiler_params=_cparams(),
    )(a, w_sq, res)


def linear_bwd_plain(name, dy, w_sq, t):
    T = dy.shape[0]
    tm = 512

    def body(dy_ref, w_ref, da_ref, dyb_ref):
        dyb = dy_ref[...].astype(BF16)
        dyb_ref[...] = dyb
        da_ref[...] = _dot_nt(dyb, w_ref[...].reshape(D_MODEL, D_MODEL)).astype(BF16)

    row = pl.BlockSpec((tm, D_MODEL), lambda i: (i, 0))
    return pl.pallas_call(
        body,
        name=name,
        grid=(T // tm,),
        in_specs=[row, pl.BlockSpec((N_CHIPS, None, SQ_ROWS, D_MODEL), lambda i: (0, t, 0, 0))],
        out_specs=[row, row],
        out_shape=[jax.ShapeDtypeStruct((T, D_MODEL), BF16), jax.ShapeDtypeStruct((T, D_MODEL), BF16)],
        compiler_params=_cparams(),
    )(dy, w_sq)


def linear_bwd_rms(name, pairs, h, g, dres, nch, tn, tm=256):
    T = h.shape[0]
    npair = len(pairs)

    def body(*refs):
        dy_refs = refs[:npair]
        w_refs = refs[npair:2 * npair]
        h_ref, g_ref, dres_ref, dh_ref, xn_ref, dg_ref, acc_s = refs[2 * npair:]
        i = pl.program_id(0)
        j = pl.program_id(1)

        @pl.when(j == 0)
        def _():
            acc_s[...] = jnp.zeros_like(acc_s)

        @pl.when((i == 0) & (j == 0))
        def _():
            dg_ref[...] = jnp.zeros_like(dg_ref)

        part = None
        for p in range(npair):
            d = _dot_nt(dy_refs[p][...], pairs[p][3](w_refs[p][...]))
            part = d if part is None else part + d
        acc_s[...] += part

        @pl.when(j == nch - 1)
        def _():
            dxn = acc_s[...]
            hh = h_ref[...]
            gg = g_ref[...]
            r = _rstd(hh)
            hr = hh * r
            xn_ref[...] = (hr * gg).astype(BF16)
            dg_ref[...] += jnp.sum(dxn * hr, axis=0, keepdims=True)
            gd = gg * dxn
            dh_ref[...] = dres_ref[...] + r * (gd - hr * jnp.mean(gd * hr, axis=-1, keepdims=True))

    row = pl.BlockSpec((tm, D_MODEL), lambda i, j: (i, 0))
    vec = pl.BlockSpec((1, D_MODEL), lambda i, j: (0, 0))
    return pl.pallas_call(
        body,
        name=name,
        grid=(T // tm, nch),
        in_specs=[pl.BlockSpec((tm, tn), lambda i, j: (i, j))] * npair + [p[2] for p in pairs] + [row, vec, row],
        out_specs=[row, row, vec],
        out_shape=[
            jax.ShapeDtypeStruct((T, D_MODEL), F32),
            jax.ShapeDtypeStruct((T, D_MODEL), BF16),
            jax.ShapeDtypeStruct((1, D_MODEL), F32),
        ],
        scratch_shapes=[pltpu.VMEM((tm, D_MODEL), F32)],
        compiler_params=_cparams(),
    )(*[p[0] for p in pairs], *[p[1] for p in pairs], h, g, dres)


def loss_bwd(h, g, tgt):
    T = h.shape[0]
    tm = 512

    def body(h_ref, g_ref, t_ref, dh_ref, loss_ref, dg_ref):
        @pl.when(pl.program_id(0) == 0)
        def _():
            loss_ref[...] = jnp.zeros_like(loss_ref)
            dg_ref[...] = jnp.zeros_like(dg_ref)

        hh = h_ref[...]
        gg = g_ref[...]
        r = _rstd(hh)
        hr = hh * r
        err = hr * gg - t_ref[...]
        loss_ref[...] += 0.5 * jnp.sum(jnp.mean(err * err, axis=-1, keepdims=True), axis=0, keepdims=True)
        dy = err * (1.0 / D_MODEL)
        dg_ref[...] += jnp.sum(dy * hr, axis=0, keepdims=True)
        gd = gg * dy
        dh_ref[...] = r * (gd - hr * jnp.mean(gd * hr, axis=-1, keepdims=True))

    row = pl.BlockSpec((tm, D_MODEL), lambda i: (i, 0))
    vec = pl.BlockSpec((1, D_MODEL), lambda i: (0, 0))
    return pl.pallas_call(
        body,
        name="loss_bwd",
        grid=(T // tm,),
        in_specs=[row, vec, row],
        out_specs=[row, pl.BlockSpec((1, LANES), lambda i: (0, 0)), vec],
        out_shape=[
            jax.ShapeDtypeStruct((T, D_MODEL), F32),
            jax.ShapeDtypeStruct((1, LANES), F32),
            jax.ShapeDtypeStruct((1, D_MODEL), F32),
        ],
        compiler_params=_cparams(),
    )(h, g, tgt)


def mm_tn(name, a, b, tk, tn, out_block, out_index, out_shape, prev=None, tt=1024):
    T = a.shape[0]
    ns, r = out_block[1], out_block[3]
    tt = min(tt, T)
    nt = T // tt

    def body(*refs):
        if prev is None:
            a_ref, b_ref, out_ref = refs
        else:
            a_ref, b_ref, _, out_ref = refs
        t = pl.program_id(2)
        res = _dot_tn(a_ref[...], b_ref[...])

        @pl.when(t == 0)
        def _():
            for u in range(ns):
                out_ref[u] = res[u * r:(u + 1) * r]

        @pl.when(t > 0)
        def _():
            for u in range(ns):
                out_ref[u] += res[u * r:(u + 1) * r]

    in_specs = [
        pl.BlockSpec((tt, tk), lambda k, n, t: (t, k)),
        pl.BlockSpec((tt, tn), lambda k, n, t: (t, n)),
    ]
    args = [a, b]
    aliases = {}
    if prev is not None:
        in_specs.append(pl.BlockSpec(memory_space=pl.ANY))
        args.append(prev)
        aliases = {2: 0}
    return pl.pallas_call(
        body,
        name=name,
        grid=(a.shape[1] // tk, b.shape[1] // tn, nt),
        in_specs=in_specs,
        out_specs=pl.BlockSpec(out_block, lambda k, n, t: out_index(k, n)),
        out_shape=jax.ShapeDtypeStruct(out_shape, F32),
        input_output_aliases=aliases,
        compiler_params=_cparams(),
    )(*args)


SB_BLOCK = 256


def _log1m_sigmoid(z):
    return -jnp.log(1.0 + jnp.exp(-jnp.abs(z))) - jnp.maximum(z, 0.0)


def _split_bf16(x):
    hi = x.astype(BF16)
    lo = (x - hi.astype(F32)).astype(BF16)
    return jnp.concatenate([hi, lo], axis=1)


def _twice(x):
    return jnp.concatenate([x, x], axis=1)


def sb_fwd(qkv):
    T = qkv.shape[0]
    tq = SB_BLOCK
    npair = SB_HEADS // 2

    def body(q_ref, k_ref, v_ref, o_ref, tot_ref, acc_s, c_s):
        i = pl.program_id(1)
        q = q_ref[...]
        lane = lax.broadcasted_iota(jnp.int32, (tq, LANES), 1)
        first = lane < HEAD_DIM
        zero = jnp.zeros_like(q)
        q_heads = (jnp.where(first, q, zero), jnp.where(first, zero, q))
        row = lax.broadcasted_iota(jnp.int32, (tq, tq), 0)
        col = lax.broadcasted_iota(jnp.int32, (tq, tq), 1)
        strict = col < row
        from_s = (row >= col).astype(BF16)
        from_s2 = jnp.concatenate([from_s, from_s], axis=0)
        acc_s[...] = jnp.zeros_like(acc_s)
        c_s[...] = jnp.zeros_like(c_s)

        def block(j, diag):
            start = pl.multiple_of(j * tq, tq)
            kb = k_ref[pl.ds(start, tq), :]
            vb = v_ref[pl.ds(start, tq), :]
            zs = [_dot_nt(q_heads[hd], kb) for hd in range(2)]
            es, parts = [], []
            for hd in range(2):
                z = jnp.where(strict, zs[hd], NEG_BIG) if diag else zs[hd]
                lm = _log1m_sigmoid(z)
                parts.append(_split_bf16(lm))
                c = c_s[hd]
                es.append(z + _twice(c))
                c_s[hd] = c + jnp.sum(lm, axis=1, keepdims=True)
            sums = [_dot(parts[hd], from_s2) for hd in range(2)]
            ws = [jnp.exp(es[hd] + sums[hd]).astype(BF16) for hd in range(2)]
            for hd in range(2):
                acc_s[hd] += _dot(ws[hd], vb)

        block(i, True)

        @pl.loop(0, i)
        def _(jj):
            block(i - 1 - jj, False)

        o_ref[...] = jnp.where(first, acc_s[0], acc_s[1]).astype(BF16)
        tot_ref[...] = jnp.where(first, c_s[0], c_s[1])

    return pl.pallas_call(
        body,
        name="sb_fwd",
        grid=(npair, T // tq),
        in_specs=[
            pl.BlockSpec((tq, LANES), lambda p, i: (i, p)),
            pl.BlockSpec((T, LANES), lambda p, i: (0, npair + p)),
            pl.BlockSpec((T, LANES), lambda p, i: (0, 2 * npair + p)),
        ],
        out_specs=[pl.BlockSpec((tq, LANES), lambda p, i: (i, p))] * 2,
        out_shape=[jax.ShapeDtypeStruct((T, D_MODEL), BF16), jax.ShapeDtypeStruct((T, D_MODEL), F32)],
        scratch_shapes=[pltpu.VMEM((2, tq, LANES), F32), pltpu.VMEM((2, tq, LANES), F32)],
        compiler_params=_cparams(),
    )(qkv, qkv, qkv)


def sb_bwd(qkv, do, tot):
    T = qkv.shape[0]
    tq = SB_BLOCK
    npair = SB_HEADS // 2
    nq = T // tq

    def body(q_ref, k_ref, v_ref, do_ref, tot_ref, dq_ref, dk_ref, dv_ref, dk_s, dv_s, dq_s, rest_s, cg_s):
        i = pl.program_id(1)

        @pl.when(i == 0)
        def _():
            dk_s[...] = jnp.zeros_like(dk_s)
            dv_s[...] = jnp.zeros_like(dv_s)

        q = q_ref[...]
        do_ = do_ref[...]
        tot_ = tot_ref[...]
        lane = lax.broadcasted_iota(jnp.int32, (tq, LANES), 1)
        first = lane < HEAD_DIM
        zero = jnp.zeros_like(q)
        q_heads = (jnp.where(first, q, zero), jnp.where(first, zero, q))
        do_heads = (jnp.where(first, do_, zero), jnp.where(first, zero, do_))
        row = lax.broadcasted_iota(jnp.int32, (tq, tq), 0)
        col = lax.broadcasted_iota(jnp.int32, (tq, tq), 1)
        strict = col < row
        before = (row < col).astype(BF16)
        before2 = jnp.concatenate([before, before], axis=0)
        rest_s[0] = jnp.broadcast_to(tot_[:, 0:1], (tq, LANES))
        rest_s[1] = jnp.broadcast_to(tot_[:, HEAD_DIM:HEAD_DIM + 1], (tq, LANES))
        cg_s[...] = jnp.zeros_like(cg_s)
        dq_s[...] = jnp.zeros_like(dq_s)

        def block(j, diag):
            start = pl.multiple_of(j * tq, tq)
            kb = k_ref[pl.ds(start, tq), :]
            vb = v_ref[pl.ds(start, tq), :]
            zs = [_dot_nt(q_heads[hd], kb) for hd in range(2)]
            das = [_dot_nt(do_heads[hd], vb) for hd in range(2)]
            es, zls, parts = [], [], []
            for hd in range(2):
                z = jnp.where(strict, zs[hd], NEG_BIG) if diag else zs[hd]
                lm = _log1m_sigmoid(z)
                parts.append(_split_bf16(lm))
                rest = rest_s[hd]
                es.append(z + _twice(rest))
                zls.append(z + lm)
                rest_s[hd] = rest - jnp.sum(lm, axis=1, keepdims=True)
            pres = [_dot(parts[hd], before2) for hd in range(2)]
            gs, gcs, gbs, abs_ = [], [], [], []
            for hd in range(2):
                a = jnp.exp(es[hd] - pres[hd])
                g = a * das[hd]
                cg = cg_s[hd]
                gs.append(g)
                gcs.append(g + _twice(cg))
                gbs.append(g.astype(BF16))
                abs_.append(a.astype(BF16))
                cg_s[hd] = cg + jnp.sum(g, axis=1, keepdims=True)
            pexs = [_dot(gbs[hd], before) for hd in range(2)]
            dv_add = _dot_tn(abs_[0], do_heads[0]) + _dot_tn(abs_[1], do_heads[1])
            dzs = [(gs[hd] - jnp.exp(zls[hd]) * (gcs[hd] + pexs[hd])).astype(BF16) for hd in range(2)]
            for hd in range(2):
                dq_s[hd] += _dot(dzs[hd], kb)
            dk_add = _dot_tn(dzs[0], q_heads[0]) + _dot_tn(dzs[1], q_heads[1])
            dk_s[pl.ds(start, tq), :] += dk_add
            dv_s[pl.ds(start, tq), :] += dv_add

        @pl.loop(0, i)
        def _(j):
            block(j, False)

        block(i, True)
        dq_ref[...] = (jnp.where(first, dq_s[0], dq_s[1]) * ATTN_SCALE).astype(BF16)

        @pl.when(i == nq - 1)
        def _():
            dk_ref[...] = dk_s[...].astype(BF16)
            dv_ref[...] = dv_s[...].astype(BF16)

    qblk = pl.BlockSpec((tq, LANES), lambda p, i: (i, p))
    full = pl.BlockSpec((T, LANES), lambda p, i: (0, p))
    return pl.pallas_call(
        body,
        name="sb_bwd",
        grid=(npair, nq),
        in_specs=[
            qblk,
            pl.BlockSpec((T, LANES), lambda p, i: (0, npair + p)),
            pl.BlockSpec((T, LANES), lambda p, i: (0, 2 * npair + p)),
            qblk, qblk,
        ],
        out_specs=[qblk, full, full],
        out_shape=[jax.ShapeDtypeStruct((T, D_MODEL), BF16)] * 3,
        scratch_shapes=[
            pltpu.VMEM((T, LANES), F32), pltpu.VMEM((T, LANES), F32),
            pltpu.VMEM((2, tq, LANES), F32), pltpu.VMEM((2, tq, LANES), F32), pltpu.VMEM((2, tq, LANES), F32),
        ],
        compiler_params=_cparams(),
    )(qkv, qkv, qkv, do, tot)


def _swa_valid(n):
    qi = lax.broadcasted_iota(jnp.int32, (WINDOW, 2 * WINDOW), 0)
    ki = lax.broadcasted_iota(jnp.int32, (WINDOW, 2 * WINDOW), 1)
    diff = qi + WINDOW - ki
    return (diff >= 0) & (diff < WINDOW) & ((n > 0) | (ki >= WINDOW))


def _to_half(x, first, src, dst):
    keep = first if src == 0 else jnp.logical_not(first)
    x = jnp.where(keep, x, jnp.zeros_like(x))
    if src != dst:
        x = pltpu.roll(x.astype(F32), HEAD_DIM, 1).astype(BF16)
    return x


def _kv_band(prev_ref, cur_ref, pb):
    cols = slice(pb * LANES, (pb + 1) * LANES)
    return jnp.concatenate([prev_ref[:, cols], cur_ref[:, cols]], axis=0)


def _swa_specs(T):
    nb = T // WINDOW
    kv_w = SWA_KV_HEADS * HEAD_DIM
    qrow = pl.BlockSpec((WINDOW, D_MODEL), lambda n: (n, 0))
    cur = pl.BlockSpec((WINDOW, kv_w), lambda n: (n, 0))
    prev = pl.BlockSpec((WINDOW, kv_w), lambda n: (jnp.maximum(n - 1, 0), 0))
    smem = pl.BlockSpec(memory_space=pltpu.SMEM)
    return nb, qrow, cur, prev, smem


def swa_fwd(q, k, v, sinks):
    T = q.shape[0]
    nb, qrow, cur, prev, smem = _swa_specs(T)

    def body(sink_ref, q_ref, kc_ref, kp_ref, vc_ref, vp_ref, o_ref, lse_ref):
        n = pl.program_id(0)
        lane = lax.broadcasted_iota(jnp.int32, (WINDOW, LANES), 1)
        first = lane < HEAD_DIM
        valid = _swa_valid(n)
        lse_acc = jnp.zeros((WINDOW, LANES), F32)
        for pb in range(SWA_KV_HEADS // 2):
            k2 = _kv_band(kp_ref, kc_ref, pb)
            v2 = _kv_band(vp_ref, vc_ref, pb)
            for b in range(2):
                kvh = 2 * pb + b
                for qq in range(2):
                    cols = slice((2 * kvh + qq) * LANES, (2 * kvh + qq + 1) * LANES)
                    qp = q_ref[:, cols]
                    outs = []
                    for a in range(2):
                        head = 4 * kvh + 2 * qq + a
                        qh = _to_half(qp, first, a, b)
                        s = jnp.where(valid, _dot_nt(qh, k2), NEG_BIG)
                        sink = sink_ref[head]
                        m = jnp.maximum(jnp.max(s, axis=1, keepdims=True), sink)
                        p = jnp.exp(s - m)
                        den = jnp.sum(p, axis=1, keepdims=True) + jnp.exp(sink - m)
                        o = _dot((p / den).astype(BF16), v2)
                        if a != b:
                            o = pltpu.roll(o, HEAD_DIM, 1)
                        outs.append(o)
                        lse_acc = jnp.where(lane == head, m + jnp.log(den), lse_acc)
                    o_ref[:, cols] = jnp.where(first, outs[0], outs[1]).astype(BF16)
        lse_ref[...] = lse_acc

    return pl.pallas_call(
        body,
        name="swa_fwd",
        grid=(nb,),
        in_specs=[smem, qrow, cur, prev, cur, prev],
        out_specs=[qrow, pl.BlockSpec((WINDOW, LANES), lambda n: (n, 0))],
        out_shape=[jax.ShapeDtypeStruct((T, D_MODEL), BF16), jax.ShapeDtypeStruct((T, LANES), F32)],
        compiler_params=_cparams(),
    )(sinks, q, k, k, v, v)


def swa_bwd(q, k, v, sinks, do, lse, cos, sin):
    T = q.shape[0]
    nb, qrow, cur, prev, smem = _swa_specs(T)
    kv_w = SWA_KV_HEADS * HEAD_DIM

    def body(sink_ref, q_ref, kc_ref, kp_ref, vc_ref, vp_ref, do_ref, lse_ref, cos_ref, sin_ref,
             dq_ref, own_ref, prv_ref, dsink_ref):
        n = pl.program_id(0)

        @pl.when(n == 0)
        def _():
            dsink_ref[...] = jnp.zeros_like(dsink_ref)

        lane = lax.broadcasted_iota(jnp.int32, (WINDOW, LANES), 1)
        lane1 = lax.broadcasted_iota(jnp.int32, (1, LANES), 1)
        first = lane < HEAD_DIM
        valid = _swa_valid(n)
        cos_ = cos_ref[...]
        sin_ = sin_ref[...]
        dsink = jnp.zeros((1, LANES), F32)
        for pb in range(SWA_KV_HEADS // 2):
            k2 = _kv_band(kp_ref, kc_ref, pb)
            v2 = _kv_band(vp_ref, vc_ref, pb)
            dk2 = jnp.zeros((2 * WINDOW, LANES), F32)
            dv2 = jnp.zeros((2 * WINDOW, LANES), F32)
            for b in range(2):
                kvh = 2 * pb + b
                for qq in range(2):
                    cols = slice((2 * kvh + qq) * LANES, (2 * kvh + qq + 1) * LANES)
                    qp = q_ref[:, cols]
                    dop = do_ref[:, cols]
                    dqs = []
                    for a in range(2):
                        head = 4 * kvh + 2 * qq + a
                        qh = _to_half(qp, first, a, b)
                        doh = _to_half(dop, first, a, b)
                        s = jnp.where(valid, _dot_nt(qh, k2), NEG_BIG)
                        lse_h = lse_ref[:, head:head + 1]
                        p = jnp.exp(s - lse_h)
                        dp = _dot_nt(doh, v2)
                        delta = jnp.sum(p * dp, axis=1, keepdims=True)
                        ds = (p * (dp - delta)).astype(BF16)
                        p_sink = jnp.exp(sink_ref[head] - lse_h)
                        dsink = dsink + jnp.where(lane1 == head, -jnp.sum(p_sink * delta, axis=0, keepdims=True), 0.0)
                        dq = _dot(ds, k2)
                        if a != b:
                            dq = pltpu.roll(dq, HEAD_DIM, 1)
                        dqs.append(dq)
                        dk2 = dk2 + _dot_tn(ds, qh)
                        dv2 = dv2 + _dot_tn(p.astype(BF16), doh)
                    dqp = jnp.where(first, dqs[0], dqs[1])
                    dq_ref[:, cols] = ((dqp * cos_ + _swap32(dqp * sin_)) * ATTN_SCALE).astype(BF16)
            kcols = slice(pb * LANES, (pb + 1) * LANES)
            vcols = slice(kv_w + pb * LANES, kv_w + (pb + 1) * LANES)
            prv_ref[:, kcols] = dk2[:WINDOW]
            own_ref[:, kcols] = dk2[WINDOW:]
            prv_ref[:, vcols] = dv2[:WINDOW]
            own_ref[:, vcols] = dv2[WINDOW:]
        dsink_ref[...] += dsink

    tab = pl.BlockSpec((WINDOW, LANES), lambda n: (n, 0))
    kvrow = pl.BlockSpec((WINDOW, 2 * kv_w), lambda n: (n, 0))
    return pl.pallas_call(
        body,
        name="swa_bwd",
        grid=(nb,),
        in_specs=[smem, qrow, cur, prev, cur, prev, qrow, tab, tab, tab],
        out_specs=[qrow, kvrow, kvrow, pl.BlockSpec((1, LANES), lambda n: (0, 0))],
        out_shape=[
            jax.ShapeDtypeStruct((T, D_MODEL), BF16),
            jax.ShapeDtypeStruct((T, 2 * kv_w), F32),
            jax.ShapeDtypeStruct((T, 2 * kv_w), F32),
            jax.ShapeDtypeStruct((1, LANES), F32),
        ],
        compiler_params=_cparams(),
    )(sinks, q, k, k, v, v, do, lse, cos, sin)


def kv_grad_combine(own, prv, cos, sin):
    T = own.shape[0]
    nb = T // WINDOW
    kv_w = SWA_KV_HEADS * HEAD_DIM

    def body(own_ref, nxt_ref, cos_ref, sin_ref, out_ref):
        n = pl.program_id(0)
        nxt = jnp.where(n + 1 < nb, nxt_ref[...], 0.0)
        tot = own_ref[...] + nxt
        dk = tot[:, :kv_w]
        c = _tile_lanes(cos_ref[...], kv_w)
        s = _tile_lanes(sin_ref[...], kv_w)
        out_ref[:, :kv_w] = (dk * c + _swap32(dk * s)).astype(BF16)
        out_ref[:, kv_w:] = tot[:, kv_w:].astype(BF16)

    tab = pl.BlockSpec((WINDOW, LANES), lambda n: (n, 0))
    kvrow = pl.BlockSpec((WINDOW, 2 * kv_w), lambda n: (n, 0))
    return pl.pallas_call(
        body,
        name="kv_grad_combine",
        grid=(nb,),
        in_specs=[kvrow, pl.BlockSpec((WINDOW, 2 * kv_w), lambda n: (jnp.minimum(n + 1, nb - 1), 0)), tab, tab],
        out_specs=kvrow,
        out_shape=jax.ShapeDtypeStruct((T, 2 * kv_w), BF16),
        compiler_params=_cparams(),
    )(own, prv, cos, sin)


ANY = pl.BlockSpec(memory_space=pl.ANY)


def _place():
    x, y, c = lax.axis_index("x"), lax.axis_index("y"), lax.axis_index("c")
    other_chips = [(1 - x, y), (x, 1 - y), (1 - x, 1 - y)]
    return x, y, c, 2 * x + y, other_chips


def all_gather_weights(shards):
    n = len(shards)

    def body(*refs):
        ins, outs = refs[:n], refs[n:2 * n]
        send_sems, recv_sems, local_sems = refs[2 * n:]
        _, _, c, me, chips = _place()
        copies = []
        for t in range(n):
            cp = pltpu.make_async_copy(ins[t], outs[t].at[me], local_sems.at[t])
            cp.start()
            copies.append(cp)
            for jdx, (px, py) in enumerate(chips):
                cp = pltpu.make_async_remote_copy(
                    src_ref=ins[t], dst_ref=outs[t].at[me], send_sem=send_sems.at[t, jdx],
                    recv_sem=recv_sems.at[t, jdx], device_id=(px, py, c), device_id_type=MESH)
                cp.start()
                copies.append(cp)
        for cp in copies:
            cp.wait()

    return pl.pallas_call(
        body,
        name="all_gather_weights",
        in_specs=[ANY] * n,
        out_specs=[ANY] * n,
        out_shape=[jax.ShapeDtypeStruct((N_CHIPS,) + s.shape, s.dtype) for s in shards],
        scratch_shapes=[
            pltpu.SemaphoreType.DMA((n, 3)), pltpu.SemaphoreType.DMA((n, 3)), pltpu.SemaphoreType.DMA((n,)),
        ],
    )(*shards)


def exchange_halves(slabs):
    n = len(slabs)

    def body(*refs):
        ins, theirs = refs[:n], refs[n:2 * n]
        send_sems, recv_sems = refs[2 * n:]
        x, y, c, _, _ = _place()
        copies = []
        for t in range(n):
            cp = pltpu.make_async_remote_copy(
                src_ref=ins[t].at[1 - c], dst_ref=theirs[t], send_sem=send_sems.at[t],
                recv_sem=recv_sems.at[t], device_id=(x, y, 1 - c), device_id_type=MESH)
            cp.start()
            copies.append(cp)
        for cp in copies:
            cp.wait()

    return pl.pallas_call(
        body,
        name="exchange_halves",
        in_specs=[ANY] * n,
        out_specs=[ANY] * n,
        out_shape=[jax.ShapeDtypeStruct(s.shape[1:], s.dtype) for s in slabs],
        scratch_shapes=[pltpu.SemaphoreType.DMA((n,)), pltpu.SemaphoreType.DMA((n,))],
    )(*slabs)


def exchange_chip_partials(parts):
    n = len(parts)

    def body(*refs):
        ins, outs = refs[:n], refs[n:2 * n]
        send_sems, recv_sems = refs[2 * n:]
        _, _, c, me, chips = _place()
        copies = []
        for t in range(n):
            for jdx, (px, py) in enumerate(chips):
                cp = pltpu.make_async_remote_copy(
                    src_ref=ins[t].at[2 * px + py], dst_ref=outs[t].at[me], send_sem=send_sems.at[t, jdx],
                    recv_sem=recv_sems.at[t, jdx], device_id=(px, py, c), device_id_type=MESH)
                cp.start()
                copies.append(cp)
        for cp in copies:
            cp.wait()

    return pl.pallas_call(
        body,
        name="exchange_chip_partials",
        in_specs=[ANY] * n,
        out_specs=[ANY] * n,
        out_shape=[jax.ShapeDtypeStruct(p.shape, p.dtype) for p in parts],
        scratch_shapes=[pltpu.SemaphoreType.DMA((n, 3)), pltpu.SemaphoreType.DMA((n, 3))],
    )(*parts)


def share_reduced_halves(halves):
    n = len(halves)

    def body(*refs):
        ins, outs = refs[:n], refs[n:2 * n]
        send_sems, recv_sems = refs[2 * n:]
        x, y, c, _, _ = _place()
        copies = []
        for t in range(n):
            cp = pltpu.make_async_remote_copy(
                src_ref=ins[t], dst_ref=outs[t], send_sem=send_sems.at[t],
                recv_sem=recv_sems.at[t], device_id=(x, y, 1 - c), device_id_type=MESH)
            cp.start()
            copies.append(cp)
        for cp in copies:
            cp.wait()

    return pl.pallas_call(
        body,
        name="share_reduced_halves",
        in_specs=[ANY] * n,
        out_specs=[ANY] * n,
        out_shape=[jax.ShapeDtypeStruct(h.shape, h.dtype) for h in halves],
        scratch_shapes=[pltpu.SemaphoreType.DMA((n,)), pltpu.SemaphoreType.DMA((n,))],
    )(*halves)


def _row_tile(r, c):
    tr = r
    while tr * c * 4 > (3 << 19) and tr % 16 == 0:
        tr //= 2
    return tr


def add_sibling(name, slab, theirs, core):
    _, ns, slots, r, c = slab.shape
    tr = _row_tile(r, c)

    def body(core_ref, a_ref, b_ref, o_ref):
        o_ref[...] = a_ref[...] + b_ref[...]

    blk = pl.BlockSpec((None, None, tr, c), lambda s, l, i, core_ref: (s, l, i, 0))
    return pl.pallas_call(
        body, name=name,
        grid_spec=pltpu.PrefetchScalarGridSpec(
            num_scalar_prefetch=1, grid=(ns, slots, r // tr),
            in_specs=[pl.BlockSpec((None, None, None, tr, c), lambda s, l, i, core_ref: (core_ref[0], s, l, i, 0)), blk],
            out_specs=blk),
        out_shape=jax.ShapeDtypeStruct(theirs.shape, F32), compiler_params=_cparams(),
    )(core, slab, theirs)


def sum_chips(name, recv, own, chip):
    _, slots, r, c = recv.shape
    tr = _row_tile(r, c)

    def body(chip_ref, r0, r1, r2, r3, own_ref, o_ref):
        me = chip_ref[0]
        mine = own_ref[...]
        terms = [jnp.where(me == s, mine, rr[...]) for s, rr in enumerate((r0, r1, r2, r3))]
        o_ref[...] = ((terms[0] + terms[1]) + terms[2]) + terms[3]

    def src(s):
        return pl.BlockSpec((None, None, tr, c),
                            lambda l, i, chip_ref: (jnp.where(chip_ref[0] == s, (s + 1) % N_CHIPS, s), l, i, 0))

    return pl.pallas_call(
        body, name=name,
        grid_spec=pltpu.PrefetchScalarGridSpec(
            num_scalar_prefetch=1, grid=(slots, r // tr),
            in_specs=[src(0), src(1), src(2), src(3),
                      pl.BlockSpec((None, None, tr, c), lambda l, i, chip_ref: (chip_ref[0], l, i, 0))],
            out_specs=pl.BlockSpec((None, tr, c), lambda l, i, chip_ref: (l, i, 0))),
        out_shape=jax.ShapeDtypeStruct((slots, r, c), F32), compiler_params=_cparams(),
    )(chip, recv, recv, recv, recv, own)


def _adamw_math(w, g, m, v):
    m = ADAM_B1 * m + (1.0 - ADAM_B1) * g
    v = ADAM_B2 * v + (1.0 - ADAM_B2) * (g * g)
    m_hat = m / (1.0 - ADAM_B1 ** ADAM_STEP)
    v_hat = v / (1.0 - ADAM_B2 ** ADAM_STEP)
    delta = -ADAM_LR * (m_hat / (jnp.sqrt(v_hat) + ADAM_EPS) + ADAM_WD * w)
    return delta, m, v


def adamw_shard(name, w, m, v, g_own, g_sib, core, slot0, row_halves):
    n = w.shape[0]
    _, r, c = g_own.shape
    tr = _row_tile(r, c)
    nr = r // tr

    def body(core_ref, w_ref, m_ref, v_ref, own_ref, sib_ref, go_ref, d_ref, mo_ref, vo_ref):
        g = jnp.where(pl.program_id(1) == core_ref[0], own_ref[...], sib_ref[...])
        delta, mm, vv = _adamw_math(w_ref[...], g, m_ref[...], v_ref[...])
        go_ref[...] = g
        d_ref[...] = delta
        mo_ref[...] = mm
        vo_ref[...] = vv

    if row_halves:
        wspec = pl.BlockSpec((None, tr, c), lambda l, h, i, core_ref: (l, h * nr + i, 0))
    else:
        wspec = pl.BlockSpec((None, tr, c), lambda l, h, i, core_ref: (l, i, h))
    gspec = pl.BlockSpec((None, tr, c), lambda l, h, i, core_ref: (slot0 + l, i, 0))
    shp = jax.ShapeDtypeStruct(w.shape, F32)
    return pl.pallas_call(
        body, name=name,
        grid_spec=pltpu.PrefetchScalarGridSpec(
            num_scalar_prefetch=1, grid=(n, 2, nr),
            in_specs=[wspec, wspec, wspec, gspec, gspec], out_specs=[wspec] * 4),
        out_shape=[shp] * 4, compiler_params=_cparams(),
    )(core, w, m, v, g_own, g_sib)


SMALL_ROWS = 16


def small_allreduce_adamw(part, w, m, v):
    def body(p_ref, w_ref, m_ref, v_ref, g_ref, d_ref, mo_ref, vo_ref, buf, send_sems, recv_sems):
        x, y, c, _, _ = _place()
        me = 4 * x + 2 * y + c
        buf[me] = p_ref[...]
        copies = []
        for k in range(1, N_DEV):
            kx, ky, kc = (k >> 2) & 1, (k >> 1) & 1, k & 1
            peer = (x ^ kx, y ^ ky, c ^ kc)
            cp = pltpu.make_async_remote_copy(
                src_ref=p_ref, dst_ref=buf.at[me], send_sem=send_sems.at[k - 1],
                recv_sem=recv_sems.at[k - 1], device_id=peer, device_id_type=MESH)
            cp.start()
            copies.append(cp)
        for cp in copies:
            cp.wait()
        g = buf[0]
        for dev in range(1, N_DEV):
            g = g + buf[dev]
        delta, mm, vv = _adamw_math(w_ref[...], g, m_ref[...], v_ref[...])
        g_ref[...] = g
        d_ref[...] = delta
        mo_ref[...] = mm
        vo_ref[...] = vv

    vm = pl.BlockSpec(memory_space=pltpu.VMEM)
    shp = jax.ShapeDtypeStruct(part.shape, F32)
    return pl.pallas_call(
        body, name="small_allreduce_adamw",
        in_specs=[vm] * 4, out_specs=[vm] * 4, out_shape=[shp] * 4,
        scratch_shapes=[
            pltpu.VMEM((N_DEV,) + part.shape, F32),
            pltpu.SemaphoreType.DMA((N_DEV - 1,)), pltpu.SemaphoreType.DMA((N_DEV - 1,)),
        ],
    )(part, w, m, v)


def _rope_tables(T):
    half = HEAD_DIM // 2
    inv_freq = ROPE_THETA ** (-jnp.arange(half, dtype=F32) / half)
    ang = jnp.arange(T).astype(F32)[:, None] * inv_freq[None, :]
    cos = jnp.tile(jnp.cos(ang), (1, LANES // half))
    sin = jnp.tile(jnp.sin(ang), (1, LANES // half))
    lane = jnp.arange(LANES)
    sign = jnp.where((lane % HEAD_DIM) < half, -1.0, 1.0).astype(F32)
    return cos, sin * sign[None, :]


def _pack_small(ffn1, mix, ffn2, kvn, fin, sinks, loss_row):
    sink_row = jnp.zeros((1, D_MODEL), F32).at[0, :SWA_Q_HEADS].set(sinks.reshape(-1))
    rows = jnp.concatenate([ffn1, mix, ffn2, kvn.reshape(1, -1), fin.reshape(1, -1), sink_row, loss_row], axis=0)
    return jnp.concatenate([rows, jnp.zeros((SMALL_ROWS - rows.shape[0], D_MODEL), F32)], axis=0)


def kernel(x, ffn1_norm, ffn1_w_in, ffn1_w_out, mix_norm, ffn2_norm, ffn2_w_in, ffn2_w_out, sb_w_qkv, sb_w_o, kv_norm, kv_w, swa_w_q, swa_sinks, swa_w_o, final_norm, loss_target, m_ffn1_norm, m_ffn1_w_in, m_ffn1_w_out, m_mix_norm, m_ffn2_norm, m_ffn2_w_in, m_ffn2_w_out, m_sb_w_qkv, m_sb_w_o, m_kv_norm, m_kv_w, m_swa_w_q, m_swa_sinks, m_swa_w_o, m_final_norm, v_ffn1_norm, v_ffn1_w_in, v_ffn1_w_out, v_mix_norm, v_ffn2_norm, v_ffn2_w_in, v_ffn2_w_out, v_sb_w_qkv, v_sb_w_o, v_kv_norm, v_kv_w, v_swa_w_q, v_swa_sinks, v_swa_w_o, v_final_norm):
    T = x.shape[1]
    kv_cols = SWA_KV_HEADS * HEAD_DIM
    x2 = x.reshape(T, D_MODEL)
    tgt = loss_target.reshape(T, D_MODEL)
    cos, sin = _rope_tables(T)

    w_in_l = jnp.concatenate([ffn1_w_in, ffn2_w_in], axis=0).astype(BF16)
    w_out_l = jnp.concatenate([ffn1_w_out, ffn2_w_out], axis=0).astype(BF16)
    sq_l = jnp.concatenate([sb_w_o, swa_w_q, swa_w_o], axis=0).astype(BF16)
    qkv_l = sb_w_qkv[0].astype(BF16)
    kvw_l = kv_w.astype(BF16)
    w_in, w_out, w_sq, w_qkv, w_kv = all_gather_weights([w_in_l, w_out_l, sq_l, qkv_l, kvw_l])
    w_kv = w_kv.reshape(D_MODEL, 2 * kv_cols)

    def vec(a, i):
        return a[i].reshape(1, D_MODEL)

    ident = lambda w: w
    sq_prep = lambda w: w.reshape(D_MODEL, w.shape[-1])
    qscale = jnp.concatenate([jnp.full((1, D_MODEL), ATTN_SCALE, F32), jnp.ones((1, 2 * D_MODEL), F32)], axis=1)
    swa_scale = jnp.full((1, D_MODEL), ATTN_SCALE, F32)
    sinks = swa_sinks.reshape(SWA_Q_HEADS)

    h1, gate1, up1 = ffn_fwd(x2, vec(ffn1_norm, 0), w_in, w_out, SLOT_FFN1[0])
    qkv = rms_linear("sb_qkv", h1, vec(mix_norm, 0), w_qkv,
                     pl.BlockSpec((None, D_MODEL, QKV_COLS), lambda i, j: (j, 0, 0)), ident,
                     3 * D_MODEL, QKV_COLS, scale=qscale)
    o_sb, tot = sb_fwd(qkv)
    h2 = linear_res("sb_out", o_sb, w_sq, SQ_SB_O, h1)
    h3, gate2, up2 = ffn_fwd(h2, vec(ffn2_norm, 0), w_in, w_out, SLOT_FFN2[0])
    kvn = kv_norm.reshape(1, D_MODEL)
    k_sw = rms_linear("kv_k", h3, kvn, w_kv, pl.BlockSpec((D_MODEL, kv_cols), lambda i, j: (0, 0)), ident,
                      kv_cols, kv_cols, rope=(cos, sin))
    v_sw = rms_linear("kv_v", h3, kvn, w_kv, pl.BlockSpec((D_MODEL, kv_cols), lambda i, j: (0, 1)), ident,
                      kv_cols, kv_cols)
    h4, gate3, up3 = ffn_fwd(h3, vec(ffn1_norm, 1), w_in, w_out, SLOT_FFN1[1])
    q_sw = rms_linear("swa_q", h4, vec(mix_norm, 1), w_sq,
                      pl.BlockSpec((N_CHIPS, None, SQ_ROWS, 512), lambda i, j: (0, SQ_SWA_Q, 0, j)), sq_prep,
                      D_MODEL, 512, rope=(cos, sin), scale=swa_scale)
    o_sw, lse = swa_fwd(q_sw, k_sw, v_sw, sinks)
    h5 = linear_res("swa_out", o_sw, w_sq, SQ_SWA_O, h4)
    h6, gate4, up4 = ffn_fwd(h5, vec(ffn2_norm, 1), w_in, w_out, SLOT_FFN2[1])
    dh6, loss_p, d_final = loss_bwd(h6, final_norm.reshape(1, D_MODEL), tgt)

    slab = {"in": None, "out": None, "sq": None}
    in_shape = (2, N_CHIPS, 4, D_MODEL // 2, FF_CHUNK)
    out_shape = (2, N_CHIPS, 4, FF_ROWS, D_MODEL // 2)
    sq_shape = (2, N_CHIPS, 3, SQ_ROWS, D_MODEL // 2)

    def ffn_bwd(tag, dh, h_in, g, gate, up, slot):
        dg_, du_, act, dhb = ffn_bwd_act(dh, gate, up, w_out, slot)
        pairs = [
            (dg_, w_in, pl.BlockSpec((None, None, D_MODEL, FF_CHUNK), lambda i, j: (j, slot, 0, 0)), ident),
            (du_, w_in, pl.BlockSpec((None, None, D_MODEL, FF_CHUNK), lambda i, j: (2 + j, slot, 0, 0)), ident),
        ]
        dh_in, xn, dnorm = linear_bwd_rms(f"ffn_bwd_in_{tag}", pairs, h_in, g, dh, D_FF // FF_CHUNK, FF_CHUNK)
        blk = (None, 1, None, D_MODEL // 2, FF_CHUNK)
        slab["in"] = mm_tn(f"dw_gate_{tag}", xn, dg_, D_MODEL // 2, FF_CHUNK, blk,
                           lambda k, n: (k, n, slot, 0, 0), in_shape, prev=slab["in"])
        slab["in"] = mm_tn(f"dw_up_{tag}", xn, du_, D_MODEL // 2, FF_CHUNK, blk,
                           lambda k, n: (k, 2 + n, slot, 0, 0), in_shape, prev=slab["in"])
        slab["out"] = mm_tn(f"dw_out_{tag}", act, dhb, FF_CHUNK, D_MODEL // 2,
                            (None, 2, None, FF_ROWS, D_MODEL // 2),
                            lambda k, n: (n, k, slot, 0, 0), out_shape, prev=slab["out"])
        return dh_in, dnorm

    def sq_grad(tag, a, dyb, t):
        slab["sq"] = mm_tn(f"dw_sq_{tag}", a, dyb, D_MODEL, D_MODEL // 2,
                           (None, N_CHIPS, None, SQ_ROWS, D_MODEL // 2),
                           lambda k, n: (n, 0, t, 0, 0), sq_shape, prev=slab["sq"])

    dh5, d_ffn2_1 = ffn_bwd("l1b", dh6, h5, vec(ffn2_norm, 1), gate4, up4, SLOT_FFN2[1])
    do_sw, dh5b = linear_bwd_plain("swa_out_bwd", dh5, w_sq, SQ_SWA_O)
    sq_grad("swa_o", o_sw, dh5b, SQ_SWA_O)
    dq_sw, kv_own, kv_prev, d_sinks = swa_bwd(q_sw, k_sw, v_sw, sinks, do_sw, lse, cos, sin)
    sq_w_spec = pl.BlockSpec((N_CHIPS, None, SQ_ROWS, D_MODEL), lambda i, j: (0, SQ_SWA_Q, 0, 0))
    dh4, hn4, d_mix_1 = linear_bwd_rms("swa_q_bwd", [(dq_sw, w_sq, sq_w_spec, sq_prep)], h4, vec(mix_norm, 1), dh5,
                                       1, D_MODEL)
    sq_grad("swa_q", hn4, dq_sw, SQ_SWA_Q)
    dh3a, d_ffn1_1 = ffn_bwd("l1a", dh4, h3, vec(ffn1_norm, 1), gate3, up3, SLOT_FFN1[1])
    dkv = kv_grad_combine(kv_own, kv_prev, cos, sin)
    kv_w_spec = pl.BlockSpec((D_MODEL, 2 * kv_cols), lambda i, j: (0, 0))
    dh3, xn3, d_kvn = linear_bwd_rms("kv_bwd", [(dkv, w_kv, kv_w_spec, ident)], h3, kvn, dh3a, 1, 2 * kv_cols)
    slab_kv = mm_tn("dw_kv", xn3, dkv, D_MODEL, kv_cols, (None, N_CHIPS, None, SQ_ROWS, kv_cols),
                    lambda k, n: (n, 0, 0, 0, 0), (2, N_CHIPS, 1, SQ_ROWS, kv_cols))
    dh2, d_ffn2_0 = ffn_bwd("l0b", dh3, h2, vec(ffn2_norm, 0), gate2, up2, SLOT_FFN2[0])
    do_sb, dh2b = linear_bwd_plain("sb_out_bwd", dh2, w_sq, SQ_SB_O)
    sq_grad("sb_o", o_sb, dh2b, SQ_SB_O)
    dq_sb, dk_sb, dv_sb = sb_bwd(qkv, do_sb, tot)
    dqkv = jnp.concatenate([dq_sb, dk_sb, dv_sb], axis=1)
    qkv_w_spec = pl.BlockSpec((None, D_MODEL, QKV_COLS), lambda i, j: (j, 0, 0))
    dh1, hn1, d_mix_0 = linear_bwd_rms("sb_qkv_bwd", [(dqkv, w_qkv, qkv_w_spec, ident)], h1, vec(mix_norm, 0), dh2,
                                       N_CHIPS, QKV_COLS)
    slab_qkv = mm_tn("dw_qkv", hn1, dqkv, D_MODEL // 2, QKV_COLS, (None, 1, None, D_MODEL // 2, QKV_COLS),
                     lambda k, n: (k, n, 0, 0, 0), (2, N_CHIPS, 1, D_MODEL // 2, QKV_COLS))
    dx, d_ffn1_0 = ffn_bwd("l0a", dh1, x2, vec(ffn1_norm, 0), gate1, up1, SLOT_FFN1[0])

    slabs = [slab["in"], slab["out"], slab["sq"], slab_qkv, slab_kv]
    names = ["in", "out", "sq", "qkv", "kv"]
    core = lax.axis_index("c").astype(jnp.int32).reshape(1)
    chip = (2 * lax.axis_index("x") + lax.axis_index("y")).astype(jnp.int32).reshape(1)
    theirs = exchange_halves(slabs)
    parts = [add_sibling(f"add_sibling_{nm}", s, t, core) for nm, s, t in zip(names, slabs, theirs)]
    gathered = exchange_chip_partials(parts)
    halves = [sum_chips(f"sum_chips_{nm}", g, p, chip) for nm, g, p in zip(names, gathered, parts)]
    sib_halves = share_reduced_halves(halves)
    g_in, g_out, g_sq, g_qkv, g_kv = zip(halves, sib_halves)

    def upd(name, w, m, v, g_pair, slot0, row_halves):
        shp = w.shape
        w3 = w.reshape((-1,) + shp[-2:])
        outs = adamw_shard(name, w3, m.reshape(w3.shape), v.reshape(w3.shape), g_pair[0], g_pair[1], core,
                           slot0, row_halves)
        return [o.reshape(shp) for o in outs]

    r_ffn1_in = upd("adamw_ffn1_in", ffn1_w_in, m_ffn1_w_in, v_ffn1_w_in, g_in, 0, True)
    r_ffn2_in = upd("adamw_ffn2_in", ffn2_w_in, m_ffn2_w_in, v_ffn2_w_in, g_in, 2, True)
    r_ffn1_out = upd("adamw_ffn1_out", ffn1_w_out, m_ffn1_w_out, v_ffn1_w_out, g_out, 0, False)
    r_ffn2_out = upd("adamw_ffn2_out", ffn2_w_out, m_ffn2_w_out, v_ffn2_w_out, g_out, 2, False)
    r_qkv = upd("adamw_qkv", sb_w_qkv, m_sb_w_qkv, v_sb_w_qkv, g_qkv, 0, True)
    r_sb_o = upd("adamw_sb_o", sb_w_o, m_sb_w_o, v_sb_w_o, g_sq, SQ_SB_O, False)
    r_swa_q = upd("adamw_swa_q", swa_w_q, m_swa_w_q, v_swa_w_q, g_sq, SQ_SWA_Q, False)
    r_swa_o = upd("adamw_swa_o", swa_w_o, m_swa_w_o, v_swa_w_o, g_sq, SQ_SWA_O, False)
    r_kv = upd("adamw_kv", kv_w, m_kv_w, v_kv_w, g_kv, 0, False)

    loss_row = jnp.zeros((1, D_MODEL), F32).at[0, :LANES].set(loss_p[0])
    d_sink_row = d_sinks[0, :SWA_Q_HEADS]
    part = _pack_small(jnp.concatenate([d_ffn1_0, d_ffn1_1], axis=0), jnp.concatenate([d_mix_0, d_mix_1], axis=0),
                       jnp.concatenate([d_ffn2_0, d_ffn2_1], axis=0), d_kvn, d_final, d_sink_row, loss_row)
    zrow = jnp.zeros((1, D_MODEL), F32)
    small = small_allreduce_adamw(
        part,
        _pack_small(ffn1_norm, mix_norm, ffn2_norm, kv_norm, final_norm, swa_sinks, zrow),
        _pack_small(m_ffn1_norm, m_mix_norm, m_ffn2_norm, m_kv_norm, m_final_norm, m_swa_sinks, zrow),
        _pack_small(v_ffn1_norm, v_mix_norm, v_ffn2_norm, v_kv_norm, v_final_norm, v_swa_sinks, zrow))

    def unpack(p):
        return dict(ffn1_norm=p[0:2], mix_norm=p[2:4], ffn2_norm=p[4:6], kv_norm=p[6], final_norm=p[7],
                    swa_sinks=p[8:9, :SWA_Q_HEADS])

    big = dict(ffn1_w_in=r_ffn1_in, ffn1_w_out=r_ffn1_out, ffn2_w_in=r_ffn2_in, ffn2_w_out=r_ffn2_out,
               sb_w_qkv=r_qkv, sb_w_o=r_sb_o, kv_w=r_kv, swa_w_q=r_swa_q, swa_w_o=r_swa_o)
    order = ["ffn1_norm", "ffn1_w_in", "ffn1_w_out", "mix_norm", "ffn2_norm", "ffn2_w_in", "ffn2_w_out",
             "sb_w_qkv", "sb_w_o", "kv_norm", "kv_w", "swa_w_q", "swa_sinks", "swa_w_o", "final_norm"]
    outs = []
    for kind in range(4):
        sm = unpack(small[kind])
        for nm in order:
            outs.append(big[nm][kind] if nm in big else sm[nm])
    loss = small[0][9, 0]
    return (loss, dx.reshape(x.shape), *outs)
```

```python
import functools

import jax
import jax.numpy as jnp
from jax import lax
from jax.experimental import pallas as pl
from jax.experimental.pallas import tpu as pltpu

F32 = jnp.float32
BF16 = jnp.bfloat16
MESH = pl.DeviceIdType.MESH

D_MODEL = 1024
D_FF = 2816
HEAD_DIM = 64
SB_HEADS = 16
SWA_Q_HEADS = 16
SWA_KV_HEADS = 4
WINDOW = 128
ROPE_THETA = 10000.0
RMS_EPS = 1e-6
FFN_RES_SCALE = 0.5
ATTN_SCALE = HEAD_DIM ** -0.5

ADAM_LR = 0.001
ADAM_B1 = 0.9
ADAM_B2 = 0.999
ADAM_EPS = 1e-08
ADAM_WD = 0.01
ADAM_STEP = 10

N_CHIPS = 4
N_DEV = 8
LANES = 128
FF_CHUNK = D_FF // 2
FF_ROWS = D_FF // N_CHIPS
SQ_ROWS = D_MODEL // N_CHIPS
QKV_COLS = 3 * D_MODEL // N_CHIPS
VMEM_LIMIT = 56 * 1024 * 1024
NEG_BIG = -1e30

SLOT_FFN1 = (0, 1)
SLOT_FFN2 = (2, 3)
SQ_SB_O, SQ_SWA_Q, SQ_SWA_O = 0, 1, 2


def _cparams():
    return pltpu.CompilerParams(vmem_limit_bytes=VMEM_LIMIT)


def _dot(a, b):
    return jnp.dot(a, b, preferred_element_type=F32)


def _dot_nt(a, b):
    return lax.dot_general(a, b, (((1,), (1,)), ((), ())), preferred_element_type=F32)


def _dot_tn(a, b):
    return lax.dot_general(a, b, (((0,), (0,)), ((), ())), preferred_element_type=F32)


def _rstd(h):
    return lax.rsqrt(jnp.mean(h * h, axis=-1, keepdims=True) + RMS_EPS)


def _swap32(x):
    n = x.shape[-1]
    lane = lax.broadcasted_iota(jnp.int32, x.shape, x.ndim - 1)
    first = (lane % HEAD_DIM) < (HEAD_DIM // 2)
    return jnp.where(first, pltpu.roll(x, n - HEAD_DIM // 2, x.ndim - 1), pltpu.roll(x, HEAD_DIM // 2, x.ndim - 1))


def _tile_lanes(t, n):
    return t if n == LANES else jnp.tile(t, (1, n // LANES))


def ffn_fwd(h, g, w_in, w_out, slot):
    T = h.shape[0]
    tm = 256
    nch = D_FF // FF_CHUNK

    def body(h_ref, g_ref, wg_ref, wu_ref, wo_ref, out_ref, gate_ref, up_ref, xn_s, acc_s):
        j = pl.program_id(1)

        @pl.when(j == 0)
        def _():
            hh = h_ref[...]
            xn_s[...] = (hh * _rstd(hh) * g_ref[...]).astype(BF16)
            acc_s[...] = jnp.zeros_like(acc_s)

        xn = xn_s[...]
        gate = _dot(xn, wg_ref[...])
        up = _dot(xn, wu_ref[...])
        gate_ref[...] = gate
        up_ref[...] = up
        a = gate * jax.nn.sigmoid(gate) * up
        acc_s[...] += _dot(a.astype(BF16), wo_ref[...].reshape(FF_CHUNK, D_MODEL))

        @pl.when(j == nch - 1)
        def _():
            out_ref[...] = h_ref[...] + FFN_RES_SCALE * acc_s[...]

    return pl.pallas_call(
        body,
        name=f"ffn_fwd_{slot}",
        grid=(T // tm, nch),
        in_specs=[
            pl.BlockSpec((tm, D_MODEL), lambda i, j: (i, 0)),
            pl.BlockSpec((1, D_MODEL), lambda i, j: (0, 0)),
            pl.BlockSpec((None, None, D_MODEL, FF_CHUNK), lambda i, j: (j, slot, 0, 0)),
            pl.BlockSpec((None, None, D_MODEL, FF_CHUNK), lambda i, j: (2 + j, slot, 0, 0)),
            pl.BlockSpec((2, None, FF_ROWS, D_MODEL), lambda i, j: (j, slot, 0, 0)),
        ],
        out_specs=[
            pl.BlockSpec((tm, D_MODEL), lambda i, j: (i, 0)),
            pl.BlockSpec((tm, FF_CHUNK), lambda i, j: (i, j)),
            pl.BlockSpec((tm, FF_CHUNK), lambda i, j: (i, j)),
        ],
        out_shape=[
            jax.ShapeDtypeStruct((T, D_MODEL), F32),
            jax.ShapeDtypeStruct((T, D_FF), F32),
            jax.ShapeDtypeStruct((T, D_FF), F32),
        ],
        scratch_shapes=[pltpu.VMEM((tm, D_MODEL), BF16), pltpu.VMEM((tm, D_MODEL), F32)],
        compiler_params=_cparams(),
    )(h, g, w_in, w_in, w_out)


def ffn_bwd_act(dh, gate, up, w_out, slot):
    T = dh.shape[0]
    tm = 256
    nch = D_FF // FF_CHUNK

    def body(dh_ref, gate_ref, up_ref, wo_ref, dg_ref, du_ref, a_ref, dhb_ref):
        dhb = (FFN_RES_SCALE * dh_ref[...]).astype(BF16)
        dhb_ref[...] = dhb
        da = _dot_nt(dhb, wo_ref[...].reshape(FF_CHUNK, D_MODEL))
        gt = gate_ref[...]
        u = up_ref[...]
        s = jax.nn.sigmoid(gt)
        silu = gt * s
        a_ref[...] = (silu * u).astype(BF16)
        dg_ref[...] = (da * u * (s * (1.0 + gt * (1.0 - s)))).astype(BF16)
        du_ref[...] = (da * silu).astype(BF16)

    ff = pl.BlockSpec((tm, FF_CHUNK), lambda i, j: (i, j))
    return pl.pallas_call(
        body,
        name=f"ffn_bwd_act_{slot}",
        grid=(T // tm, nch),
        in_specs=[
            pl.BlockSpec((tm, D_MODEL), lambda i, j: (i, 0)),
            ff, ff,
            pl.BlockSpec((2, None, FF_ROWS, D_MODEL), lambda i, j: (j, slot, 0, 0)),
        ],
        out_specs=[ff, ff, ff, pl.BlockSpec((tm, D_MODEL), lambda i, j: (i, 0))],
        out_shape=[
            jax.ShapeDtypeStruct((T, D_FF), BF16),
            jax.ShapeDtypeStruct((T, D_FF), BF16),
            jax.ShapeDtypeStruct((T, D_FF), BF16),
            jax.ShapeDtypeStruct((T, D_MODEL), BF16),
        ],
        compiler_params=_cparams(),
    )(dh, gate, up, w_out)


def rms_linear(name, h, g, w, w_spec, w_prep, n_out, tn, *, rope=None, scale=None):
    T = h.shape[0]
    tm = 512
    extra, extra_specs = [], []
    if rope is not None:
        extra += list(rope)
        extra_specs += [pl.BlockSpec((tm, LANES), lambda i, j: (i, 0))] * 2
    if scale is not None:
        extra.append(scale)
        extra_specs.append(pl.BlockSpec((1, tn), lambda i, j: (0, j)))

    def body(h_ref, g_ref, w_ref, *rest):
        rest = list(rest)
        cos_ref = sin_ref = sc_ref = None
        if rope is not None:
            cos_ref, sin_ref = rest[0], rest[1]
            rest = rest[2:]
        if scale is not None:
            sc_ref = rest[0]
            rest = rest[1:]
        out_ref, xn_s = rest

        @pl.when(pl.program_id(1) == 0)
        def _():
            hh = h_ref[...]
            xn_s[...] = (hh * _rstd(hh) * g_ref[...]).astype(BF16)

        y = _dot(xn_s[...], w_prep(w_ref[...]))
        if rope is not None:
            y = y * _tile_lanes(cos_ref[...], tn) + _swap32(y) * _tile_lanes(sin_ref[...], tn)
        if scale is not None:
            y = y * sc_ref[...]
        out_ref[...] = y.astype(BF16)

    return pl.pallas_call(
        body,
        name=name,
        grid=(T // tm, n_out // tn),
        in_specs=[
            pl.BlockSpec((tm, D_MODEL), lambda i, j: (i, 0)),
            pl.BlockSpec((1, D_MODEL), lambda i, j: (0, 0)),
            w_spec,
        ] + extra_specs,
        out_specs=pl.BlockSpec((tm, tn), lambda i, j: (i, j)),
        out_shape=jax.ShapeDtypeStruct((T, n_out), BF16),
        scratch_shapes=[pltpu.VMEM((tm, D_MODEL), BF16)],
        compiler_params=_cparams(),
    )(h, g, w, *extra)


def linear_res(name, a, w_sq, t, res):
    T = a.shape[0]
    tm = 512

    def body(a_ref, w_ref, res_ref, out_ref):
        out_ref[...] = res_ref[...] + _dot(a_ref[...], w_ref[...].reshape(D_MODEL, D_MODEL))

    row = pl.BlockSpec((tm, D_MODEL), lambda i: (i, 0))
    return pl.pallas_call(
        body,
        name=name,
        grid=(T // tm,),
        in_specs=[row, pl.BlockSpec((N_CHIPS, None, SQ_ROWS, D_MODEL), lambda i: (0, t, 0, 0)), row],
        out_specs=row,
        out_shape=jax.ShapeDtypeStruct((T, D_MODEL), F32),
        compiler_params=_cparams(),
    )(a, w_sq, res)


def linear_bwd_plain(name, dy, w_sq, t):
    T = dy.shape[0]
    tm = 512

    def body(dy_ref, w_ref, da_ref, dyb_ref):
        dyb = dy_ref[...].astype(BF16)
        dyb_ref[...] = dyb
        da_ref[...] = _dot_nt(dyb, w_ref[...].reshape(D_MODEL, D_MODEL)).astype(BF16)

    row = pl.BlockSpec((tm, D_MODEL), lambda i: (i, 0))
    return pl.pallas_call(
        body,
        name=name,
        grid=(T // tm,),
        in_specs=[row, pl.BlockSpec((N_CHIPS, None, SQ_ROWS, D_MODEL), lambda i: (0, t, 0, 0))],
        out_specs=[row, row],
        out_shape=[jax.ShapeDtypeStruct((T, D_MODEL), BF16), jax.ShapeDtypeStruct((T, D_MODEL), BF16)],
        compiler_params=_cparams(),
    )(dy, w_sq)


def linear_bwd_rms(name, pairs, h, g, dres, nch, tn, tm=256):
    T = h.shape[0]
    npair = len(pairs)

    def body(*refs):
        dy_refs = refs[:npair]
        w_refs = refs[npair:2 * npair]
        h_ref, g_ref, dres_ref, dh_ref, xn_ref, dg_ref, acc_s = refs[2 * npair:]
        i = pl.program_id(0)
        j = pl.program_id(1)

        @pl.when(j == 0)
        def _():
            acc_s[...] = jnp.zeros_like(acc_s)

        @pl.when((i == 0) & (j == 0))
        def _():
            dg_ref[...] = jnp.zeros_like(dg_ref)

        part = None
        for p in range(npair):
            d = _dot_nt(dy_refs[p][...], pairs[p][3](w_refs[p][...]))
            part = d if part is None else part + d
        acc_s[...] += part

        @pl.when(j == nch - 1)
        def _():
            dxn = acc_s[...]
            hh = h_ref[...]
            gg = g_ref[...]
            r = _rstd(hh)
            hr = hh * r
            xn_ref[...] = (hr * gg).astype(BF16)
            dg_ref[...] += jnp.sum(dxn * hr, axis=0, keepdims=True)
            gd = gg * dxn
            dh_ref[...] = dres_ref[...] + r * (gd - hr * jnp.mean(gd * hr, axis=-1, keepdims=True))

    row = pl.BlockSpec((tm, D_MODEL), lambda i, j: (i, 0))
    vec = pl.BlockSpec((1, D_MODEL), lambda i, j: (0, 0))
    return pl.pallas_call(
        body,
        name=name,
        grid=(T // tm, nch),
        in_specs=[pl.BlockSpec((tm, tn), lambda i, j: (i, j))] * npair + [p[2] for p in pairs] + [row, vec, row],
        out_specs=[row, row, vec],
        out_shape=[
            jax.ShapeDtypeStruct((T, D_MODEL), F32),
            jax.ShapeDtypeStruct((T, D_MODEL), BF16),
            jax.ShapeDtypeStruct((1, D_MODEL), F32),
        ],
        scratch_shapes=[pltpu.VMEM((tm, D_MODEL), F32)],
        compiler_params=_cparams(),
    )(*[p[0] for p in pairs], *[p[1] for p in pairs], h, g, dres)


def loss_bwd(h, g, tgt):
    T = h.shape[0]
    tm = 512

    def body(h_ref, g_ref, t_ref, dh_ref, loss_ref, dg_ref):
        @pl.when(pl.program_id(0) == 0)
        def _():
            loss_ref[...] = jnp.zeros_like(loss_ref)
            dg_ref[...] = jnp.zeros_like(dg_ref)

        hh = h_ref[...]
        gg = g_ref[...]
        r = _rstd(hh)
        hr = hh * r
        err = hr * gg - t_ref[...]
        loss_ref[...] += 0.5 * jnp.sum(jnp.mean(err * err, axis=-1, keepdims=True), axis=0, keepdims=True)
        dy = err * (1.0 / D_MODEL)
        dg_ref[...] += jnp.sum(dy * hr, axis=0, keepdims=True)
        gd = gg * dy
        dh_ref[...] = r * (gd - hr * jnp.mean(gd * hr, axis=-1, keepdims=True))

    row = pl.BlockSpec((tm, D_MODEL), lambda i: (i, 0))
    vec = pl.BlockSpec((1, D_MODEL), lambda i: (0, 0))
    return pl.pallas_call(
        body,
        name="loss_bwd",
        grid=(T // tm,),
        in_specs=[row, vec, row],
        out_specs=[row, pl.BlockSpec((1, LANES), lambda i: (0, 0)), vec],
        out_shape=[
            jax.ShapeDtypeStruct((T, D_MODEL), F32),
            jax.ShapeDtypeStruct((1, LANES), F32),
            jax.ShapeDtypeStruct((1, D_MODEL), F32),
        ],
        compiler_params=_cparams(),
    )(h, g, tgt)


def mm_tn(name, a, b, tk, tn, out_block, out_index, out_shape, prev=None, tt=1024):
    T = a.shape[0]
    ns, r = out_block[1], out_block[3]
    tt = min(tt, T)
    nt = T // tt

    def body(*refs):
        if prev is None:
            a_ref, b_ref, out_ref = refs
        else:
            a_ref, b_ref, _, out_ref = refs
        t = pl.program_id(2)
        res = _dot_tn(a_ref[...], b_ref[...])

        @pl.when(t == 0)
        def _():
            for u in range(ns):
                out_ref[u] = res[u * r:(u + 1) * r]

        @pl.when(t > 0)
        def _():
            for u in range(ns):
                out_ref[u] += res[u * r:(u + 1) * r]

    in_specs = [
        pl.BlockSpec((tt, tk), lambda k, n, t: (t, k)),
        pl.BlockSpec((tt, tn), lambda k, n, t: (t, n)),
    ]
    args = [a, b]
    aliases = {}
    if prev is not None:
        in_specs.append(pl.BlockSpec(memory_space=pl.ANY))
        args.append(prev)
        aliases = {2: 0}
    return pl.pallas_call(
        body,
        name=name,
        grid=(a.shape[1] // tk, b.shape[1] // tn, nt),
        in_specs=in_specs,
        out_specs=pl.BlockSpec(out_block, lambda k, n, t: out_index(k, n)),
        out_shape=jax.ShapeDtypeStruct(out_shape, F32),
        input_output_aliases=aliases,
        compiler_params=_cparams(),
    )(*args)


SB_BLOCK = 256
SB_CHUNK = 128


LOG2E = 1.4426950408889634


def _softplus2(z2):
    sign = jnp.uint32(0x80000000)
    neg_abs = lax.bitcast_convert_type(lax.bitcast_convert_type(z2, jnp.uint32) | sign, F32)
    return jnp.log2(1.0 + jnp.exp2(neg_abs)) + jnp.maximum(z2, 0.0)


def _split_bf16(x):
    hi = x.astype(BF16)
    lo = (x - hi.astype(F32)).astype(BF16)
    return jnp.concatenate([hi, lo], axis=1)


def _twice(x):
    return jnp.concatenate([x, x], axis=1)


def sb_fwd(qkv):
    T = qkv.shape[0]
    tq = SB_BLOCK
    npair = SB_HEADS // 2

    def body(q_ref, k_ref, v_ref, o_ref, tot_ref, acc_s, c_s, z_s, w_s):
        i = pl.program_id(1)
        q = q_ref[...]
        lane = lax.broadcasted_iota(jnp.int32, (tq, LANES), 1)
        first = lane < HEAD_DIM
        zero = jnp.zeros_like(q)
        q_heads = (jnp.where(first, q, zero), jnp.where(first, zero, q))
        row = lax.broadcasted_iota(jnp.int32, (tq, tq), 0)
        col = lax.broadcasted_iota(jnp.int32, (tq, tq), 1)
        strict = col < row
        from_s = (row >= col).astype(BF16)
        from_s2 = jnp.concatenate([from_s, from_s], axis=0)
        acc_s[...] = jnp.zeros_like(acc_s)
        c_s[...] = jnp.zeros_like(c_s)

        def rows(j):
            return pl.ds(pl.multiple_of(j * tq, tq), tq)

        def logits(j):
            kb = k_ref[rows(j), :]
            for hd in range(2):
                z_s[hd] = _dot_nt(q_heads[hd], kb) * LOG2E

        def flush(j):
            vb = v_ref[rows(j), :]
            for hd in range(2):
                acc_s[hd] += _dot(w_s[hd], vb)

        def block(j, diag):
            if not diag:
                flush(j + 1)
            chunks = [(hd, slice(r0, r0 + SB_CHUNK)) for hd in range(2) for r0 in range(0, tq, SB_CHUNK)]
            es, sums = [], []
            for hd, rs in chunks:
                z2 = z_s[hd, rs, :]
                if diag:
                    z2 = jnp.where(strict[rs, :], z2, NEG_BIG)
                sp = _softplus2(z2)
                c = c_s[hd, rs, :]
                es.append(z2 + _twice(c))
                c_s[hd, rs, :] = c - jnp.sum(sp, axis=1, keepdims=True)
                sums.append(_dot(_split_bf16(sp), from_s2))
            logits(jnp.maximum(j - 1, 0))
            for (hd, rs), e, s in zip(chunks, es, sums):
                w_s[hd, rs, :] = jnp.exp2(e - s).astype(BF16)

        logits(i)
        block(i, True)

        @pl.loop(0, i)
        def _(jj):
            block(i - 1 - jj, False)

        flush(0)
        o_ref[...] = jnp.where(first, acc_s[0], acc_s[1]).astype(BF16)
        tot_ref[...] = jnp.where(first, c_s[0], c_s[1])

    return pl.pallas_call(
        body,
        name="sb_fwd",
        grid=(npair, T // tq),
        in_specs=[
            pl.BlockSpec((tq, LANES), lambda p, i: (i, p)),
            pl.BlockSpec((T, LANES), lambda p, i: (0, npair + p)),
            pl.BlockSpec((T, LANES), lambda p, i: (0, 2 * npair + p)),
        ],
        out_specs=[pl.BlockSpec((tq, LANES), lambda p, i: (i, p))] * 2,
        out_shape=[jax.ShapeDtypeStruct((T, D_MODEL), BF16), jax.ShapeDtypeStruct((T, D_MODEL), F32)],
        scratch_shapes=[
            pltpu.VMEM((2, tq, LANES), F32), pltpu.VMEM((2, tq, LANES), F32),
            pltpu.VMEM((2, tq, tq), F32), pltpu.VMEM((2, tq, tq), BF16),
        ],
        compiler_params=_cparams(),
    )(qkv, qkv, qkv)


def sb_bwd(qkv, do, tot):
    T = qkv.shape[0]
    tq = SB_BLOCK
    npair = SB_HEADS // 2
    nq = T // tq

    def body(q_ref, k_ref, v_ref, do_ref, tot_ref, dq_ref, dk_ref, dv_ref,
             dk_s, dv_s, dq_s, rest_s, cg_s, z_s, da_s, dz_s, a_s):
        i = pl.program_id(1)

        @pl.when(i == 0)
        def _():
            dk_s[...] = jnp.zeros_like(dk_s)
            dv_s[...] = jnp.zeros_like(dv_s)

        q = q_ref[...]
        do_ = do_ref[...]
        tot_ = tot_ref[...]
        lane = lax.broadcasted_iota(jnp.int32, (tq, LANES), 1)
        first = lane < HEAD_DIM
        zero = jnp.zeros_like(q)
        q_heads = (jnp.where(first, q, zero), jnp.where(first, zero, q))
        do_heads = (jnp.where(first, do_, zero), jnp.where(first, zero, do_))
        row = lax.broadcasted_iota(jnp.int32, (tq, tq), 0)
        col = lax.broadcasted_iota(jnp.int32, (tq, tq), 1)
        strict = col < row
        before = (row < col).astype(BF16)
        before2 = jnp.concatenate([before, before], axis=0)
        rest_s[0] = jnp.broadcast_to(tot_[:, 0:1], (tq, LANES))
        rest_s[1] = jnp.broadcast_to(tot_[:, HEAD_DIM:HEAD_DIM + 1], (tq, LANES))
        cg_s[...] = jnp.zeros_like(cg_s)
        dq_s[...] = jnp.zeros_like(dq_s)
        dz_s[...] = jnp.zeros_like(dz_s)
        a_s[...] = jnp.zeros_like(a_s)

        def rows(j):
            return pl.ds(pl.multiple_of(j * tq, tq), tq)

        def logits(j):
            kb = k_ref[rows(j), :]
            vb = v_ref[rows(j), :]
            for hd in range(2):
                z_s[hd] = _dot_nt(q_heads[hd], kb) * LOG2E
                da_s[hd] = _dot_nt(do_heads[hd], vb)

        def flush(j):
            kb = k_ref[rows(j), :]
            for hd in range(2):
                dq_s[hd] += _dot(dz_s[hd], kb)
            dk_s[rows(j), :] += _dot_tn(dz_s[0], q_heads[0]) + _dot_tn(dz_s[1], q_heads[1])
            dv_s[rows(j), :] += _dot_tn(a_s[0], do_heads[0]) + _dot_tn(a_s[1], do_heads[1])

        def block(j, diag):
            flush(jnp.maximum(j - 1, 0))
            chunks = [(hd, slice(r0, r0 + SB_CHUNK)) for hd in range(2) for r0 in range(0, tq, SB_CHUNK)]
            stage1 = []
            for hd, rs in chunks:
                z2 = z_s[hd, rs, :]
                if diag:
                    z2 = jnp.where(strict[rs, :], z2, NEG_BIG)
                sp = _softplus2(z2)
                rest = rest_s[hd, rs, :]
                e = z2 + _twice(rest)
                rest_s[hd, rs, :] = rest + jnp.sum(sp, axis=1, keepdims=True)
                stage1.append((e, z2 - sp, _dot(_split_bf16(sp), before2)))
            stage2 = []
            for (hd, rs), (e, log2_beta, done) in zip(chunks, stage1):
                a = jnp.exp2(e + done)
                g = a * da_s[hd, rs, :]
                cg = cg_s[hd, rs, :]
                a_s[hd, rs, :] = a.astype(BF16)
                cg_s[hd, rs, :] = cg + jnp.sum(g, axis=1, keepdims=True)
                stage2.append((g, g + _twice(cg), log2_beta, _dot(g.astype(BF16), before)))
            logits(jnp.minimum(j + 1, i))
            for (hd, rs), (g, g_from, log2_beta, g_before) in zip(chunks, stage2):
                dz_s[hd, rs, :] = (g - jnp.exp2(log2_beta) * (g_from + g_before)).astype(BF16)

        logits(0)

        @pl.loop(0, i)
        def _(j):
            block(j, False)

        block(i, True)
        flush(i)
        dq_ref[...] = (jnp.where(first, dq_s[0], dq_s[1]) * ATTN_SCALE).astype(BF16)

        @pl.when(i == nq - 1)
        def _():
            dk_ref[...] = dk_s[...].astype(BF16)
            dv_ref[...] = dv_s[...].astype(BF16)

    qblk = pl.BlockSpec((tq, LANES), lambda p, i: (i, p))
    full = pl.BlockSpec((T, LANES), lambda p, i: (0, p))
    return pl.pallas_call(
        body,
        name="sb_bwd",
        grid=(npair, nq),
        in_specs=[
            qblk,
            pl.BlockSpec((T, LANES), lambda p, i: (0, npair + p)),
            pl.BlockSpec((T, LANES), lambda p, i: (0, 2 * npair + p)),
            qblk, qblk,
        ],
        out_specs=[qblk, full, full],
        out_shape=[jax.ShapeDtypeStruct((T, D_MODEL), BF16)] * 3,
        scratch_shapes=[
            pltpu.VMEM((T, LANES), F32), pltpu.VMEM((T, LANES), F32),
            pltpu.VMEM((2, tq, LANES), F32), pltpu.VMEM((2, tq, LANES), F32), pltpu.VMEM((2, tq, LANES), F32),
            pltpu.VMEM((2, tq, tq), F32), pltpu.VMEM((2, tq, tq), F32),
            pltpu.VMEM((2, tq, tq), BF16), pltpu.VMEM((2, tq, tq), BF16),
        ],
        compiler_params=_cparams(),
    )(qkv, qkv, qkv, do, tot)


def _swa_valid(n):
    qi = lax.broadcasted_iota(jnp.int32, (WINDOW, 2 * WINDOW), 0)
    ki = lax.broadcasted_iota(jnp.int32, (WINDOW, 2 * WINDOW), 1)
    diff = qi + WINDOW - ki
    return (diff >= 0) & (diff < WINDOW) & ((n > 0) | (ki >= WINDOW))


def _to_half(x, first, src, dst):
    keep = first if src == 0 else jnp.logical_not(first)
    x = jnp.where(keep, x, jnp.zeros_like(x))
    if src != dst:
        x = pltpu.roll(x.astype(F32), HEAD_DIM, 1).astype(BF16)
    return x


def _kv_band(prev_ref, cur_ref, pb):
    cols = slice(pb * LANES, (pb + 1) * LANES)
    return jnp.concatenate([prev_ref[:, cols], cur_ref[:, cols]], axis=0)


def _swa_specs(T):
    nb = T // WINDOW
    kv_w = SWA_KV_HEADS * HEAD_DIM
    qrow = pl.BlockSpec((WINDOW, D_MODEL), lambda n: (n, 0))
    cur = pl.BlockSpec((WINDOW, kv_w), lambda n: (n, 0))
    prev = pl.BlockSpec((WINDOW, kv_w), lambda n: (jnp.maximum(n - 1, 0), 0))
    smem = pl.BlockSpec(memory_space=pltpu.SMEM)
    return nb, qrow, cur, prev, smem


def swa_fwd(q, k, v, sinks):
    T = q.shape[0]
    nb, qrow, cur, prev, smem = _swa_specs(T)

    def body(sink_ref, q_ref, kc_ref, kp_ref, vc_ref, vp_ref, o_ref, lse_ref):
        n = pl.program_id(0)
        lane = lax.broadcasted_iota(jnp.int32, (WINDOW, LANES), 1)
        first = lane < HEAD_DIM
        valid = _swa_valid(n)
        lse_acc = jnp.zeros((WINDOW, LANES), F32)
        for pb in range(SWA_KV_HEADS // 2):
            k2 = _kv_band(kp_ref, kc_ref, pb)
            v2 = _kv_band(vp_ref, vc_ref, pb)
            for b in range(2):
                kvh = 2 * pb + b
                for qq in range(2):
                    cols = slice((2 * kvh + qq) * LANES, (2 * kvh + qq + 1) * LANES)
                    qp = q_ref[:, cols]
                    outs = []
                    for a in range(2):
                        head = 4 * kvh + 2 * qq + a
                        qh = _to_half(qp, first, a, b)
                        s = jnp.where(valid, _dot_nt(qh, k2), NEG_BIG)
                        sink = sink_ref[head]
                        m = jnp.maximum(jnp.max(s, axis=1, keepdims=True), sink)
                        p = jnp.exp(s - m)
                        den = jnp.sum(p, axis=1, keepdims=True) + jnp.exp(sink - m)
                        o = _dot((p / den).astype(BF16), v2)
                        if a != b:
                            o = pltpu.roll(o, HEAD_DIM, 1)
                        outs.append(o)
                        lse_acc = jnp.where(lane == head, m + jnp.log(den), lse_acc)
                    o_ref[:, cols] = jnp.where(first, outs[0], outs[1]).astype(BF16)
        lse_ref[...] = lse_acc

    return pl.pallas_call(
        body,
        name="swa_fwd",
        grid=(nb,),
        in_specs=[smem, qrow, cur, prev, cur, prev],
        out_specs=[qrow, pl.BlockSpec((WINDOW, LANES), lambda n: (n, 0))],
        out_shape=[jax.ShapeDtypeStruct((T, D_MODEL), BF16), jax.ShapeDtypeStruct((T, LANES), F32)],
        compiler_params=_cparams(),
    )(sinks, q, k, k, v, v)


def swa_bwd(q, k, v, sinks, do, lse, cos, sin):
    T = q.shape[0]
    nb, qrow, cur, prev, smem = _swa_specs(T)
    kv_w = SWA_KV_HEADS * HEAD_DIM

    def body(sink_ref, q_ref, kc_ref, kp_ref, vc_ref, vp_ref, do_ref, lse_ref, cos_ref, sin_ref,
             dq_ref, own_ref, prv_ref, dsink_ref):
        n = pl.program_id(0)

        @pl.when(n == 0)
        def _():
            dsink_ref[...] = jnp.zeros_like(dsink_ref)

        lane = lax.broadcasted_iota(jnp.int32, (WINDOW, LANES), 1)
        lane1 = lax.broadcasted_iota(jnp.int32, (1, LANES), 1)
        first = lane < HEAD_DIM
        valid = _swa_valid(n)
        cos_ = cos_ref[...]
        sin_ = sin_ref[...]
        dsink = jnp.zeros((1, LANES), F32)
        for pb in range(SWA_KV_HEADS // 2):
            k2 = _kv_band(kp_ref, kc_ref, pb)
            v2 = _kv_band(vp_ref, vc_ref, pb)
            dk2 = jnp.zeros((2 * WINDOW, LANES), F32)
            dv2 = jnp.zeros((2 * WINDOW, LANES), F32)
            for b in range(2):
                kvh = 2 * pb + b
                for qq in range(2):
                    cols = slice((2 * kvh + qq) * LANES, (2 * kvh + qq + 1) * LANES)
                    qp = q_ref[:, cols]
                    dop = do_ref[:, cols]
                    dqs = []
                    for a in range(2):
                        head = 4 * kvh + 2 * qq + a
                        qh = _to_half(qp, first, a, b)
                        doh = _to_half(dop, first, a, b)
                        s = jnp.where(valid, _dot_nt(qh, k2), NEG_BIG)
                        lse_h = lse_ref[:, head:head + 1]
                        p = jnp.exp(s - lse_h)
                        dp = _dot_nt(doh, v2)
                        delta = jnp.sum(p * dp, axis=1, keepdims=True)
                        ds = (p * (dp - delta)).astype(BF16)
                        p_sink = jnp.exp(sink_ref[head] - lse_h)
                        dsink = dsink + jnp.where(lane1 == head, -jnp.sum(p_sink * delta, axis=0, keepdims=True), 0.0)
                        dq = _dot(ds, k2)
                        if a != b:
                            dq = pltpu.roll(dq, HEAD_DIM, 1)
                        dqs.append(dq)
                        dk2 = dk2 + _dot_tn(ds, qh)
                        dv2 = dv2 + _dot_tn(p.astype(BF16), doh)
                    dqp = jnp.where(first, dqs[0], dqs[1])
                    dq_ref[:, cols] = ((dqp * cos_ + _swap32(dqp * sin_)) * ATTN_SCALE).astype(BF16)
            kcols = slice(pb * LANES, (pb + 1) * LANES)
            vcols = slice(kv_w + pb * LANES, kv_w + (pb + 1) * LANES)
            prv_ref[:, kcols] = dk2[:WINDOW]
            own_ref[:, kcols] = dk2[WINDOW:]
            prv_ref[:, vcols] = dv2[:WINDOW]
            own_ref[:, vcols] = dv2[WINDOW:]
        dsink_ref[...] += dsink

    tab = pl.BlockSpec((WINDOW, LANES), lambda n: (n, 0))
    kvrow = pl.BlockSpec((WINDOW, 2 * kv_w), lambda n: (n, 0))
    return pl.pallas_call(
        body,
        name="swa_bwd",
        grid=(nb,),
        in_specs=[smem, qrow, cur, prev, cur, prev, qrow, tab, tab, tab],
        out_specs=[qrow, kvrow, kvrow, pl.BlockSpec((1, LANES), lambda n: (0, 0))],
        out_shape=[
            jax.ShapeDtypeStruct((T, D_MODEL), BF16),
            jax.ShapeDtypeStruct((T, 2 * kv_w), F32),
            jax.ShapeDtypeStruct((T, 2 * kv_w), F32),
            jax.ShapeDtypeStruct((1, LANES), F32),
        ],
        compiler_params=_cparams(),
    )(sinks, q, k, k, v, v, do, lse, cos, sin)


def kv_grad_combine(own, prv, cos, sin):
    T = own.shape[0]
    nb = T // WINDOW
    kv_w = SWA_KV_HEADS * HEAD_DIM

    def body(own_ref, nxt_ref, cos_ref, sin_ref, out_ref):
        n = pl.program_id(0)
        nxt = jnp.where(n + 1 < nb, nxt_ref[...], 0.0)
        tot = own_ref[...] + nxt
        dk = tot[:, :kv_w]
        c = _tile_lanes(cos_ref[...], kv_w)
        s = _tile_lanes(sin_ref[...], kv_w)
        out_ref[:, :kv_w] = (dk * c + _swap32(dk * s)).astype(BF16)
        out_ref[:, kv_w:] = tot[:, kv_w:].astype(BF16)

    tab = pl.BlockSpec((WINDOW, LANES), lambda n: (n, 0))
    kvrow = pl.BlockSpec((WINDOW, 2 * kv_w), lambda n: (n, 0))
    return pl.pallas_call(
        body,
        name="kv_grad_combine",
        grid=(nb,),
        in_specs=[kvrow, pl.BlockSpec((WINDOW, 2 * kv_w), lambda n: (jnp.minimum(n + 1, nb - 1), 0)), tab, tab],
        out_specs=kvrow,
        out_shape=jax.ShapeDtypeStruct((T, 2 * kv_w), BF16),
        compiler_params=_cparams(),
    )(own, prv, cos, sin)


ANY = pl.BlockSpec(memory_space=pl.ANY)


def _place():
    x, y, c = lax.axis_index("x"), lax.axis_index("y"), lax.axis_index("c")
    other_chips = [(1 - x, y), (x, 1 - y), (1 - x, 1 - y)]
    return x, y, c, 2 * x + y, other_chips


def all_gather_weights(shards):
    n = len(shards)

    def body(*refs):
        ins, outs = refs[:n], refs[n:2 * n]
        send_sems, recv_sems, local_sems = refs[2 * n:]
        _, _, c, me, chips = _place()
        copies = []
        for t in range(n):
            cp = pltpu.make_async_copy(ins[t], outs[t].at[me], local_sems.at[t])
            cp.start()
            copies.append(cp)
            for jdx, (px, py) in enumerate(chips):
                cp = pltpu.make_async_remote_copy(
                    src_ref=ins[t], dst_ref=outs[t].at[me], send_sem=send_sems.at[t, jdx],
                    recv_sem=recv_sems.at[t, jdx], device_id=(px, py, c), device_id_type=MESH)
                cp.start()
                copies.append(cp)
        for cp in copies:
            cp.wait()

    return pl.pallas_call(
        body,
        name="all_gather_weights",
        in_specs=[ANY] * n,
        out_specs=[ANY] * n,
        out_shape=[jax.ShapeDtypeStruct((N_CHIPS,) + s.shape, s.dtype) for s in shards],
        scratch_shapes=[
            pltpu.SemaphoreType.DMA((n, 3)), pltpu.SemaphoreType.DMA((n, 3)), pltpu.SemaphoreType.DMA((n,)),
        ],
    )(*shards)


def exchange_halves(slabs):
    n = len(slabs)

    def body(*refs):
        ins, theirs = refs[:n], refs[n:2 * n]
        send_sems, recv_sems = refs[2 * n:]
        x, y, c, _, _ = _place()
        copies = []
        for t in range(n):
            cp = pltpu.make_async_remote_copy(
                src_ref=ins[t].at[1 - c], dst_ref=theirs[t], send_sem=send_sems.at[t],
                recv_sem=recv_sems.at[t], device_id=(x, y, 1 - c), device_id_type=MESH)
            cp.start()
            copies.append(cp)
        for cp in copies:
            cp.wait()

    return pl.pallas_call(
        body,
        name="exchange_halves",
        in_specs=[ANY] * n,
        out_specs=[ANY] * n,
        out_shape=[jax.ShapeDtypeStruct(s.shape[1:], s.dtype) for s in slabs],
        scratch_shapes=[pltpu.SemaphoreType.DMA((n,)), pltpu.SemaphoreType.DMA((n,))],
    )(*slabs)


def exchange_chip_partials(parts):
    n = len(parts)

    def body(*refs):
        ins, outs = refs[:n], refs[n:2 * n]
        send_sems, recv_sems = refs[2 * n:]
        _, _, c, me, chips = _place()
        copies = []
        for t in range(n):
            for jdx, (px, py) in enumerate(chips):
                cp = pltpu.make_async_remote_copy(
                    src_ref=ins[t].at[2 * px + py], dst_ref=outs[t].at[me], send_sem=send_sems.at[t, jdx],
                    recv_sem=recv_sems.at[t, jdx], device_id=(px, py, c), device_id_type=MESH)
                cp.start()
                copies.append(cp)
        for cp in copies:
            cp.wait()

    return pl.pallas_call(
        body,
        name="exchange_chip_partials",
        in_specs=[ANY] * n,
        out_specs=[ANY] * n,
        out_shape=[jax.ShapeDtypeStruct(p.shape, p.dtype) for p in parts],
        scratch_shapes=[pltpu.SemaphoreType.DMA((n, 3)), pltpu.SemaphoreType.DMA((n, 3))],
    )(*parts)


def share_reduced_halves(halves):
    n = len(halves)

    def body(*refs):
        ins, outs = refs[:n], refs[n:2 * n]
        send_sems, recv_sems = refs[2 * n:]
        x, y, c, _, _ = _place()
        copies = []
        for t in range(n):
            cp = pltpu.make_async_remote_copy(
                src_ref=ins[t], dst_ref=outs[t], send_sem=send_sems.at[t],
                recv_sem=recv_sems.at[t], device_id=(x, y, 1 - c), device_id_type=MESH)
            cp.start()
            copies.append(cp)
        for cp in copies:
            cp.wait()

    return pl.pallas_call(
        body,
        name="share_reduced_halves",
        in_specs=[ANY] * n,
        out_specs=[ANY] * n,
        out_shape=[jax.ShapeDtypeStruct(h.shape, h.dtype) for h in halves],
        scratch_shapes=[pltpu.SemaphoreType.DMA((n,)), pltpu.SemaphoreType.DMA((n,))],
    )(*halves)


def _row_tile(r, c):
    tr = r
    while tr * c * 4 > (3 << 19) and tr % 16 == 0:
        tr //= 2
    return tr


def add_sibling(name, slab, theirs, core):
    _, ns, slots, r, c = slab.shape
    tr = _row_tile(r, c)

    def body(core_ref, a_ref, b_ref, o_ref):
        o_ref[...] = a_ref[...] + b_ref[...]

    blk = pl.BlockSpec((None, None, tr, c), lambda s, l, i, core_ref: (s, l, i, 0))
    return pl.pallas_call(
        body, name=name,
        grid_spec=pltpu.PrefetchScalarGridSpec(
            num_scalar_prefetch=1, grid=(ns, slots, r // tr),
            in_specs=[pl.BlockSpec((None, None, None, tr, c), lambda s, l, i, core_ref: (core_ref[0], s, l, i, 0)), blk],
            out_specs=blk),
        out_shape=jax.ShapeDtypeStruct(theirs.shape, F32), compiler_params=_cparams(),
    )(core, slab, theirs)


def sum_chips(name, recv, own, chip):
    _, slots, r, c = recv.shape
    tr = _row_tile(r, c)

    def body(chip_ref, r0, r1, r2, r3, own_ref, o_ref):
        me = chip_ref[0]
        mine = own_ref[...]
        terms = [jnp.where(me == s, mine, rr[...]) for s, rr in enumerate((r0, r1, r2, r3))]
        o_ref[...] = ((terms[0] + terms[1]) + terms[2]) + terms[3]

    def src(s):
        return pl.BlockSpec((None, None, tr, c),
                            lambda l, i, chip_ref: (jnp.where(chip_ref[0] == s, (s + 1) % N_CHIPS, s), l, i, 0))

    return pl.pallas_call(
        body, name=name,
        grid_spec=pltpu.PrefetchScalarGridSpec(
            num_scalar_prefetch=1, grid=(slots, r // tr),
            in_specs=[src(0), src(1), src(2), src(3),
                      pl.BlockSpec((None, None, tr, c), lambda l, i, chip_ref: (chip_ref[0], l, i, 0))],
            out_specs=pl.BlockSpec((None, tr, c), lambda l, i, chip_ref: (l, i, 0))),
        out_shape=jax.ShapeDtypeStruct((slots, r, c), F32), compiler_params=_cparams(),
    )(chip, recv, recv, recv, recv, own)


def _adamw_math(w, g, m, v):
    m = ADAM_B1 * m + (1.0 - ADAM_B1) * g
    v = ADAM_B2 * v + (1.0 - ADAM_B2) * (g * g)
    m_hat = m / (1.0 - ADAM_B1 ** ADAM_STEP)
    v_hat = v / (1.0 - ADAM_B2 ** ADAM_STEP)
    delta = -ADAM_LR * (m_hat / (jnp.sqrt(v_hat) + ADAM_EPS) + ADAM_WD * w)
    return delta, m, v


def adamw_shard(name, w, m, v, g_own, g_sib, core, slot0, row_halves):
    n = w.shape[0]
    _, r, c = g_own.shape
    tr = _row_tile(r, c)
    nr = r // tr

    def body(core_ref, w_ref, m_ref, v_ref, own_ref, sib_ref, go_ref, d_ref, mo_ref, vo_ref):
        g = jnp.where(pl.program_id(1) == core_ref[0], own_ref[...], sib_ref[...])
        delta, mm, vv = _adamw_math(w_ref[...], g, m_ref[...], v_ref[...])
        go_ref[...] = g
        d_ref[...] = delta
        mo_ref[...] = mm
        vo_ref[...] = vv

    if row_halves:
        wspec = pl.BlockSpec((None, tr, c), lambda l, h, i, core_ref: (l, h * nr + i, 0))
    else:
        wspec = pl.BlockSpec((None, tr, c), lambda l, h, i, core_ref: (l, i, h))
    gspec = pl.BlockSpec((None, tr, c), lambda l, h, i, core_ref: (slot0 + l, i, 0))
    shp = jax.ShapeDtypeStruct(w.shape, F32)
    return pl.pallas_call(
        body, name=name,
        grid_spec=pltpu.PrefetchScalarGridSpec(
            num_scalar_prefetch=1, grid=(n, 2, nr),
            in_specs=[wspec, wspec, wspec, gspec, gspec], out_specs=[wspec] * 4),
        out_shape=[shp] * 4, compiler_params=_cparams(),
    )(core, w, m, v, g_own, g_sib)


SMALL_ROWS = 16


def small_allreduce_adamw(part, w, m, v):
    def body(p_ref, w_ref, m_ref, v_ref, g_ref, d_ref, mo_ref, vo_ref, buf, send_sems, recv_sems):
        x, y, c, _, _ = _place()
        me = 4 * x + 2 * y + c
        buf[me] = p_ref[...]
        copies = []
        for k in range(1, N_DEV):
            kx, ky, kc = (k >> 2) & 1, (k >> 1) & 1, k & 1
            peer = (x ^ kx, y ^ ky, c ^ kc)
            cp = pltpu.make_async_remote_copy(
                src_ref=p_ref, dst_ref=buf.at[me], send_sem=send_sems.at[k - 1],
                recv_sem=recv_sems.at[k - 1], device_id=peer, device_id_type=MESH)
            cp.start()
            copies.append(cp)
        for cp in copies:
            cp.wait()
        g = buf[0]
        for dev in range(1, N_DEV):
            g = g + buf[dev]
        delta, mm, vv = _adamw_math(w_ref[...], g, m_ref[...], v_ref[...])
        g_ref[...] = g
        d_ref[...] = delta
        mo_ref[...] = mm
        vo_ref[...] = vv

    vm = pl.BlockSpec(memory_space=pltpu.VMEM)
    shp = jax.ShapeDtypeStruct(part.shape, F32)
    return pl.pallas_call(
        body, name="small_allreduce_adamw",
        in_specs=[vm] * 4, out_specs=[vm] * 4, out_shape=[shp] * 4,
        scratch_shapes=[
            pltpu.VMEM((N_DEV,) + part.shape, F32),
            pltpu.SemaphoreType.DMA((N_DEV - 1,)), pltpu.SemaphoreType.DMA((N_DEV - 1,)),
        ],
    )(part, w, m, v)


def _rope_tables(T):
    half = HEAD_DIM // 2
    inv_freq = ROPE_THETA ** (-jnp.arange(half, dtype=F32) / half)
    ang = jnp.arange(T).astype(F32)[:, None] * inv_freq[None, :]
    cos = jnp.tile(jnp.cos(ang), (1, LANES // half))
    sin = jnp.tile(jnp.sin(ang), (1, LANES // half))
    lane = jnp.arange(LANES)
    sign = jnp.where((lane % HEAD_DIM) < half, -1.0, 1.0).astype(F32)
    return cos, sin * sign[None, :]


def _pack_small(ffn1, mix, ffn2, kvn, fin, sinks, loss_row):
    sink_row = jnp.zeros((1, D_MODEL), F32).at[0, :SWA_Q_HEADS].set(sinks.reshape(-1))
    rows = jnp.concatenate([ffn1, mix, ffn2, kvn.reshape(1, -1), fin.reshape(1, -1), sink_row, loss_row], axis=0)
    return jnp.concatenate([rows, jnp.zeros((SMALL_ROWS - rows.shape[0], D_MODEL), F32)], axis=0)


def kernel(x, ffn1_norm, ffn1_w_in, ffn1_w_out, mix_norm, ffn2_norm, ffn2_w_in, ffn2_w_out, sb_w_qkv, sb_w_o, kv_norm, kv_w, swa_w_q, swa_sinks, swa_w_o, final_norm, loss_target, m_ffn1_norm, m_ffn1_w_in, m_ffn1_w_out, m_mix_norm, m_ffn2_norm, m_ffn2_w_in, m_ffn2_w_out, m_sb_w_qkv, m_sb_w_o, m_kv_norm, m_kv_w, m_swa_w_q, m_swa_sinks, m_swa_w_o, m_final_norm, v_ffn1_norm, v_ffn1_w_in, v_ffn1_w_out, v_mix_norm, v_ffn2_norm, v_ffn2_w_in, v_ffn2_w_out, v_sb_w_qkv, v_sb_w_o, v_kv_norm, v_kv_w, v_swa_w_q, v_swa_sinks, v_swa_w_o, v_final_norm):
    T = x.shape[1]
    kv_cols = SWA_KV_HEADS * HEAD_DIM
    x2 = x.reshape(T, D_MODEL)
    tgt = loss_target.reshape(T, D_MODEL)
    cos, sin = _rope_tables(T)

    w_in_l = jnp.concatenate([ffn1_w_in, ffn2_w_in], axis=0).astype(BF16)
    w_out_l = jnp.concatenate([ffn1_w_out, ffn2_w_out], axis=0).astype(BF16)
    sq_l = jnp.concatenate([sb_w_o, swa_w_q, swa_w_o], axis=0).astype(BF16)
    qkv_l = sb_w_qkv[0].astype(BF16)
    kvw_l = kv_w.astype(BF16)
    w_in, w_out, w_sq, w_qkv, w_kv = all_gather_weights([w_in_l, w_out_l, sq_l, qkv_l, kvw_l])
    w_kv = w_kv.reshape(D_MODEL, 2 * kv_cols)

    def vec(a, i):
        return a[i].reshape(1, D_MODEL)

    ident = lambda w: w
    sq_prep = lambda w: w.reshape(D_MODEL, w.shape[-1])
    qscale = jnp.concatenate([jnp.full((1, D_MODEL), ATTN_SCALE, F32), jnp.ones((1, 2 * D_MODEL), F32)], axis=1)
    swa_scale = jnp.full((1, D_MODEL), ATTN_SCALE, F32)
    sinks = swa_sinks.reshape(SWA_Q_HEADS)

    h1, gate1, up1 = ffn_fwd(x2, vec(ffn1_norm, 0), w_in, w_out, SLOT_FFN1[0])
    qkv = rms_linear("sb_qkv", h1, vec(mix_norm, 0), w_qkv,
                     pl.BlockSpec((None, D_MODEL, QKV_COLS), lambda i, j: (j, 0, 0)), ident,
                     3 * D_MODEL, QKV_COLS, scale=qscale)
    o_sb, tot = sb_fwd(qkv)
    h2 = linear_res("sb_out", o_sb, w_sq, SQ_SB_O, h1)
    h3, gate2, up2 = ffn_fwd(h2, vec(ffn2_norm, 0), w_in, w_out, SLOT_FFN2[0])
    kvn = kv_norm.reshape(1, D_MODEL)
    k_sw = rms_linear("kv_k", h3, kvn, w_kv, pl.BlockSpec((D_MODEL, kv_cols), lambda i, j: (0, 0)), ident,
                      kv_cols, kv_cols, rope=(cos, sin))
    v_sw = rms_linear("kv_v", h3, kvn, w_kv, pl.BlockSpec((D_MODEL, kv_cols), lambda i, j: (0, 1)), ident,
                      kv_cols, kv_cols)
    h4, gate3, up3 = ffn_fwd(h3, vec(ffn1_norm, 1), w_in, w_out, SLOT_FFN1[1])
    q_sw = rms_linear("swa_q", h4, vec(mix_norm, 1), w_sq,
                      pl.BlockSpec((N_CHIPS, None, SQ_ROWS, 512), lambda i, j: (0, SQ_SWA_Q, 0, j)), sq_prep,
                      D_MODEL, 512, rope=(cos, sin), scale=swa_scale)
    o_sw, lse = swa_fwd(q_sw, k_sw, v_sw, sinks)
    h5 = linear_res("swa_out", o_sw, w_sq, SQ_SWA_O, h4)
    h6, gate4, up4 = ffn_fwd(h5, vec(ffn2_norm, 1), w_in, w_out, SLOT_FFN2[1])
    dh6, loss_p, d_final = loss_bwd(h6, final_norm.reshape(1, D_MODEL), tgt)

    slab = {"in": None, "out": None, "sq": None}
    in_shape = (2, N_CHIPS, 4, D_MODEL // 2, FF_CHUNK)
    out_shape = (2, N_CHIPS, 4, FF_ROWS, D_MODEL // 2)
    sq_shape = (2, N_CHIPS, 3, SQ_ROWS, D_MODEL // 2)

    def ffn_bwd(tag, dh, h_in, g, gate, up, slot):
        dg_, du_, act, dhb = ffn_bwd_act(dh, gate, up, w_out, slot)
        pairs = [
            (dg_, w_in, pl.BlockSpec((None, None, D_MODEL, FF_CHUNK), lambda i, j: (j, slot, 0, 0)), ident),
            (du_, w_in, pl.BlockSpec((None, None, D_MODEL, FF_CHUNK), lambda i, j: (2 + j, slot, 0, 0)), ident),
        ]
        dh_in, xn, dnorm = linear_bwd_rms(f"ffn_bwd_in_{tag}", pairs, h_in, g, dh, D_FF // FF_CHUNK, FF_CHUNK)
        blk = (None, 1, None, D_MODEL // 2, FF_CHUNK)
        slab["in"] = mm_tn(f"dw_gate_{tag}", xn, dg_, D_MODEL // 2, FF_CHUNK, blk,
                           lambda k, n: (k, n, slot, 0, 0), in_shape, prev=slab["in"])
        slab["in"] = mm_tn(f"dw_up_{tag}", xn, du_, D_MODEL // 2, FF_CHUNK, blk,
                           lambda k, n: (k, 2 + n, slot, 0, 0), in_shape, prev=slab["in"])
        slab["out"] = mm_tn(f"dw_out_{tag}", act, dhb, FF_CHUNK, D_MODEL // 2,
                            (None, 2, None, FF_ROWS, D_MODEL // 2),
                            lambda k, n: (n, k, slot, 0, 0), out_shape, prev=slab["out"])
        return dh_in, dnorm

    def sq_grad(tag, a, dyb, t):
        slab["sq"] = mm_tn(f"dw_sq_{tag}", a, dyb, D_MODEL, D_MODEL // 2,
                           (None, N_CHIPS, None, SQ_ROWS, D_MODEL // 2),
                           lambda k, n: (n, 0, t, 0, 0), sq_shape, prev=slab["sq"])

    dh5, d_ffn2_1 = ffn_bwd("l1b", dh6, h5, vec(ffn2_norm, 1), gate4, up4, SLOT_FFN2[1])
    do_sw, dh5b = linear_bwd_plain("swa_out_bwd", dh5, w_sq, SQ_SWA_O)
    sq_grad("swa_o", o_sw, dh5b, SQ_SWA_O)
    dq_sw, kv_own, kv_prev, d_sinks = swa_bwd(q_sw, k_sw, v_sw, sinks, do_sw, lse, cos, sin)
    sq_w_spec = pl.BlockSpec((N_CHIPS, None, SQ_ROWS, D_MODEL), lambda i, j: (0, SQ_SWA_Q, 0, 0))
    dh4, hn4, d_mix_1 = linear_bwd_rms("swa_q_bwd", [(dq_sw, w_sq, sq_w_spec, sq_prep)], h4, vec(mix_norm, 1), dh5,
                                       1, D_MODEL)
    sq_grad("swa_q", hn4, dq_sw, SQ_SWA_Q)
    dh3a, d_ffn1_1 = ffn_bwd("l1a", dh4, h3, vec(ffn1_norm, 1), gate3, up3, SLOT_FFN1[1])
    dkv = kv_grad_combine(kv_own, kv_prev, cos, sin)
    kv_w_spec = pl.BlockSpec((D_MODEL, 2 * kv_cols), lambda i, j: (0, 0))
    dh3, xn3, d_kvn = linear_bwd_rms("kv_bwd", [(dkv, w_kv, kv_w_spec, ident)], h3, kvn, dh3a, 1, 2 * kv_cols)
    slab_kv = mm_tn("dw_kv", xn3, dkv, D_MODEL, kv_cols, (None, N_CHIPS, None, SQ_ROWS, kv_cols),
                    lambda k, n: (n, 0, 0, 0, 0), (2, N_CHIPS, 1, SQ_ROWS, kv_cols))
    dh2, d_ffn2_0 = ffn_bwd("l0b", dh3, h2, vec(ffn2_norm, 0), gate2, up2, SLOT_FFN2[0])
    do_sb, dh2b = linear_bwd_plain("sb_out_bwd", dh2, w_sq, SQ_SB_O)
    sq_grad("sb_o", o_sb, dh2b, SQ_SB_O)
    dq_sb, dk_sb, dv_sb = sb_bwd(qkv, do_sb, tot)
    dqkv = jnp.concatenate([dq_sb, dk_sb, dv_sb], axis=1)
    qkv_w_spec = pl.BlockSpec((None, D_MODEL, QKV_COLS), lambda i, j: (j, 0, 0))
    dh1, hn1, d_mix_0 = linear_bwd_rms("sb_qkv_bwd", [(dqkv, w_qkv, qkv_w_spec, ident)], h1, vec(mix_norm, 0), dh2,
                                       N_CHIPS, QKV_COLS)
    slab_qkv = mm_tn("dw_qkv", hn1, dqkv, D_MODEL // 2, QKV_COLS, (None, 1, None, D_MODEL // 2, QKV_COLS),
                     lambda k, n: (k, n, 0, 0, 0), (2, N_CHIPS, 1, D_MODEL // 2, QKV_COLS))
    dx, d_ffn1_0 = ffn_bwd("l0a", dh1, x2, vec(ffn1_norm, 0), gate1, up1, SLOT_FFN1[0])

    slabs = [slab["in"], slab["out"], slab["sq"], slab_qkv, slab_kv]
    names = ["in", "out", "sq", "qkv", "kv"]
    core = lax.axis_index("c").astype(jnp.int32).reshape(1)
    chip = (2 * lax.axis_index("x") + lax.axis_index("y")).astype(jnp.int32).reshape(1)
    theirs = exchange_halves(slabs)
    parts = [add_sibling(f"add_sibling_{nm}", s, t, core) for nm, s, t in zip(names, slabs, theirs)]
    gathered = exchange_chip_partials(parts)
    halves = [sum_chips(f"sum_chips_{nm}", g, p, chip) for nm, g, p in zip(names, gathered, parts)]
    sib_halves = share_reduced_halves(halves)
    g_in, g_out, g_sq, g_qkv, g_kv = zip(halves, sib_halves)

    def upd(name, w, m, v, g_pair, slot0, row_halves):
        shp = w.shape
        w3 = w.reshape((-1,) + shp[-2:])
        outs = adamw_shard(name, w3, m.reshape(w3.shape), v.reshape(w3.shape), g_pair[0], g_pair[1], core,
                           slot0, row_halves)
        return [o.reshape(shp) for o in outs]

    r_ffn1_in = upd("adamw_ffn1_in", ffn1_w_in, m_ffn1_w_in, v_ffn1_w_in, g_in, 0, True)
    r_ffn2_in = upd("adamw_ffn2_in", ffn2_w_in, m_ffn2_w_in, v_ffn2_w_in, g_in, 2, True)
    r_ffn1_out = upd("adamw_ffn1_out", ffn1_w_out, m_ffn1_w_out, v_ffn1_w_out, g_out, 0, False)
    r_ffn2_out = upd("adamw_ffn2_out", ffn2_w_out, m_ffn2_w_out, v_ffn2_w_out, g_out, 2, False)
    r_qkv = upd("adamw_qkv", sb_w_qkv, m_sb_w_qkv, v_sb_w_qkv, g_qkv, 0, True)
    r_sb_o = upd("adamw_sb_o", sb_w_o, m_sb_w_o, v_sb_w_o, g_sq, SQ_SB_O, False)
    r_swa_q = upd("adamw_swa_q", swa_w_q, m_swa_w_q, v_swa_w_q, g_sq, SQ_SWA_Q, False)
    r_swa_o = upd("adamw_swa_o", swa_w_o, m_swa_w_o, v_swa_w_o, g_sq, SQ_SWA_O, False)
    r_kv = upd("adamw_kv", kv_w, m_kv_w, v_kv_w, g_kv, 0, False)

    loss_row = jnp.zeros((1, D_MODEL), F32).at[0, :LANES].set(loss_p[0])
    d_sink_row = d_sinks[0, :SWA_Q_HEADS]
    part = _pack_small(jnp.concatenate([d_ffn1_0, d_ffn1_1], axis=0), jnp.concatenate([d_mix_0, d_mix_1], axis=0),
                       jnp.concatenate([d_ffn2_0, d_ffn2_1], axis=0), d_kvn, d_final, d_sink_row, loss_row)
    zrow = jnp.zeros((1, D_MODEL), F32)
    small = small_allreduce_adamw(
        part,
        _pack_small(ffn1_norm, mix_norm, ffn2_norm, kv_norm, final_norm, swa_sinks, zrow),
        _pack_small(m_ffn1_norm, m_mix_norm, m_ffn2_norm, m_kv_norm, m_final_norm, m_swa_sinks, zrow),
        _pack_small(v_ffn1_norm, v_mix_norm, v_ffn2_norm, v_kv_norm, v_final_norm, v_swa_sinks, zrow))

    def unpack(p):
        return dict(ffn1_norm=p[0:2], mix_norm=p[2:4], ffn2_norm=p[4:6], kv_norm=p[6], final_norm=p[7],
                    swa_sinks=p[8:9, :SWA_Q_HEADS])

    big = dict(ffn1_w_in=r_ffn1_in, ffn1_w_out=r_ffn1_out, ffn2_w_in=r_ffn2_in, ffn2_w_out=r_ffn2_out,
               sb_w_qkv=r_qkv, sb_w_o=r_sb_o, kv_w=r_kv, swa_w_q=r_swa_q, swa_w_o=r_swa_o)
    order = ["ffn1_norm", "ffn1_w_in", "ffn1_w_out", "mix_norm", "ffn2_norm", "ffn2_w_in", "ffn2_w_out",
             "sb_w_qkv", "sb_w_o", "kv_norm", "kv_w", "swa_w_q", "swa_sinks", "swa_w_o", "final_norm"]
    outs = []
    for kind in range(4):
        sm = unpack(small[kind])
        for nm in order:
            outs.append(big[nm][kind] if nm in big else sm[nm])
    loss = small[0][9, 0]
    return (loss, dx.reshape(x.shape), *outs)
```

```python
import functools

import jax
import jax.numpy as jnp
from jax import lax
from jax.experimental import pallas as pl
from jax.experimental.pallas import tpu as pltpu

F32 = jnp.float32
BF16 = jnp.bfloat16
MESH = pl.DeviceIdType.MESH

D_MODEL = 1024
D_FF = 2816
HEAD_DIM = 64
SB_HEADS = 16
SWA_Q_HEADS = 16
SWA_KV_HEADS = 4
WINDOW = 128
ROPE_THETA = 10000.0
RMS_EPS = 1e-6
FFN_RES_SCALE = 0.5
ATTN_SCALE = HEAD_DIM ** -0.5

ADAM_LR = 0.001
ADAM_B1 = 0.9
ADAM_B2 = 0.999
ADAM_EPS = 1e-08
ADAM_WD = 0.01
ADAM_STEP = 10

N_CHIPS = 4
N_DEV = 8
LANES = 128
FF_CHUNK = D_FF // 2
FF_ROWS = D_FF // N_CHIPS
SQ_ROWS = D_MODEL // N_CHIPS
QKV_COLS = 3 * D_MODEL // N_CHIPS
VMEM_LIMIT = 56 * 1024 * 1024
NEG_BIG = -1e30

SLOT_FFN1 = (0, 1)
SLOT_FFN2 = (2, 3)
SQ_SB_O, SQ_SWA_Q, SQ_SWA_O = 0, 1, 2


def _cparams():
    return pltpu.CompilerParams(vmem_limit_bytes=VMEM_LIMIT)


def _dot(a, b):
    return jnp.dot(a, b, preferred_element_type=F32)


def _dot_nt(a, b):
    return lax.dot_general(a, b, (((1,), (1,)), ((), ())), preferred_element_type=F32)


def _dot_tn(a, b):
    return lax.dot_general(a, b, (((0,), (0,)), ((), ())), preferred_element_type=F32)


def _rstd(h):
    return lax.rsqrt(jnp.mean(h * h, axis=-1, keepdims=True) + RMS_EPS)


def _swap32(x):
    n = x.shape[-1]
    lane = lax.broadcasted_iota(jnp.int32, x.shape, x.ndim - 1)
    first = (lane % HEAD_DIM) < (HEAD_DIM // 2)
    return jnp.where(first, pltpu.roll(x, n - HEAD_DIM // 2, x.ndim - 1), pltpu.roll(x, HEAD_DIM // 2, x.ndim - 1))


def _tile_lanes(t, n):
    return t if n == LANES else jnp.tile(t, (1, n // LANES))


FFN_ROWS = 256


def _ffn_w_in_spec(slot):
    return pl.BlockSpec((N_CHIPS, None, D_MODEL, FF_CHUNK), lambda i: (0, slot, 0, 0), pipeline_mode=pl.Buffered(1))


def _ffn_w_out_spec(slot):
    return pl.BlockSpec((N_CHIPS, None, FF_ROWS, D_MODEL), lambda i: (0, slot, 0, 0), pipeline_mode=pl.Buffered(1))


def ffn_fwd(h, g, w_in, w_out, slot):
    T = h.shape[0]
    tm = FFN_ROWS
    nch = D_FF // FF_CHUNK

    def body(h_ref, g_ref, wi_ref, wo_ref, out_ref, gate_ref, up_ref):
        hh = h_ref[...]
        xn = (hh * _rstd(hh) * g_ref[...]).astype(BF16)
        acc = None
        for j in range(nch):
            cols = slice(j * FF_CHUNK, (j + 1) * FF_CHUNK)
            gate = _dot(xn, wi_ref[j])
            up = _dot(xn, wi_ref[nch + j])
            gate_ref[:, cols] = gate.astype(BF16)
            up_ref[:, cols] = up.astype(BF16)
            a = (gate * jax.nn.sigmoid(gate) * up).astype(BF16)
            part = _dot(a, wo_ref[2 * j:2 * j + 2].reshape(FF_CHUNK, D_MODEL))
            acc = part if acc is None else acc + part
        out_ref[...] = hh + FFN_RES_SCALE * acc

    row = pl.BlockSpec((tm, D_MODEL), lambda i: (i, 0))
    ff = pl.BlockSpec((tm, D_FF), lambda i: (i, 0))
    return pl.pallas_call(
        body,
        name=f"ffn_fwd_{slot}",
        grid=(T // tm,),
        in_specs=[row, pl.BlockSpec((1, D_MODEL), lambda i: (0, 0)), _ffn_w_in_spec(slot), _ffn_w_out_spec(slot)],
        out_specs=[row, ff, ff],
        out_shape=[
            jax.ShapeDtypeStruct((T, D_MODEL), F32),
            jax.ShapeDtypeStruct((T, D_FF), BF16),
            jax.ShapeDtypeStruct((T, D_FF), BF16),
        ],
        compiler_params=_cparams(),
    )(h, g, w_in, w_out)


def ffn_bwd(dh, h, g, gate, up, w_in, w_out, slot):
    T = dh.shape[0]
    tm = FFN_ROWS
    nch = D_FF // FF_CHUNK

    def body(dh_ref, h_ref, g_ref, gate_ref, up_ref, wi_ref, wo_ref,
             dhin_ref, xn_ref, dg_ref, du_ref, a_ref, dhb_ref, dnorm_ref):
        @pl.when(pl.program_id(0) == 0)
        def _():
            dnorm_ref[...] = jnp.zeros_like(dnorm_ref)

        dhh = dh_ref[...]
        dhb = (FFN_RES_SCALE * dhh).astype(BF16)
        dhb_ref[...] = dhb
        dxn = None
        for j in range(nch):
            cols = slice(j * FF_CHUNK, (j + 1) * FF_CHUNK)
            da = _dot_nt(dhb, wo_ref[2 * j:2 * j + 2].reshape(FF_CHUNK, D_MODEL))
            gt = gate_ref[:, cols].astype(F32)
            u = up_ref[:, cols].astype(F32)
            s = jax.nn.sigmoid(gt)
            silu = gt * s
            a_ref[:, cols] = (silu * u).astype(BF16)
            dgate = (da * u * (s * (1.0 + gt * (1.0 - s)))).astype(BF16)
            dup = (da * silu).astype(BF16)
            dg_ref[:, cols] = dgate
            du_ref[:, cols] = dup
            part = _dot_nt(dgate, wi_ref[j]) + _dot_nt(dup, wi_ref[nch + j])
            dxn = part if dxn is None else dxn + part
        hh = h_ref[...]
        gg = g_ref[...]
        r = _rstd(hh)
        hr = hh * r
        xn_ref[...] = (hr * gg).astype(BF16)
        dnorm_ref[...] += jnp.sum(dxn * hr, axis=0, keepdims=True)
        gd = gg * dxn
        dhin_ref[...] = dhh + r * (gd - hr * jnp.mean(gd * hr, axis=-1, keepdims=True))

    row = pl.BlockSpec((tm, D_MODEL), lambda i: (i, 0))
    ff = pl.BlockSpec((tm, D_FF), lambda i: (i, 0))
    vec = pl.BlockSpec((1, D_MODEL), lambda i: (0, 0))
    return pl.pallas_call(
        body,
        name=f"ffn_bwd_{slot}",
        grid=(T // tm,),
        in_specs=[row, row, vec, ff, ff, _ffn_w_in_spec(slot), _ffn_w_out_spec(slot)],
        out_specs=[row, row, ff, ff, ff, row, vec],
        out_shape=[
            jax.ShapeDtypeStruct((T, D_MODEL), F32),
            jax.ShapeDtypeStruct((T, D_MODEL), BF16),
            jax.ShapeDtypeStruct((T, D_FF), BF16),
            jax.ShapeDtypeStruct((T, D_FF), BF16),
            jax.ShapeDtypeStruct((T, D_FF), BF16),
            jax.ShapeDtypeStruct((T, D_MODEL), BF16),
            jax.ShapeDtypeStruct((1, D_MODEL), F32),
        ],
        compiler_params=_cparams(),
    )(dh, h, g, gate, up, w_in, w_out)


def rms_linear(name, h, g, w, w_spec, w_prep, n_out, tn, *, rope=None, scale=None):
    T = h.shape[0]
    tm = 512
    extra, extra_specs = [], []
    if rope is not None:
        extra += list(rope)
        extra_specs += [pl.BlockSpec((tm, LANES), lambda i, j: (i, 0))] * 2
    if scale is not None:
        extra.append(scale)
        extra_specs.append(pl.BlockSpec((1, tn), lambda i, j: (0, j)))

    def body(h_ref, g_ref, w_ref, *rest):
        rest = list(rest)
        cos_ref = sin_ref = sc_ref = None
        if rope is not None:
            cos_ref, sin_ref = rest[0], rest[1]
            rest = rest[2:]
        if scale is not None:
            sc_ref = rest[0]
            rest = rest[1:]
        out_ref, xn_s = rest

        @pl.when(pl.program_id(1) == 0)
        def _():
            hh = h_ref[...]
            xn_s[...] = (hh * _rstd(hh) * g_ref[...]).astype(BF16)

        y = _dot(xn_s[...], w_prep(w_ref[...]))
        if rope is not None:
            y = y * _tile_lanes(cos_ref[...], tn) + _swap32(y) * _tile_lanes(sin_ref[...], tn)
        if scale is not None:
            y = y * sc_ref[...]
        out_ref[...] = y.astype(BF16)

    return pl.pallas_call(
        body,
        name=name,
        grid=(T // tm, n_out // tn),
        in_specs=[
            pl.BlockSpec((tm, D_MODEL), lambda i, j: (i, 0)),
            pl.BlockSpec((1, D_MODEL), lambda i, j: (0, 0)),
            w_spec,
        ] + extra_specs,
        out_specs=pl.BlockSpec((tm, tn), lambda i, j: (i, j)),
        out_shape=jax.ShapeDtypeStruct((T, n_out), BF16),
        scratch_shapes=[pltpu.VMEM((tm, D_MODEL), BF16)],
        compiler_params=_cparams(),
    )(h, g, w, *extra)


def linear_res(name, a, w_sq, t, res):
    T = a.shape[0]
    tm = 512

    def body(a_ref, w_ref, res_ref, out_ref):
        out_ref[...] = res_ref[...] + _dot(a_ref[...], w_ref[...].reshape(D_MODEL, D_MODEL))

    row = pl.BlockSpec((tm, D_MODEL), lambda i: (i, 0))
    return pl.pallas_call(
        body,
        name=name,
        grid=(T // tm,),
        in_specs=[row, pl.BlockSpec((N_CHIPS, None, SQ_ROWS, D_MODEL), lambda i: (0, t, 0, 0)), row],
        out_specs=row,
        out_shape=jax.ShapeDtypeStruct((T, D_MODEL), F32),
        compiler_params=_cparams(),
    )(a, w_sq, res)


def linear_bwd_plain(name, dy, w_sq, t):
    T = dy.shape[0]
    tm = 512

    def body(dy_ref, w_ref, da_ref, dyb_ref):
        dyb = dy_ref[...].astype(BF16)
        dyb_ref[...] = dyb
        da_ref[...] = _dot_nt(dyb, w_ref[...].reshape(D_MODEL, D_MODEL)).astype(BF16)

    row = pl.BlockSpec((tm, D_MODEL), lambda i: (i, 0))
    return pl.pallas_call(
        body,
        name=name,
        grid=(T // tm,),
        in_specs=[row, pl.BlockSpec((N_CHIPS, None, SQ_ROWS, D_MODEL), lambda i: (0, t, 0, 0))],
        out_specs=[row, row],
        out_shape=[jax.ShapeDtypeStruct((T, D_MODEL), BF16), jax.ShapeDtypeStruct((T, D_MODEL), BF16)],
        compiler_params=_cparams(),
    )(dy, w_sq)


def linear_bwd_rms(name, pairs, h, g, dres, nch, tn, tm=256):
    T = h.shape[0]
    npair = len(pairs)

    def body(*refs):
        dy_refs = refs[:npair]
        w_refs = refs[npair:2 * npair]
        h_ref, g_ref, dres_ref, dh_ref, xn_ref, dg_ref, acc_s = refs[2 * npair:]
        i = pl.program_id(0)
        j = pl.program_id(1)

        @pl.when(j == 0)
        def _():
            acc_s[...] = jnp.zeros_like(acc_s)

        @pl.when((i == 0) & (j == 0))
        def _():
            dg_ref[...] = jnp.zeros_like(dg_ref)

        part = None
        for p in range(npair):
            d = _dot_nt(dy_refs[p][...], pairs[p][3](w_refs[p][...]))
            part = d if part is None else part + d
        acc_s[...] += part

        @pl.when(j == nch - 1)
        def _():
            dxn = acc_s[...]
            hh = h_ref[...]
            gg = g_ref[...]
            r = _rstd(hh)
            hr = hh * r
            xn_ref[...] = (hr * gg).astype(BF16)
            dg_ref[...] += jnp.sum(dxn * hr, axis=0, keepdims=True)
            gd = gg * dxn
            dh_ref[...] = dres_ref[...] + r * (gd - hr * jnp.mean(gd * hr, axis=-1, keepdims=True))

    row = pl.BlockSpec((tm, D_MODEL), lambda i, j: (i, 0))
    vec = pl.BlockSpec((1, D_MODEL), lambda i, j: (0, 0))
    return pl.pallas_call(
        body,
        name=name,
        grid=(T // tm, nch),
        in_specs=[pl.BlockSpec((tm, tn), lambda i, j: (i, j))] * npair + [p[2] for p in pairs] + [row, vec, row],
        out_specs=[row, row, vec],
        out_shape=[
            jax.ShapeDtypeStruct((T, D_MODEL), F32),
            jax.ShapeDtypeStruct((T, D_MODEL), BF16),
            jax.ShapeDtypeStruct((1, D_MODEL), F32),
        ],
        scratch_shapes=[pltpu.VMEM((tm, D_MODEL), F32)],
        compiler_params=_cparams(),
    )(*[p[0] for p in pairs], *[p[1] for p in pairs], h, g, dres)


def loss_bwd(h, g, tgt):
    T = h.shape[0]
    tm = 512

    def body(h_ref, g_ref, t_ref, dh_ref, loss_ref, dg_ref):
        @pl.when(pl.program_id(0) == 0)
        def _():
            loss_ref[...] = jnp.zeros_like(loss_ref)
            dg_ref[...] = jnp.zeros_like(dg_ref)

        hh = h_ref[...]
        gg = g_ref[...]
        r = _rstd(hh)
        hr = hh * r
        err = hr * gg - t_ref[...]
        loss_ref[...] += 0.5 * jnp.sum(jnp.mean(err * err, axis=-1, keepdims=True), axis=0, keepdims=True)
        dy = err * (1.0 / D_MODEL)
        dg_ref[...] += jnp.sum(dy * hr, axis=0, keepdims=True)
        gd = gg * dy
        dh_ref[...] = r * (gd - hr * jnp.mean(gd * hr, axis=-1, keepdims=True))

    row = pl.BlockSpec((tm, D_MODEL), lambda i: (i, 0))
    vec = pl.BlockSpec((1, D_MODEL), lambda i: (0, 0))
    return pl.pallas_call(
        body,
        name="loss_bwd",
        grid=(T // tm,),
        in_specs=[row, vec, row],
        out_specs=[row, pl.BlockSpec((1, LANES), lambda i: (0, 0)), vec],
        out_shape=[
            jax.ShapeDtypeStruct((T, D_MODEL), F32),
            jax.ShapeDtypeStruct((1, LANES), F32),
            jax.ShapeDtypeStruct((1, D_MODEL), F32),
        ],
        compiler_params=_cparams(),
    )(h, g, tgt)


def mm_tn(name, a, b, tk, tn, out_block, out_index, out_shape, prev=None, tt=1024):
    T = a.shape[0]
    ns, r = out_block[1], out_block[3]
    tt = min(tt, T)
    nt = T // tt

    def body(*refs):
        if prev is None:
            a_ref, b_ref, out_ref = refs
        else:
            a_ref, b_ref, _, out_ref = refs
        t = pl.program_id(2)
        res = _dot_tn(a_ref[...], b_ref[...])

        @pl.when(t == 0)
        def _():
            for u in range(ns):
                out_ref[u] = res[u * r:(u + 1) * r]

        @pl.when(t > 0)
        def _():
            for u in range(ns):
                out_ref[u] += res[u * r:(u + 1) * r]

    in_specs = [
        pl.BlockSpec((tt, tk), lambda k, n, t: (t, k)),
        pl.BlockSpec((tt, tn), lambda k, n, t: (t, n)),
    ]
    args = [a, b]
    aliases = {}
    if prev is not None:
        in_specs.append(pl.BlockSpec(memory_space=pl.ANY))
        args.append(prev)
        aliases = {2: 0}
    return pl.pallas_call(
        body,
        name=name,
        grid=(a.shape[1] // tk, b.shape[1] // tn, nt),
        in_specs=in_specs,
        out_specs=pl.BlockSpec(out_block, lambda k, n, t: out_index(k, n)),
        out_shape=jax.ShapeDtypeStruct(out_shape, F32),
        input_output_aliases=aliases,
        compiler_params=_cparams(),
    )(*args)


SB_BLOCK = 256
SB_CHUNK = 128


LOG2E = 1.4426950408889634


def _softplus2(z2):
    sign = jnp.uint32(0x80000000)
    neg_abs = lax.bitcast_convert_type(lax.bitcast_convert_type(z2, jnp.uint32) | sign, F32)
    return jnp.log2(1.0 + jnp.exp2(neg_abs)) + jnp.maximum(z2, 0.0)


def _split_bf16(x):
    hi = x.astype(BF16)
    lo = (x - hi.astype(F32)).astype(BF16)
    return jnp.concatenate([hi, lo], axis=1)


def _twice(x):
    return jnp.concatenate([x, x], axis=1)


def sb_fwd(qkv):
    T = qkv.shape[0]
    tq = SB_BLOCK
    npair = SB_HEADS // 2

    def body(q_ref, k_ref, v_ref, o_ref, tot_ref, acc_s, c_s, z_s, w_s):
        i = pl.program_id(1)
        q = q_ref[...]
        lane = lax.broadcasted_iota(jnp.int32, (tq, LANES), 1)
        first = lane < HEAD_DIM
        zero = jnp.zeros_like(q)
        q_heads = (jnp.where(first, q, zero), jnp.where(first, zero, q))
        row = lax.broadcasted_iota(jnp.int32, (tq, tq), 0)
        col = lax.broadcasted_iota(jnp.int32, (tq, tq), 1)
        strict = col < row
        from_s = (row >= col).astype(BF16)
        from_s2 = jnp.concatenate([from_s, from_s], axis=0)
        acc_s[...] = jnp.zeros_like(acc_s)
        c_s[...] = jnp.zeros_like(c_s)

        def rows(j):
            return pl.ds(pl.multiple_of(j * tq, tq), tq)

        def logits(j):
            kb = k_ref[rows(j), :]
            for hd in range(2):
                z_s[hd] = _dot_nt(q_heads[hd], kb) * LOG2E

        def flush(j):
            vb = v_ref[rows(j), :]
            for hd in range(2):
                acc_s[hd] += _dot(w_s[hd], vb)

        def block(j, diag):
            if not diag:
                flush(j + 1)
            chunks = [(hd, slice(r0, r0 + SB_CHUNK)) for hd in range(2) for r0 in range(0, tq, SB_CHUNK)]
            es, sums = [], []
            for hd, rs in chunks:
                z2 = z_s[hd, rs, :]
                if diag:
                    z2 = jnp.where(strict[rs, :], z2, NEG_BIG)
                sp = _softplus2(z2)
                c = c_s[hd, rs, :]
                es.append(z2 + _twice(c))
                c_s[hd, rs, :] = c - jnp.sum(sp, axis=1, keepdims=True)
                sums.append(_dot(_split_bf16(sp), from_s2))
            logits(jnp.maximum(j - 1, 0))
            for (hd, rs), e, s in zip(chunks, es, sums):
                w_s[hd, rs, :] = jnp.exp2(e - s).astype(BF16)

        logits(i)
        block(i, True)

        @pl.loop(0, i)
        def _(jj):
            block(i - 1 - jj, False)

        flush(0)
        o_ref[...] = jnp.where(first, acc_s[0], acc_s[1]).astype(BF16)
        tot_ref[...] = jnp.where(first, c_s[0], c_s[1])

    return pl.pallas_call(
        body,
        name="sb_fwd",
        grid=(npair, T // tq),
        in_specs=[
            pl.BlockSpec((tq, LANES), lambda p, i: (i, p)),
            pl.BlockSpec((T, LANES), lambda p, i: (0, npair + p)),
            pl.BlockSpec((T, LANES), lambda p, i: (0, 2 * npair + p)),
        ],
        out_specs=[pl.BlockSpec((tq, LANES), lambda p, i: (i, p))] * 2,
        out_shape=[jax.ShapeDtypeStruct((T, D_MODEL), BF16), jax.ShapeDtypeStruct((T, D_MODEL), F32)],
        scratch_shapes=[
            pltpu.VMEM((2, tq, LANES), F32), pltpu.VMEM((2, tq, LANES), F32),
            pltpu.VMEM((2, tq, tq), F32), pltpu.VMEM((2, tq, tq), BF16),
        ],
        compiler_params=_cparams(),
    )(qkv, qkv, qkv)


def sb_bwd(qkv, do, tot):
    T = qkv.shape[0]
    tq = SB_BLOCK
    npair = SB_HEADS // 2
    nq = T // tq

    def body(q_ref, k_ref, v_ref, do_ref, tot_ref, dq_ref, dk_ref, dv_ref,
             dk_s, dv_s, dq_s, rest_s, cg_s, z_s, da_s, dz_s, a_s):
        i = pl.program_id(1)

        @pl.when(i == 0)
        def _():
            dk_s[...] = jnp.zeros_like(dk_s)
            dv_s[...] = jnp.zeros_like(dv_s)

        q = q_ref[...]
        do_ = do_ref[...]
        tot_ = tot_ref[...]
        lane = lax.broadcasted_iota(jnp.int32, (tq, LANES), 1)
        first = lane < HEAD_DIM
        zero = jnp.zeros_like(q)
        q_heads = (jnp.where(first, q, zero), jnp.where(first, zero, q))
        do_heads = (jnp.where(first, do_, zero), jnp.where(first, zero, do_))
        row = lax.broadcasted_iota(jnp.int32, (tq, tq), 0)
        col = lax.broadcasted_iota(jnp.int32, (tq, tq), 1)
        strict = col < row
        before = (row < col).astype(BF16)
        before2 = jnp.concatenate([before, before], axis=0)
        rest_s[0] = jnp.broadcast_to(tot_[:, 0:1], (tq, LANES))
        rest_s[1] = jnp.broadcast_to(tot_[:, HEAD_DIM:HEAD_DIM + 1], (tq, LANES))
        cg_s[...] = jnp.zeros_like(cg_s)
        dq_s[...] = jnp.zeros_like(dq_s)
        dz_s[...] = jnp.zeros_like(dz_s)
        a_s[...] = jnp.zeros_like(a_s)

        def rows(j):
            return pl.ds(pl.multiple_of(j * tq, tq), tq)

        def logits(j):
            kb = k_ref[rows(j), :]
            vb = v_ref[rows(j), :]
            for hd in range(2):
                z_s[hd] = _dot_nt(q_heads[hd], kb) * LOG2E
                da_s[hd] = _dot_nt(do_heads[hd], vb)

        def flush(j):
            kb = k_ref[rows(j), :]
            for hd in range(2):
                dq_s[hd] += _dot(dz_s[hd], kb)
            dk_s[rows(j), :] += _dot_tn(dz_s[0], q_heads[0]) + _dot_tn(dz_s[1], q_heads[1])
            dv_s[rows(j), :] += _dot_tn(a_s[0], do_heads[0]) + _dot_tn(a_s[1], do_heads[1])

        def block(j, diag):
            flush(jnp.maximum(j - 1, 0))
            chunks = [(hd, slice(r0, r0 + SB_CHUNK)) for hd in range(2) for r0 in range(0, tq, SB_CHUNK)]
            stage1 = []
            for hd, rs in chunks:
                z2 = z_s[hd, rs, :]
                if diag:
                    z2 = jnp.where(strict[rs, :], z2, NEG_BIG)
                sp = _softplus2(z2)
                rest = rest_s[hd, rs, :]
                e = z2 + _twice(rest)
                rest_s[hd, rs, :] = rest + jnp.sum(sp, axis=1, keepdims=True)
                stage1.append((e, z2 - sp, _dot(_split_bf16(sp), before2)))
            stage2 = []
            for (hd, rs), (e, log2_beta, done) in zip(chunks, stage1):
                a = jnp.exp2(e + done)
                g = a * da_s[hd, rs, :]
                cg = cg_s[hd, rs, :]
                a_s[hd, rs, :] = a.astype(BF16)
                cg_s[hd, rs, :] = cg + jnp.sum(g, axis=1, keepdims=True)
                stage2.append((g, g + _twice(cg), log2_beta, _dot(g.astype(BF16), before)))
            logits(jnp.minimum(j + 1, i))
            for (hd, rs), (g, g_from, log2_beta, g_before) in zip(chunks, stage2):
                dz_s[hd, rs, :] = (g - jnp.exp2(log2_beta) * (g_from + g_before)).astype(BF16)

        logits(0)

        @pl.loop(0, i)
        def _(j):
            block(j, False)

        block(i, True)
        flush(i)
        dq_ref[...] = (jnp.where(first, dq_s[0], dq_s[1]) * ATTN_SCALE).astype(BF16)

        @pl.when(i == nq - 1)
        def _():
            dk_ref[...] = dk_s[...].astype(BF16)
            dv_ref[...] = dv_s[...].astype(BF16)

    qblk = pl.BlockSpec((tq, LANES), lambda p, i: (i, p))
    full = pl.BlockSpec((T, LANES), lambda p, i: (0, p))
    return pl.pallas_call(
        body,
        name="sb_bwd",
        grid=(npair, nq),
        in_specs=[
            qblk,
            pl.BlockSpec((T, LANES), lambda p, i: (0, npair + p)),
            pl.BlockSpec((T, LANES), lambda p, i: (0, 2 * npair + p)),
            qblk, qblk,
        ],
        out_specs=[qblk, full, full],
        out_shape=[jax.ShapeDtypeStruct((T, D_MODEL), BF16)] * 3,
        scratch_shapes=[
            pltpu.VMEM((T, LANES), F32), pltpu.VMEM((T, LANES), F32),
            pltpu.VMEM((2, tq, LANES), F32), pltpu.VMEM((2, tq, LANES), F32), pltpu.VMEM((2, tq, LANES), F32),
            pltpu.VMEM((2, tq, tq), F32), pltpu.VMEM((2, tq, tq), F32),
            pltpu.VMEM((2, tq, tq), BF16), pltpu.VMEM((2, tq, tq), BF16),
        ],
        compiler_params=_cparams(),
    )(qkv, qkv, qkv, do, tot)


def _swa_valid(n):
    qi = lax.broadcasted_iota(jnp.int32, (WINDOW, 2 * WINDOW), 0)
    ki = lax.broadcasted_iota(jnp.int32, (WINDOW, 2 * WINDOW), 1)
    diff = qi + WINDOW - ki
    return (diff >= 0) & (diff < WINDOW) & ((n > 0) | (ki >= WINDOW))


def _to_half(x, first, src, dst):
    keep = first if src == 0 else jnp.logical_not(first)
    x = jnp.where(keep, x, jnp.zeros_like(x))
    if src != dst:
        x = pltpu.roll(x.astype(F32), HEAD_DIM, 1).astype(BF16)
    return x


def _kv_band(prev_ref, cur_ref, pb):
    cols = slice(pb * LANES, (pb + 1) * LANES)
    return jnp.concatenate([prev_ref[:, cols], cur_ref[:, cols]], axis=0)


def _swa_specs(T):
    nb = T // WINDOW
    kv_w = SWA_KV_HEADS * HEAD_DIM
    qrow = pl.BlockSpec((WINDOW, D_MODEL), lambda n: (n, 0))
    cur = pl.BlockSpec((WINDOW, kv_w), lambda n: (n, 0))
    prev = pl.BlockSpec((WINDOW, kv_w), lambda n: (jnp.maximum(n - 1, 0), 0))
    smem = pl.BlockSpec(memory_space=pltpu.SMEM)
    return nb, qrow, cur, prev, smem


def swa_fwd(q, k, v, sinks):
    T = q.shape[0]
    nb, qrow, cur, prev, smem = _swa_specs(T)

    def body(sink_ref, q_ref, kc_ref, kp_ref, vc_ref, vp_ref, o_ref, lse_ref):
        n = pl.program_id(0)
        lane = lax.broadcasted_iota(jnp.int32, (WINDOW, LANES), 1)
        first = lane < HEAD_DIM
        valid = _swa_valid(n)
        lse_acc = jnp.zeros((WINDOW, LANES), F32)
        for pb in range(SWA_KV_HEADS // 2):
            k2 = _kv_band(kp_ref, kc_ref, pb)
            v2 = _kv_band(vp_ref, vc_ref, pb)
            for b in range(2):
                kvh = 2 * pb + b
                for qq in range(2):
                    cols = slice((2 * kvh + qq) * LANES, (2 * kvh + qq + 1) * LANES)
                    qp = q_ref[:, cols]
                    outs = []
                    for a in range(2):
                        head = 4 * kvh + 2 * qq + a
                        qh = _to_half(qp, first, a, b)
                        s = jnp.where(valid, _dot_nt(qh, k2), NEG_BIG)
                        sink = sink_ref[head]
                        m = jnp.maximum(jnp.max(s, axis=1, keepdims=True), sink)
                        p = jnp.exp(s - m)
                        den = jnp.sum(p, axis=1, keepdims=True) + jnp.exp(sink - m)
                        o = _dot((p / den).astype(BF16), v2)
                        if a != b:
                            o = pltpu.roll(o, HEAD_DIM, 1)
                        outs.append(o)
                        lse_acc = jnp.where(lane == head, m + jnp.log(den), lse_acc)
                    o_ref[:, cols] = jnp.where(first, outs[0], outs[1]).astype(BF16)
        lse_ref[...] = lse_acc

    return pl.pallas_call(
        body,
        name="swa_fwd",
        grid=(nb,),
        in_specs=[smem, qrow, cur, prev, cur, prev],
        out_specs=[qrow, pl.BlockSpec((WINDOW, LANES), lambda n: (n, 0))],
        out_shape=[jax.ShapeDtypeStruct((T, D_MODEL), BF16), jax.ShapeDtypeStruct((T, LANES), F32)],
        compiler_params=_cparams(),
    )(sinks, q, k, k, v, v)


def swa_bwd(q, k, v, sinks, do, lse, cos, sin):
    T = q.shape[0]
    nb, qrow, cur, prev, smem = _swa_specs(T)
    kv_w = SWA_KV_HEADS * HEAD_DIM

    def body(sink_ref, q_ref, kc_ref, kp_ref, vc_ref, vp_ref, do_ref, lse_ref, cos_ref, sin_ref,
             dq_ref, own_ref, prv_ref, dsink_ref):
        n = pl.program_id(0)

        @pl.when(n == 0)
        def _():
            dsink_ref[...] = jnp.zeros_like(dsink_ref)

        lane = lax.broadcasted_iota(jnp.int32, (WINDOW, LANES), 1)
        lane1 = lax.broadcasted_iota(jnp.int32, (1, LANES), 1)
        first = lane < HEAD_DIM
        valid = _swa_valid(n)
        cos_ = cos_ref[...]
        sin_ = sin_ref[...]
        dsink = jnp.zeros((1, LANES), F32)
        for pb in range(SWA_KV_HEADS // 2):
            k2 = _kv_band(kp_ref, kc_ref, pb)
            v2 = _kv_band(vp_ref, vc_ref, pb)
            dk2 = jnp.zeros((2 * WINDOW, LANES), F32)
            dv2 = jnp.zeros((2 * WINDOW, LANES), F32)
            for b in range(2):
                kvh = 2 * pb + b
                for qq in range(2):
                    cols = slice((2 * kvh + qq) * LANES, (2 * kvh + qq + 1) * LANES)
                    qp = q_ref[:, cols]
                    dop = do_ref[:, cols]
                    dqs = []
                    for a in range(2):
                        head = 4 * kvh + 2 * qq + a
                        qh = _to_half(qp, first, a, b)
                        doh = _to_half(dop, first, a, b)
                        s = jnp.where(valid, _dot_nt(qh, k2), NEG_BIG)
                        lse_h = lse_ref[:, head:head + 1]
                        p = jnp.exp(s - lse_h)
                        dp = _dot_nt(doh, v2)
                        delta = jnp.sum(p * dp, axis=1, keepdims=True)
                        ds = (p * (dp - delta)).astype(BF16)
                        p_sink = jnp.exp(sink_ref[head] - lse_h)
                        dsink = dsink + jnp.where(lane1 == head, -jnp.sum(p_sink * delta, axis=0, keepdims=True), 0.0)
                        dq = _dot(ds, k2)
                        if a != b:
                            dq = pltpu.roll(dq, HEAD_DIM, 1)
                        dqs.append(dq)
                        dk2 = dk2 + _dot_tn(ds, qh)
                        dv2 = dv2 + _dot_tn(p.astype(BF16), doh)
                    dqp = jnp.where(first, dqs[0], dqs[1])
                    dq_ref[:, cols] = ((dqp * cos_ + _swap32(dqp * sin_)) * ATTN_SCALE).astype(BF16)
            kcols = slice(pb * LANES, (pb + 1) * LANES)
            vcols = slice(kv_w + pb * LANES, kv_w + (pb + 1) * LANES)
            prv_ref[:, kcols] = dk2[:WINDOW]
            own_ref[:, kcols] = dk2[WINDOW:]
            prv_ref[:, vcols] = dv2[:WINDOW]
            own_ref[:, vcols] = dv2[WINDOW:]
        dsink_ref[...] += dsink

    tab = pl.BlockSpec((WINDOW, LANES), lambda n: (n, 0))
    kvrow = pl.BlockSpec((WINDOW, 2 * kv_w), lambda n: (n, 0))
    return pl.pallas_call(
        body,
        name="swa_bwd",
        grid=(nb,),
        in_specs=[smem, qrow, cur, prev, cur, prev, qrow, tab, tab, tab],
        out_specs=[qrow, kvrow, kvrow, pl.BlockSpec((1, LANES), lambda n: (0, 0))],
        out_shape=[
            jax.ShapeDtypeStruct((T, D_MODEL), BF16),
            jax.ShapeDtypeStruct((T, 2 * kv_w), F32),
            jax.ShapeDtypeStruct((T, 2 * kv_w), F32),
            jax.ShapeDtypeStruct((1, LANES), F32),
        ],
        compiler_params=_cparams(),
    )(sinks, q, k, k, v, v, do, lse, cos, sin)


def kv_grad_combine(own, prv, cos, sin):
    T = own.shape[0]
    nb = T // WINDOW
    kv_w = SWA_KV_HEADS * HEAD_DIM

    def body(own_ref, nxt_ref, cos_ref, sin_ref, out_ref):
        n = pl.program_id(0)
        nxt = jnp.where(n + 1 < nb, nxt_ref[...], 0.0)
        tot = own_ref[...] + nxt
        dk = tot[:, :kv_w]
        c = _tile_lanes(cos_ref[...], kv_w)
        s = _tile_lanes(sin_ref[...], kv_w)
        out_ref[:, :kv_w] = (dk * c + _swap32(dk * s)).astype(BF16)
        out_ref[:, kv_w:] = tot[:, kv_w:].astype(BF16)

    tab = pl.BlockSpec((WINDOW, LANES), lambda n: (n, 0))
    kvrow = pl.BlockSpec((WINDOW, 2 * kv_w), lambda n: (n, 0))
    return pl.pallas_call(
        body,
        name="kv_grad_combine",
        grid=(nb,),
        in_specs=[kvrow, pl.BlockSpec((WINDOW, 2 * kv_w), lambda n: (jnp.minimum(n + 1, nb - 1), 0)), tab, tab],
        out_specs=kvrow,
        out_shape=jax.ShapeDtypeStruct((T, 2 * kv_w), BF16),
        compiler_params=_cparams(),
    )(own, prv, cos, sin)


ANY = pl.BlockSpec(memory_space=pl.ANY)


def _place():
    x, y, c = lax.axis_index("x"), lax.axis_index("y"), lax.axis_index("c")
    other_chips = [(1 - x, y), (x, 1 - y), (1 - x, 1 - y)]
    return x, y, c, 2 * x + y, other_chips


def all_gather_weights(shards):
    n = len(shards)

    def body(*refs):
        ins, outs = refs[:n], refs[n:2 * n]
        send_sems, recv_sems, local_sems = refs[2 * n:]
        _, _, c, me, chips = _place()
        copies = []
        for t in range(n):
            cp = pltpu.make_async_copy(ins[t], outs[t].at[me], local_sems.at[t])
            cp.start()
            copies.append(cp)
            for jdx, (px, py) in enumerate(chips):
                cp = pltpu.make_async_remote_copy(
                    src_ref=ins[t], dst_ref=outs[t].at[me], send_sem=send_sems.at[t, jdx],
                    recv_sem=recv_sems.at[t, jdx], device_id=(px, py, c), device_id_type=MESH)
                cp.start()
                copies.append(cp)
        for cp in copies:
            cp.wait()

    return pl.pallas_call(
        body,
        name="all_gather_weights",
        in_specs=[ANY] * n,
        out_specs=[ANY] * n,
        out_shape=[jax.ShapeDtypeStruct((N_CHIPS,) + s.shape, s.dtype) for s in shards],
        scratch_shapes=[
            pltpu.SemaphoreType.DMA((n, 3)), pltpu.SemaphoreType.DMA((n, 3)), pltpu.SemaphoreType.DMA((n,)),
        ],
    )(*shards)


def exchange_halves(slabs):
    n = len(slabs)

    def body(*refs):
        ins, theirs = refs[:n], refs[n:2 * n]
        send_sems, recv_sems = refs[2 * n:]
        x, y, c, _, _ = _place()
        copies = []
        for t in range(n):
            cp = pltpu.make_async_remote_copy(
                src_ref=ins[t].at[1 - c], dst_ref=theirs[t], send_sem=send_sems.at[t],
                recv_sem=recv_sems.at[t], device_id=(x, y, 1 - c), device_id_type=MESH)
            cp.start()
            copies.append(cp)
        for cp in copies:
            cp.wait()

    return pl.pallas_call(
        body,
        name="exchange_halves",
        in_specs=[ANY] * n,
        out_specs=[ANY] * n,
        out_shape=[jax.ShapeDtypeStruct(s.shape[1:], s.dtype) for s in slabs],
        scratch_shapes=[pltpu.SemaphoreType.DMA((n,)), pltpu.SemaphoreType.DMA((n,))],
    )(*slabs)


def exchange_chip_partials(parts):
    n = len(parts)

    def body(*refs):
        ins, outs = refs[:n], refs[n:2 * n]
        send_sems, recv_sems = refs[2 * n:]
        _, _, c, me, chips = _place()
        copies = []
        for t in range(n):
            for jdx, (px, py) in enumerate(chips):
                cp = pltpu.make_async_remote_copy(
                    src_ref=ins[t].at[2 * px + py], dst_ref=outs[t].at[me], send_sem=send_sems.at[t, jdx],
                    recv_sem=recv_sems.at[t, jdx], device_id=(px, py, c), device_id_type=MESH)
                cp.start()
                copies.append(cp)
        for cp in copies:
            cp.wait()

    return pl.pallas_call(
        body,
        name="exchange_chip_partials",
        in_specs=[ANY] * n,
        out_specs=[ANY] * n,
        out_shape=[jax.ShapeDtypeStruct(p.shape, p.dtype) for p in parts],
        scratch_shapes=[pltpu.SemaphoreType.DMA((n, 3)), pltpu.SemaphoreType.DMA((n, 3))],
    )(*parts)


def share_reduced_halves(halves):
    n = len(halves)

    def body(*refs):
        ins, outs = refs[:n], refs[n:2 * n]
        send_sems, recv_sems = refs[2 * n:]
        x, y, c, _, _ = _place()
        copies = []
        for t in range(n):
            cp = pltpu.make_async_remote_copy(
                src_ref=ins[t], dst_ref=outs[t], send_sem=send_sems.at[t],
                recv_sem=recv_sems.at[t], device_id=(x, y, 1 - c), device_id_type=MESH)
            cp.start()
            copies.append(cp)
        for cp in copies:
            cp.wait()

    return pl.pallas_call(
        body,
        name="share_reduced_halves",
        in_specs=[ANY] * n,
        out_specs=[ANY] * n,
        out_shape=[jax.ShapeDtypeStruct(h.shape, h.dtype) for h in halves],
        scratch_shapes=[pltpu.SemaphoreType.DMA((n,)), pltpu.SemaphoreType.DMA((n,))],
    )(*halves)


def _row_tile(r, c):
    tr = r
    while tr * c * 4 > (3 << 19) and tr % 16 == 0:
        tr //= 2
    return tr


def add_sibling(name, slab, theirs, core):
    _, ns, slots, r, c = slab.shape
    tr = _row_tile(r, c)

    def body(core_ref, a_ref, b_ref, o_ref):
        o_ref[...] = (a_ref[...] + b_ref[...]).astype(BF16)

    blk = pl.BlockSpec((None, None, tr, c), lambda s, l, i, core_ref: (s, l, i, 0))
    return pl.pallas_call(
        body, name=name,
        grid_spec=pltpu.PrefetchScalarGridSpec(
            num_scalar_prefetch=1, grid=(ns, slots, r // tr),
            in_specs=[pl.BlockSpec((None, None, None, tr, c), lambda s, l, i, core_ref: (core_ref[0], s, l, i, 0)), blk],
            out_specs=blk),
        out_shape=jax.ShapeDtypeStruct(theirs.shape, BF16), compiler_params=_cparams(),
    )(core, slab, theirs)


def sum_chips(name, recv, own, chip):
    _, slots, r, c = recv.shape
    tr = _row_tile(r, c)

    def body(chip_ref, r0, r1, r2, r3, own_ref, o_ref):
        me = chip_ref[0]
        mine = own_ref[...]
        terms = [jnp.where(me == s, mine, rr[...]).astype(F32) for s, rr in enumerate((r0, r1, r2, r3))]
        o_ref[...] = ((terms[0] + terms[1]) + terms[2]) + terms[3]

    def src(s):
        return pl.BlockSpec((None, None, tr, c),
                            lambda l, i, chip_ref: (jnp.where(chip_ref[0] == s, (s + 1) % N_CHIPS, s), l, i, 0))

    return pl.pallas_call(
        body, name=name,
        grid_spec=pltpu.PrefetchScalarGridSpec(
            num_scalar_prefetch=1, grid=(slots, r // tr),
            in_specs=[src(0), src(1), src(2), src(3),
                      pl.BlockSpec((None, None, tr, c), lambda l, i, chip_ref: (chip_ref[0], l, i, 0))],
            out_specs=pl.BlockSpec((None, tr, c), lambda l, i, chip_ref: (l, i, 0))),
        out_shape=jax.ShapeDtypeStruct((slots, r, c), F32), compiler_params=_cparams(),
    )(chip, recv, recv, recv, recv, own)


def _adamw_math(w, g, m, v):
    m = ADAM_B1 * m + (1.0 - ADAM_B1) * g
    v = ADAM_B2 * v + (1.0 - ADAM_B2) * (g * g)
    m_hat = m / (1.0 - ADAM_B1 ** ADAM_STEP)
    v_hat = v / (1.0 - ADAM_B2 ** ADAM_STEP)
    delta = -ADAM_LR * (m_hat / (jnp.sqrt(v_hat) + ADAM_EPS) + ADAM_WD * w)
    return delta, m, v


def adamw_shard(name, w, m, v, g_own, g_sib, core, slot0, row_halves):
    n = w.shape[0]
    _, r, c = g_own.shape
    tr = _row_tile(r, c)
    nr = r // tr

    def body(core_ref, w_ref, m_ref, v_ref, own_ref, sib_ref, go_ref, d_ref, mo_ref, vo_ref):
        g = jnp.where(pl.program_id(1) == core_ref[0], own_ref[...], sib_ref[...])
        delta, mm, vv = _adamw_math(w_ref[...], g, m_ref[...], v_ref[...])
        go_ref[...] = g
        d_ref[...] = delta
        mo_ref[...] = mm
        vo_ref[...] = vv

    if row_halves:
        wspec = pl.BlockSpec((None, tr, c), lambda l, h, i, core_ref: (l, h * nr + i, 0))
    else:
        wspec = pl.BlockSpec((None, tr, c), lambda l, h, i, core_ref: (l, i, h))
    gspec = pl.BlockSpec((None, tr, c), lambda l, h, i, core_ref: (slot0 + l, i, 0))
    shp = jax.ShapeDtypeStruct(w.shape, F32)
    return pl.pallas_call(
        body, name=name,
        grid_spec=pltpu.PrefetchScalarGridSpec(
            num_scalar_prefetch=1, grid=(n, 2, nr),
            in_specs=[wspec, wspec, wspec, gspec, gspec], out_specs=[wspec] * 4),
        out_shape=[shp] * 4, compiler_params=_cparams(),
    )(core, w, m, v, g_own, g_sib)


SMALL_ROWS = 16


def small_allreduce_adamw(part, w, m, v):
    def body(p_ref, w_ref, m_ref, v_ref, g_ref, d_ref, mo_ref, vo_ref, buf, send_sems, recv_sems):
        x, y, c, _, _ = _place()
        me = 4 * x + 2 * y + c
        buf[me] = p_ref[...]
        copies = []
        for k in range(1, N_DEV):
            kx, ky, kc = (k >> 2) & 1, (k >> 1) & 1, k & 1
            peer = (x ^ kx, y ^ ky, c ^ kc)
            cp = pltpu.make_async_remote_copy(
                src_ref=p_ref, dst_ref=buf.at[me], send_sem=send_sems.at[k - 1],
                recv_sem=recv_sems.at[k - 1], device_id=peer, device_id_type=MESH)
            cp.start()
            copies.append(cp)
        for cp in copies:
            cp.wait()
        g = buf[0]
        for dev in range(1, N_DEV):
            g = g + buf[dev]
        delta, mm, vv = _adamw_math(w_ref[...], g, m_ref[...], v_ref[...])
        g_ref[...] = g
        d_ref[...] = delta
        mo_ref[...] = mm
        vo_ref[...] = vv

    vm = pl.BlockSpec(memory_space=pltpu.VMEM)
    shp = jax.ShapeDtypeStruct(part.shape, F32)
    return pl.pallas_call(
        body, name="small_allreduce_adamw",
        in_specs=[vm] * 4, out_specs=[vm] * 4, out_shape=[shp] * 4,
        scratch_shapes=[
            pltpu.VMEM((N_DEV,) + part.shape, F32),
            pltpu.SemaphoreType.DMA((N_DEV - 1,)), pltpu.SemaphoreType.DMA((N_DEV - 1,)),
        ],
    )(part, w, m, v)


def _rope_tables(T):
    half = HEAD_DIM // 2
    inv_freq = ROPE_THETA ** (-jnp.arange(half, dtype=F32) / half)
    ang = jnp.arange(T).astype(F32)[:, None] * inv_freq[None, :]
    cos = jnp.tile(jnp.cos(ang), (1, LANES // half))
    sin = jnp.tile(jnp.sin(ang), (1, LANES // half))
    lane = jnp.arange(LANES)
    sign = jnp.where((lane % HEAD_DIM) < half, -1.0, 1.0).astype(F32)
    return cos, sin * sign[None, :]


def _pack_small(ffn1, mix, ffn2, kvn, fin, sinks, loss_row):
    sink_row = jnp.pad(sinks.reshape(1, SWA_Q_HEADS), ((0, 0), (0, D_MODEL - SWA_Q_HEADS)))
    rows = jnp.concatenate([ffn1, mix, ffn2, kvn.reshape(1, -1), fin.reshape(1, -1), sink_row, loss_row], axis=0)
    return jnp.concatenate([rows, jnp.zeros((SMALL_ROWS - rows.shape[0], D_MODEL), F32)], axis=0)


def kernel(x, ffn1_norm, ffn1_w_in, ffn1_w_out, mix_norm, ffn2_norm, ffn2_w_in, ffn2_w_out, sb_w_qkv, sb_w_o, kv_norm, kv_w, swa_w_q, swa_sinks, swa_w_o, final_norm, loss_target, m_ffn1_norm, m_ffn1_w_in, m_ffn1_w_out, m_mix_norm, m_ffn2_norm, m_ffn2_w_in, m_ffn2_w_out, m_sb_w_qkv, m_sb_w_o, m_kv_norm, m_kv_w, m_swa_w_q, m_swa_sinks, m_swa_w_o, m_final_norm, v_ffn1_norm, v_ffn1_w_in, v_ffn1_w_out, v_mix_norm, v_ffn2_norm, v_ffn2_w_in, v_ffn2_w_out, v_sb_w_qkv, v_sb_w_o, v_kv_norm, v_kv_w, v_swa_w_q, v_swa_sinks, v_swa_w_o, v_final_norm):
    T = x.shape[1]
    kv_cols = SWA_KV_HEADS * HEAD_DIM
    x2 = x.reshape(T, D_MODEL)
    tgt = loss_target.reshape(T, D_MODEL)
    cos, sin = _rope_tables(T)

    w_in_l = jnp.concatenate([ffn1_w_in, ffn2_w_in], axis=0).astype(BF16)
    w_out_l = jnp.concatenate([ffn1_w_out, ffn2_w_out], axis=0).astype(BF16)
    sq_l = jnp.concatenate([sb_w_o, swa_w_q, swa_w_o], axis=0).astype(BF16)
    qkv_l = sb_w_qkv[0].astype(BF16)
    kvw_l = kv_w.astype(BF16)
    w_in, w_out, w_sq, w_qkv, w_kv = all_gather_weights([w_in_l, w_out_l, sq_l, qkv_l, kvw_l])
    w_kv = w_kv.reshape(D_MODEL, 2 * kv_cols)

    def vec(a, i):
        return a[i].reshape(1, D_MODEL)

    ident = lambda w: w
    sq_prep = lambda w: w.reshape(D_MODEL, w.shape[-1])
    qscale = jnp.concatenate([jnp.full((1, D_MODEL), ATTN_SCALE, F32), jnp.ones((1, 2 * D_MODEL), F32)], axis=1)
    swa_scale = jnp.full((1, D_MODEL), ATTN_SCALE, F32)
    sinks = swa_sinks.reshape(SWA_Q_HEADS)

    h1, gate1, up1 = ffn_fwd(x2, vec(ffn1_norm, 0), w_in, w_out, SLOT_FFN1[0])
    qkv = rms_linear("sb_qkv", h1, vec(mix_norm, 0), w_qkv,
                     pl.BlockSpec((None, D_MODEL, QKV_COLS), lambda i, j: (j, 0, 0)), ident,
                     3 * D_MODEL, QKV_COLS, scale=qscale)
    o_sb, tot = sb_fwd(qkv)
    h2 = linear_res("sb_out", o_sb, w_sq, SQ_SB_O, h1)
    h3, gate2, up2 = ffn_fwd(h2, vec(ffn2_norm, 0), w_in, w_out, SLOT_FFN2[0])
    kvn = kv_norm.reshape(1, D_MODEL)
    k_sw = rms_linear("kv_k", h3, kvn, w_kv, pl.BlockSpec((D_MODEL, kv_cols), lambda i, j: (0, 0)), ident,
                      kv_cols, kv_cols, rope=(cos, sin))
    v_sw = rms_linear("kv_v", h3, kvn, w_kv, pl.BlockSpec((D_MODEL, kv_cols), lambda i, j: (0, 1)), ident,
                      kv_cols, kv_cols)
    h4, gate3, up3 = ffn_fwd(h3, vec(ffn1_norm, 1), w_in, w_out, SLOT_FFN1[1])
    q_sw = rms_linear("swa_q", h4, vec(mix_norm, 1), w_sq,
                      pl.BlockSpec((N_CHIPS, None, SQ_ROWS, 512), lambda i, j: (0, SQ_SWA_Q, 0, j)), sq_prep,
                      D_MODEL, 512, rope=(cos, sin), scale=swa_scale)
    o_sw, lse = swa_fwd(q_sw, k_sw, v_sw, sinks)
    h5 = linear_res("swa_out", o_sw, w_sq, SQ_SWA_O, h4)
    h6, gate4, up4 = ffn_fwd(h5, vec(ffn2_norm, 1), w_in, w_out, SLOT_FFN2[1])
    dh6, loss_p, d_final = loss_bwd(h6, final_norm.reshape(1, D_MODEL), tgt)

    slab = {"in": None, "out": None, "sq": None}
    in_shape = (2, N_CHIPS, 4, D_MODEL // 2, FF_CHUNK)
    out_shape = (2, N_CHIPS, 4, FF_ROWS, D_MODEL // 2)
    sq_shape = (2, N_CHIPS, 3, SQ_ROWS, D_MODEL // 2)

    def ffn_grads(tag, dh, h_in, g, gate, up, slot):
        dh_in, xn, dg_, du_, act, dhb, dnorm = ffn_bwd(dh, h_in, g, gate, up, w_in, w_out, slot)
        blk = (None, 1, None, D_MODEL // 2, FF_CHUNK)
        slab["in"] = mm_tn(f"dw_gate_{tag}", xn, dg_, D_MODEL // 2, FF_CHUNK, blk,
                           lambda k, n: (k, n, slot, 0, 0), in_shape, prev=slab["in"])
        slab["in"] = mm_tn(f"dw_up_{tag}", xn, du_, D_MODEL // 2, FF_CHUNK, blk,
                           lambda k, n: (k, 2 + n, slot, 0, 0), in_shape, prev=slab["in"])
        slab["out"] = mm_tn(f"dw_out_{tag}", act, dhb, FF_CHUNK, D_MODEL // 2,
                            (None, 2, None, FF_ROWS, D_MODEL // 2),
                            lambda k, n: (n, k, slot, 0, 0), out_shape, prev=slab["out"])
        return dh_in, dnorm

    def sq_grad(tag, a, dyb, t):
        slab["sq"] = mm_tn(f"dw_sq_{tag}", a, dyb, D_MODEL, D_MODEL // 2,
                           (None, N_CHIPS, None, SQ_ROWS, D_MODEL // 2),
                           lambda k, n: (n, 0, t, 0, 0), sq_shape, prev=slab["sq"])

    dh5, d_ffn2_1 = ffn_grads("l1b", dh6, h5, vec(ffn2_norm, 1), gate4, up4, SLOT_FFN2[1])
    do_sw, dh5b = linear_bwd_plain("swa_out_bwd", dh5, w_sq, SQ_SWA_O)
    sq_grad("swa_o", o_sw, dh5b, SQ_SWA_O)
    dq_sw, kv_own, kv_prev, d_sinks = swa_bwd(q_sw, k_sw, v_sw, sinks, do_sw, lse, cos, sin)
    sq_w_spec = pl.BlockSpec((N_CHIPS, None, SQ_ROWS, D_MODEL), lambda i, j: (0, SQ_SWA_Q, 0, 0))
    dh4, hn4, d_mix_1 = linear_bwd_rms("swa_q_bwd", [(dq_sw, w_sq, sq_w_spec, sq_prep)], h4, vec(mix_norm, 1), dh5,
                                       1, D_MODEL)
    sq_grad("swa_q", hn4, dq_sw, SQ_SWA_Q)
    dh3a, d_ffn1_1 = ffn_grads("l1a", dh4, h3, vec(ffn1_norm, 1), gate3, up3, SLOT_FFN1[1])
    dkv = kv_grad_combine(kv_own, kv_prev, cos, sin)
    kv_w_spec = pl.BlockSpec((D_MODEL, 2 * kv_cols), lambda i, j: (0, 0))
    dh3, xn3, d_kvn = linear_bwd_rms("kv_bwd", [(dkv, w_kv, kv_w_spec, ident)], h3, kvn, dh3a, 1, 2 * kv_cols)
    slab_kv = mm_tn("dw_kv", xn3, dkv, D_MODEL, kv_cols, (None, N_CHIPS, None, SQ_ROWS, kv_cols),
                    lambda k, n: (n, 0, 0, 0, 0), (2, N_CHIPS, 1, SQ_ROWS, kv_cols))
    dh2, d_ffn2_0 = ffn_grads("l0b", dh3, h2, vec(ffn2_norm, 0), gate2, up2, SLOT_FFN2[0])
    do_sb, dh2b = linear_bwd_plain("sb_out_bwd", dh2, w_sq, SQ_SB_O)
    sq_grad("sb_o", o_sb, dh2b, SQ_SB_O)
    dq_sb, dk_sb, dv_sb = sb_bwd(qkv, do_sb, tot)
    dqkv = jnp.concatenate([dq_sb, dk_sb, dv_sb], axis=1)
    qkv_w_spec = pl.BlockSpec((None, D_MODEL, QKV_COLS), lambda i, j: (j, 0, 0))
    dh1, hn1, d_mix_0 = linear_bwd_rms("sb_qkv_bwd", [(dqkv, w_qkv, qkv_w_spec, ident)], h1, vec(mix_norm, 0), dh2,
                                       N_CHIPS, QKV_COLS)
    slab_qkv = mm_tn("dw_qkv", hn1, dqkv, D_MODEL // 2, QKV_COLS, (None, 1, None, D_MODEL // 2, QKV_COLS),
                     lambda k, n: (k, n, 0, 0, 0), (2, N_CHIPS, 1, D_MODEL // 2, QKV_COLS))
    dx, d_ffn1_0 = ffn_grads("l0a", dh1, x2, vec(ffn1_norm, 0), gate1, up1, SLOT_FFN1[0])

    slabs = [slab["in"], slab["out"], slab["sq"], slab_qkv, slab_kv]
    names = ["in", "out", "sq", "qkv", "kv"]
    core = lax.axis_index("c").astype(jnp.int32).reshape(1)
    chip = (2 * lax.axis_index("x") + lax.axis_index("y")).astype(jnp.int32).reshape(1)
    theirs = exchange_halves(slabs)
    parts = [add_sibling(f"add_sibling_{nm}", s, t, core) for nm, s, t in zip(names, slabs, theirs)]
    gathered = exchange_chip_partials(parts)
    halves = [sum_chips(f"sum_chips_{nm}", g, p, chip) for nm, g, p in zip(names, gathered, parts)]
    sib_halves = share_reduced_halves(halves)
    g_in, g_out, g_sq, g_qkv, g_kv = zip(halves, sib_halves)

    def upd(name, w, m, v, g_pair, slot0, row_halves):
        shp = w.shape
        w3 = w.reshape((-1,) + shp[-2:])
        outs = adamw_shard(name, w3, m.reshape(w3.shape), v.reshape(w3.shape), g_pair[0], g_pair[1], core,
                           slot0, row_halves)
        return [o.reshape(shp) for o in outs]

    r_ffn1_in = upd("adamw_ffn1_in", ffn1_w_in, m_ffn1_w_in, v_ffn1_w_in, g_in, 0, True)
    r_ffn2_in = upd("adamw_ffn2_in", ffn2_w_in, m_ffn2_w_in, v_ffn2_w_in, g_in, 2, True)
    r_ffn1_out = upd("adamw_ffn1_out", ffn1_w_out, m_ffn1_w_out, v_ffn1_w_out, g_out, 0, False)
    r_ffn2_out = upd("adamw_ffn2_out", ffn2_w_out, m_ffn2_w_out, v_ffn2_w_out, g_out, 2, False)
    r_qkv = upd("adamw_qkv", sb_w_qkv, m_sb_w_qkv, v_sb_w_qkv, g_qkv, 0, True)
    r_sb_o = upd("adamw_sb_o", sb_w_o, m_sb_w_o, v_sb_w_o, g_sq, SQ_SB_O, False)
    r_swa_q = upd("adamw_swa_q", swa_w_q, m_swa_w_q, v_swa_w_q, g_sq, SQ_SWA_Q, False)
    r_swa_o = upd("adamw_swa_o", swa_w_o, m_swa_w_o, v_swa_w_o, g_sq, SQ_SWA_O, False)
    r_kv = upd("adamw_kv", kv_w, m_kv_w, v_kv_w, g_kv, 0, False)

    loss_row = jnp.pad(loss_p, ((0, 0), (0, D_MODEL - LANES)))
    d_sink_row = d_sinks[0, :SWA_Q_HEADS]
    part = _pack_small(jnp.concatenate([d_ffn1_0, d_ffn1_1], axis=0), jnp.concatenate([d_mix_0, d_mix_1], axis=0),
                       jnp.concatenate([d_ffn2_0, d_ffn2_1], axis=0), d_kvn, d_final, d_sink_row, loss_row)
    zrow = jnp.zeros((1, D_MODEL), F32)
    small = small_allreduce_adamw(
        part,
        _pack_small(ffn1_norm, mix_norm, ffn2_norm, kv_norm, final_norm, swa_sinks, zrow),
        _pack_small(m_ffn1_norm, m_mix_norm, m_ffn2_norm, m_kv_norm, m_final_norm, m_swa_sinks, zrow),
        _pack_small(v_ffn1_norm, v_mix_norm, v_ffn2_norm, v_kv_norm, v_final_norm, v_swa_sinks, zrow))

    def unpack(p):
        return dict(ffn1_norm=p[0:2], mix_norm=p[2:4], ffn2_norm=p[4:6], kv_norm=p[6], final_norm=p[7],
                    swa_sinks=p[8:9, :SWA_Q_HEADS])

    big = dict(ffn1_w_in=r_ffn1_in, ffn1_w_out=r_ffn1_out, ffn2_w_in=r_ffn2_in, ffn2_w_out=r_ffn2_out,
               sb_w_qkv=r_qkv, sb_w_o=r_sb_o, kv_w=r_kv, swa_w_q=r_swa_q, swa_w_o=r_swa_o)
    order = ["ffn1_norm", "ffn1_w_in", "ffn1_w_out", "mix_norm", "ffn2_norm", "ffn2_w_in", "ffn2_w_out",
             "sb_w_qkv", "sb_w_o", "kv_norm", "kv_w", "swa_w_q", "swa_sinks", "swa_w_o", "final_norm"]
    outs = []
    for kind in range(4):
        sm = unpack(small[kind])
        for nm in order:
            outs.append(big[nm][kind] if nm in big else sm[nm])
    loss = small[0][9, 0]
    return (loss, dx.reshape(x.shape), *outs)
```

```python
import functools

import jax
import jax.numpy as jnp
from jax import lax
from jax.experimental import pallas as pl
from jax.experimental.pallas import tpu as pltpu

F32 = jnp.float32
BF16 = jnp.bfloat16
MESH = pl.DeviceIdType.MESH

D_MODEL = 1024
D_FF = 2816
HEAD_DIM = 64
SB_HEADS = 16
SWA_Q_HEADS = 16
SWA_KV_HEADS = 4
WINDOW = 128
ROPE_THETA = 10000.0
RMS_EPS = 1e-6
FFN_RES_SCALE = 0.5
ATTN_SCALE = HEAD_DIM ** -0.5

ADAM_LR = 0.001
ADAM_B1 = 0.9
ADAM_B2 = 0.999
ADAM_EPS = 1e-08
ADAM_WD = 0.01
ADAM_STEP = 10

N_CHIPS = 4
N_DEV = 8
LANES = 128
FF_CHUNK = D_FF // 2
FF_ROWS = D_FF // N_CHIPS
SQ_ROWS = D_MODEL // N_CHIPS
QKV_COLS = 3 * D_MODEL // N_CHIPS
VMEM_LIMIT = 56 * 1024 * 1024
NEG_BIG = -1e30

SLOT_FFN1 = (0, 1)
SLOT_FFN2 = (2, 3)
SQ_SB_O, SQ_SWA_Q, SQ_SWA_O = 0, 1, 2


def _cparams():
    return pltpu.CompilerParams(vmem_limit_bytes=VMEM_LIMIT)


def _dot(a, b):
    return jnp.dot(a, b, preferred_element_type=F32)


def _dot_nt(a, b):
    return lax.dot_general(a, b, (((1,), (1,)), ((), ())), preferred_element_type=F32)


def _dot_tn(a, b):
    return lax.dot_general(a, b, (((0,), (0,)), ((), ())), preferred_element_type=F32)


def _rstd(h):
    return lax.rsqrt(jnp.mean(h * h, axis=-1, keepdims=True) + RMS_EPS)


def _swap32(x):
    n = x.shape[-1]
    lane = lax.broadcasted_iota(jnp.int32, x.shape, x.ndim - 1)
    first = (lane % HEAD_DIM) < (HEAD_DIM // 2)
    return jnp.where(first, pltpu.roll(x, n - HEAD_DIM // 2, x.ndim - 1), pltpu.roll(x, HEAD_DIM // 2, x.ndim - 1))


def _tile_lanes(t, n):
    return t if n == LANES else jnp.tile(t, (1, n // LANES))


FFN_ROWS = 256


def _ffn_w_in_spec(slot):
    return pl.BlockSpec((N_CHIPS, None, D_MODEL, FF_CHUNK), lambda i: (0, slot, 0, 0), pipeline_mode=pl.Buffered(1))


def _ffn_w_out_spec(slot):
    return pl.BlockSpec((N_CHIPS, None, FF_ROWS, D_MODEL), lambda i: (0, slot, 0, 0), pipeline_mode=pl.Buffered(1))


def ffn_fwd(h, g, w_in, w_out, slot):
    T = h.shape[0]
    tm = FFN_ROWS
    nch = D_FF // FF_CHUNK

    def body(h_ref, g_ref, wi_ref, wo_ref, out_ref, gate_ref, up_ref):
        hh = h_ref[...]
        xn = (hh * _rstd(hh) * g_ref[...]).astype(BF16)
        acc = None
        for j in range(nch):
            cols = slice(j * FF_CHUNK, (j + 1) * FF_CHUNK)
            gate = _dot(xn, wi_ref[j])
            up = _dot(xn, wi_ref[nch + j])
            gate_ref[:, cols] = gate.astype(BF16)
            up_ref[:, cols] = up.astype(BF16)
            a = (gate * jax.nn.sigmoid(gate) * up).astype(BF16)
            part = _dot(a, wo_ref[2 * j:2 * j + 2].reshape(FF_CHUNK, D_MODEL))
            acc = part if acc is None else acc + part
        out_ref[...] = hh + FFN_RES_SCALE * acc

    row = pl.BlockSpec((tm, D_MODEL), lambda i: (i, 0))
    ff = pl.BlockSpec((tm, D_FF), lambda i: (i, 0))
    return pl.pallas_call(
        body,
        name=f"ffn_fwd_{slot}",
        grid=(T // tm,),
        in_specs=[row, pl.BlockSpec((1, D_MODEL), lambda i: (0, 0)), _ffn_w_in_spec(slot), _ffn_w_out_spec(slot)],
        out_specs=[row, ff, ff],
        out_shape=[
            jax.ShapeDtypeStruct((T, D_MODEL), F32),
            jax.ShapeDtypeStruct((T, D_FF), BF16),
            jax.ShapeDtypeStruct((T, D_FF), BF16),
        ],
        compiler_params=_cparams(),
    )(h, g, w_in, w_out)


def ffn_bwd(dh, h, g, gate, up, w_in, w_out, slot):
    T = dh.shape[0]
    tm = FFN_ROWS
    nch = D_FF // FF_CHUNK

    def body(dh_ref, h_ref, g_ref, gate_ref, up_ref, wi_ref, wo_ref,
             dhin_ref, xn_ref, dg_ref, du_ref, a_ref, dhb_ref, dnorm_ref):
        @pl.when(pl.program_id(0) == 0)
        def _():
            dnorm_ref[...] = jnp.zeros_like(dnorm_ref)

        dhh = dh_ref[...]
        dhb = (FFN_RES_SCALE * dhh).astype(BF16)
        dhb_ref[...] = dhb
        dxn = None
        for j in range(nch):
            cols = slice(j * FF_CHUNK, (j + 1) * FF_CHUNK)
            da = _dot_nt(dhb, wo_ref[2 * j:2 * j + 2].reshape(FF_CHUNK, D_MODEL))
            gt = gate_ref[:, cols].astype(F32)
            u = up_ref[:, cols].astype(F32)
            s = jax.nn.sigmoid(gt)
            silu = gt * s
            a_ref[:, cols] = (silu * u).astype(BF16)
            dgate = (da * u * (s * (1.0 + gt * (1.0 - s)))).astype(BF16)
            dup = (da * silu).astype(BF16)
            dg_ref[:, cols] = dgate
            du_ref[:, cols] = dup
            part = _dot_nt(dgate, wi_ref[j]) + _dot_nt(dup, wi_ref[nch + j])
            dxn = part if dxn is None else dxn + part
        hh = h_ref[...]
        gg = g_ref[...]
        r = _rstd(hh)
        hr = hh * r
        xn_ref[...] = (hr * gg).astype(BF16)
        dnorm_ref[...] += jnp.sum(dxn * hr, axis=0, keepdims=True)
        gd = gg * dxn
        dhin_ref[...] = dhh + r * (gd - hr * jnp.mean(gd * hr, axis=-1, keepdims=True))

    row = pl.BlockSpec((tm, D_MODEL), lambda i: (i, 0))
    ff = pl.BlockSpec((tm, D_FF), lambda i: (i, 0))
    vec = pl.BlockSpec((1, D_MODEL), lambda i: (0, 0))
    return pl.pallas_call(
        body,
        name=f"ffn_bwd_{slot}",
        grid=(T // tm,),
        in_specs=[row, row, vec, ff, ff, _ffn_w_in_spec(slot), _ffn_w_out_spec(slot)],
        out_specs=[row, row, ff, ff, ff, row, vec],
        out_shape=[
            jax.ShapeDtypeStruct((T, D_MODEL), F32),
            jax.ShapeDtypeStruct((T, D_MODEL), BF16),
            jax.ShapeDtypeStruct((T, D_FF), BF16),
            jax.ShapeDtypeStruct((T, D_FF), BF16),
            jax.ShapeDtypeStruct((T, D_FF), BF16),
            jax.ShapeDtypeStruct((T, D_MODEL), BF16),
            jax.ShapeDtypeStruct((1, D_MODEL), F32),
        ],
        compiler_params=_cparams(),
    )(dh, h, g, gate, up, w_in, w_out)


def rms_linear(name, h, g, w, w_spec, w_prep, n_out, tn, *, rope=None, scale=None):
    T = h.shape[0]
    tm = 512
    extra, extra_specs = [], []
    if rope is not None:
        extra += list(rope)
        extra_specs += [pl.BlockSpec((tm, LANES), lambda i, j: (i, 0))] * 2
    if scale is not None:
        extra.append(scale)
        extra_specs.append(pl.BlockSpec((1, tn), lambda i, j: (0, j)))

    def body(h_ref, g_ref, w_ref, *rest):
        rest = list(rest)
        cos_ref = sin_ref = sc_ref = None
        if rope is not None:
            cos_ref, sin_ref = rest[0], rest[1]
            rest = rest[2:]
        if scale is not None:
            sc_ref = rest[0]
            rest = rest[1:]
        out_ref, xn_s = rest

        @pl.when(pl.program_id(1) == 0)
        def _():
            hh = h_ref[...]
            xn_s[...] = (hh * _rstd(hh) * g_ref[...]).astype(BF16)

        y = _dot(xn_s[...], w_prep(w_ref[...]))
        if rope is not None:
            y = y * _tile_lanes(cos_ref[...], tn) + _swap32(y) * _tile_lanes(sin_ref[...], tn)
        if scale is not None:
            y = y * sc_ref[...]
        out_ref[...] = y.astype(BF16)

    return pl.pallas_call(
        body,
        name=name,
        grid=(T // tm, n_out // tn),
        in_specs=[
            pl.BlockSpec((tm, D_MODEL), lambda i, j: (i, 0)),
            pl.BlockSpec((1, D_MODEL), lambda i, j: (0, 0)),
            w_spec,
        ] + extra_specs,
        out_specs=pl.BlockSpec((tm, tn), lambda i, j: (i, j)),
        out_shape=jax.ShapeDtypeStruct((T, n_out), BF16),
        scratch_shapes=[pltpu.VMEM((tm, D_MODEL), BF16)],
        compiler_params=_cparams(),
    )(h, g, w, *extra)


def linear_res(name, a, w_sq, t, res):
    T = a.shape[0]
    tm = 512

    def body(a_ref, w_ref, res_ref, out_ref):
        out_ref[...] = res_ref[...] + _dot(a_ref[...], w_ref[...].reshape(D_MODEL, D_MODEL))

    row = pl.BlockSpec((tm, D_MODEL), lambda i: (i, 0))
    return pl.pallas_call(
        body,
        name=name,
        grid=(T // tm,),
        in_specs=[row, pl.BlockSpec((N_CHIPS, None, SQ_ROWS, D_MODEL), lambda i: (0, t, 0, 0)), row],
        out_specs=row,
        out_shape=jax.ShapeDtypeStruct((T, D_MODEL), F32),
        compiler_params=_cparams(),
    )(a, w_sq, res)


def linear_bwd_plain(name, dy, w_sq, t):
    T = dy.shape[0]
    tm = 512

    def body(dy_ref, w_ref, da_ref, dyb_ref):
        dyb = dy_ref[...].astype(BF16)
        dyb_ref[...] = dyb
        da_ref[...] = _dot_nt(dyb, w_ref[...].reshape(D_MODEL, D_MODEL)).astype(BF16)

    row = pl.BlockSpec((tm, D_MODEL), lambda i: (i, 0))
    return pl.pallas_call(
        body,
        name=name,
        grid=(T // tm,),
        in_specs=[row, pl.BlockSpec((N_CHIPS, None, SQ_ROWS, D_MODEL), lambda i: (0, t, 0, 0))],
        out_specs=[row, row],
        out_shape=[jax.ShapeDtypeStruct((T, D_MODEL), BF16), jax.ShapeDtypeStruct((T, D_MODEL), BF16)],
        compiler_params=_cparams(),
    )(dy, w_sq)


def linear_bwd_rms(name, pairs, h, g, dres, nch, tn, tm=256):
    T = h.shape[0]
    npair = len(pairs)

    def body(*refs):
        dy_refs = refs[:npair]
        w_refs = refs[npair:2 * npair]
        h_ref, g_ref, dres_ref, dh_ref, xn_ref, dg_ref, acc_s = refs[2 * npair:]
        i = pl.program_id(0)
        j = pl.program_id(1)

        @pl.when(j == 0)
        def _():
            acc_s[...] = jnp.zeros_like(acc_s)

        @pl.when((i == 0) & (j == 0))
        def _():
            dg_ref[...] = jnp.zeros_like(dg_ref)

        part = None
        for p in range(npair):
            d = _dot_nt(dy_refs[p][...], pairs[p][3](w_refs[p][...]))
            part = d if part is None else part + d
        acc_s[...] += part

        @pl.when(j == nch - 1)
        def _():
            dxn = acc_s[...]
            hh = h_ref[...]
            gg = g_ref[...]
            r = _rstd(hh)
            hr = hh * r
            xn_ref[...] = (hr * gg).astype(BF16)
            dg_ref[...] += jnp.sum(dxn * hr, axis=0, keepdims=True)
            gd = gg * dxn
            dh_ref[...] = dres_ref[...] + r * (gd - hr * jnp.mean(gd * hr, axis=-1, keepdims=True))

    row = pl.BlockSpec((tm, D_MODEL), lambda i, j: (i, 0))
    vec = pl.BlockSpec((1, D_MODEL), lambda i, j: (0, 0))
    return pl.pallas_call(
        body,
        name=name,
        grid=(T // tm, nch),
        in_specs=[pl.BlockSpec((tm, tn), lambda i, j: (i, j))] * npair + [p[2] for p in pairs] + [row, vec, row],
        out_specs=[row, row, vec],
        out_shape=[
            jax.ShapeDtypeStruct((T, D_MODEL), F32),
            jax.ShapeDtypeStruct((T, D_MODEL), BF16),
            jax.ShapeDtypeStruct((1, D_MODEL), F32),
        ],
        scratch_shapes=[pltpu.VMEM((tm, D_MODEL), F32)],
        compiler_params=_cparams(),
    )(*[p[0] for p in pairs], *[p[1] for p in pairs], h, g, dres)


def loss_bwd(h, g, tgt):
    T = h.shape[0]
    tm = 512

    def body(h_ref, g_ref, t_ref, dh_ref, loss_ref, dg_ref):
        @pl.when(pl.program_id(0) == 0)
        def _():
            loss_ref[...] = jnp.zeros_like(loss_ref)
            dg_ref[...] = jnp.zeros_like(dg_ref)

        hh = h_ref[...]
        gg = g_ref[...]
        r = _rstd(hh)
        hr = hh * r
        err = hr * gg - t_ref[...]
        loss_ref[...] += 0.5 * jnp.sum(jnp.mean(err * err, axis=-1, keepdims=True), axis=0, keepdims=True)
        dy = err * (1.0 / D_MODEL)
        dg_ref[...] += jnp.sum(dy * hr, axis=0, keepdims=True)
        gd = gg * dy
        dh_ref[...] = r * (gd - hr * jnp.mean(gd * hr, axis=-1, keepdims=True))

    row = pl.BlockSpec((tm, D_MODEL), lambda i: (i, 0))
    vec = pl.BlockSpec((1, D_MODEL), lambda i: (0, 0))
    return pl.pallas_call(
        body,
        name="loss_bwd",
        grid=(T // tm,),
        in_specs=[row, vec, row],
        out_specs=[row, pl.BlockSpec((1, LANES), lambda i: (0, 0)), vec],
        out_shape=[
            jax.ShapeDtypeStruct((T, D_MODEL), F32),
            jax.ShapeDtypeStruct((1, LANES), F32),
            jax.ShapeDtypeStruct((1, D_MODEL), F32),
        ],
        compiler_params=_cparams(),
    )(h, g, tgt)


def mm_tn(name, a, b, tk, tn, out_block, out_index, out_shape, prev=None, tt=1024):
    T = a.shape[0]
    ns, r = out_block[1], out_block[3]
    tt = min(tt, T)
    nt = T // tt

    def body(*refs):
        if prev is None:
            a_ref, b_ref, out_ref = refs
        else:
            a_ref, b_ref, _, out_ref = refs
        t = pl.program_id(2)
        res = _dot_tn(a_ref[...], b_ref[...])

        @pl.when(t == 0)
        def _():
            for u in range(ns):
                out_ref[u] = res[u * r:(u + 1) * r]

        @pl.when(t > 0)
        def _():
            for u in range(ns):
                out_ref[u] += res[u * r:(u + 1) * r]

    in_specs = [
        pl.BlockSpec((tt, tk), lambda k, n, t: (t, k)),
        pl.BlockSpec((tt, tn), lambda k, n, t: (t, n)),
    ]
    args = [a, b]
    aliases = {}
    if prev is not None:
        in_specs.append(pl.BlockSpec(memory_space=pl.ANY))
        args.append(prev)
        aliases = {2: 0}
    return pl.pallas_call(
        body,
        name=name,
        grid=(a.shape[1] // tk, b.shape[1] // tn, nt),
        in_specs=in_specs,
        out_specs=pl.BlockSpec(out_block, lambda k, n, t: out_index(k, n)),
        out_shape=jax.ShapeDtypeStruct(out_shape, F32),
        input_output_aliases=aliases,
        compiler_params=_cparams(),
    )(*args)


SB_BLOCK = 256
SB_CHUNK = 128


LOG2E = 1.4426950408889634


def _softplus2(z2):
    sign = jnp.uint32(0x80000000)
    neg_abs = lax.bitcast_convert_type(lax.bitcast_convert_type(z2, jnp.uint32) | sign, F32)
    return jnp.log2(1.0 + jnp.exp2(neg_abs)) + jnp.maximum(z2, 0.0)


def _twice(x):
    return jnp.concatenate([x, x], axis=1)


def sb_fwd(qkv):
    T = qkv.shape[0]
    tq = SB_BLOCK
    npair = SB_HEADS // 2

    def body(q_ref, k_ref, v_ref, o_ref, tot_ref, acc_s, c_s, z_s, w_s):
        i = pl.program_id(1)
        q = q_ref[...]
        lane = lax.broadcasted_iota(jnp.int32, (tq, LANES), 1)
        first = lane < HEAD_DIM
        zero = jnp.zeros_like(q)
        q_heads = (jnp.where(first, q, zero), jnp.where(first, zero, q))
        row = lax.broadcasted_iota(jnp.int32, (tq, tq), 0)
        col = lax.broadcasted_iota(jnp.int32, (tq, tq), 1)
        strict = col < row
        from_s = (row >= col).astype(BF16)
        acc_s[...] = jnp.zeros_like(acc_s)
        c_s[...] = jnp.zeros_like(c_s)

        def rows(j):
            return pl.ds(pl.multiple_of(j * tq, tq), tq)

        def logits(j):
            kb = k_ref[rows(j), :]
            for hd in range(2):
                z_s[hd] = _dot_nt(q_heads[hd], kb) * LOG2E

        def flush(j):
            vb = v_ref[rows(j), :]
            for hd in range(2):
                acc_s[hd] += _dot(w_s[hd], vb)

        def block(j, diag):
            if not diag:
                flush(j + 1)
            chunks = [(hd, slice(r0, r0 + SB_CHUNK)) for hd in range(2) for r0 in range(0, tq, SB_CHUNK)]
            k_next = k_ref[rows(jnp.maximum(j - 1, 0)), :]
            es, sums = [], []
            for hd, rs in chunks:
                z2 = z_s[hd, rs, :]
                z_s[hd, rs, :] = _dot_nt(q_heads[hd][rs, :], k_next) * LOG2E
                if diag:
                    z2 = jnp.where(strict[rs, :], z2, NEG_BIG)
                sp = _softplus2(z2)
                c = c_s[hd, rs, :]
                es.append(z2 + _twice(c))
                c_s[hd, rs, :] = c - jnp.sum(sp, axis=1, keepdims=True)
                sums.append(_dot(sp.astype(BF16), from_s))
            for (hd, rs), e, s in zip(chunks, es, sums):
                w_s[hd, rs, :] = jnp.exp2(e - s).astype(BF16)

        logits(i)
        block(i, True)

        @pl.loop(0, i)
        def _(jj):
            block(i - 1 - jj, False)

        flush(0)
        o_ref[...] = jnp.where(first, acc_s[0], acc_s[1]).astype(BF16)
        tot_ref[...] = jnp.where(first, c_s[0], c_s[1])

    return pl.pallas_call(
        body,
        name="sb_fwd",
        grid=(npair, T // tq),
        in_specs=[
            pl.BlockSpec((tq, LANES), lambda p, i: (i, p)),
            pl.BlockSpec((T, LANES), lambda p, i: (0, npair + p)),
            pl.BlockSpec((T, LANES), lambda p, i: (0, 2 * npair + p)),
        ],
        out_specs=[pl.BlockSpec((tq, LANES), lambda p, i: (i, p))] * 2,
        out_shape=[jax.ShapeDtypeStruct((T, D_MODEL), BF16), jax.ShapeDtypeStruct((T, D_MODEL), F32)],
        scratch_shapes=[
            pltpu.VMEM((2, tq, LANES), F32), pltpu.VMEM((2, tq, LANES), F32),
            pltpu.VMEM((2, tq, tq), F32), pltpu.VMEM((2, tq, tq), BF16),
        ],
        compiler_params=_cparams(),
    )(qkv, qkv, qkv)


def sb_bwd(qkv, do, tot):
    T = qkv.shape[0]
    tq = SB_BLOCK
    npair = SB_HEADS // 2
    nq = T // tq

    def body(q_ref, k_ref, v_ref, do_ref, tot_ref, dq_ref, dk_ref, dv_ref,
             dkt_s, dvt_s, dq_s, rest_s, cg_s, z_s, da_s, dz_s, a_s):
        i = pl.program_id(1)

        @pl.when(i == 0)
        def _():
            dkt_s[...] = jnp.zeros_like(dkt_s)
            dvt_s[...] = jnp.zeros_like(dvt_s)

        q = q_ref[...]
        do_ = do_ref[...]
        tot_ = tot_ref[...]
        q_t = q.astype(F32).T.astype(BF16)
        do_t = do_.astype(F32).T.astype(BF16)
        lane = lax.broadcasted_iota(jnp.int32, (tq, LANES), 1)
        first = lane < HEAD_DIM
        zero = jnp.zeros_like(q)
        q_heads = (jnp.where(first, q, zero), jnp.where(first, zero, q))
        do_heads = (jnp.where(first, do_, zero), jnp.where(first, zero, do_))
        row = lax.broadcasted_iota(jnp.int32, (tq, tq), 0)
        col = lax.broadcasted_iota(jnp.int32, (tq, tq), 1)
        strict = col < row
        before = (row < col).astype(BF16)
        from_s = (row >= col).astype(BF16)
        rest_s[0] = jnp.broadcast_to(tot_[:, 0:1], (tq, LANES))
        rest_s[1] = jnp.broadcast_to(tot_[:, HEAD_DIM:HEAD_DIM + 1], (tq, LANES))
        cg_s[...] = jnp.zeros_like(cg_s)
        dq_s[...] = jnp.zeros_like(dq_s)
        dz_s[...] = jnp.zeros_like(dz_s)
        a_s[...] = jnp.zeros_like(a_s)

        def rows(j):
            return pl.ds(pl.multiple_of(j * tq, tq), tq)

        def logits(j):
            kb = k_ref[rows(j), :]
            vb = v_ref[rows(j), :]
            for hd in range(2):
                z_s[hd] = _dot_nt(q_heads[hd], kb) * LOG2E
                da_s[hd] = _dot_nt(do_heads[hd], vb)

        def flush(j):
            kb = k_ref[rows(j), :]
            for hd in range(2):
                dims = slice(hd * HEAD_DIM, (hd + 1) * HEAD_DIM)
                dq_s[hd] += _dot(dz_s[hd], kb)
                dkt_s[j, dims, :] += _dot(q_t[dims, :], dz_s[hd])
                dvt_s[j, dims, :] += _dot(do_t[dims, :], a_s[hd])

        def block(j, diag):
            flush(jnp.maximum(j - 1, 0))
            chunks = [(hd, slice(r0, r0 + SB_CHUNK)) for hd in range(2) for r0 in range(0, tq, SB_CHUNK)]
            nxt = rows(jnp.minimum(j + 1, i))
            k_next = k_ref[nxt, :]
            v_next = v_ref[nxt, :]
            stage1 = []
            for hd, rs in chunks:
                z2 = z_s[hd, rs, :]
                z_s[hd, rs, :] = _dot_nt(q_heads[hd][rs, :], k_next) * LOG2E
                if diag:
                    z2 = jnp.where(strict[rs, :], z2, NEG_BIG)
                sp = _softplus2(z2)
                rest = rest_s[hd, rs, :] + jnp.sum(sp, axis=1, keepdims=True)
                rest_s[hd, rs, :] = rest
                stage1.append((z2 + _twice(rest), z2 - sp, _dot(sp.astype(BF16), from_s)))
            stage2 = []
            for (hd, rs), (e, log2_beta, ahead) in zip(chunks, stage1):
                a = jnp.exp2(e - ahead)
                g = a * da_s[hd, rs, :]
                da_s[hd, rs, :] = _dot_nt(do_heads[hd][rs, :], v_next)
                cg = cg_s[hd, rs, :]
                a_s[hd, rs, :] = a.astype(BF16)
                cg_s[hd, rs, :] = cg + jnp.sum(g, axis=1, keepdims=True)
                stage2.append((g, g + _twice(cg), log2_beta, _dot(g.astype(BF16), before)))
            for (hd, rs), (g, g_from, log2_beta, g_before) in zip(chunks, stage2):
                dz_s[hd, rs, :] = (g - jnp.exp2(log2_beta) * (g_from + g_before)).astype(BF16)

        logits(0)

        @pl.loop(0, i)
        def _(j):
            block(j, False)

        block(i, True)
        flush(i)
        dq_ref[...] = (jnp.where(first, dq_s[0], dq_s[1]) * ATTN_SCALE).astype(BF16)

        @pl.when(i == nq - 1)
        def _():
            @pl.loop(0, nq)
            def _(b):
                dk_ref[rows(b), :] = dkt_s[b].T.astype(BF16)
                dv_ref[rows(b), :] = dvt_s[b].T.astype(BF16)

    qblk = pl.BlockSpec((tq, LANES), lambda p, i: (i, p))
    full = pl.BlockSpec((T, LANES), lambda p, i: (0, p))
    return pl.pallas_call(
        body,
        name="sb_bwd",
        grid=(npair, nq),
        in_specs=[
            qblk,
            pl.BlockSpec((T, LANES), lambda p, i: (0, npair + p)),
            pl.BlockSpec((T, LANES), lambda p, i: (0, 2 * npair + p)),
            qblk, qblk,
        ],
        out_specs=[qblk, full, full],
        out_shape=[jax.ShapeDtypeStruct((T, D_MODEL), BF16)] * 3,
        scratch_shapes=[
            pltpu.VMEM((nq, LANES, tq), F32), pltpu.VMEM((nq, LANES, tq), F32),
            pltpu.VMEM((2, tq, LANES), F32), pltpu.VMEM((2, tq, LANES), F32), pltpu.VMEM((2, tq, LANES), F32),
            pltpu.VMEM((2, tq, tq), F32), pltpu.VMEM((2, tq, tq), F32),
            pltpu.VMEM((2, tq, tq), BF16), pltpu.VMEM((2, tq, tq), BF16),
        ],
        compiler_params=_cparams(),
    )(qkv, qkv, qkv, do, tot)


def _swa_valid(n):
    qi = lax.broadcasted_iota(jnp.int32, (WINDOW, 2 * WINDOW), 0)
    ki = lax.broadcasted_iota(jnp.int32, (WINDOW, 2 * WINDOW), 1)
    diff = qi + WINDOW - ki
    return (diff >= 0) & (diff < WINDOW) & ((n > 0) | (ki >= WINDOW))


def _to_half(x, first, src, dst):
    keep = first if src == 0 else jnp.logical_not(first)
    x = jnp.where(keep, x, jnp.zeros_like(x))
    if src != dst:
        x = pltpu.roll(x.astype(F32), HEAD_DIM, 1).astype(BF16)
    return x


def _kv_band(prev_ref, cur_ref, pb):
    cols = slice(pb * LANES, (pb + 1) * LANES)
    return jnp.concatenate([prev_ref[:, cols], cur_ref[:, cols]], axis=0)


def _swa_specs(T):
    nb = T // WINDOW
    kv_w = SWA_KV_HEADS * HEAD_DIM
    qrow = pl.BlockSpec((WINDOW, D_MODEL), lambda n: (n, 0))
    cur = pl.BlockSpec((WINDOW, kv_w), lambda n: (n, 0))
    prev = pl.BlockSpec((WINDOW, kv_w), lambda n: (jnp.maximum(n - 1, 0), 0))
    smem = pl.BlockSpec(memory_space=pltpu.SMEM)
    return nb, qrow, cur, prev, smem


def swa_fwd(q, k, v, sinks):
    T = q.shape[0]
    nb, qrow, cur, prev, smem = _swa_specs(T)

    def body(sink_ref, q_ref, kc_ref, kp_ref, vc_ref, vp_ref, o_ref, lse_ref):
        n = pl.program_id(0)
        lane = lax.broadcasted_iota(jnp.int32, (WINDOW, LANES), 1)
        first = lane < HEAD_DIM
        valid = _swa_valid(n)
        lse_acc = jnp.zeros((WINDOW, LANES), F32)
        for pb in range(SWA_KV_HEADS // 2):
            k2 = _kv_band(kp_ref, kc_ref, pb)
            v2 = _kv_band(vp_ref, vc_ref, pb)
            for b in range(2):
                kvh = 2 * pb + b
                for qq in range(2):
                    cols = slice((2 * kvh + qq) * LANES, (2 * kvh + qq + 1) * LANES)
                    qp = q_ref[:, cols]
                    outs = []
                    for a in range(2):
                        head = 4 * kvh + 2 * qq + a
                        qh = _to_half(qp, first, a, b)
                        s = jnp.where(valid, _dot_nt(qh, k2), NEG_BIG)
                        sink = sink_ref[head]
                        m = jnp.maximum(jnp.max(s, axis=1, keepdims=True), sink)
                        p = jnp.exp(s - m)
                        den = jnp.sum(p, axis=1, keepdims=True) + jnp.exp(sink - m)
                        o = _dot((p / den).astype(BF16), v2)
                        if a != b:
                            o = pltpu.roll(o, HEAD_DIM, 1)
                        outs.append(o)
                        lse_acc = jnp.where(lane == head, m + jnp.log(den), lse_acc)
                    o_ref[:, cols] = jnp.where(first, outs[0], outs[1]).astype(BF16)
        lse_ref[...] = lse_acc

    return pl.pallas_call(
        body,
        name="swa_fwd",
        grid=(nb,),
        in_specs=[smem, qrow, cur, prev, cur, prev],
        out_specs=[qrow, pl.BlockSpec((WINDOW, LANES), lambda n: (n, 0))],
        out_shape=[jax.ShapeDtypeStruct((T, D_MODEL), BF16), jax.ShapeDtypeStruct((T, LANES), F32)],
        compiler_params=_cparams(),
    )(sinks, q, k, k, v, v)


def swa_bwd(q, k, v, sinks, do, lse, cos, sin):
    T = q.shape[0]
    nb, qrow, cur, prev, smem = _swa_specs(T)
    kv_w = SWA_KV_HEADS * HEAD_DIM

    def body(sink_ref, q_ref, kc_ref, kp_ref, vc_ref, vp_ref, do_ref, lse_ref, cos_ref, sin_ref,
             dq_ref, own_ref, prv_ref, dsink_ref):
        n = pl.program_id(0)

        @pl.when(n == 0)
        def _():
            dsink_ref[...] = jnp.zeros_like(dsink_ref)

        lane = lax.broadcasted_iota(jnp.int32, (WINDOW, LANES), 1)
        lane1 = lax.broadcasted_iota(jnp.int32, (1, LANES), 1)
        first = lane < HEAD_DIM
        valid = _swa_valid(n)
        cos_ = cos_ref[...]
        sin_ = sin_ref[...]
        dsink = jnp.zeros((1, LANES), F32)
        for pb in range(SWA_KV_HEADS // 2):
            k2 = _kv_band(kp_ref, kc_ref, pb)
            v2 = _kv_band(vp_ref, vc_ref, pb)
            dk2 = jnp.zeros((2 * WINDOW, LANES), F32)
            dv2 = jnp.zeros((2 * WINDOW, LANES), F32)
            for b in range(2):
                kvh = 2 * pb + b
                for qq in range(2):
                    cols = slice((2 * kvh + qq) * LANES, (2 * kvh + qq + 1) * LANES)
                    qp = q_ref[:, cols]
                    dop = do_ref[:, cols]
                    dqs = []
                    for a in range(2):
                        head = 4 * kvh + 2 * qq + a
                        qh = _to_half(qp, first, a, b)
                        doh = _to_half(dop, first, a, b)
                        s = jnp.where(valid, _dot_nt(qh, k2), NEG_BIG)
                        lse_h = lse_ref[:, head:head + 1]
                        p = jnp.exp(s - lse_h)
                        dp = _dot_nt(doh, v2)
                        delta = jnp.sum(p * dp, axis=1, keepdims=True)
                        ds = (p * (dp - delta)).astype(BF16)
                        p_sink = jnp.exp(sink_ref[head] - lse_h)
                        dsink = dsink + jnp.where(lane1 == head, -jnp.sum(p_sink * delta, axis=0, keepdims=True), 0.0)
                        dq = _dot(ds, k2)
                        if a != b:
                            dq = pltpu.roll(dq, HEAD_DIM, 1)
                        dqs.append(dq)
                        dk2 = dk2 + _dot_tn(ds, qh)
                        dv2 = dv2 + _dot_tn(p.astype(BF16), doh)
                    dqp = jnp.where(first, dqs[0], dqs[1])
                    dq_ref[:, cols] = ((dqp * cos_ + _swap32(dqp * sin_)) * ATTN_SCALE).astype(BF16)
            kcols = slice(pb * LANES, (pb + 1) * LANES)
            vcols = slice(kv_w + pb * LANES, kv_w + (pb + 1) * LANES)
            prv_ref[:, kcols] = dk2[:WINDOW]
            own_ref[:, kcols] = dk2[WINDOW:]
            prv_ref[:, vcols] = dv2[:WINDOW]
            own_ref[:, vcols] = dv2[WINDOW:]
        dsink_ref[...] += dsink

    tab = pl.BlockSpec((WINDOW, LANES), lambda n: (n, 0))
    kvrow = pl.BlockSpec((WINDOW, 2 * kv_w), lambda n: (n, 0))
    return pl.pallas_call(
        body,
        name="swa_bwd",
        grid=(nb,),
        in_specs=[smem, qrow, cur, prev, cur, prev, qrow, tab, tab, tab],
        out_specs=[qrow, kvrow, kvrow, pl.BlockSpec((1, LANES), lambda n: (0, 0))],
        out_shape=[
            jax.ShapeDtypeStruct((T, D_MODEL), BF16),
            jax.ShapeDtypeStruct((T, 2 * kv_w), F32),
            jax.ShapeDtypeStruct((T, 2 * kv_w), F32),
            jax.ShapeDtypeStruct((1, LANES), F32),
        ],
        compiler_params=_cparams(),
    )(sinks, q, k, k, v, v, do, lse, cos, sin)


def kv_grad_combine(own, prv, cos, sin):
    T = own.shape[0]
    nb = T // WINDOW
    kv_w = SWA_KV_HEADS * HEAD_DIM

    def body(own_ref, nxt_ref, cos_ref, sin_ref, out_ref):
        n = pl.program_id(0)
        nxt = jnp.where(n + 1 < nb, nxt_ref[...], 0.0)
        tot = own_ref[...] + nxt
        dk = tot[:, :kv_w]
        c = _tile_lanes(cos_ref[...], kv_w)
        s = _tile_lanes(sin_ref[...], kv_w)
        out_ref[:, :kv_w] = (dk * c + _swap32(dk * s)).astype(BF16)
        out_ref[:, kv_w:] = tot[:, kv_w:].astype(BF16)

    tab = pl.BlockSpec((WINDOW, LANES), lambda n: (n, 0))
    kvrow = pl.BlockSpec((WINDOW, 2 * kv_w), lambda n: (n, 0))
    return pl.pallas_call(
        body,
        name="kv_grad_combine",
        grid=(nb,),
        in_specs=[kvrow, pl.BlockSpec((WINDOW, 2 * kv_w), lambda n: (jnp.minimum(n + 1, nb - 1), 0)), tab, tab],
        out_specs=kvrow,
        out_shape=jax.ShapeDtypeStruct((T, 2 * kv_w), BF16),
        compiler_params=_cparams(),
    )(own, prv, cos, sin)


ANY = pl.BlockSpec(memory_space=pl.ANY)


def _place():
    x, y, c = lax.axis_index("x"), lax.axis_index("y"), lax.axis_index("c")
    other_chips = [(1 - x, y), (x, 1 - y), (1 - x, 1 - y)]
    return x, y, c, 2 * x + y, other_chips


def all_gather_weights(shards):
    n = len(shards)

    def body(*refs):
        ins, outs = refs[:n], refs[n:2 * n]
        send_sems, recv_sems, local_sems = refs[2 * n:]
        _, _, c, me, chips = _place()
        copies = []
        for t in range(n):
            cp = pltpu.make_async_copy(ins[t], outs[t].at[me], local_sems.at[t])
            cp.start()
            copies.append(cp)
            for jdx, (px, py) in enumerate(chips):
                cp = pltpu.make_async_remote_copy(
                    src_ref=ins[t], dst_ref=outs[t].at[me], send_sem=send_sems.at[t, jdx],
                    recv_sem=recv_sems.at[t, jdx], device_id=(px, py, c), device_id_type=MESH)
                cp.start()
                copies.append(cp)
        for cp in copies:
            cp.wait()

    return pl.pallas_call(
        body,
        name="all_gather_weights",
        in_specs=[ANY] * n,
        out_specs=[ANY] * n,
        out_shape=[jax.ShapeDtypeStruct((N_CHIPS,) + s.shape, s.dtype) for s in shards],
        scratch_shapes=[
            pltpu.SemaphoreType.DMA((n, 3)), pltpu.SemaphoreType.DMA((n, 3)), pltpu.SemaphoreType.DMA((n,)),
        ],
    )(*shards)


def exchange_halves(slabs):
    n = len(slabs)

    def body(*refs):
        ins, theirs = refs[:n], refs[n:2 * n]
        send_sems, recv_sems = refs[2 * n:]
        x, y, c, _, _ = _place()
        copies = []
        for t in range(n):
            cp = pltpu.make_async_remote_copy(
                src_ref=ins[t].at[1 - c], dst_ref=theirs[t], send_sem=send_sems.at[t],
                recv_sem=recv_sems.at[t], device_id=(x, y, 1 - c), device_id_type=MESH)
            cp.start()
            copies.append(cp)
        for cp in copies:
            cp.wait()

    return pl.pallas_call(
        body,
        name="exchange_halves",
        in_specs=[ANY] * n,
        out_specs=[ANY] * n,
        out_shape=[jax.ShapeDtypeStruct(s.shape[1:], s.dtype) for s in slabs],
        scratch_shapes=[pltpu.SemaphoreType.DMA((n,)), pltpu.SemaphoreType.DMA((n,))],
    )(*slabs)


def exchange_chip_partials(parts):
    n = len(parts)

    def body(*refs):
        ins, outs = refs[:n], refs[n:2 * n]
        send_sems, recv_sems = refs[2 * n:]
        _, _, c, me, chips = _place()
        copies = []
        for t in range(n):
            for jdx, (px, py) in enumerate(chips):
                cp = pltpu.make_async_remote_copy(
                    src_ref=ins[t].at[2 * px + py], dst_ref=outs[t].at[me], send_sem=send_sems.at[t, jdx],
                    recv_sem=recv_sems.at[t, jdx], device_id=(px, py, c), device_id_type=MESH)
                cp.start()
                copies.append(cp)
        for cp in copies:
            cp.wait()

    return pl.pallas_call(
        body,
        name="exchange_chip_partials",
        in_specs=[ANY] * n,
        out_specs=[ANY] * n,
        out_shape=[jax.ShapeDtypeStruct(p.shape, p.dtype) for p in parts],
        scratch_shapes=[pltpu.SemaphoreType.DMA((n, 3)), pltpu.SemaphoreType.DMA((n, 3))],
    )(*parts)


def share_reduced_halves(halves):
    n = len(halves)

    def body(*refs):
        ins, outs = refs[:n], refs[n:2 * n]
        send_sems, recv_sems = refs[2 * n:]
        x, y, c, _, _ = _place()
        copies = []
        for t in range(n):
            cp = pltpu.make_async_remote_copy(
                src_ref=ins[t], dst_ref=outs[t], send_sem=send_sems.at[t],
                recv_sem=recv_sems.at[t], device_id=(x, y, 1 - c), device_id_type=MESH)
            cp.start()
            copies.append(cp)
        for cp in copies:
            cp.wait()

    return pl.pallas_call(
        body,
        name="share_reduced_halves",
        in_specs=[ANY] * n,
        out_specs=[ANY] * n,
        out_shape=[jax.ShapeDtypeStruct(h.shape, h.dtype) for h in halves],
        scratch_shapes=[pltpu.SemaphoreType.DMA((n,)), pltpu.SemaphoreType.DMA((n,))],
    )(*halves)


def _row_tile(r, c):
    tr = r
    while tr * c * 4 > (3 << 19) and tr % 16 == 0:
        tr //= 2
    return tr


def add_sibling(name, slab, theirs, core):
    _, ns, slots, r, c = slab.shape
    tr = _row_tile(r, c)

    def body(core_ref, a_ref, b_ref, o_ref):
        o_ref[...] = (a_ref[...] + b_ref[...]).astype(BF16)

    blk = pl.BlockSpec((None, None, tr, c), lambda s, l, i, core_ref: (s, l, i, 0))
    return pl.pallas_call(
        body, name=name,
        grid_spec=pltpu.PrefetchScalarGridSpec(
            num_scalar_prefetch=1, grid=(ns, slots, r // tr),
            in_specs=[pl.BlockSpec((None, None, None, tr, c), lambda s, l, i, core_ref: (core_ref[0], s, l, i, 0)), blk],
            out_specs=blk),
        out_shape=jax.ShapeDtypeStruct(theirs.shape, BF16), compiler_params=_cparams(),
    )(core, slab, theirs)


def sum_chips(name, recv, own, chip):
    _, slots, r, c = recv.shape
    tr = _row_tile(r, c)

    def body(chip_ref, r0, r1, r2, r3, own_ref, o_ref):
        me = chip_ref[0]
        mine = own_ref[...]
        terms = [jnp.where(me == s, mine, rr[...]).astype(F32) for s, rr in enumerate((r0, r1, r2, r3))]
        o_ref[...] = ((terms[0] + terms[1]) + terms[2]) + terms[3]

    def src(s):
        return pl.BlockSpec((None, None, tr, c),
                            lambda l, i, chip_ref: (jnp.where(chip_ref[0] == s, (s + 1) % N_CHIPS, s), l, i, 0))

    return pl.pallas_call(
        body, name=name,
        grid_spec=pltpu.PrefetchScalarGridSpec(
            num_scalar_prefetch=1, grid=(slots, r // tr),
            in_specs=[src(0), src(1), src(2), src(3),
                      pl.BlockSpec((None, None, tr, c), lambda l, i, chip_ref: (chip_ref[0], l, i, 0))],
            out_specs=pl.BlockSpec((None, tr, c), lambda l, i, chip_ref: (l, i, 0))),
        out_shape=jax.ShapeDtypeStruct((slots, r, c), F32), compiler_params=_cparams(),
    )(chip, recv, recv, recv, recv, own)


def _adamw_math(w, g, m, v):
    m = ADAM_B1 * m + (1.0 - ADAM_B1) * g
    v = ADAM_B2 * v + (1.0 - ADAM_B2) * (g * g)
    m_hat = m / (1.0 - ADAM_B1 ** ADAM_STEP)
    v_hat = v / (1.0 - ADAM_B2 ** ADAM_STEP)
    delta = -ADAM_LR * (m_hat / (jnp.sqrt(v_hat) + ADAM_EPS) + ADAM_WD * w)
    return delta, m, v


def adamw_shard(name, w, m, v, g_own, g_sib, core, slot0, row_halves):
    n = w.shape[0]
    _, r, c = g_own.shape
    tr = _row_tile(r, c)
    nr = r // tr

    def body(core_ref, w_ref, m_ref, v_ref, own_ref, sib_ref, go_ref, d_ref, mo_ref, vo_ref):
        g = jnp.where(pl.program_id(1) == core_ref[0], own_ref[...], sib_ref[...])
        delta, mm, vv = _adamw_math(w_ref[...], g, m_ref[...], v_ref[...])
        go_ref[...] = g
        d_ref[...] = delta
        mo_ref[...] = mm
        vo_ref[...] = vv

    if row_halves:
        wspec = pl.BlockSpec((None, tr, c), lambda l, h, i, core_ref: (l, h * nr + i, 0))
    else:
        wspec = pl.BlockSpec((None, tr, c), lambda l, h, i, core_ref: (l, i, h))
    gspec = pl.BlockSpec((None, tr, c), lambda l, h, i, core_ref: (slot0 + l, i, 0))
    shp = jax.ShapeDtypeStruct(w.shape, F32)
    return pl.pallas_call(
        body, name=name,
        grid_spec=pltpu.PrefetchScalarGridSpec(
            num_scalar_prefetch=1, grid=(n, 2, nr),
            in_specs=[wspec, wspec, wspec, gspec, gspec], out_specs=[wspec] * 4),
        out_shape=[shp] * 4, compiler_params=_cparams(),
    )(core, w, m, v, g_own, g_sib)


SMALL_ROWS = 16


def small_allreduce_adamw(part, w, m, v):
    def body(p_ref, w_ref, m_ref, v_ref, g_ref, d_ref, mo_ref, vo_ref, buf, send_sems, recv_sems):
        x, y, c, _, _ = _place()
        me = 4 * x + 2 * y + c
        buf[me] = p_ref[...]
        copies = []
        for k in range(1, N_DEV):
            kx, ky, kc = (k >> 2) & 1, (k >> 1) & 1, k & 1
            peer = (x ^ kx, y ^ ky, c ^ kc)
            cp = pltpu.make_async_remote_copy(
                src_ref=p_ref, dst_ref=buf.at[me], send_sem=send_sems.at[k - 1],
                recv_sem=recv_sems.at[k - 1], device_id=peer, device_id_type=MESH)
            cp.start()
            copies.append(cp)
        for cp in copies:
            cp.wait()
        g = buf[0]
        for dev in range(1, N_DEV):
            g = g + buf[dev]
        delta, mm, vv = _adamw_math(w_ref[...], g, m_ref[...], v_ref[...])
        g_ref[...] = g
        d_ref[...] = delta
        mo_ref[...] = mm
        vo_ref[...] = vv

    vm = pl.BlockSpec(memory_space=pltpu.VMEM)
    shp = jax.ShapeDtypeStruct(part.shape, F32)
    return pl.pallas_call(
        body, name="small_allreduce_adamw",
        in_specs=[vm] * 4, out_specs=[vm] * 4, out_shape=[shp] * 4,
        scratch_shapes=[
            pltpu.VMEM((N_DEV,) + part.shape, F32),
            pltpu.SemaphoreType.DMA((N_DEV - 1,)), pltpu.SemaphoreType.DMA((N_DEV - 1,)),
        ],
    )(part, w, m, v)


def _rope_tables(T):
    half = HEAD_DIM // 2
    inv_freq = ROPE_THETA ** (-jnp.arange(half, dtype=F32) / half)
    ang = jnp.arange(T).astype(F32)[:, None] * inv_freq[None, :]
    cos = jnp.tile(jnp.cos(ang), (1, LANES // half))
    sin = jnp.tile(jnp.sin(ang), (1, LANES // half))
    lane = jnp.arange(LANES)
    sign = jnp.where((lane % HEAD_DIM) < half, -1.0, 1.0).astype(F32)
    return cos, sin * sign[None, :]


def _pack_small(ffn1, mix, ffn2, kvn, fin, sinks, loss_row):
    sink_row = jnp.pad(sinks.reshape(1, SWA_Q_HEADS), ((0, 0), (0, D_MODEL - SWA_Q_HEADS)))
    rows = jnp.concatenate([ffn1, mix, ffn2, kvn.reshape(1, -1), fin.reshape(1, -1), sink_row, loss_row], axis=0)
    return jnp.concatenate([rows, jnp.zeros((SMALL_ROWS - rows.shape[0], D_MODEL), F32)], axis=0)


def kernel(x, ffn1_norm, ffn1_w_in, ffn1_w_out, mix_norm, ffn2_norm, ffn2_w_in, ffn2_w_out, sb_w_qkv, sb_w_o, kv_norm, kv_w, swa_w_q, swa_sinks, swa_w_o, final_norm, loss_target, m_ffn1_norm, m_ffn1_w_in, m_ffn1_w_out, m_mix_norm, m_ffn2_norm, m_ffn2_w_in, m_ffn2_w_out, m_sb_w_qkv, m_sb_w_o, m_kv_norm, m_kv_w, m_swa_w_q, m_swa_sinks, m_swa_w_o, m_final_norm, v_ffn1_norm, v_ffn1_w_in, v_ffn1_w_out, v_mix_norm, v_ffn2_norm, v_ffn2_w_in, v_ffn2_w_out, v_sb_w_qkv, v_sb_w_o, v_kv_norm, v_kv_w, v_swa_w_q, v_swa_sinks, v_swa_w_o, v_final_norm):
    T = x.shape[1]
    kv_cols = SWA_KV_HEADS * HEAD_DIM
    x2 = x.reshape(T, D_MODEL)
    tgt = loss_target.reshape(T, D_MODEL)
    cos, sin = _rope_tables(T)

    w_in_l = jnp.concatenate([ffn1_w_in, ffn2_w_in], axis=0).astype(BF16)
    w_out_l = jnp.concatenate([ffn1_w_out, ffn2_w_out], axis=0).astype(BF16)
    sq_l = jnp.concatenate([sb_w_o, swa_w_q, swa_w_o], axis=0).astype(BF16)
    qkv_l = sb_w_qkv[0].astype(BF16)
    kvw_l = kv_w.astype(BF16)
    w_in, w_out, w_sq, w_qkv, w_kv = all_gather_weights([w_in_l, w_out_l, sq_l, qkv_l, kvw_l])
    w_kv = w_kv.reshape(D_MODEL, 2 * kv_cols)

    def vec(a, i):
        return a[i].reshape(1, D_MODEL)

    ident = lambda w: w
    sq_prep = lambda w: w.reshape(D_MODEL, w.shape[-1])
    qscale = jnp.concatenate([jnp.full((1, D_MODEL), ATTN_SCALE, F32), jnp.ones((1, 2 * D_MODEL), F32)], axis=1)
    swa_scale = jnp.full((1, D_MODEL), ATTN_SCALE, F32)
    sinks = swa_sinks.reshape(SWA_Q_HEADS)

    h1, gate1, up1 = ffn_fwd(x2, vec(ffn1_norm, 0), w_in, w_out, SLOT_FFN1[0])
    qkv = rms_linear("sb_qkv", h1, vec(mix_norm, 0), w_qkv,
                     pl.BlockSpec((None, D_MODEL, QKV_COLS), lambda i, j: (j, 0, 0)), ident,
                     3 * D_MODEL, QKV_COLS, scale=qscale)
    o_sb, tot = sb_fwd(qkv)
    h2 = linear_res("sb_out", o_sb, w_sq, SQ_SB_O, h1)
    h3, gate2, up2 = ffn_fwd(h2, vec(ffn2_norm, 0), w_in, w_out, SLOT_FFN2[0])
    kvn = kv_norm.reshape(1, D_MODEL)
    k_sw = rms_linear("kv_k", h3, kvn, w_kv, pl.BlockSpec((D_MODEL, kv_cols), lambda i, j: (0, 0)), ident,
                      kv_cols, kv_cols, rope=(cos, sin))
    v_sw = rms_linear("kv_v", h3, kvn, w_kv, pl.BlockSpec((D_MODEL, kv_cols), lambda i, j: (0, 1)), ident,
                      kv_cols, kv_cols)
    h4, gate3, up3 = ffn_fwd(h3, vec(ffn1_norm, 1), w_in, w_out, SLOT_FFN1[1])
    q_sw = rms_linear("swa_q", h4, vec(mix_norm, 1), w_sq,
                      pl.BlockSpec((N_CHIPS, None, SQ_ROWS, 512), lambda i, j: (0, SQ_SWA_Q, 0, j)), sq_prep,
                      D_MODEL, 512, rope=(cos, sin), scale=swa_scale)
    o_sw, lse = swa_fwd(q_sw, k_sw, v_sw, sinks)
    h5 = linear_res("swa_out", o_sw, w_sq, SQ_SWA_O, h4)
    h6, gate4, up4 = ffn_fwd(h5, vec(ffn2_norm, 1), w_in, w_out, SLOT_FFN2[1])
    dh6, loss_p, d_final = loss_bwd(h6, final_norm.reshape(1, D_MODEL), tgt)

    slab = {"in": None, "out": None, "sq": None}
    in_shape = (2, N_CHIPS, 4, D_MODEL // 2, FF_CHUNK)
    out_shape = (2, N_CHIPS, 4, FF_ROWS, D_MODEL // 2)
    sq_shape = (2, N_CHIPS, 3, SQ_ROWS, D_MODEL // 2)

    def ffn_grads(tag, dh, h_in, g, gate, up, slot):
        dh_in, xn, dg_, du_, act, dhb, dnorm = ffn_bwd(dh, h_in, g, gate, up, w_in, w_out, slot)
        blk = (None, 1, None, D_MODEL // 2, FF_CHUNK)
        slab["in"] = mm_tn(f"dw_gate_{tag}", xn, dg_, D_MODEL // 2, FF_CHUNK, blk,
                           lambda k, n: (k, n, slot, 0, 0), in_shape, prev=slab["in"])
        slab["in"] = mm_tn(f"dw_up_{tag}", xn, du_, D_MODEL // 2, FF_CHUNK, blk,
                           lambda k, n: (k, 2 + n, slot, 0, 0), in_shape, prev=slab["in"])
        slab["out"] = mm_tn(f"dw_out_{tag}", act, dhb, FF_CHUNK, D_MODEL // 2,
                            (None, 2, None, FF_ROWS, D_MODEL // 2),
                            lambda k, n: (n, k, slot, 0, 0), out_shape, prev=slab["out"])
        return dh_in, dnorm

    def sq_grad(tag, a, dyb, t):
        slab["sq"] = mm_tn(f"dw_sq_{tag}", a, dyb, D_MODEL, D_MODEL // 2,
                           (None, N_CHIPS, None, SQ_ROWS, D_MODEL // 2),
                           lambda k, n: (n, 0, t, 0, 0), sq_shape, prev=slab["sq"])

    dh5, d_ffn2_1 = ffn_grads("l1b", dh6, h5, vec(ffn2_norm, 1), gate4, up4, SLOT_FFN2[1])
    do_sw, dh5b = linear_bwd_plain("swa_out_bwd", dh5, w_sq, SQ_SWA_O)
    sq_grad("swa_o", o_sw, dh5b, SQ_SWA_O)
    dq_sw, kv_own, kv_prev, d_sinks = swa_bwd(q_sw, k_sw, v_sw, sinks, do_sw, lse, cos, sin)
    sq_w_spec = pl.BlockSpec((N_CHIPS, None, SQ_ROWS, D_MODEL), lambda i, j: (0, SQ_SWA_Q, 0, 0))
    dh4, hn4, d_mix_1 = linear_bwd_rms("swa_q_bwd", [(dq_sw, w_sq, sq_w_spec, sq_prep)], h4, vec(mix_norm, 1), dh5,
                                       1, D_MODEL)
    sq_grad("swa_q", hn4, dq_sw, SQ_SWA_Q)
    dh3a, d_ffn1_1 = ffn_grads("l1a", dh4, h3, vec(ffn1_norm, 1), gate3, up3, SLOT_FFN1[1])
    dkv = kv_grad_combine(kv_own, kv_prev, cos, sin)
    kv_w_spec = pl.BlockSpec((D_MODEL, 2 * kv_cols), lambda i, j: (0, 0))
    dh3, xn3, d_kvn = linear_bwd_rms("kv_bwd", [(dkv, w_kv, kv_w_spec, ident)], h3, kvn, dh3a, 1, 2 * kv_cols)
    slab_kv = mm_tn("dw_kv", xn3, dkv, D_MODEL, kv_cols, (None, N_CHIPS, None, SQ_ROWS, kv_cols),
                    lambda k, n: (n, 0, 0, 0, 0), (2, N_CHIPS, 1, SQ_ROWS, kv_cols))
    dh2, d_ffn2_0 = ffn_grads("l0b", dh3, h2, vec(ffn2_norm, 0), gate2, up2, SLOT_FFN2[0])
    do_sb, dh2b = linear_bwd_plain("sb_out_bwd", dh2, w_sq, SQ_SB_O)
    sq_grad("sb_o", o_sb, dh2b, SQ_SB_O)
    dq_sb, dk_sb, dv_sb = sb_bwd(qkv, do_sb, tot)
    dqkv = jnp.concatenate([dq_sb, dk_sb, dv_sb], axis=1)
    qkv_w_spec = pl.BlockSpec((None, D_MODEL, QKV_COLS), lambda i, j: (j, 0, 0))
    dh1, hn1, d_mix_0 = linear_bwd_rms("sb_qkv_bwd", [(dqkv, w_qkv, qkv_w_spec, ident)], h1, vec(mix_norm, 0), dh2,
                                       N_CHIPS, QKV_COLS)
    slab_qkv = mm_tn("dw_qkv", hn1, dqkv, D_MODEL // 2, QKV_COLS, (None, 1, None, D_MODEL // 2, QKV_COLS),
                     lambda k, n: (k, n, 0, 0, 0), (2, N_CHIPS, 1, D_MODEL // 2, QKV_COLS))
    dx, d_ffn1_0 = ffn_grads("l0a", dh1, x2, vec(ffn1_norm, 0), gate1, up1, SLOT_FFN1[0])

    slabs = [slab["in"], slab["out"], slab["sq"], slab_qkv, slab_kv]
    names = ["in", "out", "sq", "qkv", "kv"]
    core = lax.axis_index("c").astype(jnp.int32).reshape(1)
    chip = (2 * lax.axis_index("x") + lax.axis_index("y")).astype(jnp.int32).reshape(1)
    theirs = exchange_halves(slabs)
    parts = [add_sibling(f"add_sibling_{nm}", s, t, core) for nm, s, t in zip(names, slabs, theirs)]
    gathered = exchange_chip_partials(parts)
    halves = [sum_chips(f"sum_chips_{nm}", g, p, chip) for nm, g, p in zip(names, gathered, parts)]
    sib_halves = share_reduced_halves(halves)
    g_in, g_out, g_sq, g_qkv, g_kv = zip(halves, sib_halves)

    def upd(name, w, m, v, g_pair, slot0, row_halves):
        shp = w.shape
        w3 = w.reshape((-1,) + shp[-2:])
        outs = adamw_shard(name, w3, m.reshape(w3.shape), v.reshape(w3.shape), g_pair[0], g_pair[1], core,
                           slot0, row_halves)
        return [o.reshape(shp) for o in outs]

    r_ffn1_in = upd("adamw_ffn1_in", ffn1_w_in, m_ffn1_w_in, v_ffn1_w_in, g_in, 0, True)
    r_ffn2_in = upd("adamw_ffn2_in", ffn2_w_in, m_ffn2_w_in, v_ffn2_w_in, g_in, 2, True)
    r_ffn1_out = upd("adamw_ffn1_out", ffn1_w_out, m_ffn1_w_out, v_ffn1_w_out, g_out, 0, False)
    r_ffn2_out = upd("adamw_ffn2_out", ffn2_w_out, m_ffn2_w_out, v_ffn2_w_out, g_out, 2, False)
    r_qkv = upd("adamw_qkv", sb_w_qkv, m_sb_w_qkv, v_sb_w_qkv, g_qkv, 0, True)
    r_sb_o = upd("adamw_sb_o", sb_w_o, m_sb_w_o, v_sb_w_o, g_sq, SQ_SB_O, False)
    r_swa_q = upd("adamw_swa_q", swa_w_q, m_swa_w_q, v_swa_w_q, g_sq, SQ_SWA_Q, False)
    r_swa_o = upd("adamw_swa_o", swa_w_o, m_swa_w_o, v_swa_w_o, g_sq, SQ_SWA_O, False)
    r_kv = upd("adamw_kv", kv_w, m_kv_w, v_kv_w, g_kv, 0, False)

    loss_row = jnp.pad(loss_p, ((0, 0), (0, D_MODEL - LANES)))
    d_sink_row = d_sinks[0, :SWA_Q_HEADS]
    part = _pack_small(jnp.concatenate([d_ffn1_0, d_ffn1_1], axis=0), jnp.concatenate([d_mix_0, d_mix_1], axis=0),
                       jnp.concatenate([d_ffn2_0, d_ffn2_1], axis=0), d_kvn, d_final, d_sink_row, loss_row)
    zrow = jnp.zeros((1, D_MODEL), F32)
    small = small_allreduce_adamw(
        part,
        _pack_small(ffn1_norm, mix_norm, ffn2_norm, kv_norm, final_norm, swa_sinks, zrow),
        _pack_small(m_ffn1_norm, m_mix_norm, m_ffn2_norm, m_kv_norm, m_final_norm, m_swa_sinks, zrow),
        _pack_small(v_ffn1_norm, v_mix_norm, v_ffn2_norm, v_kv_norm, v_final_norm, v_swa_sinks, zrow))

    def unpack(p):
        return dict(ffn1_norm=p[0:2], mix_norm=p[2:4], ffn2_norm=p[4:6], kv_norm=p[6], final_norm=p[7],
                    swa_sinks=p[8:9, :SWA_Q_HEADS])

    big = dict(ffn1_w_in=r_ffn1_in, ffn1_w_out=r_ffn1_out, ffn2_w_in=r_ffn2_in, ffn2_w_out=r_ffn2_out,
               sb_w_qkv=r_qkv, sb_w_o=r_sb_o, kv_w=r_kv, swa_w_q=r_swa_q, swa_w_o=r_swa_o)
    order = ["ffn1_norm", "ffn1_w_in", "ffn1_w_out", "mix_norm", "ffn2_norm", "ffn2_w_in", "ffn2_w_out",
             "sb_w_qkv", "sb_w_o", "kv_norm", "kv_w", "swa_w_q", "swa_sinks", "swa_w_o", "final_norm"]
    outs = []
    for kind in range(4):
        sm = unpack(small[kind])
        for nm in order:
            outs.append(big[nm][kind] if nm in big else sm[nm])
    loss = small[0][9, 0]
    return (loss, dx.reshape(x.shape), *outs)
```

```python
import functools

import jax
import jax.numpy as jnp
from jax import lax
from jax.experimental import pallas as pl
from jax.experimental.pallas import tpu as pltpu

F32 = jnp.float32
BF16 = jnp.bfloat16
MESH = pl.DeviceIdType.MESH

D_MODEL = 1024
D_FF = 2816
HEAD_DIM = 64
SB_HEADS = 16
SWA_Q_HEADS = 16
SWA_KV_HEADS = 4
WINDOW = 128
ROPE_THETA = 10000.0
RMS_EPS = 1e-6
FFN_RES_SCALE = 0.5
ATTN_SCALE = HEAD_DIM ** -0.5

ADAM_LR = 0.001
ADAM_B1 = 0.9
ADAM_B2 = 0.999
ADAM_EPS = 1e-08
ADAM_WD = 0.01
ADAM_STEP = 10

N_CHIPS = 4
N_DEV = 8
LANES = 128
FF_CHUNK = D_FF // 2
FF_ROWS = D_FF // N_CHIPS
SQ_ROWS = D_MODEL // N_CHIPS
QKV_COLS = 3 * D_MODEL // N_CHIPS
VMEM_LIMIT = 56 * 1024 * 1024
NEG_BIG = -1e30

SLOT_FFN1 = (0, 1)
SLOT_FFN2 = (2, 3)
SQ_SB_O, SQ_SWA_Q, SQ_SWA_O = 0, 1, 2


def _cparams():
    return pltpu.CompilerParams(vmem_limit_bytes=VMEM_LIMIT)


def _dot(a, b):
    return jnp.dot(a, b, preferred_element_type=F32)


def _dot_nt(a, b):
    return lax.dot_general(a, b, (((1,), (1,)), ((), ())), preferred_element_type=F32)


def _dot_tn(a, b):
    return lax.dot_general(a, b, (((0,), (0,)), ((), ())), preferred_element_type=F32)


def _rstd(h):
    return lax.rsqrt(jnp.mean(h * h, axis=-1, keepdims=True) + RMS_EPS)


def _swap32(x):
    n = x.shape[-1]
    lane = lax.broadcasted_iota(jnp.int32, x.shape, x.ndim - 1)
    first = (lane % HEAD_DIM) < (HEAD_DIM // 2)
    return jnp.where(first, pltpu.roll(x, n - HEAD_DIM // 2, x.ndim - 1), pltpu.roll(x, HEAD_DIM // 2, x.ndim - 1))


def _tile_lanes(t, n):
    return t if n == LANES else jnp.tile(t, (1, n // LANES))


FFN_ROWS = 256


def _ffn_w_in_spec(slot):
    return pl.BlockSpec((N_CHIPS, None, D_MODEL, FF_CHUNK), lambda i: (0, slot, 0, 0), pipeline_mode=pl.Buffered(1))


def _ffn_w_out_spec(slot):
    return pl.BlockSpec((N_CHIPS, None, FF_ROWS, D_MODEL), lambda i: (0, slot, 0, 0), pipeline_mode=pl.Buffered(1))


def ffn_fwd(tag, h, g, w_in, w_out, slot):
    T = h.shape[0]
    tm = FFN_ROWS
    nch = D_FF // FF_CHUNK

    def body(h_ref, g_ref, wi_ref, wo_ref, out_ref, gate_ref, up_ref):
        hh = h_ref[...]
        xn = (hh * _rstd(hh) * g_ref[...]).astype(BF16)
        acc = None
        for j in range(nch):
            cols = slice(j * FF_CHUNK, (j + 1) * FF_CHUNK)
            gate = _dot(xn, wi_ref[j])
            up = _dot(xn, wi_ref[nch + j])
            gate_ref[:, cols] = gate.astype(BF16)
            up_ref[:, cols] = up.astype(BF16)
            a = (gate * jax.nn.sigmoid(gate) * up).astype(BF16)
            part = _dot(a, wo_ref[2 * j:2 * j + 2].reshape(FF_CHUNK, D_MODEL))
            acc = part if acc is None else acc + part
        out_ref[...] = hh + FFN_RES_SCALE * acc

    row = pl.BlockSpec((tm, D_MODEL), lambda i: (i, 0))
    ff = pl.BlockSpec((tm, D_FF), lambda i: (i, 0))
    return pl.pallas_call(
        body,
        name=f"ffn_fwd_{tag}",
        grid=(T // tm,),
        in_specs=[row, pl.BlockSpec((1, D_MODEL), lambda i: (0, 0)), _ffn_w_in_spec(slot), _ffn_w_out_spec(slot)],
        out_specs=[row, ff, ff],
        out_shape=[
            jax.ShapeDtypeStruct((T, D_MODEL), F32),
            jax.ShapeDtypeStruct((T, D_FF), BF16),
            jax.ShapeDtypeStruct((T, D_FF), BF16),
        ],
        compiler_params=_cparams(),
    )(h, g, w_in, w_out)


def ffn_bwd(tag, dh, h, g, gate, up, w_in, w_out, slot):
    T = dh.shape[0]
    tm = FFN_ROWS
    nch = D_FF // FF_CHUNK

    def body(dh_ref, h_ref, g_ref, gate_ref, up_ref, wi_ref, wo_ref,
             dhin_ref, xn_ref, dg_ref, du_ref, a_ref, dhb_ref, dnorm_ref):
        @pl.when(pl.program_id(0) == 0)
        def _():
            dnorm_ref[...] = jnp.zeros_like(dnorm_ref)

        dhh = dh_ref[...]
        dhb = (FFN_RES_SCALE * dhh).astype(BF16)
        dhb_ref[...] = dhb
        dxn = None
        for j in range(nch):
            cols = slice(j * FF_CHUNK, (j + 1) * FF_CHUNK)
            da = _dot_nt(dhb, wo_ref[2 * j:2 * j + 2].reshape(FF_CHUNK, D_MODEL))
            gt = gate_ref[:, cols].astype(F32)
            u = up_ref[:, cols].astype(F32)
            s = jax.nn.sigmoid(gt)
            silu = gt * s
            a_ref[:, cols] = (silu * u).astype(BF16)
            dgate = (da * u * (s * (1.0 + gt * (1.0 - s)))).astype(BF16)
            dup = (da * silu).astype(BF16)
            dg_ref[:, cols] = dgate
            du_ref[:, cols] = dup
            part = _dot_nt(dgate, wi_ref[j]) + _dot_nt(dup, wi_ref[nch + j])
            dxn = part if dxn is None else dxn + part
        hh = h_ref[...]
        gg = g_ref[...]
        r = _rstd(hh)
        hr = hh * r
        xn_ref[...] = (hr * gg).astype(BF16)
        dnorm_ref[...] += jnp.sum(dxn * hr, axis=0, keepdims=True)
        gd = gg * dxn
        dhin_ref[...] = dhh + r * (gd - hr * jnp.mean(gd * hr, axis=-1, keepdims=True))

    row = pl.BlockSpec((tm, D_MODEL), lambda i: (i, 0))
    ff = pl.BlockSpec((tm, D_FF), lambda i: (i, 0))
    vec = pl.BlockSpec((1, D_MODEL), lambda i: (0, 0))
    return pl.pallas_call(
        body,
        name=f"ffn_bwd_{tag}",
        grid=(T // tm,),
        in_specs=[row, row, vec, ff, ff, _ffn_w_in_spec(slot), _ffn_w_out_spec(slot)],
        out_specs=[row, row, ff, ff, ff, row, vec],
        out_shape=[
            jax.ShapeDtypeStruct((T, D_MODEL), F32),
            jax.ShapeDtypeStruct((T, D_MODEL), BF16),
            jax.ShapeDtypeStruct((T, D_FF), BF16),
            jax.ShapeDtypeStruct((T, D_FF), BF16),
            jax.ShapeDtypeStruct((T, D_FF), BF16),
            jax.ShapeDtypeStruct((T, D_MODEL), BF16),
            jax.ShapeDtypeStruct((1, D_MODEL), F32),
        ],
        compiler_params=_cparams(),
    )(dh, h, g, gate, up, w_in, w_out)


def rms_linear(name, h, g, w, w_spec, w_prep, n_out, tn, *, rope=None, scale=None):
    T = h.shape[0]
    tm = 512
    extra, extra_specs = [], []
    if rope is not None:
        extra += list(rope)
        extra_specs += [pl.BlockSpec((tm, LANES), lambda i, j: (i, 0))] * 2
    if scale is not None:
        extra.append(scale)
        extra_specs.append(pl.BlockSpec((1, tn), lambda i, j: (0, j)))

    def body(h_ref, g_ref, w_ref, *rest):
        rest = list(rest)
        cos_ref = sin_ref = sc_ref = None
        if rope is not None:
            cos_ref, sin_ref = rest[0], rest[1]
            rest = rest[2:]
        if scale is not None:
            sc_ref = rest[0]
            rest = rest[1:]
        out_ref, xn_s = rest

        @pl.when(pl.program_id(1) == 0)
        def _():
            hh = h_ref[...]
            xn_s[...] = (hh * _rstd(hh) * g_ref[...]).astype(BF16)

        y = _dot(xn_s[...], w_prep(w_ref[...]))
        if rope is not None:
            y = y * _tile_lanes(cos_ref[...], tn) + _swap32(y) * _tile_lanes(sin_ref[...], tn)
        if scale is not None:
            y = y * sc_ref[...]
        out_ref[...] = y.astype(BF16)

    return pl.pallas_call(
        body,
        name=name,
        grid=(T // tm, n_out // tn),
        in_specs=[
            pl.BlockSpec((tm, D_MODEL), lambda i, j: (i, 0)),
            pl.BlockSpec((1, D_MODEL), lambda i, j: (0, 0)),
            w_spec,
        ] + extra_specs,
        out_specs=pl.BlockSpec((tm, tn), lambda i, j: (i, j)),
        out_shape=jax.ShapeDtypeStruct((T, n_out), BF16),
        scratch_shapes=[pltpu.VMEM((tm, D_MODEL), BF16)],
        compiler_params=_cparams(),
    )(h, g, w, *extra)


def linear_res(name, a, w_sq, t, res):
    T = a.shape[0]
    tm = 512

    def body(a_ref, w_ref, res_ref, out_ref):
        out_ref[...] = res_ref[...] + _dot(a_ref[...], w_ref[...].reshape(D_MODEL, D_MODEL))

    row = pl.BlockSpec((tm, D_MODEL), lambda i: (i, 0))
    return pl.pallas_call(
        body,
        name=name,
        grid=(T // tm,),
        in_specs=[row, pl.BlockSpec((N_CHIPS, None, SQ_ROWS, D_MODEL), lambda i: (0, t, 0, 0)), row],
        out_specs=row,
        out_shape=jax.ShapeDtypeStruct((T, D_MODEL), F32),
        compiler_params=_cparams(),
    )(a, w_sq, res)


def linear_bwd_plain(name, dy, w_sq, t):
    T = dy.shape[0]
    tm = 512

    def body(dy_ref, w_ref, da_ref, dyb_ref):
        dyb = dy_ref[...].astype(BF16)
        dyb_ref[...] = dyb
        da_ref[...] = _dot_nt(dyb, w_ref[...].reshape(D_MODEL, D_MODEL)).astype(BF16)

    row = pl.BlockSpec((tm, D_MODEL), lambda i: (i, 0))
    return pl.pallas_call(
        body,
        name=name,
        grid=(T // tm,),
        in_specs=[row, pl.BlockSpec((N_CHIPS, None, SQ_ROWS, D_MODEL), lambda i: (0, t, 0, 0))],
        out_specs=[row, row],
        out_shape=[jax.ShapeDtypeStruct((T, D_MODEL), BF16), jax.ShapeDtypeStruct((T, D_MODEL), BF16)],
        compiler_params=_cparams(),
    )(dy, w_sq)


def linear_bwd_rms(name, pairs, h, g, dres, nch, tn, tm=256):
    T = h.shape[0]
    npair = len(pairs)

    def body(*refs):
        dy_refs = refs[:npair]
        w_refs = refs[npair:2 * npair]
        h_ref, g_ref, dres_ref, dh_ref, xn_ref, dg_ref, acc_s = refs[2 * npair:]
        i = pl.program_id(0)
        j = pl.program_id(1)

        @pl.when(j == 0)
        def _():
            acc_s[...] = jnp.zeros_like(acc_s)

        @pl.when((i == 0) & (j == 0))
        def _():
            dg_ref[...] = jnp.zeros_like(dg_ref)

        part = None
        for p in range(npair):
            d = _dot_nt(dy_refs[p][...], pairs[p][3](w_refs[p][...]))
            part = d if part is None else part + d
        acc_s[...] += part

        @pl.when(j == nch - 1)
        def _():
            dxn = acc_s[...]
            hh = h_ref[...]
            gg = g_ref[...]
            r = _rstd(hh)
            hr = hh * r
            xn_ref[...] = (hr * gg).astype(BF16)
            dg_ref[...] += jnp.sum(dxn * hr, axis=0, keepdims=True)
            gd = gg * dxn
            dh_ref[...] = dres_ref[...] + r * (gd - hr * jnp.mean(gd * hr, axis=-1, keepdims=True))

    row = pl.BlockSpec((tm, D_MODEL), lambda i, j: (i, 0))
    vec = pl.BlockSpec((1, D_MODEL), lambda i, j: (0, 0))
    return pl.pallas_call(
        body,
        name=name,
        grid=(T // tm, nch),
        in_specs=[pl.BlockSpec((tm, tn), lambda i, j: (i, j))] * npair + [p[2] for p in pairs] + [row, vec, row],
        out_specs=[row, row, vec],
        out_shape=[
            jax.ShapeDtypeStruct((T, D_MODEL), F32),
            jax.ShapeDtypeStruct((T, D_MODEL), BF16),
            jax.ShapeDtypeStruct((1, D_MODEL), F32),
        ],
        scratch_shapes=[pltpu.VMEM((tm, D_MODEL), F32)],
        compiler_params=_cparams(),
    )(*[p[0] for p in pairs], *[p[1] for p in pairs], h, g, dres)


def loss_bwd(h, g, tgt):
    T = h.shape[0]
    tm = 512

    def body(h_ref, g_ref, t_ref, dh_ref, loss_ref, dg_ref):
        @pl.when(pl.program_id(0) == 0)
        def _():
            loss_ref[...] = jnp.zeros_like(loss_ref)
            dg_ref[...] = jnp.zeros_like(dg_ref)

        hh = h_ref[...]
        gg = g_ref[...]
        r = _rstd(hh)
        hr = hh * r
        err = hr * gg - t_ref[...]
        loss_ref[...] += 0.5 * jnp.sum(jnp.mean(err * err, axis=-1, keepdims=True), axis=0, keepdims=True)
        dy = err * (1.0 / D_MODEL)
        dg_ref[...] += jnp.sum(dy * hr, axis=0, keepdims=True)
        gd = gg * dy
        dh_ref[...] = r * (gd - hr * jnp.mean(gd * hr, axis=-1, keepdims=True))

    row = pl.BlockSpec((tm, D_MODEL), lambda i: (i, 0))
    vec = pl.BlockSpec((1, D_MODEL), lambda i: (0, 0))
    return pl.pallas_call(
        body,
        name="loss_bwd",
        grid=(T // tm,),
        in_specs=[row, vec, row],
        out_specs=[row, pl.BlockSpec((1, LANES), lambda i: (0, 0)), vec],
        out_shape=[
            jax.ShapeDtypeStruct((T, D_MODEL), F32),
            jax.ShapeDtypeStruct((1, LANES), F32),
            jax.ShapeDtypeStruct((1, D_MODEL), F32),
        ],
        compiler_params=_cparams(),
    )(h, g, tgt)


def mm_tn(name, a, b, tk, tn, out_block, out_index, out_shape, prev=None, tt=1024):
    T = a.shape[0]
    ns, r = out_block[1], out_block[3]
    tt = min(tt, T)
    nt = T // tt

    def body(*refs):
        if prev is None:
            a_ref, b_ref, out_ref = refs
        else:
            a_ref, b_ref, _, out_ref = refs
        t = pl.program_id(2)
        res = _dot_tn(a_ref[...], b_ref[...])

        @pl.when(t == 0)
        def _():
            for u in range(ns):
                out_ref[u] = res[u * r:(u + 1) * r]

        @pl.when(t > 0)
        def _():
            for u in range(ns):
                out_ref[u] += res[u * r:(u + 1) * r]

    in_specs = [
        pl.BlockSpec((tt, tk), lambda k, n, t: (t, k)),
        pl.BlockSpec((tt, tn), lambda k, n, t: (t, n)),
    ]
    args = [a, b]
    aliases = {}
    if prev is not None:
        in_specs.append(pl.BlockSpec(memory_space=pl.ANY))
        args.append(prev)
        aliases = {2: 0}
    return pl.pallas_call(
        body,
        name=name,
        grid=(a.shape[1] // tk, b.shape[1] // tn, nt),
        in_specs=in_specs,
        out_specs=pl.BlockSpec(out_block, lambda k, n, t: out_index(k, n)),
        out_shape=jax.ShapeDtypeStruct(out_shape, F32),
        input_output_aliases=aliases,
        compiler_params=_cparams(),
    )(*args)


SB_BLOCK = 256
SB_CHUNK = 128


LOG2E = 1.4426950408889634


def _softplus2(z2):
    sign = jnp.uint32(0x80000000)
    neg_abs = lax.bitcast_convert_type(lax.bitcast_convert_type(z2, jnp.uint32) | sign, F32)
    return jnp.log2(1.0 + jnp.exp2(neg_abs)) + jnp.maximum(z2, 0.0)


def _twice(x):
    return jnp.concatenate([x, x], axis=1)


def sb_fwd(qkv):
    T = qkv.shape[0]
    tq = SB_BLOCK
    npair = SB_HEADS // 2

    def body(q_ref, k_ref, v_ref, o_ref, tot_ref, acc_s, c_s, z_s, w_s):
        i = pl.program_id(1)
        q = q_ref[...]
        lane = lax.broadcasted_iota(jnp.int32, (tq, LANES), 1)
        first = lane < HEAD_DIM
        zero = jnp.zeros_like(q)
        q_heads = (jnp.where(first, q, zero), jnp.where(first, zero, q))
        row = lax.broadcasted_iota(jnp.int32, (tq, tq), 0)
        col = lax.broadcasted_iota(jnp.int32, (tq, tq), 1)
        strict = col < row
        from_s = (row >= col).astype(BF16)
        acc_s[...] = jnp.zeros_like(acc_s)
        c_s[...] = jnp.zeros_like(c_s)

        def rows(j):
            return pl.ds(pl.multiple_of(j * tq, tq), tq)

        def logits(j):
            kb = k_ref[rows(j), :]
            for hd in range(2):
                z_s[hd] = _dot_nt(q_heads[hd], kb) * LOG2E

        def flush(j):
            vb = v_ref[rows(j), :]
            for hd in range(2):
                acc_s[hd] += _dot(w_s[hd], vb)

        def block(j, diag):
            if not diag:
                flush(j + 1)
            chunks = [(hd, slice(r0, r0 + SB_CHUNK)) for hd in range(2) for r0 in range(0, tq, SB_CHUNK)]
            k_next = k_ref[rows(jnp.maximum(j - 1, 0)), :]
            es, sums = [], []
            for hd, rs in chunks:
                z2 = z_s[hd, rs, :]
                z_s[hd, rs, :] = _dot_nt(q_heads[hd][rs, :], k_next) * LOG2E
                if diag:
                    z2 = jnp.where(strict[rs, :], z2, NEG_BIG)
                sp = _softplus2(z2)
                c = c_s[hd, rs, :]
                es.append(z2 + _twice(c))
                c_s[hd, rs, :] = c - jnp.sum(sp, axis=1, keepdims=True)
                sums.append(_dot(sp.astype(BF16), from_s))
            for (hd, rs), e, s in zip(chunks, es, sums):
                w_s[hd, rs, :] = jnp.exp2(e - s).astype(BF16)

        logits(i)
        block(i, True)

        @pl.loop(0, i)
        def _(jj):
            block(i - 1 - jj, False)

        flush(0)
        o_ref[...] = jnp.where(first, acc_s[0], acc_s[1]).astype(BF16)
        tot_ref[...] = jnp.where(first, c_s[0], c_s[1])

    return pl.pallas_call(
        body,
        name="sb_fwd",
        grid=(npair, T // tq),
        in_specs=[
            pl.BlockSpec((tq, LANES), lambda p, i: (i, p)),
            pl.BlockSpec((T, LANES), lambda p, i: (0, npair + p)),
            pl.BlockSpec((T, LANES), lambda p, i: (0, 2 * npair + p)),
        ],
        out_specs=[pl.BlockSpec((tq, LANES), lambda p, i: (i, p))] * 2,
        out_shape=[jax.ShapeDtypeStruct((T, D_MODEL), BF16), jax.ShapeDtypeStruct((T, D_MODEL), F32)],
        scratch_shapes=[
            pltpu.VMEM((2, tq, LANES), F32), pltpu.VMEM((2, tq, LANES), F32),
            pltpu.VMEM((2, tq, tq), F32), pltpu.VMEM((2, tq, tq), BF16),
        ],
        compiler_params=_cparams(),
    )(qkv, qkv, qkv)


def sb_bwd(qkv, do, tot):
    T = qkv.shape[0]
    tq = SB_BLOCK
    npair = SB_HEADS // 2
    nq = T // tq

    def body(q_ref, k_ref, v_ref, do_ref, tot_ref, dq_ref, dk_ref, dv_ref,
             dkt_s, dvt_s, dq_s, rest_s, cg_s, z_s, da_s, dz_s, a_s):
        i = pl.program_id(1)

        @pl.when(i == 0)
        def _():
            dkt_s[...] = jnp.zeros_like(dkt_s)
            dvt_s[...] = jnp.zeros_like(dvt_s)

        q = q_ref[...]
        do_ = do_ref[...]
        tot_ = tot_ref[...]
        q_t = q.astype(F32).T.astype(BF16)
        do_t = do_.astype(F32).T.astype(BF16)
        lane = lax.broadcasted_iota(jnp.int32, (tq, LANES), 1)
        first = lane < HEAD_DIM
        zero = jnp.zeros_like(q)
        q_heads = (jnp.where(first, q, zero), jnp.where(first, zero, q))
        do_heads = (jnp.where(first, do_, zero), jnp.where(first, zero, do_))
        row = lax.broadcasted_iota(jnp.int32, (tq, tq), 0)
        col = lax.broadcasted_iota(jnp.int32, (tq, tq), 1)
        strict = col < row
        before = (row < col).astype(BF16)
        from_s = (row >= col).astype(BF16)
        rest_s[0] = jnp.broadcast_to(tot_[:, 0:1], (tq, LANES))
        rest_s[1] = jnp.broadcast_to(tot_[:, HEAD_DIM:HEAD_DIM + 1], (tq, LANES))
        cg_s[...] = jnp.zeros_like(cg_s)
        dq_s[...] = jnp.zeros_like(dq_s)
        dz_s[...] = jnp.zeros_like(dz_s)
        a_s[...] = jnp.zeros_like(a_s)

        def rows(j):
            return pl.ds(pl.multiple_of(j * tq, tq), tq)

        def logits(j):
            kb = k_ref[rows(j), :]
            vb = v_ref[rows(j), :]
            for hd in range(2):
                z_s[hd] = _dot_nt(q_heads[hd], kb) * LOG2E
                da_s[hd] = _dot_nt(do_heads[hd], vb)

        def flush(j):
            kb = k_ref[rows(j), :]
            for hd in range(2):
                dims = slice(hd * HEAD_DIM, (hd + 1) * HEAD_DIM)
                dq_s[hd] += _dot(dz_s[hd], kb)
                dkt_s[j, dims, :] += _dot(q_t[dims, :], dz_s[hd])
                dvt_s[j, dims, :] += _dot(do_t[dims, :], a_s[hd])

        def block(j, diag):
            flush(jnp.maximum(j - 1, 0))
            chunks = [(hd, slice(r0, r0 + SB_CHUNK)) for hd in range(2) for r0 in range(0, tq, SB_CHUNK)]
            nxt = rows(jnp.minimum(j + 1, i))
            k_next = k_ref[nxt, :]
            v_next = v_ref[nxt, :]
            stage1 = []
            for hd, rs in chunks:
                z2 = z_s[hd, rs, :]
                z_s[hd, rs, :] = _dot_nt(q_heads[hd][rs, :], k_next) * LOG2E
                if diag:
                    z2 = jnp.where(strict[rs, :], z2, NEG_BIG)
                sp = _softplus2(z2)
                rest = rest_s[hd, rs, :] + jnp.sum(sp, axis=1, keepdims=True)
                rest_s[hd, rs, :] = rest
                stage1.append((z2 + _twice(rest), z2 - sp, _dot(sp.astype(BF16), from_s)))
            stage2 = []
            for (hd, rs), (e, log2_beta, ahead) in zip(chunks, stage1):
                a = jnp.exp2(e - ahead)
                g = a * da_s[hd, rs, :]
                da_s[hd, rs, :] = _dot_nt(do_heads[hd][rs, :], v_next)
                cg = cg_s[hd, rs, :]
                a_s[hd, rs, :] = a.astype(BF16)
                cg_s[hd, rs, :] = cg + jnp.sum(g, axis=1, keepdims=True)
                stage2.append((g, g + _twice(cg), log2_beta, _dot(g.astype(BF16), before)))
            for (hd, rs), (g, g_from, log2_beta, g_before) in zip(chunks, stage2):
                dz_s[hd, rs, :] = (g - jnp.exp2(log2_beta) * (g_from + g_before)).astype(BF16)

        logits(0)

        @pl.loop(0, i)
        def _(j):
            block(j, False)

        block(i, True)
        flush(i)
        dq_ref[...] = (jnp.where(first, dq_s[0], dq_s[1]) * ATTN_SCALE).astype(BF16)

        @pl.when(i == nq - 1)
        def _():
            @pl.loop(0, nq)
            def _(b):
                dk_ref[rows(b), :] = dkt_s[b].T.astype(BF16)
                dv_ref[rows(b), :] = dvt_s[b].T.astype(BF16)

    qblk = pl.BlockSpec((tq, LANES), lambda p, i: (i, p))
    full = pl.BlockSpec((T, LANES), lambda p, i: (0, p))
    return pl.pallas_call(
        body,
        name="sb_bwd",
        grid=(npair, nq),
        in_specs=[
            qblk,
            pl.BlockSpec((T, LANES), lambda p, i: (0, npair + p)),
            pl.BlockSpec((T, LANES), lambda p, i: (0, 2 * npair + p)),
            qblk, qblk,
        ],
        out_specs=[qblk, full, full],
        out_shape=[jax.ShapeDtypeStruct((T, D_MODEL), BF16)] * 3,
        scratch_shapes=[
            pltpu.VMEM((nq, LANES, tq), F32), pltpu.VMEM((nq, LANES, tq), F32),
            pltpu.VMEM((2, tq, LANES), F32), pltpu.VMEM((2, tq, LANES), F32), pltpu.VMEM((2, tq, LANES), F32),
            pltpu.VMEM((2, tq, tq), F32), pltpu.VMEM((2, tq, tq), F32),
            pltpu.VMEM((2, tq, tq), BF16), pltpu.VMEM((2, tq, tq), BF16),
        ],
        compiler_params=_cparams(),
    )(qkv, qkv, qkv, do, tot)


def _swa_valid(n):
    qi = lax.broadcasted_iota(jnp.int32, (WINDOW, 2 * WINDOW), 0)
    ki = lax.broadcasted_iota(jnp.int32, (WINDOW, 2 * WINDOW), 1)
    diff = qi + WINDOW - ki
    return (diff >= 0) & (diff < WINDOW) & ((n > 0) | (ki >= WINDOW))


def _to_half(x, first, src, dst):
    keep = first if src == 0 else jnp.logical_not(first)
    x = jnp.where(keep, x, jnp.zeros_like(x))
    if src != dst:
        x = pltpu.roll(x.astype(F32), HEAD_DIM, 1).astype(BF16)
    return x


def _kv_band(prev_ref, cur_ref, pb):
    cols = slice(pb * LANES, (pb + 1) * LANES)
    return jnp.concatenate([prev_ref[:, cols], cur_ref[:, cols]], axis=0)


def _swa_specs(T):
    nb = T // WINDOW
    kv_w = SWA_KV_HEADS * HEAD_DIM
    qrow = pl.BlockSpec((WINDOW, D_MODEL), lambda n: (n, 0))
    cur = pl.BlockSpec((WINDOW, kv_w), lambda n: (n, 0))
    prev = pl.BlockSpec((WINDOW, kv_w), lambda n: (jnp.maximum(n - 1, 0), 0))
    smem = pl.BlockSpec(memory_space=pltpu.SMEM)
    return nb, qrow, cur, prev, smem


def swa_fwd(q, k, v, sinks):
    T = q.shape[0]
    nb, qrow, cur, prev, smem = _swa_specs(T)

    def body(sink_ref, q_ref, kc_ref, kp_ref, vc_ref, vp_ref, o_ref, lse_ref):
        n = pl.program_id(0)
        lane = lax.broadcasted_iota(jnp.int32, (WINDOW, LANES), 1)
        first = lane < HEAD_DIM
        valid = _swa_valid(n)
        lse_acc = jnp.zeros((WINDOW, LANES), F32)
        for pb in range(SWA_KV_HEADS // 2):
            k2 = _kv_band(kp_ref, kc_ref, pb)
            v2 = _kv_band(vp_ref, vc_ref, pb)
            for b in range(2):
                kvh = 2 * pb + b
                for qq in range(2):
                    cols = slice((2 * kvh + qq) * LANES, (2 * kvh + qq + 1) * LANES)
                    qp = q_ref[:, cols]
                    outs = []
                    for a in range(2):
                        head = 4 * kvh + 2 * qq + a
                        qh = _to_half(qp, first, a, b)
                        s = jnp.where(valid, _dot_nt(qh, k2), NEG_BIG)
                        sink = sink_ref[head]
                        m = jnp.maximum(jnp.max(s, axis=1, keepdims=True), sink)
                        p = jnp.exp(s - m)
                        den = jnp.sum(p, axis=1, keepdims=True) + jnp.exp(sink - m)
                        o = _dot((p / den).astype(BF16), v2)
                        if a != b:
                            o = pltpu.roll(o, HEAD_DIM, 1)
                        outs.append(o)
                        lse_acc = jnp.where(lane == head, m + jnp.log(den), lse_acc)
                    o_ref[:, cols] = jnp.where(first, outs[0], outs[1]).astype(BF16)
        lse_ref[...] = lse_acc

    return pl.pallas_call(
        body,
        name="swa_fwd",
        grid=(nb,),
        in_specs=[smem, qrow, cur, prev, cur, prev],
        out_specs=[qrow, pl.BlockSpec((WINDOW, LANES), lambda n: (n, 0))],
        out_shape=[jax.ShapeDtypeStruct((T, D_MODEL), BF16), jax.ShapeDtypeStruct((T, LANES), F32)],
        compiler_params=_cparams(),
    )(sinks, q, k, k, v, v)


def swa_bwd(q, k, v, sinks, do, lse, cos, sin):
    T = q.shape[0]
    nb, qrow, cur, prev, smem = _swa_specs(T)
    kv_w = SWA_KV_HEADS * HEAD_DIM

    def body(sink_ref, q_ref, kc_ref, kp_ref, vc_ref, vp_ref, do_ref, lse_ref, cos_ref, sin_ref,
             dq_ref, own_ref, prv_ref, dsink_ref):
        n = pl.program_id(0)

        @pl.when(n == 0)
        def _():
            dsink_ref[...] = jnp.zeros_like(dsink_ref)

        lane = lax.broadcasted_iota(jnp.int32, (WINDOW, LANES), 1)
        lane1 = lax.broadcasted_iota(jnp.int32, (1, LANES), 1)
        first = lane < HEAD_DIM
        valid = _swa_valid(n)
        cos_ = cos_ref[...]
        sin_ = sin_ref[...]
        dsink = jnp.zeros((1, LANES), F32)
        for pb in range(SWA_KV_HEADS // 2):
            k2 = _kv_band(kp_ref, kc_ref, pb)
            v2 = _kv_band(vp_ref, vc_ref, pb)
            dk2 = jnp.zeros((2 * WINDOW, LANES), F32)
            dv2 = jnp.zeros((2 * WINDOW, LANES), F32)
            for b in range(2):
                kvh = 2 * pb + b
                for qq in range(2):
                    cols = slice((2 * kvh + qq) * LANES, (2 * kvh + qq + 1) * LANES)
                    qp = q_ref[:, cols]
                    dop = do_ref[:, cols]
                    dqs = []
                    for a in range(2):
                        head = 4 * kvh + 2 * qq + a
                        qh = _to_half(qp, first, a, b)
                        doh = _to_half(dop, first, a, b)
                        s = jnp.where(valid, _dot_nt(qh, k2), NEG_BIG)
                        lse_h = lse_ref[:, head:head + 1]
                        p = jnp.exp(s - lse_h)
                        dp = _dot_nt(doh, v2)
                        delta = jnp.sum(p * dp, axis=1, keepdims=True)
                        ds = (p * (dp - delta)).astype(BF16)
                        p_sink = jnp.exp(sink_ref[head] - lse_h)
                        dsink = dsink + jnp.where(lane1 == head, -jnp.sum(p_sink * delta, axis=0, keepdims=True), 0.0)
                        dq = _dot(ds, k2)
                        if a != b:
                            dq = pltpu.roll(dq, HEAD_DIM, 1)
                        dqs.append(dq)
                        dk2 = dk2 + _dot_tn(ds, qh)
                        dv2 = dv2 + _dot_tn(p.astype(BF16), doh)
                    dqp = jnp.where(first, dqs[0], dqs[1])
                    dq_ref[:, cols] = ((dqp * cos_ + _swap32(dqp * sin_)) * ATTN_SCALE).astype(BF16)
            kcols = slice(pb * LANES, (pb + 1) * LANES)
            vcols = slice(kv_w + pb * LANES, kv_w + (pb + 1) * LANES)
            prv_ref[:, kcols] = dk2[:WINDOW]
            own_ref[:, kcols] = dk2[WINDOW:]
            prv_ref[:, vcols] = dv2[:WINDOW]
            own_ref[:, vcols] = dv2[WINDOW:]
        dsink_ref[...] += dsink

    tab = pl.BlockSpec((WINDOW, LANES), lambda n: (n, 0))
    kvrow = pl.BlockSpec((WINDOW, 2 * kv_w), lambda n: (n, 0))
    return pl.pallas_call(
        body,
        name="swa_bwd",
        grid=(nb,),
        in_specs=[smem, qrow, cur, prev, cur, prev, qrow, tab, tab, tab],
        out_specs=[qrow, kvrow, kvrow, pl.BlockSpec((1, LANES), lambda n: (0, 0))],
        out_shape=[
            jax.ShapeDtypeStruct((T, D_MODEL), BF16),
            jax.ShapeDtypeStruct((T, 2 * kv_w), F32),
            jax.ShapeDtypeStruct((T, 2 * kv_w), F32),
            jax.ShapeDtypeStruct((1, LANES), F32),
        ],
        compiler_params=_cparams(),
    )(sinks, q, k, k, v, v, do, lse, cos, sin)


def kv_grad_combine(own, prv, cos, sin):
    T = own.shape[0]
    nb = T // WINDOW
    kv_w = SWA_KV_HEADS * HEAD_DIM

    def body(own_ref, nxt_ref, cos_ref, sin_ref, out_ref):
        n = pl.program_id(0)
        nxt = jnp.where(n + 1 < nb, nxt_ref[...], 0.0)
        tot = own_ref[...] + nxt
        dk = tot[:, :kv_w]
        c = _tile_lanes(cos_ref[...], kv_w)
        s = _tile_lanes(sin_ref[...], kv_w)
        out_ref[:, :kv_w] = (dk * c + _swap32(dk * s)).astype(BF16)
        out_ref[:, kv_w:] = tot[:, kv_w:].astype(BF16)

    tab = pl.BlockSpec((WINDOW, LANES), lambda n: (n, 0))
    kvrow = pl.BlockSpec((WINDOW, 2 * kv_w), lambda n: (n, 0))
    return pl.pallas_call(
        body,
        name="kv_grad_combine",
        grid=(nb,),
        in_specs=[kvrow, pl.BlockSpec((WINDOW, 2 * kv_w), lambda n: (jnp.minimum(n + 1, nb - 1), 0)), tab, tab],
        out_specs=kvrow,
        out_shape=jax.ShapeDtypeStruct((T, 2 * kv_w), BF16),
        compiler_params=_cparams(),
    )(own, prv, cos, sin)


ANY = pl.BlockSpec(memory_space=pl.ANY)


def _place():
    x, y, c = lax.axis_index("x"), lax.axis_index("y"), lax.axis_index("c")
    other_chips = [(1 - x, y), (x, 1 - y), (1 - x, 1 - y)]
    return x, y, c, 2 * x + y, other_chips


def all_gather_weights(shards):
    n = len(shards)

    def body(*refs):
        ins, outs = refs[:n], refs[n:2 * n]
        send_sems, recv_sems, local_sems = refs[2 * n:]
        _, _, c, me, chips = _place()
        copies = []
        for t in range(n):
            cp = pltpu.make_async_copy(ins[t], outs[t].at[me], local_sems.at[t])
            cp.start()
            copies.append(cp)
            for jdx, (px, py) in enumerate(chips):
                cp = pltpu.make_async_remote_copy(
                    src_ref=ins[t], dst_ref=outs[t].at[me], send_sem=send_sems.at[t, jdx],
                    recv_sem=recv_sems.at[t, jdx], device_id=(px, py, c), device_id_type=MESH)
                cp.start()
                copies.append(cp)
        for cp in copies:
            cp.wait()

    return pl.pallas_call(
        body,
        name="all_gather_weights",
        in_specs=[ANY] * n,
        out_specs=[ANY] * n,
        out_shape=[jax.ShapeDtypeStruct((N_CHIPS,) + s.shape, s.dtype) for s in shards],
        scratch_shapes=[
            pltpu.SemaphoreType.DMA((n, 3)), pltpu.SemaphoreType.DMA((n, 3)), pltpu.SemaphoreType.DMA((n,)),
        ],
    )(*shards)


HBM = pl.BlockSpec(memory_space=pltpu.HBM)
SEM = pl.BlockSpec(memory_space=pltpu.SEMAPHORE)
N_PEER_CHIPS = N_CHIPS - 1


def place_own_shard(name, shard, chip):
    nl, r, c = shard.shape

    def body(chip_ref, s_ref, o_ref):
        o_ref[...] = s_ref[...]

    return pl.pallas_call(
        body, name=name,
        grid_spec=pltpu.PrefetchScalarGridSpec(
            num_scalar_prefetch=1, grid=(nl,),
            in_specs=[pl.BlockSpec((None, r, c), lambda l, chip_ref: (l, 0, 0))],
            out_specs=pl.BlockSpec((None, None, r, c), lambda l, chip_ref: (chip_ref[0], l, 0, 0))),
        out_shape=jax.ShapeDtypeStruct((N_CHIPS,) + shard.shape, shard.dtype), compiler_params=_cparams(),
    )(chip, shard)


def _gather_copy(src, land, send_sem, recv_sem, jdx):
    _, _, c, me, chips = _place()
    px, py = chips[jdx]
    return pltpu.make_async_remote_copy(src_ref=src, dst_ref=land.at[me], send_sem=send_sem, recv_sem=recv_sem,
                                        device_id=(px, py, c), device_id_type=MESH)


def gather_start(name, shards, lands):
    n = len(shards)
    ncopy = n * N_PEER_CHIPS

    def body(*refs):
        ins, lnd = refs[:n], refs[n:2 * n]
        sems = refs[2 * n:2 * n + 2 * ncopy]
        token = refs[-1]
        for t in range(n):
            for jdx in range(N_PEER_CHIPS):
                k = t * N_PEER_CHIPS + jdx
                _gather_copy(ins[t], lnd[t], sems[k], sems[ncopy + k], jdx).start()
        token[...] = jnp.zeros_like(token)

    res = pl.pallas_call(
        body, name=name,
        in_specs=[HBM] * (2 * n),
        out_specs=[SEM] * (2 * ncopy) + [HBM] * (2 * n) + [pl.BlockSpec(memory_space=pltpu.VMEM)],
        out_shape=[pltpu.SemaphoreType.DMA(())] * (2 * ncopy)
        + [pltpu.HBM(s.shape, s.dtype) for s in shards] + [pltpu.HBM(l.shape, l.dtype) for l in lands]
        + [jax.ShapeDtypeStruct((8, LANES), F32)],
        input_output_aliases={t: 2 * ncopy + t for t in range(2 * n)},
        compiler_params=pltpu.CompilerParams(has_side_effects=pltpu.SideEffectType.DATAFLOW_SIDE_EFFECTING),
    )(*[pltpu.with_memory_space_constraint(a, pltpu.HBM) for a in list(shards) + list(lands)])
    sems = res[:2 * ncopy]
    return sems, res[2 * ncopy:2 * ncopy + n], res[2 * ncopy + n:2 * ncopy + 2 * n], res[-1]


def gather_wait(name, sems, shards, lands, after):
    n = len(shards)
    ncopy = n * N_PEER_CHIPS

    def body(*refs):
        ins, lnd = refs[:n], refs[n:2 * n]
        sm = refs[2 * n:2 * n + 2 * ncopy]
        for t in range(n):
            for jdx in range(N_PEER_CHIPS):
                k = t * N_PEER_CHIPS + jdx
                cp = _gather_copy(ins[t], lnd[t], sm[k], sm[ncopy + k], jdx)
                cp.wait_send()
                cp.wait_recv()

    res = pl.pallas_call(
        body, name=name,
        in_specs=[HBM] * (2 * n) + [SEM] * (2 * ncopy) + [ANY],
        out_specs=[HBM] * (2 * n),
        out_shape=[pltpu.HBM(s.shape, s.dtype) for s in shards] + [pltpu.HBM(l.shape, l.dtype) for l in lands],
        input_output_aliases={t: t for t in range(2 * n)},
        compiler_params=pltpu.CompilerParams(has_side_effects=pltpu.SideEffectType.DATAFLOW_SIDE_EFFECTING),
    )(*shards, *lands, *sems, after)
    return res[n:]


def exchange_halves(slabs):
    n = len(slabs)

    def body(*refs):
        ins, theirs = refs[:n], refs[n:2 * n]
        send_sems, recv_sems = refs[2 * n:]
        x, y, c, _, _ = _place()
        copies = []
        for t in range(n):
            cp = pltpu.make_async_remote_copy(
                src_ref=ins[t].at[1 - c], dst_ref=theirs[t], send_sem=send_sems.at[t],
                recv_sem=recv_sems.at[t], device_id=(x, y, 1 - c), device_id_type=MESH)
            cp.start()
            copies.append(cp)
        for cp in copies:
            cp.wait()

    return pl.pallas_call(
        body,
        name="exchange_halves",
        in_specs=[ANY] * n,
        out_specs=[ANY] * n,
        out_shape=[jax.ShapeDtypeStruct(s.shape[1:], s.dtype) for s in slabs],
        scratch_shapes=[pltpu.SemaphoreType.DMA((n,)), pltpu.SemaphoreType.DMA((n,))],
    )(*slabs)


def exchange_chip_partials(parts):
    n = len(parts)

    def body(*refs):
        ins, outs = refs[:n], refs[n:2 * n]
        send_sems, recv_sems = refs[2 * n:]
        _, _, c, me, chips = _place()
        copies = []
        for t in range(n):
            for jdx, (px, py) in enumerate(chips):
                cp = pltpu.make_async_remote_copy(
                    src_ref=ins[t].at[2 * px + py], dst_ref=outs[t].at[me], send_sem=send_sems.at[t, jdx],
                    recv_sem=recv_sems.at[t, jdx], device_id=(px, py, c), device_id_type=MESH)
                cp.start()
                copies.append(cp)
        for cp in copies:
            cp.wait()

    return pl.pallas_call(
        body,
        name="exchange_chip_partials",
        in_specs=[ANY] * n,
        out_specs=[ANY] * n,
        out_shape=[jax.ShapeDtypeStruct(p.shape, p.dtype) for p in parts],
        scratch_shapes=[pltpu.SemaphoreType.DMA((n, 3)), pltpu.SemaphoreType.DMA((n, 3))],
    )(*parts)


def share_reduced_halves(halves):
    n = len(halves)

    def body(*refs):
        ins, outs = refs[:n], refs[n:2 * n]
        send_sems, recv_sems = refs[2 * n:]
        x, y, c, _, _ = _place()
        copies = []
        for t in range(n):
            cp = pltpu.make_async_remote_copy(
                src_ref=ins[t], dst_ref=outs[t], send_sem=send_sems.at[t],
                recv_sem=recv_sems.at[t], device_id=(x, y, 1 - c), device_id_type=MESH)
            cp.start()
            copies.append(cp)
        for cp in copies:
            cp.wait()

    return pl.pallas_call(
        body,
        name="share_reduced_halves",
        in_specs=[ANY] * n,
        out_specs=[ANY] * n,
        out_shape=[jax.ShapeDtypeStruct(h.shape, h.dtype) for h in halves],
        scratch_shapes=[pltpu.SemaphoreType.DMA((n,)), pltpu.SemaphoreType.DMA((n,))],
    )(*halves)


def _row_tile(r, c):
    tr = r
    while tr * c * 4 > (3 << 19) and tr % 16 == 0:
        tr //= 2
    return tr


def add_sibling(name, slab, theirs, core):
    _, ns, slots, r, c = slab.shape
    tr = _row_tile(r, c)

    def body(core_ref, a_ref, b_ref, o_ref):
        o_ref[...] = (a_ref[...] + b_ref[...]).astype(BF16)

    blk = pl.BlockSpec((None, None, tr, c), lambda s, l, i, core_ref: (s, l, i, 0))
    return pl.pallas_call(
        body, name=name,
        grid_spec=pltpu.PrefetchScalarGridSpec(
            num_scalar_prefetch=1, grid=(ns, slots, r // tr),
            in_specs=[pl.BlockSpec((None, None, None, tr, c), lambda s, l, i, core_ref: (core_ref[0], s, l, i, 0)), blk],
            out_specs=blk),
        out_shape=jax.ShapeDtypeStruct(theirs.shape, BF16), compiler_params=_cparams(),
    )(core, slab, theirs)


def sum_chips(name, recv, own, chip):
    _, slots, r, c = recv.shape
    tr = _row_tile(r, c)

    def body(chip_ref, r0, r1, r2, r3, own_ref, o_ref):
        me = chip_ref[0]
        mine = own_ref[...]
        terms = [jnp.where(me == s, mine, rr[...]).astype(F32) for s, rr in enumerate((r0, r1, r2, r3))]
        o_ref[...] = ((terms[0] + terms[1]) + terms[2]) + terms[3]

    def src(s):
        return pl.BlockSpec((None, None, tr, c),
                            lambda l, i, chip_ref: (jnp.where(chip_ref[0] == s, (s + 1) % N_CHIPS, s), l, i, 0))

    return pl.pallas_call(
        body, name=name,
        grid_spec=pltpu.PrefetchScalarGridSpec(
            num_scalar_prefetch=1, grid=(slots, r // tr),
            in_specs=[src(0), src(1), src(2), src(3),
                      pl.BlockSpec((None, None, tr, c), lambda l, i, chip_ref: (chip_ref[0], l, i, 0))],
            out_specs=pl.BlockSpec((None, tr, c), lambda l, i, chip_ref: (l, i, 0))),
        out_shape=jax.ShapeDtypeStruct((slots, r, c), F32), compiler_params=_cparams(),
    )(chip, recv, recv, recv, recv, own)


def _adamw_math(w, g, m, v):
    m = ADAM_B1 * m + (1.0 - ADAM_B1) * g
    v = ADAM_B2 * v + (1.0 - ADAM_B2) * (g * g)
    m_hat = m / (1.0 - ADAM_B1 ** ADAM_STEP)
    v_hat = v / (1.0 - ADAM_B2 ** ADAM_STEP)
    delta = -ADAM_LR * (m_hat / (jnp.sqrt(v_hat) + ADAM_EPS) + ADAM_WD * w)
    return delta, m, v


def adamw_shard(name, w, m, v, g_own, g_sib, core, slot0, row_halves):
    n = w.shape[0]
    _, r, c = g_own.shape
    tr = _row_tile(r, c)
    nr = r // tr

    def body(core_ref, w_ref, m_ref, v_ref, own_ref, sib_ref, go_ref, d_ref, mo_ref, vo_ref):
        g = jnp.where(pl.program_id(1) == core_ref[0], own_ref[...], sib_ref[...])
        delta, mm, vv = _adamw_math(w_ref[...], g, m_ref[...], v_ref[...])
        go_ref[...] = g
        d_ref[...] = delta
        mo_ref[...] = mm
        vo_ref[...] = vv

    if row_halves:
        wspec = pl.BlockSpec((None, tr, c), lambda l, h, i, core_ref: (l, h * nr + i, 0))
    else:
        wspec = pl.BlockSpec((None, tr, c), lambda l, h, i, core_ref: (l, i, h))
    gspec = pl.BlockSpec((None, tr, c), lambda l, h, i, core_ref: (slot0 + l, i, 0))
    shp = jax.ShapeDtypeStruct(w.shape, F32)
    return pl.pallas_call(
        body, name=name,
        grid_spec=pltpu.PrefetchScalarGridSpec(
            num_scalar_prefetch=1, grid=(n, 2, nr),
            in_specs=[wspec, wspec, wspec, gspec, gspec], out_specs=[wspec] * 4),
        out_shape=[shp] * 4, compiler_params=_cparams(),
    )(core, w, m, v, g_own, g_sib)


SMALL_ROWS = 16


def small_allreduce_adamw(part, w, m, v):
    def body(p_ref, w_ref, m_ref, v_ref, g_ref, d_ref, mo_ref, vo_ref, buf, send_sems, recv_sems):
        x, y, c, _, _ = _place()
        me = 4 * x + 2 * y + c
        buf[me] = p_ref[...]
        copies = []
        for k in range(1, N_DEV):
            kx, ky, kc = (k >> 2) & 1, (k >> 1) & 1, k & 1
            peer = (x ^ kx, y ^ ky, c ^ kc)
            cp = pltpu.make_async_remote_copy(
                src_ref=p_ref, dst_ref=buf.at[me], send_sem=send_sems.at[k - 1],
                recv_sem=recv_sems.at[k - 1], device_id=peer, device_id_type=MESH)
            cp.start()
            copies.append(cp)
        for cp in copies:
            cp.wait()
        g = buf[0]
        for dev in range(1, N_DEV):
            g = g + buf[dev]
        delta, mm, vv = _adamw_math(w_ref[...], g, m_ref[...], v_ref[...])
        g_ref[...] = g
        d_ref[...] = delta
        mo_ref[...] = mm
        vo_ref[...] = vv

    vm = pl.BlockSpec(memory_space=pltpu.VMEM)
    shp = jax.ShapeDtypeStruct(part.shape, F32)
    return pl.pallas_call(
        body, name="small_allreduce_adamw",
        in_specs=[vm] * 4, out_specs=[vm] * 4, out_shape=[shp] * 4,
        scratch_shapes=[
            pltpu.VMEM((N_DEV,) + part.shape, F32),
            pltpu.SemaphoreType.DMA((N_DEV - 1,)), pltpu.SemaphoreType.DMA((N_DEV - 1,)),
        ],
    )(part, w, m, v)


def _rope_tables(T):
    half = HEAD_DIM // 2
    inv_freq = ROPE_THETA ** (-jnp.arange(half, dtype=F32) / half)
    ang = jnp.arange(T).astype(F32)[:, None] * inv_freq[None, :]
    cos = jnp.tile(jnp.cos(ang), (1, LANES // half))
    sin = jnp.tile(jnp.sin(ang), (1, LANES // half))
    lane = jnp.arange(LANES)
    sign = jnp.where((lane % HEAD_DIM) < half, -1.0, 1.0).astype(F32)
    return cos, sin * sign[None, :]


def _pack_small(ffn1, mix, ffn2, kvn, fin, sinks, loss_row):
    sink_row = jnp.pad(sinks.reshape(1, SWA_Q_HEADS), ((0, 0), (0, D_MODEL - SWA_Q_HEADS)))
    rows = jnp.concatenate([ffn1, mix, ffn2, kvn.reshape(1, -1), fin.reshape(1, -1), sink_row, loss_row], axis=0)
    return jnp.concatenate([rows, jnp.zeros((SMALL_ROWS - rows.shape[0], D_MODEL), F32)], axis=0)


def kernel(x, ffn1_norm, ffn1_w_in, ffn1_w_out, mix_norm, ffn2_norm, ffn2_w_in, ffn2_w_out, sb_w_qkv, sb_w_o, kv_norm, kv_w, swa_w_q, swa_sinks, swa_w_o, final_norm, loss_target, m_ffn1_norm, m_ffn1_w_in, m_ffn1_w_out, m_mix_norm, m_ffn2_norm, m_ffn2_w_in, m_ffn2_w_out, m_sb_w_qkv, m_sb_w_o, m_kv_norm, m_kv_w, m_swa_w_q, m_swa_sinks, m_swa_w_o, m_final_norm, v_ffn1_norm, v_ffn1_w_in, v_ffn1_w_out, v_mix_norm, v_ffn2_norm, v_ffn2_w_in, v_ffn2_w_out, v_sb_w_qkv, v_sb_w_o, v_kv_norm, v_kv_w, v_swa_w_q, v_swa_sinks, v_swa_w_o, v_final_norm):
    T = x.shape[1]
    kv_cols = SWA_KV_HEADS * HEAD_DIM
    x2 = x.reshape(T, D_MODEL)
    tgt = loss_target.reshape(T, D_MODEL)
    cos, sin = _rope_tables(T)

    w_in_l = jnp.concatenate([ffn1_w_in, ffn2_w_in], axis=0).astype(BF16)
    w_out_l = jnp.concatenate([ffn1_w_out, ffn2_w_out], axis=0).astype(BF16)
    sq_l = jnp.concatenate([sb_w_o, swa_w_q, swa_w_o], axis=0).astype(BF16)
    qkv_l = sb_w_qkv[0].astype(BF16)
    kvw_l = kv_w.astype(BF16)
    core = lax.axis_index("c").astype(jnp.int32).reshape(1)
    chip = (2 * lax.axis_index("x") + lax.axis_index("y")).astype(jnp.int32).reshape(1)
    shards = [w_in_l[:1], w_out_l[:1], qkv_l[None], sq_l, w_in_l[1:], w_out_l[1:], kvw_l[None]]
    lands = [place_own_shard(f"own_shard_{t}", s, chip) for t, s in enumerate(shards)]
    sems, shards, lands, token = gather_start("gather_start", shards, lands)
    ncopy = len(shards) * N_PEER_CHIPS

    def gathered(name, t0, t1, after):
        k0, k1 = t0 * N_PEER_CHIPS, t1 * N_PEER_CHIPS
        return gather_wait(name, list(sems[k0:k1]) + list(sems[ncopy + k0:ncopy + k1]),
                           shards[t0:t1], lands[t0:t1], after)

    w_in0, w_out0 = gathered("gather_wait_ffn0", 0, 2, token)

    def ffn_w(slot):
        return (w_in0, w_out0, 0) if slot == 0 else (w_in_r, w_out_r, slot - 1)

    def vec(a, i):
        return a[i].reshape(1, D_MODEL)

    ident = lambda w: w
    sq_prep = lambda w: w.reshape(D_MODEL, w.shape[-1])
    qscale = jnp.concatenate([jnp.full((1, D_MODEL), ATTN_SCALE, F32), jnp.ones((1, 2 * D_MODEL), F32)], axis=1)
    swa_scale = jnp.full((1, D_MODEL), ATTN_SCALE, F32)
    sinks = swa_sinks.reshape(SWA_Q_HEADS)

    h1, gate1, up1 = ffn_fwd("l0a", x2, vec(ffn1_norm, 0), *ffn_w(SLOT_FFN1[0]))
    w_qkv, w_sq = gathered("gather_wait_attn", 2, 4, h1)
    w_qkv = w_qkv.reshape(N_CHIPS, D_MODEL, QKV_COLS)
    qkv = rms_linear("sb_qkv", h1, vec(mix_norm, 0), w_qkv,
                     pl.BlockSpec((None, D_MODEL, QKV_COLS), lambda i, j: (j, 0, 0)), ident,
                     3 * D_MODEL, QKV_COLS, scale=qscale)
    o_sb, tot = sb_fwd(qkv)
    w_in_r, w_out_r, w_kv = gathered("gather_wait_rest", 4, 7, o_sb)
    w_kv = w_kv.reshape(D_MODEL, 2 * kv_cols)
    h2 = linear_res("sb_out", o_sb, w_sq, SQ_SB_O, h1)
    h3, gate2, up2 = ffn_fwd("l0b", h2, vec(ffn2_norm, 0), *ffn_w(SLOT_FFN2[0]))
    kvn = kv_norm.reshape(1, D_MODEL)
    k_sw = rms_linear("kv_k", h3, kvn, w_kv, pl.BlockSpec((D_MODEL, kv_cols), lambda i, j: (0, 0)), ident,
                      kv_cols, kv_cols, rope=(cos, sin))
    v_sw = rms_linear("kv_v", h3, kvn, w_kv, pl.BlockSpec((D_MODEL, kv_cols), lambda i, j: (0, 1)), ident,
                      kv_cols, kv_cols)
    h4, gate3, up3 = ffn_fwd("l1a", h3, vec(ffn1_norm, 1), *ffn_w(SLOT_FFN1[1]))
    q_sw = rms_linear("swa_q", h4, vec(mix_norm, 1), w_sq,
                      pl.BlockSpec((N_CHIPS, None, SQ_ROWS, 512), lambda i, j: (0, SQ_SWA_Q, 0, j)), sq_prep,
                      D_MODEL, 512, rope=(cos, sin), scale=swa_scale)
    o_sw, lse = swa_fwd(q_sw, k_sw, v_sw, sinks)
    h5 = linear_res("swa_out", o_sw, w_sq, SQ_SWA_O, h4)
    h6, gate4, up4 = ffn_fwd("l1b", h5, vec(ffn2_norm, 1), *ffn_w(SLOT_FFN2[1]))
    dh6, loss_p, d_final = loss_bwd(h6, final_norm.reshape(1, D_MODEL), tgt)

    slab = {"in": None, "out": None, "sq": None}
    in_shape = (2, N_CHIPS, 4, D_MODEL // 2, FF_CHUNK)
    out_shape = (2, N_CHIPS, 4, FF_ROWS, D_MODEL // 2)
    sq_shape = (2, N_CHIPS, 3, SQ_ROWS, D_MODEL // 2)

    def ffn_grads(tag, dh, h_in, g, gate, up, slot):
        dh_in, xn, dg_, du_, act, dhb, dnorm = ffn_bwd(tag, dh, h_in, g, gate, up, *ffn_w(slot))
        blk = (None, 1, None, D_MODEL // 2, FF_CHUNK)
        slab["in"] = mm_tn(f"dw_gate_{tag}", xn, dg_, D_MODEL // 2, FF_CHUNK, blk,
                           lambda k, n: (k, n, slot, 0, 0), in_shape, prev=slab["in"])
        slab["in"] = mm_tn(f"dw_up_{tag}", xn, du_, D_MODEL // 2, FF_CHUNK, blk,
                           lambda k, n: (k, 2 + n, slot, 0, 0), in_shape, prev=slab["in"])
        slab["out"] = mm_tn(f"dw_out_{tag}", act, dhb, FF_CHUNK, D_MODEL // 2,
                            (None, 2, None, FF_ROWS, D_MODEL // 2),
                            lambda k, n: (n, k, slot, 0, 0), out_shape, prev=slab["out"])
        return dh_in, dnorm

    def sq_grad(tag, a, dyb, t):
        slab["sq"] = mm_tn(f"dw_sq_{tag}", a, dyb, D_MODEL, D_MODEL // 2,
                           (None, N_CHIPS, None, SQ_ROWS, D_MODEL // 2),
                           lambda k, n: (n, 0, t, 0, 0), sq_shape, prev=slab["sq"])

    dh5, d_ffn2_1 = ffn_grads("l1b", dh6, h5, vec(ffn2_norm, 1), gate4, up4, SLOT_FFN2[1])
    do_sw, dh5b = linear_bwd_plain("swa_out_bwd", dh5, w_sq, SQ_SWA_O)
    sq_grad("swa_o", o_sw, dh5b, SQ_SWA_O)
    dq_sw, kv_own, kv_prev, d_sinks = swa_bwd(q_sw, k_sw, v_sw, sinks, do_sw, lse, cos, sin)
    sq_w_spec = pl.BlockSpec((N_CHIPS, None, SQ_ROWS, D_MODEL), lambda i, j: (0, SQ_SWA_Q, 0, 0))
    dh4, hn4, d_mix_1 = linear_bwd_rms("swa_q_bwd", [(dq_sw, w_sq, sq_w_spec, sq_prep)], h4, vec(mix_norm, 1), dh5,
                                       1, D_MODEL)
    sq_grad("swa_q", hn4, dq_sw, SQ_SWA_Q)
    dh3a, d_ffn1_1 = ffn_grads("l1a", dh4, h3, vec(ffn1_norm, 1), gate3, up3, SLOT_FFN1[1])
    dkv = kv_grad_combine(kv_own, kv_prev, cos, sin)
    kv_w_spec = pl.BlockSpec((D_MODEL, 2 * kv_cols), lambda i, j: (0, 0))
    dh3, xn3, d_kvn = linear_bwd_rms("kv_bwd", [(dkv, w_kv, kv_w_spec, ident)], h3, kvn, dh3a, 1, 2 * kv_cols)
    slab_kv = mm_tn("dw_kv", xn3, dkv, D_MODEL, kv_cols, (None, N_CHIPS, None, SQ_ROWS, kv_cols),
                    lambda k, n: (n, 0, 0, 0, 0), (2, N_CHIPS, 1, SQ_ROWS, kv_cols))
    dh2, d_ffn2_0 = ffn_grads("l0b", dh3, h2, vec(ffn2_norm, 0), gate2, up2, SLOT_FFN2[0])
    do_sb, dh2b = linear_bwd_plain("sb_out_bwd", dh2, w_sq, SQ_SB_O)
    sq_grad("sb_o", o_sb, dh2b, SQ_SB_O)
    dq_sb, dk_sb, dv_sb = sb_bwd(qkv, do_sb, tot)
    dqkv = jnp.concatenate([dq_sb, dk_sb, dv_sb], axis=1)
    qkv_w_spec = pl.BlockSpec((None, D_MODEL, QKV_COLS), lambda i, j: (j, 0, 0))
    dh1, hn1, d_mix_0 = linear_bwd_rms("sb_qkv_bwd", [(dqkv, w_qkv, qkv_w_spec, ident)], h1, vec(mix_norm, 0), dh2,
                                       N_CHIPS, QKV_COLS)
    slab_qkv = mm_tn("dw_qkv", hn1, dqkv, D_MODEL // 2, QKV_COLS, (None, 1, None, D_MODEL // 2, QKV_COLS),
                     lambda k, n: (k, n, 0, 0, 0), (2, N_CHIPS, 1, D_MODEL // 2, QKV_COLS))
    dx, d_ffn1_0 = ffn_grads("l0a", dh1, x2, vec(ffn1_norm, 0), gate1, up1, SLOT_FFN1[0])

    slabs = [slab["in"], slab["out"], slab["sq"], slab_qkv, slab_kv]
    names = ["in", "out", "sq", "qkv", "kv"]
    theirs = exchange_halves(slabs)
    parts = [add_sibling(f"add_sibling_{nm}", s, t, core) for nm, s, t in zip(names, slabs, theirs)]
    gathered = exchange_chip_partials(parts)
    halves = [sum_chips(f"sum_chips_{nm}", g, p, chip) for nm, g, p in zip(names, gathered, parts)]
    sib_halves = share_reduced_halves(halves)
    g_in, g_out, g_sq, g_qkv, g_kv = zip(halves, sib_halves)

    def upd(name, w, m, v, g_pair, slot0, row_halves):
        shp = w.shape
        w3 = w.reshape((-1,) + shp[-2:])
        outs = adamw_shard(name, w3, m.reshape(w3.shape), v.reshape(w3.shape), g_pair[0], g_pair[1], core,
                           slot0, row_halves)
        return [o.reshape(shp) for o in outs]

    r_ffn1_in = upd("adamw_ffn1_in", ffn1_w_in, m_ffn1_w_in, v_ffn1_w_in, g_in, 0, True)
    r_ffn2_in = upd("adamw_ffn2_in", ffn2_w_in, m_ffn2_w_in, v_ffn2_w_in, g_in, 2, True)
    r_ffn1_out = upd("adamw_ffn1_out", ffn1_w_out, m_ffn1_w_out, v_ffn1_w_out, g_out, 0, False)
    r_ffn2_out = upd("adamw_ffn2_out", ffn2_w_out, m_ffn2_w_out, v_ffn2_w_out, g_out, 2, False)
    r_qkv = upd("adamw_qkv", sb_w_qkv, m_sb_w_qkv, v_sb_w_qkv, g_qkv, 0, True)
    r_sb_o = upd("adamw_sb_o", sb_w_o, m_sb_w_o, v_sb_w_o, g_sq, SQ_SB_O, False)
    r_swa_q = upd("adamw_swa_q", swa_w_q, m_swa_w_q, v_swa_w_q, g_sq, SQ_SWA_Q, False)
    r_swa_o = upd("adamw_swa_o", swa_w_o, m_swa_w_o, v_swa_w_o, g_sq, SQ_SWA_O, False)
    r_kv = upd("adamw_kv", kv_w, m_kv_w, v_kv_w, g_kv, 0, False)

    loss_row = jnp.pad(loss_p, ((0, 0), (0, D_MODEL - LANES)))
    d_sink_row = d_sinks[0, :SWA_Q_HEADS]
    part = _pack_small(jnp.concatenate([d_ffn1_0, d_ffn1_1], axis=0), jnp.concatenate([d_mix_0, d_mix_1], axis=0),
                       jnp.concatenate([d_ffn2_0, d_ffn2_1], axis=0), d_kvn, d_final, d_sink_row, loss_row)
    zrow = jnp.zeros((1, D_MODEL), F32)
    small = small_allreduce_adamw(
        part,
        _pack_small(ffn1_norm, mix_norm, ffn2_norm, kv_norm, final_norm, swa_sinks, zrow),
        _pack_small(m_ffn1_norm, m_mix_norm, m_ffn2_norm, m_kv_norm, m_final_norm, m_swa_sinks, zrow),
        _pack_small(v_ffn1_norm, v_mix_norm, v_ffn2_norm, v_kv_norm, v_final_norm, v_swa_sinks, zrow))

    def unpack(p):
        return dict(ffn1_norm=p[0:2], mix_norm=p[2:4], ffn2_norm=p[4:6], kv_norm=p[6], final_norm=p[7],
                    swa_sinks=p[8:9, :SWA_Q_HEADS])

    big = dict(ffn1_w_in=r_ffn1_in, ffn1_w_out=r_ffn1_out, ffn2_w_in=r_ffn2_in, ffn2_w_out=r_ffn2_out,
               sb_w_qkv=r_qkv, sb_w_o=r_sb_o, kv_w=r_kv, swa_w_q=r_swa_q, swa_w_o=r_swa_o)
    order = ["ffn1_norm", "ffn1_w_in", "ffn1_w_out", "mix_norm", "ffn2_norm", "ffn2_w_in", "ffn2_w_out",
             "sb_w_qkv", "sb_w_o", "kv_norm", "kv_w", "swa_w_q", "swa_sinks", "swa_w_o", "final_norm"]
    outs = []
    for kind in range(4):
        sm = unpack(small[kind])
        for nm in order:
            outs.append(big[nm][kind] if nm in big else sm[nm])
    loss = small[0][9, 0]
    return (loss, dx.reshape(x.shape), *outs)
```

```python
import functools

import jax
import jax.numpy as jnp
from jax import lax
from jax.experimental import pallas as pl
from jax.experimental.pallas import tpu as pltpu

F32 = jnp.float32
BF16 = jnp.bfloat16
MESH = pl.DeviceIdType.MESH

D_MODEL = 1024
D_FF = 2816
HEAD_DIM = 64
SB_HEADS = 16
SWA_Q_HEADS = 16
SWA_KV_HEADS = 4
WINDOW = 128
ROPE_THETA = 10000.0
RMS_EPS = 1e-6
FFN_RES_SCALE = 0.5
ATTN_SCALE = HEAD_DIM ** -0.5

ADAM_LR = 0.001
ADAM_B1 = 0.9
ADAM_B2 = 0.999
ADAM_EPS = 1e-08
ADAM_WD = 0.01
ADAM_STEP = 10

N_CHIPS = 4
N_DEV = 8
LANES = 128
FF_CHUNK = D_FF // 2
FF_ROWS = D_FF // N_CHIPS
SQ_ROWS = D_MODEL // N_CHIPS
QKV_COLS = 3 * D_MODEL // N_CHIPS
VMEM_LIMIT = 56 * 1024 * 1024
NEG_BIG = -1e30

SLOT_FFN1 = (0, 1)
SLOT_FFN2 = (2, 3)
SQ_SB_O, SQ_SWA_Q, SQ_SWA_O = 0, 1, 2


def _cparams():
    return pltpu.CompilerParams(vmem_limit_bytes=VMEM_LIMIT)


def _dot(a, b):
    return jnp.dot(a, b, preferred_element_type=F32)


def _dot_nt(a, b):
    return lax.dot_general(a, b, (((1,), (1,)), ((), ())), preferred_element_type=F32)


def _dot_tn(a, b):
    return lax.dot_general(a, b, (((0,), (0,)), ((), ())), preferred_element_type=F32)


def _rstd(h):
    return lax.rsqrt(jnp.mean(h * h, axis=-1, keepdims=True) + RMS_EPS)


def _swap32(x):
    n = x.shape[-1]
    lane = lax.broadcasted_iota(jnp.int32, x.shape, x.ndim - 1)
    first = (lane % HEAD_DIM) < (HEAD_DIM // 2)
    return jnp.where(first, pltpu.roll(x, n - HEAD_DIM // 2, x.ndim - 1), pltpu.roll(x, HEAD_DIM // 2, x.ndim - 1))


def _tile_lanes(t, n):
    return t if n == LANES else jnp.tile(t, (1, n // LANES))


FFN_ROWS = 256


def _ffn_w_in_spec(slot):
    return pl.BlockSpec((N_CHIPS, None, D_MODEL, FF_CHUNK), lambda i: (0, slot, 0, 0), pipeline_mode=pl.Buffered(1))


def _ffn_w_out_spec(slot):
    return pl.BlockSpec((N_CHIPS, None, FF_ROWS, D_MODEL), lambda i: (0, slot, 0, 0), pipeline_mode=pl.Buffered(1))


def ffn_fwd(tag, h, g, w_in, w_out, slot):
    T = h.shape[0]
    tm = FFN_ROWS
    nch = D_FF // FF_CHUNK

    def body(h_ref, g_ref, wi_ref, wo_ref, out_ref, gate_ref, up_ref):
        hh = h_ref[...]
        xn = (hh * _rstd(hh) * g_ref[...]).astype(BF16)
        acc = None
        for j in range(nch):
            cols = slice(j * FF_CHUNK, (j + 1) * FF_CHUNK)
            gate = _dot(xn, wi_ref[j])
            up = _dot(xn, wi_ref[nch + j])
            gate_ref[:, cols] = gate.astype(BF16)
            up_ref[:, cols] = up.astype(BF16)
            a = (gate * jax.nn.sigmoid(gate) * up).astype(BF16)
            part = _dot(a, wo_ref[2 * j:2 * j + 2].reshape(FF_CHUNK, D_MODEL))
            acc = part if acc is None else acc + part
        out_ref[...] = hh + FFN_RES_SCALE * acc

    row = pl.BlockSpec((tm, D_MODEL), lambda i: (i, 0))
    ff = pl.BlockSpec((tm, D_FF), lambda i: (i, 0))
    return pl.pallas_call(
        body,
        name=f"ffn_fwd_{tag}",
        grid=(T // tm,),
        in_specs=[row, pl.BlockSpec((1, D_MODEL), lambda i: (0, 0)), _ffn_w_in_spec(slot), _ffn_w_out_spec(slot)],
        out_specs=[row, ff, ff],
        out_shape=[
            jax.ShapeDtypeStruct((T, D_MODEL), F32),
            jax.ShapeDtypeStruct((T, D_FF), BF16),
            jax.ShapeDtypeStruct((T, D_FF), BF16),
        ],
        compiler_params=_cparams(),
    )(h, g, w_in, w_out)


def ffn_bwd(tag, dh, h, g, gate, up, w_in, w_out, slot):
    T = dh.shape[0]
    tm = FFN_ROWS
    nch = D_FF // FF_CHUNK

    def body(dh_ref, h_ref, g_ref, gate_ref, up_ref, wi_ref, wo_ref,
             dhin_ref, xn_ref, dg_ref, du_ref, a_ref, dhb_ref, dnorm_ref):
        @pl.when(pl.program_id(0) == 0)
        def _():
            dnorm_ref[...] = jnp.zeros_like(dnorm_ref)

        dhh = dh_ref[...]
        dhb = (FFN_RES_SCALE * dhh).astype(BF16)
        dhb_ref[...] = dhb
        dxn = None
        for j in range(nch):
            cols = slice(j * FF_CHUNK, (j + 1) * FF_CHUNK)
            da = _dot_nt(dhb, wo_ref[2 * j:2 * j + 2].reshape(FF_CHUNK, D_MODEL))
            gt = gate_ref[:, cols].astype(F32)
            u = up_ref[:, cols].astype(F32)
            s = jax.nn.sigmoid(gt)
            silu = gt * s
            a_ref[:, cols] = (silu * u).astype(BF16)
            dgate = (da * u * (s * (1.0 + gt * (1.0 - s)))).astype(BF16)
            dup = (da * silu).astype(BF16)
            dg_ref[:, cols] = dgate
            du_ref[:, cols] = dup
            part = _dot_nt(dgate, wi_ref[j]) + _dot_nt(dup, wi_ref[nch + j])
            dxn = part if dxn is None else dxn + part
        hh = h_ref[...]
        gg = g_ref[...]
        r = _rstd(hh)
        hr = hh * r
        xn_ref[...] = (hr * gg).astype(BF16)
        dnorm_ref[...] += jnp.sum(dxn * hr, axis=0, keepdims=True)
        gd = gg * dxn
        dhin_ref[...] = dhh + r * (gd - hr * jnp.mean(gd * hr, axis=-1, keepdims=True))

    row = pl.BlockSpec((tm, D_MODEL), lambda i: (i, 0))
    ff = pl.BlockSpec((tm, D_FF), lambda i: (i, 0))
    vec = pl.BlockSpec((1, D_MODEL), lambda i: (0, 0))
    return pl.pallas_call(
        body,
        name=f"ffn_bwd_{tag}",
        grid=(T // tm,),
        in_specs=[row, row, vec, ff, ff, _ffn_w_in_spec(slot), _ffn_w_out_spec(slot)],
        out_specs=[row, row, ff, ff, ff, row, vec],
        out_shape=[
            jax.ShapeDtypeStruct((T, D_MODEL), F32),
            jax.ShapeDtypeStruct((T, D_MODEL), BF16),
            jax.ShapeDtypeStruct((T, D_FF), BF16),
            jax.ShapeDtypeStruct((T, D_FF), BF16),
            jax.ShapeDtypeStruct((T, D_FF), BF16),
            jax.ShapeDtypeStruct((T, D_MODEL), BF16),
            jax.ShapeDtypeStruct((1, D_MODEL), F32),
        ],
        compiler_params=_cparams(),
    )(dh, h, g, gate, up, w_in, w_out)


def rms_linear(name, h, g, w, w_spec, w_prep, n_out, tn, *, rope=None, scale=None):
    T = h.shape[0]
    tm = 512
    extra, extra_specs = [], []
    if rope is not None:
        extra += list(rope)
        extra_specs += [pl.BlockSpec((tm, LANES), lambda i, j: (i, 0))] * 2
    if scale is not None:
        extra.append(scale)
        extra_specs.append(pl.BlockSpec((1, tn), lambda i, j: (0, j)))

    def body(h_ref, g_ref, w_ref, *rest):
        rest = list(rest)
        cos_ref = sin_ref = sc_ref = None
        if rope is not None:
            cos_ref, sin_ref = rest[0], rest[1]
            rest = rest[2:]
        if scale is not None:
            sc_ref = rest[0]
            rest = rest[1:]
        out_ref, xn_s = rest

        @pl.when(pl.program_id(1) == 0)
        def _():
            hh = h_ref[...]
            xn_s[...] = (hh * _rstd(hh) * g_ref[...]).astype(BF16)

        y = _dot(xn_s[...], w_prep(w_ref[...]))
        if rope is not None:
            y = y * _tile_lanes(cos_ref[...], tn) + _swap32(y) * _tile_lanes(sin_ref[...], tn)
        if scale is not None:
            y = y * sc_ref[...]
        out_ref[...] = y.astype(BF16)

    return pl.pallas_call(
        body,
        name=name,
        grid=(T // tm, n_out // tn),
        in_specs=[
            pl.BlockSpec((tm, D_MODEL), lambda i, j: (i, 0)),
            pl.BlockSpec((1, D_MODEL), lambda i, j: (0, 0)),
            w_spec,
        ] + extra_specs,
        out_specs=pl.BlockSpec((tm, tn), lambda i, j: (i, j)),
        out_shape=jax.ShapeDtypeStruct((T, n_out), BF16),
        scratch_shapes=[pltpu.VMEM((tm, D_MODEL), BF16)],
        compiler_params=_cparams(),
    )(h, g, w, *extra)


QKV_ROWS = 512


def _qkv_w_spec():
    return pl.BlockSpec((N_CHIPS, D_MODEL, QKV_COLS), lambda i: (0, 0, 0), pipeline_mode=pl.Buffered(1))


def qkv_fwd(h, g, w_qkv, scale):
    T = h.shape[0]
    tm = QKV_ROWS

    def body(h_ref, g_ref, w_ref, sc_ref, out_ref):
        hh = h_ref[...]
        xn = (hh * _rstd(hh) * g_ref[...]).astype(BF16)
        for s in range(N_CHIPS):
            cols = slice(s * QKV_COLS, (s + 1) * QKV_COLS)
            out_ref[:, cols] = (_dot(xn, w_ref[s]) * sc_ref[:, cols]).astype(BF16)

    return pl.pallas_call(
        body,
        name="sb_qkv",
        grid=(T // tm,),
        in_specs=[
            pl.BlockSpec((tm, D_MODEL), lambda i: (i, 0)),
            pl.BlockSpec((1, D_MODEL), lambda i: (0, 0)),
            _qkv_w_spec(),
            pl.BlockSpec((1, 3 * D_MODEL), lambda i: (0, 0)),
        ],
        out_specs=pl.BlockSpec((tm, 3 * D_MODEL), lambda i: (i, 0)),
        out_shape=jax.ShapeDtypeStruct((T, 3 * D_MODEL), BF16),
        compiler_params=_cparams(),
    )(h, g, w_qkv, scale)


def qkv_bwd(dy, w_qkv, h, g, dres):
    T = h.shape[0]
    tm = QKV_ROWS

    def body(dy_ref, w_ref, h_ref, g_ref, dres_ref, dh_ref, xn_ref, dg_ref):
        @pl.when(pl.program_id(0) == 0)
        def _():
            dg_ref[...] = jnp.zeros_like(dg_ref)

        dxn = None
        for s in range(N_CHIPS):
            part = _dot_nt(dy_ref[:, s * QKV_COLS:(s + 1) * QKV_COLS], w_ref[s])
            dxn = part if dxn is None else dxn + part
        hh = h_ref[...]
        gg = g_ref[...]
        r = _rstd(hh)
        hr = hh * r
        xn_ref[...] = (hr * gg).astype(BF16)
        dg_ref[...] += jnp.sum(dxn * hr, axis=0, keepdims=True)
        gd = gg * dxn
        dh_ref[...] = dres_ref[...] + r * (gd - hr * jnp.mean(gd * hr, axis=-1, keepdims=True))

    row = pl.BlockSpec((tm, D_MODEL), lambda i: (i, 0))
    vec = pl.BlockSpec((1, D_MODEL), lambda i: (0, 0))
    return pl.pallas_call(
        body,
        name="sb_qkv_bwd",
        grid=(T // tm,),
        in_specs=[pl.BlockSpec((tm, 3 * D_MODEL), lambda i: (i, 0)), _qkv_w_spec(), row, vec, row],
        out_specs=[row, row, vec],
        out_shape=[
            jax.ShapeDtypeStruct((T, D_MODEL), F32),
            jax.ShapeDtypeStruct((T, D_MODEL), BF16),
            jax.ShapeDtypeStruct((1, D_MODEL), F32),
        ],
        compiler_params=_cparams(),
    )(dy, w_qkv, h, g, dres)


def linear_res(name, a, w_sq, t, res):
    T = a.shape[0]
    tm = 512

    def body(a_ref, w_ref, res_ref, out_ref):
        out_ref[...] = res_ref[...] + _dot(a_ref[...], w_ref[...].reshape(D_MODEL, D_MODEL))

    row = pl.BlockSpec((tm, D_MODEL), lambda i: (i, 0))
    return pl.pallas_call(
        body,
        name=name,
        grid=(T // tm,),
        in_specs=[row, pl.BlockSpec((N_CHIPS, None, SQ_ROWS, D_MODEL), lambda i: (0, t, 0, 0)), row],
        out_specs=row,
        out_shape=jax.ShapeDtypeStruct((T, D_MODEL), F32),
        compiler_params=_cparams(),
    )(a, w_sq, res)


def linear_bwd_plain(name, dy, w_sq, t):
    T = dy.shape[0]
    tm = 512

    def body(dy_ref, w_ref, da_ref, dyb_ref):
        dyb = dy_ref[...].astype(BF16)
        dyb_ref[...] = dyb
        da_ref[...] = _dot_nt(dyb, w_ref[...].reshape(D_MODEL, D_MODEL)).astype(BF16)

    row = pl.BlockSpec((tm, D_MODEL), lambda i: (i, 0))
    return pl.pallas_call(
        body,
        name=name,
        grid=(T // tm,),
        in_specs=[row, pl.BlockSpec((N_CHIPS, None, SQ_ROWS, D_MODEL), lambda i: (0, t, 0, 0))],
        out_specs=[row, row],
        out_shape=[jax.ShapeDtypeStruct((T, D_MODEL), BF16), jax.ShapeDtypeStruct((T, D_MODEL), BF16)],
        compiler_params=_cparams(),
    )(dy, w_sq)


def linear_bwd_rms(name, pairs, h, g, dres, nch, tn, tm=256):
    T = h.shape[0]
    npair = len(pairs)

    def body(*refs):
        dy_refs = refs[:npair]
        w_refs = refs[npair:2 * npair]
        h_ref, g_ref, dres_ref, dh_ref, xn_ref, dg_ref, acc_s = refs[2 * npair:]
        i = pl.program_id(0)
        j = pl.program_id(1)

        @pl.when(j == 0)
        def _():
            acc_s[...] = jnp.zeros_like(acc_s)

        @pl.when((i == 0) & (j == 0))
        def _():
            dg_ref[...] = jnp.zeros_like(dg_ref)

        part = None
        for p in range(npair):
            d = _dot_nt(dy_refs[p][...], pairs[p][3](w_refs[p][...]))
            part = d if part is None else part + d
        acc_s[...] += part

        @pl.when(j == nch - 1)
        def _():
            dxn = acc_s[...]
            hh = h_ref[...]
            gg = g_ref[...]
            r = _rstd(hh)
            hr = hh * r
            xn_ref[...] = (hr * gg).astype(BF16)
            dg_ref[...] += jnp.sum(dxn * hr, axis=0, keepdims=True)
            gd = gg * dxn
            dh_ref[...] = dres_ref[...] + r * (gd - hr * jnp.mean(gd * hr, axis=-1, keepdims=True))

    row = pl.BlockSpec((tm, D_MODEL), lambda i, j: (i, 0))
    vec = pl.BlockSpec((1, D_MODEL), lambda i, j: (0, 0))
    return pl.pallas_call(
        body,
        name=name,
        grid=(T // tm, nch),
        in_specs=[pl.BlockSpec((tm, tn), lambda i, j: (i, j))] * npair + [p[2] for p in pairs] + [row, vec, row],
        out_specs=[row, row, vec],
        out_shape=[
            jax.ShapeDtypeStruct((T, D_MODEL), F32),
            jax.ShapeDtypeStruct((T, D_MODEL), BF16),
            jax.ShapeDtypeStruct((1, D_MODEL), F32),
        ],
        scratch_shapes=[pltpu.VMEM((tm, D_MODEL), F32)],
        compiler_params=_cparams(),
    )(*[p[0] for p in pairs], *[p[1] for p in pairs], h, g, dres)


def loss_bwd(h, g, tgt):
    T = h.shape[0]
    tm = 512

    def body(h_ref, g_ref, t_ref, dh_ref, loss_ref, dg_ref):
        @pl.when(pl.program_id(0) == 0)
        def _():
            loss_ref[...] = jnp.zeros_like(loss_ref)
            dg_ref[...] = jnp.zeros_like(dg_ref)

        hh = h_ref[...]
        gg = g_ref[...]
        r = _rstd(hh)
        hr = hh * r
        err = hr * gg - t_ref[...]
        loss_ref[...] += 0.5 * jnp.sum(jnp.mean(err * err, axis=-1, keepdims=True), axis=0, keepdims=True)
        dy = err * (1.0 / D_MODEL)
        dg_ref[...] += jnp.sum(dy * hr, axis=0, keepdims=True)
        gd = gg * dy
        dh_ref[...] = r * (gd - hr * jnp.mean(gd * hr, axis=-1, keepdims=True))

    row = pl.BlockSpec((tm, D_MODEL), lambda i: (i, 0))
    vec = pl.BlockSpec((1, D_MODEL), lambda i: (0, 0))
    return pl.pallas_call(
        body,
        name="loss_bwd",
        grid=(T // tm,),
        in_specs=[row, vec, row],
        out_specs=[row, pl.BlockSpec((1, LANES), lambda i: (0, 0)), vec],
        out_shape=[
            jax.ShapeDtypeStruct((T, D_MODEL), F32),
            jax.ShapeDtypeStruct((1, LANES), F32),
            jax.ShapeDtypeStruct((1, D_MODEL), F32),
        ],
        compiler_params=_cparams(),
    )(h, g, tgt)


DW_TOKENS = 4096


def mm_tn(name, a, b, tk, tn, out_block, out_index, out_shape, prev=None, tt=DW_TOKENS):
    T = a.shape[0]
    ns, r = out_block[1], out_block[3]
    tt = min(tt, T)
    nt = T // tt

    def body(*refs):
        if prev is None:
            a_ref, b_ref, out_ref = refs
        else:
            a_ref, b_ref, _, out_ref = refs
        t = pl.program_id(2)
        res = _dot_tn(a_ref[...], b_ref[...])

        @pl.when(t == 0)
        def _():
            for u in range(ns):
                out_ref[u] = res[u * r:(u + 1) * r]

        @pl.when(t > 0)
        def _():
            for u in range(ns):
                out_ref[u] += res[u * r:(u + 1) * r]

    in_specs = [
        pl.BlockSpec((tt, tk), lambda k, n, t: (t, k)),
        pl.BlockSpec((tt, tn), lambda k, n, t: (t, n)),
    ]
    args = [a, b]
    aliases = {}
    if prev is not None:
        in_specs.append(pl.BlockSpec(memory_space=pl.ANY))
        args.append(prev)
        aliases = {2: 0}
    return pl.pallas_call(
        body,
        name=name,
        grid=(a.shape[1] // tk, b.shape[1] // tn, nt),
        in_specs=in_specs,
        out_specs=pl.BlockSpec(out_block, lambda k, n, t: out_index(k, n)),
        out_shape=jax.ShapeDtypeStruct(out_shape, F32),
        input_output_aliases=aliases,
        compiler_params=_cparams(),
    )(*args)


SB_BLOCK = 256
SB_QROWS = 512
SB_QROWS_BWD = 256
SB_CHUNK = 128


LOG2E = 1.4426950408889634


def _softplus2(z2):
    sign = jnp.uint32(0x80000000)
    neg_abs = lax.bitcast_convert_type(lax.bitcast_convert_type(z2, jnp.uint32) | sign, F32)
    return jnp.log2(1.0 + jnp.exp2(neg_abs)) + jnp.maximum(z2, 0.0)


def _twice(x):
    return jnp.concatenate([x, x], axis=1)


def sb_fwd(qkv):
    T = qkv.shape[0]
    tq, tk = SB_QROWS, SB_BLOCK
    ratio = tq // tk
    npair = SB_HEADS // 2

    def body(q_ref, k_ref, v_ref, o_ref, tot_ref, acc_s, c_s, z_s, w_s):
        i = pl.program_id(1)
        q = q_ref[...]
        lane = lax.broadcasted_iota(jnp.int32, (tq, LANES), 1)
        first = lane < HEAD_DIM
        zero = jnp.zeros_like(q)
        q_heads = (jnp.where(first, q, zero), jnp.where(first, zero, q))
        row = lax.broadcasted_iota(jnp.int32, (tq, tk), 0)
        col = lax.broadcasted_iota(jnp.int32, (tq, tk), 1)
        visible = [col + r * tk < row for r in range(ratio)]
        krow = lax.broadcasted_iota(jnp.int32, (tk, tk), 0)
        kcol = lax.broadcasted_iota(jnp.int32, (tk, tk), 1)
        from_s = (krow >= kcol).astype(BF16)
        acc_s[...] = jnp.zeros_like(acc_s)
        c_s[...] = jnp.zeros_like(c_s)

        def rows(j):
            return pl.ds(pl.multiple_of(j * tk, tk), tk)

        def logits(j):
            kb = k_ref[rows(j), :]
            for hd in range(2):
                z_s[hd] = _dot_nt(q_heads[hd], kb) * LOG2E

        def flush(j):
            vb = v_ref[rows(j), :]
            for hd in range(2):
                acc_s[hd] += _dot(w_s[hd], vb)

        def block(j, mask=None, walked_before=True):
            if walked_before:
                flush(j + 1)
            chunks = [(hd, slice(r0, r0 + SB_CHUNK)) for hd in range(2) for r0 in range(0, tq, SB_CHUNK)]
            k_next = k_ref[rows(jnp.maximum(j - 1, 0)), :]
            es, sums = [], []
            for hd, rs in chunks:
                z2 = z_s[hd, rs, :]
                z_s[hd, rs, :] = _dot_nt(q_heads[hd][rs, :], k_next) * LOG2E
                if mask is not None:
                    z2 = jnp.where(mask[rs, :], z2, NEG_BIG)
                sp = _softplus2(z2)
                c = c_s[hd, rs, :]
                es.append(z2 + _twice(c))
                c_s[hd, rs, :] = c - jnp.sum(sp, axis=1, keepdims=True)
                sums.append(_dot(sp.astype(BF16), from_s))
            for (hd, rs), e, s in zip(chunks, es, sums):
                w_s[hd, rs, :] = jnp.exp2(e - s).astype(BF16)

        logits(ratio * i + ratio - 1)
        for r in reversed(range(ratio)):
            block(ratio * i + r, visible[r], walked_before=(r != ratio - 1))

        @pl.loop(0, ratio * i)
        def _(jj):
            block(ratio * i - 1 - jj)

        flush(0)
        o_ref[...] = jnp.where(first, acc_s[0], acc_s[1]).astype(BF16)
        tot_ref[...] = jnp.where(first, c_s[0], c_s[1])

    return pl.pallas_call(
        body,
        name="sb_fwd",
        grid=(npair, T // tq),
        in_specs=[
            pl.BlockSpec((tq, LANES), lambda p, i: (i, p)),
            pl.BlockSpec((T, LANES), lambda p, i: (0, npair + p)),
            pl.BlockSpec((T, LANES), lambda p, i: (0, 2 * npair + p)),
        ],
        out_specs=[pl.BlockSpec((tq, LANES), lambda p, i: (i, p))] * 2,
        out_shape=[jax.ShapeDtypeStruct((T, D_MODEL), BF16), jax.ShapeDtypeStruct((T, D_MODEL), F32)],
        scratch_shapes=[
            pltpu.VMEM((2, tq, LANES), F32), pltpu.VMEM((2, tq, LANES), F32),
            pltpu.VMEM((2, tq, tk), F32), pltpu.VMEM((2, tq, tk), BF16),
        ],
        compiler_params=_cparams(),
    )(qkv, qkv, qkv)


def sb_bwd(qkv, do, tot):
    T = qkv.shape[0]
    tq, tk = SB_QROWS_BWD, SB_BLOCK
    ratio = tq // tk
    npair = SB_HEADS // 2
    nq = T // tq
    nk = T // tk

    def body(q_ref, k_ref, v_ref, do_ref, tot_ref, dq_ref, dk_ref, dv_ref,
             dkt_s, dvt_s, dq_s, rest_s, cg_s, z_s, da_s, dz_s, a_s):
        i = pl.program_id(1)

        @pl.when(i == 0)
        def _():
            dkt_s[...] = jnp.zeros_like(dkt_s)
            dvt_s[...] = jnp.zeros_like(dvt_s)

        q = q_ref[...]
        do_ = do_ref[...]
        tot_ = tot_ref[...]
        q_t = q.astype(F32).T.astype(BF16)
        do_t = do_.astype(F32).T.astype(BF16)
        lane = lax.broadcasted_iota(jnp.int32, (tq, LANES), 1)
        first = lane < HEAD_DIM
        zero = jnp.zeros_like(q)
        q_heads = (jnp.where(first, q, zero), jnp.where(first, zero, q))
        do_heads = (jnp.where(first, do_, zero), jnp.where(first, zero, do_))
        row = lax.broadcasted_iota(jnp.int32, (tq, tk), 0)
        col = lax.broadcasted_iota(jnp.int32, (tq, tk), 1)
        visible = [col + r * tk < row for r in range(ratio)]
        krow = lax.broadcasted_iota(jnp.int32, (tk, tk), 0)
        kcol = lax.broadcasted_iota(jnp.int32, (tk, tk), 1)
        before = (krow < kcol).astype(BF16)
        from_s = (krow >= kcol).astype(BF16)
        last = ratio * i + ratio - 1
        rest_s[0] = jnp.broadcast_to(tot_[:, 0:1], (tq, LANES))
        rest_s[1] = jnp.broadcast_to(tot_[:, HEAD_DIM:HEAD_DIM + 1], (tq, LANES))
        cg_s[...] = jnp.zeros_like(cg_s)
        dq_s[...] = jnp.zeros_like(dq_s)
        dz_s[...] = jnp.zeros_like(dz_s)
        a_s[...] = jnp.zeros_like(a_s)

        def rows(j):
            return pl.ds(pl.multiple_of(j * tk, tk), tk)

        def logits(j):
            kb = k_ref[rows(j), :]
            vb = v_ref[rows(j), :]
            for hd in range(2):
                z_s[hd] = _dot_nt(q_heads[hd], kb) * LOG2E
                da_s[hd] = _dot_nt(do_heads[hd], vb)

        def flush(j):
            kb = k_ref[rows(j), :]
            for hd in range(2):
                dims = slice(hd * HEAD_DIM, (hd + 1) * HEAD_DIM)
                dq_s[hd] += _dot(dz_s[hd], kb)
                dkt_s[j, dims, :] += _dot(q_t[dims, :], dz_s[hd])
                dvt_s[j, dims, :] += _dot(do_t[dims, :], a_s[hd])

        def block(j, mask=None):
            flush(jnp.maximum(j - 1, 0))
            chunks = [(hd, slice(r0, r0 + SB_CHUNK)) for hd in range(2) for r0 in range(0, tq, SB_CHUNK)]
            nxt = rows(jnp.minimum(j + 1, last))
            k_next = k_ref[nxt, :]
            v_next = v_ref[nxt, :]
            stage1 = []
            for hd, rs in chunks:
                z2 = z_s[hd, rs, :]
                z_s[hd, rs, :] = _dot_nt(q_heads[hd][rs, :], k_next) * LOG2E
                if mask is not None:
                    z2 = jnp.where(mask[rs, :], z2, NEG_BIG)
                sp = _softplus2(z2)
                rest = rest_s[hd, rs, :] + jnp.sum(sp, axis=1, keepdims=True)
                rest_s[hd, rs, :] = rest
                stage1.append((z2 + _twice(rest), z2 - sp, _dot(sp.astype(BF16), from_s)))
            stage2 = []
            for (hd, rs), (e, log2_beta, ahead) in zip(chunks, stage1):
                a = jnp.exp2(e - ahead)
                g = a * da_s[hd, rs, :]
                da_s[hd, rs, :] = _dot_nt(do_heads[hd][rs, :], v_next)
                cg = cg_s[hd, rs, :]
                a_s[hd, rs, :] = a.astype(BF16)
                cg_s[hd, rs, :] = cg + jnp.sum(g, axis=1, keepdims=True)
                stage2.append((g, g + _twice(cg), log2_beta, _dot(g.astype(BF16), before)))
            for (hd, rs), (g, g_from, log2_beta, g_before) in zip(chunks, stage2):
                dz_s[hd, rs, :] = (g - jnp.exp2(log2_beta) * (g_from + g_before)).astype(BF16)

        logits(0)

        @pl.loop(0, ratio * i)
        def _(j):
            block(j)

        for r in range(ratio):
            block(ratio * i + r, visible[r])
        flush(last)
        dq_ref[...] = (jnp.where(first, dq_s[0], dq_s[1]) * ATTN_SCALE).astype(BF16)

        @pl.when(i == nq - 1)
        def _():
            @pl.loop(0, nk)
            def _(b):
                dk_ref[rows(b), :] = dkt_s[b].T.astype(BF16)
                dv_ref[rows(b), :] = dvt_s[b].T.astype(BF16)

    qblk = pl.BlockSpec((tq, LANES), lambda p, i: (i, p))
    full = pl.BlockSpec((T, LANES), lambda p, i: (0, p))
    return pl.pallas_call(
        body,
        name="sb_bwd",
        grid=(npair, nq),
        in_specs=[
            qblk,
            pl.BlockSpec((T, LANES), lambda p, i: (0, npair + p)),
            pl.BlockSpec((T, LANES), lambda p, i: (0, 2 * npair + p)),
            qblk, qblk,
        ],
        out_specs=[qblk, full, full],
        out_shape=[jax.ShapeDtypeStruct((T, D_MODEL), BF16)] * 3,
        scratch_shapes=[
            pltpu.VMEM((nk, LANES, tk), F32), pltpu.VMEM((nk, LANES, tk), F32),
            pltpu.VMEM((2, tq, LANES), F32), pltpu.VMEM((2, tq, LANES), F32), pltpu.VMEM((2, tq, LANES), F32),
            pltpu.VMEM((2, tq, tk), F32), pltpu.VMEM((2, tq, tk), F32),
            pltpu.VMEM((2, tq, tk), BF16), pltpu.VMEM((2, tq, tk), BF16),
        ],
        compiler_params=_cparams(),
    )(qkv, qkv, qkv, do, tot)


def _swa_valid(n):
    qi = lax.broadcasted_iota(jnp.int32, (WINDOW, 2 * WINDOW), 0)
    ki = lax.broadcasted_iota(jnp.int32, (WINDOW, 2 * WINDOW), 1)
    diff = qi + WINDOW - ki
    return (diff >= 0) & (diff < WINDOW) & ((n > 0) | (ki >= WINDOW))


def _to_half(x, first, src, dst):
    keep = first if src == 0 else jnp.logical_not(first)
    x = jnp.where(keep, x, jnp.zeros_like(x))
    if src != dst:
        x = pltpu.roll(x.astype(F32), HEAD_DIM, 1).astype(BF16)
    return x


def _kv_band(prev_ref, cur_ref, pb):
    cols = slice(pb * LANES, (pb + 1) * LANES)
    return jnp.concatenate([prev_ref[:, cols], cur_ref[:, cols]], axis=0)


def _swa_specs(T):
    nb = T // WINDOW
    kv_w = SWA_KV_HEADS * HEAD_DIM
    qrow = pl.BlockSpec((WINDOW, D_MODEL), lambda n: (n, 0))
    cur = pl.BlockSpec((WINDOW, kv_w), lambda n: (n, 0))
    prev = pl.BlockSpec((WINDOW, kv_w), lambda n: (jnp.maximum(n - 1, 0), 0))
    smem = pl.BlockSpec(memory_space=pltpu.SMEM)
    return nb, qrow, cur, prev, smem


def swa_fwd(q, k, v, sinks):
    T = q.shape[0]
    nb, qrow, cur, prev, smem = _swa_specs(T)

    def body(sink_ref, q_ref, kc_ref, kp_ref, vc_ref, vp_ref, o_ref, lse_ref):
        n = pl.program_id(0)
        lane = lax.broadcasted_iota(jnp.int32, (WINDOW, LANES), 1)
        first = lane < HEAD_DIM
        valid = _swa_valid(n)
        lse_acc = jnp.zeros((WINDOW, LANES), F32)
        for pb in range(SWA_KV_HEADS // 2):
            k2 = _kv_band(kp_ref, kc_ref, pb)
            v2 = _kv_band(vp_ref, vc_ref, pb)
            for b in range(2):
                kvh = 2 * pb + b
                for qq in range(2):
                    cols = slice((2 * kvh + qq) * LANES, (2 * kvh + qq + 1) * LANES)
                    qp = q_ref[:, cols]
                    outs = []
                    for a in range(2):
                        head = 4 * kvh + 2 * qq + a
                        qh = _to_half(qp, first, a, b)
                        s = jnp.where(valid, _dot_nt(qh, k2), NEG_BIG)
                        sink = sink_ref[head]
                        m = jnp.maximum(jnp.max(s, axis=1, keepdims=True), sink)
                        p = jnp.exp(s - m)
                        den = jnp.sum(p, axis=1, keepdims=True) + jnp.exp(sink - m)
                        o = _dot((p / den).astype(BF16), v2)
                        if a != b:
                            o = pltpu.roll(o, HEAD_DIM, 1)
                        outs.append(o)
                        lse_acc = jnp.where(lane == head, m + jnp.log(den), lse_acc)
                    o_ref[:, cols] = jnp.where(first, outs[0], outs[1]).astype(BF16)
        lse_ref[...] = lse_acc

    return pl.pallas_call(
        body,
        name="swa_fwd",
        grid=(nb,),
        in_specs=[smem, qrow, cur, prev, cur, prev],
        out_specs=[qrow, pl.BlockSpec((WINDOW, LANES), lambda n: (n, 0))],
        out_shape=[jax.ShapeDtypeStruct((T, D_MODEL), BF16), jax.ShapeDtypeStruct((T, LANES), F32)],
        compiler_params=_cparams(),
    )(sinks, q, k, k, v, v)


def swa_bwd(q, k, v, sinks, do, lse, cos, sin):
    T = q.shape[0]
    nb, qrow, cur, prev, smem = _swa_specs(T)
    kv_w = SWA_KV_HEADS * HEAD_DIM

    def body(sink_ref, q_ref, kc_ref, kp_ref, vc_ref, vp_ref, do_ref, lse_ref, cos_ref, sin_ref,
             dq_ref, own_ref, prv_ref, dsink_ref):
        n = pl.program_id(0)

        @pl.when(n == 0)
        def _():
            dsink_ref[...] = jnp.zeros_like(dsink_ref)

        lane = lax.broadcasted_iota(jnp.int32, (WINDOW, LANES), 1)
        lane1 = lax.broadcasted_iota(jnp.int32, (1, LANES), 1)
        first = lane < HEAD_DIM
        valid = _swa_valid(n)
        cos_ = cos_ref[...]
        sin_ = sin_ref[...]
        dsink = jnp.zeros((1, LANES), F32)
        for pb in range(SWA_KV_HEADS // 2):
            k2 = _kv_band(kp_ref, kc_ref, pb)
            v2 = _kv_band(vp_ref, vc_ref, pb)
            dk2 = jnp.zeros((2 * WINDOW, LANES), F32)
            dv2 = jnp.zeros((2 * WINDOW, LANES), F32)
            for b in range(2):
                kvh = 2 * pb + b
                for qq in range(2):
                    cols = slice((2 * kvh + qq) * LANES, (2 * kvh + qq + 1) * LANES)
                    qp = q_ref[:, cols]
                    dop = do_ref[:, cols]
                    dqs = []
                    for a in range(2):
                        head = 4 * kvh + 2 * qq + a
                        qh = _to_half(qp, first, a, b)
                        doh = _to_half(dop, first, a, b)
                        s = jnp.where(valid, _dot_nt(qh, k2), NEG_BIG)
                        lse_h = lse_ref[:, head:head + 1]
                        p = jnp.exp(s - lse_h)
                        dp = _dot_nt(doh, v2)
                        delta = jnp.sum(p * dp, axis=1, keepdims=True)
                        ds = (p * (dp - delta)).astype(BF16)
                        p_sink = jnp.exp(sink_ref[head] - lse_h)
                        dsink = dsink + jnp.where(lane1 == head, -jnp.sum(p_sink * delta, axis=0, keepdims=True), 0.0)
                        dq = _dot(ds, k2)
                        if a != b:
                            dq = pltpu.roll(dq, HEAD_DIM, 1)
                        dqs.append(dq)
                        dk2 = dk2 + _dot_tn(ds, qh)
                        dv2 = dv2 + _dot_tn(p.astype(BF16), doh)
                    dqp = jnp.where(first, dqs[0], dqs[1])
                    dq_ref[:, cols] = ((dqp * cos_ + _swap32(dqp * sin_)) * ATTN_SCALE).astype(BF16)
            kcols = slice(pb * LANES, (pb + 1) * LANES)
            vcols = slice(kv_w + pb * LANES, kv_w + (pb + 1) * LANES)
            prv_ref[:, kcols] = dk2[:WINDOW]
            own_ref[:, kcols] = dk2[WINDOW:]
            prv_ref[:, vcols] = dv2[:WINDOW]
            own_ref[:, vcols] = dv2[WINDOW:]
        dsink_ref[...] += dsink

    tab = pl.BlockSpec((WINDOW, LANES), lambda n: (n, 0))
    kvrow = pl.BlockSpec((WINDOW, 2 * kv_w), lambda n: (n, 0))
    return pl.pallas_call(
        body,
        name="swa_bwd",
        grid=(nb,),
        in_specs=[smem, qrow, cur, prev, cur, prev, qrow, tab, tab, tab],
        out_specs=[qrow, kvrow, kvrow, pl.BlockSpec((1, LANES), lambda n: (0, 0))],
        out_shape=[
            jax.ShapeDtypeStruct((T, D_MODEL), BF16),
            jax.ShapeDtypeStruct((T, 2 * kv_w), F32),
            jax.ShapeDtypeStruct((T, 2 * kv_w), F32),
            jax.ShapeDtypeStruct((1, LANES), F32),
        ],
        compiler_params=_cparams(),
    )(sinks, q, k, k, v, v, do, lse, cos, sin)


def kv_grad_combine(own, prv, cos, sin):
    T = own.shape[0]
    nb = T // WINDOW
    kv_w = SWA_KV_HEADS * HEAD_DIM

    def body(own_ref, nxt_ref, cos_ref, sin_ref, out_ref):
        n = pl.program_id(0)
        nxt = jnp.where(n + 1 < nb, nxt_ref[...], 0.0)
        tot = own_ref[...] + nxt
        dk = tot[:, :kv_w]
        c = _tile_lanes(cos_ref[...], kv_w)
        s = _tile_lanes(sin_ref[...], kv_w)
        out_ref[:, :kv_w] = (dk * c + _swap32(dk * s)).astype(BF16)
        out_ref[:, kv_w:] = tot[:, kv_w:].astype(BF16)

    tab = pl.BlockSpec((WINDOW, LANES), lambda n: (n, 0))
    kvrow = pl.BlockSpec((WINDOW, 2 * kv_w), lambda n: (n, 0))
    return pl.pallas_call(
        body,
        name="kv_grad_combine",
        grid=(nb,),
        in_specs=[kvrow, pl.BlockSpec((WINDOW, 2 * kv_w), lambda n: (jnp.minimum(n + 1, nb - 1), 0)), tab, tab],
        out_specs=kvrow,
        out_shape=jax.ShapeDtypeStruct((T, 2 * kv_w), BF16),
        compiler_params=_cparams(),
    )(own, prv, cos, sin)


ANY = pl.BlockSpec(memory_space=pl.ANY)


def _place():
    x, y, c = lax.axis_index("x"), lax.axis_index("y"), lax.axis_index("c")
    other_chips = [(1 - x, y), (x, 1 - y), (1 - x, 1 - y)]
    return x, y, c, 2 * x + y, other_chips


def all_gather_weights(shards):
    n = len(shards)

    def body(*refs):
        ins, outs = refs[:n], refs[n:2 * n]
        send_sems, recv_sems, local_sems = refs[2 * n:]
        _, _, c, me, chips = _place()
        copies = []
        for t in range(n):
            cp = pltpu.make_async_copy(ins[t], outs[t].at[me], local_sems.at[t])
            cp.start()
            copies.append(cp)
            for jdx, (px, py) in enumerate(chips):
                cp = pltpu.make_async_remote_copy(
                    src_ref=ins[t], dst_ref=outs[t].at[me], send_sem=send_sems.at[t, jdx],
                    recv_sem=recv_sems.at[t, jdx], device_id=(px, py, c), device_id_type=MESH)
                cp.start()
                copies.append(cp)
        for cp in copies:
            cp.wait()

    return pl.pallas_call(
        body,
        name="all_gather_weights",
        in_specs=[ANY] * n,
        out_specs=[ANY] * n,
        out_shape=[jax.ShapeDtypeStruct((N_CHIPS,) + s.shape, s.dtype) for s in shards],
        scratch_shapes=[
            pltpu.SemaphoreType.DMA((n, 3)), pltpu.SemaphoreType.DMA((n, 3)), pltpu.SemaphoreType.DMA((n,)),
        ],
    )(*shards)


HBM = pl.BlockSpec(memory_space=pltpu.HBM)
SEM = pl.BlockSpec(memory_space=pltpu.SEMAPHORE)
N_PEER_CHIPS = N_CHIPS - 1


def place_own_shard(name, shard, chip):
    nl, r, c = shard.shape

    def body(chip_ref, s_ref, o_ref):
        o_ref[...] = s_ref[...]

    return pl.pallas_call(
        body, name=name,
        grid_spec=pltpu.PrefetchScalarGridSpec(
            num_scalar_prefetch=1, grid=(nl,),
            in_specs=[pl.BlockSpec((None, r, c), lambda l, chip_ref: (l, 0, 0))],
            out_specs=pl.BlockSpec((None, None, r, c), lambda l, chip_ref: (chip_ref[0], l, 0, 0))),
        out_shape=jax.ShapeDtypeStruct((N_CHIPS,) + shard.shape, shard.dtype), compiler_params=_cparams(),
    )(chip, shard)


def _gather_copy(src, land, send_sem, recv_sem, jdx):
    _, _, c, me, chips = _place()
    px, py = chips[jdx]
    return pltpu.make_async_remote_copy(src_ref=src, dst_ref=land.at[me], send_sem=send_sem, recv_sem=recv_sem,
                                        device_id=(px, py, c), device_id_type=MESH)


def gather_start(name, shards, lands):
    n = len(shards)
    ncopy = n * N_PEER_CHIPS

    def body(*refs):
        ins, lnd = refs[:n], refs[n:2 * n]
        sems = refs[2 * n:2 * n + 2 * ncopy]
        token = refs[-1]
        for t in range(n):
            for jdx in range(N_PEER_CHIPS):
                k = t * N_PEER_CHIPS + jdx
                _gather_copy(ins[t], lnd[t], sems[k], sems[ncopy + k], jdx).start()
        token[...] = jnp.zeros_like(token)

    res = pl.pallas_call(
        body, name=name,
        in_specs=[HBM] * (2 * n),
        out_specs=[SEM] * (2 * ncopy) + [HBM] * (2 * n) + [pl.BlockSpec(memory_space=pltpu.VMEM)],
        out_shape=[pltpu.SemaphoreType.DMA(())] * (2 * ncopy)
        + [pltpu.HBM(s.shape, s.dtype) for s in shards] + [pltpu.HBM(l.shape, l.dtype) for l in lands]
        + [jax.ShapeDtypeStruct((8, LANES), F32)],
        input_output_aliases={t: 2 * ncopy + t for t in range(2 * n)},
        compiler_params=pltpu.CompilerParams(has_side_effects=pltpu.SideEffectType.DATAFLOW_SIDE_EFFECTING),
    )(*[pltpu.with_memory_space_constraint(a, pltpu.HBM) for a in list(shards) + list(lands)])
    sems = res[:2 * ncopy]
    return sems, res[2 * ncopy:2 * ncopy + n], res[2 * ncopy + n:2 * ncopy + 2 * n], res[-1]


def gather_wait(name, sems, shards, lands, after):
    n = len(shards)
    ncopy = n * N_PEER_CHIPS

    def body(*refs):
        ins, lnd = refs[:n], refs[n:2 * n]
        sm = refs[2 * n:2 * n + 2 * ncopy]
        for t in range(n):
            for jdx in range(N_PEER_CHIPS):
                k = t * N_PEER_CHIPS + jdx
                cp = _gather_copy(ins[t], lnd[t], sm[k], sm[ncopy + k], jdx)
                cp.wait_send()
                cp.wait_recv()

    res = pl.pallas_call(
        body, name=name,
        in_specs=[HBM] * (2 * n) + [SEM] * (2 * ncopy) + [ANY],
        out_specs=[HBM] * (2 * n),
        out_shape=[pltpu.HBM(s.shape, s.dtype) for s in shards] + [pltpu.HBM(l.shape, l.dtype) for l in lands],
        input_output_aliases={t: t for t in range(2 * n)},
        compiler_params=pltpu.CompilerParams(has_side_effects=pltpu.SideEffectType.DATAFLOW_SIDE_EFFECTING),
    )(*shards, *lands, *sems, after)
    return res[n:]


def exchange_halves(slabs):
    n = len(slabs)

    def body(*refs):
        ins, theirs = refs[:n], refs[n:2 * n]
        send_sems, recv_sems = refs[2 * n:]
        x, y, c, _, _ = _place()
        copies = []
        for t in range(n):
            cp = pltpu.make_async_remote_copy(
                src_ref=ins[t].at[1 - c], dst_ref=theirs[t], send_sem=send_sems.at[t],
                recv_sem=recv_sems.at[t], device_id=(x, y, 1 - c), device_id_type=MESH)
            cp.start()
            copies.append(cp)
        for cp in copies:
            cp.wait()

    return pl.pallas_call(
        body,
        name="exchange_halves",
        in_specs=[ANY] * n,
        out_specs=[ANY] * n,
        out_shape=[jax.ShapeDtypeStruct(s.shape[1:], s.dtype) for s in slabs],
        scratch_shapes=[pltpu.SemaphoreType.DMA((n,)), pltpu.SemaphoreType.DMA((n,))],
    )(*slabs)


def exchange_chip_partials(parts):
    n = len(parts)

    def body(*refs):
        ins, outs = refs[:n], refs[n:2 * n]
        send_sems, recv_sems = refs[2 * n:]
        _, _, c, me, chips = _place()
        copies = []
        for t in range(n):
            for jdx, (px, py) in enumerate(chips):
                cp = pltpu.make_async_remote_copy(
                    src_ref=ins[t].at[2 * px + py], dst_ref=outs[t].at[me], send_sem=send_sems.at[t, jdx],
                    recv_sem=recv_sems.at[t, jdx], device_id=(px, py, c), device_id_type=MESH)
                cp.start()
                copies.append(cp)
        for cp in copies:
            cp.wait()

    return pl.pallas_call(
        body,
        name="exchange_chip_partials",
        in_specs=[ANY] * n,
        out_specs=[ANY] * n,
        out_shape=[jax.ShapeDtypeStruct(p.shape, p.dtype) for p in parts],
        scratch_shapes=[pltpu.SemaphoreType.DMA((n, 3)), pltpu.SemaphoreType.DMA((n, 3))],
    )(*parts)


def share_reduced_halves(halves):
    n = len(halves)

    def body(*refs):
        ins, outs = refs[:n], refs[n:2 * n]
        send_sems, recv_sems = refs[2 * n:]
        x, y, c, _, _ = _place()
        copies = []
        for t in range(n):
            cp = pltpu.make_async_remote_copy(
                src_ref=ins[t], dst_ref=outs[t], send_sem=send_sems.at[t],
                recv_sem=recv_sems.at[t], device_id=(x, y, 1 - c), device_id_type=MESH)
            cp.start()
            copies.append(cp)
        for cp in copies:
            cp.wait()

    return pl.pallas_call(
        body,
        name="share_reduced_halves",
        in_specs=[ANY] * n,
        out_specs=[ANY] * n,
        out_shape=[jax.ShapeDtypeStruct(h.shape, h.dtype) for h in halves],
        scratch_shapes=[pltpu.SemaphoreType.DMA((n,)), pltpu.SemaphoreType.DMA((n,))],
    )(*halves)


def _row_tile(r, c):
    tr = r
    while tr * c * 4 > (3 << 19) and tr % 16 == 0:
        tr //= 2
    return tr


def add_sibling(name, slab, theirs, core):
    _, ns, slots, r, c = slab.shape
    tr = _row_tile(r, c)

    def body(core_ref, a_ref, b_ref, o_ref):
        o_ref[...] = (a_ref[...] + b_ref[...]).astype(BF16)

    blk = pl.BlockSpec((None, None, tr, c), lambda s, l, i, core_ref: (s, l, i, 0))
    return pl.pallas_call(
        body, name=name,
        grid_spec=pltpu.PrefetchScalarGridSpec(
            num_scalar_prefetch=1, grid=(ns, slots, r // tr),
            in_specs=[pl.BlockSpec((None, None, None, tr, c), lambda s, l, i, core_ref: (core_ref[0], s, l, i, 0)), blk],
            out_specs=blk),
        out_shape=jax.ShapeDtypeStruct(theirs.shape, BF16), compiler_params=_cparams(),
    )(core, slab, theirs)


def sum_chips(name, recv, own, chip):
    _, slots, r, c = recv.shape
    tr = _row_tile(r, c)

    def body(chip_ref, r0, r1, r2, r3, own_ref, o_ref):
        me = chip_ref[0]
        mine = own_ref[...]
        terms = [jnp.where(me == s, mine, rr[...]).astype(F32) for s, rr in enumerate((r0, r1, r2, r3))]
        o_ref[...] = ((terms[0] + terms[1]) + terms[2]) + terms[3]

    def src(s):
        return pl.BlockSpec((None, None, tr, c),
                            lambda l, i, chip_ref: (jnp.where(chip_ref[0] == s, (s + 1) % N_CHIPS, s), l, i, 0))

    return pl.pallas_call(
        body, name=name,
        grid_spec=pltpu.PrefetchScalarGridSpec(
            num_scalar_prefetch=1, grid=(slots, r // tr),
            in_specs=[src(0), src(1), src(2), src(3),
                      pl.BlockSpec((None, None, tr, c), lambda l, i, chip_ref: (chip_ref[0], l, i, 0))],
            out_specs=pl.BlockSpec((None, tr, c), lambda l, i, chip_ref: (l, i, 0))),
        out_shape=jax.ShapeDtypeStruct((slots, r, c), F32), compiler_params=_cparams(),
    )(chip, recv, recv, recv, recv, own)


def _adamw_math(w, g, m, v):
    m = ADAM_B1 * m + (1.0 - ADAM_B1) * g
    v = ADAM_B2 * v + (1.0 - ADAM_B2) * (g * g)
    m_hat = m / (1.0 - ADAM_B1 ** ADAM_STEP)
    v_hat = v / (1.0 - ADAM_B2 ** ADAM_STEP)
    delta = -ADAM_LR * (m_hat / (jnp.sqrt(v_hat) + ADAM_EPS) + ADAM_WD * w)
    return delta, m, v


def adamw_shard(name, w, m, v, g_own, g_sib, core, slot0, row_halves):
    n = w.shape[0]
    _, r, c = g_own.shape
    tr = _row_tile(r, c)
    nr = r // tr

    def body(core_ref, w_ref, m_ref, v_ref, own_ref, sib_ref, go_ref, d_ref, mo_ref, vo_ref):
        g = jnp.where(pl.program_id(1) == core_ref[0], own_ref[...], sib_ref[...])
        delta, mm, vv = _adamw_math(w_ref[...], g, m_ref[...], v_ref[...])
        go_ref[...] = g
        d_ref[...] = delta
        mo_ref[...] = mm
        vo_ref[...] = vv

    if row_halves:
        wspec = pl.BlockSpec((None, tr, c), lambda l, h, i, core_ref: (l, h * nr + i, 0))
    else:
        wspec = pl.BlockSpec((None, tr, c), lambda l, h, i, core_ref: (l, i, h))
    gspec = pl.BlockSpec((None, tr, c), lambda l, h, i, core_ref: (slot0 + l, i, 0))
    shp = jax.ShapeDtypeStruct(w.shape, F32)
    return pl.pallas_call(
        body, name=name,
        grid_spec=pltpu.PrefetchScalarGridSpec(
            num_scalar_prefetch=1, grid=(n, 2, nr),
            in_specs=[wspec, wspec, wspec, gspec, gspec], out_specs=[wspec] * 4),
        out_shape=[shp] * 4, compiler_params=_cparams(),
    )(core, w, m, v, g_own, g_sib)


SMALL_ROWS = 16


def small_allreduce_adamw(part, w, m, v):
    def body(p_ref, w_ref, m_ref, v_ref, g_ref, d_ref, mo_ref, vo_ref, buf, send_sems, recv_sems):
        x, y, c, _, _ = _place()
        me = 4 * x + 2 * y + c
        buf[me] = p_ref[...]
        copies = []
        for k in range(1, N_DEV):
            kx, ky, kc = (k >> 2) & 1, (k >> 1) & 1, k & 1
            peer = (x ^ kx, y ^ ky, c ^ kc)
            cp = pltpu.make_async_remote_copy(
                src_ref=p_ref, dst_ref=buf.at[me], send_sem=send_sems.at[k - 1],
                recv_sem=recv_sems.at[k - 1], device_id=peer, device_id_type=MESH)
            cp.start()
            copies.append(cp)
        for cp in copies:
            cp.wait()
        g = buf[0]
        for dev in range(1, N_DEV):
            g = g + buf[dev]
        delta, mm, vv = _adamw_math(w_ref[...], g, m_ref[...], v_ref[...])
        g_ref[...] = g
        d_ref[...] = delta
        mo_ref[...] = mm
        vo_ref[...] = vv

    vm = pl.BlockSpec(memory_space=pltpu.VMEM)
    shp = jax.ShapeDtypeStruct(part.shape, F32)
    return pl.pallas_call(
        body, name="small_allreduce_adamw",
        in_specs=[vm] * 4, out_specs=[vm] * 4, out_shape=[shp] * 4,
        scratch_shapes=[
            pltpu.VMEM((N_DEV,) + part.shape, F32),
            pltpu.SemaphoreType.DMA((N_DEV - 1,)), pltpu.SemaphoreType.DMA((N_DEV - 1,)),
        ],
    )(part, w, m, v)


def _rope_tables(T):
    half = HEAD_DIM // 2
    inv_freq = ROPE_THETA ** (-jnp.arange(half, dtype=F32) / half)
    ang = jnp.arange(T).astype(F32)[:, None] * inv_freq[None, :]
    cos = jnp.tile(jnp.cos(ang), (1, LANES // half))
    sin = jnp.tile(jnp.sin(ang), (1, LANES // half))
    lane = jnp.arange(LANES)
    sign = jnp.where((lane % HEAD_DIM) < half, -1.0, 1.0).astype(F32)
    return cos, sin * sign[None, :]


def _pack_small(ffn1, mix, ffn2, kvn, fin, sinks, loss_row):
    sink_row = jnp.pad(sinks.reshape(1, SWA_Q_HEADS), ((0, 0), (0, D_MODEL - SWA_Q_HEADS)))
    rows = jnp.concatenate([ffn1, mix, ffn2, kvn.reshape(1, -1), fin.reshape(1, -1), sink_row, loss_row], axis=0)
    return jnp.concatenate([rows, jnp.zeros((SMALL_ROWS - rows.shape[0], D_MODEL), F32)], axis=0)


def kernel(x, ffn1_norm, ffn1_w_in, ffn1_w_out, mix_norm, ffn2_norm, ffn2_w_in, ffn2_w_out, sb_w_qkv, sb_w_o, kv_norm, kv_w, swa_w_q, swa_sinks, swa_w_o, final_norm, loss_target, m_ffn1_norm, m_ffn1_w_in, m_ffn1_w_out, m_mix_norm, m_ffn2_norm, m_ffn2_w_in, m_ffn2_w_out, m_sb_w_qkv, m_sb_w_o, m_kv_norm, m_kv_w, m_swa_w_q, m_swa_sinks, m_swa_w_o, m_final_norm, v_ffn1_norm, v_ffn1_w_in, v_ffn1_w_out, v_mix_norm, v_ffn2_norm, v_ffn2_w_in, v_ffn2_w_out, v_sb_w_qkv, v_sb_w_o, v_kv_norm, v_kv_w, v_swa_w_q, v_swa_sinks, v_swa_w_o, v_final_norm):
    T = x.shape[1]
    kv_cols = SWA_KV_HEADS * HEAD_DIM
    x2 = x.reshape(T, D_MODEL)
    tgt = loss_target.reshape(T, D_MODEL)
    cos, sin = _rope_tables(T)

    w_in_l = jnp.concatenate([ffn1_w_in, ffn2_w_in], axis=0).astype(BF16)
    w_out_l = jnp.concatenate([ffn1_w_out, ffn2_w_out], axis=0).astype(BF16)
    sq_l = jnp.concatenate([sb_w_o, swa_w_q, swa_w_o], axis=0).astype(BF16)
    qkv_l = sb_w_qkv[0].astype(BF16)
    kvw_l = kv_w.astype(BF16)
    core = lax.axis_index("c").astype(jnp.int32).reshape(1)
    chip = (2 * lax.axis_index("x") + lax.axis_index("y")).astype(jnp.int32).reshape(1)
    shards = [w_in_l[:1], w_out_l[:1], qkv_l[None], sq_l, w_in_l[1:], w_out_l[1:], kvw_l[None]]
    lands = [place_own_shard(f"own_shard_{t}", s, chip) for t, s in enumerate(shards)]
    sems, shards, lands, token = gather_start("gather_start", shards, lands)
    ncopy = len(shards) * N_PEER_CHIPS

    def gathered(name, t0, t1, after):
        k0, k1 = t0 * N_PEER_CHIPS, t1 * N_PEER_CHIPS
        return gather_wait(name, list(sems[k0:k1]) + list(sems[ncopy + k0:ncopy + k1]),
                           shards[t0:t1], lands[t0:t1], after)

    w_in0, w_out0 = gathered("gather_wait_ffn0", 0, 2, token)

    def ffn_w(slot):
        return (w_in0, w_out0, 0) if slot == 0 else (w_in_r, w_out_r, slot - 1)

    def vec(a, i):
        return a[i].reshape(1, D_MODEL)

    ident = lambda w: w
    sq_prep = lambda w: w.reshape(D_MODEL, w.shape[-1])
    qscale = jnp.concatenate([jnp.full((1, D_MODEL), ATTN_SCALE, F32), jnp.ones((1, 2 * D_MODEL), F32)], axis=1)
    swa_scale = jnp.full((1, D_MODEL), ATTN_SCALE, F32)
    sinks = swa_sinks.reshape(SWA_Q_HEADS)

    h1, gate1, up1 = ffn_fwd("l0a", x2, vec(ffn1_norm, 0), *ffn_w(SLOT_FFN1[0]))
    w_qkv, w_sq = gathered("gather_wait_attn", 2, 4, h1)
    w_qkv = w_qkv.reshape(N_CHIPS, D_MODEL, QKV_COLS)
    qkv = qkv_fwd(h1, vec(mix_norm, 0), w_qkv, qscale)
    o_sb, tot = sb_fwd(qkv)
    w_in_r, w_out_r, w_kv = gathered("gather_wait_rest", 4, 7, o_sb)
    w_kv = w_kv.reshape(D_MODEL, 2 * kv_cols)
    h2 = linear_res("sb_out", o_sb, w_sq, SQ_SB_O, h1)
    h3, gate2, up2 = ffn_fwd("l0b", h2, vec(ffn2_norm, 0), *ffn_w(SLOT_FFN2[0]))
    kvn = kv_norm.reshape(1, D_MODEL)
    k_sw = rms_linear("kv_k", h3, kvn, w_kv, pl.BlockSpec((D_MODEL, kv_cols), lambda i, j: (0, 0)), ident,
                      kv_cols, kv_cols, rope=(cos, sin))
    v_sw = rms_linear("kv_v", h3, kvn, w_kv, pl.BlockSpec((D_MODEL, kv_cols), lambda i, j: (0, 1)), ident,
                      kv_cols, kv_cols)
    h4, gate3, up3 = ffn_fwd("l1a", h3, vec(ffn1_norm, 1), *ffn_w(SLOT_FFN1[1]))
    q_sw = rms_linear("swa_q", h4, vec(mix_norm, 1), w_sq,
                      pl.BlockSpec((N_CHIPS, None, SQ_ROWS, 512), lambda i, j: (0, SQ_SWA_Q, 0, j)), sq_prep,
                      D_MODEL, 512, rope=(cos, sin), scale=swa_scale)
    o_sw, lse = swa_fwd(q_sw, k_sw, v_sw, sinks)
    h5 = linear_res("swa_out", o_sw, w_sq, SQ_SWA_O, h4)
    h6, gate4, up4 = ffn_fwd("l1b", h5, vec(ffn2_norm, 1), *ffn_w(SLOT_FFN2[1]))
    dh6, loss_p, d_final = loss_bwd(h6, final_norm.reshape(1, D_MODEL), tgt)

    slab = {"in": None, "out": None, "sq": None}
    in_shape = (2, N_CHIPS, 4, D_MODEL // 2, FF_CHUNK)
    out_shape = (2, N_CHIPS, 4, FF_ROWS, D_MODEL // 2)
    sq_shape = (2, N_CHIPS, 3, SQ_ROWS, D_MODEL // 2)

    def ffn_grads(tag, dh, h_in, g, gate, up, slot):
        dh_in, xn, dg_, du_, act, dhb, dnorm = ffn_bwd(tag, dh, h_in, g, gate, up, *ffn_w(slot))
        blk = (None, 1, None, D_MODEL // 2, FF_CHUNK)
        slab["in"] = mm_tn(f"dw_gate_{tag}", xn, dg_, D_MODEL // 2, FF_CHUNK, blk,
                           lambda k, n: (k, n, slot, 0, 0), in_shape, prev=slab["in"])
        slab["in"] = mm_tn(f"dw_up_{tag}", xn, du_, D_MODEL // 2, FF_CHUNK, blk,
                           lambda k, n: (k, 2 + n, slot, 0, 0), in_shape, prev=slab["in"])
        slab["out"] = mm_tn(f"dw_out_{tag}", act, dhb, FF_CHUNK, D_MODEL // 2,
                            (None, 2, None, FF_ROWS, D_MODEL // 2),
                            lambda k, n: (n, k, slot, 0, 0), out_shape, prev=slab["out"])
        return dh_in, dnorm

    def sq_grad(tag, a, dyb, t):
        slab["sq"] = mm_tn(f"dw_sq_{tag}", a, dyb, D_MODEL, D_MODEL // 2,
                           (None, N_CHIPS, None, SQ_ROWS, D_MODEL // 2),
                           lambda k, n: (n, 0, t, 0, 0), sq_shape, prev=slab["sq"])

    dh5, d_ffn2_1 = ffn_grads("l1b", dh6, h5, vec(ffn2_norm, 1), gate4, up4, SLOT_FFN2[1])
    do_sw, dh5b = linear_bwd_plain("swa_out_bwd", dh5, w_sq, SQ_SWA_O)
    sq_grad("swa_o", o_sw, dh5b, SQ_SWA_O)
    dq_sw, kv_own, kv_prev, d_sinks = swa_bwd(q_sw, k_sw, v_sw, sinks, do_sw, lse, cos, sin)
    sq_w_spec = pl.BlockSpec((N_CHIPS, None, SQ_ROWS, D_MODEL), lambda i, j: (0, SQ_SWA_Q, 0, 0))
    dh4, hn4, d_mix_1 = linear_bwd_rms("swa_q_bwd", [(dq_sw, w_sq, sq_w_spec, sq_prep)], h4, vec(mix_norm, 1), dh5,
                                       1, D_MODEL)
    sq_grad("swa_q", hn4, dq_sw, SQ_SWA_Q)
    dh3a, d_ffn1_1 = ffn_grads("l1a", dh4, h3, vec(ffn1_norm, 1), gate3, up3, SLOT_FFN1[1])
    dkv = kv_grad_combine(kv_own, kv_prev, cos, sin)
    kv_w_spec = pl.BlockSpec((D_MODEL, 2 * kv_cols), lambda i, j: (0, 0))
    dh3, xn3, d_kvn = linear_bwd_rms("kv_bwd", [(dkv, w_kv, kv_w_spec, ident)], h3, kvn, dh3a, 1, 2 * kv_cols)
    slab_kv = mm_tn("dw_kv", xn3, dkv, D_MODEL, kv_cols, (None, N_CHIPS, None, SQ_ROWS, kv_cols),
                    lambda k, n: (n, 0, 0, 0, 0), (2, N_CHIPS, 1, SQ_ROWS, kv_cols))
    dh2, d_ffn2_0 = ffn_grads("l0b", dh3, h2, vec(ffn2_norm, 0), gate2, up2, SLOT_FFN2[0])
    do_sb, dh2b = linear_bwd_plain("sb_out_bwd", dh2, w_sq, SQ_SB_O)
    sq_grad("sb_o", o_sb, dh2b, SQ_SB_O)
    dq_sb, dk_sb, dv_sb = sb_bwd(qkv, do_sb, tot)
    dqkv = jnp.concatenate([dq_sb, dk_sb, dv_sb], axis=1)
    dh1, hn1, d_mix_0 = qkv_bwd(dqkv, w_qkv, h1, vec(mix_norm, 0), dh2)
    slab_qkv = mm_tn("dw_qkv", hn1, dqkv, D_MODEL // 2, QKV_COLS, (None, 1, None, D_MODEL // 2, QKV_COLS),
                     lambda k, n: (k, n, 0, 0, 0), (2, N_CHIPS, 1, D_MODEL // 2, QKV_COLS))
    dx, d_ffn1_0 = ffn_grads("l0a", dh1, x2, vec(ffn1_norm, 0), gate1, up1, SLOT_FFN1[0])

    slabs = [slab["in"], slab["out"], slab["sq"], slab_qkv, slab_kv]
    names = ["in", "out", "sq", "qkv", "kv"]
    theirs = exchange_halves(slabs)
    parts = [add_sibling(f"add_sibling_{nm}", s, t, core) for nm, s, t in zip(names, slabs, theirs)]
    gathered = exchange_chip_partials(parts)
    halves = [sum_chips(f"sum_chips_{nm}", g, p, chip) for nm, g, p in zip(names, gathered, parts)]
    sib_halves = share_reduced_halves(halves)
    g_in, g_out, g_sq, g_qkv, g_kv = zip(halves, sib_halves)

    def upd(name, w, m, v, g_pair, slot0, row_halves):
        shp = w.shape
        w3 = w.reshape((-1,) + shp[-2:])
        outs = adamw_shard(name, w3, m.reshape(w3.shape), v.reshape(w3.shape), g_pair[0], g_pair[1], core,
                           slot0, row_halves)
        return [o.reshape(shp) for o in outs]

    r_ffn1_in = upd("adamw_ffn1_in", ffn1_w_in, m_ffn1_w_in, v_ffn1_w_in, g_in, 0, True)
    r_ffn2_in = upd("adamw_ffn2_in", ffn2_w_in, m_ffn2_w_in, v_ffn2_w_in, g_in, 2, True)
    r_ffn1_out = upd("adamw_ffn1_out", ffn1_w_out, m_ffn1_w_out, v_ffn1_w_out, g_out, 0, False)
    r_ffn2_out = upd("adamw_ffn2_out", ffn2_w_out, m_ffn2_w_out, v_ffn2_w_out, g_out, 2, False)
    r_qkv = upd("adamw_qkv", sb_w_qkv, m_sb_w_qkv, v_sb_w_qkv, g_qkv, 0, True)
    r_sb_o = upd("adamw_sb_o", sb_w_o, m_sb_w_o, v_sb_w_o, g_sq, SQ_SB_O, False)
    r_swa_q = upd("adamw_swa_q", swa_w_q, m_swa_w_q, v_swa_w_q, g_sq, SQ_SWA_Q, False)
    r_swa_o = upd("adamw_swa_o", swa_w_o, m_swa_w_o, v_swa_w_o, g_sq, SQ_SWA_O, False)
    r_kv = upd("adamw_kv", kv_w, m_kv_w, v_kv_w, g_kv, 0, False)

    loss_row = jnp.pad(loss_p, ((0, 0), (0, D_MODEL - LANES)))
    d_sink_row = d_sinks[0, :SWA_Q_HEADS]
    part = _pack_small(jnp.concatenate([d_ffn1_0, d_ffn1_1], axis=0), jnp.concatenate([d_mix_0, d_mix_1], axis=0),
                       jnp.concatenate([d_ffn2_0, d_ffn2_1], axis=0), d_kvn, d_final, d_sink_row, loss_row)
    zrow = jnp.zeros((1, D_MODEL), F32)
    small = small_allreduce_adamw(
        part,
        _pack_small(ffn1_norm, mix_norm, ffn2_norm, kv_norm, final_norm, swa_sinks, zrow),
        _pack_small(m_ffn1_norm, m_mix_norm, m_ffn2_norm, m_kv_norm, m_final_norm, m_swa_sinks, zrow),
        _pack_small(v_ffn1_norm, v_mix_norm, v_ffn2_norm, v_kv_norm, v_final_norm, v_swa_sinks, zrow))

    def unpack(p):
        return dict(ffn1_norm=p[0:2], mix_norm=p[2:4], ffn2_norm=p[4:6], kv_norm=p[6], final_norm=p[7],
                    swa_sinks=p[8:9, :SWA_Q_HEADS])

    big = dict(ffn1_w_in=r_ffn1_in, ffn1_w_out=r_ffn1_out, ffn2_w_in=r_ffn2_in, ffn2_w_out=r_ffn2_out,
               sb_w_qkv=r_qkv, sb_w_o=r_sb_o, kv_w=r_kv, swa_w_q=r_swa_q, swa_w_o=r_swa_o)
    order = ["ffn1_norm", "ffn1_w_in", "ffn1_w_out", "mix_norm", "ffn2_norm", "ffn2_w_in", "ffn2_w_out",
             "sb_w_qkv", "sb_w_o", "kv_norm", "kv_w", "swa_w_q", "swa_sinks", "swa_w_o", "final_norm"]
    outs = []
    for kind in range(4):
        sm = unpack(small[kind])
        for nm in order:
            outs.append(big[nm][kind] if nm in big else sm[nm])
    loss = small[0][9, 0]
    return (loss, dx.reshape(x.shape), *outs)
```

```python
import functools

import jax
import jax.numpy as jnp
from jax import lax
from jax.experimental import pallas as pl
from jax.experimental.pallas import tpu as pltpu

F32 = jnp.float32
BF16 = jnp.bfloat16
MESH = pl.DeviceIdType.MESH

D_MODEL = 1024
D_FF = 2816
HEAD_DIM = 64
SB_HEADS = 16
SWA_Q_HEADS = 16
SWA_KV_HEADS = 4
WINDOW = 128
ROPE_THETA = 10000.0
RMS_EPS = 1e-6
FFN_RES_SCALE = 0.5
ATTN_SCALE = HEAD_DIM ** -0.5

ADAM_LR = 0.001
ADAM_B1 = 0.9
ADAM_B2 = 0.999
ADAM_EPS = 1e-08
ADAM_WD = 0.01
ADAM_STEP = 10

N_CHIPS = 4
N_DEV = 8
LANES = 128
FF_CHUNK = D_FF // 2
FF_ROWS = D_FF // N_CHIPS
SQ_ROWS = D_MODEL // N_CHIPS
QKV_COLS = 3 * D_MODEL // N_CHIPS
VMEM_LIMIT = 56 * 1024 * 1024
NEG_BIG = -1e30

SLOT_FFN1 = (0, 1)
SLOT_FFN2 = (2, 3)
SQ_SB_O, SQ_SWA_Q, SQ_SWA_O = 0, 1, 2


def _cparams():
    return pltpu.CompilerParams(vmem_limit_bytes=VMEM_LIMIT)


def _dot(a, b):
    return jnp.dot(a, b, preferred_element_type=F32)


def _dot_nt(a, b):
    return lax.dot_general(a, b, (((1,), (1,)), ((), ())), preferred_element_type=F32)


def _dot_tn(a, b):
    return lax.dot_general(a, b, (((0,), (0,)), ((), ())), preferred_element_type=F32)


def _rstd(h):
    return lax.rsqrt(jnp.mean(h * h, axis=-1, keepdims=True) + RMS_EPS)


def _swap32(x):
    n = x.shape[-1]
    lane = lax.broadcasted_iota(jnp.int32, x.shape, x.ndim - 1)
    first = (lane % HEAD_DIM) < (HEAD_DIM // 2)
    return jnp.where(first, pltpu.roll(x, n - HEAD_DIM // 2, x.ndim - 1), pltpu.roll(x, HEAD_DIM // 2, x.ndim - 1))


def _tile_lanes(t, n):
    return t if n == LANES else jnp.tile(t, (1, n // LANES))


FFN_ROWS = 256


def _ffn_w_in_spec(slot):
    return pl.BlockSpec((N_CHIPS, None, D_MODEL, FF_CHUNK), lambda i: (0, slot, 0, 0), pipeline_mode=pl.Buffered(1))


def _ffn_w_out_spec(slot):
    return pl.BlockSpec((N_CHIPS, None, FF_ROWS, D_MODEL), lambda i: (0, slot, 0, 0), pipeline_mode=pl.Buffered(1))


def ffn_fwd(tag, h, g, w_in, w_out, slot):
    T = h.shape[0]
    tm = FFN_ROWS
    nch = D_FF // FF_CHUNK

    def body(h_ref, g_ref, wi_ref, wo_ref, out_ref, gate_ref, up_ref):
        hh = h_ref[...]
        xn = (hh * _rstd(hh) * g_ref[...]).astype(BF16)
        acc = None
        for j in range(nch):
            cols = slice(j * FF_CHUNK, (j + 1) * FF_CHUNK)
            gate = _dot(xn, wi_ref[j])
            up = _dot(xn, wi_ref[nch + j])
            gate_ref[:, cols] = gate.astype(BF16)
            up_ref[:, cols] = up.astype(BF16)
            a = (gate * jax.nn.sigmoid(gate) * up).astype(BF16)
            part = _dot(a, wo_ref[2 * j:2 * j + 2].reshape(FF_CHUNK, D_MODEL))
            acc = part if acc is None else acc + part
        out_ref[...] = hh + FFN_RES_SCALE * acc

    row = pl.BlockSpec((tm, D_MODEL), lambda i: (i, 0))
    ff = pl.BlockSpec((tm, D_FF), lambda i: (i, 0))
    return pl.pallas_call(
        body,
        name=f"ffn_fwd_{tag}",
        grid=(T // tm,),
        in_specs=[row, pl.BlockSpec((1, D_MODEL), lambda i: (0, 0)), _ffn_w_in_spec(slot), _ffn_w_out_spec(slot)],
        out_specs=[row, ff, ff],
        out_shape=[
            jax.ShapeDtypeStruct((T, D_MODEL), F32),
            jax.ShapeDtypeStruct((T, D_FF), BF16),
            jax.ShapeDtypeStruct((T, D_FF), BF16),
        ],
        compiler_params=_cparams(),
    )(h, g, w_in, w_out)


def ffn_bwd(tag, dh, h, g, gate, up, w_in, w_out, slot):
    T = dh.shape[0]
    tm = FFN_ROWS
    nch = D_FF // FF_CHUNK

    def body(dh_ref, h_ref, g_ref, gate_ref, up_ref, wi_ref, wo_ref,
             dhin_ref, xn_ref, dg_ref, du_ref, a_ref, dhb_ref, dnorm_ref):
        @pl.when(pl.program_id(0) == 0)
        def _():
            dnorm_ref[...] = jnp.zeros_like(dnorm_ref)

        dhh = dh_ref[...]
        dhb = (FFN_RES_SCALE * dhh).astype(BF16)
        dhb_ref[...] = dhb
        dxn = None
        for j in range(nch):
            cols = slice(j * FF_CHUNK, (j + 1) * FF_CHUNK)
            da = _dot_nt(dhb, wo_ref[2 * j:2 * j + 2].reshape(FF_CHUNK, D_MODEL))
            gt = gate_ref[:, cols].astype(F32)
            u = up_ref[:, cols].astype(F32)
            s = jax.nn.sigmoid(gt)
            silu = gt * s
            a_ref[:, cols] = (silu * u).astype(BF16)
            dgate = (da * u * (s * (1.0 + gt * (1.0 - s)))).astype(BF16)
            dup = (da * silu).astype(BF16)
            dg_ref[:, cols] = dgate
            du_ref[:, cols] = dup
            part = _dot_nt(dgate, wi_ref[j]) + _dot_nt(dup, wi_ref[nch + j])
            dxn = part if dxn is None else dxn + part
        hh = h_ref[...]
        gg = g_ref[...]
        r = _rstd(hh)
        hr = hh * r
        xn_ref[...] = (hr * gg).astype(BF16)
        dnorm_ref[...] += jnp.sum(dxn * hr, axis=0, keepdims=True)
        gd = gg * dxn
        dhin_ref[...] = dhh + r * (gd - hr * jnp.mean(gd * hr, axis=-1, keepdims=True))

    row = pl.BlockSpec((tm, D_MODEL), lambda i: (i, 0))
    ff = pl.BlockSpec((tm, D_FF), lambda i: (i, 0))
    vec = pl.BlockSpec((1, D_MODEL), lambda i: (0, 0))
    return pl.pallas_call(
        body,
        name=f"ffn_bwd_{tag}",
        grid=(T // tm,),
        in_specs=[row, row, vec, ff, ff, _ffn_w_in_spec(slot), _ffn_w_out_spec(slot)],
        out_specs=[row, row, ff, ff, ff, row, vec],
        out_shape=[
            jax.ShapeDtypeStruct((T, D_MODEL), F32),
            jax.ShapeDtypeStruct((T, D_MODEL), BF16),
            jax.ShapeDtypeStruct((T, D_FF), BF16),
            jax.ShapeDtypeStruct((T, D_FF), BF16),
            jax.ShapeDtypeStruct((T, D_FF), BF16),
            jax.ShapeDtypeStruct((T, D_MODEL), BF16),
            jax.ShapeDtypeStruct((1, D_MODEL), F32),
        ],
        compiler_params=_cparams(),
    )(dh, h, g, gate, up, w_in, w_out)


def rms_linear(name, h, g, w, w_spec, w_prep, n_out, tn, *, rope=None, scale=None):
    T = h.shape[0]
    tm = 512
    extra, extra_specs = [], []
    if rope is not None:
        extra += list(rope)
        extra_specs += [pl.BlockSpec((tm, LANES), lambda i, j: (i, 0))] * 2
    if scale is not None:
        extra.append(scale)
        extra_specs.append(pl.BlockSpec((1, tn), lambda i, j: (0, j)))

    def body(h_ref, g_ref, w_ref, *rest):
        rest = list(rest)
        cos_ref = sin_ref = sc_ref = None
        if rope is not None:
            cos_ref, sin_ref = rest[0], rest[1]
            rest = rest[2:]
        if scale is not None:
            sc_ref = rest[0]
            rest = rest[1:]
        out_ref, xn_s = rest

        @pl.when(pl.program_id(1) == 0)
        def _():
            hh = h_ref[...]
            xn_s[...] = (hh * _rstd(hh) * g_ref[...]).astype(BF16)

        y = _dot(xn_s[...], w_prep(w_ref[...]))
        if rope is not None:
            y = y * _tile_lanes(cos_ref[...], tn) + _swap32(y) * _tile_lanes(sin_ref[...], tn)
        if scale is not None:
            y = y * sc_ref[...]
        out_ref[...] = y.astype(BF16)

    return pl.pallas_call(
        body,
        name=name,
        grid=(T // tm, n_out // tn),
        in_specs=[
            pl.BlockSpec((tm, D_MODEL), lambda i, j: (i, 0)),
            pl.BlockSpec((1, D_MODEL), lambda i, j: (0, 0)),
            w_spec,
        ] + extra_specs,
        out_specs=pl.BlockSpec((tm, tn), lambda i, j: (i, j)),
        out_shape=jax.ShapeDtypeStruct((T, n_out), BF16),
        scratch_shapes=[pltpu.VMEM((tm, D_MODEL), BF16)],
        compiler_params=_cparams(),
    )(h, g, w, *extra)


QKV_ROWS = 512


def _qkv_w_spec():
    return pl.BlockSpec((N_CHIPS, D_MODEL, QKV_COLS), lambda i: (0, 0, 0), pipeline_mode=pl.Buffered(1))


def qkv_fwd(h, g, w_qkv, scale):
    T = h.shape[0]
    tm = QKV_ROWS

    def body(h_ref, g_ref, w_ref, sc_ref, out_ref):
        hh = h_ref[...]
        xn = (hh * _rstd(hh) * g_ref[...]).astype(BF16)
        for s in range(N_CHIPS):
            cols = slice(s * QKV_COLS, (s + 1) * QKV_COLS)
            out_ref[:, cols] = (_dot(xn, w_ref[s]) * sc_ref[:, cols]).astype(BF16)

    return pl.pallas_call(
        body,
        name="sb_qkv",
        grid=(T // tm,),
        in_specs=[
            pl.BlockSpec((tm, D_MODEL), lambda i: (i, 0)),
            pl.BlockSpec((1, D_MODEL), lambda i: (0, 0)),
            _qkv_w_spec(),
            pl.BlockSpec((1, 3 * D_MODEL), lambda i: (0, 0)),
        ],
        out_specs=pl.BlockSpec((tm, 3 * D_MODEL), lambda i: (i, 0)),
        out_shape=jax.ShapeDtypeStruct((T, 3 * D_MODEL), BF16),
        compiler_params=_cparams(),
    )(h, g, w_qkv, scale)


def qkv_bwd(dy, w_qkv, h, g, dres):
    T = h.shape[0]
    tm = QKV_ROWS

    def body(dy_ref, w_ref, h_ref, g_ref, dres_ref, dh_ref, xn_ref, dg_ref):
        @pl.when(pl.program_id(0) == 0)
        def _():
            dg_ref[...] = jnp.zeros_like(dg_ref)

        dxn = None
        for s in range(N_CHIPS):
            part = _dot_nt(dy_ref[:, s * QKV_COLS:(s + 1) * QKV_COLS], w_ref[s])
            dxn = part if dxn is None else dxn + part
        hh = h_ref[...]
        gg = g_ref[...]
        r = _rstd(hh)
        hr = hh * r
        xn_ref[...] = (hr * gg).astype(BF16)
        dg_ref[...] += jnp.sum(dxn * hr, axis=0, keepdims=True)
        gd = gg * dxn
        dh_ref[...] = dres_ref[...] + r * (gd - hr * jnp.mean(gd * hr, axis=-1, keepdims=True))

    row = pl.BlockSpec((tm, D_MODEL), lambda i: (i, 0))
    vec = pl.BlockSpec((1, D_MODEL), lambda i: (0, 0))
    return pl.pallas_call(
        body,
        name="sb_qkv_bwd",
        grid=(T // tm,),
        in_specs=[pl.BlockSpec((tm, 3 * D_MODEL), lambda i: (i, 0)), _qkv_w_spec(), row, vec, row],
        out_specs=[row, row, vec],
        out_shape=[
            jax.ShapeDtypeStruct((T, D_MODEL), F32),
            jax.ShapeDtypeStruct((T, D_MODEL), BF16),
            jax.ShapeDtypeStruct((1, D_MODEL), F32),
        ],
        compiler_params=_cparams(),
    )(dy, w_qkv, h, g, dres)


def linear_res(name, a, w_sq, t, res):
    T = a.shape[0]
    tm = 512

    def body(a_ref, w_ref, res_ref, out_ref):
        out_ref[...] = res_ref[...] + _dot(a_ref[...], w_ref[...].reshape(D_MODEL, D_MODEL))

    row = pl.BlockSpec((tm, D_MODEL), lambda i: (i, 0))
    return pl.pallas_call(
        body,
        name=name,
        grid=(T // tm,),
        in_specs=[row, pl.BlockSpec((N_CHIPS, None, SQ_ROWS, D_MODEL), lambda i: (0, t, 0, 0)), row],
        out_specs=row,
        out_shape=jax.ShapeDtypeStruct((T, D_MODEL), F32),
        compiler_params=_cparams(),
    )(a, w_sq, res)


def linear_bwd_plain(name, dy, w_sq, t):
    T = dy.shape[0]
    tm = 512

    def body(dy_ref, w_ref, da_ref, dyb_ref):
        dyb = dy_ref[...].astype(BF16)
        dyb_ref[...] = dyb
        da_ref[...] = _dot_nt(dyb, w_ref[...].reshape(D_MODEL, D_MODEL)).astype(BF16)

    row = pl.BlockSpec((tm, D_MODEL), lambda i: (i, 0))
    return pl.pallas_call(
        body,
        name=name,
        grid=(T // tm,),
        in_specs=[row, pl.BlockSpec((N_CHIPS, None, SQ_ROWS, D_MODEL), lambda i: (0, t, 0, 0))],
        out_specs=[row, row],
        out_shape=[jax.ShapeDtypeStruct((T, D_MODEL), BF16), jax.ShapeDtypeStruct((T, D_MODEL), BF16)],
        compiler_params=_cparams(),
    )(dy, w_sq)


def linear_bwd_rms(name, pairs, h, g, dres, nch, tn, tm=256):
    T = h.shape[0]
    npair = len(pairs)

    def body(*refs):
        dy_refs = refs[:npair]
        w_refs = refs[npair:2 * npair]
        h_ref, g_ref, dres_ref, dh_ref, xn_ref, dg_ref, acc_s = refs[2 * npair:]
        i = pl.program_id(0)
        j = pl.program_id(1)

        @pl.when(j == 0)
        def _():
            acc_s[...] = jnp.zeros_like(acc_s)

        @pl.when((i == 0) & (j == 0))
        def _():
            dg_ref[...] = jnp.zeros_like(dg_ref)

        part = None
        for p in range(npair):
            d = _dot_nt(dy_refs[p][...], pairs[p][3](w_refs[p][...]))
            part = d if part is None else part + d
        acc_s[...] += part

        @pl.when(j == nch - 1)
        def _():
            dxn = acc_s[...]
            hh = h_ref[...]
            gg = g_ref[...]
            r = _rstd(hh)
            hr = hh * r
            xn_ref[...] = (hr * gg).astype(BF16)
            dg_ref[...] += jnp.sum(dxn * hr, axis=0, keepdims=True)
            gd = gg * dxn
            dh_ref[...] = dres_ref[...] + r * (gd - hr * jnp.mean(gd * hr, axis=-1, keepdims=True))

    row = pl.BlockSpec((tm, D_MODEL), lambda i, j: (i, 0))
    vec = pl.BlockSpec((1, D_MODEL), lambda i, j: (0, 0))
    return pl.pallas_call(
        body,
        name=name,
        grid=(T // tm, nch),
        in_specs=[pl.BlockSpec((tm, tn), lambda i, j: (i, j))] * npair + [p[2] for p in pairs] + [row, vec, row],
        out_specs=[row, row, vec],
        out_shape=[
            jax.ShapeDtypeStruct((T, D_MODEL), F32),
            jax.ShapeDtypeStruct((T, D_MODEL), BF16),
            jax.ShapeDtypeStruct((1, D_MODEL), F32),
        ],
        scratch_shapes=[pltpu.VMEM((tm, D_MODEL), F32)],
        compiler_params=_cparams(),
    )(*[p[0] for p in pairs], *[p[1] for p in pairs], h, g, dres)


def loss_bwd(h, g, tgt):
    T = h.shape[0]
    tm = 512

    def body(h_ref, g_ref, t_ref, dh_ref, loss_ref, dg_ref):
        @pl.when(pl.program_id(0) == 0)
        def _():
            loss_ref[...] = jnp.zeros_like(loss_ref)
            dg_ref[...] = jnp.zeros_like(dg_ref)

        hh = h_ref[...]
        gg = g_ref[...]
        r = _rstd(hh)
        hr = hh * r
        err = hr * gg - t_ref[...]
        loss_ref[...] += 0.5 * jnp.sum(jnp.mean(err * err, axis=-1, keepdims=True), axis=0, keepdims=True)
        dy = err * (1.0 / D_MODEL)
        dg_ref[...] += jnp.sum(dy * hr, axis=0, keepdims=True)
        gd = gg * dy
        dh_ref[...] = r * (gd - hr * jnp.mean(gd * hr, axis=-1, keepdims=True))

    row = pl.BlockSpec((tm, D_MODEL), lambda i: (i, 0))
    vec = pl.BlockSpec((1, D_MODEL), lambda i: (0, 0))
    return pl.pallas_call(
        body,
        name="loss_bwd",
        grid=(T // tm,),
        in_specs=[row, vec, row],
        out_specs=[row, pl.BlockSpec((1, LANES), lambda i: (0, 0)), vec],
        out_shape=[
            jax.ShapeDtypeStruct((T, D_MODEL), F32),
            jax.ShapeDtypeStruct((1, LANES), F32),
            jax.ShapeDtypeStruct((1, D_MODEL), F32),
        ],
        compiler_params=_cparams(),
    )(h, g, tgt)


DW_TOKENS = 4096


def mm_tn(name, a, b, tk, tn, out_block, out_index, out_shape, prev=None, tt=DW_TOKENS):
    T = a.shape[0]
    ns, r = out_block[1], out_block[3]
    tt = min(tt, T)
    nt = T // tt

    def body(*refs):
        if prev is None:
            a_ref, b_ref, out_ref = refs
        else:
            a_ref, b_ref, _, out_ref = refs
        t = pl.program_id(2)
        res = _dot_tn(a_ref[...], b_ref[...])

        @pl.when(t == 0)
        def _():
            for u in range(ns):
                out_ref[u] = res[u * r:(u + 1) * r]

        @pl.when(t > 0)
        def _():
            for u in range(ns):
                out_ref[u] += res[u * r:(u + 1) * r]

    in_specs = [
        pl.BlockSpec((tt, tk), lambda k, n, t: (t, k)),
        pl.BlockSpec((tt, tn), lambda k, n, t: (t, n)),
    ]
    args = [a, b]
    aliases = {}
    if prev is not None:
        in_specs.append(pl.BlockSpec(memory_space=pl.ANY))
        args.append(prev)
        aliases = {2: 0}
    return pl.pallas_call(
        body,
        name=name,
        grid=(a.shape[1] // tk, b.shape[1] // tn, nt),
        in_specs=in_specs,
        out_specs=pl.BlockSpec(out_block, lambda k, n, t: out_index(k, n)),
        out_shape=jax.ShapeDtypeStruct(out_shape, F32),
        input_output_aliases=aliases,
        compiler_params=_cparams(),
    )(*args)


SB_BLOCK = 256
SB_QROWS = 256
SB_QROWS_BWD = 256
SB_UNDERFLOW_BITS = 140.0
SB_CHUNK = 128


LOG2E = 1.4426950408889634


def _softplus2(z2):
    sign = jnp.uint32(0x80000000)
    neg_abs = lax.bitcast_convert_type(lax.bitcast_convert_type(z2, jnp.uint32) | sign, F32)
    return jnp.log2(1.0 + jnp.exp2(neg_abs)) + jnp.maximum(z2, 0.0)


def _twice(x):
    return jnp.concatenate([x, x], axis=1)


def sb_fwd(qkv):
    T = qkv.shape[0]
    tq, tk = SB_QROWS, SB_BLOCK
    ratio = tq // tk
    npair = SB_HEADS // 2

    def body(q_ref, k_ref, v_ref, o_ref, tot_ref, first_ref, acc_s, c_s, z_s, w_s, kmax_s):
        p = pl.program_id(0)
        i = pl.program_id(1)

        @pl.when(i == 0)
        def _():
            kmax_s[...] = jnp.max(jnp.abs(k_ref[...]), axis=0, keepdims=True).astype(F32)

        q = q_ref[...]
        lane = lax.broadcasted_iota(jnp.int32, (tq, LANES), 1)
        first = lane < HEAD_DIM
        zero = jnp.zeros_like(q)
        q_heads = (jnp.where(first, q, zero), jnp.where(first, zero, q))
        row = lax.broadcasted_iota(jnp.int32, (tq, tk), 0)
        col = lax.broadcasted_iota(jnp.int32, (tq, tk), 1)
        visible = [col + r * tk < row for r in range(ratio)]
        krow = lax.broadcasted_iota(jnp.int32, (tk, tk), 0)
        kcol = lax.broadcasted_iota(jnp.int32, (tk, tk), 1)
        from_s = (krow >= kcol).astype(BF16)
        acc_s[...] = jnp.zeros_like(acc_s)
        c_s[...] = jnp.zeros_like(c_s)

        def rows(j):
            return pl.ds(pl.multiple_of(j * tk, tk), tk)

        def logits(j):
            kb = k_ref[rows(j), :]
            for hd in range(2):
                z_s[hd] = _dot_nt(q_heads[hd], kb) * LOG2E

        def flush(j):
            vb = v_ref[rows(j), :]
            for hd in range(2):
                acc_s[hd] += _dot(w_s[hd], vb)

        def block(j, mask=None, walked_before=True):
            if walked_before:
                flush(j + 1)
            chunks = [(hd, slice(r0, r0 + SB_CHUNK)) for hd in range(2) for r0 in range(0, tq, SB_CHUNK)]
            k_next = k_ref[rows(jnp.maximum(j - 1, 0)), :]
            es, sums = [], []
            for hd, rs in chunks:
                z2 = z_s[hd, rs, :]
                z_s[hd, rs, :] = _dot_nt(q_heads[hd][rs, :], k_next) * LOG2E
                if mask is not None:
                    z2 = jnp.where(mask[rs, :], z2, NEG_BIG)
                sp = _softplus2(z2)
                c = c_s[hd, rs, :]
                es.append(z2 + _twice(c))
                c_s[hd, rs, :] = c - jnp.sum(sp, axis=1, keepdims=True)
                sums.append(_dot(sp.astype(BF16), from_s))
            for (hd, rs), e, s in zip(chunks, es, sums):
                w_s[hd, rs, :] = jnp.exp2(e - s).astype(BF16)

        z_bound = [LOG2E * jnp.sum(jnp.abs(q_heads[hd].astype(F32)) * kmax_s[...], axis=1, keepdims=True)
                   for hd in range(2)]

        def more_keys_matter():
            top = jnp.maximum(c_s[0] + z_bound[0], c_s[1] + z_bound[1])
            return (jnp.max(top) >= -SB_UNDERFLOW_BITS).astype(jnp.int32)

        logits(ratio * i + ratio - 1)
        for r in reversed(range(ratio)):
            block(ratio * i + r, visible[r], walked_before=(r != ratio - 1))

        def trip(carry):
            trips, _ = carry
            for r in reversed(range(ratio)):
                block(ratio * (i - 1 - trips) + r)
            return trips + 1, more_keys_matter()

        trips, _ = lax.while_loop(lambda carry: jnp.logical_and(carry[0] < i, carry[1] > 0), trip,
                                  (jnp.int32(0), more_keys_matter()))
        first_walked = ratio * (i - trips)
        flush(first_walked)
        first_ref[p, i] = first_walked.astype(F32)
        o_ref[...] = jnp.where(first, acc_s[0], acc_s[1]).astype(BF16)
        tot_ref[...] = jnp.where(first, c_s[0], c_s[1])

    return pl.pallas_call(
        body,
        name="sb_fwd",
        grid=(npair, T // tq),
        in_specs=[
            pl.BlockSpec((tq, LANES), lambda p, i: (i, p)),
            pl.BlockSpec((T, LANES), lambda p, i: (0, npair + p)),
            pl.BlockSpec((T, LANES), lambda p, i: (0, 2 * npair + p)),
        ],
        out_specs=[pl.BlockSpec((tq, LANES), lambda p, i: (i, p))] * 2 + [pl.BlockSpec(memory_space=pltpu.SMEM)],
        out_shape=[jax.ShapeDtypeStruct((T, D_MODEL), BF16), jax.ShapeDtypeStruct((T, D_MODEL), F32),
                   jax.ShapeDtypeStruct((npair, T // tq), F32)],
        scratch_shapes=[
            pltpu.VMEM((2, tq, LANES), F32), pltpu.VMEM((2, tq, LANES), F32),
            pltpu.VMEM((2, tq, tk), F32), pltpu.VMEM((2, tq, tk), BF16),
            pltpu.VMEM((1, LANES), F32),
        ],
        compiler_params=_cparams(),
    )(qkv, qkv, qkv)


def sb_bwd(qkv, do, tot, first_block):
    T = qkv.shape[0]
    tq, tk = SB_QROWS_BWD, SB_BLOCK
    ratio = tq // tk
    npair = SB_HEADS // 2
    nq = T // tq
    nk = T // tk
    assert SB_QROWS == SB_QROWS_BWD

    def body(first_ref, q_ref, k_ref, v_ref, do_ref, tot_ref, dq_ref, dk_ref, dv_ref,
             dkt_s, dvt_s, dq_s, rest_s, cg_s, z_s, da_s, dz_s, a_s):
        i = pl.program_id(1)
        start = jnp.clip(first_ref[pl.program_id(0), i].astype(jnp.int32), 0, ratio * i)

        @pl.when(i == 0)
        def _():
            dkt_s[...] = jnp.zeros_like(dkt_s)
            dvt_s[...] = jnp.zeros_like(dvt_s)

        q = q_ref[...]
        do_ = do_ref[...]
        tot_ = tot_ref[...]
        q_t = q.astype(F32).T.astype(BF16)
        do_t = do_.astype(F32).T.astype(BF16)
        lane = lax.broadcasted_iota(jnp.int32, (tq, LANES), 1)
        first = lane < HEAD_DIM
        zero = jnp.zeros_like(q)
        q_heads = (jnp.where(first, q, zero), jnp.where(first, zero, q))
        do_heads = (jnp.where(first, do_, zero), jnp.where(first, zero, do_))
        row = lax.broadcasted_iota(jnp.int32, (tq, tk), 0)
        col = lax.broadcasted_iota(jnp.int32, (tq, tk), 1)
        visible = [col + r * tk < row for r in range(ratio)]
        krow = lax.broadcasted_iota(jnp.int32, (tk, tk), 0)
        kcol = lax.broadcasted_iota(jnp.int32, (tk, tk), 1)
        before = (krow < kcol).astype(BF16)
        from_s = (krow >= kcol).astype(BF16)
        last = ratio * i + ratio - 1
        rest_s[0] = jnp.broadcast_to(tot_[:, 0:1], (tq, LANES))
        rest_s[1] = jnp.broadcast_to(tot_[:, HEAD_DIM:HEAD_DIM + 1], (tq, LANES))
        cg_s[...] = jnp.zeros_like(cg_s)
        dq_s[...] = jnp.zeros_like(dq_s)
        dz_s[...] = jnp.zeros_like(dz_s)
        a_s[...] = jnp.zeros_like(a_s)

        def rows(j):
            return pl.ds(pl.multiple_of(j * tk, tk), tk)

        def logits(j):
            kb = k_ref[rows(j), :]
            vb = v_ref[rows(j), :]
            for hd in range(2):
                z_s[hd] = _dot_nt(q_heads[hd], kb) * LOG2E
                da_s[hd] = _dot_nt(do_heads[hd], vb)

        def flush(j):
            kb = k_ref[rows(j), :]
            for hd in range(2):
                dims = slice(hd * HEAD_DIM, (hd + 1) * HEAD_DIM)
                dq_s[hd] += _dot(dz_s[hd], kb)
                dkt_s[j, dims, :] += _dot(q_t[dims, :], dz_s[hd])
                dvt_s[j, dims, :] += _dot(do_t[dims, :], a_s[hd])

        def block(j, mask=None):
            flush(jnp.maximum(j - 1, 0))
            chunks = [(hd, slice(r0, r0 + SB_CHUNK)) for hd in range(2) for r0 in range(0, tq, SB_CHUNK)]
            nxt = rows(jnp.minimum(j + 1, last))
            k_next = k_ref[nxt, :]
            v_next = v_ref[nxt, :]
            stage1 = []
            for hd, rs in chunks:
                z2 = z_s[hd, rs, :]
                z_s[hd, rs, :] = _dot_nt(q_heads[hd][rs, :], k_next) * LOG2E
                if mask is not None:
                    z2 = jnp.where(mask[rs, :], z2, NEG_BIG)
                sp = _softplus2(z2)
                rest = rest_s[hd, rs, :] + jnp.sum(sp, axis=1, keepdims=True)
                rest_s[hd, rs, :] = rest
                stage1.append((z2 + _twice(rest), z2 - sp, _dot(sp.astype(BF16), from_s)))
            stage2 = []
            for (hd, rs), (e, log2_beta, ahead) in zip(chunks, stage1):
                a = jnp.exp2(e - ahead)
                g = a * da_s[hd, rs, :]
                da_s[hd, rs, :] = _dot_nt(do_heads[hd][rs, :], v_next)
                cg = cg_s[hd, rs, :]
                a_s[hd, rs, :] = a.astype(BF16)
                cg_s[hd, rs, :] = cg + jnp.sum(g, axis=1, keepdims=True)
                stage2.append((g, g + _twice(cg), log2_beta, _dot(g.astype(BF16), before)))
            for (hd, rs), (g, g_from, log2_beta, g_before) in zip(chunks, stage2):
                dz_s[hd, rs, :] = (g - jnp.exp2(log2_beta) * (g_from + g_before)).astype(BF16)

        logits(start)

        @pl.loop(start, ratio * i)
        def _(j):
            block(j)

        for r in range(ratio):
            block(ratio * i + r, visible[r])
        flush(last)
        dq_ref[...] = (jnp.where(first, dq_s[0], dq_s[1]) * ATTN_SCALE).astype(BF16)

        @pl.when(i == nq - 1)
        def _():
            @pl.loop(0, nk)
            def _(b):
                dk_ref[rows(b), :] = dkt_s[b].T.astype(BF16)
                dv_ref[rows(b), :] = dvt_s[b].T.astype(BF16)

    qblk = pl.BlockSpec((tq, LANES), lambda p, i: (i, p))
    full = pl.BlockSpec((T, LANES), lambda p, i: (0, p))
    return pl.pallas_call(
        body,
        name="sb_bwd",
        grid=(npair, nq),
        in_specs=[
            pl.BlockSpec(memory_space=pltpu.SMEM),
            qblk,
            pl.BlockSpec((T, LANES), lambda p, i: (0, npair + p)),
            pl.BlockSpec((T, LANES), lambda p, i: (0, 2 * npair + p)),
            qblk, qblk,
        ],
        out_specs=[qblk, full, full],
        out_shape=[jax.ShapeDtypeStruct((T, D_MODEL), BF16)] * 3,
        scratch_shapes=[
            pltpu.VMEM((nk, LANES, tk), F32), pltpu.VMEM((nk, LANES, tk), F32),
            pltpu.VMEM((2, tq, LANES), F32), pltpu.VMEM((2, tq, LANES), F32), pltpu.VMEM((2, tq, LANES), F32),
            pltpu.VMEM((2, tq, tk), F32), pltpu.VMEM((2, tq, tk), F32),
            pltpu.VMEM((2, tq, tk), BF16), pltpu.VMEM((2, tq, tk), BF16),
        ],
        compiler_params=_cparams(),
    )(first_block, qkv, qkv, qkv, do, tot)


def _swa_valid(n):
    qi = lax.broadcasted_iota(jnp.int32, (WINDOW, 2 * WINDOW), 0)
    ki = lax.broadcasted_iota(jnp.int32, (WINDOW, 2 * WINDOW), 1)
    diff = qi + WINDOW - ki
    return (diff >= 0) & (diff < WINDOW) & ((n > 0) | (ki >= WINDOW))


def _to_half(x, first, src, dst):
    keep = first if src == 0 else jnp.logical_not(first)
    x = jnp.where(keep, x, jnp.zeros_like(x))
    if src != dst:
        x = pltpu.roll(x.astype(F32), HEAD_DIM, 1).astype(BF16)
    return x


def _kv_band(prev_ref, cur_ref, pb):
    cols = slice(pb * LANES, (pb + 1) * LANES)
    return jnp.concatenate([prev_ref[:, cols], cur_ref[:, cols]], axis=0)


def _swa_specs(T):
    nb = T // WINDOW
    kv_w = SWA_KV_HEADS * HEAD_DIM
    qrow = pl.BlockSpec((WINDOW, D_MODEL), lambda n: (n, 0))
    cur = pl.BlockSpec((WINDOW, kv_w), lambda n: (n, 0))
    prev = pl.BlockSpec((WINDOW, kv_w), lambda n: (jnp.maximum(n - 1, 0), 0))
    smem = pl.BlockSpec(memory_space=pltpu.SMEM)
    return nb, qrow, cur, prev, smem


def swa_fwd(q, k, v, sinks):
    T = q.shape[0]
    nb, qrow, cur, prev, smem = _swa_specs(T)

    def body(sink_ref, q_ref, kc_ref, kp_ref, vc_ref, vp_ref, o_ref, lse_ref):
        n = pl.program_id(0)
        lane = lax.broadcasted_iota(jnp.int32, (WINDOW, LANES), 1)
        first = lane < HEAD_DIM
        valid = _swa_valid(n)
        lse_acc = jnp.zeros((WINDOW, LANES), F32)
        for pb in range(SWA_KV_HEADS // 2):
            k2 = _kv_band(kp_ref, kc_ref, pb)
            v2 = _kv_band(vp_ref, vc_ref, pb)
            for b in range(2):
                kvh = 2 * pb + b
                for qq in range(2):
                    cols = slice((2 * kvh + qq) * LANES, (2 * kvh + qq + 1) * LANES)
                    qp = q_ref[:, cols]
                    outs = []
                    for a in range(2):
                        head = 4 * kvh + 2 * qq + a
                        qh = _to_half(qp, first, a, b)
                        s = jnp.where(valid, _dot_nt(qh, k2), NEG_BIG)
                        sink = sink_ref[head]
                        m = jnp.maximum(jnp.max(s, axis=1, keepdims=True), sink)
                        p = jnp.exp(s - m)
                        den = jnp.sum(p, axis=1, keepdims=True) + jnp.exp(sink - m)
                        o = _dot((p / den).astype(BF16), v2)
                        if a != b:
                            o = pltpu.roll(o, HEAD_DIM, 1)
                        outs.append(o)
                        lse_acc = jnp.where(lane == head, m + jnp.log(den), lse_acc)
                    o_ref[:, cols] = jnp.where(first, outs[0], outs[1]).astype(BF16)
        lse_ref[...] = lse_acc

    return pl.pallas_call(
        body,
        name="swa_fwd",
        grid=(nb,),
        in_specs=[smem, qrow, cur, prev, cur, prev],
        out_specs=[qrow, pl.BlockSpec((WINDOW, LANES), lambda n: (n, 0))],
        out_shape=[jax.ShapeDtypeStruct((T, D_MODEL), BF16), jax.ShapeDtypeStruct((T, LANES), F32)],
        compiler_params=_cparams(),
    )(sinks, q, k, k, v, v)


def swa_bwd(q, k, v, sinks, do, lse, cos, sin):
    T = q.shape[0]
    nb, qrow, cur, prev, smem = _swa_specs(T)
    kv_w = SWA_KV_HEADS * HEAD_DIM

    def body(sink_ref, q_ref, kc_ref, kp_ref, vc_ref, vp_ref, do_ref, lse_ref, cos_ref, sin_ref,
             dq_ref, own_ref, prv_ref, dsink_ref):
        n = pl.program_id(0)

        @pl.when(n == 0)
        def _():
            dsink_ref[...] = jnp.zeros_like(dsink_ref)

        lane = lax.broadcasted_iota(jnp.int32, (WINDOW, LANES), 1)
        lane1 = lax.broadcasted_iota(jnp.int32, (1, LANES), 1)
        first = lane < HEAD_DIM
        valid = _swa_valid(n)
        cos_ = cos_ref[...]
        sin_ = sin_ref[...]
        dsink = jnp.zeros((1, LANES), F32)
        for pb in range(SWA_KV_HEADS // 2):
            k2 = _kv_band(kp_ref, kc_ref, pb)
            v2 = _kv_band(vp_ref, vc_ref, pb)
            dk2 = jnp.zeros((2 * WINDOW, LANES), F32)
            dv2 = jnp.zeros((2 * WINDOW, LANES), F32)
            for b in range(2):
                kvh = 2 * pb + b
                for qq in range(2):
                    cols = slice((2 * kvh + qq) * LANES, (2 * kvh + qq + 1) * LANES)
                    qp = q_ref[:, cols]
                    dop = do_ref[:, cols]
                    dqs = []
                    for a in range(2):
                        head = 4 * kvh + 2 * qq + a
                        qh = _to_half(qp, first, a, b)
                        doh = _to_half(dop, first, a, b)
                        s = jnp.where(valid, _dot_nt(qh, k2), NEG_BIG)
                        lse_h = lse_ref[:, head:head + 1]
                        p = jnp.exp(s - lse_h)
                        dp = _dot_nt(doh, v2)
                        delta = jnp.sum(p * dp, axis=1, keepdims=True)
                        ds = (p * (dp - delta)).astype(BF16)
                        p_sink = jnp.exp(sink_ref[head] - lse_h)
                        dsink = dsink + jnp.where(lane1 == head, -jnp.sum(p_sink * delta, axis=0, keepdims=True), 0.0)
                        dq = _dot(ds, k2)
                        if a != b:
                            dq = pltpu.roll(dq, HEAD_DIM, 1)
                        dqs.append(dq)
                        dk2 = dk2 + _dot_tn(ds, qh)
                        dv2 = dv2 + _dot_tn(p.astype(BF16), doh)
                    dqp = jnp.where(first, dqs[0], dqs[1])
                    dq_ref[:, cols] = ((dqp * cos_ + _swap32(dqp * sin_)) * ATTN_SCALE).astype(BF16)
            kcols = slice(pb * LANES, (pb + 1) * LANES)
            vcols = slice(kv_w + pb * LANES, kv_w + (pb + 1) * LANES)
            prv_ref[:, kcols] = dk2[:WINDOW]
            own_ref[:, kcols] = dk2[WINDOW:]
            prv_ref[:, vcols] = dv2[:WINDOW]
            own_ref[:, vcols] = dv2[WINDOW:]
        dsink_ref[...] += dsink

    tab = pl.BlockSpec((WINDOW, LANES), lambda n: (n, 0))
    kvrow = pl.BlockSpec((WINDOW, 2 * kv_w), lambda n: (n, 0))
    return pl.pallas_call(
        body,
        name="swa_bwd",
        grid=(nb,),
        in_specs=[smem, qrow, cur, prev, cur, prev, qrow, tab, tab, tab],
        out_specs=[qrow, kvrow, kvrow, pl.BlockSpec((1, LANES), lambda n: (0, 0))],
        out_shape=[
            jax.ShapeDtypeStruct((T, D_MODEL), BF16),
            jax.ShapeDtypeStruct((T, 2 * kv_w), F32),
            jax.ShapeDtypeStruct((T, 2 * kv_w), F32),
            jax.ShapeDtypeStruct((1, LANES), F32),
        ],
        compiler_params=_cparams(),
    )(sinks, q, k, k, v, v, do, lse, cos, sin)


def kv_grad_combine(own, prv, cos, sin):
    T = own.shape[0]
    nb = T // WINDOW
    kv_w = SWA_KV_HEADS * HEAD_DIM

    def body(own_ref, nxt_ref, cos_ref, sin_ref, out_ref):
        n = pl.program_id(0)
        nxt = jnp.where(n + 1 < nb, nxt_ref[...], 0.0)
        tot = own_ref[...] + nxt
        dk = tot[:, :kv_w]
        c = _tile_lanes(cos_ref[...], kv_w)
        s = _tile_lanes(sin_ref[...], kv_w)
        out_ref[:, :kv_w] = (dk * c + _swap32(dk * s)).astype(BF16)
        out_ref[:, kv_w:] = tot[:, kv_w:].astype(BF16)

    tab = pl.BlockSpec((WINDOW, LANES), lambda n: (n, 0))
    kvrow = pl.BlockSpec((WINDOW, 2 * kv_w), lambda n: (n, 0))
    return pl.pallas_call(
        body,
        name="kv_grad_combine",
        grid=(nb,),
        in_specs=[kvrow, pl.BlockSpec((WINDOW, 2 * kv_w), lambda n: (jnp.minimum(n + 1, nb - 1), 0)), tab, tab],
        out_specs=kvrow,
        out_shape=jax.ShapeDtypeStruct((T, 2 * kv_w), BF16),
        compiler_params=_cparams(),
    )(own, prv, cos, sin)


ANY = pl.BlockSpec(memory_space=pl.ANY)


def _place():
    x, y, c = lax.axis_index("x"), lax.axis_index("y"), lax.axis_index("c")
    other_chips = [(1 - x, y), (x, 1 - y), (1 - x, 1 - y)]
    return x, y, c, 2 * x + y, other_chips


N_PEER_CHIPS = N_CHIPS - 1


def all_gather_weights(shards, lands):
    n = len(shards)

    def body(*refs):
        ins, outs = refs[:n], refs[2 * n:3 * n]
        ici_send, ici_recv, d2d_send, d2d_recv = refs[3 * n:]
        x, y, c, me, chips = _place()

        def half(ref, t, which):
            r = shards[t].shape[1] // 2
            return ref.at[:, pl.ds(pl.multiple_of(which * r, 16), r), :]

        def ici(t, jdx):
            px, py = chips[jdx]
            return pltpu.make_async_remote_copy(
                src_ref=half(ins[t], t, c), dst_ref=half(outs[t].at[me], t, c), send_sem=ici_send.at[t, jdx],
                recv_sem=ici_recv.at[t, jdx], device_id=(px, py, c), device_id_type=MESH)

        def landed(t, jdx):
            px, py = chips[jdx]
            blk = half(outs[t].at[2 * px + py], t, c)
            return pltpu.make_async_remote_copy(
                src_ref=blk, dst_ref=blk, send_sem=ici_send.at[t, jdx], recv_sem=ici_recv.at[t, jdx],
                device_id=(px, py, c), device_id_type=MESH)

        def d2d(t, jdx, which):
            px, py = chips[jdx]
            blk = half(outs[t].at[2 * px + py], t, which)
            return pltpu.make_async_remote_copy(
                src_ref=blk, dst_ref=blk, send_sem=d2d_send.at[t, jdx], recv_sem=d2d_recv.at[t, jdx],
                device_id=(x, y, 1 - c), device_id_type=MESH)

        for t in range(n):
            for jdx in range(N_PEER_CHIPS):
                ici(t, jdx).start()
        for t in range(n):
            for jdx in range(N_PEER_CHIPS):
                landed(t, jdx).wait_recv()
                d2d(t, jdx, c).start()
        for t in range(n):
            for jdx in range(N_PEER_CHIPS):
                d2d(t, jdx, 1 - c).wait_recv()
                d2d(t, jdx, c).wait_send()
                ici(t, jdx).wait_send()

    return pl.pallas_call(
        body,
        name="all_gather_weights",
        in_specs=[ANY] * (2 * n),
        out_specs=[ANY] * n,
        out_shape=[jax.ShapeDtypeStruct(l.shape, l.dtype) for l in lands],
        input_output_aliases={n + t: t for t in range(n)},
        scratch_shapes=[pltpu.SemaphoreType.DMA((n, N_PEER_CHIPS))] * 4,
    )(*shards, *lands)


def place_own_shard(name, shard, chip):
    nl, r, c = shard.shape

    def body(chip_ref, s_ref, o_ref):
        o_ref[...] = s_ref[...]

    return pl.pallas_call(
        body, name=name,
        grid_spec=pltpu.PrefetchScalarGridSpec(
            num_scalar_prefetch=1, grid=(nl,),
            in_specs=[pl.BlockSpec((None, r, c), lambda l, chip_ref: (l, 0, 0))],
            out_specs=pl.BlockSpec((None, None, r, c), lambda l, chip_ref: (chip_ref[0], l, 0, 0))),
        out_shape=jax.ShapeDtypeStruct((N_CHIPS,) + shard.shape, shard.dtype), compiler_params=_cparams(),
    )(chip, shard)


def exchange_halves(slabs):
    n = len(slabs)

    def body(*refs):
        ins, theirs = refs[:n], refs[n:2 * n]
        send_sems, recv_sems = refs[2 * n:]
        x, y, c, _, _ = _place()
        copies = []
        for t in range(n):
            cp = pltpu.make_async_remote_copy(
                src_ref=ins[t].at[1 - c], dst_ref=theirs[t], send_sem=send_sems.at[t],
                recv_sem=recv_sems.at[t], device_id=(x, y, 1 - c), device_id_type=MESH)
            cp.start()
            copies.append(cp)
        for cp in copies:
            cp.wait()

    return pl.pallas_call(
        body,
        name="exchange_halves",
        in_specs=[ANY] * n,
        out_specs=[ANY] * n,
        out_shape=[jax.ShapeDtypeStruct(s.shape[1:], s.dtype) for s in slabs],
        scratch_shapes=[pltpu.SemaphoreType.DMA((n,)), pltpu.SemaphoreType.DMA((n,))],
    )(*slabs)


def exchange_chip_partials(parts):
    n = len(parts)

    def body(*refs):
        ins, outs = refs[:n], refs[n:2 * n]
        send_sems, recv_sems = refs[2 * n:]
        _, _, c, me, chips = _place()
        copies = []
        for t in range(n):
            for jdx, (px, py) in enumerate(chips):
                cp = pltpu.make_async_remote_copy(
                    src_ref=ins[t].at[2 * px + py], dst_ref=outs[t].at[me], send_sem=send_sems.at[t, jdx],
                    recv_sem=recv_sems.at[t, jdx], device_id=(px, py, c), device_id_type=MESH)
                cp.start()
                copies.append(cp)
        for cp in copies:
            cp.wait()

    return pl.pallas_call(
        body,
        name="exchange_chip_partials",
        in_specs=[ANY] * n,
        out_specs=[ANY] * n,
        out_shape=[jax.ShapeDtypeStruct(p.shape, p.dtype) for p in parts],
        scratch_shapes=[pltpu.SemaphoreType.DMA((n, 3)), pltpu.SemaphoreType.DMA((n, 3))],
    )(*parts)


def share_reduced_halves(halves):
    n = len(halves)

    def body(*refs):
        ins, outs = refs[:n], refs[n:2 * n]
        send_sems, recv_sems = refs[2 * n:]
        x, y, c, _, _ = _place()
        copies = []
        for t in range(n):
            cp = pltpu.make_async_remote_copy(
                src_ref=ins[t], dst_ref=outs[t], send_sem=send_sems.at[t],
                recv_sem=recv_sems.at[t], device_id=(x, y, 1 - c), device_id_type=MESH)
            cp.start()
            copies.append(cp)
        for cp in copies:
            cp.wait()

    return pl.pallas_call(
        body,
        name="share_reduced_halves",
        in_specs=[ANY] * n,
        out_specs=[ANY] * n,
        out_shape=[jax.ShapeDtypeStruct(h.shape, h.dtype) for h in halves],
        scratch_shapes=[pltpu.SemaphoreType.DMA((n,)), pltpu.SemaphoreType.DMA((n,))],
    )(*halves)


def _row_tile(r, c):
    tr = r
    while tr * c * 4 > (3 << 19) and tr % 16 == 0:
        tr //= 2
    return tr


def add_sibling(name, slab, theirs, core):
    _, ns, slots, r, c = slab.shape
    tr = _row_tile(r, c)

    def body(core_ref, a_ref, b_ref, o_ref):
        o_ref[...] = (a_ref[...] + b_ref[...]).astype(BF16)

    blk = pl.BlockSpec((None, None, tr, c), lambda s, l, i, core_ref: (s, l, i, 0))
    return pl.pallas_call(
        body, name=name,
        grid_spec=pltpu.PrefetchScalarGridSpec(
            num_scalar_prefetch=1, grid=(ns, slots, r // tr),
            in_specs=[pl.BlockSpec((None, None, None, tr, c), lambda s, l, i, core_ref: (core_ref[0], s, l, i, 0)), blk],
            out_specs=blk),
        out_shape=jax.ShapeDtypeStruct(theirs.shape, BF16), compiler_params=_cparams(),
    )(core, slab, theirs)


def sum_chips(name, recv, own, chip):
    _, slots, r, c = recv.shape
    tr = _row_tile(r, c)

    def body(chip_ref, r0, r1, r2, r3, own_ref, o_ref):
        me = chip_ref[0]
        mine = own_ref[...]
        terms = [jnp.where(me == s, mine, rr[...]).astype(F32) for s, rr in enumerate((r0, r1, r2, r3))]
        o_ref[...] = ((terms[0] + terms[1]) + terms[2]) + terms[3]

    def src(s):
        return pl.BlockSpec((None, None, tr, c),
                            lambda l, i, chip_ref: (jnp.where(chip_ref[0] == s, (s + 1) % N_CHIPS, s), l, i, 0))

    return pl.pallas_call(
        body, name=name,
        grid_spec=pltpu.PrefetchScalarGridSpec(
            num_scalar_prefetch=1, grid=(slots, r // tr),
            in_specs=[src(0), src(1), src(2), src(3),
                      pl.BlockSpec((None, None, tr, c), lambda l, i, chip_ref: (chip_ref[0], l, i, 0))],
            out_specs=pl.BlockSpec((None, tr, c), lambda l, i, chip_ref: (l, i, 0))),
        out_shape=jax.ShapeDtypeStruct((slots, r, c), F32), compiler_params=_cparams(),
    )(chip, recv, recv, recv, recv, own)


def _adamw_math(w, g, m, v):
    m = ADAM_B1 * m + (1.0 - ADAM_B1) * g
    v = ADAM_B2 * v + (1.0 - ADAM_B2) * (g * g)
    m_hat = m / (1.0 - ADAM_B1 ** ADAM_STEP)
    v_hat = v / (1.0 - ADAM_B2 ** ADAM_STEP)
    delta = -ADAM_LR * (m_hat / (jnp.sqrt(v_hat) + ADAM_EPS) + ADAM_WD * w)
    return delta, m, v


def adamw_shard(name, w, m, v, g_own, g_sib, core, slot0, row_halves):
    n = w.shape[0]
    _, r, c = g_own.shape
    tr = _row_tile(r, c)
    nr = r // tr

    def body(core_ref, w_ref, m_ref, v_ref, own_ref, sib_ref, go_ref, d_ref, mo_ref, vo_ref):
        g = jnp.where(pl.program_id(1) == core_ref[0], own_ref[...], sib_ref[...])
        delta, mm, vv = _adamw_math(w_ref[...], g, m_ref[...], v_ref[...])
        go_ref[...] = g
        d_ref[...] = delta
        mo_ref[...] = mm
        vo_ref[...] = vv

    if row_halves:
        wspec = pl.BlockSpec((None, tr, c), lambda l, h, i, core_ref: (l, h * nr + i, 0))
    else:
        wspec = pl.BlockSpec((None, tr, c), lambda l, h, i, core_ref: (l, i, h))
    gspec = pl.BlockSpec((None, tr, c), lambda l, h, i, core_ref: (slot0 + l, i, 0))
    shp = jax.ShapeDtypeStruct(w.shape, F32)
    return pl.pallas_call(
        body, name=name,
        grid_spec=pltpu.PrefetchScalarGridSpec(
            num_scalar_prefetch=1, grid=(n, 2, nr),
            in_specs=[wspec, wspec, wspec, gspec, gspec], out_specs=[wspec] * 4),
        out_shape=[shp] * 4, compiler_params=_cparams(),
    )(core, w, m, v, g_own, g_sib)


SMALL_ROWS = 16


def small_allreduce_adamw(part, w, m, v):
    def body(p_ref, w_ref, m_ref, v_ref, g_ref, d_ref, mo_ref, vo_ref, buf, send_sems, recv_sems):
        x, y, c, _, _ = _place()
        me = 4 * x + 2 * y + c
        buf[me] = p_ref[...]
        copies = []
        for k in range(1, N_DEV):
            kx, ky, kc = (k >> 2) & 1, (k >> 1) & 1, k & 1
            peer = (x ^ kx, y ^ ky, c ^ kc)
            cp = pltpu.make_async_remote_copy(
                src_ref=p_ref, dst_ref=buf.at[me], send_sem=send_sems.at[k - 1],
                recv_sem=recv_sems.at[k - 1], device_id=peer, device_id_type=MESH)
            cp.start()
            copies.append(cp)
        for cp in copies:
            cp.wait()
        g = buf[0]
        for dev in range(1, N_DEV):
            g = g + buf[dev]
        delta, mm, vv = _adamw_math(w_ref[...], g, m_ref[...], v_ref[...])
        g_ref[...] = g
        d_ref[...] = delta
        mo_ref[...] = mm
        vo_ref[...] = vv

    vm = pl.BlockSpec(memory_space=pltpu.VMEM)
    shp = jax.ShapeDtypeStruct(part.shape, F32)
    return pl.pallas_call(
        body, name="small_allreduce_adamw",
        in_specs=[vm] * 4, out_specs=[vm] * 4, out_shape=[shp] * 4,
        scratch_shapes=[
            pltpu.VMEM((N_DEV,) + part.shape, F32),
            pltpu.SemaphoreType.DMA((N_DEV - 1,)), pltpu.SemaphoreType.DMA((N_DEV - 1,)),
        ],
    )(part, w, m, v)


def _rope_tables(T):
    half = HEAD_DIM // 2
    inv_freq = ROPE_THETA ** (-jnp.arange(half, dtype=F32) / half)
    ang = jnp.arange(T).astype(F32)[:, None] * inv_freq[None, :]
    cos = jnp.tile(jnp.cos(ang), (1, LANES // half))
    sin = jnp.tile(jnp.sin(ang), (1, LANES // half))
    lane = jnp.arange(LANES)
    sign = jnp.where((lane % HEAD_DIM) < half, -1.0, 1.0).astype(F32)
    return cos, sin * sign[None, :]


def _pack_small(ffn1, mix, ffn2, kvn, fin, sinks, loss_row):
    sink_row = jnp.pad(sinks.reshape(1, SWA_Q_HEADS), ((0, 0), (0, D_MODEL - SWA_Q_HEADS)))
    rows = jnp.concatenate([ffn1, mix, ffn2, kvn.reshape(1, -1), fin.reshape(1, -1), sink_row, loss_row], axis=0)
    return jnp.concatenate([rows, jnp.zeros((SMALL_ROWS - rows.shape[0], D_MODEL), F32)], axis=0)


def kernel(x, ffn1_norm, ffn1_w_in, ffn1_w_out, mix_norm, ffn2_norm, ffn2_w_in, ffn2_w_out, sb_w_qkv, sb_w_o, kv_norm, kv_w, swa_w_q, swa_sinks, swa_w_o, final_norm, loss_target, m_ffn1_norm, m_ffn1_w_in, m_ffn1_w_out, m_mix_norm, m_ffn2_norm, m_ffn2_w_in, m_ffn2_w_out, m_sb_w_qkv, m_sb_w_o, m_kv_norm, m_kv_w, m_swa_w_q, m_swa_sinks, m_swa_w_o, m_final_norm, v_ffn1_norm, v_ffn1_w_in, v_ffn1_w_out, v_mix_norm, v_ffn2_norm, v_ffn2_w_in, v_ffn2_w_out, v_sb_w_qkv, v_sb_w_o, v_kv_norm, v_kv_w, v_swa_w_q, v_swa_sinks, v_swa_w_o, v_final_norm):
    T = x.shape[1]
    kv_cols = SWA_KV_HEADS * HEAD_DIM
    x2 = x.reshape(T, D_MODEL)
    tgt = loss_target.reshape(T, D_MODEL)
    cos, sin = _rope_tables(T)

    w_in_l = jnp.concatenate([ffn1_w_in, ffn2_w_in], axis=0).astype(BF16)
    w_out_l = jnp.concatenate([ffn1_w_out, ffn2_w_out], axis=0).astype(BF16)
    sq_l = jnp.concatenate([sb_w_o, swa_w_q, swa_w_o], axis=0).astype(BF16)
    qkv_l = sb_w_qkv[0].astype(BF16)
    kvw_l = kv_w.astype(BF16)
    core = lax.axis_index("c").astype(jnp.int32).reshape(1)
    chip = (2 * lax.axis_index("x") + lax.axis_index("y")).astype(jnp.int32).reshape(1)
    shards = [w_in_l, w_out_l, sq_l, qkv_l[None], kvw_l[None]]
    lands = [place_own_shard(f"own_shard_{t}", s, chip) for t, s in enumerate(shards)]
    w_in, w_out, w_sq, w_qkv, w_kv = all_gather_weights(shards, lands)
    w_qkv = w_qkv.reshape(N_CHIPS, D_MODEL, QKV_COLS)
    w_kv = w_kv.reshape(D_MODEL, 2 * kv_cols)

    def ffn_w(slot):
        return w_in, w_out, slot

    def vec(a, i):
        return a[i].reshape(1, D_MODEL)

    ident = lambda w: w
    sq_prep = lambda w: w.reshape(D_MODEL, w.shape[-1])
    qscale = jnp.concatenate([jnp.full((1, D_MODEL), ATTN_SCALE, F32), jnp.ones((1, 2 * D_MODEL), F32)], axis=1)
    swa_scale = jnp.full((1, D_MODEL), ATTN_SCALE, F32)
    sinks = swa_sinks.reshape(SWA_Q_HEADS)

    h1, gate1, up1 = ffn_fwd("l0a", x2, vec(ffn1_norm, 0), *ffn_w(SLOT_FFN1[0]))
    qkv = qkv_fwd(h1, vec(mix_norm, 0), w_qkv, qscale)
    o_sb, tot, sb_first = sb_fwd(qkv)
    h2 = linear_res("sb_out", o_sb, w_sq, SQ_SB_O, h1)
    h3, gate2, up2 = ffn_fwd("l0b", h2, vec(ffn2_norm, 0), *ffn_w(SLOT_FFN2[0]))
    kvn = kv_norm.reshape(1, D_MODEL)
    k_sw = rms_linear("kv_k", h3, kvn, w_kv, pl.BlockSpec((D_MODEL, kv_cols), lambda i, j: (0, 0)), ident,
                      kv_cols, kv_cols, rope=(cos, sin))
    v_sw = rms_linear("kv_v", h3, kvn, w_kv, pl.BlockSpec((D_MODEL, kv_cols), lambda i, j: (0, 1)), ident,
                      kv_cols, kv_cols)
    h4, gate3, up3 = ffn_fwd("l1a", h3, vec(ffn1_norm, 1), *ffn_w(SLOT_FFN1[1]))
    q_sw = rms_linear("swa_q", h4, vec(mix_norm, 1), w_sq,
                      pl.BlockSpec((N_CHIPS, None, SQ_ROWS, 512), lambda i, j: (0, SQ_SWA_Q, 0, j)), sq_prep,
                      D_MODEL, 512, rope=(cos, sin), scale=swa_scale)
    o_sw, lse = swa_fwd(q_sw, k_sw, v_sw, sinks)
    h5 = linear_res("swa_out", o_sw, w_sq, SQ_SWA_O, h4)
    h6, gate4, up4 = ffn_fwd("l1b", h5, vec(ffn2_norm, 1), *ffn_w(SLOT_FFN2[1]))
    dh6, loss_p, d_final = loss_bwd(h6, final_norm.reshape(1, D_MODEL), tgt)

    slab = {"in": None, "out": None, "sq": None}
    in_shape = (2, N_CHIPS, 4, D_MODEL // 2, FF_CHUNK)
    out_shape = (2, N_CHIPS, 4, FF_ROWS, D_MODEL // 2)
    sq_shape = (2, N_CHIPS, 3, SQ_ROWS, D_MODEL // 2)

    def ffn_grads(tag, dh, h_in, g, gate, up, slot):
        dh_in, xn, dg_, du_, act, dhb, dnorm = ffn_bwd(tag, dh, h_in, g, gate, up, *ffn_w(slot))
        blk = (None, 1, None, D_MODEL // 2, FF_CHUNK)
        slab["in"] = mm_tn(f"dw_gate_{tag}", xn, dg_, D_MODEL // 2, FF_CHUNK, blk,
                           lambda k, n: (k, n, slot, 0, 0), in_shape, prev=slab["in"])
        slab["in"] = mm_tn(f"dw_up_{tag}", xn, du_, D_MODEL // 2, FF_CHUNK, blk,
                           lambda k, n: (k, 2 + n, slot, 0, 0), in_shape, prev=slab["in"])
        slab["out"] = mm_tn(f"dw_out_{tag}", act, dhb, FF_CHUNK, D_MODEL // 2,
                            (None, 2, None, FF_ROWS, D_MODEL // 2),
                            lambda k, n: (n, k, slot, 0, 0), out_shape, prev=slab["out"])
        return dh_in, dnorm

    def sq_grad(tag, a, dyb, t):
        slab["sq"] = mm_tn(f"dw_sq_{tag}", a, dyb, D_MODEL, D_MODEL // 2,
                           (None, N_CHIPS, None, SQ_ROWS, D_MODEL // 2),
                           lambda k, n: (n, 0, t, 0, 0), sq_shape, prev=slab["sq"])

    dh5, d_ffn2_1 = ffn_grads("l1b", dh6, h5, vec(ffn2_norm, 1), gate4, up4, SLOT_FFN2[1])
    do_sw, dh5b = linear_bwd_plain("swa_out_bwd", dh5, w_sq, SQ_SWA_O)
    sq_grad("swa_o", o_sw, dh5b, SQ_SWA_O)
    dq_sw, kv_own, kv_prev, d_sinks = swa_bwd(q_sw, k_sw, v_sw, sinks, do_sw, lse, cos, sin)
    sq_w_spec = pl.BlockSpec((N_CHIPS, None, SQ_ROWS, D_MODEL), lambda i, j: (0, SQ_SWA_Q, 0, 0))
    dh4, hn4, d_mix_1 = linear_bwd_rms("swa_q_bwd", [(dq_sw, w_sq, sq_w_spec, sq_prep)], h4, vec(mix_norm, 1), dh5,
                                       1, D_MODEL)
    sq_grad("swa_q", hn4, dq_sw, SQ_SWA_Q)
    dh3a, d_ffn1_1 = ffn_grads("l1a", dh4, h3, vec(ffn1_norm, 1), gate3, up3, SLOT_FFN1[1])
    dkv = kv_grad_combine(kv_own, kv_prev, cos, sin)
    kv_w_spec = pl.BlockSpec((D_MODEL, 2 * kv_cols), lambda i, j: (0, 0))
    dh3, xn3, d_kvn = linear_bwd_rms("kv_bwd", [(dkv, w_kv, kv_w_spec, ident)], h3, kvn, dh3a, 1, 2 * kv_cols)
    slab_kv = mm_tn("dw_kv", xn3, dkv, D_MODEL, kv_cols, (None, N_CHIPS, None, SQ_ROWS, kv_cols),
                    lambda k, n: (n, 0, 0, 0, 0), (2, N_CHIPS, 1, SQ_ROWS, kv_cols))
    dh2, d_ffn2_0 = ffn_grads("l0b", dh3, h2, vec(ffn2_norm, 0), gate2, up2, SLOT_FFN2[0])
    do_sb, dh2b = linear_bwd_plain("sb_out_bwd", dh2, w_sq, SQ_SB_O)
    sq_grad("sb_o", o_sb, dh2b, SQ_SB_O)
    dq_sb, dk_sb, dv_sb = sb_bwd(qkv, do_sb, tot, sb_first)
    dqkv = jnp.concatenate([dq_sb, dk_sb, dv_sb], axis=1)
    dh1, hn1, d_mix_0 = qkv_bwd(dqkv, w_qkv, h1, vec(mix_norm, 0), dh2)
    slab_qkv = mm_tn("dw_qkv", hn1, dqkv, D_MODEL // 2, QKV_COLS, (None, 1, None, D_MODEL // 2, QKV_COLS),
                     lambda k, n: (k, n, 0, 0, 0), (2, N_CHIPS, 1, D_MODEL // 2, QKV_COLS))
    dx, d_ffn1_0 = ffn_grads("l0a", dh1, x2, vec(ffn1_norm, 0), gate1, up1, SLOT_FFN1[0])

    slabs = [slab["in"], slab["out"], slab["sq"], slab_qkv, slab_kv]
    names = ["in", "out", "sq", "qkv", "kv"]
    theirs = exchange_halves(slabs)
    parts = [add_sibling(f"add_sibling_{nm}", s, t, core) for nm, s, t in zip(names, slabs, theirs)]
    gathered = exchange_chip_partials(parts)
    halves = [sum_chips(f"sum_chips_{nm}", g, p, chip) for nm, g, p in zip(names, gathered, parts)]
    sib_halves = share_reduced_halves(halves)
    g_in, g_out, g_sq, g_qkv, g_kv = zip(halves, sib_halves)

    def upd(name, w, m, v, g_pair, slot0, row_halves):
        shp = w.shape
        w3 = w.reshape((-1,) + shp[-2:])
        outs = adamw_shard(name, w3, m.reshape(w3.shape), v.reshape(w3.shape), g_pair[0], g_pair[1], core,
                           slot0, row_halves)
        return [o.reshape(shp) for o in outs]

    r_ffn1_in = upd("adamw_ffn1_in", ffn1_w_in, m_ffn1_w_in, v_ffn1_w_in, g_in, 0, True)
    r_ffn2_in = upd("adamw_ffn2_in", ffn2_w_in, m_ffn2_w_in, v_ffn2_w_in, g_in, 2, True)
    r_ffn1_out = upd("adamw_ffn1_out", ffn1_w_out, m_ffn1_w_out, v_ffn1_w_out, g_out, 0, False)
    r_ffn2_out = upd("adamw_ffn2_out", ffn2_w_out, m_ffn2_w_out, v_ffn2_w_out, g_out, 2, False)
    r_qkv = upd("adamw_qkv", sb_w_qkv, m_sb_w_qkv, v_sb_w_qkv, g_qkv, 0, True)
    r_sb_o = upd("adamw_sb_o", sb_w_o, m_sb_w_o, v_sb_w_o, g_sq, SQ_SB_O, False)
    r_swa_q = upd("adamw_swa_q", swa_w_q, m_swa_w_q, v_swa_w_q, g_sq, SQ_SWA_Q, False)
    r_swa_o = upd("adamw_swa_o", swa_w_o, m_swa_w_o, v_swa_w_o, g_sq, SQ_SWA_O, False)
    r_kv = upd("adamw_kv", kv_w, m_kv_w, v_kv_w, g_kv, 0, False)

    loss_row = jnp.pad(loss_p, ((0, 0), (0, D_MODEL - LANES)))
    d_sink_row = d_sinks[0, :SWA_Q_HEADS]
    part = _pack_small(jnp.concatenate([d_ffn1_0, d_ffn1_1], axis=0), jnp.concatenate([d_mix_0, d_mix_1], axis=0),
                       jnp.concatenate([d_ffn2_0, d_ffn2_1], axis=0), d_kvn, d_final, d_sink_row, loss_row)
    zrow = jnp.zeros((1, D_MODEL), F32)
    small = small_allreduce_adamw(
        part,
        _pack_small(ffn1_norm, mix_norm, ffn2_norm, kv_norm, final_norm, swa_sinks, zrow),
        _pack_small(m_ffn1_norm, m_mix_norm, m_ffn2_norm, m_kv_norm, m_final_norm, m_swa_sinks, zrow),
        _pack_small(v_ffn1_norm, v_mix_norm, v_ffn2_norm, v_kv_norm, v_final_norm, v_swa_sinks, zrow))

    def unpack(p):
        return dict(ffn1_norm=p[0:2], mix_norm=p[2:4], ffn2_norm=p[4:6], kv_norm=p[6], final_norm=p[7],
                    swa_sinks=p[8:9, :SWA_Q_HEADS])

    big = dict(ffn1_w_in=r_ffn1_in, ffn1_w_out=r_ffn1_out, ffn2_w_in=r_ffn2_in, ffn2_w_out=r_ffn2_out,
               sb_w_qkv=r_qkv, sb_w_o=r_sb_o, kv_w=r_kv, swa_w_q=r_swa_q, swa_w_o=r_swa_o)
    order = ["ffn1_norm", "ffn1_w_in", "ffn1_w_out", "mix_norm", "ffn2_norm", "ffn2_w_in", "ffn2_w_out",
             "sb_w_qkv", "sb_w_o", "kv_norm", "kv_w", "swa_w_q", "swa_sinks", "swa_w_o", "final_norm"]
    outs = []
    for kind in range(4):
        sm = unpack(small[kind])
        for nm in order:
            outs.append(big[nm][kind] if nm in big else sm[nm])
    loss = small[0][9, 0]
    return (loss, dx.reshape(x.shape), *outs)
```

```python
import functools

import jax
import jax.numpy as jnp
from jax import lax
from jax.experimental import pallas as pl
from jax.experimental.pallas import tpu as pltpu

F32 = jnp.float32
BF16 = jnp.bfloat16
MESH = pl.DeviceIdType.MESH

D_MODEL = 1024
D_FF = 2816
HEAD_DIM = 64
SB_HEADS = 16
SWA_Q_HEADS = 16
SWA_KV_HEADS = 4
WINDOW = 128
ROPE_THETA = 10000.0
RMS_EPS = 1e-6
FFN_RES_SCALE = 0.5
ATTN_SCALE = HEAD_DIM ** -0.5

ADAM_LR = 0.001
ADAM_B1 = 0.9
ADAM_B2 = 0.999
ADAM_EPS = 1e-08
ADAM_WD = 0.01
ADAM_STEP = 10

N_CHIPS = 4
N_DEV = 8
LANES = 128
FF_CHUNK = D_FF // 2
FF_ROWS = D_FF // N_CHIPS
SQ_ROWS = D_MODEL // N_CHIPS
QKV_COLS = 3 * D_MODEL // N_CHIPS
VMEM_LIMIT = 56 * 1024 * 1024
NEG_BIG = -1e30

SLOT_FFN1 = (0, 1)
SLOT_FFN2 = (2, 3)
SQ_SB_O, SQ_SWA_Q, SQ_SWA_O = 0, 1, 2


def _cparams():
    return pltpu.CompilerParams(vmem_limit_bytes=VMEM_LIMIT)


def _dot(a, b):
    return jnp.dot(a, b, preferred_element_type=F32)


def _dot_nt(a, b):
    return lax.dot_general(a, b, (((1,), (1,)), ((), ())), preferred_element_type=F32)


def _dot_tn(a, b):
    return lax.dot_general(a, b, (((0,), (0,)), ((), ())), preferred_element_type=F32)


def _rstd(h):
    return lax.rsqrt(jnp.mean(h * h, axis=-1, keepdims=True) + RMS_EPS)


def _swap32(x):
    n = x.shape[-1]
    lane = lax.broadcasted_iota(jnp.int32, x.shape, x.ndim - 1)
    first = (lane % HEAD_DIM) < (HEAD_DIM // 2)
    return jnp.where(first, pltpu.roll(x, n - HEAD_DIM // 2, x.ndim - 1), pltpu.roll(x, HEAD_DIM // 2, x.ndim - 1))


def _tile_lanes(t, n):
    return t if n == LANES else jnp.tile(t, (1, n // LANES))


FFN_ROWS = 256


def _ffn_w_in_spec(slot):
    return pl.BlockSpec((N_CHIPS, None, D_MODEL, FF_CHUNK), lambda i: (0, slot, 0, 0), pipeline_mode=pl.Buffered(1))


def _ffn_w_out_spec(slot):
    return pl.BlockSpec((N_CHIPS, None, FF_ROWS, D_MODEL), lambda i: (0, slot, 0, 0), pipeline_mode=pl.Buffered(1))


def ffn_fwd(tag, h, g, w_in, w_out, slot):
    T = h.shape[0]
    tm = FFN_ROWS
    nch = D_FF // FF_CHUNK

    def body(h_ref, g_ref, wi_ref, wo_ref, out_ref, gate_ref, up_ref):
        hh = h_ref[...]
        xn = (hh * _rstd(hh) * g_ref[...]).astype(BF16)
        acc = None
        for j in range(nch):
            cols = slice(j * FF_CHUNK, (j + 1) * FF_CHUNK)
            gate = _dot(xn, wi_ref[j])
            up = _dot(xn, wi_ref[nch + j])
            gate_ref[:, cols] = gate.astype(BF16)
            up_ref[:, cols] = up.astype(BF16)
            a = (gate * jax.nn.sigmoid(gate) * up).astype(BF16)
            part = _dot(a, wo_ref[2 * j:2 * j + 2].reshape(FF_CHUNK, D_MODEL))
            acc = part if acc is None else acc + part
        out_ref[...] = hh + FFN_RES_SCALE * acc

    row = pl.BlockSpec((tm, D_MODEL), lambda i: (i, 0))
    ff = pl.BlockSpec((tm, D_FF), lambda i: (i, 0))
    return pl.pallas_call(
        body,
        name=f"ffn_fwd_{tag}",
        grid=(T // tm,),
        in_specs=[row, pl.BlockSpec((1, D_MODEL), lambda i: (0, 0)), _ffn_w_in_spec(slot), _ffn_w_out_spec(slot)],
        out_specs=[row, ff, ff],
        out_shape=[
            jax.ShapeDtypeStruct((T, D_MODEL), F32),
            jax.ShapeDtypeStruct((T, D_FF), BF16),
            jax.ShapeDtypeStruct((T, D_FF), BF16),
        ],
        compiler_params=_cparams(),
    )(h, g, w_in, w_out)


def ffn_bwd(tag, dh, h, g, gate, up, w_in, w_out, slot):
    T = dh.shape[0]
    tm = FFN_ROWS
    nch = D_FF // FF_CHUNK

    def body(dh_ref, h_ref, g_ref, gate_ref, up_ref, wi_ref, wo_ref,
             dhin_ref, xn_ref, dg_ref, du_ref, a_ref, dhb_ref, dnorm_ref):
        @pl.when(pl.program_id(0) == 0)
        def _():
            dnorm_ref[...] = jnp.zeros_like(dnorm_ref)

        dhh = dh_ref[...]
        dhb = (FFN_RES_SCALE * dhh).astype(BF16)
        dhb_ref[...] = dhb
        dxn = None
        for j in range(nch):
            cols = slice(j * FF_CHUNK, (j + 1) * FF_CHUNK)
            da = _dot_nt(dhb, wo_ref[2 * j:2 * j + 2].reshape(FF_CHUNK, D_MODEL))
            gt = gate_ref[:, cols].astype(F32)
            u = up_ref[:, cols].astype(F32)
            s = jax.nn.sigmoid(gt)
            silu = gt * s
            a_ref[:, cols] = (silu * u).astype(BF16)
            dgate = (da * u * (s * (1.0 + gt * (1.0 - s)))).astype(BF16)
            dup = (da * silu).astype(BF16)
            dg_ref[:, cols] = dgate
            du_ref[:, cols] = dup
            part = _dot_nt(dgate, wi_ref[j]) + _dot_nt(dup, wi_ref[nch + j])
            dxn = part if dxn is None else dxn + part
        hh = h_ref[...]
        gg = g_ref[...]
        r = _rstd(hh)
        hr = hh * r
        xn_ref[...] = (hr * gg).astype(BF16)
        dnorm_ref[...] += jnp.sum(dxn * hr, axis=0, keepdims=True)
        gd = gg * dxn
        dhin_ref[...] = dhh + r * (gd - hr * jnp.mean(gd * hr, axis=-1, keepdims=True))

    row = pl.BlockSpec((tm, D_MODEL), lambda i: (i, 0))
    ff = pl.BlockSpec((tm, D_FF), lambda i: (i, 0))
    vec = pl.BlockSpec((1, D_MODEL), lambda i: (0, 0))
    return pl.pallas_call(
        body,
        name=f"ffn_bwd_{tag}",
        grid=(T // tm,),
        in_specs=[row, row, vec, ff, ff, _ffn_w_in_spec(slot), _ffn_w_out_spec(slot)],
        out_specs=[row, row, ff, ff, ff, row, vec],
        out_shape=[
            jax.ShapeDtypeStruct((T, D_MODEL), F32),
            jax.ShapeDtypeStruct((T, D_MODEL), BF16),
            jax.ShapeDtypeStruct((T, D_FF), BF16),
            jax.ShapeDtypeStruct((T, D_FF), BF16),
            jax.ShapeDtypeStruct((T, D_FF), BF16),
            jax.ShapeDtypeStruct((T, D_MODEL), BF16),
            jax.ShapeDtypeStruct((1, D_MODEL), F32),
        ],
        compiler_params=_cparams(),
    )(dh, h, g, gate, up, w_in, w_out)


def rms_linear(name, h, g, w, w_spec, w_prep, n_out, tn, *, rope=None, scale=None):
    T = h.shape[0]
    tm = 512
    extra, extra_specs = [], []
    if rope is not None:
        extra += list(rope)
        extra_specs += [pl.BlockSpec((tm, LANES), lambda i, j: (i, 0))] * 2
    if scale is not None:
        extra.append(scale)
        extra_specs.append(pl.BlockSpec((1, tn), lambda i, j: (0, j)))

    def body(h_ref, g_ref, w_ref, *rest):
        rest = list(rest)
        cos_ref = sin_ref = sc_ref = None
        if rope is not None:
            cos_ref, sin_ref = rest[0], rest[1]
            rest = rest[2:]
        if scale is not None:
            sc_ref = rest[0]
            rest = rest[1:]
        out_ref, xn_s = rest

        @pl.when(pl.program_id(1) == 0)
        def _():
            hh = h_ref[...]
            xn_s[...] = (hh * _rstd(hh) * g_ref[...]).astype(BF16)

        y = _dot(xn_s[...], w_prep(w_ref[...]))
        if rope is not None:
            y = y * _tile_lanes(cos_ref[...], tn) + _swap32(y) * _tile_lanes(sin_ref[...], tn)
        if scale is not None:
            y = y * sc_ref[...]
        out_ref[...] = y.astype(BF16)

    return pl.pallas_call(
        body,
        name=name,
        grid=(T // tm, n_out // tn),
        in_specs=[
            pl.BlockSpec((tm, D_MODEL), lambda i, j: (i, 0)),
            pl.BlockSpec((1, D_MODEL), lambda i, j: (0, 0)),
            w_spec,
        ] + extra_specs,
        out_specs=pl.BlockSpec((tm, tn), lambda i, j: (i, j)),
        out_shape=jax.ShapeDtypeStruct((T, n_out), BF16),
        scratch_shapes=[pltpu.VMEM((tm, D_MODEL), BF16)],
        compiler_params=_cparams(),
    )(h, g, w, *extra)


QKV_ROWS = 512


def _qkv_w_spec():
    return pl.BlockSpec((N_CHIPS, D_MODEL, QKV_COLS), lambda i: (0, 0, 0), pipeline_mode=pl.Buffered(1))


def qkv_fwd(h, g, w_qkv, scale):
    T = h.shape[0]
    tm = QKV_ROWS

    def body(h_ref, g_ref, w_ref, sc_ref, out_ref):
        hh = h_ref[...]
        xn = (hh * _rstd(hh) * g_ref[...]).astype(BF16)
        for s in range(N_CHIPS):
            cols = slice(s * QKV_COLS, (s + 1) * QKV_COLS)
            out_ref[:, cols] = (_dot(xn, w_ref[s]) * sc_ref[:, cols]).astype(BF16)

    return pl.pallas_call(
        body,
        name="sb_qkv",
        grid=(T // tm,),
        in_specs=[
            pl.BlockSpec((tm, D_MODEL), lambda i: (i, 0)),
            pl.BlockSpec((1, D_MODEL), lambda i: (0, 0)),
            _qkv_w_spec(),
            pl.BlockSpec((1, 3 * D_MODEL), lambda i: (0, 0)),
        ],
        out_specs=pl.BlockSpec((tm, 3 * D_MODEL), lambda i: (i, 0)),
        out_shape=jax.ShapeDtypeStruct((T, 3 * D_MODEL), BF16),
        compiler_params=_cparams(),
    )(h, g, w_qkv, scale)


def qkv_bwd(dy, w_qkv, h, g, dres):
    T = h.shape[0]
    tm = QKV_ROWS

    def body(dy_ref, w_ref, h_ref, g_ref, dres_ref, dh_ref, xn_ref, dg_ref):
        @pl.when(pl.program_id(0) == 0)
        def _():
            dg_ref[...] = jnp.zeros_like(dg_ref)

        dxn = None
        for s in range(N_CHIPS):
            part = _dot_nt(dy_ref[:, s * QKV_COLS:(s + 1) * QKV_COLS], w_ref[s])
            dxn = part if dxn is None else dxn + part
        hh = h_ref[...]
        gg = g_ref[...]
        r = _rstd(hh)
        hr = hh * r
        xn_ref[...] = (hr * gg).astype(BF16)
        dg_ref[...] += jnp.sum(dxn * hr, axis=0, keepdims=True)
        gd = gg * dxn
        dh_ref[...] = dres_ref[...] + r * (gd - hr * jnp.mean(gd * hr, axis=-1, keepdims=True))

    row = pl.BlockSpec((tm, D_MODEL), lambda i: (i, 0))
    vec = pl.BlockSpec((1, D_MODEL), lambda i: (0, 0))
    return pl.pallas_call(
        body,
        name="sb_qkv_bwd",
        grid=(T // tm,),
        in_specs=[pl.BlockSpec((tm, 3 * D_MODEL), lambda i: (i, 0)), _qkv_w_spec(), row, vec, row],
        out_specs=[row, row, vec],
        out_shape=[
            jax.ShapeDtypeStruct((T, D_MODEL), F32),
            jax.ShapeDtypeStruct((T, D_MODEL), BF16),
            jax.ShapeDtypeStruct((1, D_MODEL), F32),
        ],
        compiler_params=_cparams(),
    )(dy, w_qkv, h, g, dres)


def linear_res(name, a, w_sq, t, res):
    T = a.shape[0]
    tm = 512

    def body(a_ref, w_ref, res_ref, out_ref):
        out_ref[...] = res_ref[...] + _dot(a_ref[...], w_ref[...].reshape(D_MODEL, D_MODEL))

    row = pl.BlockSpec((tm, D_MODEL), lambda i: (i, 0))
    return pl.pallas_call(
        body,
        name=name,
        grid=(T // tm,),
        in_specs=[row, pl.BlockSpec((N_CHIPS, None, SQ_ROWS, D_MODEL), lambda i: (0, t, 0, 0)), row],
        out_specs=row,
        out_shape=jax.ShapeDtypeStruct((T, D_MODEL), F32),
        compiler_params=_cparams(),
    )(a, w_sq, res)


def linear_bwd_plain(name, dy, w_sq, t):
    T = dy.shape[0]
    tm = 512

    def body(dy_ref, w_ref, da_ref, dyb_ref):
        dyb = dy_ref[...].astype(BF16)
        dyb_ref[...] = dyb
        da_ref[...] = _dot_nt(dyb, w_ref[...].reshape(D_MODEL, D_MODEL)).astype(BF16)

    row = pl.BlockSpec((tm, D_MODEL), lambda i: (i, 0))
    return pl.pallas_call(
        body,
        name=name,
        grid=(T // tm,),
        in_specs=[row, pl.BlockSpec((N_CHIPS, None, SQ_ROWS, D_MODEL), lambda i: (0, t, 0, 0))],
        out_specs=[row, row],
        out_shape=[jax.ShapeDtypeStruct((T, D_MODEL), BF16), jax.ShapeDtypeStruct((T, D_MODEL), BF16)],
        compiler_params=_cparams(),
    )(dy, w_sq)


def linear_bwd_rms(name, pairs, h, g, dres, nch, tn, tm=256):
    T = h.shape[0]
    npair = len(pairs)

    def body(*refs):
        dy_refs = refs[:npair]
        w_refs = refs[npair:2 * npair]
        h_ref, g_ref, dres_ref, dh_ref, xn_ref, dg_ref, acc_s = refs[2 * npair:]
        i = pl.program_id(0)
        j = pl.program_id(1)

        @pl.when(j == 0)
        def _():
            acc_s[...] = jnp.zeros_like(acc_s)

        @pl.when((i == 0) & (j == 0))
        def _():
            dg_ref[...] = jnp.zeros_like(dg_ref)

        part = None
        for p in range(npair):
            d = _dot_nt(dy_refs[p][...], pairs[p][3](w_refs[p][...]))
            part = d if part is None else part + d
        acc_s[...] += part

        @pl.when(j == nch - 1)
        def _():
            dxn = acc_s[...]
            hh = h_ref[...]
            gg = g_ref[...]
            r = _rstd(hh)
            hr = hh * r
            xn_ref[...] = (hr * gg).astype(BF16)
            dg_ref[...] += jnp.sum(dxn * hr, axis=0, keepdims=True)
            gd = gg * dxn
            dh_ref[...] = dres_ref[...] + r * (gd - hr * jnp.mean(gd * hr, axis=-1, keepdims=True))

    row = pl.BlockSpec((tm, D_MODEL), lambda i, j: (i, 0))
    vec = pl.BlockSpec((1, D_MODEL), lambda i, j: (0, 0))
    return pl.pallas_call(
        body,
        name=name,
        grid=(T // tm, nch),
        in_specs=[pl.BlockSpec((tm, tn), lambda i, j: (i, j))] * npair + [p[2] for p in pairs] + [row, vec, row],
        out_specs=[row, row, vec],
        out_shape=[
            jax.ShapeDtypeStruct((T, D_MODEL), F32),
            jax.ShapeDtypeStruct((T, D_MODEL), BF16),
            jax.ShapeDtypeStruct((1, D_MODEL), F32),
        ],
        scratch_shapes=[pltpu.VMEM((tm, D_MODEL), F32)],
        compiler_params=_cparams(),
    )(*[p[0] for p in pairs], *[p[1] for p in pairs], h, g, dres)


def loss_bwd(h, g, tgt):
    T = h.shape[0]
    tm = 512

    def body(h_ref, g_ref, t_ref, dh_ref, loss_ref, dg_ref):
        @pl.when(pl.program_id(0) == 0)
        def _():
            loss_ref[...] = jnp.zeros_like(loss_ref)
            dg_ref[...] = jnp.zeros_like(dg_ref)

        hh = h_ref[...]
        gg = g_ref[...]
        r = _rstd(hh)
        hr = hh * r
        err = hr * gg - t_ref[...]
        loss_ref[...] += 0.5 * jnp.sum(jnp.mean(err * err, axis=-1, keepdims=True), axis=0, keepdims=True)
        dy = err * (1.0 / D_MODEL)
        dg_ref[...] += jnp.sum(dy * hr, axis=0, keepdims=True)
        gd = gg * dy
        dh_ref[...] = r * (gd - hr * jnp.mean(gd * hr, axis=-1, keepdims=True))

    row = pl.BlockSpec((tm, D_MODEL), lambda i: (i, 0))
    vec = pl.BlockSpec((1, D_MODEL), lambda i: (0, 0))
    return pl.pallas_call(
        body,
        name="loss_bwd",
        grid=(T // tm,),
        in_specs=[row, vec, row],
        out_specs=[row, pl.BlockSpec((1, LANES), lambda i: (0, 0)), vec],
        out_shape=[
            jax.ShapeDtypeStruct((T, D_MODEL), F32),
            jax.ShapeDtypeStruct((1, LANES), F32),
            jax.ShapeDtypeStruct((1, D_MODEL), F32),
        ],
        compiler_params=_cparams(),
    )(h, g, tgt)


DW_TOKENS = 4096


def mm_tn(name, a, b, tk, tn, out_block, out_index, out_shape, prev=None, tt=DW_TOKENS):
    T = a.shape[0]
    ns, r = out_block[1], out_block[3]
    tt = min(tt, T)
    nt = T // tt

    def body(*refs):
        if prev is None:
            a_ref, b_ref, out_ref = refs
        else:
            a_ref, b_ref, _, out_ref = refs
        t = pl.program_id(2)
        res = _dot_tn(a_ref[...], b_ref[...])

        @pl.when(t == 0)
        def _():
            for u in range(ns):
                out_ref[u] = res[u * r:(u + 1) * r]

        @pl.when(t > 0)
        def _():
            for u in range(ns):
                out_ref[u] += res[u * r:(u + 1) * r]

    in_specs = [
        pl.BlockSpec((tt, tk), lambda k, n, t: (t, k)),
        pl.BlockSpec((tt, tn), lambda k, n, t: (t, n)),
    ]
    args = [a, b]
    aliases = {}
    if prev is not None:
        in_specs.append(pl.BlockSpec(memory_space=pl.ANY))
        args.append(prev)
        aliases = {2: 0}
    return pl.pallas_call(
        body,
        name=name,
        grid=(a.shape[1] // tk, b.shape[1] // tn, nt),
        in_specs=in_specs,
        out_specs=pl.BlockSpec(out_block, lambda k, n, t: out_index(k, n)),
        out_shape=jax.ShapeDtypeStruct(out_shape, F32),
        input_output_aliases=aliases,
        compiler_params=_cparams(),
    )(*args)


SB_BLOCK = 256
SB_QROWS = 256
SB_QROWS_BWD = 256
SB_UNDERFLOW_BITS = 140.0
SB_CHUNK = 128


LOG2E = 1.4426950408889634


def _softplus2(z2):
    sign = jnp.uint32(0x80000000)
    neg_abs = lax.bitcast_convert_type(lax.bitcast_convert_type(z2, jnp.uint32) | sign, F32)
    return jnp.log2(1.0 + jnp.exp2(neg_abs)) + jnp.maximum(z2, 0.0)


def _twice(x):
    return jnp.concatenate([x, x], axis=1)


def sb_fwd(qkv):
    T = qkv.shape[0]
    tq, tk = SB_QROWS, SB_BLOCK
    ratio = tq // tk
    npair = SB_HEADS // 2

    def body(q_ref, k_ref, v_ref, o_ref, tot_ref, first_ref, acc_s, c_s, z_s, w_s, kmax_s):
        p = pl.program_id(0)
        i = pl.program_id(1)

        @pl.when(i == 0)
        def _():
            kmax_s[...] = jnp.max(jnp.abs(k_ref[...]), axis=0, keepdims=True).astype(F32)

        q = q_ref[...]
        lane = lax.broadcasted_iota(jnp.int32, (tq, LANES), 1)
        first = lane < HEAD_DIM
        zero = jnp.zeros_like(q)
        q_heads = (jnp.where(first, q, zero), jnp.where(first, zero, q))
        row = lax.broadcasted_iota(jnp.int32, (tq, tk), 0)
        col = lax.broadcasted_iota(jnp.int32, (tq, tk), 1)
        visible = [col + r * tk < row for r in range(ratio)]
        krow = lax.broadcasted_iota(jnp.int32, (tk, tk), 0)
        kcol = lax.broadcasted_iota(jnp.int32, (tk, tk), 1)
        from_s = (krow >= kcol).astype(BF16)
        acc_s[...] = jnp.zeros_like(acc_s)
        c_s[...] = jnp.zeros_like(c_s)

        def rows(j):
            return pl.ds(pl.multiple_of(j * tk, tk), tk)

        def logits(j):
            kb = k_ref[rows(j), :]
            for hd in range(2):
                z_s[hd] = _dot_nt(q_heads[hd], kb) * LOG2E

        def flush(j):
            vb = v_ref[rows(j), :]
            for hd in range(2):
                acc_s[hd] += _dot(w_s[hd], vb)

        def block(j, mask=None, walked_before=True):
            if walked_before:
                flush(j + 1)
            chunks = [(hd, slice(r0, r0 + SB_CHUNK)) for hd in range(2) for r0 in range(0, tq, SB_CHUNK)]
            k_next = k_ref[rows(jnp.maximum(j - 1, 0)), :]
            es, sums = [], []
            for hd, rs in chunks:
                z2 = z_s[hd, rs, :]
                z_s[hd, rs, :] = _dot_nt(q_heads[hd][rs, :], k_next) * LOG2E
                if mask is not None:
                    z2 = jnp.where(mask[rs, :], z2, NEG_BIG)
                sp = _softplus2(z2)
                c = c_s[hd, rs, :]
                es.append(z2 + _twice(c))
                c_s[hd, rs, :] = c - jnp.sum(sp, axis=1, keepdims=True)
                sums.append(_dot(sp.astype(BF16), from_s))
            for (hd, rs), e, s in zip(chunks, es, sums):
                w_s[hd, rs, :] = jnp.exp2(e - s).astype(BF16)

        z_bound = [LOG2E * jnp.sum(jnp.abs(q_heads[hd].astype(F32)) * kmax_s[...], axis=1, keepdims=True)
                   for hd in range(2)]

        def more_keys_matter():
            top = jnp.maximum(c_s[0] + z_bound[0], c_s[1] + z_bound[1])
            return (jnp.max(top) >= -SB_UNDERFLOW_BITS).astype(jnp.int32)

        logits(ratio * i + ratio - 1)
        for r in reversed(range(ratio)):
            block(ratio * i + r, visible[r], walked_before=(r != ratio - 1))

        def trip(carry):
            trips, _ = carry
            for r in reversed(range(ratio)):
                block(ratio * (i - 1 - trips) + r)
            return trips + 1, more_keys_matter()

        trips, _ = lax.while_loop(lambda carry: jnp.logical_and(carry[0] < i, carry[1] > 0), trip,
                                  (jnp.int32(0), jnp.int32(1)))
        first_walked = ratio * (i - trips)
        flush(first_walked)
        first_ref[p, i] = first_walked.astype(F32)
        o_ref[...] = jnp.where(first, acc_s[0], acc_s[1]).astype(BF16)
        tot_ref[...] = jnp.where(first, c_s[0], c_s[1])

    return pl.pallas_call(
        body,
        name="sb_fwd",
        grid=(npair, T // tq),
        in_specs=[
            pl.BlockSpec((tq, LANES), lambda p, i: (i, p)),
            pl.BlockSpec((T, LANES), lambda p, i: (0, npair + p)),
            pl.BlockSpec((T, LANES), lambda p, i: (0, 2 * npair + p)),
        ],
        out_specs=[pl.BlockSpec((tq, LANES), lambda p, i: (i, p))] * 2 + [pl.BlockSpec(memory_space=pltpu.SMEM)],
        out_shape=[jax.ShapeDtypeStruct((T, D_MODEL), BF16), jax.ShapeDtypeStruct((T, D_MODEL), F32),
                   jax.ShapeDtypeStruct((npair, T // tq), F32)],
        scratch_shapes=[
            pltpu.VMEM((2, tq, LANES), F32), pltpu.VMEM((2, tq, LANES), F32),
            pltpu.VMEM((2, tq, tk), F32), pltpu.VMEM((2, tq, tk), BF16),
            pltpu.VMEM((1, LANES), F32),
        ],
        compiler_params=_cparams(),
    )(qkv, qkv, qkv)


def sb_bwd(qkv, do, tot, first_block):
    T = qkv.shape[0]
    tq, tk = SB_QROWS_BWD, SB_BLOCK
    ratio = tq // tk
    npair = SB_HEADS // 2
    nq = T // tq
    nk = T // tk
    assert SB_QROWS == SB_QROWS_BWD

    def body(first_ref, q_ref, k_ref, v_ref, do_ref, tot_ref, dq_ref, dk_ref, dv_ref,
             dkt_s, dvt_s, dq_s, rest_s, cg_s, z_s, da_s, dz_s, a_s):
        i = pl.program_id(1)
        start = jnp.clip(first_ref[pl.program_id(0), i].astype(jnp.int32), 0, ratio * i)

        @pl.when(i == 0)
        def _():
            dkt_s[...] = jnp.zeros_like(dkt_s)
            dvt_s[...] = jnp.zeros_like(dvt_s)

        q = q_ref[...]
        do_ = do_ref[...]
        tot_ = tot_ref[...]
        q_t = q.astype(F32).T.astype(BF16)
        do_t = do_.astype(F32).T.astype(BF16)
        lane = lax.broadcasted_iota(jnp.int32, (tq, LANES), 1)
        first = lane < HEAD_DIM
        zero = jnp.zeros_like(q)
        q_heads = (jnp.where(first, q, zero), jnp.where(first, zero, q))
        do_heads = (jnp.where(first, do_, zero), jnp.where(first, zero, do_))
        row = lax.broadcasted_iota(jnp.int32, (tq, tk), 0)
        col = lax.broadcasted_iota(jnp.int32, (tq, tk), 1)
        visible = [col + r * tk < row for r in range(ratio)]
        krow = lax.broadcasted_iota(jnp.int32, (tk, tk), 0)
        kcol = lax.broadcasted_iota(jnp.int32, (tk, tk), 1)
        before = (krow < kcol).astype(BF16)
        from_s = (krow >= kcol).astype(BF16)
        last = ratio * i + ratio - 1
        rest_s[0] = jnp.broadcast_to(tot_[:, 0:1], (tq, LANES))
        rest_s[1] = jnp.broadcast_to(tot_[:, HEAD_DIM:HEAD_DIM + 1], (tq, LANES))
        cg_s[...] = jnp.zeros_like(cg_s)
        dq_s[...] = jnp.zeros_like(dq_s)
        dz_s[...] = jnp.zeros_like(dz_s)
        a_s[...] = jnp.zeros_like(a_s)

        def rows(j):
            return pl.ds(pl.multiple_of(j * tk, tk), tk)

        def logits(j):
            kb = k_ref[rows(j), :]
            vb = v_ref[rows(j), :]
            for hd in range(2):
                z_s[hd] = _dot_nt(q_heads[hd], kb) * LOG2E
                da_s[hd] = _dot_nt(do_heads[hd], vb)

        def flush(j):
            kb = k_ref[rows(j), :]
            for hd in range(2):
                dims = slice(hd * HEAD_DIM, (hd + 1) * HEAD_DIM)
                dq_s[hd] += _dot(dz_s[hd], kb)
                dkt_s[j, dims, :] += _dot(q_t[dims, :], dz_s[hd])
                dvt_s[j, dims, :] += _dot(do_t[dims, :], a_s[hd])

        def block(j, mask=None):
            flush(jnp.maximum(j - 1, 0))
            chunks = [(hd, slice(r0, r0 + SB_CHUNK)) for hd in range(2) for r0 in range(0, tq, SB_CHUNK)]
            nxt = rows(jnp.minimum(j + 1, last))
            k_next = k_ref[nxt, :]
            v_next = v_ref[nxt, :]
            stage1 = []
            for hd, rs in chunks:
                z2 = z_s[hd, rs, :]
                z_s[hd, rs, :] = _dot_nt(q_heads[hd][rs, :], k_next) * LOG2E
                if mask is not None:
                    z2 = jnp.where(mask[rs, :], z2, NEG_BIG)
                sp = _softplus2(z2)
                rest = rest_s[hd, rs, :] + jnp.sum(sp, axis=1, keepdims=True)
                rest_s[hd, rs, :] = rest
                stage1.append((z2 + _twice(rest), z2 - sp, _dot(sp.astype(BF16), from_s)))
            stage2 = []
            for (hd, rs), (e, log2_beta, ahead) in zip(chunks, stage1):
                a = jnp.exp2(e - ahead)
                g = a * da_s[hd, rs, :]
                da_s[hd, rs, :] = _dot_nt(do_heads[hd][rs, :], v_next)
                cg = cg_s[hd, rs, :]
                a_s[hd, rs, :] = a.astype(BF16)
                cg_s[hd, rs, :] = cg + jnp.sum(g, axis=1, keepdims=True)
                stage2.append((g, g + _twice(cg), log2_beta, _dot(g.astype(BF16), before)))
            for (hd, rs), (g, g_from, log2_beta, g_before) in zip(chunks, stage2):
                dz_s[hd, rs, :] = (g - jnp.exp2(log2_beta) * (g_from + g_before)).astype(BF16)

        logits(start)

        @pl.loop(start, ratio * i)
        def _(j):
            block(j)

        for r in range(ratio):
            block(ratio * i + r, visible[r])
        flush(last)
        dq_ref[...] = (jnp.where(first, dq_s[0], dq_s[1]) * ATTN_SCALE).astype(BF16)

        @pl.when(i == nq - 1)
        def _():
            @pl.loop(0, nk)
            def _(b):
                dk_ref[rows(b), :] = dkt_s[b].T.astype(BF16)
                dv_ref[rows(b), :] = dvt_s[b].T.astype(BF16)

    qblk = pl.BlockSpec((tq, LANES), lambda p, i: (i, p))
    full = pl.BlockSpec((T, LANES), lambda p, i: (0, p))
    return pl.pallas_call(
        body,
        name="sb_bwd",
        grid=(npair, nq),
        in_specs=[
            pl.BlockSpec(memory_space=pltpu.SMEM),
            qblk,
            pl.BlockSpec((T, LANES), lambda p, i: (0, npair + p)),
            pl.BlockSpec((T, LANES), lambda p, i: (0, 2 * npair + p)),
            qblk, qblk,
        ],
        out_specs=[qblk, full, full],
        out_shape=[jax.ShapeDtypeStruct((T, D_MODEL), BF16)] * 3,
        scratch_shapes=[
            pltpu.VMEM((nk, LANES, tk), F32), pltpu.VMEM((nk, LANES, tk), F32),
            pltpu.VMEM((2, tq, LANES), F32), pltpu.VMEM((2, tq, LANES), F32), pltpu.VMEM((2, tq, LANES), F32),
            pltpu.VMEM((2, tq, tk), F32), pltpu.VMEM((2, tq, tk), F32),
            pltpu.VMEM((2, tq, tk), BF16), pltpu.VMEM((2, tq, tk), BF16),
        ],
        compiler_params=_cparams(),
    )(first_block, qkv, qkv, qkv, do, tot)


def _swa_valid(n):
    qi = lax.broadcasted_iota(jnp.int32, (WINDOW, 2 * WINDOW), 0)
    ki = lax.broadcasted_iota(jnp.int32, (WINDOW, 2 * WINDOW), 1)
    diff = qi + WINDOW - ki
    return (diff >= 0) & (diff < WINDOW) & ((n > 0) | (ki >= WINDOW))


def _to_half(x, first, src, dst):
    keep = first if src == 0 else jnp.logical_not(first)
    x = jnp.where(keep, x, jnp.zeros_like(x))
    if src != dst:
        x = pltpu.roll(x.astype(F32), HEAD_DIM, 1).astype(BF16)
    return x


SWA_GROUP = SWA_Q_HEADS // SWA_KV_HEADS


def _swa_cols(h):
    return slice((h // 2) * LANES, (h // 2 + 1) * LANES)


def _swa_kv_pair(h):
    return (h // SWA_GROUP) // 2


def _swa_kv_half(h):
    return (h // SWA_GROUP) % 2


def _kv_band(prev_ref, cur_ref, pb):
    cols = slice(pb * LANES, (pb + 1) * LANES)
    return jnp.concatenate([prev_ref[:, cols], cur_ref[:, cols]], axis=0)


def _swa_specs(T):
    nb = T // WINDOW
    kv_w = SWA_KV_HEADS * HEAD_DIM
    qrow = pl.BlockSpec((WINDOW, D_MODEL), lambda n: (n, 0))
    cur = pl.BlockSpec((WINDOW, kv_w), lambda n: (n, 0))
    prev = pl.BlockSpec((WINDOW, kv_w), lambda n: (jnp.maximum(n - 1, 0), 0))
    smem = pl.BlockSpec(memory_space=pltpu.SMEM)
    return nb, qrow, cur, prev, smem


def swa_fwd(q, k, v, sinks):
    T = q.shape[0]
    nb, qrow, cur, prev, smem = _swa_specs(T)

    def body(sink_ref, q_ref, kc_ref, kp_ref, vc_ref, vp_ref, o_ref, lse_ref):
        n = pl.program_id(0)
        lane = lax.broadcasted_iota(jnp.int32, (WINDOW, LANES), 1)
        first = lane < HEAD_DIM
        valid = _swa_valid(n)
        k2 = [_kv_band(kp_ref, kc_ref, pb) for pb in range(SWA_KV_HEADS // 2)]
        v2 = [_kv_band(vp_ref, vc_ref, pb) for pb in range(SWA_KV_HEADS // 2)]
        logits = [jnp.where(valid, _dot_nt(_to_half(q_ref[:, _swa_cols(h)], first, h % 2, _swa_kv_half(h)),
                                            k2[_swa_kv_pair(h)]), NEG_BIG) for h in range(SWA_Q_HEADS)]
        probs = []
        lse_acc = jnp.zeros((WINDOW, LANES), F32)
        for h, s in enumerate(logits):
            sink = sink_ref[h]
            m = jnp.maximum(jnp.max(s, axis=1, keepdims=True), sink)
            p = jnp.exp(s - m)
            den = jnp.sum(p, axis=1, keepdims=True) + jnp.exp(sink - m)
            probs.append((p / den).astype(BF16))
            lse_acc = jnp.where(lane == h, m + jnp.log(den), lse_acc)
        outs = []
        for h, p in enumerate(probs):
            o = _dot(p, v2[_swa_kv_pair(h)])
            outs.append(pltpu.roll(o, HEAD_DIM, 1) if h % 2 != _swa_kv_half(h) else o)
        for pair in range(SWA_Q_HEADS // 2):
            o_ref[:, _swa_cols(2 * pair)] = jnp.where(first, outs[2 * pair], outs[2 * pair + 1]).astype(BF16)
        lse_ref[...] = lse_acc

    return pl.pallas_call(
        body,
        name="swa_fwd",
        grid=(nb,),
        in_specs=[smem, qrow, cur, prev, cur, prev],
        out_specs=[qrow, pl.BlockSpec((WINDOW, LANES), lambda n: (n, 0))],
        out_shape=[jax.ShapeDtypeStruct((T, D_MODEL), BF16), jax.ShapeDtypeStruct((T, LANES), F32)],
        compiler_params=_cparams(),
    )(sinks, q, k, k, v, v)


def swa_bwd(q, k, v, sinks, do, o, lse, cos, sin):
    T = q.shape[0]
    nb, qrow, cur, prev, smem = _swa_specs(T)
    kv_w = SWA_KV_HEADS * HEAD_DIM

    def body(sink_ref, q_ref, kc_ref, kp_ref, vc_ref, vp_ref, do_ref, o_ref, lse_ref, cos_ref, sin_ref,
             dq_ref, own_ref, prv_ref, dsink_ref):
        n = pl.program_id(0)

        @pl.when(n == 0)
        def _():
            dsink_ref[...] = jnp.zeros_like(dsink_ref)

        lane = lax.broadcasted_iota(jnp.int32, (WINDOW, LANES), 1)
        lane1 = lax.broadcasted_iota(jnp.int32, (1, LANES), 1)
        first = lane < HEAD_DIM
        valid = _swa_valid(n)
        cos_ = cos_ref[...]
        sin_ = sin_ref[...]
        k2 = [_kv_band(kp_ref, kc_ref, pb) for pb in range(SWA_KV_HEADS // 2)]
        v2 = [_kv_band(vp_ref, vc_ref, pb) for pb in range(SWA_KV_HEADS // 2)]
        q_t = q_ref[...].astype(F32).T.astype(BF16)
        do_t = do_ref[...].astype(F32).T.astype(BF16)
        stage1 = []
        for h in range(SWA_Q_HEADS):
            a, b, pb = h % 2, _swa_kv_half(h), _swa_kv_pair(h)
            qh = _to_half(q_ref[:, _swa_cols(h)], first, a, b)
            doh = _to_half(do_ref[:, _swa_cols(h)], first, a, b)
            stage1.append((jnp.where(valid, _dot_nt(qh, k2[pb]), NEG_BIG), _dot_nt(doh, v2[pb])))
        deltas = []
        for pair in range(SWA_Q_HEADS // 2):
            prod = do_ref[:, _swa_cols(2 * pair)].astype(F32) * o_ref[:, _swa_cols(2 * pair)].astype(F32)
            deltas += [jnp.sum(jnp.where(first, prod, 0.0), axis=1, keepdims=True),
                       jnp.sum(jnp.where(first, 0.0, prod), axis=1, keepdims=True)]
        stage2 = []
        dsink = jnp.zeros((1, LANES), F32)
        for h, (s, dp) in enumerate(stage1):
            lse_h = lse_ref[:, h:h + 1]
            p = jnp.exp(s - lse_h)
            delta = deltas[h]
            p_sink = jnp.exp(sink_ref[h] - lse_h)
            dsink = dsink + jnp.where(lane1 == h, -jnp.sum(p_sink * delta, axis=0, keepdims=True), 0.0)
            stage2.append(((p * (dp - delta)).astype(BF16), p.astype(BF16)))
        dqs = []
        dk_t = [None] * SWA_KV_HEADS
        dv_t = [None] * SWA_KV_HEADS
        for h, (ds, pb16) in enumerate(stage2):
            kvh = h // SWA_GROUP
            dims = slice(h * HEAD_DIM, (h + 1) * HEAD_DIM)
            dq = _dot(ds, k2[_swa_kv_pair(h)])
            dqs.append(pltpu.roll(dq, HEAD_DIM, 1) if h % 2 != _swa_kv_half(h) else dq)
            dk_h = _dot(q_t[dims, :], ds)
            dv_h = _dot(do_t[dims, :], pb16)
            dk_t[kvh] = dk_h if dk_t[kvh] is None else dk_t[kvh] + dk_h
            dv_t[kvh] = dv_h if dv_t[kvh] is None else dv_t[kvh] + dv_h
        for pair in range(SWA_Q_HEADS // 2):
            dqp = jnp.where(first, dqs[2 * pair], dqs[2 * pair + 1])
            dq_ref[:, _swa_cols(2 * pair)] = ((dqp * cos_ + _swap32(dqp * sin_)) * ATTN_SCALE).astype(BF16)
        for pb in range(SWA_KV_HEADS // 2):
            dk2 = jnp.concatenate([dk_t[2 * pb], dk_t[2 * pb + 1]], axis=0).T
            dv2 = jnp.concatenate([dv_t[2 * pb], dv_t[2 * pb + 1]], axis=0).T
            kcols = slice(pb * LANES, (pb + 1) * LANES)
            vcols = slice(kv_w + pb * LANES, kv_w + (pb + 1) * LANES)
            prv_ref[:, kcols] = dk2[:WINDOW]
            own_ref[:, kcols] = dk2[WINDOW:]
            prv_ref[:, vcols] = dv2[:WINDOW]
            own_ref[:, vcols] = dv2[WINDOW:]
        dsink_ref[...] += dsink

    tab = pl.BlockSpec((WINDOW, LANES), lambda n: (n, 0))
    kvrow = pl.BlockSpec((WINDOW, 2 * kv_w), lambda n: (n, 0))
    return pl.pallas_call(
        body,
        name="swa_bwd",
        grid=(nb,),
        in_specs=[smem, qrow, cur, prev, cur, prev, qrow, qrow, tab, tab, tab],
        out_specs=[qrow, kvrow, kvrow, pl.BlockSpec((1, LANES), lambda n: (0, 0))],
        out_shape=[
            jax.ShapeDtypeStruct((T, D_MODEL), BF16),
            jax.ShapeDtypeStruct((T, 2 * kv_w), F32),
            jax.ShapeDtypeStruct((T, 2 * kv_w), F32),
            jax.ShapeDtypeStruct((1, LANES), F32),
        ],
        compiler_params=_cparams(),
    )(sinks, q, k, k, v, v, do, o, lse, cos, sin)


def kv_grad_combine(own, prv, cos, sin):
    T = own.shape[0]
    nb = T // WINDOW
    kv_w = SWA_KV_HEADS * HEAD_DIM

    def body(own_ref, nxt_ref, cos_ref, sin_ref, out_ref):
        n = pl.program_id(0)
        nxt = jnp.where(n + 1 < nb, nxt_ref[...], 0.0)
        tot = own_ref[...] + nxt
        dk = tot[:, :kv_w]
        c = _tile_lanes(cos_ref[...], kv_w)
        s = _tile_lanes(sin_ref[...], kv_w)
        out_ref[:, :kv_w] = (dk * c + _swap32(dk * s)).astype(BF16)
        out_ref[:, kv_w:] = tot[:, kv_w:].astype(BF16)

    tab = pl.BlockSpec((WINDOW, LANES), lambda n: (n, 0))
    kvrow = pl.BlockSpec((WINDOW, 2 * kv_w), lambda n: (n, 0))
    return pl.pallas_call(
        body,
        name="kv_grad_combine",
        grid=(nb,),
        in_specs=[kvrow, pl.BlockSpec((WINDOW, 2 * kv_w), lambda n: (jnp.minimum(n + 1, nb - 1), 0)), tab, tab],
        out_specs=kvrow,
        out_shape=jax.ShapeDtypeStruct((T, 2 * kv_w), BF16),
        compiler_params=_cparams(),
    )(own, prv, cos, sin)


ANY = pl.BlockSpec(memory_space=pl.ANY)


def _place():
    x, y, c = lax.axis_index("x"), lax.axis_index("y"), lax.axis_index("c")
    other_chips = [(1 - x, y), (x, 1 - y), (1 - x, 1 - y)]
    return x, y, c, 2 * x + y, other_chips


N_PEER_CHIPS = N_CHIPS - 1


def all_gather_weights(shards, lands):
    n = len(shards)

    def body(*refs):
        ins, outs = refs[:n], refs[2 * n:3 * n]
        ici_send, ici_recv, d2d_send, d2d_recv = refs[3 * n:]
        x, y, c, me, chips = _place()

        def half(ref, t, which):
            r = shards[t].shape[1] // 2
            return ref.at[:, pl.ds(pl.multiple_of(which * r, 16), r), :]

        def ici(t, jdx):
            px, py = chips[jdx]
            return pltpu.make_async_remote_copy(
                src_ref=half(ins[t], t, c), dst_ref=half(outs[t].at[me], t, c), send_sem=ici_send.at[t, jdx],
                recv_sem=ici_recv.at[t, jdx], device_id=(px, py, c), device_id_type=MESH)

        def landed(t, jdx):
            px, py = chips[jdx]
            blk = half(outs[t].at[2 * px + py], t, c)
            return pltpu.make_async_remote_copy(
                src_ref=blk, dst_ref=blk, send_sem=ici_send.at[t, jdx], recv_sem=ici_recv.at[t, jdx],
                device_id=(px, py, c), device_id_type=MESH)

        def d2d(t, jdx, which):
            px, py = chips[jdx]
            blk = half(outs[t].at[2 * px + py], t, which)
            return pltpu.make_async_remote_copy(
                src_ref=blk, dst_ref=blk, send_sem=d2d_send.at[t, jdx], recv_sem=d2d_recv.at[t, jdx],
                device_id=(x, y, 1 - c), device_id_type=MESH)

        for t in range(n):
            for jdx in range(N_PEER_CHIPS):
                ici(t, jdx).start()
        for t in range(n):
            for jdx in range(N_PEER_CHIPS):
                landed(t, jdx).wait_recv()
                d2d(t, jdx, c).start()
        for t in range(n):
            for jdx in range(N_PEER_CHIPS):
                d2d(t, jdx, 1 - c).wait_recv()
                d2d(t, jdx, c).wait_send()
                ici(t, jdx).wait_send()

    return pl.pallas_call(
        body,
        name="all_gather_weights",
        in_specs=[ANY] * (2 * n),
        out_specs=[ANY] * n,
        out_shape=[jax.ShapeDtypeStruct(l.shape, l.dtype) for l in lands],
        input_output_aliases={n + t: t for t in range(n)},
        scratch_shapes=[pltpu.SemaphoreType.DMA((n, N_PEER_CHIPS))] * 4,
    )(*shards, *lands)


def place_own_shard(name, shard, chip):
    nl, r, c = shard.shape

    def body(chip_ref, s_ref, o_ref):
        o_ref[...] = s_ref[...]

    return pl.pallas_call(
        body, name=name,
        grid_spec=pltpu.PrefetchScalarGridSpec(
            num_scalar_prefetch=1, grid=(nl,),
            in_specs=[pl.BlockSpec((None, r, c), lambda l, chip_ref: (l, 0, 0))],
            out_specs=pl.BlockSpec((None, None, r, c), lambda l, chip_ref: (chip_ref[0], l, 0, 0))),
        out_shape=jax.ShapeDtypeStruct((N_CHIPS,) + shard.shape, shard.dtype), compiler_params=_cparams(),
    )(chip, shard)


def exchange_halves(slabs):
    n = len(slabs)

    def body(*refs):
        ins, theirs = refs[:n], refs[n:2 * n]
        send_sems, recv_sems = refs[2 * n:]
        x, y, c, _, _ = _place()
        copies = []
        for t in range(n):
            cp = pltpu.make_async_remote_copy(
                src_ref=ins[t].at[1 - c], dst_ref=theirs[t], send_sem=send_sems.at[t],
                recv_sem=recv_sems.at[t], device_id=(x, y, 1 - c), device_id_type=MESH)
            cp.start()
            copies.append(cp)
        for cp in copies:
            cp.wait()

    return pl.pallas_call(
        body,
        name="exchange_halves",
        in_specs=[ANY] * n,
        out_specs=[ANY] * n,
        out_shape=[jax.ShapeDtypeStruct(s.shape[1:], s.dtype) for s in slabs],
        scratch_shapes=[pltpu.SemaphoreType.DMA((n,)), pltpu.SemaphoreType.DMA((n,))],
    )(*slabs)


def exchange_chip_partials(parts):
    n = len(parts)

    def body(*refs):
        ins, outs = refs[:n], refs[n:2 * n]
        send_sems, recv_sems = refs[2 * n:]
        _, _, c, me, chips = _place()
        copies = []
        for t in range(n):
            for jdx, (px, py) in enumerate(chips):
                cp = pltpu.make_async_remote_copy(
                    src_ref=ins[t].at[2 * px + py], dst_ref=outs[t].at[me], send_sem=send_sems.at[t, jdx],
                    recv_sem=recv_sems.at[t, jdx], device_id=(px, py, c), device_id_type=MESH)
                cp.start()
                copies.append(cp)
        for cp in copies:
            cp.wait()

    return pl.pallas_call(
        body,
        name="exchange_chip_partials",
        in_specs=[ANY] * n,
        out_specs=[ANY] * n,
        out_shape=[jax.ShapeDtypeStruct(p.shape, p.dtype) for p in parts],
        scratch_shapes=[pltpu.SemaphoreType.DMA((n, 3)), pltpu.SemaphoreType.DMA((n, 3))],
    )(*parts)


def share_reduced_halves(halves):
    n = len(halves)

    def body(*refs):
        ins, outs = refs[:n], refs[n:2 * n]
        send_sems, recv_sems = refs[2 * n:]
        x, y, c, _, _ = _place()
        copies = []
        for t in range(n):
            cp = pltpu.make_async_remote_copy(
                src_ref=ins[t], dst_ref=outs[t], send_sem=send_sems.at[t],
                recv_sem=recv_sems.at[t], device_id=(x, y, 1 - c), device_id_type=MESH)
            cp.start()
            copies.append(cp)
        for cp in copies:
            cp.wait()

    return pl.pallas_call(
        body,
        name="share_reduced_halves",
        in_specs=[ANY] * n,
        out_specs=[ANY] * n,
        out_shape=[jax.ShapeDtypeStruct(h.shape, h.dtype) for h in halves],
        scratch_shapes=[pltpu.SemaphoreType.DMA((n,)), pltpu.SemaphoreType.DMA((n,))],
    )(*halves)


def _row_tile(r, c):
    tr = r
    while tr * c * 4 > (3 << 19) and tr % 16 == 0:
        tr //= 2
    return tr


def add_sibling(name, slab, theirs, core):
    _, ns, slots, r, c = slab.shape
    tr = _row_tile(r, c)

    def body(core_ref, a_ref, b_ref, o_ref):
        o_ref[...] = (a_ref[...] + b_ref[...]).astype(BF16)

    blk = pl.BlockSpec((None, None, tr, c), lambda s, l, i, core_ref: (s, l, i, 0))
    return pl.pallas_call(
        body, name=name,
        grid_spec=pltpu.PrefetchScalarGridSpec(
            num_scalar_prefetch=1, grid=(ns, slots, r // tr),
            in_specs=[pl.BlockSpec((None, None, None, tr, c), lambda s, l, i, core_ref: (core_ref[0], s, l, i, 0)), blk],
            out_specs=blk),
        out_shape=jax.ShapeDtypeStruct(theirs.shape, BF16), compiler_params=_cparams(),
    )(core, slab, theirs)


def sum_chips(name, recv, own, chip):
    _, slots, r, c = recv.shape
    tr = _row_tile(r, c)

    def body(chip_ref, r0, r1, r2, r3, own_ref, o_ref):
        me = chip_ref[0]
        mine = own_ref[...]
        terms = [jnp.where(me == s, mine, rr[...]).astype(F32) for s, rr in enumerate((r0, r1, r2, r3))]
        o_ref[...] = ((terms[0] + terms[1]) + terms[2]) + terms[3]

    def src(s):
        return pl.BlockSpec((None, None, tr, c),
                            lambda l, i, chip_ref: (jnp.where(chip_ref[0] == s, (s + 1) % N_CHIPS, s), l, i, 0))

    return pl.pallas_call(
        body, name=name,
        grid_spec=pltpu.PrefetchScalarGridSpec(
            num_scalar_prefetch=1, grid=(slots, r // tr),
            in_specs=[src(0), src(1), src(2), src(3),
                      pl.BlockSpec((None, None, tr, c), lambda l, i, chip_ref: (chip_ref[0], l, i, 0))],
            out_specs=pl.BlockSpec((None, tr, c), lambda l, i, chip_ref: (l, i, 0))),
        out_shape=jax.ShapeDtypeStruct((slots, r, c), F32), compiler_params=_cparams(),
    )(chip, recv, recv, recv, recv, own)


def _adamw_math(w, g, m, v):
    m = ADAM_B1 * m + (1.0 - ADAM_B1) * g
    v = ADAM_B2 * v + (1.0 - ADAM_B2) * (g * g)
    m_hat = m / (1.0 - ADAM_B1 ** ADAM_STEP)
    v_hat = v / (1.0 - ADAM_B2 ** ADAM_STEP)
    delta = -ADAM_LR * (m_hat / (jnp.sqrt(v_hat) + ADAM_EPS) + ADAM_WD * w)
    return delta, m, v


def adamw_shard(name, w, m, v, g_own, g_sib, core, slot0, row_halves):
    n = w.shape[0]
    _, r, c = g_own.shape
    tr = _row_tile(r, c)
    nr = r // tr

    def body(core_ref, w_ref, m_ref, v_ref, own_ref, sib_ref, go_ref, d_ref, mo_ref, vo_ref):
        g = jnp.where(pl.program_id(1) == core_ref[0], own_ref[...], sib_ref[...])
        delta, mm, vv = _adamw_math(w_ref[...], g, m_ref[...], v_ref[...])
        go_ref[...] = g
        d_ref[...] = delta
        mo_ref[...] = mm
        vo_ref[...] = vv

    if row_halves:
        wspec = pl.BlockSpec((None, tr, c), lambda l, h, i, core_ref: (l, h * nr + i, 0))
    else:
        wspec = pl.BlockSpec((None, tr, c), lambda l, h, i, core_ref: (l, i, h))
    gspec = pl.BlockSpec((None, tr, c), lambda l, h, i, core_ref: (slot0 + l, i, 0))
    shp = jax.ShapeDtypeStruct(w.shape, F32)
    return pl.pallas_call(
        body, name=name,
        grid_spec=pltpu.PrefetchScalarGridSpec(
            num_scalar_prefetch=1, grid=(n, 2, nr),
            in_specs=[wspec, wspec, wspec, gspec, gspec], out_specs=[wspec] * 4),
        out_shape=[shp] * 4, compiler_params=_cparams(),
    )(core, w, m, v, g_own, g_sib)


SMALL_ROWS = 16


def small_allreduce_adamw(part, w, m, v):
    def body(p_ref, w_ref, m_ref, v_ref, g_ref, d_ref, mo_ref, vo_ref, buf, send_sems, recv_sems):
        x, y, c, _, _ = _place()
        me = 4 * x + 2 * y + c
        buf[me] = p_ref[...]
        copies = []
        for k in range(1, N_DEV):
            kx, ky, kc = (k >> 2) & 1, (k >> 1) & 1, k & 1
            peer = (x ^ kx, y ^ ky, c ^ kc)
            cp = pltpu.make_async_remote_copy(
                src_ref=p_ref, dst_ref=buf.at[me], send_sem=send_sems.at[k - 1],
                recv_sem=recv_sems.at[k - 1], device_id=peer, device_id_type=MESH)
            cp.start()
            copies.append(cp)
        for cp in copies:
            cp.wait()
        g = buf[0]
        for dev in range(1, N_DEV):
            g = g + buf[dev]
        delta, mm, vv = _adamw_math(w_ref[...], g, m_ref[...], v_ref[...])
        g_ref[...] = g
        d_ref[...] = delta
        mo_ref[...] = mm
        vo_ref[...] = vv

    vm = pl.BlockSpec(memory_space=pltpu.VMEM)
    shp = jax.ShapeDtypeStruct(part.shape, F32)
    return pl.pallas_call(
        body, name="small_allreduce_adamw",
        in_specs=[vm] * 4, out_specs=[vm] * 4, out_shape=[shp] * 4,
        scratch_shapes=[
            pltpu.VMEM((N_DEV,) + part.shape, F32),
            pltpu.SemaphoreType.DMA((N_DEV - 1,)), pltpu.SemaphoreType.DMA((N_DEV - 1,)),
        ],
    )(part, w, m, v)


def _rope_tables(T):
    half = HEAD_DIM // 2
    inv_freq = ROPE_THETA ** (-jnp.arange(half, dtype=F32) / half)
    ang = jnp.arange(T).astype(F32)[:, None] * inv_freq[None, :]
    cos = jnp.tile(jnp.cos(ang), (1, LANES // half))
    sin = jnp.tile(jnp.sin(ang), (1, LANES // half))
    lane = jnp.arange(LANES)
    sign = jnp.where((lane % HEAD_DIM) < half, -1.0, 1.0).astype(F32)
    return cos, sin * sign[None, :]


def _pack_small(ffn1, mix, ffn2, kvn, fin, sinks, loss_row):
    sink_row = jnp.pad(sinks.reshape(1, SWA_Q_HEADS), ((0, 0), (0, D_MODEL - SWA_Q_HEADS)))
    rows = jnp.concatenate([ffn1, mix, ffn2, kvn.reshape(1, -1), fin.reshape(1, -1), sink_row, loss_row], axis=0)
    return jnp.concatenate([rows, jnp.zeros((SMALL_ROWS - rows.shape[0], D_MODEL), F32)], axis=0)


def kernel(x, ffn1_norm, ffn1_w_in, ffn1_w_out, mix_norm, ffn2_norm, ffn2_w_in, ffn2_w_out, sb_w_qkv, sb_w_o, kv_norm, kv_w, swa_w_q, swa_sinks, swa_w_o, final_norm, loss_target, m_ffn1_norm, m_ffn1_w_in, m_ffn1_w_out, m_mix_norm, m_ffn2_norm, m_ffn2_w_in, m_ffn2_w_out, m_sb_w_qkv, m_sb_w_o, m_kv_norm, m_kv_w, m_swa_w_q, m_swa_sinks, m_swa_w_o, m_final_norm, v_ffn1_norm, v_ffn1_w_in, v_ffn1_w_out, v_mix_norm, v_ffn2_norm, v_ffn2_w_in, v_ffn2_w_out, v_sb_w_qkv, v_sb_w_o, v_kv_norm, v_kv_w, v_swa_w_q, v_swa_sinks, v_swa_w_o, v_final_norm):
    T = x.shape[1]
    kv_cols = SWA_KV_HEADS * HEAD_DIM
    x2 = x.reshape(T, D_MODEL)
    tgt = loss_target.reshape(T, D_MODEL)
    cos, sin = _rope_tables(T)

    w_in_l = jnp.concatenate([ffn1_w_in, ffn2_w_in], axis=0).astype(BF16)
    w_out_l = jnp.concatenate([ffn1_w_out, ffn2_w_out], axis=0).astype(BF16)
    sq_l = jnp.concatenate([sb_w_o, swa_w_q, swa_w_o], axis=0).astype(BF16)
    qkv_l = sb_w_qkv[0].astype(BF16)
    kvw_l = kv_w.astype(BF16)
    core = lax.axis_index("c").astype(jnp.int32).reshape(1)
    chip = (2 * lax.axis_index("x") + lax.axis_index("y")).astype(jnp.int32).reshape(1)
    shards = [w_in_l, w_out_l, sq_l, qkv_l[None], kvw_l[None]]
    lands = [place_own_shard(f"own_shard_{t}", s, chip) for t, s in enumerate(shards)]
    w_in, w_out, w_sq, w_qkv, w_kv = all_gather_weights(shards, lands)
    w_qkv = w_qkv.reshape(N_CHIPS, D_MODEL, QKV_COLS)
    w_kv = w_kv.reshape(D_MODEL, 2 * kv_cols)

    def ffn_w(slot):
        return w_in, w_out, slot

    def vec(a, i):
        return a[i].reshape(1, D_MODEL)

    ident = lambda w: w
    sq_prep = lambda w: w.reshape(D_MODEL, w.shape[-1])
    qscale = jnp.concatenate([jnp.full((1, D_MODEL), ATTN_SCALE, F32), jnp.ones((1, 2 * D_MODEL), F32)], axis=1)
    swa_scale = jnp.full((1, D_MODEL), ATTN_SCALE, F32)
    sinks = swa_sinks.reshape(SWA_Q_HEADS)

    h1, gate1, up1 = ffn_fwd("l0a", x2, vec(ffn1_norm, 0), *ffn_w(SLOT_FFN1[0]))
    qkv = qkv_fwd(h1, vec(mix_norm, 0), w_qkv, qscale)
    o_sb, tot, sb_first = sb_fwd(qkv)
    h2 = linear_res("sb_out", o_sb, w_sq, SQ_SB_O, h1)
    h3, gate2, up2 = ffn_fwd("l0b", h2, vec(ffn2_norm, 0), *ffn_w(SLOT_FFN2[0]))
    kvn = kv_norm.reshape(1, D_MODEL)
    k_sw = rms_linear("kv_k", h3, kvn, w_kv, pl.BlockSpec((D_MODEL, kv_cols), lambda i, j: (0, 0)), ident,
                      kv_cols, kv_cols, rope=(cos, sin))
    v_sw = rms_linear("kv_v", h3, kvn, w_kv, pl.BlockSpec((D_MODEL, kv_cols), lambda i, j: (0, 1)), ident,
                      kv_cols, kv_cols)
    h4, gate3, up3 = ffn_fwd("l1a", h3, vec(ffn1_norm, 1), *ffn_w(SLOT_FFN1[1]))
    q_sw = rms_linear("swa_q", h4, vec(mix_norm, 1), w_sq,
                      pl.BlockSpec((N_CHIPS, None, SQ_ROWS, 512), lambda i, j: (0, SQ_SWA_Q, 0, j)), sq_prep,
                      D_MODEL, 512, rope=(cos, sin), scale=swa_scale)
    o_sw, lse = swa_fwd(q_sw, k_sw, v_sw, sinks)
    h5 = linear_res("swa_out", o_sw, w_sq, SQ_SWA_O, h4)
    h6, gate4, up4 = ffn_fwd("l1b", h5, vec(ffn2_norm, 1), *ffn_w(SLOT_FFN2[1]))
    dh6, loss_p, d_final = loss_bwd(h6, final_norm.reshape(1, D_MODEL), tgt)

    slab = {"in": None, "out": None, "sq": None}
    in_shape = (2, N_CHIPS, 4, D_MODEL // 2, FF_CHUNK)
    out_shape = (2, N_CHIPS, 4, FF_ROWS, D_MODEL // 2)
    sq_shape = (2, N_CHIPS, 3, SQ_ROWS, D_MODEL // 2)

    def ffn_grads(tag, dh, h_in, g, gate, up, slot):
        dh_in, xn, dg_, du_, act, dhb, dnorm = ffn_bwd(tag, dh, h_in, g, gate, up, *ffn_w(slot))
        blk = (None, 1, None, D_MODEL // 2, FF_CHUNK)
        slab["in"] = mm_tn(f"dw_gate_{tag}", xn, dg_, D_MODEL // 2, FF_CHUNK, blk,
                           lambda k, n: (k, n, slot, 0, 0), in_shape, prev=slab["in"])
        slab["in"] = mm_tn(f"dw_up_{tag}", xn, du_, D_MODEL // 2, FF_CHUNK, blk,
                           lambda k, n: (k, 2 + n, slot, 0, 0), in_shape, prev=slab["in"])
        slab["out"] = mm_tn(f"dw_out_{tag}", act, dhb, FF_CHUNK, D_MODEL // 2,
                            (None, 2, None, FF_ROWS, D_MODEL // 2),
                            lambda k, n: (n, k, slot, 0, 0), out_shape, prev=slab["out"])
        return dh_in, dnorm

    def sq_grad(tag, a, dyb, t):
        slab["sq"] = mm_tn(f"dw_sq_{tag}", a, dyb, D_MODEL, D_MODEL // 2,
                           (None, N_CHIPS, None, SQ_ROWS, D_MODEL // 2),
                           lambda k, n: (n, 0, t, 0, 0), sq_shape, prev=slab["sq"])

    dh5, d_ffn2_1 = ffn_grads("l1b", dh6, h5, vec(ffn2_norm, 1), gate4, up4, SLOT_FFN2[1])
    do_sw, dh5b = linear_bwd_plain("swa_out_bwd", dh5, w_sq, SQ_SWA_O)
    sq_grad("swa_o", o_sw, dh5b, SQ_SWA_O)
    dq_sw, kv_own, kv_prev, d_sinks = swa_bwd(q_sw, k_sw, v_sw, sinks, do_sw, o_sw, lse, cos, sin)
    sq_w_spec = pl.BlockSpec((N_CHIPS, None, SQ_ROWS, D_MODEL), lambda i, j: (0, SQ_SWA_Q, 0, 0))
    dh4, hn4, d_mix_1 = linear_bwd_rms("swa_q_bwd", [(dq_sw, w_sq, sq_w_spec, sq_prep)], h4, vec(mix_norm, 1), dh5,
                                       1, D_MODEL)
    sq_grad("swa_q", hn4, dq_sw, SQ_SWA_Q)
    dh3a, d_ffn1_1 = ffn_grads("l1a", dh4, h3, vec(ffn1_norm, 1), gate3, up3, SLOT_FFN1[1])
    dkv = kv_grad_combine(kv_own, kv_prev, cos, sin)
    kv_w_spec = pl.BlockSpec((D_MODEL, 2 * kv_cols), lambda i, j: (0, 0))
    dh3, xn3, d_kvn = linear_bwd_rms("kv_bwd", [(dkv, w_kv, kv_w_spec, ident)], h3, kvn, dh3a, 1, 2 * kv_cols)
    slab_kv = mm_tn("dw_kv", xn3, dkv, D_MODEL, kv_cols, (None, N_CHIPS, None, SQ_ROWS, kv_cols),
                    lambda k, n: (n, 0, 0, 0, 0), (2, N_CHIPS, 1, SQ_ROWS, kv_cols))
    dh2, d_ffn2_0 = ffn_grads("l0b", dh3, h2, vec(ffn2_norm, 0), gate2, up2, SLOT_FFN2[0])
    do_sb, dh2b = linear_bwd_plain("sb_out_bwd", dh2, w_sq, SQ_SB_O)
    sq_grad("sb_o", o_sb, dh2b, SQ_SB_O)
    dq_sb, dk_sb, dv_sb = sb_bwd(qkv, do_sb, tot, sb_first)
    dqkv = jnp.concatenate([dq_sb, dk_sb, dv_sb], axis=1)
    dh1, hn1, d_mix_0 = qkv_bwd(dqkv, w_qkv, h1, vec(mix_norm, 0), dh2)
    slab_qkv = mm_tn("dw_qkv", hn1, dqkv, D_MODEL // 2, QKV_COLS, (None, 1, None, D_MODEL // 2, QKV_COLS),
                     lambda k, n: (k, n, 0, 0, 0), (2, N_CHIPS, 1, D_MODEL // 2, QKV_COLS))
    dx, d_ffn1_0 = ffn_grads("l0a", dh1, x2, vec(ffn1_norm, 0), gate1, up1, SLOT_FFN1[0])

    slabs = [slab["in"], slab["out"], slab["sq"], slab_qkv, slab_kv]
    names = ["in", "out", "sq", "qkv", "kv"]
    theirs = exchange_halves(slabs)
    parts = [add_sibling(f"add_sibling_{nm}", s, t, core) for nm, s, t in zip(names, slabs, theirs)]
    gathered = exchange_chip_partials(parts)
    halves = [sum_chips(f"sum_chips_{nm}", g, p, chip) for nm, g, p in zip(names, gathered, parts)]
    sib_halves = share_reduced_halves(halves)
    g_in, g_out, g_sq, g_qkv, g_kv = zip(halves, sib_halves)

    def upd(name, w, m, v, g_pair, slot0, row_halves):
        shp = w.shape
        w3 = w.reshape((-1,) + shp[-2:])
        outs = adamw_shard(name, w3, m.reshape(w3.shape), v.reshape(w3.shape), g_pair[0], g_pair[1], core,
                           slot0, row_halves)
        return [o.reshape(shp) for o in outs]

    r_ffn1_in = upd("adamw_ffn1_in", ffn1_w_in, m_ffn1_w_in, v_ffn1_w_in, g_in, 0, True)
    r_ffn2_in = upd("adamw_ffn2_in", ffn2_w_in, m_ffn2_w_in, v_ffn2_w_in, g_in, 2, True)
    r_ffn1_out = upd("adamw_ffn1_out", ffn1_w_out, m_ffn1_w_out, v_ffn1_w_out, g_out, 0, False)
    r_ffn2_out = upd("adamw_ffn2_out", ffn2_w_out, m_ffn2_w_out, v_ffn2_w_out, g_out, 2, False)
    r_qkv = upd("adamw_qkv", sb_w_qkv, m_sb_w_qkv, v_sb_w_qkv, g_qkv, 0, True)
    r_sb_o = upd("adamw_sb_o", sb_w_o, m_sb_w_o, v_sb_w_o, g_sq, SQ_SB_O, False)
    r_swa_q = upd("adamw_swa_q", swa_w_q, m_swa_w_q, v_swa_w_q, g_sq, SQ_SWA_Q, False)
    r_swa_o = upd("adamw_swa_o", swa_w_o, m_swa_w_o, v_swa_w_o, g_sq, SQ_SWA_O, False)
    r_kv = upd("adamw_kv", kv_w, m_kv_w, v_kv_w, g_kv, 0, False)

    loss_row = jnp.pad(loss_p, ((0, 0), (0, D_MODEL - LANES)))
    d_sink_row = d_sinks[0, :SWA_Q_HEADS]
    part = _pack_small(jnp.concatenate([d_ffn1_0, d_ffn1_1], axis=0), jnp.concatenate([d_mix_0, d_mix_1], axis=0),
                       jnp.concatenate([d_ffn2_0, d_ffn2_1], axis=0), d_kvn, d_final, d_sink_row, loss_row)
    zrow = jnp.zeros((1, D_MODEL), F32)
    small = small_allreduce_adamw(
        part,
        _pack_small(ffn1_norm, mix_norm, ffn2_norm, kv_norm, final_norm, swa_sinks, zrow),
        _pack_small(m_ffn1_norm, m_mix_norm, m_ffn2_norm, m_kv_norm, m_final_norm, m_swa_sinks, zrow),
        _pack_small(v_ffn1_norm, v_mix_norm, v_ffn2_norm, v_kv_norm, v_final_norm, v_swa_sinks, zrow))

    def unpack(p):
        return dict(ffn1_norm=p[0:2], mix_norm=p[2:4], ffn2_norm=p[4:6], kv_norm=p[6], final_norm=p[7],
                    swa_sinks=p[8:9, :SWA_Q_HEADS])

    big = dict(ffn1_w_in=r_ffn1_in, ffn1_w_out=r_ffn1_out, ffn2_w_in=r_ffn2_in, ffn2_w_out=r_ffn2_out,
               sb_w_qkv=r_qkv, sb_w_o=r_sb_o, kv_w=r_kv, swa_w_q=r_swa_q, swa_w_o=r_swa_o)
    order = ["ffn1_norm", "ffn1_w_in", "ffn1_w_out", "mix_norm", "ffn2_norm", "ffn2_w_in", "ffn2_w_out",
             "sb_w_qkv", "sb_w_o", "kv_norm", "kv_w", "swa_w_q", "swa_sinks", "swa_w_o", "final_norm"]
    outs = []
    for kind in range(4):
        sm = unpack(small[kind])
        for nm in order:
            outs.append(big[nm][kind] if nm in big else sm[nm])
    loss = small[0][9, 0]
    return (loss, dx.reshape(x.shape), *outs)
```

```python
import functools

import jax
import jax.numpy as jnp
from jax import lax
from jax.experimental import pallas as pl
from jax.experimental.pallas import tpu as pltpu

F32 = jnp.float32
BF16 = jnp.bfloat16
MESH = pl.DeviceIdType.MESH

D_MODEL = 1024
D_FF = 2816
HEAD_DIM = 64
SB_HEADS = 16
SWA_Q_HEADS = 16
SWA_KV_HEADS = 4
WINDOW = 128
ROPE_THETA = 10000.0
RMS_EPS = 1e-6
FFN_RES_SCALE = 0.5
ATTN_SCALE = HEAD_DIM ** -0.5

ADAM_LR = 0.001
ADAM_B1 = 0.9
ADAM_B2 = 0.999
ADAM_EPS = 1e-08
ADAM_WD = 0.01
ADAM_STEP = 10

N_CHIPS = 4
N_DEV = 8
LANES = 128
FF_CHUNK = D_FF // 2
FF_ROWS = D_FF // N_CHIPS
SQ_ROWS = D_MODEL // N_CHIPS
QKV_COLS = 3 * D_MODEL // N_CHIPS
VMEM_LIMIT = 56 * 1024 * 1024
NEG_BIG = -1e30

SLOT_FFN1 = (0, 1)
SLOT_FFN2 = (2, 3)
SQ_SB_O, SQ_SWA_Q, SQ_SWA_O = 0, 1, 2


def _cparams():
    return pltpu.CompilerParams(vmem_limit_bytes=VMEM_LIMIT)


def _dot(a, b):
    return jnp.dot(a, b, preferred_element_type=F32)


def _dot_nt(a, b):
    return lax.dot_general(a, b, (((1,), (1,)), ((), ())), preferred_element_type=F32)


def _dot_tn(a, b):
    return lax.dot_general(a, b, (((0,), (0,)), ((), ())), preferred_element_type=F32)


def _rstd(h):
    return lax.rsqrt(jnp.mean(h * h, axis=-1, keepdims=True) + RMS_EPS)


def _swap32(x):
    n = x.shape[-1]
    lane = lax.broadcasted_iota(jnp.int32, x.shape, x.ndim - 1)
    first = (lane % HEAD_DIM) < (HEAD_DIM // 2)
    return jnp.where(first, pltpu.roll(x, n - HEAD_DIM // 2, x.ndim - 1), pltpu.roll(x, HEAD_DIM // 2, x.ndim - 1))


def _tile_lanes(t, n):
    return t if n == LANES else jnp.tile(t, (1, n // LANES))


FFN_ROWS = 256


def _ffn_w_in_spec(slot):
    return pl.BlockSpec((N_CHIPS, None, D_MODEL, FF_CHUNK), lambda i: (0, slot, 0, 0), pipeline_mode=pl.Buffered(1))


def _ffn_w_out_spec(slot):
    return pl.BlockSpec((N_CHIPS, None, FF_ROWS, D_MODEL), lambda i: (0, slot, 0, 0), pipeline_mode=pl.Buffered(1))


def ffn_fwd(tag, h, g, w_in, w_out, slot):
    T = h.shape[0]
    tm = FFN_ROWS
    nch = D_FF // FF_CHUNK

    def body(h_ref, g_ref, wi_ref, wo_ref, out_ref, gate_ref, up_ref):
        hh = h_ref[...]
        xn = (hh * _rstd(hh) * g_ref[...]).astype(BF16)
        acc = None
        for j in range(nch):
            cols = slice(j * FF_CHUNK, (j + 1) * FF_CHUNK)
            gate = _dot(xn, wi_ref[j])
            up = _dot(xn, wi_ref[nch + j])
            gate_ref[:, cols] = gate.astype(BF16)
            up_ref[:, cols] = up.astype(BF16)
            a = (gate * jax.nn.sigmoid(gate) * up).astype(BF16)
            part = _dot(a, wo_ref[2 * j:2 * j + 2].reshape(FF_CHUNK, D_MODEL))
            acc = part if acc is None else acc + part
        out_ref[...] = hh + FFN_RES_SCALE * acc

    row = pl.BlockSpec((tm, D_MODEL), lambda i: (i, 0))
    ff = pl.BlockSpec((tm, D_FF), lambda i: (i, 0))
    return pl.pallas_call(
        body,
        name=f"ffn_fwd_{tag}",
        grid=(T // tm,),
        in_specs=[row, pl.BlockSpec((1, D_MODEL), lambda i: (0, 0)), _ffn_w_in_spec(slot), _ffn_w_out_spec(slot)],
        out_specs=[row, ff, ff],
        out_shape=[
            jax.ShapeDtypeStruct((T, D_MODEL), F32),
            jax.ShapeDtypeStruct((T, D_FF), BF16),
            jax.ShapeDtypeStruct((T, D_FF), BF16),
        ],
        compiler_params=_cparams(),
    )(h, g, w_in, w_out)


def ffn_bwd(tag, dh, h, g, gate, up, w_in, w_out, slot):
    T = dh.shape[0]
    tm = FFN_ROWS
    nch = D_FF // FF_CHUNK

    def body(dh_ref, h_ref, g_ref, gate_ref, up_ref, wi_ref, wo_ref,
             dhin_ref, xn_ref, dg_ref, du_ref, a_ref, dhb_ref, dnorm_ref):
        @pl.when(pl.program_id(0) == 0)
        def _():
            dnorm_ref[...] = jnp.zeros_like(dnorm_ref)

        dhh = dh_ref[...]
        dhb = (FFN_RES_SCALE * dhh).astype(BF16)
        dhb_ref[...] = dhb
        dxn = None
        for j in range(nch):
            cols = slice(j * FF_CHUNK, (j + 1) * FF_CHUNK)
            da = _dot_nt(dhb, wo_ref[2 * j:2 * j + 2].reshape(FF_CHUNK, D_MODEL))
            gt = gate_ref[:, cols].astype(F32)
            u = up_ref[:, cols].astype(F32)
            s = jax.nn.sigmoid(gt)
            silu = gt * s
            a_ref[:, cols] = (silu * u).astype(BF16)
            dgate = (da * u * (s * (1.0 + gt * (1.0 - s)))).astype(BF16)
            dup = (da * silu).astype(BF16)
            dg_ref[:, cols] = dgate
            du_ref[:, cols] = dup
            part = _dot_nt(dgate, wi_ref[j]) + _dot_nt(dup, wi_ref[nch + j])
            dxn = part if dxn is None else dxn + part
        hh = h_ref[...]
        gg = g_ref[...]
        r = _rstd(hh)
        hr = hh * r
        xn_ref[...] = (hr * gg).astype(BF16)
        dnorm_ref[...] += jnp.sum(dxn * hr, axis=0, keepdims=True)
        gd = gg * dxn
        dhin_ref[...] = dhh + r * (gd - hr * jnp.mean(gd * hr, axis=-1, keepdims=True))

    row = pl.BlockSpec((tm, D_MODEL), lambda i: (i, 0))
    ff = pl.BlockSpec((tm, D_FF), lambda i: (i, 0))
    vec = pl.BlockSpec((1, D_MODEL), lambda i: (0, 0))
    return pl.pallas_call(
        body,
        name=f"ffn_bwd_{tag}",
        grid=(T // tm,),
        in_specs=[row, row, vec, ff, ff, _ffn_w_in_spec(slot), _ffn_w_out_spec(slot)],
        out_specs=[row, row, ff, ff, ff, row, vec],
        out_shape=[
            jax.ShapeDtypeStruct((T, D_MODEL), F32),
            jax.ShapeDtypeStruct((T, D_MODEL), BF16),
            jax.ShapeDtypeStruct((T, D_FF), BF16),
            jax.ShapeDtypeStruct((T, D_FF), BF16),
            jax.ShapeDtypeStruct((T, D_FF), BF16),
            jax.ShapeDtypeStruct((T, D_MODEL), BF16),
            jax.ShapeDtypeStruct((1, D_MODEL), F32),
        ],
        compiler_params=_cparams(),
    )(dh, h, g, gate, up, w_in, w_out)


def rms_linear(name, h, g, w, w_spec, w_prep, n_out, tn, *, rope=None, scale=None):
    T = h.shape[0]
    tm = 512
    extra, extra_specs = [], []
    if rope is not None:
        extra += list(rope)
        extra_specs += [pl.BlockSpec((tm, LANES), lambda i, j: (i, 0))] * 2
    if scale is not None:
        extra.append(scale)
        extra_specs.append(pl.BlockSpec((1, tn), lambda i, j: (0, j)))

    def body(h_ref, g_ref, w_ref, *rest):
        rest = list(rest)
        cos_ref = sin_ref = sc_ref = None
        if rope is not None:
            cos_ref, sin_ref = rest[0], rest[1]
            rest = rest[2:]
        if scale is not None:
            sc_ref = rest[0]
            rest = rest[1:]
        out_ref, xn_s = rest

        @pl.when(pl.program_id(1) == 0)
        def _():
            hh = h_ref[...]
            xn_s[...] = (hh * _rstd(hh) * g_ref[...]).astype(BF16)

        y = _dot(xn_s[...], w_prep(w_ref[...]))
        if rope is not None:
            y = y * _tile_lanes(cos_ref[...], tn) + _swap32(y) * _tile_lanes(sin_ref[...], tn)
        if scale is not None:
            y = y * sc_ref[...]
        out_ref[...] = y.astype(BF16)

    return pl.pallas_call(
        body,
        name=name,
        grid=(T // tm, n_out // tn),
        in_specs=[
            pl.BlockSpec((tm, D_MODEL), lambda i, j: (i, 0)),
            pl.BlockSpec((1, D_MODEL), lambda i, j: (0, 0)),
            w_spec,
        ] + extra_specs,
        out_specs=pl.BlockSpec((tm, tn), lambda i, j: (i, j)),
        out_shape=jax.ShapeDtypeStruct((T, n_out), BF16),
        scratch_shapes=[pltpu.VMEM((tm, D_MODEL), BF16)],
        compiler_params=_cparams(),
    )(h, g, w, *extra)


QKV_ROWS = 512


def _qkv_w_spec():
    return pl.BlockSpec((N_CHIPS, D_MODEL, QKV_COLS), lambda i: (0, 0, 0), pipeline_mode=pl.Buffered(1))


def qkv_fwd(h, g, w_qkv, scale):
    T = h.shape[0]
    tm = QKV_ROWS

    def body(h_ref, g_ref, w_ref, sc_ref, out_ref):
        hh = h_ref[...]
        xn = (hh * _rstd(hh) * g_ref[...]).astype(BF16)
        for s in range(N_CHIPS):
            cols = slice(s * QKV_COLS, (s + 1) * QKV_COLS)
            out_ref[:, cols] = (_dot(xn, w_ref[s]) * sc_ref[:, cols]).astype(BF16)

    return pl.pallas_call(
        body,
        name="sb_qkv",
        grid=(T // tm,),
        in_specs=[
            pl.BlockSpec((tm, D_MODEL), lambda i: (i, 0)),
            pl.BlockSpec((1, D_MODEL), lambda i: (0, 0)),
            _qkv_w_spec(),
            pl.BlockSpec((1, 3 * D_MODEL), lambda i: (0, 0)),
        ],
        out_specs=pl.BlockSpec((tm, 3 * D_MODEL), lambda i: (i, 0)),
        out_shape=jax.ShapeDtypeStruct((T, 3 * D_MODEL), BF16),
        compiler_params=_cparams(),
    )(h, g, w_qkv, scale)


def qkv_bwd(dy, w_qkv, h, g, dres):
    T = h.shape[0]
    tm = QKV_ROWS

    def body(dy_ref, w_ref, h_ref, g_ref, dres_ref, dh_ref, xn_ref, dg_ref):
        @pl.when(pl.program_id(0) == 0)
        def _():
            dg_ref[...] = jnp.zeros_like(dg_ref)

        dxn = None
        for s in range(N_CHIPS):
            part = _dot_nt(dy_ref[:, s * QKV_COLS:(s + 1) * QKV_COLS], w_ref[s])
            dxn = part if dxn is None else dxn + part
        hh = h_ref[...]
        gg = g_ref[...]
        r = _rstd(hh)
        hr = hh * r
        xn_ref[...] = (hr * gg).astype(BF16)
        dg_ref[...] += jnp.sum(dxn * hr, axis=0, keepdims=True)
        gd = gg * dxn
        dh_ref[...] = dres_ref[...] + r * (gd - hr * jnp.mean(gd * hr, axis=-1, keepdims=True))

    row = pl.BlockSpec((tm, D_MODEL), lambda i: (i, 0))
    vec = pl.BlockSpec((1, D_MODEL), lambda i: (0, 0))
    return pl.pallas_call(
        body,
        name="sb_qkv_bwd",
        grid=(T // tm,),
        in_specs=[pl.BlockSpec((tm, 3 * D_MODEL), lambda i: (i, 0)), _qkv_w_spec(), row, vec, row],
        out_specs=[row, row, vec],
        out_shape=[
            jax.ShapeDtypeStruct((T, D_MODEL), F32),
            jax.ShapeDtypeStruct((T, D_MODEL), BF16),
            jax.ShapeDtypeStruct((1, D_MODEL), F32),
        ],
        compiler_params=_cparams(),
    )(dy, w_qkv, h, g, dres)


def linear_res(name, a, w_sq, t, res, bg_lands=()):
    T = a.shape[0]
    tm = 512
    nbg = len(bg_lands)
    nt = T // tm

    def body(a_ref, w_ref, res_ref, *rest):
        out_ref = rest[nbg]
        if nbg:
            gather = GatherOps([l.shape[2] for l in bg_lands], None, rest[nbg + 1:2 * nbg + 1], None, None,
                               *rest[2 * nbg + 1:])

            @pl.when(pl.program_id(0) == 0)
            def _():
                gather.start_forwards()

        out_ref[...] = res_ref[...] + _dot(a_ref[...], w_ref[...].reshape(D_MODEL, D_MODEL))
        if nbg:
            @pl.when(pl.program_id(0) == nt - 1)
            def _():
                gather.wait_forwards()

    row = pl.BlockSpec((tm, D_MODEL), lambda i: (i, 0))
    res_ = pl.pallas_call(
        body,
        name=name,
        grid=(nt,),
        in_specs=[row, pl.BlockSpec((N_CHIPS, None, SQ_ROWS, D_MODEL), lambda i: (0, t, 0, 0)), row] + [ANY] * nbg,
        out_specs=[row] + [ANY] * nbg,
        out_shape=[jax.ShapeDtypeStruct((T, D_MODEL), F32)] + [jax.ShapeDtypeStruct(l.shape, l.dtype) for l in bg_lands],
        input_output_aliases={3 + k: 1 + k for k in range(nbg)},
        scratch_shapes=[pltpu.SemaphoreType.DMA((nbg, N_PEER_CHIPS))] * (2 if nbg else 0),
        compiler_params=_cparams(),
    )(a, w_sq, res, *bg_lands)
    return (res_[0], list(res_[1:])) if nbg else res_[0]


def linear_bwd_plain(name, dy, w_sq, t):
    T = dy.shape[0]
    tm = 512

    def body(dy_ref, w_ref, da_ref, dyb_ref):
        dyb = dy_ref[...].astype(BF16)
        dyb_ref[...] = dyb
        da_ref[...] = _dot_nt(dyb, w_ref[...].reshape(D_MODEL, D_MODEL)).astype(BF16)

    row = pl.BlockSpec((tm, D_MODEL), lambda i: (i, 0))
    return pl.pallas_call(
        body,
        name=name,
        grid=(T // tm,),
        in_specs=[row, pl.BlockSpec((N_CHIPS, None, SQ_ROWS, D_MODEL), lambda i: (0, t, 0, 0))],
        out_specs=[row, row],
        out_shape=[jax.ShapeDtypeStruct((T, D_MODEL), BF16), jax.ShapeDtypeStruct((T, D_MODEL), BF16)],
        compiler_params=_cparams(),
    )(dy, w_sq)


def linear_bwd_rms(name, pairs, h, g, dres, nch, tn, tm=256):
    T = h.shape[0]
    npair = len(pairs)

    def body(*refs):
        dy_refs = refs[:npair]
        w_refs = refs[npair:2 * npair]
        h_ref, g_ref, dres_ref, dh_ref, xn_ref, dg_ref, acc_s = refs[2 * npair:]
        i = pl.program_id(0)
        j = pl.program_id(1)

        @pl.when(j == 0)
        def _():
            acc_s[...] = jnp.zeros_like(acc_s)

        @pl.when((i == 0) & (j == 0))
        def _():
            dg_ref[...] = jnp.zeros_like(dg_ref)

        part = None
        for p in range(npair):
            d = _dot_nt(dy_refs[p][...], pairs[p][3](w_refs[p][...]))
            part = d if part is None else part + d
        acc_s[...] += part

        @pl.when(j == nch - 1)
        def _():
            dxn = acc_s[...]
            hh = h_ref[...]
            gg = g_ref[...]
            r = _rstd(hh)
            hr = hh * r
            xn_ref[...] = (hr * gg).astype(BF16)
            dg_ref[...] += jnp.sum(dxn * hr, axis=0, keepdims=True)
            gd = gg * dxn
            dh_ref[...] = dres_ref[...] + r * (gd - hr * jnp.mean(gd * hr, axis=-1, keepdims=True))

    row = pl.BlockSpec((tm, D_MODEL), lambda i, j: (i, 0))
    vec = pl.BlockSpec((1, D_MODEL), lambda i, j: (0, 0))
    return pl.pallas_call(
        body,
        name=name,
        grid=(T // tm, nch),
        in_specs=[pl.BlockSpec((tm, tn), lambda i, j: (i, j))] * npair + [p[2] for p in pairs] + [row, vec, row],
        out_specs=[row, row, vec],
        out_shape=[
            jax.ShapeDtypeStruct((T, D_MODEL), F32),
            jax.ShapeDtypeStruct((T, D_MODEL), BF16),
            jax.ShapeDtypeStruct((1, D_MODEL), F32),
        ],
        scratch_shapes=[pltpu.VMEM((tm, D_MODEL), F32)],
        compiler_params=_cparams(),
    )(*[p[0] for p in pairs], *[p[1] for p in pairs], h, g, dres)


def loss_bwd(h, g, tgt):
    T = h.shape[0]
    tm = 512

    def body(h_ref, g_ref, t_ref, dh_ref, loss_ref, dg_ref):
        @pl.when(pl.program_id(0) == 0)
        def _():
            loss_ref[...] = jnp.zeros_like(loss_ref)
            dg_ref[...] = jnp.zeros_like(dg_ref)

        hh = h_ref[...]
        gg = g_ref[...]
        r = _rstd(hh)
        hr = hh * r
        err = hr * gg - t_ref[...]
        loss_ref[...] += 0.5 * jnp.sum(jnp.mean(err * err, axis=-1, keepdims=True), axis=0, keepdims=True)
        dy = err * (1.0 / D_MODEL)
        dg_ref[...] += jnp.sum(dy * hr, axis=0, keepdims=True)
        gd = gg * dy
        dh_ref[...] = r * (gd - hr * jnp.mean(gd * hr, axis=-1, keepdims=True))

    row = pl.BlockSpec((tm, D_MODEL), lambda i: (i, 0))
    vec = pl.BlockSpec((1, D_MODEL), lambda i: (0, 0))
    return pl.pallas_call(
        body,
        name="loss_bwd",
        grid=(T // tm,),
        in_specs=[row, vec, row],
        out_specs=[row, pl.BlockSpec((1, LANES), lambda i: (0, 0)), vec],
        out_shape=[
            jax.ShapeDtypeStruct((T, D_MODEL), F32),
            jax.ShapeDtypeStruct((1, LANES), F32),
            jax.ShapeDtypeStruct((1, D_MODEL), F32),
        ],
        compiler_params=_cparams(),
    )(h, g, tgt)


DW_TOKENS = 4096


def mm_tn(name, a, b, tk, tn, out_block, out_index, out_shape, prev=None, tt=DW_TOKENS):
    T = a.shape[0]
    ns, r = out_block[1], out_block[3]
    tt = min(tt, T)
    nt = T // tt

    def body(*refs):
        if prev is None:
            a_ref, b_ref, out_ref = refs
        else:
            a_ref, b_ref, _, out_ref = refs
        t = pl.program_id(2)
        res = _dot_tn(a_ref[...], b_ref[...])

        @pl.when(t == 0)
        def _():
            for u in range(ns):
                out_ref[u] = res[u * r:(u + 1) * r]

        @pl.when(t > 0)
        def _():
            for u in range(ns):
                out_ref[u] += res[u * r:(u + 1) * r]

    in_specs = [
        pl.BlockSpec((tt, tk), lambda k, n, t: (t, k)),
        pl.BlockSpec((tt, tn), lambda k, n, t: (t, n)),
    ]
    args = [a, b]
    aliases = {}
    if prev is not None:
        in_specs.append(pl.BlockSpec(memory_space=pl.ANY))
        args.append(prev)
        aliases = {2: 0}
    return pl.pallas_call(
        body,
        name=name,
        grid=(a.shape[1] // tk, b.shape[1] // tn, nt),
        in_specs=in_specs,
        out_specs=pl.BlockSpec(out_block, lambda k, n, t: out_index(k, n)),
        out_shape=jax.ShapeDtypeStruct(out_shape, F32),
        input_output_aliases=aliases,
        compiler_params=_cparams(),
    )(*args)


SB_BLOCK = 256
SB_QROWS = 256
SB_QROWS_BWD = 256
SB_UNDERFLOW_BITS = 140.0
SB_CHUNK = 128


LOG2E = 1.4426950408889634


def _softplus2(z2):
    sign = jnp.uint32(0x80000000)
    neg_abs = lax.bitcast_convert_type(lax.bitcast_convert_type(z2, jnp.uint32) | sign, F32)
    return jnp.log2(1.0 + jnp.exp2(neg_abs)) + jnp.maximum(z2, 0.0)


def _twice(x):
    return jnp.concatenate([x, x], axis=1)


def sb_fwd(qkv, bg_shards=(), bg_lands=()):
    T = qkv.shape[0]
    tq, tk = SB_QROWS, SB_BLOCK
    ratio = tq // tk
    npair = SB_HEADS // 2
    nbg = len(bg_shards)
    nq = T // tq

    def body(q_ref, k_ref, v_ref, *rest):
        bg_in = rest[:nbg]
        o_ref, tot_ref, first_ref = rest[2 * nbg:2 * nbg + 3]
        bg_out = rest[2 * nbg + 3:3 * nbg + 3]
        acc_s, c_s, z_s, w_s, kmax_s = rest[3 * nbg + 3:3 * nbg + 8]
        p = pl.program_id(0)
        i = pl.program_id(1)
        if nbg:
            gather = GatherOps([s.shape[1] for s in bg_shards], bg_in, bg_out, *rest[3 * nbg + 8:])

            @pl.when((p == 0) & (i == 0))
            def _():
                gather.start_ici()

        @pl.when(i == 0)
        def _():
            kmax_s[...] = jnp.max(jnp.abs(k_ref[...]), axis=0, keepdims=True).astype(F32)

        q = q_ref[...]
        lane = lax.broadcasted_iota(jnp.int32, (tq, LANES), 1)
        first = lane < HEAD_DIM
        zero = jnp.zeros_like(q)
        q_heads = (jnp.where(first, q, zero), jnp.where(first, zero, q))
        row = lax.broadcasted_iota(jnp.int32, (tq, tk), 0)
        col = lax.broadcasted_iota(jnp.int32, (tq, tk), 1)
        visible = [col + r * tk < row for r in range(ratio)]
        krow = lax.broadcasted_iota(jnp.int32, (tk, tk), 0)
        kcol = lax.broadcasted_iota(jnp.int32, (tk, tk), 1)
        from_s = (krow >= kcol).astype(BF16)
        acc_s[...] = jnp.zeros_like(acc_s)
        c_s[...] = jnp.zeros_like(c_s)

        def rows(j):
            return pl.ds(pl.multiple_of(j * tk, tk), tk)

        def logits(j):
            kb = k_ref[rows(j), :]
            for hd in range(2):
                z_s[hd] = _dot_nt(q_heads[hd], kb) * LOG2E

        def flush(j):
            vb = v_ref[rows(j), :]
            for hd in range(2):
                acc_s[hd] += _dot(w_s[hd], vb)

        def block(j, mask=None, walked_before=True):
            if walked_before:
                flush(j + 1)
            chunks = [(hd, slice(r0, r0 + SB_CHUNK)) for hd in range(2) for r0 in range(0, tq, SB_CHUNK)]
            k_next = k_ref[rows(jnp.maximum(j - 1, 0)), :]
            es, sums = [], []
            for hd, rs in chunks:
                z2 = z_s[hd, rs, :]
                z_s[hd, rs, :] = _dot_nt(q_heads[hd][rs, :], k_next) * LOG2E
                if mask is not None:
                    z2 = jnp.where(mask[rs, :], z2, NEG_BIG)
                sp = _softplus2(z2)
                c = c_s[hd, rs, :]
                es.append(z2 + _twice(c))
                c_s[hd, rs, :] = c - jnp.sum(sp, axis=1, keepdims=True)
                sums.append(_dot(sp.astype(BF16), from_s))
            for (hd, rs), e, s in zip(chunks, es, sums):
                w_s[hd, rs, :] = jnp.exp2(e - s).astype(BF16)

        z_bound = [LOG2E * jnp.sum(jnp.abs(q_heads[hd].astype(F32)) * kmax_s[...], axis=1, keepdims=True)
                   for hd in range(2)]

        def more_keys_matter():
            top = jnp.maximum(c_s[0] + z_bound[0], c_s[1] + z_bound[1])
            return (jnp.max(top) >= -SB_UNDERFLOW_BITS).astype(jnp.int32)

        logits(ratio * i + ratio - 1)
        for r in reversed(range(ratio)):
            block(ratio * i + r, visible[r], walked_before=(r != ratio - 1))

        def trip(carry):
            trips, _ = carry
            for r in reversed(range(ratio)):
                block(ratio * (i - 1 - trips) + r)
            return trips + 1, more_keys_matter()

        trips, _ = lax.while_loop(lambda carry: jnp.logical_and(carry[0] < i, carry[1] > 0), trip,
                                  (jnp.int32(0), jnp.int32(1)))
        first_walked = ratio * (i - trips)
        flush(first_walked)
        first_ref[p, i] = first_walked.astype(F32)
        o_ref[...] = jnp.where(first, acc_s[0], acc_s[1]).astype(BF16)
        tot_ref[...] = jnp.where(first, c_s[0], c_s[1])
        if nbg:
            @pl.when((p == npair - 1) & (i == nq - 1))
            def _():
                gather.wait_ici()

    res = pl.pallas_call(
        body,
        name="sb_fwd",
        grid=(npair, nq),
        in_specs=[
            pl.BlockSpec((tq, LANES), lambda p, i: (i, p)),
            pl.BlockSpec((T, LANES), lambda p, i: (0, npair + p)),
            pl.BlockSpec((T, LANES), lambda p, i: (0, 2 * npair + p)),
        ] + [ANY] * (2 * nbg),
        out_specs=[pl.BlockSpec((tq, LANES), lambda p, i: (i, p))] * 2 + [pl.BlockSpec(memory_space=pltpu.SMEM)]
        + [ANY] * nbg,
        out_shape=[jax.ShapeDtypeStruct((T, D_MODEL), BF16), jax.ShapeDtypeStruct((T, D_MODEL), F32),
                   jax.ShapeDtypeStruct((npair, nq), F32)]
        + [jax.ShapeDtypeStruct(l.shape, l.dtype) for l in bg_lands],
        input_output_aliases={3 + nbg + t: 3 + t for t in range(nbg)},
        scratch_shapes=[
            pltpu.VMEM((2, tq, LANES), F32), pltpu.VMEM((2, tq, LANES), F32),
            pltpu.VMEM((2, tq, tk), F32), pltpu.VMEM((2, tq, tk), BF16),
            pltpu.VMEM((1, LANES), F32),
        ] + [pltpu.SemaphoreType.DMA((nbg, N_PEER_CHIPS))] * (2 if nbg else 0),
        compiler_params=_cparams(),
    )(qkv, qkv, qkv, *bg_shards, *bg_lands)
    return res[0], res[1], res[2], list(res[3:])


def sb_bwd(qkv, do, tot, first_block):
    T = qkv.shape[0]
    tq, tk = SB_QROWS_BWD, SB_BLOCK
    ratio = tq // tk
    npair = SB_HEADS // 2
    nq = T // tq
    nk = T // tk
    assert SB_QROWS == SB_QROWS_BWD

    def body(first_ref, q_ref, k_ref, v_ref, do_ref, tot_ref, dq_ref, dk_ref, dv_ref,
             dkt_s, dvt_s, dq_s, rest_s, cg_s, z_s, da_s, dz_s, a_s):
        i = pl.program_id(1)
        start = jnp.clip(first_ref[pl.program_id(0), i].astype(jnp.int32), 0, ratio * i)

        @pl.when(i == 0)
        def _():
            dkt_s[...] = jnp.zeros_like(dkt_s)
            dvt_s[...] = jnp.zeros_like(dvt_s)

        q = q_ref[...]
        do_ = do_ref[...]
        tot_ = tot_ref[...]
        q_t = q.astype(F32).T.astype(BF16)
        do_t = do_.astype(F32).T.astype(BF16)
        lane = lax.broadcasted_iota(jnp.int32, (tq, LANES), 1)
        first = lane < HEAD_DIM
        zero = jnp.zeros_like(q)
        q_heads = (jnp.where(first, q, zero), jnp.where(first, zero, q))
        do_heads = (jnp.where(first, do_, zero), jnp.where(first, zero, do_))
        row = lax.broadcasted_iota(jnp.int32, (tq, tk), 0)
        col = lax.broadcasted_iota(jnp.int32, (tq, tk), 1)
        visible = [col + r * tk < row for r in range(ratio)]
        krow = lax.broadcasted_iota(jnp.int32, (tk, tk), 0)
        kcol = lax.broadcasted_iota(jnp.int32, (tk, tk), 1)
        before = (krow < kcol).astype(BF16)
        from_s = (krow >= kcol).astype(BF16)
        last = ratio * i + ratio - 1
        rest_s[0] = jnp.broadcast_to(tot_[:, 0:1], (tq, LANES))
        rest_s[1] = jnp.broadcast_to(tot_[:, HEAD_DIM:HEAD_DIM + 1], (tq, LANES))
        cg_s[...] = jnp.zeros_like(cg_s)
        dq_s[...] = jnp.zeros_like(dq_s)
        dz_s[...] = jnp.zeros_like(dz_s)
        a_s[...] = jnp.zeros_like(a_s)

        def rows(j):
            return pl.ds(pl.multiple_of(j * tk, tk), tk)

        def logits(j):
            kb = k_ref[rows(j), :]
            vb = v_ref[rows(j), :]
            for hd in range(2):
                z_s[hd] = _dot_nt(q_heads[hd], kb) * LOG2E
                da_s[hd] = _dot_nt(do_heads[hd], vb)

        def flush(j):
            kb = k_ref[rows(j), :]
            for hd in range(2):
                dims = slice(hd * HEAD_DIM, (hd + 1) * HEAD_DIM)
                dq_s[hd] += _dot(dz_s[hd], kb)
                dkt_s[j, dims, :] += _dot(q_t[dims, :], dz_s[hd])
                dvt_s[j, dims, :] += _dot(do_t[dims, :], a_s[hd])

        def block(j, mask=None):
            flush(jnp.maximum(j - 1, 0))
            chunks = [(hd, slice(r0, r0 + SB_CHUNK)) for hd in range(2) for r0 in range(0, tq, SB_CHUNK)]
            nxt = rows(jnp.minimum(j + 1, last))
            k_next = k_ref[nxt, :]
            v_next = v_ref[nxt, :]
            stage1 = []
            for hd, rs in chunks:
                z2 = z_s[hd, rs, :]
                z_s[hd, rs, :] = _dot_nt(q_heads[hd][rs, :], k_next) * LOG2E
                if mask is not None:
                    z2 = jnp.where(mask[rs, :], z2, NEG_BIG)
                sp = _softplus2(z2)
                rest = rest_s[hd, rs, :] + jnp.sum(sp, axis=1, keepdims=True)
                rest_s[hd, rs, :] = rest
                stage1.append((z2 + _twice(rest), z2 - sp, _dot(sp.astype(BF16), from_s)))
            stage2 = []
            for (hd, rs), (e, log2_beta, ahead) in zip(chunks, stage1):
                a = jnp.exp2(e - ahead)
                g = a * da_s[hd, rs, :]
                da_s[hd, rs, :] = _dot_nt(do_heads[hd][rs, :], v_next)
                cg = cg_s[hd, rs, :]
                a_s[hd, rs, :] = a.astype(BF16)
                cg_s[hd, rs, :] = cg + jnp.sum(g, axis=1, keepdims=True)
                stage2.append((g, g + _twice(cg), log2_beta, _dot(g.astype(BF16), before)))
            for (hd, rs), (g, g_from, log2_beta, g_before) in zip(chunks, stage2):
                dz_s[hd, rs, :] = (g - jnp.exp2(log2_beta) * (g_from + g_before)).astype(BF16)

        logits(start)

        @pl.loop(start, ratio * i)
        def _(j):
            block(j)

        for r in range(ratio):
            block(ratio * i + r, visible[r])
        flush(last)
        dq_ref[...] = (jnp.where(first, dq_s[0], dq_s[1]) * ATTN_SCALE).astype(BF16)

        @pl.when(i == nq - 1)
        def _():
            @pl.loop(0, nk)
            def _(b):
                dk_ref[rows(b), :] = dkt_s[b].T.astype(BF16)
                dv_ref[rows(b), :] = dvt_s[b].T.astype(BF16)

    qblk = pl.BlockSpec((tq, LANES), lambda p, i: (i, p))
    full = pl.BlockSpec((T, LANES), lambda p, i: (0, p))
    return pl.pallas_call(
        body,
        name="sb_bwd",
        grid=(npair, nq),
        in_specs=[
            pl.BlockSpec(memory_space=pltpu.SMEM),
            qblk,
            pl.BlockSpec((T, LANES), lambda p, i: (0, npair + p)),
            pl.BlockSpec((T, LANES), lambda p, i: (0, 2 * npair + p)),
            qblk, qblk,
        ],
        out_specs=[qblk, full, full],
        out_shape=[jax.ShapeDtypeStruct((T, D_MODEL), BF16)] * 3,
        scratch_shapes=[
            pltpu.VMEM((nk, LANES, tk), F32), pltpu.VMEM((nk, LANES, tk), F32),
            pltpu.VMEM((2, tq, LANES), F32), pltpu.VMEM((2, tq, LANES), F32), pltpu.VMEM((2, tq, LANES), F32),
            pltpu.VMEM((2, tq, tk), F32), pltpu.VMEM((2, tq, tk), F32),
            pltpu.VMEM((2, tq, tk), BF16), pltpu.VMEM((2, tq, tk), BF16),
        ],
        compiler_params=_cparams(),
    )(first_block, qkv, qkv, qkv, do, tot)


def _swa_valid(n):
    qi = lax.broadcasted_iota(jnp.int32, (WINDOW, 2 * WINDOW), 0)
    ki = lax.broadcasted_iota(jnp.int32, (WINDOW, 2 * WINDOW), 1)
    diff = qi + WINDOW - ki
    return (diff >= 0) & (diff < WINDOW) & ((n > 0) | (ki >= WINDOW))


def _to_half(x, first, src, dst):
    keep = first if src == 0 else jnp.logical_not(first)
    x = jnp.where(keep, x, jnp.zeros_like(x))
    if src != dst:
        x = pltpu.roll(x.astype(F32), HEAD_DIM, 1).astype(BF16)
    return x


SWA_GROUP = SWA_Q_HEADS // SWA_KV_HEADS


def _swa_cols(h):
    return slice((h // 2) * LANES, (h // 2 + 1) * LANES)


def _swa_kv_pair(h):
    return (h // SWA_GROUP) // 2


def _swa_kv_half(h):
    return (h // SWA_GROUP) % 2


def _kv_band(prev_ref, cur_ref, pb):
    cols = slice(pb * LANES, (pb + 1) * LANES)
    return jnp.concatenate([prev_ref[:, cols], cur_ref[:, cols]], axis=0)


def _swa_specs(T):
    nb = T // WINDOW
    kv_w = SWA_KV_HEADS * HEAD_DIM
    qrow = pl.BlockSpec((WINDOW, D_MODEL), lambda n: (n, 0))
    cur = pl.BlockSpec((WINDOW, kv_w), lambda n: (n, 0))
    prev = pl.BlockSpec((WINDOW, kv_w), lambda n: (jnp.maximum(n - 1, 0), 0))
    smem = pl.BlockSpec(memory_space=pltpu.SMEM)
    return nb, qrow, cur, prev, smem


def swa_fwd(q, k, v, sinks):
    T = q.shape[0]
    nb, qrow, cur, prev, smem = _swa_specs(T)

    def body(sink_ref, q_ref, kc_ref, kp_ref, vc_ref, vp_ref, o_ref, lse_ref):
        n = pl.program_id(0)
        lane = lax.broadcasted_iota(jnp.int32, (WINDOW, LANES), 1)
        first = lane < HEAD_DIM
        valid = _swa_valid(n)
        k2 = [_kv_band(kp_ref, kc_ref, pb) for pb in range(SWA_KV_HEADS // 2)]
        v2 = [_kv_band(vp_ref, vc_ref, pb) for pb in range(SWA_KV_HEADS // 2)]
        logits = [jnp.where(valid, _dot_nt(_to_half(q_ref[:, _swa_cols(h)], first, h % 2, _swa_kv_half(h)),
                                            k2[_swa_kv_pair(h)]), NEG_BIG) for h in range(SWA_Q_HEADS)]
        probs = []
        lse_acc = jnp.zeros((WINDOW, LANES), F32)
        for h, s in enumerate(logits):
            sink = sink_ref[h]
            m = jnp.maximum(jnp.max(s, axis=1, keepdims=True), sink)
            p = jnp.exp(s - m)
            den = jnp.sum(p, axis=1, keepdims=True) + jnp.exp(sink - m)
            probs.append((p / den).astype(BF16))
            lse_acc = jnp.where(lane == h, m + jnp.log(den), lse_acc)
        outs = []
        for h, p in enumerate(probs):
            o = _dot(p, v2[_swa_kv_pair(h)])
            outs.append(pltpu.roll(o, HEAD_DIM, 1) if h % 2 != _swa_kv_half(h) else o)
        for pair in range(SWA_Q_HEADS // 2):
            o_ref[:, _swa_cols(2 * pair)] = jnp.where(first, outs[2 * pair], outs[2 * pair + 1]).astype(BF16)
        lse_ref[...] = lse_acc

    return pl.pallas_call(
        body,
        name="swa_fwd",
        grid=(nb,),
        in_specs=[smem, qrow, cur, prev, cur, prev],
        out_specs=[qrow, pl.BlockSpec((WINDOW, LANES), lambda n: (n, 0))],
        out_shape=[jax.ShapeDtypeStruct((T, D_MODEL), BF16), jax.ShapeDtypeStruct((T, LANES), F32)],
        compiler_params=_cparams(),
    )(sinks, q, k, k, v, v)


def swa_bwd(q, k, v, sinks, do, o, lse, cos, sin):
    T = q.shape[0]
    nb, qrow, cur, prev, smem = _swa_specs(T)
    kv_w = SWA_KV_HEADS * HEAD_DIM

    def body(sink_ref, q_ref, kc_ref, kp_ref, vc_ref, vp_ref, do_ref, o_ref, lse_ref, cos_ref, sin_ref,
             dq_ref, own_ref, prv_ref, dsink_ref):
        n = pl.program_id(0)

        @pl.when(n == 0)
        def _():
            dsink_ref[...] = jnp.zeros_like(dsink_ref)

        lane = lax.broadcasted_iota(jnp.int32, (WINDOW, LANES), 1)
        lane1 = lax.broadcasted_iota(jnp.int32, (1, LANES), 1)
        first = lane < HEAD_DIM
        valid = _swa_valid(n)
        cos_ = cos_ref[...]
        sin_ = sin_ref[...]
        k2 = [_kv_band(kp_ref, kc_ref, pb) for pb in range(SWA_KV_HEADS // 2)]
        v2 = [_kv_band(vp_ref, vc_ref, pb) for pb in range(SWA_KV_HEADS // 2)]
        q_t = q_ref[...].astype(F32).T.astype(BF16)
        do_t = do_ref[...].astype(F32).T.astype(BF16)
        stage1 = []
        for h in range(SWA_Q_HEADS):
            a, b, pb = h % 2, _swa_kv_half(h), _swa_kv_pair(h)
            qh = _to_half(q_ref[:, _swa_cols(h)], first, a, b)
            doh = _to_half(do_ref[:, _swa_cols(h)], first, a, b)
            stage1.append((jnp.where(valid, _dot_nt(qh, k2[pb]), NEG_BIG), _dot_nt(doh, v2[pb])))
        deltas = []
        for pair in range(SWA_Q_HEADS // 2):
            prod = do_ref[:, _swa_cols(2 * pair)].astype(F32) * o_ref[:, _swa_cols(2 * pair)].astype(F32)
            deltas += [jnp.sum(jnp.where(first, prod, 0.0), axis=1, keepdims=True),
                       jnp.sum(jnp.where(first, 0.0, prod), axis=1, keepdims=True)]
        stage2 = []
        dsink = jnp.zeros((1, LANES), F32)
        for h, (s, dp) in enumerate(stage1):
            lse_h = lse_ref[:, h:h + 1]
            p = jnp.exp(s - lse_h)
            delta = deltas[h]
            p_sink = jnp.exp(sink_ref[h] - lse_h)
            dsink = dsink + jnp.where(lane1 == h, -jnp.sum(p_sink * delta, axis=0, keepdims=True), 0.0)
            stage2.append(((p * (dp - delta)).astype(BF16), p.astype(BF16)))
        dqs = []
        dk_t = [None] * SWA_KV_HEADS
        dv_t = [None] * SWA_KV_HEADS
        for h, (ds, pb16) in enumerate(stage2):
            kvh = h // SWA_GROUP
            dims = slice(h * HEAD_DIM, (h + 1) * HEAD_DIM)
            dq = _dot(ds, k2[_swa_kv_pair(h)])
            dqs.append(pltpu.roll(dq, HEAD_DIM, 1) if h % 2 != _swa_kv_half(h) else dq)
            dk_h = _dot(q_t[dims, :], ds)
            dv_h = _dot(do_t[dims, :], pb16)
            dk_t[kvh] = dk_h if dk_t[kvh] is None else dk_t[kvh] + dk_h
            dv_t[kvh] = dv_h if dv_t[kvh] is None else dv_t[kvh] + dv_h
        for pair in range(SWA_Q_HEADS // 2):
            dqp = jnp.where(first, dqs[2 * pair], dqs[2 * pair + 1])
            dq_ref[:, _swa_cols(2 * pair)] = ((dqp * cos_ + _swap32(dqp * sin_)) * ATTN_SCALE).astype(BF16)
        for pb in range(SWA_KV_HEADS // 2):
            dk2 = jnp.concatenate([dk_t[2 * pb], dk_t[2 * pb + 1]], axis=0).T
            dv2 = jnp.concatenate([dv_t[2 * pb], dv_t[2 * pb + 1]], axis=0).T
            kcols = slice(pb * LANES, (pb + 1) * LANES)
            vcols = slice(kv_w + pb * LANES, kv_w + (pb + 1) * LANES)
            prv_ref[:, kcols] = dk2[:WINDOW]
            own_ref[:, kcols] = dk2[WINDOW:]
            prv_ref[:, vcols] = dv2[:WINDOW]
            own_ref[:, vcols] = dv2[WINDOW:]
        dsink_ref[...] += dsink

    tab = pl.BlockSpec((WINDOW, LANES), lambda n: (n, 0))
    kvrow = pl.BlockSpec((WINDOW, 2 * kv_w), lambda n: (n, 0))
    return pl.pallas_call(
        body,
        name="swa_bwd",
        grid=(nb,),
        in_specs=[smem, qrow, cur, prev, cur, prev, qrow, qrow, tab, tab, tab],
        out_specs=[qrow, kvrow, kvrow, pl.BlockSpec((1, LANES), lambda n: (0, 0))],
        out_shape=[
            jax.ShapeDtypeStruct((T, D_MODEL), BF16),
            jax.ShapeDtypeStruct((T, 2 * kv_w), F32),
            jax.ShapeDtypeStruct((T, 2 * kv_w), F32),
            jax.ShapeDtypeStruct((1, LANES), F32),
        ],
        compiler_params=_cparams(),
    )(sinks, q, k, k, v, v, do, o, lse, cos, sin)


def kv_grad_combine(own, prv, cos, sin):
    T = own.shape[0]
    nb = T // WINDOW
    kv_w = SWA_KV_HEADS * HEAD_DIM

    def body(own_ref, nxt_ref, cos_ref, sin_ref, out_ref):
        n = pl.program_id(0)
        nxt = jnp.where(n + 1 < nb, nxt_ref[...], 0.0)
        tot = own_ref[...] + nxt
        dk = tot[:, :kv_w]
        c = _tile_lanes(cos_ref[...], kv_w)
        s = _tile_lanes(sin_ref[...], kv_w)
        out_ref[:, :kv_w] = (dk * c + _swap32(dk * s)).astype(BF16)
        out_ref[:, kv_w:] = tot[:, kv_w:].astype(BF16)

    tab = pl.BlockSpec((WINDOW, LANES), lambda n: (n, 0))
    kvrow = pl.BlockSpec((WINDOW, 2 * kv_w), lambda n: (n, 0))
    return pl.pallas_call(
        body,
        name="kv_grad_combine",
        grid=(nb,),
        in_specs=[kvrow, pl.BlockSpec((WINDOW, 2 * kv_w), lambda n: (jnp.minimum(n + 1, nb - 1), 0)), tab, tab],
        out_specs=kvrow,
        out_shape=jax.ShapeDtypeStruct((T, 2 * kv_w), BF16),
        compiler_params=_cparams(),
    )(own, prv, cos, sin)


ANY = pl.BlockSpec(memory_space=pl.ANY)


def _place():
    x, y, c = lax.axis_index("x"), lax.axis_index("y"), lax.axis_index("c")
    other_chips = [(1 - x, y), (x, 1 - y), (1 - x, 1 - y)]
    return x, y, c, 2 * x + y, other_chips


N_PEER_CHIPS = N_CHIPS - 1


class GatherOps:
    def __init__(self, rows, shards, lands, ici_send, ici_recv, d2d_send=None, d2d_recv=None):
        self.rows, self.shards, self.lands = rows, shards, lands
        self.ici_send, self.ici_recv, self.d2d_send, self.d2d_recv = ici_send, ici_recv, d2d_send, d2d_recv
        self.x, self.y, self.c, self.me, self.chips = _place()
        self.pairs = [(t, jdx) for t in range(len(rows)) for jdx in range(N_PEER_CHIPS)]

    def _half(self, ref, t, which):
        r = self.rows[t] // 2
        return ref.at[:, pl.ds(pl.multiple_of(which * r, 16), r), :]

    def _ici(self, t, jdx):
        px, py = self.chips[jdx]
        return pltpu.make_async_remote_copy(
            src_ref=self._half(self.shards[t], t, self.c), dst_ref=self._half(self.lands[t].at[self.me], t, self.c),
            send_sem=self.ici_send.at[t, jdx], recv_sem=self.ici_recv.at[t, jdx],
            device_id=(px, py, self.c), device_id_type=MESH)

    def _landed(self, t, jdx):
        px, py = self.chips[jdx]
        blk = self._half(self.lands[t].at[2 * px + py], t, self.c)
        return pltpu.make_async_remote_copy(
            src_ref=blk, dst_ref=blk, send_sem=self.ici_send.at[t, jdx], recv_sem=self.ici_recv.at[t, jdx],
            device_id=(px, py, self.c), device_id_type=MESH)

    def _d2d(self, t, jdx, which):
        px, py = self.chips[jdx]
        blk = self._half(self.lands[t].at[2 * px + py], t, which)
        return pltpu.make_async_remote_copy(
            src_ref=blk, dst_ref=blk, send_sem=self.d2d_send.at[t, jdx], recv_sem=self.d2d_recv.at[t, jdx],
            device_id=(self.x, self.y, 1 - self.c), device_id_type=MESH)

    def start_ici(self):
        for t, jdx in self.pairs:
            self._ici(t, jdx).start()

    def wait_ici(self):
        for t, jdx in self.pairs:
            self._landed(t, jdx).wait_recv()
        self.wait_ici_sends()

    def wait_ici_sends(self):
        for t, jdx in self.pairs:
            self._ici(t, jdx).wait_send()

    def forward_arrivals(self):
        for t, jdx in self.pairs:
            self._landed(t, jdx).wait_recv()
            self._d2d(t, jdx, self.c).start()

    def start_forwards(self):
        for t, jdx in self.pairs:
            self._d2d(t, jdx, self.c).start()

    def wait_forwards(self):
        for t, jdx in self.pairs:
            self._d2d(t, jdx, 1 - self.c).wait_recv()
            self._d2d(t, jdx, self.c).wait_send()


def all_gather_weights(shards, lands):
    n = len(shards)
    rows = [s.shape[1] for s in shards]

    def body(*refs):
        ins, outs = refs[:n], refs[2 * n:3 * n]
        ops = GatherOps(rows, ins, outs, *refs[3 * n:])
        ops.start_ici()
        ops.forward_arrivals()
        ops.wait_forwards()
        ops.wait_ici_sends()

    return pl.pallas_call(
        body,
        name="all_gather_weights",
        in_specs=[ANY] * (2 * n),
        out_specs=[ANY] * n,
        out_shape=[jax.ShapeDtypeStruct(l.shape, l.dtype) for l in lands],
        input_output_aliases={n + t: t for t in range(n)},
        scratch_shapes=[pltpu.SemaphoreType.DMA((n, N_PEER_CHIPS))] * 4,
    )(*shards, *lands)


def place_own_shard(name, shard, chip):
    nl, r, c = shard.shape

    def body(chip_ref, s_ref, o_ref):
        o_ref[...] = s_ref[...]

    return pl.pallas_call(
        body, name=name,
        grid_spec=pltpu.PrefetchScalarGridSpec(
            num_scalar_prefetch=1, grid=(nl,),
            in_specs=[pl.BlockSpec((None, r, c), lambda l, chip_ref: (l, 0, 0))],
            out_specs=pl.BlockSpec((None, None, r, c), lambda l, chip_ref: (chip_ref[0], l, 0, 0))),
        out_shape=jax.ShapeDtypeStruct((N_CHIPS,) + shard.shape, shard.dtype), compiler_params=_cparams(),
    )(chip, shard)


def exchange_halves(slabs):
    n = len(slabs)

    def body(*refs):
        ins, theirs = refs[:n], refs[n:2 * n]
        send_sems, recv_sems = refs[2 * n:]
        x, y, c, _, _ = _place()
        copies = []
        for t in range(n):
            cp = pltpu.make_async_remote_copy(
                src_ref=ins[t].at[1 - c], dst_ref=theirs[t], send_sem=send_sems.at[t],
                recv_sem=recv_sems.at[t], device_id=(x, y, 1 - c), device_id_type=MESH)
            cp.start()
            copies.append(cp)
        for cp in copies:
            cp.wait()

    return pl.pallas_call(
        body,
        name="exchange_halves",
        in_specs=[ANY] * n,
        out_specs=[ANY] * n,
        out_shape=[jax.ShapeDtypeStruct(s.shape[1:], s.dtype) for s in slabs],
        scratch_shapes=[pltpu.SemaphoreType.DMA((n,)), pltpu.SemaphoreType.DMA((n,))],
    )(*slabs)


def exchange_chip_partials(parts):
    n = len(parts)

    def body(*refs):
        ins, outs = refs[:n], refs[n:2 * n]
        send_sems, recv_sems = refs[2 * n:]
        _, _, c, me, chips = _place()
        copies = []
        for t in range(n):
            for jdx, (px, py) in enumerate(chips):
                cp = pltpu.make_async_remote_copy(
                    src_ref=ins[t].at[2 * px + py], dst_ref=outs[t].at[me], send_sem=send_sems.at[t, jdx],
                    recv_sem=recv_sems.at[t, jdx], device_id=(px, py, c), device_id_type=MESH)
                cp.start()
                copies.append(cp)
        for cp in copies:
            cp.wait()

    return pl.pallas_call(
        body,
        name="exchange_chip_partials",
        in_specs=[ANY] * n,
        out_specs=[ANY] * n,
        out_shape=[jax.ShapeDtypeStruct(p.shape, p.dtype) for p in parts],
        scratch_shapes=[pltpu.SemaphoreType.DMA((n, 3)), pltpu.SemaphoreType.DMA((n, 3))],
    )(*parts)


def share_reduced_halves(halves):
    n = len(halves)

    def body(*refs):
        ins, outs = refs[:n], refs[n:2 * n]
        send_sems, recv_sems = refs[2 * n:]
        x, y, c, _, _ = _place()
        copies = []
        for t in range(n):
            cp = pltpu.make_async_remote_copy(
                src_ref=ins[t], dst_ref=outs[t], send_sem=send_sems.at[t],
                recv_sem=recv_sems.at[t], device_id=(x, y, 1 - c), device_id_type=MESH)
            cp.start()
            copies.append(cp)
        for cp in copies:
            cp.wait()

    return pl.pallas_call(
        body,
        name="share_reduced_halves",
        in_specs=[ANY] * n,
        out_specs=[ANY] * n,
        out_shape=[jax.ShapeDtypeStruct(h.shape, h.dtype) for h in halves],
        scratch_shapes=[pltpu.SemaphoreType.DMA((n,)), pltpu.SemaphoreType.DMA((n,))],
    )(*halves)


def _row_tile(r, c):
    tr = r
    while tr * c * 4 > (3 << 19) and tr % 16 == 0:
        tr //= 2
    return tr


def add_sibling(name, slab, theirs, core):
    _, ns, slots, r, c = slab.shape
    tr = _row_tile(r, c)

    def body(core_ref, a_ref, b_ref, o_ref):
        o_ref[...] = (a_ref[...] + b_ref[...]).astype(BF16)

    blk = pl.BlockSpec((None, None, tr, c), lambda s, l, i, core_ref: (s, l, i, 0))
    return pl.pallas_call(
        body, name=name,
        grid_spec=pltpu.PrefetchScalarGridSpec(
            num_scalar_prefetch=1, grid=(ns, slots, r // tr),
            in_specs=[pl.BlockSpec((None, None, None, tr, c), lambda s, l, i, core_ref: (core_ref[0], s, l, i, 0)), blk],
            out_specs=blk),
        out_shape=jax.ShapeDtypeStruct(theirs.shape, BF16), compiler_params=_cparams(),
    )(core, slab, theirs)


def sum_chips(name, recv, own, chip):
    _, slots, r, c = recv.shape
    tr = _row_tile(r, c)

    def body(chip_ref, r0, r1, r2, r3, own_ref, o_ref):
        me = chip_ref[0]
        mine = own_ref[...]
        terms = [jnp.where(me == s, mine, rr[...]).astype(F32) for s, rr in enumerate((r0, r1, r2, r3))]
        o_ref[...] = ((terms[0] + terms[1]) + terms[2]) + terms[3]

    def src(s):
        return pl.BlockSpec((None, None, tr, c),
                            lambda l, i, chip_ref: (jnp.where(chip_ref[0] == s, (s + 1) % N_CHIPS, s), l, i, 0))

    return pl.pallas_call(
        body, name=name,
        grid_spec=pltpu.PrefetchScalarGridSpec(
            num_scalar_prefetch=1, grid=(slots, r // tr),
            in_specs=[src(0), src(1), src(2), src(3),
                      pl.BlockSpec((None, None, tr, c), lambda l, i, chip_ref: (chip_ref[0], l, i, 0))],
            out_specs=pl.BlockSpec((None, tr, c), lambda l, i, chip_ref: (l, i, 0))),
        out_shape=jax.ShapeDtypeStruct((slots, r, c), F32), compiler_params=_cparams(),
    )(chip, recv, recv, recv, recv, own)


def _adamw_math(w, g, m, v):
    m = ADAM_B1 * m + (1.0 - ADAM_B1) * g
    v = ADAM_B2 * v + (1.0 - ADAM_B2) * (g * g)
    m_hat = m / (1.0 - ADAM_B1 ** ADAM_STEP)
    v_hat = v / (1.0 - ADAM_B2 ** ADAM_STEP)
    delta = -ADAM_LR * (m_hat / (jnp.sqrt(v_hat) + ADAM_EPS) + ADAM_WD * w)
    return delta, m, v


def adamw_shard(name, w, m, v, g_own, g_sib, core, slot0, row_halves):
    n = w.shape[0]
    _, r, c = g_own.shape
    tr = _row_tile(r, c)
    nr = r // tr

    def body(core_ref, w_ref, m_ref, v_ref, own_ref, sib_ref, go_ref, d_ref, mo_ref, vo_ref):
        g = jnp.where(pl.program_id(1) == core_ref[0], own_ref[...], sib_ref[...])
        delta, mm, vv = _adamw_math(w_ref[...], g, m_ref[...], v_ref[...])
        go_ref[...] = g
        d_ref[...] = delta
        mo_ref[...] = mm
        vo_ref[...] = vv

    if row_halves:
        wspec = pl.BlockSpec((None, tr, c), lambda l, h, i, core_ref: (l, h * nr + i, 0))
    else:
        wspec = pl.BlockSpec((None, tr, c), lambda l, h, i, core_ref: (l, i, h))
    gspec = pl.BlockSpec((None, tr, c), lambda l, h, i, core_ref: (slot0 + l, i, 0))
    shp = jax.ShapeDtypeStruct(w.shape, F32)
    return pl.pallas_call(
        body, name=name,
        grid_spec=pltpu.PrefetchScalarGridSpec(
            num_scalar_prefetch=1, grid=(n, 2, nr),
            in_specs=[wspec, wspec, wspec, gspec, gspec], out_specs=[wspec] * 4),
        out_shape=[shp] * 4, compiler_params=_cparams(),
    )(core, w, m, v, g_own, g_sib)


SMALL_ROWS = 16


def small_allreduce_adamw(part, w, m, v):
    def body(p_ref, w_ref, m_ref, v_ref, g_ref, d_ref, mo_ref, vo_ref, buf, send_sems, recv_sems):
        x, y, c, _, _ = _place()
        me = 4 * x + 2 * y + c
        buf[me] = p_ref[...]
        copies = []
        for k in range(1, N_DEV):
            kx, ky, kc = (k >> 2) & 1, (k >> 1) & 1, k & 1
            peer = (x ^ kx, y ^ ky, c ^ kc)
            cp = pltpu.make_async_remote_copy(
                src_ref=p_ref, dst_ref=buf.at[me], send_sem=send_sems.at[k - 1],
                recv_sem=recv_sems.at[k - 1], device_id=peer, device_id_type=MESH)
            cp.start()
            copies.append(cp)
        for cp in copies:
            cp.wait()
        g = buf[0]
        for dev in range(1, N_DEV):
            g = g + buf[dev]
        delta, mm, vv = _adamw_math(w_ref[...], g, m_ref[...], v_ref[...])
        g_ref[...] = g
        d_ref[...] = delta
        mo_ref[...] = mm
        vo_ref[...] = vv

    vm = pl.BlockSpec(memory_space=pltpu.VMEM)
    shp = jax.ShapeDtypeStruct(part.shape, F32)
    return pl.pallas_call(
        body, name="small_allreduce_adamw",
        in_specs=[vm] * 4, out_specs=[vm] * 4, out_shape=[shp] * 4,
        scratch_shapes=[
            pltpu.VMEM((N_DEV,) + part.shape, F32),
            pltpu.SemaphoreType.DMA((N_DEV - 1,)), pltpu.SemaphoreType.DMA((N_DEV - 1,)),
        ],
    )(part, w, m, v)


def _rope_tables(T):
    half = HEAD_DIM // 2
    inv_freq = ROPE_THETA ** (-jnp.arange(half, dtype=F32) / half)
    ang = jnp.arange(T).astype(F32)[:, None] * inv_freq[None, :]
    cos = jnp.tile(jnp.cos(ang), (1, LANES // half))
    sin = jnp.tile(jnp.sin(ang), (1, LANES // half))
    lane = jnp.arange(LANES)
    sign = jnp.where((lane % HEAD_DIM) < half, -1.0, 1.0).astype(F32)
    return cos, sin * sign[None, :]


def _pack_small(ffn1, mix, ffn2, kvn, fin, sinks, loss_row):
    sink_row = jnp.pad(sinks.reshape(1, SWA_Q_HEADS), ((0, 0), (0, D_MODEL - SWA_Q_HEADS)))
    rows = jnp.concatenate([ffn1, mix, ffn2, kvn.reshape(1, -1), fin.reshape(1, -1), sink_row, loss_row], axis=0)
    return jnp.concatenate([rows, jnp.zeros((SMALL_ROWS - rows.shape[0], D_MODEL), F32)], axis=0)


def kernel(x, ffn1_norm, ffn1_w_in, ffn1_w_out, mix_norm, ffn2_norm, ffn2_w_in, ffn2_w_out, sb_w_qkv, sb_w_o, kv_norm, kv_w, swa_w_q, swa_sinks, swa_w_o, final_norm, loss_target, m_ffn1_norm, m_ffn1_w_in, m_ffn1_w_out, m_mix_norm, m_ffn2_norm, m_ffn2_w_in, m_ffn2_w_out, m_sb_w_qkv, m_sb_w_o, m_kv_norm, m_kv_w, m_swa_w_q, m_swa_sinks, m_swa_w_o, m_final_norm, v_ffn1_norm, v_ffn1_w_in, v_ffn1_w_out, v_mix_norm, v_ffn2_norm, v_ffn2_w_in, v_ffn2_w_out, v_sb_w_qkv, v_sb_w_o, v_kv_norm, v_kv_w, v_swa_w_q, v_swa_sinks, v_swa_w_o, v_final_norm):
    T = x.shape[1]
    kv_cols = SWA_KV_HEADS * HEAD_DIM
    x2 = x.reshape(T, D_MODEL)
    tgt = loss_target.reshape(T, D_MODEL)
    cos, sin = _rope_tables(T)

    w_in_l = jnp.concatenate([ffn1_w_in, ffn2_w_in], axis=0).astype(BF16)
    w_out_l = jnp.concatenate([ffn1_w_out, ffn2_w_out], axis=0).astype(BF16)
    sq_l = jnp.concatenate([sb_w_o, swa_w_q, swa_w_o], axis=0).astype(BF16)
    qkv_l = sb_w_qkv[0].astype(BF16)
    kvw_l = kv_w.astype(BF16)
    core = lax.axis_index("c").astype(jnp.int32).reshape(1)
    chip = (2 * lax.axis_index("x") + lax.axis_index("y")).astype(jnp.int32).reshape(1)
    early = [w_in_l[:1], w_out_l[:1], sq_l, qkv_l[None]]
    late = [w_in_l[1:], w_out_l[1:], kvw_l[None]]
    early_lands = [place_own_shard(f"own_early_{t}", s, chip) for t, s in enumerate(early)]
    late_lands = [place_own_shard(f"own_late_{t}", s, chip) for t, s in enumerate(late)]
    w_in0, w_out0, w_sq, w_qkv = all_gather_weights(early, early_lands)
    w_qkv = w_qkv.reshape(N_CHIPS, D_MODEL, QKV_COLS)

    def ffn_w(slot):
        return (w_in0, w_out0, 0) if slot == 0 else (w_in_r, w_out_r, slot - 1)

    def vec(a, i):
        return a[i].reshape(1, D_MODEL)

    ident = lambda w: w
    sq_prep = lambda w: w.reshape(D_MODEL, w.shape[-1])
    qscale = jnp.concatenate([jnp.full((1, D_MODEL), ATTN_SCALE, F32), jnp.ones((1, 2 * D_MODEL), F32)], axis=1)
    swa_scale = jnp.full((1, D_MODEL), ATTN_SCALE, F32)
    sinks = swa_sinks.reshape(SWA_Q_HEADS)

    h1, gate1, up1 = ffn_fwd("l0a", x2, vec(ffn1_norm, 0), *ffn_w(SLOT_FFN1[0]))
    qkv = qkv_fwd(h1, vec(mix_norm, 0), w_qkv, qscale)
    o_sb, tot, sb_first, late_lands = sb_fwd(qkv, late, late_lands)
    h2, (w_in_r, w_out_r, w_kv) = linear_res("sb_out", o_sb, w_sq, SQ_SB_O, h1, late_lands)
    w_kv = w_kv.reshape(D_MODEL, 2 * kv_cols)
    h3, gate2, up2 = ffn_fwd("l0b", h2, vec(ffn2_norm, 0), *ffn_w(SLOT_FFN2[0]))
    kvn = kv_norm.reshape(1, D_MODEL)
    k_sw = rms_linear("kv_k", h3, kvn, w_kv, pl.BlockSpec((D_MODEL, kv_cols), lambda i, j: (0, 0)), ident,
                      kv_cols, kv_cols, rope=(cos, sin))
    v_sw = rms_linear("kv_v", h3, kvn, w_kv, pl.BlockSpec((D_MODEL, kv_cols), lambda i, j: (0, 1)), ident,
                      kv_cols, kv_cols)
    h4, gate3, up3 = ffn_fwd("l1a", h3, vec(ffn1_norm, 1), *ffn_w(SLOT_FFN1[1]))
    q_sw = rms_linear("swa_q", h4, vec(mix_norm, 1), w_sq,
                      pl.BlockSpec((N_CHIPS, None, SQ_ROWS, 512), lambda i, j: (0, SQ_SWA_Q, 0, j)), sq_prep,
                      D_MODEL, 512, rope=(cos, sin), scale=swa_scale)
    o_sw, lse = swa_fwd(q_sw, k_sw, v_sw, sinks)
    h5 = linear_res("swa_out", o_sw, w_sq, SQ_SWA_O, h4)
    h6, gate4, up4 = ffn_fwd("l1b", h5, vec(ffn2_norm, 1), *ffn_w(SLOT_FFN2[1]))
    dh6, loss_p, d_final = loss_bwd(h6, final_norm.reshape(1, D_MODEL), tgt)

    slab = {"in": None, "out": None, "sq": None}
    in_shape = (2, N_CHIPS, 4, D_MODEL // 2, FF_CHUNK)
    out_shape = (2, N_CHIPS, 4, FF_ROWS, D_MODEL // 2)
    sq_shape = (2, N_CHIPS, 3, SQ_ROWS, D_MODEL // 2)

    def ffn_grads(tag, dh, h_in, g, gate, up, slot):
        dh_in, xn, dg_, du_, act, dhb, dnorm = ffn_bwd(tag, dh, h_in, g, gate, up, *ffn_w(slot))
        blk = (None, 1, None, D_MODEL // 2, FF_CHUNK)
        slab["in"] = mm_tn(f"dw_gate_{tag}", xn, dg_, D_MODEL // 2, FF_CHUNK, blk,
                           lambda k, n: (k, n, slot, 0, 0), in_shape, prev=slab["in"])
        slab["in"] = mm_tn(f"dw_up_{tag}", xn, du_, D_MODEL // 2, FF_CHUNK, blk,
                           lambda k, n: (k, 2 + n, slot, 0, 0), in_shape, prev=slab["in"])
        slab["out"] = mm_tn(f"dw_out_{tag}", act, dhb, FF_CHUNK, D_MODEL // 2,
                            (None, 2, None, FF_ROWS, D_MODEL // 2),
                            lambda k, n: (n, k, slot, 0, 0), out_shape, prev=slab["out"])
        return dh_in, dnorm

    def sq_grad(tag, a, dyb, t):
        slab["sq"] = mm_tn(f"dw_sq_{tag}", a, dyb, D_MODEL, D_MODEL // 2,
                           (None, N_CHIPS, None, SQ_ROWS, D_MODEL // 2),
                           lambda k, n: (n, 0, t, 0, 0), sq_shape, prev=slab["sq"])

    dh5, d_ffn2_1 = ffn_grads("l1b", dh6, h5, vec(ffn2_norm, 1), gate4, up4, SLOT_FFN2[1])
    do_sw, dh5b = linear_bwd_plain("swa_out_bwd", dh5, w_sq, SQ_SWA_O)
    sq_grad("swa_o", o_sw, dh5b, SQ_SWA_O)
    dq_sw, kv_own, kv_prev, d_sinks = swa_bwd(q_sw, k_sw, v_sw, sinks, do_sw, o_sw, lse, cos, sin)
    sq_w_spec = pl.BlockSpec((N_CHIPS, None, SQ_ROWS, D_MODEL), lambda i, j: (0, SQ_SWA_Q, 0, 0))
    dh4, hn4, d_mix_1 = linear_bwd_rms("swa_q_bwd", [(dq_sw, w_sq, sq_w_spec, sq_prep)], h4, vec(mix_norm, 1), dh5,
                                       1, D_MODEL)
    sq_grad("swa_q", hn4, dq_sw, SQ_SWA_Q)
    dh3a, d_ffn1_1 = ffn_grads("l1a", dh4, h3, vec(ffn1_norm, 1), gate3, up3, SLOT_FFN1[1])
    dkv = kv_grad_combine(kv_own, kv_prev, cos, sin)
    kv_w_spec = pl.BlockSpec((D_MODEL, 2 * kv_cols), lambda i, j: (0, 0))
    dh3, xn3, d_kvn = linear_bwd_rms("kv_bwd", [(dkv, w_kv, kv_w_spec, ident)], h3, kvn, dh3a, 1, 2 * kv_cols)
    slab_kv = mm_tn("dw_kv", xn3, dkv, D_MODEL, kv_cols, (None, N_CHIPS, None, SQ_ROWS, kv_cols),
                    lambda k, n: (n, 0, 0, 0, 0), (2, N_CHIPS, 1, SQ_ROWS, kv_cols))
    dh2, d_ffn2_0 = ffn_grads("l0b", dh3, h2, vec(ffn2_norm, 0), gate2, up2, SLOT_FFN2[0])
    do_sb, dh2b = linear_bwd_plain("sb_out_bwd", dh2, w_sq, SQ_SB_O)
    sq_grad("sb_o", o_sb, dh2b, SQ_SB_O)
    dq_sb, dk_sb, dv_sb = sb_bwd(qkv, do_sb, tot, sb_first)
    dqkv = jnp.concatenate([dq_sb, dk_sb, dv_sb], axis=1)
    dh1, hn1, d_mix_0 = qkv_bwd(dqkv, w_qkv, h1, vec(mix_norm, 0), dh2)
    slab_qkv = mm_tn("dw_qkv", hn1, dqkv, D_MODEL // 2, QKV_COLS, (None, 1, None, D_MODEL // 2, QKV_COLS),
                     lambda k, n: (k, n, 0, 0, 0), (2, N_CHIPS, 1, D_MODEL // 2, QKV_COLS))
    dx, d_ffn1_0 = ffn_grads("l0a", dh1, x2, vec(ffn1_norm, 0), gate1, up1, SLOT_FFN1[0])

    slabs = [slab["in"], slab["out"], slab["sq"], slab_qkv, slab_kv]
    names = ["in", "out", "sq", "qkv", "kv"]
    theirs = exchange_halves(slabs)
    parts = [add_sibling(f"add_sibling_{nm}", s, t, core) for nm, s, t in zip(names, slabs, theirs)]
    gathered = exchange_chip_partials(parts)
    halves = [sum_chips(f"sum_chips_{nm}", g, p, chip) for nm, g, p in zip(names, gathered, parts)]
    sib_halves = share_reduced_halves(halves)
    g_in, g_out, g_sq, g_qkv, g_kv = zip(halves, sib_halves)

    def upd(name, w, m, v, g_pair, slot0, row_halves):
        shp = w.shape
        w3 = w.reshape((-1,) + shp[-2:])
        outs = adamw_shard(name, w3, m.reshape(w3.shape), v.reshape(w3.shape), g_pair[0], g_pair[1], core,
                           slot0, row_halves)
        return [o.reshape(shp) for o in outs]

    r_ffn1_in = upd("adamw_ffn1_in", ffn1_w_in, m_ffn1_w_in, v_ffn1_w_in, g_in, 0, True)
    r_ffn2_in = upd("adamw_ffn2_in", ffn2_w_in, m_ffn2_w_in, v_ffn2_w_in, g_in, 2, True)
    r_ffn1_out = upd("adamw_ffn1_out", ffn1_w_out, m_ffn1_w_out, v_ffn1_w_out, g_out, 0, False)
    r_ffn2_out = upd("adamw_ffn2_out", ffn2_w_out, m_ffn2_w_out, v_ffn2_w_out, g_out, 2, False)
    r_qkv = upd("adamw_qkv", sb_w_qkv, m_sb_w_qkv, v_sb_w_qkv, g_qkv, 0, True)
    r_sb_o = upd("adamw_sb_o", sb_w_o, m_sb_w_o, v_sb_w_o, g_sq, SQ_SB_O, False)
    r_swa_q = upd("adamw_swa_q", swa_w_q, m_swa_w_q, v_swa_w_q, g_sq, SQ_SWA_Q, False)
    r_swa_o = upd("adamw_swa_o", swa_w_o, m_swa_w_o, v_swa_w_o, g_sq, SQ_SWA_O, False)
    r_kv = upd("adamw_kv", kv_w, m_kv_w, v_kv_w, g_kv, 0, False)

    loss_row = jnp.pad(loss_p, ((0, 0), (0, D_MODEL - LANES)))
    d_sink_row = d_sinks[0, :SWA_Q_HEADS]
    part = _pack_small(jnp.concatenate([d_ffn1_0, d_ffn1_1], axis=0), jnp.concatenate([d_mix_0, d_mix_1], axis=0),
                       jnp.concatenate([d_ffn2_0, d_ffn2_1], axis=0), d_kvn, d_final, d_sink_row, loss_row)
    zrow = jnp.zeros((1, D_MODEL), F32)
    small = small_allreduce_adamw(
        part,
        _pack_small(ffn1_norm, mix_norm, ffn2_norm, kv_norm, final_norm, swa_sinks, zrow),
        _pack_small(m_ffn1_norm, m_mix_norm, m_ffn2_norm, m_kv_norm, m_final_norm, m_swa_sinks, zrow),
        _pack_small(v_ffn1_norm, v_mix_norm, v_ffn2_norm, v_kv_norm, v_final_norm, v_swa_sinks, zrow))

    def unpack(p):
        return dict(ffn1_norm=p[0:2], mix_norm=p[2:4], ffn2_norm=p[4:6], kv_norm=p[6], final_norm=p[7],
                    swa_sinks=p[8:9, :SWA_Q_HEADS])

    big = dict(ffn1_w_in=r_ffn1_in, ffn1_w_out=r_ffn1_out, ffn2_w_in=r_ffn2_in, ffn2_w_out=r_ffn2_out,
               sb_w_qkv=r_qkv, sb_w_o=r_sb_o, kv_w=r_kv, swa_w_q=r_swa_q, swa_w_o=r_swa_o)
    order = ["ffn1_norm", "ffn1_w_in", "ffn1_w_out", "mix_norm", "ffn2_norm", "ffn2_w_in", "ffn2_w_out",
             "sb_w_qkv", "sb_w_o", "kv_norm", "kv_w", "swa_w_q", "swa_sinks", "swa_w_o", "final_norm"]
    outs = []
    for kind in range(4):
        sm = unpack(small[kind])
        for nm in order:
            outs.append(big[nm][kind] if nm in big else sm[nm])
    loss = small[0][9, 0]
    return (loss, dx.reshape(x.shape), *outs)
```

```python
import functools

import jax
import jax.numpy as jnp
from jax import lax
from jax.experimental import pallas as pl
from jax.experimental.pallas import tpu as pltpu

F32 = jnp.float32
BF16 = jnp.bfloat16
MESH = pl.DeviceIdType.MESH

D_MODEL = 1024
D_FF = 2816
HEAD_DIM = 64
SB_HEADS = 16
SWA_Q_HEADS = 16
SWA_KV_HEADS = 4
WINDOW = 128
ROPE_THETA = 10000.0
RMS_EPS = 1e-6
FFN_RES_SCALE = 0.5
ATTN_SCALE = HEAD_DIM ** -0.5

ADAM_LR = 0.001
ADAM_B1 = 0.9
ADAM_B2 = 0.999
ADAM_EPS = 1e-08
ADAM_WD = 0.01
ADAM_STEP = 10

N_CHIPS = 4
N_DEV = 8
LANES = 128
FF_CHUNK = D_FF // 2
FF_ROWS = D_FF // N_CHIPS
SQ_ROWS = D_MODEL // N_CHIPS
QKV_COLS = 3 * D_MODEL // N_CHIPS
VMEM_LIMIT = 56 * 1024 * 1024
NEG_BIG = -1e30

SLOT_FFN1 = (0, 1)
SLOT_FFN2 = (2, 3)
SQ_SB_O, SQ_SWA_Q, SQ_SWA_O = 0, 1, 2


def _cparams():
    return pltpu.CompilerParams(vmem_limit_bytes=VMEM_LIMIT)


def _dot(a, b):
    return jnp.dot(a, b, preferred_element_type=F32)


def _dot_nt(a, b):
    return lax.dot_general(a, b, (((1,), (1,)), ((), ())), preferred_element_type=F32)


def _dot_tn(a, b):
    return lax.dot_general(a, b, (((0,), (0,)), ((), ())), preferred_element_type=F32)


def _rstd(h):
    return lax.rsqrt(jnp.mean(h * h, axis=-1, keepdims=True) + RMS_EPS)


def _swap32(x):
    n = x.shape[-1]
    lane = lax.broadcasted_iota(jnp.int32, x.shape, x.ndim - 1)
    first = (lane % HEAD_DIM) < (HEAD_DIM // 2)
    return jnp.where(first, pltpu.roll(x, n - HEAD_DIM // 2, x.ndim - 1), pltpu.roll(x, HEAD_DIM // 2, x.ndim - 1))


def _tile_lanes(t, n):
    return t if n == LANES else jnp.tile(t, (1, n // LANES))


FFN_ROWS = 256


def _ffn_w_in_spec(slot):
    return pl.BlockSpec((N_CHIPS, None, D_MODEL, FF_CHUNK), lambda i: (0, slot, 0, 0), pipeline_mode=pl.Buffered(1))


def _ffn_w_out_spec(slot):
    return pl.BlockSpec((N_CHIPS, None, FF_ROWS, D_MODEL), lambda i: (0, slot, 0, 0), pipeline_mode=pl.Buffered(1))


def ffn_fwd(tag, h, g, w_in, w_out, slot):
    T = h.shape[0]
    tm = FFN_ROWS
    nch = D_FF // FF_CHUNK

    def body(h_ref, g_ref, wi_ref, wo_ref, out_ref, gate_ref, up_ref):
        hh = h_ref[...]
        xn = (hh * _rstd(hh) * g_ref[...]).astype(BF16)
        acc = None
        for j in range(nch):
            cols = slice(j * FF_CHUNK, (j + 1) * FF_CHUNK)
            gate = _dot(xn, wi_ref[j])
            up = _dot(xn, wi_ref[nch + j])
            gate_ref[:, cols] = gate.astype(BF16)
            up_ref[:, cols] = up.astype(BF16)
            a = (gate * jax.nn.sigmoid(gate) * up).astype(BF16)
            part = _dot(a, wo_ref[2 * j:2 * j + 2].reshape(FF_CHUNK, D_MODEL))
            acc = part if acc is None else acc + part
        out_ref[...] = hh + FFN_RES_SCALE * acc

    row = pl.BlockSpec((tm, D_MODEL), lambda i: (i, 0))
    ff = pl.BlockSpec((tm, D_FF), lambda i: (i, 0))
    return pl.pallas_call(
        body,
        name=f"ffn_fwd_{tag}",
        grid=(T // tm,),
        in_specs=[row, pl.BlockSpec((1, D_MODEL), lambda i: (0, 0)), _ffn_w_in_spec(slot), _ffn_w_out_spec(slot)],
        out_specs=[row, ff, ff],
        out_shape=[
            jax.ShapeDtypeStruct((T, D_MODEL), F32),
            jax.ShapeDtypeStruct((T, D_FF), BF16),
            jax.ShapeDtypeStruct((T, D_FF), BF16),
        ],
        compiler_params=_cparams(),
    )(h, g, w_in, w_out)


def ffn_bwd(tag, dh, h, g, gate, up, w_in, w_out, slot):
    T = dh.shape[0]
    tm = FFN_ROWS
    nch = D_FF // FF_CHUNK

    def body(dh_ref, h_ref, g_ref, gate_ref, up_ref, wi_ref, wo_ref,
             dhin_ref, xn_ref, dg_ref, du_ref, a_ref, dhb_ref, dnorm_ref):
        @pl.when(pl.program_id(0) == 0)
        def _():
            dnorm_ref[...] = jnp.zeros_like(dnorm_ref)

        dhh = dh_ref[...]
        dhb = (FFN_RES_SCALE * dhh).astype(BF16)
        dhb_ref[...] = dhb
        dxn = None
        for j in range(nch):
            cols = slice(j * FF_CHUNK, (j + 1) * FF_CHUNK)
            da = _dot_nt(dhb, wo_ref[2 * j:2 * j + 2].reshape(FF_CHUNK, D_MODEL))
            gt = gate_ref[:, cols].astype(F32)
            u = up_ref[:, cols].astype(F32)
            s = jax.nn.sigmoid(gt)
            silu = gt * s
            a_ref[:, cols] = (silu * u).astype(BF16)
            dgate = (da * u * (s * (1.0 + gt * (1.0 - s)))).astype(BF16)
            dup = (da * silu).astype(BF16)
            dg_ref[:, cols] = dgate
            du_ref[:, cols] = dup
            part = _dot_nt(dgate, wi_ref[j]) + _dot_nt(dup, wi_ref[nch + j])
            dxn = part if dxn is None else dxn + part
        hh = h_ref[...]
        gg = g_ref[...]
        r = _rstd(hh)
        hr = hh * r
        xn_ref[...] = (hr * gg).astype(BF16)
        dnorm_ref[...] += jnp.sum(dxn * hr, axis=0, keepdims=True)
        gd = gg * dxn
        dhin_ref[...] = dhh + r * (gd - hr * jnp.mean(gd * hr, axis=-1, keepdims=True))

    row = pl.BlockSpec((tm, D_MODEL), lambda i: (i, 0))
    ff = pl.BlockSpec((tm, D_FF), lambda i: (i, 0))
    vec = pl.BlockSpec((1, D_MODEL), lambda i: (0, 0))
    return pl.pallas_call(
        body,
        name=f"ffn_bwd_{tag}",
        grid=(T // tm,),
        in_specs=[row, row, vec, ff, ff, _ffn_w_in_spec(slot), _ffn_w_out_spec(slot)],
        out_specs=[row, row, ff, ff, ff, row, vec],
        out_shape=[
            jax.ShapeDtypeStruct((T, D_MODEL), F32),
            jax.ShapeDtypeStruct((T, D_MODEL), BF16),
            jax.ShapeDtypeStruct((T, D_FF), BF16),
            jax.ShapeDtypeStruct((T, D_FF), BF16),
            jax.ShapeDtypeStruct((T, D_FF), BF16),
            jax.ShapeDtypeStruct((T, D_MODEL), BF16),
            jax.ShapeDtypeStruct((1, D_MODEL), F32),
        ],
        compiler_params=_cparams(),
    )(dh, h, g, gate, up, w_in, w_out)


def rms_linear(name, h, g, w, w_spec, w_prep, n_out, tn, *, rope=None, scale=None):
    T = h.shape[0]
    tm = 512
    extra, extra_specs = [], []
    if rope is not None:
        extra += list(rope)
        extra_specs += [pl.BlockSpec((tm, LANES), lambda i, j: (i, 0))] * 2
    if scale is not None:
        extra.append(scale)
        extra_specs.append(pl.BlockSpec((1, tn), lambda i, j: (0, j)))

    def body(h_ref, g_ref, w_ref, *rest):
        rest = list(rest)
        cos_ref = sin_ref = sc_ref = None
        if rope is not None:
            cos_ref, sin_ref = rest[0], rest[1]
            rest = rest[2:]
        if scale is not None:
            sc_ref = rest[0]
            rest = rest[1:]
        out_ref, xn_s = rest

        @pl.when(pl.program_id(1) == 0)
        def _():
            hh = h_ref[...]
            xn_s[...] = (hh * _rstd(hh) * g_ref[...]).astype(BF16)

        y = _dot(xn_s[...], w_prep(w_ref[...]))
        if rope is not None:
            y = y * _tile_lanes(cos_ref[...], tn) + _swap32(y) * _tile_lanes(sin_ref[...], tn)
        if scale is not None:
            y = y * sc_ref[...]
        out_ref[...] = y.astype(BF16)

    return pl.pallas_call(
        body,
        name=name,
        grid=(T // tm, n_out // tn),
        in_specs=[
            pl.BlockSpec((tm, D_MODEL), lambda i, j: (i, 0)),
            pl.BlockSpec((1, D_MODEL), lambda i, j: (0, 0)),
            w_spec,
        ] + extra_specs,
        out_specs=pl.BlockSpec((tm, tn), lambda i, j: (i, j)),
        out_shape=jax.ShapeDtypeStruct((T, n_out), BF16),
        scratch_shapes=[pltpu.VMEM((tm, D_MODEL), BF16)],
        compiler_params=_cparams(),
    )(h, g, w, *extra)


QKV_ROWS = 512


def _qkv_w_spec():
    return pl.BlockSpec((N_CHIPS, D_MODEL, QKV_COLS), lambda i: (0, 0, 0), pipeline_mode=pl.Buffered(1))


def qkv_fwd(h, g, w_qkv, scale):
    T = h.shape[0]
    tm = QKV_ROWS

    def body(h_ref, g_ref, w_ref, sc_ref, out_ref):
        hh = h_ref[...]
        xn = (hh * _rstd(hh) * g_ref[...]).astype(BF16)
        for s in range(N_CHIPS):
            cols = slice(s * QKV_COLS, (s + 1) * QKV_COLS)
            out_ref[:, cols] = (_dot(xn, w_ref[s]) * sc_ref[:, cols]).astype(BF16)

    return pl.pallas_call(
        body,
        name="sb_qkv",
        grid=(T // tm,),
        in_specs=[
            pl.BlockSpec((tm, D_MODEL), lambda i: (i, 0)),
            pl.BlockSpec((1, D_MODEL), lambda i: (0, 0)),
            _qkv_w_spec(),
            pl.BlockSpec((1, 3 * D_MODEL), lambda i: (0, 0)),
        ],
        out_specs=pl.BlockSpec((tm, 3 * D_MODEL), lambda i: (i, 0)),
        out_shape=jax.ShapeDtypeStruct((T, 3 * D_MODEL), BF16),
        compiler_params=_cparams(),
    )(h, g, w_qkv, scale)


def qkv_bwd(dy, w_qkv, h, g, dres):
    T = h.shape[0]
    tm = QKV_ROWS

    def body(dy_ref, w_ref, h_ref, g_ref, dres_ref, dh_ref, xn_ref, dg_ref):
        @pl.when(pl.program_id(0) == 0)
        def _():
            dg_ref[...] = jnp.zeros_like(dg_ref)

        dxn = None
        for s in range(N_CHIPS):
            part = _dot_nt(dy_ref[:, s * QKV_COLS:(s + 1) * QKV_COLS], w_ref[s])
            dxn = part if dxn is None else dxn + part
        hh = h_ref[...]
        gg = g_ref[...]
        r = _rstd(hh)
        hr = hh * r
        xn_ref[...] = (hr * gg).astype(BF16)
        dg_ref[...] += jnp.sum(dxn * hr, axis=0, keepdims=True)
        gd = gg * dxn
        dh_ref[...] = dres_ref[...] + r * (gd - hr * jnp.mean(gd * hr, axis=-1, keepdims=True))

    row = pl.BlockSpec((tm, D_MODEL), lambda i: (i, 0))
    vec = pl.BlockSpec((1, D_MODEL), lambda i: (0, 0))
    return pl.pallas_call(
        body,
        name="sb_qkv_bwd",
        grid=(T // tm,),
        in_specs=[pl.BlockSpec((tm, 3 * D_MODEL), lambda i: (i, 0)), _qkv_w_spec(), row, vec, row],
        out_specs=[row, row, vec],
        out_shape=[
            jax.ShapeDtypeStruct((T, D_MODEL), F32),
            jax.ShapeDtypeStruct((T, D_MODEL), BF16),
            jax.ShapeDtypeStruct((1, D_MODEL), F32),
        ],
        compiler_params=_cparams(),
    )(dy, w_qkv, h, g, dres)


def linear_res(name, a, w_sq, t, res, bg_lands=()):
    T = a.shape[0]
    tm = 512
    nbg = len(bg_lands)
    nt = T // tm

    def body(a_ref, w_ref, res_ref, *rest):
        out_ref = rest[nbg]
        if nbg:
            gather = GatherOps([l.shape[2] for l in bg_lands], None, rest[nbg + 1:2 * nbg + 1], None, None,
                               *rest[2 * nbg + 1:])

            @pl.when(pl.program_id(0) == 0)
            def _():
                gather.start_forwards()

        out_ref[...] = res_ref[...] + _dot(a_ref[...], w_ref[...].reshape(D_MODEL, D_MODEL))
        if nbg:
            @pl.when(pl.program_id(0) == nt - 1)
            def _():
                gather.wait_forwards()

    row = pl.BlockSpec((tm, D_MODEL), lambda i: (i, 0))
    res_ = pl.pallas_call(
        body,
        name=name,
        grid=(nt,),
        in_specs=[row, pl.BlockSpec((N_CHIPS, None, SQ_ROWS, D_MODEL), lambda i: (0, t, 0, 0)), row] + [ANY] * nbg,
        out_specs=[row] + [ANY] * nbg,
        out_shape=[jax.ShapeDtypeStruct((T, D_MODEL), F32)] + [jax.ShapeDtypeStruct(l.shape, l.dtype) for l in bg_lands],
        input_output_aliases={3 + k: 1 + k for k in range(nbg)},
        scratch_shapes=[pltpu.SemaphoreType.DMA((nbg, N_PEER_CHIPS))] * (2 if nbg else 0),
        compiler_params=_cparams(),
    )(a, w_sq, res, *bg_lands)
    return (res_[0], list(res_[1:])) if nbg else res_[0]


def linear_bwd_plain(name, dy, w_sq, t):
    T = dy.shape[0]
    tm = 512

    def body(dy_ref, w_ref, da_ref, dyb_ref):
        dyb = dy_ref[...].astype(BF16)
        dyb_ref[...] = dyb
        da_ref[...] = _dot_nt(dyb, w_ref[...].reshape(D_MODEL, D_MODEL)).astype(BF16)

    row = pl.BlockSpec((tm, D_MODEL), lambda i: (i, 0))
    return pl.pallas_call(
        body,
        name=name,
        grid=(T // tm,),
        in_specs=[row, pl.BlockSpec((N_CHIPS, None, SQ_ROWS, D_MODEL), lambda i: (0, t, 0, 0))],
        out_specs=[row, row],
        out_shape=[jax.ShapeDtypeStruct((T, D_MODEL), BF16), jax.ShapeDtypeStruct((T, D_MODEL), BF16)],
        compiler_params=_cparams(),
    )(dy, w_sq)


def linear_bwd_rms(name, pairs, h, g, dres, nch, tn, tm=256):
    T = h.shape[0]
    npair = len(pairs)

    def body(*refs):
        dy_refs = refs[:npair]
        w_refs = refs[npair:2 * npair]
        h_ref, g_ref, dres_ref, dh_ref, xn_ref, dg_ref, acc_s = refs[2 * npair:]
        i = pl.program_id(0)
        j = pl.program_id(1)

        @pl.when(j == 0)
        def _():
            acc_s[...] = jnp.zeros_like(acc_s)

        @pl.when((i == 0) & (j == 0))
        def _():
            dg_ref[...] = jnp.zeros_like(dg_ref)

        part = None
        for p in range(npair):
            d = _dot_nt(dy_refs[p][...], pairs[p][3](w_refs[p][...]))
            part = d if part is None else part + d
        acc_s[...] += part

        @pl.when(j == nch - 1)
        def _():
            dxn = acc_s[...]
            hh = h_ref[...]
            gg = g_ref[...]
            r = _rstd(hh)
            hr = hh * r
            xn_ref[...] = (hr * gg).astype(BF16)
            dg_ref[...] += jnp.sum(dxn * hr, axis=0, keepdims=True)
            gd = gg * dxn
            dh_ref[...] = dres_ref[...] + r * (gd - hr * jnp.mean(gd * hr, axis=-1, keepdims=True))

    row = pl.BlockSpec((tm, D_MODEL), lambda i, j: (i, 0))
    vec = pl.BlockSpec((1, D_MODEL), lambda i, j: (0, 0))
    return pl.pallas_call(
        body,
        name=name,
        grid=(T // tm, nch),
        in_specs=[pl.BlockSpec((tm, tn), lambda i, j: (i, j))] * npair + [p[2] for p in pairs] + [row, vec, row],
        out_specs=[row, row, vec],
        out_shape=[
            jax.ShapeDtypeStruct((T, D_MODEL), F32),
            jax.ShapeDtypeStruct((T, D_MODEL), BF16),
            jax.ShapeDtypeStruct((1, D_MODEL), F32),
        ],
        scratch_shapes=[pltpu.VMEM((tm, D_MODEL), F32)],
        compiler_params=_cparams(),
    )(*[p[0] for p in pairs], *[p[1] for p in pairs], h, g, dres)


def loss_bwd(h, g, tgt):
    T = h.shape[0]
    tm = 512

    def body(h_ref, g_ref, t_ref, dh_ref, loss_ref, dg_ref):
        @pl.when(pl.program_id(0) == 0)
        def _():
            loss_ref[...] = jnp.zeros_like(loss_ref)
            dg_ref[...] = jnp.zeros_like(dg_ref)

        hh = h_ref[...]
        gg = g_ref[...]
        r = _rstd(hh)
        hr = hh * r
        err = hr * gg - t_ref[...]
        loss_ref[...] += 0.5 * jnp.sum(jnp.mean(err * err, axis=-1, keepdims=True), axis=0, keepdims=True)
        dy = err * (1.0 / D_MODEL)
        dg_ref[...] += jnp.sum(dy * hr, axis=0, keepdims=True)
        gd = gg * dy
        dh_ref[...] = r * (gd - hr * jnp.mean(gd * hr, axis=-1, keepdims=True))

    row = pl.BlockSpec((tm, D_MODEL), lambda i: (i, 0))
    vec = pl.BlockSpec((1, D_MODEL), lambda i: (0, 0))
    return pl.pallas_call(
        body,
        name="loss_bwd",
        grid=(T // tm,),
        in_specs=[row, vec, row],
        out_specs=[row, pl.BlockSpec((1, LANES), lambda i: (0, 0)), vec],
        out_shape=[
            jax.ShapeDtypeStruct((T, D_MODEL), F32),
            jax.ShapeDtypeStruct((1, LANES), F32),
            jax.ShapeDtypeStruct((1, D_MODEL), F32),
        ],
        compiler_params=_cparams(),
    )(h, g, tgt)


DW_TOKENS = 4096


def mm_tn(name, a, b, tk, tn, out_block, out_index, out_shape, prev=None, tt=DW_TOKENS):
    T = a.shape[0]
    ns, r = out_block[1], out_block[3]
    tt = min(tt, T)
    nt = T // tt

    def body(*refs):
        if prev is None:
            a_ref, b_ref, out_ref = refs
        else:
            a_ref, b_ref, _, out_ref = refs
        t = pl.program_id(2)
        res = _dot_tn(a_ref[...], b_ref[...])

        @pl.when(t == 0)
        def _():
            for u in range(ns):
                out_ref[u] = res[u * r:(u + 1) * r]

        @pl.when(t > 0)
        def _():
            for u in range(ns):
                out_ref[u] += res[u * r:(u + 1) * r]

    in_specs = [
        pl.BlockSpec((tt, tk), lambda k, n, t: (t, k)),
        pl.BlockSpec((tt, tn), lambda k, n, t: (t, n)),
    ]
    args = [a, b]
    aliases = {}
    if prev is not None:
        in_specs.append(pl.BlockSpec(memory_space=pl.ANY))
        args.append(prev)
        aliases = {2: 0}
    return pl.pallas_call(
        body,
        name=name,
        grid=(a.shape[1] // tk, b.shape[1] // tn, nt),
        in_specs=in_specs,
        out_specs=pl.BlockSpec(out_block, lambda k, n, t: out_index(k, n)),
        out_shape=jax.ShapeDtypeStruct(out_shape, F32),
        input_output_aliases=aliases,
        compiler_params=_cparams(),
    )(*args)


SB_BLOCK = 256
SB_QROWS = 256
SB_QROWS_BWD = 256
SB_UNDERFLOW_BITS = 140.0
SB_CHUNK = 128


LOG2E = 1.4426950408889634


def _softplus2(z2):
    sign = jnp.uint32(0x80000000)
    neg_abs = lax.bitcast_convert_type(lax.bitcast_convert_type(z2, jnp.uint32) | sign, F32)
    return jnp.log2(1.0 + jnp.exp2(neg_abs)) + jnp.maximum(z2, 0.0)


def _twice(x):
    return jnp.concatenate([x, x], axis=1)


def sb_fwd(qkv, bg_shards=(), bg_lands=()):
    T = qkv.shape[0]
    tq, tk = SB_QROWS, SB_BLOCK
    ratio = tq // tk
    npair = SB_HEADS // 2
    nbg = len(bg_shards)
    nq = T // tq

    def body(q_ref, k_ref, v_ref, *rest):
        bg_in = rest[:nbg]
        o_ref, tot_ref, first_ref = rest[2 * nbg:2 * nbg + 3]
        bg_out = rest[2 * nbg + 3:3 * nbg + 3]
        acc_s, c_s, z_s, w_s, kmax_s = rest[3 * nbg + 3:3 * nbg + 8]
        p = pl.program_id(0)
        i = pl.program_id(1)
        if nbg:
            gather = GatherOps([s.shape[1] for s in bg_shards], bg_in, bg_out, *rest[3 * nbg + 8:])

            @pl.when((p == 0) & (i == 0))
            def _():
                gather.start_ici()

        @pl.when(i == 0)
        def _():
            kmax_s[...] = jnp.max(jnp.abs(k_ref[...]), axis=0, keepdims=True).astype(F32)

        q = q_ref[...]
        lane = lax.broadcasted_iota(jnp.int32, (tq, LANES), 1)
        first = lane < HEAD_DIM
        zero = jnp.zeros_like(q)
        q_heads = (jnp.where(first, q, zero), jnp.where(first, zero, q))
        row = lax.broadcasted_iota(jnp.int32, (tq, tk), 0)
        col = lax.broadcasted_iota(jnp.int32, (tq, tk), 1)
        visible = [col + r * tk < row for r in range(ratio)]
        krow = lax.broadcasted_iota(jnp.int32, (tk, tk), 0)
        kcol = lax.broadcasted_iota(jnp.int32, (tk, tk), 1)
        from_s = (krow >= kcol).astype(BF16)
        acc_s[...] = jnp.zeros_like(acc_s)
        c_s[...] = jnp.zeros_like(c_s)

        def rows(j):
            return pl.ds(pl.multiple_of(j * tk, tk), tk)

        def logits(j):
            kb = k_ref[rows(j), :]
            for hd in range(2):
                z_s[hd] = _dot_nt(q_heads[hd], kb) * LOG2E

        def flush(j):
            vb = v_ref[rows(j), :]
            for hd in range(2):
                acc_s[hd] += _dot(w_s[hd], vb)

        def block(j, mask=None, walked_before=True):
            if walked_before:
                flush(j + 1)
            chunks = [(hd, slice(r0, r0 + SB_CHUNK)) for hd in range(2) for r0 in range(0, tq, SB_CHUNK)]
            k_next = k_ref[rows(jnp.maximum(j - 1, 0)), :]
            es, sums = [], []
            for hd, rs in chunks:
                z2 = z_s[hd, rs, :]
                z_s[hd, rs, :] = _dot_nt(q_heads[hd][rs, :], k_next) * LOG2E
                if mask is not None:
                    z2 = jnp.where(mask[rs, :], z2, NEG_BIG)
                sp = _softplus2(z2)
                c = c_s[hd, rs, :]
                es.append(z2 + _twice(c))
                c_s[hd, rs, :] = c - jnp.sum(sp, axis=1, keepdims=True)
                sums.append(_dot(sp.astype(BF16), from_s))
            for (hd, rs), e, s in zip(chunks, es, sums):
                w_s[hd, rs, :] = jnp.exp2(e - s).astype(BF16)

        z_bound = [LOG2E * jnp.sum(jnp.abs(q_heads[hd].astype(F32)) * kmax_s[...], axis=1, keepdims=True)
                   for hd in range(2)]

        def more_keys_matter():
            top = jnp.maximum(c_s[0] + z_bound[0], c_s[1] + z_bound[1])
            return (jnp.max(top) >= -SB_UNDERFLOW_BITS).astype(jnp.int32)

        logits(ratio * i + ratio - 1)
        for r in reversed(range(ratio)):
            block(ratio * i + r, visible[r], walked_before=(r != ratio - 1))

        def trip(carry):
            trips, _ = carry
            for r in reversed(range(ratio)):
                block(ratio * (i - 1 - trips) + r)
            return trips + 1, more_keys_matter()

        trips, _ = lax.while_loop(lambda carry: jnp.logical_and(carry[0] < i, carry[1] > 0), trip,
                                  (jnp.int32(0), jnp.int32(1)))
        first_walked = ratio * (i - trips)
        flush(first_walked)
        first_ref[p, i] = first_walked.astype(F32)
        o_ref[...] = jnp.where(first, acc_s[0], acc_s[1]).astype(BF16)
        tot_ref[...] = jnp.where(first, c_s[0], c_s[1])
        if nbg:
            @pl.when((p == npair - 1) & (i == nq - 1))
            def _():
                gather.wait_ici()

    res = pl.pallas_call(
        body,
        name="sb_fwd",
        grid=(npair, nq),
        in_specs=[
            pl.BlockSpec((tq, LANES), lambda p, i: (i, p)),
            pl.BlockSpec((T, LANES), lambda p, i: (0, npair + p)),
            pl.BlockSpec((T, LANES), lambda p, i: (0, 2 * npair + p)),
        ] + [ANY] * (2 * nbg),
        out_specs=[pl.BlockSpec((tq, LANES), lambda p, i: (i, p))] * 2 + [pl.BlockSpec(memory_space=pltpu.SMEM)]
        + [ANY] * nbg,
        out_shape=[jax.ShapeDtypeStruct((T, D_MODEL), BF16), jax.ShapeDtypeStruct((T, D_MODEL), F32),
                   jax.ShapeDtypeStruct((npair, nq), F32)]
        + [jax.ShapeDtypeStruct(l.shape, l.dtype) for l in bg_lands],
        input_output_aliases={3 + nbg + t: 3 + t for t in range(nbg)},
        scratch_shapes=[
            pltpu.VMEM((2, tq, LANES), F32), pltpu.VMEM((2, tq, LANES), F32),
            pltpu.VMEM((2, tq, tk), F32), pltpu.VMEM((2, tq, tk), BF16),
            pltpu.VMEM((1, LANES), F32),
        ] + [pltpu.SemaphoreType.DMA((nbg, N_PEER_CHIPS))] * (2 if nbg else 0),
        compiler_params=_cparams(),
    )(qkv, qkv, qkv, *bg_shards, *bg_lands)
    return res[0], res[1], res[2], list(res[3:])


def sb_bwd(qkv, do, tot, first_block, bg_parts=()):
    T = qkv.shape[0]
    tq, tk = SB_QROWS_BWD, SB_BLOCK
    ratio = tq // tk
    npair = SB_HEADS // 2
    nq = T // tq
    nk = T // tk
    nbg = len(bg_parts)
    assert SB_QROWS == SB_QROWS_BWD

    def body(first_ref, q_ref, k_ref, v_ref, do_ref, tot_ref, *rest):
        bg_in = rest[:nbg]
        dq_ref, dk_ref, dv_ref = rest[nbg:nbg + 3]
        bg_out = rest[nbg + 3:2 * nbg + 3]
        dkt_s, dvt_s, dq_s, rest_s, cg_s, z_s, da_s, dz_s, a_s = rest[2 * nbg + 3:2 * nbg + 12]
        i = pl.program_id(1)
        start = jnp.clip(first_ref[pl.program_id(0), i].astype(jnp.int32), 0, ratio * i)
        if nbg:
            @pl.when((pl.program_id(0) == 0) & (i == 0))
            def _():
                for cp in chip_partial_copies(bg_in, bg_out, *rest[2 * nbg + 12:]):
                    cp.start()

        @pl.when(i == 0)
        def _():
            dkt_s[...] = jnp.zeros_like(dkt_s)
            dvt_s[...] = jnp.zeros_like(dvt_s)

        q = q_ref[...]
        do_ = do_ref[...]
        tot_ = tot_ref[...]
        q_t = q.astype(F32).T.astype(BF16)
        do_t = do_.astype(F32).T.astype(BF16)
        lane = lax.broadcasted_iota(jnp.int32, (tq, LANES), 1)
        first = lane < HEAD_DIM
        zero = jnp.zeros_like(q)
        q_heads = (jnp.where(first, q, zero), jnp.where(first, zero, q))
        do_heads = (jnp.where(first, do_, zero), jnp.where(first, zero, do_))
        row = lax.broadcasted_iota(jnp.int32, (tq, tk), 0)
        col = lax.broadcasted_iota(jnp.int32, (tq, tk), 1)
        visible = [col + r * tk < row for r in range(ratio)]
        krow = lax.broadcasted_iota(jnp.int32, (tk, tk), 0)
        kcol = lax.broadcasted_iota(jnp.int32, (tk, tk), 1)
        before = (krow < kcol).astype(BF16)
        from_s = (krow >= kcol).astype(BF16)
        last = ratio * i + ratio - 1
        rest_s[0] = jnp.broadcast_to(tot_[:, 0:1], (tq, LANES))
        rest_s[1] = jnp.broadcast_to(tot_[:, HEAD_DIM:HEAD_DIM + 1], (tq, LANES))
        cg_s[...] = jnp.zeros_like(cg_s)
        dq_s[...] = jnp.zeros_like(dq_s)
        dz_s[...] = jnp.zeros_like(dz_s)
        a_s[...] = jnp.zeros_like(a_s)

        def rows(j):
            return pl.ds(pl.multiple_of(j * tk, tk), tk)

        def logits(j):
            kb = k_ref[rows(j), :]
            vb = v_ref[rows(j), :]
            for hd in range(2):
                z_s[hd] = _dot_nt(q_heads[hd], kb) * LOG2E
                da_s[hd] = _dot_nt(do_heads[hd], vb)

        def flush(j):
            kb = k_ref[rows(j), :]
            for hd in range(2):
                dims = slice(hd * HEAD_DIM, (hd + 1) * HEAD_DIM)
                dq_s[hd] += _dot(dz_s[hd], kb)
                dkt_s[j, dims, :] += _dot(q_t[dims, :], dz_s[hd])
                dvt_s[j, dims, :] += _dot(do_t[dims, :], a_s[hd])

        def block(j, mask=None):
            flush(jnp.maximum(j - 1, 0))
            chunks = [(hd, slice(r0, r0 + SB_CHUNK)) for hd in range(2) for r0 in range(0, tq, SB_CHUNK)]
            nxt = rows(jnp.minimum(j + 1, last))
            k_next = k_ref[nxt, :]
            v_next = v_ref[nxt, :]
            stage1 = []
            for hd, rs in chunks:
                z2 = z_s[hd, rs, :]
                z_s[hd, rs, :] = _dot_nt(q_heads[hd][rs, :], k_next) * LOG2E
                if mask is not None:
                    z2 = jnp.where(mask[rs, :], z2, NEG_BIG)
                sp = _softplus2(z2)
                rest = rest_s[hd, rs, :] + jnp.sum(sp, axis=1, keepdims=True)
                rest_s[hd, rs, :] = rest
                stage1.append((z2 + _twice(rest), z2 - sp, _dot(sp.astype(BF16), from_s)))
            stage2 = []
            for (hd, rs), (e, log2_beta, ahead) in zip(chunks, stage1):
                a = jnp.exp2(e - ahead)
                g = a * da_s[hd, rs, :]
                da_s[hd, rs, :] = _dot_nt(do_heads[hd][rs, :], v_next)
                cg = cg_s[hd, rs, :]
                a_s[hd, rs, :] = a.astype(BF16)
                cg_s[hd, rs, :] = cg + jnp.sum(g, axis=1, keepdims=True)
                stage2.append((g, g + _twice(cg), log2_beta, _dot(g.astype(BF16), before)))
            for (hd, rs), (g, g_from, log2_beta, g_before) in zip(chunks, stage2):
                dz_s[hd, rs, :] = (g - jnp.exp2(log2_beta) * (g_from + g_before)).astype(BF16)

        logits(start)

        @pl.loop(start, ratio * i)
        def _(j):
            block(j)

        for r in range(ratio):
            block(ratio * i + r, visible[r])
        flush(last)
        dq_ref[...] = (jnp.where(first, dq_s[0], dq_s[1]) * ATTN_SCALE).astype(BF16)

        @pl.when(i == nq - 1)
        def _():
            @pl.loop(0, nk)
            def _(b):
                dk_ref[rows(b), :] = dkt_s[b].T.astype(BF16)
                dv_ref[rows(b), :] = dvt_s[b].T.astype(BF16)

        if nbg:
            @pl.when((pl.program_id(0) == npair - 1) & (i == nq - 1))
            def _():
                for cp in chip_partial_copies(bg_in, bg_out, *rest[2 * nbg + 12:]):
                    cp.wait()

    qblk = pl.BlockSpec((tq, LANES), lambda p, i: (i, p))
    full = pl.BlockSpec((T, LANES), lambda p, i: (0, p))
    res = pl.pallas_call(
        body,
        name="sb_bwd",
        grid=(npair, nq),
        in_specs=[
            pl.BlockSpec(memory_space=pltpu.SMEM),
            qblk,
            pl.BlockSpec((T, LANES), lambda p, i: (0, npair + p)),
            pl.BlockSpec((T, LANES), lambda p, i: (0, 2 * npair + p)),
            qblk, qblk,
        ] + [ANY] * nbg,
        out_specs=[qblk, full, full] + [ANY] * nbg,
        out_shape=[jax.ShapeDtypeStruct((T, D_MODEL), BF16)] * 3
        + [jax.ShapeDtypeStruct(b.shape, b.dtype) for b in bg_parts],
        scratch_shapes=[
            pltpu.VMEM((nk, LANES, tk), F32), pltpu.VMEM((nk, LANES, tk), F32),
            pltpu.VMEM((2, tq, LANES), F32), pltpu.VMEM((2, tq, LANES), F32), pltpu.VMEM((2, tq, LANES), F32),
            pltpu.VMEM((2, tq, tk), F32), pltpu.VMEM((2, tq, tk), F32),
            pltpu.VMEM((2, tq, tk), BF16), pltpu.VMEM((2, tq, tk), BF16),
        ] + [pltpu.SemaphoreType.DMA((nbg, N_PEER_CHIPS))] * (2 if nbg else 0),
        compiler_params=_cparams(),
    )(first_block, qkv, qkv, qkv, do, tot, *bg_parts)
    return res[0], res[1], res[2], list(res[3:])


def _swa_valid(n):
    qi = lax.broadcasted_iota(jnp.int32, (WINDOW, 2 * WINDOW), 0)
    ki = lax.broadcasted_iota(jnp.int32, (WINDOW, 2 * WINDOW), 1)
    diff = qi + WINDOW - ki
    return (diff >= 0) & (diff < WINDOW) & ((n > 0) | (ki >= WINDOW))


def _to_half(x, first, src, dst):
    keep = first if src == 0 else jnp.logical_not(first)
    x = jnp.where(keep, x, jnp.zeros_like(x))
    if src != dst:
        x = pltpu.roll(x.astype(F32), HEAD_DIM, 1).astype(BF16)
    return x


SWA_GROUP = SWA_Q_HEADS // SWA_KV_HEADS


def _swa_cols(h):
    return slice((h // 2) * LANES, (h // 2 + 1) * LANES)


def _swa_kv_pair(h):
    return (h // SWA_GROUP) // 2


def _swa_kv_half(h):
    return (h // SWA_GROUP) % 2


def _kv_band(prev_ref, cur_ref, pb):
    cols = slice(pb * LANES, (pb + 1) * LANES)
    return jnp.concatenate([prev_ref[:, cols], cur_ref[:, cols]], axis=0)


def _swa_specs(T):
    nb = T // WINDOW
    kv_w = SWA_KV_HEADS * HEAD_DIM
    qrow = pl.BlockSpec((WINDOW, D_MODEL), lambda n: (n, 0))
    cur = pl.BlockSpec((WINDOW, kv_w), lambda n: (n, 0))
    prev = pl.BlockSpec((WINDOW, kv_w), lambda n: (jnp.maximum(n - 1, 0), 0))
    smem = pl.BlockSpec(memory_space=pltpu.SMEM)
    return nb, qrow, cur, prev, smem


def swa_fwd(q, k, v, sinks):
    T = q.shape[0]
    nb, qrow, cur, prev, smem = _swa_specs(T)

    def body(sink_ref, q_ref, kc_ref, kp_ref, vc_ref, vp_ref, o_ref, lse_ref):
        n = pl.program_id(0)
        lane = lax.broadcasted_iota(jnp.int32, (WINDOW, LANES), 1)
        first = lane < HEAD_DIM
        valid = _swa_valid(n)
        k2 = [_kv_band(kp_ref, kc_ref, pb) for pb in range(SWA_KV_HEADS // 2)]
        v2 = [_kv_band(vp_ref, vc_ref, pb) for pb in range(SWA_KV_HEADS // 2)]
        logits = [jnp.where(valid, _dot_nt(_to_half(q_ref[:, _swa_cols(h)], first, h % 2, _swa_kv_half(h)),
                                            k2[_swa_kv_pair(h)]), NEG_BIG) for h in range(SWA_Q_HEADS)]
        probs = []
        lse_acc = jnp.zeros((WINDOW, LANES), F32)
        for h, s in enumerate(logits):
            sink = sink_ref[h]
            m = jnp.maximum(jnp.max(s, axis=1, keepdims=True), sink)
            p = jnp.exp(s - m)
            den = jnp.sum(p, axis=1, keepdims=True) + jnp.exp(sink - m)
            probs.append((p / den).astype(BF16))
            lse_acc = jnp.where(lane == h, m + jnp.log(den), lse_acc)
        outs = []
        for h, p in enumerate(probs):
            o = _dot(p, v2[_swa_kv_pair(h)])
            outs.append(pltpu.roll(o, HEAD_DIM, 1) if h % 2 != _swa_kv_half(h) else o)
        for pair in range(SWA_Q_HEADS // 2):
            o_ref[:, _swa_cols(2 * pair)] = jnp.where(first, outs[2 * pair], outs[2 * pair + 1]).astype(BF16)
        lse_ref[...] = lse_acc

    return pl.pallas_call(
        body,
        name="swa_fwd",
        grid=(nb,),
        in_specs=[smem, qrow, cur, prev, cur, prev],
        out_specs=[qrow, pl.BlockSpec((WINDOW, LANES), lambda n: (n, 0))],
        out_shape=[jax.ShapeDtypeStruct((T, D_MODEL), BF16), jax.ShapeDtypeStruct((T, LANES), F32)],
        compiler_params=_cparams(),
    )(sinks, q, k, k, v, v)


def swa_bwd(q, k, v, sinks, do, o, lse, cos, sin):
    T = q.shape[0]
    nb, qrow, cur, prev, smem = _swa_specs(T)
    kv_w = SWA_KV_HEADS * HEAD_DIM

    def body(sink_ref, q_ref, kc_ref, kp_ref, vc_ref, vp_ref, do_ref, o_ref, lse_ref, cos_ref, sin_ref,
             dq_ref, own_ref, prv_ref, dsink_ref):
        n = pl.program_id(0)

        @pl.when(n == 0)
        def _():
            dsink_ref[...] = jnp.zeros_like(dsink_ref)

        lane = lax.broadcasted_iota(jnp.int32, (WINDOW, LANES), 1)
        lane1 = lax.broadcasted_iota(jnp.int32, (1, LANES), 1)
        first = lane < HEAD_DIM
        valid = _swa_valid(n)
        cos_ = cos_ref[...]
        sin_ = sin_ref[...]
        k2 = [_kv_band(kp_ref, kc_ref, pb) for pb in range(SWA_KV_HEADS // 2)]
        v2 = [_kv_band(vp_ref, vc_ref, pb) for pb in range(SWA_KV_HEADS // 2)]
        q_t = q_ref[...].astype(F32).T.astype(BF16)
        do_t = do_ref[...].astype(F32).T.astype(BF16)
        stage1 = []
        for h in range(SWA_Q_HEADS):
            a, b, pb = h % 2, _swa_kv_half(h), _swa_kv_pair(h)
            qh = _to_half(q_ref[:, _swa_cols(h)], first, a, b)
            doh = _to_half(do_ref[:, _swa_cols(h)], first, a, b)
            stage1.append((jnp.where(valid, _dot_nt(qh, k2[pb]), NEG_BIG), _dot_nt(doh, v2[pb])))
        deltas = []
        for pair in range(SWA_Q_HEADS // 2):
            prod = do_ref[:, _swa_cols(2 * pair)].astype(F32) * o_ref[:, _swa_cols(2 * pair)].astype(F32)
            deltas += [jnp.sum(jnp.where(first, prod, 0.0), axis=1, keepdims=True),
                       jnp.sum(jnp.where(first, 0.0, prod), axis=1, keepdims=True)]
        stage2 = []
        dsink = jnp.zeros((1, LANES), F32)
        for h, (s, dp) in enumerate(stage1):
            lse_h = lse_ref[:, h:h + 1]
            p = jnp.exp(s - lse_h)
            delta = deltas[h]
            p_sink = jnp.exp(sink_ref[h] - lse_h)
            dsink = dsink + jnp.where(lane1 == h, -jnp.sum(p_sink * delta, axis=0, keepdims=True), 0.0)
            stage2.append(((p * (dp - delta)).astype(BF16), p.astype(BF16)))
        dqs = []
        dk_t = [None] * SWA_KV_HEADS
        dv_t = [None] * SWA_KV_HEADS
        for h, (ds, pb16) in enumerate(stage2):
            kvh = h // SWA_GROUP
            dims = slice(h * HEAD_DIM, (h + 1) * HEAD_DIM)
            dq = _dot(ds, k2[_swa_kv_pair(h)])
            dqs.append(pltpu.roll(dq, HEAD_DIM, 1) if h % 2 != _swa_kv_half(h) else dq)
            dk_h = _dot(q_t[dims, :], ds)
            dv_h = _dot(do_t[dims, :], pb16)
            dk_t[kvh] = dk_h if dk_t[kvh] is None else dk_t[kvh] + dk_h
            dv_t[kvh] = dv_h if dv_t[kvh] is None else dv_t[kvh] + dv_h
        for pair in range(SWA_Q_HEADS // 2):
            dqp = jnp.where(first, dqs[2 * pair], dqs[2 * pair + 1])
            dq_ref[:, _swa_cols(2 * pair)] = ((dqp * cos_ + _swap32(dqp * sin_)) * ATTN_SCALE).astype(BF16)
        for pb in range(SWA_KV_HEADS // 2):
            dk2 = jnp.concatenate([dk_t[2 * pb], dk_t[2 * pb + 1]], axis=0).T
            dv2 = jnp.concatenate([dv_t[2 * pb], dv_t[2 * pb + 1]], axis=0).T
            kcols = slice(pb * LANES, (pb + 1) * LANES)
            vcols = slice(kv_w + pb * LANES, kv_w + (pb + 1) * LANES)
            prv_ref[:, kcols] = dk2[:WINDOW]
            own_ref[:, kcols] = dk2[WINDOW:]
            prv_ref[:, vcols] = dv2[:WINDOW]
            own_ref[:, vcols] = dv2[WINDOW:]
        dsink_ref[...] += dsink

    tab = pl.BlockSpec((WINDOW, LANES), lambda n: (n, 0))
    kvrow = pl.BlockSpec((WINDOW, 2 * kv_w), lambda n: (n, 0))
    return pl.pallas_call(
        body,
        name="swa_bwd",
        grid=(nb,),
        in_specs=[smem, qrow, cur, prev, cur, prev, qrow, qrow, tab, tab, tab],
        out_specs=[qrow, kvrow, kvrow, pl.BlockSpec((1, LANES), lambda n: (0, 0))],
        out_shape=[
            jax.ShapeDtypeStruct((T, D_MODEL), BF16),
            jax.ShapeDtypeStruct((T, 2 * kv_w), F32),
            jax.ShapeDtypeStruct((T, 2 * kv_w), F32),
            jax.ShapeDtypeStruct((1, LANES), F32),
        ],
        compiler_params=_cparams(),
    )(sinks, q, k, k, v, v, do, o, lse, cos, sin)


def kv_grad_combine(own, prv, cos, sin):
    T = own.shape[0]
    nb = T // WINDOW
    kv_w = SWA_KV_HEADS * HEAD_DIM

    def body(own_ref, nxt_ref, cos_ref, sin_ref, out_ref):
        n = pl.program_id(0)
        nxt = jnp.where(n + 1 < nb, nxt_ref[...], 0.0)
        tot = own_ref[...] + nxt
        dk = tot[:, :kv_w]
        c = _tile_lanes(cos_ref[...], kv_w)
        s = _tile_lanes(sin_ref[...], kv_w)
        out_ref[:, :kv_w] = (dk * c + _swap32(dk * s)).astype(BF16)
        out_ref[:, kv_w:] = tot[:, kv_w:].astype(BF16)

    tab = pl.BlockSpec((WINDOW, LANES), lambda n: (n, 0))
    kvrow = pl.BlockSpec((WINDOW, 2 * kv_w), lambda n: (n, 0))
    return pl.pallas_call(
        body,
        name="kv_grad_combine",
        grid=(nb,),
        in_specs=[kvrow, pl.BlockSpec((WINDOW, 2 * kv_w), lambda n: (jnp.minimum(n + 1, nb - 1), 0)), tab, tab],
        out_specs=kvrow,
        out_shape=jax.ShapeDtypeStruct((T, 2 * kv_w), BF16),
        compiler_params=_cparams(),
    )(own, prv, cos, sin)


ANY = pl.BlockSpec(memory_space=pl.ANY)


def _place():
    x, y, c = lax.axis_index("x"), lax.axis_index("y"), lax.axis_index("c")
    other_chips = [(1 - x, y), (x, 1 - y), (1 - x, 1 - y)]
    return x, y, c, 2 * x + y, other_chips


N_PEER_CHIPS = N_CHIPS - 1


class GatherOps:
    def __init__(self, rows, shards, lands, ici_send, ici_recv, d2d_send=None, d2d_recv=None):
        self.rows, self.shards, self.lands = rows, shards, lands
        self.ici_send, self.ici_recv, self.d2d_send, self.d2d_recv = ici_send, ici_recv, d2d_send, d2d_recv
        self.x, self.y, self.c, self.me, self.chips = _place()
        self.pairs = [(t, jdx) for t in range(len(rows)) for jdx in range(N_PEER_CHIPS)]

    def _half(self, ref, t, which):
        r = self.rows[t] // 2
        return ref.at[:, pl.ds(pl.multiple_of(which * r, 16), r), :]

    def _ici(self, t, jdx):
        px, py = self.chips[jdx]
        return pltpu.make_async_remote_copy(
            src_ref=self._half(self.shards[t], t, self.c), dst_ref=self._half(self.lands[t].at[self.me], t, self.c),
            send_sem=self.ici_send.at[t, jdx], recv_sem=self.ici_recv.at[t, jdx],
            device_id=(px, py, self.c), device_id_type=MESH)

    def _landed(self, t, jdx):
        px, py = self.chips[jdx]
        blk = self._half(self.lands[t].at[2 * px + py], t, self.c)
        return pltpu.make_async_remote_copy(
            src_ref=blk, dst_ref=blk, send_sem=self.ici_send.at[t, jdx], recv_sem=self.ici_recv.at[t, jdx],
            device_id=(px, py, self.c), device_id_type=MESH)

    def _d2d(self, t, jdx, which):
        px, py = self.chips[jdx]
        blk = self._half(self.lands[t].at[2 * px + py], t, which)
        return pltpu.make_async_remote_copy(
            src_ref=blk, dst_ref=blk, send_sem=self.d2d_send.at[t, jdx], recv_sem=self.d2d_recv.at[t, jdx],
            device_id=(self.x, self.y, 1 - self.c), device_id_type=MESH)

    def start_ici(self):
        for t, jdx in self.pairs:
            self._ici(t, jdx).start()

    def wait_ici(self):
        for t, jdx in self.pairs:
            self._landed(t, jdx).wait_recv()
        self.wait_ici_sends()

    def wait_ici_sends(self):
        for t, jdx in self.pairs:
            self._ici(t, jdx).wait_send()

    def forward_arrivals(self):
        for t, jdx in self.pairs:
            self._landed(t, jdx).wait_recv()
            self._d2d(t, jdx, self.c).start()

    def start_forwards(self):
        for t, jdx in self.pairs:
            self._d2d(t, jdx, self.c).start()

    def wait_forwards(self):
        for t, jdx in self.pairs:
            self._d2d(t, jdx, 1 - self.c).wait_recv()
            self._d2d(t, jdx, self.c).wait_send()


def all_gather_weights(shards, lands):
    n = len(shards)
    rows = [s.shape[1] for s in shards]

    def body(*refs):
        ins, outs = refs[:n], refs[2 * n:3 * n]
        ops = GatherOps(rows, ins, outs, *refs[3 * n:])
        ops.start_ici()
        ops.forward_arrivals()
        ops.wait_forwards()
        ops.wait_ici_sends()

    return pl.pallas_call(
        body,
        name="all_gather_weights",
        in_specs=[ANY] * (2 * n),
        out_specs=[ANY] * n,
        out_shape=[jax.ShapeDtypeStruct(l.shape, l.dtype) for l in lands],
        input_output_aliases={n + t: t for t in range(n)},
        scratch_shapes=[pltpu.SemaphoreType.DMA((n, N_PEER_CHIPS))] * 4,
    )(*shards, *lands)


def place_own_shard(name, shard, chip):
    nl, r, c = shard.shape

    def body(chip_ref, s_ref, o_ref):
        o_ref[...] = s_ref[...]

    return pl.pallas_call(
        body, name=name,
        grid_spec=pltpu.PrefetchScalarGridSpec(
            num_scalar_prefetch=1, grid=(nl,),
            in_specs=[pl.BlockSpec((None, r, c), lambda l, chip_ref: (l, 0, 0))],
            out_specs=pl.BlockSpec((None, None, r, c), lambda l, chip_ref: (chip_ref[0], l, 0, 0))),
        out_shape=jax.ShapeDtypeStruct((N_CHIPS,) + shard.shape, shard.dtype), compiler_params=_cparams(),
    )(chip, shard)


def exchange_halves(name, slabs):
    n = len(slabs)

    def body(*refs):
        ins, theirs = refs[:n], refs[n:2 * n]
        send_sems, recv_sems = refs[2 * n:]
        x, y, c, _, _ = _place()
        copies = []
        for t in range(n):
            cp = pltpu.make_async_remote_copy(
                src_ref=ins[t].at[1 - c], dst_ref=theirs[t], send_sem=send_sems.at[t],
                recv_sem=recv_sems.at[t], device_id=(x, y, 1 - c), device_id_type=MESH)
            cp.start()
            copies.append(cp)
        for cp in copies:
            cp.wait()

    return pl.pallas_call(
        body,
        name=name,
        in_specs=[ANY] * n,
        out_specs=[ANY] * n,
        out_shape=[jax.ShapeDtypeStruct(s.shape[1:], s.dtype) for s in slabs],
        scratch_shapes=[pltpu.SemaphoreType.DMA((n,)), pltpu.SemaphoreType.DMA((n,))],
    )(*slabs)


def chip_partial_copies(ins, outs, send_sems, recv_sems):
    _, _, c, me, chips = _place()
    return [pltpu.make_async_remote_copy(
        src_ref=ins[t].at[2 * px + py], dst_ref=outs[t].at[me], send_sem=send_sems.at[t, jdx],
        recv_sem=recv_sems.at[t, jdx], device_id=(px, py, c), device_id_type=MESH)
        for t in range(len(ins)) for jdx, (px, py) in enumerate(chips)]


def exchange_chip_partials(name, parts):
    n = len(parts)

    def body(*refs):
        copies = chip_partial_copies(refs[:n], refs[n:2 * n], *refs[2 * n:])
        for cp in copies:
            cp.start()
        for cp in copies:
            cp.wait()

    return pl.pallas_call(
        body,
        name=name,
        in_specs=[ANY] * n,
        out_specs=[ANY] * n,
        out_shape=[jax.ShapeDtypeStruct(p.shape, p.dtype) for p in parts],
        scratch_shapes=[pltpu.SemaphoreType.DMA((n, 3)), pltpu.SemaphoreType.DMA((n, 3))],
    )(*parts)


def share_reduced_halves(name, halves):
    n = len(halves)

    def body(*refs):
        ins, outs = refs[:n], refs[n:2 * n]
        send_sems, recv_sems = refs[2 * n:]
        x, y, c, _, _ = _place()
        copies = []
        for t in range(n):
            cp = pltpu.make_async_remote_copy(
                src_ref=ins[t], dst_ref=outs[t], send_sem=send_sems.at[t],
                recv_sem=recv_sems.at[t], device_id=(x, y, 1 - c), device_id_type=MESH)
            cp.start()
            copies.append(cp)
        for cp in copies:
            cp.wait()

    return pl.pallas_call(
        body,
        name=name,
        in_specs=[ANY] * n,
        out_specs=[ANY] * n,
        out_shape=[jax.ShapeDtypeStruct(h.shape, h.dtype) for h in halves],
        scratch_shapes=[pltpu.SemaphoreType.DMA((n,)), pltpu.SemaphoreType.DMA((n,))],
    )(*halves)


def _row_tile(r, c):
    tr = r
    while tr * c * 4 > (3 << 19) and tr % 16 == 0:
        tr //= 2
    return tr


def add_sibling(name, slab, theirs, core):
    _, ns, slots, r, c = slab.shape
    tr = _row_tile(r, c)

    def body(core_ref, a_ref, b_ref, o_ref):
        o_ref[...] = (a_ref[...] + b_ref[...]).astype(BF16)

    blk = pl.BlockSpec((None, None, tr, c), lambda s, l, i, core_ref: (s, l, i, 0))
    return pl.pallas_call(
        body, name=name,
        grid_spec=pltpu.PrefetchScalarGridSpec(
            num_scalar_prefetch=1, grid=(ns, slots, r // tr),
            in_specs=[pl.BlockSpec((None, None, None, tr, c), lambda s, l, i, core_ref: (core_ref[0], s, l, i, 0)), blk],
            out_specs=blk),
        out_shape=jax.ShapeDtypeStruct(theirs.shape, BF16), compiler_params=_cparams(),
    )(core, slab, theirs)


def sum_chips(name, recv, own, chip):
    _, slots, r, c = recv.shape
    tr = _row_tile(r, c)

    def body(chip_ref, r0, r1, r2, r3, own_ref, o_ref):
        me = chip_ref[0]
        mine = own_ref[...]
        terms = [jnp.where(me == s, mine, rr[...]).astype(F32) for s, rr in enumerate((r0, r1, r2, r3))]
        o_ref[...] = ((terms[0] + terms[1]) + terms[2]) + terms[3]

    def src(s):
        return pl.BlockSpec((None, None, tr, c),
                            lambda l, i, chip_ref: (jnp.where(chip_ref[0] == s, (s + 1) % N_CHIPS, s), l, i, 0))

    return pl.pallas_call(
        body, name=name,
        grid_spec=pltpu.PrefetchScalarGridSpec(
            num_scalar_prefetch=1, grid=(slots, r // tr),
            in_specs=[src(0), src(1), src(2), src(3),
                      pl.BlockSpec((None, None, tr, c), lambda l, i, chip_ref: (chip_ref[0], l, i, 0))],
            out_specs=pl.BlockSpec((None, tr, c), lambda l, i, chip_ref: (l, i, 0))),
        out_shape=jax.ShapeDtypeStruct((slots, r, c), F32), compiler_params=_cparams(),
    )(chip, recv, recv, recv, recv, own)


def _adamw_math(w, g, m, v):
    m = ADAM_B1 * m + (1.0 - ADAM_B1) * g
    v = ADAM_B2 * v + (1.0 - ADAM_B2) * (g * g)
    m_hat = m / (1.0 - ADAM_B1 ** ADAM_STEP)
    v_hat = v / (1.0 - ADAM_B2 ** ADAM_STEP)
    delta = -ADAM_LR * (m_hat / (jnp.sqrt(v_hat) + ADAM_EPS) + ADAM_WD * w)
    return delta, m, v


def adamw_shard(name, w, m, v, g_pairs, core, slot, row_halves):
    n = w.shape[0]
    assert n == len(g_pairs)
    _, r, c = g_pairs[0][0].shape
    tr = _row_tile(r, c)
    nr = r // tr

    def body(core_ref, w_ref, m_ref, v_ref, *rest):
        g_refs, (go_ref, d_ref, mo_ref, vo_ref) = rest[:2 * n], rest[2 * n:]
        mine = pl.program_id(1) == core_ref[0]
        g = jnp.where(mine, g_refs[0][...], g_refs[1][...])
        for l in range(1, n):
            g = jnp.where(pl.program_id(0) == l, jnp.where(mine, g_refs[2 * l][...], g_refs[2 * l + 1][...]), g)
        delta, mm, vv = _adamw_math(w_ref[...], g, m_ref[...], v_ref[...])
        go_ref[...] = g
        d_ref[...] = delta
        mo_ref[...] = mm
        vo_ref[...] = vv

    if row_halves:
        wspec = pl.BlockSpec((None, tr, c), lambda l, h, i, core_ref: (l, h * nr + i, 0))
    else:
        wspec = pl.BlockSpec((None, tr, c), lambda l, h, i, core_ref: (l, i, h))
    gspec = pl.BlockSpec((None, tr, c), lambda l, h, i, core_ref: (slot, i, 0))
    shp = jax.ShapeDtypeStruct(w.shape, F32)
    return pl.pallas_call(
        body, name=name,
        grid_spec=pltpu.PrefetchScalarGridSpec(
            num_scalar_prefetch=1, grid=(n, 2, nr),
            in_specs=[wspec, wspec, wspec] + [gspec] * (2 * n), out_specs=[wspec] * 4),
        out_shape=[shp] * 4, compiler_params=_cparams(),
    )(core, w, m, v, *[g for pair in g_pairs for g in pair])


SMALL_ROWS = 16


def small_allreduce_adamw(part, w, m, v):
    def body(p_ref, w_ref, m_ref, v_ref, g_ref, d_ref, mo_ref, vo_ref, buf, send_sems, recv_sems):
        x, y, c, _, _ = _place()
        me = 4 * x + 2 * y + c
        buf[me] = p_ref[...]
        copies = []
        for k in range(1, N_DEV):
            kx, ky, kc = (k >> 2) & 1, (k >> 1) & 1, k & 1
            peer = (x ^ kx, y ^ ky, c ^ kc)
            cp = pltpu.make_async_remote_copy(
                src_ref=p_ref, dst_ref=buf.at[me], send_sem=send_sems.at[k - 1],
                recv_sem=recv_sems.at[k - 1], device_id=peer, device_id_type=MESH)
            cp.start()
            copies.append(cp)
        for cp in copies:
            cp.wait()
        g = buf[0]
        for dev in range(1, N_DEV):
            g = g + buf[dev]
        delta, mm, vv = _adamw_math(w_ref[...], g, m_ref[...], v_ref[...])
        g_ref[...] = g
        d_ref[...] = delta
        mo_ref[...] = mm
        vo_ref[...] = vv

    vm = pl.BlockSpec(memory_space=pltpu.VMEM)
    shp = jax.ShapeDtypeStruct(part.shape, F32)
    return pl.pallas_call(
        body, name="small_allreduce_adamw",
        in_specs=[vm] * 4, out_specs=[vm] * 4, out_shape=[shp] * 4,
        scratch_shapes=[
            pltpu.VMEM((N_DEV,) + part.shape, F32),
            pltpu.SemaphoreType.DMA((N_DEV - 1,)), pltpu.SemaphoreType.DMA((N_DEV - 1,)),
        ],
    )(part, w, m, v)


def _rope_tables(T):
    half = HEAD_DIM // 2
    inv_freq = ROPE_THETA ** (-jnp.arange(half, dtype=F32) / half)
    ang = jnp.arange(T).astype(F32)[:, None] * inv_freq[None, :]
    cos = jnp.tile(jnp.cos(ang), (1, LANES // half))
    sin = jnp.tile(jnp.sin(ang), (1, LANES // half))
    lane = jnp.arange(LANES)
    sign = jnp.where((lane % HEAD_DIM) < half, -1.0, 1.0).astype(F32)
    return cos, sin * sign[None, :]


def _pack_small(ffn1, mix, ffn2, kvn, fin, sinks, loss_row):
    sink_row = jnp.pad(sinks.reshape(1, SWA_Q_HEADS), ((0, 0), (0, D_MODEL - SWA_Q_HEADS)))
    rows = jnp.concatenate([ffn1, mix, ffn2, kvn.reshape(1, -1), fin.reshape(1, -1), sink_row, loss_row], axis=0)
    return jnp.concatenate([rows, jnp.zeros((SMALL_ROWS - rows.shape[0], D_MODEL), F32)], axis=0)


def kernel(x, ffn1_norm, ffn1_w_in, ffn1_w_out, mix_norm, ffn2_norm, ffn2_w_in, ffn2_w_out, sb_w_qkv, sb_w_o, kv_norm, kv_w, swa_w_q, swa_sinks, swa_w_o, final_norm, loss_target, m_ffn1_norm, m_ffn1_w_in, m_ffn1_w_out, m_mix_norm, m_ffn2_norm, m_ffn2_w_in, m_ffn2_w_out, m_sb_w_qkv, m_sb_w_o, m_kv_norm, m_kv_w, m_swa_w_q, m_swa_sinks, m_swa_w_o, m_final_norm, v_ffn1_norm, v_ffn1_w_in, v_ffn1_w_out, v_mix_norm, v_ffn2_norm, v_ffn2_w_in, v_ffn2_w_out, v_sb_w_qkv, v_sb_w_o, v_kv_norm, v_kv_w, v_swa_w_q, v_swa_sinks, v_swa_w_o, v_final_norm):
    T = x.shape[1]
    kv_cols = SWA_KV_HEADS * HEAD_DIM
    x2 = x.reshape(T, D_MODEL)
    tgt = loss_target.reshape(T, D_MODEL)
    cos, sin = _rope_tables(T)

    w_in_l = jnp.concatenate([ffn1_w_in, ffn2_w_in], axis=0).astype(BF16)
    w_out_l = jnp.concatenate([ffn1_w_out, ffn2_w_out], axis=0).astype(BF16)
    sq_l = jnp.concatenate([sb_w_o, swa_w_q, swa_w_o], axis=0).astype(BF16)
    qkv_l = sb_w_qkv[0].astype(BF16)
    kvw_l = kv_w.astype(BF16)
    core = lax.axis_index("c").astype(jnp.int32).reshape(1)
    chip = (2 * lax.axis_index("x") + lax.axis_index("y")).astype(jnp.int32).reshape(1)
    early = [w_in_l[:1], w_out_l[:1], sq_l, qkv_l[None]]
    late = [w_in_l[1:], w_out_l[1:], kvw_l[None]]
    early_lands = [place_own_shard(f"own_early_{t}", s, chip) for t, s in enumerate(early)]
    late_lands = [place_own_shard(f"own_late_{t}", s, chip) for t, s in enumerate(late)]
    w_in0, w_out0, w_sq, w_qkv = all_gather_weights(early, early_lands)
    w_qkv = w_qkv.reshape(N_CHIPS, D_MODEL, QKV_COLS)

    def ffn_w(slot):
        return (w_in0, w_out0, 0) if slot == 0 else (w_in_r, w_out_r, slot - 1)

    def vec(a, i):
        return a[i].reshape(1, D_MODEL)

    ident = lambda w: w
    sq_prep = lambda w: w.reshape(D_MODEL, w.shape[-1])
    qscale = jnp.concatenate([jnp.full((1, D_MODEL), ATTN_SCALE, F32), jnp.ones((1, 2 * D_MODEL), F32)], axis=1)
    swa_scale = jnp.full((1, D_MODEL), ATTN_SCALE, F32)
    sinks = swa_sinks.reshape(SWA_Q_HEADS)

    h1, gate1, up1 = ffn_fwd("l0a", x2, vec(ffn1_norm, 0), *ffn_w(SLOT_FFN1[0]))
    qkv = qkv_fwd(h1, vec(mix_norm, 0), w_qkv, qscale)
    o_sb, tot, sb_first, late_lands = sb_fwd(qkv, late, late_lands)
    h2, (w_in_r, w_out_r, w_kv) = linear_res("sb_out", o_sb, w_sq, SQ_SB_O, h1, late_lands)
    w_kv = w_kv.reshape(D_MODEL, 2 * kv_cols)
    h3, gate2, up2 = ffn_fwd("l0b", h2, vec(ffn2_norm, 0), *ffn_w(SLOT_FFN2[0]))
    kvn = kv_norm.reshape(1, D_MODEL)
    k_sw = rms_linear("kv_k", h3, kvn, w_kv, pl.BlockSpec((D_MODEL, kv_cols), lambda i, j: (0, 0)), ident,
                      kv_cols, kv_cols, rope=(cos, sin))
    v_sw = rms_linear("kv_v", h3, kvn, w_kv, pl.BlockSpec((D_MODEL, kv_cols), lambda i, j: (0, 1)), ident,
                      kv_cols, kv_cols)
    h4, gate3, up3 = ffn_fwd("l1a", h3, vec(ffn1_norm, 1), *ffn_w(SLOT_FFN1[1]))
    q_sw = rms_linear("swa_q", h4, vec(mix_norm, 1), w_sq,
                      pl.BlockSpec((N_CHIPS, None, SQ_ROWS, 512), lambda i, j: (0, SQ_SWA_Q, 0, j)), sq_prep,
                      D_MODEL, 512, rope=(cos, sin), scale=swa_scale)
    o_sw, lse = swa_fwd(q_sw, k_sw, v_sw, sinks)
    h5 = linear_res("swa_out", o_sw, w_sq, SQ_SWA_O, h4)
    h6, gate4, up4 = ffn_fwd("l1b", h5, vec(ffn2_norm, 1), *ffn_w(SLOT_FFN2[1]))
    dh6, loss_p, d_final = loss_bwd(h6, final_norm.reshape(1, D_MODEL), tgt)

    slab = {}
    in_shape = (2, N_CHIPS, 2, D_MODEL // 2, FF_CHUNK)
    out_shape = (2, N_CHIPS, 2, FF_ROWS, D_MODEL // 2)
    sq_slots = {SQ_SB_O: (0, 0, 1), SQ_SWA_Q: (1, 0, 2), SQ_SWA_O: (1, 1, 2)}

    def ffn_grads(tag, dh, h_in, g, gate, up, slot):
        dh_in, xn, dg_, du_, act, dhb, dnorm = ffn_bwd(tag, dh, h_in, g, gate, up, *ffn_w(slot))
        ffn, layer = slot // 2, slot % 2
        blk = (None, 1, None, D_MODEL // 2, FF_CHUNK)
        slab["in", layer] = mm_tn(f"dw_gate_{tag}", xn, dg_, D_MODEL // 2, FF_CHUNK, blk,
                                  lambda k, n: (k, n, ffn, 0, 0), in_shape, prev=slab.get(("in", layer)))
        slab["in", layer] = mm_tn(f"dw_up_{tag}", xn, du_, D_MODEL // 2, FF_CHUNK, blk,
                                  lambda k, n: (k, 2 + n, ffn, 0, 0), in_shape, prev=slab["in", layer])
        slab["out", layer] = mm_tn(f"dw_out_{tag}", act, dhb, FF_CHUNK, D_MODEL // 2,
                                   (None, 2, None, FF_ROWS, D_MODEL // 2),
                                   lambda k, n: (n, k, ffn, 0, 0), out_shape, prev=slab.get(("out", layer)))
        return dh_in, dnorm

    def sq_grad(tag, a, dyb, t):
        layer, s, ns = sq_slots[t]
        slab["sq", layer] = mm_tn(f"dw_sq_{tag}", a, dyb, D_MODEL, D_MODEL // 2,
                                  (None, N_CHIPS, None, SQ_ROWS, D_MODEL // 2),
                                  lambda k, n: (n, 0, s, 0, 0), (2, N_CHIPS, ns, SQ_ROWS, D_MODEL // 2),
                                  prev=slab.get(("sq", layer)))

    def reduce_layer(layer, kinds, host=None):
        slabs = [slab[kind, layer] for kind in kinds]
        names = [f"{kind}{layer}" for kind in kinds]
        theirs = exchange_halves(f"exchange_halves_{layer}", slabs)
        parts = [add_sibling(f"add_sibling_{nm}", s, t, core) for nm, s, t in zip(names, slabs, theirs)]
        arrived = host(parts) if host else exchange_chip_partials(f"exchange_chip_partials_{layer}", parts)
        halves = [sum_chips(f"sum_chips_{nm}", g, p, chip) for nm, g, p in zip(names, arrived, parts)]
        sib_halves = share_reduced_halves(f"share_reduced_halves_{layer}", halves)
        return {kind: pair for kind, pair in zip(kinds, zip(halves, sib_halves))}

    dh5, d_ffn2_1 = ffn_grads("l1b", dh6, h5, vec(ffn2_norm, 1), gate4, up4, SLOT_FFN2[1])
    do_sw, dh5b = linear_bwd_plain("swa_out_bwd", dh5, w_sq, SQ_SWA_O)
    sq_grad("swa_o", o_sw, dh5b, SQ_SWA_O)
    dq_sw, kv_own, kv_prev, d_sinks = swa_bwd(q_sw, k_sw, v_sw, sinks, do_sw, o_sw, lse, cos, sin)
    sq_w_spec = pl.BlockSpec((N_CHIPS, None, SQ_ROWS, D_MODEL), lambda i, j: (0, SQ_SWA_Q, 0, 0))
    dh4, hn4, d_mix_1 = linear_bwd_rms("swa_q_bwd", [(dq_sw, w_sq, sq_w_spec, sq_prep)], h4, vec(mix_norm, 1), dh5,
                                       1, D_MODEL)
    sq_grad("swa_q", hn4, dq_sw, SQ_SWA_Q)
    dh3a, d_ffn1_1 = ffn_grads("l1a", dh4, h3, vec(ffn1_norm, 1), gate3, up3, SLOT_FFN1[1])
    dkv = kv_grad_combine(kv_own, kv_prev, cos, sin)
    kv_w_spec = pl.BlockSpec((D_MODEL, 2 * kv_cols), lambda i, j: (0, 0))
    dh3, xn3, d_kvn = linear_bwd_rms("kv_bwd", [(dkv, w_kv, kv_w_spec, ident)], h3, kvn, dh3a, 1, 2 * kv_cols)
    slab["kv", 1] = mm_tn("dw_kv", xn3, dkv, D_MODEL, kv_cols, (None, N_CHIPS, None, SQ_ROWS, kv_cols),
                          lambda k, n: (n, 0, 0, 0, 0), (2, N_CHIPS, 1, SQ_ROWS, kv_cols))
    dh2, d_ffn2_0 = ffn_grads("l0b", dh3, h2, vec(ffn2_norm, 0), gate2, up2, SLOT_FFN2[0])
    do_sb, dh2b = linear_bwd_plain("sb_out_bwd", dh2, w_sq, SQ_SB_O)
    sq_grad("sb_o", o_sb, dh2b, SQ_SB_O)
    sb_grads = []

    def behind_sb_bwd(parts):
        dq_sb, dk_sb, dv_sb, arrived = sb_bwd(qkv, do_sb, tot, sb_first, parts)
        sb_grads.extend([dq_sb, dk_sb, dv_sb])
        return arrived

    g1 = reduce_layer(1, ["in", "out", "sq", "kv"], host=behind_sb_bwd)
    dqkv = jnp.concatenate(sb_grads, axis=1)
    dh1, hn1, d_mix_0 = qkv_bwd(dqkv, w_qkv, h1, vec(mix_norm, 0), dh2)
    slab["qkv", 0] = mm_tn("dw_qkv", hn1, dqkv, D_MODEL // 2, QKV_COLS, (None, 1, None, D_MODEL // 2, QKV_COLS),
                           lambda k, n: (k, n, 0, 0, 0), (2, N_CHIPS, 1, D_MODEL // 2, QKV_COLS))
    dx, d_ffn1_0 = ffn_grads("l0a", dh1, x2, vec(ffn1_norm, 0), gate1, up1, SLOT_FFN1[0])
    g0 = reduce_layer(0, ["in", "out", "sq", "qkv"])

    def upd(name, w, m, v, g_pairs, slot, row_halves):
        shp = w.shape
        w3 = w.reshape((-1,) + shp[-2:])
        outs = adamw_shard(name, w3, m.reshape(w3.shape), v.reshape(w3.shape), g_pairs, core, slot, row_halves)
        return [o.reshape(shp) for o in outs]

    r_ffn1_in = upd("adamw_ffn1_in", ffn1_w_in, m_ffn1_w_in, v_ffn1_w_in, [g0["in"], g1["in"]], 0, True)
    r_ffn2_in = upd("adamw_ffn2_in", ffn2_w_in, m_ffn2_w_in, v_ffn2_w_in, [g0["in"], g1["in"]], 1, True)
    r_ffn1_out = upd("adamw_ffn1_out", ffn1_w_out, m_ffn1_w_out, v_ffn1_w_out, [g0["out"], g1["out"]], 0, False)
    r_ffn2_out = upd("adamw_ffn2_out", ffn2_w_out, m_ffn2_w_out, v_ffn2_w_out, [g0["out"], g1["out"]], 1, False)
    r_qkv = upd("adamw_qkv", sb_w_qkv, m_sb_w_qkv, v_sb_w_qkv, [g0["qkv"]], 0, True)
    r_sb_o = upd("adamw_sb_o", sb_w_o, m_sb_w_o, v_sb_w_o, [g0["sq"]], sq_slots[SQ_SB_O][1], False)
    r_swa_q = upd("adamw_swa_q", swa_w_q, m_swa_w_q, v_swa_w_q, [g1["sq"]], sq_slots[SQ_SWA_Q][1], False)
    r_swa_o = upd("adamw_swa_o", swa_w_o, m_swa_w_o, v_swa_w_o, [g1["sq"]], sq_slots[SQ_SWA_O][1], False)
    r_kv = upd("adamw_kv", kv_w, m_kv_w, v_kv_w, [g1["kv"]], 0, False)

    loss_row = jnp.pad(loss_p, ((0, 0), (0, D_MODEL - LANES)))
    d_sink_row = d_sinks[0, :SWA_Q_HEADS]
    part = _pack_small(jnp.concatenate([d_ffn1_0, d_ffn1_1], axis=0), jnp.concatenate([d_mix_0, d_mix_1], axis=0),
                       jnp.concatenate([d_ffn2_0, d_ffn2_1], axis=0), d_kvn, d_final, d_sink_row, loss_row)
    zrow = jnp.zeros((1, D_MODEL), F32)
    small = small_allreduce_adamw(
        part,
        _pack_small(ffn1_norm, mix_norm, ffn2_norm, kv_norm, final_norm, swa_sinks, zrow),
        _pack_small(m_ffn1_norm, m_mix_norm, m_ffn2_norm, m_kv_norm, m_final_norm, m_swa_sinks, zrow),
        _pack_small(v_ffn1_norm, v_mix_norm, v_ffn2_norm, v_kv_norm, v_final_norm, v_swa_sinks, zrow))

    def unpack(p):
        return dict(ffn1_norm=p[0:2], mix_norm=p[2:4], ffn2_norm=p[4:6], kv_norm=p[6], final_norm=p[7],
                    swa_sinks=p[8:9, :SWA_Q_HEADS])

    big = dict(ffn1_w_in=r_ffn1_in, ffn1_w_out=r_ffn1_out, ffn2_w_in=r_ffn2_in, ffn2_w_out=r_ffn2_out,
               sb_w_qkv=r_qkv, sb_w_o=r_sb_o, kv_w=r_kv, swa_w_q=r_swa_q, swa_w_o=r_swa_o)
    order = ["ffn1_norm", "ffn1_w_in", "ffn1_w_out", "mix_norm", "ffn2_norm", "ffn2_w_in", "ffn2_w_out",
             "sb_w_qkv", "sb_w_o", "kv_norm", "kv_w", "swa_w_q", "swa_sinks", "swa_w_o", "final_norm"]
    outs = []
    for kind in range(4):
        sm = unpack(small[kind])
        for nm in order:
            outs.append(big[nm][kind] if nm in big else sm[nm])
    loss = small[0][9, 0]
    return (loss, dx.reshape(x.shape), *outs)
```

```python
import functools

import jax
import jax.numpy as jnp
from jax import lax
from jax.experimental import pallas as pl
from jax.experimental.pallas import tpu as pltpu

F32 = jnp.float32
BF16 = jnp.bfloat16
MESH = pl.DeviceIdType.MESH

D_MODEL = 1024
D_FF = 2816
HEAD_DIM = 64
SB_HEADS = 16
SWA_Q_HEADS = 16
SWA_KV_HEADS = 4
WINDOW = 128
ROPE_THETA = 10000.0
RMS_EPS = 1e-6
FFN_RES_SCALE = 0.5
ATTN_SCALE = HEAD_DIM ** -0.5

ADAM_LR = 0.001
ADAM_B1 = 0.9
ADAM_B2 = 0.999
ADAM_EPS = 1e-08
ADAM_WD = 0.01
ADAM_STEP = 10

N_CHIPS = 4
N_DEV = 8
LANES = 128
FF_CHUNK = D_FF // 2
FF_ROWS = D_FF // N_CHIPS
SQ_ROWS = D_MODEL // N_CHIPS
QKV_COLS = 3 * D_MODEL // N_CHIPS
VMEM_LIMIT = 56 * 1024 * 1024
NEG_BIG = -1e30

SLOT_FFN1 = (0, 1)
SLOT_FFN2 = (2, 3)
SQ_SB_O, SQ_SWA_Q, SQ_SWA_O = 0, 1, 2


def _cparams():
    return pltpu.CompilerParams(vmem_limit_bytes=VMEM_LIMIT)


def _dot(a, b):
    return jnp.dot(a, b, preferred_element_type=F32)


def _dot_nt(a, b):
    return lax.dot_general(a, b, (((1,), (1,)), ((), ())), preferred_element_type=F32)


def _dot_tn(a, b):
    return lax.dot_general(a, b, (((0,), (0,)), ((), ())), preferred_element_type=F32)


def _rstd(h):
    return lax.rsqrt(jnp.mean(h * h, axis=-1, keepdims=True) + RMS_EPS)


def _swap32(x):
    n = x.shape[-1]
    lane = lax.broadcasted_iota(jnp.int32, x.shape, x.ndim - 1)
    first = (lane % HEAD_DIM) < (HEAD_DIM // 2)
    return jnp.where(first, pltpu.roll(x, n - HEAD_DIM // 2, x.ndim - 1), pltpu.roll(x, HEAD_DIM // 2, x.ndim - 1))


def _tile_lanes(t, n):
    return t if n == LANES else jnp.tile(t, (1, n // LANES))


FFN_ROWS = 256


def _ffn_w_in_spec(slot):
    return pl.BlockSpec((N_CHIPS, None, D_MODEL, FF_CHUNK), lambda i: (0, slot, 0, 0), pipeline_mode=pl.Buffered(1))


def _ffn_w_out_spec(slot):
    return pl.BlockSpec((N_CHIPS, None, FF_ROWS, D_MODEL), lambda i: (0, slot, 0, 0), pipeline_mode=pl.Buffered(1))


def ffn_fwd(tag, h, g, w_in, w_out, slot):
    T = h.shape[0]
    tm = FFN_ROWS
    nch = D_FF // FF_CHUNK

    def body(h_ref, g_ref, wi_ref, wo_ref, out_ref, gate_ref, up_ref):
        hh = h_ref[...]
        xn = (hh * _rstd(hh) * g_ref[...]).astype(BF16)
        acc = None
        for j in range(nch):
            cols = slice(j * FF_CHUNK, (j + 1) * FF_CHUNK)
            gate = _dot(xn, wi_ref[j])
            up = _dot(xn, wi_ref[nch + j])
            gate_ref[:, cols] = gate.astype(BF16)
            up_ref[:, cols] = up.astype(BF16)
            a = (gate * jax.nn.sigmoid(gate) * up).astype(BF16)
            part = _dot(a, wo_ref[2 * j:2 * j + 2].reshape(FF_CHUNK, D_MODEL))
            acc = part if acc is None else acc + part
        out_ref[...] = hh + FFN_RES_SCALE * acc

    row = pl.BlockSpec((tm, D_MODEL), lambda i: (i, 0))
    ff = pl.BlockSpec((tm, D_FF), lambda i: (i, 0))
    return pl.pallas_call(
        body,
        name=f"ffn_fwd_{tag}",
        grid=(T // tm,),
        in_specs=[row, pl.BlockSpec((1, D_MODEL), lambda i: (0, 0)), _ffn_w_in_spec(slot), _ffn_w_out_spec(slot)],
        out_specs=[row, ff, ff],
        out_shape=[
            jax.ShapeDtypeStruct((T, D_MODEL), F32),
            jax.ShapeDtypeStruct((T, D_FF), BF16),
            jax.ShapeDtypeStruct((T, D_FF), BF16),
        ],
        compiler_params=_cparams(),
    )(h, g, w_in, w_out)


def ffn_bwd(tag, dh, h, g, gate, up, w_in, w_out, slot):
    T = dh.shape[0]
    tm = FFN_ROWS
    nch = D_FF // FF_CHUNK

    def body(dh_ref, h_ref, g_ref, gate_ref, up_ref, wi_ref, wo_ref,
             dhin_ref, xn_ref, dg_ref, du_ref, a_ref, dhb_ref, dnorm_ref):
        @pl.when(pl.program_id(0) == 0)
        def _():
            dnorm_ref[...] = jnp.zeros_like(dnorm_ref)

        dhh = dh_ref[...]
        dhb = (FFN_RES_SCALE * dhh).astype(BF16)
        dhb_ref[...] = dhb
        dxn = None
        for j in range(nch):
            cols = slice(j * FF_CHUNK, (j + 1) * FF_CHUNK)
            da = _dot_nt(dhb, wo_ref[2 * j:2 * j + 2].reshape(FF_CHUNK, D_MODEL))
            gt = gate_ref[:, cols].astype(F32)
            u = up_ref[:, cols].astype(F32)
            s = jax.nn.sigmoid(gt)
            silu = gt * s
            a_ref[:, cols] = (silu * u).astype(BF16)
            dgate = (da * u * (s * (1.0 + gt * (1.0 - s)))).astype(BF16)
            dup = (da * silu).astype(BF16)
            dg_ref[:, cols] = dgate
            du_ref[:, cols] = dup
            part = _dot_nt(dgate, wi_ref[j]) + _dot_nt(dup, wi_ref[nch + j])
            dxn = part if dxn is None else dxn + part
        hh = h_ref[...]
        gg = g_ref[...]
        r = _rstd(hh)
        hr = hh * r
        xn_ref[...] = (hr * gg).astype(BF16)
        dnorm_ref[...] += jnp.sum(dxn * hr, axis=0, keepdims=True)
        gd = gg * dxn
        dhin_ref[...] = dhh + r * (gd - hr * jnp.mean(gd * hr, axis=-1, keepdims=True))

    row = pl.BlockSpec((tm, D_MODEL), lambda i: (i, 0))
    ff = pl.BlockSpec((tm, D_FF), lambda i: (i, 0))
    vec = pl.BlockSpec((1, D_MODEL), lambda i: (0, 0))
    return pl.pallas_call(
        body,
        name=f"ffn_bwd_{tag}",
        grid=(T // tm,),
        in_specs=[row, row, vec, ff, ff, _ffn_w_in_spec(slot), _ffn_w_out_spec(slot)],
        out_specs=[row, row, ff, ff, ff, row, vec],
        out_shape=[
            jax.ShapeDtypeStruct((T, D_MODEL), F32),
            jax.ShapeDtypeStruct((T, D_MODEL), BF16),
            jax.ShapeDtypeStruct((T, D_FF), BF16),
            jax.ShapeDtypeStruct((T, D_FF), BF16),
            jax.ShapeDtypeStruct((T, D_FF), BF16),
            jax.ShapeDtypeStruct((T, D_MODEL), BF16),
            jax.ShapeDtypeStruct((1, D_MODEL), F32),
        ],
        compiler_params=_cparams(),
    )(dh, h, g, gate, up, w_in, w_out)


def rms_linear(name, h, g, w, w_spec, w_prep, n_out, tn, *, rope=None, scale=None):
    T = h.shape[0]
    tm = 512
    extra, extra_specs = [], []
    if rope is not None:
        extra += list(rope)
        extra_specs += [pl.BlockSpec((tm, LANES), lambda i, j: (i, 0))] * 2
    if scale is not None:
        extra.append(scale)
        extra_specs.append(pl.BlockSpec((1, tn), lambda i, j: (0, j)))

    def body(h_ref, g_ref, w_ref, *rest):
        rest = list(rest)
        cos_ref = sin_ref = sc_ref = None
        if rope is not None:
            cos_ref, sin_ref = rest[0], rest[1]
            rest = rest[2:]
        if scale is not None:
            sc_ref = rest[0]
            rest = rest[1:]
        out_ref, xn_s = rest

        @pl.when(pl.program_id(1) == 0)
        def _():
            hh = h_ref[...]
            xn_s[...] = (hh * _rstd(hh) * g_ref[...]).astype(BF16)

        y = _dot(xn_s[...], w_prep(w_ref[...]))
        if rope is not None:
            y = y * _tile_lanes(cos_ref[...], tn) + _swap32(y) * _tile_lanes(sin_ref[...], tn)
        if scale is not None:
            y = y * sc_ref[...]
        out_ref[...] = y.astype(BF16)

    return pl.pallas_call(
        body,
        name=name,
        grid=(T // tm, n_out // tn),
        in_specs=[
            pl.BlockSpec((tm, D_MODEL), lambda i, j: (i, 0)),
            pl.BlockSpec((1, D_MODEL), lambda i, j: (0, 0)),
            w_spec,
        ] + extra_specs,
        out_specs=pl.BlockSpec((tm, tn), lambda i, j: (i, j)),
        out_shape=jax.ShapeDtypeStruct((T, n_out), BF16),
        scratch_shapes=[pltpu.VMEM((tm, D_MODEL), BF16)],
        compiler_params=_cparams(),
    )(h, g, w, *extra)


QKV_ROWS = 512


def _qkv_w_spec():
    return pl.BlockSpec((N_CHIPS, D_MODEL, QKV_COLS), lambda i: (0, 0, 0), pipeline_mode=pl.Buffered(1))


def qkv_fwd(h, g, w_qkv, scale):
    T = h.shape[0]
    tm = QKV_ROWS

    def body(h_ref, g_ref, w_ref, sc_ref, out_ref):
        hh = h_ref[...]
        xn = (hh * _rstd(hh) * g_ref[...]).astype(BF16)
        for s in range(N_CHIPS):
            cols = slice(s * QKV_COLS, (s + 1) * QKV_COLS)
            out_ref[:, cols] = (_dot(xn, w_ref[s]) * sc_ref[:, cols]).astype(BF16)

    return pl.pallas_call(
        body,
        name="sb_qkv",
        grid=(T // tm,),
        in_specs=[
            pl.BlockSpec((tm, D_MODEL), lambda i: (i, 0)),
            pl.BlockSpec((1, D_MODEL), lambda i: (0, 0)),
            _qkv_w_spec(),
            pl.BlockSpec((1, 3 * D_MODEL), lambda i: (0, 0)),
        ],
        out_specs=pl.BlockSpec((tm, 3 * D_MODEL), lambda i: (i, 0)),
        out_shape=jax.ShapeDtypeStruct((T, 3 * D_MODEL), BF16),
        compiler_params=_cparams(),
    )(h, g, w_qkv, scale)


def qkv_bwd(dy, w_qkv, h, g, dres):
    T = h.shape[0]
    tm = QKV_ROWS

    def body(dy_ref, w_ref, h_ref, g_ref, dres_ref, dh_ref, xn_ref, dg_ref):
        @pl.when(pl.program_id(0) == 0)
        def _():
            dg_ref[...] = jnp.zeros_like(dg_ref)

        dxn = None
        for s in range(N_CHIPS):
            part = _dot_nt(dy_ref[:, s * QKV_COLS:(s + 1) * QKV_COLS], w_ref[s])
            dxn = part if dxn is None else dxn + part
        hh = h_ref[...]
        gg = g_ref[...]
        r = _rstd(hh)
        hr = hh * r
        xn_ref[...] = (hr * gg).astype(BF16)
        dg_ref[...] += jnp.sum(dxn * hr, axis=0, keepdims=True)
        gd = gg * dxn
        dh_ref[...] = dres_ref[...] + r * (gd - hr * jnp.mean(gd * hr, axis=-1, keepdims=True))

    row = pl.BlockSpec((tm, D_MODEL), lambda i: (i, 0))
    vec = pl.BlockSpec((1, D_MODEL), lambda i: (0, 0))
    return pl.pallas_call(
        body,
        name="sb_qkv_bwd",
        grid=(T // tm,),
        in_specs=[pl.BlockSpec((tm, 3 * D_MODEL), lambda i: (i, 0)), _qkv_w_spec(), row, vec, row],
        out_specs=[row, row, vec],
        out_shape=[
            jax.ShapeDtypeStruct((T, D_MODEL), F32),
            jax.ShapeDtypeStruct((T, D_MODEL), BF16),
            jax.ShapeDtypeStruct((1, D_MODEL), F32),
        ],
        compiler_params=_cparams(),
    )(dy, w_qkv, h, g, dres)


def linear_res(name, a, w_sq, t, res, bg_lands=()):
    T = a.shape[0]
    tm = 512
    nbg = len(bg_lands)
    nt = T // tm

    def body(a_ref, w_ref, res_ref, *rest):
        out_ref = rest[nbg]
        if nbg:
            gather = GatherOps([l.shape[2] for l in bg_lands], None, rest[nbg + 1:2 * nbg + 1], None, None,
                               *rest[2 * nbg + 1:])

            @pl.when(pl.program_id(0) == 0)
            def _():
                gather.start_forwards()

        out_ref[...] = res_ref[...] + _dot(a_ref[...], w_ref[...].reshape(D_MODEL, D_MODEL))
        if nbg:
            @pl.when(pl.program_id(0) == nt - 1)
            def _():
                gather.wait_forwards()

    row = pl.BlockSpec((tm, D_MODEL), lambda i: (i, 0))
    res_ = pl.pallas_call(
        body,
        name=name,
        grid=(nt,),
        in_specs=[row, pl.BlockSpec((N_CHIPS, None, SQ_ROWS, D_MODEL), lambda i: (0, t, 0, 0)), row] + [ANY] * nbg,
        out_specs=[row] + [ANY] * nbg,
        out_shape=[jax.ShapeDtypeStruct((T, D_MODEL), F32)] + [jax.ShapeDtypeStruct(l.shape, l.dtype) for l in bg_lands],
        input_output_aliases={3 + k: 1 + k for k in range(nbg)},
        scratch_shapes=[pltpu.SemaphoreType.DMA((nbg, N_PEER_CHIPS))] * (2 if nbg else 0),
        compiler_params=_cparams(),
    )(a, w_sq, res, *bg_lands)
    return (res_[0], list(res_[1:])) if nbg else res_[0]


def linear_bwd_plain(name, dy, w_sq, t):
    T = dy.shape[0]
    tm = 512

    def body(dy_ref, w_ref, da_ref, dyb_ref):
        dyb = dy_ref[...].astype(BF16)
        dyb_ref[...] = dyb
        da_ref[...] = _dot_nt(dyb, w_ref[...].reshape(D_MODEL, D_MODEL)).astype(BF16)

    row = pl.BlockSpec((tm, D_MODEL), lambda i: (i, 0))
    return pl.pallas_call(
        body,
        name=name,
        grid=(T // tm,),
        in_specs=[row, pl.BlockSpec((N_CHIPS, None, SQ_ROWS, D_MODEL), lambda i: (0, t, 0, 0))],
        out_specs=[row, row],
        out_shape=[jax.ShapeDtypeStruct((T, D_MODEL), BF16), jax.ShapeDtypeStruct((T, D_MODEL), BF16)],
        compiler_params=_cparams(),
    )(dy, w_sq)


def linear_bwd_rms(name, pairs, h, g, dres, nch, tn, tm=256):
    T = h.shape[0]
    npair = len(pairs)

    def body(*refs):
        dy_refs = refs[:npair]
        w_refs = refs[npair:2 * npair]
        h_ref, g_ref, dres_ref, dh_ref, xn_ref, dg_ref, acc_s = refs[2 * npair:]
        i = pl.program_id(0)
        j = pl.program_id(1)

        @pl.when(j == 0)
        def _():
            acc_s[...] = jnp.zeros_like(acc_s)

        @pl.when((i == 0) & (j == 0))
        def _():
            dg_ref[...] = jnp.zeros_like(dg_ref)

        part = None
        for p in range(npair):
            d = _dot_nt(dy_refs[p][...], pairs[p][3](w_refs[p][...]))
            part = d if part is None else part + d
        acc_s[...] += part

        @pl.when(j == nch - 1)
        def _():
            dxn = acc_s[...]
            hh = h_ref[...]
            gg = g_ref[...]
            r = _rstd(hh)
            hr = hh * r
            xn_ref[...] = (hr * gg).astype(BF16)
            dg_ref[...] += jnp.sum(dxn * hr, axis=0, keepdims=True)
            gd = gg * dxn
            dh_ref[...] = dres_ref[...] + r * (gd - hr * jnp.mean(gd * hr, axis=-1, keepdims=True))

    row = pl.BlockSpec((tm, D_MODEL), lambda i, j: (i, 0))
    vec = pl.BlockSpec((1, D_MODEL), lambda i, j: (0, 0))
    return pl.pallas_call(
        body,
        name=name,
        grid=(T // tm, nch),
        in_specs=[pl.BlockSpec((tm, tn), lambda i, j: (i, j))] * npair + [p[2] for p in pairs] + [row, vec, row],
        out_specs=[row, row, vec],
        out_shape=[
            jax.ShapeDtypeStruct((T, D_MODEL), F32),
            jax.ShapeDtypeStruct((T, D_MODEL), BF16),
            jax.ShapeDtypeStruct((1, D_MODEL), F32),
        ],
        scratch_shapes=[pltpu.VMEM((tm, D_MODEL), F32)],
        compiler_params=_cparams(),
    )(*[p[0] for p in pairs], *[p[1] for p in pairs], h, g, dres)


def loss_bwd(h, g, tgt):
    T = h.shape[0]
    tm = 512

    def body(h_ref, g_ref, t_ref, dh_ref, loss_ref, dg_ref):
        @pl.when(pl.program_id(0) == 0)
        def _():
            loss_ref[...] = jnp.zeros_like(loss_ref)
            dg_ref[...] = jnp.zeros_like(dg_ref)

        hh = h_ref[...]
        gg = g_ref[...]
        r = _rstd(hh)
        hr = hh * r
        err = hr * gg - t_ref[...]
        loss_ref[...] += 0.5 * jnp.sum(jnp.mean(err * err, axis=-1, keepdims=True), axis=0, keepdims=True)
        dy = err * (1.0 / D_MODEL)
        dg_ref[...] += jnp.sum(dy * hr, axis=0, keepdims=True)
        gd = gg * dy
        dh_ref[...] = r * (gd - hr * jnp.mean(gd * hr, axis=-1, keepdims=True))

    row = pl.BlockSpec((tm, D_MODEL), lambda i: (i, 0))
    vec = pl.BlockSpec((1, D_MODEL), lambda i: (0, 0))
    return pl.pallas_call(
        body,
        name="loss_bwd",
        grid=(T // tm,),
        in_specs=[row, vec, row],
        out_specs=[row, pl.BlockSpec((1, LANES), lambda i: (0, 0)), vec],
        out_shape=[
            jax.ShapeDtypeStruct((T, D_MODEL), F32),
            jax.ShapeDtypeStruct((1, LANES), F32),
            jax.ShapeDtypeStruct((1, D_MODEL), F32),
        ],
        compiler_params=_cparams(),
    )(h, g, tgt)


DW_TOKENS = 4096


def mm_tn(name, a, b, tk, tn, out_block, out_index, out_shape, prev=None, tt=DW_TOKENS):
    T = a.shape[0]
    ns, r = out_block[1], out_block[3]
    tt = min(tt, T)
    nt = T // tt

    def body(*refs):
        if prev is None:
            a_ref, b_ref, out_ref = refs
        else:
            a_ref, b_ref, _, out_ref = refs
        t = pl.program_id(2)
        res = _dot_tn(a_ref[...], b_ref[...])

        @pl.when(t == 0)
        def _():
            for u in range(ns):
                out_ref[u] = res[u * r:(u + 1) * r]

        @pl.when(t > 0)
        def _():
            for u in range(ns):
                out_ref[u] += res[u * r:(u + 1) * r]

    in_specs = [
        pl.BlockSpec((tt, tk), lambda k, n, t: (t, k)),
        pl.BlockSpec((tt, tn), lambda k, n, t: (t, n)),
    ]
    args = [a, b]
    aliases = {}
    if prev is not None:
        in_specs.append(pl.BlockSpec(memory_space=pl.ANY))
        args.append(prev)
        aliases = {2: 0}
    return pl.pallas_call(
        body,
        name=name,
        grid=(a.shape[1] // tk, b.shape[1] // tn, nt),
        in_specs=in_specs,
        out_specs=pl.BlockSpec(out_block, lambda k, n, t: out_index(k, n)),
        out_shape=jax.ShapeDtypeStruct(out_shape, F32),
        input_output_aliases=aliases,
        compiler_params=_cparams(),
    )(*args)


SB_BLOCK = 256
SB_QROWS = 256
SB_QROWS_BWD = 256
SB_UNDERFLOW_BITS = 140.0
SB_CHUNK = 128


LOG2E = 1.4426950408889634


def _softplus2(z2):
    sign = jnp.uint32(0x80000000)
    neg_abs = lax.bitcast_convert_type(lax.bitcast_convert_type(z2, jnp.uint32) | sign, F32)
    return jnp.log2(1.0 + jnp.exp2(neg_abs)) + jnp.maximum(z2, 0.0)


def _twice(x):
    return jnp.concatenate([x, x], axis=1)


def sb_fwd(qkv, bg_shards=(), bg_lands=()):
    T = qkv.shape[0]
    tq, tk = SB_QROWS, SB_BLOCK
    ratio = tq // tk
    npair = SB_HEADS // 2
    nbg = len(bg_shards)
    nq = T // tq

    def body(q_ref, k_ref, v_ref, *rest):
        bg_in = rest[:nbg]
        o_ref, tot_ref, first_ref = rest[2 * nbg:2 * nbg + 3]
        bg_out = rest[2 * nbg + 3:3 * nbg + 3]
        acc_s, c_s, z_s, w_s, kmax_s = rest[3 * nbg + 3:3 * nbg + 8]
        p = pl.program_id(0)
        i = pl.program_id(1)
        if nbg:
            gather = GatherOps([s.shape[1] for s in bg_shards], bg_in, bg_out, *rest[3 * nbg + 8:])

            @pl.when((p == 0) & (i == 0))
            def _():
                gather.start_ici()

        @pl.when(i == 0)
        def _():
            kmax_s[...] = jnp.max(jnp.abs(k_ref[...]), axis=0, keepdims=True).astype(F32)

        q = q_ref[...]
        lane = lax.broadcasted_iota(jnp.int32, (tq, LANES), 1)
        first = lane < HEAD_DIM
        zero = jnp.zeros_like(q)
        q_heads = (jnp.where(first, q, zero), jnp.where(first, zero, q))
        row = lax.broadcasted_iota(jnp.int32, (tq, tk), 0)
        col = lax.broadcasted_iota(jnp.int32, (tq, tk), 1)
        visible = [col + r * tk < row for r in range(ratio)]
        krow = lax.broadcasted_iota(jnp.int32, (tk, tk), 0)
        kcol = lax.broadcasted_iota(jnp.int32, (tk, tk), 1)
        from_s = (krow >= kcol).astype(BF16)
        acc_s[...] = jnp.zeros_like(acc_s)
        c_s[...] = jnp.zeros_like(c_s)

        def rows(j):
            return pl.ds(pl.multiple_of(j * tk, tk), tk)

        def logits(j):
            kb = k_ref[rows(j), :]
            for hd in range(2):
                z_s[hd] = _dot_nt(q_heads[hd], kb) * LOG2E

        def flush(j):
            vb = v_ref[rows(j), :]
            for hd in range(2):
                acc_s[hd] += _dot(w_s[hd], vb)

        def block(j, mask=None, walked_before=True):
            if walked_before:
                flush(j + 1)
            chunks = [(hd, slice(r0, r0 + SB_CHUNK)) for hd in range(2) for r0 in range(0, tq, SB_CHUNK)]
            k_next = k_ref[rows(jnp.maximum(j - 1, 0)), :]
            es, sums = [], []
            for hd, rs in chunks:
                z2 = z_s[hd, rs, :]
                z_s[hd, rs, :] = _dot_nt(q_heads[hd][rs, :], k_next) * LOG2E
                if mask is not None:
                    z2 = jnp.where(mask[rs, :], z2, NEG_BIG)
                sp = _softplus2(z2)
                c = c_s[hd, rs, :]
                es.append(z2 + _twice(c))
                c_s[hd, rs, :] = c - jnp.sum(sp, axis=1, keepdims=True)
                sums.append(_dot(sp.astype(BF16), from_s))
            for (hd, rs), e, s in zip(chunks, es, sums):
                w_s[hd, rs, :] = jnp.exp2(e - s).astype(BF16)

        z_bound = [LOG2E * jnp.sum(jnp.abs(q_heads[hd].astype(F32)) * kmax_s[...], axis=1, keepdims=True)
                   for hd in range(2)]

        def more_keys_matter():
            top = jnp.maximum(c_s[0] + z_bound[0], c_s[1] + z_bound[1])
            return (jnp.max(top) >= -SB_UNDERFLOW_BITS).astype(jnp.int32)

        logits(ratio * i + ratio - 1)
        for r in reversed(range(ratio)):
            block(ratio * i + r, visible[r], walked_before=(r != ratio - 1))

        def trip(carry):
            trips, _ = carry
            for r in reversed(range(ratio)):
                block(ratio * (i - 1 - trips) + r)
            return trips + 1, more_keys_matter()

        trips, _ = lax.while_loop(lambda carry: jnp.logical_and(carry[0] < i, carry[1] > 0), trip,
                                  (jnp.int32(0), jnp.int32(1)))
        first_walked = ratio * (i - trips)
        flush(first_walked)
        first_ref[p, i] = first_walked.astype(F32)
        o_ref[...] = jnp.where(first, acc_s[0], acc_s[1]).astype(BF16)
        tot_ref[...] = jnp.where(first, c_s[0], c_s[1])
        if nbg:
            @pl.when((p == npair - 1) & (i == nq - 1))
            def _():
                gather.wait_ici()

    res = pl.pallas_call(
        body,
        name="sb_fwd",
        grid=(npair, nq),
        in_specs=[
            pl.BlockSpec((tq, LANES), lambda p, i: (i, p)),
            pl.BlockSpec((T, LANES), lambda p, i: (0, npair + p)),
            pl.BlockSpec((T, LANES), lambda p, i: (0, 2 * npair + p)),
        ] + [ANY] * (2 * nbg),
        out_specs=[pl.BlockSpec((tq, LANES), lambda p, i: (i, p))] * 2 + [pl.BlockSpec(memory_space=pltpu.SMEM)]
        + [ANY] * nbg,
        out_shape=[jax.ShapeDtypeStruct((T, D_MODEL), BF16), jax.ShapeDtypeStruct((T, D_MODEL), F32),
                   jax.ShapeDtypeStruct((npair, nq), F32)]
        + [jax.ShapeDtypeStruct(l.shape, l.dtype) for l in bg_lands],
        input_output_aliases={3 + nbg + t: 3 + t for t in range(nbg)},
        scratch_shapes=[
            pltpu.VMEM((2, tq, LANES), F32), pltpu.VMEM((2, tq, LANES), F32),
            pltpu.VMEM((2, tq, tk), F32), pltpu.VMEM((2, tq, tk), BF16),
            pltpu.VMEM((1, LANES), F32),
        ] + [pltpu.SemaphoreType.DMA((nbg, N_PEER_CHIPS))] * (2 if nbg else 0),
        compiler_params=_cparams(),
    )(qkv, qkv, qkv, *bg_shards, *bg_lands)
    return res[0], res[1], res[2], list(res[3:])


def sb_bwd(qkv, do, tot, first_block, bg_parts=()):
    T = qkv.shape[0]
    tq, tk = SB_QROWS_BWD, SB_BLOCK
    ratio = tq // tk
    npair = SB_HEADS // 2
    nq = T // tq
    nk = T // tk
    nbg = len(bg_parts)
    assert SB_QROWS == SB_QROWS_BWD

    def body(first_ref, q_ref, k_ref, v_ref, do_ref, tot_ref, *rest):
        bg_in = rest[:nbg]
        dq_ref, dk_ref, dv_ref = rest[nbg:nbg + 3]
        bg_out = rest[nbg + 3:2 * nbg + 3]
        dkt_s, dvt_s, dq_s, rest_s, cg_s, z_s, da_s, dz_s, a_s = rest[2 * nbg + 3:2 * nbg + 12]
        i = pl.program_id(1)
        start = jnp.clip(first_ref[pl.program_id(0), i].astype(jnp.int32), 0, ratio * i)
        if nbg:
            @pl.when((pl.program_id(0) == 0) & (i == 0))
            def _():
                for cp in chip_partial_copies(bg_in, bg_out, *rest[2 * nbg + 12:]):
                    cp.start()

        @pl.when(i == 0)
        def _():
            dkt_s[...] = jnp.zeros_like(dkt_s)
            dvt_s[...] = jnp.zeros_like(dvt_s)

        q = q_ref[...]
        do_ = do_ref[...]
        tot_ = tot_ref[...]
        q_t = q.astype(F32).T.astype(BF16)
        do_t = do_.astype(F32).T.astype(BF16)
        lane = lax.broadcasted_iota(jnp.int32, (tq, LANES), 1)
        first = lane < HEAD_DIM
        zero = jnp.zeros_like(q)
        q_heads = (jnp.where(first, q, zero), jnp.where(first, zero, q))
        do_heads = (jnp.where(first, do_, zero), jnp.where(first, zero, do_))
        row = lax.broadcasted_iota(jnp.int32, (tq, tk), 0)
        col = lax.broadcasted_iota(jnp.int32, (tq, tk), 1)
        visible = [col + r * tk < row for r in range(ratio)]
        krow = lax.broadcasted_iota(jnp.int32, (tk, tk), 0)
        kcol = lax.broadcasted_iota(jnp.int32, (tk, tk), 1)
        before = (krow < kcol).astype(BF16)
        from_s = (krow >= kcol).astype(BF16)
        last = ratio * i + ratio - 1
        rest_s[0] = jnp.broadcast_to(tot_[:, 0:1], (tq, LANES))
        rest_s[1] = jnp.broadcast_to(tot_[:, HEAD_DIM:HEAD_DIM + 1], (tq, LANES))
        cg_s[...] = jnp.zeros_like(cg_s)
        dq_s[...] = jnp.zeros_like(dq_s)
        dz_s[...] = jnp.zeros_like(dz_s)
        a_s[...] = jnp.zeros_like(a_s)

        def rows(j):
            return pl.ds(pl.multiple_of(j * tk, tk), tk)

        def logits(j):
            kb = k_ref[rows(j), :]
            vb = v_ref[rows(j), :]
            for hd in range(2):
                z_s[hd] = _dot_nt(q_heads[hd], kb) * LOG2E
                da_s[hd] = _dot_nt(do_heads[hd], vb)

        def flush(j):
            kb = k_ref[rows(j), :]
            for hd in range(2):
                dims = slice(hd * HEAD_DIM, (hd + 1) * HEAD_DIM)
                dq_s[hd] += _dot(dz_s[hd], kb)
                dkt_s[j, dims, :] += _dot(q_t[dims, :], dz_s[hd])
                dvt_s[j, dims, :] += _dot(do_t[dims, :], a_s[hd])

        def block(j, mask=None):
            flush(jnp.maximum(j - 1, 0))
            chunks = [(hd, slice(r0, r0 + SB_CHUNK)) for hd in range(2) for r0 in range(0, tq, SB_CHUNK)]
            nxt = rows(jnp.minimum(j + 1, last))
            k_next = k_ref[nxt, :]
            v_next = v_ref[nxt, :]
            stage1 = []
            for hd, rs in chunks:
                z2 = z_s[hd, rs, :]
                z_s[hd, rs, :] = _dot_nt(q_heads[hd][rs, :], k_next) * LOG2E
                if mask is not None:
                    z2 = jnp.where(mask[rs, :], z2, NEG_BIG)
                sp = _softplus2(z2)
                rest = rest_s[hd, rs, :] + jnp.sum(sp, axis=1, keepdims=True)
                rest_s[hd, rs, :] = rest
                stage1.append((z2 + _twice(rest), z2 - sp, _dot(sp.astype(BF16), from_s)))
            stage2 = []
            for (hd, rs), (e, log2_beta, ahead) in zip(chunks, stage1):
                a = jnp.exp2(e - ahead)
                g = a * da_s[hd, rs, :]
                da_s[hd, rs, :] = _dot_nt(do_heads[hd][rs, :], v_next)
                cg = cg_s[hd, rs, :]
                a_s[hd, rs, :] = a.astype(BF16)
                cg_s[hd, rs, :] = cg + jnp.sum(g, axis=1, keepdims=True)
                stage2.append((g, g + _twice(cg), log2_beta, _dot(g.astype(BF16), before)))
            for (hd, rs), (g, g_from, log2_beta, g_before) in zip(chunks, stage2):
                dz_s[hd, rs, :] = (g - jnp.exp2(log2_beta) * (g_from + g_before)).astype(BF16)

        logits(start)

        @pl.loop(start, ratio * i)
        def _(j):
            block(j)

        for r in range(ratio):
            block(ratio * i + r, visible[r])
        flush(last)
        dq_ref[...] = (jnp.where(first, dq_s[0], dq_s[1]) * ATTN_SCALE).astype(BF16)

        @pl.when(i == nq - 1)
        def _():
            @pl.loop(0, nk)
            def _(b):
                dk_ref[rows(b), :] = dkt_s[b].T.astype(BF16)
                dv_ref[rows(b), :] = dvt_s[b].T.astype(BF16)

        if nbg:
            @pl.when((pl.program_id(0) == npair - 1) & (i == nq - 1))
            def _():
                for cp in chip_partial_copies(bg_in, bg_out, *rest[2 * nbg + 12:]):
                    cp.wait()

    qblk = pl.BlockSpec((tq, LANES), lambda p, i: (i, p))
    full = pl.BlockSpec((T, LANES), lambda p, i: (0, p))
    res = pl.pallas_call(
        body,
        name="sb_bwd",
        grid=(npair, nq),
        in_specs=[
            pl.BlockSpec(memory_space=pltpu.SMEM),
            qblk,
            pl.BlockSpec((T, LANES), lambda p, i: (0, npair + p)),
            pl.BlockSpec((T, LANES), lambda p, i: (0, 2 * npair + p)),
            qblk, qblk,
        ] + [ANY] * nbg,
        out_specs=[qblk, full, full] + [ANY] * nbg,
        out_shape=[jax.ShapeDtypeStruct((T, D_MODEL), BF16)] * 3
        + [jax.ShapeDtypeStruct(b.shape, b.dtype) for b in bg_parts],
        scratch_shapes=[
            pltpu.VMEM((nk, LANES, tk), F32), pltpu.VMEM((nk, LANES, tk), F32),
            pltpu.VMEM((2, tq, LANES), F32), pltpu.VMEM((2, tq, LANES), F32), pltpu.VMEM((2, tq, LANES), F32),
            pltpu.VMEM((2, tq, tk), F32), pltpu.VMEM((2, tq, tk), F32),
            pltpu.VMEM((2, tq, tk), BF16), pltpu.VMEM((2, tq, tk), BF16),
        ] + [pltpu.SemaphoreType.DMA((nbg, N_PEER_CHIPS))] * (2 if nbg else 0),
        compiler_params=_cparams(),
    )(first_block, qkv, qkv, qkv, do, tot, *bg_parts)
    return res[0], res[1], res[2], list(res[3:])


def _swa_valid(n):
    qi = lax.broadcasted_iota(jnp.int32, (WINDOW, 2 * WINDOW), 0)
    ki = lax.broadcasted_iota(jnp.int32, (WINDOW, 2 * WINDOW), 1)
    diff = qi + WINDOW - ki
    return (diff >= 0) & (diff < WINDOW) & ((n > 0) | (ki >= WINDOW))


def _to_half(x, first, src, dst):
    keep = first if src == 0 else jnp.logical_not(first)
    x = jnp.where(keep, x, jnp.zeros_like(x))
    if src != dst:
        x = pltpu.roll(x.astype(F32), HEAD_DIM, 1).astype(BF16)
    return x


SWA_GROUP = SWA_Q_HEADS // SWA_KV_HEADS


def _swa_cols(h):
    return slice((h // 2) * LANES, (h // 2 + 1) * LANES)


def _swa_kv_pair(h):
    return (h // SWA_GROUP) // 2


def _swa_kv_half(h):
    return (h // SWA_GROUP) % 2


def _kv_band(prev_ref, cur_ref, pb):
    cols = slice(pb * LANES, (pb + 1) * LANES)
    return jnp.concatenate([prev_ref[:, cols], cur_ref[:, cols]], axis=0)


def _swa_specs(T):
    nb = T // WINDOW
    kv_w = SWA_KV_HEADS * HEAD_DIM
    qrow = pl.BlockSpec((WINDOW, D_MODEL), lambda n: (n, 0))
    cur = pl.BlockSpec((WINDOW, kv_w), lambda n: (n, 0))
    prev = pl.BlockSpec((WINDOW, kv_w), lambda n: (jnp.maximum(n - 1, 0), 0))
    smem = pl.BlockSpec(memory_space=pltpu.SMEM)
    return nb, qrow, cur, prev, smem


def swa_fwd(q, k, v, sinks):
    T = q.shape[0]
    nb, qrow, cur, prev, smem = _swa_specs(T)

    def body(sink_ref, q_ref, kc_ref, kp_ref, vc_ref, vp_ref, o_ref, lse_ref):
        n = pl.program_id(0)
        lane = lax.broadcasted_iota(jnp.int32, (WINDOW, LANES), 1)
        first = lane < HEAD_DIM
        valid = _swa_valid(n)
        k2 = [_kv_band(kp_ref, kc_ref, pb) for pb in range(SWA_KV_HEADS // 2)]
        v2 = [_kv_band(vp_ref, vc_ref, pb) for pb in range(SWA_KV_HEADS // 2)]
        logits = [jnp.where(valid, _dot_nt(_to_half(q_ref[:, _swa_cols(h)], first, h % 2, _swa_kv_half(h)),
                                            k2[_swa_kv_pair(h)]), NEG_BIG) for h in range(SWA_Q_HEADS)]
        probs = []
        lse_acc = jnp.zeros((WINDOW, LANES), F32)
        for h, s in enumerate(logits):
            sink = sink_ref[h]
            m = jnp.maximum(jnp.max(s, axis=1, keepdims=True), sink)
            p = jnp.exp(s - m)
            den = jnp.sum(p, axis=1, keepdims=True) + jnp.exp(sink - m)
            probs.append((p / den).astype(BF16))
            lse_acc = jnp.where(lane == h, m + jnp.log(den), lse_acc)
        outs = []
        for h, p in enumerate(probs):
            o = _dot(p, v2[_swa_kv_pair(h)])
            outs.append(pltpu.roll(o, HEAD_DIM, 1) if h % 2 != _swa_kv_half(h) else o)
        for pair in range(SWA_Q_HEADS // 2):
            o_ref[:, _swa_cols(2 * pair)] = jnp.where(first, outs[2 * pair], outs[2 * pair + 1]).astype(BF16)
        lse_ref[...] = lse_acc

    return pl.pallas_call(
        body,
        name="swa_fwd",
        grid=(nb,),
        in_specs=[smem, qrow, cur, prev, cur, prev],
        out_specs=[qrow, pl.BlockSpec((WINDOW, LANES), lambda n: (n, 0))],
        out_shape=[jax.ShapeDtypeStruct((T, D_MODEL), BF16), jax.ShapeDtypeStruct((T, LANES), F32)],
        compiler_params=_cparams(),
    )(sinks, q, k, k, v, v)


def swa_bwd(q, k, v, sinks, do, o, lse, cos, sin):
    T = q.shape[0]
    nb, qrow, cur, prev, smem = _swa_specs(T)
    kv_w = SWA_KV_HEADS * HEAD_DIM

    def body(sink_ref, q_ref, kc_ref, kp_ref, vc_ref, vp_ref, do_ref, o_ref, lse_ref, cos_ref, sin_ref,
             dq_ref, own_ref, prv_ref, dsink_ref):
        n = pl.program_id(0)

        @pl.when(n == 0)
        def _():
            dsink_ref[...] = jnp.zeros_like(dsink_ref)

        lane = lax.broadcasted_iota(jnp.int32, (WINDOW, LANES), 1)
        lane1 = lax.broadcasted_iota(jnp.int32, (1, LANES), 1)
        first = lane < HEAD_DIM
        valid = _swa_valid(n)
        cos_ = cos_ref[...]
        sin_ = sin_ref[...]
        k2 = [_kv_band(kp_ref, kc_ref, pb) for pb in range(SWA_KV_HEADS // 2)]
        v2 = [_kv_band(vp_ref, vc_ref, pb) for pb in range(SWA_KV_HEADS // 2)]
        q_t = q_ref[...].astype(F32).T.astype(BF16)
        do_t = do_ref[...].astype(F32).T.astype(BF16)
        stage1 = []
        for h in range(SWA_Q_HEADS):
            a, b, pb = h % 2, _swa_kv_half(h), _swa_kv_pair(h)
            qh = _to_half(q_ref[:, _swa_cols(h)], first, a, b)
            doh = _to_half(do_ref[:, _swa_cols(h)], first, a, b)
            stage1.append((jnp.where(valid, _dot_nt(qh, k2[pb]), NEG_BIG), _dot_nt(doh, v2[pb])))
        deltas = []
        for pair in range(SWA_Q_HEADS // 2):
            prod = do_ref[:, _swa_cols(2 * pair)].astype(F32) * o_ref[:, _swa_cols(2 * pair)].astype(F32)
            deltas += [jnp.sum(jnp.where(first, prod, 0.0), axis=1, keepdims=True),
                       jnp.sum(jnp.where(first, 0.0, prod), axis=1, keepdims=True)]
        stage2 = []
        dsink = jnp.zeros((1, LANES), F32)
        for h, (s, dp) in enumerate(stage1):
            lse_h = lse_ref[:, h:h + 1]
            p = jnp.exp(s - lse_h)
            delta = deltas[h]
            p_sink = jnp.exp(sink_ref[h] - lse_h)
            dsink = dsink + jnp.where(lane1 == h, -jnp.sum(p_sink * delta, axis=0, keepdims=True), 0.0)
            stage2.append(((p * (dp - delta)).astype(BF16), p.astype(BF16)))
        dqs = []
        dk_t = [None] * SWA_KV_HEADS
        dv_t = [None] * SWA_KV_HEADS
        for h, (ds, pb16) in enumerate(stage2):
            kvh = h // SWA_GROUP
            dims = slice(h * HEAD_DIM, (h + 1) * HEAD_DIM)
            dq = _dot(ds, k2[_swa_kv_pair(h)])
            dqs.append(pltpu.roll(dq, HEAD_DIM, 1) if h % 2 != _swa_kv_half(h) else dq)
            dk_h = _dot(q_t[dims, :], ds)
            dv_h = _dot(do_t[dims, :], pb16)
            dk_t[kvh] = dk_h if dk_t[kvh] is None else dk_t[kvh] + dk_h
            dv_t[kvh] = dv_h if dv_t[kvh] is None else dv_t[kvh] + dv_h
        for pair in range(SWA_Q_HEADS // 2):
            dqp = jnp.where(first, dqs[2 * pair], dqs[2 * pair + 1])
            dq_ref[:, _swa_cols(2 * pair)] = ((dqp * cos_ + _swap32(dqp * sin_)) * ATTN_SCALE).astype(BF16)
        for pb in range(SWA_KV_HEADS // 2):
            dk2 = jnp.concatenate([dk_t[2 * pb], dk_t[2 * pb + 1]], axis=0).T
            dv2 = jnp.concatenate([dv_t[2 * pb], dv_t[2 * pb + 1]], axis=0).T
            kcols = slice(pb * LANES, (pb + 1) * LANES)
            vcols = slice(kv_w + pb * LANES, kv_w + (pb + 1) * LANES)
            prv_ref[:, kcols] = dk2[:WINDOW]
            own_ref[:, kcols] = dk2[WINDOW:]
            prv_ref[:, vcols] = dv2[:WINDOW]
            own_ref[:, vcols] = dv2[WINDOW:]
        dsink_ref[...] += dsink

    tab = pl.BlockSpec((WINDOW, LANES), lambda n: (n, 0))
    kvrow = pl.BlockSpec((WINDOW, 2 * kv_w), lambda n: (n, 0))
    return pl.pallas_call(
        body,
        name="swa_bwd",
        grid=(nb,),
        in_specs=[smem, qrow, cur, prev, cur, prev, qrow, qrow, tab, tab, tab],
        out_specs=[qrow, kvrow, kvrow, pl.BlockSpec((1, LANES), lambda n: (0, 0))],
        out_shape=[
            jax.ShapeDtypeStruct((T, D_MODEL), BF16),
            jax.ShapeDtypeStruct((T, 2 * kv_w), F32),
            jax.ShapeDtypeStruct((T, 2 * kv_w), F32),
            jax.ShapeDtypeStruct((1, LANES), F32),
        ],
        compiler_params=_cparams(),
    )(sinks, q, k, k, v, v, do, o, lse, cos, sin)


def kv_grad_combine(own, prv, cos, sin):
    T = own.shape[0]
    nb = T // WINDOW
    kv_w = SWA_KV_HEADS * HEAD_DIM

    def body(own_ref, nxt_ref, cos_ref, sin_ref, out_ref):
        n = pl.program_id(0)
        nxt = jnp.where(n + 1 < nb, nxt_ref[...], 0.0)
        tot = own_ref[...] + nxt
        dk = tot[:, :kv_w]
        c = _tile_lanes(cos_ref[...], kv_w)
        s = _tile_lanes(sin_ref[...], kv_w)
        out_ref[:, :kv_w] = (dk * c + _swap32(dk * s)).astype(BF16)
        out_ref[:, kv_w:] = tot[:, kv_w:].astype(BF16)

    tab = pl.BlockSpec((WINDOW, LANES), lambda n: (n, 0))
    kvrow = pl.BlockSpec((WINDOW, 2 * kv_w), lambda n: (n, 0))
    return pl.pallas_call(
        body,
        name="kv_grad_combine",
        grid=(nb,),
        in_specs=[kvrow, pl.BlockSpec((WINDOW, 2 * kv_w), lambda n: (jnp.minimum(n + 1, nb - 1), 0)), tab, tab],
        out_specs=kvrow,
        out_shape=jax.ShapeDtypeStruct((T, 2 * kv_w), BF16),
        compiler_params=_cparams(),
    )(own, prv, cos, sin)


ANY = pl.BlockSpec(memory_space=pl.ANY)


def _place():
    x, y, c = lax.axis_index("x"), lax.axis_index("y"), lax.axis_index("c")
    other_chips = [(1 - x, y), (x, 1 - y), (1 - x, 1 - y)]
    return x, y, c, 2 * x + y, other_chips


N_PEER_CHIPS = N_CHIPS - 1


class GatherOps:
    def __init__(self, rows, shards, lands, ici_send, ici_recv, d2d_send=None, d2d_recv=None):
        self.rows, self.shards, self.lands = rows, shards, lands
        self.ici_send, self.ici_recv, self.d2d_send, self.d2d_recv = ici_send, ici_recv, d2d_send, d2d_recv
        self.x, self.y, self.c, self.me, self.chips = _place()
        self.pairs = [(t, jdx) for t in range(len(rows)) for jdx in range(N_PEER_CHIPS)]

    def _half(self, ref, t, which):
        r = self.rows[t] // 2
        return ref.at[:, pl.ds(pl.multiple_of(which * r, 16), r), :]

    def _ici(self, t, jdx):
        px, py = self.chips[jdx]
        return pltpu.make_async_remote_copy(
            src_ref=self._half(self.shards[t], t, self.c), dst_ref=self._half(self.lands[t].at[self.me], t, self.c),
            send_sem=self.ici_send.at[t, jdx], recv_sem=self.ici_recv.at[t, jdx],
            device_id=(px, py, self.c), device_id_type=MESH)

    def _landed(self, t, jdx):
        px, py = self.chips[jdx]
        blk = self._half(self.lands[t].at[2 * px + py], t, self.c)
        return pltpu.make_async_remote_copy(
            src_ref=blk, dst_ref=blk, send_sem=self.ici_send.at[t, jdx], recv_sem=self.ici_recv.at[t, jdx],
            device_id=(px, py, self.c), device_id_type=MESH)

    def _d2d(self, t, jdx, which):
        px, py = self.chips[jdx]
        blk = self._half(self.lands[t].at[2 * px + py], t, which)
        return pltpu.make_async_remote_copy(
            src_ref=blk, dst_ref=blk, send_sem=self.d2d_send.at[t, jdx], recv_sem=self.d2d_recv.at[t, jdx],
            device_id=(self.x, self.y, 1 - self.c), device_id_type=MESH)

    def start_ici(self):
        for t, jdx in self.pairs:
            self._ici(t, jdx).start()

    def wait_ici(self):
        for t, jdx in self.pairs:
            self._landed(t, jdx).wait_recv()
        self.wait_ici_sends()

    def wait_ici_sends(self):
        for t, jdx in self.pairs:
            self._ici(t, jdx).wait_send()

    def forward_arrivals(self):
        for t, jdx in self.pairs:
            self._landed(t, jdx).wait_recv()
            self._d2d(t, jdx, self.c).start()

    def start_forwards(self):
        for t, jdx in self.pairs:
            self._d2d(t, jdx, self.c).start()

    def wait_forwards(self):
        for t, jdx in self.pairs:
            self._d2d(t, jdx, 1 - self.c).wait_recv()
            self._d2d(t, jdx, self.c).wait_send()


def all_gather_weights(shards, lands):
    n = len(shards)
    rows = [s.shape[1] for s in shards]

    def body(*refs):
        ins, outs = refs[:n], refs[2 * n:3 * n]
        ops = GatherOps(rows, ins, outs, *refs[3 * n:])
        ops.start_ici()
        ops.forward_arrivals()
        ops.wait_forwards()
        ops.wait_ici_sends()

    return pl.pallas_call(
        body,
        name="all_gather_weights",
        in_specs=[ANY] * (2 * n),
        out_specs=[ANY] * n,
        out_shape=[jax.ShapeDtypeStruct(l.shape, l.dtype) for l in lands],
        input_output_aliases={n + t: t for t in range(n)},
        scratch_shapes=[pltpu.SemaphoreType.DMA((n, N_PEER_CHIPS))] * 4,
    )(*shards, *lands)


def place_own_shard(name, shard, chip):
    nl, r, c = shard.shape

    def body(chip_ref, s_ref, o_ref):
        o_ref[...] = s_ref[...]

    return pl.pallas_call(
        body, name=name,
        grid_spec=pltpu.PrefetchScalarGridSpec(
            num_scalar_prefetch=1, grid=(nl,),
            in_specs=[pl.BlockSpec((None, r, c), lambda l, chip_ref: (l, 0, 0))],
            out_specs=pl.BlockSpec((None, None, r, c), lambda l, chip_ref: (chip_ref[0], l, 0, 0))),
        out_shape=jax.ShapeDtypeStruct((N_CHIPS,) + shard.shape, shard.dtype), compiler_params=_cparams(),
    )(chip, shard)


def exchange_halves(name, slabs):
    n = len(slabs)

    def body(*refs):
        ins, theirs = refs[:n], refs[n:2 * n]
        send_sems, recv_sems = refs[2 * n:]
        x, y, c, _, _ = _place()
        copies = []
        for t in range(n):
            cp = pltpu.make_async_remote_copy(
                src_ref=ins[t].at[1 - c], dst_ref=theirs[t], send_sem=send_sems.at[t],
                recv_sem=recv_sems.at[t], device_id=(x, y, 1 - c), device_id_type=MESH)
            cp.start()
            copies.append(cp)
        for cp in copies:
            cp.wait()

    return pl.pallas_call(
        body,
        name=name,
        in_specs=[ANY] * n,
        out_specs=[ANY] * n,
        out_shape=[jax.ShapeDtypeStruct(s.shape[1:], s.dtype) for s in slabs],
        scratch_shapes=[pltpu.SemaphoreType.DMA((n,)), pltpu.SemaphoreType.DMA((n,))],
    )(*slabs)


def chip_partial_copies(ins, outs, send_sems, recv_sems):
    _, _, c, me, chips = _place()
    return [pltpu.make_async_remote_copy(
        src_ref=ins[t].at[2 * px + py], dst_ref=outs[t].at[me], send_sem=send_sems.at[t, jdx],
        recv_sem=recv_sems.at[t, jdx], device_id=(px, py, c), device_id_type=MESH)
        for t in range(len(ins)) for jdx, (px, py) in enumerate(chips)]


def exchange_chip_partials(name, parts):
    n = len(parts)

    def body(*refs):
        copies = chip_partial_copies(refs[:n], refs[n:2 * n], *refs[2 * n:])
        for cp in copies:
            cp.start()
        for cp in copies:
            cp.wait()

    return pl.pallas_call(
        body,
        name=name,
        in_specs=[ANY] * n,
        out_specs=[ANY] * n,
        out_shape=[jax.ShapeDtypeStruct(p.shape, p.dtype) for p in parts],
        scratch_shapes=[pltpu.SemaphoreType.DMA((n, 3)), pltpu.SemaphoreType.DMA((n, 3))],
    )(*parts)


def share_reduced_halves(name, halves):
    n = len(halves)

    def body(*refs):
        ins, outs = refs[:n], refs[n:2 * n]
        send_sems, recv_sems = refs[2 * n:]
        x, y, c, _, _ = _place()
        copies = []
        for t in range(n):
            cp = pltpu.make_async_remote_copy(
                src_ref=ins[t], dst_ref=outs[t], send_sem=send_sems.at[t],
                recv_sem=recv_sems.at[t], device_id=(x, y, 1 - c), device_id_type=MESH)
            cp.start()
            copies.append(cp)
        for cp in copies:
            cp.wait()

    return pl.pallas_call(
        body,
        name=name,
        in_specs=[ANY] * n,
        out_specs=[ANY] * n,
        out_shape=[jax.ShapeDtypeStruct(h.shape, h.dtype) for h in halves],
        scratch_shapes=[pltpu.SemaphoreType.DMA((n,)), pltpu.SemaphoreType.DMA((n,))],
    )(*halves)


def _row_tile(r, c):
    tr = r
    while tr * c * 4 > (3 << 19) and tr % 16 == 0:
        tr //= 2
    return tr


def add_sibling(name, slab, theirs, core):
    _, ns, slots, r, c = slab.shape
    tr = _row_tile(r, c)

    def body(core_ref, a_ref, b_ref, o_ref):
        o_ref[...] = (a_ref[...] + b_ref[...]).astype(BF16)

    blk = pl.BlockSpec((None, None, tr, c), lambda s, l, i, core_ref: (s, l, i, 0))
    return pl.pallas_call(
        body, name=name,
        grid_spec=pltpu.PrefetchScalarGridSpec(
            num_scalar_prefetch=1, grid=(ns, slots, r // tr),
            in_specs=[pl.BlockSpec((None, None, None, tr, c), lambda s, l, i, core_ref: (core_ref[0], s, l, i, 0)), blk],
            out_specs=blk),
        out_shape=jax.ShapeDtypeStruct(theirs.shape, BF16), compiler_params=_cparams(),
    )(core, slab, theirs)


def sum_chips(name, recv, own, chip):
    _, slots, r, c = recv.shape
    tr = _row_tile(r, c)

    def body(chip_ref, r0, r1, r2, r3, own_ref, o_ref):
        me = chip_ref[0]
        mine = own_ref[...]
        terms = [jnp.where(me == s, mine, rr[...]).astype(F32) for s, rr in enumerate((r0, r1, r2, r3))]
        o_ref[...] = ((terms[0] + terms[1]) + terms[2]) + terms[3]

    def src(s):
        return pl.BlockSpec((None, None, tr, c),
                            lambda l, i, chip_ref: (jnp.where(chip_ref[0] == s, (s + 1) % N_CHIPS, s), l, i, 0))

    return pl.pallas_call(
        body, name=name,
        grid_spec=pltpu.PrefetchScalarGridSpec(
            num_scalar_prefetch=1, grid=(slots, r // tr),
            in_specs=[src(0), src(1), src(2), src(3),
                      pl.BlockSpec((None, None, tr, c), lambda l, i, chip_ref: (chip_ref[0], l, i, 0))],
            out_specs=pl.BlockSpec((None, tr, c), lambda l, i, chip_ref: (l, i, 0))),
        out_shape=jax.ShapeDtypeStruct((slots, r, c), F32), compiler_params=_cparams(),
    )(chip, recv, recv, recv, recv, own)


def _adamw_math(w, g, m, v):
    m = ADAM_B1 * m + (1.0 - ADAM_B1) * g
    v = ADAM_B2 * v + (1.0 - ADAM_B2) * (g * g)
    m_hat = m / (1.0 - ADAM_B1 ** ADAM_STEP)
    v_hat = v / (1.0 - ADAM_B2 ** ADAM_STEP)
    delta = -ADAM_LR * (m_hat / (jnp.sqrt(v_hat) + ADAM_EPS) + ADAM_WD * w)
    return delta, m, v


def adamw_shard(name, w, m, v, g_pairs, core, slots, row_halves):
    n = w.shape[0]
    assert n == len(g_pairs)
    _, r, c = g_pairs[0][0].shape
    tr = _row_tile(r, c)
    nr = r // tr

    def body(core_ref, w_ref, m_ref, v_ref, *rest):
        g_refs, (go_ref, d_ref, mo_ref, vo_ref) = rest[:2 * n], rest[2 * n:]
        mine = pl.program_id(1) == core_ref[0]
        g = jnp.where(mine, g_refs[0][...], g_refs[1][...])
        for l in range(1, n):
            g = jnp.where(pl.program_id(0) == l, jnp.where(mine, g_refs[2 * l][...], g_refs[2 * l + 1][...]), g)
        delta, mm, vv = _adamw_math(w_ref[...], g, m_ref[...], v_ref[...])
        go_ref[...] = g
        d_ref[...] = delta
        mo_ref[...] = mm
        vo_ref[...] = vv

    if row_halves:
        wspec = pl.BlockSpec((None, tr, c), lambda l, h, i, core_ref: (l, h * nr + i, 0))
    else:
        wspec = pl.BlockSpec((None, tr, c), lambda l, h, i, core_ref: (l, i, h))
    def gspec(slot):
        return pl.BlockSpec((None, tr, c), lambda l, h, i, core_ref: (slot, i, 0))

    shp = jax.ShapeDtypeStruct(w.shape, F32)
    return pl.pallas_call(
        body, name=name,
        grid_spec=pltpu.PrefetchScalarGridSpec(
            num_scalar_prefetch=1, grid=(n, 2, nr),
            in_specs=[wspec, wspec, wspec] + [gspec(s) for s in slots for _ in range(2)], out_specs=[wspec] * 4),
        out_shape=[shp] * 4, compiler_params=_cparams(),
    )(core, w, m, v, *[g for pair in g_pairs for g in pair])


SMALL_ROWS = 16


def small_allreduce_adamw(part, w, m, v):
    def body(p_ref, w_ref, m_ref, v_ref, g_ref, d_ref, mo_ref, vo_ref, buf, send_sems, recv_sems):
        x, y, c, _, _ = _place()
        me = 4 * x + 2 * y + c
        buf[me] = p_ref[...]
        copies = []
        for k in range(1, N_DEV):
            kx, ky, kc = (k >> 2) & 1, (k >> 1) & 1, k & 1
            peer = (x ^ kx, y ^ ky, c ^ kc)
            cp = pltpu.make_async_remote_copy(
                src_ref=p_ref, dst_ref=buf.at[me], send_sem=send_sems.at[k - 1],
                recv_sem=recv_sems.at[k - 1], device_id=peer, device_id_type=MESH)
            cp.start()
            copies.append(cp)
        for cp in copies:
            cp.wait()
        g = buf[0]
        for dev in range(1, N_DEV):
            g = g + buf[dev]
        delta, mm, vv = _adamw_math(w_ref[...], g, m_ref[...], v_ref[...])
        g_ref[...] = g
        d_ref[...] = delta
        mo_ref[...] = mm
        vo_ref[...] = vv

    vm = pl.BlockSpec(memory_space=pltpu.VMEM)
    shp = jax.ShapeDtypeStruct(part.shape, F32)
    return pl.pallas_call(
        body, name="small_allreduce_adamw",
        in_specs=[vm] * 4, out_specs=[vm] * 4, out_shape=[shp] * 4,
        scratch_shapes=[
            pltpu.VMEM((N_DEV,) + part.shape, F32),
            pltpu.SemaphoreType.DMA((N_DEV - 1,)), pltpu.SemaphoreType.DMA((N_DEV - 1,)),
        ],
    )(part, w, m, v)


def _rope_tables(T):
    half = HEAD_DIM // 2
    inv_freq = ROPE_THETA ** (-jnp.arange(half, dtype=F32) / half)
    ang = jnp.arange(T).astype(F32)[:, None] * inv_freq[None, :]
    cos = jnp.tile(jnp.cos(ang), (1, LANES // half))
    sin = jnp.tile(jnp.sin(ang), (1, LANES // half))
    lane = jnp.arange(LANES)
    sign = jnp.where((lane % HEAD_DIM) < half, -1.0, 1.0).astype(F32)
    return cos, sin * sign[None, :]


def _pack_small(ffn1, mix, ffn2, kvn, fin, sinks, loss_row):
    sink_row = jnp.pad(sinks.reshape(1, SWA_Q_HEADS), ((0, 0), (0, D_MODEL - SWA_Q_HEADS)))
    rows = jnp.concatenate([ffn1, mix, ffn2, kvn.reshape(1, -1), fin.reshape(1, -1), sink_row, loss_row], axis=0)
    return jnp.concatenate([rows, jnp.zeros((SMALL_ROWS - rows.shape[0], D_MODEL), F32)], axis=0)


def kernel(x, ffn1_norm, ffn1_w_in, ffn1_w_out, mix_norm, ffn2_norm, ffn2_w_in, ffn2_w_out, sb_w_qkv, sb_w_o, kv_norm, kv_w, swa_w_q, swa_sinks, swa_w_o, final_norm, loss_target, m_ffn1_norm, m_ffn1_w_in, m_ffn1_w_out, m_mix_norm, m_ffn2_norm, m_ffn2_w_in, m_ffn2_w_out, m_sb_w_qkv, m_sb_w_o, m_kv_norm, m_kv_w, m_swa_w_q, m_swa_sinks, m_swa_w_o, m_final_norm, v_ffn1_norm, v_ffn1_w_in, v_ffn1_w_out, v_mix_norm, v_ffn2_norm, v_ffn2_w_in, v_ffn2_w_out, v_sb_w_qkv, v_sb_w_o, v_kv_norm, v_kv_w, v_swa_w_q, v_swa_sinks, v_swa_w_o, v_final_norm):
    T = x.shape[1]
    kv_cols = SWA_KV_HEADS * HEAD_DIM
    x2 = x.reshape(T, D_MODEL)
    tgt = loss_target.reshape(T, D_MODEL)
    cos, sin = _rope_tables(T)

    w_in_l = jnp.concatenate([ffn1_w_in, ffn2_w_in], axis=0).astype(BF16)
    w_out_l = jnp.concatenate([ffn1_w_out, ffn2_w_out], axis=0).astype(BF16)
    sq_l = jnp.concatenate([sb_w_o, swa_w_q, swa_w_o], axis=0).astype(BF16)
    qkv_l = sb_w_qkv[0].astype(BF16)
    kvw_l = kv_w.astype(BF16)
    core = lax.axis_index("c").astype(jnp.int32).reshape(1)
    chip = (2 * lax.axis_index("x") + lax.axis_index("y")).astype(jnp.int32).reshape(1)
    early = [w_in_l[:1], w_out_l[:1], sq_l, qkv_l[None]]
    late = [w_in_l[1:], w_out_l[1:], kvw_l[None]]
    early_lands = [place_own_shard(f"own_early_{t}", s, chip) for t, s in enumerate(early)]
    late_lands = [place_own_shard(f"own_late_{t}", s, chip) for t, s in enumerate(late)]
    w_in0, w_out0, w_sq, w_qkv = all_gather_weights(early, early_lands)
    w_qkv = w_qkv.reshape(N_CHIPS, D_MODEL, QKV_COLS)

    def ffn_w(slot):
        return (w_in0, w_out0, 0) if slot == 0 else (w_in_r, w_out_r, slot - 1)

    def vec(a, i):
        return a[i].reshape(1, D_MODEL)

    ident = lambda w: w
    sq_prep = lambda w: w.reshape(D_MODEL, w.shape[-1])
    qscale = jnp.concatenate([jnp.full((1, D_MODEL), ATTN_SCALE, F32), jnp.ones((1, 2 * D_MODEL), F32)], axis=1)
    swa_scale = jnp.full((1, D_MODEL), ATTN_SCALE, F32)
    sinks = swa_sinks.reshape(SWA_Q_HEADS)

    h1, gate1, up1 = ffn_fwd("l0a", x2, vec(ffn1_norm, 0), *ffn_w(SLOT_FFN1[0]))
    qkv = qkv_fwd(h1, vec(mix_norm, 0), w_qkv, qscale)
    o_sb, tot, sb_first, late_lands = sb_fwd(qkv, late, late_lands)
    h2, (w_in_r, w_out_r, w_kv) = linear_res("sb_out", o_sb, w_sq, SQ_SB_O, h1, late_lands)
    w_kv = w_kv.reshape(D_MODEL, 2 * kv_cols)
    h3, gate2, up2 = ffn_fwd("l0b", h2, vec(ffn2_norm, 0), *ffn_w(SLOT_FFN2[0]))
    kvn = kv_norm.reshape(1, D_MODEL)
    k_sw = rms_linear("kv_k", h3, kvn, w_kv, pl.BlockSpec((D_MODEL, kv_cols), lambda i, j: (0, 0)), ident,
                      kv_cols, kv_cols, rope=(cos, sin))
    v_sw = rms_linear("kv_v", h3, kvn, w_kv, pl.BlockSpec((D_MODEL, kv_cols), lambda i, j: (0, 1)), ident,
                      kv_cols, kv_cols)
    h4, gate3, up3 = ffn_fwd("l1a", h3, vec(ffn1_norm, 1), *ffn_w(SLOT_FFN1[1]))
    q_sw = rms_linear("swa_q", h4, vec(mix_norm, 1), w_sq,
                      pl.BlockSpec((N_CHIPS, None, SQ_ROWS, 512), lambda i, j: (0, SQ_SWA_Q, 0, j)), sq_prep,
                      D_MODEL, 512, rope=(cos, sin), scale=swa_scale)
    o_sw, lse = swa_fwd(q_sw, k_sw, v_sw, sinks)
    h5 = linear_res("swa_out", o_sw, w_sq, SQ_SWA_O, h4)
    h6, gate4, up4 = ffn_fwd("l1b", h5, vec(ffn2_norm, 1), *ffn_w(SLOT_FFN2[1]))
    dh6, loss_p, d_final = loss_bwd(h6, final_norm.reshape(1, D_MODEL), tgt)

    slab = {}
    ffn_place = {SLOT_FFN1[0]: (0, 0, 1), SLOT_FFN1[1]: (1, 0, 3), SLOT_FFN2[0]: (1, 1, 3), SLOT_FFN2[1]: (1, 2, 3)}
    sq_place = {SQ_SB_O: (1, 0, 3), SQ_SWA_Q: (1, 1, 3), SQ_SWA_O: (1, 2, 3)}

    def ffn_grads(tag, dh, h_in, g, gate, up, slot):
        dh_in, xn, dg_, du_, act, dhb, dnorm = ffn_bwd(tag, dh, h_in, g, gate, up, *ffn_w(slot))
        grp, s, ns = ffn_place[slot]
        in_shape = (2, N_CHIPS, ns, D_MODEL // 2, FF_CHUNK)
        out_shape = (2, N_CHIPS, ns, FF_ROWS, D_MODEL // 2)
        blk = (None, 1, None, D_MODEL // 2, FF_CHUNK)
        slab["in", grp] = mm_tn(f"dw_gate_{tag}", xn, dg_, D_MODEL // 2, FF_CHUNK, blk,
                                lambda k, n: (k, n, s, 0, 0), in_shape, prev=slab.get(("in", grp)))
        slab["in", grp] = mm_tn(f"dw_up_{tag}", xn, du_, D_MODEL // 2, FF_CHUNK, blk,
                                lambda k, n: (k, 2 + n, s, 0, 0), in_shape, prev=slab["in", grp])
        slab["out", grp] = mm_tn(f"dw_out_{tag}", act, dhb, FF_CHUNK, D_MODEL // 2,
                                 (None, 2, None, FF_ROWS, D_MODEL // 2),
                                 lambda k, n: (n, k, s, 0, 0), out_shape, prev=slab.get(("out", grp)))
        return dh_in, dnorm

    def sq_grad(tag, a, dyb, t):
        grp, s, ns = sq_place[t]
        slab["sq", grp] = mm_tn(f"dw_sq_{tag}", a, dyb, D_MODEL, D_MODEL // 2,
                                (None, N_CHIPS, None, SQ_ROWS, D_MODEL // 2),
                                lambda k, n: (n, 0, s, 0, 0), (2, N_CHIPS, ns, SQ_ROWS, D_MODEL // 2),
                                prev=slab.get(("sq", grp)))

    def reduce_group(grp, kinds, host=None):
        slabs = [slab[kind, grp] for kind in kinds]
        names = [f"{kind}{grp}" for kind in kinds]
        theirs = exchange_halves(f"exchange_halves_{grp}", slabs)
        parts = [add_sibling(f"add_sibling_{nm}", s, t, core) for nm, s, t in zip(names, slabs, theirs)]
        arrived = host(parts) if host else exchange_chip_partials(f"exchange_chip_partials_{grp}", parts)
        halves = [sum_chips(f"sum_chips_{nm}", g, p, chip) for nm, g, p in zip(names, arrived, parts)]
        sib_halves = share_reduced_halves(f"share_reduced_halves_{grp}", halves)
        return {kind: pair for kind, pair in zip(kinds, zip(halves, sib_halves))}

    dh5, d_ffn2_1 = ffn_grads("l1b", dh6, h5, vec(ffn2_norm, 1), gate4, up4, SLOT_FFN2[1])
    do_sw, dh5b = linear_bwd_plain("swa_out_bwd", dh5, w_sq, SQ_SWA_O)
    sq_grad("swa_o", o_sw, dh5b, SQ_SWA_O)
    dq_sw, kv_own, kv_prev, d_sinks = swa_bwd(q_sw, k_sw, v_sw, sinks, do_sw, o_sw, lse, cos, sin)
    sq_w_spec = pl.BlockSpec((N_CHIPS, None, SQ_ROWS, D_MODEL), lambda i, j: (0, SQ_SWA_Q, 0, 0))
    dh4, hn4, d_mix_1 = linear_bwd_rms("swa_q_bwd", [(dq_sw, w_sq, sq_w_spec, sq_prep)], h4, vec(mix_norm, 1), dh5,
                                       1, D_MODEL)
    sq_grad("swa_q", hn4, dq_sw, SQ_SWA_Q)
    dh3a, d_ffn1_1 = ffn_grads("l1a", dh4, h3, vec(ffn1_norm, 1), gate3, up3, SLOT_FFN1[1])
    dkv = kv_grad_combine(kv_own, kv_prev, cos, sin)
    kv_w_spec = pl.BlockSpec((D_MODEL, 2 * kv_cols), lambda i, j: (0, 0))
    dh3, xn3, d_kvn = linear_bwd_rms("kv_bwd", [(dkv, w_kv, kv_w_spec, ident)], h3, kvn, dh3a, 1, 2 * kv_cols)
    slab["kv", 1] = mm_tn("dw_kv", xn3, dkv, D_MODEL, kv_cols, (None, N_CHIPS, None, SQ_ROWS, kv_cols),
                          lambda k, n: (n, 0, 0, 0, 0), (2, N_CHIPS, 1, SQ_ROWS, kv_cols))
    dh2, d_ffn2_0 = ffn_grads("l0b", dh3, h2, vec(ffn2_norm, 0), gate2, up2, SLOT_FFN2[0])
    do_sb, dh2b = linear_bwd_plain("sb_out_bwd", dh2, w_sq, SQ_SB_O)
    sq_grad("sb_o", o_sb, dh2b, SQ_SB_O)
    sb_grads = []

    def behind_sb_bwd(parts):
        dq_sb, dk_sb, dv_sb, arrived = sb_bwd(qkv, do_sb, tot, sb_first, parts)
        sb_grads.extend([dq_sb, dk_sb, dv_sb])
        return arrived

    red = {1: reduce_group(1, ["in", "out", "sq", "kv"], host=behind_sb_bwd)}
    dqkv = jnp.concatenate(sb_grads, axis=1)
    dh1, hn1, d_mix_0 = qkv_bwd(dqkv, w_qkv, h1, vec(mix_norm, 0), dh2)
    slab["qkv", 0] = mm_tn("dw_qkv", hn1, dqkv, D_MODEL // 2, QKV_COLS, (None, 1, None, D_MODEL // 2, QKV_COLS),
                           lambda k, n: (k, n, 0, 0, 0), (2, N_CHIPS, 1, D_MODEL // 2, QKV_COLS))
    dx, d_ffn1_0 = ffn_grads("l0a", dh1, x2, vec(ffn1_norm, 0), gate1, up1, SLOT_FFN1[0])
    red[0] = reduce_group(0, ["in", "out", "qkv"])

    def upd(name, w, m, v, kind, places, row_halves):
        shp = w.shape
        w3 = w.reshape((-1,) + shp[-2:])
        outs = adamw_shard(name, w3, m.reshape(w3.shape), v.reshape(w3.shape),
                           [red[grp][kind] for grp, _ in places], core, [s for _, s in places], row_halves)
        return [o.reshape(shp) for o in outs]

    ffn1_places = [ffn_place[s][:2] for s in SLOT_FFN1]
    ffn2_places = [ffn_place[s][:2] for s in SLOT_FFN2]
    r_ffn1_in = upd("adamw_ffn1_in", ffn1_w_in, m_ffn1_w_in, v_ffn1_w_in, "in", ffn1_places, True)
    r_ffn2_in = upd("adamw_ffn2_in", ffn2_w_in, m_ffn2_w_in, v_ffn2_w_in, "in", ffn2_places, True)
    r_ffn1_out = upd("adamw_ffn1_out", ffn1_w_out, m_ffn1_w_out, v_ffn1_w_out, "out", ffn1_places, False)
    r_ffn2_out = upd("adamw_ffn2_out", ffn2_w_out, m_ffn2_w_out, v_ffn2_w_out, "out", ffn2_places, False)
    r_qkv = upd("adamw_qkv", sb_w_qkv, m_sb_w_qkv, v_sb_w_qkv, "qkv", [(0, 0)], True)
    r_sb_o = upd("adamw_sb_o", sb_w_o, m_sb_w_o, v_sb_w_o, "sq", [sq_place[SQ_SB_O][:2]], False)
    r_swa_q = upd("adamw_swa_q", swa_w_q, m_swa_w_q, v_swa_w_q, "sq", [sq_place[SQ_SWA_Q][:2]], False)
    r_swa_o = upd("adamw_swa_o", swa_w_o, m_swa_w_o, v_swa_w_o, "sq", [sq_place[SQ_SWA_O][:2]], False)
    r_kv = upd("adamw_kv", kv_w, m_kv_w, v_kv_w, "kv", [(1, 0)], False)

    loss_row = jnp.pad(loss_p, ((0, 0), (0, D_MODEL - LANES)))
    d_sink_row = d_sinks[0, :SWA_Q_HEADS]
    part = _pack_small(jnp.concatenate([d_ffn1_0, d_ffn1_1], axis=0), jnp.concatenate([d_mix_0, d_mix_1], axis=0),
                       jnp.concatenate([d_ffn2_0, d_ffn2_1], axis=0), d_kvn, d_final, d_sink_row, loss_row)
    zrow = jnp.zeros((1, D_MODEL), F32)
    small = small_allreduce_adamw(
        part,
        _pack_small(ffn1_norm, mix_norm, ffn2_norm, kv_norm, final_norm, swa_sinks, zrow),
        _pack_small(m_ffn1_norm, m_mix_norm, m_ffn2_norm, m_kv_norm, m_final_norm, m_swa_sinks, zrow),
        _pack_small(v_ffn1_norm, v_mix_norm, v_ffn2_norm, v_kv_norm, v_final_norm, v_swa_sinks, zrow))

    def unpack(p):
        return dict(ffn1_norm=p[0:2], mix_norm=p[2:4], ffn2_norm=p[4:6], kv_norm=p[6], final_norm=p[7],
                    swa_sinks=p[8:9, :SWA_Q_HEADS])

    big = dict(ffn1_w_in=r_ffn1_in, ffn1_w_out=r_ffn1_out, ffn2_w_in=r_ffn2_in, ffn2_w_out=r_ffn2_out,
               sb_w_qkv=r_qkv, sb_w_o=r_sb_o, kv_w=r_kv, swa_w_q=r_swa_q, swa_w_o=r_swa_o)
    order = ["ffn1_norm", "ffn1_w_in", "ffn1_w_out", "mix_norm", "ffn2_norm", "ffn2_w_in", "ffn2_w_out",
             "sb_w_qkv", "sb_w_o", "kv_norm", "kv_w", "swa_w_q", "swa_sinks", "swa_w_o", "final_norm"]
    outs = []
    for kind in range(4):
        sm = unpack(small[kind])
        for nm in order:
            outs.append(big[nm][kind] if nm in big else sm[nm])
    loss = small[0][9, 0]
    return (loss, dx.reshape(x.shape), *outs)
```

```python
import functools

import jax
import jax.numpy as jnp
from jax import lax
from jax.experimental import pallas as pl
from jax.experimental.pallas import tpu as pltpu

F32 = jnp.float32
BF16 = jnp.bfloat16
MESH = pl.DeviceIdType.MESH

D_MODEL = 1024
D_FF = 2816
HEAD_DIM = 64
SB_HEADS = 16
SWA_Q_HEADS = 16
SWA_KV_HEADS = 4
WINDOW = 128
ROPE_THETA = 10000.0
RMS_EPS = 1e-6
FFN_RES_SCALE = 0.5
ATTN_SCALE = HEAD_DIM ** -0.5

ADAM_LR = 0.001
ADAM_B1 = 0.9
ADAM_B2 = 0.999
ADAM_EPS = 1e-08
ADAM_WD = 0.01
ADAM_STEP = 10

N_CHIPS = 4
N_DEV = 8
LANES = 128
FF_CHUNK = D_FF // 2
FF_ROWS = D_FF // N_CHIPS
SQ_ROWS = D_MODEL // N_CHIPS
QKV_COLS = 3 * D_MODEL // N_CHIPS
VMEM_LIMIT = 56 * 1024 * 1024
NEG_BIG = -1e30

SLOT_FFN1 = (0, 1)
SLOT_FFN2 = (2, 3)
SQ_SB_O, SQ_SWA_Q, SQ_SWA_O = 0, 1, 2


def _cparams():
    return pltpu.CompilerParams(vmem_limit_bytes=VMEM_LIMIT)


def _dot(a, b):
    return jnp.dot(a, b, preferred_element_type=F32)


def _dot_nt(a, b):
    return lax.dot_general(a, b, (((1,), (1,)), ((), ())), preferred_element_type=F32)


def _dot_tn(a, b):
    return lax.dot_general(a, b, (((0,), (0,)), ((), ())), preferred_element_type=F32)


def _rstd(h):
    return lax.rsqrt(jnp.mean(h * h, axis=-1, keepdims=True) + RMS_EPS)


def _swap32(x):
    n = x.shape[-1]
    lane = lax.broadcasted_iota(jnp.int32, x.shape, x.ndim - 1)
    first = (lane % HEAD_DIM) < (HEAD_DIM // 2)
    return jnp.where(first, pltpu.roll(x, n - HEAD_DIM // 2, x.ndim - 1), pltpu.roll(x, HEAD_DIM // 2, x.ndim - 1))


def _tile_lanes(t, n):
    return t if n == LANES else jnp.tile(t, (1, n // LANES))


FFN_ROWS = 256


def _ffn_w_in_spec(slot):
    return pl.BlockSpec((N_CHIPS, None, D_MODEL, FF_CHUNK), lambda i: (0, slot, 0, 0), pipeline_mode=pl.Buffered(1))


def _ffn_w_out_spec(slot):
    return pl.BlockSpec((N_CHIPS, None, FF_ROWS, D_MODEL), lambda i: (0, slot, 0, 0), pipeline_mode=pl.Buffered(1))


def ffn_fwd(tag, h, g, w_in, w_out, slot, bg_shards=(), bg_lands=()):
    T = h.shape[0]
    tm = FFN_ROWS
    nch = D_FF // FF_CHUNK
    nbg = len(bg_shards)
    nt = T // tm

    def body(h_ref, g_ref, wi_ref, wo_ref, *rest):
        out_ref, gate_ref, up_ref = rest[2 * nbg:2 * nbg + 3]
        if nbg:
            step = pl.program_id(0)
            gather = GatherOps([s.shape[1] for s in bg_shards], rest[:nbg], rest[2 * nbg + 3:3 * nbg + 3],
                               *rest[3 * nbg + 3:])
            pl.when(step == 0)(gather.start_ici)
            pl.when(step == nt // 2)(gather.forward_arrivals)

            @pl.when(step == nt - 1)
            def _():
                gather.wait_forwards()
                gather.wait_ici_sends()

        hh = h_ref[...]
        xn = (hh * _rstd(hh) * g_ref[...]).astype(BF16)
        acc = None
        for j in range(nch):
            cols = slice(j * FF_CHUNK, (j + 1) * FF_CHUNK)
            gate = _dot(xn, wi_ref[j])
            up = _dot(xn, wi_ref[nch + j])
            gate_ref[:, cols] = gate.astype(BF16)
            up_ref[:, cols] = up.astype(BF16)
            a = (gate * jax.nn.sigmoid(gate) * up).astype(BF16)
            part = _dot(a, wo_ref[2 * j:2 * j + 2].reshape(FF_CHUNK, D_MODEL))
            acc = part if acc is None else acc + part
        out_ref[...] = hh + FFN_RES_SCALE * acc

    row = pl.BlockSpec((tm, D_MODEL), lambda i: (i, 0))
    ff = pl.BlockSpec((tm, D_FF), lambda i: (i, 0))
    res = pl.pallas_call(
        body,
        name=f"ffn_fwd_{tag}",
        grid=(nt,),
        in_specs=[row, pl.BlockSpec((1, D_MODEL), lambda i: (0, 0)), _ffn_w_in_spec(slot), _ffn_w_out_spec(slot)]
        + [ANY] * (2 * nbg),
        out_specs=[row, ff, ff] + [ANY] * nbg,
        out_shape=[
            jax.ShapeDtypeStruct((T, D_MODEL), F32),
            jax.ShapeDtypeStruct((T, D_FF), BF16),
            jax.ShapeDtypeStruct((T, D_FF), BF16),
        ] + [jax.ShapeDtypeStruct(l.shape, l.dtype) for l in bg_lands],
        input_output_aliases={4 + nbg + t: 3 + t for t in range(nbg)},
        scratch_shapes=[pltpu.SemaphoreType.DMA((nbg, N_PEER_CHIPS))] * (4 if nbg else 0),
        compiler_params=_cparams(),
    )(h, g, w_in, w_out, *bg_shards, *bg_lands)
    return (res[0], res[1], res[2], list(res[3:])) if nbg else tuple(res)


def ffn_bwd(tag, dh, h, g, gate, up, w_in, w_out, slot):
    T = dh.shape[0]
    tm = FFN_ROWS
    nch = D_FF // FF_CHUNK

    def body(dh_ref, h_ref, g_ref, gate_ref, up_ref, wi_ref, wo_ref,
             dhin_ref, xn_ref, dg_ref, du_ref, a_ref, dhb_ref, dnorm_ref):
        @pl.when(pl.program_id(0) == 0)
        def _():
            dnorm_ref[...] = jnp.zeros_like(dnorm_ref)

        dhh = dh_ref[...]
        dhb = (FFN_RES_SCALE * dhh).astype(BF16)
        dhb_ref[...] = dhb
        dxn = None
        for j in range(nch):
            cols = slice(j * FF_CHUNK, (j + 1) * FF_CHUNK)
            da = _dot_nt(dhb, wo_ref[2 * j:2 * j + 2].reshape(FF_CHUNK, D_MODEL))
            gt = gate_ref[:, cols].astype(F32)
            u = up_ref[:, cols].astype(F32)
            s = jax.nn.sigmoid(gt)
            silu = gt * s
            a_ref[:, cols] = (silu * u).astype(BF16)
            dgate = (da * u * (s * (1.0 + gt * (1.0 - s)))).astype(BF16)
            dup = (da * silu).astype(BF16)
            dg_ref[:, cols] = dgate
            du_ref[:, cols] = dup
            part = _dot_nt(dgate, wi_ref[j]) + _dot_nt(dup, wi_ref[nch + j])
            dxn = part if dxn is None else dxn + part
        hh = h_ref[...]
        gg = g_ref[...]
        r = _rstd(hh)
        hr = hh * r
        xn_ref[...] = (hr * gg).astype(BF16)
        dnorm_ref[...] += jnp.sum(dxn * hr, axis=0, keepdims=True)
        gd = gg * dxn
        dhin_ref[...] = dhh + r * (gd - hr * jnp.mean(gd * hr, axis=-1, keepdims=True))

    row = pl.BlockSpec((tm, D_MODEL), lambda i: (i, 0))
    ff = pl.BlockSpec((tm, D_FF), lambda i: (i, 0))
    vec = pl.BlockSpec((1, D_MODEL), lambda i: (0, 0))
    return pl.pallas_call(
        body,
        name=f"ffn_bwd_{tag}",
        grid=(T // tm,),
        in_specs=[row, row, vec, ff, ff, _ffn_w_in_spec(slot), _ffn_w_out_spec(slot)],
        out_specs=[row, row, ff, ff, ff, row, vec],
        out_shape=[
            jax.ShapeDtypeStruct((T, D_MODEL), F32),
            jax.ShapeDtypeStruct((T, D_MODEL), BF16),
            jax.ShapeDtypeStruct((T, D_FF), BF16),
            jax.ShapeDtypeStruct((T, D_FF), BF16),
            jax.ShapeDtypeStruct((T, D_FF), BF16),
            jax.ShapeDtypeStruct((T, D_MODEL), BF16),
            jax.ShapeDtypeStruct((1, D_MODEL), F32),
        ],
        compiler_params=_cparams(),
    )(dh, h, g, gate, up, w_in, w_out)


def rms_linear(name, h, g, w, w_spec, w_prep, n_out, tn, *, rope=None, scale=None):
    T = h.shape[0]
    tm = 512
    extra, extra_specs = [], []
    if rope is not None:
        extra += list(rope)
        extra_specs += [pl.BlockSpec((tm, LANES), lambda i, j: (i, 0))] * 2
    if scale is not None:
        extra.append(scale)
        extra_specs.append(pl.BlockSpec((1, tn), lambda i, j: (0, j)))

    def body(h_ref, g_ref, w_ref, *rest):
        rest = list(rest)
        cos_ref = sin_ref = sc_ref = None
        if rope is not None:
            cos_ref, sin_ref = rest[0], rest[1]
            rest = rest[2:]
        if scale is not None:
            sc_ref = rest[0]
            rest = rest[1:]
        out_ref, xn_s = rest

        @pl.when(pl.program_id(1) == 0)
        def _():
            hh = h_ref[...]
            xn_s[...] = (hh * _rstd(hh) * g_ref[...]).astype(BF16)

        y = _dot(xn_s[...], w_prep(w_ref[...]))
        if rope is not None:
            y = y * _tile_lanes(cos_ref[...], tn) + _swap32(y) * _tile_lanes(sin_ref[...], tn)
        if scale is not None:
            y = y * sc_ref[...]
        out_ref[...] = y.astype(BF16)

    return pl.pallas_call(
        body,
        name=name,
        grid=(T // tm, n_out // tn),
        in_specs=[
            pl.BlockSpec((tm, D_MODEL), lambda i, j: (i, 0)),
            pl.BlockSpec((1, D_MODEL), lambda i, j: (0, 0)),
            w_spec,
        ] + extra_specs,
        out_specs=pl.BlockSpec((tm, tn), lambda i, j: (i, j)),
        out_shape=jax.ShapeDtypeStruct((T, n_out), BF16),
        scratch_shapes=[pltpu.VMEM((tm, D_MODEL), BF16)],
        compiler_params=_cparams(),
    )(h, g, w, *extra)


QKV_ROWS = 512


def _qkv_w_spec():
    return pl.BlockSpec((N_CHIPS, D_MODEL, QKV_COLS), lambda i: (0, 0, 0), pipeline_mode=pl.Buffered(1))


def qkv_fwd(h, g, w_qkv, scale):
    T = h.shape[0]
    tm = QKV_ROWS

    def body(h_ref, g_ref, w_ref, sc_ref, out_ref):
        hh = h_ref[...]
        xn = (hh * _rstd(hh) * g_ref[...]).astype(BF16)
        for s in range(N_CHIPS):
            cols = slice(s * QKV_COLS, (s + 1) * QKV_COLS)
            out_ref[:, cols] = (_dot(xn, w_ref[s]) * sc_ref[:, cols]).astype(BF16)

    return pl.pallas_call(
        body,
        name="sb_qkv",
        grid=(T // tm,),
        in_specs=[
            pl.BlockSpec((tm, D_MODEL), lambda i: (i, 0)),
            pl.BlockSpec((1, D_MODEL), lambda i: (0, 0)),
            _qkv_w_spec(),
            pl.BlockSpec((1, 3 * D_MODEL), lambda i: (0, 0)),
        ],
        out_specs=pl.BlockSpec((tm, 3 * D_MODEL), lambda i: (i, 0)),
        out_shape=jax.ShapeDtypeStruct((T, 3 * D_MODEL), BF16),
        compiler_params=_cparams(),
    )(h, g, w_qkv, scale)


def qkv_bwd(dy, w_qkv, h, g, dres):
    T = h.shape[0]
    tm = QKV_ROWS

    def body(dy_ref, w_ref, h_ref, g_ref, dres_ref, dh_ref, xn_ref, dg_ref):
        @pl.when(pl.program_id(0) == 0)
        def _():
            dg_ref[...] = jnp.zeros_like(dg_ref)

        dxn = None
        for s in range(N_CHIPS):
            part = _dot_nt(dy_ref[:, s * QKV_COLS:(s + 1) * QKV_COLS], w_ref[s])
            dxn = part if dxn is None else dxn + part
        hh = h_ref[...]
        gg = g_ref[...]
        r = _rstd(hh)
        hr = hh * r
        xn_ref[...] = (hr * gg).astype(BF16)
        dg_ref[...] += jnp.sum(dxn * hr, axis=0, keepdims=True)
        gd = gg * dxn
        dh_ref[...] = dres_ref[...] + r * (gd - hr * jnp.mean(gd * hr, axis=-1, keepdims=True))

    row = pl.BlockSpec((tm, D_MODEL), lambda i: (i, 0))
    vec = pl.BlockSpec((1, D_MODEL), lambda i: (0, 0))
    return pl.pallas_call(
        body,
        name="sb_qkv_bwd",
        grid=(T // tm,),
        in_specs=[pl.BlockSpec((tm, 3 * D_MODEL), lambda i: (i, 0)), _qkv_w_spec(), row, vec, row],
        out_specs=[row, row, vec],
        out_shape=[
            jax.ShapeDtypeStruct((T, D_MODEL), F32),
            jax.ShapeDtypeStruct((T, D_MODEL), BF16),
            jax.ShapeDtypeStruct((1, D_MODEL), F32),
        ],
        compiler_params=_cparams(),
    )(dy, w_qkv, h, g, dres)


def linear_res(name, a, w_sq, t, res, bg_lands=()):
    T = a.shape[0]
    tm = 512
    nbg = len(bg_lands)
    nt = T // tm

    def body(a_ref, w_ref, res_ref, *rest):
        out_ref = rest[nbg]
        if nbg:
            gather = GatherOps([l.shape[2] for l in bg_lands], None, rest[nbg + 1:2 * nbg + 1], None, None,
                               *rest[2 * nbg + 1:])

            @pl.when(pl.program_id(0) == 0)
            def _():
                gather.start_forwards()

        out_ref[...] = res_ref[...] + _dot(a_ref[...], w_ref[...].reshape(D_MODEL, D_MODEL))
        if nbg:
            @pl.when(pl.program_id(0) == nt - 1)
            def _():
                gather.wait_forwards()

    row = pl.BlockSpec((tm, D_MODEL), lambda i: (i, 0))
    res_ = pl.pallas_call(
        body,
        name=name,
        grid=(nt,),
        in_specs=[row, pl.BlockSpec((N_CHIPS, None, SQ_ROWS, D_MODEL), lambda i: (0, t, 0, 0)), row] + [ANY] * nbg,
        out_specs=[row] + [ANY] * nbg,
        out_shape=[jax.ShapeDtypeStruct((T, D_MODEL), F32)] + [jax.ShapeDtypeStruct(l.shape, l.dtype) for l in bg_lands],
        input_output_aliases={3 + k: 1 + k for k in range(nbg)},
        scratch_shapes=[pltpu.SemaphoreType.DMA((nbg, N_PEER_CHIPS))] * (2 if nbg else 0),
        compiler_params=_cparams(),
    )(a, w_sq, res, *bg_lands)
    return (res_[0], list(res_[1:])) if nbg else res_[0]


def linear_bwd_plain(name, dy, w_sq, t):
    T = dy.shape[0]
    tm = 512

    def body(dy_ref, w_ref, da_ref, dyb_ref):
        dyb = dy_ref[...].astype(BF16)
        dyb_ref[...] = dyb
        da_ref[...] = _dot_nt(dyb, w_ref[...].reshape(D_MODEL, D_MODEL)).astype(BF16)

    row = pl.BlockSpec((tm, D_MODEL), lambda i: (i, 0))
    return pl.pallas_call(
        body,
        name=name,
        grid=(T // tm,),
        in_specs=[row, pl.BlockSpec((N_CHIPS, None, SQ_ROWS, D_MODEL), lambda i: (0, t, 0, 0))],
        out_specs=[row, row],
        out_shape=[jax.ShapeDtypeStruct((T, D_MODEL), BF16), jax.ShapeDtypeStruct((T, D_MODEL), BF16)],
        compiler_params=_cparams(),
    )(dy, w_sq)


def linear_bwd_rms(name, pairs, h, g, dres, nch, tn, tm=256):
    T = h.shape[0]
    npair = len(pairs)

    def body(*refs):
        dy_refs = refs[:npair]
        w_refs = refs[npair:2 * npair]
        h_ref, g_ref, dres_ref, dh_ref, xn_ref, dg_ref, acc_s = refs[2 * npair:]
        i = pl.program_id(0)
        j = pl.program_id(1)

        @pl.when(j == 0)
        def _():
            acc_s[...] = jnp.zeros_like(acc_s)

        @pl.when((i == 0) & (j == 0))
        def _():
            dg_ref[...] = jnp.zeros_like(dg_ref)

        part = None
        for p in range(npair):
            d = _dot_nt(dy_refs[p][...], pairs[p][3](w_refs[p][...]))
            part = d if part is None else part + d
        acc_s[...] += part

        @pl.when(j == nch - 1)
        def _():
            dxn = acc_s[...]
            hh = h_ref[...]
            gg = g_ref[...]
            r = _rstd(hh)
            hr = hh * r
            xn_ref[...] = (hr * gg).astype(BF16)
            dg_ref[...] += jnp.sum(dxn * hr, axis=0, keepdims=True)
            gd = gg * dxn
            dh_ref[...] = dres_ref[...] + r * (gd - hr * jnp.mean(gd * hr, axis=-1, keepdims=True))

    row = pl.BlockSpec((tm, D_MODEL), lambda i, j: (i, 0))
    vec = pl.BlockSpec((1, D_MODEL), lambda i, j: (0, 0))
    return pl.pallas_call(
        body,
        name=name,
        grid=(T // tm, nch),
        in_specs=[pl.BlockSpec((tm, tn), lambda i, j: (i, j))] * npair + [p[2] for p in pairs] + [row, vec, row],
        out_specs=[row, row, vec],
        out_shape=[
            jax.ShapeDtypeStruct((T, D_MODEL), F32),
            jax.ShapeDtypeStruct((T, D_MODEL), BF16),
            jax.ShapeDtypeStruct((1, D_MODEL), F32),
        ],
        scratch_shapes=[pltpu.VMEM((tm, D_MODEL), F32)],
        compiler_params=_cparams(),
    )(*[p[0] for p in pairs], *[p[1] for p in pairs], h, g, dres)


def loss_bwd(h, g, tgt):
    T = h.shape[0]
    tm = 512

    def body(h_ref, g_ref, t_ref, dh_ref, loss_ref, dg_ref):
        @pl.when(pl.program_id(0) == 0)
        def _():
            loss_ref[...] = jnp.zeros_like(loss_ref)
            dg_ref[...] = jnp.zeros_like(dg_ref)

        hh = h_ref[...]
        gg = g_ref[...]
        r = _rstd(hh)
        hr = hh * r
        err = hr * gg - t_ref[...]
        loss_ref[...] += 0.5 * jnp.sum(jnp.mean(err * err, axis=-1, keepdims=True), axis=0, keepdims=True)
        dy = err * (1.0 / D_MODEL)
        dg_ref[...] += jnp.sum(dy * hr, axis=0, keepdims=True)
        gd = gg * dy
        dh_ref[...] = r * (gd - hr * jnp.mean(gd * hr, axis=-1, keepdims=True))

    row = pl.BlockSpec((tm, D_MODEL), lambda i: (i, 0))
    vec = pl.BlockSpec((1, D_MODEL), lambda i: (0, 0))
    return pl.pallas_call(
        body,
        name="loss_bwd",
        grid=(T // tm,),
        in_specs=[row, vec, row],
        out_specs=[row, pl.BlockSpec((1, LANES), lambda i: (0, 0)), vec],
        out_shape=[
            jax.ShapeDtypeStruct((T, D_MODEL), F32),
            jax.ShapeDtypeStruct((1, LANES), F32),
            jax.ShapeDtypeStruct((1, D_MODEL), F32),
        ],
        compiler_params=_cparams(),
    )(h, g, tgt)


DW_TOKENS = 4096


def mm_tn(name, a, b, tk, tn, out_block, out_index, out_shape, prev=None, tt=DW_TOKENS):
    T = a.shape[0]
    ns, r = out_block[1], out_block[3]
    tt = min(tt, T)
    nt = T // tt

    def body(*refs):
        if prev is None:
            a_ref, b_ref, out_ref = refs
        else:
            a_ref, b_ref, _, out_ref = refs
        t = pl.program_id(2)
        res = _dot_tn(a_ref[...], b_ref[...])

        @pl.when(t == 0)
        def _():
            for u in range(ns):
                out_ref[u] = res[u * r:(u + 1) * r]

        @pl.when(t > 0)
        def _():
            for u in range(ns):
                out_ref[u] += res[u * r:(u + 1) * r]

    in_specs = [
        pl.BlockSpec((tt, tk), lambda k, n, t: (t, k)),
        pl.BlockSpec((tt, tn), lambda k, n, t: (t, n)),
    ]
    args = [a, b]
    aliases = {}
    if prev is not None:
        in_specs.append(pl.BlockSpec(memory_space=pl.ANY))
        args.append(prev)
        aliases = {2: 0}
    return pl.pallas_call(
        body,
        name=name,
        grid=(a.shape[1] // tk, b.shape[1] // tn, nt),
        in_specs=in_specs,
        out_specs=pl.BlockSpec(out_block, lambda k, n, t: out_index(k, n)),
        out_shape=jax.ShapeDtypeStruct(out_shape, F32),
        input_output_aliases=aliases,
        compiler_params=_cparams(),
    )(*args)


SB_BLOCK = 256
SB_QROWS = 256
SB_QROWS_BWD = 256
SB_UNDERFLOW_BITS = 140.0
SB_CHUNK = 128


LOG2E = 1.4426950408889634


def _softplus2(z2):
    sign = jnp.uint32(0x80000000)
    neg_abs = lax.bitcast_convert_type(lax.bitcast_convert_type(z2, jnp.uint32) | sign, F32)
    return jnp.log2(1.0 + jnp.exp2(neg_abs)) + jnp.maximum(z2, 0.0)


def _twice(x):
    return jnp.concatenate([x, x], axis=1)


def sb_fwd(qkv, bg_shards=(), bg_lands=()):
    T = qkv.shape[0]
    tq, tk = SB_QROWS, SB_BLOCK
    ratio = tq // tk
    npair = SB_HEADS // 2
    nbg = len(bg_shards)
    nq = T // tq

    def body(q_ref, k_ref, v_ref, *rest):
        bg_in = rest[:nbg]
        o_ref, tot_ref, first_ref = rest[2 * nbg:2 * nbg + 3]
        bg_out = rest[2 * nbg + 3:3 * nbg + 3]
        acc_s, c_s, z_s, w_s, kmax_s = rest[3 * nbg + 3:3 * nbg + 8]
        p = pl.program_id(0)
        i = pl.program_id(1)
        if nbg:
            gather = GatherOps([s.shape[1] for s in bg_shards], bg_in, bg_out, *rest[3 * nbg + 8:])

            @pl.when((p == 0) & (i == 0))
            def _():
                gather.start_ici()

        @pl.when(i == 0)
        def _():
            kmax_s[...] = jnp.max(jnp.abs(k_ref[...]), axis=0, keepdims=True).astype(F32)

        q = q_ref[...]
        lane = lax.broadcasted_iota(jnp.int32, (tq, LANES), 1)
        first = lane < HEAD_DIM
        zero = jnp.zeros_like(q)
        q_heads = (jnp.where(first, q, zero), jnp.where(first, zero, q))
        row = lax.broadcasted_iota(jnp.int32, (tq, tk), 0)
        col = lax.broadcasted_iota(jnp.int32, (tq, tk), 1)
        visible = [col + r * tk < row for r in range(ratio)]
        krow = lax.broadcasted_iota(jnp.int32, (tk, tk), 0)
        kcol = lax.broadcasted_iota(jnp.int32, (tk, tk), 1)
        from_s = (krow >= kcol).astype(BF16)
        acc_s[...] = jnp.zeros_like(acc_s)
        c_s[...] = jnp.zeros_like(c_s)

        def rows(j):
            return pl.ds(pl.multiple_of(j * tk, tk), tk)

        def logits(j):
            kb = k_ref[rows(j), :]
            for hd in range(2):
                z_s[hd] = _dot_nt(q_heads[hd], kb) * LOG2E

        def flush(j):
            vb = v_ref[rows(j), :]
            for hd in range(2):
                acc_s[hd] += _dot(w_s[hd], vb)

        def block(j, mask=None, walked_before=True):
            if walked_before:
                flush(j + 1)
            chunks = [(hd, slice(r0, r0 + SB_CHUNK)) for hd in range(2) for r0 in range(0, tq, SB_CHUNK)]
            k_next = k_ref[rows(jnp.maximum(j - 1, 0)), :]
            es, sums = [], []
            for hd, rs in chunks:
                z2 = z_s[hd, rs, :]
                z_s[hd, rs, :] = _dot_nt(q_heads[hd][rs, :], k_next) * LOG2E
                if mask is not None:
                    z2 = jnp.where(mask[rs, :], z2, NEG_BIG)
                sp = _softplus2(z2)
                c = c_s[hd, rs, :]
                es.append(z2 + _twice(c))
                c_s[hd, rs, :] = c - jnp.sum(sp, axis=1, keepdims=True)
                sums.append(_dot(sp.astype(BF16), from_s))
            for (hd, rs), e, s in zip(chunks, es, sums):
                w_s[hd, rs, :] = jnp.exp2(e - s).astype(BF16)

        z_bound = [LOG2E * jnp.sum(jnp.abs(q_heads[hd].astype(F32)) * kmax_s[...], axis=1, keepdims=True)
                   for hd in range(2)]

        def more_keys_matter():
            top = jnp.maximum(c_s[0] + z_bound[0], c_s[1] + z_bound[1])
            return (jnp.max(top) >= -SB_UNDERFLOW_BITS).astype(jnp.int32)

        logits(ratio * i + ratio - 1)
        for r in reversed(range(ratio)):
            block(ratio * i + r, visible[r], walked_before=(r != ratio - 1))

        def trip(carry):
            trips, _ = carry
            for r in reversed(range(ratio)):
                block(ratio * (i - 1 - trips) + r)
            return trips + 1, more_keys_matter()

        trips, _ = lax.while_loop(lambda carry: jnp.logical_and(carry[0] < i, carry[1] > 0), trip,
                                  (jnp.int32(0), jnp.int32(1)))
        first_walked = ratio * (i - trips)
        flush(first_walked)
        first_ref[p, i] = first_walked.astype(F32)
        o_ref[...] = jnp.where(first, acc_s[0], acc_s[1]).astype(BF16)
        tot_ref[...] = jnp.where(first, c_s[0], c_s[1])
        if nbg:
            @pl.when((p == npair - 1) & (i == nq - 1))
            def _():
                gather.wait_ici()

    res = pl.pallas_call(
        body,
        name="sb_fwd",
        grid=(npair, nq),
        in_specs=[
            pl.BlockSpec((tq, LANES), lambda p, i: (i, p)),
            pl.BlockSpec((T, LANES), lambda p, i: (0, npair + p)),
            pl.BlockSpec((T, LANES), lambda p, i: (0, 2 * npair + p)),
        ] + [ANY] * (2 * nbg),
        out_specs=[pl.BlockSpec((tq, LANES), lambda p, i: (i, p))] * 2 + [pl.BlockSpec(memory_space=pltpu.SMEM)]
        + [ANY] * nbg,
        out_shape=[jax.ShapeDtypeStruct((T, D_MODEL), BF16), jax.ShapeDtypeStruct((T, D_MODEL), F32),
                   jax.ShapeDtypeStruct((npair, nq), F32)]
        + [jax.ShapeDtypeStruct(l.shape, l.dtype) for l in bg_lands],
        input_output_aliases={3 + nbg + t: 3 + t for t in range(nbg)},
        scratch_shapes=[
            pltpu.VMEM((2, tq, LANES), F32), pltpu.VMEM((2, tq, LANES), F32),
            pltpu.VMEM((2, tq, tk), F32), pltpu.VMEM((2, tq, tk), BF16),
            pltpu.VMEM((1, LANES), F32),
        ] + [pltpu.SemaphoreType.DMA((nbg, N_PEER_CHIPS))] * (2 if nbg else 0),
        compiler_params=_cparams(),
    )(qkv, qkv, qkv, *bg_shards, *bg_lands)
    return res[0], res[1], res[2], list(res[3:])


def sb_bwd(qkv, do, tot, first_block, bg_parts=()):
    T = qkv.shape[0]
    tq, tk = SB_QROWS_BWD, SB_BLOCK
    ratio = tq // tk
    npair = SB_HEADS // 2
    nq = T // tq
    nk = T // tk
    nbg = len(bg_parts)
    assert SB_QROWS == SB_QROWS_BWD

    def body(first_ref, q_ref, k_ref, v_ref, do_ref, tot_ref, *rest):
        bg_in = rest[:nbg]
        dq_ref, dk_ref, dv_ref = rest[nbg:nbg + 3]
        bg_out = rest[nbg + 3:2 * nbg + 3]
        dkt_s, dvt_s, dq_s, rest_s, cg_s, z_s, da_s, dz_s, a_s = rest[2 * nbg + 3:2 * nbg + 12]
        i = pl.program_id(1)
        start = jnp.clip(first_ref[pl.program_id(0), i].astype(jnp.int32), 0, ratio * i)
        if nbg:
            @pl.when((pl.program_id(0) == 0) & (i == 0))
            def _():
                for cp in chip_partial_copies(bg_in, bg_out, *rest[2 * nbg + 12:]):
                    cp.start()

        @pl.when(i == 0)
        def _():
            dkt_s[...] = jnp.zeros_like(dkt_s)
            dvt_s[...] = jnp.zeros_like(dvt_s)

        q = q_ref[...]
        do_ = do_ref[...]
        tot_ = tot_ref[...]
        q_t = q.astype(F32).T.astype(BF16)
        do_t = do_.astype(F32).T.astype(BF16)
        lane = lax.broadcasted_iota(jnp.int32, (tq, LANES), 1)
        first = lane < HEAD_DIM
        zero = jnp.zeros_like(q)
        q_heads = (jnp.where(first, q, zero), jnp.where(first, zero, q))
        do_heads = (jnp.where(first, do_, zero), jnp.where(first, zero, do_))
        row = lax.broadcasted_iota(jnp.int32, (tq, tk), 0)
        col = lax.broadcasted_iota(jnp.int32, (tq, tk), 1)
        visible = [col + r * tk < row for r in range(ratio)]
        krow = lax.broadcasted_iota(jnp.int32, (tk, tk), 0)
        kcol = lax.broadcasted_iota(jnp.int32, (tk, tk), 1)
        before = (krow < kcol).astype(BF16)
        from_s = (krow >= kcol).astype(BF16)
        last = ratio * i + ratio - 1
        rest_s[0] = jnp.broadcast_to(tot_[:, 0:1], (tq, LANES))
        rest_s[1] = jnp.broadcast_to(tot_[:, HEAD_DIM:HEAD_DIM + 1], (tq, LANES))
        cg_s[...] = jnp.zeros_like(cg_s)
        dq_s[...] = jnp.zeros_like(dq_s)
        dz_s[...] = jnp.zeros_like(dz_s)
        a_s[...] = jnp.zeros_like(a_s)

        def rows(j):
            return pl.ds(pl.multiple_of(j * tk, tk), tk)

        def logits(j):
            kb = k_ref[rows(j), :]
            vb = v_ref[rows(j), :]
            for hd in range(2):
                z_s[hd] = _dot_nt(q_heads[hd], kb) * LOG2E
                da_s[hd] = _dot_nt(do_heads[hd], vb)

        def flush(j):
            kb = k_ref[rows(j), :]
            for hd in range(2):
                dims = slice(hd * HEAD_DIM, (hd + 1) * HEAD_DIM)
                dq_s[hd] += _dot(dz_s[hd], kb)
                dkt_s[j, dims, :] += _dot(q_t[dims, :], dz_s[hd])
                dvt_s[j, dims, :] += _dot(do_t[dims, :], a_s[hd])

        def block(j, mask=None):
            flush(jnp.maximum(j - 1, 0))
            chunks = [(hd, slice(r0, r0 + SB_CHUNK)) for hd in range(2) for r0 in range(0, tq, SB_CHUNK)]
            nxt = rows(jnp.minimum(j + 1, last))
            k_next = k_ref[nxt, :]
            v_next = v_ref[nxt, :]
            stage1 = []
            for hd, rs in chunks:
                z2 = z_s[hd, rs, :]
                z_s[hd, rs, :] = _dot_nt(q_heads[hd][rs, :], k_next) * LOG2E
                if mask is not None:
                    z2 = jnp.where(mask[rs, :], z2, NEG_BIG)
                sp = _softplus2(z2)
                rest = rest_s[hd, rs, :] + jnp.sum(sp, axis=1, keepdims=True)
                rest_s[hd, rs, :] = rest
                stage1.append((z2 + _twice(rest), z2 - sp, _dot(sp.astype(BF16), from_s)))
            stage2 = []
            for (hd, rs), (e, log2_beta, ahead) in zip(chunks, stage1):
                a = jnp.exp2(e - ahead)
                g = a * da_s[hd, rs, :]
                da_s[hd, rs, :] = _dot_nt(do_heads[hd][rs, :], v_next)
                cg = cg_s[hd, rs, :]
                a_s[hd, rs, :] = a.astype(BF16)
                cg_s[hd, rs, :] = cg + jnp.sum(g, axis=1, keepdims=True)
                stage2.append((g, g + _twice(cg), log2_beta, _dot(g.astype(BF16), before)))
            for (hd, rs), (g, g_from, log2_beta, g_before) in zip(chunks, stage2):
                dz_s[hd, rs, :] = (g - jnp.exp2(log2_beta) * (g_from + g_before)).astype(BF16)

        logits(start)

        @pl.loop(start, ratio * i)
        def _(j):
            block(j)

        for r in range(ratio):
            block(ratio * i + r, visible[r])
        flush(last)
        dq_ref[...] = (jnp.where(first, dq_s[0], dq_s[1]) * ATTN_SCALE).astype(BF16)

        @pl.when(i == nq - 1)
        def _():
            @pl.loop(0, nk)
            def _(b):
                dk_ref[rows(b), :] = dkt_s[b].T.astype(BF16)
                dv_ref[rows(b), :] = dvt_s[b].T.astype(BF16)

        if nbg:
            @pl.when((pl.program_id(0) == npair - 1) & (i == nq - 1))
            def _():
                for cp in chip_partial_copies(bg_in, bg_out, *rest[2 * nbg + 12:]):
                    cp.wait()

    qblk = pl.BlockSpec((tq, LANES), lambda p, i: (i, p))
    full = pl.BlockSpec((T, LANES), lambda p, i: (0, p))
    res = pl.pallas_call(
        body,
        name="sb_bwd",
        grid=(npair, nq),
        in_specs=[
            pl.BlockSpec(memory_space=pltpu.SMEM),
            qblk,
            pl.BlockSpec((T, LANES), lambda p, i: (0, npair + p)),
            pl.BlockSpec((T, LANES), lambda p, i: (0, 2 * npair + p)),
            qblk, qblk,
        ] + [ANY] * nbg,
        out_specs=[qblk, full, full] + [ANY] * nbg,
        out_shape=[jax.ShapeDtypeStruct((T, D_MODEL), BF16)] * 3
        + [jax.ShapeDtypeStruct(b.shape, b.dtype) for b in bg_parts],
        scratch_shapes=[
            pltpu.VMEM((nk, LANES, tk), F32), pltpu.VMEM((nk, LANES, tk), F32),
            pltpu.VMEM((2, tq, LANES), F32), pltpu.VMEM((2, tq, LANES), F32), pltpu.VMEM((2, tq, LANES), F32),
            pltpu.VMEM((2, tq, tk), F32), pltpu.VMEM((2, tq, tk), F32),
            pltpu.VMEM((2, tq, tk), BF16), pltpu.VMEM((2, tq, tk), BF16),
        ] + [pltpu.SemaphoreType.DMA((nbg, N_PEER_CHIPS))] * (2 if nbg else 0),
        compiler_params=_cparams(),
    )(first_block, qkv, qkv, qkv, do, tot, *bg_parts)
    return res[0], res[1], res[2], list(res[3:])


def _swa_valid(n):
    qi = lax.broadcasted_iota(jnp.int32, (WINDOW, 2 * WINDOW), 0)
    ki = lax.broadcasted_iota(jnp.int32, (WINDOW, 2 * WINDOW), 1)
    diff = qi + WINDOW - ki
    return (diff >= 0) & (diff < WINDOW) & ((n > 0) | (ki >= WINDOW))


def _to_half(x, first, src, dst):
    keep = first if src == 0 else jnp.logical_not(first)
    x = jnp.where(keep, x, jnp.zeros_like(x))
    if src != dst:
        x = pltpu.roll(x.astype(F32), HEAD_DIM, 1).astype(BF16)
    return x


SWA_GROUP = SWA_Q_HEADS // SWA_KV_HEADS


def _swa_cols(h):
    return slice((h // 2) * LANES, (h // 2 + 1) * LANES)


def _swa_kv_pair(h):
    return (h // SWA_GROUP) // 2


def _swa_kv_half(h):
    return (h // SWA_GROUP) % 2


def _kv_band(prev_ref, cur_ref, pb):
    cols = slice(pb * LANES, (pb + 1) * LANES)
    return jnp.concatenate([prev_ref[:, cols], cur_ref[:, cols]], axis=0)


def _swa_specs(T):
    nb = T // WINDOW
    kv_w = SWA_KV_HEADS * HEAD_DIM
    qrow = pl.BlockSpec((WINDOW, D_MODEL), lambda n: (n, 0))
    cur = pl.BlockSpec((WINDOW, kv_w), lambda n: (n, 0))
    prev = pl.BlockSpec((WINDOW, kv_w), lambda n: (jnp.maximum(n - 1, 0), 0))
    smem = pl.BlockSpec(memory_space=pltpu.SMEM)
    return nb, qrow, cur, prev, smem


def swa_fwd(q, k, v, sinks):
    T = q.shape[0]
    nb, qrow, cur, prev, smem = _swa_specs(T)

    def body(sink_ref, q_ref, kc_ref, kp_ref, vc_ref, vp_ref, o_ref, lse_ref):
        n = pl.program_id(0)
        lane = lax.broadcasted_iota(jnp.int32, (WINDOW, LANES), 1)
        first = lane < HEAD_DIM
        valid = _swa_valid(n)
        k2 = [_kv_band(kp_ref, kc_ref, pb) for pb in range(SWA_KV_HEADS // 2)]
        v2 = [_kv_band(vp_ref, vc_ref, pb) for pb in range(SWA_KV_HEADS // 2)]
        logits = [jnp.where(valid, _dot_nt(_to_half(q_ref[:, _swa_cols(h)], first, h % 2, _swa_kv_half(h)),
                                            k2[_swa_kv_pair(h)]), NEG_BIG) for h in range(SWA_Q_HEADS)]
        probs = []
        lse_acc = jnp.zeros((WINDOW, LANES), F32)
        for h, s in enumerate(logits):
            sink = sink_ref[h]
            m = jnp.maximum(jnp.max(s, axis=1, keepdims=True), sink)
            p = jnp.exp(s - m)
            den = jnp.sum(p, axis=1, keepdims=True) + jnp.exp(sink - m)
            probs.append((p / den).astype(BF16))
            lse_acc = jnp.where(lane == h, m + jnp.log(den), lse_acc)
        outs = []
        for h, p in enumerate(probs):
            o = _dot(p, v2[_swa_kv_pair(h)])
            outs.append(pltpu.roll(o, HEAD_DIM, 1) if h % 2 != _swa_kv_half(h) else o)
        for pair in range(SWA_Q_HEADS // 2):
            o_ref[:, _swa_cols(2 * pair)] = jnp.where(first, outs[2 * pair], outs[2 * pair + 1]).astype(BF16)
        lse_ref[...] = lse_acc

    return pl.pallas_call(
        body,
        name="swa_fwd",
        grid=(nb,),
        in_specs=[smem, qrow, cur, prev, cur, prev],
        out_specs=[qrow, pl.BlockSpec((WINDOW, LANES), lambda n: (n, 0))],
        out_shape=[jax.ShapeDtypeStruct((T, D_MODEL), BF16), jax.ShapeDtypeStruct((T, LANES), F32)],
        compiler_params=_cparams(),
    )(sinks, q, k, k, v, v)


def swa_bwd(q, k, v, sinks, do, o, lse, cos, sin):
    T = q.shape[0]
    nb, qrow, cur, prev, smem = _swa_specs(T)
    kv_w = SWA_KV_HEADS * HEAD_DIM

    def body(sink_ref, q_ref, kc_ref, kp_ref, vc_ref, vp_ref, do_ref, o_ref, lse_ref, cos_ref, sin_ref,
             dq_ref, own_ref, prv_ref, dsink_ref):
        n = pl.program_id(0)

        @pl.when(n == 0)
        def _():
            dsink_ref[...] = jnp.zeros_like(dsink_ref)

        lane = lax.broadcasted_iota(jnp.int32, (WINDOW, LANES), 1)
        lane1 = lax.broadcasted_iota(jnp.int32, (1, LANES), 1)
        first = lane < HEAD_DIM
        valid = _swa_valid(n)
        cos_ = cos_ref[...]
        sin_ = sin_ref[...]
        k2 = [_kv_band(kp_ref, kc_ref, pb) for pb in range(SWA_KV_HEADS // 2)]
        v2 = [_kv_band(vp_ref, vc_ref, pb) for pb in range(SWA_KV_HEADS // 2)]
        q_t = q_ref[...].astype(F32).T.astype(BF16)
        do_t = do_ref[...].astype(F32).T.astype(BF16)
        stage1 = []
        for h in range(SWA_Q_HEADS):
            a, b, pb = h % 2, _swa_kv_half(h), _swa_kv_pair(h)
            qh = _to_half(q_ref[:, _swa_cols(h)], first, a, b)
            doh = _to_half(do_ref[:, _swa_cols(h)], first, a, b)
            stage1.append((jnp.where(valid, _dot_nt(qh, k2[pb]), NEG_BIG), _dot_nt(doh, v2[pb])))
        deltas = []
        for pair in range(SWA_Q_HEADS // 2):
            prod = do_ref[:, _swa_cols(2 * pair)].astype(F32) * o_ref[:, _swa_cols(2 * pair)].astype(F32)
            deltas += [jnp.sum(jnp.where(first, prod, 0.0), axis=1, keepdims=True),
                       jnp.sum(jnp.where(first, 0.0, prod), axis=1, keepdims=True)]
        stage2 = []
        dsink = jnp.zeros((1, LANES), F32)
        for h, (s, dp) in enumerate(stage1):
            lse_h = lse_ref[:, h:h + 1]
            p = jnp.exp(s - lse_h)
            delta = deltas[h]
            p_sink = jnp.exp(sink_ref[h] - lse_h)
            dsink = dsink + jnp.where(lane1 == h, -jnp.sum(p_sink * delta, axis=0, keepdims=True), 0.0)
            stage2.append(((p * (dp - delta)).astype(BF16), p.astype(BF16)))
        dqs = []
        dk_t = [None] * SWA_KV_HEADS
        dv_t = [None] * SWA_KV_HEADS
        for h, (ds, pb16) in enumerate(stage2):
            kvh = h // SWA_GROUP
            dims = slice(h * HEAD_DIM, (h + 1) * HEAD_DIM)
            dq = _dot(ds, k2[_swa_kv_pair(h)])
            dqs.append(pltpu.roll(dq, HEAD_DIM, 1) if h % 2 != _swa_kv_half(h) else dq)
            dk_h = _dot(q_t[dims, :], ds)
            dv_h = _dot(do_t[dims, :], pb16)
            dk_t[kvh] = dk_h if dk_t[kvh] is None else dk_t[kvh] + dk_h
            dv_t[kvh] = dv_h if dv_t[kvh] is None else dv_t[kvh] + dv_h
        for pair in range(SWA_Q_HEADS // 2):
            dqp = jnp.where(first, dqs[2 * pair], dqs[2 * pair + 1])
            dq_ref[:, _swa_cols(2 * pair)] = ((dqp * cos_ + _swap32(dqp * sin_)) * ATTN_SCALE).astype(BF16)
        for pb in range(SWA_KV_HEADS // 2):
            dk2 = jnp.concatenate([dk_t[2 * pb], dk_t[2 * pb + 1]], axis=0).T
            dv2 = jnp.concatenate([dv_t[2 * pb], dv_t[2 * pb + 1]], axis=0).T
            kcols = slice(pb * LANES, (pb + 1) * LANES)
            vcols = slice(kv_w + pb * LANES, kv_w + (pb + 1) * LANES)
            prv_ref[:, kcols] = dk2[:WINDOW]
            own_ref[:, kcols] = dk2[WINDOW:]
            prv_ref[:, vcols] = dv2[:WINDOW]
            own_ref[:, vcols] = dv2[WINDOW:]
        dsink_ref[...] += dsink

    tab = pl.BlockSpec((WINDOW, LANES), lambda n: (n, 0))
    kvrow = pl.BlockSpec((WINDOW, 2 * kv_w), lambda n: (n, 0))
    return pl.pallas_call(
        body,
        name="swa_bwd",
        grid=(nb,),
        in_specs=[smem, qrow, cur, prev, cur, prev, qrow, qrow, tab, tab, tab],
        out_specs=[qrow, kvrow, kvrow, pl.BlockSpec((1, LANES), lambda n: (0, 0))],
        out_shape=[
            jax.ShapeDtypeStruct((T, D_MODEL), BF16),
            jax.ShapeDtypeStruct((T, 2 * kv_w), F32),
            jax.ShapeDtypeStruct((T, 2 * kv_w), F32),
            jax.ShapeDtypeStruct((1, LANES), F32),
        ],
        compiler_params=_cparams(),
    )(sinks, q, k, k, v, v, do, o, lse, cos, sin)


def kv_grad_combine(own, prv, cos, sin):
    T = own.shape[0]
    nb = T // WINDOW
    kv_w = SWA_KV_HEADS * HEAD_DIM

    def body(own_ref, nxt_ref, cos_ref, sin_ref, out_ref):
        n = pl.program_id(0)
        nxt = jnp.where(n + 1 < nb, nxt_ref[...], 0.0)
        tot = own_ref[...] + nxt
        dk = tot[:, :kv_w]
        c = _tile_lanes(cos_ref[...], kv_w)
        s = _tile_lanes(sin_ref[...], kv_w)
        out_ref[:, :kv_w] = (dk * c + _swap32(dk * s)).astype(BF16)
        out_ref[:, kv_w:] = tot[:, kv_w:].astype(BF16)

    tab = pl.BlockSpec((WINDOW, LANES), lambda n: (n, 0))
    kvrow = pl.BlockSpec((WINDOW, 2 * kv_w), lambda n: (n, 0))
    return pl.pallas_call(
        body,
        name="kv_grad_combine",
        grid=(nb,),
        in_specs=[kvrow, pl.BlockSpec((WINDOW, 2 * kv_w), lambda n: (jnp.minimum(n + 1, nb - 1), 0)), tab, tab],
        out_specs=kvrow,
        out_shape=jax.ShapeDtypeStruct((T, 2 * kv_w), BF16),
        compiler_params=_cparams(),
    )(own, prv, cos, sin)


ANY = pl.BlockSpec(memory_space=pl.ANY)


def _place():
    x, y, c = lax.axis_index("x"), lax.axis_index("y"), lax.axis_index("c")
    other_chips = [(1 - x, y), (x, 1 - y), (1 - x, 1 - y)]
    return x, y, c, 2 * x + y, other_chips


N_PEER_CHIPS = N_CHIPS - 1


class GatherOps:
    def __init__(self, rows, shards, lands, ici_send, ici_recv, d2d_send=None, d2d_recv=None):
        self.rows, self.shards, self.lands = rows, shards, lands
        self.ici_send, self.ici_recv, self.d2d_send, self.d2d_recv = ici_send, ici_recv, d2d_send, d2d_recv
        self.x, self.y, self.c, self.me, self.chips = _place()
        self.pairs = [(t, jdx) for t in range(len(rows)) for jdx in range(N_PEER_CHIPS)]

    def _half(self, ref, t, which):
        r = self.rows[t] // 2
        return ref.at[:, pl.ds(pl.multiple_of(which * r, 16), r), :]

    def _ici(self, t, jdx):
        px, py = self.chips[jdx]
        return pltpu.make_async_remote_copy(
            src_ref=self._half(self.shards[t], t, self.c), dst_ref=self._half(self.lands[t].at[self.me], t, self.c),
            send_sem=self.ici_send.at[t, jdx], recv_sem=self.ici_recv.at[t, jdx],
            device_id=(px, py, self.c), device_id_type=MESH)

    def _landed(self, t, jdx):
        px, py = self.chips[jdx]
        blk = self._half(self.lands[t].at[2 * px + py], t, self.c)
        return pltpu.make_async_remote_copy(
            src_ref=blk, dst_ref=blk, send_sem=self.ici_send.at[t, jdx], recv_sem=self.ici_recv.at[t, jdx],
            device_id=(px, py, self.c), device_id_type=MESH)

    def _d2d(self, t, jdx, which):
        px, py = self.chips[jdx]
        blk = self._half(self.lands[t].at[2 * px + py], t, which)
        return pltpu.make_async_remote_copy(
            src_ref=blk, dst_ref=blk, send_sem=self.d2d_send.at[t, jdx], recv_sem=self.d2d_recv.at[t, jdx],
            device_id=(self.x, self.y, 1 - self.c), device_id_type=MESH)

    def start_ici(self):
        for t, jdx in self.pairs:
            self._ici(t, jdx).start()

    def wait_ici(self):
        for t, jdx in self.pairs:
            self._landed(t, jdx).wait_recv()
        self.wait_ici_sends()

    def wait_ici_sends(self):
        for t, jdx in self.pairs:
            self._ici(t, jdx).wait_send()

    def forward_arrivals(self):
        for t, jdx in self.pairs:
            self._landed(t, jdx).wait_recv()
            self._d2d(t, jdx, self.c).start()

    def start_forwards(self):
        for t, jdx in self.pairs:
            self._d2d(t, jdx, self.c).start()

    def wait_forwards(self):
        for t, jdx in self.pairs:
            self._d2d(t, jdx, 1 - self.c).wait_recv()
            self._d2d(t, jdx, self.c).wait_send()


def all_gather_weights(shards, lands):
    n = len(shards)
    rows = [s.shape[1] for s in shards]

    def body(*refs):
        ins, outs = refs[:n], refs[2 * n:3 * n]
        ops = GatherOps(rows, ins, outs, *refs[3 * n:])
        ops.start_ici()
        ops.forward_arrivals()
        ops.wait_forwards()
        ops.wait_ici_sends()

    return pl.pallas_call(
        body,
        name="all_gather_weights",
        in_specs=[ANY] * (2 * n),
        out_specs=[ANY] * n,
        out_shape=[jax.ShapeDtypeStruct(l.shape, l.dtype) for l in lands],
        input_output_aliases={n + t: t for t in range(n)},
        scratch_shapes=[pltpu.SemaphoreType.DMA((n, N_PEER_CHIPS))] * 4,
    )(*shards, *lands)


def place_own_shard(name, shard, chip):
    nl, r, c = shard.shape

    def body(chip_ref, s_ref, o_ref):
        o_ref[...] = s_ref[...]

    return pl.pallas_call(
        body, name=name,
        grid_spec=pltpu.PrefetchScalarGridSpec(
            num_scalar_prefetch=1, grid=(nl,),
            in_specs=[pl.BlockSpec((None, r, c), lambda l, chip_ref: (l, 0, 0))],
            out_specs=pl.BlockSpec((None, None, r, c), lambda l, chip_ref: (chip_ref[0], l, 0, 0))),
        out_shape=jax.ShapeDtypeStruct((N_CHIPS,) + shard.shape, shard.dtype), compiler_params=_cparams(),
    )(chip, shard)


def exchange_halves(name, slabs):
    n = len(slabs)

    def body(*refs):
        ins, theirs = refs[:n], refs[n:2 * n]
        send_sems, recv_sems = refs[2 * n:]
        x, y, c, _, _ = _place()
        copies = []
        for t in range(n):
            cp = pltpu.make_async_remote_copy(
                src_ref=ins[t].at[1 - c], dst_ref=theirs[t], send_sem=send_sems.at[t],
                recv_sem=recv_sems.at[t], device_id=(x, y, 1 - c), device_id_type=MESH)
            cp.start()
            copies.append(cp)
        for cp in copies:
            cp.wait()

    return pl.pallas_call(
        body,
        name=name,
        in_specs=[ANY] * n,
        out_specs=[ANY] * n,
        out_shape=[jax.ShapeDtypeStruct(s.shape[1:], s.dtype) for s in slabs],
        scratch_shapes=[pltpu.SemaphoreType.DMA((n,)), pltpu.SemaphoreType.DMA((n,))],
    )(*slabs)


def chip_partial_copies(ins, outs, send_sems, recv_sems):
    _, _, c, me, chips = _place()
    return [pltpu.make_async_remote_copy(
        src_ref=ins[t].at[2 * px + py], dst_ref=outs[t].at[me], send_sem=send_sems.at[t, jdx],
        recv_sem=recv_sems.at[t, jdx], device_id=(px, py, c), device_id_type=MESH)
        for t in range(len(ins)) for jdx, (px, py) in enumerate(chips)]


def exchange_chip_partials(name, parts):
    n = len(parts)

    def body(*refs):
        copies = chip_partial_copies(refs[:n], refs[n:2 * n], *refs[2 * n:])
        for cp in copies:
            cp.start()
        for cp in copies:
            cp.wait()

    return pl.pallas_call(
        body,
        name=name,
        in_specs=[ANY] * n,
        out_specs=[ANY] * n,
        out_shape=[jax.ShapeDtypeStruct(p.shape, p.dtype) for p in parts],
        scratch_shapes=[pltpu.SemaphoreType.DMA((n, 3)), pltpu.SemaphoreType.DMA((n, 3))],
    )(*parts)


def share_reduced_halves(name, halves):
    n = len(halves)

    def body(*refs):
        ins, outs = refs[:n], refs[n:2 * n]
        send_sems, recv_sems = refs[2 * n:]
        x, y, c, _, _ = _place()
        copies = []
        for t in range(n):
            cp = pltpu.make_async_remote_copy(
                src_ref=ins[t], dst_ref=outs[t], send_sem=send_sems.at[t],
                recv_sem=recv_sems.at[t], device_id=(x, y, 1 - c), device_id_type=MESH)
            cp.start()
            copies.append(cp)
        for cp in copies:
            cp.wait()

    return pl.pallas_call(
        body,
        name=name,
        in_specs=[ANY] * n,
        out_specs=[ANY] * n,
        out_shape=[jax.ShapeDtypeStruct(h.shape, h.dtype) for h in halves],
        scratch_shapes=[pltpu.SemaphoreType.DMA((n,)), pltpu.SemaphoreType.DMA((n,))],
    )(*halves)


def _row_tile(r, c):
    tr = r
    while tr * c * 4 > (3 << 19) and tr % 16 == 0:
        tr //= 2
    return tr


def add_sibling(name, slab, theirs, core):
    _, ns, slots, r, c = slab.shape
    tr = _row_tile(r, c)

    def body(core_ref, a_ref, b_ref, o_ref):
        o_ref[...] = (a_ref[...] + b_ref[...]).astype(BF16)

    blk = pl.BlockSpec((None, None, tr, c), lambda s, l, i, core_ref: (s, l, i, 0))
    return pl.pallas_call(
        body, name=name,
        grid_spec=pltpu.PrefetchScalarGridSpec(
            num_scalar_prefetch=1, grid=(ns, slots, r // tr),
            in_specs=[pl.BlockSpec((None, None, None, tr, c), lambda s, l, i, core_ref: (core_ref[0], s, l, i, 0)), blk],
            out_specs=blk),
        out_shape=jax.ShapeDtypeStruct(theirs.shape, BF16), compiler_params=_cparams(),
    )(core, slab, theirs)


def sum_chips(name, recv, own, chip):
    _, slots, r, c = recv.shape
    tr = _row_tile(r, c)

    def body(chip_ref, r0, r1, r2, r3, own_ref, o_ref):
        me = chip_ref[0]
        mine = own_ref[...]
        terms = [jnp.where(me == s, mine, rr[...]).astype(F32) for s, rr in enumerate((r0, r1, r2, r3))]
        o_ref[...] = ((terms[0] + terms[1]) + terms[2]) + terms[3]

    def src(s):
        return pl.BlockSpec((None, None, tr, c),
                            lambda l, i, chip_ref: (jnp.where(chip_ref[0] == s, (s + 1) % N_CHIPS, s), l, i, 0))

    return pl.pallas_call(
        body, name=name,
        grid_spec=pltpu.PrefetchScalarGridSpec(
            num_scalar_prefetch=1, grid=(slots, r // tr),
            in_specs=[src(0), src(1), src(2), src(3),
                      pl.BlockSpec((None, None, tr, c), lambda l, i, chip_ref: (chip_ref[0], l, i, 0))],
            out_specs=pl.BlockSpec((None, tr, c), lambda l, i, chip_ref: (l, i, 0))),
        out_shape=jax.ShapeDtypeStruct((slots, r, c), F32), compiler_params=_cparams(),
    )(chip, recv, recv, recv, recv, own)


def _adamw_math(w, g, m, v):
    m = ADAM_B1 * m + (1.0 - ADAM_B1) * g
    v = ADAM_B2 * v + (1.0 - ADAM_B2) * (g * g)
    m_hat = m / (1.0 - ADAM_B1 ** ADAM_STEP)
    v_hat = v / (1.0 - ADAM_B2 ** ADAM_STEP)
    delta = -ADAM_LR * (m_hat / (jnp.sqrt(v_hat) + ADAM_EPS) + ADAM_WD * w)
    return delta, m, v


def adamw_shard(name, w, m, v, g_pairs, core, slots, row_halves):
    n = w.shape[0]
    assert n == len(g_pairs)
    _, r, c = g_pairs[0][0].shape
    tr = _row_tile(r, c)
    nr = r // tr

    def body(core_ref, w_ref, m_ref, v_ref, *rest):
        g_refs, (go_ref, d_ref, mo_ref, vo_ref) = rest[:2 * n], rest[2 * n:]
        mine = pl.program_id(1) == core_ref[0]
        g = jnp.where(mine, g_refs[0][...], g_refs[1][...])
        for l in range(1, n):
            g = jnp.where(pl.program_id(0) == l, jnp.where(mine, g_refs[2 * l][...], g_refs[2 * l + 1][...]), g)
        delta, mm, vv = _adamw_math(w_ref[...], g, m_ref[...], v_ref[...])
        go_ref[...] = g
        d_ref[...] = delta
        mo_ref[...] = mm
        vo_ref[...] = vv

    if row_halves:
        wspec = pl.BlockSpec((None, tr, c), lambda l, h, i, core_ref: (l, h * nr + i, 0))
    else:
        wspec = pl.BlockSpec((None, tr, c), lambda l, h, i, core_ref: (l, i, h))
    def gspec(slot):
        return pl.BlockSpec((None, tr, c), lambda l, h, i, core_ref: (slot, i, 0))

    shp = jax.ShapeDtypeStruct(w.shape, F32)
    return pl.pallas_call(
        body, name=name,
        grid_spec=pltpu.PrefetchScalarGridSpec(
            num_scalar_prefetch=1, grid=(n, 2, nr),
            in_specs=[wspec, wspec, wspec] + [gspec(s) for s in slots for _ in range(2)], out_specs=[wspec] * 4),
        out_shape=[shp] * 4, compiler_params=_cparams(),
    )(core, w, m, v, *[g for pair in g_pairs for g in pair])


SMALL_ROWS = 16


def small_allreduce_adamw(part, w, m, v):
    def body(p_ref, w_ref, m_ref, v_ref, g_ref, d_ref, mo_ref, vo_ref, buf, send_sems, recv_sems):
        x, y, c, _, _ = _place()
        me = 4 * x + 2 * y + c
        buf[me] = p_ref[...]
        copies = []
        for k in range(1, N_DEV):
            kx, ky, kc = (k >> 2) & 1, (k >> 1) & 1, k & 1
            peer = (x ^ kx, y ^ ky, c ^ kc)
            cp = pltpu.make_async_remote_copy(
                src_ref=p_ref, dst_ref=buf.at[me], send_sem=send_sems.at[k - 1],
                recv_sem=recv_sems.at[k - 1], device_id=peer, device_id_type=MESH)
            cp.start()
            copies.append(cp)
        for cp in copies:
            cp.wait()
        g = buf[0]
        for dev in range(1, N_DEV):
            g = g + buf[dev]
        delta, mm, vv = _adamw_math(w_ref[...], g, m_ref[...], v_ref[...])
        g_ref[...] = g
        d_ref[...] = delta
        mo_ref[...] = mm
        vo_ref[...] = vv

    vm = pl.BlockSpec(memory_space=pltpu.VMEM)
    shp = jax.ShapeDtypeStruct(part.shape, F32)
    return pl.pallas_call(
        body, name="small_allreduce_adamw",
        in_specs=[vm] * 4, out_specs=[vm] * 4, out_shape=[shp] * 4,
        scratch_shapes=[
            pltpu.VMEM((N_DEV,) + part.shape, F32),
            pltpu.SemaphoreType.DMA((N_DEV - 1,)), pltpu.SemaphoreType.DMA((N_DEV - 1,)),
        ],
    )(part, w, m, v)


def _rope_tables(T):
    half = HEAD_DIM // 2
    inv_freq = ROPE_THETA ** (-jnp.arange(half, dtype=F32) / half)
    ang = jnp.arange(T).astype(F32)[:, None] * inv_freq[None, :]
    cos = jnp.tile(jnp.cos(ang), (1, LANES // half))
    sin = jnp.tile(jnp.sin(ang), (1, LANES // half))
    lane = jnp.arange(LANES)
    sign = jnp.where((lane % HEAD_DIM) < half, -1.0, 1.0).astype(F32)
    return cos, sin * sign[None, :]


def _pack_small(ffn1, mix, ffn2, kvn, fin, sinks, loss_row):
    sink_row = jnp.pad(sinks.reshape(1, SWA_Q_HEADS), ((0, 0), (0, D_MODEL - SWA_Q_HEADS)))
    rows = jnp.concatenate([ffn1, mix, ffn2, kvn.reshape(1, -1), fin.reshape(1, -1), sink_row, loss_row], axis=0)
    return jnp.concatenate([rows, jnp.zeros((SMALL_ROWS - rows.shape[0], D_MODEL), F32)], axis=0)


def kernel(x, ffn1_norm, ffn1_w_in, ffn1_w_out, mix_norm, ffn2_norm, ffn2_w_in, ffn2_w_out, sb_w_qkv, sb_w_o, kv_norm, kv_w, swa_w_q, swa_sinks, swa_w_o, final_norm, loss_target, m_ffn1_norm, m_ffn1_w_in, m_ffn1_w_out, m_mix_norm, m_ffn2_norm, m_ffn2_w_in, m_ffn2_w_out, m_sb_w_qkv, m_sb_w_o, m_kv_norm, m_kv_w, m_swa_w_q, m_swa_sinks, m_swa_w_o, m_final_norm, v_ffn1_norm, v_ffn1_w_in, v_ffn1_w_out, v_mix_norm, v_ffn2_norm, v_ffn2_w_in, v_ffn2_w_out, v_sb_w_qkv, v_sb_w_o, v_kv_norm, v_kv_w, v_swa_w_q, v_swa_sinks, v_swa_w_o, v_final_norm):
    T = x.shape[1]
    kv_cols = SWA_KV_HEADS * HEAD_DIM
    x2 = x.reshape(T, D_MODEL)
    tgt = loss_target.reshape(T, D_MODEL)
    cos, sin = _rope_tables(T)

    w_in_l = jnp.concatenate([ffn1_w_in, ffn2_w_in], axis=0).astype(BF16)
    w_out_l = jnp.concatenate([ffn1_w_out, ffn2_w_out], axis=0).astype(BF16)
    sq_l = jnp.concatenate([sb_w_o, swa_w_q, swa_w_o], axis=0).astype(BF16)
    qkv_l = sb_w_qkv[0].astype(BF16)
    kvw_l = kv_w.astype(BF16)
    core = lax.axis_index("c").astype(jnp.int32).reshape(1)
    chip = (2 * lax.axis_index("x") + lax.axis_index("y")).astype(jnp.int32).reshape(1)
    early = [w_in_l[:1], w_out_l[:1]]
    mid = [sq_l, qkv_l[None]]
    late = [w_in_l[1:], w_out_l[1:], kvw_l[None]]
    early_lands = [place_own_shard(f"own_early_{t}", s, chip) for t, s in enumerate(early)]
    mid_lands = [place_own_shard(f"own_mid_{t}", s, chip) for t, s in enumerate(mid)]
    late_lands = [place_own_shard(f"own_late_{t}", s, chip) for t, s in enumerate(late)]
    w_in0, w_out0 = all_gather_weights(early, early_lands)

    def ffn_w(slot):
        return (w_in0, w_out0, 0) if slot == 0 else (w_in_r, w_out_r, slot - 1)

    def vec(a, i):
        return a[i].reshape(1, D_MODEL)

    ident = lambda w: w
    sq_prep = lambda w: w.reshape(D_MODEL, w.shape[-1])
    qscale = jnp.concatenate([jnp.full((1, D_MODEL), ATTN_SCALE, F32), jnp.ones((1, 2 * D_MODEL), F32)], axis=1)
    swa_scale = jnp.full((1, D_MODEL), ATTN_SCALE, F32)
    sinks = swa_sinks.reshape(SWA_Q_HEADS)

    h1, gate1, up1, (w_sq, w_qkv) = ffn_fwd("l0a", x2, vec(ffn1_norm, 0), *ffn_w(SLOT_FFN1[0]), mid, mid_lands)
    w_qkv = w_qkv.reshape(N_CHIPS, D_MODEL, QKV_COLS)
    qkv = qkv_fwd(h1, vec(mix_norm, 0), w_qkv, qscale)
    o_sb, tot, sb_first, late_lands = sb_fwd(qkv, late, late_lands)
    h2, (w_in_r, w_out_r, w_kv) = linear_res("sb_out", o_sb, w_sq, SQ_SB_O, h1, late_lands)
    w_kv = w_kv.reshape(D_MODEL, 2 * kv_cols)
    h3, gate2, up2 = ffn_fwd("l0b", h2, vec(ffn2_norm, 0), *ffn_w(SLOT_FFN2[0]))
    kvn = kv_norm.reshape(1, D_MODEL)
    k_sw = rms_linear("kv_k", h3, kvn, w_kv, pl.BlockSpec((D_MODEL, kv_cols), lambda i, j: (0, 0)), ident,
                      kv_cols, kv_cols, rope=(cos, sin))
    v_sw = rms_linear("kv_v", h3, kvn, w_kv, pl.BlockSpec((D_MODEL, kv_cols), lambda i, j: (0, 1)), ident,
                      kv_cols, kv_cols)
    h4, gate3, up3 = ffn_fwd("l1a", h3, vec(ffn1_norm, 1), *ffn_w(SLOT_FFN1[1]))
    q_sw = rms_linear("swa_q", h4, vec(mix_norm, 1), w_sq,
                      pl.BlockSpec((N_CHIPS, None, SQ_ROWS, 512), lambda i, j: (0, SQ_SWA_Q, 0, j)), sq_prep,
                      D_MODEL, 512, rope=(cos, sin), scale=swa_scale)
    o_sw, lse = swa_fwd(q_sw, k_sw, v_sw, sinks)
    h5 = linear_res("swa_out", o_sw, w_sq, SQ_SWA_O, h4)
    h6, gate4, up4 = ffn_fwd("l1b", h5, vec(ffn2_norm, 1), *ffn_w(SLOT_FFN2[1]))
    dh6, loss_p, d_final = loss_bwd(h6, final_norm.reshape(1, D_MODEL), tgt)

    slab = {}
    ffn_place = {SLOT_FFN1[0]: (0, 0, 1), SLOT_FFN1[1]: (1, 0, 3), SLOT_FFN2[0]: (1, 1, 3), SLOT_FFN2[1]: (1, 2, 3)}
    sq_place = {SQ_SB_O: (1, 0, 3), SQ_SWA_Q: (1, 1, 3), SQ_SWA_O: (1, 2, 3)}

    def ffn_grads(tag, dh, h_in, g, gate, up, slot):
        dh_in, xn, dg_, du_, act, dhb, dnorm = ffn_bwd(tag, dh, h_in, g, gate, up, *ffn_w(slot))
        grp, s, ns = ffn_place[slot]
        in_shape = (2, N_CHIPS, ns, D_MODEL // 2, FF_CHUNK)
        out_shape = (2, N_CHIPS, ns, FF_ROWS, D_MODEL // 2)
        blk = (None, 1, None, D_MODEL // 2, FF_CHUNK)
        slab["in", grp] = mm_tn(f"dw_gate_{tag}", xn, dg_, D_MODEL // 2, FF_CHUNK, blk,
                                lambda k, n: (k, n, s, 0, 0), in_shape, prev=slab.get(("in", grp)))
        slab["in", grp] = mm_tn(f"dw_up_{tag}", xn, du_, D_MODEL // 2, FF_CHUNK, blk,
                                lambda k, n: (k, 2 + n, s, 0, 0), in_shape, prev=slab["in", grp])
        slab["out", grp] = mm_tn(f"dw_out_{tag}", act, dhb, FF_CHUNK, D_MODEL // 2,
                                 (None, 2, None, FF_ROWS, D_MODEL // 2),
                                 lambda k, n: (n, k, s, 0, 0), out_shape, prev=slab.get(("out", grp)))
        return dh_in, dnorm

    def sq_grad(tag, a, dyb, t):
        grp, s, ns = sq_place[t]
        slab["sq", grp] = mm_tn(f"dw_sq_{tag}", a, dyb, D_MODEL, D_MODEL // 2,
                                (None, N_CHIPS, None, SQ_ROWS, D_MODEL // 2),
                                lambda k, n: (n, 0, s, 0, 0), (2, N_CHIPS, ns, SQ_ROWS, D_MODEL // 2),
                                prev=slab.get(("sq", grp)))

    def reduce_group(grp, kinds, host=None):
        slabs = [slab[kind, grp] for kind in kinds]
        names = [f"{kind}{grp}" for kind in kinds]
        theirs = exchange_halves(f"exchange_halves_{grp}", slabs)
        parts = [add_sibling(f"add_sibling_{nm}", s, t, core) for nm, s, t in zip(names, slabs, theirs)]
        arrived = host(parts) if host else exchange_chip_partials(f"exchange_chip_partials_{grp}", parts)
        halves = [sum_chips(f"sum_chips_{nm}", g, p, chip) for nm, g, p in zip(names, arrived, parts)]
        sib_halves = share_reduced_halves(f"share_reduced_halves_{grp}", halves)
        return {kind: pair for kind, pair in zip(kinds, zip(halves, sib_halves))}

    dh5, d_ffn2_1 = ffn_grads("l1b", dh6, h5, vec(ffn2_norm, 1), gate4, up4, SLOT_FFN2[1])
    do_sw, dh5b = linear_bwd_plain("swa_out_bwd", dh5, w_sq, SQ_SWA_O)
    sq_grad("swa_o", o_sw, dh5b, SQ_SWA_O)
    dq_sw, kv_own, kv_prev, d_sinks = swa_bwd(q_sw, k_sw, v_sw, sinks, do_sw, o_sw, lse, cos, sin)
    sq_w_spec = pl.BlockSpec((N_CHIPS, None, SQ_ROWS, D_MODEL), lambda i, j: (0, SQ_SWA_Q, 0, 0))
    dh4, hn4, d_mix_1 = linear_bwd_rms("swa_q_bwd", [(dq_sw, w_sq, sq_w_spec, sq_prep)], h4, vec(mix_norm, 1), dh5,
                                       1, D_MODEL)
    sq_grad("swa_q", hn4, dq_sw, SQ_SWA_Q)
    dh3a, d_ffn1_1 = ffn_grads("l1a", dh4, h3, vec(ffn1_norm, 1), gate3, up3, SLOT_FFN1[1])
    dkv = kv_grad_combine(kv_own, kv_prev, cos, sin)
    kv_w_spec = pl.BlockSpec((D_MODEL, 2 * kv_cols), lambda i, j: (0, 0))
    dh3, xn3, d_kvn = linear_bwd_rms("kv_bwd", [(dkv, w_kv, kv_w_spec, ident)], h3, kvn, dh3a, 1, 2 * kv_cols)
    slab["kv", 1] = mm_tn("dw_kv", xn3, dkv, D_MODEL, kv_cols, (None, N_CHIPS, None, SQ_ROWS, kv_cols),
                          lambda k, n: (n, 0, 0, 0, 0), (2, N_CHIPS, 1, SQ_ROWS, kv_cols))
    dh2, d_ffn2_0 = ffn_grads("l0b", dh3, h2, vec(ffn2_norm, 0), gate2, up2, SLOT_FFN2[0])
    do_sb, dh2b = linear_bwd_plain("sb_out_bwd", dh2, w_sq, SQ_SB_O)
    sq_grad("sb_o", o_sb, dh2b, SQ_SB_O)
    sb_grads = []

    def behind_sb_bwd(parts):
        dq_sb, dk_sb, dv_sb, arrived = sb_bwd(qkv, do_sb, tot, sb_first, parts)
        sb_grads.extend([dq_sb, dk_sb, dv_sb])
        return arrived

    red = {1: reduce_group(1, ["in", "out", "sq", "kv"], host=behind_sb_bwd)}
    dqkv = jnp.concatenate(sb_grads, axis=1)
    dh1, hn1, d_mix_0 = qkv_bwd(dqkv, w_qkv, h1, vec(mix_norm, 0), dh2)
    slab["qkv", 0] = mm_tn("dw_qkv", hn1, dqkv, D_MODEL // 2, QKV_COLS, (None, 1, None, D_MODEL // 2, QKV_COLS),
                           lambda k, n: (k, n, 0, 0, 0), (2, N_CHIPS, 1, D_MODEL // 2, QKV_COLS))
    dx, d_ffn1_0 = ffn_grads("l0a", dh1, x2, vec(ffn1_norm, 0), gate1, up1, SLOT_FFN1[0])
    red[0] = reduce_group(0, ["in", "out", "qkv"])

    def upd(name, w, m, v, kind, places, row_halves):
        shp = w.shape
        w3 = w.reshape((-1,) + shp[-2:])
        outs = adamw_shard(name, w3, m.reshape(w3.shape), v.reshape(w3.shape),
                           [red[grp][kind] for grp, _ in places], core, [s for _, s in places], row_halves)
        return [o.reshape(shp) for o in outs]

    ffn1_places = [ffn_place[s][:2] for s in SLOT_FFN1]
    ffn2_places = [ffn_place[s][:2] for s in SLOT_FFN2]
    r_ffn1_in = upd("adamw_ffn1_in", ffn1_w_in, m_ffn1_w_in, v_ffn1_w_in, "in", ffn1_places, True)
    r_ffn2_in = upd("adamw_ffn2_in", ffn2_w_in, m_ffn2_w_in, v_ffn2_w_in, "in", ffn2_places, True)
    r_ffn1_out = upd("adamw_ffn1_out", ffn1_w_out, m_ffn1_w_out, v_ffn1_w_out, "out", ffn1_places, False)
    r_ffn2_out = upd("adamw_ffn2_out", ffn2_w_out, m_ffn2_w_out, v_ffn2_w_out, "out", ffn2_places, False)
    r_qkv = upd("adamw_qkv", sb_w_qkv, m_sb_w_qkv, v_sb_w_qkv, "qkv", [(0, 0)], True)
    r_sb_o = upd("adamw_sb_o", sb_w_o, m_sb_w_o, v_sb_w_o, "sq", [sq_place[SQ_SB_O][:2]], False)
    r_swa_q = upd("adamw_swa_q", swa_w_q, m_swa_w_q, v_swa_w_q, "sq", [sq_place[SQ_SWA_Q][:2]], False)
    r_swa_o = upd("adamw_swa_o", swa_w_o, m_swa_w_o, v_swa_w_o, "sq", [sq_place[SQ_SWA_O][:2]], False)
    r_kv = upd("adamw_kv", kv_w, m_kv_w, v_kv_w, "kv", [(1, 0)], False)

    loss_row = jnp.pad(loss_p, ((0, 0), (0, D_MODEL - LANES)))
    d_sink_row = d_sinks[0, :SWA_Q_HEADS]
    part = _pack_small(jnp.concatenate([d_ffn1_0, d_ffn1_1], axis=0), jnp.concatenate([d_mix_0, d_mix_1], axis=0),
                       jnp.concatenate([d_ffn2_0, d_ffn2_1], axis=0), d_kvn, d_final, d_sink_row, loss_row)
    zrow = jnp.zeros((1, D_MODEL), F32)
    small = small_allreduce_adamw(
        part,
        _pack_small(ffn1_norm, mix_norm, ffn2_norm, kv_norm, final_norm, swa_sinks, zrow),
        _pack_small(m_ffn1_norm, m_mix_norm, m_ffn2_norm, m_kv_norm, m_final_norm, m_swa_sinks, zrow),
        _pack_small(v_ffn1_norm, v_mix_norm, v_ffn2_norm, v_kv_norm, v_final_norm, v_swa_sinks, zrow))

    def unpack(p):
        return dict(ffn1_norm=p[0:2], mix_norm=p[2:4], ffn2_norm=p[4:6], kv_norm=p[6], final_norm=p[7],
                    swa_sinks=p[8:9, :SWA_Q_HEADS])

    big = dict(ffn1_w_in=r_ffn1_in, ffn1_w_out=r_ffn1_out, ffn2_w_in=r_ffn2_in, ffn2_w_out=r_ffn2_out,
               sb_w_qkv=r_qkv, sb_w_o=r_sb_o, kv_w=r_kv, swa_w_q=r_swa_q, swa_w_o=r_swa_o)
    order = ["ffn1_norm", "ffn1_w_in", "ffn1_w_out", "mix_norm", "ffn2_norm", "ffn2_w_in", "ffn2_w_out",
             "sb_w_qkv", "sb_w_o", "kv_norm", "kv_w", "swa_w_q", "swa_sinks", "swa_w_o", "final_norm"]
    outs = []
    for kind in range(4):
        sm = unpack(small[kind])
        for nm in order:
            outs.append(big[nm][kind] if nm in big else sm[nm])
    loss = small[0][9, 0]
    return (loss, dx.reshape(x.shape), *outs)
```

```python
import functools

import jax
import jax.numpy as jnp
from jax import lax
from jax.experimental import pallas as pl
from jax.experimental.pallas import tpu as pltpu

F32 = jnp.float32
BF16 = jnp.bfloat16
MESH = pl.DeviceIdType.MESH

D_MODEL = 1024
D_FF = 2816
HEAD_DIM = 64
SB_HEADS = 16
SWA_Q_HEADS = 16
SWA_KV_HEADS = 4
WINDOW = 128
ROPE_THETA = 10000.0
RMS_EPS = 1e-6
FFN_RES_SCALE = 0.5
ATTN_SCALE = HEAD_DIM ** -0.5

ADAM_LR = 0.001
ADAM_B1 = 0.9
ADAM_B2 = 0.999
ADAM_EPS = 1e-08
ADAM_WD = 0.01
ADAM_STEP = 10

N_CHIPS = 4
N_DEV = 8
LANES = 128
FF_CHUNK = D_FF // 2
FF_ROWS = D_FF // N_CHIPS
SQ_ROWS = D_MODEL // N_CHIPS
QKV_COLS = 3 * D_MODEL // N_CHIPS
VMEM_LIMIT = 56 * 1024 * 1024
NEG_BIG = -1e30

SLOT_FFN1 = (0, 1)
SLOT_FFN2 = (2, 3)
SQ_SB_O, SQ_SWA_Q, SQ_SWA_O = 0, 1, 2


def _cparams():
    return pltpu.CompilerParams(vmem_limit_bytes=VMEM_LIMIT)


def _dot(a, b):
    return jnp.dot(a, b, preferred_element_type=F32)


def _dot_nt(a, b):
    return lax.dot_general(a, b, (((1,), (1,)), ((), ())), preferred_element_type=F32)


def _dot_tn(a, b):
    return lax.dot_general(a, b, (((0,), (0,)), ((), ())), preferred_element_type=F32)


def _rstd(h):
    return lax.rsqrt(jnp.mean(h * h, axis=-1, keepdims=True) + RMS_EPS)


def _swap32(x):
    n = x.shape[-1]
    lane = lax.broadcasted_iota(jnp.int32, x.shape, x.ndim - 1)
    first = (lane % HEAD_DIM) < (HEAD_DIM // 2)
    return jnp.where(first, pltpu.roll(x, n - HEAD_DIM // 2, x.ndim - 1), pltpu.roll(x, HEAD_DIM // 2, x.ndim - 1))


def _tile_lanes(t, n):
    return t if n == LANES else jnp.tile(t, (1, n // LANES))


FFN_ROWS = 256


def _ffn_w_in_spec(slot):
    return pl.BlockSpec((N_CHIPS, None, D_MODEL, FF_CHUNK), lambda i: (0, slot, 0, 0), pipeline_mode=pl.Buffered(1))


def _ffn_w_out_spec(slot):
    return pl.BlockSpec((N_CHIPS, None, FF_ROWS, D_MODEL), lambda i: (0, slot, 0, 0), pipeline_mode=pl.Buffered(1))


def ffn_fwd(tag, h, g, w_in, w_out, slot, bg_shards=(), bg_lands=()):
    T = h.shape[0]
    tm = FFN_ROWS
    nch = D_FF // FF_CHUNK
    nbg = len(bg_shards)
    nt = T // tm

    def body(h_ref, g_ref, wi_ref, wo_ref, *rest):
        out_ref, gate_ref, up_ref = rest[2 * nbg:2 * nbg + 3]
        if nbg:
            step = pl.program_id(0)
            gather = GatherOps([s.shape[1] for s in bg_shards], rest[:nbg], rest[2 * nbg + 3:3 * nbg + 3],
                               *rest[3 * nbg + 3:])
            pl.when(step == 0)(gather.start_ici)
            pl.when(step == nt // 2)(gather.forward_arrivals)

            @pl.when(step == nt - 1)
            def _():
                gather.wait_forwards()
                gather.wait_ici_sends()

        hh = h_ref[...]
        xn = (hh * _rstd(hh) * g_ref[...]).astype(BF16)
        acc = None
        for j in range(nch):
            cols = slice(j * FF_CHUNK, (j + 1) * FF_CHUNK)
            gate = _dot(xn, wi_ref[j])
            up = _dot(xn, wi_ref[nch + j])
            gate_ref[:, cols] = gate.astype(BF16)
            up_ref[:, cols] = up.astype(BF16)
            a = (gate * jax.nn.sigmoid(gate) * up).astype(BF16)
            part = _dot(a, wo_ref[2 * j:2 * j + 2].reshape(FF_CHUNK, D_MODEL))
            acc = part if acc is None else acc + part
        out_ref[...] = hh + FFN_RES_SCALE * acc

    row = pl.BlockSpec((tm, D_MODEL), lambda i: (i, 0))
    ff = pl.BlockSpec((tm, D_FF), lambda i: (i, 0))
    res = pl.pallas_call(
        body,
        name=f"ffn_fwd_{tag}",
        grid=(nt,),
        in_specs=[row, pl.BlockSpec((1, D_MODEL), lambda i: (0, 0)), _ffn_w_in_spec(slot), _ffn_w_out_spec(slot)]
        + [ANY] * (2 * nbg),
        out_specs=[row, ff, ff] + [ANY] * nbg,
        out_shape=[
            jax.ShapeDtypeStruct((T, D_MODEL), F32),
            jax.ShapeDtypeStruct((T, D_FF), BF16),
            jax.ShapeDtypeStruct((T, D_FF), BF16),
        ] + [jax.ShapeDtypeStruct(l.shape, l.dtype) for l in bg_lands],
        input_output_aliases={4 + nbg + t: 3 + t for t in range(nbg)},
        scratch_shapes=[pltpu.SemaphoreType.DMA((nbg, N_PEER_CHIPS))] * (4 if nbg else 0),
        compiler_params=_cparams(),
    )(h, g, w_in, w_out, *bg_shards, *bg_lands)
    return (res[0], res[1], res[2], list(res[3:])) if nbg else tuple(res)


def ffn_bwd(tag, dh, h, g, gate, up, w_in, w_out, slot):
    T = dh.shape[0]
    tm = FFN_ROWS
    nch = D_FF // FF_CHUNK

    def body(dh_ref, h_ref, g_ref, gate_ref, up_ref, wi_ref, wo_ref,
             dhin_ref, xn_ref, dg_ref, du_ref, a_ref, dhb_ref, dnorm_ref):
        @pl.when(pl.program_id(0) == 0)
        def _():
            dnorm_ref[...] = jnp.zeros_like(dnorm_ref)

        dhh = dh_ref[...]
        dhb = (FFN_RES_SCALE * dhh).astype(BF16)
        dhb_ref[...] = dhb
        dxn = None
        for j in range(nch):
            cols = slice(j * FF_CHUNK, (j + 1) * FF_CHUNK)
            da = _dot_nt(dhb, wo_ref[2 * j:2 * j + 2].reshape(FF_CHUNK, D_MODEL))
            gt = gate_ref[:, cols].astype(F32)
            u = up_ref[:, cols].astype(F32)
            s = jax.nn.sigmoid(gt)
            silu = gt * s
            a_ref[:, cols] = (silu * u).astype(BF16)
            dgate = (da * u * (s * (1.0 + gt * (1.0 - s)))).astype(BF16)
            dup = (da * silu).astype(BF16)
            dg_ref[:, cols] = dgate
            du_ref[:, cols] = dup
            part = _dot_nt(dgate, wi_ref[j]) + _dot_nt(dup, wi_ref[nch + j])
            dxn = part if dxn is None else dxn + part
        hh = h_ref[...]
        gg = g_ref[...]
        r = _rstd(hh)
        hr = hh * r
        xn_ref[...] = (hr * gg).astype(BF16)
        dnorm_ref[...] += jnp.sum(dxn * hr, axis=0, keepdims=True)
        gd = gg * dxn
        dhin_ref[...] = dhh + r * (gd - hr * jnp.mean(gd * hr, axis=-1, keepdims=True))

    row = pl.BlockSpec((tm, D_MODEL), lambda i: (i, 0))
    ff = pl.BlockSpec((tm, D_FF), lambda i: (i, 0))
    vec = pl.BlockSpec((1, D_MODEL), lambda i: (0, 0))
    return pl.pallas_call(
        body,
        name=f"ffn_bwd_{tag}",
        grid=(T // tm,),
        in_specs=[row, row, vec, ff, ff, _ffn_w_in_spec(slot), _ffn_w_out_spec(slot)],
        out_specs=[row, row, ff, ff, ff, row, vec],
        out_shape=[
            jax.ShapeDtypeStruct((T, D_MODEL), F32),
            jax.ShapeDtypeStruct((T, D_MODEL), BF16),
            jax.ShapeDtypeStruct((T, D_FF), BF16),
            jax.ShapeDtypeStruct((T, D_FF), BF16),
            jax.ShapeDtypeStruct((T, D_FF), BF16),
            jax.ShapeDtypeStruct((T, D_MODEL), BF16),
            jax.ShapeDtypeStruct((1, D_MODEL), F32),
        ],
        compiler_params=_cparams(),
    )(dh, h, g, gate, up, w_in, w_out)


def rms_linear(name, h, g, w, w_spec, w_prep, n_out, tn, *, rope=None, scale=None):
    T = h.shape[0]
    tm = 512
    extra, extra_specs = [], []
    if rope is not None:
        extra += list(rope)
        extra_specs += [pl.BlockSpec((tm, LANES), lambda i, j: (i, 0))] * 2
    if scale is not None:
        extra.append(scale)
        extra_specs.append(pl.BlockSpec((1, tn), lambda i, j: (0, j)))

    def body(h_ref, g_ref, w_ref, *rest):
        rest = list(rest)
        cos_ref = sin_ref = sc_ref = None
        if rope is not None:
            cos_ref, sin_ref = rest[0], rest[1]
            rest = rest[2:]
        if scale is not None:
            sc_ref = rest[0]
            rest = rest[1:]
        out_ref, xn_s = rest

        @pl.when(pl.program_id(1) == 0)
        def _():
            hh = h_ref[...]
            xn_s[...] = (hh * _rstd(hh) * g_ref[...]).astype(BF16)

        y = _dot(xn_s[...], w_prep(w_ref[...]))
        if rope is not None:
            y = y * _tile_lanes(cos_ref[...], tn) + _swap32(y) * _tile_lanes(sin_ref[...], tn)
        if scale is not None:
            y = y * sc_ref[...]
        out_ref[...] = y.astype(BF16)

    return pl.pallas_call(
        body,
        name=name,
        grid=(T // tm, n_out // tn),
        in_specs=[
            pl.BlockSpec((tm, D_MODEL), lambda i, j: (i, 0)),
            pl.BlockSpec((1, D_MODEL), lambda i, j: (0, 0)),
            w_spec,
        ] + extra_specs,
        out_specs=pl.BlockSpec((tm, tn), lambda i, j: (i, j)),
        out_shape=jax.ShapeDtypeStruct((T, n_out), BF16),
        scratch_shapes=[pltpu.VMEM((tm, D_MODEL), BF16)],
        compiler_params=_cparams(),
    )(h, g, w, *extra)


QKV_ROWS = 512


def _qkv_w_spec():
    return pl.BlockSpec((N_CHIPS, D_MODEL, QKV_COLS), lambda i: (0, 0, 0), pipeline_mode=pl.Buffered(1))


def qkv_fwd(h, g, w_qkv, scale):
    T = h.shape[0]
    tm = QKV_ROWS

    def body(h_ref, g_ref, w_ref, sc_ref, out_ref):
        hh = h_ref[...]
        xn = (hh * _rstd(hh) * g_ref[...]).astype(BF16)
        for s in range(N_CHIPS):
            cols = slice(s * QKV_COLS, (s + 1) * QKV_COLS)
            out_ref[:, cols] = (_dot(xn, w_ref[s]) * sc_ref[:, cols]).astype(BF16)

    return pl.pallas_call(
        body,
        name="sb_qkv",
        grid=(T // tm,),
        in_specs=[
            pl.BlockSpec((tm, D_MODEL), lambda i: (i, 0)),
            pl.BlockSpec((1, D_MODEL), lambda i: (0, 0)),
            _qkv_w_spec(),
            pl.BlockSpec((1, 3 * D_MODEL), lambda i: (0, 0)),
        ],
        out_specs=pl.BlockSpec((tm, 3 * D_MODEL), lambda i: (i, 0)),
        out_shape=jax.ShapeDtypeStruct((T, 3 * D_MODEL), BF16),
        compiler_params=_cparams(),
    )(h, g, w_qkv, scale)


def qkv_bwd(dy, w_qkv, h, g, dres):
    T = h.shape[0]
    tm = QKV_ROWS

    def body(dy_ref, w_ref, h_ref, g_ref, dres_ref, dh_ref, xn_ref, dg_ref):
        @pl.when(pl.program_id(0) == 0)
        def _():
            dg_ref[...] = jnp.zeros_like(dg_ref)

        dxn = None
        for s in range(N_CHIPS):
            part = _dot_nt(dy_ref[:, s * QKV_COLS:(s + 1) * QKV_COLS], w_ref[s])
            dxn = part if dxn is None else dxn + part
        hh = h_ref[...]
        gg = g_ref[...]
        r = _rstd(hh)
        hr = hh * r
        xn_ref[...] = (hr * gg).astype(BF16)
        dg_ref[...] += jnp.sum(dxn * hr, axis=0, keepdims=True)
        gd = gg * dxn
        dh_ref[...] = dres_ref[...] + r * (gd - hr * jnp.mean(gd * hr, axis=-1, keepdims=True))

    row = pl.BlockSpec((tm, D_MODEL), lambda i: (i, 0))
    vec = pl.BlockSpec((1, D_MODEL), lambda i: (0, 0))
    return pl.pallas_call(
        body,
        name="sb_qkv_bwd",
        grid=(T // tm,),
        in_specs=[pl.BlockSpec((tm, 3 * D_MODEL), lambda i: (i, 0)), _qkv_w_spec(), row, vec, row],
        out_specs=[row, row, vec],
        out_shape=[
            jax.ShapeDtypeStruct((T, D_MODEL), F32),
            jax.ShapeDtypeStruct((T, D_MODEL), BF16),
            jax.ShapeDtypeStruct((1, D_MODEL), F32),
        ],
        compiler_params=_cparams(),
    )(dy, w_qkv, h, g, dres)


def linear_res(name, a, w_sq, t, res, bg_lands=()):
    T = a.shape[0]
    tm = 512
    nbg = len(bg_lands)
    nt = T // tm

    def body(a_ref, w_ref, res_ref, *rest):
        out_ref = rest[nbg]
        if nbg:
            gather = GatherOps([l.shape[2] for l in bg_lands], None, rest[nbg + 1:2 * nbg + 1], None, None,
                               *rest[2 * nbg + 1:])

            @pl.when(pl.program_id(0) == 0)
            def _():
                gather.start_forwards()

        out_ref[...] = res_ref[...] + _dot(a_ref[...], w_ref[...].reshape(D_MODEL, D_MODEL))
        if nbg:
            @pl.when(pl.program_id(0) == nt - 1)
            def _():
                gather.wait_forwards()

    row = pl.BlockSpec((tm, D_MODEL), lambda i: (i, 0))
    res_ = pl.pallas_call(
        body,
        name=name,
        grid=(nt,),
        in_specs=[row, pl.BlockSpec((N_CHIPS, None, SQ_ROWS, D_MODEL), lambda i: (0, t, 0, 0)), row] + [ANY] * nbg,
        out_specs=[row] + [ANY] * nbg,
        out_shape=[jax.ShapeDtypeStruct((T, D_MODEL), F32)] + [jax.ShapeDtypeStruct(l.shape, l.dtype) for l in bg_lands],
        input_output_aliases={3 + k: 1 + k for k in range(nbg)},
        scratch_shapes=[pltpu.SemaphoreType.DMA((nbg, N_PEER_CHIPS))] * (2 if nbg else 0),
        compiler_params=_cparams(),
    )(a, w_sq, res, *bg_lands)
    return (res_[0], list(res_[1:])) if nbg else res_[0]


def linear_bwd_plain(name, dy, w_sq, t):
    T = dy.shape[0]
    tm = 512

    def body(dy_ref, w_ref, da_ref, dyb_ref):
        dyb = dy_ref[...].astype(BF16)
        dyb_ref[...] = dyb
        da_ref[...] = _dot_nt(dyb, w_ref[...].reshape(D_MODEL, D_MODEL)).astype(BF16)

    row = pl.BlockSpec((tm, D_MODEL), lambda i: (i, 0))
    return pl.pallas_call(
        body,
        name=name,
        grid=(T // tm,),
        in_specs=[row, pl.BlockSpec((N_CHIPS, None, SQ_ROWS, D_MODEL), lambda i: (0, t, 0, 0))],
        out_specs=[row, row],
        out_shape=[jax.ShapeDtypeStruct((T, D_MODEL), BF16), jax.ShapeDtypeStruct((T, D_MODEL), BF16)],
        compiler_params=_cparams(),
    )(dy, w_sq)


def linear_bwd_rms(name, pairs, h, g, dres, nch, tn, tm=256):
    T = h.shape[0]
    npair = len(pairs)

    def body(*refs):
        dy_refs = refs[:npair]
        w_refs = refs[npair:2 * npair]
        h_ref, g_ref, dres_ref, dh_ref, xn_ref, dg_ref, acc_s = refs[2 * npair:]
        i = pl.program_id(0)
        j = pl.program_id(1)

        @pl.when(j == 0)
        def _():
            acc_s[...] = jnp.zeros_like(acc_s)

        @pl.when((i == 0) & (j == 0))
        def _():
            dg_ref[...] = jnp.zeros_like(dg_ref)

        part = None
        for p in range(npair):
            d = _dot_nt(dy_refs[p][...], pairs[p][3](w_refs[p][...]))
            part = d if part is None else part + d
        acc_s[...] += part

        @pl.when(j == nch - 1)
        def _():
            dxn = acc_s[...]
            hh = h_ref[...]
            gg = g_ref[...]
            r = _rstd(hh)
            hr = hh * r
            xn_ref[...] = (hr * gg).astype(BF16)
            dg_ref[...] += jnp.sum(dxn * hr, axis=0, keepdims=True)
            gd = gg * dxn
            dh_ref[...] = dres_ref[...] + r * (gd - hr * jnp.mean(gd * hr, axis=-1, keepdims=True))

    row = pl.BlockSpec((tm, D_MODEL), lambda i, j: (i, 0))
    vec = pl.BlockSpec((1, D_MODEL), lambda i, j: (0, 0))
    return pl.pallas_call(
        body,
        name=name,
        grid=(T // tm, nch),
        in_specs=[pl.BlockSpec((tm, tn), lambda i, j: (i, j))] * npair + [p[2] for p in pairs] + [row, vec, row],
        out_specs=[row, row, vec],
        out_shape=[
            jax.ShapeDtypeStruct((T, D_MODEL), F32),
            jax.ShapeDtypeStruct((T, D_MODEL), BF16),
            jax.ShapeDtypeStruct((1, D_MODEL), F32),
        ],
        scratch_shapes=[pltpu.VMEM((tm, D_MODEL), F32)],
        compiler_params=_cparams(),
    )(*[p[0] for p in pairs], *[p[1] for p in pairs], h, g, dres)


def loss_bwd(h, g, tgt):
    T = h.shape[0]
    tm = 512

    def body(h_ref, g_ref, t_ref, dh_ref, loss_ref, dg_ref):
        @pl.when(pl.program_id(0) == 0)
        def _():
            loss_ref[...] = jnp.zeros_like(loss_ref)
            dg_ref[...] = jnp.zeros_like(dg_ref)

        hh = h_ref[...]
        gg = g_ref[...]
        r = _rstd(hh)
        hr = hh * r
        err = hr * gg - t_ref[...]
        loss_ref[...] += 0.5 * jnp.sum(jnp.mean(err * err, axis=-1, keepdims=True), axis=0, keepdims=True)
        dy = err * (1.0 / D_MODEL)
        dg_ref[...] += jnp.sum(dy * hr, axis=0, keepdims=True)
        gd = gg * dy
        dh_ref[...] = r * (gd - hr * jnp.mean(gd * hr, axis=-1, keepdims=True))

    row = pl.BlockSpec((tm, D_MODEL), lambda i: (i, 0))
    vec = pl.BlockSpec((1, D_MODEL), lambda i: (0, 0))
    return pl.pallas_call(
        body,
        name="loss_bwd",
        grid=(T // tm,),
        in_specs=[row, vec, row],
        out_specs=[row, pl.BlockSpec((1, LANES), lambda i: (0, 0)), vec],
        out_shape=[
            jax.ShapeDtypeStruct((T, D_MODEL), F32),
            jax.ShapeDtypeStruct((1, LANES), F32),
            jax.ShapeDtypeStruct((1, D_MODEL), F32),
        ],
        compiler_params=_cparams(),
    )(h, g, tgt)


DW_TOKENS = 4096


def mm_tn(name, a, b, tk, tn, out_block, out_index, out_shape, prev=None, tt=DW_TOKENS):
    T = a.shape[0]
    ns, r = out_block[1], out_block[3]
    tt = min(tt, T)
    nt = T // tt

    def body(*refs):
        a_ref, b_ref = refs[:2]
        out_ref, copy_ref = refs[-2:]
        t = pl.program_id(2)
        res = _dot_tn(a_ref[...], b_ref[...])

        @pl.when(t == 0)
        def _():
            for u in range(ns):
                out_ref[u] = res[u * r:(u + 1) * r]

        @pl.when(t > 0)
        def _():
            for u in range(ns):
                out_ref[u] += res[u * r:(u + 1) * r]

        @pl.when(t == nt - 1)
        def _():
            copy_ref[...] = out_ref[...].astype(BF16)

    in_specs = [
        pl.BlockSpec((tt, tk), lambda k, n, t: (t, k)),
        pl.BlockSpec((tt, tn), lambda k, n, t: (t, n)),
    ]
    args = [a, b]
    aliases = {}
    if prev is not None:
        in_specs += [pl.BlockSpec(memory_space=pl.ANY)] * 2
        args += list(prev)
        aliases = {2: 0, 3: 1}
    out_spec = pl.BlockSpec(out_block, lambda k, n, t: out_index(k, n))
    return tuple(pl.pallas_call(
        body,
        name=name,
        grid=(a.shape[1] // tk, b.shape[1] // tn, nt),
        in_specs=in_specs,
        out_specs=[out_spec, out_spec],
        out_shape=[jax.ShapeDtypeStruct(out_shape, F32), jax.ShapeDtypeStruct(out_shape, BF16)],
        input_output_aliases=aliases,
        compiler_params=_cparams(),
    )(*args))


SB_BLOCK = 256
SB_QROWS = 256
SB_QROWS_BWD = 256
SB_UNDERFLOW_BITS = 140.0
SB_CHUNK = 128


LOG2E = 1.4426950408889634


def _softplus2(z2):
    sign = jnp.uint32(0x80000000)
    neg_abs = lax.bitcast_convert_type(lax.bitcast_convert_type(z2, jnp.uint32) | sign, F32)
    return jnp.log2(1.0 + jnp.exp2(neg_abs)) + jnp.maximum(z2, 0.0)


def _twice(x):
    return jnp.concatenate([x, x], axis=1)


def sb_fwd(qkv, bg_shards=(), bg_lands=()):
    T = qkv.shape[0]
    tq, tk = SB_QROWS, SB_BLOCK
    ratio = tq // tk
    npair = SB_HEADS // 2
    nbg = len(bg_shards)
    nq = T // tq

    def body(q_ref, k_ref, v_ref, *rest):
        bg_in = rest[:nbg]
        o_ref, tot_ref, first_ref = rest[2 * nbg:2 * nbg + 3]
        bg_out = rest[2 * nbg + 3:3 * nbg + 3]
        acc_s, c_s, z_s, w_s, kmax_s = rest[3 * nbg + 3:3 * nbg + 8]
        p = pl.program_id(0)
        i = pl.program_id(1)
        if nbg:
            gather = GatherOps([s.shape[1] for s in bg_shards], bg_in, bg_out, *rest[3 * nbg + 8:])

            @pl.when((p == 0) & (i == 0))
            def _():
                gather.start_ici()

        @pl.when(i == 0)
        def _():
            kmax_s[...] = jnp.max(jnp.abs(k_ref[...]), axis=0, keepdims=True).astype(F32)

        q = q_ref[...]
        lane = lax.broadcasted_iota(jnp.int32, (tq, LANES), 1)
        first = lane < HEAD_DIM
        zero = jnp.zeros_like(q)
        q_heads = (jnp.where(first, q, zero), jnp.where(first, zero, q))
        row = lax.broadcasted_iota(jnp.int32, (tq, tk), 0)
        col = lax.broadcasted_iota(jnp.int32, (tq, tk), 1)
        visible = [col + r * tk < row for r in range(ratio)]
        krow = lax.broadcasted_iota(jnp.int32, (tk, tk), 0)
        kcol = lax.broadcasted_iota(jnp.int32, (tk, tk), 1)
        from_s = (krow >= kcol).astype(BF16)
        acc_s[...] = jnp.zeros_like(acc_s)
        c_s[...] = jnp.zeros_like(c_s)

        def rows(j):
            return pl.ds(pl.multiple_of(j * tk, tk), tk)

        def logits(j):
            kb = k_ref[rows(j), :]
            for hd in range(2):
                z_s[hd] = _dot_nt(q_heads[hd], kb) * LOG2E

        def flush(j):
            vb = v_ref[rows(j), :]
            for hd in range(2):
                acc_s[hd] += _dot(w_s[hd], vb)

        def block(j, mask=None, flush_block=None):
            if flush_block is not None:
                flush(flush_block)
            chunks = [(hd, slice(r0, r0 + SB_CHUNK)) for hd in range(2) for r0 in range(0, tq, SB_CHUNK)]
            k_next = k_ref[rows(jnp.maximum(j - 1, 0)), :]
            es, sums = [], []
            for hd, rs in chunks:
                z2 = z_s[hd, rs, :]
                z_s[hd, rs, :] = _dot_nt(q_heads[hd][rs, :], k_next) * LOG2E
                if mask is not None:
                    z2 = jnp.where(mask if mask.ndim == 0 else mask[rs, :], z2, NEG_BIG)
                sp = _softplus2(z2)
                c = c_s[hd, rs, :]
                es.append(z2 + _twice(c))
                c_s[hd, rs, :] = c - jnp.sum(sp, axis=1, keepdims=True)
                sums.append(_dot(sp.astype(BF16), from_s))
            for (hd, rs), e, s in zip(chunks, es, sums):
                w_s[hd, rs, :] = jnp.exp2(e - s).astype(BF16)

        z_bound = [LOG2E * jnp.sum(jnp.abs(q_heads[hd].astype(F32)) * kmax_s[...], axis=1, keepdims=True)
                   for hd in range(2)]

        def more_keys_matter():
            top = jnp.maximum(c_s[0] + z_bound[0], c_s[1] + z_bound[1])
            return (jnp.max(top) >= -SB_UNDERFLOW_BITS).astype(jnp.int32)

        assert ratio == 1
        logits(i)
        block(i, visible[0])

        def trip(carry):
            trips, _ = carry
            j = i - 1 - trips
            block(j, flush_block=j + 1)
            return trips + 1, more_keys_matter()

        trips, _ = lax.while_loop(lambda carry: jnp.logical_and(carry[0] < i, carry[1] > 0), trip,
                                  (jnp.int32(0), jnp.int32(1)))
        first_walked = i - trips
        flush(first_walked)
        first_ref[p, i] = first_walked.astype(F32)
        o_ref[...] = jnp.where(first, acc_s[0], acc_s[1]).astype(BF16)
        tot_ref[...] = jnp.where(first, c_s[0], c_s[1])
        if nbg:
            @pl.when((p == npair - 1) & (i == nq - 1))
            def _():
                gather.wait_ici()

    res = pl.pallas_call(
        body,
        name="sb_fwd",
        grid=(npair, nq),
        in_specs=[
            pl.BlockSpec((tq, LANES), lambda p, i: (i, p)),
            pl.BlockSpec((T, LANES), lambda p, i: (0, npair + p)),
            pl.BlockSpec((T, LANES), lambda p, i: (0, 2 * npair + p)),
        ] + [ANY] * (2 * nbg),
        out_specs=[pl.BlockSpec((tq, LANES), lambda p, i: (i, p))] * 2 + [pl.BlockSpec(memory_space=pltpu.SMEM)]
        + [ANY] * nbg,
        out_shape=[jax.ShapeDtypeStruct((T, D_MODEL), BF16), jax.ShapeDtypeStruct((T, D_MODEL), F32),
                   jax.ShapeDtypeStruct((npair, nq), F32)]
        + [jax.ShapeDtypeStruct(l.shape, l.dtype) for l in bg_lands],
        input_output_aliases={3 + nbg + t: 3 + t for t in range(nbg)},
        scratch_shapes=[
            pltpu.VMEM((2, tq, LANES), F32), pltpu.VMEM((2, tq, LANES), F32),
            pltpu.VMEM((2, tq, tk), F32), pltpu.VMEM((2, tq, tk), BF16),
            pltpu.VMEM((1, LANES), F32),
        ] + [pltpu.SemaphoreType.DMA((nbg, N_PEER_CHIPS))] * (2 if nbg else 0),
        compiler_params=_cparams(),
    )(qkv, qkv, qkv, *bg_shards, *bg_lands)
    return res[0], res[1], res[2], list(res[3:])


def sb_bwd(qkv, do, tot, first_block, bg_parts=()):
    T = qkv.shape[0]
    tq, tk = SB_QROWS_BWD, SB_BLOCK
    ratio = tq // tk
    npair = SB_HEADS // 2
    nq = T // tq
    nk = T // tk
    nbg = len(bg_parts)
    assert SB_QROWS == SB_QROWS_BWD

    def body(first_ref, q_ref, k_ref, v_ref, do_ref, tot_ref, *rest):
        bg_in = rest[:nbg]
        dq_ref, dk_ref, dv_ref = rest[nbg:nbg + 3]
        bg_out = rest[nbg + 3:2 * nbg + 3]
        dkt_s, dvt_s, dq_s, rest_s, cg_s, z_s, da_s, dz_s, a_s = rest[2 * nbg + 3:2 * nbg + 12]
        i = pl.program_id(1)
        start = jnp.clip(first_ref[pl.program_id(0), i].astype(jnp.int32), 0, i)
        if nbg:
            @pl.when((pl.program_id(0) == 0) & (i == 0))
            def _():
                for cp in chip_partial_copies(bg_in, bg_out, *rest[2 * nbg + 12:]):
                    cp.start()

        @pl.when(i == 0)
        def _():
            dkt_s[...] = jnp.zeros_like(dkt_s)
            dvt_s[...] = jnp.zeros_like(dvt_s)

        q = q_ref[...]
        do_ = do_ref[...]
        tot_ = tot_ref[...]
        q_t = q.astype(F32).T.astype(BF16)
        do_t = do_.astype(F32).T.astype(BF16)
        lane = lax.broadcasted_iota(jnp.int32, (tq, LANES), 1)
        first = lane < HEAD_DIM
        zero = jnp.zeros_like(q)
        q_heads = (jnp.where(first, q, zero), jnp.where(first, zero, q))
        do_heads = (jnp.where(first, do_, zero), jnp.where(first, zero, do_))
        row = lax.broadcasted_iota(jnp.int32, (tq, tk), 0)
        col = lax.broadcasted_iota(jnp.int32, (tq, tk), 1)
        visible = [col + r * tk < row for r in range(ratio)]
        krow = lax.broadcasted_iota(jnp.int32, (tk, tk), 0)
        kcol = lax.broadcasted_iota(jnp.int32, (tk, tk), 1)
        before = (krow < kcol).astype(BF16)
        from_s = (krow >= kcol).astype(BF16)
        last = ratio * i + ratio - 1
        rest_s[0] = jnp.broadcast_to(tot_[:, 0:1], (tq, LANES))
        rest_s[1] = jnp.broadcast_to(tot_[:, HEAD_DIM:HEAD_DIM + 1], (tq, LANES))
        cg_s[...] = jnp.zeros_like(cg_s)
        dq_s[...] = jnp.zeros_like(dq_s)
        dz_s[...] = jnp.zeros_like(dz_s)
        a_s[...] = jnp.zeros_like(a_s)

        def rows(j):
            return pl.ds(pl.multiple_of(j * tk, tk), tk)

        def logits(j):
            kb = k_ref[rows(j), :]
            vb = v_ref[rows(j), :]
            for hd in range(2):
                z_s[hd] = _dot_nt(q_heads[hd], kb) * LOG2E
                da_s[hd] = _dot_nt(do_heads[hd], vb)

        def flush(j):
            kb = k_ref[rows(j), :]
            for hd in range(2):
                dims = slice(hd * HEAD_DIM, (hd + 1) * HEAD_DIM)
                dq_s[hd] += _dot(dz_s[hd], kb)
                dkt_s[j, dims, :] += _dot(q_t[dims, :], dz_s[hd])
                dvt_s[j, dims, :] += _dot(do_t[dims, :], a_s[hd])

        def block(j, mask=None):
            flush(jnp.maximum(j - 1, 0))
            chunks = [(hd, slice(r0, r0 + SB_CHUNK)) for hd in range(2) for r0 in range(0, tq, SB_CHUNK)]
            nxt = rows(jnp.minimum(j + 1, last))
            k_next = k_ref[nxt, :]
            v_next = v_ref[nxt, :]
            stage1 = []
            for hd, rs in chunks:
                z2 = z_s[hd, rs, :]
                z_s[hd, rs, :] = _dot_nt(q_heads[hd][rs, :], k_next) * LOG2E
                if mask is not None:
                    z2 = jnp.where(mask if mask.ndim == 0 else mask[rs, :], z2, NEG_BIG)
                sp = _softplus2(z2)
                rest = rest_s[hd, rs, :] + jnp.sum(sp, axis=1, keepdims=True)
                rest_s[hd, rs, :] = rest
                stage1.append((z2 + _twice(rest), z2 - sp, _dot(sp.astype(BF16), from_s)))
            stage2 = []
            for (hd, rs), (e, log2_beta, ahead) in zip(chunks, stage1):
                a = jnp.exp2(e - ahead)
                g = a * da_s[hd, rs, :]
                da_s[hd, rs, :] = _dot_nt(do_heads[hd][rs, :], v_next)
                cg = cg_s[hd, rs, :]
                a_s[hd, rs, :] = a.astype(BF16)
                cg_s[hd, rs, :] = cg + jnp.sum(g, axis=1, keepdims=True)
                stage2.append((g, g + _twice(cg), log2_beta, _dot(g.astype(BF16), before)))
            for (hd, rs), (g, g_from, log2_beta, g_before) in zip(chunks, stage2):
                dz_s[hd, rs, :] = (g - jnp.exp2(log2_beta) * (g_from + g_before)).astype(BF16)

        assert ratio == 1
        logits(start)

        @pl.loop(start, i)
        def _(j):
            block(j)

        block(i, visible[0])
        flush(last)
        dq_ref[...] = (jnp.where(first, dq_s[0], dq_s[1]) * ATTN_SCALE).astype(BF16)

        @pl.when(i == nq - 1)
        def _():
            @pl.loop(0, nk)
            def _(b):
                dk_ref[rows(b), :] = dkt_s[b].T.astype(BF16)
                dv_ref[rows(b), :] = dvt_s[b].T.astype(BF16)

        if nbg:
            @pl.when((pl.program_id(0) == npair - 1) & (i == nq - 1))
            def _():
                for cp in chip_partial_copies(bg_in, bg_out, *rest[2 * nbg + 12:]):
                    cp.wait()

    qblk = pl.BlockSpec((tq, LANES), lambda p, i: (i, p))
    full = pl.BlockSpec((T, LANES), lambda p, i: (0, p))
    res = pl.pallas_call(
        body,
        name="sb_bwd",
        grid=(npair, nq),
        in_specs=[
            pl.BlockSpec(memory_space=pltpu.SMEM),
            qblk,
            pl.BlockSpec((T, LANES), lambda p, i: (0, npair + p)),
            pl.BlockSpec((T, LANES), lambda p, i: (0, 2 * npair + p)),
            qblk, qblk,
        ] + [ANY] * nbg,
        out_specs=[qblk, full, full] + [ANY] * nbg,
        out_shape=[jax.ShapeDtypeStruct((T, D_MODEL), BF16)] * 3
        + [jax.ShapeDtypeStruct(b.shape, b.dtype) for b in bg_parts],
        scratch_shapes=[
            pltpu.VMEM((nk, LANES, tk), F32), pltpu.VMEM((nk, LANES, tk), F32),
            pltpu.VMEM((2, tq, LANES), F32), pltpu.VMEM((2, tq, LANES), F32), pltpu.VMEM((2, tq, LANES), F32),
            pltpu.VMEM((2, tq, tk), F32), pltpu.VMEM((2, tq, tk), F32),
            pltpu.VMEM((2, tq, tk), BF16), pltpu.VMEM((2, tq, tk), BF16),
        ] + [pltpu.SemaphoreType.DMA((nbg, N_PEER_CHIPS))] * (2 if nbg else 0),
        compiler_params=_cparams(),
    )(first_block, qkv, qkv, qkv, do, tot, *bg_parts)
    return res[0], res[1], res[2], list(res[3:])


def _swa_valid(n):
    qi = lax.broadcasted_iota(jnp.int32, (WINDOW, 2 * WINDOW), 0)
    ki = lax.broadcasted_iota(jnp.int32, (WINDOW, 2 * WINDOW), 1)
    diff = qi + WINDOW - ki
    return (diff >= 0) & (diff < WINDOW) & ((n > 0) | (ki >= WINDOW))


def _to_half(x, first, src, dst):
    keep = first if src == 0 else jnp.logical_not(first)
    x = jnp.where(keep, x, jnp.zeros_like(x))
    if src != dst:
        x = pltpu.roll(x.astype(F32), HEAD_DIM, 1).astype(BF16)
    return x


SWA_GROUP = SWA_Q_HEADS // SWA_KV_HEADS


def _swa_cols(h):
    return slice((h // 2) * LANES, (h // 2 + 1) * LANES)


def _swa_kv_pair(h):
    return (h // SWA_GROUP) // 2


def _swa_kv_half(h):
    return (h // SWA_GROUP) % 2


def _kv_band(prev_ref, cur_ref, pb):
    cols = slice(pb * LANES, (pb + 1) * LANES)
    return jnp.concatenate([prev_ref[:, cols], cur_ref[:, cols]], axis=0)


def _swa_specs(T):
    nb = T // WINDOW
    kv_w = SWA_KV_HEADS * HEAD_DIM
    qrow = pl.BlockSpec((WINDOW, D_MODEL), lambda n: (n, 0))
    cur = pl.BlockSpec((WINDOW, kv_w), lambda n: (n, 0))
    prev = pl.BlockSpec((WINDOW, kv_w), lambda n: (jnp.maximum(n - 1, 0), 0))
    smem = pl.BlockSpec(memory_space=pltpu.SMEM)
    return nb, qrow, cur, prev, smem


def swa_fwd(q, k, v, sinks):
    T = q.shape[0]
    nb, qrow, cur, prev, smem = _swa_specs(T)

    def body(sink_ref, q_ref, kc_ref, kp_ref, vc_ref, vp_ref, o_ref, lse_ref):
        n = pl.program_id(0)
        lane = lax.broadcasted_iota(jnp.int32, (WINDOW, LANES), 1)
        first = lane < HEAD_DIM
        valid = _swa_valid(n)
        k2 = [_kv_band(kp_ref, kc_ref, pb) for pb in range(SWA_KV_HEADS // 2)]
        v2 = [_kv_band(vp_ref, vc_ref, pb) for pb in range(SWA_KV_HEADS // 2)]
        logits = [jnp.where(valid, _dot_nt(_to_half(q_ref[:, _swa_cols(h)], first, h % 2, _swa_kv_half(h)),
                                            k2[_swa_kv_pair(h)]), NEG_BIG) for h in range(SWA_Q_HEADS)]
        probs = []
        lse_acc = jnp.zeros((WINDOW, LANES), F32)
        for h, s in enumerate(logits):
            sink = sink_ref[h]
            m = jnp.maximum(jnp.max(s, axis=1, keepdims=True), sink)
            p = jnp.exp(s - m)
            den = jnp.sum(p, axis=1, keepdims=True) + jnp.exp(sink - m)
            probs.append((p / den).astype(BF16))
            lse_acc = jnp.where(lane == h, m + jnp.log(den), lse_acc)
        outs = []
        for h, p in enumerate(probs):
            o = _dot(p, v2[_swa_kv_pair(h)])
            outs.append(pltpu.roll(o, HEAD_DIM, 1) if h % 2 != _swa_kv_half(h) else o)
        for pair in range(SWA_Q_HEADS // 2):
            o_ref[:, _swa_cols(2 * pair)] = jnp.where(first, outs[2 * pair], outs[2 * pair + 1]).astype(BF16)
        lse_ref[...] = lse_acc

    return pl.pallas_call(
        body,
        name="swa_fwd",
        grid=(nb,),
        in_specs=[smem, qrow, cur, prev, cur, prev],
        out_specs=[qrow, pl.BlockSpec((WINDOW, LANES), lambda n: (n, 0))],
        out_shape=[jax.ShapeDtypeStruct((T, D_MODEL), BF16), jax.ShapeDtypeStruct((T, LANES), F32)],
        compiler_params=_cparams(),
    )(sinks, q, k, k, v, v)


def swa_bwd(q, k, v, sinks, do, o, lse, cos, sin):
    T = q.shape[0]
    nb, qrow, cur, prev, smem = _swa_specs(T)
    kv_w = SWA_KV_HEADS * HEAD_DIM

    def body(sink_ref, q_ref, kc_ref, kp_ref, vc_ref, vp_ref, do_ref, o_ref, lse_ref, cos_ref, sin_ref,
             dq_ref, own_ref, prv_ref, dsink_ref):
        n = pl.program_id(0)

        @pl.when(n == 0)
        def _():
            dsink_ref[...] = jnp.zeros_like(dsink_ref)

        lane = lax.broadcasted_iota(jnp.int32, (WINDOW, LANES), 1)
        lane1 = lax.broadcasted_iota(jnp.int32, (1, LANES), 1)
        first = lane < HEAD_DIM
        valid = _swa_valid(n)
        cos_ = cos_ref[...]
        sin_ = sin_ref[...]
        k2 = [_kv_band(kp_ref, kc_ref, pb) for pb in range(SWA_KV_HEADS // 2)]
        v2 = [_kv_band(vp_ref, vc_ref, pb) for pb in range(SWA_KV_HEADS // 2)]
        q_t = q_ref[...].astype(F32).T.astype(BF16)
        do_t = do_ref[...].astype(F32).T.astype(BF16)
        stage1 = []
        for h in range(SWA_Q_HEADS):
            a, b, pb = h % 2, _swa_kv_half(h), _swa_kv_pair(h)
            qh = _to_half(q_ref[:, _swa_cols(h)], first, a, b)
            doh = _to_half(do_ref[:, _swa_cols(h)], first, a, b)
            stage1.append((jnp.where(valid, _dot_nt(qh, k2[pb]), NEG_BIG), _dot_nt(doh, v2[pb])))
        deltas = []
        for pair in range(SWA_Q_HEADS // 2):
            prod = do_ref[:, _swa_cols(2 * pair)].astype(F32) * o_ref[:, _swa_cols(2 * pair)].astype(F32)
            deltas += [jnp.sum(jnp.where(first, prod, 0.0), axis=1, keepdims=True),
                       jnp.sum(jnp.where(first, 0.0, prod), axis=1, keepdims=True)]
        stage2 = []
        dsink = jnp.zeros((1, LANES), F32)
        for h, (s, dp) in enumerate(stage1):
            lse_h = lse_ref[:, h:h + 1]
            p = jnp.exp(s - lse_h)
            delta = deltas[h]
            p_sink = jnp.exp(sink_ref[h] - lse_h)
            dsink = dsink + jnp.where(lane1 == h, -jnp.sum(p_sink * delta, axis=0, keepdims=True), 0.0)
            stage2.append(((p * (dp - delta)).astype(BF16), p.astype(BF16)))
        dqs = []
        dk_t = [None] * SWA_KV_HEADS
        dv_t = [None] * SWA_KV_HEADS
        for h, (ds, pb16) in enumerate(stage2):
            kvh = h // SWA_GROUP
            dims = slice(h * HEAD_DIM, (h + 1) * HEAD_DIM)
            dq = _dot(ds, k2[_swa_kv_pair(h)])
            dqs.append(pltpu.roll(dq, HEAD_DIM, 1) if h % 2 != _swa_kv_half(h) else dq)
            dk_h = _dot(q_t[dims, :], ds)
            dv_h = _dot(do_t[dims, :], pb16)
            dk_t[kvh] = dk_h if dk_t[kvh] is None else dk_t[kvh] + dk_h
            dv_t[kvh] = dv_h if dv_t[kvh] is None else dv_t[kvh] + dv_h
        for pair in range(SWA_Q_HEADS // 2):
            dqp = jnp.where(first, dqs[2 * pair], dqs[2 * pair + 1])
            dq_ref[:, _swa_cols(2 * pair)] = ((dqp * cos_ + _swap32(dqp * sin_)) * ATTN_SCALE).astype(BF16)
        for pb in range(SWA_KV_HEADS // 2):
            dk2 = jnp.concatenate([dk_t[2 * pb], dk_t[2 * pb + 1]], axis=0).T
            dv2 = jnp.concatenate([dv_t[2 * pb], dv_t[2 * pb + 1]], axis=0).T
            kcols = slice(pb * LANES, (pb + 1) * LANES)
            vcols = slice(kv_w + pb * LANES, kv_w + (pb + 1) * LANES)
            prv_ref[:, kcols] = dk2[:WINDOW]
            own_ref[:, kcols] = dk2[WINDOW:]
            prv_ref[:, vcols] = dv2[:WINDOW]
            own_ref[:, vcols] = dv2[WINDOW:]
        dsink_ref[...] += dsink

    tab = pl.BlockSpec((WINDOW, LANES), lambda n: (n, 0))
    kvrow = pl.BlockSpec((WINDOW, 2 * kv_w), lambda n: (n, 0))
    return pl.pallas_call(
        body,
        name="swa_bwd",
        grid=(nb,),
        in_specs=[smem, qrow, cur, prev, cur, prev, qrow, qrow, tab, tab, tab],
        out_specs=[qrow, kvrow, kvrow, pl.BlockSpec((1, LANES), lambda n: (0, 0))],
        out_shape=[
            jax.ShapeDtypeStruct((T, D_MODEL), BF16),
            jax.ShapeDtypeStruct((T, 2 * kv_w), F32),
            jax.ShapeDtypeStruct((T, 2 * kv_w), F32),
            jax.ShapeDtypeStruct((1, LANES), F32),
        ],
        compiler_params=_cparams(),
    )(sinks, q, k, k, v, v, do, o, lse, cos, sin)


def kv_grad_combine(own, prv, cos, sin):
    T = own.shape[0]
    nb = T // WINDOW
    kv_w = SWA_KV_HEADS * HEAD_DIM

    def body(own_ref, nxt_ref, cos_ref, sin_ref, out_ref):
        n = pl.program_id(0)
        nxt = jnp.where(n + 1 < nb, nxt_ref[...], 0.0)
        tot = own_ref[...] + nxt
        dk = tot[:, :kv_w]
        c = _tile_lanes(cos_ref[...], kv_w)
        s = _tile_lanes(sin_ref[...], kv_w)
        out_ref[:, :kv_w] = (dk * c + _swap32(dk * s)).astype(BF16)
        out_ref[:, kv_w:] = tot[:, kv_w:].astype(BF16)

    tab = pl.BlockSpec((WINDOW, LANES), lambda n: (n, 0))
    kvrow = pl.BlockSpec((WINDOW, 2 * kv_w), lambda n: (n, 0))
    return pl.pallas_call(
        body,
        name="kv_grad_combine",
        grid=(nb,),
        in_specs=[kvrow, pl.BlockSpec((WINDOW, 2 * kv_w), lambda n: (jnp.minimum(n + 1, nb - 1), 0)), tab, tab],
        out_specs=kvrow,
        out_shape=jax.ShapeDtypeStruct((T, 2 * kv_w), BF16),
        compiler_params=_cparams(),
    )(own, prv, cos, sin)


ANY = pl.BlockSpec(memory_space=pl.ANY)


def _place():
    x, y, c = lax.axis_index("x"), lax.axis_index("y"), lax.axis_index("c")
    other_chips = [(1 - x, y), (x, 1 - y), (1 - x, 1 - y)]
    return x, y, c, 2 * x + y, other_chips


N_PEER_CHIPS = N_CHIPS - 1


class GatherOps:
    def __init__(self, rows, shards, lands, ici_send, ici_recv, d2d_send=None, d2d_recv=None):
        self.rows, self.shards, self.lands = rows, shards, lands
        self.ici_send, self.ici_recv, self.d2d_send, self.d2d_recv = ici_send, ici_recv, d2d_send, d2d_recv
        self.x, self.y, self.c, self.me, self.chips = _place()
        self.pairs = [(t, jdx) for t in range(len(rows)) for jdx in range(N_PEER_CHIPS)]

    def _half(self, ref, t, which):
        r = self.rows[t] // 2
        return ref.at[:, pl.ds(pl.multiple_of(which * r, 16), r), :]

    def _ici(self, t, jdx):
        px, py = self.chips[jdx]
        return pltpu.make_async_remote_copy(
            src_ref=self._half(self.shards[t], t, self.c), dst_ref=self._half(self.lands[t].at[self.me], t, self.c),
            send_sem=self.ici_send.at[t, jdx], recv_sem=self.ici_recv.at[t, jdx],
            device_id=(px, py, self.c), device_id_type=MESH)

    def _landed(self, t, jdx):
        px, py = self.chips[jdx]
        blk = self._half(self.lands[t].at[2 * px + py], t, self.c)
        return pltpu.make_async_remote_copy(
            src_ref=blk, dst_ref=blk, send_sem=self.ici_send.at[t, jdx], recv_sem=self.ici_recv.at[t, jdx],
            device_id=(px, py, self.c), device_id_type=MESH)

    def _d2d(self, t, jdx, which):
        px, py = self.chips[jdx]
        blk = self._half(self.lands[t].at[2 * px + py], t, which)
        return pltpu.make_async_remote_copy(
            src_ref=blk, dst_ref=blk, send_sem=self.d2d_send.at[t, jdx], recv_sem=self.d2d_recv.at[t, jdx],
            device_id=(self.x, self.y, 1 - self.c), device_id_type=MESH)

    def start_ici(self):
        for t, jdx in self.pairs:
            self._ici(t, jdx).start()

    def wait_ici(self):
        for t, jdx in self.pairs:
            self._landed(t, jdx).wait_recv()
        self.wait_ici_sends()

    def wait_ici_sends(self):
        for t, jdx in self.pairs:
            self._ici(t, jdx).wait_send()

    def forward_arrivals(self):
        for t, jdx in self.pairs:
            self._landed(t, jdx).wait_recv()
            self._d2d(t, jdx, self.c).start()

    def start_forwards(self):
        for t, jdx in self.pairs:
            self._d2d(t, jdx, self.c).start()

    def wait_forwards(self):
        for t, jdx in self.pairs:
            self._d2d(t, jdx, 1 - self.c).wait_recv()
            self._d2d(t, jdx, self.c).wait_send()


def all_gather_weights(shards, lands):
    n = len(shards)
    rows = [s.shape[1] for s in shards]

    def body(*refs):
        ins, outs = refs[:n], refs[2 * n:3 * n]
        ops = GatherOps(rows, ins, outs, *refs[3 * n:])
        ops.start_ici()
        ops.forward_arrivals()
        ops.wait_forwards()
        ops.wait_ici_sends()

    return pl.pallas_call(
        body,
        name="all_gather_weights",
        in_specs=[ANY] * (2 * n),
        out_specs=[ANY] * n,
        out_shape=[jax.ShapeDtypeStruct(l.shape, l.dtype) for l in lands],
        input_output_aliases={n + t: t for t in range(n)},
        scratch_shapes=[pltpu.SemaphoreType.DMA((n, N_PEER_CHIPS))] * 4,
    )(*shards, *lands)


def place_own_shard(name, shard, chip):
    nl, r, c = shard.shape

    def body(chip_ref, s_ref, o_ref):
        o_ref[...] = s_ref[...]

    return pl.pallas_call(
        body, name=name,
        grid_spec=pltpu.PrefetchScalarGridSpec(
            num_scalar_prefetch=1, grid=(nl,),
            in_specs=[pl.BlockSpec((None, r, c), lambda l, chip_ref: (l, 0, 0))],
            out_specs=pl.BlockSpec((None, None, r, c), lambda l, chip_ref: (chip_ref[0], l, 0, 0))),
        out_shape=jax.ShapeDtypeStruct((N_CHIPS,) + shard.shape, shard.dtype), compiler_params=_cparams(),
    )(chip, shard)


def exchange_halves(name, slabs):
    n = len(slabs)

    def body(*refs):
        ins, theirs = refs[:n], refs[n:2 * n]
        send_sems, recv_sems = refs[2 * n:]
        x, y, c, _, _ = _place()
        copies = []
        for t in range(n):
            cp = pltpu.make_async_remote_copy(
                src_ref=ins[t].at[1 - c], dst_ref=theirs[t], send_sem=send_sems.at[t],
                recv_sem=recv_sems.at[t], device_id=(x, y, 1 - c), device_id_type=MESH)
            cp.start()
            copies.append(cp)
        for cp in copies:
            cp.wait()

    return pl.pallas_call(
        body,
        name=name,
        in_specs=[ANY] * n,
        out_specs=[ANY] * n,
        out_shape=[jax.ShapeDtypeStruct(s.shape[1:], s.dtype) for s in slabs],
        scratch_shapes=[pltpu.SemaphoreType.DMA((n,)), pltpu.SemaphoreType.DMA((n,))],
    )(*slabs)


def chip_partial_copies(ins, outs, send_sems, recv_sems):
    _, _, c, me, chips = _place()
    return [pltpu.make_async_remote_copy(
        src_ref=ins[t].at[2 * px + py], dst_ref=outs[t].at[me], send_sem=send_sems.at[t, jdx],
        recv_sem=recv_sems.at[t, jdx], device_id=(px, py, c), device_id_type=MESH)
        for t in range(len(ins)) for jdx, (px, py) in enumerate(chips)]


def exchange_chip_partials(name, parts):
    n = len(parts)

    def body(*refs):
        copies = chip_partial_copies(refs[:n], refs[n:2 * n], *refs[2 * n:])
        for cp in copies:
            cp.start()
        for cp in copies:
            cp.wait()

    return pl.pallas_call(
        body,
        name=name,
        in_specs=[ANY] * n,
        out_specs=[ANY] * n,
        out_shape=[jax.ShapeDtypeStruct(p.shape, p.dtype) for p in parts],
        scratch_shapes=[pltpu.SemaphoreType.DMA((n, 3)), pltpu.SemaphoreType.DMA((n, 3))],
    )(*parts)


def share_reduced_halves(name, halves):
    n = len(halves)

    def body(*refs):
        ins, outs = refs[:n], refs[n:2 * n]
        send_sems, recv_sems = refs[2 * n:]
        x, y, c, _, _ = _place()
        copies = []
        for t in range(n):
            cp = pltpu.make_async_remote_copy(
                src_ref=ins[t], dst_ref=outs[t], send_sem=send_sems.at[t],
                recv_sem=recv_sems.at[t], device_id=(x, y, 1 - c), device_id_type=MESH)
            cp.start()
            copies.append(cp)
        for cp in copies:
            cp.wait()

    return pl.pallas_call(
        body,
        name=name,
        in_specs=[ANY] * n,
        out_specs=[ANY] * n,
        out_shape=[jax.ShapeDtypeStruct(h.shape, h.dtype) for h in halves],
        scratch_shapes=[pltpu.SemaphoreType.DMA((n,)), pltpu.SemaphoreType.DMA((n,))],
    )(*halves)


def _row_tile(r, c):
    tr = r
    while tr * c * 4 > (3 << 19) and tr % 16 == 0:
        tr //= 2
    return tr


def add_sibling(name, slab, theirs, core):
    _, ns, slots, r, c = slab.shape
    tr = _row_tile(r, c)

    def body(core_ref, a_ref, b_ref, o_ref):
        o_ref[...] = (a_ref[...] + b_ref[...]).astype(BF16)

    blk = pl.BlockSpec((None, None, tr, c), lambda s, l, i, core_ref: (s, l, i, 0))
    return pl.pallas_call(
        body, name=name,
        grid_spec=pltpu.PrefetchScalarGridSpec(
            num_scalar_prefetch=1, grid=(ns, slots, r // tr),
            in_specs=[pl.BlockSpec((None, None, None, tr, c), lambda s, l, i, core_ref: (core_ref[0], s, l, i, 0)), blk],
            out_specs=blk),
        out_shape=jax.ShapeDtypeStruct(theirs.shape, BF16), compiler_params=_cparams(),
    )(core, slab, theirs)


def sum_chips(name, recv, own, chip):
    _, slots, r, c = recv.shape
    tr = _row_tile(r, c)

    def body(chip_ref, r0, r1, r2, r3, own_ref, o_ref):
        me = chip_ref[0]
        mine = own_ref[...]
        terms = [jnp.where(me == s, mine, rr[...]).astype(F32) for s, rr in enumerate((r0, r1, r2, r3))]
        o_ref[...] = ((terms[0] + terms[1]) + terms[2]) + terms[3]

    def src(s):
        return pl.BlockSpec((None, None, tr, c),
                            lambda l, i, chip_ref: (jnp.where(chip_ref[0] == s, (s + 1) % N_CHIPS, s), l, i, 0))

    return pl.pallas_call(
        body, name=name,
        grid_spec=pltpu.PrefetchScalarGridSpec(
            num_scalar_prefetch=1, grid=(slots, r // tr),
            in_specs=[src(0), src(1), src(2), src(3),
                      pl.BlockSpec((None, None, tr, c), lambda l, i, chip_ref: (chip_ref[0], l, i, 0))],
            out_specs=pl.BlockSpec((None, tr, c), lambda l, i, chip_ref: (l, i, 0))),
        out_shape=jax.ShapeDtypeStruct((slots, r, c), F32), compiler_params=_cparams(),
    )(chip, recv, recv, recv, recv, own)


def _adamw_math(w, g, m, v):
    m = ADAM_B1 * m + (1.0 - ADAM_B1) * g
    v = ADAM_B2 * v + (1.0 - ADAM_B2) * (g * g)
    m_hat = m / (1.0 - ADAM_B1 ** ADAM_STEP)
    v_hat = v / (1.0 - ADAM_B2 ** ADAM_STEP)
    delta = -ADAM_LR * (m_hat / (jnp.sqrt(v_hat) + ADAM_EPS) + ADAM_WD * w)
    return delta, m, v


def adamw_shard(name, w, m, v, g_pairs, core, slots, row_halves):
    n = w.shape[0]
    assert n == len(g_pairs)
    _, r, c = g_pairs[0][0].shape
    tr = _row_tile(r, c)
    nr = r // tr

    def body(core_ref, w_ref, m_ref, v_ref, *rest):
        g_refs, (go_ref, d_ref, mo_ref, vo_ref) = rest[:2 * n], rest[2 * n:]
        mine = pl.program_id(1) == core_ref[0]
        g = jnp.where(mine, g_refs[0][...], g_refs[1][...])
        for l in range(1, n):
            g = jnp.where(pl.program_id(0) == l, jnp.where(mine, g_refs[2 * l][...], g_refs[2 * l + 1][...]), g)
        delta, mm, vv = _adamw_math(w_ref[...], g, m_ref[...], v_ref[...])
        go_ref[...] = g
        d_ref[...] = delta
        mo_ref[...] = mm
        vo_ref[...] = vv

    if row_halves:
        wspec = pl.BlockSpec((None, tr, c), lambda l, h, i, core_ref: (l, h * nr + i, 0))
    else:
        wspec = pl.BlockSpec((None, tr, c), lambda l, h, i, core_ref: (l, i, h))
    def gspec(slot):
        return pl.BlockSpec((None, tr, c), lambda l, h, i, core_ref: (slot, i, 0))

    shp = jax.ShapeDtypeStruct(w.shape, F32)
    return pl.pallas_call(
        body, name=name,
        grid_spec=pltpu.PrefetchScalarGridSpec(
            num_scalar_prefetch=1, grid=(n, 2, nr),
            in_specs=[wspec, wspec, wspec] + [gspec(s) for s in slots for _ in range(2)], out_specs=[wspec] * 4),
        out_shape=[shp] * 4, compiler_params=_cparams(),
    )(core, w, m, v, *[g for pair in g_pairs for g in pair])


SMALL_ROWS = 16


def small_allreduce_adamw(part, w, m, v):
    def body(p_ref, w_ref, m_ref, v_ref, g_ref, d_ref, mo_ref, vo_ref, buf, send_sems, recv_sems):
        x, y, c, _, _ = _place()
        me = 4 * x + 2 * y + c
        buf[me] = p_ref[...]
        copies = []
        for k in range(1, N_DEV):
            kx, ky, kc = (k >> 2) & 1, (k >> 1) & 1, k & 1
            peer = (x ^ kx, y ^ ky, c ^ kc)
            cp = pltpu.make_async_remote_copy(
                src_ref=p_ref, dst_ref=buf.at[me], send_sem=send_sems.at[k - 1],
                recv_sem=recv_sems.at[k - 1], device_id=peer, device_id_type=MESH)
            cp.start()
            copies.append(cp)
        for cp in copies:
            cp.wait()
        g = buf[0]
        for dev in range(1, N_DEV):
            g = g + buf[dev]
        delta, mm, vv = _adamw_math(w_ref[...], g, m_ref[...], v_ref[...])
        g_ref[...] = g
        d_ref[...] = delta
        mo_ref[...] = mm
        vo_ref[...] = vv

    vm = pl.BlockSpec(memory_space=pltpu.VMEM)
    shp = jax.ShapeDtypeStruct(part.shape, F32)
    return pl.pallas_call(
        body, name="small_allreduce_adamw",
        in_specs=[vm] * 4, out_specs=[vm] * 4, out_shape=[shp] * 4,
        scratch_shapes=[
            pltpu.VMEM((N_DEV,) + part.shape, F32),
            pltpu.SemaphoreType.DMA((N_DEV - 1,)), pltpu.SemaphoreType.DMA((N_DEV - 1,)),
        ],
    )(part, w, m, v)


def _rope_tables(T):
    half = HEAD_DIM // 2
    inv_freq = ROPE_THETA ** (-jnp.arange(half, dtype=F32) / half)
    ang = jnp.arange(T).astype(F32)[:, None] * inv_freq[None, :]
    cos = jnp.tile(jnp.cos(ang), (1, LANES // half))
    sin = jnp.tile(jnp.sin(ang), (1, LANES // half))
    lane = jnp.arange(LANES)
    sign = jnp.where((lane % HEAD_DIM) < half, -1.0, 1.0).astype(F32)
    return cos, sin * sign[None, :]


def _pack_small(ffn1, mix, ffn2, kvn, fin, sinks, loss_row):
    sink_row = jnp.pad(sinks.reshape(1, SWA_Q_HEADS), ((0, 0), (0, D_MODEL - SWA_Q_HEADS)))
    rows = jnp.concatenate([ffn1, mix, ffn2, kvn.reshape(1, -1), fin.reshape(1, -1), sink_row, loss_row], axis=0)
    return jnp.concatenate([rows, jnp.zeros((SMALL_ROWS - rows.shape[0], D_MODEL), F32)], axis=0)


def kernel(x, ffn1_norm, ffn1_w_in, ffn1_w_out, mix_norm, ffn2_norm, ffn2_w_in, ffn2_w_out, sb_w_qkv, sb_w_o, kv_norm, kv_w, swa_w_q, swa_sinks, swa_w_o, final_norm, loss_target, m_ffn1_norm, m_ffn1_w_in, m_ffn1_w_out, m_mix_norm, m_ffn2_norm, m_ffn2_w_in, m_ffn2_w_out, m_sb_w_qkv, m_sb_w_o, m_kv_norm, m_kv_w, m_swa_w_q, m_swa_sinks, m_swa_w_o, m_final_norm, v_ffn1_norm, v_ffn1_w_in, v_ffn1_w_out, v_mix_norm, v_ffn2_norm, v_ffn2_w_in, v_ffn2_w_out, v_sb_w_qkv, v_sb_w_o, v_kv_norm, v_kv_w, v_swa_w_q, v_swa_sinks, v_swa_w_o, v_final_norm):
    T = x.shape[1]
    kv_cols = SWA_KV_HEADS * HEAD_DIM
    x2 = x.reshape(T, D_MODEL)
    tgt = loss_target.reshape(T, D_MODEL)
    cos, sin = _rope_tables(T)

    w_in_l = jnp.concatenate([ffn1_w_in, ffn2_w_in], axis=0).astype(BF16)
    w_out_l = jnp.concatenate([ffn1_w_out, ffn2_w_out], axis=0).astype(BF16)
    sq_l = jnp.concatenate([sb_w_o, swa_w_q, swa_w_o], axis=0).astype(BF16)
    qkv_l = sb_w_qkv[0].astype(BF16)
    kvw_l = kv_w.astype(BF16)
    core = lax.axis_index("c").astype(jnp.int32).reshape(1)
    chip = (2 * lax.axis_index("x") + lax.axis_index("y")).astype(jnp.int32).reshape(1)
    early = [w_in_l[:1], w_out_l[:1]]
    mid = [sq_l, qkv_l[None]]
    late = [w_in_l[1:], w_out_l[1:], kvw_l[None]]
    early_lands = [place_own_shard(f"own_early_{t}", s, chip) for t, s in enumerate(early)]
    mid_lands = [place_own_shard(f"own_mid_{t}", s, chip) for t, s in enumerate(mid)]
    late_lands = [place_own_shard(f"own_late_{t}", s, chip) for t, s in enumerate(late)]
    w_in0, w_out0 = all_gather_weights(early, early_lands)

    def ffn_w(slot):
        return (w_in0, w_out0, 0) if slot == 0 else (w_in_r, w_out_r, slot - 1)

    def vec(a, i):
        return a[i].reshape(1, D_MODEL)

    ident = lambda w: w
    sq_prep = lambda w: w.reshape(D_MODEL, w.shape[-1])
    qscale = jnp.concatenate([jnp.full((1, D_MODEL), ATTN_SCALE, F32), jnp.ones((1, 2 * D_MODEL), F32)], axis=1)
    swa_scale = jnp.full((1, D_MODEL), ATTN_SCALE, F32)
    sinks = swa_sinks.reshape(SWA_Q_HEADS)

    h1, gate1, up1, (w_sq, w_qkv) = ffn_fwd("l0a", x2, vec(ffn1_norm, 0), *ffn_w(SLOT_FFN1[0]), mid, mid_lands)
    w_qkv = w_qkv.reshape(N_CHIPS, D_MODEL, QKV_COLS)
    qkv = qkv_fwd(h1, vec(mix_norm, 0), w_qkv, qscale)
    o_sb, tot, sb_first, late_lands = sb_fwd(qkv, late, late_lands)
    h2, (w_in_r, w_out_r, w_kv) = linear_res("sb_out", o_sb, w_sq, SQ_SB_O, h1, late_lands)
    w_kv = w_kv.reshape(D_MODEL, 2 * kv_cols)
    h3, gate2, up2 = ffn_fwd("l0b", h2, vec(ffn2_norm, 0), *ffn_w(SLOT_FFN2[0]))
    kvn = kv_norm.reshape(1, D_MODEL)
    k_sw = rms_linear("kv_k", h3, kvn, w_kv, pl.BlockSpec((D_MODEL, kv_cols), lambda i, j: (0, 0)), ident,
                      kv_cols, kv_cols, rope=(cos, sin))
    v_sw = rms_linear("kv_v", h3, kvn, w_kv, pl.BlockSpec((D_MODEL, kv_cols), lambda i, j: (0, 1)), ident,
                      kv_cols, kv_cols)
    h4, gate3, up3 = ffn_fwd("l1a", h3, vec(ffn1_norm, 1), *ffn_w(SLOT_FFN1[1]))
    q_sw = rms_linear("swa_q", h4, vec(mix_norm, 1), w_sq,
                      pl.BlockSpec((N_CHIPS, None, SQ_ROWS, 512), lambda i, j: (0, SQ_SWA_Q, 0, j)), sq_prep,
                      D_MODEL, 512, rope=(cos, sin), scale=swa_scale)
    o_sw, lse = swa_fwd(q_sw, k_sw, v_sw, sinks)
    h5 = linear_res("swa_out", o_sw, w_sq, SQ_SWA_O, h4)
    h6, gate4, up4 = ffn_fwd("l1b", h5, vec(ffn2_norm, 1), *ffn_w(SLOT_FFN2[1]))
    dh6, loss_p, d_final = loss_bwd(h6, final_norm.reshape(1, D_MODEL), tgt)

    slab = {}
    ffn_place = {SLOT_FFN1[0]: (0, 0, 1), SLOT_FFN1[1]: (1, 0, 3), SLOT_FFN2[0]: (1, 1, 3), SLOT_FFN2[1]: (1, 2, 3)}
    sq_place = {SQ_SB_O: (1, 0, 3), SQ_SWA_Q: (1, 1, 3), SQ_SWA_O: (1, 2, 3)}

    def ffn_grads(tag, dh, h_in, g, gate, up, slot):
        dh_in, xn, dg_, du_, act, dhb, dnorm = ffn_bwd(tag, dh, h_in, g, gate, up, *ffn_w(slot))
        grp, s, ns = ffn_place[slot]
        in_shape = (2, N_CHIPS, ns, D_MODEL // 2, FF_CHUNK)
        out_shape = (2, N_CHIPS, ns, FF_ROWS, D_MODEL // 2)
        blk = (None, 1, None, D_MODEL // 2, FF_CHUNK)
        slab["in", grp] = mm_tn(f"dw_gate_{tag}", xn, dg_, D_MODEL // 2, FF_CHUNK, blk,
                                lambda k, n: (k, n, s, 0, 0), in_shape, prev=slab.get(("in", grp)))
        slab["in", grp] = mm_tn(f"dw_up_{tag}", xn, du_, D_MODEL // 2, FF_CHUNK, blk,
                                lambda k, n: (k, 2 + n, s, 0, 0), in_shape, prev=slab["in", grp])
        slab["out", grp] = mm_tn(f"dw_out_{tag}", act, dhb, FF_CHUNK, D_MODEL // 2,
                                 (None, 2, None, FF_ROWS, D_MODEL // 2),
                                 lambda k, n: (n, k, s, 0, 0), out_shape, prev=slab.get(("out", grp)))
        return dh_in, dnorm

    def sq_grad(tag, a, dyb, t):
        grp, s, ns = sq_place[t]
        slab["sq", grp] = mm_tn(f"dw_sq_{tag}", a, dyb, D_MODEL, D_MODEL // 2,
                                (None, N_CHIPS, None, SQ_ROWS, D_MODEL // 2),
                                lambda k, n: (n, 0, s, 0, 0), (2, N_CHIPS, ns, SQ_ROWS, D_MODEL // 2),
                                prev=slab.get(("sq", grp)))

    def reduce_group(grp, kinds, host=None):
        slabs = [slab[kind, grp][0] for kind in kinds]
        names = [f"{kind}{grp}" for kind in kinds]
        theirs = exchange_halves(f"exchange_halves_{grp}", [slab[kind, grp][1] for kind in kinds])
        parts = [add_sibling(f"add_sibling_{nm}", s, t, core) for nm, s, t in zip(names, slabs, theirs)]
        arrived = host(parts) if host else exchange_chip_partials(f"exchange_chip_partials_{grp}", parts)
        halves = [sum_chips(f"sum_chips_{nm}", g, p, chip) for nm, g, p in zip(names, arrived, parts)]
        sib_halves = share_reduced_halves(f"share_reduced_halves_{grp}", halves)
        return {kind: pair for kind, pair in zip(kinds, zip(halves, sib_halves))}

    dh5, d_ffn2_1 = ffn_grads("l1b", dh6, h5, vec(ffn2_norm, 1), gate4, up4, SLOT_FFN2[1])
    do_sw, dh5b = linear_bwd_plain("swa_out_bwd", dh5, w_sq, SQ_SWA_O)
    sq_grad("swa_o", o_sw, dh5b, SQ_SWA_O)
    dq_sw, kv_own, kv_prev, d_sinks = swa_bwd(q_sw, k_sw, v_sw, sinks, do_sw, o_sw, lse, cos, sin)
    sq_w_spec = pl.BlockSpec((N_CHIPS, None, SQ_ROWS, D_MODEL), lambda i, j: (0, SQ_SWA_Q, 0, 0))
    dh4, hn4, d_mix_1 = linear_bwd_rms("swa_q_bwd", [(dq_sw, w_sq, sq_w_spec, sq_prep)], h4, vec(mix_norm, 1), dh5,
                                       1, D_MODEL)
    sq_grad("swa_q", hn4, dq_sw, SQ_SWA_Q)
    dh3a, d_ffn1_1 = ffn_grads("l1a", dh4, h3, vec(ffn1_norm, 1), gate3, up3, SLOT_FFN1[1])
    dkv = kv_grad_combine(kv_own, kv_prev, cos, sin)
    kv_w_spec = pl.BlockSpec((D_MODEL, 2 * kv_cols), lambda i, j: (0, 0))
    dh3, xn3, d_kvn = linear_bwd_rms("kv_bwd", [(dkv, w_kv, kv_w_spec, ident)], h3, kvn, dh3a, 1, 2 * kv_cols)
    slab["kv", 1] = mm_tn("dw_kv", xn3, dkv, D_MODEL, kv_cols, (None, N_CHIPS, None, SQ_ROWS, kv_cols),
                          lambda k, n: (n, 0, 0, 0, 0), (2, N_CHIPS, 1, SQ_ROWS, kv_cols))
    dh2, d_ffn2_0 = ffn_grads("l0b", dh3, h2, vec(ffn2_norm, 0), gate2, up2, SLOT_FFN2[0])
    do_sb, dh2b = linear_bwd_plain("sb_out_bwd", dh2, w_sq, SQ_SB_O)
    sq_grad("sb_o", o_sb, dh2b, SQ_SB_O)
    sb_grads = []

    def behind_sb_bwd(parts):
        dq_sb, dk_sb, dv_sb, arrived = sb_bwd(qkv, do_sb, tot, sb_first, parts)
        sb_grads.extend([dq_sb, dk_sb, dv_sb])
        return arrived

    red = {1: reduce_group(1, ["in", "out", "sq", "kv"], host=behind_sb_bwd)}
    dqkv = jnp.concatenate(sb_grads, axis=1)
    dh1, hn1, d_mix_0 = qkv_bwd(dqkv, w_qkv, h1, vec(mix_norm, 0), dh2)
    slab["qkv", 0] = mm_tn("dw_qkv", hn1, dqkv, D_MODEL // 2, QKV_COLS, (None, 1, None, D_MODEL // 2, QKV_COLS),
                           lambda k, n: (k, n, 0, 0, 0), (2, N_CHIPS, 1, D_MODEL // 2, QKV_COLS))
    dx, d_ffn1_0 = ffn_grads("l0a", dh1, x2, vec(ffn1_norm, 0), gate1, up1, SLOT_FFN1[0])
    red[0] = reduce_group(0, ["in", "out", "qkv"])

    def upd(name, w, m, v, kind, places, row_halves):
        shp = w.shape
        w3 = w.reshape((-1,) + shp[-2:])
        outs = adamw_shard(name, w3, m.reshape(w3.shape), v.reshape(w3.shape),
                           [red[grp][kind] for grp, _ in places], core, [s for _, s in places], row_halves)
        return [o.reshape(shp) for o in outs]

    ffn1_places = [ffn_place[s][:2] for s in SLOT_FFN1]
    ffn2_places = [ffn_place[s][:2] for s in SLOT_FFN2]
    r_ffn1_in = upd("adamw_ffn1_in", ffn1_w_in, m_ffn1_w_in, v_ffn1_w_in, "in", ffn1_places, True)
    r_ffn2_in = upd("adamw_ffn2_in", ffn2_w_in, m_ffn2_w_in, v_ffn2_w_in, "in", ffn2_places, True)
    r_ffn1_out = upd("adamw_ffn1_out", ffn1_w_out, m_ffn1_w_out, v_ffn1_w_out, "out", ffn1_places, False)
    r_ffn2_out = upd("adamw_ffn2_out", ffn2_w_out, m_ffn2_w_out, v_ffn2_w_out, "out", ffn2_places, False)
    r_qkv = upd("adamw_qkv", sb_w_qkv, m_sb_w_qkv, v_sb_w_qkv, "qkv", [(0, 0)], True)
    r_sb_o = upd("adamw_sb_o", sb_w_o, m_sb_w_o, v_sb_w_o, "sq", [sq_place[SQ_SB_O][:2]], False)
    r_swa_q = upd("adamw_swa_q", swa_w_q, m_swa_w_q, v_swa_w_q, "sq", [sq_place[SQ_SWA_Q][:2]], False)
    r_swa_o = upd("adamw_swa_o", swa_w_o, m_swa_w_o, v_swa_w_o, "sq", [sq_place[SQ_SWA_O][:2]], False)
    r_kv = upd("adamw_kv", kv_w, m_kv_w, v_kv_w, "kv", [(1, 0)], False)

    loss_row = jnp.pad(loss_p, ((0, 0), (0, D_MODEL - LANES)))
    d_sink_row = d_sinks[0, :SWA_Q_HEADS]
    part = _pack_small(jnp.concatenate([d_ffn1_0, d_ffn1_1], axis=0), jnp.concatenate([d_mix_0, d_mix_1], axis=0),
                       jnp.concatenate([d_ffn2_0, d_ffn2_1], axis=0), d_kvn, d_final, d_sink_row, loss_row)
    zrow = jnp.zeros((1, D_MODEL), F32)
    small = small_allreduce_adamw(
        part,
        _pack_small(ffn1_norm, mix_norm, ffn2_norm, kv_norm, final_norm, swa_sinks, zrow),
        _pack_small(m_ffn1_norm, m_mix_norm, m_ffn2_norm, m_kv_norm, m_final_norm, m_swa_sinks, zrow),
        _pack_small(v_ffn1_norm, v_mix_norm, v_ffn2_norm, v_kv_norm, v_final_norm, v_swa_sinks, zrow))

    def unpack(p):
        return dict(ffn1_norm=p[0:2], mix_norm=p[2:4], ffn2_norm=p[4:6], kv_norm=p[6], final_norm=p[7],
                    swa_sinks=p[8:9, :SWA_Q_HEADS])

    big = dict(ffn1_w_in=r_ffn1_in, ffn1_w_out=r_ffn1_out, ffn2_w_in=r_ffn2_in, ffn2_w_out=r_ffn2_out,
               sb_w_qkv=r_qkv, sb_w_o=r_sb_o, kv_w=r_kv, swa_w_q=r_swa_q, swa_w_o=r_swa_o)
    order = ["ffn1_norm", "ffn1_w_in", "ffn1_w_out", "mix_norm", "ffn2_norm", "ffn2_w_in", "ffn2_w_out",
             "sb_w_qkv", "sb_w_o", "kv_norm", "kv_w", "swa_w_q", "swa_sinks", "swa_w_o", "final_norm"]
    outs = []
    for kind in range(4):
        sm = unpack(small[kind])
        for nm in order:
            outs.append(big[nm][kind] if nm in big else sm[nm])
    loss = small[0][9, 0]
    return (loss, dx.reshape(x.shape), *outs)
```

```python
import functools

import jax
import jax.numpy as jnp
from jax import lax
from jax.experimental import pallas as pl
from jax.experimental.pallas import tpu as pltpu

F32 = jnp.float32
BF16 = jnp.bfloat16
MESH = pl.DeviceIdType.MESH

D_MODEL = 1024
D_FF = 2816
HEAD_DIM = 64
SB_HEADS = 16
SWA_Q_HEADS = 16
SWA_KV_HEADS = 4
WINDOW = 128
ROPE_THETA = 10000.0
RMS_EPS = 1e-6
FFN_RES_SCALE = 0.5
ATTN_SCALE = HEAD_DIM ** -0.5

ADAM_LR = 0.001
ADAM_B1 = 0.9
ADAM_B2 = 0.999
ADAM_EPS = 1e-08
ADAM_WD = 0.01
ADAM_STEP = 10

N_CHIPS = 4
N_DEV = 8
LANES = 128
FF_CHUNK = D_FF // 2
FF_ROWS = D_FF // N_CHIPS
SQ_ROWS = D_MODEL // N_CHIPS
QKV_COLS = 3 * D_MODEL // N_CHIPS
VMEM_LIMIT = 56 * 1024 * 1024
NEG_BIG = -1e30

SLOT_FFN1 = (0, 1)
SLOT_FFN2 = (2, 3)
SQ_SB_O, SQ_SWA_Q, SQ_SWA_O = 0, 1, 2


def _cparams():
    return pltpu.CompilerParams(vmem_limit_bytes=VMEM_LIMIT)


def _dot(a, b):
    return jnp.dot(a, b, preferred_element_type=F32)


def _dot_nt(a, b):
    return lax.dot_general(a, b, (((1,), (1,)), ((), ())), preferred_element_type=F32)


def _dot_tn(a, b):
    return lax.dot_general(a, b, (((0,), (0,)), ((), ())), preferred_element_type=F32)


def _rstd(h):
    return lax.rsqrt(jnp.mean(h * h, axis=-1, keepdims=True) + RMS_EPS)


def _swap32(x):
    n = x.shape[-1]
    lane = lax.broadcasted_iota(jnp.int32, x.shape, x.ndim - 1)
    first = (lane % HEAD_DIM) < (HEAD_DIM // 2)
    return jnp.where(first, pltpu.roll(x, n - HEAD_DIM // 2, x.ndim - 1), pltpu.roll(x, HEAD_DIM // 2, x.ndim - 1))


def _tile_lanes(t, n):
    return t if n == LANES else jnp.tile(t, (1, n // LANES))


FFN_ROWS = 256


def _ffn_w_in_spec(slot):
    return pl.BlockSpec((N_CHIPS, None, D_MODEL, FF_CHUNK), lambda i: (0, slot, 0, 0), pipeline_mode=pl.Buffered(1))


def _ffn_w_out_spec(slot):
    return pl.BlockSpec((N_CHIPS, None, FF_ROWS, D_MODEL), lambda i: (0, slot, 0, 0), pipeline_mode=pl.Buffered(1))


def ffn_fwd(tag, h, g, w_in, w_out, slot, bg_shards=(), bg_lands=()):
    T = h.shape[0]
    tm = FFN_ROWS
    nch = D_FF // FF_CHUNK
    nbg = len(bg_shards)
    nt = T // tm

    def body(h_ref, g_ref, wi_ref, wo_ref, *rest):
        out_ref, gate_ref, up_ref = rest[2 * nbg:2 * nbg + 3]
        wg_s, wu_s = rest[3 * nbg + 3:3 * nbg + 5]
        step = pl.program_id(0)
        if nbg:
            gather = GatherOps([s.shape[1] for s in bg_shards], rest[:nbg], rest[2 * nbg + 3:3 * nbg + 3],
                               *rest[3 * nbg + 5:])
            pl.when(step == 0)(gather.start_ici)
            pl.when(step == nt // 2)(gather.forward_arrivals)

            @pl.when(step == nt - 1)
            def _():
                gather.wait_forwards()
                gather.wait_ici_sends()

        @pl.when(step == 0)
        def _():
            for j in range(nch):
                cols = slice(j * FF_CHUNK, (j + 1) * FF_CHUNK)
                wg_s[:, cols] = wi_ref[j]
                wu_s[:, cols] = wi_ref[nch + j]

        hh = h_ref[...]
        xn = (hh * _rstd(hh) * g_ref[...]).astype(BF16)
        gate = _dot(xn, wg_s[...])
        up = _dot(xn, wu_s[...])
        gate_ref[...] = gate.astype(BF16)
        up_ref[...] = up.astype(BF16)
        a = (gate * jax.nn.sigmoid(gate) * up).astype(BF16)
        out_ref[...] = hh + FFN_RES_SCALE * _dot(a, wo_ref[...].reshape(D_FF, D_MODEL))

    row = pl.BlockSpec((tm, D_MODEL), lambda i: (i, 0))
    ff = pl.BlockSpec((tm, D_FF), lambda i: (i, 0))
    res = pl.pallas_call(
        body,
        name=f"ffn_fwd_{tag}",
        grid=(nt,),
        in_specs=[row, pl.BlockSpec((1, D_MODEL), lambda i: (0, 0)), _ffn_w_in_spec(slot), _ffn_w_out_spec(slot)]
        + [ANY] * (2 * nbg),
        out_specs=[row, ff, ff] + [ANY] * nbg,
        out_shape=[
            jax.ShapeDtypeStruct((T, D_MODEL), F32),
            jax.ShapeDtypeStruct((T, D_FF), BF16),
            jax.ShapeDtypeStruct((T, D_FF), BF16),
        ] + [jax.ShapeDtypeStruct(l.shape, l.dtype) for l in bg_lands],
        input_output_aliases={4 + nbg + t: 3 + t for t in range(nbg)},
        scratch_shapes=[pltpu.VMEM((D_MODEL, D_FF), BF16), pltpu.VMEM((D_MODEL, D_FF), BF16)]
        + [pltpu.SemaphoreType.DMA((nbg, N_PEER_CHIPS))] * (4 if nbg else 0),
        compiler_params=_cparams(),
    )(h, g, w_in, w_out, *bg_shards, *bg_lands)
    return (res[0], res[1], res[2], list(res[3:])) if nbg else tuple(res)


def ffn_bwd(tag, dh, h, g, gate, up, w_in, w_out, slot):
    T = dh.shape[0]
    tm = FFN_ROWS
    nch = D_FF // FF_CHUNK

    def body(dh_ref, h_ref, g_ref, gate_ref, up_ref, wi_ref, wo_ref,
             dhin_ref, xn_ref, dg_ref, du_ref, a_ref, dhb_ref, dnorm_ref):
        @pl.when(pl.program_id(0) == 0)
        def _():
            dnorm_ref[...] = jnp.zeros_like(dnorm_ref)

        dhh = dh_ref[...]
        dhb = (FFN_RES_SCALE * dhh).astype(BF16)
        dhb_ref[...] = dhb
        dxn = None
        for j in range(nch):
            cols = slice(j * FF_CHUNK, (j + 1) * FF_CHUNK)
            da = _dot_nt(dhb, wo_ref[2 * j:2 * j + 2].reshape(FF_CHUNK, D_MODEL))
            gt = gate_ref[:, cols].astype(F32)
            u = up_ref[:, cols].astype(F32)
            s = jax.nn.sigmoid(gt)
            silu = gt * s
            a_ref[:, cols] = (silu * u).astype(BF16)
            dgate = (da * u * (s * (1.0 + gt * (1.0 - s)))).astype(BF16)
            dup = (da * silu).astype(BF16)
            dg_ref[:, cols] = dgate
            du_ref[:, cols] = dup
            part = _dot_nt(dgate, wi_ref[j]) + _dot_nt(dup, wi_ref[nch + j])
            dxn = part if dxn is None else dxn + part
        hh = h_ref[...]
        gg = g_ref[...]
        r = _rstd(hh)
        hr = hh * r
        xn_ref[...] = (hr * gg).astype(BF16)
        dnorm_ref[...] += jnp.sum(dxn * hr, axis=0, keepdims=True)
        gd = gg * dxn
        dhin_ref[...] = dhh + r * (gd - hr * jnp.mean(gd * hr, axis=-1, keepdims=True))

    row = pl.BlockSpec((tm, D_MODEL), lambda i: (i, 0))
    ff = pl.BlockSpec((tm, D_FF), lambda i: (i, 0))
    vec = pl.BlockSpec((1, D_MODEL), lambda i: (0, 0))
    return pl.pallas_call(
        body,
        name=f"ffn_bwd_{tag}",
        grid=(T // tm,),
        in_specs=[row, row, vec, ff, ff, _ffn_w_in_spec(slot), _ffn_w_out_spec(slot)],
        out_specs=[row, row, ff, ff, ff, row, vec],
        out_shape=[
            jax.ShapeDtypeStruct((T, D_MODEL), F32),
            jax.ShapeDtypeStruct((T, D_MODEL), BF16),
            jax.ShapeDtypeStruct((T, D_FF), BF16),
            jax.ShapeDtypeStruct((T, D_FF), BF16),
            jax.ShapeDtypeStruct((T, D_FF), BF16),
            jax.ShapeDtypeStruct((T, D_MODEL), BF16),
            jax.ShapeDtypeStruct((1, D_MODEL), F32),
        ],
        compiler_params=_cparams(),
    )(dh, h, g, gate, up, w_in, w_out)


def rms_linear(name, h, g, w, w_spec, w_prep, n_out, tn, *, rope=None, scale=None):
    T = h.shape[0]
    tm = 512
    extra, extra_specs = [], []
    if rope is not None:
        extra += list(rope)
        extra_specs += [pl.BlockSpec((tm, LANES), lambda i, j: (i, 0))] * 2
    if scale is not None:
        extra.append(scale)
        extra_specs.append(pl.BlockSpec((1, tn), lambda i, j: (0, j)))

    def body(h_ref, g_ref, w_ref, *rest):
        rest = list(rest)
        cos_ref = sin_ref = sc_ref = None
        if rope is not None:
            cos_ref, sin_ref = rest[0], rest[1]
            rest = rest[2:]
        if scale is not None:
            sc_ref = rest[0]
            rest = rest[1:]
        out_ref, xn_s = rest

        @pl.when(pl.program_id(1) == 0)
        def _():
            hh = h_ref[...]
            xn_s[...] = (hh * _rstd(hh) * g_ref[...]).astype(BF16)

        y = _dot(xn_s[...], w_prep(w_ref[...]))
        if rope is not None:
            y = y * _tile_lanes(cos_ref[...], tn) + _swap32(y) * _tile_lanes(sin_ref[...], tn)
        if scale is not None:
            y = y * sc_ref[...]
        out_ref[...] = y.astype(BF16)

    return pl.pallas_call(
        body,
        name=name,
        grid=(T // tm, n_out // tn),
        in_specs=[
            pl.BlockSpec((tm, D_MODEL), lambda i, j: (i, 0)),
            pl.BlockSpec((1, D_MODEL), lambda i, j: (0, 0)),
            w_spec,
        ] + extra_specs,
        out_specs=pl.BlockSpec((tm, tn), lambda i, j: (i, j)),
        out_shape=jax.ShapeDtypeStruct((T, n_out), BF16),
        scratch_shapes=[pltpu.VMEM((tm, D_MODEL), BF16)],
        compiler_params=_cparams(),
    )(h, g, w, *extra)


QKV_ROWS = 512


def _qkv_w_spec():
    return pl.BlockSpec((N_CHIPS, D_MODEL, QKV_COLS), lambda i: (0, 0, 0), pipeline_mode=pl.Buffered(1))


def qkv_fwd(h, g, w_qkv, scale):
    T = h.shape[0]
    tm = QKV_ROWS

    def body(h_ref, g_ref, w_ref, sc_ref, out_ref):
        hh = h_ref[...]
        xn = (hh * _rstd(hh) * g_ref[...]).astype(BF16)
        for s in range(N_CHIPS):
            cols = slice(s * QKV_COLS, (s + 1) * QKV_COLS)
            out_ref[:, cols] = (_dot(xn, w_ref[s]) * sc_ref[:, cols]).astype(BF16)

    return pl.pallas_call(
        body,
        name="sb_qkv",
        grid=(T // tm,),
        in_specs=[
            pl.BlockSpec((tm, D_MODEL), lambda i: (i, 0)),
            pl.BlockSpec((1, D_MODEL), lambda i: (0, 0)),
            _qkv_w_spec(),
            pl.BlockSpec((1, 3 * D_MODEL), lambda i: (0, 0)),
        ],
        out_specs=pl.BlockSpec((tm, 3 * D_MODEL), lambda i: (i, 0)),
        out_shape=jax.ShapeDtypeStruct((T, 3 * D_MODEL), BF16),
        compiler_params=_cparams(),
    )(h, g, w_qkv, scale)


def qkv_bwd(dy, w_qkv, h, g, dres):
    T = h.shape[0]
    tm = QKV_ROWS

    def body(dy_ref, w_ref, h_ref, g_ref, dres_ref, dh_ref, xn_ref, dg_ref):
        @pl.when(pl.program_id(0) == 0)
        def _():
            dg_ref[...] = jnp.zeros_like(dg_ref)

        dxn = None
        for s in range(N_CHIPS):
            part = _dot_nt(dy_ref[:, s * QKV_COLS:(s + 1) * QKV_COLS], w_ref[s])
            dxn = part if dxn is None else dxn + part
        hh = h_ref[...]
        gg = g_ref[...]
        r = _rstd(hh)
        hr = hh * r
        xn_ref[...] = (hr * gg).astype(BF16)
        dg_ref[...] += jnp.sum(dxn * hr, axis=0, keepdims=True)
        gd = gg * dxn
        dh_ref[...] = dres_ref[...] + r * (gd - hr * jnp.mean(gd * hr, axis=-1, keepdims=True))

    row = pl.BlockSpec((tm, D_MODEL), lambda i: (i, 0))
    vec = pl.BlockSpec((1, D_MODEL), lambda i: (0, 0))
    return pl.pallas_call(
        body,
        name="sb_qkv_bwd",
        grid=(T // tm,),
        in_specs=[pl.BlockSpec((tm, 3 * D_MODEL), lambda i: (i, 0)), _qkv_w_spec(), row, vec, row],
        out_specs=[row, row, vec],
        out_shape=[
            jax.ShapeDtypeStruct((T, D_MODEL), F32),
            jax.ShapeDtypeStruct((T, D_MODEL), BF16),
            jax.ShapeDtypeStruct((1, D_MODEL), F32),
        ],
        compiler_params=_cparams(),
    )(dy, w_qkv, h, g, dres)


def linear_res(name, a, w_sq, t, res, bg_lands=()):
    T = a.shape[0]
    tm = 512
    nbg = len(bg_lands)
    nt = T // tm

    def body(a_ref, w_ref, res_ref, *rest):
        out_ref = rest[nbg]
        if nbg:
            gather = GatherOps([l.shape[2] for l in bg_lands], None, rest[nbg + 1:2 * nbg + 1], None, None,
                               *rest[2 * nbg + 1:])

            @pl.when(pl.program_id(0) == 0)
            def _():
                gather.start_forwards()

        out_ref[...] = res_ref[...] + _dot(a_ref[...], w_ref[...].reshape(D_MODEL, D_MODEL))
        if nbg:
            @pl.when(pl.program_id(0) == nt - 1)
            def _():
                gather.wait_forwards()

    row = pl.BlockSpec((tm, D_MODEL), lambda i: (i, 0))
    res_ = pl.pallas_call(
        body,
        name=name,
        grid=(nt,),
        in_specs=[row, pl.BlockSpec((N_CHIPS, None, SQ_ROWS, D_MODEL), lambda i: (0, t, 0, 0)), row] + [ANY] * nbg,
        out_specs=[row] + [ANY] * nbg,
        out_shape=[jax.ShapeDtypeStruct((T, D_MODEL), F32)] + [jax.ShapeDtypeStruct(l.shape, l.dtype) for l in bg_lands],
        input_output_aliases={3 + k: 1 + k for k in range(nbg)},
        scratch_shapes=[pltpu.SemaphoreType.DMA((nbg, N_PEER_CHIPS))] * (2 if nbg else 0),
        compiler_params=_cparams(),
    )(a, w_sq, res, *bg_lands)
    return (res_[0], list(res_[1:])) if nbg else res_[0]


def linear_bwd_plain(name, dy, w_sq, t):
    T = dy.shape[0]
    tm = 512

    def body(dy_ref, w_ref, da_ref, dyb_ref):
        dyb = dy_ref[...].astype(BF16)
        dyb_ref[...] = dyb
        da_ref[...] = _dot_nt(dyb, w_ref[...].reshape(D_MODEL, D_MODEL)).astype(BF16)

    row = pl.BlockSpec((tm, D_MODEL), lambda i: (i, 0))
    return pl.pallas_call(
        body,
        name=name,
        grid=(T // tm,),
        in_specs=[row, pl.BlockSpec((N_CHIPS, None, SQ_ROWS, D_MODEL), lambda i: (0, t, 0, 0))],
        out_specs=[row, row],
        out_shape=[jax.ShapeDtypeStruct((T, D_MODEL), BF16), jax.ShapeDtypeStruct((T, D_MODEL), BF16)],
        compiler_params=_cparams(),
    )(dy, w_sq)


def linear_bwd_rms(name, pairs, h, g, dres, nch, tn, tm=256):
    T = h.shape[0]
    npair = len(pairs)

    def body(*refs):
        dy_refs = refs[:npair]
        w_refs = refs[npair:2 * npair]
        h_ref, g_ref, dres_ref, dh_ref, xn_ref, dg_ref, acc_s = refs[2 * npair:]
        i = pl.program_id(0)
        j = pl.program_id(1)

        @pl.when(j == 0)
        def _():
            acc_s[...] = jnp.zeros_like(acc_s)

        @pl.when((i == 0) & (j == 0))
        def _():
            dg_ref[...] = jnp.zeros_like(dg_ref)

        part = None
        for p in range(npair):
            d = _dot_nt(dy_refs[p][...], pairs[p][3](w_refs[p][...]))
            part = d if part is None else part + d
        acc_s[...] += part

        @pl.when(j == nch - 1)
        def _():
            dxn = acc_s[...]
            hh = h_ref[...]
            gg = g_ref[...]
            r = _rstd(hh)
            hr = hh * r
            xn_ref[...] = (hr * gg).astype(BF16)
            dg_ref[...] += jnp.sum(dxn * hr, axis=0, keepdims=True)
            gd = gg * dxn
            dh_ref[...] = dres_ref[...] + r * (gd - hr * jnp.mean(gd * hr, axis=-1, keepdims=True))

    row = pl.BlockSpec((tm, D_MODEL), lambda i, j: (i, 0))
    vec = pl.BlockSpec((1, D_MODEL), lambda i, j: (0, 0))
    return pl.pallas_call(
        body,
        name=name,
        grid=(T // tm, nch),
        in_specs=[pl.BlockSpec((tm, tn), lambda i, j: (i, j))] * npair + [p[2] for p in pairs] + [row, vec, row],
        out_specs=[row, row, vec],
        out_shape=[
            jax.ShapeDtypeStruct((T, D_MODEL), F32),
            jax.ShapeDtypeStruct((T, D_MODEL), BF16),
            jax.ShapeDtypeStruct((1, D_MODEL), F32),
        ],
        scratch_shapes=[pltpu.VMEM((tm, D_MODEL), F32)],
        compiler_params=_cparams(),
    )(*[p[0] for p in pairs], *[p[1] for p in pairs], h, g, dres)


def loss_bwd(h, g, tgt):
    T = h.shape[0]
    tm = 512

    def body(h_ref, g_ref, t_ref, dh_ref, loss_ref, dg_ref):
        @pl.when(pl.program_id(0) == 0)
        def _():
            loss_ref[...] = jnp.zeros_like(loss_ref)
            dg_ref[...] = jnp.zeros_like(dg_ref)

        hh = h_ref[...]
        gg = g_ref[...]
        r = _rstd(hh)
        hr = hh * r
        err = hr * gg - t_ref[...]
        loss_ref[...] += 0.5 * jnp.sum(jnp.mean(err * err, axis=-1, keepdims=True), axis=0, keepdims=True)
        dy = err * (1.0 / D_MODEL)
        dg_ref[...] += jnp.sum(dy * hr, axis=0, keepdims=True)
        gd = gg * dy
        dh_ref[...] = r * (gd - hr * jnp.mean(gd * hr, axis=-1, keepdims=True))

    row = pl.BlockSpec((tm, D_MODEL), lambda i: (i, 0))
    vec = pl.BlockSpec((1, D_MODEL), lambda i: (0, 0))
    return pl.pallas_call(
        body,
        name="loss_bwd",
        grid=(T // tm,),
        in_specs=[row, vec, row],
        out_specs=[row, pl.BlockSpec((1, LANES), lambda i: (0, 0)), vec],
        out_shape=[
            jax.ShapeDtypeStruct((T, D_MODEL), F32),
            jax.ShapeDtypeStruct((1, LANES), F32),
            jax.ShapeDtypeStruct((1, D_MODEL), F32),
        ],
        compiler_params=_cparams(),
    )(h, g, tgt)


DW_TOKENS = 4096


def mm_tn(name, a, b, tk, tn, out_block, out_index, out_shape, prev=None, tt=DW_TOKENS):
    T = a.shape[0]
    ns, r = out_block[1], out_block[3]
    tt = min(tt, T)
    nt = T // tt

    def body(*refs):
        a_ref, b_ref = refs[:2]
        out_ref, copy_ref = refs[-2:]
        t = pl.program_id(2)
        res = _dot_tn(a_ref[...], b_ref[...])

        @pl.when(t == 0)
        def _():
            for u in range(ns):
                out_ref[u] = res[u * r:(u + 1) * r]

        @pl.when(t > 0)
        def _():
            for u in range(ns):
                out_ref[u] += res[u * r:(u + 1) * r]

        @pl.when(t == nt - 1)
        def _():
            copy_ref[...] = out_ref[...].astype(BF16)

    in_specs = [
        pl.BlockSpec((tt, tk), lambda k, n, t: (t, k)),
        pl.BlockSpec((tt, tn), lambda k, n, t: (t, n)),
    ]
    args = [a, b]
    aliases = {}
    if prev is not None:
        in_specs += [pl.BlockSpec(memory_space=pl.ANY)] * 2
        args += list(prev)
        aliases = {2: 0, 3: 1}
    out_spec = pl.BlockSpec(out_block, lambda k, n, t: out_index(k, n))
    return tuple(pl.pallas_call(
        body,
        name=name,
        grid=(a.shape[1] // tk, b.shape[1] // tn, nt),
        in_specs=in_specs,
        out_specs=[out_spec, out_spec],
        out_shape=[jax.ShapeDtypeStruct(out_shape, F32), jax.ShapeDtypeStruct(out_shape, BF16)],
        input_output_aliases=aliases,
        compiler_params=_cparams(),
    )(*args))


SB_BLOCK = 256
SB_QROWS = 256
SB_QROWS_BWD = 256
SB_UNDERFLOW_BITS = 140.0
SB_CHUNK = 128


LOG2E = 1.4426950408889634


def _softplus2(z2):
    sign = jnp.uint32(0x80000000)
    neg_abs = lax.bitcast_convert_type(lax.bitcast_convert_type(z2, jnp.uint32) | sign, F32)
    return jnp.log2(1.0 + jnp.exp2(neg_abs)) + jnp.maximum(z2, 0.0)


def _twice(x):
    return jnp.concatenate([x, x], axis=1)


def sb_fwd(qkv, bg_shards=(), bg_lands=()):
    T = qkv.shape[0]
    tq, tk = SB_QROWS, SB_BLOCK
    ratio = tq // tk
    npair = SB_HEADS // 2
    nbg = len(bg_shards)
    nq = T // tq

    def body(q_ref, k_ref, v_ref, *rest):
        bg_in = rest[:nbg]
        o_ref, tot_ref, first_ref = rest[2 * nbg:2 * nbg + 3]
        bg_out = rest[2 * nbg + 3:3 * nbg + 3]
        acc_s, c_s, z_s, w_s, kmax_s = rest[3 * nbg + 3:3 * nbg + 8]
        p = pl.program_id(0)
        i = pl.program_id(1)
        if nbg:
            gather = GatherOps([s.shape[1] for s in bg_shards], bg_in, bg_out, *rest[3 * nbg + 8:])

            @pl.when((p == 0) & (i == 0))
            def _():
                gather.start_ici()

        @pl.when(i == 0)
        def _():
            kmax_s[...] = jnp.max(jnp.abs(k_ref[...]), axis=0, keepdims=True).astype(F32)

        q = q_ref[...]
        lane = lax.broadcasted_iota(jnp.int32, (tq, LANES), 1)
        first = lane < HEAD_DIM
        zero = jnp.zeros_like(q)
        q_heads = (jnp.where(first, q, zero), jnp.where(first, zero, q))
        row = lax.broadcasted_iota(jnp.int32, (tq, tk), 0)
        col = lax.broadcasted_iota(jnp.int32, (tq, tk), 1)
        visible = [col + r * tk < row for r in range(ratio)]
        krow = lax.broadcasted_iota(jnp.int32, (tk, tk), 0)
        kcol = lax.broadcasted_iota(jnp.int32, (tk, tk), 1)
        from_s = (krow >= kcol).astype(BF16)
        acc_s[...] = jnp.zeros_like(acc_s)
        c_s[...] = jnp.zeros_like(c_s)

        def rows(j):
            return pl.ds(pl.multiple_of(j * tk, tk), tk)

        def logits(j):
            kb = k_ref[rows(j), :]
            for hd in range(2):
                z_s[hd] = _dot_nt(q_heads[hd], kb) * LOG2E

        def flush(j):
            vb = v_ref[rows(j), :]
            for hd in range(2):
                acc_s[hd] += _dot(w_s[hd], vb)

        def block(j, mask=None, flush_block=None):
            if flush_block is not None:
                flush(flush_block)
            chunks = [(hd, slice(r0, r0 + SB_CHUNK)) for hd in range(2) for r0 in range(0, tq, SB_CHUNK)]
            k_next = k_ref[rows(jnp.maximum(j - 1, 0)), :]
            es, sums = [], []
            for hd, rs in chunks:
                z2 = z_s[hd, rs, :]
                z_s[hd, rs, :] = _dot_nt(q_heads[hd][rs, :], k_next) * LOG2E
                if mask is not None:
                    z2 = jnp.where(mask if mask.ndim == 0 else mask[rs, :], z2, NEG_BIG)
                sp = _softplus2(z2)
                c = c_s[hd, rs, :]
                es.append(z2 + _twice(c))
                c_s[hd, rs, :] = c - jnp.sum(sp, axis=1, keepdims=True)
                sums.append(_dot(sp.astype(BF16), from_s))
            for (hd, rs), e, s in zip(chunks, es, sums):
                w_s[hd, rs, :] = jnp.exp2(e - s).astype(BF16)

        z_bound = [LOG2E * jnp.sum(jnp.abs(q_heads[hd].astype(F32)) * kmax_s[...], axis=1, keepdims=True)
                   for hd in range(2)]

        def more_keys_matter():
            top = jnp.maximum(c_s[0] + z_bound[0], c_s[1] + z_bound[1])
            return (jnp.max(top) >= -SB_UNDERFLOW_BITS).astype(jnp.int32)

        assert ratio == 1
        logits(i)
        block(i, visible[0])

        def trip(carry):
            trips, _ = carry
            j = i - 1 - trips
            block(j, flush_block=j + 1)
            return trips + 1, more_keys_matter()

        trips, _ = lax.while_loop(lambda carry: jnp.logical_and(carry[0] < i, carry[1] > 0), trip,
                                  (jnp.int32(0), jnp.int32(1)))
        first_walked = i - trips
        flush(first_walked)
        first_ref[p, i] = first_walked.astype(F32)
        o_ref[...] = jnp.where(first, acc_s[0], acc_s[1]).astype(BF16)
        tot_ref[...] = jnp.where(first, c_s[0], c_s[1])
        if nbg:
            @pl.when((p == npair - 1) & (i == nq - 1))
            def _():
                gather.wait_ici()

    res = pl.pallas_call(
        body,
        name="sb_fwd",
        grid=(npair, nq),
        in_specs=[
            pl.BlockSpec((tq, LANES), lambda p, i: (i, p)),
            pl.BlockSpec((T, LANES), lambda p, i: (0, npair + p)),
            pl.BlockSpec((T, LANES), lambda p, i: (0, 2 * npair + p)),
        ] + [ANY] * (2 * nbg),
        out_specs=[pl.BlockSpec((tq, LANES), lambda p, i: (i, p))] * 2 + [pl.BlockSpec(memory_space=pltpu.SMEM)]
        + [ANY] * nbg,
        out_shape=[jax.ShapeDtypeStruct((T, D_MODEL), BF16), jax.ShapeDtypeStruct((T, D_MODEL), F32),
                   jax.ShapeDtypeStruct((npair, nq), F32)]
        + [jax.ShapeDtypeStruct(l.shape, l.dtype) for l in bg_lands],
        input_output_aliases={3 + nbg + t: 3 + t for t in range(nbg)},
        scratch_shapes=[
            pltpu.VMEM((2, tq, LANES), F32), pltpu.VMEM((2, tq, LANES), F32),
            pltpu.VMEM((2, tq, tk), F32), pltpu.VMEM((2, tq, tk), BF16),
            pltpu.VMEM((1, LANES), F32),
        ] + [pltpu.SemaphoreType.DMA((nbg, N_PEER_CHIPS))] * (2 if nbg else 0),
        compiler_params=_cparams(),
    )(qkv, qkv, qkv, *bg_shards, *bg_lands)
    return res[0], res[1], res[2], list(res[3:])


def sb_bwd(qkv, do, tot, first_block, bg_parts=()):
    T = qkv.shape[0]
    tq, tk = SB_QROWS_BWD, SB_BLOCK
    ratio = tq // tk
    npair = SB_HEADS // 2
    nq = T // tq
    nk = T // tk
    nbg = len(bg_parts)
    assert SB_QROWS == SB_QROWS_BWD

    def body(first_ref, q_ref, k_ref, v_ref, do_ref, tot_ref, *rest):
        bg_in = rest[:nbg]
        dq_ref, dk_ref, dv_ref = rest[nbg:nbg + 3]
        bg_out = rest[nbg + 3:2 * nbg + 3]
        dkt_s, dvt_s, dq_s, rest_s, cg_s, z_s, da_s, dz_s, a_s = rest[2 * nbg + 3:2 * nbg + 12]
        i = pl.program_id(1)
        start = jnp.clip(first_ref[pl.program_id(0), i].astype(jnp.int32), 0, i)
        if nbg:
            @pl.when((pl.program_id(0) == 0) & (i == 0))
            def _():
                for cp in chip_partial_copies(bg_in, bg_out, *rest[2 * nbg + 12:]):
                    cp.start()

        @pl.when(i == 0)
        def _():
            dkt_s[...] = jnp.zeros_like(dkt_s)
            dvt_s[...] = jnp.zeros_like(dvt_s)

        q = q_ref[...]
        do_ = do_ref[...]
        tot_ = tot_ref[...]
        q_t = q.astype(F32).T.astype(BF16)
        do_t = do_.astype(F32).T.astype(BF16)
        lane = lax.broadcasted_iota(jnp.int32, (tq, LANES), 1)
        first = lane < HEAD_DIM
        zero = jnp.zeros_like(q)
        q_heads = (jnp.where(first, q, zero), jnp.where(first, zero, q))
        do_heads = (jnp.where(first, do_, zero), jnp.where(first, zero, do_))
        row = lax.broadcasted_iota(jnp.int32, (tq, tk), 0)
        col = lax.broadcasted_iota(jnp.int32, (tq, tk), 1)
        visible = [col + r * tk < row for r in range(ratio)]
        krow = lax.broadcasted_iota(jnp.int32, (tk, tk), 0)
        kcol = lax.broadcasted_iota(jnp.int32, (tk, tk), 1)
        before = (krow < kcol).astype(BF16)
        from_s = (krow >= kcol).astype(BF16)
        last = ratio * i + ratio - 1
        rest_s[0] = jnp.broadcast_to(tot_[:, 0:1], (tq, LANES))
        rest_s[1] = jnp.broadcast_to(tot_[:, HEAD_DIM:HEAD_DIM + 1], (tq, LANES))
        cg_s[...] = jnp.zeros_like(cg_s)
        dq_s[...] = jnp.zeros_like(dq_s)
        dz_s[...] = jnp.zeros_like(dz_s)
        a_s[...] = jnp.zeros_like(a_s)

        def rows(j):
            return pl.ds(pl.multiple_of(j * tk, tk), tk)

        def logits(j):
            kb = k_ref[rows(j), :]
            vb = v_ref[rows(j), :]
            for hd in range(2):
                z_s[hd] = _dot_nt(q_heads[hd], kb) * LOG2E
                da_s[hd] = _dot_nt(do_heads[hd], vb)

        def flush(j):
            kb = k_ref[rows(j), :]
            for hd in range(2):
                dims = slice(hd * HEAD_DIM, (hd + 1) * HEAD_DIM)
                dq_s[hd] += _dot(dz_s[hd], kb)
                dkt_s[j, dims, :] += _dot(q_t[dims, :], dz_s[hd])
                dvt_s[j, dims, :] += _dot(do_t[dims, :], a_s[hd])

        def block(j, mask=None):
            flush(jnp.maximum(j - 1, 0))
            chunks = [(hd, slice(r0, r0 + SB_CHUNK)) for hd in range(2) for r0 in range(0, tq, SB_CHUNK)]
            nxt = rows(jnp.minimum(j + 1, last))
            k_next = k_ref[nxt, :]
            v_next = v_ref[nxt, :]
            stage1 = []
            for hd, rs in chunks:
                z2 = z_s[hd, rs, :]
                z_s[hd, rs, :] = _dot_nt(q_heads[hd][rs, :], k_next) * LOG2E
                if mask is not None:
                    z2 = jnp.where(mask if mask.ndim == 0 else mask[rs, :], z2, NEG_BIG)
                sp = _softplus2(z2)
                rest = rest_s[hd, rs, :] + jnp.sum(sp, axis=1, keepdims=True)
                rest_s[hd, rs, :] = rest
                stage1.append((z2 + _twice(rest), z2 - sp, _dot(sp.astype(BF16), from_s)))
            stage2 = []
            for (hd, rs), (e, log2_beta, ahead) in zip(chunks, stage1):
                a = jnp.exp2(e - ahead)
                g = a * da_s[hd, rs, :]
                da_s[hd, rs, :] = _dot_nt(do_heads[hd][rs, :], v_next)
                cg = cg_s[hd, rs, :]
                a_s[hd, rs, :] = a.astype(BF16)
                cg_s[hd, rs, :] = cg + jnp.sum(g, axis=1, keepdims=True)
                stage2.append((g, g + _twice(cg), log2_beta, _dot(g.astype(BF16), before)))
            for (hd, rs), (g, g_from, log2_beta, g_before) in zip(chunks, stage2):
                dz_s[hd, rs, :] = (g - jnp.exp2(log2_beta) * (g_from + g_before)).astype(BF16)

        assert ratio == 1
        logits(start)

        @pl.loop(start, i)
        def _(j):
            block(j)

        block(i, visible[0])
        flush(last)
        dq_ref[...] = (jnp.where(first, dq_s[0], dq_s[1]) * ATTN_SCALE).astype(BF16)

        @pl.when(i == nq - 1)
        def _():
            @pl.loop(0, nk)
            def _(b):
                dk_ref[rows(b), :] = dkt_s[b].T.astype(BF16)
                dv_ref[rows(b), :] = dvt_s[b].T.astype(BF16)

        if nbg:
            @pl.when((pl.program_id(0) == npair - 1) & (i == nq - 1))
            def _():
                for cp in chip_partial_copies(bg_in, bg_out, *rest[2 * nbg + 12:]):
                    cp.wait()

    qblk = pl.BlockSpec((tq, LANES), lambda p, i: (i, p))
    full = pl.BlockSpec((T, LANES), lambda p, i: (0, p))
    res = pl.pallas_call(
        body,
        name="sb_bwd",
        grid=(npair, nq),
        in_specs=[
            pl.BlockSpec(memory_space=pltpu.SMEM),
            qblk,
            pl.BlockSpec((T, LANES), lambda p, i: (0, npair + p)),
            pl.BlockSpec((T, LANES), lambda p, i: (0, 2 * npair + p)),
            qblk, qblk,
        ] + [ANY] * nbg,
        out_specs=[qblk, full, full] + [ANY] * nbg,
        out_shape=[jax.ShapeDtypeStruct((T, D_MODEL), BF16)] * 3
        + [jax.ShapeDtypeStruct(b.shape, b.dtype) for b in bg_parts],
        scratch_shapes=[
            pltpu.VMEM((nk, LANES, tk), F32), pltpu.VMEM((nk, LANES, tk), F32),
            pltpu.VMEM((2, tq, LANES), F32), pltpu.VMEM((2, tq, LANES), F32), pltpu.VMEM((2, tq, LANES), F32),
            pltpu.VMEM((2, tq, tk), F32), pltpu.VMEM((2, tq, tk), F32),
            pltpu.VMEM((2, tq, tk), BF16), pltpu.VMEM((2, tq, tk), BF16),
        ] + [pltpu.SemaphoreType.DMA((nbg, N_PEER_CHIPS))] * (2 if nbg else 0),
        compiler_params=_cparams(),
    )(first_block, qkv, qkv, qkv, do, tot, *bg_parts)
    return res[0], res[1], res[2], list(res[3:])


def _swa_valid(n):
    qi = lax.broadcasted_iota(jnp.int32, (WINDOW, 2 * WINDOW), 0)
    ki = lax.broadcasted_iota(jnp.int32, (WINDOW, 2 * WINDOW), 1)
    diff = qi + WINDOW - ki
    return (diff >= 0) & (diff < WINDOW) & ((n > 0) | (ki >= WINDOW))


def _to_half(x, first, src, dst):
    keep = first if src == 0 else jnp.logical_not(first)
    x = jnp.where(keep, x, jnp.zeros_like(x))
    if src != dst:
        x = pltpu.roll(x.astype(F32), HEAD_DIM, 1).astype(BF16)
    return x


SWA_GROUP = SWA_Q_HEADS // SWA_KV_HEADS


def _swa_cols(h):
    return slice((h // 2) * LANES, (h // 2 + 1) * LANES)


def _swa_kv_pair(h):
    return (h // SWA_GROUP) // 2


def _swa_kv_half(h):
    return (h // SWA_GROUP) % 2


def _kv_band(prev_ref, cur_ref, pb):
    cols = slice(pb * LANES, (pb + 1) * LANES)
    return jnp.concatenate([prev_ref[:, cols], cur_ref[:, cols]], axis=0)


def _swa_specs(T):
    nb = T // WINDOW
    kv_w = SWA_KV_HEADS * HEAD_DIM
    qrow = pl.BlockSpec((WINDOW, D_MODEL), lambda n: (n, 0))
    cur = pl.BlockSpec((WINDOW, kv_w), lambda n: (n, 0))
    prev = pl.BlockSpec((WINDOW, kv_w), lambda n: (jnp.maximum(n - 1, 0), 0))
    smem = pl.BlockSpec(memory_space=pltpu.SMEM)
    return nb, qrow, cur, prev, smem


def swa_fwd(q, k, v, sinks):
    T = q.shape[0]
    nb, qrow, cur, prev, smem = _swa_specs(T)

    def body(sink_ref, q_ref, kc_ref, kp_ref, vc_ref, vp_ref, o_ref, lse_ref):
        n = pl.program_id(0)
        lane = lax.broadcasted_iota(jnp.int32, (WINDOW, LANES), 1)
        first = lane < HEAD_DIM
        valid = _swa_valid(n)
        k2 = [_kv_band(kp_ref, kc_ref, pb) for pb in range(SWA_KV_HEADS // 2)]
        v2 = [_kv_band(vp_ref, vc_ref, pb) for pb in range(SWA_KV_HEADS // 2)]
        logits = [jnp.where(valid, _dot_nt(_to_half(q_ref[:, _swa_cols(h)], first, h % 2, _swa_kv_half(h)),
                                            k2[_swa_kv_pair(h)]), NEG_BIG) for h in range(SWA_Q_HEADS)]
        probs = []
        lse_acc = jnp.zeros((WINDOW, LANES), F32)
        for h, s in enumerate(logits):
            sink = sink_ref[h]
            m = jnp.maximum(jnp.max(s, axis=1, keepdims=True), sink)
            p = jnp.exp(s - m)
            den = jnp.sum(p, axis=1, keepdims=True) + jnp.exp(sink - m)
            probs.append((p / den).astype(BF16))
            lse_acc = jnp.where(lane == h, m + jnp.log(den), lse_acc)
        outs = []
        for h, p in enumerate(probs):
            o = _dot(p, v2[_swa_kv_pair(h)])
            outs.append(pltpu.roll(o, HEAD_DIM, 1) if h % 2 != _swa_kv_half(h) else o)
        for pair in range(SWA_Q_HEADS // 2):
            o_ref[:, _swa_cols(2 * pair)] = jnp.where(first, outs[2 * pair], outs[2 * pair + 1]).astype(BF16)
        lse_ref[...] = lse_acc

    return pl.pallas_call(
        body,
        name="swa_fwd",
        grid=(nb,),
        in_specs=[smem, qrow, cur, prev, cur, prev],
        out_specs=[qrow, pl.BlockSpec((WINDOW, LANES), lambda n: (n, 0))],
        out_shape=[jax.ShapeDtypeStruct((T, D_MODEL), BF16), jax.ShapeDtypeStruct((T, LANES), F32)],
        compiler_params=_cparams(),
    )(sinks, q, k, k, v, v)


def swa_bwd(q, k, v, sinks, do, o, lse, cos, sin):
    T = q.shape[0]
    nb, qrow, cur, prev, smem = _swa_specs(T)
    kv_w = SWA_KV_HEADS * HEAD_DIM

    def body(sink_ref, q_ref, kc_ref, kp_ref, vc_ref, vp_ref, do_ref, o_ref, lse_ref, cos_ref, sin_ref,
             dq_ref, own_ref, prv_ref, dsink_ref):
        n = pl.program_id(0)

        @pl.when(n == 0)
        def _():
            dsink_ref[...] = jnp.zeros_like(dsink_ref)

        lane = lax.broadcasted_iota(jnp.int32, (WINDOW, LANES), 1)
        lane1 = lax.broadcasted_iota(jnp.int32, (1, LANES), 1)
        first = lane < HEAD_DIM
        valid = _swa_valid(n)
        cos_ = cos_ref[...]
        sin_ = sin_ref[...]
        k2 = [_kv_band(kp_ref, kc_ref, pb) for pb in range(SWA_KV_HEADS // 2)]
        v2 = [_kv_band(vp_ref, vc_ref, pb) for pb in range(SWA_KV_HEADS // 2)]
        q_t = q_ref[...].astype(F32).T.astype(BF16)
        do_t = do_ref[...].astype(F32).T.astype(BF16)
        stage1 = []
        for h in range(SWA_Q_HEADS):
            a, b, pb = h % 2, _swa_kv_half(h), _swa_kv_pair(h)
            qh = _to_half(q_ref[:, _swa_cols(h)], first, a, b)
            doh = _to_half(do_ref[:, _swa_cols(h)], first, a, b)
            stage1.append((jnp.where(valid, _dot_nt(qh, k2[pb]), NEG_BIG), _dot_nt(doh, v2[pb])))
        deltas = []
        for pair in range(SWA_Q_HEADS // 2):
            prod = do_ref[:, _swa_cols(2 * pair)].astype(F32) * o_ref[:, _swa_cols(2 * pair)].astype(F32)
            deltas += [jnp.sum(jnp.where(first, prod, 0.0), axis=1, keepdims=True),
                       jnp.sum(jnp.where(first, 0.0, prod), axis=1, keepdims=True)]
        stage2 = []
        dsink = jnp.zeros((1, LANES), F32)
        for h, (s, dp) in enumerate(stage1):
            lse_h = lse_ref[:, h:h + 1]
            p = jnp.exp(s - lse_h)
            delta = deltas[h]
            p_sink = jnp.exp(sink_ref[h] - lse_h)
            dsink = dsink + jnp.where(lane1 == h, -jnp.sum(p_sink * delta, axis=0, keepdims=True), 0.0)
            stage2.append(((p * (dp - delta)).astype(BF16), p.astype(BF16)))
        dqs = []
        dk_t = [None] * SWA_KV_HEADS
        dv_t = [None] * SWA_KV_HEADS
        for h, (ds, pb16) in enumerate(stage2):
            kvh = h // SWA_GROUP
            dims = slice(h * HEAD_DIM, (h + 1) * HEAD_DIM)
            dq = _dot(ds, k2[_swa_kv_pair(h)])
            dqs.append(pltpu.roll(dq, HEAD_DIM, 1) if h % 2 != _swa_kv_half(h) else dq)
            dk_h = _dot(q_t[dims, :], ds)
            dv_h = _dot(do_t[dims, :], pb16)
            dk_t[kvh] = dk_h if dk_t[kvh] is None else dk_t[kvh] + dk_h
            dv_t[kvh] = dv_h if dv_t[kvh] is None else dv_t[kvh] + dv_h
        for pair in range(SWA_Q_HEADS // 2):
            dqp = jnp.where(first, dqs[2 * pair], dqs[2 * pair + 1])
            dq_ref[:, _swa_cols(2 * pair)] = ((dqp * cos_ + _swap32(dqp * sin_)) * ATTN_SCALE).astype(BF16)
        for pb in range(SWA_KV_HEADS // 2):
            dk2 = jnp.concatenate([dk_t[2 * pb], dk_t[2 * pb + 1]], axis=0).T
            dv2 = jnp.concatenate([dv_t[2 * pb], dv_t[2 * pb + 1]], axis=0).T
            kcols = slice(pb * LANES, (pb + 1) * LANES)
            vcols = slice(kv_w + pb * LANES, kv_w + (pb + 1) * LANES)
            prv_ref[:, kcols] = dk2[:WINDOW]
            own_ref[:, kcols] = dk2[WINDOW:]
            prv_ref[:, vcols] = dv2[:WINDOW]
            own_ref[:, vcols] = dv2[WINDOW:]
        dsink_ref[...] += dsink

    tab = pl.BlockSpec((WINDOW, LANES), lambda n: (n, 0))
    kvrow = pl.BlockSpec((WINDOW, 2 * kv_w), lambda n: (n, 0))
    return pl.pallas_call(
        body,
        name="swa_bwd",
        grid=(nb,),
        in_specs=[smem, qrow, cur, prev, cur, prev, qrow, qrow, tab, tab, tab],
        out_specs=[qrow, kvrow, kvrow, pl.BlockSpec((1, LANES), lambda n: (0, 0))],
        out_shape=[
            jax.ShapeDtypeStruct((T, D_MODEL), BF16),
            jax.ShapeDtypeStruct((T, 2 * kv_w), F32),
            jax.ShapeDtypeStruct((T, 2 * kv_w), F32),
            jax.ShapeDtypeStruct((1, LANES), F32),
        ],
        compiler_params=_cparams(),
    )(sinks, q, k, k, v, v, do, o, lse, cos, sin)


def kv_grad_combine(own, prv, cos, sin):
    T = own.shape[0]
    nb = T // WINDOW
    kv_w = SWA_KV_HEADS * HEAD_DIM

    def body(own_ref, nxt_ref, cos_ref, sin_ref, out_ref):
        n = pl.program_id(0)
        nxt = jnp.where(n + 1 < nb, nxt_ref[...], 0.0)
        tot = own_ref[...] + nxt
        dk = tot[:, :kv_w]
        c = _tile_lanes(cos_ref[...], kv_w)
        s = _tile_lanes(sin_ref[...], kv_w)
        out_ref[:, :kv_w] = (dk * c + _swap32(dk * s)).astype(BF16)
        out_ref[:, kv_w:] = tot[:, kv_w:].astype(BF16)

    tab = pl.BlockSpec((WINDOW, LANES), lambda n: (n, 0))
    kvrow = pl.BlockSpec((WINDOW, 2 * kv_w), lambda n: (n, 0))
    return pl.pallas_call(
        body,
        name="kv_grad_combine",
        grid=(nb,),
        in_specs=[kvrow, pl.BlockSpec((WINDOW, 2 * kv_w), lambda n: (jnp.minimum(n + 1, nb - 1), 0)), tab, tab],
        out_specs=kvrow,
        out_shape=jax.ShapeDtypeStruct((T, 2 * kv_w), BF16),
        compiler_params=_cparams(),
    )(own, prv, cos, sin)


ANY = pl.BlockSpec(memory_space=pl.ANY)


def _place():
    x, y, c = lax.axis_index("x"), lax.axis_index("y"), lax.axis_index("c")
    other_chips = [(1 - x, y), (x, 1 - y), (1 - x, 1 - y)]
    return x, y, c, 2 * x + y, other_chips


N_PEER_CHIPS = N_CHIPS - 1


class GatherOps:
    def __init__(self, rows, shards, lands, ici_send, ici_recv, d2d_send=None, d2d_recv=None):
        self.rows, self.shards, self.lands = rows, shards, lands
        self.ici_send, self.ici_recv, self.d2d_send, self.d2d_recv = ici_send, ici_recv, d2d_send, d2d_recv
        self.x, self.y, self.c, self.me, self.chips = _place()
        self.pairs = [(t, jdx) for t in range(len(rows)) for jdx in range(N_PEER_CHIPS)]

    def _half(self, ref, t, which):
        r = self.rows[t] // 2
        return ref.at[:, pl.ds(pl.multiple_of(which * r, 16), r), :]

    def _ici(self, t, jdx):
        px, py = self.chips[jdx]
        return pltpu.make_async_remote_copy(
            src_ref=self._half(self.shards[t], t, self.c), dst_ref=self._half(self.lands[t].at[self.me], t, self.c),
            send_sem=self.ici_send.at[t, jdx], recv_sem=self.ici_recv.at[t, jdx],
            device_id=(px, py, self.c), device_id_type=MESH)

    def _landed(self, t, jdx):
        px, py = self.chips[jdx]
        blk = self._half(self.lands[t].at[2 * px + py], t, self.c)
        return pltpu.make_async_remote_copy(
            src_ref=blk, dst_ref=blk, send_sem=self.ici_send.at[t, jdx], recv_sem=self.ici_recv.at[t, jdx],
            device_id=(px, py, self.c), device_id_type=MESH)

    def _d2d(self, t, jdx, which):
        px, py = self.chips[jdx]
        blk = self._half(self.lands[t].at[2 * px + py], t, which)
        return pltpu.make_async_remote_copy(
            src_ref=blk, dst_ref=blk, send_sem=self.d2d_send.at[t, jdx], recv_sem=self.d2d_recv.at[t, jdx],
            device_id=(self.x, self.y, 1 - self.c), device_id_type=MESH)

    def start_ici(self):
        for t, jdx in self.pairs:
            self._ici(t, jdx).start()

    def wait_ici(self):
        for t, jdx in self.pairs:
            self._landed(t, jdx).wait_recv()
        self.wait_ici_sends()

    def wait_ici_sends(self):
        for t, jdx in self.pairs:
            self._ici(t, jdx).wait_send()

    def forward_arrivals(self):
        for t, jdx in self.pairs:
            self._landed(t, jdx).wait_recv()
            self._d2d(t, jdx, self.c).start()

    def start_forwards(self):
        for t, jdx in self.pairs:
            self._d2d(t, jdx, self.c).start()

    def wait_forwards(self):
        for t, jdx in self.pairs:
            self._d2d(t, jdx, 1 - self.c).wait_recv()
            self._d2d(t, jdx, self.c).wait_send()


def all_gather_weights(shards, lands):
    n = len(shards)
    rows = [s.shape[1] for s in shards]

    def body(*refs):
        ins, outs = refs[:n], refs[2 * n:3 * n]
        ops = GatherOps(rows, ins, outs, *refs[3 * n:])
        ops.start_ici()
        ops.forward_arrivals()
        ops.wait_forwards()
        ops.wait_ici_sends()

    return pl.pallas_call(
        body,
        name="all_gather_weights",
        in_specs=[ANY] * (2 * n),
        out_specs=[ANY] * n,
        out_shape=[jax.ShapeDtypeStruct(l.shape, l.dtype) for l in lands],
        input_output_aliases={n + t: t for t in range(n)},
        scratch_shapes=[pltpu.SemaphoreType.DMA((n, N_PEER_CHIPS))] * 4,
    )(*shards, *lands)


def place_own_shard(name, shard, chip):
    nl, r, c = shard.shape

    def body(chip_ref, s_ref, o_ref):
        o_ref[...] = s_ref[...]

    return pl.pallas_call(
        body, name=name,
        grid_spec=pltpu.PrefetchScalarGridSpec(
            num_scalar_prefetch=1, grid=(nl,),
            in_specs=[pl.BlockSpec((None, r, c), lambda l, chip_ref: (l, 0, 0))],
            out_specs=pl.BlockSpec((None, None, r, c), lambda l, chip_ref: (chip_ref[0], l, 0, 0))),
        out_shape=jax.ShapeDtypeStruct((N_CHIPS,) + shard.shape, shard.dtype), compiler_params=_cparams(),
    )(chip, shard)


def exchange_halves(name, slabs):
    n = len(slabs)

    def body(*refs):
        ins, theirs = refs[:n], refs[n:2 * n]
        send_sems, recv_sems = refs[2 * n:]
        x, y, c, _, _ = _place()
        copies = []
        for t in range(n):
            cp = pltpu.make_async_remote_copy(
                src_ref=ins[t].at[1 - c], dst_ref=theirs[t], send_sem=send_sems.at[t],
                recv_sem=recv_sems.at[t], device_id=(x, y, 1 - c), device_id_type=MESH)
            cp.start()
            copies.append(cp)
        for cp in copies:
            cp.wait()

    return pl.pallas_call(
        body,
        name=name,
        in_specs=[ANY] * n,
        out_specs=[ANY] * n,
        out_shape=[jax.ShapeDtypeStruct(s.shape[1:], s.dtype) for s in slabs],
        scratch_shapes=[pltpu.SemaphoreType.DMA((n,)), pltpu.SemaphoreType.DMA((n,))],
    )(*slabs)


def chip_partial_copies(ins, outs, send_sems, recv_sems):
    _, _, c, me, chips = _place()
    return [pltpu.make_async_remote_copy(
        src_ref=ins[t].at[2 * px + py], dst_ref=outs[t].at[me], send_sem=send_sems.at[t, jdx],
        recv_sem=recv_sems.at[t, jdx], device_id=(px, py, c), device_id_type=MESH)
        for t in range(len(ins)) for jdx, (px, py) in enumerate(chips)]


def exchange_chip_partials(name, parts):
    n = len(parts)

    def body(*refs):
        copies = chip_partial_copies(refs[:n], refs[n:2 * n], *refs[2 * n:])
        for cp in copies:
            cp.start()
        for cp in copies:
            cp.wait()

    return pl.pallas_call(
        body,
        name=name,
        in_specs=[ANY] * n,
        out_specs=[ANY] * n,
        out_shape=[jax.ShapeDtypeStruct(p.shape, p.dtype) for p in parts],
        scratch_shapes=[pltpu.SemaphoreType.DMA((n, 3)), pltpu.SemaphoreType.DMA((n, 3))],
    )(*parts)


def share_reduced_halves(name, halves):
    n = len(halves)

    def body(*refs):
        ins, outs = refs[:n], refs[n:2 * n]
        send_sems, recv_sems = refs[2 * n:]
        x, y, c, _, _ = _place()
        copies = []
        for t in range(n):
            cp = pltpu.make_async_remote_copy(
                src_ref=ins[t], dst_ref=outs[t], send_sem=send_sems.at[t],
                recv_sem=recv_sems.at[t], device_id=(x, y, 1 - c), device_id_type=MESH)
            cp.start()
            copies.append(cp)
        for cp in copies:
            cp.wait()

    return pl.pallas_call(
        body,
        name=name,
        in_specs=[ANY] * n,
        out_specs=[ANY] * n,
        out_shape=[jax.ShapeDtypeStruct(h.shape, h.dtype) for h in halves],
        scratch_shapes=[pltpu.SemaphoreType.DMA((n,)), pltpu.SemaphoreType.DMA((n,))],
    )(*halves)


def _row_tile(r, c):
    tr = r
    while tr * c * 4 > (3 << 19) and tr % 16 == 0:
        tr //= 2
    return tr


def add_sibling(name, slab, theirs, core):
    _, ns, slots, r, c = slab.shape
    tr = _row_tile(r, c)

    def body(core_ref, a_ref, b_ref, o_ref):
        o_ref[...] = (a_ref[...] + b_ref[...]).astype(BF16)

    blk = pl.BlockSpec((None, None, tr, c), lambda s, l, i, core_ref: (s, l, i, 0))
    return pl.pallas_call(
        body, name=name,
        grid_spec=pltpu.PrefetchScalarGridSpec(
            num_scalar_prefetch=1, grid=(ns, slots, r // tr),
            in_specs=[pl.BlockSpec((None, None, None, tr, c), lambda s, l, i, core_ref: (core_ref[0], s, l, i, 0)), blk],
            out_specs=blk),
        out_shape=jax.ShapeDtypeStruct(theirs.shape, BF16), compiler_params=_cparams(),
    )(core, slab, theirs)


def sum_chips(name, recv, own, chip):
    _, slots, r, c = recv.shape
    tr = _row_tile(r, c)

    def body(chip_ref, r0, r1, r2, r3, own_ref, o_ref):
        me = chip_ref[0]
        mine = own_ref[...]
        terms = [jnp.where(me == s, mine, rr[...]).astype(F32) for s, rr in enumerate((r0, r1, r2, r3))]
        o_ref[...] = ((terms[0] + terms[1]) + terms[2]) + terms[3]

    def src(s):
        return pl.BlockSpec((None, None, tr, c),
                            lambda l, i, chip_ref: (jnp.where(chip_ref[0] == s, (s + 1) % N_CHIPS, s), l, i, 0))

    return pl.pallas_call(
        body, name=name,
        grid_spec=pltpu.PrefetchScalarGridSpec(
            num_scalar_prefetch=1, grid=(slots, r // tr),
            in_specs=[src(0), src(1), src(2), src(3),
                      pl.BlockSpec((None, None, tr, c), lambda l, i, chip_ref: (chip_ref[0], l, i, 0))],
            out_specs=pl.BlockSpec((None, tr, c), lambda l, i, chip_ref: (l, i, 0))),
        out_shape=jax.ShapeDtypeStruct((slots, r, c), F32), compiler_params=_cparams(),
    )(chip, recv, recv, recv, recv, own)


def _adamw_math(w, g, m, v):
    m = ADAM_B1 * m + (1.0 - ADAM_B1) * g
    v = ADAM_B2 * v + (1.0 - ADAM_B2) * (g * g)
    m_hat = m / (1.0 - ADAM_B1 ** ADAM_STEP)
    v_hat = v / (1.0 - ADAM_B2 ** ADAM_STEP)
    delta = -ADAM_LR * (m_hat / (jnp.sqrt(v_hat) + ADAM_EPS) + ADAM_WD * w)
    return delta, m, v


def adamw_shard(name, w, m, v, g_pairs, core, slots, row_halves):
    n = w.shape[0]
    assert n == len(g_pairs)
    _, r, c = g_pairs[0][0].shape
    tr = _row_tile(r, c)
    nr = r // tr

    def body(core_ref, w_ref, m_ref, v_ref, *rest):
        g_refs, (go_ref, d_ref, mo_ref, vo_ref) = rest[:2 * n], rest[2 * n:]
        mine = pl.program_id(1) == core_ref[0]
        g = jnp.where(mine, g_refs[0][...], g_refs[1][...])
        for l in range(1, n):
            g = jnp.where(pl.program_id(0) == l, jnp.where(mine, g_refs[2 * l][...], g_refs[2 * l + 1][...]), g)
        delta, mm, vv = _adamw_math(w_ref[...], g, m_ref[...], v_ref[...])
        go_ref[...] = g
        d_ref[...] = delta
        mo_ref[...] = mm
        vo_ref[...] = vv

    if row_halves:
        wspec = pl.BlockSpec((None, tr, c), lambda l, h, i, core_ref: (l, h * nr + i, 0))
    else:
        wspec = pl.BlockSpec((None, tr, c), lambda l, h, i, core_ref: (l, i, h))
    def gspec(slot):
        return pl.BlockSpec((None, tr, c), lambda l, h, i, core_ref: (slot, i, 0))

    shp = jax.ShapeDtypeStruct(w.shape, F32)
    return pl.pallas_call(
        body, name=name,
        grid_spec=pltpu.PrefetchScalarGridSpec(
            num_scalar_prefetch=1, grid=(n, 2, nr),
            in_specs=[wspec, wspec, wspec] + [gspec(s) for s in slots for _ in range(2)], out_specs=[wspec] * 4),
        out_shape=[shp] * 4, compiler_params=_cparams(),
    )(core, w, m, v, *[g for pair in g_pairs for g in pair])


SMALL_ROWS = 16


def small_allreduce_adamw(part, w, m, v):
    def body(p_ref, w_ref, m_ref, v_ref, g_ref, d_ref, mo_ref, vo_ref, buf, send_sems, recv_sems):
        x, y, c, _, _ = _place()
        me = 4 * x + 2 * y + c
        buf[me] = p_ref[...]
        copies = []
        for k in range(1, N_DEV):
            kx, ky, kc = (k >> 2) & 1, (k >> 1) & 1, k & 1
            peer = (x ^ kx, y ^ ky, c ^ kc)
            cp = pltpu.make_async_remote_copy(
                src_ref=p_ref, dst_ref=buf.at[me], send_sem=send_sems.at[k - 1],
                recv_sem=recv_sems.at[k - 1], device_id=peer, device_id_type=MESH)
            cp.start()
            copies.append(cp)
        for cp in copies:
            cp.wait()
        g = buf[0]
        for dev in range(1, N_DEV):
            g = g + buf[dev]
        delta, mm, vv = _adamw_math(w_ref[...], g, m_ref[...], v_ref[...])
        g_ref[...] = g
        d_ref[...] = delta
        mo_ref[...] = mm
        vo_ref[...] = vv

    vm = pl.BlockSpec(memory_space=pltpu.VMEM)
    shp = jax.ShapeDtypeStruct(part.shape, F32)
    return pl.pallas_call(
        body, name="small_allreduce_adamw",
        in_specs=[vm] * 4, out_specs=[vm] * 4, out_shape=[shp] * 4,
        scratch_shapes=[
            pltpu.VMEM((N_DEV,) + part.shape, F32),
            pltpu.SemaphoreType.DMA((N_DEV - 1,)), pltpu.SemaphoreType.DMA((N_DEV - 1,)),
        ],
    )(part, w, m, v)


def _rope_tables(T):
    half = HEAD_DIM // 2
    inv_freq = ROPE_THETA ** (-jnp.arange(half, dtype=F32) / half)
    ang = jnp.arange(T).astype(F32)[:, None] * inv_freq[None, :]
    cos = jnp.tile(jnp.cos(ang), (1, LANES // half))
    sin = jnp.tile(jnp.sin(ang), (1, LANES // half))
    lane = jnp.arange(LANES)
    sign = jnp.where((lane % HEAD_DIM) < half, -1.0, 1.0).astype(F32)
    return cos, sin * sign[None, :]


def _pack_small(ffn1, mix, ffn2, kvn, fin, sinks, loss_row):
    sink_row = jnp.pad(sinks.reshape(1, SWA_Q_HEADS), ((0, 0), (0, D_MODEL - SWA_Q_HEADS)))
    rows = jnp.concatenate([ffn1, mix, ffn2, kvn.reshape(1, -1), fin.reshape(1, -1), sink_row, loss_row], axis=0)
    return jnp.concatenate([rows, jnp.zeros((SMALL_ROWS - rows.shape[0], D_MODEL), F32)], axis=0)


def kernel(x, ffn1_norm, ffn1_w_in, ffn1_w_out, mix_norm, ffn2_norm, ffn2_w_in, ffn2_w_out, sb_w_qkv, sb_w_o, kv_norm, kv_w, swa_w_q, swa_sinks, swa_w_o, final_norm, loss_target, m_ffn1_norm, m_ffn1_w_in, m_ffn1_w_out, m_mix_norm, m_ffn2_norm, m_ffn2_w_in, m_ffn2_w_out, m_sb_w_qkv, m_sb_w_o, m_kv_norm, m_kv_w, m_swa_w_q, m_swa_sinks, m_swa_w_o, m_final_norm, v_ffn1_norm, v_ffn1_w_in, v_ffn1_w_out, v_mix_norm, v_ffn2_norm, v_ffn2_w_in, v_ffn2_w_out, v_sb_w_qkv, v_sb_w_o, v_kv_norm, v_kv_w, v_swa_w_q, v_swa_sinks, v_swa_w_o, v_final_norm):
    T = x.shape[1]
    kv_cols = SWA_KV_HEADS * HEAD_DIM
    x2 = x.reshape(T, D_MODEL)
    tgt = loss_target.reshape(T, D_MODEL)
    cos, sin = _rope_tables(T)

    w_in_l = jnp.concatenate([ffn1_w_in, ffn2_w_in], axis=0).astype(BF16)
    w_out_l = jnp.concatenate([ffn1_w_out, ffn2_w_out], axis=0).astype(BF16)
    sq_l = jnp.concatenate([sb_w_o, swa_w_q, swa_w_o], axis=0).astype(BF16)
    qkv_l = sb_w_qkv[0].astype(BF16)
    kvw_l = kv_w.astype(BF16)
    core = lax.axis_index("c").astype(jnp.int32).reshape(1)
    chip = (2 * lax.axis_index("x") + lax.axis_index("y")).astype(jnp.int32).reshape(1)
    early = [w_in_l[:1], w_out_l[:1]]
    mid = [sq_l, qkv_l[None]]
    late = [w_in_l[1:], w_out_l[1:], kvw_l[None]]
    early_lands = [place_own_shard(f"own_early_{t}", s, chip) for t, s in enumerate(early)]
    mid_lands = [place_own_shard(f"own_mid_{t}", s, chip) for t, s in enumerate(mid)]
    late_lands = [place_own_shard(f"own_late_{t}", s, chip) for t, s in enumerate(late)]
    w_in0, w_out0 = all_gather_weights(early, early_lands)

    def ffn_w(slot):
        return (w_in0, w_out0, 0) if slot == 0 else (w_in_r, w_out_r, slot - 1)

    def vec(a, i):
        return a[i].reshape(1, D_MODEL)

    ident = lambda w: w
    sq_prep = lambda w: w.reshape(D_MODEL, w.shape[-1])
    qscale = jnp.concatenate([jnp.full((1, D_MODEL), ATTN_SCALE, F32), jnp.ones((1, 2 * D_MODEL), F32)], axis=1)
    swa_scale = jnp.full((1, D_MODEL), ATTN_SCALE, F32)
    sinks = swa_sinks.reshape(SWA_Q_HEADS)

    h1, gate1, up1, (w_sq, w_qkv) = ffn_fwd("l0a", x2, vec(ffn1_norm, 0), *ffn_w(SLOT_FFN1[0]), mid, mid_lands)
    w_qkv = w_qkv.reshape(N_CHIPS, D_MODEL, QKV_COLS)
    qkv = qkv_fwd(h1, vec(mix_norm, 0), w_qkv, qscale)
    o_sb, tot, sb_first, late_lands = sb_fwd(qkv, late, late_lands)
    h2, (w_in_r, w_out_r, w_kv) = linear_res("sb_out", o_sb, w_sq, SQ_SB_O, h1, late_lands)
    w_kv = w_kv.reshape(D_MODEL, 2 * kv_cols)
    h3, gate2, up2 = ffn_fwd("l0b", h2, vec(ffn2_norm, 0), *ffn_w(SLOT_FFN2[0]))
    kvn = kv_norm.reshape(1, D_MODEL)
    k_sw = rms_linear("kv_k", h3, kvn, w_kv, pl.BlockSpec((D_MODEL, kv_cols), lambda i, j: (0, 0)), ident,
                      kv_cols, kv_cols, rope=(cos, sin))
    v_sw = rms_linear("kv_v", h3, kvn, w_kv, pl.BlockSpec((D_MODEL, kv_cols), lambda i, j: (0, 1)), ident,
                      kv_cols, kv_cols)
    h4, gate3, up3 = ffn_fwd("l1a", h3, vec(ffn1_norm, 1), *ffn_w(SLOT_FFN1[1]))
    q_sw = rms_linear("swa_q", h4, vec(mix_norm, 1), w_sq,
                      pl.BlockSpec((N_CHIPS, None, SQ_ROWS, 512), lambda i, j: (0, SQ_SWA_Q, 0, j)), sq_prep,
                      D_MODEL, 512, rope=(cos, sin), scale=swa_scale)
    o_sw, lse = swa_fwd(q_sw, k_sw, v_sw, sinks)
    h5 = linear_res("swa_out", o_sw, w_sq, SQ_SWA_O, h4)
    h6, gate4, up4 = ffn_fwd("l1b", h5, vec(ffn2_norm, 1), *ffn_w(SLOT_FFN2[1]))
    dh6, loss_p, d_final = loss_bwd(h6, final_norm.reshape(1, D_MODEL), tgt)

    slab = {}
    ffn_place = {SLOT_FFN1[0]: (0, 0, 1), SLOT_FFN1[1]: (1, 0, 3), SLOT_FFN2[0]: (1, 1, 3), SLOT_FFN2[1]: (1, 2, 3)}
    sq_place = {SQ_SB_O: (1, 0, 3), SQ_SWA_Q: (1, 1, 3), SQ_SWA_O: (1, 2, 3)}

    def ffn_grads(tag, dh, h_in, g, gate, up, slot):
        dh_in, xn, dg_, du_, act, dhb, dnorm = ffn_bwd(tag, dh, h_in, g, gate, up, *ffn_w(slot))
        grp, s, ns = ffn_place[slot]
        in_shape = (2, N_CHIPS, ns, D_MODEL // 2, FF_CHUNK)
        out_shape = (2, N_CHIPS, ns, FF_ROWS, D_MODEL // 2)
        blk = (None, 1, None, D_MODEL // 2, FF_CHUNK)
        slab["in", grp] = mm_tn(f"dw_gate_{tag}", xn, dg_, D_MODEL // 2, FF_CHUNK, blk,
                                lambda k, n: (k, n, s, 0, 0), in_shape, prev=slab.get(("in", grp)))
        slab["in", grp] = mm_tn(f"dw_up_{tag}", xn, du_, D_MODEL // 2, FF_CHUNK, blk,
                                lambda k, n: (k, 2 + n, s, 0, 0), in_shape, prev=slab["in", grp])
        slab["out", grp] = mm_tn(f"dw_out_{tag}", act, dhb, FF_CHUNK, D_MODEL // 2,
                                 (None, 2, None, FF_ROWS, D_MODEL // 2),
                                 lambda k, n: (n, k, s, 0, 0), out_shape, prev=slab.get(("out", grp)))
        return dh_in, dnorm

    def sq_grad(tag, a, dyb, t):
        grp, s, ns = sq_place[t]
        slab["sq", grp] = mm_tn(f"dw_sq_{tag}", a, dyb, D_MODEL, D_MODEL // 2,
                                (None, N_CHIPS, None, SQ_ROWS, D_MODEL // 2),
                                lambda k, n: (n, 0, s, 0, 0), (2, N_CHIPS, ns, SQ_ROWS, D_MODEL // 2),
                                prev=slab.get(("sq", grp)))

    def reduce_group(grp, kinds, host=None):
        slabs = [slab[kind, grp][0] for kind in kinds]
        names = [f"{kind}{grp}" for kind in kinds]
        theirs = exchange_halves(f"exchange_halves_{grp}", [slab[kind, grp][1] for kind in kinds])
        parts = [add_sibling(f"add_sibling_{nm}", s, t, core) for nm, s, t in zip(names, slabs, theirs)]
        arrived = host(parts) if host else exchange_chip_partials(f"exchange_chip_partials_{grp}", parts)
        halves = [sum_chips(f"sum_chips_{nm}", g, p, chip) for nm, g, p in zip(names, arrived, parts)]
        sib_halves = share_reduced_halves(f"share_reduced_halves_{grp}", halves)
        return {kind: pair for kind, pair in zip(kinds, zip(halves, sib_halves))}

    dh5, d_ffn2_1 = ffn_grads("l1b", dh6, h5, vec(ffn2_norm, 1), gate4, up4, SLOT_FFN2[1])
    do_sw, dh5b = linear_bwd_plain("swa_out_bwd", dh5, w_sq, SQ_SWA_O)
    sq_grad("swa_o", o_sw, dh5b, SQ_SWA_O)
    dq_sw, kv_own, kv_prev, d_sinks = swa_bwd(q_sw, k_sw, v_sw, sinks, do_sw, o_sw, lse, cos, sin)
    sq_w_spec = pl.BlockSpec((N_CHIPS, None, SQ_ROWS, D_MODEL), lambda i, j: (0, SQ_SWA_Q, 0, 0))
    dh4, hn4, d_mix_1 = linear_bwd_rms("swa_q_bwd", [(dq_sw, w_sq, sq_w_spec, sq_prep)], h4, vec(mix_norm, 1), dh5,
                                       1, D_MODEL)
    sq_grad("swa_q", hn4, dq_sw, SQ_SWA_Q)
    dh3a, d_ffn1_1 = ffn_grads("l1a", dh4, h3, vec(ffn1_norm, 1), gate3, up3, SLOT_FFN1[1])
    dkv = kv_grad_combine(kv_own, kv_prev, cos, sin)
    kv_w_spec = pl.BlockSpec((D_MODEL, 2 * kv_cols), lambda i, j: (0, 0))
    dh3, xn3, d_kvn = linear_bwd_rms("kv_bwd", [(dkv, w_kv, kv_w_spec, ident)], h3, kvn, dh3a, 1, 2 * kv_cols)
    slab["kv", 1] = mm_tn("dw_kv", xn3, dkv, D_MODEL, kv_cols, (None, N_CHIPS, None, SQ_ROWS, kv_cols),
                          lambda k, n: (n, 0, 0, 0, 0), (2, N_CHIPS, 1, SQ_ROWS, kv_cols))
    dh2, d_ffn2_0 = ffn_grads("l0b", dh3, h2, vec(ffn2_norm, 0), gate2, up2, SLOT_FFN2[0])
    do_sb, dh2b = linear_bwd_plain("sb_out_bwd", dh2, w_sq, SQ_SB_O)
    sq_grad("sb_o", o_sb, dh2b, SQ_SB_O)
    sb_grads = []

    def behind_sb_bwd(parts):
        dq_sb, dk_sb, dv_sb, arrived = sb_bwd(qkv, do_sb, tot, sb_first, parts)
        sb_grads.extend([dq_sb, dk_sb, dv_sb])
        return arrived

    red = {1: reduce_group(1, ["in", "out", "sq", "kv"], host=behind_sb_bwd)}
    dqkv = jnp.concatenate(sb_grads, axis=1)
    dh1, hn1, d_mix_0 = qkv_bwd(dqkv, w_qkv, h1, vec(mix_norm, 0), dh2)
    slab["qkv", 0] = mm_tn("dw_qkv", hn1, dqkv, D_MODEL // 2, QKV_COLS, (None, 1, None, D_MODEL // 2, QKV_COLS),
                           lambda k, n: (k, n, 0, 0, 0), (2, N_CHIPS, 1, D_MODEL // 2, QKV_COLS))
    dx, d_ffn1_0 = ffn_grads("l0a", dh1, x2, vec(ffn1_norm, 0), gate1, up1, SLOT_FFN1[0])
    red[0] = reduce_group(0, ["in", "out", "qkv"])

    def upd(name, w, m, v, kind, places, row_halves):
        shp = w.shape
        w3 = w.reshape((-1,) + shp[-2:])
        outs = adamw_shard(name, w3, m.reshape(w3.shape), v.reshape(w3.shape),
                           [red[grp][kind] for grp, _ in places], core, [s for _, s in places], row_halves)
        return [o.reshape(shp) for o in outs]

    ffn1_places = [ffn_place[s][:2] for s in SLOT_FFN1]
    ffn2_places = [ffn_place[s][:2] for s in SLOT_FFN2]
    r_ffn1_in = upd("adamw_ffn1_in", ffn1_w_in, m_ffn1_w_in, v_ffn1_w_in, "in", ffn1_places, True)
    r_ffn2_in = upd("adamw_ffn2_in", ffn2_w_in, m_ffn2_w_in, v_ffn2_w_in, "in", ffn2_places, True)
    r_ffn1_out = upd("adamw_ffn1_out", ffn1_w_out, m_ffn1_w_out, v_ffn1_w_out, "out", ffn1_places, False)
    r_ffn2_out = upd("adamw_ffn2_out", ffn2_w_out, m_ffn2_w_out, v_ffn2_w_out, "out", ffn2_places, False)
    r_qkv = upd("adamw_qkv", sb_w_qkv, m_sb_w_qkv, v_sb_w_qkv, "qkv", [(0, 0)], True)
    r_sb_o = upd("adamw_sb_o", sb_w_o, m_sb_w_o, v_sb_w_o, "sq", [sq_place[SQ_SB_O][:2]], False)
    r_swa_q = upd("adamw_swa_q", swa_w_q, m_swa_w_q, v_swa_w_q, "sq", [sq_place[SQ_SWA_Q][:2]], False)
    r_swa_o = upd("adamw_swa_o", swa_w_o, m_swa_w_o, v_swa_w_o, "sq", [sq_place[SQ_SWA_O][:2]], False)
    r_kv = upd("adamw_kv", kv_w, m_kv_w, v_kv_w, "kv", [(1, 0)], False)

    loss_row = jnp.pad(loss_p, ((0, 0), (0, D_MODEL - LANES)))
    d_sink_row = d_sinks[0, :SWA_Q_HEADS]
    part = _pack_small(jnp.concatenate([d_ffn1_0, d_ffn1_1], axis=0), jnp.concatenate([d_mix_0, d_mix_1], axis=0),
                       jnp.concatenate([d_ffn2_0, d_ffn2_1], axis=0), d_kvn, d_final, d_sink_row, loss_row)
    zrow = jnp.zeros((1, D_MODEL), F32)
    small = small_allreduce_adamw(
        part,
        _pack_small(ffn1_norm, mix_norm, ffn2_norm, kv_norm, final_norm, swa_sinks, zrow),
        _pack_small(m_ffn1_norm, m_mix_norm, m_ffn2_norm, m_kv_norm, m_final_norm, m_swa_sinks, zrow),
        _pack_small(v_ffn1_norm, v_mix_norm, v_ffn2_norm, v_kv_norm, v_final_norm, v_swa_sinks, zrow))

    def unpack(p):
        return dict(ffn1_norm=p[0:2], mix_norm=p[2:4], ffn2_norm=p[4:6], kv_norm=p[6], final_norm=p[7],
                    swa_sinks=p[8:9, :SWA_Q_HEADS])

    big = dict(ffn1_w_in=r_ffn1_in, ffn1_w_out=r_ffn1_out, ffn2_w_in=r_ffn2_in, ffn2_w_out=r_ffn2_out,
               sb_w_qkv=r_qkv, sb_w_o=r_sb_o, kv_w=r_kv, swa_w_q=r_swa_q, swa_w_o=r_swa_o)
    order = ["ffn1_norm", "ffn1_w_in", "ffn1_w_out", "mix_norm", "ffn2_norm", "ffn2_w_in", "ffn2_w_out",
             "sb_w_qkv", "sb_w_o", "kv_norm", "kv_w", "swa_w_q", "swa_sinks", "swa_w_o", "final_norm"]
    outs = []
    for kind in range(4):
        sm = unpack(small[kind])
        for nm in order:
            outs.append(big[nm][kind] if nm in big else sm[nm])
    loss = small[0][9, 0]
    return (loss, dx.reshape(x.shape), *outs)
```

```python
import jax
import jax.numpy as jnp
from jax import lax
from jax.experimental import pallas as pl
from jax.experimental.pallas import tpu as pltpu

F32 = jnp.float32
BF16 = jnp.bfloat16
MESH = pl.DeviceIdType.MESH

D_MODEL = 1024
D_FF = 2816
HEAD_DIM = 64
SB_HEADS = 16
SWA_Q_HEADS = 16
SWA_KV_HEADS = 4
WINDOW = 128
ROPE_THETA = 10000.0
RMS_EPS = 1e-6
FFN_RES_SCALE = 0.5
ATTN_SCALE = HEAD_DIM ** -0.5

ADAM_LR = 0.001
ADAM_B1 = 0.9
ADAM_B2 = 0.999
ADAM_EPS = 1e-08
ADAM_WD = 0.01
ADAM_STEP = 10

N_CHIPS = 4
N_DEV = 8
LANES = 128
FF_CHUNK = D_FF // 2
FF_ROWS = D_FF // N_CHIPS
SQ_ROWS = D_MODEL // N_CHIPS
QKV_COLS = 3 * D_MODEL // N_CHIPS
VMEM_LIMIT = 56 * 1024 * 1024
NEG_BIG = -1e30

SLOT_FFN1 = (0, 1)
SLOT_FFN2 = (2, 3)
SQ_SB_O, SQ_SWA_Q, SQ_SWA_O = 0, 1, 2


def _cparams():
    return pltpu.CompilerParams(vmem_limit_bytes=VMEM_LIMIT)


def _dot(a, b):
    return jnp.dot(a, b, preferred_element_type=F32)


def _dot_nt(a, b):
    return lax.dot_general(a, b, (((1,), (1,)), ((), ())), preferred_element_type=F32)


def _dot_tn(a, b):
    return lax.dot_general(a, b, (((0,), (0,)), ((), ())), preferred_element_type=F32)


def _rstd(h):
    return lax.rsqrt(jnp.mean(h * h, axis=-1, keepdims=True) + RMS_EPS)


def _swap32(x):
    n = x.shape[-1]
    lane = lax.broadcasted_iota(jnp.int32, x.shape, x.ndim - 1)
    first = (lane % HEAD_DIM) < (HEAD_DIM // 2)
    return jnp.where(first, pltpu.roll(x, n - HEAD_DIM // 2, x.ndim - 1), pltpu.roll(x, HEAD_DIM // 2, x.ndim - 1))


def _tile_lanes(t, n):
    return t if n == LANES else jnp.tile(t, (1, n // LANES))


FFN_ROWS = 256


def _ffn_w_in_spec(slot):
    return pl.BlockSpec((N_CHIPS, None, D_MODEL, FF_CHUNK), lambda i: (0, slot, 0, 0), pipeline_mode=pl.Buffered(1))


def _ffn_w_out_spec(slot):
    return pl.BlockSpec((N_CHIPS, None, FF_ROWS, D_MODEL), lambda i: (0, slot, 0, 0), pipeline_mode=pl.Buffered(1))


def ffn_fwd(tag, h, g, w_in, w_out, slot, bg_shards=(), bg_lands=()):
    T = h.shape[0]
    tm = FFN_ROWS
    nch = D_FF // FF_CHUNK
    nbg = len(bg_shards)
    nt = T // tm

    def body(h_ref, g_ref, wi_ref, wo_ref, *rest):
        out_ref, gate_ref, up_ref = rest[2 * nbg:2 * nbg + 3]
        wg_s, wu_s = rest[3 * nbg + 3:3 * nbg + 5]
        step = pl.program_id(0)
        if nbg:
            gather = GatherOps([s.shape[1] for s in bg_shards], rest[:nbg], rest[2 * nbg + 3:3 * nbg + 3],
                               *rest[3 * nbg + 5:])
            pl.when(step == 0)(gather.start_ici)
            pl.when(step == nt // 2)(gather.forward_arrivals)

            @pl.when(step == nt - 1)
            def _():
                gather.wait_forwards()
                gather.wait_ici_sends()

        @pl.when(step == 0)
        def _():
            for j in range(nch):
                cols = slice(j * FF_CHUNK, (j + 1) * FF_CHUNK)
                wg_s[:, cols] = wi_ref[j]
                wu_s[:, cols] = wi_ref[nch + j]

        hh = h_ref[...]
        xn = (hh * _rstd(hh) * g_ref[...]).astype(BF16)
        gate = _dot(xn, wg_s[...])
        up = _dot(xn, wu_s[...])
        gate_ref[...] = gate.astype(BF16)
        up_ref[...] = up.astype(BF16)
        a = (gate * jax.nn.sigmoid(gate) * up).astype(BF16)
        out_ref[...] = hh + FFN_RES_SCALE * _dot(a, wo_ref[...].reshape(D_FF, D_MODEL))

    row = pl.BlockSpec((tm, D_MODEL), lambda i: (i, 0))
    ff = pl.BlockSpec((tm, D_FF), lambda i: (i, 0))
    res = pl.pallas_call(
        body,
        name=f"ffn_fwd_{tag}",
        grid=(nt,),
        in_specs=[row, pl.BlockSpec((1, D_MODEL), lambda i: (0, 0)), _ffn_w_in_spec(slot), _ffn_w_out_spec(slot)]
        + [ANY] * (2 * nbg),
        out_specs=[row, ff, ff] + [ANY] * nbg,
        out_shape=[
            jax.ShapeDtypeStruct((T, D_MODEL), F32),
            jax.ShapeDtypeStruct((T, D_FF), BF16),
            jax.ShapeDtypeStruct((T, D_FF), BF16),
        ] + [jax.ShapeDtypeStruct(l.shape, l.dtype) for l in bg_lands],
        input_output_aliases={4 + nbg + t: 3 + t for t in range(nbg)},
        scratch_shapes=[pltpu.VMEM((D_MODEL, D_FF), BF16), pltpu.VMEM((D_MODEL, D_FF), BF16)]
        + [pltpu.SemaphoreType.DMA((nbg, N_PEER_CHIPS))] * (4 if nbg else 0),
        compiler_params=_cparams(),
    )(h, g, w_in, w_out, *bg_shards, *bg_lands)
    return (res[0], res[1], res[2], list(res[3:])) if nbg else tuple(res)


def ffn_bwd(tag, dh, h, g, gate, up, w_in, w_out, slot):
    T = dh.shape[0]
    tm = FFN_ROWS
    nch = D_FF // FF_CHUNK

    def body(dh_ref, h_ref, g_ref, gate_ref, up_ref, wi_ref, wo_ref,
             dhin_ref, xn_ref, dg_ref, du_ref, a_ref, dhb_ref, dnorm_ref):
        @pl.when(pl.program_id(0) == 0)
        def _():
            dnorm_ref[...] = jnp.zeros_like(dnorm_ref)

        dhh = dh_ref[...]
        dhb = (FFN_RES_SCALE * dhh).astype(BF16)
        dhb_ref[...] = dhb
        dxn = None
        for j in range(nch):
            cols = slice(j * FF_CHUNK, (j + 1) * FF_CHUNK)
            da = _dot_nt(dhb, wo_ref[2 * j:2 * j + 2].reshape(FF_CHUNK, D_MODEL))
            gt = gate_ref[:, cols].astype(F32)
            u = up_ref[:, cols].astype(F32)
            s = jax.nn.sigmoid(gt)
            silu = gt * s
            a_ref[:, cols] = (silu * u).astype(BF16)
            dgate = (da * u * (s * (1.0 + gt * (1.0 - s)))).astype(BF16)
            dup = (da * silu).astype(BF16)
            dg_ref[:, cols] = dgate
            du_ref[:, cols] = dup
            part = _dot_nt(dgate, wi_ref[j]) + _dot_nt(dup, wi_ref[nch + j])
            dxn = part if dxn is None else dxn + part
        hh = h_ref[...]
        gg = g_ref[...]
        r = _rstd(hh)
        hr = hh * r
        xn_ref[...] = (hr * gg).astype(BF16)
        dnorm_ref[...] += jnp.sum(dxn * hr, axis=0, keepdims=True)
        gd = gg * dxn
        dhin_ref[...] = dhh + r * (gd - hr * jnp.mean(gd * hr, axis=-1, keepdims=True))

    row = pl.BlockSpec((tm, D_MODEL), lambda i: (i, 0))
    ff = pl.BlockSpec((tm, D_FF), lambda i: (i, 0))
    vec = pl.BlockSpec((1, D_MODEL), lambda i: (0, 0))
    return pl.pallas_call(
        body,
        name=f"ffn_bwd_{tag}",
        grid=(T // tm,),
        in_specs=[row, row, vec, ff, ff, _ffn_w_in_spec(slot), _ffn_w_out_spec(slot)],
        out_specs=[row, row, ff, ff, ff, row, vec],
        out_shape=[
            jax.ShapeDtypeStruct((T, D_MODEL), F32),
            jax.ShapeDtypeStruct((T, D_MODEL), BF16),
            jax.ShapeDtypeStruct((T, D_FF), BF16),
            jax.ShapeDtypeStruct((T, D_FF), BF16),
            jax.ShapeDtypeStruct((T, D_FF), BF16),
            jax.ShapeDtypeStruct((T, D_MODEL), BF16),
            jax.ShapeDtypeStruct((1, D_MODEL), F32),
        ],
        compiler_params=_cparams(),
    )(dh, h, g, gate, up, w_in, w_out)


def rms_linear(name, h, g, w, w_spec, w_prep, n_out, tn, *, rope=None, rope_blocks=None, scale=None):
    T = h.shape[0]
    tm = 512
    extra, extra_specs = [], []
    if rope is not None:
        extra += list(rope)
        extra_specs += [pl.BlockSpec((tm, LANES), lambda i, j: (i, 0))] * 2
    if scale is not None:
        extra.append(scale)
        extra_specs.append(pl.BlockSpec((1, tn), lambda i, j: (0, j)))

    def body(h_ref, g_ref, w_ref, *rest):
        rest = list(rest)
        cos_ref = sin_ref = sc_ref = None
        if rope is not None:
            cos_ref, sin_ref = rest[0], rest[1]
            rest = rest[2:]
        if scale is not None:
            sc_ref = rest[0]
            rest = rest[1:]
        out_ref, xn_s = rest

        @pl.when(pl.program_id(1) == 0)
        def _():
            hh = h_ref[...]
            xn_s[...] = (hh * _rstd(hh) * g_ref[...]).astype(BF16)

        y = _dot(xn_s[...], w_prep(w_ref[...]))
        if rope is not None:
            turned = y * _tile_lanes(cos_ref[...], tn) + _swap32(y) * _tile_lanes(sin_ref[...], tn)
            y = turned if rope_blocks is None else jnp.where(pl.program_id(1) < rope_blocks, turned, y)
        if scale is not None:
            y = y * sc_ref[...]
        out_ref[...] = y.astype(BF16)

    return pl.pallas_call(
        body,
        name=name,
        grid=(T // tm, n_out // tn),
        in_specs=[
            pl.BlockSpec((tm, D_MODEL), lambda i, j: (i, 0)),
            pl.BlockSpec((1, D_MODEL), lambda i, j: (0, 0)),
            w_spec,
        ] + extra_specs,
        out_specs=pl.BlockSpec((tm, tn), lambda i, j: (i, j)),
        out_shape=jax.ShapeDtypeStruct((T, n_out), BF16),
        scratch_shapes=[pltpu.VMEM((tm, D_MODEL), BF16)],
        compiler_params=_cparams(),
    )(h, g, w, *extra)


QKV_ROWS = 512


def _qkv_w_spec():
    return pl.BlockSpec((N_CHIPS, D_MODEL, QKV_COLS), lambda i: (0, 0, 0), pipeline_mode=pl.Buffered(1))


def qkv_fwd(h, g, w_qkv, scale):
    T = h.shape[0]
    tm = QKV_ROWS

    def body(h_ref, g_ref, w_ref, sc_ref, out_ref):
        hh = h_ref[...]
        xn = (hh * _rstd(hh) * g_ref[...]).astype(BF16)
        for s in range(N_CHIPS):
            cols = slice(s * QKV_COLS, (s + 1) * QKV_COLS)
            out_ref[:, cols] = (_dot(xn, w_ref[s]) * sc_ref[:, cols]).astype(BF16)

    return pl.pallas_call(
        body,
        name="sb_qkv",
        grid=(T // tm,),
        in_specs=[
            pl.BlockSpec((tm, D_MODEL), lambda i: (i, 0)),
            pl.BlockSpec((1, D_MODEL), lambda i: (0, 0)),
            _qkv_w_spec(),
            pl.BlockSpec((1, 3 * D_MODEL), lambda i: (0, 0)),
        ],
        out_specs=pl.BlockSpec((tm, 3 * D_MODEL), lambda i: (i, 0)),
        out_shape=jax.ShapeDtypeStruct((T, 3 * D_MODEL), BF16),
        compiler_params=_cparams(),
    )(h, g, w_qkv, scale)


def qkv_bwd(dy, w_qkv, h, g, dres):
    T = h.shape[0]
    tm = QKV_ROWS

    def body(dy_ref, w_ref, h_ref, g_ref, dres_ref, dh_ref, xn_ref, dg_ref):
        @pl.when(pl.program_id(0) == 0)
        def _():
            dg_ref[...] = jnp.zeros_like(dg_ref)

        dxn = None
        for s in range(N_CHIPS):
            part = _dot_nt(dy_ref[:, s * QKV_COLS:(s + 1) * QKV_COLS], w_ref[s])
            dxn = part if dxn is None else dxn + part
        hh = h_ref[...]
        gg = g_ref[...]
        r = _rstd(hh)
        hr = hh * r
        xn_ref[...] = (hr * gg).astype(BF16)
        dg_ref[...] += jnp.sum(dxn * hr, axis=0, keepdims=True)
        gd = gg * dxn
        dh_ref[...] = dres_ref[...] + r * (gd - hr * jnp.mean(gd * hr, axis=-1, keepdims=True))

    row = pl.BlockSpec((tm, D_MODEL), lambda i: (i, 0))
    vec = pl.BlockSpec((1, D_MODEL), lambda i: (0, 0))
    return pl.pallas_call(
        body,
        name="sb_qkv_bwd",
        grid=(T // tm,),
        in_specs=[pl.BlockSpec((tm, 3 * D_MODEL), lambda i: (i, 0)), _qkv_w_spec(), row, vec, row],
        out_specs=[row, row, vec],
        out_shape=[
            jax.ShapeDtypeStruct((T, D_MODEL), F32),
            jax.ShapeDtypeStruct((T, D_MODEL), BF16),
            jax.ShapeDtypeStruct((1, D_MODEL), F32),
        ],
        compiler_params=_cparams(),
    )(dy, w_qkv, h, g, dres)


def linear_res(name, a, w_sq, t, res, bg_lands=()):
    T = a.shape[0]
    tm = 512
    nbg = len(bg_lands)
    nt = T // tm

    def body(a_ref, w_ref, res_ref, *rest):
        out_ref = rest[nbg]
        if nbg:
            gather = GatherOps([l.shape[2] for l in bg_lands], None, rest[nbg + 1:2 * nbg + 1], None, None,
                               *rest[2 * nbg + 1:])

            @pl.when(pl.program_id(0) == 0)
            def _():
                gather.start_forwards()

        out_ref[...] = res_ref[...] + _dot(a_ref[...], w_ref[...].reshape(D_MODEL, D_MODEL))
        if nbg:
            @pl.when(pl.program_id(0) == nt - 1)
            def _():
                gather.wait_forwards()

    row = pl.BlockSpec((tm, D_MODEL), lambda i: (i, 0))
    res_ = pl.pallas_call(
        body,
        name=name,
        grid=(nt,),
        in_specs=[row, pl.BlockSpec((N_CHIPS, None, SQ_ROWS, D_MODEL), lambda i: (0, t, 0, 0)), row] + [ANY] * nbg,
        out_specs=[row] + [ANY] * nbg,
        out_shape=[jax.ShapeDtypeStruct((T, D_MODEL), F32)] + [jax.ShapeDtypeStruct(l.shape, l.dtype) for l in bg_lands],
        input_output_aliases={3 + k: 1 + k for k in range(nbg)},
        scratch_shapes=[pltpu.SemaphoreType.DMA((nbg, N_PEER_CHIPS))] * (2 if nbg else 0),
        compiler_params=_cparams(),
    )(a, w_sq, res, *bg_lands)
    return (res_[0], list(res_[1:])) if nbg else res_[0]


def linear_bwd_plain(name, dy, w_sq, t):
    T = dy.shape[0]
    tm = 512

    def body(dy_ref, w_ref, da_ref, dyb_ref):
        dyb = dy_ref[...].astype(BF16)
        dyb_ref[...] = dyb
        da_ref[...] = _dot_nt(dyb, w_ref[...].reshape(D_MODEL, D_MODEL)).astype(BF16)

    row = pl.BlockSpec((tm, D_MODEL), lambda i: (i, 0))
    return pl.pallas_call(
        body,
        name=name,
        grid=(T // tm,),
        in_specs=[row, pl.BlockSpec((N_CHIPS, None, SQ_ROWS, D_MODEL), lambda i: (0, t, 0, 0))],
        out_specs=[row, row],
        out_shape=[jax.ShapeDtypeStruct((T, D_MODEL), BF16), jax.ShapeDtypeStruct((T, D_MODEL), BF16)],
        compiler_params=_cparams(),
    )(dy, w_sq)


def linear_bwd_rms(name, pairs, h, g, dres, nch, tn, tm=256):
    T = h.shape[0]
    npair = len(pairs)

    def body(*refs):
        dy_refs = refs[:npair]
        w_refs = refs[npair:2 * npair]
        h_ref, g_ref, dres_ref, dh_ref, xn_ref, dg_ref, acc_s = refs[2 * npair:]
        i = pl.program_id(0)
        j = pl.program_id(1)

        @pl.when(j == 0)
        def _():
            acc_s[...] = jnp.zeros_like(acc_s)

        @pl.when((i == 0) & (j == 0))
        def _():
            dg_ref[...] = jnp.zeros_like(dg_ref)

        part = None
        for p in range(npair):
            d = _dot_nt(dy_refs[p][...], pairs[p][3](w_refs[p][...]))
            part = d if part is None else part + d
        acc_s[...] += part

        @pl.when(j == nch - 1)
        def _():
            dxn = acc_s[...]
            hh = h_ref[...]
            gg = g_ref[...]
            r = _rstd(hh)
            hr = hh * r
            xn_ref[...] = (hr * gg).astype(BF16)
            dg_ref[...] += jnp.sum(dxn * hr, axis=0, keepdims=True)
            gd = gg * dxn
            dh_ref[...] = dres_ref[...] + r * (gd - hr * jnp.mean(gd * hr, axis=-1, keepdims=True))

    row = pl.BlockSpec((tm, D_MODEL), lambda i, j: (i, 0))
    vec = pl.BlockSpec((1, D_MODEL), lambda i, j: (0, 0))
    return pl.pallas_call(
        body,
        name=name,
        grid=(T // tm, nch),
        in_specs=[pl.BlockSpec((tm, tn), lambda i, j: (i, j))] * npair + [p[2] for p in pairs] + [row, vec, row],
        out_specs=[row, row, vec],
        out_shape=[
            jax.ShapeDtypeStruct((T, D_MODEL), F32),
            jax.ShapeDtypeStruct((T, D_MODEL), BF16),
            jax.ShapeDtypeStruct((1, D_MODEL), F32),
        ],
        scratch_shapes=[pltpu.VMEM((tm, D_MODEL), F32)],
        compiler_params=_cparams(),
    )(*[p[0] for p in pairs], *[p[1] for p in pairs], h, g, dres)


def loss_bwd(h, g, tgt):
    T = h.shape[0]
    tm = 512

    def body(h_ref, g_ref, t_ref, dh_ref, loss_ref, dg_ref):
        @pl.when(pl.program_id(0) == 0)
        def _():
            loss_ref[...] = jnp.zeros_like(loss_ref)
            dg_ref[...] = jnp.zeros_like(dg_ref)

        hh = h_ref[...]
        gg = g_ref[...]
        r = _rstd(hh)
        hr = hh * r
        err = hr * gg - t_ref[...]
        loss_ref[...] += 0.5 * jnp.sum(jnp.mean(err * err, axis=-1, keepdims=True), axis=0, keepdims=True)
        dy = err * (1.0 / D_MODEL)
        dg_ref[...] += jnp.sum(dy * hr, axis=0, keepdims=True)
        gd = gg * dy
        dh_ref[...] = r * (gd - hr * jnp.mean(gd * hr, axis=-1, keepdims=True))

    row = pl.BlockSpec((tm, D_MODEL), lambda i: (i, 0))
    vec = pl.BlockSpec((1, D_MODEL), lambda i: (0, 0))
    return pl.pallas_call(
        body,
        name="loss_bwd",
        grid=(T // tm,),
        in_specs=[row, vec, row],
        out_specs=[row, pl.BlockSpec((1, LANES), lambda i: (0, 0)), vec],
        out_shape=[
            jax.ShapeDtypeStruct((T, D_MODEL), F32),
            jax.ShapeDtypeStruct((1, LANES), F32),
            jax.ShapeDtypeStruct((1, D_MODEL), F32),
        ],
        compiler_params=_cparams(),
    )(h, g, tgt)


DW_TOKENS = 4096


def mm_tn(name, a, b, tk, tn, out_block, out_index, out_shape, prev=None, tt=DW_TOKENS):
    T = a.shape[0]
    ns, r = out_block[1], out_block[3]
    tt = min(tt, T)
    nt = T // tt

    def body(*refs):
        a_ref, b_ref = refs[:2]
        out_ref, copy_ref = refs[-2:]
        t = pl.program_id(2)
        res = _dot_tn(a_ref[...], b_ref[...])

        @pl.when(t == 0)
        def _():
            for u in range(ns):
                out_ref[u] = res[u * r:(u + 1) * r]

        @pl.when(t > 0)
        def _():
            for u in range(ns):
                out_ref[u] += res[u * r:(u + 1) * r]

        @pl.when(t == nt - 1)
        def _():
            copy_ref[...] = out_ref[...].astype(BF16)

    in_specs = [
        pl.BlockSpec((tt, tk), lambda k, n, t: (t, k)),
        pl.BlockSpec((tt, tn), lambda k, n, t: (t, n)),
    ]
    args = [a, b]
    aliases = {}
    if prev is not None:
        in_specs += [pl.BlockSpec(memory_space=pl.ANY)] * 2
        args += list(prev)
        aliases = {2: 0, 3: 1}
    out_spec = pl.BlockSpec(out_block, lambda k, n, t: out_index(k, n))
    return tuple(pl.pallas_call(
        body,
        name=name,
        grid=(a.shape[1] // tk, b.shape[1] // tn, nt),
        in_specs=in_specs,
        out_specs=[out_spec, out_spec],
        out_shape=[jax.ShapeDtypeStruct(out_shape, F32), jax.ShapeDtypeStruct(out_shape, BF16)],
        input_output_aliases=aliases,
        compiler_params=_cparams(),
    )(*args))


SB_BLOCK = 256
SB_QROWS = 256
SB_QROWS_BWD = 256
SB_UNDERFLOW_BITS = 140.0
SB_CHUNK = 128


LOG2E = 1.4426950408889634


def _softplus2(z2):
    sign = jnp.uint32(0x80000000)
    neg_abs = lax.bitcast_convert_type(lax.bitcast_convert_type(z2, jnp.uint32) | sign, F32)
    return jnp.log2(1.0 + jnp.exp2(neg_abs)) + jnp.maximum(z2, 0.0)


def _twice(x):
    return jnp.concatenate([x, x], axis=1)


def sb_fwd(qkv, bg_shards=(), bg_lands=()):
    T = qkv.shape[0]
    tq, tk = SB_QROWS, SB_BLOCK
    ratio = tq // tk
    npair = SB_HEADS // 2
    nbg = len(bg_shards)
    nq = T // tq

    def body(q_ref, k_ref, v_ref, *rest):
        bg_in = rest[:nbg]
        o_ref, tot_ref, first_ref = rest[2 * nbg:2 * nbg + 3]
        bg_out = rest[2 * nbg + 3:3 * nbg + 3]
        acc_s, c_s, z_s, w_s, kmax_s = rest[3 * nbg + 3:3 * nbg + 8]
        p = pl.program_id(0)
        i = pl.program_id(1)
        if nbg:
            gather = GatherOps([s.shape[1] for s in bg_shards], bg_in, bg_out, *rest[3 * nbg + 8:])

            @pl.when((p == 0) & (i == 0))
            def _():
                gather.start_ici()

        @pl.when(i == 0)
        def _():
            kmax_s[...] = jnp.max(jnp.abs(k_ref[...]), axis=0, keepdims=True).astype(F32)

        q = q_ref[...]
        lane = lax.broadcasted_iota(jnp.int32, (tq, LANES), 1)
        first = lane < HEAD_DIM
        zero = jnp.zeros_like(q)
        q_heads = (jnp.where(first, q, zero), jnp.where(first, zero, q))
        row = lax.broadcasted_iota(jnp.int32, (tq, tk), 0)
        col = lax.broadcasted_iota(jnp.int32, (tq, tk), 1)
        visible = [col + r * tk < row for r in range(ratio)]
        krow = lax.broadcasted_iota(jnp.int32, (tk, tk), 0)
        kcol = lax.broadcasted_iota(jnp.int32, (tk, tk), 1)
        from_s = (krow >= kcol).astype(BF16)
        acc_s[...] = jnp.zeros_like(acc_s)
        c_s[...] = jnp.zeros_like(c_s)

        def rows(j):
            return pl.ds(pl.multiple_of(j * tk, tk), tk)

        def logits(j):
            kb = k_ref[rows(j), :]
            for hd in range(2):
                z_s[hd] = _dot_nt(q_heads[hd], kb) * LOG2E

        def flush(j):
            vb = v_ref[rows(j), :]
            for hd in range(2):
                acc_s[hd] += _dot(w_s[hd], vb)

        def block(j, mask=None, flush_block=None):
            if flush_block is not None:
                flush(flush_block)
            chunks = [(hd, slice(r0, r0 + SB_CHUNK)) for hd in range(2) for r0 in range(0, tq, SB_CHUNK)]
            k_next = k_ref[rows(jnp.maximum(j - 1, 0)), :]
            es, sums = [], []
            for hd, rs in chunks:
                z2 = z_s[hd, rs, :]
                z_s[hd, rs, :] = _dot_nt(q_heads[hd][rs, :], k_next) * LOG2E
                if mask is not None:
                    z2 = jnp.where(mask if mask.ndim == 0 else mask[rs, :], z2, NEG_BIG)
                sp = _softplus2(z2)
                c = c_s[hd, rs, :]
                es.append(z2 + _twice(c))
                c_s[hd, rs, :] = c - jnp.sum(sp, axis=1, keepdims=True)
                sums.append(_dot(sp.astype(BF16), from_s))
            for (hd, rs), e, s in zip(chunks, es, sums):
                w_s[hd, rs, :] = jnp.exp2(e - s).astype(BF16)

        z_bound = [LOG2E * jnp.sum(jnp.abs(q_heads[hd].astype(F32)) * kmax_s[...], axis=1, keepdims=True)
                   for hd in range(2)]

        def more_keys_matter():
            top = jnp.maximum(c_s[0] + z_bound[0], c_s[1] + z_bound[1])
            return (jnp.max(top) >= -SB_UNDERFLOW_BITS).astype(jnp.int32)

        assert ratio == 1
        logits(i)
        block(i, visible[0])

        def trip(carry):
            trips, _ = carry
            j = i - 1 - trips
            block(j, flush_block=j + 1)
            return trips + 1, more_keys_matter()

        trips, _ = lax.while_loop(lambda carry: jnp.logical_and(carry[0] < i, carry[1] > 0), trip,
                                  (jnp.int32(0), jnp.int32(1)))
        first_walked = i - trips
        flush(first_walked)
        first_ref[p, i] = first_walked.astype(F32)
        o_ref[...] = jnp.where(first, acc_s[0], acc_s[1]).astype(BF16)
        tot_ref[...] = jnp.where(first, c_s[0], c_s[1])
        if nbg:
            @pl.when((p == npair - 1) & (i == nq - 1))
            def _():
                gather.wait_ici()

    res = pl.pallas_call(
        body,
        name="sb_fwd",
        grid=(npair, nq),
        in_specs=[
            pl.BlockSpec((tq, LANES), lambda p, i: (i, p)),
            pl.BlockSpec((T, LANES), lambda p, i: (0, npair + p)),
            pl.BlockSpec((T, LANES), lambda p, i: (0, 2 * npair + p)),
        ] + [ANY] * (2 * nbg),
        out_specs=[pl.BlockSpec((tq, LANES), lambda p, i: (i, p))] * 2 + [pl.BlockSpec(memory_space=pltpu.SMEM)]
        + [ANY] * nbg,
        out_shape=[jax.ShapeDtypeStruct((T, D_MODEL), BF16), jax.ShapeDtypeStruct((T, D_MODEL), F32),
                   jax.ShapeDtypeStruct((npair, nq), F32)]
        + [jax.ShapeDtypeStruct(l.shape, l.dtype) for l in bg_lands],
        input_output_aliases={3 + nbg + t: 3 + t for t in range(nbg)},
        scratch_shapes=[
            pltpu.VMEM((2, tq, LANES), F32), pltpu.VMEM((2, tq, LANES), F32),
            pltpu.VMEM((2, tq, tk), F32), pltpu.VMEM((2, tq, tk), BF16),
            pltpu.VMEM((1, LANES), F32),
        ] + [pltpu.SemaphoreType.DMA((nbg, N_PEER_CHIPS))] * (2 if nbg else 0),
        compiler_params=_cparams(),
    )(qkv, qkv, qkv, *bg_shards, *bg_lands)
    return res[0], res[1], res[2], list(res[3:])


def sb_bwd(qkv, do, tot, first_block, bg_parts=()):
    T = qkv.shape[0]
    tq, tk = SB_QROWS_BWD, SB_BLOCK
    ratio = tq // tk
    npair = SB_HEADS // 2
    nq = T // tq
    nk = T // tk
    nbg = len(bg_parts)
    assert SB_QROWS == SB_QROWS_BWD

    def body(first_ref, q_ref, k_ref, v_ref, do_ref, tot_ref, *rest):
        bg_in = rest[:nbg]
        dq_ref, dk_ref, dv_ref = rest[nbg:nbg + 3]
        bg_out = rest[nbg + 3:2 * nbg + 3]
        dkt_s, dvt_s, dq_s, rest_s, cg_s, z_s, da_s, dz_s, a_s = rest[2 * nbg + 3:2 * nbg + 12]
        i = pl.program_id(1)
        start = jnp.clip(first_ref[pl.program_id(0), i].astype(jnp.int32), 0, i)
        if nbg:
            @pl.when((pl.program_id(0) == 0) & (i == 0))
            def _():
                for cp in chip_partial_copies(bg_in, bg_out, *rest[2 * nbg + 12:]):
                    cp.start()

        @pl.when(i == 0)
        def _():
            dkt_s[...] = jnp.zeros_like(dkt_s)
            dvt_s[...] = jnp.zeros_like(dvt_s)

        q = q_ref[...]
        do_ = do_ref[...]
        tot_ = tot_ref[...]
        q_t = q.astype(F32).T.astype(BF16)
        do_t = do_.astype(F32).T.astype(BF16)
        lane = lax.broadcasted_iota(jnp.int32, (tq, LANES), 1)
        first = lane < HEAD_DIM
        zero = jnp.zeros_like(q)
        q_heads = (jnp.where(first, q, zero), jnp.where(first, zero, q))
        do_heads = (jnp.where(first, do_, zero), jnp.where(first, zero, do_))
        row = lax.broadcasted_iota(jnp.int32, (tq, tk), 0)
        col = lax.broadcasted_iota(jnp.int32, (tq, tk), 1)
        visible = [col + r * tk < row for r in range(ratio)]
        krow = lax.broadcasted_iota(jnp.int32, (tk, tk), 0)
        kcol = lax.broadcasted_iota(jnp.int32, (tk, tk), 1)
        before = (krow < kcol).astype(BF16)
        from_s = (krow >= kcol).astype(BF16)
        last = ratio * i + ratio - 1
        rest_s[0] = jnp.broadcast_to(tot_[:, 0:1], (tq, LANES))
        rest_s[1] = jnp.broadcast_to(tot_[:, HEAD_DIM:HEAD_DIM + 1], (tq, LANES))
        cg_s[...] = jnp.zeros_like(cg_s)
        dq_s[...] = jnp.zeros_like(dq_s)
        dz_s[...] = jnp.zeros_like(dz_s)
        a_s[...] = jnp.zeros_like(a_s)

        def rows(j):
            return pl.ds(pl.multiple_of(j * tk, tk), tk)

        def logits(j):
            kb = k_ref[rows(j), :]
            vb = v_ref[rows(j), :]
            for hd in range(2):
                z_s[hd] = _dot_nt(q_heads[hd], kb) * LOG2E
                da_s[hd] = _dot_nt(do_heads[hd], vb)

        def flush(j):
            kb = k_ref[rows(j), :]
            for hd in range(2):
                dims = slice(hd * HEAD_DIM, (hd + 1) * HEAD_DIM)
                dq_s[hd] += _dot(dz_s[hd], kb)
                dkt_s[j, dims, :] += _dot(q_t[dims, :], dz_s[hd])
                dvt_s[j, dims, :] += _dot(do_t[dims, :], a_s[hd])

        def block(j, mask=None):
            flush(jnp.maximum(j - 1, 0))
            chunks = [(hd, slice(r0, r0 + SB_CHUNK)) for hd in range(2) for r0 in range(0, tq, SB_CHUNK)]
            nxt = rows(jnp.minimum(j + 1, last))
            k_next = k_ref[nxt, :]
            v_next = v_ref[nxt, :]
            stage1 = []
            for hd, rs in chunks:
                z2 = z_s[hd, rs, :]
                z_s[hd, rs, :] = _dot_nt(q_heads[hd][rs, :], k_next) * LOG2E
                if mask is not None:
                    z2 = jnp.where(mask if mask.ndim == 0 else mask[rs, :], z2, NEG_BIG)
                sp = _softplus2(z2)
                rest = rest_s[hd, rs, :] + jnp.sum(sp, axis=1, keepdims=True)
                rest_s[hd, rs, :] = rest
                stage1.append((z2 + _twice(rest), z2 - sp, _dot(sp.astype(BF16), from_s)))
            stage2 = []
            for (hd, rs), (e, log2_beta, ahead) in zip(chunks, stage1):
                a = jnp.exp2(e - ahead)
                g = a * da_s[hd, rs, :]
                da_s[hd, rs, :] = _dot_nt(do_heads[hd][rs, :], v_next)
                cg = cg_s[hd, rs, :]
                a_s[hd, rs, :] = a.astype(BF16)
                cg_s[hd, rs, :] = cg + jnp.sum(g, axis=1, keepdims=True)
                stage2.append((g, g + _twice(cg), log2_beta, _dot(g.astype(BF16), before)))
            for (hd, rs), (g, g_from, log2_beta, g_before) in zip(chunks, stage2):
                dz_s[hd, rs, :] = (g - jnp.exp2(log2_beta) * (g_from + g_before)).astype(BF16)

        assert ratio == 1
        logits(start)

        @pl.loop(start, i)
        def _(j):
            block(j)

        block(i, visible[0])
        flush(last)
        dq_ref[...] = (jnp.where(first, dq_s[0], dq_s[1]) * ATTN_SCALE).astype(BF16)

        @pl.when(i == nq - 1)
        def _():
            @pl.loop(0, nk)
            def _(b):
                dk_ref[rows(b), :] = dkt_s[b].T.astype(BF16)
                dv_ref[rows(b), :] = dvt_s[b].T.astype(BF16)

        if nbg:
            @pl.when((pl.program_id(0) == npair - 1) & (i == nq - 1))
            def _():
                for cp in chip_partial_copies(bg_in, bg_out, *rest[2 * nbg + 12:]):
                    cp.wait()

    qblk = pl.BlockSpec((tq, LANES), lambda p, i: (i, p))
    full = pl.BlockSpec((T, LANES), lambda p, i: (0, p))
    res = pl.pallas_call(
        body,
        name="sb_bwd",
        grid=(npair, nq),
        in_specs=[
            pl.BlockSpec(memory_space=pltpu.SMEM),
            qblk,
            pl.BlockSpec((T, LANES), lambda p, i: (0, npair + p)),
            pl.BlockSpec((T, LANES), lambda p, i: (0, 2 * npair + p)),
            qblk, qblk,
        ] + [ANY] * nbg,
        out_specs=[qblk, full, full] + [ANY] * nbg,
        out_shape=[jax.ShapeDtypeStruct((T, D_MODEL), BF16)] * 3
        + [jax.ShapeDtypeStruct(b.shape, b.dtype) for b in bg_parts],
        scratch_shapes=[
            pltpu.VMEM((nk, LANES, tk), F32), pltpu.VMEM((nk, LANES, tk), F32),
            pltpu.VMEM((2, tq, LANES), F32), pltpu.VMEM((2, tq, LANES), F32), pltpu.VMEM((2, tq, LANES), F32),
            pltpu.VMEM((2, tq, tk), F32), pltpu.VMEM((2, tq, tk), F32),
            pltpu.VMEM((2, tq, tk), BF16), pltpu.VMEM((2, tq, tk), BF16),
        ] + [pltpu.SemaphoreType.DMA((nbg, N_PEER_CHIPS))] * (2 if nbg else 0),
        compiler_params=_cparams(),
    )(first_block, qkv, qkv, qkv, do, tot, *bg_parts)
    return res[0], res[1], res[2], list(res[3:])


def _swa_valid(n):
    qi = lax.broadcasted_iota(jnp.int32, (WINDOW, 2 * WINDOW), 0)
    ki = lax.broadcasted_iota(jnp.int32, (WINDOW, 2 * WINDOW), 1)
    diff = qi + WINDOW - ki
    return (diff >= 0) & (diff < WINDOW) & ((n > 0) | (ki >= WINDOW))


def _to_half(x, first, src, dst):
    keep = first if src == 0 else jnp.logical_not(first)
    x = jnp.where(keep, x, jnp.zeros_like(x))
    if src != dst:
        x = pltpu.roll(x.astype(F32), HEAD_DIM, 1).astype(BF16)
    return x


SWA_GROUP = SWA_Q_HEADS // SWA_KV_HEADS


def _swa_cols(h):
    return slice((h // 2) * LANES, (h // 2 + 1) * LANES)


def _swa_kv_pair(h):
    return (h // SWA_GROUP) // 2


def _swa_kv_half(h):
    return (h // SWA_GROUP) % 2


def _kv_band(prev_ref, cur_ref, pb):
    cols = slice(pb * LANES, (pb + 1) * LANES)
    return jnp.concatenate([prev_ref[:, cols], cur_ref[:, cols]], axis=0)


def _swa_specs(T):
    nb = T // WINDOW
    kv_w = SWA_KV_HEADS * HEAD_DIM
    qrow = pl.BlockSpec((WINDOW, D_MODEL), lambda n: (n, 0))
    kv = [pl.BlockSpec((WINDOW, kv_w), lambda n, col=col, back=back: (jnp.maximum(n - back, 0), col))
          for col in (0, 1) for back in (0, 1)]
    smem = pl.BlockSpec(memory_space=pltpu.SMEM)
    return nb, qrow, kv, smem


def swa_fwd(q, kv, sinks):
    T = q.shape[0]
    nb, qrow, kv_specs, smem = _swa_specs(T)

    def body(sink_ref, q_ref, kc_ref, kp_ref, vc_ref, vp_ref, o_ref, lse_ref):
        n = pl.program_id(0)
        lane = lax.broadcasted_iota(jnp.int32, (WINDOW, LANES), 1)
        first = lane < HEAD_DIM
        valid = _swa_valid(n)
        k2 = [_kv_band(kp_ref, kc_ref, pb) for pb in range(SWA_KV_HEADS // 2)]
        v2 = [_kv_band(vp_ref, vc_ref, pb) for pb in range(SWA_KV_HEADS // 2)]
        logits = [jnp.where(valid, _dot_nt(_to_half(q_ref[:, _swa_cols(h)], first, h % 2, _swa_kv_half(h)),
                                            k2[_swa_kv_pair(h)]), NEG_BIG) for h in range(SWA_Q_HEADS)]
        probs = []
        lse_acc = jnp.zeros((WINDOW, LANES), F32)
        for h, s in enumerate(logits):
            sink = sink_ref[h]
            m = jnp.maximum(jnp.max(s, axis=1, keepdims=True), sink)
            p = jnp.exp(s - m)
            den = jnp.sum(p, axis=1, keepdims=True) + jnp.exp(sink - m)
            probs.append((p / den).astype(BF16))
            lse_acc = jnp.where(lane == h, m + jnp.log(den), lse_acc)
        outs = []
        for h, p in enumerate(probs):
            o = _dot(p, v2[_swa_kv_pair(h)])
            outs.append(pltpu.roll(o, HEAD_DIM, 1) if h % 2 != _swa_kv_half(h) else o)
        for pair in range(SWA_Q_HEADS // 2):
            o_ref[:, _swa_cols(2 * pair)] = jnp.where(first, outs[2 * pair], outs[2 * pair + 1]).astype(BF16)
        lse_ref[...] = lse_acc

    return pl.pallas_call(
        body,
        name="swa_fwd",
        grid=(nb,),
        in_specs=[smem, qrow] + kv_specs,
        out_specs=[qrow, pl.BlockSpec((WINDOW, LANES), lambda n: (n, 0))],
        out_shape=[jax.ShapeDtypeStruct((T, D_MODEL), BF16), jax.ShapeDtypeStruct((T, LANES), F32)],
        compiler_params=_cparams(),
    )(sinks, q, kv, kv, kv, kv)


def swa_bwd(q, kv, sinks, do, o, lse, cos, sin):
    T = q.shape[0]
    nb, qrow, kv_specs, smem = _swa_specs(T)
    kv_w = SWA_KV_HEADS * HEAD_DIM

    def body(sink_ref, q_ref, kc_ref, kp_ref, vc_ref, vp_ref, do_ref, o_ref, lse_ref, cos_ref, sin_ref,
             dq_ref, own_ref, prv_ref, dsink_ref):
        n = pl.program_id(0)

        @pl.when(n == 0)
        def _():
            dsink_ref[...] = jnp.zeros_like(dsink_ref)

        lane = lax.broadcasted_iota(jnp.int32, (WINDOW, LANES), 1)
        lane1 = lax.broadcasted_iota(jnp.int32, (1, LANES), 1)
        first = lane < HEAD_DIM
        valid = _swa_valid(n)
        cos_ = cos_ref[...]
        sin_ = sin_ref[...]
        k2 = [_kv_band(kp_ref, kc_ref, pb) for pb in range(SWA_KV_HEADS // 2)]
        v2 = [_kv_band(vp_ref, vc_ref, pb) for pb in range(SWA_KV_HEADS // 2)]
        q_t = q_ref[...].astype(F32).T.astype(BF16)
        do_t = do_ref[...].astype(F32).T.astype(BF16)
        stage1 = []
        for h in range(SWA_Q_HEADS):
            a, b, pb = h % 2, _swa_kv_half(h), _swa_kv_pair(h)
            qh = _to_half(q_ref[:, _swa_cols(h)], first, a, b)
            doh = _to_half(do_ref[:, _swa_cols(h)], first, a, b)
            stage1.append((jnp.where(valid, _dot_nt(qh, k2[pb]), NEG_BIG), _dot_nt(doh, v2[pb])))
        deltas = []
        for pair in range(SWA_Q_HEADS // 2):
            prod = do_ref[:, _swa_cols(2 * pair)].astype(F32) * o_ref[:, _swa_cols(2 * pair)].astype(F32)
            deltas += [jnp.sum(jnp.where(first, prod, 0.0), axis=1, keepdims=True),
                       jnp.sum(jnp.where(first, 0.0, prod), axis=1, keepdims=True)]
        stage2 = []
        dsink = jnp.zeros((1, LANES), F32)
        for h, (s, dp) in enumerate(stage1):
            lse_h = lse_ref[:, h:h + 1]
            p = jnp.exp(s - lse_h)
            delta = deltas[h]
            p_sink = jnp.exp(sink_ref[h] - lse_h)
            dsink = dsink + jnp.where(lane1 == h, -jnp.sum(p_sink * delta, axis=0, keepdims=True), 0.0)
            stage2.append(((p * (dp - delta)).astype(BF16), p.astype(BF16)))
        dqs = []
        dk_t = [None] * SWA_KV_HEADS
        dv_t = [None] * SWA_KV_HEADS
        for h, (ds, pb16) in enumerate(stage2):
            kvh = h // SWA_GROUP
            dims = slice(h * HEAD_DIM, (h + 1) * HEAD_DIM)
            dq = _dot(ds, k2[_swa_kv_pair(h)])
            dqs.append(pltpu.roll(dq, HEAD_DIM, 1) if h % 2 != _swa_kv_half(h) else dq)
            dk_h = _dot(q_t[dims, :], ds)
            dv_h = _dot(do_t[dims, :], pb16)
            dk_t[kvh] = dk_h if dk_t[kvh] is None else dk_t[kvh] + dk_h
            dv_t[kvh] = dv_h if dv_t[kvh] is None else dv_t[kvh] + dv_h
        for pair in range(SWA_Q_HEADS // 2):
            dqp = jnp.where(first, dqs[2 * pair], dqs[2 * pair + 1])
            dq_ref[:, _swa_cols(2 * pair)] = ((dqp * cos_ + _swap32(dqp * sin_)) * ATTN_SCALE).astype(BF16)
        for pb in range(SWA_KV_HEADS // 2):
            dk2 = jnp.concatenate([dk_t[2 * pb], dk_t[2 * pb + 1]], axis=0).T
            dv2 = jnp.concatenate([dv_t[2 * pb], dv_t[2 * pb + 1]], axis=0).T
            kcols = slice(pb * LANES, (pb + 1) * LANES)
            vcols = slice(kv_w + pb * LANES, kv_w + (pb + 1) * LANES)
            prv_ref[:, kcols] = dk2[:WINDOW]
            own_ref[:, kcols] = dk2[WINDOW:]
            prv_ref[:, vcols] = dv2[:WINDOW]
            own_ref[:, vcols] = dv2[WINDOW:]
        dsink_ref[...] += dsink

    tab = pl.BlockSpec((WINDOW, LANES), lambda n: (n, 0))
    kvrow = pl.BlockSpec((WINDOW, 2 * kv_w), lambda n: (n, 0))
    return pl.pallas_call(
        body,
        name="swa_bwd",
        grid=(nb,),
        in_specs=[smem, qrow] + kv_specs + [qrow, qrow, tab, tab, tab],
        out_specs=[qrow, kvrow, kvrow, pl.BlockSpec((1, LANES), lambda n: (0, 0))],
        out_shape=[
            jax.ShapeDtypeStruct((T, D_MODEL), BF16),
            jax.ShapeDtypeStruct((T, 2 * kv_w), F32),
            jax.ShapeDtypeStruct((T, 2 * kv_w), F32),
            jax.ShapeDtypeStruct((1, LANES), F32),
        ],
        compiler_params=_cparams(),
    )(sinks, q, kv, kv, kv, kv, do, o, lse, cos, sin)


def kv_grad_combine(own, prv, cos, sin):
    T = own.shape[0]
    nb = T // WINDOW
    kv_w = SWA_KV_HEADS * HEAD_DIM

    def body(own_ref, nxt_ref, cos_ref, sin_ref, out_ref):
        n = pl.program_id(0)
        nxt = jnp.where(n + 1 < nb, nxt_ref[...], 0.0)
        tot = own_ref[...] + nxt
        dk = tot[:, :kv_w]
        c = _tile_lanes(cos_ref[...], kv_w)
        s = _tile_lanes(sin_ref[...], kv_w)
        out_ref[:, :kv_w] = (dk * c + _swap32(dk * s)).astype(BF16)
        out_ref[:, kv_w:] = tot[:, kv_w:].astype(BF16)

    tab = pl.BlockSpec((WINDOW, LANES), lambda n: (n, 0))
    kvrow = pl.BlockSpec((WINDOW, 2 * kv_w), lambda n: (n, 0))
    return pl.pallas_call(
        body,
        name="kv_grad_combine",
        grid=(nb,),
        in_specs=[kvrow, pl.BlockSpec((WINDOW, 2 * kv_w), lambda n: (jnp.minimum(n + 1, nb - 1), 0)), tab, tab],
        out_specs=kvrow,
        out_shape=jax.ShapeDtypeStruct((T, 2 * kv_w), BF16),
        compiler_params=_cparams(),
    )(own, prv, cos, sin)


ANY = pl.BlockSpec(memory_space=pl.ANY)


def _place():
    x, y, c = lax.axis_index("x"), lax.axis_index("y"), lax.axis_index("c")
    other_chips = [(1 - x, y), (x, 1 - y), (1 - x, 1 - y)]
    return x, y, c, 2 * x + y, other_chips


N_PEER_CHIPS = N_CHIPS - 1


class GatherOps:
    def __init__(self, rows, shards, lands, ici_send, ici_recv, d2d_send=None, d2d_recv=None):
        self.rows, self.shards, self.lands = rows, shards, lands
        self.ici_send, self.ici_recv, self.d2d_send, self.d2d_recv = ici_send, ici_recv, d2d_send, d2d_recv
        self.x, self.y, self.c, self.me, self.chips = _place()
        self.pairs = [(t, jdx) for t in range(len(rows)) for jdx in range(N_PEER_CHIPS)]

    def _half(self, ref, t, which):
        r = self.rows[t] // 2
        return ref.at[:, pl.ds(pl.multiple_of(which * r, 16), r), :]

    def _ici(self, t, jdx):
        px, py = self.chips[jdx]
        return pltpu.make_async_remote_copy(
            src_ref=self._half(self.shards[t], t, self.c), dst_ref=self._half(self.lands[t].at[self.me], t, self.c),
            send_sem=self.ici_send.at[t, jdx], recv_sem=self.ici_recv.at[t, jdx],
            device_id=(px, py, self.c), device_id_type=MESH)

    def _landed(self, t, jdx):
        px, py = self.chips[jdx]
        blk = self._half(self.lands[t].at[2 * px + py], t, self.c)
        return pltpu.make_async_remote_copy(
            src_ref=blk, dst_ref=blk, send_sem=self.ici_send.at[t, jdx], recv_sem=self.ici_recv.at[t, jdx],
            device_id=(px, py, self.c), device_id_type=MESH)

    def _d2d(self, t, jdx, which):
        px, py = self.chips[jdx]
        blk = self._half(self.lands[t].at[2 * px + py], t, which)
        return pltpu.make_async_remote_copy(
            src_ref=blk, dst_ref=blk, send_sem=self.d2d_send.at[t, jdx], recv_sem=self.d2d_recv.at[t, jdx],
            device_id=(self.x, self.y, 1 - self.c), device_id_type=MESH)

    def start_ici(self):
        for t, jdx in self.pairs:
            self._ici(t, jdx).start()

    def wait_ici(self):
        for t, jdx in self.pairs:
            self._landed(t, jdx).wait_recv()
        self.wait_ici_sends()

    def wait_ici_sends(self):
        for t, jdx in self.pairs:
            self._ici(t, jdx).wait_send()

    def forward_arrivals(self):
        for t, jdx in self.pairs:
            self._landed(t, jdx).wait_recv()
            self._d2d(t, jdx, self.c).start()

    def start_forwards(self):
        for t, jdx in self.pairs:
            self._d2d(t, jdx, self.c).start()

    def wait_forwards(self):
        for t, jdx in self.pairs:
            self._d2d(t, jdx, 1 - self.c).wait_recv()
            self._d2d(t, jdx, self.c).wait_send()


def all_gather_weights(shards, lands):
    n = len(shards)
    rows = [s.shape[1] for s in shards]

    def body(*refs):
        ins, outs = refs[:n], refs[2 * n:3 * n]
        ops = GatherOps(rows, ins, outs, *refs[3 * n:])
        ops.start_ici()
        ops.forward_arrivals()
        ops.wait_forwards()
        ops.wait_ici_sends()

    return pl.pallas_call(
        body,
        name="all_gather_weights",
        in_specs=[ANY] * (2 * n),
        out_specs=[ANY] * n,
        out_shape=[jax.ShapeDtypeStruct(l.shape, l.dtype) for l in lands],
        input_output_aliases={n + t: t for t in range(n)},
        scratch_shapes=[pltpu.SemaphoreType.DMA((n, N_PEER_CHIPS))] * 4,
    )(*shards, *lands)


def place_own_shard(name, shard, chip):
    nl, r, c = shard.shape

    def body(chip_ref, s_ref, o_ref):
        o_ref[...] = s_ref[...]

    return pl.pallas_call(
        body, name=name,
        grid_spec=pltpu.PrefetchScalarGridSpec(
            num_scalar_prefetch=1, grid=(nl,),
            in_specs=[pl.BlockSpec((None, r, c), lambda l, chip_ref: (l, 0, 0))],
            out_specs=pl.BlockSpec((None, None, r, c), lambda l, chip_ref: (chip_ref[0], l, 0, 0))),
        out_shape=jax.ShapeDtypeStruct((N_CHIPS,) + shard.shape, shard.dtype), compiler_params=_cparams(),
    )(chip, shard)


def exchange_halves(name, slabs):
    n = len(slabs)

    def body(*refs):
        ins, theirs = refs[:n], refs[n:2 * n]
        send_sems, recv_sems = refs[2 * n:]
        x, y, c, _, _ = _place()
        copies = []
        for t in range(n):
            cp = pltpu.make_async_remote_copy(
                src_ref=ins[t].at[1 - c], dst_ref=theirs[t], send_sem=send_sems.at[t],
                recv_sem=recv_sems.at[t], device_id=(x, y, 1 - c), device_id_type=MESH)
            cp.start()
            copies.append(cp)
        for cp in copies:
            cp.wait()

    return pl.pallas_call(
        body,
        name=name,
        in_specs=[ANY] * n,
        out_specs=[ANY] * n,
        out_shape=[jax.ShapeDtypeStruct(s.shape[1:], s.dtype) for s in slabs],
        scratch_shapes=[pltpu.SemaphoreType.DMA((n,)), pltpu.SemaphoreType.DMA((n,))],
    )(*slabs)


def chip_partial_copies(ins, outs, send_sems, recv_sems):
    _, _, c, me, chips = _place()
    return [pltpu.make_async_remote_copy(
        src_ref=ins[t].at[2 * px + py], dst_ref=outs[t].at[me], send_sem=send_sems.at[t, jdx],
        recv_sem=recv_sems.at[t, jdx], device_id=(px, py, c), device_id_type=MESH)
        for t in range(len(ins)) for jdx, (px, py) in enumerate(chips)]


def exchange_chip_partials(name, parts):
    n = len(parts)

    def body(*refs):
        copies = chip_partial_copies(refs[:n], refs[n:2 * n], *refs[2 * n:])
        for cp in copies:
            cp.start()
        for cp in copies:
            cp.wait()

    return pl.pallas_call(
        body,
        name=name,
        in_specs=[ANY] * n,
        out_specs=[ANY] * n,
        out_shape=[jax.ShapeDtypeStruct(p.shape, p.dtype) for p in parts],
        scratch_shapes=[pltpu.SemaphoreType.DMA((n, 3)), pltpu.SemaphoreType.DMA((n, 3))],
    )(*parts)


def share_reduced_halves(name, halves):
    n = len(halves)

    def body(*refs):
        ins, outs = refs[:n], refs[n:2 * n]
        send_sems, recv_sems = refs[2 * n:]
        x, y, c, _, _ = _place()
        copies = []
        for t in range(n):
            cp = pltpu.make_async_remote_copy(
                src_ref=ins[t], dst_ref=outs[t], send_sem=send_sems.at[t],
                recv_sem=recv_sems.at[t], device_id=(x, y, 1 - c), device_id_type=MESH)
            cp.start()
            copies.append(cp)
        for cp in copies:
            cp.wait()

    return pl.pallas_call(
        body,
        name=name,
        in_specs=[ANY] * n,
        out_specs=[ANY] * n,
        out_shape=[jax.ShapeDtypeStruct(h.shape, h.dtype) for h in halves],
        scratch_shapes=[pltpu.SemaphoreType.DMA((n,)), pltpu.SemaphoreType.DMA((n,))],
    )(*halves)


def _row_tile(r, c):
    tr = r
    while tr * c * 4 > (3 << 19) and tr % 16 == 0:
        tr //= 2
    return tr


def add_sibling(name, slab, theirs, core):
    _, ns, slots, r, c = slab.shape
    tr = _row_tile(r, c)

    def body(core_ref, a_ref, b_ref, o_ref):
        o_ref[...] = (a_ref[...] + b_ref[...]).astype(BF16)

    blk = pl.BlockSpec((None, None, tr, c), lambda s, l, i, core_ref: (s, l, i, 0))
    return pl.pallas_call(
        body, name=name,
        grid_spec=pltpu.PrefetchScalarGridSpec(
            num_scalar_prefetch=1, grid=(ns, slots, r // tr),
            in_specs=[pl.BlockSpec((None, None, None, tr, c), lambda s, l, i, core_ref: (core_ref[0], s, l, i, 0)), blk],
            out_specs=blk),
        out_shape=jax.ShapeDtypeStruct(theirs.shape, BF16), compiler_params=_cparams(),
    )(core, slab, theirs)


def sum_chips(name, recv, own, chip):
    _, slots, r, c = recv.shape
    tr = _row_tile(r, c)

    def body(chip_ref, r0, r1, r2, r3, own_ref, o_ref):
        me = chip_ref[0]
        mine = own_ref[...]
        terms = [jnp.where(me == s, mine, rr[...]).astype(F32) for s, rr in enumerate((r0, r1, r2, r3))]
        o_ref[...] = ((terms[0] + terms[1]) + terms[2]) + terms[3]

    def src(s):
        return pl.BlockSpec((None, None, tr, c),
                            lambda l, i, chip_ref: (jnp.where(chip_ref[0] == s, (s + 1) % N_CHIPS, s), l, i, 0))

    return pl.pallas_call(
        body, name=name,
        grid_spec=pltpu.PrefetchScalarGridSpec(
            num_scalar_prefetch=1, grid=(slots, r // tr),
            in_specs=[src(0), src(1), src(2), src(3),
                      pl.BlockSpec((None, None, tr, c), lambda l, i, chip_ref: (chip_ref[0], l, i, 0))],
            out_specs=pl.BlockSpec((None, tr, c), lambda l, i, chip_ref: (l, i, 0))),
        out_shape=jax.ShapeDtypeStruct((slots, r, c), F32), compiler_params=_cparams(),
    )(chip, recv, recv, recv, recv, own)


def _adamw_math(w, g, m, v):
    m = ADAM_B1 * m + (1.0 - ADAM_B1) * g
    v = ADAM_B2 * v + (1.0 - ADAM_B2) * (g * g)
    m_hat = m / (1.0 - ADAM_B1 ** ADAM_STEP)
    v_hat = v / (1.0 - ADAM_B2 ** ADAM_STEP)
    delta = -ADAM_LR * (m_hat / (jnp.sqrt(v_hat) + ADAM_EPS) + ADAM_WD * w)
    return delta, m, v


def adamw_shard(name, w, m, v, g_pairs, core, slots, row_halves):
    n = w.shape[0]
    assert n == len(g_pairs)
    _, r, c = g_pairs[0][0].shape
    tr = _row_tile(r, c)
    nr = r // tr

    def body(core_ref, w_ref, m_ref, v_ref, *rest):
        g_refs, (go_ref, d_ref, mo_ref, vo_ref) = rest[:2 * n], rest[2 * n:]
        mine = pl.program_id(1) == core_ref[0]
        g = jnp.where(mine, g_refs[0][...], g_refs[1][...])
        for l in range(1, n):
            g = jnp.where(pl.program_id(0) == l, jnp.where(mine, g_refs[2 * l][...], g_refs[2 * l + 1][...]), g)
        delta, mm, vv = _adamw_math(w_ref[...], g, m_ref[...], v_ref[...])
        go_ref[...] = g
        d_ref[...] = delta
        mo_ref[...] = mm
        vo_ref[...] = vv

    if row_halves:
        wspec = pl.BlockSpec((None, tr, c), lambda l, h, i, core_ref: (l, h * nr + i, 0))
    else:
        wspec = pl.BlockSpec((None, tr, c), lambda l, h, i, core_ref: (l, i, h))
    def gspec(slot):
        return pl.BlockSpec((None, tr, c), lambda l, h, i, core_ref: (slot, i, 0))

    shp = jax.ShapeDtypeStruct(w.shape, F32)
    return pl.pallas_call(
        body, name=name,
        grid_spec=pltpu.PrefetchScalarGridSpec(
            num_scalar_prefetch=1, grid=(n, 2, nr),
            in_specs=[wspec, wspec, wspec] + [gspec(s) for s in slots for _ in range(2)], out_specs=[wspec] * 4),
        out_shape=[shp] * 4, compiler_params=_cparams(),
    )(core, w, m, v, *[g for pair in g_pairs for g in pair])


SMALL_ROWS = 16


def small_allreduce_adamw(part, w, m, v):
    def body(p_ref, w_ref, m_ref, v_ref, g_ref, d_ref, mo_ref, vo_ref, buf, send_sems, recv_sems):
        x, y, c, _, _ = _place()
        me = 4 * x + 2 * y + c
        buf[me] = p_ref[...]
        copies = []
        for k in range(1, N_DEV):
            kx, ky, kc = (k >> 2) & 1, (k >> 1) & 1, k & 1
            peer = (x ^ kx, y ^ ky, c ^ kc)
            cp = pltpu.make_async_remote_copy(
                src_ref=p_ref, dst_ref=buf.at[me], send_sem=send_sems.at[k - 1],
                recv_sem=recv_sems.at[k - 1], device_id=peer, device_id_type=MESH)
            cp.start()
            copies.append(cp)
        for cp in copies:
            cp.wait()
        g = buf[0]
        for dev in range(1, N_DEV):
            g = g + buf[dev]
        delta, mm, vv = _adamw_math(w_ref[...], g, m_ref[...], v_ref[...])
        g_ref[...] = g
        d_ref[...] = delta
        mo_ref[...] = mm
        vo_ref[...] = vv

    vm = pl.BlockSpec(memory_space=pltpu.VMEM)
    shp = jax.ShapeDtypeStruct(part.shape, F32)
    return pl.pallas_call(
        body, name="small_allreduce_adamw",
        in_specs=[vm] * 4, out_specs=[vm] * 4, out_shape=[shp] * 4,
        scratch_shapes=[
            pltpu.VMEM((N_DEV,) + part.shape, F32),
            pltpu.SemaphoreType.DMA((N_DEV - 1,)), pltpu.SemaphoreType.DMA((N_DEV - 1,)),
        ],
    )(part, w, m, v)


def _rope_tables(T):
    half = HEAD_DIM // 2
    inv_freq = ROPE_THETA ** (-jnp.arange(half, dtype=F32) / half)
    ang = jnp.arange(T).astype(F32)[:, None] * inv_freq[None, :]
    cos = jnp.tile(jnp.cos(ang), (1, LANES // half))
    sin = jnp.tile(jnp.sin(ang), (1, LANES // half))
    lane = jnp.arange(LANES)
    sign = jnp.where((lane % HEAD_DIM) < half, -1.0, 1.0).astype(F32)
    return cos, sin * sign[None, :]


def _pack_small(ffn1, mix, ffn2, kvn, fin, sinks, loss_row):
    sink_row = jnp.pad(sinks.reshape(1, SWA_Q_HEADS), ((0, 0), (0, D_MODEL - SWA_Q_HEADS)))
    rows = jnp.concatenate([ffn1, mix, ffn2, kvn.reshape(1, -1), fin.reshape(1, -1), sink_row, loss_row], axis=0)
    return jnp.concatenate([rows, jnp.zeros((SMALL_ROWS - rows.shape[0], D_MODEL), F32)], axis=0)


def kernel(x, ffn1_norm, ffn1_w_in, ffn1_w_out, mix_norm, ffn2_norm, ffn2_w_in, ffn2_w_out, sb_w_qkv, sb_w_o, kv_norm, kv_w, swa_w_q, swa_sinks, swa_w_o, final_norm, loss_target, m_ffn1_norm, m_ffn1_w_in, m_ffn1_w_out, m_mix_norm, m_ffn2_norm, m_ffn2_w_in, m_ffn2_w_out, m_sb_w_qkv, m_sb_w_o, m_kv_norm, m_kv_w, m_swa_w_q, m_swa_sinks, m_swa_w_o, m_final_norm, v_ffn1_norm, v_ffn1_w_in, v_ffn1_w_out, v_mix_norm, v_ffn2_norm, v_ffn2_w_in, v_ffn2_w_out, v_sb_w_qkv, v_sb_w_o, v_kv_norm, v_kv_w, v_swa_w_q, v_swa_sinks, v_swa_w_o, v_final_norm):
    T = x.shape[1]
    kv_cols = SWA_KV_HEADS * HEAD_DIM
    x2 = x.reshape(T, D_MODEL)
    tgt = loss_target.reshape(T, D_MODEL)
    cos, sin = _rope_tables(T)

    w_in_l = jnp.concatenate([ffn1_w_in, ffn2_w_in], axis=0).astype(BF16)
    w_out_l = jnp.concatenate([ffn1_w_out, ffn2_w_out], axis=0).astype(BF16)
    sq_l = jnp.concatenate([sb_w_o, swa_w_q, swa_w_o], axis=0).astype(BF16)
    qkv_l = sb_w_qkv[0].astype(BF16)
    kvw_l = kv_w.astype(BF16)
    core = lax.axis_index("c").astype(jnp.int32).reshape(1)
    chip = (2 * lax.axis_index("x") + lax.axis_index("y")).astype(jnp.int32).reshape(1)
    early = [w_in_l[:1], w_out_l[:1]]
    mid = [sq_l, qkv_l[None]]
    late = [w_in_l[1:], w_out_l[1:], kvw_l[None]]
    early_lands = [place_own_shard(f"own_early_{t}", s, chip) for t, s in enumerate(early)]
    mid_lands = [place_own_shard(f"own_mid_{t}", s, chip) for t, s in enumerate(mid)]
    late_lands = [place_own_shard(f"own_late_{t}", s, chip) for t, s in enumerate(late)]
    w_in0, w_out0 = all_gather_weights(early, early_lands)

    def ffn_w(slot):
        return (w_in0, w_out0, 0) if slot == 0 else (w_in_r, w_out_r, slot - 1)

    def vec(a, i):
        return a[i].reshape(1, D_MODEL)

    ident = lambda w: w
    sq_prep = lambda w: w.reshape(D_MODEL, w.shape[-1])
    qscale = jnp.concatenate([jnp.full((1, D_MODEL), ATTN_SCALE, F32), jnp.ones((1, 2 * D_MODEL), F32)], axis=1)
    swa_scale = jnp.full((1, D_MODEL), ATTN_SCALE, F32)
    sinks = swa_sinks.reshape(SWA_Q_HEADS)

    h1, gate1, up1, (w_sq, w_qkv) = ffn_fwd("l0a", x2, vec(ffn1_norm, 0), *ffn_w(SLOT_FFN1[0]), mid, mid_lands)
    w_qkv = w_qkv.reshape(N_CHIPS, D_MODEL, QKV_COLS)
    qkv = qkv_fwd(h1, vec(mix_norm, 0), w_qkv, qscale)
    o_sb, tot, sb_first, late_lands = sb_fwd(qkv, late, late_lands)
    h2, (w_in_r, w_out_r, w_kv) = linear_res("sb_out", o_sb, w_sq, SQ_SB_O, h1, late_lands)
    w_kv = w_kv.reshape(D_MODEL, 2 * kv_cols)
    h3, gate2, up2 = ffn_fwd("l0b", h2, vec(ffn2_norm, 0), *ffn_w(SLOT_FFN2[0]))
    kvn = kv_norm.reshape(1, D_MODEL)
    kv_sw = rms_linear("kv_proj", h3, kvn, w_kv, pl.BlockSpec((D_MODEL, kv_cols), lambda i, j: (0, j)), ident,
                       2 * kv_cols, kv_cols, rope=(cos, sin), rope_blocks=1)
    h4, gate3, up3 = ffn_fwd("l1a", h3, vec(ffn1_norm, 1), *ffn_w(SLOT_FFN1[1]))
    q_sw = rms_linear("swa_q", h4, vec(mix_norm, 1), w_sq,
                      pl.BlockSpec((N_CHIPS, None, SQ_ROWS, 512), lambda i, j: (0, SQ_SWA_Q, 0, j)), sq_prep,
                      D_MODEL, 512, rope=(cos, sin), scale=swa_scale)
    o_sw, lse = swa_fwd(q_sw, kv_sw, sinks)
    h5 = linear_res("swa_out", o_sw, w_sq, SQ_SWA_O, h4)
    h6, gate4, up4 = ffn_fwd("l1b", h5, vec(ffn2_norm, 1), *ffn_w(SLOT_FFN2[1]))
    dh6, loss_p, d_final = loss_bwd(h6, final_norm.reshape(1, D_MODEL), tgt)

    slab = {}
    ffn_place = {SLOT_FFN1[0]: (0, 0, 1), SLOT_FFN1[1]: (1, 0, 3), SLOT_FFN2[0]: (1, 1, 3), SLOT_FFN2[1]: (1, 2, 3)}
    sq_place = {SQ_SB_O: (1, 0, 3), SQ_SWA_Q: (1, 1, 3), SQ_SWA_O: (1, 2, 3)}

    def ffn_grads(tag, dh, h_in, g, gate, up, slot):
        dh_in, xn, dg_, du_, act, dhb, dnorm = ffn_bwd(tag, dh, h_in, g, gate, up, *ffn_w(slot))
        grp, s, ns = ffn_place[slot]
        in_shape = (2, N_CHIPS, ns, D_MODEL // 2, FF_CHUNK)
        out_shape = (2, N_CHIPS, ns, FF_ROWS, D_MODEL // 2)
        blk = (None, 1, None, D_MODEL // 2, FF_CHUNK)
        slab["in", grp] = mm_tn(f"dw_gate_{tag}", xn, dg_, D_MODEL // 2, FF_CHUNK, blk,
                                lambda k, n: (k, n, s, 0, 0), in_shape, prev=slab.get(("in", grp)))
        slab["in", grp] = mm_tn(f"dw_up_{tag}", xn, du_, D_MODEL // 2, FF_CHUNK, blk,
                                lambda k, n: (k, 2 + n, s, 0, 0), in_shape, prev=slab["in", grp])
        slab["out", grp] = mm_tn(f"dw_out_{tag}", act, dhb, FF_CHUNK, D_MODEL // 2,
                                 (None, 2, None, FF_ROWS, D_MODEL // 2),
                                 lambda k, n: (n, k, s, 0, 0), out_shape, prev=slab.get(("out", grp)))
        return dh_in, dnorm

    def sq_grad(tag, a, dyb, t):
        grp, s, ns = sq_place[t]
        slab["sq", grp] = mm_tn(f"dw_sq_{tag}", a, dyb, D_MODEL, D_MODEL // 2,
                                (None, N_CHIPS, None, SQ_ROWS, D_MODEL // 2),
                                lambda k, n: (n, 0, s, 0, 0), (2, N_CHIPS, ns, SQ_ROWS, D_MODEL // 2),
                                prev=slab.get(("sq", grp)))

    def reduce_group(grp, kinds, host=None):
        slabs = [slab[kind, grp][0] for kind in kinds]
        names = [f"{kind}{grp}" for kind in kinds]
        theirs = exchange_halves(f"exchange_halves_{grp}", [slab[kind, grp][1] for kind in kinds])
        parts = [add_sibling(f"add_sibling_{nm}", s, t, core) for nm, s, t in zip(names, slabs, theirs)]
        arrived = host(parts) if host else exchange_chip_partials(f"exchange_chip_partials_{grp}", parts)
        halves = [sum_chips(f"sum_chips_{nm}", g, p, chip) for nm, g, p in zip(names, arrived, parts)]
        sib_halves = share_reduced_halves(f"share_reduced_halves_{grp}", halves)
        return {kind: pair for kind, pair in zip(kinds, zip(halves, sib_halves))}

    dh5, d_ffn2_1 = ffn_grads("l1b", dh6, h5, vec(ffn2_norm, 1), gate4, up4, SLOT_FFN2[1])
    do_sw, dh5b = linear_bwd_plain("swa_out_bwd", dh5, w_sq, SQ_SWA_O)
    sq_grad("swa_o", o_sw, dh5b, SQ_SWA_O)
    dq_sw, kv_own, kv_prev, d_sinks = swa_bwd(q_sw, kv_sw, sinks, do_sw, o_sw, lse, cos, sin)
    sq_w_spec = pl.BlockSpec((N_CHIPS, None, SQ_ROWS, D_MODEL), lambda i, j: (0, SQ_SWA_Q, 0, 0))
    dh4, hn4, d_mix_1 = linear_bwd_rms("swa_q_bwd", [(dq_sw, w_sq, sq_w_spec, sq_prep)], h4, vec(mix_norm, 1), dh5,
                                       1, D_MODEL)
    sq_grad("swa_q", hn4, dq_sw, SQ_SWA_Q)
    dh3a, d_ffn1_1 = ffn_grads("l1a", dh4, h3, vec(ffn1_norm, 1), gate3, up3, SLOT_FFN1[1])
    dkv = kv_grad_combine(kv_own, kv_prev, cos, sin)
    kv_w_spec = pl.BlockSpec((D_MODEL, 2 * kv_cols), lambda i, j: (0, 0))
    dh3, xn3, d_kvn = linear_bwd_rms("kv_bwd", [(dkv, w_kv, kv_w_spec, ident)], h3, kvn, dh3a, 1, 2 * kv_cols)
    slab["kv", 1] = mm_tn("dw_kv", xn3, dkv, D_MODEL, kv_cols, (None, N_CHIPS, None, SQ_ROWS, kv_cols),
                          lambda k, n: (n, 0, 0, 0, 0), (2, N_CHIPS, 1, SQ_ROWS, kv_cols))
    dh2, d_ffn2_0 = ffn_grads("l0b", dh3, h2, vec(ffn2_norm, 0), gate2, up2, SLOT_FFN2[0])
    do_sb, dh2b = linear_bwd_plain("sb_out_bwd", dh2, w_sq, SQ_SB_O)
    sq_grad("sb_o", o_sb, dh2b, SQ_SB_O)
    sb_grads = []

    def behind_sb_bwd(parts):
        dq_sb, dk_sb, dv_sb, arrived = sb_bwd(qkv, do_sb, tot, sb_first, parts)
        sb_grads.extend([dq_sb, dk_sb, dv_sb])
        return arrived

    red = {1: reduce_group(1, ["in", "out", "sq", "kv"], host=behind_sb_bwd)}
    dqkv = jnp.concatenate(sb_grads, axis=1)
    dh1, hn1, d_mix_0 = qkv_bwd(dqkv, w_qkv, h1, vec(mix_norm, 0), dh2)
    slab["qkv", 0] = mm_tn("dw_qkv", hn1, dqkv, D_MODEL // 2, QKV_COLS, (None, 1, None, D_MODEL // 2, QKV_COLS),
                           lambda k, n: (k, n, 0, 0, 0), (2, N_CHIPS, 1, D_MODEL // 2, QKV_COLS))
    dx, d_ffn1_0 = ffn_grads("l0a", dh1, x2, vec(ffn1_norm, 0), gate1, up1, SLOT_FFN1[0])
    red[0] = reduce_group(0, ["in", "out", "qkv"])

    def upd(name, w, m, v, kind, places, row_halves):
        shp = w.shape
        w3 = w.reshape((-1,) + shp[-2:])
        outs = adamw_shard(name, w3, m.reshape(w3.shape), v.reshape(w3.shape),
                           [red[grp][kind] for grp, _ in places], core, [s for _, s in places], row_halves)
        return [o.reshape(shp) for o in outs]

    ffn1_places = [ffn_place[s][:2] for s in SLOT_FFN1]
    ffn2_places = [ffn_place[s][:2] for s in SLOT_FFN2]
    r_ffn1_in = upd("adamw_ffn1_in", ffn1_w_in, m_ffn1_w_in, v_ffn1_w_in, "in", ffn1_places, True)
    r_ffn2_in = upd("adamw_ffn2_in", ffn2_w_in, m_ffn2_w_in, v_ffn2_w_in, "in", ffn2_places, True)
    r_ffn1_out = upd("adamw_ffn1_out", ffn1_w_out, m_ffn1_w_out, v_ffn1_w_out, "out", ffn1_places, False)
    r_ffn2_out = upd("adamw_ffn2_out", ffn2_w_out, m_ffn2_w_out, v_ffn2_w_out, "out", ffn2_places, False)
    r_qkv = upd("adamw_qkv", sb_w_qkv, m_sb_w_qkv, v_sb_w_qkv, "qkv", [(0, 0)], True)
    r_sb_o = upd("adamw_sb_o", sb_w_o, m_sb_w_o, v_sb_w_o, "sq", [sq_place[SQ_SB_O][:2]], False)
    r_swa_q = upd("adamw_swa_q", swa_w_q, m_swa_w_q, v_swa_w_q, "sq", [sq_place[SQ_SWA_Q][:2]], False)
    r_swa_o = upd("adamw_swa_o", swa_w_o, m_swa_w_o, v_swa_w_o, "sq", [sq_place[SQ_SWA_O][:2]], False)
    r_kv = upd("adamw_kv", kv_w, m_kv_w, v_kv_w, "kv", [(1, 0)], False)

    loss_row = jnp.pad(loss_p, ((0, 0), (0, D_MODEL - LANES)))
    d_sink_row = d_sinks[0, :SWA_Q_HEADS]
    part = _pack_small(jnp.concatenate([d_ffn1_0, d_ffn1_1], axis=0), jnp.concatenate([d_mix_0, d_mix_1], axis=0),
                       jnp.concatenate([d_ffn2_0, d_ffn2_1], axis=0), d_kvn, d_final, d_sink_row, loss_row)
    zrow = jnp.zeros((1, D_MODEL), F32)
    small = small_allreduce_adamw(
        part,
        _pack_small(ffn1_norm, mix_norm, ffn2_norm, kv_norm, final_norm, swa_sinks, zrow),
        _pack_small(m_ffn1_norm, m_mix_norm, m_ffn2_norm, m_kv_norm, m_final_norm, m_swa_sinks, zrow),
        _pack_small(v_ffn1_norm, v_mix_norm, v_ffn2_norm, v_kv_norm, v_final_norm, v_swa_sinks, zrow))

    def unpack(p):
        return dict(ffn1_norm=p[0:2], mix_norm=p[2:4], ffn2_norm=p[4:6], kv_norm=p[6], final_norm=p[7],
                    swa_sinks=p[8:9, :SWA_Q_HEADS])

    big = dict(ffn1_w_in=r_ffn1_in, ffn1_w_out=r_ffn1_out, ffn2_w_in=r_ffn2_in, ffn2_w_out=r_ffn2_out,
               sb_w_qkv=r_qkv, sb_w_o=r_sb_o, kv_w=r_kv, swa_w_q=r_swa_q, swa_w_o=r_swa_o)
    order = ["ffn1_norm", "ffn1_w_in", "ffn1_w_out", "mix_norm", "ffn2_norm", "ffn2_w_in", "ffn2_w_out",
             "sb_w_qkv", "sb_w_o", "kv_norm", "kv_w", "swa_w_q", "swa_sinks", "swa_w_o", "final_norm"]
    outs = []
    for kind in range(4):
        sm = unpack(small[kind])
        for nm in order:
            outs.append(big[nm][kind] if nm in big else sm[nm])
    loss = small[0][9, 0]
    return (loss, dx.reshape(x.shape), *outs)
```

```python
import jax
import jax.numpy as jnp
from jax import lax
from jax.experimental import pallas as pl
from jax.experimental.pallas import tpu as pltpu

F32 = jnp.float32
BF16 = jnp.bfloat16
MESH = pl.DeviceIdType.MESH

D_MODEL = 1024
D_FF = 2816
HEAD_DIM = 64
SB_HEADS = 16
SWA_Q_HEADS = 16
SWA_KV_HEADS = 4
WINDOW = 128
ROPE_THETA = 10000.0
RMS_EPS = 1e-6
FFN_RES_SCALE = 0.5
ATTN_SCALE = HEAD_DIM ** -0.5

ADAM_LR = 0.001
ADAM_B1 = 0.9
ADAM_B2 = 0.999
ADAM_EPS = 1e-08
ADAM_WD = 0.01
ADAM_STEP = 10

N_CHIPS = 4
N_DEV = 8
LANES = 128
FF_CHUNK = D_FF // 2
FF_ROWS = D_FF // N_CHIPS
SQ_ROWS = D_MODEL // N_CHIPS
QKV_COLS = 3 * D_MODEL // N_CHIPS
VMEM_LIMIT = 56 * 1024 * 1024
NEG_BIG = -1e30

SLOT_FFN1 = (0, 1)
SLOT_FFN2 = (2, 3)
SQ_SB_O, SQ_SWA_Q, SQ_SWA_O = 0, 1, 2


def _cparams():
    return pltpu.CompilerParams(vmem_limit_bytes=VMEM_LIMIT)


def _dot(a, b):
    return jnp.dot(a, b, preferred_element_type=F32)


def _dot_nt(a, b):
    return lax.dot_general(a, b, (((1,), (1,)), ((), ())), preferred_element_type=F32)


def _dot_tn(a, b):
    return lax.dot_general(a, b, (((0,), (0,)), ((), ())), preferred_element_type=F32)


def _rstd(h):
    return lax.rsqrt(jnp.mean(h * h, axis=-1, keepdims=True) + RMS_EPS)


def _swap32(x):
    n = x.shape[-1]
    lane = lax.broadcasted_iota(jnp.int32, x.shape, x.ndim - 1)
    first = (lane % HEAD_DIM) < (HEAD_DIM // 2)
    return jnp.where(first, pltpu.roll(x, n - HEAD_DIM // 2, x.ndim - 1), pltpu.roll(x, HEAD_DIM // 2, x.ndim - 1))


def _tile_lanes(t, n):
    return t if n == LANES else jnp.tile(t, (1, n // LANES))


FFN_ROWS = 256


def _ffn_w_in_spec(slot):
    return pl.BlockSpec((N_CHIPS, None, D_MODEL, FF_CHUNK), lambda i: (0, slot, 0, 0), pipeline_mode=pl.Buffered(1))


def _ffn_w_out_spec(slot):
    return pl.BlockSpec((N_CHIPS, None, FF_ROWS, D_MODEL), lambda i: (0, slot, 0, 0), pipeline_mode=pl.Buffered(1))


def ffn_fwd(tag, h, g, w_in, w_out, slot, bg_shards=(), bg_lands=()):
    T = h.shape[0]
    tm = FFN_ROWS
    nch = D_FF // FF_CHUNK
    nbg = len(bg_shards)
    nt = T // tm

    def body(h_ref, g_ref, wi_ref, wo_ref, *rest):
        out_ref, gate_ref, up_ref = rest[2 * nbg:2 * nbg + 3]
        wg_s, wu_s = rest[3 * nbg + 3:3 * nbg + 5]
        step = pl.program_id(0)
        if nbg:
            gather = GatherOps([s.shape[1] for s in bg_shards], rest[:nbg], rest[2 * nbg + 3:3 * nbg + 3],
                               *rest[3 * nbg + 5:])
            pl.when(step == 0)(gather.start_ici)
            pl.when(step == nt // 2)(gather.forward_arrivals)

            @pl.when(step == nt - 1)
            def _():
                gather.wait_forwards()
                gather.wait_ici_sends()

        @pl.when(step == 0)
        def _():
            for j in range(nch):
                cols = slice(j * FF_CHUNK, (j + 1) * FF_CHUNK)
                wg_s[:, cols] = wi_ref[j]
                wu_s[:, cols] = wi_ref[nch + j]

        hh = h_ref[...]
        xn = (hh * _rstd(hh) * g_ref[...]).astype(BF16)
        gate = _dot(xn, wg_s[...])
        up = _dot(xn, wu_s[...])
        gate_ref[...] = gate.astype(BF16)
        up_ref[...] = up.astype(BF16)
        a = (gate * jax.nn.sigmoid(gate) * up).astype(BF16)
        out_ref[...] = hh + FFN_RES_SCALE * _dot(a, wo_ref[...].reshape(D_FF, D_MODEL))

    row = pl.BlockSpec((tm, D_MODEL), lambda i: (i, 0))
    ff = pl.BlockSpec((tm, D_FF), lambda i: (i, 0))
    res = pl.pallas_call(
        body,
        name=f"ffn_fwd_{tag}",
        grid=(nt,),
        in_specs=[row, pl.BlockSpec((1, D_MODEL), lambda i: (0, 0)), _ffn_w_in_spec(slot), _ffn_w_out_spec(slot)]
        + [ANY] * (2 * nbg),
        out_specs=[row, ff, ff] + [ANY] * nbg,
        out_shape=[
            jax.ShapeDtypeStruct((T, D_MODEL), F32),
            jax.ShapeDtypeStruct((T, D_FF), BF16),
            jax.ShapeDtypeStruct((T, D_FF), BF16),
        ] + [jax.ShapeDtypeStruct(l.shape, l.dtype) for l in bg_lands],
        input_output_aliases={4 + nbg + t: 3 + t for t in range(nbg)},
        scratch_shapes=[pltpu.VMEM((D_MODEL, D_FF), BF16), pltpu.VMEM((D_MODEL, D_FF), BF16)]
        + [pltpu.SemaphoreType.DMA((nbg, N_PEER_CHIPS))] * (4 if nbg else 0),
        compiler_params=_cparams(),
    )(h, g, w_in, w_out, *bg_shards, *bg_lands)
    return (res[0], res[1], res[2], list(res[3:])) if nbg else tuple(res)


def ffn_bwd(tag, dh, h, g, gate, up, w_in, w_out, slot, proj=None):
    T = dh.shape[0]
    tm = FFN_ROWS
    nch = D_FF // FF_CHUNK

    def body(dh_ref, h_ref, g_ref, gate_ref, up_ref, wi_ref, wo_ref, *rest):
        if proj is not None:
            wp_ref, rest = rest[0], rest[1:]
        dhin_ref, xn_ref, dg_ref, du_ref, a_ref, dhb_ref, dnorm_ref = rest[:7]
        @pl.when(pl.program_id(0) == 0)
        def _():
            dnorm_ref[...] = jnp.zeros_like(dnorm_ref)

        dhh = dh_ref[...]
        dhb = (FFN_RES_SCALE * dhh).astype(BF16)
        dhb_ref[...] = dhb
        dxn = None
        for j in range(nch):
            cols = slice(j * FF_CHUNK, (j + 1) * FF_CHUNK)
            da = _dot_nt(dhb, wo_ref[2 * j:2 * j + 2].reshape(FF_CHUNK, D_MODEL))
            gt = gate_ref[:, cols].astype(F32)
            u = up_ref[:, cols].astype(F32)
            s = jax.nn.sigmoid(gt)
            silu = gt * s
            a_ref[:, cols] = (silu * u).astype(BF16)
            dgate = (da * u * (s * (1.0 + gt * (1.0 - s)))).astype(BF16)
            dup = (da * silu).astype(BF16)
            dg_ref[:, cols] = dgate
            du_ref[:, cols] = dup
            part = _dot_nt(dgate, wi_ref[j]) + _dot_nt(dup, wi_ref[nch + j])
            dxn = part if dxn is None else dxn + part
        hh = h_ref[...]
        gg = g_ref[...]
        r = _rstd(hh)
        hr = hh * r
        xn_ref[...] = (hr * gg).astype(BF16)
        dnorm_ref[...] += jnp.sum(dxn * hr, axis=0, keepdims=True)
        gd = gg * dxn
        dh_in = dhh + r * (gd - hr * jnp.mean(gd * hr, axis=-1, keepdims=True))
        dhin_ref[...] = dh_in
        if proj is not None:
            dyb_ref, da_ref = rest[7:9]
            dyb = dh_in.astype(BF16)
            dyb_ref[...] = dyb
            da_ref[...] = _dot_nt(dyb, wp_ref[...].reshape(D_MODEL, D_MODEL)).astype(BF16)

    row = pl.BlockSpec((tm, D_MODEL), lambda i: (i, 0))
    ff = pl.BlockSpec((tm, D_FF), lambda i: (i, 0))
    vec = pl.BlockSpec((1, D_MODEL), lambda i: (0, 0))
    in_specs = [row, row, vec, ff, ff, _ffn_w_in_spec(slot), _ffn_w_out_spec(slot)]
    args = [dh, h, g, gate, up, w_in, w_out]
    out_specs = [row, row, ff, ff, ff, row, vec]
    out_shape = [
        jax.ShapeDtypeStruct((T, D_MODEL), F32),
        jax.ShapeDtypeStruct((T, D_MODEL), BF16),
        jax.ShapeDtypeStruct((T, D_FF), BF16),
        jax.ShapeDtypeStruct((T, D_FF), BF16),
        jax.ShapeDtypeStruct((T, D_FF), BF16),
        jax.ShapeDtypeStruct((T, D_MODEL), BF16),
        jax.ShapeDtypeStruct((1, D_MODEL), F32),
    ]
    if proj is not None:
        w_sq, t = proj
        in_specs.append(pl.BlockSpec((N_CHIPS, None, SQ_ROWS, D_MODEL), lambda i: (0, t, 0, 0),
                                     pipeline_mode=pl.Buffered(1)))
        args.append(w_sq)
        out_specs += [row, row]
        out_shape += [jax.ShapeDtypeStruct((T, D_MODEL), BF16)] * 2
    return pl.pallas_call(
        body,
        name=f"ffn_bwd_{tag}",
        grid=(T // tm,),
        in_specs=in_specs,
        out_specs=out_specs,
        out_shape=out_shape,
        compiler_params=_cparams(),
    )(*args)


def rms_linear(name, h, g, w, w_spec, w_prep, n_out, tn, *, rope=None, rope_blocks=None, scale=None):
    T = h.shape[0]
    tm = 512
    extra, extra_specs = [], []
    if rope is not None:
        extra += list(rope)
        extra_specs += [pl.BlockSpec((tm, LANES), lambda i, j: (i, 0))] * 2
    if scale is not None:
        extra.append(scale)
        extra_specs.append(pl.BlockSpec((1, tn), lambda i, j: (0, j)))

    def body(h_ref, g_ref, w_ref, *rest):
        rest = list(rest)
        cos_ref = sin_ref = sc_ref = None
        if rope is not None:
            cos_ref, sin_ref = rest[0], rest[1]
            rest = rest[2:]
        if scale is not None:
            sc_ref = rest[0]
            rest = rest[1:]
        out_ref, xn_s = rest

        @pl.when(pl.program_id(1) == 0)
        def _():
            hh = h_ref[...]
            xn_s[...] = (hh * _rstd(hh) * g_ref[...]).astype(BF16)

        y = _dot(xn_s[...], w_prep(w_ref[...]))
        if rope is not None:
            turned = y * _tile_lanes(cos_ref[...], tn) + _swap32(y) * _tile_lanes(sin_ref[...], tn)
            y = turned if rope_blocks is None else jnp.where(pl.program_id(1) < rope_blocks, turned, y)
        if scale is not None:
            y = y * sc_ref[...]
        out_ref[...] = y.astype(BF16)

    return pl.pallas_call(
        body,
        name=name,
        grid=(T // tm, n_out // tn),
        in_specs=[
            pl.BlockSpec((tm, D_MODEL), lambda i, j: (i, 0)),
            pl.BlockSpec((1, D_MODEL), lambda i, j: (0, 0)),
            w_spec,
        ] + extra_specs,
        out_specs=pl.BlockSpec((tm, tn), lambda i, j: (i, j)),
        out_shape=jax.ShapeDtypeStruct((T, n_out), BF16),
        scratch_shapes=[pltpu.VMEM((tm, D_MODEL), BF16)],
        compiler_params=_cparams(),
    )(h, g, w, *extra)


QKV_ROWS = 512


def _qkv_w_spec():
    return pl.BlockSpec((N_CHIPS, D_MODEL, QKV_COLS), lambda i: (0, 0, 0), pipeline_mode=pl.Buffered(1))


def qkv_fwd(h, g, w_qkv, scale):
    T = h.shape[0]
    tm = QKV_ROWS

    def body(h_ref, g_ref, w_ref, sc_ref, out_ref):
        hh = h_ref[...]
        xn = (hh * _rstd(hh) * g_ref[...]).astype(BF16)
        for s in range(N_CHIPS):
            cols = slice(s * QKV_COLS, (s + 1) * QKV_COLS)
            out_ref[:, cols] = (_dot(xn, w_ref[s]) * sc_ref[:, cols]).astype(BF16)

    return pl.pallas_call(
        body,
        name="sb_qkv",
        grid=(T // tm,),
        in_specs=[
            pl.BlockSpec((tm, D_MODEL), lambda i: (i, 0)),
            pl.BlockSpec((1, D_MODEL), lambda i: (0, 0)),
            _qkv_w_spec(),
            pl.BlockSpec((1, 3 * D_MODEL), lambda i: (0, 0)),
        ],
        out_specs=pl.BlockSpec((tm, 3 * D_MODEL), lambda i: (i, 0)),
        out_shape=jax.ShapeDtypeStruct((T, 3 * D_MODEL), BF16),
        compiler_params=_cparams(),
    )(h, g, w_qkv, scale)


def qkv_bwd(dy, w_qkv, h, g, dres):
    T = h.shape[0]
    tm = QKV_ROWS

    def body(dy_ref, w_ref, h_ref, g_ref, dres_ref, dh_ref, xn_ref, dg_ref):
        @pl.when(pl.program_id(0) == 0)
        def _():
            dg_ref[...] = jnp.zeros_like(dg_ref)

        dxn = None
        for s in range(N_CHIPS):
            part = _dot_nt(dy_ref[:, s * QKV_COLS:(s + 1) * QKV_COLS], w_ref[s])
            dxn = part if dxn is None else dxn + part
        hh = h_ref[...]
        gg = g_ref[...]
        r = _rstd(hh)
        hr = hh * r
        xn_ref[...] = (hr * gg).astype(BF16)
        dg_ref[...] += jnp.sum(dxn * hr, axis=0, keepdims=True)
        gd = gg * dxn
        dh_ref[...] = dres_ref[...] + r * (gd - hr * jnp.mean(gd * hr, axis=-1, keepdims=True))

    row = pl.BlockSpec((tm, D_MODEL), lambda i: (i, 0))
    vec = pl.BlockSpec((1, D_MODEL), lambda i: (0, 0))
    return pl.pallas_call(
        body,
        name="sb_qkv_bwd",
        grid=(T // tm,),
        in_specs=[pl.BlockSpec((tm, 3 * D_MODEL), lambda i: (i, 0)), _qkv_w_spec(), row, vec, row],
        out_specs=[row, row, vec],
        out_shape=[
            jax.ShapeDtypeStruct((T, D_MODEL), F32),
            jax.ShapeDtypeStruct((T, D_MODEL), BF16),
            jax.ShapeDtypeStruct((1, D_MODEL), F32),
        ],
        compiler_params=_cparams(),
    )(dy, w_qkv, h, g, dres)


def linear_res(name, a, w_sq, t, res, bg_lands=()):
    T = a.shape[0]
    tm = 512
    nbg = len(bg_lands)
    nt = T // tm

    def body(a_ref, w_ref, res_ref, *rest):
        out_ref = rest[nbg]
        if nbg:
            gather = GatherOps([l.shape[2] for l in bg_lands], None, rest[nbg + 1:2 * nbg + 1], None, None,
                               *rest[2 * nbg + 1:])

            @pl.when(pl.program_id(0) == 0)
            def _():
                gather.start_forwards()

        out_ref[...] = res_ref[...] + _dot(a_ref[...], w_ref[...].reshape(D_MODEL, D_MODEL))
        if nbg:
            @pl.when(pl.program_id(0) == nt - 1)
            def _():
                gather.wait_forwards()

    row = pl.BlockSpec((tm, D_MODEL), lambda i: (i, 0))
    res_ = pl.pallas_call(
        body,
        name=name,
        grid=(nt,),
        in_specs=[row, pl.BlockSpec((N_CHIPS, None, SQ_ROWS, D_MODEL), lambda i: (0, t, 0, 0)), row] + [ANY] * nbg,
        out_specs=[row] + [ANY] * nbg,
        out_shape=[jax.ShapeDtypeStruct((T, D_MODEL), F32)] + [jax.ShapeDtypeStruct(l.shape, l.dtype) for l in bg_lands],
        input_output_aliases={3 + k: 1 + k for k in range(nbg)},
        scratch_shapes=[pltpu.SemaphoreType.DMA((nbg, N_PEER_CHIPS))] * (2 if nbg else 0),
        compiler_params=_cparams(),
    )(a, w_sq, res, *bg_lands)
    return (res_[0], list(res_[1:])) if nbg else res_[0]


def linear_bwd_rms(name, pairs, h, g, dres, nch, tn, tm=256):
    T = h.shape[0]
    npair = len(pairs)

    def body(*refs):
        dy_refs = refs[:npair]
        w_refs = refs[npair:2 * npair]
        h_ref, g_ref, dres_ref, dh_ref, xn_ref, dg_ref, acc_s = refs[2 * npair:]
        i = pl.program_id(0)
        j = pl.program_id(1)

        @pl.when(j == 0)
        def _():
            acc_s[...] = jnp.zeros_like(acc_s)

        @pl.when((i == 0) & (j == 0))
        def _():
            dg_ref[...] = jnp.zeros_like(dg_ref)

        part = None
        for p in range(npair):
            d = _dot_nt(dy_refs[p][...], pairs[p][3](w_refs[p][...]))
            part = d if part is None else part + d
        acc_s[...] += part

        @pl.when(j == nch - 1)
        def _():
            dxn = acc_s[...]
            hh = h_ref[...]
            gg = g_ref[...]
            r = _rstd(hh)
            hr = hh * r
            xn_ref[...] = (hr * gg).astype(BF16)
            dg_ref[...] += jnp.sum(dxn * hr, axis=0, keepdims=True)
            gd = gg * dxn
            dh_ref[...] = dres_ref[...] + r * (gd - hr * jnp.mean(gd * hr, axis=-1, keepdims=True))

    row = pl.BlockSpec((tm, D_MODEL), lambda i, j: (i, 0))
    vec = pl.BlockSpec((1, D_MODEL), lambda i, j: (0, 0))
    return pl.pallas_call(
        body,
        name=name,
        grid=(T // tm, nch),
        in_specs=[pl.BlockSpec((tm, tn), lambda i, j: (i, j))] * npair + [p[2] for p in pairs] + [row, vec, row],
        out_specs=[row, row, vec],
        out_shape=[
            jax.ShapeDtypeStruct((T, D_MODEL), F32),
            jax.ShapeDtypeStruct((T, D_MODEL), BF16),
            jax.ShapeDtypeStruct((1, D_MODEL), F32),
        ],
        scratch_shapes=[pltpu.VMEM((tm, D_MODEL), F32)],
        compiler_params=_cparams(),
    )(*[p[0] for p in pairs], *[p[1] for p in pairs], h, g, dres)


def loss_bwd(h, g, tgt):
    T = h.shape[0]
    tm = 512

    def body(h_ref, g_ref, t_ref, dh_ref, loss_ref, dg_ref):
        @pl.when(pl.program_id(0) == 0)
        def _():
            loss_ref[...] = jnp.zeros_like(loss_ref)
            dg_ref[...] = jnp.zeros_like(dg_ref)

        hh = h_ref[...]
        gg = g_ref[...]
        r = _rstd(hh)
        hr = hh * r
        err = hr * gg - t_ref[...]
        loss_ref[...] += 0.5 * jnp.sum(jnp.mean(err * err, axis=-1, keepdims=True), axis=0, keepdims=True)
        dy = err * (1.0 / D_MODEL)
        dg_ref[...] += jnp.sum(dy * hr, axis=0, keepdims=True)
        gd = gg * dy
        dh_ref[...] = r * (gd - hr * jnp.mean(gd * hr, axis=-1, keepdims=True))

    row = pl.BlockSpec((tm, D_MODEL), lambda i: (i, 0))
    vec = pl.BlockSpec((1, D_MODEL), lambda i: (0, 0))
    return pl.pallas_call(
        body,
        name="loss_bwd",
        grid=(T // tm,),
        in_specs=[row, vec, row],
        out_specs=[row, pl.BlockSpec((1, LANES), lambda i: (0, 0)), vec],
        out_shape=[
            jax.ShapeDtypeStruct((T, D_MODEL), F32),
            jax.ShapeDtypeStruct((1, LANES), F32),
            jax.ShapeDtypeStruct((1, D_MODEL), F32),
        ],
        compiler_params=_cparams(),
    )(h, g, tgt)


DW_TOKENS = 4096


def mm_tn(name, a, b, tk, tn, out_block, out_index, out_shape, prev=None, tt=DW_TOKENS):
    T = a.shape[0]
    ns, r = out_block[1], out_block[3]
    tt = min(tt, T)
    nt = T // tt

    def body(*refs):
        a_ref, b_ref = refs[:2]
        out_ref, copy_ref = refs[-2:]
        t = pl.program_id(2)
        res = _dot_tn(a_ref[...], b_ref[...])

        @pl.when(t == 0)
        def _():
            for u in range(ns):
                out_ref[u] = res[u * r:(u + 1) * r]

        @pl.when(t > 0)
        def _():
            for u in range(ns):
                out_ref[u] += res[u * r:(u + 1) * r]

        @pl.when(t == nt - 1)
        def _():
            copy_ref[...] = out_ref[...].astype(BF16)

    in_specs = [
        pl.BlockSpec((tt, tk), lambda k, n, t: (t, k)),
        pl.BlockSpec((tt, tn), lambda k, n, t: (t, n)),
    ]
    args = [a, b]
    aliases = {}
    if prev is not None:
        in_specs += [pl.BlockSpec(memory_space=pl.ANY)] * 2
        args += list(prev)
        aliases = {2: 0, 3: 1}
    out_spec = pl.BlockSpec(out_block, lambda k, n, t: out_index(k, n))
    return tuple(pl.pallas_call(
        body,
        name=name,
        grid=(a.shape[1] // tk, b.shape[1] // tn, nt),
        in_specs=in_specs,
        out_specs=[out_spec, out_spec],
        out_shape=[jax.ShapeDtypeStruct(out_shape, F32), jax.ShapeDtypeStruct(out_shape, BF16)],
        input_output_aliases=aliases,
        compiler_params=_cparams(),
    )(*args))


SB_BLOCK = 256
SB_QROWS = 256
SB_QROWS_BWD = 256
SB_UNDERFLOW_BITS = 140.0
SB_CHUNK = 128


LOG2E = 1.4426950408889634


def _softplus2(z2):
    sign = jnp.uint32(0x80000000)
    neg_abs = lax.bitcast_convert_type(lax.bitcast_convert_type(z2, jnp.uint32) | sign, F32)
    return jnp.log2(1.0 + jnp.exp2(neg_abs)) + jnp.maximum(z2, 0.0)


def _twice(x):
    return jnp.concatenate([x, x], axis=1)


def sb_fwd(qkv, bg_shards=(), bg_lands=()):
    T = qkv.shape[0]
    tq, tk = SB_QROWS, SB_BLOCK
    ratio = tq // tk
    npair = SB_HEADS // 2
    nbg = len(bg_shards)
    nq = T // tq

    def body(q_ref, k_ref, v_ref, *rest):
        bg_in = rest[:nbg]
        o_ref, tot_ref, first_ref = rest[2 * nbg:2 * nbg + 3]
        bg_out = rest[2 * nbg + 3:3 * nbg + 3]
        acc_s, c_s, z_s, w_s, kmax_s = rest[3 * nbg + 3:3 * nbg + 8]
        p = pl.program_id(0)
        i = pl.program_id(1)
        if nbg:
            gather = GatherOps([s.shape[1] for s in bg_shards], bg_in, bg_out, *rest[3 * nbg + 8:])

            @pl.when((p == 0) & (i == 0))
            def _():
                gather.start_ici()

        @pl.when(i == 0)
        def _():
            kmax_s[...] = jnp.max(jnp.abs(k_ref[...]), axis=0, keepdims=True).astype(F32)

        q = q_ref[...]
        lane = lax.broadcasted_iota(jnp.int32, (tq, LANES), 1)
        first = lane < HEAD_DIM
        zero = jnp.zeros_like(q)
        q_heads = (jnp.where(first, q, zero), jnp.where(first, zero, q))
        row = lax.broadcasted_iota(jnp.int32, (tq, tk), 0)
        col = lax.broadcasted_iota(jnp.int32, (tq, tk), 1)
        visible = [col + r * tk < row for r in range(ratio)]
        krow = lax.broadcasted_iota(jnp.int32, (tk, tk), 0)
        kcol = lax.broadcasted_iota(jnp.int32, (tk, tk), 1)
        from_s = (krow >= kcol).astype(BF16)
        acc_s[...] = jnp.zeros_like(acc_s)
        c_s[...] = jnp.zeros_like(c_s)

        def rows(j):
            return pl.ds(pl.multiple_of(j * tk, tk), tk)

        def logits(j):
            kb = k_ref[rows(j), :]
            for hd in range(2):
                z_s[hd] = _dot_nt(q_heads[hd], kb) * LOG2E

        def flush(j):
            vb = v_ref[rows(j), :]
            for hd in range(2):
                acc_s[hd] += _dot(w_s[hd], vb)

        def block(j, mask=None, flush_block=None):
            if flush_block is not None:
                flush(flush_block)
            chunks = [(hd, slice(r0, r0 + SB_CHUNK)) for hd in range(2) for r0 in range(0, tq, SB_CHUNK)]
            k_next = k_ref[rows(jnp.maximum(j - 1, 0)), :]
            es, sums = [], []
            for hd, rs in chunks:
                z2 = z_s[hd, rs, :]
                z_s[hd, rs, :] = _dot_nt(q_heads[hd][rs, :], k_next) * LOG2E
                if mask is not None:
                    z2 = jnp.where(mask if mask.ndim == 0 else mask[rs, :], z2, NEG_BIG)
                sp = _softplus2(z2)
                c = c_s[hd, rs, :]
                es.append(z2 + _twice(c))
                c_s[hd, rs, :] = c - jnp.sum(sp, axis=1, keepdims=True)
                sums.append(_dot(sp.astype(BF16), from_s))
            for (hd, rs), e, s in zip(chunks, es, sums):
                w_s[hd, rs, :] = jnp.exp2(e - s).astype(BF16)

        z_bound = [LOG2E * jnp.sum(jnp.abs(q_heads[hd].astype(F32)) * kmax_s[...], axis=1, keepdims=True)
                   for hd in range(2)]

        def more_keys_matter():
            top = jnp.maximum(c_s[0] + z_bound[0], c_s[1] + z_bound[1])
            return (jnp.max(top) >= -SB_UNDERFLOW_BITS).astype(jnp.int32)

        assert ratio == 1
        logits(i)
        block(i, visible[0])

        def trip(carry):
            trips, _ = carry
            j = i - 1 - trips
            block(j, flush_block=j + 1)
            return trips + 1, more_keys_matter()

        trips, _ = lax.while_loop(lambda carry: jnp.logical_and(carry[0] < i, carry[1] > 0), trip,
                                  (jnp.int32(0), jnp.int32(1)))
        first_walked = i - trips
        flush(first_walked)
        first_ref[p, i] = first_walked.astype(F32)
        o_ref[...] = jnp.where(first, acc_s[0], acc_s[1]).astype(BF16)
        tot_ref[...] = jnp.where(first, c_s[0], c_s[1])
        if nbg:
            @pl.when((p == npair - 1) & (i == nq - 1))
            def _():
                gather.wait_ici()

    res = pl.pallas_call(
        body,
        name="sb_fwd",
        grid=(npair, nq),
        in_specs=[
            pl.BlockSpec((tq, LANES), lambda p, i: (i, p)),
            pl.BlockSpec((T, LANES), lambda p, i: (0, npair + p)),
            pl.BlockSpec((T, LANES), lambda p, i: (0, 2 * npair + p)),
        ] + [ANY] * (2 * nbg),
        out_specs=[pl.BlockSpec((tq, LANES), lambda p, i: (i, p))] * 2 + [pl.BlockSpec(memory_space=pltpu.SMEM)]
        + [ANY] * nbg,
        out_shape=[jax.ShapeDtypeStruct((T, D_MODEL), BF16), jax.ShapeDtypeStruct((T, D_MODEL), F32),
                   jax.ShapeDtypeStruct((npair, nq), F32)]
        + [jax.ShapeDtypeStruct(l.shape, l.dtype) for l in bg_lands],
        input_output_aliases={3 + nbg + t: 3 + t for t in range(nbg)},
        scratch_shapes=[
            pltpu.VMEM((2, tq, LANES), F32), pltpu.VMEM((2, tq, LANES), F32),
            pltpu.VMEM((2, tq, tk), F32), pltpu.VMEM((2, tq, tk), BF16),
            pltpu.VMEM((1, LANES), F32),
        ] + [pltpu.SemaphoreType.DMA((nbg, N_PEER_CHIPS))] * (2 if nbg else 0),
        compiler_params=_cparams(),
    )(qkv, qkv, qkv, *bg_shards, *bg_lands)
    return res[0], res[1], res[2], list(res[3:])


def sb_bwd(qkv, do, tot, first_block, bg_parts=()):
    T = qkv.shape[0]
    tq, tk = SB_QROWS_BWD, SB_BLOCK
    ratio = tq // tk
    npair = SB_HEADS // 2
    nq = T // tq
    nk = T // tk
    nbg = len(bg_parts)
    assert SB_QROWS == SB_QROWS_BWD

    def body(first_ref, q_ref, k_ref, v_ref, do_ref, tot_ref, *rest):
        bg_in = rest[:nbg]
        dq_ref, dk_ref, dv_ref = rest[nbg:nbg + 3]
        bg_out = rest[nbg + 3:2 * nbg + 3]
        dkt_s, dvt_s, dq_s, rest_s, cg_s, z_s, da_s, dz_s, a_s = rest[2 * nbg + 3:2 * nbg + 12]
        i = pl.program_id(1)
        start = jnp.clip(first_ref[pl.program_id(0), i].astype(jnp.int32), 0, i)
        if nbg:
            @pl.when((pl.program_id(0) == 0) & (i == 0))
            def _():
                for cp in chip_partial_copies(bg_in, bg_out, *rest[2 * nbg + 12:]):
                    cp.start()

        @pl.when(i == 0)
        def _():
            dkt_s[...] = jnp.zeros_like(dkt_s)
            dvt_s[...] = jnp.zeros_like(dvt_s)

        q = q_ref[...]
        do_ = do_ref[...]
        tot_ = tot_ref[...]
        q_t = q.astype(F32).T.astype(BF16)
        do_t = do_.astype(F32).T.astype(BF16)
        lane = lax.broadcasted_iota(jnp.int32, (tq, LANES), 1)
        first = lane < HEAD_DIM
        zero = jnp.zeros_like(q)
        q_heads = (jnp.where(first, q, zero), jnp.where(first, zero, q))
        do_heads = (jnp.where(first, do_, zero), jnp.where(first, zero, do_))
        row = lax.broadcasted_iota(jnp.int32, (tq, tk), 0)
        col = lax.broadcasted_iota(jnp.int32, (tq, tk), 1)
        visible = [col + r * tk < row for r in range(ratio)]
        krow = lax.broadcasted_iota(jnp.int32, (tk, tk), 0)
        kcol = lax.broadcasted_iota(jnp.int32, (tk, tk), 1)
        before = (krow < kcol).astype(BF16)
        from_s = (krow >= kcol).astype(BF16)
        last = ratio * i + ratio - 1
        rest_s[0] = jnp.broadcast_to(tot_[:, 0:1], (tq, LANES))
        rest_s[1] = jnp.broadcast_to(tot_[:, HEAD_DIM:HEAD_DIM + 1], (tq, LANES))
        cg_s[...] = jnp.zeros_like(cg_s)
        dq_s[...] = jnp.zeros_like(dq_s)
        dz_s[...] = jnp.zeros_like(dz_s)
        a_s[...] = jnp.zeros_like(a_s)

        def rows(j):
            return pl.ds(pl.multiple_of(j * tk, tk), tk)

        def logits(j):
            kb = k_ref[rows(j), :]
            vb = v_ref[rows(j), :]
            for hd in range(2):
                z_s[hd] = _dot_nt(q_heads[hd], kb) * LOG2E
                da_s[hd] = _dot_nt(do_heads[hd], vb)

        def flush(j):
            kb = k_ref[rows(j), :]
            for hd in range(2):
                dims = slice(hd * HEAD_DIM, (hd + 1) * HEAD_DIM)
                dq_s[hd] += _dot(dz_s[hd], kb)
                dkt_s[j, dims, :] += _dot(q_t[dims, :], dz_s[hd])
                dvt_s[j, dims, :] += _dot(do_t[dims, :], a_s[hd])

        def block(j, mask=None):
            flush(jnp.maximum(j - 1, 0))
            chunks = [(hd, slice(r0, r0 + SB_CHUNK)) for hd in range(2) for r0 in range(0, tq, SB_CHUNK)]
            nxt = rows(jnp.minimum(j + 1, last))
            k_next = k_ref[nxt, :]
            v_next = v_ref[nxt, :]
            stage1 = []
            for hd, rs in chunks:
                z2 = z_s[hd, rs, :]
                z_s[hd, rs, :] = _dot_nt(q_heads[hd][rs, :], k_next) * LOG2E
                if mask is not None:
                    z2 = jnp.where(mask if mask.ndim == 0 else mask[rs, :], z2, NEG_BIG)
                sp = _softplus2(z2)
                rest = rest_s[hd, rs, :] + jnp.sum(sp, axis=1, keepdims=True)
                rest_s[hd, rs, :] = rest
                stage1.append((z2 + _twice(rest), z2 - sp, _dot(sp.astype(BF16), from_s)))
            stage2 = []
            for (hd, rs), (e, log2_beta, ahead) in zip(chunks, stage1):
                a = jnp.exp2(e - ahead)
                g = a * da_s[hd, rs, :]
                da_s[hd, rs, :] = _dot_nt(do_heads[hd][rs, :], v_next)
                cg = cg_s[hd, rs, :]
                a_s[hd, rs, :] = a.astype(BF16)
                cg_s[hd, rs, :] = cg + jnp.sum(g, axis=1, keepdims=True)
                stage2.append((g, g + _twice(cg), log2_beta, _dot(g.astype(BF16), before)))
            for (hd, rs), (g, g_from, log2_beta, g_before) in zip(chunks, stage2):
                dz_s[hd, rs, :] = (g - jnp.exp2(log2_beta) * (g_from + g_before)).astype(BF16)

        assert ratio == 1
        logits(start)

        @pl.loop(start, i)
        def _(j):
            block(j)

        block(i, visible[0])
        flush(last)
        dq_ref[...] = (jnp.where(first, dq_s[0], dq_s[1]) * ATTN_SCALE).astype(BF16)

        @pl.when(i == nq - 1)
        def _():
            @pl.loop(0, nk)
            def _(b):
                dk_ref[rows(b), :] = dkt_s[b].T.astype(BF16)
                dv_ref[rows(b), :] = dvt_s[b].T.astype(BF16)

        if nbg:
            @pl.when((pl.program_id(0) == npair - 1) & (i == nq - 1))
            def _():
                for cp in chip_partial_copies(bg_in, bg_out, *rest[2 * nbg + 12:]):
                    cp.wait()

    qblk = pl.BlockSpec((tq, LANES), lambda p, i: (i, p))
    full = pl.BlockSpec((T, LANES), lambda p, i: (0, p))
    res = pl.pallas_call(
        body,
        name="sb_bwd",
        grid=(npair, nq),
        in_specs=[
            pl.BlockSpec(memory_space=pltpu.SMEM),
            qblk,
            pl.BlockSpec((T, LANES), lambda p, i: (0, npair + p)),
            pl.BlockSpec((T, LANES), lambda p, i: (0, 2 * npair + p)),
            qblk, qblk,
        ] + [ANY] * nbg,
        out_specs=[qblk, full, full] + [ANY] * nbg,
        out_shape=[jax.ShapeDtypeStruct((T, D_MODEL), BF16)] * 3
        + [jax.ShapeDtypeStruct(b.shape, b.dtype) for b in bg_parts],
        scratch_shapes=[
            pltpu.VMEM((nk, LANES, tk), F32), pltpu.VMEM((nk, LANES, tk), F32),
            pltpu.VMEM((2, tq, LANES), F32), pltpu.VMEM((2, tq, LANES), F32), pltpu.VMEM((2, tq, LANES), F32),
            pltpu.VMEM((2, tq, tk), F32), pltpu.VMEM((2, tq, tk), F32),
            pltpu.VMEM((2, tq, tk), BF16), pltpu.VMEM((2, tq, tk), BF16),
        ] + [pltpu.SemaphoreType.DMA((nbg, N_PEER_CHIPS))] * (2 if nbg else 0),
        compiler_params=_cparams(),
    )(first_block, qkv, qkv, qkv, do, tot, *bg_parts)
    return res[0], res[1], res[2], list(res[3:])


def _swa_valid(n):
    qi = lax.broadcasted_iota(jnp.int32, (WINDOW, 2 * WINDOW), 0)
    ki = lax.broadcasted_iota(jnp.int32, (WINDOW, 2 * WINDOW), 1)
    diff = qi + WINDOW - ki
    return (diff >= 0) & (diff < WINDOW) & ((n > 0) | (ki >= WINDOW))


def _to_half(x, first, src, dst):
    keep = first if src == 0 else jnp.logical_not(first)
    x = jnp.where(keep, x, jnp.zeros_like(x))
    if src != dst:
        x = pltpu.roll(x.astype(F32), HEAD_DIM, 1).astype(BF16)
    return x


SWA_GROUP = SWA_Q_HEADS // SWA_KV_HEADS


def _swa_cols(h):
    return slice((h // 2) * LANES, (h // 2 + 1) * LANES)


def _swa_kv_pair(h):
    return (h // SWA_GROUP) // 2


def _swa_kv_half(h):
    return (h // SWA_GROUP) % 2


def _kv_band(prev_ref, cur_ref, pb):
    cols = slice(pb * LANES, (pb + 1) * LANES)
    return jnp.concatenate([prev_ref[:, cols], cur_ref[:, cols]], axis=0)


def _swa_specs(T):
    nb = T // WINDOW
    kv_w = SWA_KV_HEADS * HEAD_DIM
    qrow = pl.BlockSpec((WINDOW, D_MODEL), lambda n: (n, 0))
    kv = [pl.BlockSpec((WINDOW, kv_w), lambda n, col=col, back=back: (jnp.maximum(n - back, 0), col))
          for col in (0, 1) for back in (0, 1)]
    smem = pl.BlockSpec(memory_space=pltpu.SMEM)
    return nb, qrow, kv, smem


def swa_fwd(q, kv, sinks):
    T = q.shape[0]
    nb, qrow, kv_specs, smem = _swa_specs(T)

    def body(sink_ref, q_ref, kc_ref, kp_ref, vc_ref, vp_ref, o_ref, lse_ref):
        n = pl.program_id(0)
        lane = lax.broadcasted_iota(jnp.int32, (WINDOW, LANES), 1)
        first = lane < HEAD_DIM
        valid = _swa_valid(n)
        k2 = [_kv_band(kp_ref, kc_ref, pb) for pb in range(SWA_KV_HEADS // 2)]
        v2 = [_kv_band(vp_ref, vc_ref, pb) for pb in range(SWA_KV_HEADS // 2)]
        logits = [jnp.where(valid, _dot_nt(_to_half(q_ref[:, _swa_cols(h)], first, h % 2, _swa_kv_half(h)),
                                            k2[_swa_kv_pair(h)]), NEG_BIG) for h in range(SWA_Q_HEADS)]
        probs = []
        lse_acc = jnp.zeros((WINDOW, LANES), F32)
        for h, s in enumerate(logits):
            sink = sink_ref[h]
            m = jnp.maximum(jnp.max(s, axis=1, keepdims=True), sink)
            p = jnp.exp(s - m)
            den = jnp.sum(p, axis=1, keepdims=True) + jnp.exp(sink - m)
            probs.append((p / den).astype(BF16))
            lse_acc = jnp.where(lane == h, m + jnp.log(den), lse_acc)
        outs = []
        for h, p in enumerate(probs):
            o = _dot(p, v2[_swa_kv_pair(h)])
            outs.append(pltpu.roll(o, HEAD_DIM, 1) if h % 2 != _swa_kv_half(h) else o)
        for pair in range(SWA_Q_HEADS // 2):
            o_ref[:, _swa_cols(2 * pair)] = jnp.where(first, outs[2 * pair], outs[2 * pair + 1]).astype(BF16)
        lse_ref[...] = lse_acc

    return pl.pallas_call(
        body,
        name="swa_fwd",
        grid=(nb,),
        in_specs=[smem, qrow] + kv_specs,
        out_specs=[qrow, pl.BlockSpec((WINDOW, LANES), lambda n: (n, 0))],
        out_shape=[jax.ShapeDtypeStruct((T, D_MODEL), BF16), jax.ShapeDtypeStruct((T, LANES), F32)],
        compiler_params=_cparams(),
    )(sinks, q, kv, kv, kv, kv)


def swa_bwd(q, kv, sinks, do, o, lse, cos, sin):
    T = q.shape[0]
    nb, qrow, kv_specs, smem = _swa_specs(T)
    kv_w = SWA_KV_HEADS * HEAD_DIM

    def body(sink_ref, q_ref, kc_ref, kp_ref, vc_ref, vp_ref, do_ref, o_ref, lse_ref, cos_ref, sin_ref,
             dq_ref, own_ref, prv_ref, dsink_ref):
        n = pl.program_id(0)

        @pl.when(n == 0)
        def _():
            dsink_ref[...] = jnp.zeros_like(dsink_ref)

        lane = lax.broadcasted_iota(jnp.int32, (WINDOW, LANES), 1)
        lane1 = lax.broadcasted_iota(jnp.int32, (1, LANES), 1)
        first = lane < HEAD_DIM
        valid = _swa_valid(n)
        cos_ = cos_ref[...]
        sin_ = sin_ref[...]
        k2 = [_kv_band(kp_ref, kc_ref, pb) for pb in range(SWA_KV_HEADS // 2)]
        v2 = [_kv_band(vp_ref, vc_ref, pb) for pb in range(SWA_KV_HEADS // 2)]
        q_t = q_ref[...].astype(F32).T.astype(BF16)
        do_t = do_ref[...].astype(F32).T.astype(BF16)
        stage1 = []
        for h in range(SWA_Q_HEADS):
            a, b, pb = h % 2, _swa_kv_half(h), _swa_kv_pair(h)
            qh = _to_half(q_ref[:, _swa_cols(h)], first, a, b)
            doh = _to_half(do_ref[:, _swa_cols(h)], first, a, b)
            stage1.append((jnp.where(valid, _dot_nt(qh, k2[pb]), NEG_BIG), _dot_nt(doh, v2[pb])))
        deltas = []
        for pair in range(SWA_Q_HEADS // 2):
            prod = do_ref[:, _swa_cols(2 * pair)].astype(F32) * o_ref[:, _swa_cols(2 * pair)].astype(F32)
            deltas += [jnp.sum(jnp.where(first, prod, 0.0), axis=1, keepdims=True),
                       jnp.sum(jnp.where(first, 0.0, prod), axis=1, keepdims=True)]
        stage2 = []
        dsink = jnp.zeros((1, LANES), F32)
        for h, (s, dp) in enumerate(stage1):
            lse_h = lse_ref[:, h:h + 1]
            p = jnp.exp(s - lse_h)
            delta = deltas[h]
            p_sink = jnp.exp(sink_ref[h] - lse_h)
            dsink = dsink + jnp.where(lane1 == h, -jnp.sum(p_sink * delta, axis=0, keepdims=True), 0.0)
            stage2.append(((p * (dp - delta)).astype(BF16), p.astype(BF16)))
        dqs = []
        dk_t = [None] * SWA_KV_HEADS
        dv_t = [None] * SWA_KV_HEADS
        for h, (ds, pb16) in enumerate(stage2):
            kvh = h // SWA_GROUP
            dims = slice(h * HEAD_DIM, (h + 1) * HEAD_DIM)
            dq = _dot(ds, k2[_swa_kv_pair(h)])
            dqs.append(pltpu.roll(dq, HEAD_DIM, 1) if h % 2 != _swa_kv_half(h) else dq)
            dk_h = _dot(q_t[dims, :], ds)
            dv_h = _dot(do_t[dims, :], pb16)
            dk_t[kvh] = dk_h if dk_t[kvh] is None else dk_t[kvh] + dk_h
            dv_t[kvh] = dv_h if dv_t[kvh] is None else dv_t[kvh] + dv_h
        for pair in range(SWA_Q_HEADS // 2):
            dqp = jnp.where(first, dqs[2 * pair], dqs[2 * pair + 1])
            dq_ref[:, _swa_cols(2 * pair)] = ((dqp * cos_ + _swap32(dqp * sin_)) * ATTN_SCALE).astype(BF16)
        for pb in range(SWA_KV_HEADS // 2):
            dk2 = jnp.concatenate([dk_t[2 * pb], dk_t[2 * pb + 1]], axis=0).T
            dv2 = jnp.concatenate([dv_t[2 * pb], dv_t[2 * pb + 1]], axis=0).T
            kcols = slice(pb * LANES, (pb + 1) * LANES)
            vcols = slice(kv_w + pb * LANES, kv_w + (pb + 1) * LANES)
            prv_ref[:, kcols] = dk2[:WINDOW]
            own_ref[:, kcols] = dk2[WINDOW:]
            prv_ref[:, vcols] = dv2[:WINDOW]
            own_ref[:, vcols] = dv2[WINDOW:]
        dsink_ref[...] += dsink

    tab = pl.BlockSpec((WINDOW, LANES), lambda n: (n, 0))
    kvrow = pl.BlockSpec((WINDOW, 2 * kv_w), lambda n: (n, 0))
    return pl.pallas_call(
        body,
        name="swa_bwd",
        grid=(nb,),
        in_specs=[smem, qrow] + kv_specs + [qrow, qrow, tab, tab, tab],
        out_specs=[qrow, kvrow, kvrow, pl.BlockSpec((1, LANES), lambda n: (0, 0))],
        out_shape=[
            jax.ShapeDtypeStruct((T, D_MODEL), BF16),
            jax.ShapeDtypeStruct((T, 2 * kv_w), F32),
            jax.ShapeDtypeStruct((T, 2 * kv_w), F32),
            jax.ShapeDtypeStruct((1, LANES), F32),
        ],
        compiler_params=_cparams(),
    )(sinks, q, kv, kv, kv, kv, do, o, lse, cos, sin)


def kv_grad_combine(own, prv, cos, sin):
    T = own.shape[0]
    nb = T // WINDOW
    kv_w = SWA_KV_HEADS * HEAD_DIM

    def body(own_ref, nxt_ref, cos_ref, sin_ref, out_ref):
        n = pl.program_id(0)
        nxt = jnp.where(n + 1 < nb, nxt_ref[...], 0.0)
        tot = own_ref[...] + nxt
        dk = tot[:, :kv_w]
        c = _tile_lanes(cos_ref[...], kv_w)
        s = _tile_lanes(sin_ref[...], kv_w)
        out_ref[:, :kv_w] = (dk * c + _swap32(dk * s)).astype(BF16)
        out_ref[:, kv_w:] = tot[:, kv_w:].astype(BF16)

    tab = pl.BlockSpec((WINDOW, LANES), lambda n: (n, 0))
    kvrow = pl.BlockSpec((WINDOW, 2 * kv_w), lambda n: (n, 0))
    return pl.pallas_call(
        body,
        name="kv_grad_combine",
        grid=(nb,),
        in_specs=[kvrow, pl.BlockSpec((WINDOW, 2 * kv_w), lambda n: (jnp.minimum(n + 1, nb - 1), 0)), tab, tab],
        out_specs=kvrow,
        out_shape=jax.ShapeDtypeStruct((T, 2 * kv_w), BF16),
        compiler_params=_cparams(),
    )(own, prv, cos, sin)


ANY = pl.BlockSpec(memory_space=pl.ANY)


def _place():
    x, y, c = lax.axis_index("x"), lax.axis_index("y"), lax.axis_index("c")
    other_chips = [(1 - x, y), (x, 1 - y), (1 - x, 1 - y)]
    return x, y, c, 2 * x + y, other_chips


N_PEER_CHIPS = N_CHIPS - 1


class GatherOps:
    def __init__(self, rows, shards, lands, ici_send, ici_recv, d2d_send=None, d2d_recv=None):
        self.rows, self.shards, self.lands = rows, shards, lands
        self.ici_send, self.ici_recv, self.d2d_send, self.d2d_recv = ici_send, ici_recv, d2d_send, d2d_recv
        self.x, self.y, self.c, self.me, self.chips = _place()
        self.pairs = [(t, jdx) for t in range(len(rows)) for jdx in range(N_PEER_CHIPS)]

    def _half(self, ref, t, which):
        r = self.rows[t] // 2
        return ref.at[:, pl.ds(pl.multiple_of(which * r, 16), r), :]

    def _ici(self, t, jdx):
        px, py = self.chips[jdx]
        return pltpu.make_async_remote_copy(
            src_ref=self._half(self.shards[t], t, self.c), dst_ref=self._half(self.lands[t].at[self.me], t, self.c),
            send_sem=self.ici_send.at[t, jdx], recv_sem=self.ici_recv.at[t, jdx],
            device_id=(px, py, self.c), device_id_type=MESH)

    def _landed(self, t, jdx):
        px, py = self.chips[jdx]
        blk = self._half(self.lands[t].at[2 * px + py], t, self.c)
        return pltpu.make_async_remote_copy(
            src_ref=blk, dst_ref=blk, send_sem=self.ici_send.at[t, jdx], recv_sem=self.ici_recv.at[t, jdx],
            device_id=(px, py, self.c), device_id_type=MESH)

    def _d2d(self, t, jdx, which):
        px, py = self.chips[jdx]
        blk = self._half(self.lands[t].at[2 * px + py], t, which)
        return pltpu.make_async_remote_copy(
            src_ref=blk, dst_ref=blk, send_sem=self.d2d_send.at[t, jdx], recv_sem=self.d2d_recv.at[t, jdx],
            device_id=(self.x, self.y, 1 - self.c), device_id_type=MESH)

    def start_ici(self):
        for t, jdx in self.pairs:
            self._ici(t, jdx).start()

    def wait_ici(self):
        for t, jdx in self.pairs:
            self._landed(t, jdx).wait_recv()
        self.wait_ici_sends()

    def wait_ici_sends(self):
        for t, jdx in self.pairs:
            self._ici(t, jdx).wait_send()

    def forward_arrivals(self):
        for t, jdx in self.pairs:
            self._landed(t, jdx).wait_recv()
            self._d2d(t, jdx, self.c).start()

    def start_forwards(self):
        for t, jdx in self.pairs:
            self._d2d(t, jdx, self.c).start()

    def wait_forwards(self):
        for t, jdx in self.pairs:
            self._d2d(t, jdx, 1 - self.c).wait_recv()
            self._d2d(t, jdx, self.c).wait_send()


def all_gather_weights(shards, lands):
    n = len(shards)
    rows = [s.shape[1] for s in shards]

    def body(*refs):
        ins, outs = refs[:n], refs[2 * n:3 * n]
        ops = GatherOps(rows, ins, outs, *refs[3 * n:])
        ops.start_ici()
        ops.forward_arrivals()
        ops.wait_forwards()
        ops.wait_ici_sends()

    return pl.pallas_call(
        body,
        name="all_gather_weights",
        in_specs=[ANY] * (2 * n),
        out_specs=[ANY] * n,
        out_shape=[jax.ShapeDtypeStruct(l.shape, l.dtype) for l in lands],
        input_output_aliases={n + t: t for t in range(n)},
        scratch_shapes=[pltpu.SemaphoreType.DMA((n, N_PEER_CHIPS))] * 4,
    )(*shards, *lands)


def place_own_shard(name, shard, chip):
    nl, r, c = shard.shape

    def body(chip_ref, s_ref, o_ref):
        o_ref[...] = s_ref[...]

    return pl.pallas_call(
        body, name=name,
        grid_spec=pltpu.PrefetchScalarGridSpec(
            num_scalar_prefetch=1, grid=(nl,),
            in_specs=[pl.BlockSpec((None, r, c), lambda l, chip_ref: (l, 0, 0))],
            out_specs=pl.BlockSpec((None, None, r, c), lambda l, chip_ref: (chip_ref[0], l, 0, 0))),
        out_shape=jax.ShapeDtypeStruct((N_CHIPS,) + shard.shape, shard.dtype), compiler_params=_cparams(),
    )(chip, shard)


def exchange_halves(name, slabs):
    n = len(slabs)

    def body(*refs):
        ins, theirs = refs[:n], refs[n:2 * n]
        send_sems, recv_sems = refs[2 * n:]
        x, y, c, _, _ = _place()
        copies = []
        for t in range(n):
            cp = pltpu.make_async_remote_copy(
                src_ref=ins[t].at[1 - c], dst_ref=theirs[t], send_sem=send_sems.at[t],
                recv_sem=recv_sems.at[t], device_id=(x, y, 1 - c), device_id_type=MESH)
            cp.start()
            copies.append(cp)
        for cp in copies:
            cp.wait()

    return pl.pallas_call(
        body,
        name=name,
        in_specs=[ANY] * n,
        out_specs=[ANY] * n,
        out_shape=[jax.ShapeDtypeStruct(s.shape[1:], s.dtype) for s in slabs],
        scratch_shapes=[pltpu.SemaphoreType.DMA((n,)), pltpu.SemaphoreType.DMA((n,))],
    )(*slabs)


def chip_partial_copies(ins, outs, send_sems, recv_sems):
    _, _, c, me, chips = _place()
    return [pltpu.make_async_remote_copy(
        src_ref=ins[t].at[2 * px + py], dst_ref=outs[t].at[me], send_sem=send_sems.at[t, jdx],
        recv_sem=recv_sems.at[t, jdx], device_id=(px, py, c), device_id_type=MESH)
        for t in range(len(ins)) for jdx, (px, py) in enumerate(chips)]


def exchange_chip_partials(name, parts):
    n = len(parts)

    def body(*refs):
        copies = chip_partial_copies(refs[:n], refs[n:2 * n], *refs[2 * n:])
        for cp in copies:
            cp.start()
        for cp in copies:
            cp.wait()

    return pl.pallas_call(
        body,
        name=name,
        in_specs=[ANY] * n,
        out_specs=[ANY] * n,
        out_shape=[jax.ShapeDtypeStruct(p.shape, p.dtype) for p in parts],
        scratch_shapes=[pltpu.SemaphoreType.DMA((n, 3)), pltpu.SemaphoreType.DMA((n, 3))],
    )(*parts)


def share_reduced_halves(name, halves):
    n = len(halves)

    def body(*refs):
        ins, outs = refs[:n], refs[n:2 * n]
        send_sems, recv_sems = refs[2 * n:]
        x, y, c, _, _ = _place()
        copies = []
        for t in range(n):
            cp = pltpu.make_async_remote_copy(
                src_ref=ins[t], dst_ref=outs[t], send_sem=send_sems.at[t],
                recv_sem=recv_sems.at[t], device_id=(x, y, 1 - c), device_id_type=MESH)
            cp.start()
            copies.append(cp)
        for cp in copies:
            cp.wait()

    return pl.pallas_call(
        body,
        name=name,
        in_specs=[ANY] * n,
        out_specs=[ANY] * n,
        out_shape=[jax.ShapeDtypeStruct(h.shape, h.dtype) for h in halves],
        scratch_shapes=[pltpu.SemaphoreType.DMA((n,)), pltpu.SemaphoreType.DMA((n,))],
    )(*halves)


def _row_tile(r, c):
    tr = r
    while tr * c * 4 > (3 << 19) and tr % 16 == 0:
        tr //= 2
    return tr


def add_sibling(name, slab, theirs, core):
    _, ns, slots, r, c = slab.shape
    tr = _row_tile(r, c)

    def body(core_ref, a_ref, b_ref, o_ref):
        o_ref[...] = (a_ref[...] + b_ref[...]).astype(BF16)

    blk = pl.BlockSpec((None, None, tr, c), lambda s, l, i, core_ref: (s, l, i, 0))
    return pl.pallas_call(
        body, name=name,
        grid_spec=pltpu.PrefetchScalarGridSpec(
            num_scalar_prefetch=1, grid=(ns, slots, r // tr),
            in_specs=[pl.BlockSpec((None, None, None, tr, c), lambda s, l, i, core_ref: (core_ref[0], s, l, i, 0)), blk],
            out_specs=blk),
        out_shape=jax.ShapeDtypeStruct(theirs.shape, BF16), compiler_params=_cparams(),
    )(core, slab, theirs)


def sum_chips(name, recv, own, chip):
    _, slots, r, c = recv.shape
    tr = _row_tile(r, c)

    def body(chip_ref, r0, r1, r2, r3, own_ref, o_ref):
        me = chip_ref[0]
        mine = own_ref[...]
        terms = [jnp.where(me == s, mine, rr[...]).astype(F32) for s, rr in enumerate((r0, r1, r2, r3))]
        o_ref[...] = ((terms[0] + terms[1]) + terms[2]) + terms[3]

    def src(s):
        return pl.BlockSpec((None, None, tr, c),
                            lambda l, i, chip_ref: (jnp.where(chip_ref[0] == s, (s + 1) % N_CHIPS, s), l, i, 0))

    return pl.pallas_call(
        body, name=name,
        grid_spec=pltpu.PrefetchScalarGridSpec(
            num_scalar_prefetch=1, grid=(slots, r // tr),
            in_specs=[src(0), src(1), src(2), src(3),
                      pl.BlockSpec((None, None, tr, c), lambda l, i, chip_ref: (chip_ref[0], l, i, 0))],
            out_specs=pl.BlockSpec((None, tr, c), lambda l, i, chip_ref: (l, i, 0))),
        out_shape=jax.ShapeDtypeStruct((slots, r, c), F32), compiler_params=_cparams(),
    )(chip, recv, recv, recv, recv, own)


def _adamw_math(w, g, m, v):
    m = ADAM_B1 * m + (1.0 - ADAM_B1) * g
    v = ADAM_B2 * v + (1.0 - ADAM_B2) * (g * g)
    m_hat = m / (1.0 - ADAM_B1 ** ADAM_STEP)
    v_hat = v / (1.0 - ADAM_B2 ** ADAM_STEP)
    delta = -ADAM_LR * (m_hat / (jnp.sqrt(v_hat) + ADAM_EPS) + ADAM_WD * w)
    return delta, m, v


def adamw_shard(name, w, m, v, g_pairs, core, slots, row_halves):
    n = w.shape[0]
    assert n == len(g_pairs)
    _, r, c = g_pairs[0][0].shape
    tr = _row_tile(r, c)
    nr = r // tr

    def body(core_ref, w_ref, m_ref, v_ref, *rest):
        g_refs, (go_ref, d_ref, mo_ref, vo_ref) = rest[:2 * n], rest[2 * n:]
        mine = pl.program_id(1) == core_ref[0]
        g = jnp.where(mine, g_refs[0][...], g_refs[1][...])
        for l in range(1, n):
            g = jnp.where(pl.program_id(0) == l, jnp.where(mine, g_refs[2 * l][...], g_refs[2 * l + 1][...]), g)
        delta, mm, vv = _adamw_math(w_ref[...], g, m_ref[...], v_ref[...])
        go_ref[...] = g
        d_ref[...] = delta
        mo_ref[...] = mm
        vo_ref[...] = vv

    if row_halves:
        wspec = pl.BlockSpec((None, tr, c), lambda l, h, i, core_ref: (l, h * nr + i, 0))
    else:
        wspec = pl.BlockSpec((None, tr, c), lambda l, h, i, core_ref: (l, i, h))
    def gspec(slot):
        return pl.BlockSpec((None, tr, c), lambda l, h, i, core_ref: (slot, i, 0))

    shp = jax.ShapeDtypeStruct(w.shape, F32)
    return pl.pallas_call(
        body, name=name,
        grid_spec=pltpu.PrefetchScalarGridSpec(
            num_scalar_prefetch=1, grid=(n, 2, nr),
            in_specs=[wspec, wspec, wspec] + [gspec(s) for s in slots for _ in range(2)], out_specs=[wspec] * 4),
        out_shape=[shp] * 4, compiler_params=_cparams(),
    )(core, w, m, v, *[g for pair in g_pairs for g in pair])


SMALL_ROWS = 16


def small_allreduce_adamw(part, w, m, v):
    def body(p_ref, w_ref, m_ref, v_ref, g_ref, d_ref, mo_ref, vo_ref, buf, send_sems, recv_sems):
        x, y, c, _, _ = _place()
        me = 4 * x + 2 * y + c
        buf[me] = p_ref[...]
        copies = []
        for k in range(1, N_DEV):
            kx, ky, kc = (k >> 2) & 1, (k >> 1) & 1, k & 1
            peer = (x ^ kx, y ^ ky, c ^ kc)
            cp = pltpu.make_async_remote_copy(
                src_ref=p_ref, dst_ref=buf.at[me], send_sem=send_sems.at[k - 1],
                recv_sem=recv_sems.at[k - 1], device_id=peer, device_id_type=MESH)
            cp.start()
            copies.append(cp)
        for cp in copies:
            cp.wait()
        g = buf[0]
        for dev in range(1, N_DEV):
            g = g + buf[dev]
        delta, mm, vv = _adamw_math(w_ref[...], g, m_ref[...], v_ref[...])
        g_ref[...] = g
        d_ref[...] = delta
        mo_ref[...] = mm
        vo_ref[...] = vv

    vm = pl.BlockSpec(memory_space=pltpu.VMEM)
    shp = jax.ShapeDtypeStruct(part.shape, F32)
    return pl.pallas_call(
        body, name="small_allreduce_adamw",
        in_specs=[vm] * 4, out_specs=[vm] * 4, out_shape=[shp] * 4,
        scratch_shapes=[
            pltpu.VMEM((N_DEV,) + part.shape, F32),
            pltpu.SemaphoreType.DMA((N_DEV - 1,)), pltpu.SemaphoreType.DMA((N_DEV - 1,)),
        ],
    )(part, w, m, v)


def _rope_tables(T):
    half = HEAD_DIM // 2
    inv_freq = ROPE_THETA ** (-jnp.arange(half, dtype=F32) / half)
    ang = jnp.arange(T).astype(F32)[:, None] * inv_freq[None, :]
    cos = jnp.tile(jnp.cos(ang), (1, LANES // half))
    sin = jnp.tile(jnp.sin(ang), (1, LANES // half))
    lane = jnp.arange(LANES)
    sign = jnp.where((lane % HEAD_DIM) < half, -1.0, 1.0).astype(F32)
    return cos, sin * sign[None, :]


def _pack_small(ffn1, mix, ffn2, kvn, fin, sinks, loss_row):
    sink_row = jnp.pad(sinks.reshape(1, SWA_Q_HEADS), ((0, 0), (0, D_MODEL - SWA_Q_HEADS)))
    rows = jnp.concatenate([ffn1, mix, ffn2, kvn.reshape(1, -1), fin.reshape(1, -1), sink_row, loss_row], axis=0)
    return jnp.concatenate([rows, jnp.zeros((SMALL_ROWS - rows.shape[0], D_MODEL), F32)], axis=0)


def kernel(x, ffn1_norm, ffn1_w_in, ffn1_w_out, mix_norm, ffn2_norm, ffn2_w_in, ffn2_w_out, sb_w_qkv, sb_w_o, kv_norm, kv_w, swa_w_q, swa_sinks, swa_w_o, final_norm, loss_target, m_ffn1_norm, m_ffn1_w_in, m_ffn1_w_out, m_mix_norm, m_ffn2_norm, m_ffn2_w_in, m_ffn2_w_out, m_sb_w_qkv, m_sb_w_o, m_kv_norm, m_kv_w, m_swa_w_q, m_swa_sinks, m_swa_w_o, m_final_norm, v_ffn1_norm, v_ffn1_w_in, v_ffn1_w_out, v_mix_norm, v_ffn2_norm, v_ffn2_w_in, v_ffn2_w_out, v_sb_w_qkv, v_sb_w_o, v_kv_norm, v_kv_w, v_swa_w_q, v_swa_sinks, v_swa_w_o, v_final_norm):
    T = x.shape[1]
    kv_cols = SWA_KV_HEADS * HEAD_DIM
    x2 = x.reshape(T, D_MODEL)
    tgt = loss_target.reshape(T, D_MODEL)
    cos, sin = _rope_tables(T)

    w_in_l = jnp.concatenate([ffn1_w_in, ffn2_w_in], axis=0).astype(BF16)
    w_out_l = jnp.concatenate([ffn1_w_out, ffn2_w_out], axis=0).astype(BF16)
    sq_l = jnp.concatenate([sb_w_o, swa_w_q, swa_w_o], axis=0).astype(BF16)
    qkv_l = sb_w_qkv[0].astype(BF16)
    kvw_l = kv_w.astype(BF16)
    core = lax.axis_index("c").astype(jnp.int32).reshape(1)
    chip = (2 * lax.axis_index("x") + lax.axis_index("y")).astype(jnp.int32).reshape(1)
    early = [w_in_l[:1], w_out_l[:1]]
    mid = [sq_l, qkv_l[None]]
    late = [w_in_l[1:], w_out_l[1:], kvw_l[None]]
    early_lands = [place_own_shard(f"own_early_{t}", s, chip) for t, s in enumerate(early)]
    mid_lands = [place_own_shard(f"own_mid_{t}", s, chip) for t, s in enumerate(mid)]
    late_lands = [place_own_shard(f"own_late_{t}", s, chip) for t, s in enumerate(late)]
    w_in0, w_out0 = all_gather_weights(early, early_lands)

    def ffn_w(slot):
        return (w_in0, w_out0, 0) if slot == 0 else (w_in_r, w_out_r, slot - 1)

    def vec(a, i):
        return a[i].reshape(1, D_MODEL)

    ident = lambda w: w
    sq_prep = lambda w: w.reshape(D_MODEL, w.shape[-1])
    qscale = jnp.concatenate([jnp.full((1, D_MODEL), ATTN_SCALE, F32), jnp.ones((1, 2 * D_MODEL), F32)], axis=1)
    swa_scale = jnp.full((1, D_MODEL), ATTN_SCALE, F32)
    sinks = swa_sinks.reshape(SWA_Q_HEADS)

    h1, gate1, up1, (w_sq, w_qkv) = ffn_fwd("l0a", x2, vec(ffn1_norm, 0), *ffn_w(SLOT_FFN1[0]), mid, mid_lands)
    w_qkv = w_qkv.reshape(N_CHIPS, D_MODEL, QKV_COLS)
    qkv = qkv_fwd(h1, vec(mix_norm, 0), w_qkv, qscale)
    o_sb, tot, sb_first, late_lands = sb_fwd(qkv, late, late_lands)
    h2, (w_in_r, w_out_r, w_kv) = linear_res("sb_out", o_sb, w_sq, SQ_SB_O, h1, late_lands)
    w_kv = w_kv.reshape(D_MODEL, 2 * kv_cols)
    h3, gate2, up2 = ffn_fwd("l0b", h2, vec(ffn2_norm, 0), *ffn_w(SLOT_FFN2[0]))
    kvn = kv_norm.reshape(1, D_MODEL)
    kv_sw = rms_linear("kv_proj", h3, kvn, w_kv, pl.BlockSpec((D_MODEL, kv_cols), lambda i, j: (0, j)), ident,
                       2 * kv_cols, kv_cols, rope=(cos, sin), rope_blocks=1)
    h4, gate3, up3 = ffn_fwd("l1a", h3, vec(ffn1_norm, 1), *ffn_w(SLOT_FFN1[1]))
    q_sw = rms_linear("swa_q", h4, vec(mix_norm, 1), w_sq,
                      pl.BlockSpec((N_CHIPS, None, SQ_ROWS, 512), lambda i, j: (0, SQ_SWA_Q, 0, j)), sq_prep,
                      D_MODEL, 512, rope=(cos, sin), scale=swa_scale)
    o_sw, lse = swa_fwd(q_sw, kv_sw, sinks)
    h5 = linear_res("swa_out", o_sw, w_sq, SQ_SWA_O, h4)
    h6, gate4, up4 = ffn_fwd("l1b", h5, vec(ffn2_norm, 1), *ffn_w(SLOT_FFN2[1]))
    dh6, loss_p, d_final = loss_bwd(h6, final_norm.reshape(1, D_MODEL), tgt)

    slab = {}
    ffn_place = {SLOT_FFN1[0]: (0, 0, 1), SLOT_FFN1[1]: (1, 0, 3), SLOT_FFN2[0]: (1, 1, 3), SLOT_FFN2[1]: (1, 2, 3)}
    sq_place = {SQ_SB_O: (1, 0, 3), SQ_SWA_Q: (1, 1, 3), SQ_SWA_O: (1, 2, 3)}

    def ffn_grads(tag, dh, h_in, g, gate, up, slot, proj=None):
        dh_in, xn, dg_, du_, act, dhb, dnorm, *through_proj = ffn_bwd(tag, dh, h_in, g, gate, up, *ffn_w(slot),
                                                                      proj=proj)
        grp, s, ns = ffn_place[slot]
        in_shape = (2, N_CHIPS, ns, D_MODEL // 2, FF_CHUNK)
        out_shape = (2, N_CHIPS, ns, FF_ROWS, D_MODEL // 2)
        blk = (None, 1, None, D_MODEL // 2, FF_CHUNK)
        slab["in", grp] = mm_tn(f"dw_gate_{tag}", xn, dg_, D_MODEL // 2, FF_CHUNK, blk,
                                lambda k, n: (k, n, s, 0, 0), in_shape, prev=slab.get(("in", grp)))
        slab["in", grp] = mm_tn(f"dw_up_{tag}", xn, du_, D_MODEL // 2, FF_CHUNK, blk,
                                lambda k, n: (k, 2 + n, s, 0, 0), in_shape, prev=slab["in", grp])
        slab["out", grp] = mm_tn(f"dw_out_{tag}", act, dhb, FF_CHUNK, D_MODEL // 2,
                                 (None, 2, None, FF_ROWS, D_MODEL // 2),
                                 lambda k, n: (n, k, s, 0, 0), out_shape, prev=slab.get(("out", grp)))
        return (dh_in, dnorm, *through_proj)

    def sq_grad(tag, a, dyb, t):
        grp, s, ns = sq_place[t]
        slab["sq", grp] = mm_tn(f"dw_sq_{tag}", a, dyb, D_MODEL, D_MODEL // 2,
                                (None, N_CHIPS, None, SQ_ROWS, D_MODEL // 2),
                                lambda k, n: (n, 0, s, 0, 0), (2, N_CHIPS, ns, SQ_ROWS, D_MODEL // 2),
                                prev=slab.get(("sq", grp)))

    def reduce_group(grp, kinds, host=None):
        slabs = [slab[kind, grp][0] for kind in kinds]
        names = [f"{kind}{grp}" for kind in kinds]
        theirs = exchange_halves(f"exchange_halves_{grp}", [slab[kind, grp][1] for kind in kinds])
        parts = [add_sibling(f"add_sibling_{nm}", s, t, core) for nm, s, t in zip(names, slabs, theirs)]
        arrived = host(parts) if host else exchange_chip_partials(f"exchange_chip_partials_{grp}", parts)
        halves = [sum_chips(f"sum_chips_{nm}", g, p, chip) for nm, g, p in zip(names, arrived, parts)]
        sib_halves = share_reduced_halves(f"share_reduced_halves_{grp}", halves)
        return {kind: pair for kind, pair in zip(kinds, zip(halves, sib_halves))}

    dh5, d_ffn2_1, dh5b, do_sw = ffn_grads("l1b", dh6, h5, vec(ffn2_norm, 1), gate4, up4, SLOT_FFN2[1],
                                           proj=(w_sq, SQ_SWA_O))
    sq_grad("swa_o", o_sw, dh5b, SQ_SWA_O)
    dq_sw, kv_own, kv_prev, d_sinks = swa_bwd(q_sw, kv_sw, sinks, do_sw, o_sw, lse, cos, sin)
    sq_w_spec = pl.BlockSpec((N_CHIPS, None, SQ_ROWS, D_MODEL), lambda i, j: (0, SQ_SWA_Q, 0, 0))
    dh4, hn4, d_mix_1 = linear_bwd_rms("swa_q_bwd", [(dq_sw, w_sq, sq_w_spec, sq_prep)], h4, vec(mix_norm, 1), dh5,
                                       1, D_MODEL)
    sq_grad("swa_q", hn4, dq_sw, SQ_SWA_Q)
    dh3a, d_ffn1_1 = ffn_grads("l1a", dh4, h3, vec(ffn1_norm, 1), gate3, up3, SLOT_FFN1[1])
    dkv = kv_grad_combine(kv_own, kv_prev, cos, sin)
    kv_w_spec = pl.BlockSpec((D_MODEL, 2 * kv_cols), lambda i, j: (0, 0))
    dh3, xn3, d_kvn = linear_bwd_rms("kv_bwd", [(dkv, w_kv, kv_w_spec, ident)], h3, kvn, dh3a, 1, 2 * kv_cols)
    slab["kv", 1] = mm_tn("dw_kv", xn3, dkv, D_MODEL, kv_cols, (None, N_CHIPS, None, SQ_ROWS, kv_cols),
                          lambda k, n: (n, 0, 0, 0, 0), (2, N_CHIPS, 1, SQ_ROWS, kv_cols))
    dh2, d_ffn2_0, dh2b, do_sb = ffn_grads("l0b", dh3, h2, vec(ffn2_norm, 0), gate2, up2, SLOT_FFN2[0],
                                           proj=(w_sq, SQ_SB_O))
    sq_grad("sb_o", o_sb, dh2b, SQ_SB_O)
    sb_grads = []

    def behind_sb_bwd(parts):
        dq_sb, dk_sb, dv_sb, arrived = sb_bwd(qkv, do_sb, tot, sb_first, parts)
        sb_grads.extend([dq_sb, dk_sb, dv_sb])
        return arrived

    red = {1: reduce_group(1, ["in", "out", "sq", "kv"], host=behind_sb_bwd)}
    dqkv = jnp.concatenate(sb_grads, axis=1)
    dh1, hn1, d_mix_0 = qkv_bwd(dqkv, w_qkv, h1, vec(mix_norm, 0), dh2)
    slab["qkv", 0] = mm_tn("dw_qkv", hn1, dqkv, D_MODEL // 2, QKV_COLS, (None, 1, None, D_MODEL // 2, QKV_COLS),
                           lambda k, n: (k, n, 0, 0, 0), (2, N_CHIPS, 1, D_MODEL // 2, QKV_COLS))
    dx, d_ffn1_0 = ffn_grads("l0a", dh1, x2, vec(ffn1_norm, 0), gate1, up1, SLOT_FFN1[0])
    red[0] = reduce_group(0, ["in", "out", "qkv"])

    def upd(name, w, m, v, kind, places, row_halves):
        shp = w.shape
        w3 = w.reshape((-1,) + shp[-2:])
        outs = adamw_shard(name, w3, m.reshape(w3.shape), v.reshape(w3.shape),
                           [red[grp][kind] for grp, _ in places], core, [s for _, s in places], row_halves)
        return [o.reshape(shp) for o in outs]

    ffn1_places = [ffn_place[s][:2] for s in SLOT_FFN1]
    ffn2_places = [ffn_place[s][:2] for s in SLOT_FFN2]
    r_ffn1_in = upd("adamw_ffn1_in", ffn1_w_in, m_ffn1_w_in, v_ffn1_w_in, "in", ffn1_places, True)
    r_ffn2_in = upd("adamw_ffn2_in", ffn2_w_in, m_ffn2_w_in, v_ffn2_w_in, "in", ffn2_places, True)
    r_ffn1_out = upd("adamw_ffn1_out", ffn1_w_out, m_ffn1_w_out, v_ffn1_w_out, "out", ffn1_places, False)
    r_ffn2_out = upd("adamw_ffn2_out", ffn2_w_out, m_ffn2_w_out, v_ffn2_w_out, "out", ffn2_places, False)
    r_qkv = upd("adamw_qkv", sb_w_qkv, m_sb_w_qkv, v_sb_w_qkv, "qkv", [(0, 0)], True)
    r_sb_o = upd("adamw_sb_o", sb_w_o, m_sb_w_o, v_sb_w_o, "sq", [sq_place[SQ_SB_O][:2]], False)
    r_swa_q = upd("adamw_swa_q", swa_w_q, m_swa_w_q, v_swa_w_q, "sq", [sq_place[SQ_SWA_Q][:2]], False)
    r_swa_o = upd("adamw_swa_o", swa_w_o, m_swa_w_o, v_swa_w_o, "sq", [sq_place[SQ_SWA_O][:2]], False)
    r_kv = upd("adamw_kv", kv_w, m_kv_w, v_kv_w, "kv", [(1, 0)], False)

    loss_row = jnp.pad(loss_p, ((0, 0), (0, D_MODEL - LANES)))
    d_sink_row = d_sinks[0, :SWA_Q_HEADS]
    part = _pack_small(jnp.concatenate([d_ffn1_0, d_ffn1_1], axis=0), jnp.concatenate([d_mix_0, d_mix_1], axis=0),
                       jnp.concatenate([d_ffn2_0, d_ffn2_1], axis=0), d_kvn, d_final, d_sink_row, loss_row)
    zrow = jnp.zeros((1, D_MODEL), F32)
    small = small_allreduce_adamw(
        part,
        _pack_small(ffn1_norm, mix_norm, ffn2_norm, kv_norm, final_norm, swa_sinks, zrow),
        _pack_small(m_ffn1_norm, m_mix_norm, m_ffn2_norm, m_kv_norm, m_final_norm, m_swa_sinks, zrow),
        _pack_small(v_ffn1_norm, v_mix_norm, v_ffn2_norm, v_kv_norm, v_final_norm, v_swa_sinks, zrow))

    def unpack(p):
        return dict(ffn1_norm=p[0:2], mix_norm=p[2:4], ffn2_norm=p[4:6], kv_norm=p[6], final_norm=p[7],
                    swa_sinks=p[8:9, :SWA_Q_HEADS])

    big = dict(ffn1_w_in=r_ffn1_in, ffn1_w_out=r_ffn1_out, ffn2_w_in=r_ffn2_in, ffn2_w_out=r_ffn2_out,
               sb_w_qkv=r_qkv, sb_w_o=r_sb_o, kv_w=r_kv, swa_w_q=r_swa_q, swa_w_o=r_swa_o)
    order = ["ffn1_norm", "ffn1_w_in", "ffn1_w_out", "mix_norm", "ffn2_norm", "ffn2_w_in", "ffn2_w_out",
             "sb_w_qkv", "sb_w_o", "kv_norm", "kv_w", "swa_w_q", "swa_sinks", "swa_w_o", "final_norm"]
    outs = []
    for kind in range(4):
        sm = unpack(small[kind])
        for nm in order:
            outs.append(big[nm][kind] if nm in big else sm[nm])
    loss = small[0][9, 0]
    return (loss, dx.reshape(x.shape), *outs)
```

```python
import jax
import jax.numpy as jnp
from jax import lax
from jax.experimental import pallas as pl
from jax.experimental.pallas import tpu as pltpu

F32 = jnp.float32
BF16 = jnp.bfloat16
MESH = pl.DeviceIdType.MESH

D_MODEL = 1024
D_FF = 2816
HEAD_DIM = 64
SB_HEADS = 16
SWA_Q_HEADS = 16
SWA_KV_HEADS = 4
WINDOW = 128
ROPE_THETA = 10000.0
RMS_EPS = 1e-6
FFN_RES_SCALE = 0.5
ATTN_SCALE = HEAD_DIM ** -0.5

ADAM_LR = 0.001
ADAM_B1 = 0.9
ADAM_B2 = 0.999
ADAM_EPS = 1e-08
ADAM_WD = 0.01
ADAM_STEP = 10

N_CHIPS = 4
N_DEV = 8
LANES = 128
FF_CHUNK = D_FF // 2
FF_ROWS = D_FF // N_CHIPS
SQ_ROWS = D_MODEL // N_CHIPS
QKV_COLS = 3 * D_MODEL // N_CHIPS
VMEM_LIMIT = 56 * 1024 * 1024
NEG_BIG = -1e30

SLOT_FFN1 = (0, 1)
SLOT_FFN2 = (2, 3)
SQ_SB_O, SQ_SWA_Q, SQ_SWA_O = 0, 1, 2


def _cparams():
    return pltpu.CompilerParams(vmem_limit_bytes=VMEM_LIMIT)


def _dot(a, b):
    return jnp.dot(a, b, preferred_element_type=F32)


def _dot_nt(a, b):
    return lax.dot_general(a, b, (((1,), (1,)), ((), ())), preferred_element_type=F32)


def _dot_tn(a, b):
    return lax.dot_general(a, b, (((0,), (0,)), ((), ())), preferred_element_type=F32)


def _rstd(h):
    return lax.rsqrt(jnp.mean(h * h, axis=-1, keepdims=True) + RMS_EPS)


def _swap32(x):
    n = x.shape[-1]
    lane = lax.broadcasted_iota(jnp.int32, x.shape, x.ndim - 1)
    first = (lane % HEAD_DIM) < (HEAD_DIM // 2)
    return jnp.where(first, pltpu.roll(x, n - HEAD_DIM // 2, x.ndim - 1), pltpu.roll(x, HEAD_DIM // 2, x.ndim - 1))


def _tile_lanes(t, n):
    return t if n == LANES else jnp.tile(t, (1, n // LANES))


FFN_ROWS = 256


def _ffn_w_in_spec(slot):
    return pl.BlockSpec((N_CHIPS, None, D_MODEL, FF_CHUNK), lambda i: (0, slot, 0, 0), pipeline_mode=pl.Buffered(1))


def _ffn_w_out_spec(slot):
    return pl.BlockSpec((N_CHIPS, None, FF_ROWS, D_MODEL), lambda i: (0, slot, 0, 0), pipeline_mode=pl.Buffered(1))


def ffn_fwd(tag, h, g, w_in, w_out, slot, bg_shards=(), bg_lands=(), loss=None):
    T = h.shape[0]
    tm = FFN_ROWS
    nch = D_FF // FF_CHUNK
    nbg = len(bg_shards)
    nt = T // tm
    nloss = 2 if loss is not None else 0

    def body(h_ref, g_ref, wi_ref, wo_ref, *rest):
        loss_in, rest = rest[:nloss], rest[nloss:]
        out_ref, gate_ref, up_ref = rest[2 * nbg:2 * nbg + 3]
        loss_out, rest = rest[3 * nbg + 3:3 * nbg + 3 + nloss], rest[:3 * nbg + 3] + rest[3 * nbg + 3 + nloss:]
        wg_s, wu_s = rest[3 * nbg + 3:3 * nbg + 5]
        step = pl.program_id(0)
        if nbg:
            gather = GatherOps([s.shape[1] for s in bg_shards], rest[:nbg], rest[2 * nbg + 3:3 * nbg + 3],
                               *rest[3 * nbg + 5:])
            pl.when(step == 0)(gather.start_ici)
            pl.when(step == nt // 2)(gather.forward_arrivals)

            @pl.when(step == nt - 1)
            def _():
                gather.wait_forwards()
                gather.wait_ici_sends()

        @pl.when(step == 0)
        def _():
            for j in range(nch):
                cols = slice(j * FF_CHUNK, (j + 1) * FF_CHUNK)
                wg_s[:, cols] = wi_ref[j]
                wu_s[:, cols] = wi_ref[nch + j]

        hh = h_ref[...]
        xn = (hh * _rstd(hh) * g_ref[...]).astype(BF16)
        gate = _dot(xn, wg_s[...])
        up = _dot(xn, wu_s[...])
        gate_ref[...] = gate.astype(BF16)
        up_ref[...] = up.astype(BF16)
        a = (gate * jax.nn.sigmoid(gate) * up).astype(BF16)
        h_out = hh + FFN_RES_SCALE * _dot(a, wo_ref[...].reshape(D_FF, D_MODEL))
        if loss is None:
            out_ref[...] = h_out
        else:
            (gf_ref, t_ref), (loss_ref, dgf_ref) = loss_in, loss_out

            @pl.when(step == 0)
            def _():
                loss_ref[...] = jnp.zeros_like(loss_ref)
                dgf_ref[...] = jnp.zeros_like(dgf_ref)

            gf = gf_ref[...]
            r = _rstd(h_out)
            hr = h_out * r
            err = hr * gf - t_ref[...]
            loss_ref[...] += 0.5 * jnp.sum(jnp.mean(err * err, axis=-1, keepdims=True), axis=0, keepdims=True)
            dy = err * (1.0 / D_MODEL)
            dgf_ref[...] += jnp.sum(dy * hr, axis=0, keepdims=True)
            gd = gf * dy
            out_ref[...] = r * (gd - hr * jnp.mean(gd * hr, axis=-1, keepdims=True))

    row = pl.BlockSpec((tm, D_MODEL), lambda i: (i, 0))
    ff = pl.BlockSpec((tm, D_FF), lambda i: (i, 0))
    vec = pl.BlockSpec((1, D_MODEL), lambda i: (0, 0))
    res = pl.pallas_call(
        body,
        name=f"ffn_fwd_{tag}",
        grid=(nt,),
        in_specs=[row, vec, _ffn_w_in_spec(slot), _ffn_w_out_spec(slot)] + [vec, row][:nloss] + [ANY] * (2 * nbg),
        out_specs=[row, ff, ff] + [ANY] * nbg + [pl.BlockSpec((1, LANES), lambda i: (0, 0)), vec][:nloss],
        out_shape=[
            jax.ShapeDtypeStruct((T, D_MODEL), F32),
            jax.ShapeDtypeStruct((T, D_FF), BF16),
            jax.ShapeDtypeStruct((T, D_FF), BF16),
        ] + [jax.ShapeDtypeStruct(l.shape, l.dtype) for l in bg_lands]
        + [jax.ShapeDtypeStruct((1, LANES), F32), jax.ShapeDtypeStruct((1, D_MODEL), F32)][:nloss],
        input_output_aliases={4 + nloss + nbg + t: 3 + t for t in range(nbg)},
        scratch_shapes=[pltpu.VMEM((D_MODEL, D_FF), BF16), pltpu.VMEM((D_MODEL, D_FF), BF16)]
        + [pltpu.SemaphoreType.DMA((nbg, N_PEER_CHIPS))] * (4 if nbg else 0),
        compiler_params=_cparams(),
    )(h, g, w_in, w_out, *(loss or ()), *bg_shards, *bg_lands)
    if nbg:
        return res[0], res[1], res[2], list(res[3:3 + nbg])
    return tuple(res)


def ffn_bwd(tag, dh, h, g, gate, up, w_in, w_out, slot, proj=None):
    T = dh.shape[0]
    tm = FFN_ROWS
    nch = D_FF // FF_CHUNK

    def body(dh_ref, h_ref, g_ref, gate_ref, up_ref, wi_ref, wo_ref, *rest):
        if proj is not None:
            wp_ref, rest = rest[0], rest[1:]
        dhin_ref, xn_ref, dg_ref, du_ref, a_ref, dhb_ref, dnorm_ref = rest[:7]
        @pl.when(pl.program_id(0) == 0)
        def _():
            dnorm_ref[...] = jnp.zeros_like(dnorm_ref)

        dhh = dh_ref[...]
        dhb = (FFN_RES_SCALE * dhh).astype(BF16)
        dhb_ref[...] = dhb
        dxn = None
        for j in range(nch):
            cols = slice(j * FF_CHUNK, (j + 1) * FF_CHUNK)
            da = _dot_nt(dhb, wo_ref[2 * j:2 * j + 2].reshape(FF_CHUNK, D_MODEL))
            gt = gate_ref[:, cols].astype(F32)
            u = up_ref[:, cols].astype(F32)
            s = jax.nn.sigmoid(gt)
            silu = gt * s
            a_ref[:, cols] = (silu * u).astype(BF16)
            dgate = (da * u * (s * (1.0 + gt * (1.0 - s)))).astype(BF16)
            dup = (da * silu).astype(BF16)
            dg_ref[:, cols] = dgate
            du_ref[:, cols] = dup
            part = _dot_nt(dgate, wi_ref[j]) + _dot_nt(dup, wi_ref[nch + j])
            dxn = part if dxn is None else dxn + part
        hh = h_ref[...]
        gg = g_ref[...]
        r = _rstd(hh)
        hr = hh * r
        xn_ref[...] = (hr * gg).astype(BF16)
        dnorm_ref[...] += jnp.sum(dxn * hr, axis=0, keepdims=True)
        gd = gg * dxn
        dh_in = dhh + r * (gd - hr * jnp.mean(gd * hr, axis=-1, keepdims=True))
        dhin_ref[...] = dh_in
        if proj is not None:
            dyb_ref, da_ref = rest[7:9]
            dyb = dh_in.astype(BF16)
            dyb_ref[...] = dyb
            da_ref[...] = _dot_nt(dyb, wp_ref[...].reshape(D_MODEL, D_MODEL)).astype(BF16)

    row = pl.BlockSpec((tm, D_MODEL), lambda i: (i, 0))
    ff = pl.BlockSpec((tm, D_FF), lambda i: (i, 0))
    vec = pl.BlockSpec((1, D_MODEL), lambda i: (0, 0))
    in_specs = [row, row, vec, ff, ff, _ffn_w_in_spec(slot), _ffn_w_out_spec(slot)]
    args = [dh, h, g, gate, up, w_in, w_out]
    out_specs = [row, row, ff, ff, ff, row, vec]
    out_shape = [
        jax.ShapeDtypeStruct((T, D_MODEL), F32),
        jax.ShapeDtypeStruct((T, D_MODEL), BF16),
        jax.ShapeDtypeStruct((T, D_FF), BF16),
        jax.ShapeDtypeStruct((T, D_FF), BF16),
        jax.ShapeDtypeStruct((T, D_FF), BF16),
        jax.ShapeDtypeStruct((T, D_MODEL), BF16),
        jax.ShapeDtypeStruct((1, D_MODEL), F32),
    ]
    if proj is not None:
        w_sq, t = proj
        in_specs.append(pl.BlockSpec((N_CHIPS, None, SQ_ROWS, D_MODEL), lambda i: (0, t, 0, 0),
                                     pipeline_mode=pl.Buffered(1)))
        args.append(w_sq)
        out_specs += [row, row]
        out_shape += [jax.ShapeDtypeStruct((T, D_MODEL), BF16)] * 2
    return pl.pallas_call(
        body,
        name=f"ffn_bwd_{tag}",
        grid=(T // tm,),
        in_specs=in_specs,
        out_specs=out_specs,
        out_shape=out_shape,
        compiler_params=_cparams(),
    )(*args)


def rms_linear(name, h, g, w, w_spec, w_prep, n_out, tn, *, rope=None, rope_blocks=None, scale=None):
    T = h.shape[0]
    tm = 512
    extra, extra_specs = [], []
    if rope is not None:
        extra += list(rope)
        extra_specs += [pl.BlockSpec((tm, LANES), lambda i, j: (i, 0))] * 2
    if scale is not None:
        extra.append(scale)
        extra_specs.append(pl.BlockSpec((1, tn), lambda i, j: (0, j)))

    def body(h_ref, g_ref, w_ref, *rest):
        rest = list(rest)
        cos_ref = sin_ref = sc_ref = None
        if rope is not None:
            cos_ref, sin_ref = rest[0], rest[1]
            rest = rest[2:]
        if scale is not None:
            sc_ref = rest[0]
            rest = rest[1:]
        out_ref, xn_s = rest

        @pl.when(pl.program_id(1) == 0)
        def _():
            hh = h_ref[...]
            xn_s[...] = (hh * _rstd(hh) * g_ref[...]).astype(BF16)

        y = _dot(xn_s[...], w_prep(w_ref[...]))
        if rope is not None:
            turned = y * _tile_lanes(cos_ref[...], tn) + _swap32(y) * _tile_lanes(sin_ref[...], tn)
            y = turned if rope_blocks is None else jnp.where(pl.program_id(1) < rope_blocks, turned, y)
        if scale is not None:
            y = y * sc_ref[...]
        out_ref[...] = y.astype(BF16)

    return pl.pallas_call(
        body,
        name=name,
        grid=(T // tm, n_out // tn),
        in_specs=[
            pl.BlockSpec((tm, D_MODEL), lambda i, j: (i, 0)),
            pl.BlockSpec((1, D_MODEL), lambda i, j: (0, 0)),
            w_spec,
        ] + extra_specs,
        out_specs=pl.BlockSpec((tm, tn), lambda i, j: (i, j)),
        out_shape=jax.ShapeDtypeStruct((T, n_out), BF16),
        scratch_shapes=[pltpu.VMEM((tm, D_MODEL), BF16)],
        compiler_params=_cparams(),
    )(h, g, w, *extra)


QKV_ROWS = 512


def _qkv_w_spec():
    return pl.BlockSpec((N_CHIPS, D_MODEL, QKV_COLS), lambda i: (0, 0, 0), pipeline_mode=pl.Buffered(1))


def qkv_fwd(h, g, w_qkv, scale):
    T = h.shape[0]
    tm = QKV_ROWS

    def body(h_ref, g_ref, w_ref, sc_ref, out_ref):
        hh = h_ref[...]
        xn = (hh * _rstd(hh) * g_ref[...]).astype(BF16)
        for s in range(N_CHIPS):
            cols = slice(s * QKV_COLS, (s + 1) * QKV_COLS)
            out_ref[:, cols] = (_dot(xn, w_ref[s]) * sc_ref[:, cols]).astype(BF16)

    return pl.pallas_call(
        body,
        name="sb_qkv",
        grid=(T // tm,),
        in_specs=[
            pl.BlockSpec((tm, D_MODEL), lambda i: (i, 0)),
            pl.BlockSpec((1, D_MODEL), lambda i: (0, 0)),
            _qkv_w_spec(),
            pl.BlockSpec((1, 3 * D_MODEL), lambda i: (0, 0)),
        ],
        out_specs=pl.BlockSpec((tm, 3 * D_MODEL), lambda i: (i, 0)),
        out_shape=jax.ShapeDtypeStruct((T, 3 * D_MODEL), BF16),
        compiler_params=_cparams(),
    )(h, g, w_qkv, scale)


def qkv_bwd(dy, w_qkv, h, g, dres):
    T = h.shape[0]
    tm = QKV_ROWS

    def body(dy_ref, w_ref, h_ref, g_ref, dres_ref, dh_ref, xn_ref, dg_ref):
        @pl.when(pl.program_id(0) == 0)
        def _():
            dg_ref[...] = jnp.zeros_like(dg_ref)

        dxn = None
        for s in range(N_CHIPS):
            part = _dot_nt(dy_ref[:, s * QKV_COLS:(s + 1) * QKV_COLS], w_ref[s])
            dxn = part if dxn is None else dxn + part
        hh = h_ref[...]
        gg = g_ref[...]
        r = _rstd(hh)
        hr = hh * r
        xn_ref[...] = (hr * gg).astype(BF16)
        dg_ref[...] += jnp.sum(dxn * hr, axis=0, keepdims=True)
        gd = gg * dxn
        dh_ref[...] = dres_ref[...] + r * (gd - hr * jnp.mean(gd * hr, axis=-1, keepdims=True))

    row = pl.BlockSpec((tm, D_MODEL), lambda i: (i, 0))
    vec = pl.BlockSpec((1, D_MODEL), lambda i: (0, 0))
    return pl.pallas_call(
        body,
        name="sb_qkv_bwd",
        grid=(T // tm,),
        in_specs=[pl.BlockSpec((tm, 3 * D_MODEL), lambda i: (i, 0)), _qkv_w_spec(), row, vec, row],
        out_specs=[row, row, vec],
        out_shape=[
            jax.ShapeDtypeStruct((T, D_MODEL), F32),
            jax.ShapeDtypeStruct((T, D_MODEL), BF16),
            jax.ShapeDtypeStruct((1, D_MODEL), F32),
        ],
        compiler_params=_cparams(),
    )(dy, w_qkv, h, g, dres)


def linear_res(name, a, w_sq, t, res, bg_lands=()):
    T = a.shape[0]
    tm = 512
    nbg = len(bg_lands)
    nt = T // tm

    def body(a_ref, w_ref, res_ref, *rest):
        out_ref = rest[nbg]
        if nbg:
            gather = GatherOps([l.shape[2] for l in bg_lands], None, rest[nbg + 1:2 * nbg + 1], None, None,
                               *rest[2 * nbg + 1:])

            @pl.when(pl.program_id(0) == 0)
            def _():
                gather.start_forwards()

        out_ref[...] = res_ref[...] + _dot(a_ref[...], w_ref[...].reshape(D_MODEL, D_MODEL))
        if nbg:
            @pl.when(pl.program_id(0) == nt - 1)
            def _():
                gather.wait_forwards()

    row = pl.BlockSpec((tm, D_MODEL), lambda i: (i, 0))
    res_ = pl.pallas_call(
        body,
        name=name,
        grid=(nt,),
        in_specs=[row, pl.BlockSpec((N_CHIPS, None, SQ_ROWS, D_MODEL), lambda i: (0, t, 0, 0)), row] + [ANY] * nbg,
        out_specs=[row] + [ANY] * nbg,
        out_shape=[jax.ShapeDtypeStruct((T, D_MODEL), F32)] + [jax.ShapeDtypeStruct(l.shape, l.dtype) for l in bg_lands],
        input_output_aliases={3 + k: 1 + k for k in range(nbg)},
        scratch_shapes=[pltpu.SemaphoreType.DMA((nbg, N_PEER_CHIPS))] * (2 if nbg else 0),
        compiler_params=_cparams(),
    )(a, w_sq, res, *bg_lands)
    return (res_[0], list(res_[1:])) if nbg else res_[0]


def linear_bwd_rms(name, pairs, h, g, dres, nch, tn, tm=256):
    T = h.shape[0]
    npair = len(pairs)

    def body(*refs):
        dy_refs = refs[:npair]
        w_refs = refs[npair:2 * npair]
        h_ref, g_ref, dres_ref, dh_ref, xn_ref, dg_ref, acc_s = refs[2 * npair:]
        i = pl.program_id(0)
        j = pl.program_id(1)

        @pl.when(j == 0)
        def _():
            acc_s[...] = jnp.zeros_like(acc_s)

        @pl.when((i == 0) & (j == 0))
        def _():
            dg_ref[...] = jnp.zeros_like(dg_ref)

        part = None
        for p in range(npair):
            d = _dot_nt(dy_refs[p][...], pairs[p][3](w_refs[p][...]))
            part = d if part is None else part + d
        acc_s[...] += part

        @pl.when(j == nch - 1)
        def _():
            dxn = acc_s[...]
            hh = h_ref[...]
            gg = g_ref[...]
            r = _rstd(hh)
            hr = hh * r
            xn_ref[...] = (hr * gg).astype(BF16)
            dg_ref[...] += jnp.sum(dxn * hr, axis=0, keepdims=True)
            gd = gg * dxn
            dh_ref[...] = dres_ref[...] + r * (gd - hr * jnp.mean(gd * hr, axis=-1, keepdims=True))

    row = pl.BlockSpec((tm, D_MODEL), lambda i, j: (i, 0))
    vec = pl.BlockSpec((1, D_MODEL), lambda i, j: (0, 0))
    return pl.pallas_call(
        body,
        name=name,
        grid=(T // tm, nch),
        in_specs=[pl.BlockSpec((tm, tn), lambda i, j: (i, j))] * npair + [p[2] for p in pairs] + [row, vec, row],
        out_specs=[row, row, vec],
        out_shape=[
            jax.ShapeDtypeStruct((T, D_MODEL), F32),
            jax.ShapeDtypeStruct((T, D_MODEL), BF16),
            jax.ShapeDtypeStruct((1, D_MODEL), F32),
        ],
        scratch_shapes=[pltpu.VMEM((tm, D_MODEL), F32)],
        compiler_params=_cparams(),
    )(*[p[0] for p in pairs], *[p[1] for p in pairs], h, g, dres)


DW_TOKENS = 4096


def mm_tn(name, a, b, tk, tn, out_block, out_index, out_shape, prev=None, tt=DW_TOKENS):
    T = a.shape[0]
    ns, r = out_block[1], out_block[3]
    tt = min(tt, T)
    nt = T // tt

    def body(*refs):
        a_ref, b_ref = refs[:2]
        out_ref, copy_ref = refs[-2:]
        t = pl.program_id(2)
        res = _dot_tn(a_ref[...], b_ref[...])

        @pl.when(t == 0)
        def _():
            for u in range(ns):
                out_ref[u] = res[u * r:(u + 1) * r]

        @pl.when(t > 0)
        def _():
            for u in range(ns):
                out_ref[u] += res[u * r:(u + 1) * r]

        @pl.when(t == nt - 1)
        def _():
            copy_ref[...] = out_ref[...].astype(BF16)

    in_specs = [
        pl.BlockSpec((tt, tk), lambda k, n, t: (t, k)),
        pl.BlockSpec((tt, tn), lambda k, n, t: (t, n)),
    ]
    args = [a, b]
    aliases = {}
    if prev is not None:
        in_specs += [pl.BlockSpec(memory_space=pl.ANY)] * 2
        args += list(prev)
        aliases = {2: 0, 3: 1}
    out_spec = pl.BlockSpec(out_block, lambda k, n, t: out_index(k, n))
    return tuple(pl.pallas_call(
        body,
        name=name,
        grid=(a.shape[1] // tk, b.shape[1] // tn, nt),
        in_specs=in_specs,
        out_specs=[out_spec, out_spec],
        out_shape=[jax.ShapeDtypeStruct(out_shape, F32), jax.ShapeDtypeStruct(out_shape, BF16)],
        input_output_aliases=aliases,
        compiler_params=_cparams(),
    )(*args))


SB_BLOCK = 256
SB_QROWS = 256
SB_QROWS_BWD = 256
SB_UNDERFLOW_BITS = 140.0
SB_CHUNK = 128


LOG2E = 1.4426950408889634


def _softplus2(z2):
    sign = jnp.uint32(0x80000000)
    neg_abs = lax.bitcast_convert_type(lax.bitcast_convert_type(z2, jnp.uint32) | sign, F32)
    return jnp.log2(1.0 + jnp.exp2(neg_abs)) + jnp.maximum(z2, 0.0)


def _twice(x):
    return jnp.concatenate([x, x], axis=1)


def sb_fwd(qkv, bg_shards=(), bg_lands=()):
    T = qkv.shape[0]
    tq, tk = SB_QROWS, SB_BLOCK
    ratio = tq // tk
    npair = SB_HEADS // 2
    nbg = len(bg_shards)
    nq = T // tq

    def body(q_ref, k_ref, v_ref, *rest):
        bg_in = rest[:nbg]
        o_ref, tot_ref, first_ref = rest[2 * nbg:2 * nbg + 3]
        bg_out = rest[2 * nbg + 3:3 * nbg + 3]
        acc_s, c_s, z_s, w_s, kmax_s = rest[3 * nbg + 3:3 * nbg + 8]
        p = pl.program_id(0)
        i = pl.program_id(1)
        if nbg:
            gather = GatherOps([s.shape[1] for s in bg_shards], bg_in, bg_out, *rest[3 * nbg + 8:])

            @pl.when((p == 0) & (i == 0))
            def _():
                gather.start_ici()

        @pl.when(i == 0)
        def _():
            kmax_s[...] = jnp.max(jnp.abs(k_ref[...]), axis=0, keepdims=True).astype(F32)

        q = q_ref[...]
        lane = lax.broadcasted_iota(jnp.int32, (tq, LANES), 1)
        first = lane < HEAD_DIM
        zero = jnp.zeros_like(q)
        q_heads = (jnp.where(first, q, zero), jnp.where(first, zero, q))
        row = lax.broadcasted_iota(jnp.int32, (tq, tk), 0)
        col = lax.broadcasted_iota(jnp.int32, (tq, tk), 1)
        visible = [col + r * tk < row for r in range(ratio)]
        krow = lax.broadcasted_iota(jnp.int32, (tk, tk), 0)
        kcol = lax.broadcasted_iota(jnp.int32, (tk, tk), 1)
        from_s = (krow >= kcol).astype(BF16)
        acc_s[...] = jnp.zeros_like(acc_s)
        c_s[...] = jnp.zeros_like(c_s)

        def rows(j):
            return pl.ds(pl.multiple_of(j * tk, tk), tk)

        def logits(j):
            kb = k_ref[rows(j), :]
            for hd in range(2):
                z_s[hd] = _dot_nt(q_heads[hd], kb) * LOG2E

        def flush(j):
            vb = v_ref[rows(j), :]
            for hd in range(2):
                acc_s[hd] += _dot(w_s[hd], vb)

        def block(j, mask=None, flush_block=None):
            if flush_block is not None:
                flush(flush_block)
            chunks = [(hd, slice(r0, r0 + SB_CHUNK)) for hd in range(2) for r0 in range(0, tq, SB_CHUNK)]
            k_next = k_ref[rows(jnp.maximum(j - 1, 0)), :]
            es, sums = [], []
            for hd, rs in chunks:
                z2 = z_s[hd, rs, :]
                z_s[hd, rs, :] = _dot_nt(q_heads[hd][rs, :], k_next) * LOG2E
                if mask is not None:
                    z2 = jnp.where(mask if mask.ndim == 0 else mask[rs, :], z2, NEG_BIG)
                sp = _softplus2(z2)
                c = c_s[hd, rs, :]
                es.append(z2 + _twice(c))
                c_s[hd, rs, :] = c - jnp.sum(sp, axis=1, keepdims=True)
                sums.append(_dot(sp.astype(BF16), from_s))
            for (hd, rs), e, s in zip(chunks, es, sums):
                w_s[hd, rs, :] = jnp.exp2(e - s).astype(BF16)

        z_bound = [LOG2E * jnp.sum(jnp.abs(q_heads[hd].astype(F32)) * kmax_s[...], axis=1, keepdims=True)
                   for hd in range(2)]

        def more_keys_matter():
            top = jnp.maximum(c_s[0] + z_bound[0], c_s[1] + z_bound[1])
            return (jnp.max(top) >= -SB_UNDERFLOW_BITS).astype(jnp.int32)

        assert ratio == 1
        logits(i)
        block(i, visible[0])

        def trip(carry):
            trips, _ = carry
            j = i - 1 - trips
            block(j, flush_block=j + 1)
            return trips + 1, more_keys_matter()

        trips, _ = lax.while_loop(lambda carry: jnp.logical_and(carry[0] < i, carry[1] > 0), trip,
                                  (jnp.int32(0), jnp.int32(1)))
        first_walked = i - trips
        flush(first_walked)
        first_ref[p, i] = first_walked.astype(F32)
        o_ref[...] = jnp.where(first, acc_s[0], acc_s[1]).astype(BF16)
        tot_ref[...] = jnp.where(first, c_s[0], c_s[1])
        if nbg:
            @pl.when((p == npair - 1) & (i == nq - 1))
            def _():
                gather.wait_ici()

    res = pl.pallas_call(
        body,
        name="sb_fwd",
        grid=(npair, nq),
        in_specs=[
            pl.BlockSpec((tq, LANES), lambda p, i: (i, p)),
            pl.BlockSpec((T, LANES), lambda p, i: (0, npair + p)),
            pl.BlockSpec((T, LANES), lambda p, i: (0, 2 * npair + p)),
        ] + [ANY] * (2 * nbg),
        out_specs=[pl.BlockSpec((tq, LANES), lambda p, i: (i, p))] * 2 + [pl.BlockSpec(memory_space=pltpu.SMEM)]
        + [ANY] * nbg,
        out_shape=[jax.ShapeDtypeStruct((T, D_MODEL), BF16), jax.ShapeDtypeStruct((T, D_MODEL), F32),
                   jax.ShapeDtypeStruct((npair, nq), F32)]
        + [jax.ShapeDtypeStruct(l.shape, l.dtype) for l in bg_lands],
        input_output_aliases={3 + nbg + t: 3 + t for t in range(nbg)},
        scratch_shapes=[
            pltpu.VMEM((2, tq, LANES), F32), pltpu.VMEM((2, tq, LANES), F32),
            pltpu.VMEM((2, tq, tk), F32), pltpu.VMEM((2, tq, tk), BF16),
            pltpu.VMEM((1, LANES), F32),
        ] + [pltpu.SemaphoreType.DMA((nbg, N_PEER_CHIPS))] * (2 if nbg else 0),
        compiler_params=_cparams(),
    )(qkv, qkv, qkv, *bg_shards, *bg_lands)
    return res[0], res[1], res[2], list(res[3:])


def sb_bwd(qkv, do, tot, first_block, bg_parts=()):
    T = qkv.shape[0]
    tq, tk = SB_QROWS_BWD, SB_BLOCK
    ratio = tq // tk
    npair = SB_HEADS // 2
    nq = T // tq
    nk = T // tk
    nbg = len(bg_parts)
    assert SB_QROWS == SB_QROWS_BWD

    def body(first_ref, q_ref, k_ref, v_ref, do_ref, tot_ref, *rest):
        bg_in = rest[:nbg]
        dq_ref, dk_ref, dv_ref = rest[nbg:nbg + 3]
        bg_out = rest[nbg + 3:2 * nbg + 3]
        dkt_s, dvt_s, dq_s, rest_s, cg_s, z_s, da_s, dz_s, a_s = rest[2 * nbg + 3:2 * nbg + 12]
        i = pl.program_id(1)
        start = jnp.clip(first_ref[pl.program_id(0), i].astype(jnp.int32), 0, i)
        if nbg:
            @pl.when((pl.program_id(0) == 0) & (i == 0))
            def _():
                for cp in chip_partial_copies(bg_in, bg_out, *rest[2 * nbg + 12:]):
                    cp.start()

        @pl.when(i == 0)
        def _():
            dkt_s[...] = jnp.zeros_like(dkt_s)
            dvt_s[...] = jnp.zeros_like(dvt_s)

        q = q_ref[...]
        do_ = do_ref[...]
        tot_ = tot_ref[...]
        q_t = q.astype(F32).T.astype(BF16)
        do_t = do_.astype(F32).T.astype(BF16)
        lane = lax.broadcasted_iota(jnp.int32, (tq, LANES), 1)
        first = lane < HEAD_DIM
        zero = jnp.zeros_like(q)
        q_heads = (jnp.where(first, q, zero), jnp.where(first, zero, q))
        do_heads = (jnp.where(first, do_, zero), jnp.where(first, zero, do_))
        row = lax.broadcasted_iota(jnp.int32, (tq, tk), 0)
        col = lax.broadcasted_iota(jnp.int32, (tq, tk), 1)
        visible = [col + r * tk < row for r in range(ratio)]
        krow = lax.broadcasted_iota(jnp.int32, (tk, tk), 0)
        kcol = lax.broadcasted_iota(jnp.int32, (tk, tk), 1)
        before = (krow < kcol).astype(BF16)
        from_s = (krow >= kcol).astype(BF16)
        last = ratio * i + ratio - 1
        rest_s[0] = jnp.broadcast_to(tot_[:, 0:1], (tq, LANES))
        rest_s[1] = jnp.broadcast_to(tot_[:, HEAD_DIM:HEAD_DIM + 1], (tq, LANES))
        cg_s[...] = jnp.zeros_like(cg_s)
        dq_s[...] = jnp.zeros_like(dq_s)
        dz_s[...] = jnp.zeros_like(dz_s)
        a_s[...] = jnp.zeros_like(a_s)

        def rows(j):
            return pl.ds(pl.multiple_of(j * tk, tk), tk)

        def logits(j):
            kb = k_ref[rows(j), :]
            vb = v_ref[rows(j), :]
            for hd in range(2):
                z_s[hd] = _dot_nt(q_heads[hd], kb) * LOG2E
                da_s[hd] = _dot_nt(do_heads[hd], vb)

        def flush(j):
            kb = k_ref[rows(j), :]
            for hd in range(2):
                dims = slice(hd * HEAD_DIM, (hd + 1) * HEAD_DIM)
                dq_s[hd] += _dot(dz_s[hd], kb)
                dkt_s[j, dims, :] += _dot(q_t[dims, :], dz_s[hd])
                dvt_s[j, dims, :] += _dot(do_t[dims, :], a_s[hd])

        def block(j, mask=None):
            flush(jnp.maximum(j - 1, 0))
            chunks = [(hd, slice(r0, r0 + SB_CHUNK)) for hd in range(2) for r0 in range(0, tq, SB_CHUNK)]
            nxt = rows(jnp.minimum(j + 1, last))
            k_next = k_ref[nxt, :]
            v_next = v_ref[nxt, :]
            stage1 = []
            for hd, rs in chunks:
                z2 = z_s[hd, rs, :]
                z_s[hd, rs, :] = _dot_nt(q_heads[hd][rs, :], k_next) * LOG2E
                if mask is not None:
                    z2 = jnp.where(mask if mask.ndim == 0 else mask[rs, :], z2, NEG_BIG)
                sp = _softplus2(z2)
                rest = rest_s[hd, rs, :] + jnp.sum(sp, axis=1, keepdims=True)
                rest_s[hd, rs, :] = rest
                stage1.append((z2 + _twice(rest), z2 - sp, _dot(sp.astype(BF16), from_s)))
            stage2 = []
            for (hd, rs), (e, log2_beta, ahead) in zip(chunks, stage1):
                a = jnp.exp2(e - ahead)
                g = a * da_s[hd, rs, :]
                da_s[hd, rs, :] = _dot_nt(do_heads[hd][rs, :], v_next)
                cg = cg_s[hd, rs, :]
                a_s[hd, rs, :] = a.astype(BF16)
                cg_s[hd, rs, :] = cg + jnp.sum(g, axis=1, keepdims=True)
                stage2.append((g, g + _twice(cg), log2_beta, _dot(g.astype(BF16), before)))
            for (hd, rs), (g, g_from, log2_beta, g_before) in zip(chunks, stage2):
                dz_s[hd, rs, :] = (g - jnp.exp2(log2_beta) * (g_from + g_before)).astype(BF16)

        assert ratio == 1
        logits(start)

        @pl.loop(start, i)
        def _(j):
            block(j)

        block(i, visible[0])
        flush(last)
        dq_ref[...] = (jnp.where(first, dq_s[0], dq_s[1]) * ATTN_SCALE).astype(BF16)

        @pl.when(i == nq - 1)
        def _():
            @pl.loop(0, nk)
            def _(b):
                dk_ref[rows(b), :] = dkt_s[b].T.astype(BF16)
                dv_ref[rows(b), :] = dvt_s[b].T.astype(BF16)

        if nbg:
            @pl.when((pl.program_id(0) == npair - 1) & (i == nq - 1))
            def _():
                for cp in chip_partial_copies(bg_in, bg_out, *rest[2 * nbg + 12:]):
                    cp.wait()

    qblk = pl.BlockSpec((tq, LANES), lambda p, i: (i, p))
    full = pl.BlockSpec((T, LANES), lambda p, i: (0, p))
    res = pl.pallas_call(
        body,
        name="sb_bwd",
        grid=(npair, nq),
        in_specs=[
            pl.BlockSpec(memory_space=pltpu.SMEM),
            qblk,
            pl.BlockSpec((T, LANES), lambda p, i: (0, npair + p)),
            pl.BlockSpec((T, LANES), lambda p, i: (0, 2 * npair + p)),
            qblk, qblk,
        ] + [ANY] * nbg,
        out_specs=[qblk, full, full] + [ANY] * nbg,
        out_shape=[jax.ShapeDtypeStruct((T, D_MODEL), BF16)] * 3
        + [jax.ShapeDtypeStruct(b.shape, b.dtype) for b in bg_parts],
        scratch_shapes=[
            pltpu.VMEM((nk, LANES, tk), F32), pltpu.VMEM((nk, LANES, tk), F32),
            pltpu.VMEM((2, tq, LANES), F32), pltpu.VMEM((2, tq, LANES), F32), pltpu.VMEM((2, tq, LANES), F32),
            pltpu.VMEM((2, tq, tk), F32), pltpu.VMEM((2, tq, tk), F32),
            pltpu.VMEM((2, tq, tk), BF16), pltpu.VMEM((2, tq, tk), BF16),
        ] + [pltpu.SemaphoreType.DMA((nbg, N_PEER_CHIPS))] * (2 if nbg else 0),
        compiler_params=_cparams(),
    )(first_block, qkv, qkv, qkv, do, tot, *bg_parts)
    return res[0], res[1], res[2], list(res[3:])


def _swa_valid(n):
    qi = lax.broadcasted_iota(jnp.int32, (WINDOW, 2 * WINDOW), 0)
    ki = lax.broadcasted_iota(jnp.int32, (WINDOW, 2 * WINDOW), 1)
    diff = qi + WINDOW - ki
    return (diff >= 0) & (diff < WINDOW) & ((n > 0) | (ki >= WINDOW))


def _to_half(x, first, src, dst):
    keep = first if src == 0 else jnp.logical_not(first)
    x = jnp.where(keep, x, jnp.zeros_like(x))
    if src != dst:
        x = pltpu.roll(x.astype(F32), HEAD_DIM, 1).astype(BF16)
    return x


SWA_GROUP = SWA_Q_HEADS // SWA_KV_HEADS


def _swa_cols(h):
    return slice((h // 2) * LANES, (h // 2 + 1) * LANES)


def _swa_kv_pair(h):
    return (h // SWA_GROUP) // 2


def _swa_kv_half(h):
    return (h // SWA_GROUP) % 2


def _kv_band(prev_ref, cur_ref, pb):
    cols = slice(pb * LANES, (pb + 1) * LANES)
    return jnp.concatenate([prev_ref[:, cols], cur_ref[:, cols]], axis=0)


def _swa_specs(T):
    nb = T // WINDOW
    kv_w = SWA_KV_HEADS * HEAD_DIM
    qrow = pl.BlockSpec((WINDOW, D_MODEL), lambda n: (n, 0))
    kv = [pl.BlockSpec((WINDOW, kv_w), lambda n, col=col, back=back: (jnp.maximum(n - back, 0), col))
          for col in (0, 1) for back in (0, 1)]
    smem = pl.BlockSpec(memory_space=pltpu.SMEM)
    return nb, qrow, kv, smem


def swa_fwd(q, kv, sinks):
    T = q.shape[0]
    nb, qrow, kv_specs, smem = _swa_specs(T)

    def body(sink_ref, q_ref, kc_ref, kp_ref, vc_ref, vp_ref, o_ref, lse_ref):
        n = pl.program_id(0)
        lane = lax.broadcasted_iota(jnp.int32, (WINDOW, LANES), 1)
        first = lane < HEAD_DIM
        valid = _swa_valid(n)
        k2 = [_kv_band(kp_ref, kc_ref, pb) for pb in range(SWA_KV_HEADS // 2)]
        v2 = [_kv_band(vp_ref, vc_ref, pb) for pb in range(SWA_KV_HEADS // 2)]
        logits = [jnp.where(valid, _dot_nt(_to_half(q_ref[:, _swa_cols(h)], first, h % 2, _swa_kv_half(h)),
                                            k2[_swa_kv_pair(h)]), NEG_BIG) for h in range(SWA_Q_HEADS)]
        probs = []
        lse_acc = jnp.zeros((WINDOW, LANES), F32)
        for h, s in enumerate(logits):
            sink = sink_ref[h]
            m = jnp.maximum(jnp.max(s, axis=1, keepdims=True), sink)
            p = jnp.exp(s - m)
            den = jnp.sum(p, axis=1, keepdims=True) + jnp.exp(sink - m)
            probs.append((p / den).astype(BF16))
            lse_acc = jnp.where(lane == h, m + jnp.log(den), lse_acc)
        outs = []
        for h, p in enumerate(probs):
            o = _dot(p, v2[_swa_kv_pair(h)])
            outs.append(pltpu.roll(o, HEAD_DIM, 1) if h % 2 != _swa_kv_half(h) else o)
        for pair in range(SWA_Q_HEADS // 2):
            o_ref[:, _swa_cols(2 * pair)] = jnp.where(first, outs[2 * pair], outs[2 * pair + 1]).astype(BF16)
        lse_ref[...] = lse_acc

    return pl.pallas_call(
        body,
        name="swa_fwd",
        grid=(nb,),
        in_specs=[smem, qrow] + kv_specs,
        out_specs=[qrow, pl.BlockSpec((WINDOW, LANES), lambda n: (n, 0))],
        out_shape=[jax.ShapeDtypeStruct((T, D_MODEL), BF16), jax.ShapeDtypeStruct((T, LANES), F32)],
        compiler_params=_cparams(),
    )(sinks, q, kv, kv, kv, kv)


def swa_bwd(q, kv, sinks, do, o, lse, cos, sin):
    T = q.shape[0]
    nb, qrow, kv_specs, smem = _swa_specs(T)
    kv_w = SWA_KV_HEADS * HEAD_DIM

    def body(sink_ref, q_ref, kc_ref, kp_ref, vc_ref, vp_ref, do_ref, o_ref, lse_ref, cos_ref, sin_ref,
             dq_ref, own_ref, prv_ref, dsink_ref):
        n = pl.program_id(0)

        @pl.when(n == 0)
        def _():
            dsink_ref[...] = jnp.zeros_like(dsink_ref)

        lane = lax.broadcasted_iota(jnp.int32, (WINDOW, LANES), 1)
        lane1 = lax.broadcasted_iota(jnp.int32, (1, LANES), 1)
        first = lane < HEAD_DIM
        valid = _swa_valid(n)
        cos_ = cos_ref[...]
        sin_ = sin_ref[...]
        k2 = [_kv_band(kp_ref, kc_ref, pb) for pb in range(SWA_KV_HEADS // 2)]
        v2 = [_kv_band(vp_ref, vc_ref, pb) for pb in range(SWA_KV_HEADS // 2)]
        q_t = q_ref[...].astype(F32).T.astype(BF16)
        do_t = do_ref[...].astype(F32).T.astype(BF16)
        stage1 = []
        for h in range(SWA_Q_HEADS):
            a, b, pb = h % 2, _swa_kv_half(h), _swa_kv_pair(h)
            qh = _to_half(q_ref[:, _swa_cols(h)], first, a, b)
            doh = _to_half(do_ref[:, _swa_cols(h)], first, a, b)
            stage1.append((jnp.where(valid, _dot_nt(qh, k2[pb]), NEG_BIG), _dot_nt(doh, v2[pb])))
        deltas = []
        for pair in range(SWA_Q_HEADS // 2):
            prod = do_ref[:, _swa_cols(2 * pair)].astype(F32) * o_ref[:, _swa_cols(2 * pair)].astype(F32)
            deltas += [jnp.sum(jnp.where(first, prod, 0.0), axis=1, keepdims=True),
                       jnp.sum(jnp.where(first, 0.0, prod), axis=1, keepdims=True)]
        stage2 = []
        dsink = jnp.zeros((1, LANES), F32)
        for h, (s, dp) in enumerate(stage1):
            lse_h = lse_ref[:, h:h + 1]
            p = jnp.exp(s - lse_h)
            delta = deltas[h]
            p_sink = jnp.exp(sink_ref[h] - lse_h)
            dsink = dsink + jnp.where(lane1 == h, -jnp.sum(p_sink * delta, axis=0, keepdims=True), 0.0)
            stage2.append(((p * (dp - delta)).astype(BF16), p.astype(BF16)))
        dqs = []
        dk_t = [None] * SWA_KV_HEADS
        dv_t = [None] * SWA_KV_HEADS
        for h, (ds, pb16) in enumerate(stage2):
            kvh = h // SWA_GROUP
            dims = slice(h * HEAD_DIM, (h + 1) * HEAD_DIM)
            dq = _dot(ds, k2[_swa_kv_pair(h)])
            dqs.append(pltpu.roll(dq, HEAD_DIM, 1) if h % 2 != _swa_kv_half(h) else dq)
            dk_h = _dot(q_t[dims, :], ds)
            dv_h = _dot(do_t[dims, :], pb16)
            dk_t[kvh] = dk_h if dk_t[kvh] is None else dk_t[kvh] + dk_h
            dv_t[kvh] = dv_h if dv_t[kvh] is None else dv_t[kvh] + dv_h
        for pair in range(SWA_Q_HEADS // 2):
            dqp = jnp.where(first, dqs[2 * pair], dqs[2 * pair + 1])
            dq_ref[:, _swa_cols(2 * pair)] = ((dqp * cos_ + _swap32(dqp * sin_)) * ATTN_SCALE).astype(BF16)
        for pb in range(SWA_KV_HEADS // 2):
            dk2 = jnp.concatenate([dk_t[2 * pb], dk_t[2 * pb + 1]], axis=0).T
            dv2 = jnp.concatenate([dv_t[2 * pb], dv_t[2 * pb + 1]], axis=0).T
            kcols = slice(pb * LANES, (pb + 1) * LANES)
            vcols = slice(kv_w + pb * LANES, kv_w + (pb + 1) * LANES)
            prv_ref[:, kcols] = dk2[:WINDOW]
            own_ref[:, kcols] = dk2[WINDOW:]
            prv_ref[:, vcols] = dv2[:WINDOW]
            own_ref[:, vcols] = dv2[WINDOW:]
        dsink_ref[...] += dsink

    tab = pl.BlockSpec((WINDOW, LANES), lambda n: (n, 0))
    kvrow = pl.BlockSpec((WINDOW, 2 * kv_w), lambda n: (n, 0))
    return pl.pallas_call(
        body,
        name="swa_bwd",
        grid=(nb,),
        in_specs=[smem, qrow] + kv_specs + [qrow, qrow, tab, tab, tab],
        out_specs=[qrow, kvrow, kvrow, pl.BlockSpec((1, LANES), lambda n: (0, 0))],
        out_shape=[
            jax.ShapeDtypeStruct((T, D_MODEL), BF16),
            jax.ShapeDtypeStruct((T, 2 * kv_w), F32),
            jax.ShapeDtypeStruct((T, 2 * kv_w), F32),
            jax.ShapeDtypeStruct((1, LANES), F32),
        ],
        compiler_params=_cparams(),
    )(sinks, q, kv, kv, kv, kv, do, o, lse, cos, sin)


def kv_grad_combine(own, prv, cos, sin):
    T = own.shape[0]
    nb = T // WINDOW
    kv_w = SWA_KV_HEADS * HEAD_DIM

    def body(own_ref, nxt_ref, cos_ref, sin_ref, out_ref):
        n = pl.program_id(0)
        nxt = jnp.where(n + 1 < nb, nxt_ref[...], 0.0)
        tot = own_ref[...] + nxt
        dk = tot[:, :kv_w]
        c = _tile_lanes(cos_ref[...], kv_w)
        s = _tile_lanes(sin_ref[...], kv_w)
        out_ref[:, :kv_w] = (dk * c + _swap32(dk * s)).astype(BF16)
        out_ref[:, kv_w:] = tot[:, kv_w:].astype(BF16)

    tab = pl.BlockSpec((WINDOW, LANES), lambda n: (n, 0))
    kvrow = pl.BlockSpec((WINDOW, 2 * kv_w), lambda n: (n, 0))
    return pl.pallas_call(
        body,
        name="kv_grad_combine",
        grid=(nb,),
        in_specs=[kvrow, pl.BlockSpec((WINDOW, 2 * kv_w), lambda n: (jnp.minimum(n + 1, nb - 1), 0)), tab, tab],
        out_specs=kvrow,
        out_shape=jax.ShapeDtypeStruct((T, 2 * kv_w), BF16),
        compiler_params=_cparams(),
    )(own, prv, cos, sin)


ANY = pl.BlockSpec(memory_space=pl.ANY)


def _place():
    x, y, c = lax.axis_index("x"), lax.axis_index("y"), lax.axis_index("c")
    other_chips = [(1 - x, y), (x, 1 - y), (1 - x, 1 - y)]
    return x, y, c, 2 * x + y, other_chips


N_PEER_CHIPS = N_CHIPS - 1


class GatherOps:
    def __init__(self, rows, shards, lands, ici_send, ici_recv, d2d_send=None, d2d_recv=None):
        self.rows, self.shards, self.lands = rows, shards, lands
        self.ici_send, self.ici_recv, self.d2d_send, self.d2d_recv = ici_send, ici_recv, d2d_send, d2d_recv
        self.x, self.y, self.c, self.me, self.chips = _place()
        self.pairs = [(t, jdx) for t in range(len(rows)) for jdx in range(N_PEER_CHIPS)]

    def _half(self, ref, t, which):
        r = self.rows[t] // 2
        return ref.at[:, pl.ds(pl.multiple_of(which * r, 16), r), :]

    def _ici(self, t, jdx):
        px, py = self.chips[jdx]
        return pltpu.make_async_remote_copy(
            src_ref=self._half(self.shards[t], t, self.c), dst_ref=self._half(self.lands[t].at[self.me], t, self.c),
            send_sem=self.ici_send.at[t, jdx], recv_sem=self.ici_recv.at[t, jdx],
            device_id=(px, py, self.c), device_id_type=MESH)

    def _landed(self, t, jdx):
        px, py = self.chips[jdx]
        blk = self._half(self.lands[t].at[2 * px + py], t, self.c)
        return pltpu.make_async_remote_copy(
            src_ref=blk, dst_ref=blk, send_sem=self.ici_send.at[t, jdx], recv_sem=self.ici_recv.at[t, jdx],
            device_id=(px, py, self.c), device_id_type=MESH)

    def _d2d(self, t, jdx, which):
        px, py = self.chips[jdx]
        blk = self._half(self.lands[t].at[2 * px + py], t, which)
        return pltpu.make_async_remote_copy(
            src_ref=blk, dst_ref=blk, send_sem=self.d2d_send.at[t, jdx], recv_sem=self.d2d_recv.at[t, jdx],
            device_id=(self.x, self.y, 1 - self.c), device_id_type=MESH)

    def start_ici(self):
        for t, jdx in self.pairs:
            self._ici(t, jdx).start()

    def wait_ici(self):
        for t, jdx in self.pairs:
            self._landed(t, jdx).wait_recv()
        self.wait_ici_sends()

    def wait_ici_sends(self):
        for t, jdx in self.pairs:
            self._ici(t, jdx).wait_send()

    def forward_arrivals(self):
        for t, jdx in self.pairs:
            self._landed(t, jdx).wait_recv()
            self._d2d(t, jdx, self.c).start()

    def start_forwards(self):
        for t, jdx in self.pairs:
            self._d2d(t, jdx, self.c).start()

    def wait_forwards(self):
        for t, jdx in self.pairs:
            self._d2d(t, jdx, 1 - self.c).wait_recv()
            self._d2d(t, jdx, self.c).wait_send()


def all_gather_weights(shards, lands):
    n = len(shards)
    rows = [s.shape[1] for s in shards]

    def body(*refs):
        ins, outs = refs[:n], refs[2 * n:3 * n]
        ops = GatherOps(rows, ins, outs, *refs[3 * n:])
        ops.start_ici()
        ops.forward_arrivals()
        ops.wait_forwards()
        ops.wait_ici_sends()

    return pl.pallas_call(
        body,
        name="all_gather_weights",
        in_specs=[ANY] * (2 * n),
        out_specs=[ANY] * n,
        out_shape=[jax.ShapeDtypeStruct(l.shape, l.dtype) for l in lands],
        input_output_aliases={n + t: t for t in range(n)},
        scratch_shapes=[pltpu.SemaphoreType.DMA((n, N_PEER_CHIPS))] * 4,
    )(*shards, *lands)


def place_own_shard(name, shard, chip):
    nl, r, c = shard.shape

    def body(chip_ref, s_ref, o_ref):
        o_ref[...] = s_ref[...]

    return pl.pallas_call(
        body, name=name,
        grid_spec=pltpu.PrefetchScalarGridSpec(
            num_scalar_prefetch=1, grid=(nl,),
            in_specs=[pl.BlockSpec((None, r, c), lambda l, chip_ref: (l, 0, 0))],
            out_specs=pl.BlockSpec((None, None, r, c), lambda l, chip_ref: (chip_ref[0], l, 0, 0))),
        out_shape=jax.ShapeDtypeStruct((N_CHIPS,) + shard.shape, shard.dtype), compiler_params=_cparams(),
    )(chip, shard)


def exchange_halves(name, slabs):
    n = len(slabs)

    def body(*refs):
        ins, theirs = refs[:n], refs[n:2 * n]
        send_sems, recv_sems = refs[2 * n:]
        x, y, c, _, _ = _place()
        copies = []
        for t in range(n):
            cp = pltpu.make_async_remote_copy(
                src_ref=ins[t].at[1 - c], dst_ref=theirs[t], send_sem=send_sems.at[t],
                recv_sem=recv_sems.at[t], device_id=(x, y, 1 - c), device_id_type=MESH)
            cp.start()
            copies.append(cp)
        for cp in copies:
            cp.wait()

    return pl.pallas_call(
        body,
        name=name,
        in_specs=[ANY] * n,
        out_specs=[ANY] * n,
        out_shape=[jax.ShapeDtypeStruct(s.shape[1:], s.dtype) for s in slabs],
        scratch_shapes=[pltpu.SemaphoreType.DMA((n,)), pltpu.SemaphoreType.DMA((n,))],
    )(*slabs)


def chip_partial_copies(ins, outs, send_sems, recv_sems):
    _, _, c, me, chips = _place()
    return [pltpu.make_async_remote_copy(
        src_ref=ins[t].at[2 * px + py], dst_ref=outs[t].at[me], send_sem=send_sems.at[t, jdx],
        recv_sem=recv_sems.at[t, jdx], device_id=(px, py, c), device_id_type=MESH)
        for t in range(len(ins)) for jdx, (px, py) in enumerate(chips)]


def exchange_chip_partials(name, parts):
    n = len(parts)

    def body(*refs):
        copies = chip_partial_copies(refs[:n], refs[n:2 * n], *refs[2 * n:])
        for cp in copies:
            cp.start()
        for cp in copies:
            cp.wait()

    return pl.pallas_call(
        body,
        name=name,
        in_specs=[ANY] * n,
        out_specs=[ANY] * n,
        out_shape=[jax.ShapeDtypeStruct(p.shape, p.dtype) for p in parts],
        scratch_shapes=[pltpu.SemaphoreType.DMA((n, 3)), pltpu.SemaphoreType.DMA((n, 3))],
    )(*parts)


def share_reduced_halves(name, halves):
    n = len(halves)

    def body(*refs):
        ins, outs = refs[:n], refs[n:2 * n]
        send_sems, recv_sems = refs[2 * n:]
        x, y, c, _, _ = _place()
        copies = []
        for t in range(n):
            cp = pltpu.make_async_remote_copy(
                src_ref=ins[t], dst_ref=outs[t], send_sem=send_sems.at[t],
                recv_sem=recv_sems.at[t], device_id=(x, y, 1 - c), device_id_type=MESH)
            cp.start()
            copies.append(cp)
        for cp in copies:
            cp.wait()

    return pl.pallas_call(
        body,
        name=name,
        in_specs=[ANY] * n,
        out_specs=[ANY] * n,
        out_shape=[jax.ShapeDtypeStruct(h.shape, h.dtype) for h in halves],
        scratch_shapes=[pltpu.SemaphoreType.DMA((n,)), pltpu.SemaphoreType.DMA((n,))],
    )(*halves)


def _row_tile(r, c):
    tr = r
    while tr * c * 4 > (3 << 19) and tr % 16 == 0:
        tr //= 2
    return tr


def add_sibling(name, slab, theirs, core):
    _, ns, slots, r, c = slab.shape
    tr = _row_tile(r, c)

    def body(core_ref, a_ref, b_ref, o_ref):
        o_ref[...] = (a_ref[...] + b_ref[...]).astype(BF16)

    blk = pl.BlockSpec((None, None, tr, c), lambda s, l, i, core_ref: (s, l, i, 0))
    return pl.pallas_call(
        body, name=name,
        grid_spec=pltpu.PrefetchScalarGridSpec(
            num_scalar_prefetch=1, grid=(ns, slots, r // tr),
            in_specs=[pl.BlockSpec((None, None, None, tr, c), lambda s, l, i, core_ref: (core_ref[0], s, l, i, 0)), blk],
            out_specs=blk),
        out_shape=jax.ShapeDtypeStruct(theirs.shape, BF16), compiler_params=_cparams(),
    )(core, slab, theirs)


def sum_chips(name, recv, own, chip):
    _, slots, r, c = recv.shape
    tr = _row_tile(r, c)

    def body(chip_ref, r0, r1, r2, r3, own_ref, o_ref):
        me = chip_ref[0]
        mine = own_ref[...]
        terms = [jnp.where(me == s, mine, rr[...]).astype(F32) for s, rr in enumerate((r0, r1, r2, r3))]
        o_ref[...] = ((terms[0] + terms[1]) + terms[2]) + terms[3]

    def src(s):
        return pl.BlockSpec((None, None, tr, c),
                            lambda l, i, chip_ref: (jnp.where(chip_ref[0] == s, (s + 1) % N_CHIPS, s), l, i, 0))

    return pl.pallas_call(
        body, name=name,
        grid_spec=pltpu.PrefetchScalarGridSpec(
            num_scalar_prefetch=1, grid=(slots, r // tr),
            in_specs=[src(0), src(1), src(2), src(3),
                      pl.BlockSpec((None, None, tr, c), lambda l, i, chip_ref: (chip_ref[0], l, i, 0))],
            out_specs=pl.BlockSpec((None, tr, c), lambda l, i, chip_ref: (l, i, 0))),
        out_shape=jax.ShapeDtypeStruct((slots, r, c), F32), compiler_params=_cparams(),
    )(chip, recv, recv, recv, recv, own)


def _adamw_math(w, g, m, v):
    m = ADAM_B1 * m + (1.0 - ADAM_B1) * g
    v = ADAM_B2 * v + (1.0 - ADAM_B2) * (g * g)
    m_hat = m / (1.0 - ADAM_B1 ** ADAM_STEP)
    v_hat = v / (1.0 - ADAM_B2 ** ADAM_STEP)
    delta = -ADAM_LR * (m_hat / (jnp.sqrt(v_hat) + ADAM_EPS) + ADAM_WD * w)
    return delta, m, v


def adamw_shard(name, w, m, v, g_pairs, core, slots, row_halves):
    n = w.shape[0]
    assert n == len(g_pairs)
    _, r, c = g_pairs[0][0].shape
    tr = _row_tile(r, c)
    nr = r // tr

    def body(core_ref, w_ref, m_ref, v_ref, *rest):
        g_refs, (go_ref, d_ref, mo_ref, vo_ref) = rest[:2 * n], rest[2 * n:]
        mine = pl.program_id(1) == core_ref[0]
        g = jnp.where(mine, g_refs[0][...], g_refs[1][...])
        for l in range(1, n):
            g = jnp.where(pl.program_id(0) == l, jnp.where(mine, g_refs[2 * l][...], g_refs[2 * l + 1][...]), g)
        delta, mm, vv = _adamw_math(w_ref[...], g, m_ref[...], v_ref[...])
        go_ref[...] = g
        d_ref[...] = delta
        mo_ref[...] = mm
        vo_ref[...] = vv

    if row_halves:
        wspec = pl.BlockSpec((None, tr, c), lambda l, h, i, core_ref: (l, h * nr + i, 0))
    else:
        wspec = pl.BlockSpec((None, tr, c), lambda l, h, i, core_ref: (l, i, h))
    def gspec(slot):
        return pl.BlockSpec((None, tr, c), lambda l, h, i, core_ref: (slot, i, 0))

    shp = jax.ShapeDtypeStruct(w.shape, F32)
    return pl.pallas_call(
        body, name=name,
        grid_spec=pltpu.PrefetchScalarGridSpec(
            num_scalar_prefetch=1, grid=(n, 2, nr),
            in_specs=[wspec, wspec, wspec] + [gspec(s) for s in slots for _ in range(2)], out_specs=[wspec] * 4),
        out_shape=[shp] * 4, compiler_params=_cparams(),
    )(core, w, m, v, *[g for pair in g_pairs for g in pair])


SMALL_ROWS = 16


def small_allreduce_adamw(part, w, m, v):
    def body(p_ref, w_ref, m_ref, v_ref, g_ref, d_ref, mo_ref, vo_ref, buf, send_sems, recv_sems):
        x, y, c, _, _ = _place()
        me = 4 * x + 2 * y + c
        buf[me] = p_ref[...]
        copies = []
        for k in range(1, N_DEV):
            kx, ky, kc = (k >> 2) & 1, (k >> 1) & 1, k & 1
            peer = (x ^ kx, y ^ ky, c ^ kc)
            cp = pltpu.make_async_remote_copy(
                src_ref=p_ref, dst_ref=buf.at[me], send_sem=send_sems.at[k - 1],
                recv_sem=recv_sems.at[k - 1], device_id=peer, device_id_type=MESH)
            cp.start()
            copies.append(cp)
        for cp in copies:
            cp.wait()
        g = buf[0]
        for dev in range(1, N_DEV):
            g = g + buf[dev]
        delta, mm, vv = _adamw_math(w_ref[...], g, m_ref[...], v_ref[...])
        g_ref[...] = g
        d_ref[...] = delta
        mo_ref[...] = mm
        vo_ref[...] = vv

    vm = pl.BlockSpec(memory_space=pltpu.VMEM)
    shp = jax.ShapeDtypeStruct(part.shape, F32)
    return pl.pallas_call(
        body, name="small_allreduce_adamw",
        in_specs=[vm] * 4, out_specs=[vm] * 4, out_shape=[shp] * 4,
        scratch_shapes=[
            pltpu.VMEM((N_DEV,) + part.shape, F32),
            pltpu.SemaphoreType.DMA((N_DEV - 1,)), pltpu.SemaphoreType.DMA((N_DEV - 1,)),
        ],
    )(part, w, m, v)


def _rope_tables(T):
    half = HEAD_DIM // 2
    inv_freq = ROPE_THETA ** (-jnp.arange(half, dtype=F32) / half)
    ang = jnp.arange(T).astype(F32)[:, None] * inv_freq[None, :]
    cos = jnp.tile(jnp.cos(ang), (1, LANES // half))
    sin = jnp.tile(jnp.sin(ang), (1, LANES // half))
    lane = jnp.arange(LANES)
    sign = jnp.where((lane % HEAD_DIM) < half, -1.0, 1.0).astype(F32)
    return cos, sin * sign[None, :]


def _pack_small(ffn1, mix, ffn2, kvn, fin, sinks, loss_row):
    sink_row = jnp.pad(sinks.reshape(1, SWA_Q_HEADS), ((0, 0), (0, D_MODEL - SWA_Q_HEADS)))
    rows = jnp.concatenate([ffn1, mix, ffn2, kvn.reshape(1, -1), fin.reshape(1, -1), sink_row, loss_row], axis=0)
    return jnp.concatenate([rows, jnp.zeros((SMALL_ROWS - rows.shape[0], D_MODEL), F32)], axis=0)


def kernel(x, ffn1_norm, ffn1_w_in, ffn1_w_out, mix_norm, ffn2_norm, ffn2_w_in, ffn2_w_out, sb_w_qkv, sb_w_o, kv_norm, kv_w, swa_w_q, swa_sinks, swa_w_o, final_norm, loss_target, m_ffn1_norm, m_ffn1_w_in, m_ffn1_w_out, m_mix_norm, m_ffn2_norm, m_ffn2_w_in, m_ffn2_w_out, m_sb_w_qkv, m_sb_w_o, m_kv_norm, m_kv_w, m_swa_w_q, m_swa_sinks, m_swa_w_o, m_final_norm, v_ffn1_norm, v_ffn1_w_in, v_ffn1_w_out, v_mix_norm, v_ffn2_norm, v_ffn2_w_in, v_ffn2_w_out, v_sb_w_qkv, v_sb_w_o, v_kv_norm, v_kv_w, v_swa_w_q, v_swa_sinks, v_swa_w_o, v_final_norm):
    T = x.shape[1]
    kv_cols = SWA_KV_HEADS * HEAD_DIM
    x2 = x.reshape(T, D_MODEL)
    tgt = loss_target.reshape(T, D_MODEL)
    cos, sin = _rope_tables(T)

    w_in_l = jnp.concatenate([ffn1_w_in, ffn2_w_in], axis=0).astype(BF16)
    w_out_l = jnp.concatenate([ffn1_w_out, ffn2_w_out], axis=0).astype(BF16)
    sq_l = jnp.concatenate([sb_w_o, swa_w_q, swa_w_o], axis=0).astype(BF16)
    qkv_l = sb_w_qkv[0].astype(BF16)
    kvw_l = kv_w.astype(BF16)
    core = lax.axis_index("c").astype(jnp.int32).reshape(1)
    chip = (2 * lax.axis_index("x") + lax.axis_index("y")).astype(jnp.int32).reshape(1)
    early = [w_in_l[:1], w_out_l[:1]]
    mid = [sq_l, qkv_l[None]]
    late = [w_in_l[1:], w_out_l[1:], kvw_l[None]]
    early_lands = [place_own_shard(f"own_early_{t}", s, chip) for t, s in enumerate(early)]
    mid_lands = [place_own_shard(f"own_mid_{t}", s, chip) for t, s in enumerate(mid)]
    late_lands = [place_own_shard(f"own_late_{t}", s, chip) for t, s in enumerate(late)]
    w_in0, w_out0 = all_gather_weights(early, early_lands)

    def ffn_w(slot):
        return (w_in0, w_out0, 0) if slot == 0 else (w_in_r, w_out_r, slot - 1)

    def vec(a, i):
        return a[i].reshape(1, D_MODEL)

    ident = lambda w: w
    sq_prep = lambda w: w.reshape(D_MODEL, w.shape[-1])
    qscale = jnp.concatenate([jnp.full((1, D_MODEL), ATTN_SCALE, F32), jnp.ones((1, 2 * D_MODEL), F32)], axis=1)
    swa_scale = jnp.full((1, D_MODEL), ATTN_SCALE, F32)
    sinks = swa_sinks.reshape(SWA_Q_HEADS)

    h1, gate1, up1, (w_sq, w_qkv) = ffn_fwd("l0a", x2, vec(ffn1_norm, 0), *ffn_w(SLOT_FFN1[0]), mid, mid_lands)
    w_qkv = w_qkv.reshape(N_CHIPS, D_MODEL, QKV_COLS)
    qkv = qkv_fwd(h1, vec(mix_norm, 0), w_qkv, qscale)
    o_sb, tot, sb_first, late_lands = sb_fwd(qkv, late, late_lands)
    h2, (w_in_r, w_out_r, w_kv) = linear_res("sb_out", o_sb, w_sq, SQ_SB_O, h1, late_lands)
    w_kv = w_kv.reshape(D_MODEL, 2 * kv_cols)
    h3, gate2, up2 = ffn_fwd("l0b", h2, vec(ffn2_norm, 0), *ffn_w(SLOT_FFN2[0]))
    kvn = kv_norm.reshape(1, D_MODEL)
    kv_sw = rms_linear("kv_proj", h3, kvn, w_kv, pl.BlockSpec((D_MODEL, kv_cols), lambda i, j: (0, j)), ident,
                       2 * kv_cols, kv_cols, rope=(cos, sin), rope_blocks=1)
    h4, gate3, up3 = ffn_fwd("l1a", h3, vec(ffn1_norm, 1), *ffn_w(SLOT_FFN1[1]))
    q_sw = rms_linear("swa_q", h4, vec(mix_norm, 1), w_sq,
                      pl.BlockSpec((N_CHIPS, None, SQ_ROWS, 512), lambda i, j: (0, SQ_SWA_Q, 0, j)), sq_prep,
                      D_MODEL, 512, rope=(cos, sin), scale=swa_scale)
    o_sw, lse = swa_fwd(q_sw, kv_sw, sinks)
    h5 = linear_res("swa_out", o_sw, w_sq, SQ_SWA_O, h4)
    dh6, gate4, up4, loss_p, d_final = ffn_fwd("l1b", h5, vec(ffn2_norm, 1), *ffn_w(SLOT_FFN2[1]),
                                               loss=(final_norm.reshape(1, D_MODEL), tgt))

    slab = {}
    ffn_place = {SLOT_FFN1[0]: (0, 0, 1), SLOT_FFN1[1]: (1, 0, 3), SLOT_FFN2[0]: (1, 1, 3), SLOT_FFN2[1]: (1, 2, 3)}
    sq_place = {SQ_SB_O: (1, 0, 3), SQ_SWA_Q: (1, 1, 3), SQ_SWA_O: (1, 2, 3)}

    def ffn_grads(tag, dh, h_in, g, gate, up, slot, proj=None):
        dh_in, xn, dg_, du_, act, dhb, dnorm, *through_proj = ffn_bwd(tag, dh, h_in, g, gate, up, *ffn_w(slot),
                                                                      proj=proj)
        grp, s, ns = ffn_place[slot]
        in_shape = (2, N_CHIPS, ns, D_MODEL // 2, FF_CHUNK)
        out_shape = (2, N_CHIPS, ns, FF_ROWS, D_MODEL // 2)
        blk = (None, 1, None, D_MODEL // 2, FF_CHUNK)
        slab["in", grp] = mm_tn(f"dw_gate_{tag}", xn, dg_, D_MODEL // 2, FF_CHUNK, blk,
                                lambda k, n: (k, n, s, 0, 0), in_shape, prev=slab.get(("in", grp)))
        slab["in", grp] = mm_tn(f"dw_up_{tag}", xn, du_, D_MODEL // 2, FF_CHUNK, blk,
                                lambda k, n: (k, 2 + n, s, 0, 0), in_shape, prev=slab["in", grp])
        slab["out", grp] = mm_tn(f"dw_out_{tag}", act, dhb, FF_CHUNK, D_MODEL // 2,
                                 (None, 2, None, FF_ROWS, D_MODEL // 2),
                                 lambda k, n: (n, k, s, 0, 0), out_shape, prev=slab.get(("out", grp)))
        return (dh_in, dnorm, *through_proj)

    def sq_grad(tag, a, dyb, t):
        grp, s, ns = sq_place[t]
        slab["sq", grp] = mm_tn(f"dw_sq_{tag}", a, dyb, D_MODEL, D_MODEL // 2,
                                (None, N_CHIPS, None, SQ_ROWS, D_MODEL // 2),
                                lambda k, n: (n, 0, s, 0, 0), (2, N_CHIPS, ns, SQ_ROWS, D_MODEL // 2),
                                prev=slab.get(("sq", grp)))

    def reduce_group(grp, kinds, host=None):
        slabs = [slab[kind, grp][0] for kind in kinds]
        names = [f"{kind}{grp}" for kind in kinds]
        theirs = exchange_halves(f"exchange_halves_{grp}", [slab[kind, grp][1] for kind in kinds])
        parts = [add_sibling(f"add_sibling_{nm}", s, t, core) for nm, s, t in zip(names, slabs, theirs)]
        arrived = host(parts) if host else exchange_chip_partials(f"exchange_chip_partials_{grp}", parts)
        halves = [sum_chips(f"sum_chips_{nm}", g, p, chip) for nm, g, p in zip(names, arrived, parts)]
        sib_halves = share_reduced_halves(f"share_reduced_halves_{grp}", halves)
        return {kind: pair for kind, pair in zip(kinds, zip(halves, sib_halves))}

    dh5, d_ffn2_1, dh5b, do_sw = ffn_grads("l1b", dh6, h5, vec(ffn2_norm, 1), gate4, up4, SLOT_FFN2[1],
                                           proj=(w_sq, SQ_SWA_O))
    sq_grad("swa_o", o_sw, dh5b, SQ_SWA_O)
    dq_sw, kv_own, kv_prev, d_sinks = swa_bwd(q_sw, kv_sw, sinks, do_sw, o_sw, lse, cos, sin)
    sq_w_spec = pl.BlockSpec((N_CHIPS, None, SQ_ROWS, D_MODEL), lambda i, j: (0, SQ_SWA_Q, 0, 0))
    dh4, hn4, d_mix_1 = linear_bwd_rms("swa_q_bwd", [(dq_sw, w_sq, sq_w_spec, sq_prep)], h4, vec(mix_norm, 1), dh5,
                                       1, D_MODEL)
    sq_grad("swa_q", hn4, dq_sw, SQ_SWA_Q)
    dh3a, d_ffn1_1 = ffn_grads("l1a", dh4, h3, vec(ffn1_norm, 1), gate3, up3, SLOT_FFN1[1])
    dkv = kv_grad_combine(kv_own, kv_prev, cos, sin)
    kv_w_spec = pl.BlockSpec((D_MODEL, 2 * kv_cols), lambda i, j: (0, 0))
    dh3, xn3, d_kvn = linear_bwd_rms("kv_bwd", [(dkv, w_kv, kv_w_spec, ident)], h3, kvn, dh3a, 1, 2 * kv_cols)
    slab["kv", 1] = mm_tn("dw_kv", xn3, dkv, D_MODEL, kv_cols, (None, N_CHIPS, None, SQ_ROWS, kv_cols),
                          lambda k, n: (n, 0, 0, 0, 0), (2, N_CHIPS, 1, SQ_ROWS, kv_cols))
    dh2, d_ffn2_0, dh2b, do_sb = ffn_grads("l0b", dh3, h2, vec(ffn2_norm, 0), gate2, up2, SLOT_FFN2[0],
                                           proj=(w_sq, SQ_SB_O))
    sq_grad("sb_o", o_sb, dh2b, SQ_SB_O)
    sb_grads = []

    def behind_sb_bwd(parts):
        dq_sb, dk_sb, dv_sb, arrived = sb_bwd(qkv, do_sb, tot, sb_first, parts)
        sb_grads.extend([dq_sb, dk_sb, dv_sb])
        return arrived

    red = {1: reduce_group(1, ["in", "out", "sq", "kv"], host=behind_sb_bwd)}
    dqkv = jnp.concatenate(sb_grads, axis=1)
    dh1, hn1, d_mix_0 = qkv_bwd(dqkv, w_qkv, h1, vec(mix_norm, 0), dh2)
    slab["qkv", 0] = mm_tn("dw_qkv", hn1, dqkv, D_MODEL // 2, QKV_COLS, (None, 1, None, D_MODEL // 2, QKV_COLS),
                           lambda k, n: (k, n, 0, 0, 0), (2, N_CHIPS, 1, D_MODEL // 2, QKV_COLS))
    dx, d_ffn1_0 = ffn_grads("l0a", dh1, x2, vec(ffn1_norm, 0), gate1, up1, SLOT_FFN1[0])
    red[0] = reduce_group(0, ["in", "out", "qkv"])

    def upd(name, w, m, v, kind, places, row_halves):
        shp = w.shape
        w3 = w.reshape((-1,) + shp[-2:])
        outs = adamw_shard(name, w3, m.reshape(w3.shape), v.reshape(w3.shape),
                           [red[grp][kind] for grp, _ in places], core, [s for _, s in places], row_halves)
        return [o.reshape(shp) for o in outs]

    ffn1_places = [ffn_place[s][:2] for s in SLOT_FFN1]
    ffn2_places = [ffn_place[s][:2] for s in SLOT_FFN2]
    r_ffn1_in = upd("adamw_ffn1_in", ffn1_w_in, m_ffn1_w_in, v_ffn1_w_in, "in", ffn1_places, True)
    r_ffn2_in = upd("adamw_ffn2_in", ffn2_w_in, m_ffn2_w_in, v_ffn2_w_in, "in", ffn2_places, True)
    r_ffn1_out = upd("adamw_ffn1_out", ffn1_w_out, m_ffn1_w_out, v_ffn1_w_out, "out", ffn1_places, False)
    r_ffn2_out = upd("adamw_ffn2_out", ffn2_w_out, m_ffn2_w_out, v_ffn2_w_out, "out", ffn2_places, False)
    r_qkv = upd("adamw_qkv", sb_w_qkv, m_sb_w_qkv, v_sb_w_qkv, "qkv", [(0, 0)], True)
    r_sb_o = upd("adamw_sb_o", sb_w_o, m_sb_w_o, v_sb_w_o, "sq", [sq_place[SQ_SB_O][:2]], False)
    r_swa_q = upd("adamw_swa_q", swa_w_q, m_swa_w_q, v_swa_w_q, "sq", [sq_place[SQ_SWA_Q][:2]], False)
    r_swa_o = upd("adamw_swa_o", swa_w_o, m_swa_w_o, v_swa_w_o, "sq", [sq_place[SQ_SWA_O][:2]], False)
    r_kv = upd("adamw_kv", kv_w, m_kv_w, v_kv_w, "kv", [(1, 0)], False)

    loss_row = jnp.pad(loss_p, ((0, 0), (0, D_MODEL - LANES)))
    d_sink_row = d_sinks[0, :SWA_Q_HEADS]
    part = _pack_small(jnp.concatenate([d_ffn1_0, d_ffn1_1], axis=0), jnp.concatenate([d_mix_0, d_mix_1], axis=0),
                       jnp.concatenate([d_ffn2_0, d_ffn2_1], axis=0), d_kvn, d_final, d_sink_row, loss_row)
    zrow = jnp.zeros((1, D_MODEL), F32)
    small = small_allreduce_adamw(
        part,
        _pack_small(ffn1_norm, mix_norm, ffn2_norm, kv_norm, final_norm, swa_sinks, zrow),
        _pack_small(m_ffn1_norm, m_mix_norm, m_ffn2_norm, m_kv_norm, m_final_norm, m_swa_sinks, zrow),
        _pack_small(v_ffn1_norm, v_mix_norm, v_ffn2_norm, v_kv_norm, v_final_norm, v_swa_sinks, zrow))

    def unpack(p):
        return dict(ffn1_norm=p[0:2], mix_norm=p[2:4], ffn2_norm=p[4:6], kv_norm=p[6], final_norm=p[7],
                    swa_sinks=p[8:9, :SWA_Q_HEADS])

    big = dict(ffn1_w_in=r_ffn1_in, ffn1_w_out=r_ffn1_out, ffn2_w_in=r_ffn2_in, ffn2_w_out=r_ffn2_out,
               sb_w_qkv=r_qkv, sb_w_o=r_sb_o, kv_w=r_kv, swa_w_q=r_swa_q, swa_w_o=r_swa_o)
    order = ["ffn1_norm", "ffn1_w_in", "ffn1_w_out", "mix_norm", "ffn2_norm", "ffn2_w_in", "ffn2_w_out",
             "sb_w_qkv", "sb_w_o", "kv_norm", "kv_w", "swa_w_q", "swa_sinks", "swa_w_o", "final_norm"]
    outs = []
    for kind in range(4):
        sm = unpack(small[kind])
        for nm in order:
            outs.append(big[nm][kind] if nm in big else sm[nm])
    loss = small[0][9, 0]
    return (loss, dx.reshape(x.shape), *outs)
```

```python
import jax
import jax.numpy as jnp
from jax import lax
from jax.experimental import pallas as pl
from jax.experimental.pallas import tpu as pltpu

F32 = jnp.float32
BF16 = jnp.bfloat16
MESH = pl.DeviceIdType.MESH

D_MODEL = 1024
D_FF = 2816
HEAD_DIM = 64
SB_HEADS = 16
SWA_Q_HEADS = 16
SWA_KV_HEADS = 4
WINDOW = 128
ROPE_THETA = 10000.0
RMS_EPS = 1e-6
FFN_RES_SCALE = 0.5
ATTN_SCALE = HEAD_DIM ** -0.5

ADAM_LR = 0.001
ADAM_B1 = 0.9
ADAM_B2 = 0.999
ADAM_EPS = 1e-08
ADAM_WD = 0.01
ADAM_STEP = 10

N_CHIPS = 4
N_DEV = 8
LANES = 128
FF_CHUNK = D_FF // 2
FF_ROWS = D_FF // N_CHIPS
SQ_ROWS = D_MODEL // N_CHIPS
QKV_COLS = 3 * D_MODEL // N_CHIPS
VMEM_LIMIT = 56 * 1024 * 1024
NEG_BIG = -1e30

SLOT_FFN1 = (0, 1)
SLOT_FFN2 = (2, 3)
SQ_SB_O, SQ_SWA_Q, SQ_SWA_O = 0, 1, 2


def _cparams():
    return pltpu.CompilerParams(vmem_limit_bytes=VMEM_LIMIT)


def _dot(a, b):
    return jnp.dot(a, b, preferred_element_type=F32)


def _dot_nt(a, b):
    return lax.dot_general(a, b, (((1,), (1,)), ((), ())), preferred_element_type=F32)


def _dot_tn(a, b):
    return lax.dot_general(a, b, (((0,), (0,)), ((), ())), preferred_element_type=F32)


def _rstd(h):
    return lax.rsqrt(jnp.mean(h * h, axis=-1, keepdims=True) + RMS_EPS)


def _swap32(x):
    n = x.shape[-1]
    lane = lax.broadcasted_iota(jnp.int32, x.shape, x.ndim - 1)
    first = (lane % HEAD_DIM) < (HEAD_DIM // 2)
    return jnp.where(first, pltpu.roll(x, n - HEAD_DIM // 2, x.ndim - 1), pltpu.roll(x, HEAD_DIM // 2, x.ndim - 1))


def _tile_lanes(t, n):
    return t if n == LANES else jnp.tile(t, (1, n // LANES))


FFN_ROWS = 256


def _ffn_w_in_spec(slot):
    return pl.BlockSpec((N_CHIPS, None, D_MODEL, FF_CHUNK), lambda i: (0, slot, 0, 0), pipeline_mode=pl.Buffered(1))


def _ffn_w_out_spec(slot):
    return pl.BlockSpec((N_CHIPS, None, FF_ROWS, D_MODEL), lambda i: (0, slot, 0, 0), pipeline_mode=pl.Buffered(1))


def ffn_fwd(tag, h, g, w_in, w_out, slot, bg_shards=(), bg_lands=(), loss=None):
    T = h.shape[0]
    tm = FFN_ROWS
    nch = D_FF // FF_CHUNK
    nbg = len(bg_shards)
    nt = T // tm
    nloss = 2 if loss is not None else 0

    def body(h_ref, g_ref, wi_ref, wo_ref, *rest):
        loss_in, rest = rest[:nloss], rest[nloss:]
        out_ref, gate_ref, up_ref = rest[2 * nbg:2 * nbg + 3]
        loss_out, rest = rest[3 * nbg + 3:3 * nbg + 3 + nloss], rest[:3 * nbg + 3] + rest[3 * nbg + 3 + nloss:]
        wg_s, wu_s = rest[3 * nbg + 3:3 * nbg + 5]
        step = pl.program_id(0)
        if nbg:
            gather = GatherOps([s.shape[1] for s in bg_shards], rest[:nbg], rest[2 * nbg + 3:3 * nbg + 3],
                               *rest[3 * nbg + 5:])
            pl.when(step == 0)(gather.start_ici)
            pl.when(step == nt // 2)(gather.forward_arrivals)

            @pl.when(step == nt - 1)
            def _():
                gather.wait_forwards()
                gather.wait_ici_sends()

        @pl.when(step == 0)
        def _():
            for j in range(nch):
                cols = slice(j * FF_CHUNK, (j + 1) * FF_CHUNK)
                wg_s[:, cols] = wi_ref[j]
                wu_s[:, cols] = wi_ref[nch + j]

        hh = h_ref[...]
        xn = (hh * _rstd(hh) * g_ref[...]).astype(BF16)
        gate = _dot(xn, wg_s[...])
        up = _dot(xn, wu_s[...])
        gate_ref[...] = gate.astype(BF16)
        up_ref[...] = up.astype(BF16)
        a = (gate * jax.nn.sigmoid(gate) * up).astype(BF16)
        h_out = hh + FFN_RES_SCALE * _dot(a, wo_ref[...].reshape(D_FF, D_MODEL))
        if loss is None:
            out_ref[...] = h_out
        else:
            (gf_ref, t_ref), (loss_ref, dgf_ref) = loss_in, loss_out

            @pl.when(step == 0)
            def _():
                loss_ref[...] = jnp.zeros_like(loss_ref)
                dgf_ref[...] = jnp.zeros_like(dgf_ref)

            gf = gf_ref[...]
            r = _rstd(h_out)
            hr = h_out * r
            err = hr * gf - t_ref[...]
            loss_ref[...] += 0.5 * jnp.sum(jnp.mean(err * err, axis=-1, keepdims=True), axis=0, keepdims=True)
            dy = err * (1.0 / D_MODEL)
            dgf_ref[...] += jnp.sum(dy * hr, axis=0, keepdims=True)
            gd = gf * dy
            out_ref[...] = r * (gd - hr * jnp.mean(gd * hr, axis=-1, keepdims=True))

    row = pl.BlockSpec((tm, D_MODEL), lambda i: (i, 0))
    ff = pl.BlockSpec((tm, D_FF), lambda i: (i, 0))
    vec = pl.BlockSpec((1, D_MODEL), lambda i: (0, 0))
    res = pl.pallas_call(
        body,
        name=f"ffn_fwd_{tag}",
        grid=(nt,),
        in_specs=[row, vec, _ffn_w_in_spec(slot), _ffn_w_out_spec(slot)] + [vec, row][:nloss] + [ANY] * (2 * nbg),
        out_specs=[row, ff, ff] + [ANY] * nbg + [pl.BlockSpec((1, LANES), lambda i: (0, 0)), vec][:nloss],
        out_shape=[
            jax.ShapeDtypeStruct((T, D_MODEL), F32),
            jax.ShapeDtypeStruct((T, D_FF), BF16),
            jax.ShapeDtypeStruct((T, D_FF), BF16),
        ] + [jax.ShapeDtypeStruct(l.shape, l.dtype) for l in bg_lands]
        + [jax.ShapeDtypeStruct((1, LANES), F32), jax.ShapeDtypeStruct((1, D_MODEL), F32)][:nloss],
        input_output_aliases={4 + nloss + nbg + t: 3 + t for t in range(nbg)},
        scratch_shapes=[pltpu.VMEM((D_MODEL, D_FF), BF16), pltpu.VMEM((D_MODEL, D_FF), BF16)]
        + [pltpu.SemaphoreType.DMA((nbg, N_PEER_CHIPS))] * (4 if nbg else 0),
        compiler_params=_cparams(),
    )(h, g, w_in, w_out, *(loss or ()), *bg_shards, *bg_lands)
    if nbg:
        return res[0], res[1], res[2], list(res[3:3 + nbg])
    return tuple(res)


def ffn_bwd(tag, dh, h, g, gate, up, w_in, w_out, slot, proj=None):
    T = dh.shape[0]
    tm = FFN_ROWS
    nch = D_FF // FF_CHUNK

    def body(dh_ref, h_ref, g_ref, gate_ref, up_ref, wi_ref, wo_ref, *rest):
        if proj is not None:
            wp_ref, rest = rest[0], rest[1:]
        dhin_ref, xn_ref, dg_ref, du_ref, a_ref, dhb_ref, dnorm_ref = rest[:7]
        @pl.when(pl.program_id(0) == 0)
        def _():
            dnorm_ref[...] = jnp.zeros_like(dnorm_ref)

        dhh = dh_ref[...]
        dhb = (FFN_RES_SCALE * dhh).astype(BF16)
        dhb_ref[...] = dhb
        dxn = None
        for j in range(nch):
            cols = slice(j * FF_CHUNK, (j + 1) * FF_CHUNK)
            da = _dot_nt(dhb, wo_ref[2 * j:2 * j + 2].reshape(FF_CHUNK, D_MODEL))
            gt = gate_ref[:, cols].astype(F32)
            u = up_ref[:, cols].astype(F32)
            s = jax.nn.sigmoid(gt)
            silu = gt * s
            a_ref[:, cols] = (silu * u).astype(BF16)
            dgate = (da * u * (s * (1.0 + gt * (1.0 - s)))).astype(BF16)
            dup = (da * silu).astype(BF16)
            dg_ref[:, cols] = dgate
            du_ref[:, cols] = dup
            part = _dot_nt(dgate, wi_ref[j]) + _dot_nt(dup, wi_ref[nch + j])
            dxn = part if dxn is None else dxn + part
        hh = h_ref[...]
        gg = g_ref[...]
        r = _rstd(hh)
        hr = hh * r
        xn_ref[...] = (hr * gg).astype(BF16)
        dnorm_ref[...] += jnp.sum(dxn * hr, axis=0, keepdims=True)
        gd = gg * dxn
        dh_in = dhh + r * (gd - hr * jnp.mean(gd * hr, axis=-1, keepdims=True))
        dhin_ref[...] = dh_in
        if proj is not None:
            dyb_ref, da_ref = rest[7:9]
            dyb = dh_in.astype(BF16)
            dyb_ref[...] = dyb
            da_ref[...] = _dot_nt(dyb, wp_ref[...].reshape(D_MODEL, D_MODEL)).astype(BF16)

    row = pl.BlockSpec((tm, D_MODEL), lambda i: (i, 0))
    ff = pl.BlockSpec((tm, D_FF), lambda i: (i, 0))
    vec = pl.BlockSpec((1, D_MODEL), lambda i: (0, 0))
    in_specs = [row, row, vec, ff, ff, _ffn_w_in_spec(slot), _ffn_w_out_spec(slot)]
    args = [dh, h, g, gate, up, w_in, w_out]
    out_specs = [row, row, ff, ff, ff, row, vec]
    out_shape = [
        jax.ShapeDtypeStruct((T, D_MODEL), F32),
        jax.ShapeDtypeStruct((T, D_MODEL), BF16),
        jax.ShapeDtypeStruct((T, D_FF), BF16),
        jax.ShapeDtypeStruct((T, D_FF), BF16),
        jax.ShapeDtypeStruct((T, D_FF), BF16),
        jax.ShapeDtypeStruct((T, D_MODEL), BF16),
        jax.ShapeDtypeStruct((1, D_MODEL), F32),
    ]
    if proj is not None:
        w_sq, t = proj
        in_specs.append(pl.BlockSpec((N_CHIPS, None, SQ_ROWS, D_MODEL), lambda i: (0, t, 0, 0),
                                     pipeline_mode=pl.Buffered(1)))
        args.append(w_sq)
        out_specs += [row, row]
        out_shape += [jax.ShapeDtypeStruct((T, D_MODEL), BF16)] * 2
    return pl.pallas_call(
        body,
        name=f"ffn_bwd_{tag}",
        grid=(T // tm,),
        in_specs=in_specs,
        out_specs=out_specs,
        out_shape=out_shape,
        compiler_params=_cparams(),
    )(*args)


def rms_linear(name, h, g, w, w_spec, w_prep, n_out, tn, *, rope=None, rope_blocks=None, scale=None):
    T = h.shape[0]
    tm = 512
    extra, extra_specs = [], []
    if rope is not None:
        extra += list(rope)
        extra_specs += [pl.BlockSpec((tm, LANES), lambda i, j: (i, 0))] * 2
    if scale is not None:
        extra.append(scale)
        extra_specs.append(pl.BlockSpec((1, tn), lambda i, j: (0, j)))

    def body(h_ref, g_ref, w_ref, *rest):
        rest = list(rest)
        cos_ref = sin_ref = sc_ref = None
        if rope is not None:
            cos_ref, sin_ref = rest[0], rest[1]
            rest = rest[2:]
        if scale is not None:
            sc_ref = rest[0]
            rest = rest[1:]
        out_ref, xn_s = rest

        @pl.when(pl.program_id(1) == 0)
        def _():
            hh = h_ref[...]
            xn_s[...] = (hh * _rstd(hh) * g_ref[...]).astype(BF16)

        y = _dot(xn_s[...], w_prep(w_ref[...]))
        if rope is not None:
            turned = y * _tile_lanes(cos_ref[...], tn) + _swap32(y) * _tile_lanes(sin_ref[...], tn)
            y = turned if rope_blocks is None else jnp.where(pl.program_id(1) < rope_blocks, turned, y)
        if scale is not None:
            y = y * sc_ref[...]
        out_ref[...] = y.astype(BF16)

    return pl.pallas_call(
        body,
        name=name,
        grid=(T // tm, n_out // tn),
        in_specs=[
            pl.BlockSpec((tm, D_MODEL), lambda i, j: (i, 0)),
            pl.BlockSpec((1, D_MODEL), lambda i, j: (0, 0)),
            w_spec,
        ] + extra_specs,
        out_specs=pl.BlockSpec((tm, tn), lambda i, j: (i, j)),
        out_shape=jax.ShapeDtypeStruct((T, n_out), BF16),
        scratch_shapes=[pltpu.VMEM((tm, D_MODEL), BF16)],
        compiler_params=_cparams(),
    )(h, g, w, *extra)


QKV_ROWS = 512


def _qkv_w_spec():
    return pl.BlockSpec((N_CHIPS, D_MODEL, QKV_COLS), lambda i: (0, 0, 0), pipeline_mode=pl.Buffered(1))


def qkv_fwd(h, g, w_qkv, scale):
    T = h.shape[0]
    tm = QKV_ROWS

    def body(h_ref, g_ref, w_ref, sc_ref, out_ref):
        hh = h_ref[...]
        xn = (hh * _rstd(hh) * g_ref[...]).astype(BF16)
        for s in range(N_CHIPS):
            cols = slice(s * QKV_COLS, (s + 1) * QKV_COLS)
            out_ref[:, cols] = (_dot(xn, w_ref[s]) * sc_ref[:, cols]).astype(BF16)

    return pl.pallas_call(
        body,
        name="sb_qkv",
        grid=(T // tm,),
        in_specs=[
            pl.BlockSpec((tm, D_MODEL), lambda i: (i, 0)),
            pl.BlockSpec((1, D_MODEL), lambda i: (0, 0)),
            _qkv_w_spec(),
            pl.BlockSpec((1, 3 * D_MODEL), lambda i: (0, 0)),
        ],
        out_specs=pl.BlockSpec((tm, 3 * D_MODEL), lambda i: (i, 0)),
        out_shape=jax.ShapeDtypeStruct((T, 3 * D_MODEL), BF16),
        compiler_params=_cparams(),
    )(h, g, w_qkv, scale)


def qkv_bwd(dy, w_qkv, h, g, dres, bg_halves=()):
    T = h.shape[0]
    tm = QKV_ROWS
    nbg = len(bg_halves)
    nt = T // tm

    def body(dy_ref, w_ref, h_ref, g_ref, dres_ref, *rest):
        dh_ref, xn_ref, dg_ref = rest[nbg:nbg + 3]
        if nbg:
            def share():
                return sibling_share_copies(rest[:nbg], rest[nbg + 3:2 * nbg + 3], *rest[2 * nbg + 3:])

            @pl.when(pl.program_id(0) == 0)
            def _():
                for cp in share():
                    cp.start()

            @pl.when(pl.program_id(0) == nt - 1)
            def _():
                for cp in share():
                    cp.wait()

        @pl.when(pl.program_id(0) == 0)
        def _():
            dg_ref[...] = jnp.zeros_like(dg_ref)

        dxn = None
        for s in range(N_CHIPS):
            part = _dot_nt(dy_ref[:, s * QKV_COLS:(s + 1) * QKV_COLS], w_ref[s])
            dxn = part if dxn is None else dxn + part
        hh = h_ref[...]
        gg = g_ref[...]
        r = _rstd(hh)
        hr = hh * r
        xn_ref[...] = (hr * gg).astype(BF16)
        dg_ref[...] += jnp.sum(dxn * hr, axis=0, keepdims=True)
        gd = gg * dxn
        dh_ref[...] = dres_ref[...] + r * (gd - hr * jnp.mean(gd * hr, axis=-1, keepdims=True))

    row = pl.BlockSpec((tm, D_MODEL), lambda i: (i, 0))
    vec = pl.BlockSpec((1, D_MODEL), lambda i: (0, 0))
    res = pl.pallas_call(
        body,
        name="sb_qkv_bwd",
        grid=(nt,),
        in_specs=[pl.BlockSpec((tm, 3 * D_MODEL), lambda i: (i, 0)), _qkv_w_spec(), row, vec, row] + [ANY] * nbg,
        out_specs=[row, row, vec] + [ANY] * nbg,
        out_shape=[
            jax.ShapeDtypeStruct((T, D_MODEL), F32),
            jax.ShapeDtypeStruct((T, D_MODEL), BF16),
            jax.ShapeDtypeStruct((1, D_MODEL), F32),
        ] + [jax.ShapeDtypeStruct(b.shape, b.dtype) for b in bg_halves],
        scratch_shapes=[pltpu.SemaphoreType.DMA((nbg,))] * (2 if nbg else 0),
        compiler_params=_cparams(),
    )(dy, w_qkv, h, g, dres, *bg_halves)
    return res[0], res[1], res[2], list(res[3:])


def linear_res(name, a, w_sq, t, res, bg_lands=()):
    T = a.shape[0]
    tm = 512
    nbg = len(bg_lands)
    nt = T // tm

    def body(a_ref, w_ref, res_ref, *rest):
        out_ref = rest[nbg]
        if nbg:
            gather = GatherOps([l.shape[2] for l in bg_lands], None, rest[nbg + 1:2 * nbg + 1], None, None,
                               *rest[2 * nbg + 1:])

            @pl.when(pl.program_id(0) == 0)
            def _():
                gather.start_forwards()

        out_ref[...] = res_ref[...] + _dot(a_ref[...], w_ref[...].reshape(D_MODEL, D_MODEL))
        if nbg:
            @pl.when(pl.program_id(0) == nt - 1)
            def _():
                gather.wait_forwards()

    row = pl.BlockSpec((tm, D_MODEL), lambda i: (i, 0))
    res_ = pl.pallas_call(
        body,
        name=name,
        grid=(nt,),
        in_specs=[row, pl.BlockSpec((N_CHIPS, None, SQ_ROWS, D_MODEL), lambda i: (0, t, 0, 0)), row] + [ANY] * nbg,
        out_specs=[row] + [ANY] * nbg,
        out_shape=[jax.ShapeDtypeStruct((T, D_MODEL), F32)] + [jax.ShapeDtypeStruct(l.shape, l.dtype) for l in bg_lands],
        input_output_aliases={3 + k: 1 + k for k in range(nbg)},
        scratch_shapes=[pltpu.SemaphoreType.DMA((nbg, N_PEER_CHIPS))] * (2 if nbg else 0),
        compiler_params=_cparams(),
    )(a, w_sq, res, *bg_lands)
    return (res_[0], list(res_[1:])) if nbg else res_[0]


def linear_bwd_rms(name, pairs, h, g, dres, nch, tn, tm=256):
    T = h.shape[0]
    npair = len(pairs)

    def body(*refs):
        dy_refs = refs[:npair]
        w_refs = refs[npair:2 * npair]
        h_ref, g_ref, dres_ref, dh_ref, xn_ref, dg_ref, acc_s = refs[2 * npair:]
        i = pl.program_id(0)
        j = pl.program_id(1)

        @pl.when(j == 0)
        def _():
            acc_s[...] = jnp.zeros_like(acc_s)

        @pl.when((i == 0) & (j == 0))
        def _():
            dg_ref[...] = jnp.zeros_like(dg_ref)

        part = None
        for p in range(npair):
            d = _dot_nt(dy_refs[p][...], pairs[p][3](w_refs[p][...]))
            part = d if part is None else part + d
        acc_s[...] += part

        @pl.when(j == nch - 1)
        def _():
            dxn = acc_s[...]
            hh = h_ref[...]
            gg = g_ref[...]
            r = _rstd(hh)
            hr = hh * r
            xn_ref[...] = (hr * gg).astype(BF16)
            dg_ref[...] += jnp.sum(dxn * hr, axis=0, keepdims=True)
            gd = gg * dxn
            dh_ref[...] = dres_ref[...] + r * (gd - hr * jnp.mean(gd * hr, axis=-1, keepdims=True))

    row = pl.BlockSpec((tm, D_MODEL), lambda i, j: (i, 0))
    vec = pl.BlockSpec((1, D_MODEL), lambda i, j: (0, 0))
    return pl.pallas_call(
        body,
        name=name,
        grid=(T // tm, nch),
        in_specs=[pl.BlockSpec((tm, tn), lambda i, j: (i, j))] * npair + [p[2] for p in pairs] + [row, vec, row],
        out_specs=[row, row, vec],
        out_shape=[
            jax.ShapeDtypeStruct((T, D_MODEL), F32),
            jax.ShapeDtypeStruct((T, D_MODEL), BF16),
            jax.ShapeDtypeStruct((1, D_MODEL), F32),
        ],
        scratch_shapes=[pltpu.VMEM((tm, D_MODEL), F32)],
        compiler_params=_cparams(),
    )(*[p[0] for p in pairs], *[p[1] for p in pairs], h, g, dres)


DW_TOKENS = 4096


def mm_tn(name, a, b, tk, tn, out_block, out_index, out_shape, prev=None, tt=DW_TOKENS):
    T = a.shape[0]
    ns, r = out_block[1], out_block[3]
    tt = min(tt, T)
    nt = T // tt

    def body(*refs):
        a_ref, b_ref = refs[:2]
        out_ref, copy_ref = refs[-2:]
        t = pl.program_id(2)
        res = _dot_tn(a_ref[...], b_ref[...])

        @pl.when(t == 0)
        def _():
            for u in range(ns):
                out_ref[u] = res[u * r:(u + 1) * r]

        @pl.when(t > 0)
        def _():
            for u in range(ns):
                out_ref[u] += res[u * r:(u + 1) * r]

        @pl.when(t == nt - 1)
        def _():
            copy_ref[...] = out_ref[...].astype(BF16)

    in_specs = [
        pl.BlockSpec((tt, tk), lambda k, n, t: (t, k)),
        pl.BlockSpec((tt, tn), lambda k, n, t: (t, n)),
    ]
    args = [a, b]
    aliases = {}
    if prev is not None:
        in_specs += [pl.BlockSpec(memory_space=pl.ANY)] * 2
        args += list(prev)
        aliases = {2: 0, 3: 1}
    out_spec = pl.BlockSpec(out_block, lambda k, n, t: out_index(k, n))
    return tuple(pl.pallas_call(
        body,
        name=name,
        grid=(a.shape[1] // tk, b.shape[1] // tn, nt),
        in_specs=in_specs,
        out_specs=[out_spec, out_spec],
        out_shape=[jax.ShapeDtypeStruct(out_shape, F32), jax.ShapeDtypeStruct(out_shape, BF16)],
        input_output_aliases=aliases,
        compiler_params=_cparams(),
    )(*args))


SB_BLOCK = 256
SB_QROWS = 256
SB_QROWS_BWD = 256
SB_UNDERFLOW_BITS = 140.0
SB_CHUNK = 128


LOG2E = 1.4426950408889634


def _softplus2(z2):
    sign = jnp.uint32(0x80000000)
    neg_abs = lax.bitcast_convert_type(lax.bitcast_convert_type(z2, jnp.uint32) | sign, F32)
    return jnp.log2(1.0 + jnp.exp2(neg_abs)) + jnp.maximum(z2, 0.0)


def _twice(x):
    return jnp.concatenate([x, x], axis=1)


def sb_fwd(qkv, bg_shards=(), bg_lands=()):
    T = qkv.shape[0]
    tq, tk = SB_QROWS, SB_BLOCK
    ratio = tq // tk
    npair = SB_HEADS // 2
    nbg = len(bg_shards)
    nq = T // tq

    def body(q_ref, k_ref, v_ref, *rest):
        bg_in = rest[:nbg]
        o_ref, tot_ref, first_ref = rest[2 * nbg:2 * nbg + 3]
        bg_out = rest[2 * nbg + 3:3 * nbg + 3]
        acc_s, c_s, z_s, w_s, kmax_s = rest[3 * nbg + 3:3 * nbg + 8]
        p = pl.program_id(0)
        i = pl.program_id(1)
        if nbg:
            gather = GatherOps([s.shape[1] for s in bg_shards], bg_in, bg_out, *rest[3 * nbg + 8:])

            @pl.when((p == 0) & (i == 0))
            def _():
                gather.start_ici()

        @pl.when(i == 0)
        def _():
            kmax_s[...] = jnp.max(jnp.abs(k_ref[...]), axis=0, keepdims=True).astype(F32)

        q = q_ref[...]
        lane = lax.broadcasted_iota(jnp.int32, (tq, LANES), 1)
        first = lane < HEAD_DIM
        zero = jnp.zeros_like(q)
        q_heads = (jnp.where(first, q, zero), jnp.where(first, zero, q))
        row = lax.broadcasted_iota(jnp.int32, (tq, tk), 0)
        col = lax.broadcasted_iota(jnp.int32, (tq, tk), 1)
        visible = [col + r * tk < row for r in range(ratio)]
        krow = lax.broadcasted_iota(jnp.int32, (tk, tk), 0)
        kcol = lax.broadcasted_iota(jnp.int32, (tk, tk), 1)
        from_s = (krow >= kcol).astype(BF16)
        acc_s[...] = jnp.zeros_like(acc_s)
        c_s[...] = jnp.zeros_like(c_s)

        def rows(j):
            return pl.ds(pl.multiple_of(j * tk, tk), tk)

        def logits(j):
            kb = k_ref[rows(j), :]
            for hd in range(2):
                z_s[hd] = _dot_nt(q_heads[hd], kb) * LOG2E

        def flush(j):
            vb = v_ref[rows(j), :]
            for hd in range(2):
                acc_s[hd] += _dot(w_s[hd], vb)

        def block(j, mask=None, flush_block=None):
            if flush_block is not None:
                flush(flush_block)
            chunks = [(hd, slice(r0, r0 + SB_CHUNK)) for hd in range(2) for r0 in range(0, tq, SB_CHUNK)]
            k_next = k_ref[rows(jnp.maximum(j - 1, 0)), :]
            es, sums = [], []
            for hd, rs in chunks:
                z2 = z_s[hd, rs, :]
                z_s[hd, rs, :] = _dot_nt(q_heads[hd][rs, :], k_next) * LOG2E
                if mask is not None:
                    z2 = jnp.where(mask if mask.ndim == 0 else mask[rs, :], z2, NEG_BIG)
                sp = _softplus2(z2)
                c = c_s[hd, rs, :]
                es.append(z2 + _twice(c))
                c_s[hd, rs, :] = c - jnp.sum(sp, axis=1, keepdims=True)
                sums.append(_dot(sp.astype(BF16), from_s))
            for (hd, rs), e, s in zip(chunks, es, sums):
                w_s[hd, rs, :] = jnp.exp2(e - s).astype(BF16)

        z_bound = [LOG2E * jnp.sum(jnp.abs(q_heads[hd].astype(F32)) * kmax_s[...], axis=1, keepdims=True)
                   for hd in range(2)]

        def more_keys_matter():
            top = jnp.maximum(c_s[0] + z_bound[0], c_s[1] + z_bound[1])
            return (jnp.max(top) >= -SB_UNDERFLOW_BITS).astype(jnp.int32)

        assert ratio == 1
        logits(i)
        block(i, visible[0])

        def trip(carry):
            trips, _ = carry
            j = i - 1 - trips
            block(j, flush_block=j + 1)
            return trips + 1, more_keys_matter()

        trips, _ = lax.while_loop(lambda carry: jnp.logical_and(carry[0] < i, carry[1] > 0), trip,
                                  (jnp.int32(0), jnp.int32(1)))
        first_walked = i - trips
        flush(first_walked)
        first_ref[p, i] = first_walked.astype(F32)
        o_ref[...] = jnp.where(first, acc_s[0], acc_s[1]).astype(BF16)
        tot_ref[...] = jnp.where(first, c_s[0], c_s[1])
        if nbg:
            @pl.when((p == npair - 1) & (i == nq - 1))
            def _():
                gather.wait_ici()

    res = pl.pallas_call(
        body,
        name="sb_fwd",
        grid=(npair, nq),
        in_specs=[
            pl.BlockSpec((tq, LANES), lambda p, i: (i, p)),
            pl.BlockSpec((T, LANES), lambda p, i: (0, npair + p)),
            pl.BlockSpec((T, LANES), lambda p, i: (0, 2 * npair + p)),
        ] + [ANY] * (2 * nbg),
        out_specs=[pl.BlockSpec((tq, LANES), lambda p, i: (i, p))] * 2 + [pl.BlockSpec(memory_space=pltpu.SMEM)]
        + [ANY] * nbg,
        out_shape=[jax.ShapeDtypeStruct((T, D_MODEL), BF16), jax.ShapeDtypeStruct((T, D_MODEL), F32),
                   jax.ShapeDtypeStruct((npair, nq), F32)]
        + [jax.ShapeDtypeStruct(l.shape, l.dtype) for l in bg_lands],
        input_output_aliases={3 + nbg + t: 3 + t for t in range(nbg)},
        scratch_shapes=[
            pltpu.VMEM((2, tq, LANES), F32), pltpu.VMEM((2, tq, LANES), F32),
            pltpu.VMEM((2, tq, tk), F32), pltpu.VMEM((2, tq, tk), BF16),
            pltpu.VMEM((1, LANES), F32),
        ] + [pltpu.SemaphoreType.DMA((nbg, N_PEER_CHIPS))] * (2 if nbg else 0),
        compiler_params=_cparams(),
    )(qkv, qkv, qkv, *bg_shards, *bg_lands)
    return res[0], res[1], res[2], list(res[3:])


def sb_bwd(qkv, do, tot, first_block, bg_parts=()):
    T = qkv.shape[0]
    tq, tk = SB_QROWS_BWD, SB_BLOCK
    ratio = tq // tk
    npair = SB_HEADS // 2
    nq = T // tq
    nk = T // tk
    nbg = len(bg_parts)
    assert SB_QROWS == SB_QROWS_BWD

    def body(first_ref, q_ref, k_ref, v_ref, do_ref, tot_ref, *rest):
        bg_in = rest[:nbg]
        dq_ref, dk_ref, dv_ref = rest[nbg:nbg + 3]
        bg_out = rest[nbg + 3:2 * nbg + 3]
        dkt_s, dvt_s, dq_s, rest_s, cg_s, z_s, da_s, dz_s, a_s = rest[2 * nbg + 3:2 * nbg + 12]
        i = pl.program_id(1)
        start = jnp.clip(first_ref[pl.program_id(0), i].astype(jnp.int32), 0, i)
        if nbg:
            @pl.when((pl.program_id(0) == 0) & (i == 0))
            def _():
                for cp in chip_partial_copies(bg_in, bg_out, *rest[2 * nbg + 12:]):
                    cp.start()

        @pl.when(i == 0)
        def _():
            dkt_s[...] = jnp.zeros_like(dkt_s)
            dvt_s[...] = jnp.zeros_like(dvt_s)

        q = q_ref[...]
        do_ = do_ref[...]
        tot_ = tot_ref[...]
        q_t = q.astype(F32).T.astype(BF16)
        do_t = do_.astype(F32).T.astype(BF16)
        lane = lax.broadcasted_iota(jnp.int32, (tq, LANES), 1)
        first = lane < HEAD_DIM
        zero = jnp.zeros_like(q)
        q_heads = (jnp.where(first, q, zero), jnp.where(first, zero, q))
        do_heads = (jnp.where(first, do_, zero), jnp.where(first, zero, do_))
        row = lax.broadcasted_iota(jnp.int32, (tq, tk), 0)
        col = lax.broadcasted_iota(jnp.int32, (tq, tk), 1)
        visible = [col + r * tk < row for r in range(ratio)]
        krow = lax.broadcasted_iota(jnp.int32, (tk, tk), 0)
        kcol = lax.broadcasted_iota(jnp.int32, (tk, tk), 1)
        before = (krow < kcol).astype(BF16)
        from_s = (krow >= kcol).astype(BF16)
        last = ratio * i + ratio - 1
        rest_s[0] = jnp.broadcast_to(tot_[:, 0:1], (tq, LANES))
        rest_s[1] = jnp.broadcast_to(tot_[:, HEAD_DIM:HEAD_DIM + 1], (tq, LANES))
        cg_s[...] = jnp.zeros_like(cg_s)
        dq_s[...] = jnp.zeros_like(dq_s)
        dz_s[...] = jnp.zeros_like(dz_s)
        a_s[...] = jnp.zeros_like(a_s)

        def rows(j):
            return pl.ds(pl.multiple_of(j * tk, tk), tk)

        def logits(j):
            kb = k_ref[rows(j), :]
            vb = v_ref[rows(j), :]
            for hd in range(2):
                z_s[hd] = _dot_nt(q_heads[hd], kb) * LOG2E
                da_s[hd] = _dot_nt(do_heads[hd], vb)

        def flush(j):
            kb = k_ref[rows(j), :]
            for hd in range(2):
                dims = slice(hd * HEAD_DIM, (hd + 1) * HEAD_DIM)
                dq_s[hd] += _dot(dz_s[hd], kb)
                dkt_s[j, dims, :] += _dot(q_t[dims, :], dz_s[hd])
                dvt_s[j, dims, :] += _dot(do_t[dims, :], a_s[hd])

        def block(j, mask=None):
            flush(jnp.maximum(j - 1, 0))
            chunks = [(hd, slice(r0, r0 + SB_CHUNK)) for hd in range(2) for r0 in range(0, tq, SB_CHUNK)]
            nxt = rows(jnp.minimum(j + 1, last))
            k_next = k_ref[nxt, :]
            v_next = v_ref[nxt, :]
            stage1 = []
            for hd, rs in chunks:
                z2 = z_s[hd, rs, :]
                z_s[hd, rs, :] = _dot_nt(q_heads[hd][rs, :], k_next) * LOG2E
                if mask is not None:
                    z2 = jnp.where(mask if mask.ndim == 0 else mask[rs, :], z2, NEG_BIG)
                sp = _softplus2(z2)
                rest = rest_s[hd, rs, :] + jnp.sum(sp, axis=1, keepdims=True)
                rest_s[hd, rs, :] = rest
                stage1.append((z2 + _twice(rest), z2 - sp, _dot(sp.astype(BF16), from_s)))
            stage2 = []
            for (hd, rs), (e, log2_beta, ahead) in zip(chunks, stage1):
                a = jnp.exp2(e - ahead)
                g = a * da_s[hd, rs, :]
                da_s[hd, rs, :] = _dot_nt(do_heads[hd][rs, :], v_next)
                cg = cg_s[hd, rs, :]
                a_s[hd, rs, :] = a.astype(BF16)
                cg_s[hd, rs, :] = cg + jnp.sum(g, axis=1, keepdims=True)
                stage2.append((g, g + _twice(cg), log2_beta, _dot(g.astype(BF16), before)))
            for (hd, rs), (g, g_from, log2_beta, g_before) in zip(chunks, stage2):
                dz_s[hd, rs, :] = (g - jnp.exp2(log2_beta) * (g_from + g_before)).astype(BF16)

        assert ratio == 1
        logits(start)

        @pl.loop(start, i)
        def _(j):
            block(j)

        block(i, visible[0])
        flush(last)
        dq_ref[...] = (jnp.where(first, dq_s[0], dq_s[1]) * ATTN_SCALE).astype(BF16)

        @pl.when(i == nq - 1)
        def _():
            @pl.loop(0, nk)
            def _(b):
                dk_ref[rows(b), :] = dkt_s[b].T.astype(BF16)
                dv_ref[rows(b), :] = dvt_s[b].T.astype(BF16)

        if nbg:
            @pl.when((pl.program_id(0) == npair - 1) & (i == nq - 1))
            def _():
                for cp in chip_partial_copies(bg_in, bg_out, *rest[2 * nbg + 12:]):
                    cp.wait()

    qblk = pl.BlockSpec((tq, LANES), lambda p, i: (i, p))
    full = pl.BlockSpec((T, LANES), lambda p, i: (0, p))
    res = pl.pallas_call(
        body,
        name="sb_bwd",
        grid=(npair, nq),
        in_specs=[
            pl.BlockSpec(memory_space=pltpu.SMEM),
            qblk,
            pl.BlockSpec((T, LANES), lambda p, i: (0, npair + p)),
            pl.BlockSpec((T, LANES), lambda p, i: (0, 2 * npair + p)),
            qblk, qblk,
        ] + [ANY] * nbg,
        out_specs=[qblk, full, full] + [ANY] * nbg,
        out_shape=[jax.ShapeDtypeStruct((T, D_MODEL), BF16)] * 3
        + [jax.ShapeDtypeStruct(b.shape, b.dtype) for b in bg_parts],
        scratch_shapes=[
            pltpu.VMEM((nk, LANES, tk), F32), pltpu.VMEM((nk, LANES, tk), F32),
            pltpu.VMEM((2, tq, LANES), F32), pltpu.VMEM((2, tq, LANES), F32), pltpu.VMEM((2, tq, LANES), F32),
            pltpu.VMEM((2, tq, tk), F32), pltpu.VMEM((2, tq, tk), F32),
            pltpu.VMEM((2, tq, tk), BF16), pltpu.VMEM((2, tq, tk), BF16),
        ] + [pltpu.SemaphoreType.DMA((nbg, N_PEER_CHIPS))] * (2 if nbg else 0),
        compiler_params=_cparams(),
    )(first_block, qkv, qkv, qkv, do, tot, *bg_parts)
    return res[0], res[1], res[2], list(res[3:])


def _swa_valid(n):
    qi = lax.broadcasted_iota(jnp.int32, (WINDOW, 2 * WINDOW), 0)
    ki = lax.broadcasted_iota(jnp.int32, (WINDOW, 2 * WINDOW), 1)
    diff = qi + WINDOW - ki
    return (diff >= 0) & (diff < WINDOW) & ((n > 0) | (ki >= WINDOW))


def _to_half(x, first, src, dst):
    keep = first if src == 0 else jnp.logical_not(first)
    x = jnp.where(keep, x, jnp.zeros_like(x))
    if src != dst:
        x = pltpu.roll(x.astype(F32), HEAD_DIM, 1).astype(BF16)
    return x


SWA_GROUP = SWA_Q_HEADS // SWA_KV_HEADS


def _swa_cols(h):
    return slice((h // 2) * LANES, (h // 2 + 1) * LANES)


def _swa_kv_pair(h):
    return (h // SWA_GROUP) // 2


def _swa_kv_half(h):
    return (h // SWA_GROUP) % 2


def _kv_band(prev_ref, cur_ref, pb):
    cols = slice(pb * LANES, (pb + 1) * LANES)
    return jnp.concatenate([prev_ref[:, cols], cur_ref[:, cols]], axis=0)


def _swa_specs(T):
    nb = T // WINDOW
    kv_w = SWA_KV_HEADS * HEAD_DIM
    qrow = pl.BlockSpec((WINDOW, D_MODEL), lambda n: (n, 0))
    kv = [pl.BlockSpec((WINDOW, kv_w), lambda n, col=col, back=back: (jnp.maximum(n - back, 0), col))
          for col in (0, 1) for back in (0, 1)]
    smem = pl.BlockSpec(memory_space=pltpu.SMEM)
    return nb, qrow, kv, smem


def swa_fwd(q, kv, sinks):
    T = q.shape[0]
    nb, qrow, kv_specs, smem = _swa_specs(T)

    def body(sink_ref, q_ref, kc_ref, kp_ref, vc_ref, vp_ref, o_ref, lse_ref):
        n = pl.program_id(0)
        lane = lax.broadcasted_iota(jnp.int32, (WINDOW, LANES), 1)
        first = lane < HEAD_DIM
        valid = _swa_valid(n)
        k2 = [_kv_band(kp_ref, kc_ref, pb) for pb in range(SWA_KV_HEADS // 2)]
        v2 = [_kv_band(vp_ref, vc_ref, pb) for pb in range(SWA_KV_HEADS // 2)]
        logits = [jnp.where(valid, _dot_nt(_to_half(q_ref[:, _swa_cols(h)], first, h % 2, _swa_kv_half(h)),
                                            k2[_swa_kv_pair(h)]), NEG_BIG) for h in range(SWA_Q_HEADS)]
        probs = []
        lse_acc = jnp.zeros((WINDOW, LANES), F32)
        for h, s in enumerate(logits):
            sink = sink_ref[h]
            m = jnp.maximum(jnp.max(s, axis=1, keepdims=True), sink)
            p = jnp.exp(s - m)
            den = jnp.sum(p, axis=1, keepdims=True) + jnp.exp(sink - m)
            probs.append((p / den).astype(BF16))
            lse_acc = jnp.where(lane == h, m + jnp.log(den), lse_acc)
        outs = []
        for h, p in enumerate(probs):
            o = _dot(p, v2[_swa_kv_pair(h)])
            outs.append(pltpu.roll(o, HEAD_DIM, 1) if h % 2 != _swa_kv_half(h) else o)
        for pair in range(SWA_Q_HEADS // 2):
            o_ref[:, _swa_cols(2 * pair)] = jnp.where(first, outs[2 * pair], outs[2 * pair + 1]).astype(BF16)
        lse_ref[...] = lse_acc

    return pl.pallas_call(
        body,
        name="swa_fwd",
        grid=(nb,),
        in_specs=[smem, qrow] + kv_specs,
        out_specs=[qrow, pl.BlockSpec((WINDOW, LANES), lambda n: (n, 0))],
        out_shape=[jax.ShapeDtypeStruct((T, D_MODEL), BF16), jax.ShapeDtypeStruct((T, LANES), F32)],
        compiler_params=_cparams(),
    )(sinks, q, kv, kv, kv, kv)


def swa_bwd(q, kv, sinks, do, o, lse, cos, sin):
    T = q.shape[0]
    nb, qrow, kv_specs, smem = _swa_specs(T)
    kv_w = SWA_KV_HEADS * HEAD_DIM

    def body(sink_ref, q_ref, kc_ref, kp_ref, vc_ref, vp_ref, do_ref, o_ref, lse_ref, cos_ref, sin_ref,
             dq_ref, own_ref, prv_ref, dsink_ref):
        n = pl.program_id(0)

        @pl.when(n == 0)
        def _():
            dsink_ref[...] = jnp.zeros_like(dsink_ref)

        lane = lax.broadcasted_iota(jnp.int32, (WINDOW, LANES), 1)
        lane1 = lax.broadcasted_iota(jnp.int32, (1, LANES), 1)
        first = lane < HEAD_DIM
        valid = _swa_valid(n)
        cos_ = cos_ref[...]
        sin_ = sin_ref[...]
        k2 = [_kv_band(kp_ref, kc_ref, pb) for pb in range(SWA_KV_HEADS // 2)]
        v2 = [_kv_band(vp_ref, vc_ref, pb) for pb in range(SWA_KV_HEADS // 2)]
        q_t = q_ref[...].astype(F32).T.astype(BF16)
        do_t = do_ref[...].astype(F32).T.astype(BF16)
        stage1 = []
        for h in range(SWA_Q_HEADS):
            a, b, pb = h % 2, _swa_kv_half(h), _swa_kv_pair(h)
            qh = _to_half(q_ref[:, _swa_cols(h)], first, a, b)
            doh = _to_half(do_ref[:, _swa_cols(h)], first, a, b)
            stage1.append((jnp.where(valid, _dot_nt(qh, k2[pb]), NEG_BIG), _dot_nt(doh, v2[pb])))
        deltas = []
        for pair in range(SWA_Q_HEADS // 2):
            prod = do_ref[:, _swa_cols(2 * pair)].astype(F32) * o_ref[:, _swa_cols(2 * pair)].astype(F32)
            deltas += [jnp.sum(jnp.where(first, prod, 0.0), axis=1, keepdims=True),
                       jnp.sum(jnp.where(first, 0.0, prod), axis=1, keepdims=True)]
        stage2 = []
        dsink = jnp.zeros((1, LANES), F32)
        for h, (s, dp) in enumerate(stage1):
            lse_h = lse_ref[:, h:h + 1]
            p = jnp.exp(s - lse_h)
            delta = deltas[h]
            p_sink = jnp.exp(sink_ref[h] - lse_h)
            dsink = dsink + jnp.where(lane1 == h, -jnp.sum(p_sink * delta, axis=0, keepdims=True), 0.0)
            stage2.append(((p * (dp - delta)).astype(BF16), p.astype(BF16)))
        dqs = []
        dk_t = [None] * SWA_KV_HEADS
        dv_t = [None] * SWA_KV_HEADS
        for h, (ds, pb16) in enumerate(stage2):
            kvh = h // SWA_GROUP
            dims = slice(h * HEAD_DIM, (h + 1) * HEAD_DIM)
            dq = _dot(ds, k2[_swa_kv_pair(h)])
            dqs.append(pltpu.roll(dq, HEAD_DIM, 1) if h % 2 != _swa_kv_half(h) else dq)
            dk_h = _dot(q_t[dims, :], ds)
            dv_h = _dot(do_t[dims, :], pb16)
            dk_t[kvh] = dk_h if dk_t[kvh] is None else dk_t[kvh] + dk_h
            dv_t[kvh] = dv_h if dv_t[kvh] is None else dv_t[kvh] + dv_h
        for pair in range(SWA_Q_HEADS // 2):
            dqp = jnp.where(first, dqs[2 * pair], dqs[2 * pair + 1])
            dq_ref[:, _swa_cols(2 * pair)] = ((dqp * cos_ + _swap32(dqp * sin_)) * ATTN_SCALE).astype(BF16)
        for pb in range(SWA_KV_HEADS // 2):
            dk2 = jnp.concatenate([dk_t[2 * pb], dk_t[2 * pb + 1]], axis=0).T
            dv2 = jnp.concatenate([dv_t[2 * pb], dv_t[2 * pb + 1]], axis=0).T
            kcols = slice(pb * LANES, (pb + 1) * LANES)
            vcols = slice(kv_w + pb * LANES, kv_w + (pb + 1) * LANES)
            prv_ref[:, kcols] = dk2[:WINDOW]
            own_ref[:, kcols] = dk2[WINDOW:]
            prv_ref[:, vcols] = dv2[:WINDOW]
            own_ref[:, vcols] = dv2[WINDOW:]
        dsink_ref[...] += dsink

    tab = pl.BlockSpec((WINDOW, LANES), lambda n: (n, 0))
    kvrow = pl.BlockSpec((WINDOW, 2 * kv_w), lambda n: (n, 0))
    return pl.pallas_call(
        body,
        name="swa_bwd",
        grid=(nb,),
        in_specs=[smem, qrow] + kv_specs + [qrow, qrow, tab, tab, tab],
        out_specs=[qrow, kvrow, kvrow, pl.BlockSpec((1, LANES), lambda n: (0, 0))],
        out_shape=[
            jax.ShapeDtypeStruct((T, D_MODEL), BF16),
            jax.ShapeDtypeStruct((T, 2 * kv_w), F32),
            jax.ShapeDtypeStruct((T, 2 * kv_w), F32),
            jax.ShapeDtypeStruct((1, LANES), F32),
        ],
        compiler_params=_cparams(),
    )(sinks, q, kv, kv, kv, kv, do, o, lse, cos, sin)


def kv_grad_combine(own, prv, cos, sin):
    T = own.shape[0]
    nb = T // WINDOW
    kv_w = SWA_KV_HEADS * HEAD_DIM

    def body(own_ref, nxt_ref, cos_ref, sin_ref, out_ref):
        n = pl.program_id(0)
        nxt = jnp.where(n + 1 < nb, nxt_ref[...], 0.0)
        tot = own_ref[...] + nxt
        dk = tot[:, :kv_w]
        c = _tile_lanes(cos_ref[...], kv_w)
        s = _tile_lanes(sin_ref[...], kv_w)
        out_ref[:, :kv_w] = (dk * c + _swap32(dk * s)).astype(BF16)
        out_ref[:, kv_w:] = tot[:, kv_w:].astype(BF16)

    tab = pl.BlockSpec((WINDOW, LANES), lambda n: (n, 0))
    kvrow = pl.BlockSpec((WINDOW, 2 * kv_w), lambda n: (n, 0))
    return pl.pallas_call(
        body,
        name="kv_grad_combine",
        grid=(nb,),
        in_specs=[kvrow, pl.BlockSpec((WINDOW, 2 * kv_w), lambda n: (jnp.minimum(n + 1, nb - 1), 0)), tab, tab],
        out_specs=kvrow,
        out_shape=jax.ShapeDtypeStruct((T, 2 * kv_w), BF16),
        compiler_params=_cparams(),
    )(own, prv, cos, sin)


ANY = pl.BlockSpec(memory_space=pl.ANY)


def _place():
    x, y, c = lax.axis_index("x"), lax.axis_index("y"), lax.axis_index("c")
    other_chips = [(1 - x, y), (x, 1 - y), (1 - x, 1 - y)]
    return x, y, c, 2 * x + y, other_chips


N_PEER_CHIPS = N_CHIPS - 1


class GatherOps:
    def __init__(self, rows, shards, lands, ici_send, ici_recv, d2d_send=None, d2d_recv=None):
        self.rows, self.shards, self.lands = rows, shards, lands
        self.ici_send, self.ici_recv, self.d2d_send, self.d2d_recv = ici_send, ici_recv, d2d_send, d2d_recv
        self.x, self.y, self.c, self.me, self.chips = _place()
        self.pairs = [(t, jdx) for t in range(len(rows)) for jdx in range(N_PEER_CHIPS)]

    def _half(self, ref, t, which):
        r = self.rows[t] // 2
        return ref.at[:, pl.ds(pl.multiple_of(which * r, 16), r), :]

    def _ici(self, t, jdx):
        px, py = self.chips[jdx]
        return pltpu.make_async_remote_copy(
            src_ref=self._half(self.shards[t], t, self.c), dst_ref=self._half(self.lands[t].at[self.me], t, self.c),
            send_sem=self.ici_send.at[t, jdx], recv_sem=self.ici_recv.at[t, jdx],
            device_id=(px, py, self.c), device_id_type=MESH)

    def _landed(self, t, jdx):
        px, py = self.chips[jdx]
        blk = self._half(self.lands[t].at[2 * px + py], t, self.c)
        return pltpu.make_async_remote_copy(
            src_ref=blk, dst_ref=blk, send_sem=self.ici_send.at[t, jdx], recv_sem=self.ici_recv.at[t, jdx],
            device_id=(px, py, self.c), device_id_type=MESH)

    def _d2d(self, t, jdx, which):
        px, py = self.chips[jdx]
        blk = self._half(self.lands[t].at[2 * px + py], t, which)
        return pltpu.make_async_remote_copy(
            src_ref=blk, dst_ref=blk, send_sem=self.d2d_send.at[t, jdx], recv_sem=self.d2d_recv.at[t, jdx],
            device_id=(self.x, self.y, 1 - self.c), device_id_type=MESH)

    def start_ici(self):
        for t, jdx in self.pairs:
            self._ici(t, jdx).start()

    def wait_ici(self):
        for t, jdx in self.pairs:
            self._landed(t, jdx).wait_recv()
        self.wait_ici_sends()

    def wait_ici_sends(self):
        for t, jdx in self.pairs:
            self._ici(t, jdx).wait_send()

    def forward_arrivals(self):
        for t, jdx in self.pairs:
            self._landed(t, jdx).wait_recv()
            self._d2d(t, jdx, self.c).start()

    def start_forwards(self):
        for t, jdx in self.pairs:
            self._d2d(t, jdx, self.c).start()

    def wait_forwards(self):
        for t, jdx in self.pairs:
            self._d2d(t, jdx, 1 - self.c).wait_recv()
            self._d2d(t, jdx, self.c).wait_send()


def all_gather_weights(shards, lands):
    n = len(shards)
    rows = [s.shape[1] for s in shards]

    def body(*refs):
        ins, outs = refs[:n], refs[2 * n:3 * n]
        ops = GatherOps(rows, ins, outs, *refs[3 * n:])
        ops.start_ici()
        ops.forward_arrivals()
        ops.wait_forwards()
        ops.wait_ici_sends()

    return pl.pallas_call(
        body,
        name="all_gather_weights",
        in_specs=[ANY] * (2 * n),
        out_specs=[ANY] * n,
        out_shape=[jax.ShapeDtypeStruct(l.shape, l.dtype) for l in lands],
        input_output_aliases={n + t: t for t in range(n)},
        scratch_shapes=[pltpu.SemaphoreType.DMA((n, N_PEER_CHIPS))] * 4,
    )(*shards, *lands)


def place_own_shard(name, shard, chip):
    nl, r, c = shard.shape

    def body(chip_ref, s_ref, o_ref):
        o_ref[...] = s_ref[...]

    return pl.pallas_call(
        body, name=name,
        grid_spec=pltpu.PrefetchScalarGridSpec(
            num_scalar_prefetch=1, grid=(nl,),
            in_specs=[pl.BlockSpec((None, r, c), lambda l, chip_ref: (l, 0, 0))],
            out_specs=pl.BlockSpec((None, None, r, c), lambda l, chip_ref: (chip_ref[0], l, 0, 0))),
        out_shape=jax.ShapeDtypeStruct((N_CHIPS,) + shard.shape, shard.dtype), compiler_params=_cparams(),
    )(chip, shard)


def exchange_halves(name, slabs):
    n = len(slabs)

    def body(*refs):
        ins, theirs = refs[:n], refs[n:2 * n]
        send_sems, recv_sems = refs[2 * n:]
        x, y, c, _, _ = _place()
        copies = []
        for t in range(n):
            cp = pltpu.make_async_remote_copy(
                src_ref=ins[t].at[1 - c], dst_ref=theirs[t], send_sem=send_sems.at[t],
                recv_sem=recv_sems.at[t], device_id=(x, y, 1 - c), device_id_type=MESH)
            cp.start()
            copies.append(cp)
        for cp in copies:
            cp.wait()

    return pl.pallas_call(
        body,
        name=name,
        in_specs=[ANY] * n,
        out_specs=[ANY] * n,
        out_shape=[jax.ShapeDtypeStruct(s.shape[1:], s.dtype) for s in slabs],
        scratch_shapes=[pltpu.SemaphoreType.DMA((n,)), pltpu.SemaphoreType.DMA((n,))],
    )(*slabs)


def chip_partial_copies(ins, outs, send_sems, recv_sems):
    _, _, c, me, chips = _place()
    return [pltpu.make_async_remote_copy(
        src_ref=ins[t].at[2 * px + py], dst_ref=outs[t].at[me], send_sem=send_sems.at[t, jdx],
        recv_sem=recv_sems.at[t, jdx], device_id=(px, py, c), device_id_type=MESH)
        for t in range(len(ins)) for jdx, (px, py) in enumerate(chips)]


def exchange_chip_partials(name, parts):
    n = len(parts)

    def body(*refs):
        copies = chip_partial_copies(refs[:n], refs[n:2 * n], *refs[2 * n:])
        for cp in copies:
            cp.start()
        for cp in copies:
            cp.wait()

    return pl.pallas_call(
        body,
        name=name,
        in_specs=[ANY] * n,
        out_specs=[ANY] * n,
        out_shape=[jax.ShapeDtypeStruct(p.shape, p.dtype) for p in parts],
        scratch_shapes=[pltpu.SemaphoreType.DMA((n, 3)), pltpu.SemaphoreType.DMA((n, 3))],
    )(*parts)


def sibling_share_copies(ins, outs, send_sems, recv_sems):
    x, y, c, _, _ = _place()
    return [pltpu.make_async_remote_copy(
        src_ref=ins[t], dst_ref=outs[t], send_sem=send_sems.at[t], recv_sem=recv_sems.at[t],
        device_id=(x, y, 1 - c), device_id_type=MESH) for t in range(len(ins))]


def share_reduced_halves(name, halves):
    n = len(halves)

    def body(*refs):
        copies = sibling_share_copies(refs[:n], refs[n:2 * n], *refs[2 * n:])
        for cp in copies:
            cp.start()
        for cp in copies:
            cp.wait()

    return pl.pallas_call(
        body,
        name=name,
        in_specs=[ANY] * n,
        out_specs=[ANY] * n,
        out_shape=[jax.ShapeDtypeStruct(h.shape, h.dtype) for h in halves],
        scratch_shapes=[pltpu.SemaphoreType.DMA((n,)), pltpu.SemaphoreType.DMA((n,))],
    )(*halves)


def _row_tile(r, c):
    tr = r
    while tr * c * 4 > (3 << 19) and tr % 16 == 0:
        tr //= 2
    return tr


def add_sibling(name, slab, theirs, core):
    _, ns, slots, r, c = slab.shape
    tr = _row_tile(r, c)

    def body(core_ref, a_ref, b_ref, o_ref):
        o_ref[...] = (a_ref[...] + b_ref[...]).astype(BF16)

    blk = pl.BlockSpec((None, None, tr, c), lambda s, l, i, core_ref: (s, l, i, 0))
    return pl.pallas_call(
        body, name=name,
        grid_spec=pltpu.PrefetchScalarGridSpec(
            num_scalar_prefetch=1, grid=(ns, slots, r // tr),
            in_specs=[pl.BlockSpec((None, None, None, tr, c), lambda s, l, i, core_ref: (core_ref[0], s, l, i, 0)), blk],
            out_specs=blk),
        out_shape=jax.ShapeDtypeStruct(theirs.shape, BF16), compiler_params=_cparams(),
    )(core, slab, theirs)


def sum_chips(name, recv, own, chip):
    _, slots, r, c = recv.shape
    tr = _row_tile(r, c)

    def body(chip_ref, r0, r1, r2, r3, own_ref, o_ref):
        me = chip_ref[0]
        mine = own_ref[...]
        terms = [jnp.where(me == s, mine, rr[...]).astype(F32) for s, rr in enumerate((r0, r1, r2, r3))]
        o_ref[...] = ((terms[0] + terms[1]) + terms[2]) + terms[3]

    def src(s):
        return pl.BlockSpec((None, None, tr, c),
                            lambda l, i, chip_ref: (jnp.where(chip_ref[0] == s, (s + 1) % N_CHIPS, s), l, i, 0))

    return pl.pallas_call(
        body, name=name,
        grid_spec=pltpu.PrefetchScalarGridSpec(
            num_scalar_prefetch=1, grid=(slots, r // tr),
            in_specs=[src(0), src(1), src(2), src(3),
                      pl.BlockSpec((None, None, tr, c), lambda l, i, chip_ref: (chip_ref[0], l, i, 0))],
            out_specs=pl.BlockSpec((None, tr, c), lambda l, i, chip_ref: (l, i, 0))),
        out_shape=jax.ShapeDtypeStruct((slots, r, c), F32), compiler_params=_cparams(),
    )(chip, recv, recv, recv, recv, own)


def _adamw_math(w, g, m, v):
    m = ADAM_B1 * m + (1.0 - ADAM_B1) * g
    v = ADAM_B2 * v + (1.0 - ADAM_B2) * (g * g)
    m_hat = m / (1.0 - ADAM_B1 ** ADAM_STEP)
    v_hat = v / (1.0 - ADAM_B2 ** ADAM_STEP)
    delta = -ADAM_LR * (m_hat / (jnp.sqrt(v_hat) + ADAM_EPS) + ADAM_WD * w)
    return delta, m, v


def adamw_shard(name, w, m, v, g_pairs, core, slots, row_halves):
    n = w.shape[0]
    assert n == len(g_pairs)
    _, r, c = g_pairs[0][0].shape
    tr = _row_tile(r, c)
    nr = r // tr

    def body(core_ref, w_ref, m_ref, v_ref, *rest):
        g_refs, (go_ref, d_ref, mo_ref, vo_ref) = rest[:2 * n], rest[2 * n:]
        mine = pl.program_id(1) == core_ref[0]
        g = jnp.where(mine, g_refs[0][...], g_refs[1][...])
        for l in range(1, n):
            g = jnp.where(pl.program_id(0) == l, jnp.where(mine, g_refs[2 * l][...], g_refs[2 * l + 1][...]), g)
        delta, mm, vv = _adamw_math(w_ref[...], g, m_ref[...], v_ref[...])
        go_ref[...] = g
        d_ref[...] = delta
        mo_ref[...] = mm
        vo_ref[...] = vv

    if row_halves:
        wspec = pl.BlockSpec((None, tr, c), lambda l, h, i, core_ref: (l, h * nr + i, 0))
    else:
        wspec = pl.BlockSpec((None, tr, c), lambda l, h, i, core_ref: (l, i, h))
    def gspec(slot):
        return pl.BlockSpec((None, tr, c), lambda l, h, i, core_ref: (slot, i, 0))

    shp = jax.ShapeDtypeStruct(w.shape, F32)
    return pl.pallas_call(
        body, name=name,
        grid_spec=pltpu.PrefetchScalarGridSpec(
            num_scalar_prefetch=1, grid=(n, 2, nr),
            in_specs=[wspec, wspec, wspec] + [gspec(s) for s in slots for _ in range(2)], out_specs=[wspec] * 4),
        out_shape=[shp] * 4, compiler_params=_cparams(),
    )(core, w, m, v, *[g for pair in g_pairs for g in pair])


SMALL_ROWS = 16


def small_allreduce_adamw(part, w, m, v):
    def body(p_ref, w_ref, m_ref, v_ref, g_ref, d_ref, mo_ref, vo_ref, buf, send_sems, recv_sems):
        x, y, c, _, _ = _place()
        me = 4 * x + 2 * y + c
        buf[me] = p_ref[...]
        copies = []
        for k in range(1, N_DEV):
            kx, ky, kc = (k >> 2) & 1, (k >> 1) & 1, k & 1
            peer = (x ^ kx, y ^ ky, c ^ kc)
            cp = pltpu.make_async_remote_copy(
                src_ref=p_ref, dst_ref=buf.at[me], send_sem=send_sems.at[k - 1],
                recv_sem=recv_sems.at[k - 1], device_id=peer, device_id_type=MESH)
            cp.start()
            copies.append(cp)
        for cp in copies:
            cp.wait()
        g = buf[0]
        for dev in range(1, N_DEV):
            g = g + buf[dev]
        delta, mm, vv = _adamw_math(w_ref[...], g, m_ref[...], v_ref[...])
        g_ref[...] = g
        d_ref[...] = delta
        mo_ref[...] = mm
        vo_ref[...] = vv

    vm = pl.BlockSpec(memory_space=pltpu.VMEM)
    shp = jax.ShapeDtypeStruct(part.shape, F32)
    return pl.pallas_call(
        body, name="small_allreduce_adamw",
        in_specs=[vm] * 4, out_specs=[vm] * 4, out_shape=[shp] * 4,
        scratch_shapes=[
            pltpu.VMEM((N_DEV,) + part.shape, F32),
            pltpu.SemaphoreType.DMA((N_DEV - 1,)), pltpu.SemaphoreType.DMA((N_DEV - 1,)),
        ],
    )(part, w, m, v)


def _rope_tables(T):
    half = HEAD_DIM // 2
    inv_freq = ROPE_THETA ** (-jnp.arange(half, dtype=F32) / half)
    ang = jnp.arange(T).astype(F32)[:, None] * inv_freq[None, :]
    cos = jnp.tile(jnp.cos(ang), (1, LANES // half))
    sin = jnp.tile(jnp.sin(ang), (1, LANES // half))
    lane = jnp.arange(LANES)
    sign = jnp.where((lane % HEAD_DIM) < half, -1.0, 1.0).astype(F32)
    return cos, sin * sign[None, :]


def _pack_small(ffn1, mix, ffn2, kvn, fin, sinks, loss_row):
    sink_row = jnp.pad(sinks.reshape(1, SWA_Q_HEADS), ((0, 0), (0, D_MODEL - SWA_Q_HEADS)))
    rows = jnp.concatenate([ffn1, mix, ffn2, kvn.reshape(1, -1), fin.reshape(1, -1), sink_row, loss_row], axis=0)
    return jnp.concatenate([rows, jnp.zeros((SMALL_ROWS - rows.shape[0], D_MODEL), F32)], axis=0)


def kernel(x, ffn1_norm, ffn1_w_in, ffn1_w_out, mix_norm, ffn2_norm, ffn2_w_in, ffn2_w_out, sb_w_qkv, sb_w_o, kv_norm, kv_w, swa_w_q, swa_sinks, swa_w_o, final_norm, loss_target, m_ffn1_norm, m_ffn1_w_in, m_ffn1_w_out, m_mix_norm, m_ffn2_norm, m_ffn2_w_in, m_ffn2_w_out, m_sb_w_qkv, m_sb_w_o, m_kv_norm, m_kv_w, m_swa_w_q, m_swa_sinks, m_swa_w_o, m_final_norm, v_ffn1_norm, v_ffn1_w_in, v_ffn1_w_out, v_mix_norm, v_ffn2_norm, v_ffn2_w_in, v_ffn2_w_out, v_sb_w_qkv, v_sb_w_o, v_kv_norm, v_kv_w, v_swa_w_q, v_swa_sinks, v_swa_w_o, v_final_norm):
    T = x.shape[1]
    kv_cols = SWA_KV_HEADS * HEAD_DIM
    x2 = x.reshape(T, D_MODEL)
    tgt = loss_target.reshape(T, D_MODEL)
    cos, sin = _rope_tables(T)

    w_in_l = jnp.concatenate([ffn1_w_in, ffn2_w_in], axis=0).astype(BF16)
    w_out_l = jnp.concatenate([ffn1_w_out, ffn2_w_out], axis=0).astype(BF16)
    sq_l = jnp.concatenate([sb_w_o, swa_w_q, swa_w_o], axis=0).astype(BF16)
    qkv_l = sb_w_qkv[0].astype(BF16)
    kvw_l = kv_w.astype(BF16)
    core = lax.axis_index("c").astype(jnp.int32).reshape(1)
    chip = (2 * lax.axis_index("x") + lax.axis_index("y")).astype(jnp.int32).reshape(1)
    early = [w_in_l[:1], w_out_l[:1]]
    mid = [sq_l, qkv_l[None]]
    late = [w_in_l[1:], w_out_l[1:], kvw_l[None]]
    early_lands = [place_own_shard(f"own_early_{t}", s, chip) for t, s in enumerate(early)]
    mid_lands = [place_own_shard(f"own_mid_{t}", s, chip) for t, s in enumerate(mid)]
    late_lands = [place_own_shard(f"own_late_{t}", s, chip) for t, s in enumerate(late)]
    w_in0, w_out0 = all_gather_weights(early, early_lands)

    def ffn_w(slot):
        return (w_in0, w_out0, 0) if slot == 0 else (w_in_r, w_out_r, slot - 1)

    def vec(a, i):
        return a[i].reshape(1, D_MODEL)

    ident = lambda w: w
    sq_prep = lambda w: w.reshape(D_MODEL, w.shape[-1])
    qscale = jnp.concatenate([jnp.full((1, D_MODEL), ATTN_SCALE, F32), jnp.ones((1, 2 * D_MODEL), F32)], axis=1)
    swa_scale = jnp.full((1, D_MODEL), ATTN_SCALE, F32)
    sinks = swa_sinks.reshape(SWA_Q_HEADS)

    h1, gate1, up1, (w_sq, w_qkv) = ffn_fwd("l0a", x2, vec(ffn1_norm, 0), *ffn_w(SLOT_FFN1[0]), mid, mid_lands)
    w_qkv = w_qkv.reshape(N_CHIPS, D_MODEL, QKV_COLS)
    qkv = qkv_fwd(h1, vec(mix_norm, 0), w_qkv, qscale)
    o_sb, tot, sb_first, late_lands = sb_fwd(qkv, late, late_lands)
    h2, (w_in_r, w_out_r, w_kv) = linear_res("sb_out", o_sb, w_sq, SQ_SB_O, h1, late_lands)
    w_kv = w_kv.reshape(D_MODEL, 2 * kv_cols)
    h3, gate2, up2 = ffn_fwd("l0b", h2, vec(ffn2_norm, 0), *ffn_w(SLOT_FFN2[0]))
    kvn = kv_norm.reshape(1, D_MODEL)
    kv_sw = rms_linear("kv_proj", h3, kvn, w_kv, pl.BlockSpec((D_MODEL, kv_cols), lambda i, j: (0, j)), ident,
                       2 * kv_cols, kv_cols, rope=(cos, sin), rope_blocks=1)
    h4, gate3, up3 = ffn_fwd("l1a", h3, vec(ffn1_norm, 1), *ffn_w(SLOT_FFN1[1]))
    q_sw = rms_linear("swa_q", h4, vec(mix_norm, 1), w_sq,
                      pl.BlockSpec((N_CHIPS, None, SQ_ROWS, 512), lambda i, j: (0, SQ_SWA_Q, 0, j)), sq_prep,
                      D_MODEL, 512, rope=(cos, sin), scale=swa_scale)
    o_sw, lse = swa_fwd(q_sw, kv_sw, sinks)
    h5 = linear_res("swa_out", o_sw, w_sq, SQ_SWA_O, h4)
    dh6, gate4, up4, loss_p, d_final = ffn_fwd("l1b", h5, vec(ffn2_norm, 1), *ffn_w(SLOT_FFN2[1]),
                                               loss=(final_norm.reshape(1, D_MODEL), tgt))

    slab = {}
    ffn_place = {SLOT_FFN1[0]: (0, 0, 1), SLOT_FFN1[1]: (1, 0, 3), SLOT_FFN2[0]: (1, 1, 3), SLOT_FFN2[1]: (1, 2, 3)}
    sq_place = {SQ_SB_O: (1, 0, 3), SQ_SWA_Q: (1, 1, 3), SQ_SWA_O: (1, 2, 3)}

    def ffn_grads(tag, dh, h_in, g, gate, up, slot, proj=None):
        dh_in, xn, dg_, du_, act, dhb, dnorm, *through_proj = ffn_bwd(tag, dh, h_in, g, gate, up, *ffn_w(slot),
                                                                      proj=proj)
        grp, s, ns = ffn_place[slot]
        in_shape = (2, N_CHIPS, ns, D_MODEL // 2, FF_CHUNK)
        out_shape = (2, N_CHIPS, ns, FF_ROWS, D_MODEL // 2)
        blk = (None, 1, None, D_MODEL // 2, FF_CHUNK)
        slab["in", grp] = mm_tn(f"dw_gate_{tag}", xn, dg_, D_MODEL // 2, FF_CHUNK, blk,
                                lambda k, n: (k, n, s, 0, 0), in_shape, prev=slab.get(("in", grp)))
        slab["in", grp] = mm_tn(f"dw_up_{tag}", xn, du_, D_MODEL // 2, FF_CHUNK, blk,
                                lambda k, n: (k, 2 + n, s, 0, 0), in_shape, prev=slab["in", grp])
        slab["out", grp] = mm_tn(f"dw_out_{tag}", act, dhb, FF_CHUNK, D_MODEL // 2,
                                 (None, 2, None, FF_ROWS, D_MODEL // 2),
                                 lambda k, n: (n, k, s, 0, 0), out_shape, prev=slab.get(("out", grp)))
        return (dh_in, dnorm, *through_proj)

    def sq_grad(tag, a, dyb, t):
        grp, s, ns = sq_place[t]
        slab["sq", grp] = mm_tn(f"dw_sq_{tag}", a, dyb, D_MODEL, D_MODEL // 2,
                                (None, N_CHIPS, None, SQ_ROWS, D_MODEL // 2),
                                lambda k, n: (n, 0, s, 0, 0), (2, N_CHIPS, ns, SQ_ROWS, D_MODEL // 2),
                                prev=slab.get(("sq", grp)))

    def reduce_group(grp, kinds, host=None, share_host=None):
        slabs = [slab[kind, grp][0] for kind in kinds]
        names = [f"{kind}{grp}" for kind in kinds]
        theirs = exchange_halves(f"exchange_halves_{grp}", [slab[kind, grp][1] for kind in kinds])
        parts = [add_sibling(f"add_sibling_{nm}", s, t, core) for nm, s, t in zip(names, slabs, theirs)]
        arrived = host(parts) if host else exchange_chip_partials(f"exchange_chip_partials_{grp}", parts)
        halves = [sum_chips(f"sum_chips_{nm}", g, p, chip) for nm, g, p in zip(names, arrived, parts)]
        if share_host:
            sib_halves = share_host(halves)
        else:
            sib_halves = share_reduced_halves(f"share_reduced_halves_{grp}", halves)
        return {kind: pair for kind, pair in zip(kinds, zip(halves, sib_halves))}

    dh5, d_ffn2_1, dh5b, do_sw = ffn_grads("l1b", dh6, h5, vec(ffn2_norm, 1), gate4, up4, SLOT_FFN2[1],
                                           proj=(w_sq, SQ_SWA_O))
    sq_grad("swa_o", o_sw, dh5b, SQ_SWA_O)
    dq_sw, kv_own, kv_prev, d_sinks = swa_bwd(q_sw, kv_sw, sinks, do_sw, o_sw, lse, cos, sin)
    sq_w_spec = pl.BlockSpec((N_CHIPS, None, SQ_ROWS, D_MODEL), lambda i, j: (0, SQ_SWA_Q, 0, 0))
    dh4, hn4, d_mix_1 = linear_bwd_rms("swa_q_bwd", [(dq_sw, w_sq, sq_w_spec, sq_prep)], h4, vec(mix_norm, 1), dh5,
                                       1, D_MODEL)
    sq_grad("swa_q", hn4, dq_sw, SQ_SWA_Q)
    dh3a, d_ffn1_1 = ffn_grads("l1a", dh4, h3, vec(ffn1_norm, 1), gate3, up3, SLOT_FFN1[1])
    dkv = kv_grad_combine(kv_own, kv_prev, cos, sin)
    kv_w_spec = pl.BlockSpec((D_MODEL, 2 * kv_cols), lambda i, j: (0, 0))
    dh3, xn3, d_kvn = linear_bwd_rms("kv_bwd", [(dkv, w_kv, kv_w_spec, ident)], h3, kvn, dh3a, 1, 2 * kv_cols)
    slab["kv", 1] = mm_tn("dw_kv", xn3, dkv, D_MODEL, kv_cols, (None, N_CHIPS, None, SQ_ROWS, kv_cols),
                          lambda k, n: (n, 0, 0, 0, 0), (2, N_CHIPS, 1, SQ_ROWS, kv_cols))
    dh2, d_ffn2_0, dh2b, do_sb = ffn_grads("l0b", dh3, h2, vec(ffn2_norm, 0), gate2, up2, SLOT_FFN2[0],
                                           proj=(w_sq, SQ_SB_O))
    sq_grad("sb_o", o_sb, dh2b, SQ_SB_O)
    sb_grads = []

    def behind_sb_bwd(parts):
        dq_sb, dk_sb, dv_sb, arrived = sb_bwd(qkv, do_sb, tot, sb_first, parts)
        sb_grads.extend([dq_sb, dk_sb, dv_sb])
        return arrived

    qkv_grads = []

    def behind_qkv_bwd(halves):
        dy = jnp.concatenate(sb_grads, axis=1)
        dh, hn, d_mix, theirs = qkv_bwd(dy, w_qkv, h1, vec(mix_norm, 0), dh2, halves)
        qkv_grads.extend([dy, dh, hn, d_mix])
        return theirs

    red = {1: reduce_group(1, ["in", "out", "sq", "kv"], host=behind_sb_bwd, share_host=behind_qkv_bwd)}
    dqkv, dh1, hn1, d_mix_0 = qkv_grads
    slab["qkv", 0] = mm_tn("dw_qkv", hn1, dqkv, D_MODEL // 2, QKV_COLS, (None, 1, None, D_MODEL // 2, QKV_COLS),
                           lambda k, n: (k, n, 0, 0, 0), (2, N_CHIPS, 1, D_MODEL // 2, QKV_COLS))
    dx, d_ffn1_0 = ffn_grads("l0a", dh1, x2, vec(ffn1_norm, 0), gate1, up1, SLOT_FFN1[0])
    red[0] = reduce_group(0, ["in", "out", "qkv"])

    def upd(name, w, m, v, kind, places, row_halves):
        shp = w.shape
        w3 = w.reshape((-1,) + shp[-2:])
        outs = adamw_shard(name, w3, m.reshape(w3.shape), v.reshape(w3.shape),
                           [red[grp][kind] for grp, _ in places], core, [s for _, s in places], row_halves)
        return [o.reshape(shp) for o in outs]

    ffn1_places = [ffn_place[s][:2] for s in SLOT_FFN1]
    ffn2_places = [ffn_place[s][:2] for s in SLOT_FFN2]
    r_ffn1_in = upd("adamw_ffn1_in", ffn1_w_in, m_ffn1_w_in, v_ffn1_w_in, "in", ffn1_places, True)
    r_ffn2_in = upd("adamw_ffn2_in", ffn2_w_in, m_ffn2_w_in, v_ffn2_w_in, "in", ffn2_places, True)
    r_ffn1_out = upd("adamw_ffn1_out", ffn1_w_out, m_ffn1_w_out, v_ffn1_w_out, "out", ffn1_places, False)
    r_ffn2_out = upd("adamw_ffn2_out", ffn2_w_out, m_ffn2_w_out, v_ffn2_w_out, "out", ffn2_places, False)
    r_qkv = upd("adamw_qkv", sb_w_qkv, m_sb_w_qkv, v_sb_w_qkv, "qkv", [(0, 0)], True)
    r_sb_o = upd("adamw_sb_o", sb_w_o, m_sb_w_o, v_sb_w_o, "sq", [sq_place[SQ_SB_O][:2]], False)
    r_swa_q = upd("adamw_swa_q", swa_w_q, m_swa_w_q, v_swa_w_q, "sq", [sq_place[SQ_SWA_Q][:2]], False)
    r_swa_o = upd("adamw_swa_o", swa_w_o, m_swa_w_o, v_swa_w_o, "sq", [sq_place[SQ_SWA_O][:2]], False)
    r_kv = upd("adamw_kv", kv_w, m_kv_w, v_kv_w, "kv", [(1, 0)], False)

    loss_row = jnp.pad(loss_p, ((0, 0), (0, D_MODEL - LANES)))
    d_sink_row = d_sinks[0, :SWA_Q_HEADS]
    part = _pack_small(jnp.concatenate([d_ffn1_0, d_ffn1_1], axis=0), jnp.concatenate([d_mix_0, d_mix_1], axis=0),
                       jnp.concatenate([d_ffn2_0, d_ffn2_1], axis=0), d_kvn, d_final, d_sink_row, loss_row)
    zrow = jnp.zeros((1, D_MODEL), F32)
    small = small_allreduce_adamw(
        part,
        _pack_small(ffn1_norm, mix_norm, ffn2_norm, kv_norm, final_norm, swa_sinks, zrow),
        _pack_small(m_ffn1_norm, m_mix_norm, m_ffn2_norm, m_kv_norm, m_final_norm, m_swa_sinks, zrow),
        _pack_small(v_ffn1_norm, v_mix_norm, v_ffn2_norm, v_kv_norm, v_final_norm, v_swa_sinks, zrow))

    def unpack(p):
        return dict(ffn1_norm=p[0:2], mix_norm=p[2:4], ffn2_norm=p[4:6], kv_norm=p[6], final_norm=p[7],
                    swa_sinks=p[8:9, :SWA_Q_HEADS])

    big = dict(ffn1_w_in=r_ffn1_in, ffn1_w_out=r_ffn1_out, ffn2_w_in=r_ffn2_in, ffn2_w_out=r_ffn2_out,
               sb_w_qkv=r_qkv, sb_w_o=r_sb_o, kv_w=r_kv, swa_w_q=r_swa_q, swa_w_o=r_swa_o)
    order = ["ffn1_norm", "ffn1_w_in", "ffn1_w_out", "mix_norm", "ffn2_norm", "ffn2_w_in", "ffn2_w_out",
             "sb_w_qkv", "sb_w_o", "kv_norm", "kv_w", "swa_w_q", "swa_sinks", "swa_w_o", "final_norm"]
    outs = []
    for kind in range(4):
        sm = unpack(small[kind])
        for nm in order:
            outs.append(big[nm][kind] if nm in big else sm[nm])
    loss = small[0][9, 0]
    return (loss, dx.reshape(x.shape), *outs)
```

```python
import jax
import jax.numpy as jnp
from jax import lax
from jax.experimental import pallas as pl
from jax.experimental.pallas import tpu as pltpu

F32 = jnp.float32
BF16 = jnp.bfloat16
MESH = pl.DeviceIdType.MESH

D_MODEL = 1024
D_FF = 2816
HEAD_DIM = 64
SB_HEADS = 16
SWA_Q_HEADS = 16
SWA_KV_HEADS = 4
WINDOW = 128
ROPE_THETA = 10000.0
RMS_EPS = 1e-6
FFN_RES_SCALE = 0.5
ATTN_SCALE = HEAD_DIM ** -0.5

ADAM_LR = 0.001
ADAM_B1 = 0.9
ADAM_B2 = 0.999
ADAM_EPS = 1e-08
ADAM_WD = 0.01
ADAM_STEP = 10

N_CHIPS = 4
N_DEV = 8
LANES = 128
FF_CHUNK = D_FF // 2
FF_ROWS = D_FF // N_CHIPS
SQ_ROWS = D_MODEL // N_CHIPS
QKV_COLS = 3 * D_MODEL // N_CHIPS
VMEM_LIMIT = 56 * 1024 * 1024
NEG_BIG = -1e30

SLOT_FFN1 = (0, 1)
SLOT_FFN2 = (2, 3)
SQ_SB_O, SQ_SWA_Q, SQ_SWA_O = 0, 1, 2


def _cparams():
    return pltpu.CompilerParams(vmem_limit_bytes=VMEM_LIMIT)


def _dot(a, b):
    return jnp.dot(a, b, preferred_element_type=F32)


def _dot_nt(a, b):
    return lax.dot_general(a, b, (((1,), (1,)), ((), ())), preferred_element_type=F32)


def _dot_tn(a, b):
    return lax.dot_general(a, b, (((0,), (0,)), ((), ())), preferred_element_type=F32)


def _rstd(h):
    return lax.rsqrt(jnp.mean(h * h, axis=-1, keepdims=True) + RMS_EPS)


def _swap32(x):
    n = x.shape[-1]
    lane = lax.broadcasted_iota(jnp.int32, x.shape, x.ndim - 1)
    first = (lane % HEAD_DIM) < (HEAD_DIM // 2)
    return jnp.where(first, pltpu.roll(x, n - HEAD_DIM // 2, x.ndim - 1), pltpu.roll(x, HEAD_DIM // 2, x.ndim - 1))


def _tile_lanes(t, n):
    return t if n == LANES else jnp.tile(t, (1, n // LANES))


FFN_ROWS = 256


def _ffn_w_in_spec(slot):
    return pl.BlockSpec((N_CHIPS, None, D_MODEL, FF_CHUNK), lambda i: (0, slot, 0, 0), pipeline_mode=pl.Buffered(1))


def _ffn_w_out_spec(slot):
    return pl.BlockSpec((N_CHIPS, None, FF_ROWS, D_MODEL), lambda i: (0, slot, 0, 0), pipeline_mode=pl.Buffered(1))


def ffn_fwd(tag, h, g, w_in, w_out, slot, bg_shards=(), bg_lands=(), loss=None):
    T = h.shape[0]
    tm = FFN_ROWS
    nch = D_FF // FF_CHUNK
    nbg = len(bg_shards)
    nt = T // tm
    nloss = 2 if loss is not None else 0

    def body(h_ref, g_ref, wi_ref, wo_ref, *rest):
        loss_in, rest = rest[:nloss], rest[nloss:]
        out_ref, gate_ref, up_ref = rest[2 * nbg:2 * nbg + 3]
        loss_out, rest = rest[3 * nbg + 3:3 * nbg + 3 + nloss], rest[:3 * nbg + 3] + rest[3 * nbg + 3 + nloss:]
        wg_s, wu_s = rest[3 * nbg + 3:3 * nbg + 5]
        step = pl.program_id(0)
        if nbg:
            gather = GatherOps([s.shape[1] for s in bg_shards], rest[:nbg], rest[2 * nbg + 3:3 * nbg + 3],
                               *rest[3 * nbg + 5:])
            pl.when(step == 0)(gather.start_ici)
            pl.when(step == nt // 2)(gather.forward_arrivals)

            @pl.when(step == nt - 1)
            def _():
                gather.wait_forwards()
                gather.wait_ici_sends()

        @pl.when(step == 0)
        def _():
            for j in range(nch):
                cols = slice(j * FF_CHUNK, (j + 1) * FF_CHUNK)
                wg_s[:, cols] = wi_ref[j]
                wu_s[:, cols] = wi_ref[nch + j]

        hh = h_ref[...]
        xn = (hh * _rstd(hh) * g_ref[...]).astype(BF16)
        gate = _dot(xn, wg_s[...])
        up = _dot(xn, wu_s[...])
        gate_ref[...] = gate.astype(BF16)
        up_ref[...] = up.astype(BF16)
        a = (gate * jax.nn.sigmoid(gate) * up).astype(BF16)
        h_out = hh + FFN_RES_SCALE * _dot(a, wo_ref[...].reshape(D_FF, D_MODEL))
        if loss is None:
            out_ref[...] = h_out
        else:
            (gf_ref, t_ref), (loss_ref, dgf_ref) = loss_in, loss_out

            @pl.when(step == 0)
            def _():
                loss_ref[...] = jnp.zeros_like(loss_ref)
                dgf_ref[...] = jnp.zeros_like(dgf_ref)

            gf = gf_ref[...]
            r = _rstd(h_out)
            hr = h_out * r
            err = hr * gf - t_ref[...]
            loss_ref[...] += 0.5 * jnp.sum(jnp.mean(err * err, axis=-1, keepdims=True), axis=0, keepdims=True)
            dy = err * (1.0 / D_MODEL)
            dgf_ref[...] += jnp.sum(dy * hr, axis=0, keepdims=True)
            gd = gf * dy
            out_ref[...] = r * (gd - hr * jnp.mean(gd * hr, axis=-1, keepdims=True))

    row = pl.BlockSpec((tm, D_MODEL), lambda i: (i, 0))
    ff = pl.BlockSpec((tm, D_FF), lambda i: (i, 0))
    vec = pl.BlockSpec((1, D_MODEL), lambda i: (0, 0))
    res = pl.pallas_call(
        body,
        name=f"ffn_fwd_{tag}",
        grid=(nt,),
        in_specs=[row, vec, _ffn_w_in_spec(slot), _ffn_w_out_spec(slot)] + [vec, row][:nloss] + [ANY] * (2 * nbg),
        out_specs=[row, ff, ff] + [ANY] * nbg + [pl.BlockSpec((1, LANES), lambda i: (0, 0)), vec][:nloss],
        out_shape=[
            jax.ShapeDtypeStruct((T, D_MODEL), F32),
            jax.ShapeDtypeStruct((T, D_FF), BF16),
            jax.ShapeDtypeStruct((T, D_FF), BF16),
        ] + [jax.ShapeDtypeStruct(l.shape, l.dtype) for l in bg_lands]
        + [jax.ShapeDtypeStruct((1, LANES), F32), jax.ShapeDtypeStruct((1, D_MODEL), F32)][:nloss],
        input_output_aliases={4 + nloss + nbg + t: 3 + t for t in range(nbg)},
        scratch_shapes=[pltpu.VMEM((D_MODEL, D_FF), BF16), pltpu.VMEM((D_MODEL, D_FF), BF16)]
        + [pltpu.SemaphoreType.DMA((nbg, N_PEER_CHIPS))] * (4 if nbg else 0),
        compiler_params=_cparams(),
    )(h, g, w_in, w_out, *(loss or ()), *bg_shards, *bg_lands)
    if nbg:
        return res[0], res[1], res[2], list(res[3:3 + nbg])
    return tuple(res)


def ffn_bwd(tag, dh, h, g, gate, up, w_in, w_out, slot, proj=None):
    T = dh.shape[0]
    tm = FFN_ROWS
    nch = D_FF // FF_CHUNK

    def body(dh_ref, h_ref, g_ref, gate_ref, up_ref, wi_ref, wo_ref, *rest):
        if proj is not None:
            wp_ref, rest = rest[0], rest[1:]
        dhin_ref, xn_ref, dg_ref, du_ref, a_ref, dhb_ref, dnorm_ref = rest[:7]
        @pl.when(pl.program_id(0) == 0)
        def _():
            dnorm_ref[...] = jnp.zeros_like(dnorm_ref)

        dhh = dh_ref[...]
        dhb = (FFN_RES_SCALE * dhh).astype(BF16)
        dhb_ref[...] = dhb
        dxn = None
        for j in range(nch):
            cols = slice(j * FF_CHUNK, (j + 1) * FF_CHUNK)
            da = _dot_nt(dhb, wo_ref[2 * j:2 * j + 2].reshape(FF_CHUNK, D_MODEL))
            gt = gate_ref[:, cols].astype(F32)
            u = up_ref[:, cols].astype(F32)
            s = jax.nn.sigmoid(gt)
            silu = gt * s
            a_ref[:, cols] = (silu * u).astype(BF16)
            dgate = (da * u * (s * (1.0 + gt * (1.0 - s)))).astype(BF16)
            dup = (da * silu).astype(BF16)
            dg_ref[:, cols] = dgate
            du_ref[:, cols] = dup
            part = _dot_nt(dgate, wi_ref[j]) + _dot_nt(dup, wi_ref[nch + j])
            dxn = part if dxn is None else dxn + part
        hh = h_ref[...]
        gg = g_ref[...]
        r = _rstd(hh)
        hr = hh * r
        xn_ref[...] = (hr * gg).astype(BF16)
        dnorm_ref[...] += jnp.sum(dxn * hr, axis=0, keepdims=True)
        gd = gg * dxn
        dh_in = dhh + r * (gd - hr * jnp.mean(gd * hr, axis=-1, keepdims=True))
        dhin_ref[...] = dh_in
        if proj is not None:
            dyb_ref, da_ref = rest[7:9]
            dyb = dh_in.astype(BF16)
            dyb_ref[...] = dyb
            da_ref[...] = _dot_nt(dyb, wp_ref[...].reshape(D_MODEL, D_MODEL)).astype(BF16)

    row = pl.BlockSpec((tm, D_MODEL), lambda i: (i, 0))
    ff = pl.BlockSpec((tm, D_FF), lambda i: (i, 0))
    vec = pl.BlockSpec((1, D_MODEL), lambda i: (0, 0))
    in_specs = [row, row, vec, ff, ff, _ffn_w_in_spec(slot), _ffn_w_out_spec(slot)]
    args = [dh, h, g, gate, up, w_in, w_out]
    out_specs = [row, row, ff, ff, ff, row, vec]
    out_shape = [
        jax.ShapeDtypeStruct((T, D_MODEL), F32),
        jax.ShapeDtypeStruct((T, D_MODEL), BF16),
        jax.ShapeDtypeStruct((T, D_FF), BF16),
        jax.ShapeDtypeStruct((T, D_FF), BF16),
        jax.ShapeDtypeStruct((T, D_FF), BF16),
        jax.ShapeDtypeStruct((T, D_MODEL), BF16),
        jax.ShapeDtypeStruct((1, D_MODEL), F32),
    ]
    if proj is not None:
        w_sq, t = proj
        in_specs.append(pl.BlockSpec((N_CHIPS, None, SQ_ROWS, D_MODEL), lambda i: (0, t, 0, 0),
                                     pipeline_mode=pl.Buffered(1)))
        args.append(w_sq)
        out_specs += [row, row]
        out_shape += [jax.ShapeDtypeStruct((T, D_MODEL), BF16)] * 2
    return pl.pallas_call(
        body,
        name=f"ffn_bwd_{tag}",
        grid=(T // tm,),
        in_specs=in_specs,
        out_specs=out_specs,
        out_shape=out_shape,
        compiler_params=_cparams(),
    )(*args)


def rms_linear(name, h, g, w, w_spec, w_prep, n_out, tn, *, rope=None, rope_blocks=None, scale=None):
    T = h.shape[0]
    tm = 512
    extra, extra_specs = [], []
    if rope is not None:
        extra += list(rope)
        extra_specs += [pl.BlockSpec((tm, LANES), lambda i, j: (i, 0))] * 2
    if scale is not None:
        extra.append(scale)
        extra_specs.append(pl.BlockSpec((1, tn), lambda i, j: (0, j)))

    def body(h_ref, g_ref, w_ref, *rest):
        rest = list(rest)
        cos_ref = sin_ref = sc_ref = None
        if rope is not None:
            cos_ref, sin_ref = rest[0], rest[1]
            rest = rest[2:]
        if scale is not None:
            sc_ref = rest[0]
            rest = rest[1:]
        out_ref, xn_s = rest

        @pl.when(pl.program_id(1) == 0)
        def _():
            hh = h_ref[...]
            xn_s[...] = (hh * _rstd(hh) * g_ref[...]).astype(BF16)

        y = _dot(xn_s[...], w_prep(w_ref[...]))
        if rope is not None:
            turned = y * _tile_lanes(cos_ref[...], tn) + _swap32(y) * _tile_lanes(sin_ref[...], tn)
            y = turned if rope_blocks is None else jnp.where(pl.program_id(1) < rope_blocks, turned, y)
        if scale is not None:
            y = y * sc_ref[...]
        out_ref[...] = y.astype(BF16)

    return pl.pallas_call(
        body,
        name=name,
        grid=(T // tm, n_out // tn),
        in_specs=[
            pl.BlockSpec((tm, D_MODEL), lambda i, j: (i, 0)),
            pl.BlockSpec((1, D_MODEL), lambda i, j: (0, 0)),
            w_spec,
        ] + extra_specs,
        out_specs=pl.BlockSpec((tm, tn), lambda i, j: (i, j)),
        out_shape=jax.ShapeDtypeStruct((T, n_out), BF16),
        scratch_shapes=[pltpu.VMEM((tm, D_MODEL), BF16)],
        compiler_params=_cparams(),
    )(h, g, w, *extra)


QKV_ROWS = 512


def _qkv_w_spec():
    return pl.BlockSpec((N_CHIPS, D_MODEL, QKV_COLS), lambda i: (0, 0, 0), pipeline_mode=pl.Buffered(1))


def qkv_fwd(h, g, w_qkv, scale):
    T = h.shape[0]
    tm = QKV_ROWS

    def body(h_ref, g_ref, w_ref, sc_ref, out_ref):
        hh = h_ref[...]
        xn = (hh * _rstd(hh) * g_ref[...]).astype(BF16)
        for s in range(N_CHIPS):
            cols = slice(s * QKV_COLS, (s + 1) * QKV_COLS)
            out_ref[:, cols] = (_dot(xn, w_ref[s]) * sc_ref[:, cols]).astype(BF16)

    return pl.pallas_call(
        body,
        name="sb_qkv",
        grid=(T // tm,),
        in_specs=[
            pl.BlockSpec((tm, D_MODEL), lambda i: (i, 0)),
            pl.BlockSpec((1, D_MODEL), lambda i: (0, 0)),
            _qkv_w_spec(),
            pl.BlockSpec((1, 3 * D_MODEL), lambda i: (0, 0)),
        ],
        out_specs=pl.BlockSpec((tm, 3 * D_MODEL), lambda i: (i, 0)),
        out_shape=jax.ShapeDtypeStruct((T, 3 * D_MODEL), BF16),
        compiler_params=_cparams(),
    )(h, g, w_qkv, scale)


def qkv_bwd(dy, w_qkv, h, g, dres, bg_halves=()):
    T = h.shape[0]
    tm = QKV_ROWS
    nbg = len(bg_halves)
    nt = T // tm

    def body(dy_ref, w_ref, h_ref, g_ref, dres_ref, *rest):
        dh_ref, xn_ref, dg_ref = rest[nbg:nbg + 3]
        if nbg:
            def share():
                return sibling_share_copies(rest[:nbg], rest[nbg + 3:2 * nbg + 3], *rest[2 * nbg + 3:])

            @pl.when(pl.program_id(0) == 0)
            def _():
                for cp in share():
                    cp.start()

            @pl.when(pl.program_id(0) == nt - 1)
            def _():
                for cp in share():
                    cp.wait()

        @pl.when(pl.program_id(0) == 0)
        def _():
            dg_ref[...] = jnp.zeros_like(dg_ref)

        dxn = None
        for s in range(N_CHIPS):
            part = _dot_nt(dy_ref[:, s * QKV_COLS:(s + 1) * QKV_COLS], w_ref[s])
            dxn = part if dxn is None else dxn + part
        hh = h_ref[...]
        gg = g_ref[...]
        r = _rstd(hh)
        hr = hh * r
        xn_ref[...] = (hr * gg).astype(BF16)
        dg_ref[...] += jnp.sum(dxn * hr, axis=0, keepdims=True)
        gd = gg * dxn
        dh_ref[...] = dres_ref[...] + r * (gd - hr * jnp.mean(gd * hr, axis=-1, keepdims=True))

    row = pl.BlockSpec((tm, D_MODEL), lambda i: (i, 0))
    vec = pl.BlockSpec((1, D_MODEL), lambda i: (0, 0))
    res = pl.pallas_call(
        body,
        name="sb_qkv_bwd",
        grid=(nt,),
        in_specs=[pl.BlockSpec((tm, 3 * D_MODEL), lambda i: (i, 0)), _qkv_w_spec(), row, vec, row] + [ANY] * nbg,
        out_specs=[row, row, vec] + [ANY] * nbg,
        out_shape=[
            jax.ShapeDtypeStruct((T, D_MODEL), F32),
            jax.ShapeDtypeStruct((T, D_MODEL), BF16),
            jax.ShapeDtypeStruct((1, D_MODEL), F32),
        ] + [jax.ShapeDtypeStruct(b.shape, b.dtype) for b in bg_halves],
        scratch_shapes=[pltpu.SemaphoreType.DMA((nbg,))] * (2 if nbg else 0),
        compiler_params=_cparams(),
    )(dy, w_qkv, h, g, dres, *bg_halves)
    return res[0], res[1], res[2], list(res[3:])


def linear_res(name, a, w_sq, t, res, bg_lands=()):
    T = a.shape[0]
    tm = 512
    nbg = len(bg_lands)
    nt = T // tm

    def body(a_ref, w_ref, res_ref, *rest):
        out_ref = rest[nbg]
        if nbg:
            gather = GatherOps([l.shape[2] for l in bg_lands], None, rest[nbg + 1:2 * nbg + 1], None, None,
                               *rest[2 * nbg + 1:])

            @pl.when(pl.program_id(0) == 0)
            def _():
                gather.start_forwards()

        out_ref[...] = res_ref[...] + _dot(a_ref[...], w_ref[...].reshape(D_MODEL, D_MODEL))
        if nbg:
            @pl.when(pl.program_id(0) == nt - 1)
            def _():
                gather.wait_forwards()

    row = pl.BlockSpec((tm, D_MODEL), lambda i: (i, 0))
    res_ = pl.pallas_call(
        body,
        name=name,
        grid=(nt,),
        in_specs=[row, pl.BlockSpec((N_CHIPS, None, SQ_ROWS, D_MODEL), lambda i: (0, t, 0, 0)), row] + [ANY] * nbg,
        out_specs=[row] + [ANY] * nbg,
        out_shape=[jax.ShapeDtypeStruct((T, D_MODEL), F32)] + [jax.ShapeDtypeStruct(l.shape, l.dtype) for l in bg_lands],
        input_output_aliases={3 + k: 1 + k for k in range(nbg)},
        scratch_shapes=[pltpu.SemaphoreType.DMA((nbg, N_PEER_CHIPS))] * (2 if nbg else 0),
        compiler_params=_cparams(),
    )(a, w_sq, res, *bg_lands)
    return (res_[0], list(res_[1:])) if nbg else res_[0]


def linear_bwd_rms(name, pairs, h, g, dres, nch, tn, tm=256):
    T = h.shape[0]
    npair = len(pairs)

    def body(*refs):
        dy_refs = refs[:npair]
        w_refs = refs[npair:2 * npair]
        h_ref, g_ref, dres_ref, dh_ref, xn_ref, dg_ref, acc_s = refs[2 * npair:]
        i = pl.program_id(0)
        j = pl.program_id(1)

        @pl.when(j == 0)
        def _():
            acc_s[...] = jnp.zeros_like(acc_s)

        @pl.when((i == 0) & (j == 0))
        def _():
            dg_ref[...] = jnp.zeros_like(dg_ref)

        part = None
        for p in range(npair):
            d = _dot_nt(dy_refs[p][...], pairs[p][3](w_refs[p][...]))
            part = d if part is None else part + d
        acc_s[...] += part

        @pl.when(j == nch - 1)
        def _():
            dxn = acc_s[...]
            hh = h_ref[...]
            gg = g_ref[...]
            r = _rstd(hh)
            hr = hh * r
            xn_ref[...] = (hr * gg).astype(BF16)
            dg_ref[...] += jnp.sum(dxn * hr, axis=0, keepdims=True)
            gd = gg * dxn
            dh_ref[...] = dres_ref[...] + r * (gd - hr * jnp.mean(gd * hr, axis=-1, keepdims=True))

    row = pl.BlockSpec((tm, D_MODEL), lambda i, j: (i, 0))
    vec = pl.BlockSpec((1, D_MODEL), lambda i, j: (0, 0))
    return pl.pallas_call(
        body,
        name=name,
        grid=(T // tm, nch),
        in_specs=[pl.BlockSpec((tm, tn), lambda i, j: (i, j))] * npair + [p[2] for p in pairs] + [row, vec, row],
        out_specs=[row, row, vec],
        out_shape=[
            jax.ShapeDtypeStruct((T, D_MODEL), F32),
            jax.ShapeDtypeStruct((T, D_MODEL), BF16),
            jax.ShapeDtypeStruct((1, D_MODEL), F32),
        ],
        scratch_shapes=[pltpu.VMEM((tm, D_MODEL), F32)],
        compiler_params=_cparams(),
    )(*[p[0] for p in pairs], *[p[1] for p in pairs], h, g, dres)


DW_TOKENS = 4096


def mm_tn(name, a, b, tk, tn, out_block, out_index, out_shape, prev=None, tt=DW_TOKENS, bg_parts=()):
    T = a.shape[0]
    ns, r = out_block[1], out_block[3]
    tt = min(tt, T)
    nt = T // tt
    nbg = len(bg_parts)
    n_in = 2 + (2 if prev is not None else 0)
    grid = (a.shape[1] // tk, b.shape[1] // tn, nt)

    def body(*refs):
        a_ref, b_ref = refs[:2]
        out_ref, copy_ref = refs[n_in + nbg:n_in + nbg + 2]
        t = pl.program_id(2)
        if nbg:
            def exchange():
                return chip_partial_copies(refs[n_in:n_in + nbg], refs[n_in + nbg + 2:n_in + 2 * nbg + 2],
                                           *refs[n_in + 2 * nbg + 2:])

            step = (pl.program_id(0) * grid[1] + pl.program_id(1)) * nt + t

            @pl.when(step == 0)
            def _():
                for cp in exchange():
                    cp.start()

            @pl.when(step == grid[0] * grid[1] * nt - 1)
            def _():
                for cp in exchange():
                    cp.wait()

        res = _dot_tn(a_ref[...], b_ref[...])

        @pl.when(t == 0)
        def _():
            for u in range(ns):
                out_ref[u] = res[u * r:(u + 1) * r]

        @pl.when(t > 0)
        def _():
            for u in range(ns):
                out_ref[u] += res[u * r:(u + 1) * r]

        @pl.when(t == nt - 1)
        def _():
            copy_ref[...] = out_ref[...].astype(BF16)

    in_specs = [
        pl.BlockSpec((tt, tk), lambda k, n, t: (t, k)),
        pl.BlockSpec((tt, tn), lambda k, n, t: (t, n)),
    ]
    args = [a, b]
    aliases = {}
    if prev is not None:
        in_specs += [pl.BlockSpec(memory_space=pl.ANY)] * 2
        args += list(prev)
        aliases = {2: 0, 3: 1}
    out_spec = pl.BlockSpec(out_block, lambda k, n, t: out_index(k, n))
    return tuple(pl.pallas_call(
        body,
        name=name,
        grid=grid,
        in_specs=in_specs + [ANY] * nbg,
        out_specs=[out_spec, out_spec] + [ANY] * nbg,
        out_shape=[jax.ShapeDtypeStruct(out_shape, F32), jax.ShapeDtypeStruct(out_shape, BF16)]
        + [jax.ShapeDtypeStruct(p.shape, p.dtype) for p in bg_parts],
        input_output_aliases=aliases,
        scratch_shapes=[pltpu.SemaphoreType.DMA((nbg, N_PEER_CHIPS))] * (2 if nbg else 0),
        compiler_params=_cparams(),
    )(*args, *bg_parts))


SB_BLOCK = 256
SB_QROWS = 256
SB_QROWS_BWD = 256
SB_UNDERFLOW_BITS = 140.0
SB_CHUNK = 128


LOG2E = 1.4426950408889634


def _softplus2(z2):
    sign = jnp.uint32(0x80000000)
    neg_abs = lax.bitcast_convert_type(lax.bitcast_convert_type(z2, jnp.uint32) | sign, F32)
    return jnp.log2(1.0 + jnp.exp2(neg_abs)) + jnp.maximum(z2, 0.0)


def _twice(x):
    return jnp.concatenate([x, x], axis=1)


def sb_fwd(qkv, bg_shards=(), bg_lands=()):
    T = qkv.shape[0]
    tq, tk = SB_QROWS, SB_BLOCK
    ratio = tq // tk
    npair = SB_HEADS // 2
    nbg = len(bg_shards)
    nq = T // tq

    def body(q_ref, k_ref, v_ref, *rest):
        bg_in = rest[:nbg]
        o_ref, tot_ref, first_ref = rest[2 * nbg:2 * nbg + 3]
        bg_out = rest[2 * nbg + 3:3 * nbg + 3]
        acc_s, c_s, z_s, w_s, kmax_s = rest[3 * nbg + 3:3 * nbg + 8]
        p = pl.program_id(0)
        i = pl.program_id(1)
        if nbg:
            gather = GatherOps([s.shape[1] for s in bg_shards], bg_in, bg_out, *rest[3 * nbg + 8:])

            @pl.when((p == 0) & (i == 0))
            def _():
                gather.start_ici()

        @pl.when(i == 0)
        def _():
            kmax_s[...] = jnp.max(jnp.abs(k_ref[...]), axis=0, keepdims=True).astype(F32)

        q = q_ref[...]
        lane = lax.broadcasted_iota(jnp.int32, (tq, LANES), 1)
        first = lane < HEAD_DIM
        zero = jnp.zeros_like(q)
        q_heads = (jnp.where(first, q, zero), jnp.where(first, zero, q))
        row = lax.broadcasted_iota(jnp.int32, (tq, tk), 0)
        col = lax.broadcasted_iota(jnp.int32, (tq, tk), 1)
        visible = [col + r * tk < row for r in range(ratio)]
        krow = lax.broadcasted_iota(jnp.int32, (tk, tk), 0)
        kcol = lax.broadcasted_iota(jnp.int32, (tk, tk), 1)
        from_s = (krow >= kcol).astype(BF16)
        acc_s[...] = jnp.zeros_like(acc_s)
        c_s[...] = jnp.zeros_like(c_s)

        def rows(j):
            return pl.ds(pl.multiple_of(j * tk, tk), tk)

        def logits(j):
            kb = k_ref[rows(j), :]
            for hd in range(2):
                z_s[hd] = _dot_nt(q_heads[hd], kb) * LOG2E

        def flush(j):
            vb = v_ref[rows(j), :]
            for hd in range(2):
                acc_s[hd] += _dot(w_s[hd], vb)

        def block(j, mask=None, flush_block=None):
            if flush_block is not None:
                flush(flush_block)
            chunks = [(hd, slice(r0, r0 + SB_CHUNK)) for hd in range(2) for r0 in range(0, tq, SB_CHUNK)]
            k_next = k_ref[rows(jnp.maximum(j - 1, 0)), :]
            es, sums = [], []
            for hd, rs in chunks:
                z2 = z_s[hd, rs, :]
                z_s[hd, rs, :] = _dot_nt(q_heads[hd][rs, :], k_next) * LOG2E
                if mask is not None:
                    z2 = jnp.where(mask if mask.ndim == 0 else mask[rs, :], z2, NEG_BIG)
                sp = _softplus2(z2)
                c = c_s[hd, rs, :]
                es.append(z2 + _twice(c))
                c_s[hd, rs, :] = c - jnp.sum(sp, axis=1, keepdims=True)
                sums.append(_dot(sp.astype(BF16), from_s))
            for (hd, rs), e, s in zip(chunks, es, sums):
                w_s[hd, rs, :] = jnp.exp2(e - s).astype(BF16)

        z_bound = [LOG2E * jnp.sum(jnp.abs(q_heads[hd].astype(F32)) * kmax_s[...], axis=1, keepdims=True)
                   for hd in range(2)]

        def more_keys_matter():
            top = jnp.maximum(c_s[0] + z_bound[0], c_s[1] + z_bound[1])
            return (jnp.max(top) >= -SB_UNDERFLOW_BITS).astype(jnp.int32)

        assert ratio == 1
        logits(i)
        block(i, visible[0])

        def trip(carry):
            trips, _ = carry
            j = i - 1 - trips
            block(j, flush_block=j + 1)
            return trips + 1, more_keys_matter()

        trips, _ = lax.while_loop(lambda carry: jnp.logical_and(carry[0] < i, carry[1] > 0), trip,
                                  (jnp.int32(0), jnp.int32(1)))
        first_walked = i - trips
        flush(first_walked)
        first_ref[p, i] = first_walked.astype(F32)
        o_ref[...] = jnp.where(first, acc_s[0], acc_s[1]).astype(BF16)
        tot_ref[...] = jnp.where(first, c_s[0], c_s[1])
        if nbg:
            @pl.when((p == npair - 1) & (i == nq - 1))
            def _():
                gather.wait_ici()

    res = pl.pallas_call(
        body,
        name="sb_fwd",
        grid=(npair, nq),
        in_specs=[
            pl.BlockSpec((tq, LANES), lambda p, i: (i, p)),
            pl.BlockSpec((T, LANES), lambda p, i: (0, npair + p)),
            pl.BlockSpec((T, LANES), lambda p, i: (0, 2 * npair + p)),
        ] + [ANY] * (2 * nbg),
        out_specs=[pl.BlockSpec((tq, LANES), lambda p, i: (i, p))] * 2 + [pl.BlockSpec(memory_space=pltpu.SMEM)]
        + [ANY] * nbg,
        out_shape=[jax.ShapeDtypeStruct((T, D_MODEL), BF16), jax.ShapeDtypeStruct((T, D_MODEL), F32),
                   jax.ShapeDtypeStruct((npair, nq), F32)]
        + [jax.ShapeDtypeStruct(l.shape, l.dtype) for l in bg_lands],
        input_output_aliases={3 + nbg + t: 3 + t for t in range(nbg)},
        scratch_shapes=[
            pltpu.VMEM((2, tq, LANES), F32), pltpu.VMEM((2, tq, LANES), F32),
            pltpu.VMEM((2, tq, tk), F32), pltpu.VMEM((2, tq, tk), BF16),
            pltpu.VMEM((1, LANES), F32),
        ] + [pltpu.SemaphoreType.DMA((nbg, N_PEER_CHIPS))] * (2 if nbg else 0),
        compiler_params=_cparams(),
    )(qkv, qkv, qkv, *bg_shards, *bg_lands)
    return res[0], res[1], res[2], list(res[3:])


def sb_bwd(qkv, do, tot, first_block, bg_parts=()):
    T = qkv.shape[0]
    tq, tk = SB_QROWS_BWD, SB_BLOCK
    ratio = tq // tk
    npair = SB_HEADS // 2
    nq = T // tq
    nk = T // tk
    nbg = len(bg_parts)
    assert SB_QROWS == SB_QROWS_BWD

    def body(first_ref, q_ref, k_ref, v_ref, do_ref, tot_ref, *rest):
        bg_in = rest[:nbg]
        dq_ref, dk_ref, dv_ref = rest[nbg:nbg + 3]
        bg_out = rest[nbg + 3:2 * nbg + 3]
        dkt_s, dvt_s, dq_s, rest_s, cg_s, z_s, da_s, dz_s, a_s = rest[2 * nbg + 3:2 * nbg + 12]
        i = pl.program_id(1)
        start = jnp.clip(first_ref[pl.program_id(0), i].astype(jnp.int32), 0, i)
        if nbg:
            @pl.when((pl.program_id(0) == 0) & (i == 0))
            def _():
                for cp in chip_partial_copies(bg_in, bg_out, *rest[2 * nbg + 12:]):
                    cp.start()

        @pl.when(i == 0)
        def _():
            dkt_s[...] = jnp.zeros_like(dkt_s)
            dvt_s[...] = jnp.zeros_like(dvt_s)

        q = q_ref[...]
        do_ = do_ref[...]
        tot_ = tot_ref[...]
        q_t = q.astype(F32).T.astype(BF16)
        do_t = do_.astype(F32).T.astype(BF16)
        lane = lax.broadcasted_iota(jnp.int32, (tq, LANES), 1)
        first = lane < HEAD_DIM
        zero = jnp.zeros_like(q)
        q_heads = (jnp.where(first, q, zero), jnp.where(first, zero, q))
        do_heads = (jnp.where(first, do_, zero), jnp.where(first, zero, do_))
        row = lax.broadcasted_iota(jnp.int32, (tq, tk), 0)
        col = lax.broadcasted_iota(jnp.int32, (tq, tk), 1)
        visible = [col + r * tk < row for r in range(ratio)]
        krow = lax.broadcasted_iota(jnp.int32, (tk, tk), 0)
        kcol = lax.broadcasted_iota(jnp.int32, (tk, tk), 1)
        before = (krow < kcol).astype(BF16)
        from_s = (krow >= kcol).astype(BF16)
        last = ratio * i + ratio - 1
        rest_s[0] = jnp.broadcast_to(tot_[:, 0:1], (tq, LANES))
        rest_s[1] = jnp.broadcast_to(tot_[:, HEAD_DIM:HEAD_DIM + 1], (tq, LANES))
        cg_s[...] = jnp.zeros_like(cg_s)
        dq_s[...] = jnp.zeros_like(dq_s)
        dz_s[...] = jnp.zeros_like(dz_s)
        a_s[...] = jnp.zeros_like(a_s)

        def rows(j):
            return pl.ds(pl.multiple_of(j * tk, tk), tk)

        def logits(j):
            kb = k_ref[rows(j), :]
            vb = v_ref[rows(j), :]
            for hd in range(2):
                z_s[hd] = _dot_nt(q_heads[hd], kb) * LOG2E
                da_s[hd] = _dot_nt(do_heads[hd], vb)

        def flush(j):
            kb = k_ref[rows(j), :]
            for hd in range(2):
                dims = slice(hd * HEAD_DIM, (hd + 1) * HEAD_DIM)
                dq_s[hd] += _dot(dz_s[hd], kb)
                dkt_s[j, dims, :] += _dot(q_t[dims, :], dz_s[hd])
                dvt_s[j, dims, :] += _dot(do_t[dims, :], a_s[hd])

        def block(j, mask=None):
            flush(jnp.maximum(j - 1, 0))
            chunks = [(hd, slice(r0, r0 + SB_CHUNK)) for hd in range(2) for r0 in range(0, tq, SB_CHUNK)]
            nxt = rows(jnp.minimum(j + 1, last))
            k_next = k_ref[nxt, :]
            v_next = v_ref[nxt, :]
            stage1 = []
            for hd, rs in chunks:
                z2 = z_s[hd, rs, :]
                z_s[hd, rs, :] = _dot_nt(q_heads[hd][rs, :], k_next) * LOG2E
                if mask is not None:
                    z2 = jnp.where(mask if mask.ndim == 0 else mask[rs, :], z2, NEG_BIG)
                sp = _softplus2(z2)
                rest = rest_s[hd, rs, :] + jnp.sum(sp, axis=1, keepdims=True)
                rest_s[hd, rs, :] = rest
                stage1.append((z2 + _twice(rest), z2 - sp, _dot(sp.astype(BF16), from_s)))
            stage2 = []
            for (hd, rs), (e, log2_beta, ahead) in zip(chunks, stage1):
                a = jnp.exp2(e - ahead)
                g = a * da_s[hd, rs, :]
                da_s[hd, rs, :] = _dot_nt(do_heads[hd][rs, :], v_next)
                cg = cg_s[hd, rs, :]
                a_s[hd, rs, :] = a.astype(BF16)
                cg_s[hd, rs, :] = cg + jnp.sum(g, axis=1, keepdims=True)
                stage2.append((g, g + _twice(cg), log2_beta, _dot(g.astype(BF16), before)))
            for (hd, rs), (g, g_from, log2_beta, g_before) in zip(chunks, stage2):
                dz_s[hd, rs, :] = (g - jnp.exp2(log2_beta) * (g_from + g_before)).astype(BF16)

        assert ratio == 1
        logits(start)

        @pl.loop(start, i)
        def _(j):
            block(j)

        block(i, visible[0])
        flush(last)
        dq_ref[...] = (jnp.where(first, dq_s[0], dq_s[1]) * ATTN_SCALE).astype(BF16)

        @pl.when(i == nq - 1)
        def _():
            @pl.loop(0, nk)
            def _(b):
                dk_ref[rows(b), :] = dkt_s[b].T.astype(BF16)
                dv_ref[rows(b), :] = dvt_s[b].T.astype(BF16)

        if nbg:
            @pl.when((pl.program_id(0) == npair - 1) & (i == nq - 1))
            def _():
                for cp in chip_partial_copies(bg_in, bg_out, *rest[2 * nbg + 12:]):
                    cp.wait()

    qblk = pl.BlockSpec((tq, LANES), lambda p, i: (i, p))
    full = pl.BlockSpec((T, LANES), lambda p, i: (0, p))
    res = pl.pallas_call(
        body,
        name="sb_bwd",
        grid=(npair, nq),
        in_specs=[
            pl.BlockSpec(memory_space=pltpu.SMEM),
            qblk,
            pl.BlockSpec((T, LANES), lambda p, i: (0, npair + p)),
            pl.BlockSpec((T, LANES), lambda p, i: (0, 2 * npair + p)),
            qblk, qblk,
        ] + [ANY] * nbg,
        out_specs=[qblk, full, full] + [ANY] * nbg,
        out_shape=[jax.ShapeDtypeStruct((T, D_MODEL), BF16)] * 3
        + [jax.ShapeDtypeStruct(b.shape, b.dtype) for b in bg_parts],
        scratch_shapes=[
            pltpu.VMEM((nk, LANES, tk), F32), pltpu.VMEM((nk, LANES, tk), F32),
            pltpu.VMEM((2, tq, LANES), F32), pltpu.VMEM((2, tq, LANES), F32), pltpu.VMEM((2, tq, LANES), F32),
            pltpu.VMEM((2, tq, tk), F32), pltpu.VMEM((2, tq, tk), F32),
            pltpu.VMEM((2, tq, tk), BF16), pltpu.VMEM((2, tq, tk), BF16),
        ] + [pltpu.SemaphoreType.DMA((nbg, N_PEER_CHIPS))] * (2 if nbg else 0),
        compiler_params=_cparams(),
    )(first_block, qkv, qkv, qkv, do, tot, *bg_parts)
    return res[0], res[1], res[2], list(res[3:])


def _swa_valid(n):
    qi = lax.broadcasted_iota(jnp.int32, (WINDOW, 2 * WINDOW), 0)
    ki = lax.broadcasted_iota(jnp.int32, (WINDOW, 2 * WINDOW), 1)
    diff = qi + WINDOW - ki
    return (diff >= 0) & (diff < WINDOW) & ((n > 0) | (ki >= WINDOW))


def _to_half(x, first, src, dst):
    keep = first if src == 0 else jnp.logical_not(first)
    x = jnp.where(keep, x, jnp.zeros_like(x))
    if src != dst:
        x = pltpu.roll(x.astype(F32), HEAD_DIM, 1).astype(BF16)
    return x


SWA_GROUP = SWA_Q_HEADS // SWA_KV_HEADS


def _swa_cols(h):
    return slice((h // 2) * LANES, (h // 2 + 1) * LANES)


def _swa_kv_pair(h):
    return (h // SWA_GROUP) // 2


def _swa_kv_half(h):
    return (h // SWA_GROUP) % 2


def _kv_band(prev_ref, cur_ref, pb):
    cols = slice(pb * LANES, (pb + 1) * LANES)
    return jnp.concatenate([prev_ref[:, cols], cur_ref[:, cols]], axis=0)


def _swa_specs(T):
    nb = T // WINDOW
    kv_w = SWA_KV_HEADS * HEAD_DIM
    qrow = pl.BlockSpec((WINDOW, D_MODEL), lambda n: (n, 0))
    kv = [pl.BlockSpec((WINDOW, kv_w), lambda n, col=col, back=back: (jnp.maximum(n - back, 0), col))
          for col in (0, 1) for back in (0, 1)]
    smem = pl.BlockSpec(memory_space=pltpu.SMEM)
    return nb, qrow, kv, smem


def swa_fwd(q, kv, sinks):
    T = q.shape[0]
    nb, qrow, kv_specs, smem = _swa_specs(T)

    def body(sink_ref, q_ref, kc_ref, kp_ref, vc_ref, vp_ref, o_ref, lse_ref):
        n = pl.program_id(0)
        lane = lax.broadcasted_iota(jnp.int32, (WINDOW, LANES), 1)
        first = lane < HEAD_DIM
        valid = _swa_valid(n)
        k2 = [_kv_band(kp_ref, kc_ref, pb) for pb in range(SWA_KV_HEADS // 2)]
        v2 = [_kv_band(vp_ref, vc_ref, pb) for pb in range(SWA_KV_HEADS // 2)]
        logits = [jnp.where(valid, _dot_nt(_to_half(q_ref[:, _swa_cols(h)], first, h % 2, _swa_kv_half(h)),
                                            k2[_swa_kv_pair(h)]), NEG_BIG) for h in range(SWA_Q_HEADS)]
        probs = []
        lse_acc = jnp.zeros((WINDOW, LANES), F32)
        for h, s in enumerate(logits):
            sink = sink_ref[h]
            m = jnp.maximum(jnp.max(s, axis=1, keepdims=True), sink)
            p = jnp.exp(s - m)
            den = jnp.sum(p, axis=1, keepdims=True) + jnp.exp(sink - m)
            probs.append((p / den).astype(BF16))
            lse_acc = jnp.where(lane == h, m + jnp.log(den), lse_acc)
        outs = []
        for h, p in enumerate(probs):
            o = _dot(p, v2[_swa_kv_pair(h)])
            outs.append(pltpu.roll(o, HEAD_DIM, 1) if h % 2 != _swa_kv_half(h) else o)
        for pair in range(SWA_Q_HEADS // 2):
            o_ref[:, _swa_cols(2 * pair)] = jnp.where(first, outs[2 * pair], outs[2 * pair + 1]).astype(BF16)
        lse_ref[...] = lse_acc

    return pl.pallas_call(
        body,
        name="swa_fwd",
        grid=(nb,),
        in_specs=[smem, qrow] + kv_specs,
        out_specs=[qrow, pl.BlockSpec((WINDOW, LANES), lambda n: (n, 0))],
        out_shape=[jax.ShapeDtypeStruct((T, D_MODEL), BF16), jax.ShapeDtypeStruct((T, LANES), F32)],
        compiler_params=_cparams(),
    )(sinks, q, kv, kv, kv, kv)


def swa_bwd(q, kv, sinks, do, o, lse, cos, sin):
    T = q.shape[0]
    nb, qrow, kv_specs, smem = _swa_specs(T)
    kv_w = SWA_KV_HEADS * HEAD_DIM

    def body(sink_ref, q_ref, kc_ref, kp_ref, vc_ref, vp_ref, do_ref, o_ref, lse_ref, cos_ref, sin_ref,
             dq_ref, own_ref, prv_ref, dsink_ref):
        n = pl.program_id(0)

        @pl.when(n == 0)
        def _():
            dsink_ref[...] = jnp.zeros_like(dsink_ref)

        lane = lax.broadcasted_iota(jnp.int32, (WINDOW, LANES), 1)
        lane1 = lax.broadcasted_iota(jnp.int32, (1, LANES), 1)
        first = lane < HEAD_DIM
        valid = _swa_valid(n)
        cos_ = cos_ref[...]
        sin_ = sin_ref[...]
        k2 = [_kv_band(kp_ref, kc_ref, pb) for pb in range(SWA_KV_HEADS // 2)]
        v2 = [_kv_band(vp_ref, vc_ref, pb) for pb in range(SWA_KV_HEADS // 2)]
        q_t = q_ref[...].astype(F32).T.astype(BF16)
        do_t = do_ref[...].astype(F32).T.astype(BF16)
        stage1 = []
        for h in range(SWA_Q_HEADS):
            a, b, pb = h % 2, _swa_kv_half(h), _swa_kv_pair(h)
            qh = _to_half(q_ref[:, _swa_cols(h)], first, a, b)
            doh = _to_half(do_ref[:, _swa_cols(h)], first, a, b)
            stage1.append((jnp.where(valid, _dot_nt(qh, k2[pb]), NEG_BIG), _dot_nt(doh, v2[pb])))
        deltas = []
        for pair in range(SWA_Q_HEADS // 2):
            prod = do_ref[:, _swa_cols(2 * pair)].astype(F32) * o_ref[:, _swa_cols(2 * pair)].astype(F32)
            deltas += [jnp.sum(jnp.where(first, prod, 0.0), axis=1, keepdims=True),
                       jnp.sum(jnp.where(first, 0.0, prod), axis=1, keepdims=True)]
        stage2 = []
        dsink = jnp.zeros((1, LANES), F32)
        for h, (s, dp) in enumerate(stage1):
            lse_h = lse_ref[:, h:h + 1]
            p = jnp.exp(s - lse_h)
            delta = deltas[h]
            p_sink = jnp.exp(sink_ref[h] - lse_h)
            dsink = dsink + jnp.where(lane1 == h, -jnp.sum(p_sink * delta, axis=0, keepdims=True), 0.0)
            stage2.append(((p * (dp - delta)).astype(BF16), p.astype(BF16)))
        dqs = []
        dk_t = [None] * SWA_KV_HEADS
        dv_t = [None] * SWA_KV_HEADS
        for h, (ds, pb16) in enumerate(stage2):
            kvh = h // SWA_GROUP
            dims = slice(h * HEAD_DIM, (h + 1) * HEAD_DIM)
            dq = _dot(ds, k2[_swa_kv_pair(h)])
            dqs.append(pltpu.roll(dq, HEAD_DIM, 1) if h % 2 != _swa_kv_half(h) else dq)
            dk_h = _dot(q_t[dims, :], ds)
            dv_h = _dot(do_t[dims, :], pb16)
            dk_t[kvh] = dk_h if dk_t[kvh] is None else dk_t[kvh] + dk_h
            dv_t[kvh] = dv_h if dv_t[kvh] is None else dv_t[kvh] + dv_h
        for pair in range(SWA_Q_HEADS // 2):
            dqp = jnp.where(first, dqs[2 * pair], dqs[2 * pair + 1])
            dq_ref[:, _swa_cols(2 * pair)] = ((dqp * cos_ + _swap32(dqp * sin_)) * ATTN_SCALE).astype(BF16)
        for pb in range(SWA_KV_HEADS // 2):
            dk2 = jnp.concatenate([dk_t[2 * pb], dk_t[2 * pb + 1]], axis=0).T
            dv2 = jnp.concatenate([dv_t[2 * pb], dv_t[2 * pb + 1]], axis=0).T
            kcols = slice(pb * LANES, (pb + 1) * LANES)
            vcols = slice(kv_w + pb * LANES, kv_w + (pb + 1) * LANES)
            prv_ref[:, kcols] = dk2[:WINDOW]
            own_ref[:, kcols] = dk2[WINDOW:]
            prv_ref[:, vcols] = dv2[:WINDOW]
            own_ref[:, vcols] = dv2[WINDOW:]
        dsink_ref[...] += dsink

    tab = pl.BlockSpec((WINDOW, LANES), lambda n: (n, 0))
    kvrow = pl.BlockSpec((WINDOW, 2 * kv_w), lambda n: (n, 0))
    return pl.pallas_call(
        body,
        name="swa_bwd",
        grid=(nb,),
        in_specs=[smem, qrow] + kv_specs + [qrow, qrow, tab, tab, tab],
        out_specs=[qrow, kvrow, kvrow, pl.BlockSpec((1, LANES), lambda n: (0, 0))],
        out_shape=[
            jax.ShapeDtypeStruct((T, D_MODEL), BF16),
            jax.ShapeDtypeStruct((T, 2 * kv_w), F32),
            jax.ShapeDtypeStruct((T, 2 * kv_w), F32),
            jax.ShapeDtypeStruct((1, LANES), F32),
        ],
        compiler_params=_cparams(),
    )(sinks, q, kv, kv, kv, kv, do, o, lse, cos, sin)


def kv_grad_combine(own, prv, cos, sin):
    T = own.shape[0]
    nb = T // WINDOW
    kv_w = SWA_KV_HEADS * HEAD_DIM

    def body(own_ref, nxt_ref, cos_ref, sin_ref, out_ref):
        n = pl.program_id(0)
        nxt = jnp.where(n + 1 < nb, nxt_ref[...], 0.0)
        tot = own_ref[...] + nxt
        dk = tot[:, :kv_w]
        c = _tile_lanes(cos_ref[...], kv_w)
        s = _tile_lanes(sin_ref[...], kv_w)
        out_ref[:, :kv_w] = (dk * c + _swap32(dk * s)).astype(BF16)
        out_ref[:, kv_w:] = tot[:, kv_w:].astype(BF16)

    tab = pl.BlockSpec((WINDOW, LANES), lambda n: (n, 0))
    kvrow = pl.BlockSpec((WINDOW, 2 * kv_w), lambda n: (n, 0))
    return pl.pallas_call(
        body,
        name="kv_grad_combine",
        grid=(nb,),
        in_specs=[kvrow, pl.BlockSpec((WINDOW, 2 * kv_w), lambda n: (jnp.minimum(n + 1, nb - 1), 0)), tab, tab],
        out_specs=kvrow,
        out_shape=jax.ShapeDtypeStruct((T, 2 * kv_w), BF16),
        compiler_params=_cparams(),
    )(own, prv, cos, sin)


ANY = pl.BlockSpec(memory_space=pl.ANY)


def _place():
    x, y, c = lax.axis_index("x"), lax.axis_index("y"), lax.axis_index("c")
    other_chips = [(1 - x, y), (x, 1 - y), (1 - x, 1 - y)]
    return x, y, c, 2 * x + y, other_chips


N_PEER_CHIPS = N_CHIPS - 1


class GatherOps:
    def __init__(self, rows, shards, lands, ici_send, ici_recv, d2d_send=None, d2d_recv=None):
        self.rows, self.shards, self.lands = rows, shards, lands
        self.ici_send, self.ici_recv, self.d2d_send, self.d2d_recv = ici_send, ici_recv, d2d_send, d2d_recv
        self.x, self.y, self.c, self.me, self.chips = _place()
        self.pairs = [(t, jdx) for t in range(len(rows)) for jdx in range(N_PEER_CHIPS)]

    def _half(self, ref, t, which):
        r = self.rows[t] // 2
        return ref.at[:, pl.ds(pl.multiple_of(which * r, 16), r), :]

    def _ici(self, t, jdx):
        px, py = self.chips[jdx]
        return pltpu.make_async_remote_copy(
            src_ref=self._half(self.shards[t], t, self.c), dst_ref=self._half(self.lands[t].at[self.me], t, self.c),
            send_sem=self.ici_send.at[t, jdx], recv_sem=self.ici_recv.at[t, jdx],
            device_id=(px, py, self.c), device_id_type=MESH)

    def _landed(self, t, jdx):
        px, py = self.chips[jdx]
        blk = self._half(self.lands[t].at[2 * px + py], t, self.c)
        return pltpu.make_async_remote_copy(
            src_ref=blk, dst_ref=blk, send_sem=self.ici_send.at[t, jdx], recv_sem=self.ici_recv.at[t, jdx],
            device_id=(px, py, self.c), device_id_type=MESH)

    def _d2d(self, t, jdx, which):
        px, py = self.chips[jdx]
        blk = self._half(self.lands[t].at[2 * px + py], t, which)
        return pltpu.make_async_remote_copy(
            src_ref=blk, dst_ref=blk, send_sem=self.d2d_send.at[t, jdx], recv_sem=self.d2d_recv.at[t, jdx],
            device_id=(self.x, self.y, 1 - self.c), device_id_type=MESH)

    def start_ici(self):
        for t, jdx in self.pairs:
            self._ici(t, jdx).start()

    def wait_ici(self):
        for t, jdx in self.pairs:
            self._landed(t, jdx).wait_recv()
        self.wait_ici_sends()

    def wait_ici_sends(self):
        for t, jdx in self.pairs:
            self._ici(t, jdx).wait_send()

    def forward_arrivals(self):
        for t, jdx in self.pairs:
            self._landed(t, jdx).wait_recv()
            self._d2d(t, jdx, self.c).start()

    def start_forwards(self):
        for t, jdx in self.pairs:
            self._d2d(t, jdx, self.c).start()

    def wait_forwards(self):
        for t, jdx in self.pairs:
            self._d2d(t, jdx, 1 - self.c).wait_recv()
            self._d2d(t, jdx, self.c).wait_send()


def all_gather_weights(shards, lands):
    n = len(shards)
    rows = [s.shape[1] for s in shards]

    def body(*refs):
        ins, outs = refs[:n], refs[2 * n:3 * n]
        ops = GatherOps(rows, ins, outs, *refs[3 * n:])
        ops.start_ici()
        ops.forward_arrivals()
        ops.wait_forwards()
        ops.wait_ici_sends()

    return pl.pallas_call(
        body,
        name="all_gather_weights",
        in_specs=[ANY] * (2 * n),
        out_specs=[ANY] * n,
        out_shape=[jax.ShapeDtypeStruct(l.shape, l.dtype) for l in lands],
        input_output_aliases={n + t: t for t in range(n)},
        scratch_shapes=[pltpu.SemaphoreType.DMA((n, N_PEER_CHIPS))] * 4,
    )(*shards, *lands)


def place_own_shard(name, shard, chip):
    nl, r, c = shard.shape

    def body(chip_ref, s_ref, o_ref):
        o_ref[...] = s_ref[...]

    return pl.pallas_call(
        body, name=name,
        grid_spec=pltpu.PrefetchScalarGridSpec(
            num_scalar_prefetch=1, grid=(nl,),
            in_specs=[pl.BlockSpec((None, r, c), lambda l, chip_ref: (l, 0, 0))],
            out_specs=pl.BlockSpec((None, None, r, c), lambda l, chip_ref: (chip_ref[0], l, 0, 0))),
        out_shape=jax.ShapeDtypeStruct((N_CHIPS,) + shard.shape, shard.dtype), compiler_params=_cparams(),
    )(chip, shard)


def exchange_halves(name, slabs):
    n = len(slabs)

    def body(*refs):
        ins, theirs = refs[:n], refs[n:2 * n]
        send_sems, recv_sems = refs[2 * n:]
        x, y, c, _, _ = _place()
        copies = []
        for t in range(n):
            cp = pltpu.make_async_remote_copy(
                src_ref=ins[t].at[1 - c], dst_ref=theirs[t], send_sem=send_sems.at[t],
                recv_sem=recv_sems.at[t], device_id=(x, y, 1 - c), device_id_type=MESH)
            cp.start()
            copies.append(cp)
        for cp in copies:
            cp.wait()

    return pl.pallas_call(
        body,
        name=name,
        in_specs=[ANY] * n,
        out_specs=[ANY] * n,
        out_shape=[jax.ShapeDtypeStruct(s.shape[1:], s.dtype) for s in slabs],
        scratch_shapes=[pltpu.SemaphoreType.DMA((n,)), pltpu.SemaphoreType.DMA((n,))],
    )(*slabs)


def chip_partial_copies(ins, outs, send_sems, recv_sems):
    _, _, c, me, chips = _place()
    return [pltpu.make_async_remote_copy(
        src_ref=ins[t].at[2 * px + py], dst_ref=outs[t].at[me], send_sem=send_sems.at[t, jdx],
        recv_sem=recv_sems.at[t, jdx], device_id=(px, py, c), device_id_type=MESH)
        for t in range(len(ins)) for jdx, (px, py) in enumerate(chips)]


def exchange_chip_partials(name, parts):
    n = len(parts)

    def body(*refs):
        copies = chip_partial_copies(refs[:n], refs[n:2 * n], *refs[2 * n:])
        for cp in copies:
            cp.start()
        for cp in copies:
            cp.wait()

    return pl.pallas_call(
        body,
        name=name,
        in_specs=[ANY] * n,
        out_specs=[ANY] * n,
        out_shape=[jax.ShapeDtypeStruct(p.shape, p.dtype) for p in parts],
        scratch_shapes=[pltpu.SemaphoreType.DMA((n, 3)), pltpu.SemaphoreType.DMA((n, 3))],
    )(*parts)


def sibling_share_copies(ins, outs, send_sems, recv_sems):
    x, y, c, _, _ = _place()
    return [pltpu.make_async_remote_copy(
        src_ref=ins[t], dst_ref=outs[t], send_sem=send_sems.at[t], recv_sem=recv_sems.at[t],
        device_id=(x, y, 1 - c), device_id_type=MESH) for t in range(len(ins))]


def share_reduced_halves(name, halves):
    n = len(halves)

    def body(*refs):
        copies = sibling_share_copies(refs[:n], refs[n:2 * n], *refs[2 * n:])
        for cp in copies:
            cp.start()
        for cp in copies:
            cp.wait()

    return pl.pallas_call(
        body,
        name=name,
        in_specs=[ANY] * n,
        out_specs=[ANY] * n,
        out_shape=[jax.ShapeDtypeStruct(h.shape, h.dtype) for h in halves],
        scratch_shapes=[pltpu.SemaphoreType.DMA((n,)), pltpu.SemaphoreType.DMA((n,))],
    )(*halves)


def _row_tile(r, c):
    tr = r
    while tr * c * 4 > (3 << 19) and tr % 16 == 0:
        tr //= 2
    return tr


def add_sibling(name, slab, theirs, core):
    _, ns, slots, r, c = slab.shape
    tr = _row_tile(r, c)

    def body(core_ref, a_ref, b_ref, o_ref):
        o_ref[...] = (a_ref[...] + b_ref[...]).astype(BF16)

    blk = pl.BlockSpec((None, None, tr, c), lambda s, l, i, core_ref: (s, l, i, 0))
    return pl.pallas_call(
        body, name=name,
        grid_spec=pltpu.PrefetchScalarGridSpec(
            num_scalar_prefetch=1, grid=(ns, slots, r // tr),
            in_specs=[pl.BlockSpec((None, None, None, tr, c), lambda s, l, i, core_ref: (core_ref[0], s, l, i, 0)), blk],
            out_specs=blk),
        out_shape=jax.ShapeDtypeStruct(theirs.shape, BF16), compiler_params=_cparams(),
    )(core, slab, theirs)


def sum_chips(name, recv, own, chip):
    _, slots, r, c = recv.shape
    tr = _row_tile(r, c)

    def body(chip_ref, r0, r1, r2, r3, own_ref, o_ref):
        me = chip_ref[0]
        mine = own_ref[...]
        terms = [jnp.where(me == s, mine, rr[...]).astype(F32) for s, rr in enumerate((r0, r1, r2, r3))]
        o_ref[...] = ((terms[0] + terms[1]) + terms[2]) + terms[3]

    def src(s):
        return pl.BlockSpec((None, None, tr, c),
                            lambda l, i, chip_ref: (jnp.where(chip_ref[0] == s, (s + 1) % N_CHIPS, s), l, i, 0))

    return pl.pallas_call(
        body, name=name,
        grid_spec=pltpu.PrefetchScalarGridSpec(
            num_scalar_prefetch=1, grid=(slots, r // tr),
            in_specs=[src(0), src(1), src(2), src(3),
                      pl.BlockSpec((None, None, tr, c), lambda l, i, chip_ref: (chip_ref[0], l, i, 0))],
            out_specs=pl.BlockSpec((None, tr, c), lambda l, i, chip_ref: (l, i, 0))),
        out_shape=jax.ShapeDtypeStruct((slots, r, c), F32), compiler_params=_cparams(),
    )(chip, recv, recv, recv, recv, own)


def _adamw_math(w, g, m, v):
    m = ADAM_B1 * m + (1.0 - ADAM_B1) * g
    v = ADAM_B2 * v + (1.0 - ADAM_B2) * (g * g)
    m_hat = m / (1.0 - ADAM_B1 ** ADAM_STEP)
    v_hat = v / (1.0 - ADAM_B2 ** ADAM_STEP)
    delta = -ADAM_LR * (m_hat / (jnp.sqrt(v_hat) + ADAM_EPS) + ADAM_WD * w)
    return delta, m, v


def adamw_shard(name, w, m, v, g_pairs, core, slots, row_halves):
    n = w.shape[0]
    assert n == len(g_pairs)
    _, r, c = g_pairs[0][0].shape
    tr = _row_tile(r, c)
    nr = r // tr

    def body(core_ref, w_ref, m_ref, v_ref, *rest):
        g_refs, (go_ref, d_ref, mo_ref, vo_ref) = rest[:2 * n], rest[2 * n:]
        mine = pl.program_id(1) == core_ref[0]
        g = jnp.where(mine, g_refs[0][...], g_refs[1][...])
        for l in range(1, n):
            g = jnp.where(pl.program_id(0) == l, jnp.where(mine, g_refs[2 * l][...], g_refs[2 * l + 1][...]), g)
        delta, mm, vv = _adamw_math(w_ref[...], g, m_ref[...], v_ref[...])
        go_ref[...] = g
        d_ref[...] = delta
        mo_ref[...] = mm
        vo_ref[...] = vv

    if row_halves:
        wspec = pl.BlockSpec((None, tr, c), lambda l, h, i, core_ref: (l, h * nr + i, 0))
    else:
        wspec = pl.BlockSpec((None, tr, c), lambda l, h, i, core_ref: (l, i, h))
    def gspec(slot):
        return pl.BlockSpec((None, tr, c), lambda l, h, i, core_ref: (slot, i, 0))

    shp = jax.ShapeDtypeStruct(w.shape, F32)
    return pl.pallas_call(
        body, name=name,
        grid_spec=pltpu.PrefetchScalarGridSpec(
            num_scalar_prefetch=1, grid=(n, 2, nr),
            in_specs=[wspec, wspec, wspec] + [gspec(s) for s in slots for _ in range(2)], out_specs=[wspec] * 4),
        out_shape=[shp] * 4, compiler_params=_cparams(),
    )(core, w, m, v, *[g for pair in g_pairs for g in pair])


SMALL_ROWS = 16


def small_allreduce_adamw(part, w, m, v):
    def body(p_ref, w_ref, m_ref, v_ref, g_ref, d_ref, mo_ref, vo_ref, buf, send_sems, recv_sems):
        x, y, c, _, _ = _place()
        me = 4 * x + 2 * y + c
        buf[me] = p_ref[...]
        copies = []
        for k in range(1, N_DEV):
            kx, ky, kc = (k >> 2) & 1, (k >> 1) & 1, k & 1
            peer = (x ^ kx, y ^ ky, c ^ kc)
            cp = pltpu.make_async_remote_copy(
                src_ref=p_ref, dst_ref=buf.at[me], send_sem=send_sems.at[k - 1],
                recv_sem=recv_sems.at[k - 1], device_id=peer, device_id_type=MESH)
            cp.start()
            copies.append(cp)
        for cp in copies:
            cp.wait()
        g = buf[0]
        for dev in range(1, N_DEV):
            g = g + buf[dev]
        delta, mm, vv = _adamw_math(w_ref[...], g, m_ref[...], v_ref[...])
        g_ref[...] = g
        d_ref[...] = delta
        mo_ref[...] = mm
        vo_ref[...] = vv

    vm = pl.BlockSpec(memory_space=pltpu.VMEM)
    shp = jax.ShapeDtypeStruct(part.shape, F32)
    return pl.pallas_call(
        body, name="small_allreduce_adamw",
        in_specs=[vm] * 4, out_specs=[vm] * 4, out_shape=[shp] * 4,
        scratch_shapes=[
            pltpu.VMEM((N_DEV,) + part.shape, F32),
            pltpu.SemaphoreType.DMA((N_DEV - 1,)), pltpu.SemaphoreType.DMA((N_DEV - 1,)),
        ],
    )(part, w, m, v)


def _rope_tables(T):
    half = HEAD_DIM // 2
    inv_freq = ROPE_THETA ** (-jnp.arange(half, dtype=F32) / half)
    ang = jnp.arange(T).astype(F32)[:, None] * inv_freq[None, :]
    cos = jnp.tile(jnp.cos(ang), (1, LANES // half))
    sin = jnp.tile(jnp.sin(ang), (1, LANES // half))
    lane = jnp.arange(LANES)
    sign = jnp.where((lane % HEAD_DIM) < half, -1.0, 1.0).astype(F32)
    return cos, sin * sign[None, :]


def _pack_small(ffn1, mix, ffn2, kvn, fin, sinks, loss_row):
    sink_row = jnp.pad(sinks.reshape(1, SWA_Q_HEADS), ((0, 0), (0, D_MODEL - SWA_Q_HEADS)))
    rows = jnp.concatenate([ffn1, mix, ffn2, kvn.reshape(1, -1), fin.reshape(1, -1), sink_row, loss_row], axis=0)
    return jnp.concatenate([rows, jnp.zeros((SMALL_ROWS - rows.shape[0], D_MODEL), F32)], axis=0)


def kernel(x, ffn1_norm, ffn1_w_in, ffn1_w_out, mix_norm, ffn2_norm, ffn2_w_in, ffn2_w_out, sb_w_qkv, sb_w_o, kv_norm, kv_w, swa_w_q, swa_sinks, swa_w_o, final_norm, loss_target, m_ffn1_norm, m_ffn1_w_in, m_ffn1_w_out, m_mix_norm, m_ffn2_norm, m_ffn2_w_in, m_ffn2_w_out, m_sb_w_qkv, m_sb_w_o, m_kv_norm, m_kv_w, m_swa_w_q, m_swa_sinks, m_swa_w_o, m_final_norm, v_ffn1_norm, v_ffn1_w_in, v_ffn1_w_out, v_mix_norm, v_ffn2_norm, v_ffn2_w_in, v_ffn2_w_out, v_sb_w_qkv, v_sb_w_o, v_kv_norm, v_kv_w, v_swa_w_q, v_swa_sinks, v_swa_w_o, v_final_norm):
    T = x.shape[1]
    kv_cols = SWA_KV_HEADS * HEAD_DIM
    x2 = x.reshape(T, D_MODEL)
    tgt = loss_target.reshape(T, D_MODEL)
    cos, sin = _rope_tables(T)

    w_in_l = jnp.concatenate([ffn1_w_in, ffn2_w_in], axis=0).astype(BF16)
    w_out_l = jnp.concatenate([ffn1_w_out, ffn2_w_out], axis=0).astype(BF16)
    sq_l = jnp.concatenate([sb_w_o, swa_w_q, swa_w_o], axis=0).astype(BF16)
    qkv_l = sb_w_qkv[0].astype(BF16)
    kvw_l = kv_w.astype(BF16)
    core = lax.axis_index("c").astype(jnp.int32).reshape(1)
    chip = (2 * lax.axis_index("x") + lax.axis_index("y")).astype(jnp.int32).reshape(1)
    early = [w_in_l[:1], w_out_l[:1]]
    mid = [sq_l, qkv_l[None]]
    late = [w_in_l[1:], w_out_l[1:], kvw_l[None]]
    early_lands = [place_own_shard(f"own_early_{t}", s, chip) for t, s in enumerate(early)]
    mid_lands = [place_own_shard(f"own_mid_{t}", s, chip) for t, s in enumerate(mid)]
    late_lands = [place_own_shard(f"own_late_{t}", s, chip) for t, s in enumerate(late)]
    w_in0, w_out0 = all_gather_weights(early, early_lands)

    def ffn_w(slot):
        return (w_in0, w_out0, 0) if slot == 0 else (w_in_r, w_out_r, slot - 1)

    def vec(a, i):
        return a[i].reshape(1, D_MODEL)

    ident = lambda w: w
    sq_prep = lambda w: w.reshape(D_MODEL, w.shape[-1])
    qscale = jnp.concatenate([jnp.full((1, D_MODEL), ATTN_SCALE, F32), jnp.ones((1, 2 * D_MODEL), F32)], axis=1)
    swa_scale = jnp.full((1, D_MODEL), ATTN_SCALE, F32)
    sinks = swa_sinks.reshape(SWA_Q_HEADS)

    h1, gate1, up1, (w_sq, w_qkv) = ffn_fwd("l0a", x2, vec(ffn1_norm, 0), *ffn_w(SLOT_FFN1[0]), mid, mid_lands)
    w_qkv = w_qkv.reshape(N_CHIPS, D_MODEL, QKV_COLS)
    qkv = qkv_fwd(h1, vec(mix_norm, 0), w_qkv, qscale)
    o_sb, tot, sb_first, late_lands = sb_fwd(qkv, late, late_lands)
    h2, (w_in_r, w_out_r, w_kv) = linear_res("sb_out", o_sb, w_sq, SQ_SB_O, h1, late_lands)
    w_kv = w_kv.reshape(D_MODEL, 2 * kv_cols)
    h3, gate2, up2 = ffn_fwd("l0b", h2, vec(ffn2_norm, 0), *ffn_w(SLOT_FFN2[0]))
    kvn = kv_norm.reshape(1, D_MODEL)
    kv_sw = rms_linear("kv_proj", h3, kvn, w_kv, pl.BlockSpec((D_MODEL, kv_cols), lambda i, j: (0, j)), ident,
                       2 * kv_cols, kv_cols, rope=(cos, sin), rope_blocks=1)
    h4, gate3, up3 = ffn_fwd("l1a", h3, vec(ffn1_norm, 1), *ffn_w(SLOT_FFN1[1]))
    q_sw = rms_linear("swa_q", h4, vec(mix_norm, 1), w_sq,
                      pl.BlockSpec((N_CHIPS, None, SQ_ROWS, 512), lambda i, j: (0, SQ_SWA_Q, 0, j)), sq_prep,
                      D_MODEL, 512, rope=(cos, sin), scale=swa_scale)
    o_sw, lse = swa_fwd(q_sw, kv_sw, sinks)
    h5 = linear_res("swa_out", o_sw, w_sq, SQ_SWA_O, h4)
    dh6, gate4, up4, loss_p, d_final = ffn_fwd("l1b", h5, vec(ffn2_norm, 1), *ffn_w(SLOT_FFN2[1]),
                                               loss=(final_norm.reshape(1, D_MODEL), tgt))

    slab = {}
    ffn_place = {SLOT_FFN1[0]: (0, 0, 1), SLOT_FFN1[1]: (1, 0, 3), SLOT_FFN2[0]: (1, 1, 3), SLOT_FFN2[1]: (1, 2, 3)}
    sq_place = {SQ_SB_O: (1, 0, 3), SQ_SWA_Q: (1, 1, 3), SQ_SWA_O: (1, 2, 3)}

    exchanged = {}

    def chip_parts(grp, kinds):
        theirs = exchange_halves(f"exchange_halves_{grp}{kinds[0]}", [slab[kind, grp][1] for kind in kinds])
        return [add_sibling(f"add_sibling_{kind}{grp}", slab[kind, grp][0], t, core) for kind, t in zip(kinds, theirs)]

    def ffn_grads(tag, dh, h_in, g, gate, up, slot, proj=None, early=()):
        dh_in, xn, dg_, du_, act, dhb, dnorm, *through_proj = ffn_bwd(tag, dh, h_in, g, gate, up, *ffn_w(slot),
                                                                      proj=proj)
        grp, s, ns = ffn_place[slot]
        in_shape = (2, N_CHIPS, ns, D_MODEL // 2, FF_CHUNK)
        out_shape = (2, N_CHIPS, ns, FF_ROWS, D_MODEL // 2)
        blk = (None, 1, None, D_MODEL // 2, FF_CHUNK)
        slab["in", grp] = mm_tn(f"dw_gate_{tag}", xn, dg_, D_MODEL // 2, FF_CHUNK, blk,
                                lambda k, n: (k, n, s, 0, 0), in_shape, prev=slab.get(("in", grp)))
        slab["in", grp] = mm_tn(f"dw_up_{tag}", xn, du_, D_MODEL // 2, FF_CHUNK, blk,
                                lambda k, n: (k, 2 + n, s, 0, 0), in_shape, prev=slab["in", grp])
        parts = chip_parts(grp, list(early)) if early else []
        res = mm_tn(f"dw_out_{tag}", act, dhb, FF_CHUNK, D_MODEL // 2, (None, 2, None, FF_ROWS, D_MODEL // 2),
                    lambda k, n: (n, k, s, 0, 0), out_shape, prev=slab.get(("out", grp)), bg_parts=parts)
        slab["out", grp] = res[:2]
        exchanged.update({(kind, grp): (a, p) for kind, a, p in zip(early, res[2:], parts)})
        return (dh_in, dnorm, *through_proj)

    def sq_grad(tag, a, dyb, t):
        grp, s, ns = sq_place[t]
        slab["sq", grp] = mm_tn(f"dw_sq_{tag}", a, dyb, D_MODEL, D_MODEL // 2,
                                (None, N_CHIPS, None, SQ_ROWS, D_MODEL // 2),
                                lambda k, n: (n, 0, s, 0, 0), (2, N_CHIPS, ns, SQ_ROWS, D_MODEL // 2),
                                prev=slab.get(("sq", grp)))

    def reduce_group(grp, kinds, host=None, share_host=None):
        todo = [kind for kind in kinds if (kind, grp) not in exchanged]
        parts = chip_parts(grp, todo)
        arrived = host(parts) if host else exchange_chip_partials(f"exchange_chip_partials_{grp}", parts)
        exchanged.update({(kind, grp): (a, p) for kind, a, p in zip(todo, arrived, parts)})
        halves = [sum_chips(f"sum_chips_{kind}{grp}", *exchanged[kind, grp], chip) for kind in kinds]
        if share_host:
            sib_halves = share_host(halves)
        else:
            sib_halves = share_reduced_halves(f"share_reduced_halves_{grp}", halves)
        return {kind: pair for kind, pair in zip(kinds, zip(halves, sib_halves))}

    dh5, d_ffn2_1, dh5b, do_sw = ffn_grads("l1b", dh6, h5, vec(ffn2_norm, 1), gate4, up4, SLOT_FFN2[1],
                                           proj=(w_sq, SQ_SWA_O))
    sq_grad("swa_o", o_sw, dh5b, SQ_SWA_O)
    dq_sw, kv_own, kv_prev, d_sinks = swa_bwd(q_sw, kv_sw, sinks, do_sw, o_sw, lse, cos, sin)
    sq_w_spec = pl.BlockSpec((N_CHIPS, None, SQ_ROWS, D_MODEL), lambda i, j: (0, SQ_SWA_Q, 0, 0))
    dh4, hn4, d_mix_1 = linear_bwd_rms("swa_q_bwd", [(dq_sw, w_sq, sq_w_spec, sq_prep)], h4, vec(mix_norm, 1), dh5,
                                       1, D_MODEL)
    sq_grad("swa_q", hn4, dq_sw, SQ_SWA_Q)
    dh3a, d_ffn1_1 = ffn_grads("l1a", dh4, h3, vec(ffn1_norm, 1), gate3, up3, SLOT_FFN1[1])
    dkv = kv_grad_combine(kv_own, kv_prev, cos, sin)
    kv_w_spec = pl.BlockSpec((D_MODEL, 2 * kv_cols), lambda i, j: (0, 0))
    dh3, xn3, d_kvn = linear_bwd_rms("kv_bwd", [(dkv, w_kv, kv_w_spec, ident)], h3, kvn, dh3a, 1, 2 * kv_cols)
    slab["kv", 1] = mm_tn("dw_kv", xn3, dkv, D_MODEL, kv_cols, (None, N_CHIPS, None, SQ_ROWS, kv_cols),
                          lambda k, n: (n, 0, 0, 0, 0), (2, N_CHIPS, 1, SQ_ROWS, kv_cols))
    dh2, d_ffn2_0, dh2b, do_sb = ffn_grads("l0b", dh3, h2, vec(ffn2_norm, 0), gate2, up2, SLOT_FFN2[0],
                                           proj=(w_sq, SQ_SB_O))
    sq_grad("sb_o", o_sb, dh2b, SQ_SB_O)
    sb_grads = []

    def behind_sb_bwd(parts):
        dq_sb, dk_sb, dv_sb, arrived = sb_bwd(qkv, do_sb, tot, sb_first, parts)
        sb_grads.extend([dq_sb, dk_sb, dv_sb])
        return arrived

    qkv_grads = []

    def behind_qkv_bwd(halves):
        dy = jnp.concatenate(sb_grads, axis=1)
        dh, hn, d_mix, theirs = qkv_bwd(dy, w_qkv, h1, vec(mix_norm, 0), dh2, halves)
        qkv_grads.extend([dy, dh, hn, d_mix])
        return theirs

    red = {1: reduce_group(1, ["in", "out", "sq", "kv"], host=behind_sb_bwd, share_host=behind_qkv_bwd)}
    dqkv, dh1, hn1, d_mix_0 = qkv_grads
    slab["qkv", 0] = mm_tn("dw_qkv", hn1, dqkv, D_MODEL // 2, QKV_COLS, (None, 1, None, D_MODEL // 2, QKV_COLS),
                           lambda k, n: (k, n, 0, 0, 0), (2, N_CHIPS, 1, D_MODEL // 2, QKV_COLS))
    dx, d_ffn1_0 = ffn_grads("l0a", dh1, x2, vec(ffn1_norm, 0), gate1, up1, SLOT_FFN1[0], early=("in", "qkv"))
    red[0] = reduce_group(0, ["in", "out", "qkv"])

    def upd(name, w, m, v, kind, places, row_halves):
        shp = w.shape
        w3 = w.reshape((-1,) + shp[-2:])
        outs = adamw_shard(name, w3, m.reshape(w3.shape), v.reshape(w3.shape),
                           [red[grp][kind] for grp, _ in places], core, [s for _, s in places], row_halves)
        return [o.reshape(shp) for o in outs]

    ffn1_places = [ffn_place[s][:2] for s in SLOT_FFN1]
    ffn2_places = [ffn_place[s][:2] for s in SLOT_FFN2]
    r_ffn1_in = upd("adamw_ffn1_in", ffn1_w_in, m_ffn1_w_in, v_ffn1_w_in, "in", ffn1_places, True)
    r_ffn2_in = upd("adamw_ffn2_in", ffn2_w_in, m_ffn2_w_in, v_ffn2_w_in, "in", ffn2_places, True)
    r_ffn1_out = upd("adamw_ffn1_out", ffn1_w_out, m_ffn1_w_out, v_ffn1_w_out, "out", ffn1_places, False)
    r_ffn2_out = upd("adamw_ffn2_out", ffn2_w_out, m_ffn2_w_out, v_ffn2_w_out, "out", ffn2_places, False)
    r_qkv = upd("adamw_qkv", sb_w_qkv, m_sb_w_qkv, v_sb_w_qkv, "qkv", [(0, 0)], True)
    r_sb_o = upd("adamw_sb_o", sb_w_o, m_sb_w_o, v_sb_w_o, "sq", [sq_place[SQ_SB_O][:2]], False)
    r_swa_q = upd("adamw_swa_q", swa_w_q, m_swa_w_q, v_swa_w_q, "sq", [sq_place[SQ_SWA_Q][:2]], False)
    r_swa_o = upd("adamw_swa_o", swa_w_o, m_swa_w_o, v_swa_w_o, "sq", [sq_place[SQ_SWA_O][:2]], False)
    r_kv = upd("adamw_kv", kv_w, m_kv_w, v_kv_w, "kv", [(1, 0)], False)

    loss_row = jnp.pad(loss_p, ((0, 0), (0, D_MODEL - LANES)))
    d_sink_row = d_sinks[0, :SWA_Q_HEADS]
    part = _pack_small(jnp.concatenate([d_ffn1_0, d_ffn1_1], axis=0), jnp.concatenate([d_mix_0, d_mix_1], axis=0),
                       jnp.concatenate([d_ffn2_0, d_ffn2_1], axis=0), d_kvn, d_final, d_sink_row, loss_row)
    zrow = jnp.zeros((1, D_MODEL), F32)
    small = small_allreduce_adamw(
        part,
        _pack_small(ffn1_norm, mix_norm, ffn2_norm, kv_norm, final_norm, swa_sinks, zrow),
        _pack_small(m_ffn1_norm, m_mix_norm, m_ffn2_norm, m_kv_norm, m_final_norm, m_swa_sinks, zrow),
        _pack_small(v_ffn1_norm, v_mix_norm, v_ffn2_norm, v_kv_norm, v_final_norm, v_swa_sinks, zrow))

    def unpack(p):
        return dict(ffn1_norm=p[0:2], mix_norm=p[2:4], ffn2_norm=p[4:6], kv_norm=p[6], final_norm=p[7],
                    swa_sinks=p[8:9, :SWA_Q_HEADS])

    big = dict(ffn1_w_in=r_ffn1_in, ffn1_w_out=r_ffn1_out, ffn2_w_in=r_ffn2_in, ffn2_w_out=r_ffn2_out,
               sb_w_qkv=r_qkv, sb_w_o=r_sb_o, kv_w=r_kv, swa_w_q=r_swa_q, swa_w_o=r_swa_o)
    order = ["ffn1_norm", "ffn1_w_in", "ffn1_w_out", "mix_norm", "ffn2_norm", "ffn2_w_in", "ffn2_w_out",
             "sb_w_qkv", "sb_w_o", "kv_norm", "kv_w", "swa_w_q", "swa_sinks", "swa_w_o", "final_norm"]
    outs = []
    for kind in range(4):
        sm = unpack(small[kind])
        for nm in order:
            outs.append(big[nm][kind] if nm in big else sm[nm])
    loss = small[0][9, 0]
    return (loss, dx.reshape(x.shape), *outs)
```

```python
import jax
import jax.numpy as jnp
from jax import lax
from jax.experimental import pallas as pl
from jax.experimental.pallas import tpu as pltpu

F32 = jnp.float32
BF16 = jnp.bfloat16
MESH = pl.DeviceIdType.MESH

D_MODEL = 1024
D_FF = 2816
HEAD_DIM = 64
SB_HEADS = 16
SWA_Q_HEADS = 16
SWA_KV_HEADS = 4
WINDOW = 128
ROPE_THETA = 10000.0
RMS_EPS = 1e-6
FFN_RES_SCALE = 0.5
ATTN_SCALE = HEAD_DIM ** -0.5

ADAM_LR = 0.001
ADAM_B1 = 0.9
ADAM_B2 = 0.999
ADAM_EPS = 1e-08
ADAM_WD = 0.01
ADAM_STEP = 10

N_CHIPS = 4
N_DEV = 8
LANES = 128
FF_CHUNK = D_FF // 2
FF_ROWS = D_FF // N_CHIPS
SQ_ROWS = D_MODEL // N_CHIPS
QKV_COLS = 3 * D_MODEL // N_CHIPS
VMEM_LIMIT = 56 * 1024 * 1024
NEG_BIG = -1e30

SLOT_FFN1 = (0, 1)
SLOT_FFN2 = (2, 3)
SQ_SB_O, SQ_SWA_Q, SQ_SWA_O = 0, 1, 2


def _cparams():
    return pltpu.CompilerParams(vmem_limit_bytes=VMEM_LIMIT)


def _dot(a, b):
    return jnp.dot(a, b, preferred_element_type=F32)


def _dot_nt(a, b):
    return lax.dot_general(a, b, (((1,), (1,)), ((), ())), preferred_element_type=F32)


def _dot_tn(a, b):
    return lax.dot_general(a, b, (((0,), (0,)), ((), ())), preferred_element_type=F32)


def _rstd(h):
    return lax.rsqrt(jnp.mean(h * h, axis=-1, keepdims=True) + RMS_EPS)


def _swap32(x):
    n = x.shape[-1]
    lane = lax.broadcasted_iota(jnp.int32, x.shape, x.ndim - 1)
    first = (lane % HEAD_DIM) < (HEAD_DIM // 2)
    return jnp.where(first, pltpu.roll(x, n - HEAD_DIM // 2, x.ndim - 1), pltpu.roll(x, HEAD_DIM // 2, x.ndim - 1))


def _tile_lanes(t, n):
    return t if n == LANES else jnp.tile(t, (1, n // LANES))


FFN_ROWS = 256


def _ffn_w_in_spec(slot):
    return pl.BlockSpec((N_CHIPS, None, D_MODEL, FF_CHUNK), lambda i: (0, slot, 0, 0), pipeline_mode=pl.Buffered(1))


def _ffn_w_out_spec(slot):
    return pl.BlockSpec((N_CHIPS, None, FF_ROWS, D_MODEL), lambda i: (0, slot, 0, 0), pipeline_mode=pl.Buffered(1))


def ffn_fwd(tag, h, g, w_in, w_out, slot, bg_shards=(), bg_lands=(), loss=None):
    T = h.shape[0]
    tm = FFN_ROWS
    nch = D_FF // FF_CHUNK
    nbg = len(bg_shards)
    nt = T // tm
    nloss = 2 if loss is not None else 0

    def body(h_ref, g_ref, wi_ref, wo_ref, *rest):
        loss_in, rest = rest[:nloss], rest[nloss:]
        out_ref, gate_ref, up_ref = rest[2 * nbg:2 * nbg + 3]
        loss_out, rest = rest[3 * nbg + 3:3 * nbg + 3 + nloss], rest[:3 * nbg + 3] + rest[3 * nbg + 3 + nloss:]
        wg_s, wu_s = rest[3 * nbg + 3:3 * nbg + 5]
        step = pl.program_id(0)
        if nbg:
            gather = GatherOps([s.shape[1] for s in bg_shards], rest[:nbg], rest[2 * nbg + 3:3 * nbg + 3],
                               *rest[3 * nbg + 5:])
            pl.when(step == 0)(gather.start_ici)
            pl.when(step == nt // 2)(gather.forward_arrivals)

            @pl.when(step == nt - 1)
            def _():
                gather.wait_forwards()
                gather.wait_ici_sends()

        @pl.when(step == 0)
        def _():
            for j in range(nch):
                cols = slice(j * FF_CHUNK, (j + 1) * FF_CHUNK)
                wg_s[:, cols] = wi_ref[j]
                wu_s[:, cols] = wi_ref[nch + j]

        hh = h_ref[...]
        xn = (hh * _rstd(hh) * g_ref[...]).astype(BF16)
        gate = _dot(xn, wg_s[...])
        up = _dot(xn, wu_s[...])
        gate_ref[...] = gate.astype(BF16)
        up_ref[...] = up.astype(BF16)
        a = (gate * jax.nn.sigmoid(gate) * up).astype(BF16)
        h_out = hh + FFN_RES_SCALE * _dot(a, wo_ref[...].reshape(D_FF, D_MODEL))
        if loss is None:
            out_ref[...] = h_out
        else:
            (gf_ref, t_ref), (loss_ref, dgf_ref) = loss_in, loss_out

            @pl.when(step == 0)
            def _():
                loss_ref[...] = jnp.zeros_like(loss_ref)
                dgf_ref[...] = jnp.zeros_like(dgf_ref)

            gf = gf_ref[...]
            r = _rstd(h_out)
            hr = h_out * r
            err = hr * gf - t_ref[...]
            loss_ref[...] += 0.5 * jnp.sum(jnp.mean(err * err, axis=-1, keepdims=True), axis=0, keepdims=True)
            dy = err * (1.0 / D_MODEL)
            dgf_ref[...] += jnp.sum(dy * hr, axis=0, keepdims=True)
            gd = gf * dy
            out_ref[...] = r * (gd - hr * jnp.mean(gd * hr, axis=-1, keepdims=True))

    row = pl.BlockSpec((tm, D_MODEL), lambda i: (i, 0))
    ff = pl.BlockSpec((tm, D_FF), lambda i: (i, 0))
    vec = pl.BlockSpec((1, D_MODEL), lambda i: (0, 0))
    res = pl.pallas_call(
        body,
        name=f"ffn_fwd_{tag}",
        grid=(nt,),
        in_specs=[row, vec, _ffn_w_in_spec(slot), _ffn_w_out_spec(slot)] + [vec, row][:nloss] + [ANY] * (2 * nbg),
        out_specs=[row, ff, ff] + [ANY] * nbg + [pl.BlockSpec((1, LANES), lambda i: (0, 0)), vec][:nloss],
        out_shape=[
            jax.ShapeDtypeStruct((T, D_MODEL), F32),
            jax.ShapeDtypeStruct((T, D_FF), BF16),
            jax.ShapeDtypeStruct((T, D_FF), BF16),
        ] + [jax.ShapeDtypeStruct(l.shape, l.dtype) for l in bg_lands]
        + [jax.ShapeDtypeStruct((1, LANES), F32), jax.ShapeDtypeStruct((1, D_MODEL), F32)][:nloss],
        input_output_aliases={4 + nloss + nbg + t: 3 + t for t in range(nbg)},
        scratch_shapes=[pltpu.VMEM((D_MODEL, D_FF), BF16), pltpu.VMEM((D_MODEL, D_FF), BF16)]
        + [pltpu.SemaphoreType.DMA((nbg, N_PEER_CHIPS))] * (4 if nbg else 0),
        compiler_params=_cparams(),
    )(h, g, w_in, w_out, *(loss or ()), *bg_shards, *bg_lands)
    if nbg:
        return res[0], res[1], res[2], list(res[3:3 + nbg])
    return tuple(res)


def ffn_bwd(tag, dh, h, g, gate, up, w_in, w_out, slot, proj=None):
    T = dh.shape[0]
    tm = FFN_ROWS
    nch = D_FF // FF_CHUNK

    def body(dh_ref, h_ref, g_ref, gate_ref, up_ref, wi_ref, wo_ref, *rest):
        if proj is not None:
            wp_ref, rest = rest[0], rest[1:]
        dhin_ref, xn_ref, dg_ref, du_ref, a_ref, dhb_ref, dnorm_ref = rest[:7]
        @pl.when(pl.program_id(0) == 0)
        def _():
            dnorm_ref[...] = jnp.zeros_like(dnorm_ref)

        dhh = dh_ref[...]
        dhb = (FFN_RES_SCALE * dhh).astype(BF16)
        dhb_ref[...] = dhb
        dxn = None
        for j in range(nch):
            cols = slice(j * FF_CHUNK, (j + 1) * FF_CHUNK)
            da = _dot_nt(dhb, wo_ref[2 * j:2 * j + 2].reshape(FF_CHUNK, D_MODEL))
            gt = gate_ref[:, cols].astype(F32)
            u = up_ref[:, cols].astype(F32)
            s = jax.nn.sigmoid(gt)
            silu = gt * s
            a_ref[:, cols] = (silu * u).astype(BF16)
            dgate = (da * u * (s * (1.0 + gt * (1.0 - s)))).astype(BF16)
            dup = (da * silu).astype(BF16)
            dg_ref[:, cols] = dgate
            du_ref[:, cols] = dup
            part = _dot_nt(dgate, wi_ref[j]) + _dot_nt(dup, wi_ref[nch + j])
            dxn = part if dxn is None else dxn + part
        hh = h_ref[...]
        gg = g_ref[...]
        r = _rstd(hh)
        hr = hh * r
        xn_ref[...] = (hr * gg).astype(BF16)
        dnorm_ref[...] += jnp.sum(dxn * hr, axis=0, keepdims=True)
        gd = gg * dxn
        dh_in = dhh + r * (gd - hr * jnp.mean(gd * hr, axis=-1, keepdims=True))
        dhin_ref[...] = dh_in
        if proj is not None:
            dyb_ref, da_ref = rest[7:9]
            dyb = dh_in.astype(BF16)
            dyb_ref[...] = dyb
            da_ref[...] = _dot_nt(dyb, wp_ref[...].reshape(D_MODEL, D_MODEL)).astype(BF16)

    row = pl.BlockSpec((tm, D_MODEL), lambda i: (i, 0))
    ff = pl.BlockSpec((tm, D_FF), lambda i: (i, 0))
    vec = pl.BlockSpec((1, D_MODEL), lambda i: (0, 0))
    in_specs = [row, row, vec, ff, ff, _ffn_w_in_spec(slot), _ffn_w_out_spec(slot)]
    args = [dh, h, g, gate, up, w_in, w_out]
    out_specs = [row, row, ff, ff, ff, row, vec]
    out_shape = [
        jax.ShapeDtypeStruct((T, D_MODEL), F32),
        jax.ShapeDtypeStruct((T, D_MODEL), BF16),
        jax.ShapeDtypeStruct((T, D_FF), BF16),
        jax.ShapeDtypeStruct((T, D_FF), BF16),
        jax.ShapeDtypeStruct((T, D_FF), BF16),
        jax.ShapeDtypeStruct((T, D_MODEL), BF16),
        jax.ShapeDtypeStruct((1, D_MODEL), F32),
    ]
    if proj is not None:
        w_sq, t = proj
        in_specs.append(pl.BlockSpec((N_CHIPS, None, SQ_ROWS, D_MODEL), lambda i: (0, t, 0, 0),
                                     pipeline_mode=pl.Buffered(1)))
        args.append(w_sq)
        out_specs += [row, row]
        out_shape += [jax.ShapeDtypeStruct((T, D_MODEL), BF16)] * 2
    return pl.pallas_call(
        body,
        name=f"ffn_bwd_{tag}",
        grid=(T // tm,),
        in_specs=in_specs,
        out_specs=out_specs,
        out_shape=out_shape,
        compiler_params=_cparams(),
    )(*args)


def rms_linear(name, h, g, w, w_spec, w_prep, n_out, tn, *, rope=None, rope_blocks=None, scale=None):
    T = h.shape[0]
    tm = 512
    extra, extra_specs = [], []
    if rope is not None:
        extra += list(rope)
        extra_specs += [pl.BlockSpec((tm, LANES), lambda i, j: (i, 0))] * 2
    if scale is not None:
        extra.append(scale)
        extra_specs.append(pl.BlockSpec((1, tn), lambda i, j: (0, j)))

    def body(h_ref, g_ref, w_ref, *rest):
        rest = list(rest)
        cos_ref = sin_ref = sc_ref = None
        if rope is not None:
            cos_ref, sin_ref = rest[0], rest[1]
            rest = rest[2:]
        if scale is not None:
            sc_ref = rest[0]
            rest = rest[1:]
        out_ref, xn_s = rest

        @pl.when(pl.program_id(1) == 0)
        def _():
            hh = h_ref[...]
            xn_s[...] = (hh * _rstd(hh) * g_ref[...]).astype(BF16)

        y = _dot(xn_s[...], w_prep(w_ref[...]))
        if rope is not None:
            turned = y * _tile_lanes(cos_ref[...], tn) + _swap32(y) * _tile_lanes(sin_ref[...], tn)
            y = turned if rope_blocks is None else jnp.where(pl.program_id(1) < rope_blocks, turned, y)
        if scale is not None:
            y = y * sc_ref[...]
        out_ref[...] = y.astype(BF16)

    return pl.pallas_call(
        body,
        name=name,
        grid=(T // tm, n_out // tn),
        in_specs=[
            pl.BlockSpec((tm, D_MODEL), lambda i, j: (i, 0)),
            pl.BlockSpec((1, D_MODEL), lambda i, j: (0, 0)),
            w_spec,
        ] + extra_specs,
        out_specs=pl.BlockSpec((tm, tn), lambda i, j: (i, j)),
        out_shape=jax.ShapeDtypeStruct((T, n_out), BF16),
        scratch_shapes=[pltpu.VMEM((tm, D_MODEL), BF16)],
        compiler_params=_cparams(),
    )(h, g, w, *extra)


QKV_ROWS = 512


def _qkv_w_spec():
    return pl.BlockSpec((N_CHIPS, D_MODEL, QKV_COLS), lambda i: (0, 0, 0), pipeline_mode=pl.Buffered(1))


def qkv_fwd(h, g, w_qkv, scale):
    T = h.shape[0]
    tm = QKV_ROWS

    def body(h_ref, g_ref, w_ref, sc_ref, out_ref):
        hh = h_ref[...]
        xn = (hh * _rstd(hh) * g_ref[...]).astype(BF16)
        for s in range(N_CHIPS):
            cols = slice(s * QKV_COLS, (s + 1) * QKV_COLS)
            out_ref[:, cols] = (_dot(xn, w_ref[s]) * sc_ref[:, cols]).astype(BF16)

    return pl.pallas_call(
        body,
        name="sb_qkv",
        grid=(T // tm,),
        in_specs=[
            pl.BlockSpec((tm, D_MODEL), lambda i: (i, 0)),
            pl.BlockSpec((1, D_MODEL), lambda i: (0, 0)),
            _qkv_w_spec(),
            pl.BlockSpec((1, 3 * D_MODEL), lambda i: (0, 0)),
        ],
        out_specs=pl.BlockSpec((tm, 3 * D_MODEL), lambda i: (i, 0)),
        out_shape=jax.ShapeDtypeStruct((T, 3 * D_MODEL), BF16),
        compiler_params=_cparams(),
    )(h, g, w_qkv, scale)


def qkv_bwd(dy, w_qkv, h, g, dres, bg_halves=()):
    T = h.shape[0]
    tm = QKV_ROWS
    nbg = len(bg_halves)
    nt = T // tm

    def body(dy_ref, w_ref, h_ref, g_ref, dres_ref, *rest):
        dh_ref, xn_ref, dg_ref = rest[nbg:nbg + 3]
        if nbg:
            def share():
                return sibling_share_copies(rest[:nbg], rest[nbg + 3:2 * nbg + 3], *rest[2 * nbg + 3:])

            @pl.when(pl.program_id(0) == 0)
            def _():
                for cp in share():
                    cp.start()

            @pl.when(pl.program_id(0) == nt - 1)
            def _():
                for cp in share():
                    cp.wait()

        @pl.when(pl.program_id(0) == 0)
        def _():
            dg_ref[...] = jnp.zeros_like(dg_ref)

        dxn = None
        for s in range(N_CHIPS):
            part = _dot_nt(dy_ref[:, s * QKV_COLS:(s + 1) * QKV_COLS], w_ref[s])
            dxn = part if dxn is None else dxn + part
        hh = h_ref[...]
        gg = g_ref[...]
        r = _rstd(hh)
        hr = hh * r
        xn_ref[...] = (hr * gg).astype(BF16)
        dg_ref[...] += jnp.sum(dxn * hr, axis=0, keepdims=True)
        gd = gg * dxn
        dh_ref[...] = dres_ref[...] + r * (gd - hr * jnp.mean(gd * hr, axis=-1, keepdims=True))

    row = pl.BlockSpec((tm, D_MODEL), lambda i: (i, 0))
    vec = pl.BlockSpec((1, D_MODEL), lambda i: (0, 0))
    res = pl.pallas_call(
        body,
        name="sb_qkv_bwd",
        grid=(nt,),
        in_specs=[pl.BlockSpec((tm, 3 * D_MODEL), lambda i: (i, 0)), _qkv_w_spec(), row, vec, row] + [ANY] * nbg,
        out_specs=[row, row, vec] + [ANY] * nbg,
        out_shape=[
            jax.ShapeDtypeStruct((T, D_MODEL), F32),
            jax.ShapeDtypeStruct((T, D_MODEL), BF16),
            jax.ShapeDtypeStruct((1, D_MODEL), F32),
        ] + [jax.ShapeDtypeStruct(b.shape, b.dtype) for b in bg_halves],
        scratch_shapes=[pltpu.SemaphoreType.DMA((nbg,))] * (2 if nbg else 0),
        compiler_params=_cparams(),
    )(dy, w_qkv, h, g, dres, *bg_halves)
    return res[0], res[1], res[2], list(res[3:])


def linear_res(name, a, w_sq, t, res, bg_lands=()):
    T = a.shape[0]
    tm = 512
    nbg = len(bg_lands)
    nt = T // tm

    def body(a_ref, w_ref, res_ref, *rest):
        out_ref = rest[nbg]
        if nbg:
            gather = GatherOps([l.shape[2] for l in bg_lands], None, rest[nbg + 1:2 * nbg + 1], None, None,
                               *rest[2 * nbg + 1:])

            @pl.when(pl.program_id(0) == 0)
            def _():
                gather.start_forwards()

        out_ref[...] = res_ref[...] + _dot(a_ref[...], w_ref[...].reshape(D_MODEL, D_MODEL))
        if nbg:
            @pl.when(pl.program_id(0) == nt - 1)
            def _():
                gather.wait_forwards()

    row = pl.BlockSpec((tm, D_MODEL), lambda i: (i, 0))
    res_ = pl.pallas_call(
        body,
        name=name,
        grid=(nt,),
        in_specs=[row, pl.BlockSpec((N_CHIPS, None, SQ_ROWS, D_MODEL), lambda i: (0, t, 0, 0)), row] + [ANY] * nbg,
        out_specs=[row] + [ANY] * nbg,
        out_shape=[jax.ShapeDtypeStruct((T, D_MODEL), F32)] + [jax.ShapeDtypeStruct(l.shape, l.dtype) for l in bg_lands],
        input_output_aliases={3 + k: 1 + k for k in range(nbg)},
        scratch_shapes=[pltpu.SemaphoreType.DMA((nbg, N_PEER_CHIPS))] * (2 if nbg else 0),
        compiler_params=_cparams(),
    )(a, w_sq, res, *bg_lands)
    return (res_[0], list(res_[1:])) if nbg else res_[0]


def linear_bwd_rms(name, pairs, h, g, dres, nch, tn, tm=256):
    T = h.shape[0]
    npair = len(pairs)

    def body(*refs):
        dy_refs = refs[:npair]
        w_refs = refs[npair:2 * npair]
        h_ref, g_ref, dres_ref, dh_ref, xn_ref, dg_ref, acc_s = refs[2 * npair:]
        i = pl.program_id(0)
        j = pl.program_id(1)

        @pl.when(j == 0)
        def _():
            acc_s[...] = jnp.zeros_like(acc_s)

        @pl.when((i == 0) & (j == 0))
        def _():
            dg_ref[...] = jnp.zeros_like(dg_ref)

        part = None
        for p in range(npair):
            d = _dot_nt(dy_refs[p][...], pairs[p][3](w_refs[p][...]))
            part = d if part is None else part + d
        acc_s[...] += part

        @pl.when(j == nch - 1)
        def _():
            dxn = acc_s[...]
            hh = h_ref[...]
            gg = g_ref[...]
            r = _rstd(hh)
            hr = hh * r
            xn_ref[...] = (hr * gg).astype(BF16)
            dg_ref[...] += jnp.sum(dxn * hr, axis=0, keepdims=True)
            gd = gg * dxn
            dh_ref[...] = dres_ref[...] + r * (gd - hr * jnp.mean(gd * hr, axis=-1, keepdims=True))

    row = pl.BlockSpec((tm, D_MODEL), lambda i, j: (i, 0))
    vec = pl.BlockSpec((1, D_MODEL), lambda i, j: (0, 0))
    return pl.pallas_call(
        body,
        name=name,
        grid=(T // tm, nch),
        in_specs=[pl.BlockSpec((tm, tn), lambda i, j: (i, j))] * npair + [p[2] for p in pairs] + [row, vec, row],
        out_specs=[row, row, vec],
        out_shape=[
            jax.ShapeDtypeStruct((T, D_MODEL), F32),
            jax.ShapeDtypeStruct((T, D_MODEL), BF16),
            jax.ShapeDtypeStruct((1, D_MODEL), F32),
        ],
        scratch_shapes=[pltpu.VMEM((tm, D_MODEL), F32)],
        compiler_params=_cparams(),
    )(*[p[0] for p in pairs], *[p[1] for p in pairs], h, g, dres)


DW_TOKENS = 4096
DW_TOKENS_BOTH = 2048


def mm_tn(name, a, b, tk, tn, out_block, out_index, out_shape, prev=None, tt=DW_TOKENS, bg_parts=(),
          halves=None):
    T = a.shape[0]
    ns, r = out_block[1], out_block[3]
    tt = min(tt, T)
    nt = T // tt
    nbg = len(bg_parts)
    n_in = 2 + (2 if prev is not None else 0)
    grid = (a.shape[1] // tk, b.shape[1] // tn, nt)

    def body(*refs):
        a_ref, b_ref = refs[:2]
        out_ref, copy_ref = refs[n_in + nbg:n_in + nbg + 2]
        t = pl.program_id(2)
        if nbg:
            def exchange():
                return chip_partial_copies(refs[n_in:n_in + nbg], refs[n_in + nbg + 2:n_in + 2 * nbg + 2],
                                           *refs[n_in + 2 * nbg + 2:])

            step = (pl.program_id(0) * grid[1] + pl.program_id(1)) * nt + t

            @pl.when(step == 0)
            def _():
                for cp in exchange():
                    cp.start()

            @pl.when(step == grid[0] * grid[1] * nt - 1)
            def _():
                for cp in exchange():
                    cp.wait()

        res = _dot_tn(a_ref[...], b_ref[...])
        if halves is None:
            pieces = [(u, res[u * r:(u + 1) * r]) for u in range(ns)]
        elif halves == "rows":
            pieces = [((hf, u), res[(hf * ns + u) * r:(hf * ns + u + 1) * r]) for hf in range(2) for u in range(ns)]
        else:
            c = out_block[4]
            pieces = [((hf, u), res[u * r:(u + 1) * r, hf * c:(hf + 1) * c]) for hf in range(2) for u in range(ns)]

        @pl.when(t == 0)
        def _():
            for at, piece in pieces:
                out_ref[at] = piece

        @pl.when(t > 0)
        def _():
            for at, piece in pieces:
                out_ref[at] += piece

        @pl.when(t == nt - 1)
        def _():
            copy_ref[...] = out_ref[...].astype(BF16)

    in_specs = [
        pl.BlockSpec((tt, tk), lambda k, n, t: (t, k)),
        pl.BlockSpec((tt, tn), lambda k, n, t: (t, n)),
    ]
    args = [a, b]
    aliases = {}
    if prev is not None:
        in_specs += [pl.BlockSpec(memory_space=pl.ANY)] * 2
        args += list(prev)
        aliases = {2: 0, 3: 1}
    out_spec = pl.BlockSpec(out_block, lambda k, n, t: out_index(k, n))
    return tuple(pl.pallas_call(
        body,
        name=name,
        grid=grid,
        in_specs=in_specs + [ANY] * nbg,
        out_specs=[out_spec, out_spec] + [ANY] * nbg,
        out_shape=[jax.ShapeDtypeStruct(out_shape, F32), jax.ShapeDtypeStruct(out_shape, BF16)]
        + [jax.ShapeDtypeStruct(p.shape, p.dtype) for p in bg_parts],
        input_output_aliases=aliases,
        scratch_shapes=[pltpu.SemaphoreType.DMA((nbg, N_PEER_CHIPS))] * (2 if nbg else 0),
        compiler_params=_cparams(),
    )(*args, *bg_parts))


SB_BLOCK = 256
SB_QROWS = 256
SB_QROWS_BWD = 256
SB_UNDERFLOW_BITS = 140.0
SB_CHUNK = 128


LOG2E = 1.4426950408889634


def _softplus2(z2):
    sign = jnp.uint32(0x80000000)
    neg_abs = lax.bitcast_convert_type(lax.bitcast_convert_type(z2, jnp.uint32) | sign, F32)
    return jnp.log2(1.0 + jnp.exp2(neg_abs)) + jnp.maximum(z2, 0.0)


def _twice(x):
    return jnp.concatenate([x, x], axis=1)


def sb_fwd(qkv, bg_shards=(), bg_lands=()):
    T = qkv.shape[0]
    tq, tk = SB_QROWS, SB_BLOCK
    ratio = tq // tk
    npair = SB_HEADS // 2
    nbg = len(bg_shards)
    nq = T // tq

    def body(q_ref, k_ref, v_ref, *rest):
        bg_in = rest[:nbg]
        o_ref, tot_ref, first_ref = rest[2 * nbg:2 * nbg + 3]
        bg_out = rest[2 * nbg + 3:3 * nbg + 3]
        acc_s, c_s, z_s, w_s, kmax_s = rest[3 * nbg + 3:3 * nbg + 8]
        p = pl.program_id(0)
        i = pl.program_id(1)
        if nbg:
            gather = GatherOps([s.shape[1] for s in bg_shards], bg_in, bg_out, *rest[3 * nbg + 8:])

            @pl.when((p == 0) & (i == 0))
            def _():
                gather.start_ici()

        @pl.when(i == 0)
        def _():
            kmax_s[...] = jnp.max(jnp.abs(k_ref[...]), axis=0, keepdims=True).astype(F32)

        q = q_ref[...]
        lane = lax.broadcasted_iota(jnp.int32, (tq, LANES), 1)
        first = lane < HEAD_DIM
        zero = jnp.zeros_like(q)
        q_heads = (jnp.where(first, q, zero), jnp.where(first, zero, q))
        row = lax.broadcasted_iota(jnp.int32, (tq, tk), 0)
        col = lax.broadcasted_iota(jnp.int32, (tq, tk), 1)
        visible = [col + r * tk < row for r in range(ratio)]
        krow = lax.broadcasted_iota(jnp.int32, (tk, tk), 0)
        kcol = lax.broadcasted_iota(jnp.int32, (tk, tk), 1)
        from_s = (krow >= kcol).astype(BF16)
        acc_s[...] = jnp.zeros_like(acc_s)
        c_s[...] = jnp.zeros_like(c_s)

        def rows(j):
            return pl.ds(pl.multiple_of(j * tk, tk), tk)

        def logits(j):
            kb = k_ref[rows(j), :]
            for hd in range(2):
                z_s[hd] = _dot_nt(q_heads[hd], kb) * LOG2E

        def flush(j):
            vb = v_ref[rows(j), :]
            for hd in range(2):
                acc_s[hd] += _dot(w_s[hd], vb)

        def block(j, mask=None, flush_block=None):
            if flush_block is not None:
                flush(flush_block)
            chunks = [(hd, slice(r0, r0 + SB_CHUNK)) for hd in range(2) for r0 in range(0, tq, SB_CHUNK)]
            k_next = k_ref[rows(jnp.maximum(j - 1, 0)), :]
            es, sums = [], []
            for hd, rs in chunks:
                z2 = z_s[hd, rs, :]
                z_s[hd, rs, :] = _dot_nt(q_heads[hd][rs, :], k_next) * LOG2E
                if mask is not None:
                    z2 = jnp.where(mask if mask.ndim == 0 else mask[rs, :], z2, NEG_BIG)
                sp = _softplus2(z2)
                c = c_s[hd, rs, :]
                es.append(z2 + _twice(c))
                c_s[hd, rs, :] = c - jnp.sum(sp, axis=1, keepdims=True)
                sums.append(_dot(sp.astype(BF16), from_s))
            for (hd, rs), e, s in zip(chunks, es, sums):
                w_s[hd, rs, :] = jnp.exp2(e - s).astype(BF16)

        z_bound = [LOG2E * jnp.sum(jnp.abs(q_heads[hd].astype(F32)) * kmax_s[...], axis=1, keepdims=True)
                   for hd in range(2)]

        def more_keys_matter():
            top = jnp.maximum(c_s[0] + z_bound[0], c_s[1] + z_bound[1])
            return (jnp.max(top) >= -SB_UNDERFLOW_BITS).astype(jnp.int32)

        assert ratio == 1
        logits(i)
        block(i, visible[0])

        def trip(carry):
            trips, _ = carry
            j = i - 1 - trips
            block(j, flush_block=j + 1)
            return trips + 1, more_keys_matter()

        trips, _ = lax.while_loop(lambda carry: jnp.logical_and(carry[0] < i, carry[1] > 0), trip,
                                  (jnp.int32(0), jnp.int32(1)))
        first_walked = i - trips
        flush(first_walked)
        first_ref[p, i] = first_walked.astype(F32)
        o_ref[...] = jnp.where(first, acc_s[0], acc_s[1]).astype(BF16)
        tot_ref[...] = jnp.where(first, c_s[0], c_s[1])
        if nbg:
            @pl.when((p == npair - 1) & (i == nq - 1))
            def _():
                gather.wait_ici()

    res = pl.pallas_call(
        body,
        name="sb_fwd",
        grid=(npair, nq),
        in_specs=[
            pl.BlockSpec((tq, LANES), lambda p, i: (i, p)),
            pl.BlockSpec((T, LANES), lambda p, i: (0, npair + p)),
            pl.BlockSpec((T, LANES), lambda p, i: (0, 2 * npair + p)),
        ] + [ANY] * (2 * nbg),
        out_specs=[pl.BlockSpec((tq, LANES), lambda p, i: (i, p))] * 2 + [pl.BlockSpec(memory_space=pltpu.SMEM)]
        + [ANY] * nbg,
        out_shape=[jax.ShapeDtypeStruct((T, D_MODEL), BF16), jax.ShapeDtypeStruct((T, D_MODEL), F32),
                   jax.ShapeDtypeStruct((npair, nq), F32)]
        + [jax.ShapeDtypeStruct(l.shape, l.dtype) for l in bg_lands],
        input_output_aliases={3 + nbg + t: 3 + t for t in range(nbg)},
        scratch_shapes=[
            pltpu.VMEM((2, tq, LANES), F32), pltpu.VMEM((2, tq, LANES), F32),
            pltpu.VMEM((2, tq, tk), F32), pltpu.VMEM((2, tq, tk), BF16),
            pltpu.VMEM((1, LANES), F32),
        ] + [pltpu.SemaphoreType.DMA((nbg, N_PEER_CHIPS))] * (2 if nbg else 0),
        compiler_params=_cparams(),
    )(qkv, qkv, qkv, *bg_shards, *bg_lands)
    return res[0], res[1], res[2], list(res[3:])


def sb_bwd(qkv, do, tot, first_block, bg_parts=()):
    T = qkv.shape[0]
    tq, tk = SB_QROWS_BWD, SB_BLOCK
    ratio = tq // tk
    npair = SB_HEADS // 2
    nq = T // tq
    nk = T // tk
    nbg = len(bg_parts)
    assert SB_QROWS == SB_QROWS_BWD

    def body(first_ref, q_ref, k_ref, v_ref, do_ref, tot_ref, *rest):
        bg_in = rest[:nbg]
        dq_ref, dk_ref, dv_ref = rest[nbg:nbg + 3]
        bg_out = rest[nbg + 3:2 * nbg + 3]
        dkt_s, dvt_s, dq_s, rest_s, cg_s, z_s, da_s, dz_s, a_s = rest[2 * nbg + 3:2 * nbg + 12]
        i = pl.program_id(1)
        start = jnp.clip(first_ref[pl.program_id(0), i].astype(jnp.int32), 0, i)
        if nbg:
            @pl.when((pl.program_id(0) == 0) & (i == 0))
            def _():
                for cp in chip_partial_copies(bg_in, bg_out, *rest[2 * nbg + 12:]):
                    cp.start()

        @pl.when(i == 0)
        def _():
            dkt_s[...] = jnp.zeros_like(dkt_s)
            dvt_s[...] = jnp.zeros_like(dvt_s)

        q = q_ref[...]
        do_ = do_ref[...]
        tot_ = tot_ref[...]
        q_t = q.astype(F32).T.astype(BF16)
        do_t = do_.astype(F32).T.astype(BF16)
        lane = lax.broadcasted_iota(jnp.int32, (tq, LANES), 1)
        first = lane < HEAD_DIM
        zero = jnp.zeros_like(q)
        q_heads = (jnp.where(first, q, zero), jnp.where(first, zero, q))
        do_heads = (jnp.where(first, do_, zero), jnp.where(first, zero, do_))
        row = lax.broadcasted_iota(jnp.int32, (tq, tk), 0)
        col = lax.broadcasted_iota(jnp.int32, (tq, tk), 1)
        visible = [col + r * tk < row for r in range(ratio)]
        krow = lax.broadcasted_iota(jnp.int32, (tk, tk), 0)
        kcol = lax.broadcasted_iota(jnp.int32, (tk, tk), 1)
        before = (krow < kcol).astype(BF16)
        from_s = (krow >= kcol).astype(BF16)
        last = ratio * i + ratio - 1
        rest_s[0] = jnp.broadcast_to(tot_[:, 0:1], (tq, LANES))
        rest_s[1] = jnp.broadcast_to(tot_[:, HEAD_DIM:HEAD_DIM + 1], (tq, LANES))
        cg_s[...] = jnp.zeros_like(cg_s)
        dq_s[...] = jnp.zeros_like(dq_s)
        dz_s[...] = jnp.zeros_like(dz_s)
        a_s[...] = jnp.zeros_like(a_s)

        def rows(j):
            return pl.ds(pl.multiple_of(j * tk, tk), tk)

        def logits(j):
            kb = k_ref[rows(j), :]
            vb = v_ref[rows(j), :]
            for hd in range(2):
                z_s[hd] = _dot_nt(q_heads[hd], kb) * LOG2E
                da_s[hd] = _dot_nt(do_heads[hd], vb)

        def flush(j):
            kb = k_ref[rows(j), :]
            for hd in range(2):
                dims = slice(hd * HEAD_DIM, (hd + 1) * HEAD_DIM)
                dq_s[hd] += _dot(dz_s[hd], kb)
                dkt_s[j, dims, :] += _dot(q_t[dims, :], dz_s[hd])
                dvt_s[j, dims, :] += _dot(do_t[dims, :], a_s[hd])

        def block(j, mask=None):
            flush(jnp.maximum(j - 1, 0))
            chunks = [(hd, slice(r0, r0 + SB_CHUNK)) for hd in range(2) for r0 in range(0, tq, SB_CHUNK)]
            nxt = rows(jnp.minimum(j + 1, last))
            k_next = k_ref[nxt, :]
            v_next = v_ref[nxt, :]
            stage1 = []
            for hd, rs in chunks:
                z2 = z_s[hd, rs, :]
                z_s[hd, rs, :] = _dot_nt(q_heads[hd][rs, :], k_next) * LOG2E
                if mask is not None:
                    z2 = jnp.where(mask if mask.ndim == 0 else mask[rs, :], z2, NEG_BIG)
                sp = _softplus2(z2)
                rest = rest_s[hd, rs, :] + jnp.sum(sp, axis=1, keepdims=True)
                rest_s[hd, rs, :] = rest
                stage1.append((z2 + _twice(rest), z2 - sp, _dot(sp.astype(BF16), from_s)))
            stage2 = []
            for (hd, rs), (e, log2_beta, ahead) in zip(chunks, stage1):
                a = jnp.exp2(e - ahead)
                g = a * da_s[hd, rs, :]
                da_s[hd, rs, :] = _dot_nt(do_heads[hd][rs, :], v_next)
                cg = cg_s[hd, rs, :]
                a_s[hd, rs, :] = a.astype(BF16)
                cg_s[hd, rs, :] = cg + jnp.sum(g, axis=1, keepdims=True)
                stage2.append((g, g + _twice(cg), log2_beta, _dot(g.astype(BF16), before)))
            for (hd, rs), (g, g_from, log2_beta, g_before) in zip(chunks, stage2):
                dz_s[hd, rs, :] = (g - jnp.exp2(log2_beta) * (g_from + g_before)).astype(BF16)

        assert ratio == 1
        logits(start)

        @pl.loop(start, i)
        def _(j):
            block(j)

        block(i, visible[0])
        flush(last)
        dq_ref[...] = (jnp.where(first, dq_s[0], dq_s[1]) * ATTN_SCALE).astype(BF16)

        @pl.when(i == nq - 1)
        def _():
            @pl.loop(0, nk)
            def _(b):
                dk_ref[rows(b), :] = dkt_s[b].T.astype(BF16)
                dv_ref[rows(b), :] = dvt_s[b].T.astype(BF16)

        if nbg:
            @pl.when((pl.program_id(0) == npair - 1) & (i == nq - 1))
            def _():
                for cp in chip_partial_copies(bg_in, bg_out, *rest[2 * nbg + 12:]):
                    cp.wait()

    qblk = pl.BlockSpec((tq, LANES), lambda p, i: (i, p))
    full = pl.BlockSpec((T, LANES), lambda p, i: (0, p))
    res = pl.pallas_call(
        body,
        name="sb_bwd",
        grid=(npair, nq),
        in_specs=[
            pl.BlockSpec(memory_space=pltpu.SMEM),
            qblk,
            pl.BlockSpec((T, LANES), lambda p, i: (0, npair + p)),
            pl.BlockSpec((T, LANES), lambda p, i: (0, 2 * npair + p)),
            qblk, qblk,
        ] + [ANY] * nbg,
        out_specs=[qblk, full, full] + [ANY] * nbg,
        out_shape=[jax.ShapeDtypeStruct((T, D_MODEL), BF16)] * 3
        + [jax.ShapeDtypeStruct(b.shape, b.dtype) for b in bg_parts],
        scratch_shapes=[
            pltpu.VMEM((nk, LANES, tk), F32), pltpu.VMEM((nk, LANES, tk), F32),
            pltpu.VMEM((2, tq, LANES), F32), pltpu.VMEM((2, tq, LANES), F32), pltpu.VMEM((2, tq, LANES), F32),
            pltpu.VMEM((2, tq, tk), F32), pltpu.VMEM((2, tq, tk), F32),
            pltpu.VMEM((2, tq, tk), BF16), pltpu.VMEM((2, tq, tk), BF16),
        ] + [pltpu.SemaphoreType.DMA((nbg, N_PEER_CHIPS))] * (2 if nbg else 0),
        compiler_params=_cparams(),
    )(first_block, qkv, qkv, qkv, do, tot, *bg_parts)
    return res[0], res[1], res[2], list(res[3:])


def _swa_valid(n):
    qi = lax.broadcasted_iota(jnp.int32, (WINDOW, 2 * WINDOW), 0)
    ki = lax.broadcasted_iota(jnp.int32, (WINDOW, 2 * WINDOW), 1)
    diff = qi + WINDOW - ki
    return (diff >= 0) & (diff < WINDOW) & ((n > 0) | (ki >= WINDOW))


def _to_half(x, first, src, dst):
    keep = first if src == 0 else jnp.logical_not(first)
    x = jnp.where(keep, x, jnp.zeros_like(x))
    if src != dst:
        x = pltpu.roll(x.astype(F32), HEAD_DIM, 1).astype(BF16)
    return x


SWA_GROUP = SWA_Q_HEADS // SWA_KV_HEADS


def _swa_cols(h):
    return slice((h // 2) * LANES, (h // 2 + 1) * LANES)


def _swa_kv_pair(h):
    return (h // SWA_GROUP) // 2


def _swa_kv_half(h):
    return (h // SWA_GROUP) % 2


def _kv_band(prev_ref, cur_ref, pb):
    cols = slice(pb * LANES, (pb + 1) * LANES)
    return jnp.concatenate([prev_ref[:, cols], cur_ref[:, cols]], axis=0)


def _swa_specs(T):
    nb = T // WINDOW
    kv_w = SWA_KV_HEADS * HEAD_DIM
    qrow = pl.BlockSpec((WINDOW, D_MODEL), lambda n: (n, 0))
    kv = [pl.BlockSpec((WINDOW, kv_w), lambda n, col=col, back=back: (jnp.maximum(n - back, 0), col))
          for col in (0, 1) for back in (0, 1)]
    smem = pl.BlockSpec(memory_space=pltpu.SMEM)
    return nb, qrow, kv, smem


def swa_fwd(q, kv, sinks):
    T = q.shape[0]
    nb, qrow, kv_specs, smem = _swa_specs(T)

    def body(sink_ref, q_ref, kc_ref, kp_ref, vc_ref, vp_ref, o_ref, lse_ref):
        n = pl.program_id(0)
        lane = lax.broadcasted_iota(jnp.int32, (WINDOW, LANES), 1)
        first = lane < HEAD_DIM
        valid = _swa_valid(n)
        k2 = [_kv_band(kp_ref, kc_ref, pb) for pb in range(SWA_KV_HEADS // 2)]
        v2 = [_kv_band(vp_ref, vc_ref, pb) for pb in range(SWA_KV_HEADS // 2)]
        logits = [jnp.where(valid, _dot_nt(_to_half(q_ref[:, _swa_cols(h)], first, h % 2, _swa_kv_half(h)),
                                            k2[_swa_kv_pair(h)]), NEG_BIG) for h in range(SWA_Q_HEADS)]
        probs = []
        lse_acc = jnp.zeros((WINDOW, LANES), F32)
        for h, s in enumerate(logits):
            sink = sink_ref[h]
            m = jnp.maximum(jnp.max(s, axis=1, keepdims=True), sink)
            p = jnp.exp(s - m)
            den = jnp.sum(p, axis=1, keepdims=True) + jnp.exp(sink - m)
            probs.append((p / den).astype(BF16))
            lse_acc = jnp.where(lane == h, m + jnp.log(den), lse_acc)
        outs = []
        for h, p in enumerate(probs):
            o = _dot(p, v2[_swa_kv_pair(h)])
            outs.append(pltpu.roll(o, HEAD_DIM, 1) if h % 2 != _swa_kv_half(h) else o)
        for pair in range(SWA_Q_HEADS // 2):
            o_ref[:, _swa_cols(2 * pair)] = jnp.where(first, outs[2 * pair], outs[2 * pair + 1]).astype(BF16)
        lse_ref[...] = lse_acc

    return pl.pallas_call(
        body,
        name="swa_fwd",
        grid=(nb,),
        in_specs=[smem, qrow] + kv_specs,
        out_specs=[qrow, pl.BlockSpec((WINDOW, LANES), lambda n: (n, 0))],
        out_shape=[jax.ShapeDtypeStruct((T, D_MODEL), BF16), jax.ShapeDtypeStruct((T, LANES), F32)],
        compiler_params=_cparams(),
    )(sinks, q, kv, kv, kv, kv)


def swa_bwd(q, kv, sinks, do, o, lse, cos, sin):
    T = q.shape[0]
    nb, qrow, kv_specs, smem = _swa_specs(T)
    kv_w = SWA_KV_HEADS * HEAD_DIM

    def body(sink_ref, q_ref, kc_ref, kp_ref, vc_ref, vp_ref, do_ref, o_ref, lse_ref, cos_ref, sin_ref,
             dq_ref, own_ref, prv_ref, dsink_ref):
        n = pl.program_id(0)

        @pl.when(n == 0)
        def _():
            dsink_ref[...] = jnp.zeros_like(dsink_ref)

        lane = lax.broadcasted_iota(jnp.int32, (WINDOW, LANES), 1)
        lane1 = lax.broadcasted_iota(jnp.int32, (1, LANES), 1)
        first = lane < HEAD_DIM
        valid = _swa_valid(n)
        cos_ = cos_ref[...]
        sin_ = sin_ref[...]
        k2 = [_kv_band(kp_ref, kc_ref, pb) for pb in range(SWA_KV_HEADS // 2)]
        v2 = [_kv_band(vp_ref, vc_ref, pb) for pb in range(SWA_KV_HEADS // 2)]
        q_t = q_ref[...].astype(F32).T.astype(BF16)
        do_t = do_ref[...].astype(F32).T.astype(BF16)
        stage1 = []
        for h in range(SWA_Q_HEADS):
            a, b, pb = h % 2, _swa_kv_half(h), _swa_kv_pair(h)
            qh = _to_half(q_ref[:, _swa_cols(h)], first, a, b)
            doh = _to_half(do_ref[:, _swa_cols(h)], first, a, b)
            stage1.append((jnp.where(valid, _dot_nt(qh, k2[pb]), NEG_BIG), _dot_nt(doh, v2[pb])))
        deltas = []
        for pair in range(SWA_Q_HEADS // 2):
            prod = do_ref[:, _swa_cols(2 * pair)].astype(F32) * o_ref[:, _swa_cols(2 * pair)].astype(F32)
            deltas += [jnp.sum(jnp.where(first, prod, 0.0), axis=1, keepdims=True),
                       jnp.sum(jnp.where(first, 0.0, prod), axis=1, keepdims=True)]
        stage2 = []
        dsink = jnp.zeros((1, LANES), F32)
        for h, (s, dp) in enumerate(stage1):
            lse_h = lse_ref[:, h:h + 1]
            p = jnp.exp(s - lse_h)
            delta = deltas[h]
            p_sink = jnp.exp(sink_ref[h] - lse_h)
            dsink = dsink + jnp.where(lane1 == h, -jnp.sum(p_sink * delta, axis=0, keepdims=True), 0.0)
            stage2.append(((p * (dp - delta)).astype(BF16), p.astype(BF16)))
        dqs = []
        dk_t = [None] * SWA_KV_HEADS
        dv_t = [None] * SWA_KV_HEADS
        for h, (ds, pb16) in enumerate(stage2):
            kvh = h // SWA_GROUP
            dims = slice(h * HEAD_DIM, (h + 1) * HEAD_DIM)
            dq = _dot(ds, k2[_swa_kv_pair(h)])
            dqs.append(pltpu.roll(dq, HEAD_DIM, 1) if h % 2 != _swa_kv_half(h) else dq)
            dk_h = _dot(q_t[dims, :], ds)
            dv_h = _dot(do_t[dims, :], pb16)
            dk_t[kvh] = dk_h if dk_t[kvh] is None else dk_t[kvh] + dk_h
            dv_t[kvh] = dv_h if dv_t[kvh] is None else dv_t[kvh] + dv_h
        for pair in range(SWA_Q_HEADS // 2):
            dqp = jnp.where(first, dqs[2 * pair], dqs[2 * pair + 1])
            dq_ref[:, _swa_cols(2 * pair)] = ((dqp * cos_ + _swap32(dqp * sin_)) * ATTN_SCALE).astype(BF16)
        for pb in range(SWA_KV_HEADS // 2):
            dk2 = jnp.concatenate([dk_t[2 * pb], dk_t[2 * pb + 1]], axis=0).T
            dv2 = jnp.concatenate([dv_t[2 * pb], dv_t[2 * pb + 1]], axis=0).T
            kcols = slice(pb * LANES, (pb + 1) * LANES)
            vcols = slice(kv_w + pb * LANES, kv_w + (pb + 1) * LANES)
            prv_ref[:, kcols] = dk2[:WINDOW]
            own_ref[:, kcols] = dk2[WINDOW:]
            prv_ref[:, vcols] = dv2[:WINDOW]
            own_ref[:, vcols] = dv2[WINDOW:]
        dsink_ref[...] += dsink

    tab = pl.BlockSpec((WINDOW, LANES), lambda n: (n, 0))
    kvrow = pl.BlockSpec((WINDOW, 2 * kv_w), lambda n: (n, 0))
    return pl.pallas_call(
        body,
        name="swa_bwd",
        grid=(nb,),
        in_specs=[smem, qrow] + kv_specs + [qrow, qrow, tab, tab, tab],
        out_specs=[qrow, kvrow, kvrow, pl.BlockSpec((1, LANES), lambda n: (0, 0))],
        out_shape=[
            jax.ShapeDtypeStruct((T, D_MODEL), BF16),
            jax.ShapeDtypeStruct((T, 2 * kv_w), F32),
            jax.ShapeDtypeStruct((T, 2 * kv_w), F32),
            jax.ShapeDtypeStruct((1, LANES), F32),
        ],
        compiler_params=_cparams(),
    )(sinks, q, kv, kv, kv, kv, do, o, lse, cos, sin)


def kv_grad_combine(own, prv, cos, sin):
    T = own.shape[0]
    nb = T // WINDOW
    kv_w = SWA_KV_HEADS * HEAD_DIM

    def body(own_ref, nxt_ref, cos_ref, sin_ref, out_ref):
        n = pl.program_id(0)
        nxt = jnp.where(n + 1 < nb, nxt_ref[...], 0.0)
        tot = own_ref[...] + nxt
        dk = tot[:, :kv_w]
        c = _tile_lanes(cos_ref[...], kv_w)
        s = _tile_lanes(sin_ref[...], kv_w)
        out_ref[:, :kv_w] = (dk * c + _swap32(dk * s)).astype(BF16)
        out_ref[:, kv_w:] = tot[:, kv_w:].astype(BF16)

    tab = pl.BlockSpec((WINDOW, LANES), lambda n: (n, 0))
    kvrow = pl.BlockSpec((WINDOW, 2 * kv_w), lambda n: (n, 0))
    return pl.pallas_call(
        body,
        name="kv_grad_combine",
        grid=(nb,),
        in_specs=[kvrow, pl.BlockSpec((WINDOW, 2 * kv_w), lambda n: (jnp.minimum(n + 1, nb - 1), 0)), tab, tab],
        out_specs=kvrow,
        out_shape=jax.ShapeDtypeStruct((T, 2 * kv_w), BF16),
        compiler_params=_cparams(),
    )(own, prv, cos, sin)


ANY = pl.BlockSpec(memory_space=pl.ANY)


def _place():
    x, y, c = lax.axis_index("x"), lax.axis_index("y"), lax.axis_index("c")
    other_chips = [(1 - x, y), (x, 1 - y), (1 - x, 1 - y)]
    return x, y, c, 2 * x + y, other_chips


N_PEER_CHIPS = N_CHIPS - 1


class GatherOps:
    def __init__(self, rows, shards, lands, ici_send, ici_recv, d2d_send=None, d2d_recv=None):
        self.rows, self.shards, self.lands = rows, shards, lands
        self.ici_send, self.ici_recv, self.d2d_send, self.d2d_recv = ici_send, ici_recv, d2d_send, d2d_recv
        self.x, self.y, self.c, self.me, self.chips = _place()
        self.pairs = [(t, jdx) for t in range(len(rows)) for jdx in range(N_PEER_CHIPS)]

    def _half(self, ref, t, which):
        r = self.rows[t] // 2
        return ref.at[:, pl.ds(pl.multiple_of(which * r, 16), r), :]

    def _ici(self, t, jdx):
        px, py = self.chips[jdx]
        return pltpu.make_async_remote_copy(
            src_ref=self._half(self.shards[t], t, self.c), dst_ref=self._half(self.lands[t].at[self.me], t, self.c),
            send_sem=self.ici_send.at[t, jdx], recv_sem=self.ici_recv.at[t, jdx],
            device_id=(px, py, self.c), device_id_type=MESH)

    def _landed(self, t, jdx):
        px, py = self.chips[jdx]
        blk = self._half(self.lands[t].at[2 * px + py], t, self.c)
        return pltpu.make_async_remote_copy(
            src_ref=blk, dst_ref=blk, send_sem=self.ici_send.at[t, jdx], recv_sem=self.ici_recv.at[t, jdx],
            device_id=(px, py, self.c), device_id_type=MESH)

    def _d2d(self, t, jdx, which):
        px, py = self.chips[jdx]
        blk = self._half(self.lands[t].at[2 * px + py], t, which)
        return pltpu.make_async_remote_copy(
            src_ref=blk, dst_ref=blk, send_sem=self.d2d_send.at[t, jdx], recv_sem=self.d2d_recv.at[t, jdx],
            device_id=(self.x, self.y, 1 - self.c), device_id_type=MESH)

    def start_ici(self):
        for t, jdx in self.pairs:
            self._ici(t, jdx).start()

    def wait_ici(self):
        for t, jdx in self.pairs:
            self._landed(t, jdx).wait_recv()
        self.wait_ici_sends()

    def wait_ici_sends(self):
        for t, jdx in self.pairs:
            self._ici(t, jdx).wait_send()

    def forward_arrivals(self):
        for t, jdx in self.pairs:
            self._landed(t, jdx).wait_recv()
            self._d2d(t, jdx, self.c).start()

    def start_forwards(self):
        for t, jdx in self.pairs:
            self._d2d(t, jdx, self.c).start()

    def wait_forwards(self):
        for t, jdx in self.pairs:
            self._d2d(t, jdx, 1 - self.c).wait_recv()
            self._d2d(t, jdx, self.c).wait_send()


def all_gather_weights(shards, lands):
    n = len(shards)
    rows = [s.shape[1] for s in shards]

    def body(*refs):
        ins, outs = refs[:n], refs[2 * n:3 * n]
        ops = GatherOps(rows, ins, outs, *refs[3 * n:])
        ops.start_ici()
        ops.forward_arrivals()
        ops.wait_forwards()
        ops.wait_ici_sends()

    return pl.pallas_call(
        body,
        name="all_gather_weights",
        in_specs=[ANY] * (2 * n),
        out_specs=[ANY] * n,
        out_shape=[jax.ShapeDtypeStruct(l.shape, l.dtype) for l in lands],
        input_output_aliases={n + t: t for t in range(n)},
        scratch_shapes=[pltpu.SemaphoreType.DMA((n, N_PEER_CHIPS))] * 4,
    )(*shards, *lands)


def place_own_shard(name, shard, chip):
    nl, r, c = shard.shape

    def body(chip_ref, s_ref, o_ref):
        o_ref[...] = s_ref[...]

    return pl.pallas_call(
        body, name=name,
        grid_spec=pltpu.PrefetchScalarGridSpec(
            num_scalar_prefetch=1, grid=(nl,),
            in_specs=[pl.BlockSpec((None, r, c), lambda l, chip_ref: (l, 0, 0))],
            out_specs=pl.BlockSpec((None, None, r, c), lambda l, chip_ref: (chip_ref[0], l, 0, 0))),
        out_shape=jax.ShapeDtypeStruct((N_CHIPS,) + shard.shape, shard.dtype), compiler_params=_cparams(),
    )(chip, shard)


def exchange_halves(name, slabs):
    n = len(slabs)

    def body(*refs):
        ins, theirs = refs[:n], refs[n:2 * n]
        send_sems, recv_sems = refs[2 * n:]
        x, y, c, _, _ = _place()
        copies = []
        for t in range(n):
            cp = pltpu.make_async_remote_copy(
                src_ref=ins[t].at[1 - c], dst_ref=theirs[t], send_sem=send_sems.at[t],
                recv_sem=recv_sems.at[t], device_id=(x, y, 1 - c), device_id_type=MESH)
            cp.start()
            copies.append(cp)
        for cp in copies:
            cp.wait()

    return pl.pallas_call(
        body,
        name=name,
        in_specs=[ANY] * n,
        out_specs=[ANY] * n,
        out_shape=[jax.ShapeDtypeStruct(s.shape[1:], s.dtype) for s in slabs],
        scratch_shapes=[pltpu.SemaphoreType.DMA((n,)), pltpu.SemaphoreType.DMA((n,))],
    )(*slabs)


def chip_partial_copies(ins, outs, send_sems, recv_sems):
    _, _, c, me, chips = _place()
    return [pltpu.make_async_remote_copy(
        src_ref=ins[t].at[2 * px + py], dst_ref=outs[t].at[me], send_sem=send_sems.at[t, jdx],
        recv_sem=recv_sems.at[t, jdx], device_id=(px, py, c), device_id_type=MESH)
        for t in range(len(ins)) for jdx, (px, py) in enumerate(chips)]


def exchange_chip_partials(name, parts):
    n = len(parts)

    def body(*refs):
        copies = chip_partial_copies(refs[:n], refs[n:2 * n], *refs[2 * n:])
        for cp in copies:
            cp.start()
        for cp in copies:
            cp.wait()

    return pl.pallas_call(
        body,
        name=name,
        in_specs=[ANY] * n,
        out_specs=[ANY] * n,
        out_shape=[jax.ShapeDtypeStruct(p.shape, p.dtype) for p in parts],
        scratch_shapes=[pltpu.SemaphoreType.DMA((n, 3)), pltpu.SemaphoreType.DMA((n, 3))],
    )(*parts)


def sibling_share_copies(ins, outs, send_sems, recv_sems):
    x, y, c, _, _ = _place()
    return [pltpu.make_async_remote_copy(
        src_ref=ins[t], dst_ref=outs[t], send_sem=send_sems.at[t], recv_sem=recv_sems.at[t],
        device_id=(x, y, 1 - c), device_id_type=MESH) for t in range(len(ins))]


def share_reduced_halves(name, halves):
    n = len(halves)

    def body(*refs):
        copies = sibling_share_copies(refs[:n], refs[n:2 * n], *refs[2 * n:])
        for cp in copies:
            cp.start()
        for cp in copies:
            cp.wait()

    return pl.pallas_call(
        body,
        name=name,
        in_specs=[ANY] * n,
        out_specs=[ANY] * n,
        out_shape=[jax.ShapeDtypeStruct(h.shape, h.dtype) for h in halves],
        scratch_shapes=[pltpu.SemaphoreType.DMA((n,)), pltpu.SemaphoreType.DMA((n,))],
    )(*halves)


def _row_tile(r, c):
    tr = r
    while tr * c * 4 > (3 << 19) and tr % 16 == 0:
        tr //= 2
    return tr


def add_sibling(name, slab, theirs, core):
    _, ns, slots, r, c = slab.shape
    tr = _row_tile(r, c)

    def body(core_ref, a_ref, b_ref, o_ref):
        o_ref[...] = (a_ref[...] + b_ref[...]).astype(BF16)

    blk = pl.BlockSpec((None, None, tr, c), lambda s, l, i, core_ref: (s, l, i, 0))
    return pl.pallas_call(
        body, name=name,
        grid_spec=pltpu.PrefetchScalarGridSpec(
            num_scalar_prefetch=1, grid=(ns, slots, r // tr),
            in_specs=[pl.BlockSpec((None, None, None, tr, c), lambda s, l, i, core_ref: (core_ref[0], s, l, i, 0)), blk],
            out_specs=blk),
        out_shape=jax.ShapeDtypeStruct(theirs.shape, BF16), compiler_params=_cparams(),
    )(core, slab, theirs)


def sum_chips(name, recv, own, chip):
    _, slots, r, c = recv.shape
    tr = _row_tile(r, c)

    def body(chip_ref, r0, r1, r2, r3, own_ref, o_ref):
        me = chip_ref[0]
        mine = own_ref[...]
        terms = [jnp.where(me == s, mine, rr[...]).astype(F32) for s, rr in enumerate((r0, r1, r2, r3))]
        o_ref[...] = ((terms[0] + terms[1]) + terms[2]) + terms[3]

    def src(s):
        return pl.BlockSpec((None, None, tr, c),
                            lambda l, i, chip_ref: (jnp.where(chip_ref[0] == s, (s + 1) % N_CHIPS, s), l, i, 0))

    return pl.pallas_call(
        body, name=name,
        grid_spec=pltpu.PrefetchScalarGridSpec(
            num_scalar_prefetch=1, grid=(slots, r // tr),
            in_specs=[src(0), src(1), src(2), src(3),
                      pl.BlockSpec((None, None, tr, c), lambda l, i, chip_ref: (chip_ref[0], l, i, 0))],
            out_specs=pl.BlockSpec((None, tr, c), lambda l, i, chip_ref: (l, i, 0))),
        out_shape=jax.ShapeDtypeStruct((slots, r, c), F32), compiler_params=_cparams(),
    )(chip, recv, recv, recv, recv, own)


def _adamw_math(w, g, m, v):
    m = ADAM_B1 * m + (1.0 - ADAM_B1) * g
    v = ADAM_B2 * v + (1.0 - ADAM_B2) * (g * g)
    m_hat = m / (1.0 - ADAM_B1 ** ADAM_STEP)
    v_hat = v / (1.0 - ADAM_B2 ** ADAM_STEP)
    delta = -ADAM_LR * (m_hat / (jnp.sqrt(v_hat) + ADAM_EPS) + ADAM_WD * w)
    return delta, m, v


def adamw_shard(name, w, m, v, g_pairs, core, slots, row_halves):
    n = w.shape[0]
    assert n == len(g_pairs)
    _, r, c = g_pairs[0][0].shape
    tr = _row_tile(r, c)
    nr = r // tr

    def body(core_ref, w_ref, m_ref, v_ref, *rest):
        g_refs, (go_ref, d_ref, mo_ref, vo_ref) = rest[:2 * n], rest[2 * n:]
        mine = pl.program_id(1) == core_ref[0]
        g = jnp.where(mine, g_refs[0][...], g_refs[1][...])
        for l in range(1, n):
            g = jnp.where(pl.program_id(0) == l, jnp.where(mine, g_refs[2 * l][...], g_refs[2 * l + 1][...]), g)
        delta, mm, vv = _adamw_math(w_ref[...], g, m_ref[...], v_ref[...])
        go_ref[...] = g
        d_ref[...] = delta
        mo_ref[...] = mm
        vo_ref[...] = vv

    if row_halves:
        wspec = pl.BlockSpec((None, tr, c), lambda l, h, i, core_ref: (l, h * nr + i, 0))
    else:
        wspec = pl.BlockSpec((None, tr, c), lambda l, h, i, core_ref: (l, i, h))
    def gspec(slot):
        return pl.BlockSpec((None, tr, c), lambda l, h, i, core_ref: (slot, i, 0))

    shp = jax.ShapeDtypeStruct(w.shape, F32)
    return pl.pallas_call(
        body, name=name,
        grid_spec=pltpu.PrefetchScalarGridSpec(
            num_scalar_prefetch=1, grid=(n, 2, nr),
            in_specs=[wspec, wspec, wspec] + [gspec(s) for s in slots for _ in range(2)], out_specs=[wspec] * 4),
        out_shape=[shp] * 4, compiler_params=_cparams(),
    )(core, w, m, v, *[g for pair in g_pairs for g in pair])


SMALL_ROWS = 16


def small_allreduce_adamw(part, w, m, v):
    def body(p_ref, w_ref, m_ref, v_ref, g_ref, d_ref, mo_ref, vo_ref, buf, send_sems, recv_sems):
        x, y, c, _, _ = _place()
        me = 4 * x + 2 * y + c
        buf[me] = p_ref[...]
        copies = []
        for k in range(1, N_DEV):
            kx, ky, kc = (k >> 2) & 1, (k >> 1) & 1, k & 1
            peer = (x ^ kx, y ^ ky, c ^ kc)
            cp = pltpu.make_async_remote_copy(
                src_ref=p_ref, dst_ref=buf.at[me], send_sem=send_sems.at[k - 1],
                recv_sem=recv_sems.at[k - 1], device_id=peer, device_id_type=MESH)
            cp.start()
            copies.append(cp)
        for cp in copies:
            cp.wait()
        g = buf[0]
        for dev in range(1, N_DEV):
            g = g + buf[dev]
        delta, mm, vv = _adamw_math(w_ref[...], g, m_ref[...], v_ref[...])
        g_ref[...] = g
        d_ref[...] = delta
        mo_ref[...] = mm
        vo_ref[...] = vv

    vm = pl.BlockSpec(memory_space=pltpu.VMEM)
    shp = jax.ShapeDtypeStruct(part.shape, F32)
    return pl.pallas_call(
        body, name="small_allreduce_adamw",
        in_specs=[vm] * 4, out_specs=[vm] * 4, out_shape=[shp] * 4,
        scratch_shapes=[
            pltpu.VMEM((N_DEV,) + part.shape, F32),
            pltpu.SemaphoreType.DMA((N_DEV - 1,)), pltpu.SemaphoreType.DMA((N_DEV - 1,)),
        ],
    )(part, w, m, v)


def _rope_tables(T):
    half = HEAD_DIM // 2
    inv_freq = ROPE_THETA ** (-jnp.arange(half, dtype=F32) / half)
    ang = jnp.arange(T).astype(F32)[:, None] * inv_freq[None, :]
    cos = jnp.tile(jnp.cos(ang), (1, LANES // half))
    sin = jnp.tile(jnp.sin(ang), (1, LANES // half))
    lane = jnp.arange(LANES)
    sign = jnp.where((lane % HEAD_DIM) < half, -1.0, 1.0).astype(F32)
    return cos, sin * sign[None, :]


def _pack_small(ffn1, mix, ffn2, kvn, fin, sinks, loss_row):
    sink_row = jnp.pad(sinks.reshape(1, SWA_Q_HEADS), ((0, 0), (0, D_MODEL - SWA_Q_HEADS)))
    rows = jnp.concatenate([ffn1, mix, ffn2, kvn.reshape(1, -1), fin.reshape(1, -1), sink_row, loss_row], axis=0)
    return jnp.concatenate([rows, jnp.zeros((SMALL_ROWS - rows.shape[0], D_MODEL), F32)], axis=0)


def kernel(x, ffn1_norm, ffn1_w_in, ffn1_w_out, mix_norm, ffn2_norm, ffn2_w_in, ffn2_w_out, sb_w_qkv, sb_w_o, kv_norm, kv_w, swa_w_q, swa_sinks, swa_w_o, final_norm, loss_target, m_ffn1_norm, m_ffn1_w_in, m_ffn1_w_out, m_mix_norm, m_ffn2_norm, m_ffn2_w_in, m_ffn2_w_out, m_sb_w_qkv, m_sb_w_o, m_kv_norm, m_kv_w, m_swa_w_q, m_swa_sinks, m_swa_w_o, m_final_norm, v_ffn1_norm, v_ffn1_w_in, v_ffn1_w_out, v_mix_norm, v_ffn2_norm, v_ffn2_w_in, v_ffn2_w_out, v_sb_w_qkv, v_sb_w_o, v_kv_norm, v_kv_w, v_swa_w_q, v_swa_sinks, v_swa_w_o, v_final_norm):
    T = x.shape[1]
    kv_cols = SWA_KV_HEADS * HEAD_DIM
    x2 = x.reshape(T, D_MODEL)
    tgt = loss_target.reshape(T, D_MODEL)
    cos, sin = _rope_tables(T)

    w_in_l = jnp.concatenate([ffn1_w_in, ffn2_w_in], axis=0).astype(BF16)
    w_out_l = jnp.concatenate([ffn1_w_out, ffn2_w_out], axis=0).astype(BF16)
    sq_l = jnp.concatenate([sb_w_o, swa_w_q, swa_w_o], axis=0).astype(BF16)
    qkv_l = sb_w_qkv[0].astype(BF16)
    kvw_l = kv_w.astype(BF16)
    core = lax.axis_index("c").astype(jnp.int32).reshape(1)
    chip = (2 * lax.axis_index("x") + lax.axis_index("y")).astype(jnp.int32).reshape(1)
    early = [w_in_l[:1], w_out_l[:1]]
    mid = [sq_l, qkv_l[None]]
    late = [w_in_l[1:], w_out_l[1:], kvw_l[None]]
    early_lands = [place_own_shard(f"own_early_{t}", s, chip) for t, s in enumerate(early)]
    mid_lands = [place_own_shard(f"own_mid_{t}", s, chip) for t, s in enumerate(mid)]
    late_lands = [place_own_shard(f"own_late_{t}", s, chip) for t, s in enumerate(late)]
    w_in0, w_out0 = all_gather_weights(early, early_lands)

    def ffn_w(slot):
        return (w_in0, w_out0, 0) if slot == 0 else (w_in_r, w_out_r, slot - 1)

    def vec(a, i):
        return a[i].reshape(1, D_MODEL)

    ident = lambda w: w
    sq_prep = lambda w: w.reshape(D_MODEL, w.shape[-1])
    qscale = jnp.concatenate([jnp.full((1, D_MODEL), ATTN_SCALE, F32), jnp.ones((1, 2 * D_MODEL), F32)], axis=1)
    swa_scale = jnp.full((1, D_MODEL), ATTN_SCALE, F32)
    sinks = swa_sinks.reshape(SWA_Q_HEADS)

    h1, gate1, up1, (w_sq, w_qkv) = ffn_fwd("l0a", x2, vec(ffn1_norm, 0), *ffn_w(SLOT_FFN1[0]), mid, mid_lands)
    w_qkv = w_qkv.reshape(N_CHIPS, D_MODEL, QKV_COLS)
    qkv = qkv_fwd(h1, vec(mix_norm, 0), w_qkv, qscale)
    o_sb, tot, sb_first, late_lands = sb_fwd(qkv, late, late_lands)
    h2, (w_in_r, w_out_r, w_kv) = linear_res("sb_out", o_sb, w_sq, SQ_SB_O, h1, late_lands)
    w_kv = w_kv.reshape(D_MODEL, 2 * kv_cols)
    h3, gate2, up2 = ffn_fwd("l0b", h2, vec(ffn2_norm, 0), *ffn_w(SLOT_FFN2[0]))
    kvn = kv_norm.reshape(1, D_MODEL)
    kv_sw = rms_linear("kv_proj", h3, kvn, w_kv, pl.BlockSpec((D_MODEL, kv_cols), lambda i, j: (0, j)), ident,
                       2 * kv_cols, kv_cols, rope=(cos, sin), rope_blocks=1)
    h4, gate3, up3 = ffn_fwd("l1a", h3, vec(ffn1_norm, 1), *ffn_w(SLOT_FFN1[1]))
    q_sw = rms_linear("swa_q", h4, vec(mix_norm, 1), w_sq,
                      pl.BlockSpec((N_CHIPS, None, SQ_ROWS, 512), lambda i, j: (0, SQ_SWA_Q, 0, j)), sq_prep,
                      D_MODEL, 512, rope=(cos, sin), scale=swa_scale)
    o_sw, lse = swa_fwd(q_sw, kv_sw, sinks)
    h5 = linear_res("swa_out", o_sw, w_sq, SQ_SWA_O, h4)
    dh6, gate4, up4, loss_p, d_final = ffn_fwd("l1b", h5, vec(ffn2_norm, 1), *ffn_w(SLOT_FFN2[1]),
                                               loss=(final_norm.reshape(1, D_MODEL), tgt))

    slab = {}
    ffn_place = {SLOT_FFN1[0]: (0, 0, 1), SLOT_FFN1[1]: (1, 0, 3), SLOT_FFN2[0]: (1, 1, 3), SLOT_FFN2[1]: (1, 2, 3)}
    sq_place = {SQ_SB_O: (1, 0, 3), SQ_SWA_Q: (1, 1, 3), SQ_SWA_O: (1, 2, 3)}

    exchanged = {}

    def chip_parts(grp, kinds):
        theirs = exchange_halves(f"exchange_halves_{grp}{kinds[0]}", [slab[kind, grp][1] for kind in kinds])
        return [add_sibling(f"add_sibling_{kind}{grp}", slab[kind, grp][0], t, core) for kind, t in zip(kinds, theirs)]

    def ffn_grads(tag, dh, h_in, g, gate, up, slot, proj=None, early=()):
        dh_in, xn, dg_, du_, act, dhb, dnorm, *through_proj = ffn_bwd(tag, dh, h_in, g, gate, up, *ffn_w(slot),
                                                                      proj=proj)
        grp, s, ns = ffn_place[slot]
        in_shape = (2, N_CHIPS, ns, D_MODEL // 2, FF_CHUNK)
        out_shape = (2, N_CHIPS, ns, FF_ROWS, D_MODEL // 2)
        blk = (2, 1, None, D_MODEL // 2, FF_CHUNK)
        slab["in", grp] = mm_tn(f"dw_gate_{tag}", xn, dg_, D_MODEL, FF_CHUNK, blk,
                                lambda k, n: (0, n, s, 0, 0), in_shape, prev=slab.get(("in", grp)),
                                tt=DW_TOKENS_BOTH, halves="rows")
        slab["in", grp] = mm_tn(f"dw_up_{tag}", xn, du_, D_MODEL, FF_CHUNK, blk,
                                lambda k, n: (0, 2 + n, s, 0, 0), in_shape, prev=slab["in", grp],
                                tt=DW_TOKENS_BOTH, halves="rows")
        parts = chip_parts(grp, list(early)) if early else []
        res = mm_tn(f"dw_out_{tag}", act, dhb, FF_CHUNK, D_MODEL, (2, 2, None, FF_ROWS, D_MODEL // 2),
                    lambda k, n: (0, k, s, 0, 0), out_shape, prev=slab.get(("out", grp)), bg_parts=parts,
                    tt=DW_TOKENS_BOTH, halves="cols")
        slab["out", grp] = res[:2]
        exchanged.update({(kind, grp): (a, p) for kind, a, p in zip(early, res[2:], parts)})
        return (dh_in, dnorm, *through_proj)

    def sq_grad(tag, a, dyb, t):
        grp, s, ns = sq_place[t]
        slab["sq", grp] = mm_tn(f"dw_sq_{tag}", a, dyb, D_MODEL, D_MODEL // 2,
                                (None, N_CHIPS, None, SQ_ROWS, D_MODEL // 2),
                                lambda k, n: (n, 0, s, 0, 0), (2, N_CHIPS, ns, SQ_ROWS, D_MODEL // 2),
                                prev=slab.get(("sq", grp)))

    def reduce_group(grp, kinds, host=None, share_host=None):
        todo = [kind for kind in kinds if (kind, grp) not in exchanged]
        parts = chip_parts(grp, todo)
        arrived = host(parts) if host else exchange_chip_partials(f"exchange_chip_partials_{grp}", parts)
        exchanged.update({(kind, grp): (a, p) for kind, a, p in zip(todo, arrived, parts)})
        halves = [sum_chips(f"sum_chips_{kind}{grp}", *exchanged[kind, grp], chip) for kind in kinds]
        if share_host:
            sib_halves = share_host(halves)
        else:
            sib_halves = share_reduced_halves(f"share_reduced_halves_{grp}", halves)
        return {kind: pair for kind, pair in zip(kinds, zip(halves, sib_halves))}

    dh5, d_ffn2_1, dh5b, do_sw = ffn_grads("l1b", dh6, h5, vec(ffn2_norm, 1), gate4, up4, SLOT_FFN2[1],
                                           proj=(w_sq, SQ_SWA_O))
    sq_grad("swa_o", o_sw, dh5b, SQ_SWA_O)
    dq_sw, kv_own, kv_prev, d_sinks = swa_bwd(q_sw, kv_sw, sinks, do_sw, o_sw, lse, cos, sin)
    sq_w_spec = pl.BlockSpec((N_CHIPS, None, SQ_ROWS, D_MODEL), lambda i, j: (0, SQ_SWA_Q, 0, 0))
    dh4, hn4, d_mix_1 = linear_bwd_rms("swa_q_bwd", [(dq_sw, w_sq, sq_w_spec, sq_prep)], h4, vec(mix_norm, 1), dh5,
                                       1, D_MODEL)
    sq_grad("swa_q", hn4, dq_sw, SQ_SWA_Q)
    dh3a, d_ffn1_1 = ffn_grads("l1a", dh4, h3, vec(ffn1_norm, 1), gate3, up3, SLOT_FFN1[1])
    dkv = kv_grad_combine(kv_own, kv_prev, cos, sin)
    kv_w_spec = pl.BlockSpec((D_MODEL, 2 * kv_cols), lambda i, j: (0, 0))
    dh3, xn3, d_kvn = linear_bwd_rms("kv_bwd", [(dkv, w_kv, kv_w_spec, ident)], h3, kvn, dh3a, 1, 2 * kv_cols)
    slab["kv", 1] = mm_tn("dw_kv", xn3, dkv, D_MODEL, kv_cols, (None, N_CHIPS, None, SQ_ROWS, kv_cols),
                          lambda k, n: (n, 0, 0, 0, 0), (2, N_CHIPS, 1, SQ_ROWS, kv_cols))
    dh2, d_ffn2_0, dh2b, do_sb = ffn_grads("l0b", dh3, h2, vec(ffn2_norm, 0), gate2, up2, SLOT_FFN2[0],
                                           proj=(w_sq, SQ_SB_O))
    sq_grad("sb_o", o_sb, dh2b, SQ_SB_O)
    sb_grads = []

    def behind_sb_bwd(parts):
        dq_sb, dk_sb, dv_sb, arrived = sb_bwd(qkv, do_sb, tot, sb_first, parts)
        sb_grads.extend([dq_sb, dk_sb, dv_sb])
        return arrived

    qkv_grads = []

    def behind_qkv_bwd(halves):
        dy = jnp.concatenate(sb_grads, axis=1)
        dh, hn, d_mix, theirs = qkv_bwd(dy, w_qkv, h1, vec(mix_norm, 0), dh2, halves)
        qkv_grads.extend([dy, dh, hn, d_mix])
        return theirs

    red = {1: reduce_group(1, ["in", "out", "sq", "kv"], host=behind_sb_bwd, share_host=behind_qkv_bwd)}
    dqkv, dh1, hn1, d_mix_0 = qkv_grads
    slab["qkv", 0] = mm_tn("dw_qkv", hn1, dqkv, D_MODEL // 2, QKV_COLS, (None, 1, None, D_MODEL // 2, QKV_COLS),
                           lambda k, n: (k, n, 0, 0, 0), (2, N_CHIPS, 1, D_MODEL // 2, QKV_COLS))
    dx, d_ffn1_0 = ffn_grads("l0a", dh1, x2, vec(ffn1_norm, 0), gate1, up1, SLOT_FFN1[0], early=("in", "qkv"))
    red[0] = reduce_group(0, ["in", "out", "qkv"])

    def upd(name, w, m, v, kind, places, row_halves):
        shp = w.shape
        w3 = w.reshape((-1,) + shp[-2:])
        outs = adamw_shard(name, w3, m.reshape(w3.shape), v.reshape(w3.shape),
                           [red[grp][kind] for grp, _ in places], core, [s for _, s in places], row_halves)
        return [o.reshape(shp) for o in outs]

    ffn1_places = [ffn_place[s][:2] for s in SLOT_FFN1]
    ffn2_places = [ffn_place[s][:2] for s in SLOT_FFN2]
    r_ffn1_in = upd("adamw_ffn1_in", ffn1_w_in, m_ffn1_w_in, v_ffn1_w_in, "in", ffn1_places, True)
    r_ffn2_in = upd("adamw_ffn2_in", ffn2_w_in, m_ffn2_w_in, v_ffn2_w_in, "in", ffn2_places, True)
    r_ffn1_out = upd("adamw_ffn1_out", ffn1_w_out, m_ffn1_w_out, v_ffn1_w_out, "out", ffn1_places, False)
    r_ffn2_out = upd("adamw_ffn2_out", ffn2_w_out, m_ffn2_w_out, v_ffn2_w_out, "out", ffn2_places, False)
    r_qkv = upd("adamw_qkv", sb_w_qkv, m_sb_w_qkv, v_sb_w_qkv, "qkv", [(0, 0)], True)
    r_sb_o = upd("adamw_sb_o", sb_w_o, m_sb_w_o, v_sb_w_o, "sq", [sq_place[SQ_SB_O][:2]], False)
    r_swa_q = upd("adamw_swa_q", swa_w_q, m_swa_w_q, v_swa_w_q, "sq", [sq_place[SQ_SWA_Q][:2]], False)
    r_swa_o = upd("adamw_swa_o", swa_w_o, m_swa_w_o, v_swa_w_o, "sq", [sq_place[SQ_SWA_O][:2]], False)
    r_kv = upd("adamw_kv", kv_w, m_kv_w, v_kv_w, "kv", [(1, 0)], False)

    loss_row = jnp.pad(loss_p, ((0, 0), (0, D_MODEL - LANES)))
    d_sink_row = d_sinks[0, :SWA_Q_HEADS]
    part = _pack_small(jnp.concatenate([d_ffn1_0, d_ffn1_1], axis=0), jnp.concatenate([d_mix_0, d_mix_1], axis=0),
                       jnp.concatenate([d_ffn2_0, d_ffn2_1], axis=0), d_kvn, d_final, d_sink_row, loss_row)
    zrow = jnp.zeros((1, D_MODEL), F32)
    small = small_allreduce_adamw(
        part,
        _pack_small(ffn1_norm, mix_norm, ffn2_norm, kv_norm, final_norm, swa_sinks, zrow),
        _pack_small(m_ffn1_norm, m_mix_norm, m_ffn2_norm, m_kv_norm, m_final_norm, m_swa_sinks, zrow),
        _pack_small(v_ffn1_norm, v_mix_norm, v_ffn2_norm, v_kv_norm, v_final_norm, v_swa_sinks, zrow))

    def unpack(p):
        return dict(ffn1_norm=p[0:2], mix_norm=p[2:4], ffn2_norm=p[4:6], kv_norm=p[6], final_norm=p[7],
                    swa_sinks=p[8:9, :SWA_Q_HEADS])

    big = dict(ffn1_w_in=r_ffn1_in, ffn1_w_out=r_ffn1_out, ffn2_w_in=r_ffn2_in, ffn2_w_out=r_ffn2_out,
               sb_w_qkv=r_qkv, sb_w_o=r_sb_o, kv_w=r_kv, swa_w_q=r_swa_q, swa_w_o=r_swa_o)
    order = ["ffn1_norm", "ffn1_w_in", "ffn1_w_out", "mix_norm", "ffn2_norm", "ffn2_w_in", "ffn2_w_out",
             "sb_w_qkv", "sb_w_o", "kv_norm", "kv_w", "swa_w_q", "swa_sinks", "swa_w_o", "final_norm"]
    outs = []
    for kind in range(4):
        sm = unpack(small[kind])
        for nm in order:
            outs.append(big[nm][kind] if nm in big else sm[nm])
    loss = small[0][9, 0]
    return (loss, dx.reshape(x.shape), *outs)
```

```python
import jax
import jax.numpy as jnp
from jax import lax
from jax.experimental import pallas as pl
from jax.experimental.pallas import tpu as pltpu

F32 = jnp.float32
BF16 = jnp.bfloat16
MESH = pl.DeviceIdType.MESH

D_MODEL = 1024
D_FF = 2816
HEAD_DIM = 64
SB_HEADS = 16
SWA_Q_HEADS = 16
SWA_KV_HEADS = 4
WINDOW = 128
ROPE_THETA = 10000.0
RMS_EPS = 1e-6
FFN_RES_SCALE = 0.5
ATTN_SCALE = HEAD_DIM ** -0.5

ADAM_LR = 0.001
ADAM_B1 = 0.9
ADAM_B2 = 0.999
ADAM_EPS = 1e-08
ADAM_WD = 0.01
ADAM_STEP = 10

N_CHIPS = 4
N_DEV = 8
LANES = 128
FF_CHUNK = D_FF // 2
FF_ROWS = D_FF // N_CHIPS
SQ_ROWS = D_MODEL // N_CHIPS
QKV_COLS = 3 * D_MODEL // N_CHIPS
VMEM_LIMIT = 56 * 1024 * 1024
NEG_BIG = -1e30

SLOT_FFN1 = (0, 1)
SLOT_FFN2 = (2, 3)
SQ_SB_O, SQ_SWA_Q, SQ_SWA_O = 0, 1, 2


def _cparams():
    return pltpu.CompilerParams(vmem_limit_bytes=VMEM_LIMIT)


def _dot(a, b):
    return jnp.dot(a, b, preferred_element_type=F32)


def _dot_nt(a, b):
    return lax.dot_general(a, b, (((1,), (1,)), ((), ())), preferred_element_type=F32)


def _dot_tn(a, b):
    return lax.dot_general(a, b, (((0,), (0,)), ((), ())), preferred_element_type=F32)


def _rstd(h):
    return lax.rsqrt(jnp.mean(h * h, axis=-1, keepdims=True) + RMS_EPS)


def _swap32(x):
    n = x.shape[-1]
    lane = lax.broadcasted_iota(jnp.int32, x.shape, x.ndim - 1)
    first = (lane % HEAD_DIM) < (HEAD_DIM // 2)
    return jnp.where(first, pltpu.roll(x, n - HEAD_DIM // 2, x.ndim - 1), pltpu.roll(x, HEAD_DIM // 2, x.ndim - 1))


def _tile_lanes(t, n):
    return t if n == LANES else jnp.tile(t, (1, n // LANES))


FFN_ROWS = 256


def _ffn_w_in_spec(slot):
    return pl.BlockSpec((N_CHIPS, None, D_MODEL, FF_CHUNK), lambda i: (0, slot, 0, 0), pipeline_mode=pl.Buffered(1))


def _ffn_w_out_spec(slot):
    return pl.BlockSpec((N_CHIPS, None, FF_ROWS, D_MODEL), lambda i: (0, slot, 0, 0), pipeline_mode=pl.Buffered(1))


def ffn_fwd(tag, h, g, w_in, w_out, slot, bg_shards=(), bg_lands=(), loss=None):
    T = h.shape[0]
    tm = FFN_ROWS
    nch = D_FF // FF_CHUNK
    nbg = len(bg_shards)
    nt = T // tm
    nloss = 2 if loss is not None else 0

    def body(h_ref, g_ref, wi_ref, wo_ref, *rest):
        loss_in, rest = rest[:nloss], rest[nloss:]
        out_ref, gate_ref, up_ref = rest[2 * nbg:2 * nbg + 3]
        loss_out, rest = rest[3 * nbg + 3:3 * nbg + 3 + nloss], rest[:3 * nbg + 3] + rest[3 * nbg + 3 + nloss:]
        wg_s, wu_s = rest[3 * nbg + 3:3 * nbg + 5]
        step = pl.program_id(0)
        if nbg:
            gather = GatherOps([s.shape[1] for s in bg_shards], rest[:nbg], rest[2 * nbg + 3:3 * nbg + 3],
                               *rest[3 * nbg + 5:])
            pl.when(step == 0)(gather.start_ici)
            pl.when(step == nt // 2)(gather.forward_arrivals)

            @pl.when(step == nt - 1)
            def _():
                gather.wait_forwards()
                gather.wait_ici_sends()

        @pl.when(step == 0)
        def _():
            for j in range(nch):
                cols = slice(j * FF_CHUNK, (j + 1) * FF_CHUNK)
                wg_s[:, cols] = wi_ref[j]
                wu_s[:, cols] = wi_ref[nch + j]

        hh = h_ref[...]
        xn = (hh * _rstd(hh) * g_ref[...]).astype(BF16)
        gate = _dot(xn, wg_s[...])
        up = _dot(xn, wu_s[...])
        gate_ref[...] = gate.astype(BF16)
        up_ref[...] = up.astype(BF16)
        a = (gate * jax.nn.sigmoid(gate) * up).astype(BF16)
        h_out = hh + FFN_RES_SCALE * _dot(a, wo_ref[...].reshape(D_FF, D_MODEL))
        if loss is None:
            out_ref[...] = h_out
        else:
            (gf_ref, t_ref), (loss_ref, dgf_ref) = loss_in, loss_out

            @pl.when(step == 0)
            def _():
                loss_ref[...] = jnp.zeros_like(loss_ref)
                dgf_ref[...] = jnp.zeros_like(dgf_ref)

            gf = gf_ref[...]
            r = _rstd(h_out)
            hr = h_out * r
            err = hr * gf - t_ref[...]
            loss_ref[...] += 0.5 * jnp.sum(jnp.mean(err * err, axis=-1, keepdims=True), axis=0, keepdims=True)
            dy = err * (1.0 / D_MODEL)
            dgf_ref[...] += jnp.sum(dy * hr, axis=0, keepdims=True)
            gd = gf * dy
            out_ref[...] = r * (gd - hr * jnp.mean(gd * hr, axis=-1, keepdims=True))

    row = pl.BlockSpec((tm, D_MODEL), lambda i: (i, 0))
    ff = pl.BlockSpec((tm, D_FF), lambda i: (i, 0))
    vec = pl.BlockSpec((1, D_MODEL), lambda i: (0, 0))
    res = pl.pallas_call(
        body,
        name=f"ffn_fwd_{tag}",
        grid=(nt,),
        in_specs=[row, vec, _ffn_w_in_spec(slot), _ffn_w_out_spec(slot)] + [vec, row][:nloss] + [ANY] * (2 * nbg),
        out_specs=[row, ff, ff] + [ANY] * nbg + [pl.BlockSpec((1, LANES), lambda i: (0, 0)), vec][:nloss],
        out_shape=[
            jax.ShapeDtypeStruct((T, D_MODEL), F32),
            jax.ShapeDtypeStruct((T, D_FF), BF16),
            jax.ShapeDtypeStruct((T, D_FF), BF16),
        ] + [jax.ShapeDtypeStruct(l.shape, l.dtype) for l in bg_lands]
        + [jax.ShapeDtypeStruct((1, LANES), F32), jax.ShapeDtypeStruct((1, D_MODEL), F32)][:nloss],
        input_output_aliases={4 + nloss + nbg + t: 3 + t for t in range(nbg)},
        scratch_shapes=[pltpu.VMEM((D_MODEL, D_FF), BF16), pltpu.VMEM((D_MODEL, D_FF), BF16)]
        + [pltpu.SemaphoreType.DMA((nbg, N_PEER_CHIPS))] * (4 if nbg else 0),
        compiler_params=_cparams(),
    )(h, g, w_in, w_out, *(loss or ()), *bg_shards, *bg_lands)
    if nbg:
        return res[0], res[1], res[2], list(res[3:3 + nbg])
    return tuple(res)


def ffn_bwd(tag, dh, h, g, gate, up, w_in, w_out, slot, proj=None):
    T = dh.shape[0]
    tm = FFN_ROWS
    nch = D_FF // FF_CHUNK

    def body(dh_ref, h_ref, g_ref, gate_ref, up_ref, wi_ref, wo_ref, *rest):
        if proj is not None:
            wp_ref, rest = rest[0], rest[1:]
        dhin_ref, xn_ref, dg_ref, du_ref, a_ref, dhb_ref, dnorm_ref = rest[:7]
        @pl.when(pl.program_id(0) == 0)
        def _():
            dnorm_ref[...] = jnp.zeros_like(dnorm_ref)

        dhh = dh_ref[...]
        dhb = (FFN_RES_SCALE * dhh).astype(BF16)
        dhb_ref[...] = dhb
        dxn = None
        for j in range(nch):
            cols = slice(j * FF_CHUNK, (j + 1) * FF_CHUNK)
            da = _dot_nt(dhb, wo_ref[2 * j:2 * j + 2].reshape(FF_CHUNK, D_MODEL))
            gt = gate_ref[:, cols].astype(F32)
            u = up_ref[:, cols].astype(F32)
            s = jax.nn.sigmoid(gt)
            silu = gt * s
            a_ref[:, cols] = (silu * u).astype(BF16)
            dgate = (da * u * (s * (1.0 + gt * (1.0 - s)))).astype(BF16)
            dup = (da * silu).astype(BF16)
            dg_ref[:, cols] = dgate
            du_ref[:, cols] = dup
            part = _dot_nt(dgate, wi_ref[j]) + _dot_nt(dup, wi_ref[nch + j])
            dxn = part if dxn is None else dxn + part
        hh = h_ref[...]
        gg = g_ref[...]
        r = _rstd(hh)
        hr = hh * r
        xn_ref[...] = (hr * gg).astype(BF16)
        dnorm_ref[...] += jnp.sum(dxn * hr, axis=0, keepdims=True)
        gd = gg * dxn
        dh_in = dhh + r * (gd - hr * jnp.mean(gd * hr, axis=-1, keepdims=True))
        dhin_ref[...] = dh_in
        if proj is not None:
            dyb_ref, da_ref = rest[7:9]
            dyb = dh_in.astype(BF16)
            dyb_ref[...] = dyb
            da_ref[...] = _dot_nt(dyb, wp_ref[...].reshape(D_MODEL, D_MODEL)).astype(BF16)

    row = pl.BlockSpec((tm, D_MODEL), lambda i: (i, 0))
    ff = pl.BlockSpec((tm, D_FF), lambda i: (i, 0))
    vec = pl.BlockSpec((1, D_MODEL), lambda i: (0, 0))
    in_specs = [row, row, vec, ff, ff, _ffn_w_in_spec(slot), _ffn_w_out_spec(slot)]
    args = [dh, h, g, gate, up, w_in, w_out]
    out_specs = [row, row, ff, ff, ff, row, vec]
    out_shape = [
        jax.ShapeDtypeStruct((T, D_MODEL), F32),
        jax.ShapeDtypeStruct((T, D_MODEL), BF16),
        jax.ShapeDtypeStruct((T, D_FF), BF16),
        jax.ShapeDtypeStruct((T, D_FF), BF16),
        jax.ShapeDtypeStruct((T, D_FF), BF16),
        jax.ShapeDtypeStruct((T, D_MODEL), BF16),
        jax.ShapeDtypeStruct((1, D_MODEL), F32),
    ]
    if proj is not None:
        w_sq, t = proj
        in_specs.append(pl.BlockSpec((N_CHIPS, None, SQ_ROWS, D_MODEL), lambda i: (0, t, 0, 0),
                                     pipeline_mode=pl.Buffered(1)))
        args.append(w_sq)
        out_specs += [row, row]
        out_shape += [jax.ShapeDtypeStruct((T, D_MODEL), BF16)] * 2
    return pl.pallas_call(
        body,
        name=f"ffn_bwd_{tag}",
        grid=(T // tm,),
        in_specs=in_specs,
        out_specs=out_specs,
        out_shape=out_shape,
        compiler_params=_cparams(),
    )(*args)


def rms_linear(name, h, g, w, w_spec, w_prep, n_out, tn, *, rope=None, rope_blocks=None, scale=None):
    T = h.shape[0]
    tm = 512
    extra, extra_specs = [], []
    if rope is not None:
        extra += list(rope)
        extra_specs += [pl.BlockSpec((tm, LANES), lambda i, j: (i, 0))] * 2
    if scale is not None:
        extra.append(scale)
        extra_specs.append(pl.BlockSpec((1, tn), lambda i, j: (0, j)))

    def body(h_ref, g_ref, w_ref, *rest):
        rest = list(rest)
        cos_ref = sin_ref = sc_ref = None
        if rope is not None:
            cos_ref, sin_ref = rest[0], rest[1]
            rest = rest[2:]
        if scale is not None:
            sc_ref = rest[0]
            rest = rest[1:]
        out_ref, xn_s = rest

        @pl.when(pl.program_id(1) == 0)
        def _():
            hh = h_ref[...]
            xn_s[...] = (hh * _rstd(hh) * g_ref[...]).astype(BF16)

        y = _dot(xn_s[...], w_prep(w_ref[...]))
        if rope is not None:
            turned = y * _tile_lanes(cos_ref[...], tn) + _swap32(y) * _tile_lanes(sin_ref[...], tn)
            y = turned if rope_blocks is None else jnp.where(pl.program_id(1) < rope_blocks, turned, y)
        if scale is not None:
            y = y * sc_ref[...]
        out_ref[...] = y.astype(BF16)

    return pl.pallas_call(
        body,
        name=name,
        grid=(T // tm, n_out // tn),
        in_specs=[
            pl.BlockSpec((tm, D_MODEL), lambda i, j: (i, 0)),
            pl.BlockSpec((1, D_MODEL), lambda i, j: (0, 0)),
            w_spec,
        ] + extra_specs,
        out_specs=pl.BlockSpec((tm, tn), lambda i, j: (i, j)),
        out_shape=jax.ShapeDtypeStruct((T, n_out), BF16),
        scratch_shapes=[pltpu.VMEM((tm, D_MODEL), BF16)],
        compiler_params=_cparams(),
    )(h, g, w, *extra)


QKV_ROWS = 512


def _qkv_w_spec():
    return pl.BlockSpec((N_CHIPS, D_MODEL, QKV_COLS), lambda i: (0, 0, 0), pipeline_mode=pl.Buffered(1))


def qkv_fwd(h, g, w_qkv, scale):
    T = h.shape[0]
    tm = QKV_ROWS

    def body(h_ref, g_ref, w_ref, sc_ref, out_ref):
        hh = h_ref[...]
        xn = (hh * _rstd(hh) * g_ref[...]).astype(BF16)
        for s in range(N_CHIPS):
            cols = slice(s * QKV_COLS, (s + 1) * QKV_COLS)
            out_ref[:, cols] = (_dot(xn, w_ref[s]) * sc_ref[:, cols]).astype(BF16)

    return pl.pallas_call(
        body,
        name="sb_qkv",
        grid=(T // tm,),
        in_specs=[
            pl.BlockSpec((tm, D_MODEL), lambda i: (i, 0)),
            pl.BlockSpec((1, D_MODEL), lambda i: (0, 0)),
            _qkv_w_spec(),
            pl.BlockSpec((1, 3 * D_MODEL), lambda i: (0, 0)),
        ],
        out_specs=pl.BlockSpec((tm, 3 * D_MODEL), lambda i: (i, 0)),
        out_shape=jax.ShapeDtypeStruct((T, 3 * D_MODEL), BF16),
        compiler_params=_cparams(),
    )(h, g, w_qkv, scale)


def qkv_bwd(dy, w_qkv, h, g, dres, bg_halves=()):
    T = h.shape[0]
    tm = QKV_ROWS
    nbg = len(bg_halves)
    nt = T // tm

    def body(dy_ref, w_ref, h_ref, g_ref, dres_ref, *rest):
        dh_ref, xn_ref, dg_ref = rest[nbg:nbg + 3]
        if nbg:
            def share():
                return sibling_share_copies(rest[:nbg], rest[nbg + 3:2 * nbg + 3], *rest[2 * nbg + 3:])

            @pl.when(pl.program_id(0) == 0)
            def _():
                for cp in share():
                    cp.start()

            @pl.when(pl.program_id(0) == nt - 1)
            def _():
                for cp in share():
                    cp.wait()

        @pl.when(pl.program_id(0) == 0)
        def _():
            dg_ref[...] = jnp.zeros_like(dg_ref)

        dxn = None
        for s in range(N_CHIPS):
            part = _dot_nt(dy_ref[:, s * QKV_COLS:(s + 1) * QKV_COLS], w_ref[s])
            dxn = part if dxn is None else dxn + part
        hh = h_ref[...]
        gg = g_ref[...]
        r = _rstd(hh)
        hr = hh * r
        xn_ref[...] = (hr * gg).astype(BF16)
        dg_ref[...] += jnp.sum(dxn * hr, axis=0, keepdims=True)
        gd = gg * dxn
        dh_ref[...] = dres_ref[...] + r * (gd - hr * jnp.mean(gd * hr, axis=-1, keepdims=True))

    row = pl.BlockSpec((tm, D_MODEL), lambda i: (i, 0))
    vec = pl.BlockSpec((1, D_MODEL), lambda i: (0, 0))
    res = pl.pallas_call(
        body,
        name="sb_qkv_bwd",
        grid=(nt,),
        in_specs=[pl.BlockSpec((tm, 3 * D_MODEL), lambda i: (i, 0)), _qkv_w_spec(), row, vec, row] + [ANY] * nbg,
        out_specs=[row, row, vec] + [ANY] * nbg,
        out_shape=[
            jax.ShapeDtypeStruct((T, D_MODEL), F32),
            jax.ShapeDtypeStruct((T, D_MODEL), BF16),
            jax.ShapeDtypeStruct((1, D_MODEL), F32),
        ] + [jax.ShapeDtypeStruct(b.shape, b.dtype) for b in bg_halves],
        scratch_shapes=[pltpu.SemaphoreType.DMA((nbg,))] * (2 if nbg else 0),
        compiler_params=_cparams(),
    )(dy, w_qkv, h, g, dres, *bg_halves)
    return res[0], res[1], res[2], list(res[3:])


def linear_res(name, a, w_sq, t, res, bg_lands=()):
    T = a.shape[0]
    tm = 512
    nbg = len(bg_lands)
    nt = T // tm

    def body(a_ref, w_ref, res_ref, *rest):
        out_ref = rest[nbg]
        if nbg:
            gather = GatherOps([l.shape[2] for l in bg_lands], None, rest[nbg + 1:2 * nbg + 1], None, None,
                               *rest[2 * nbg + 1:])

            @pl.when(pl.program_id(0) == 0)
            def _():
                gather.start_forwards()

        out_ref[...] = res_ref[...] + _dot(a_ref[...], w_ref[...].reshape(D_MODEL, D_MODEL))
        if nbg:
            @pl.when(pl.program_id(0) == nt - 1)
            def _():
                gather.wait_forwards()

    row = pl.BlockSpec((tm, D_MODEL), lambda i: (i, 0))
    res_ = pl.pallas_call(
        body,
        name=name,
        grid=(nt,),
        in_specs=[row, pl.BlockSpec((N_CHIPS, None, SQ_ROWS, D_MODEL), lambda i: (0, t, 0, 0)), row] + [ANY] * nbg,
        out_specs=[row] + [ANY] * nbg,
        out_shape=[jax.ShapeDtypeStruct((T, D_MODEL), F32)] + [jax.ShapeDtypeStruct(l.shape, l.dtype) for l in bg_lands],
        input_output_aliases={3 + k: 1 + k for k in range(nbg)},
        scratch_shapes=[pltpu.SemaphoreType.DMA((nbg, N_PEER_CHIPS))] * (2 if nbg else 0),
        compiler_params=_cparams(),
    )(a, w_sq, res, *bg_lands)
    return (res_[0], list(res_[1:])) if nbg else res_[0]


def linear_bwd_rms(name, pairs, h, g, dres, nch, tn, tm=256):
    T = h.shape[0]
    npair = len(pairs)

    def body(*refs):
        dy_refs = refs[:npair]
        w_refs = refs[npair:2 * npair]
        h_ref, g_ref, dres_ref, dh_ref, xn_ref, dg_ref, acc_s = refs[2 * npair:]
        i = pl.program_id(0)
        j = pl.program_id(1)

        @pl.when(j == 0)
        def _():
            acc_s[...] = jnp.zeros_like(acc_s)

        @pl.when((i == 0) & (j == 0))
        def _():
            dg_ref[...] = jnp.zeros_like(dg_ref)

        part = None
        for p in range(npair):
            d = _dot_nt(dy_refs[p][...], pairs[p][3](w_refs[p][...]))
            part = d if part is None else part + d
        acc_s[...] += part

        @pl.when(j == nch - 1)
        def _():
            dxn = acc_s[...]
            hh = h_ref[...]
            gg = g_ref[...]
            r = _rstd(hh)
            hr = hh * r
            xn_ref[...] = (hr * gg).astype(BF16)
            dg_ref[...] += jnp.sum(dxn * hr, axis=0, keepdims=True)
            gd = gg * dxn
            dh_ref[...] = dres_ref[...] + r * (gd - hr * jnp.mean(gd * hr, axis=-1, keepdims=True))

    row = pl.BlockSpec((tm, D_MODEL), lambda i, j: (i, 0))
    vec = pl.BlockSpec((1, D_MODEL), lambda i, j: (0, 0))
    return pl.pallas_call(
        body,
        name=name,
        grid=(T // tm, nch),
        in_specs=[pl.BlockSpec((tm, tn), lambda i, j: (i, j))] * npair + [p[2] for p in pairs] + [row, vec, row],
        out_specs=[row, row, vec],
        out_shape=[
            jax.ShapeDtypeStruct((T, D_MODEL), F32),
            jax.ShapeDtypeStruct((T, D_MODEL), BF16),
            jax.ShapeDtypeStruct((1, D_MODEL), F32),
        ],
        scratch_shapes=[pltpu.VMEM((tm, D_MODEL), F32)],
        compiler_params=_cparams(),
    )(*[p[0] for p in pairs], *[p[1] for p in pairs], h, g, dres)


DW_TOKENS = 4096


def mm_tn(name, a, b, tk, tn, out_block, out_index, out_shape, prev=None, tt=DW_TOKENS, bg_parts=()):
    T = a.shape[0]
    ns, r = out_block[1], out_block[3]
    tt = min(tt, T)
    nt = T // tt
    nbg = len(bg_parts)
    n_in = 2 + (2 if prev is not None else 0)
    grid = (a.shape[1] // tk, b.shape[1] // tn, nt)

    def body(*refs):
        a_ref, b_ref = refs[:2]
        out_ref, copy_ref = refs[n_in + nbg:n_in + nbg + 2]
        t = pl.program_id(2)
        if nbg:
            def exchange():
                return chip_partial_copies(refs[n_in:n_in + nbg], refs[n_in + nbg + 2:n_in + 2 * nbg + 2],
                                           *refs[n_in + 2 * nbg + 2:])

            step = (pl.program_id(0) * grid[1] + pl.program_id(1)) * nt + t

            @pl.when(step == 0)
            def _():
                for cp in exchange():
                    cp.start()

            @pl.when(step == grid[0] * grid[1] * nt - 1)
            def _():
                for cp in exchange():
                    cp.wait()

        res = _dot_tn(a_ref[...], b_ref[...])

        @pl.when(t == 0)
        def _():
            for u in range(ns):
                out_ref[u] = res[u * r:(u + 1) * r]

        @pl.when(t > 0)
        def _():
            for u in range(ns):
                out_ref[u] += res[u * r:(u + 1) * r]

        @pl.when(t == nt - 1)
        def _():
            copy_ref[...] = out_ref[...].astype(BF16)

    in_specs = [
        pl.BlockSpec((tt, tk), lambda k, n, t: (t, k)),
        pl.BlockSpec((tt, tn), lambda k, n, t: (t, n)),
    ]
    args = [a, b]
    aliases = {}
    if prev is not None:
        in_specs += [pl.BlockSpec(memory_space=pl.ANY)] * 2
        args += list(prev)
        aliases = {2: 0, 3: 1}
    out_spec = pl.BlockSpec(out_block, lambda k, n, t: out_index(k, n))
    return tuple(pl.pallas_call(
        body,
        name=name,
        grid=grid,
        in_specs=in_specs + [ANY] * nbg,
        out_specs=[out_spec, out_spec] + [ANY] * nbg,
        out_shape=[jax.ShapeDtypeStruct(out_shape, F32), jax.ShapeDtypeStruct(out_shape, BF16)]
        + [jax.ShapeDtypeStruct(p.shape, p.dtype) for p in bg_parts],
        input_output_aliases=aliases,
        scratch_shapes=[pltpu.SemaphoreType.DMA((nbg, N_PEER_CHIPS))] * (2 if nbg else 0),
        compiler_params=_cparams(),
    )(*args, *bg_parts))


SB_BLOCK = 256
SB_QROWS = 256
SB_QROWS_BWD = 256
SB_UNDERFLOW_BITS = 140.0
SB_CHUNK = 128


LOG2E = 1.4426950408889634


def _softplus2(z2):
    sign = jnp.uint32(0x80000000)
    neg_abs = lax.bitcast_convert_type(lax.bitcast_convert_type(z2, jnp.uint32) | sign, F32)
    return jnp.log2(1.0 + jnp.exp2(neg_abs)) + jnp.maximum(z2, 0.0)


def _twice(x):
    return jnp.concatenate([x, x], axis=1)


def sb_fwd(qkv, bg_shards=(), bg_lands=()):
    T = qkv.shape[0]
    tq, tk = SB_QROWS, SB_BLOCK
    ratio = tq // tk
    npair = SB_HEADS // 2
    nbg = len(bg_shards)
    nq = T // tq

    def body(q_ref, k_ref, v_ref, *rest):
        bg_in = rest[:nbg]
        o_ref, tot_ref, first_ref = rest[2 * nbg:2 * nbg + 3]
        bg_out = rest[2 * nbg + 3:3 * nbg + 3]
        acc_s, c_s, z_s, w_s, kmax_s = rest[3 * nbg + 3:3 * nbg + 8]
        p = pl.program_id(0)
        i = pl.program_id(1)
        if nbg:
            gather = GatherOps([s.shape[1] for s in bg_shards], bg_in, bg_out, *rest[3 * nbg + 8:])

            @pl.when((p == 0) & (i == 0))
            def _():
                gather.start_ici()

        @pl.when(i == 0)
        def _():
            kmax_s[...] = jnp.max(jnp.abs(k_ref[...]), axis=0, keepdims=True).astype(F32)

        q = q_ref[...]
        lane = lax.broadcasted_iota(jnp.int32, (tq, LANES), 1)
        first = lane < HEAD_DIM
        zero = jnp.zeros_like(q)
        q_heads = (jnp.where(first, q, zero), jnp.where(first, zero, q))
        row = lax.broadcasted_iota(jnp.int32, (tq, tk), 0)
        col = lax.broadcasted_iota(jnp.int32, (tq, tk), 1)
        visible = [col + r * tk < row for r in range(ratio)]
        krow = lax.broadcasted_iota(jnp.int32, (tk, tk), 0)
        kcol = lax.broadcasted_iota(jnp.int32, (tk, tk), 1)
        from_s = (krow >= kcol).astype(BF16)
        acc_s[...] = jnp.zeros_like(acc_s)
        c_s[...] = jnp.zeros_like(c_s)

        def rows(j):
            return pl.ds(pl.multiple_of(j * tk, tk), tk)

        def logits(j):
            kb = k_ref[rows(j), :]
            for hd in range(2):
                z_s[hd] = _dot_nt(q_heads[hd], kb) * LOG2E

        def flush(j):
            vb = v_ref[rows(j), :]
            for hd in range(2):
                acc_s[hd] += _dot(w_s[hd], vb)

        def block(j, mask=None, flush_block=None):
            if flush_block is not None:
                flush(flush_block)
            chunks = [(hd, slice(r0, r0 + SB_CHUNK)) for hd in range(2) for r0 in range(0, tq, SB_CHUNK)]
            k_next = k_ref[rows(jnp.maximum(j - 1, 0)), :]
            es, sums = [], []
            for hd, rs in chunks:
                z2 = z_s[hd, rs, :]
                z_s[hd, rs, :] = _dot_nt(q_heads[hd][rs, :], k_next) * LOG2E
                if mask is not None:
                    z2 = jnp.where(mask if mask.ndim == 0 else mask[rs, :], z2, NEG_BIG)
                sp = _softplus2(z2)
                c = c_s[hd, rs, :]
                es.append(z2 + _twice(c))
                c_s[hd, rs, :] = c - jnp.sum(sp, axis=1, keepdims=True)
                sums.append(_dot(sp.astype(BF16), from_s))
            for (hd, rs), e, s in zip(chunks, es, sums):
                w_s[hd, rs, :] = jnp.exp2(e - s).astype(BF16)

        z_bound = [LOG2E * jnp.sum(jnp.abs(q_heads[hd].astype(F32)) * kmax_s[...], axis=1, keepdims=True)
                   for hd in range(2)]

        def more_keys_matter():
            top = jnp.maximum(c_s[0] + z_bound[0], c_s[1] + z_bound[1])
            return (jnp.max(top) >= -SB_UNDERFLOW_BITS).astype(jnp.int32)

        assert ratio == 1
        logits(i)
        block(i, visible[0])

        def trip(carry):
            trips, _ = carry
            j = i - 1 - trips
            block(j, flush_block=j + 1)
            return trips + 1, more_keys_matter()

        trips, _ = lax.while_loop(lambda carry: jnp.logical_and(carry[0] < i, carry[1] > 0), trip,
                                  (jnp.int32(0), jnp.int32(1)))
        first_walked = i - trips
        flush(first_walked)
        first_ref[p, i] = first_walked.astype(F32)
        o_ref[...] = jnp.where(first, acc_s[0], acc_s[1]).astype(BF16)
        tot_ref[...] = jnp.where(first, c_s[0], c_s[1])
        if nbg:
            @pl.when((p == npair - 1) & (i == nq - 1))
            def _():
                gather.wait_ici()

    res = pl.pallas_call(
        body,
        name="sb_fwd",
        grid=(npair, nq),
        in_specs=[
            pl.BlockSpec((tq, LANES), lambda p, i: (i, p)),
            pl.BlockSpec((T, LANES), lambda p, i: (0, npair + p)),
            pl.BlockSpec((T, LANES), lambda p, i: (0, 2 * npair + p)),
        ] + [ANY] * (2 * nbg),
        out_specs=[pl.BlockSpec((tq, LANES), lambda p, i: (i, p))] * 2 + [pl.BlockSpec(memory_space=pltpu.SMEM)]
        + [ANY] * nbg,
        out_shape=[jax.ShapeDtypeStruct((T, D_MODEL), BF16), jax.ShapeDtypeStruct((T, D_MODEL), F32),
                   jax.ShapeDtypeStruct((npair, nq), F32)]
        + [jax.ShapeDtypeStruct(l.shape, l.dtype) for l in bg_lands],
        input_output_aliases={3 + nbg + t: 3 + t for t in range(nbg)},
        scratch_shapes=[
            pltpu.VMEM((2, tq, LANES), F32), pltpu.VMEM((2, tq, LANES), F32),
            pltpu.VMEM((2, tq, tk), F32), pltpu.VMEM((2, tq, tk), BF16),
            pltpu.VMEM((1, LANES), F32),
        ] + [pltpu.SemaphoreType.DMA((nbg, N_PEER_CHIPS))] * (2 if nbg else 0),
        compiler_params=_cparams(),
    )(qkv, qkv, qkv, *bg_shards, *bg_lands)
    return res[0], res[1], res[2], list(res[3:])


def sb_bwd(qkv, do, tot, first_block, bg_parts=()):
    T = qkv.shape[0]
    tq, tk = SB_QROWS_BWD, SB_BLOCK
    ratio = tq // tk
    npair = SB_HEADS // 2
    nq = T // tq
    nk = T // tk
    nbg = len(bg_parts)
    assert SB_QROWS == SB_QROWS_BWD

    def body(first_ref, q_ref, k_ref, v_ref, do_ref, tot_ref, *rest):
        bg_in = rest[:nbg]
        dq_ref, dk_ref, dv_ref = rest[nbg:nbg + 3]
        bg_out = rest[nbg + 3:2 * nbg + 3]
        dkt_s, dvt_s, dq_s, rest_s, cg_s, z_s, da_s, dz_s, a_s = rest[2 * nbg + 3:2 * nbg + 12]
        i = pl.program_id(1)
        start = jnp.clip(first_ref[pl.program_id(0), i].astype(jnp.int32), 0, i)
        if nbg:
            @pl.when((pl.program_id(0) == 0) & (i == 0))
            def _():
                for cp in chip_partial_copies(bg_in, bg_out, *rest[2 * nbg + 12:]):
                    cp.start()

        @pl.when(i == 0)
        def _():
            dkt_s[...] = jnp.zeros_like(dkt_s)
            dvt_s[...] = jnp.zeros_like(dvt_s)

        q = q_ref[...]
        do_ = do_ref[...]
        tot_ = tot_ref[...]
        q_t = q.astype(F32).T.astype(BF16)
        do_t = do_.astype(F32).T.astype(BF16)
        lane = lax.broadcasted_iota(jnp.int32, (tq, LANES), 1)
        first = lane < HEAD_DIM
        zero = jnp.zeros_like(q)
        q_heads = (jnp.where(first, q, zero), jnp.where(first, zero, q))
        do_heads = (jnp.where(first, do_, zero), jnp.where(first, zero, do_))
        row = lax.broadcasted_iota(jnp.int32, (tq, tk), 0)
        col = lax.broadcasted_iota(jnp.int32, (tq, tk), 1)
        visible = [col + r * tk < row for r in range(ratio)]
        krow = lax.broadcasted_iota(jnp.int32, (tk, tk), 0)
        kcol = lax.broadcasted_iota(jnp.int32, (tk, tk), 1)
        before = (krow < kcol).astype(BF16)
        from_s = (krow >= kcol).astype(BF16)
        last = ratio * i + ratio - 1
        rest_s[0] = jnp.broadcast_to(tot_[:, 0:1], (tq, LANES))
        rest_s[1] = jnp.broadcast_to(tot_[:, HEAD_DIM:HEAD_DIM + 1], (tq, LANES))
        cg_s[...] = jnp.zeros_like(cg_s)
        dq_s[...] = jnp.zeros_like(dq_s)
        dz_s[...] = jnp.zeros_like(dz_s)
        a_s[...] = jnp.zeros_like(a_s)

        def rows(j):
            return pl.ds(pl.multiple_of(j * tk, tk), tk)

        def logits(j):
            kb = k_ref[rows(j), :]
            vb = v_ref[rows(j), :]
            for hd in range(2):
                z_s[hd] = _dot_nt(q_heads[hd], kb) * LOG2E
                da_s[hd] = _dot_nt(do_heads[hd], vb)

        def flush(j):
            kb = k_ref[rows(j), :]
            for hd in range(2):
                dims = slice(hd * HEAD_DIM, (hd + 1) * HEAD_DIM)
                dq_s[hd] += _dot(dz_s[hd], kb)
                dkt_s[j, dims, :] += _dot(q_t[dims, :], dz_s[hd])
                dvt_s[j, dims, :] += _dot(do_t[dims, :], a_s[hd])

        def block(j, mask=None):
            flush(jnp.maximum(j - 1, 0))
            chunks = [(hd, slice(r0, r0 + SB_CHUNK)) for hd in range(2) for r0 in range(0, tq, SB_CHUNK)]
            nxt = rows(jnp.minimum(j + 1, last))
            k_next = k_ref[nxt, :]
            v_next = v_ref[nxt, :]
            stage1 = []
            for hd, rs in chunks:
                z2 = z_s[hd, rs, :]
                z_s[hd, rs, :] = _dot_nt(q_heads[hd][rs, :], k_next) * LOG2E
                if mask is not None:
                    z2 = jnp.where(mask if mask.ndim == 0 else mask[rs, :], z2, NEG_BIG)
                sp = _softplus2(z2)
                rest = rest_s[hd, rs, :] + jnp.sum(sp, axis=1, keepdims=True)
                rest_s[hd, rs, :] = rest
                stage1.append((z2 + _twice(rest), z2 - sp, _dot(sp.astype(BF16), from_s)))
            stage2 = []
            for (hd, rs), (e, log2_beta, ahead) in zip(chunks, stage1):
                a = jnp.exp2(e - ahead)
                g = a * da_s[hd, rs, :]
                da_s[hd, rs, :] = _dot_nt(do_heads[hd][rs, :], v_next)
                cg = cg_s[hd, rs, :]
                a_s[hd, rs, :] = a.astype(BF16)
                cg_s[hd, rs, :] = cg + jnp.sum(g, axis=1, keepdims=True)
                stage2.append((g, g + _twice(cg), log2_beta, _dot(g.astype(BF16), before)))
            for (hd, rs), (g, g_from, log2_beta, g_before) in zip(chunks, stage2):
                dz_s[hd, rs, :] = (g - jnp.exp2(log2_beta) * (g_from + g_before)).astype(BF16)

        assert ratio == 1
        logits(start)

        @pl.loop(start, i)
        def _(j):
            block(j)

        block(i, visible[0])
        flush(last)
        dq_ref[...] = (jnp.where(first, dq_s[0], dq_s[1]) * ATTN_SCALE).astype(BF16)

        @pl.when(i == nq - 1)
        def _():
            @pl.loop(0, nk)
            def _(b):
                dk_ref[rows(b), :] = dkt_s[b].T.astype(BF16)
                dv_ref[rows(b), :] = dvt_s[b].T.astype(BF16)

        if nbg:
            @pl.when((pl.program_id(0) == npair - 1) & (i == nq - 1))
            def _():
                for cp in chip_partial_copies(bg_in, bg_out, *rest[2 * nbg + 12:]):
                    cp.wait()

    qblk = pl.BlockSpec((tq, LANES), lambda p, i: (i, p))
    full = pl.BlockSpec((T, LANES), lambda p, i: (0, p))
    res = pl.pallas_call(
        body,
        name="sb_bwd",
        grid=(npair, nq),
        in_specs=[
            pl.BlockSpec(memory_space=pltpu.SMEM),
            qblk,
            pl.BlockSpec((T, LANES), lambda p, i: (0, npair + p)),
            pl.BlockSpec((T, LANES), lambda p, i: (0, 2 * npair + p)),
            qblk, qblk,
        ] + [ANY] * nbg,
        out_specs=[qblk, full, full] + [ANY] * nbg,
        out_shape=[jax.ShapeDtypeStruct((T, D_MODEL), BF16)] * 3
        + [jax.ShapeDtypeStruct(b.shape, b.dtype) for b in bg_parts],
        scratch_shapes=[
            pltpu.VMEM((nk, LANES, tk), F32), pltpu.VMEM((nk, LANES, tk), F32),
            pltpu.VMEM((2, tq, LANES), F32), pltpu.VMEM((2, tq, LANES), F32), pltpu.VMEM((2, tq, LANES), F32),
            pltpu.VMEM((2, tq, tk), F32), pltpu.VMEM((2, tq, tk), F32),
            pltpu.VMEM((2, tq, tk), BF16), pltpu.VMEM((2, tq, tk), BF16),
        ] + [pltpu.SemaphoreType.DMA((nbg, N_PEER_CHIPS))] * (2 if nbg else 0),
        compiler_params=_cparams(),
    )(first_block, qkv, qkv, qkv, do, tot, *bg_parts)
    return res[0], res[1], res[2], list(res[3:])


def _swa_valid(n):
    qi = lax.broadcasted_iota(jnp.int32, (WINDOW, 2 * WINDOW), 0)
    ki = lax.broadcasted_iota(jnp.int32, (WINDOW, 2 * WINDOW), 1)
    diff = qi + WINDOW - ki
    return (diff >= 0) & (diff < WINDOW) & ((n > 0) | (ki >= WINDOW))


def _to_half(x, first, src, dst):
    keep = first if src == 0 else jnp.logical_not(first)
    x = jnp.where(keep, x, jnp.zeros_like(x))
    if src != dst:
        x = pltpu.roll(x.astype(F32), HEAD_DIM, 1).astype(BF16)
    return x


SWA_GROUP = SWA_Q_HEADS // SWA_KV_HEADS


def _swa_cols(h):
    return slice((h // 2) * LANES, (h // 2 + 1) * LANES)


def _swa_kv_pair(h):
    return (h // SWA_GROUP) // 2


def _swa_kv_half(h):
    return (h // SWA_GROUP) % 2


def _kv_band(prev_ref, cur_ref, pb):
    cols = slice(pb * LANES, (pb + 1) * LANES)
    return jnp.concatenate([prev_ref[:, cols], cur_ref[:, cols]], axis=0)


def _swa_specs(T):
    nb = T // WINDOW
    kv_w = SWA_KV_HEADS * HEAD_DIM
    qrow = pl.BlockSpec((WINDOW, D_MODEL), lambda n: (n, 0))
    kv = [pl.BlockSpec((WINDOW, kv_w), lambda n, col=col, back=back: (jnp.maximum(n - back, 0), col))
          for col in (0, 1) for back in (0, 1)]
    smem = pl.BlockSpec(memory_space=pltpu.SMEM)
    return nb, qrow, kv, smem


def swa_fwd(q, kv, sinks):
    T = q.shape[0]
    nb, qrow, kv_specs, smem = _swa_specs(T)

    def body(sink_ref, q_ref, kc_ref, kp_ref, vc_ref, vp_ref, o_ref, lse_ref):
        n = pl.program_id(0)
        lane = lax.broadcasted_iota(jnp.int32, (WINDOW, LANES), 1)
        first = lane < HEAD_DIM
        valid = _swa_valid(n)
        k2 = [_kv_band(kp_ref, kc_ref, pb) for pb in range(SWA_KV_HEADS // 2)]
        v2 = [_kv_band(vp_ref, vc_ref, pb) for pb in range(SWA_KV_HEADS // 2)]
        logits = [jnp.where(valid, _dot_nt(_to_half(q_ref[:, _swa_cols(h)], first, h % 2, _swa_kv_half(h)),
                                            k2[_swa_kv_pair(h)]), NEG_BIG) for h in range(SWA_Q_HEADS)]
        probs = []
        lse_acc = jnp.zeros((WINDOW, LANES), F32)
        for h, s in enumerate(logits):
            sink = sink_ref[h]
            m = jnp.maximum(jnp.max(s, axis=1, keepdims=True), sink)
            p = jnp.exp(s - m)
            den = jnp.sum(p, axis=1, keepdims=True) + jnp.exp(sink - m)
            probs.append((p / den).astype(BF16))
            lse_acc = jnp.where(lane == h, m + jnp.log(den), lse_acc)
        outs = []
        for h, p in enumerate(probs):
            o = _dot(p, v2[_swa_kv_pair(h)])
            outs.append(pltpu.roll(o, HEAD_DIM, 1) if h % 2 != _swa_kv_half(h) else o)
        for pair in range(SWA_Q_HEADS // 2):
            o_ref[:, _swa_cols(2 * pair)] = jnp.where(first, outs[2 * pair], outs[2 * pair + 1]).astype(BF16)
        lse_ref[...] = lse_acc

    return pl.pallas_call(
        body,
        name="swa_fwd",
        grid=(nb,),
        in_specs=[smem, qrow] + kv_specs,
        out_specs=[qrow, pl.BlockSpec((WINDOW, LANES), lambda n: (n, 0))],
        out_shape=[jax.ShapeDtypeStruct((T, D_MODEL), BF16), jax.ShapeDtypeStruct((T, LANES), F32)],
        compiler_params=_cparams(),
    )(sinks, q, kv, kv, kv, kv)


def swa_bwd(q, kv, sinks, do, o, lse, cos, sin):
    T = q.shape[0]
    nb, qrow, kv_specs, smem = _swa_specs(T)
    kv_w = SWA_KV_HEADS * HEAD_DIM

    def body(sink_ref, q_ref, kc_ref, kp_ref, vc_ref, vp_ref, do_ref, o_ref, lse_ref, cos_ref, sin_ref,
             dq_ref, own_ref, prv_ref, dsink_ref):
        n = pl.program_id(0)

        @pl.when(n == 0)
        def _():
            dsink_ref[...] = jnp.zeros_like(dsink_ref)

        lane = lax.broadcasted_iota(jnp.int32, (WINDOW, LANES), 1)
        lane1 = lax.broadcasted_iota(jnp.int32, (1, LANES), 1)
        first = lane < HEAD_DIM
        valid = _swa_valid(n)
        cos_ = cos_ref[...]
        sin_ = sin_ref[...]
        k2 = [_kv_band(kp_ref, kc_ref, pb) for pb in range(SWA_KV_HEADS // 2)]
        v2 = [_kv_band(vp_ref, vc_ref, pb) for pb in range(SWA_KV_HEADS // 2)]
        q_t = q_ref[...].astype(F32).T.astype(BF16)
        do_t = do_ref[...].astype(F32).T.astype(BF16)
        stage1 = []
        for h in range(SWA_Q_HEADS):
            a, b, pb = h % 2, _swa_kv_half(h), _swa_kv_pair(h)
            qh = _to_half(q_ref[:, _swa_cols(h)], first, a, b)
            doh = _to_half(do_ref[:, _swa_cols(h)], first, a, b)
            stage1.append((jnp.where(valid, _dot_nt(qh, k2[pb]), NEG_BIG), _dot_nt(doh, v2[pb])))
        deltas = []
        for pair in range(SWA_Q_HEADS // 2):
            prod = do_ref[:, _swa_cols(2 * pair)].astype(F32) * o_ref[:, _swa_cols(2 * pair)].astype(F32)
            deltas += [jnp.sum(jnp.where(first, prod, 0.0), axis=1, keepdims=True),
                       jnp.sum(jnp.where(first, 0.0, prod), axis=1, keepdims=True)]
        stage2 = []
        dsink = jnp.zeros((1, LANES), F32)
        for h, (s, dp) in enumerate(stage1):
            lse_h = lse_ref[:, h:h + 1]
            p = jnp.exp(s - lse_h)
            delta = deltas[h]
            p_sink = jnp.exp(sink_ref[h] - lse_h)
            dsink = dsink + jnp.where(lane1 == h, -jnp.sum(p_sink * delta, axis=0, keepdims=True), 0.0)
            stage2.append(((p * (dp - delta)).astype(BF16), p.astype(BF16)))
        dqs = []
        dk_t = [None] * SWA_KV_HEADS
        dv_t = [None] * SWA_KV_HEADS
        for h, (ds, pb16) in enumerate(stage2):
            kvh = h // SWA_GROUP
            dims = slice(h * HEAD_DIM, (h + 1) * HEAD_DIM)
            dq = _dot(ds, k2[_swa_kv_pair(h)])
            dqs.append(pltpu.roll(dq, HEAD_DIM, 1) if h % 2 != _swa_kv_half(h) else dq)
            dk_h = _dot(q_t[dims, :], ds)
            dv_h = _dot(do_t[dims, :], pb16)
            dk_t[kvh] = dk_h if dk_t[kvh] is None else dk_t[kvh] + dk_h
            dv_t[kvh] = dv_h if dv_t[kvh] is None else dv_t[kvh] + dv_h
        for pair in range(SWA_Q_HEADS // 2):
            dqp = jnp.where(first, dqs[2 * pair], dqs[2 * pair + 1])
            dq_ref[:, _swa_cols(2 * pair)] = ((dqp * cos_ + _swap32(dqp * sin_)) * ATTN_SCALE).astype(BF16)
        for pb in range(SWA_KV_HEADS // 2):
            dk2 = jnp.concatenate([dk_t[2 * pb], dk_t[2 * pb + 1]], axis=0).T
            dv2 = jnp.concatenate([dv_t[2 * pb], dv_t[2 * pb + 1]], axis=0).T
            kcols = slice(pb * LANES, (pb + 1) * LANES)
            vcols = slice(kv_w + pb * LANES, kv_w + (pb + 1) * LANES)
            prv_ref[:, kcols] = dk2[:WINDOW]
            own_ref[:, kcols] = dk2[WINDOW:]
            prv_ref[:, vcols] = dv2[:WINDOW]
            own_ref[:, vcols] = dv2[WINDOW:]
        dsink_ref[...] += dsink

    tab = pl.BlockSpec((WINDOW, LANES), lambda n: (n, 0))
    kvrow = pl.BlockSpec((WINDOW, 2 * kv_w), lambda n: (n, 0))
    return pl.pallas_call(
        body,
        name="swa_bwd",
        grid=(nb,),
        in_specs=[smem, qrow] + kv_specs + [qrow, qrow, tab, tab, tab],
        out_specs=[qrow, kvrow, kvrow, pl.BlockSpec((1, LANES), lambda n: (0, 0))],
        out_shape=[
            jax.ShapeDtypeStruct((T, D_MODEL), BF16),
            jax.ShapeDtypeStruct((T, 2 * kv_w), F32),
            jax.ShapeDtypeStruct((T, 2 * kv_w), F32),
            jax.ShapeDtypeStruct((1, LANES), F32),
        ],
        compiler_params=_cparams(),
    )(sinks, q, kv, kv, kv, kv, do, o, lse, cos, sin)


def kv_grad_combine(own, prv, cos, sin):
    T = own.shape[0]
    nb = T // WINDOW
    kv_w = SWA_KV_HEADS * HEAD_DIM

    def body(own_ref, nxt_ref, cos_ref, sin_ref, out_ref):
        n = pl.program_id(0)
        nxt = jnp.where(n + 1 < nb, nxt_ref[...], 0.0)
        tot = own_ref[...] + nxt
        dk = tot[:, :kv_w]
        c = _tile_lanes(cos_ref[...], kv_w)
        s = _tile_lanes(sin_ref[...], kv_w)
        out_ref[:, :kv_w] = (dk * c + _swap32(dk * s)).astype(BF16)
        out_ref[:, kv_w:] = tot[:, kv_w:].astype(BF16)

    tab = pl.BlockSpec((WINDOW, LANES), lambda n: (n, 0))
    kvrow = pl.BlockSpec((WINDOW, 2 * kv_w), lambda n: (n, 0))
    return pl.pallas_call(
        body,
        name="kv_grad_combine",
        grid=(nb,),
        in_specs=[kvrow, pl.BlockSpec((WINDOW, 2 * kv_w), lambda n: (jnp.minimum(n + 1, nb - 1), 0)), tab, tab],
        out_specs=kvrow,
        out_shape=jax.ShapeDtypeStruct((T, 2 * kv_w), BF16),
        compiler_params=_cparams(),
    )(own, prv, cos, sin)


ANY = pl.BlockSpec(memory_space=pl.ANY)


def _place():
    x, y, c = lax.axis_index("x"), lax.axis_index("y"), lax.axis_index("c")
    other_chips = [(1 - x, y), (x, 1 - y), (1 - x, 1 - y)]
    return x, y, c, 2 * x + y, other_chips


N_PEER_CHIPS = N_CHIPS - 1


class GatherOps:
    def __init__(self, rows, shards, lands, ici_send, ici_recv, d2d_send=None, d2d_recv=None):
        self.rows, self.shards, self.lands = rows, shards, lands
        self.ici_send, self.ici_recv, self.d2d_send, self.d2d_recv = ici_send, ici_recv, d2d_send, d2d_recv
        self.x, self.y, self.c, self.me, self.chips = _place()
        self.pairs = [(t, jdx) for t in range(len(rows)) for jdx in range(N_PEER_CHIPS)]

    def _half(self, ref, t, which):
        r = self.rows[t] // 2
        return ref.at[:, pl.ds(pl.multiple_of(which * r, 16), r), :]

    def _ici(self, t, jdx):
        px, py = self.chips[jdx]
        return pltpu.make_async_remote_copy(
            src_ref=self._half(self.shards[t], t, self.c), dst_ref=self._half(self.lands[t].at[self.me], t, self.c),
            send_sem=self.ici_send.at[t, jdx], recv_sem=self.ici_recv.at[t, jdx],
            device_id=(px, py, self.c), device_id_type=MESH)

    def _landed(self, t, jdx):
        px, py = self.chips[jdx]
        blk = self._half(self.lands[t].at[2 * px + py], t, self.c)
        return pltpu.make_async_remote_copy(
            src_ref=blk, dst_ref=blk, send_sem=self.ici_send.at[t, jdx], recv_sem=self.ici_recv.at[t, jdx],
            device_id=(px, py, self.c), device_id_type=MESH)

    def _d2d(self, t, jdx, which):
        px, py = self.chips[jdx]
        blk = self._half(self.lands[t].at[2 * px + py], t, which)
        return pltpu.make_async_remote_copy(
            src_ref=blk, dst_ref=blk, send_sem=self.d2d_send.at[t, jdx], recv_sem=self.d2d_recv.at[t, jdx],
            device_id=(self.x, self.y, 1 - self.c), device_id_type=MESH)

    def start_ici(self):
        for t, jdx in self.pairs:
            self._ici(t, jdx).start()

    def wait_ici(self):
        for t, jdx in self.pairs:
            self._landed(t, jdx).wait_recv()
        self.wait_ici_sends()

    def wait_ici_sends(self):
        for t, jdx in self.pairs:
            self._ici(t, jdx).wait_send()

    def forward_arrivals(self):
        for t, jdx in self.pairs:
            self._landed(t, jdx).wait_recv()
            self._d2d(t, jdx, self.c).start()

    def start_forwards(self):
        for t, jdx in self.pairs:
            self._d2d(t, jdx, self.c).start()

    def wait_forwards(self):
        for t, jdx in self.pairs:
            self._d2d(t, jdx, 1 - self.c).wait_recv()
            self._d2d(t, jdx, self.c).wait_send()


def all_gather_weights(shards, lands):
    n = len(shards)
    rows = [s.shape[1] for s in shards]

    def body(*refs):
        ins, outs = refs[:n], refs[2 * n:3 * n]
        ops = GatherOps(rows, ins, outs, *refs[3 * n:])
        ops.start_ici()
        ops.forward_arrivals()
        ops.wait_forwards()
        ops.wait_ici_sends()

    return pl.pallas_call(
        body,
        name="all_gather_weights",
        in_specs=[ANY] * (2 * n),
        out_specs=[ANY] * n,
        out_shape=[jax.ShapeDtypeStruct(l.shape, l.dtype) for l in lands],
        input_output_aliases={n + t: t for t in range(n)},
        scratch_shapes=[pltpu.SemaphoreType.DMA((n, N_PEER_CHIPS))] * 4,
    )(*shards, *lands)


def place_own_shard(name, shard, chip):
    nl, r, c = shard.shape

    def body(chip_ref, s_ref, o_ref):
        o_ref[...] = s_ref[...]

    return pl.pallas_call(
        body, name=name,
        grid_spec=pltpu.PrefetchScalarGridSpec(
            num_scalar_prefetch=1, grid=(nl,),
            in_specs=[pl.BlockSpec((None, r, c), lambda l, chip_ref: (l, 0, 0))],
            out_specs=pl.BlockSpec((None, None, r, c), lambda l, chip_ref: (chip_ref[0], l, 0, 0))),
        out_shape=jax.ShapeDtypeStruct((N_CHIPS,) + shard.shape, shard.dtype), compiler_params=_cparams(),
    )(chip, shard)


def exchange_halves(name, slabs):
    n = len(slabs)

    def body(*refs):
        ins, theirs = refs[:n], refs[n:2 * n]
        send_sems, recv_sems = refs[2 * n:]
        x, y, c, _, _ = _place()
        copies = []
        for t in range(n):
            cp = pltpu.make_async_remote_copy(
                src_ref=ins[t].at[1 - c], dst_ref=theirs[t], send_sem=send_sems.at[t],
                recv_sem=recv_sems.at[t], device_id=(x, y, 1 - c), device_id_type=MESH)
            cp.start()
            copies.append(cp)
        for cp in copies:
            cp.wait()

    return pl.pallas_call(
        body,
        name=name,
        in_specs=[ANY] * n,
        out_specs=[ANY] * n,
        out_shape=[jax.ShapeDtypeStruct(s.shape[1:], s.dtype) for s in slabs],
        scratch_shapes=[pltpu.SemaphoreType.DMA((n,)), pltpu.SemaphoreType.DMA((n,))],
    )(*slabs)


def chip_partial_copies(ins, outs, send_sems, recv_sems):
    _, _, c, me, chips = _place()
    return [pltpu.make_async_remote_copy(
        src_ref=ins[t].at[2 * px + py], dst_ref=outs[t].at[me], send_sem=send_sems.at[t, jdx],
        recv_sem=recv_sems.at[t, jdx], device_id=(px, py, c), device_id_type=MESH)
        for t in range(len(ins)) for jdx, (px, py) in enumerate(chips)]


def exchange_chip_partials(name, parts):
    n = len(parts)

    def body(*refs):
        copies = chip_partial_copies(refs[:n], refs[n:2 * n], *refs[2 * n:])
        for cp in copies:
            cp.start()
        for cp in copies:
            cp.wait()

    return pl.pallas_call(
        body,
        name=name,
        in_specs=[ANY] * n,
        out_specs=[ANY] * n,
        out_shape=[jax.ShapeDtypeStruct(p.shape, p.dtype) for p in parts],
        scratch_shapes=[pltpu.SemaphoreType.DMA((n, 3)), pltpu.SemaphoreType.DMA((n, 3))],
    )(*parts)


def sibling_share_copies(ins, outs, send_sems, recv_sems):
    x, y, c, _, _ = _place()
    return [pltpu.make_async_remote_copy(
        src_ref=ins[t], dst_ref=outs[t], send_sem=send_sems.at[t], recv_sem=recv_sems.at[t],
        device_id=(x, y, 1 - c), device_id_type=MESH) for t in range(len(ins))]


def share_reduced_halves(name, halves):
    n = len(halves)

    def body(*refs):
        copies = sibling_share_copies(refs[:n], refs[n:2 * n], *refs[2 * n:])
        for cp in copies:
            cp.start()
        for cp in copies:
            cp.wait()

    return pl.pallas_call(
        body,
        name=name,
        in_specs=[ANY] * n,
        out_specs=[ANY] * n,
        out_shape=[jax.ShapeDtypeStruct(h.shape, h.dtype) for h in halves],
        scratch_shapes=[pltpu.SemaphoreType.DMA((n,)), pltpu.SemaphoreType.DMA((n,))],
    )(*halves)


def _row_tile(r, c):
    tr = r
    while tr * c * 4 > (3 << 19) and tr % 16 == 0:
        tr //= 2
    return tr


def add_sibling(name, slab, theirs, core):
    _, ns, slots, r, c = slab.shape
    tr = _row_tile(r, c)

    def body(core_ref, a_ref, b_ref, o_ref):
        o_ref[...] = (a_ref[...] + b_ref[...]).astype(BF16)

    blk = pl.BlockSpec((None, None, tr, c), lambda s, l, i, core_ref: (s, l, i, 0))
    return pl.pallas_call(
        body, name=name,
        grid_spec=pltpu.PrefetchScalarGridSpec(
            num_scalar_prefetch=1, grid=(ns, slots, r // tr),
            in_specs=[pl.BlockSpec((None, None, None, tr, c), lambda s, l, i, core_ref: (core_ref[0], s, l, i, 0)), blk],
            out_specs=blk),
        out_shape=jax.ShapeDtypeStruct(theirs.shape, BF16), compiler_params=_cparams(),
    )(core, slab, theirs)


def sum_chips(name, recv, own, chip):
    _, slots, r, c = recv.shape
    tr = _row_tile(r, c)

    def body(chip_ref, r0, r1, r2, r3, own_ref, o_ref):
        me = chip_ref[0]
        mine = own_ref[...]
        terms = [jnp.where(me == s, mine, rr[...]).astype(F32) for s, rr in enumerate((r0, r1, r2, r3))]
        o_ref[...] = ((terms[0] + terms[1]) + terms[2]) + terms[3]

    def src(s):
        return pl.BlockSpec((None, None, tr, c),
                            lambda l, i, chip_ref: (jnp.where(chip_ref[0] == s, (s + 1) % N_CHIPS, s), l, i, 0))

    return pl.pallas_call(
        body, name=name,
        grid_spec=pltpu.PrefetchScalarGridSpec(
            num_scalar_prefetch=1, grid=(slots, r // tr),
            in_specs=[src(0), src(1), src(2), src(3),
                      pl.BlockSpec((None, None, tr, c), lambda l, i, chip_ref: (chip_ref[0], l, i, 0))],
            out_specs=pl.BlockSpec((None, tr, c), lambda l, i, chip_ref: (l, i, 0))),
        out_shape=jax.ShapeDtypeStruct((slots, r, c), F32), compiler_params=_cparams(),
    )(chip, recv, recv, recv, recv, own)


def _adamw_math(w, g, m, v):
    m = ADAM_B1 * m + (1.0 - ADAM_B1) * g
    v = ADAM_B2 * v + (1.0 - ADAM_B2) * (g * g)
    m_hat = m / (1.0 - ADAM_B1 ** ADAM_STEP)
    v_hat = v / (1.0 - ADAM_B2 ** ADAM_STEP)
    delta = -ADAM_LR * (m_hat / (jnp.sqrt(v_hat) + ADAM_EPS) + ADAM_WD * w)
    return delta, m, v


def adamw_shard(name, w, m, v, g_pairs, core, slots, row_halves):
    n = w.shape[0]
    assert n == len(g_pairs)
    _, r, c = g_pairs[0][0].shape
    tr = _row_tile(r, c)
    nr = r // tr

    def body(core_ref, w_ref, m_ref, v_ref, *rest):
        g_refs, (go_ref, d_ref, mo_ref, vo_ref) = rest[:2 * n], rest[2 * n:]
        mine = pl.program_id(1) == core_ref[0]
        g = jnp.where(mine, g_refs[0][...], g_refs[1][...])
        for l in range(1, n):
            g = jnp.where(pl.program_id(0) == l, jnp.where(mine, g_refs[2 * l][...], g_refs[2 * l + 1][...]), g)
        delta, mm, vv = _adamw_math(w_ref[...], g, m_ref[...], v_ref[...])
        go_ref[...] = g
        d_ref[...] = delta
        mo_ref[...] = mm
        vo_ref[...] = vv

    if row_halves:
        wspec = pl.BlockSpec((None, tr, c), lambda l, h, i, core_ref: (l, h * nr + i, 0))
    else:
        wspec = pl.BlockSpec((None, tr, c), lambda l, h, i, core_ref: (l, i, h))
    def gspec(slot):
        return pl.BlockSpec((None, tr, c), lambda l, h, i, core_ref: (slot, i, 0))

    shp = jax.ShapeDtypeStruct(w.shape, F32)
    return pl.pallas_call(
        body, name=name,
        grid_spec=pltpu.PrefetchScalarGridSpec(
            num_scalar_prefetch=1, grid=(n, 2, nr),
            in_specs=[wspec, wspec, wspec] + [gspec(s) for s in slots for _ in range(2)], out_specs=[wspec] * 4),
        out_shape=[shp] * 4, compiler_params=_cparams(),
    )(core, w, m, v, *[g for pair in g_pairs for g in pair])


SMALL_ROWS = 16


def small_allreduce_adamw(part, w, m, v):
    def body(p_ref, w_ref, m_ref, v_ref, g_ref, d_ref, mo_ref, vo_ref, buf, send_sems, recv_sems):
        x, y, c, _, _ = _place()
        me = 4 * x + 2 * y + c
        buf[me] = p_ref[...]
        copies = []
        for k in range(1, N_DEV):
            kx, ky, kc = (k >> 2) & 1, (k >> 1) & 1, k & 1
            peer = (x ^ kx, y ^ ky, c ^ kc)
            cp = pltpu.make_async_remote_copy(
                src_ref=p_ref, dst_ref=buf.at[me], send_sem=send_sems.at[k - 1],
                recv_sem=recv_sems.at[k - 1], device_id=peer, device_id_type=MESH)
            cp.start()
            copies.append(cp)
        for cp in copies:
            cp.wait()
        g = buf[0]
        for dev in range(1, N_DEV):
            g = g + buf[dev]
        delta, mm, vv = _adamw_math(w_ref[...], g, m_ref[...], v_ref[...])
        g_ref[...] = g
        d_ref[...] = delta
        mo_ref[...] = mm
        vo_ref[...] = vv

    vm = pl.BlockSpec(memory_space=pltpu.VMEM)
    shp = jax.ShapeDtypeStruct(part.shape, F32)
    return pl.pallas_call(
        body, name="small_allreduce_adamw",
        in_specs=[vm] * 4, out_specs=[vm] * 4, out_shape=[shp] * 4,
        scratch_shapes=[
            pltpu.VMEM((N_DEV,) + part.shape, F32),
            pltpu.SemaphoreType.DMA((N_DEV - 1,)), pltpu.SemaphoreType.DMA((N_DEV - 1,)),
        ],
    )(part, w, m, v)


def _rope_tables(T):
    half = HEAD_DIM // 2
    inv_freq = ROPE_THETA ** (-jnp.arange(half, dtype=F32) / half)
    ang = jnp.arange(T).astype(F32)[:, None] * inv_freq[None, :]
    cos = jnp.tile(jnp.cos(ang), (1, LANES // half))
    sin = jnp.tile(jnp.sin(ang), (1, LANES // half))
    lane = jnp.arange(LANES)
    sign = jnp.where((lane % HEAD_DIM) < half, -1.0, 1.0).astype(F32)
    return cos, sin * sign[None, :]


def _pack_small(ffn1, mix, ffn2, kvn, fin, sinks, loss_row):
    sink_row = jnp.pad(sinks.reshape(1, SWA_Q_HEADS), ((0, 0), (0, D_MODEL - SWA_Q_HEADS)))
    rows = jnp.concatenate([ffn1, mix, ffn2, kvn.reshape(1, -1), fin.reshape(1, -1), sink_row, loss_row], axis=0)
    return jnp.concatenate([rows, jnp.zeros((SMALL_ROWS - rows.shape[0], D_MODEL), F32)], axis=0)


def kernel(x, ffn1_norm, ffn1_w_in, ffn1_w_out, mix_norm, ffn2_norm, ffn2_w_in, ffn2_w_out, sb_w_qkv, sb_w_o, kv_norm, kv_w, swa_w_q, swa_sinks, swa_w_o, final_norm, loss_target, m_ffn1_norm, m_ffn1_w_in, m_ffn1_w_out, m_mix_norm, m_ffn2_norm, m_ffn2_w_in, m_ffn2_w_out, m_sb_w_qkv, m_sb_w_o, m_kv_norm, m_kv_w, m_swa_w_q, m_swa_sinks, m_swa_w_o, m_final_norm, v_ffn1_norm, v_ffn1_w_in, v_ffn1_w_out, v_mix_norm, v_ffn2_norm, v_ffn2_w_in, v_ffn2_w_out, v_sb_w_qkv, v_sb_w_o, v_kv_norm, v_kv_w, v_swa_w_q, v_swa_sinks, v_swa_w_o, v_final_norm):
    T = x.shape[1]
    kv_cols = SWA_KV_HEADS * HEAD_DIM
    x2 = x.reshape(T, D_MODEL)
    tgt = loss_target.reshape(T, D_MODEL)
    cos, sin = _rope_tables(T)

    w_in_l = jnp.concatenate([ffn1_w_in, ffn2_w_in], axis=0).astype(BF16)
    w_out_l = jnp.concatenate([ffn1_w_out, ffn2_w_out], axis=0).astype(BF16)
    sq_l = jnp.concatenate([sb_w_o, swa_w_q, swa_w_o], axis=0).astype(BF16)
    qkv_l = sb_w_qkv[0].astype(BF16)
    kvw_l = kv_w.astype(BF16)
    core = lax.axis_index("c").astype(jnp.int32).reshape(1)
    chip = (2 * lax.axis_index("x") + lax.axis_index("y")).astype(jnp.int32).reshape(1)
    early = [w_in_l[:1], w_out_l[:1]]
    mid = [sq_l, qkv_l[None]]
    late = [w_in_l[1:], w_out_l[1:], kvw_l[None]]
    early_lands = [place_own_shard(f"own_early_{t}", s, chip) for t, s in enumerate(early)]
    mid_lands = [place_own_shard(f"own_mid_{t}", s, chip) for t, s in enumerate(mid)]
    late_lands = [place_own_shard(f"own_late_{t}", s, chip) for t, s in enumerate(late)]
    w_in0, w_out0 = all_gather_weights(early, early_lands)

    def ffn_w(slot):
        return (w_in0, w_out0, 0) if slot == 0 else (w_in_r, w_out_r, slot - 1)

    def vec(a, i):
        return a[i].reshape(1, D_MODEL)

    ident = lambda w: w
    sq_prep = lambda w: w.reshape(D_MODEL, w.shape[-1])
    qscale = jnp.concatenate([jnp.full((1, D_MODEL), ATTN_SCALE, F32), jnp.ones((1, 2 * D_MODEL), F32)], axis=1)
    swa_scale = jnp.full((1, D_MODEL), ATTN_SCALE, F32)
    sinks = swa_sinks.reshape(SWA_Q_HEADS)

    h1, gate1, up1, (w_sq, w_qkv) = ffn_fwd("l0a", x2, vec(ffn1_norm, 0), *ffn_w(SLOT_FFN1[0]), mid, mid_lands)
    w_qkv = w_qkv.reshape(N_CHIPS, D_MODEL, QKV_COLS)
    qkv = qkv_fwd(h1, vec(mix_norm, 0), w_qkv, qscale)
    o_sb, tot, sb_first, late_lands = sb_fwd(qkv, late, late_lands)
    h2, (w_in_r, w_out_r, w_kv) = linear_res("sb_out", o_sb, w_sq, SQ_SB_O, h1, late_lands)
    w_kv = w_kv.reshape(D_MODEL, 2 * kv_cols)
    h3, gate2, up2 = ffn_fwd("l0b", h2, vec(ffn2_norm, 0), *ffn_w(SLOT_FFN2[0]))
    kvn = kv_norm.reshape(1, D_MODEL)
    kv_sw = rms_linear("kv_proj", h3, kvn, w_kv, pl.BlockSpec((D_MODEL, kv_cols), lambda i, j: (0, j)), ident,
                       2 * kv_cols, kv_cols, rope=(cos, sin), rope_blocks=1)
    h4, gate3, up3 = ffn_fwd("l1a", h3, vec(ffn1_norm, 1), *ffn_w(SLOT_FFN1[1]))
    q_sw = rms_linear("swa_q", h4, vec(mix_norm, 1), w_sq,
                      pl.BlockSpec((N_CHIPS, None, SQ_ROWS, 512), lambda i, j: (0, SQ_SWA_Q, 0, j)), sq_prep,
                      D_MODEL, 512, rope=(cos, sin), scale=swa_scale)
    o_sw, lse = swa_fwd(q_sw, kv_sw, sinks)
    h5 = linear_res("swa_out", o_sw, w_sq, SQ_SWA_O, h4)
    dh6, gate4, up4, loss_p, d_final = ffn_fwd("l1b", h5, vec(ffn2_norm, 1), *ffn_w(SLOT_FFN2[1]),
                                               loss=(final_norm.reshape(1, D_MODEL), tgt))

    slab = {}
    ffn_place = {SLOT_FFN1[0]: (0, 0, 1), SLOT_FFN1[1]: (1, 0, 3), SLOT_FFN2[0]: (1, 1, 3), SLOT_FFN2[1]: (1, 2, 3)}
    sq_place = {SQ_SB_O: (1, 0, 3), SQ_SWA_Q: (1, 1, 3), SQ_SWA_O: (1, 2, 3)}

    exchanged = {}

    def chip_parts(grp, kinds):
        theirs = exchange_halves(f"exchange_halves_{grp}{kinds[0]}", [slab[kind, grp][1] for kind in kinds])
        return [add_sibling(f"add_sibling_{kind}{grp}", slab[kind, grp][0], t, core) for kind, t in zip(kinds, theirs)]

    def ffn_grads(tag, dh, h_in, g, gate, up, slot, proj=None, early_gate=(), early=()):
        dh_in, xn, dg_, du_, act, dhb, dnorm, *through_proj = ffn_bwd(tag, dh, h_in, g, gate, up, *ffn_w(slot),
                                                                      proj=proj)
        grp, s, ns = ffn_place[slot]
        in_shape = (2, N_CHIPS, ns, D_MODEL // 2, FF_CHUNK)
        out_shape = (2, N_CHIPS, ns, FF_ROWS, D_MODEL // 2)
        blk = (None, 1, None, D_MODEL // 2, FF_CHUNK)
        gate_parts = chip_parts(grp, list(early_gate)) if early_gate else []
        res = mm_tn(f"dw_gate_{tag}", xn, dg_, D_MODEL // 2, FF_CHUNK, blk,
                    lambda k, n: (k, n, s, 0, 0), in_shape, prev=slab.get(("in", grp)), bg_parts=gate_parts)
        slab["in", grp] = res[:2]
        exchanged.update({(kind, grp): (a, p) for kind, a, p in zip(early_gate, res[2:], gate_parts)})
        slab["in", grp] = mm_tn(f"dw_up_{tag}", xn, du_, D_MODEL // 2, FF_CHUNK, blk,
                                lambda k, n: (k, 2 + n, s, 0, 0), in_shape, prev=slab["in", grp])
        parts = chip_parts(grp, list(early)) if early else []
        res = mm_tn(f"dw_out_{tag}", act, dhb, FF_CHUNK, D_MODEL // 2, (None, 2, None, FF_ROWS, D_MODEL // 2),
                    lambda k, n: (n, k, s, 0, 0), out_shape, prev=slab.get(("out", grp)), bg_parts=parts)
        slab["out", grp] = res[:2]
        exchanged.update({(kind, grp): (a, p) for kind, a, p in zip(early, res[2:], parts)})
        return (dh_in, dnorm, *through_proj)

    def sq_grad(tag, a, dyb, t):
        grp, s, ns = sq_place[t]
        slab["sq", grp] = mm_tn(f"dw_sq_{tag}", a, dyb, D_MODEL, D_MODEL // 2,
                                (None, N_CHIPS, None, SQ_ROWS, D_MODEL // 2),
                                lambda k, n: (n, 0, s, 0, 0), (2, N_CHIPS, ns, SQ_ROWS, D_MODEL // 2),
                                prev=slab.get(("sq", grp)))

    def reduce_group(grp, kinds, host=None, share_host=None):
        todo = [kind for kind in kinds if (kind, grp) not in exchanged]
        parts = chip_parts(grp, todo)
        arrived = host(parts) if host else exchange_chip_partials(f"exchange_chip_partials_{grp}", parts)
        exchanged.update({(kind, grp): (a, p) for kind, a, p in zip(todo, arrived, parts)})
        halves = [sum_chips(f"sum_chips_{kind}{grp}", *exchanged[kind, grp], chip) for kind in kinds]
        if share_host:
            sib_halves = share_host(halves)
        else:
            sib_halves = share_reduced_halves(f"share_reduced_halves_{grp}", halves)
        return {kind: pair for kind, pair in zip(kinds, zip(halves, sib_halves))}

    dh5, d_ffn2_1, dh5b, do_sw = ffn_grads("l1b", dh6, h5, vec(ffn2_norm, 1), gate4, up4, SLOT_FFN2[1],
                                           proj=(w_sq, SQ_SWA_O))
    sq_grad("swa_o", o_sw, dh5b, SQ_SWA_O)
    dq_sw, kv_own, kv_prev, d_sinks = swa_bwd(q_sw, kv_sw, sinks, do_sw, o_sw, lse, cos, sin)
    sq_w_spec = pl.BlockSpec((N_CHIPS, None, SQ_ROWS, D_MODEL), lambda i, j: (0, SQ_SWA_Q, 0, 0))
    dh4, hn4, d_mix_1 = linear_bwd_rms("swa_q_bwd", [(dq_sw, w_sq, sq_w_spec, sq_prep)], h4, vec(mix_norm, 1), dh5,
                                       1, D_MODEL)
    sq_grad("swa_q", hn4, dq_sw, SQ_SWA_Q)
    dh3a, d_ffn1_1 = ffn_grads("l1a", dh4, h3, vec(ffn1_norm, 1), gate3, up3, SLOT_FFN1[1])
    dkv = kv_grad_combine(kv_own, kv_prev, cos, sin)
    kv_w_spec = pl.BlockSpec((D_MODEL, 2 * kv_cols), lambda i, j: (0, 0))
    dh3, xn3, d_kvn = linear_bwd_rms("kv_bwd", [(dkv, w_kv, kv_w_spec, ident)], h3, kvn, dh3a, 1, 2 * kv_cols)
    slab["kv", 1] = mm_tn("dw_kv", xn3, dkv, D_MODEL, kv_cols, (None, N_CHIPS, None, SQ_ROWS, kv_cols),
                          lambda k, n: (n, 0, 0, 0, 0), (2, N_CHIPS, 1, SQ_ROWS, kv_cols))
    dh2, d_ffn2_0, dh2b, do_sb = ffn_grads("l0b", dh3, h2, vec(ffn2_norm, 0), gate2, up2, SLOT_FFN2[0],
                                           proj=(w_sq, SQ_SB_O))
    sq_grad("sb_o", o_sb, dh2b, SQ_SB_O)
    sb_grads = []

    def behind_sb_bwd(parts):
        dq_sb, dk_sb, dv_sb, arrived = sb_bwd(qkv, do_sb, tot, sb_first, parts)
        sb_grads.extend([dq_sb, dk_sb, dv_sb])
        return arrived

    qkv_grads = []

    def behind_qkv_bwd(halves):
        dy = jnp.concatenate(sb_grads, axis=1)
        dh, hn, d_mix, theirs = qkv_bwd(dy, w_qkv, h1, vec(mix_norm, 0), dh2, halves)
        qkv_grads.extend([dy, dh, hn, d_mix])
        return theirs

    red = {1: reduce_group(1, ["in", "out", "sq", "kv"], host=behind_sb_bwd, share_host=behind_qkv_bwd)}
    dqkv, dh1, hn1, d_mix_0 = qkv_grads
    slab["qkv", 0] = mm_tn("dw_qkv", hn1, dqkv, D_MODEL // 2, QKV_COLS, (None, 1, None, D_MODEL // 2, QKV_COLS),
                           lambda k, n: (k, n, 0, 0, 0), (2, N_CHIPS, 1, D_MODEL // 2, QKV_COLS))
    dx, d_ffn1_0 = ffn_grads("l0a", dh1, x2, vec(ffn1_norm, 0), gate1, up1, SLOT_FFN1[0], early_gate=("qkv",),
                             early=("in",))
    red[0] = reduce_group(0, ["in", "out", "qkv"])

    def upd(name, w, m, v, kind, places, row_halves):
        shp = w.shape
        w3 = w.reshape((-1,) + shp[-2:])
        outs = adamw_shard(name, w3, m.reshape(w3.shape), v.reshape(w3.shape),
                           [red[grp][kind] for grp, _ in places], core, [s for _, s in places], row_halves)
        return [o.reshape(shp) for o in outs]

    ffn1_places = [ffn_place[s][:2] for s in SLOT_FFN1]
    ffn2_places = [ffn_place[s][:2] for s in SLOT_FFN2]
    r_ffn1_in = upd("adamw_ffn1_in", ffn1_w_in, m_ffn1_w_in, v_ffn1_w_in, "in", ffn1_places, True)
    r_ffn2_in = upd("adamw_ffn2_in", ffn2_w_in, m_ffn2_w_in, v_ffn2_w_in, "in", ffn2_places, True)
    r_ffn1_out = upd("adamw_ffn1_out", ffn1_w_out, m_ffn1_w_out, v_ffn1_w_out, "out", ffn1_places, False)
    r_ffn2_out = upd("adamw_ffn2_out", ffn2_w_out, m_ffn2_w_out, v_ffn2_w_out, "out", ffn2_places, False)
    r_qkv = upd("adamw_qkv", sb_w_qkv, m_sb_w_qkv, v_sb_w_qkv, "qkv", [(0, 0)], True)
    r_sb_o = upd("adamw_sb_o", sb_w_o, m_sb_w_o, v_sb_w_o, "sq", [sq_place[SQ_SB_O][:2]], False)
    r_swa_q = upd("adamw_swa_q", swa_w_q, m_swa_w_q, v_swa_w_q, "sq", [sq_place[SQ_SWA_Q][:2]], False)
    r_swa_o = upd("adamw_swa_o", swa_w_o, m_swa_w_o, v_swa_w_o, "sq", [sq_place[SQ_SWA_O][:2]], False)
    r_kv = upd("adamw_kv", kv_w, m_kv_w, v_kv_w, "kv", [(1, 0)], False)

    loss_row = jnp.pad(loss_p, ((0, 0), (0, D_MODEL - LANES)))
    d_sink_row = d_sinks[0, :SWA_Q_HEADS]
    part = _pack_small(jnp.concatenate([d_ffn1_0, d_ffn1_1], axis=0), jnp.concatenate([d_mix_0, d_mix_1], axis=0),
                       jnp.concatenate([d_ffn2_0, d_ffn2_1], axis=0), d_kvn, d_final, d_sink_row, loss_row)
    zrow = jnp.zeros((1, D_MODEL), F32)
    small = small_allreduce_adamw(
        part,
        _pack_small(ffn1_norm, mix_norm, ffn2_norm, kv_norm, final_norm, swa_sinks, zrow),
        _pack_small(m_ffn1_norm, m_mix_norm, m_ffn2_norm, m_kv_norm, m_final_norm, m_swa_sinks, zrow),
        _pack_small(v_ffn1_norm, v_mix_norm, v_ffn2_norm, v_kv_norm, v_final_norm, v_swa_sinks, zrow))

    def unpack(p):
        return dict(ffn1_norm=p[0:2], mix_norm=p[2:4], ffn2_norm=p[4:6], kv_norm=p[6], final_norm=p[7],
                    swa_sinks=p[8:9, :SWA_Q_HEADS])

    big = dict(ffn1_w_in=r_ffn1_in, ffn1_w_out=r_ffn1_out, ffn2_w_in=r_ffn2_in, ffn2_w_out=r_ffn2_out,
               sb_w_qkv=r_qkv, sb_w_o=r_sb_o, kv_w=r_kv, swa_w_q=r_swa_q, swa_w_o=r_swa_o)
    order = ["ffn1_norm", "ffn1_w_in", "ffn1_w_out", "mix_norm", "ffn2_norm", "ffn2_w_in", "ffn2_w_out",
             "sb_w_qkv", "sb_w_o", "kv_norm", "kv_w", "swa_w_q", "swa_sinks", "swa_w_o", "final_norm"]
    outs = []
    for kind in range(4):
        sm = unpack(small[kind])
        for nm in order:
            outs.append(big[nm][kind] if nm in big else sm[nm])
    loss = small[0][9, 0]
    return (loss, dx.reshape(x.shape), *outs)
```
